```python
import jax, jax.numpy as jnp
from jax import lax
import numpy as np

D_MODEL = 1024
BATCH = 8
SEQ = 8192
DEPTH = 4

N_MIXERS = 2
EPS = 1e-6

SSM_EXPAND = 2
SSM_D_INNER = SSM_EXPAND * D_MODEL
SSM_HEAD_DIM = 64
SSM_HEADS = SSM_D_INNER // SSM_HEAD_DIM
SSM_GROUPS = 8
SSM_HEADS_PER_GROUP = SSM_HEADS // SSM_GROUPS
SSM_STATE = 128
SSM_CONV = 4
SSM_CHUNK = 128
SSM_BC_DIM = SSM_GROUPS * SSM_STATE
SSM_CONV_DIM = SSM_D_INNER + 2 * SSM_BC_DIM
SSM_IN_DIM = SSM_D_INNER + SSM_CONV_DIM + SSM_HEADS

ATT_HEAD_DIM = 64
ATT_Q_HEADS = D_MODEL // ATT_HEAD_DIM
ATT_KV_HEADS = 4
ATT_GQA = ATT_Q_HEADS // ATT_KV_HEADS
ATT_WIDTH = ATT_Q_HEADS * ATT_HEAD_DIM
ATT_KV_WIDTH = ATT_KV_HEADS * ATT_HEAD_DIM
ATT_IN_DIM = 2 * ATT_WIDTH + 2 * ATT_KV_WIDTH
WINDOW = 128
ATT_BLOCK = 128
ROPE_THETA = 500000.0
ROPE_DIM = ATT_HEAD_DIM // 4

N_SSM_LAYERS = (DEPTH + 1) // 2
N_ATT_LAYERS = DEPTH // 2

kernel_name = "hybrid_ssd_swa_sink_trunk"


def rmsnorm(x, w):
    xf = x.astype(jnp.float32)
    y = xf * lax.rsqrt(jnp.mean(xf * xf, axis=-1, keepdims=True) + EPS)
    return (y * w.astype(jnp.float32)).astype(x.dtype)


def causal_depthwise_conv(u, w, b):
    c = u.shape[-1]
    out = lax.conv_general_dilated(
        u, w[:, None, :].astype(u.dtype), window_strides=(1,),
        padding=((SSM_CONV - 1, 0),), dimension_numbers=('NWC', 'WIO', 'NWC'),
        feature_group_count=c)
    return out + b.astype(u.dtype)


def ssd_chunked(xs, dt, A, Bm, Cm):
    b, L = xs.shape[:2]
    c, l = L // SSM_CHUNK, SSM_CHUNK
    G, R, P, N = SSM_GROUPS, SSM_HEADS_PER_GROUP, SSM_HEAD_DIM, SSM_STATE
    x = (xs * dt[..., None]).reshape(b, c, l, G, R, P)
    a = jnp.moveaxis((dt * A).reshape(b, c, l, G, R), 2, -1)
    a_cs = jnp.cumsum(a, axis=-1)
    Bc = Bm.reshape(b, c, l, G, N)
    Cc = Cm.reshape(b, c, l, G, N)
    causal = jnp.tril(jnp.ones((l, l), dtype=bool))
    seg = a_cs[..., :, None] - a_cs[..., None, :]
    decay = jnp.exp(jnp.where(causal, seg, -jnp.inf))
    cb = jnp.einsum('bclgn,bcsgn->bcgls', Cc, Bc)
    y_diag = jnp.einsum('bcgrls,bcsgrp->bclgrp', cb[:, :, :, None] * decay, x)
    decay_to_end = jnp.exp(a_cs[..., -1:] - a_cs)
    states = jnp.einsum('bclgn,bcgrl,bclgrp->bcgrpn', Bc, decay_to_end, x)
    chunk_decay = jnp.exp(a_cs[..., -1])

    def step(h, inp):
        s, d = inp
        return h * d[..., None, None] + s, h

    h0 = jnp.zeros_like(states[:, 0])
    _, h_in = lax.scan(step, h0, (jnp.moveaxis(states, 1, 0), jnp.moveaxis(chunk_decay, 1, 0)))
    h_in = jnp.moveaxis(h_in, 0, 1)
    y_off = jnp.einsum('bclgn,bcgrpn,bcgrl->bclgrp', Cc, h_in, jnp.exp(a_cs))
    return (y_diag + y_off).reshape(b, L, SSM_HEADS, P)


def mamba2_mixer(h, w_in, conv_w, conv_b, dt_bias, a_log, d_skip, gate_norm, w_out):
    b, L, _ = h.shape
    f32 = jnp.float32
    z, xbc, dt = jnp.split(h @ w_in, [SSM_D_INNER, SSM_D_INNER + SSM_CONV_DIM], axis=-1)
    xbc = jax.nn.silu(causal_depthwise_conv(xbc, conv_w, conv_b))
    xs, Bm, Cm = jnp.split(xbc, [SSM_D_INNER, SSM_D_INNER + SSM_BC_DIM], axis=-1)
    xs = xs.reshape(b, L, SSM_HEADS, SSM_HEAD_DIM).astype(f32)
    Bm = Bm.reshape(b, L, SSM_GROUPS, SSM_STATE).astype(f32)
    Cm = Cm.reshape(b, L, SSM_GROUPS, SSM_STATE).astype(f32)
    dt = jax.nn.softplus(dt.astype(f32) + dt_bias.astype(f32))
    A = -jnp.exp(a_log.astype(f32))
    y = ssd_chunked(xs, dt, A, Bm, Cm) + d_skip.astype(f32)[:, None] * xs
    y = y.reshape(b, L, SSM_D_INNER) * jax.nn.silu(z.astype(f32))
    y = rmsnorm(y, gate_norm)
    return y.astype(h.dtype) @ w_out


def rope_tables(positions):
    inv = ROPE_THETA ** (-jnp.arange(0, ROPE_DIM, 2, dtype=jnp.float32) / ROPE_DIM)
    ang = positions.astype(jnp.float32)[..., None] * inv
    return jnp.cos(ang)[:, :, None, :], jnp.sin(ang)[:, :, None, :]


def apply_partial_rope(t, cos, sin):
    half = ROPE_DIM // 2
    t1, t2, rest = t[..., :half], t[..., half:ROPE_DIM], t[..., ROPE_DIM:]
    return jnp.concatenate([t1 * cos - t2 * sin, t2 * cos + t1 * sin, rest], axis=-1)


def swa_sink_mixer(h, cos, sin, w_in, sinks, w_out):
    b, L, _ = h.shape
    f32 = jnp.float32
    Hk, D, BLK = ATT_KV_HEADS, ATT_HEAD_DIM, ATT_BLOCK
    q, k, v, gate = jnp.split(
        h @ w_in, [ATT_WIDTH, ATT_WIDTH + ATT_KV_WIDTH, ATT_WIDTH + 2 * ATT_KV_WIDTH], axis=-1)
    q = apply_partial_rope(q.reshape(b, L, ATT_Q_HEADS, D).astype(f32), cos, sin)
    k = apply_partial_rope(k.reshape(b, L, Hk, D).astype(f32), cos, sin)
    v = v.reshape(b, L, Hk, D).astype(f32)
    nb = L // BLK
    qb = q.reshape(b, nb, BLK, Hk, ATT_GQA, D)

    def band(t):
        tp = jnp.pad(t, ((0, 0), (BLK, 0), (0, 0), (0, 0))).reshape(b, nb + 1, BLK, Hk, D)
        return jnp.concatenate([tp[:, :-1], tp[:, 1:]], axis=2)

    kb, vb = band(k), band(v)
    s = jnp.einsum('bnikgd,bnjkd->bnkgij', qb, kb) * (D ** -0.5)
    qi = jnp.arange(BLK)[:, None]
    kj = jnp.arange(2 * BLK)[None, :]
    dist = qi + BLK - kj
    blk = jnp.arange(nb)[:, None, None]
    valid = (dist >= 0) & (dist < WINDOW) & ((blk - 1) * BLK + kj >= 0)
    s = jnp.where(valid[None, :, None, None], s, -jnp.inf)
    sink = sinks.astype(f32).reshape(Hk, ATT_GQA)[None, None, :, :, None, None]
    m = jnp.maximum(jnp.max(s, axis=-1, keepdims=True), sink)
    p = jnp.exp(s - m)
    denom = jnp.sum(p, axis=-1, keepdims=True) + jnp.exp(sink - m)
    o = jnp.einsum('bnkgij,bnjkd->bnikgd', p / denom, vb)
    o = o.reshape(b, L, ATT_WIDTH) * jax.nn.silu(gate.astype(f32))
    return o.astype(h.dtype) @ w_out


def _fwd_setup_inputs(seed: int = 0) -> dict:
    key = jax.random.key(seed)
    ks = jax.random.split(key, 16)
    f32 = jnp.float32
    nS, nA = N_SSM_LAYERS, N_ATT_LAYERS
    x = jax.random.normal(ks[0], (BATCH, SEQ, D_MODEL), f32)
    positions = jnp.broadcast_to(jnp.arange(SEQ, dtype=jnp.int32), (BATCH, SEQ))
    pre_norm = 1.0 + 0.1 * jax.random.normal(ks[1], (DEPTH, D_MODEL), f32)
    post_norm = 1.0 + 0.1 * jax.random.normal(ks[2], (DEPTH, D_MODEL), f32)
    ssm_w_in = jax.random.normal(ks[3], (nS, D_MODEL, SSM_IN_DIM), f32) * D_MODEL ** -0.5
    ssm_conv_w = jax.random.normal(ks[4], (nS, SSM_CONV, SSM_CONV_DIM), f32) * SSM_CONV ** -0.5
    ssm_conv_b = 0.02 * jax.random.normal(ks[5], (nS, SSM_CONV_DIM), f32)
    dt0 = jnp.exp(jax.random.uniform(ks[6], (nS, SSM_HEADS), f32)
                  * (np.log(0.1) - np.log(0.001)) + np.log(0.001)).astype(f32)
    ssm_dt_bias = dt0 + jnp.log(-jnp.expm1(-dt0))
    ssm_a_log = jnp.log(jax.random.uniform(ks[7], (nS, SSM_HEADS), f32, 1.0, 16.0))
    ssm_d = 1.0 + 0.1 * jax.random.normal(ks[8], (nS, SSM_HEADS), f32)
    ssm_gate_norm = 1.0 + 0.1 * jax.random.normal(ks[9], (nS, SSM_D_INNER), f32)
    ssm_w_out = jax.random.normal(ks[10], (nS, SSM_D_INNER, D_MODEL), f32) * SSM_D_INNER ** -0.5
    att_w_in = jax.random.normal(ks[11], (nA, D_MODEL, ATT_IN_DIM), f32) * D_MODEL ** -0.5
    att_sinks = 0.5 * jax.random.normal(ks[12], (nA, ATT_Q_HEADS), f32)
    att_w_out = jax.random.normal(ks[13], (nA, ATT_WIDTH, D_MODEL), f32) * ATT_WIDTH ** -0.5
    return {"x": x, "positions": positions, "pre_norm": pre_norm, "post_norm": post_norm,
            "ssm_w_in": ssm_w_in, "ssm_conv_w": ssm_conv_w, "ssm_conv_b": ssm_conv_b,
            "ssm_dt_bias": ssm_dt_bias, "ssm_a_log": ssm_a_log, "ssm_d": ssm_d,
            "ssm_gate_norm": ssm_gate_norm, "ssm_w_out": ssm_w_out,
            "att_w_in": att_w_in, "att_sinks": att_sinks, "att_w_out": att_w_out}


def _fwd_reference(x, positions, pre_norm, post_norm, ssm_w_in, ssm_conv_w, ssm_conv_b,
              ssm_dt_bias, ssm_a_log, ssm_d, ssm_gate_norm, ssm_w_out,
              att_w_in, att_sinks, att_w_out):
    cos, sin = rope_tables(positions)
    for i in range(DEPTH):
        h = rmsnorm(x, pre_norm[i])
        j = i // N_MIXERS
        if i % N_MIXERS == 0:
            y = mamba2_mixer(h, ssm_w_in[j], ssm_conv_w[j], ssm_conv_b[j], ssm_dt_bias[j],
                             ssm_a_log[j], ssm_d[j], ssm_gate_norm[j], ssm_w_out[j])
        else:
            y = swa_sink_mixer(h, cos, sin, att_w_in[j], att_sinks[j], att_w_out[j])
        x = x + rmsnorm(y, post_norm[i])
    return x


import jax as _jax
import jax.numpy as _jnp

TWIN_FORMAT = 'train_step'
FWD_PARAMS = ['x', 'positions', 'pre_norm', 'post_norm', 'ssm_w_in', 'ssm_conv_w', 'ssm_conv_b', 'ssm_dt_bias', 'ssm_a_log', 'ssm_d', 'ssm_gate_norm', 'ssm_w_out', 'att_w_in', 'att_sinks', 'att_w_out']
TWIN_WEIGHTS = ['pre_norm', 'post_norm', 'ssm_w_in', 'ssm_conv_w', 'ssm_conv_b', 'ssm_dt_bias', 'ssm_a_log', 'ssm_d', 'ssm_gate_norm', 'ssm_w_out', 'att_w_in', 'att_sinks', 'att_w_out']
TWIN_DIFF_INPUT = 'x'
TWIN_INPUTS = ['x', 'positions', 'pre_norm', 'post_norm', 'ssm_w_in', 'ssm_conv_w', 'ssm_conv_b', 'ssm_dt_bias', 'ssm_a_log', 'ssm_d', 'ssm_gate_norm', 'ssm_w_out', 'att_w_in', 'att_sinks', 'att_w_out', 'loss_target', 'm_pre_norm', 'm_post_norm', 'm_ssm_w_in', 'm_ssm_conv_w', 'm_ssm_conv_b', 'm_ssm_dt_bias', 'm_ssm_a_log', 'm_ssm_d', 'm_ssm_gate_norm', 'm_ssm_w_out', 'm_att_w_in', 'm_att_sinks', 'm_att_w_out', 'v_pre_norm', 'v_post_norm', 'v_ssm_w_in', 'v_ssm_conv_w', 'v_ssm_conv_b', 'v_ssm_dt_bias', 'v_ssm_a_log', 'v_ssm_d', 'v_ssm_gate_norm', 'v_ssm_w_out', 'v_att_w_in', 'v_att_sinks', 'v_att_w_out']
TWIN_OUTPUTS = ['loss', 'grad_x', 'grad_pre_norm', 'grad_post_norm', 'grad_ssm_w_in', 'grad_ssm_conv_w', 'grad_ssm_conv_b', 'grad_ssm_dt_bias', 'grad_ssm_a_log', 'grad_ssm_d', 'grad_ssm_gate_norm', 'grad_ssm_w_out', 'grad_att_w_in', 'grad_att_sinks', 'grad_att_w_out', 'delta_pre_norm', 'delta_post_norm', 'delta_ssm_w_in', 'delta_ssm_conv_w', 'delta_ssm_conv_b', 'delta_ssm_dt_bias', 'delta_ssm_a_log', 'delta_ssm_d', 'delta_ssm_gate_norm', 'delta_ssm_w_out', 'delta_att_w_in', 'delta_att_sinks', 'delta_att_w_out', 'new_m_pre_norm', 'new_m_post_norm', 'new_m_ssm_w_in', 'new_m_ssm_conv_w', 'new_m_ssm_conv_b', 'new_m_ssm_dt_bias', 'new_m_ssm_a_log', 'new_m_ssm_d', 'new_m_ssm_gate_norm', 'new_m_ssm_w_out', 'new_m_att_w_in', 'new_m_att_sinks', 'new_m_att_w_out', 'new_v_pre_norm', 'new_v_post_norm', 'new_v_ssm_w_in', 'new_v_ssm_conv_w', 'new_v_ssm_conv_b', 'new_v_ssm_dt_bias', 'new_v_ssm_a_log', 'new_v_ssm_d', 'new_v_ssm_gate_norm', 'new_v_ssm_w_out', 'new_v_att_w_in', 'new_v_att_sinks', 'new_v_att_w_out']
TWIN_LEAF_KINDS = {'loss': 'loss', 'grad_x': 'grad_x', 'grad_pre_norm': 'grad_w', 'grad_post_norm': 'grad_w', 'grad_ssm_w_in': 'grad_w', 'grad_ssm_conv_w': 'grad_w', 'grad_ssm_conv_b': 'grad_w', 'grad_ssm_dt_bias': 'grad_w', 'grad_ssm_a_log': 'grad_w', 'grad_ssm_d': 'grad_w', 'grad_ssm_gate_norm': 'grad_w', 'grad_ssm_w_out': 'grad_w', 'grad_att_w_in': 'grad_w', 'grad_att_sinks': 'grad_w', 'grad_att_w_out': 'grad_w', 'delta_pre_norm': 'delta_w', 'delta_post_norm': 'delta_w', 'delta_ssm_w_in': 'delta_w', 'delta_ssm_conv_w': 'delta_w', 'delta_ssm_conv_b': 'delta_w', 'delta_ssm_dt_bias': 'delta_w', 'delta_ssm_a_log': 'delta_w', 'delta_ssm_d': 'delta_w', 'delta_ssm_gate_norm': 'delta_w', 'delta_ssm_w_out': 'delta_w', 'delta_att_w_in': 'delta_w', 'delta_att_sinks': 'delta_w', 'delta_att_w_out': 'delta_w', 'new_m_pre_norm': 'new_m', 'new_m_post_norm': 'new_m', 'new_m_ssm_w_in': 'new_m', 'new_m_ssm_conv_w': 'new_m', 'new_m_ssm_conv_b': 'new_m', 'new_m_ssm_dt_bias': 'new_m', 'new_m_ssm_a_log': 'new_m', 'new_m_ssm_d': 'new_m', 'new_m_ssm_gate_norm': 'new_m', 'new_m_ssm_w_out': 'new_m', 'new_m_att_w_in': 'new_m', 'new_m_att_sinks': 'new_m', 'new_m_att_w_out': 'new_m', 'new_v_pre_norm': 'new_v', 'new_v_post_norm': 'new_v', 'new_v_ssm_w_in': 'new_v', 'new_v_ssm_conv_w': 'new_v', 'new_v_ssm_conv_b': 'new_v', 'new_v_ssm_dt_bias': 'new_v', 'new_v_ssm_a_log': 'new_v', 'new_v_ssm_d': 'new_v', 'new_v_ssm_gate_norm': 'new_v', 'new_v_ssm_w_out': 'new_v', 'new_v_att_w_in': 'new_v', 'new_v_att_sinks': 'new_v', 'new_v_att_w_out': 'new_v'}


def _forward(args):
    return _fwd_reference(*[args[k] for k in FWD_PARAMS])


def _output_shape():
    def fwd():
        inp = _fwd_setup_inputs(0)
        return _fwd_reference(*[inp[k] for k in FWD_PARAMS])
    out = _jax.eval_shape(fwd)
    return out.shape, out.dtype

N_MICROBATCH = 1
ADAM_LR = 0.001
ADAM_B1 = 0.9
ADAM_B2 = 0.999
ADAM_EPS = 1e-08
ADAM_WD = 0.01
ADAM_STEP = 10
PER_EXAMPLE_BATCH_AXIS = {'x': 0, 'positions': 0, 'loss_target': 0}
SHARED_INPUTS = []
_WEIGHT_DTYPES = {'pre_norm': _jnp.float32, 'post_norm': _jnp.float32, 'ssm_w_in': _jnp.float32, 'ssm_conv_w': _jnp.float32, 'ssm_conv_b': _jnp.float32, 'ssm_dt_bias': _jnp.float32, 'ssm_a_log': _jnp.float32, 'ssm_d': _jnp.float32, 'ssm_gate_norm': _jnp.float32, 'ssm_w_out': _jnp.float32, 'att_w_in': _jnp.float32, 'att_sinks': _jnp.float32, 'att_w_out': _jnp.float32}
MOMENT_SCALE = {'pre_norm': 5.196824e+00, 'post_norm': 6.523940e+01, 'ssm_w_in': 1.093300e+00, 'ssm_conv_w': 2.973658e+00, 'ssm_conv_b': 9.790397e+00, 'ssm_dt_bias': 2.822953e+00, 'ssm_a_log': 2.204634e+01, 'ssm_d': 2.661335e+01, 'ssm_gate_norm': 6.844428e+00, 'ssm_w_out': 1.019491e+01, 'att_w_in': 4.928191e+00, 'att_sinks': 3.006804e-01, 'att_w_out': 7.931713e+00}


def _to_microbatches(a, axis):
    t = _jnp.moveaxis(a, axis, 0)
    t = t.reshape((N_MICROBATCH, t.shape[0] // N_MICROBATCH) + t.shape[1:])
    return _jnp.moveaxis(t, 1, axis + 1)


def setup_inputs(seed: int = 0) -> dict:
    inp = _fwd_setup_inputs(seed)
    key = _jax.random.fold_in(_jax.random.key(seed), 7919)
    shape, _ = _output_shape()
    out = dict(inp)
    out["loss_target"] = _jax.random.normal(_jax.random.fold_in(key, 0), shape, _jnp.float32)
    for i, name in enumerate(TWIN_WEIGHTS):
        w = inp[name].astype(_jnp.float32)
        if MOMENT_SCALE is None:
            s = _jnp.sqrt(_jnp.mean(_jnp.square(w)) + 1e-30)
        else:
            s = MOMENT_SCALE[name]
        km, kv = _jax.random.split(_jax.random.fold_in(key, i + 1))
        out[name] = w
        out["m_" + name] = s * _jax.random.normal(km, w.shape, _jnp.float32)
        out["v_" + name] = (s * s) * _jax.random.uniform(kv, w.shape, _jnp.float32, 0.5, 1.5)
    if N_MICROBATCH > 1:
        for name, axis in PER_EXAMPLE_BATCH_AXIS.items():
            out[name] = _to_microbatches(out[name], axis)
    return {'x': out['x'], 'positions': out['positions'], 'pre_norm': out['pre_norm'], 'post_norm': out['post_norm'], 'ssm_w_in': out['ssm_w_in'], 'ssm_conv_w': out['ssm_conv_w'], 'ssm_conv_b': out['ssm_conv_b'], 'ssm_dt_bias': out['ssm_dt_bias'], 'ssm_a_log': out['ssm_a_log'], 'ssm_d': out['ssm_d'], 'ssm_gate_norm': out['ssm_gate_norm'], 'ssm_w_out': out['ssm_w_out'], 'att_w_in': out['att_w_in'], 'att_sinks': out['att_sinks'], 'att_w_out': out['att_w_out'], 'loss_target': out['loss_target'], 'm_pre_norm': out['m_pre_norm'], 'm_post_norm': out['m_post_norm'], 'm_ssm_w_in': out['m_ssm_w_in'], 'm_ssm_conv_w': out['m_ssm_conv_w'], 'm_ssm_conv_b': out['m_ssm_conv_b'], 'm_ssm_dt_bias': out['m_ssm_dt_bias'], 'm_ssm_a_log': out['m_ssm_a_log'], 'm_ssm_d': out['m_ssm_d'], 'm_ssm_gate_norm': out['m_ssm_gate_norm'], 'm_ssm_w_out': out['m_ssm_w_out'], 'm_att_w_in': out['m_att_w_in'], 'm_att_sinks': out['m_att_sinks'], 'm_att_w_out': out['m_att_w_out'], 'v_pre_norm': out['v_pre_norm'], 'v_post_norm': out['v_post_norm'], 'v_ssm_w_in': out['v_ssm_w_in'], 'v_ssm_conv_w': out['v_ssm_conv_w'], 'v_ssm_conv_b': out['v_ssm_conv_b'], 'v_ssm_dt_bias': out['v_ssm_dt_bias'], 'v_ssm_a_log': out['v_ssm_a_log'], 'v_ssm_d': out['v_ssm_d'], 'v_ssm_gate_norm': out['v_ssm_gate_norm'], 'v_ssm_w_out': out['v_ssm_w_out'], 'v_att_w_in': out['v_att_w_in'], 'v_att_sinks': out['v_att_sinks'], 'v_att_w_out': out['v_att_w_out']}


def _loss(weights, diff, rest, loss_target):
    with _jax.named_scope("forward"):
        args = {**rest, TWIN_DIFF_INPUT: diff, **{k: w.astype(_WEIGHT_DTYPES[k]) for k, w in weights.items()}}
        y = _forward(args)
    with _jax.named_scope("loss_head"):
        err = _jnp.square(y.astype(_jnp.float32) - loss_target)
        return 0.5 * _jnp.sum(_jnp.mean(err, axis=-1)) if err.ndim else 0.5 * err


def _adamw(w, g, m, v):
    m = ADAM_B1 * m + (1.0 - ADAM_B1) * g
    v = ADAM_B2 * v + (1.0 - ADAM_B2) * _jnp.square(g)
    m_hat = m / (1.0 - ADAM_B1 ** ADAM_STEP)
    v_hat = v / (1.0 - ADAM_B2 ** ADAM_STEP)
    delta = -ADAM_LR * (m_hat / (_jnp.sqrt(v_hat) + ADAM_EPS) + ADAM_WD * w)
    return delta, m, v


def reference(x, positions, pre_norm, post_norm, ssm_w_in, ssm_conv_w, ssm_conv_b, ssm_dt_bias, ssm_a_log, ssm_d, ssm_gate_norm, ssm_w_out, att_w_in, att_sinks, att_w_out, loss_target, m_pre_norm, m_post_norm, m_ssm_w_in, m_ssm_conv_w, m_ssm_conv_b, m_ssm_dt_bias, m_ssm_a_log, m_ssm_d, m_ssm_gate_norm, m_ssm_w_out, m_att_w_in, m_att_sinks, m_att_w_out, v_pre_norm, v_post_norm, v_ssm_w_in, v_ssm_conv_w, v_ssm_conv_b, v_ssm_dt_bias, v_ssm_a_log, v_ssm_d, v_ssm_gate_norm, v_ssm_w_out, v_att_w_in, v_att_sinks, v_att_w_out):
    given = dict(x=x, positions=positions, pre_norm=pre_norm, post_norm=post_norm, ssm_w_in=ssm_w_in, ssm_conv_w=ssm_conv_w, ssm_conv_b=ssm_conv_b, ssm_dt_bias=ssm_dt_bias, ssm_a_log=ssm_a_log, ssm_d=ssm_d, ssm_gate_norm=ssm_gate_norm, ssm_w_out=ssm_w_out, att_w_in=att_w_in, att_sinks=att_sinks, att_w_out=att_w_out, loss_target=loss_target, m_pre_norm=m_pre_norm, m_post_norm=m_post_norm, m_ssm_w_in=m_ssm_w_in, m_ssm_conv_w=m_ssm_conv_w, m_ssm_conv_b=m_ssm_conv_b, m_ssm_dt_bias=m_ssm_dt_bias, m_ssm_a_log=m_ssm_a_log, m_ssm_d=m_ssm_d, m_ssm_gate_norm=m_ssm_gate_norm, m_ssm_w_out=m_ssm_w_out, m_att_w_in=m_att_w_in, m_att_sinks=m_att_sinks, m_att_w_out=m_att_w_out, v_pre_norm=v_pre_norm, v_post_norm=v_post_norm, v_ssm_w_in=v_ssm_w_in, v_ssm_conv_w=v_ssm_conv_w, v_ssm_conv_b=v_ssm_conv_b, v_ssm_dt_bias=v_ssm_dt_bias, v_ssm_a_log=v_ssm_a_log, v_ssm_d=v_ssm_d, v_ssm_gate_norm=v_ssm_gate_norm, v_ssm_w_out=v_ssm_w_out, v_att_w_in=v_att_w_in, v_att_sinks=v_att_sinks, v_att_w_out=v_att_w_out)
    weights = {n: given[n] for n in TWIN_WEIGHTS}
    shared = {n: given[n] for n in SHARED_INPUTS}
    per_example = {n: given[n] for n in ['x', 'positions']}
    grad_fn = _jax.value_and_grad(_loss, argnums=(0, 1))

    def one_microbatch(ex, loss_target):
        ex = dict(ex)
        diff = ex.pop(TWIN_DIFF_INPUT)
        return grad_fn(weights, diff, {**shared, **ex}, loss_target)

    if N_MICROBATCH == 1:
        loss, (grad_w, grad_x) = one_microbatch(per_example, given["loss_target"])
    else:
        def body(carry, xs):
            loss_sum, grad_sum = carry
            l_k, (gw_k, gx_k) = one_microbatch(xs[0], xs[1])
            with _jax.named_scope("update"):
                return (loss_sum + l_k, _jax.tree.map(_jnp.add, grad_sum, gw_k)), gx_k

        init = (_jnp.zeros((), _jnp.float32), _jax.tree.map(_jnp.zeros_like, weights))
        (loss, grad_w), grad_x = _jax.lax.scan(body, init, (per_example, given["loss_target"]))
    with _jax.named_scope("update"):
        delta_w, new_m, new_v = {}, {}, {}
        for n in TWIN_WEIGHTS:
            delta_w[n], new_m[n], new_v[n] = _adamw(weights[n], grad_w[n], given["m_" + n], given["v_" + n])
    return (loss, grad_x, *[grad_w[n] for n in TWIN_WEIGHTS], *[delta_w[n] for n in TWIN_WEIGHTS],
            *[new_m[n] for n in TWIN_WEIGHTS], *[new_v[n] for n in TWIN_WEIGHTS])
```

```python
import functools

import jax
import jax.numpy as jnp
from jax import lax
from jax.experimental import pallas as pl
from jax.experimental.pallas import tpu as pltpu

F32 = jnp.float32
BF16 = jnp.bfloat16
EPS = 1e-6
NEG_INF = float("-inf")

D_MODEL = 1024
DEPTH = 4
SSM_D_INNER = 2048
SSM_HEAD_DIM = 64
SSM_HEADS = 32
SSM_GROUPS = 8
SSM_HPG = 4
SSM_STATE = 128
SSM_CONV = 4
SSM_CHUNK = 128
SSM_BC_DIM = 1024
SSM_CONV_DIM = 4096
SSM_IN_DIM = 6176
SSM_IN_PAD = 6272
SSM_DT_PAD = 128
ATT_HEAD_DIM = 64
ATT_Q_HEADS = 16
ATT_KV_HEADS = 4
ATT_GQA = 4
ATT_WIDTH = 1024
ATT_KV_WIDTH = 256
ATT_IN_DIM = 2560
ATT_QKV = ATT_WIDTH + 2 * ATT_KV_WIDTH
ATT_BLOCK = 128
ROPE_THETA = 500000.0
ROPE_DIM = 16
ROPE_HALF = 8
Q_SCALE = ATT_HEAD_DIM ** -0.5

ADAM_LR = 0.001
ADAM_B1 = 0.9
ADAM_B2 = 0.999
ADAM_EPS = 1e-08
ADAM_WD = 0.01
ADAM_STEP = 10

VMEM_LIMIT_BYTES = 48 * 1024 * 1024
NT_DIMS = (((1,), (1,)), ((), ()))
TN_DIMS = (((0,), (0,)), ((), ()))


def _params(*sem):
    return pltpu.CompilerParams(dimension_semantics=sem, vmem_limit_bytes=VMEM_LIMIT_BYTES)


def _pick(n, cands):
    for c in cands:
        if n % c == 0:
            return c
    return n


def _sigmoid(v):
    return 1.0 / (1.0 + jnp.exp(-v))


def _bdot(a, b):
    return jnp.dot(a.astype(BF16), b.astype(BF16), preferred_element_type=F32)


def _bdot_nt(a, b):
    return lax.dot_general(a.astype(BF16), b.astype(BF16), NT_DIMS, preferred_element_type=F32)


def _bdot_tn(a, b):
    return lax.dot_general(a.astype(BF16), b.astype(BF16), TN_DIMS, preferred_element_type=F32)


def _matmul(a, b, mode, out_dtype, name):
    if mode == "nn":
        (m, k), n = a.shape, b.shape[1]
    elif mode == "nt":
        (m, k), n = a.shape, b.shape[0]
    else:
        (k, m), n = a.shape, b.shape[1]
    tm = _pick(m, (1024, 512) if mode == "tn" else (512,))
    tn = _pick(n, (896, 640, 512))
    tk = _pick(k, (2048, 1024, 896, 512))
    nk = k // tk
    dims = {"nn": (((1,), (0,)), ((), ())), "nt": NT_DIMS, "tn": TN_DIMS}[mode]

    def body(a_ref, b_ref, o_ref, acc_ref):
        kk = pl.program_id(2)
        part = lax.dot_general(a_ref[...], b_ref[...], dims, preferred_element_type=F32)
        if nk == 1:
            o_ref[...] = part.astype(o_ref.dtype)
        else:
            @pl.when(kk == 0)
            def _():
                acc_ref[...] = part

            @pl.when(kk > 0)
            def _():
                acc_ref[...] += part

            @pl.when(kk == nk - 1)
            def _():
                o_ref[...] = acc_ref[...].astype(o_ref.dtype)

    if mode == "nn":
        a_spec = pl.BlockSpec((tm, tk), lambda j, i, kk: (i, kk))
        b_spec = pl.BlockSpec((tk, tn), lambda j, i, kk: (kk, j))
    elif mode == "nt":
        a_spec = pl.BlockSpec((tm, tk), lambda j, i, kk: (i, kk))
        b_spec = pl.BlockSpec((tn, tk), lambda j, i, kk: (j, kk))
    else:
        a_spec = pl.BlockSpec((tk, tm), lambda j, i, kk: (kk, i))
        b_spec = pl.BlockSpec((tk, tn), lambda j, i, kk: (kk, j))
    return pl.pallas_call(
        body, grid=(n // tn, m // tm, nk), in_specs=[a_spec, b_spec],
        out_specs=pl.BlockSpec((tm, tn), lambda j, i, kk: (i, j)),
        out_shape=jax.ShapeDtypeStruct((m, n), out_dtype),
        scratch_shapes=[pltpu.VMEM((tm, tn), F32)],
        compiler_params=_params("parallel", "parallel", "arbitrary"), name=name,
    )(a, b)


def _row_tile(l):
    return _pick(l, (512, 256, 128))


def _rmsnorm_fwd(x, w, name):
    l, d = x.shape
    tl = _row_tile(l)

    def body(x_ref, w_ref, o_ref):
        xv = x_ref[...]
        r = lax.rsqrt(jnp.mean(xv * xv, axis=-1, keepdims=True) + EPS)
        o_ref[...] = (xv * r * w_ref[...]).astype(o_ref.dtype)

    return pl.pallas_call(
        body, grid=(l // tl,),
        in_specs=[pl.BlockSpec((tl, d), lambda i: (i, 0)), pl.BlockSpec((1, d), lambda i: (0, 0))],
        out_specs=pl.BlockSpec((tl, d), lambda i: (i, 0)),
        out_shape=jax.ShapeDtypeStruct((l, d), BF16), compiler_params=_params("parallel"), name=name,
    )(x, w.reshape(1, d))


def _post_fwd(x, y, w, name):
    l, d = x.shape
    tl = _row_tile(l)

    def body(x_ref, y_ref, w_ref, o_ref):
        yv = y_ref[...]
        r = lax.rsqrt(jnp.mean(yv * yv, axis=-1, keepdims=True) + EPS)
        o_ref[...] = x_ref[...] + yv * r * w_ref[...]

    return pl.pallas_call(
        body, grid=(l // tl,),
        in_specs=[pl.BlockSpec((tl, d), lambda i: (i, 0)), pl.BlockSpec((tl, d), lambda i: (i, 0)),
                  pl.BlockSpec((1, d), lambda i: (0, 0))],
        out_specs=pl.BlockSpec((tl, d), lambda i: (i, 0)),
        out_shape=jax.ShapeDtypeStruct((l, d), F32), compiler_params=_params("parallel"), name=name,
    )(x, y, w.reshape(1, d))


def _rmsnorm_bwd(g, y, w, resid, out_dtype, name):
    l, d = y.shape
    tl = _row_tile(l)
    nt = l // tl
    has_resid = resid is not None

    def body(*refs):
        if has_resid:
            g_ref, y_ref, w_ref, r_ref, dy_ref, dw_ref, acc_ref = refs
        else:
            g_ref, y_ref, w_ref, dy_ref, dw_ref, acc_ref = refs
        i = pl.program_id(0)

        @pl.when(i == 0)
        def _():
            acc_ref[...] = jnp.zeros_like(acc_ref)

        yv = y_ref[...]
        gv = g_ref[...].astype(F32)
        r = lax.rsqrt(jnp.mean(yv * yv, axis=-1, keepdims=True) + EPS)
        nrm = yv * r
        gw = gv * w_ref[...]
        dy = r * (gw - nrm * jnp.mean(gw * nrm, axis=-1, keepdims=True))
        if has_resid:
            dy = dy + r_ref[...]
        dy_ref[...] = dy.astype(dy_ref.dtype)
        acc_ref[...] += jnp.sum((gv * nrm).reshape(tl // 8, 8, d), axis=0)

        @pl.when(i == nt - 1)
        def _():
            dw_ref[...] = jnp.sum(acc_ref[...], axis=0, keepdims=True)

    row = pl.BlockSpec((tl, d), lambda i: (i, 0))
    vec = pl.BlockSpec((1, d), lambda i: (0, 0))
    ins = [g, y, w.reshape(1, d)] + ([resid] if has_resid else [])
    return pl.pallas_call(
        body, grid=(nt,), in_specs=[row, row, vec] + ([row] if has_resid else []),
        out_specs=[row, vec],
        out_shape=[jax.ShapeDtypeStruct((l, d), out_dtype), jax.ShapeDtypeStruct((1, d), F32)],
        scratch_shapes=[pltpu.VMEM((8, d), F32)], compiler_params=_params("arbitrary"), name=name,
    )(*ins)


def _loss_grad(y, t, name):
    l, d = y.shape
    tl = _row_tile(l)

    def body(y_ref, t_ref, dy_ref, ls_ref):
        @pl.when(pl.program_id(0) == 0)
        def _():
            ls_ref[...] = jnp.zeros_like(ls_ref)

        e = y_ref[...] - t_ref[...]
        dy_ref[...] = e * (1.0 / d)
        ls_ref[...] += jnp.sum((e * e).reshape(tl // 8, 8, d), axis=0)

    row = pl.BlockSpec((tl, d), lambda i: (i, 0))
    return pl.pallas_call(
        body, grid=(l // tl,), in_specs=[row, row],
        out_specs=[row, pl.BlockSpec((8, d), lambda i: (0, 0))],
        out_shape=[jax.ShapeDtypeStruct((l, d), F32), jax.ShapeDtypeStruct((8, d), F32)],
        compiler_params=_params("arbitrary"), name=name,
    )(y, t)


CONV_COLS = 512
HALO = 8


def _conv_fwd(proj, cw, cb, name):
    l = proj.shape[0]
    tl = _row_tile(l)
    off = SSM_D_INNER // CONV_COLS

    def body(u_ref, halo_ref, w_ref, b_ref, pre_ref, act_ref, ext_ref):
        i = pl.program_id(1)
        ext_ref[0:HALO, :] = jnp.where(i > 0, halo_ref[...], 0.0)
        ext_ref[HALO:HALO + tl, :] = u_ref[...]
        acc = jnp.broadcast_to(b_ref[...], (tl, CONV_COLS))
        for k in range(SSM_CONV):
            acc = acc + w_ref[k:k + 1, :] * ext_ref[pl.ds(HALO - SSM_CONV + 1 + k, tl), :]
        pre_ref[...] = acc
        act_ref[...] = acc * _sigmoid(acc)

    hb = tl // HALO
    out = pl.BlockSpec((tl, CONV_COLS), lambda j, i: (i, j))
    return pl.pallas_call(
        body, grid=(SSM_CONV_DIM // CONV_COLS, l // tl),
        in_specs=[pl.BlockSpec((tl, CONV_COLS), lambda j, i: (i, off + j)),
                  pl.BlockSpec((HALO, CONV_COLS), lambda j, i: (jnp.maximum(i * hb - 1, 0), off + j)),
                  pl.BlockSpec((SSM_CONV, CONV_COLS), lambda j, i: (0, j)),
                  pl.BlockSpec((1, CONV_COLS), lambda j, i: (0, j))],
        out_specs=[out, out],
        out_shape=[jax.ShapeDtypeStruct((l, SSM_CONV_DIM), F32)] * 2,
        scratch_shapes=[pltpu.VMEM((tl + HALO, CONV_COLS), F32)],
        compiler_params=_params("parallel", "arbitrary"), name=name,
    )(proj, proj, cw, cb.reshape(1, SSM_CONV_DIM))


def _conv_bwd(dact, pre, proj, cw, c0, name):
    l, width = dact.shape
    tl = _row_tile(l)
    nt = l // tl
    pre_off = c0 // CONV_COLS
    u_off = (SSM_D_INNER + c0) // CONV_COLS
    hb = tl // HALO
    last_hb = l // HALO - 1

    def body(da_ref, da_h_ref, p_ref, p_h_ref, u_ref, u_h_ref, w_ref, du_ref, dw_ref, db_ref, ext_ref, uext_ref):
        i = pl.program_id(1)

        @pl.when(i == 0)
        def _():
            dw_ref[...] = jnp.zeros_like(dw_ref)
            db_ref[...] = jnp.zeros_like(db_ref)

        def dpre_of(da, p):
            s = _sigmoid(p)
            return da * (s * (1.0 + p * (1.0 - s)))

        dp = dpre_of(da_ref[...], p_ref[...])
        ext_ref[0:tl, :] = dp
        ext_ref[tl:tl + HALO, :] = jnp.where(i < nt - 1, dpre_of(da_h_ref[...], p_h_ref[...]), 0.0)
        uext_ref[0:HALO, :] = jnp.where(i > 0, u_h_ref[...], 0.0)
        uext_ref[HALO:HALO + tl, :] = u_ref[...]
        du = jnp.zeros((tl, CONV_COLS), F32)
        for k in range(SSM_CONV):
            du = du + w_ref[k:k + 1, :] * ext_ref[pl.ds(SSM_CONV - 1 - k, tl), :]
            dw_ref[k:k + 1, :] += jnp.sum(dp * uext_ref[pl.ds(HALO - SSM_CONV + 1 + k, tl), :], axis=0, keepdims=True)
        du_ref[...] = du.astype(du_ref.dtype)
        db_ref[...] += jnp.sum(dp, axis=0, keepdims=True)

    return pl.pallas_call(
        body, grid=(width // CONV_COLS, nt),
        in_specs=[pl.BlockSpec((tl, CONV_COLS), lambda j, i: (i, j)),
                  pl.BlockSpec((HALO, CONV_COLS), lambda j, i: (jnp.minimum((i + 1) * hb, last_hb), j)),
                  pl.BlockSpec((tl, CONV_COLS), lambda j, i: (i, pre_off + j)),
                  pl.BlockSpec((HALO, CONV_COLS), lambda j, i: (jnp.minimum((i + 1) * hb, last_hb), pre_off + j)),
                  pl.BlockSpec((tl, CONV_COLS), lambda j, i: (i, u_off + j)),
                  pl.BlockSpec((HALO, CONV_COLS), lambda j, i: (jnp.maximum(i * hb - 1, 0), u_off + j)),
                  pl.BlockSpec((SSM_CONV, CONV_COLS), lambda j, i: (0, pre_off + j))],
        out_specs=[pl.BlockSpec((tl, CONV_COLS), lambda j, i: (i, j)),
                   pl.BlockSpec((SSM_CONV, CONV_COLS), lambda j, i: (0, j)),
                   pl.BlockSpec((1, CONV_COLS), lambda j, i: (0, j))],
        out_shape=[jax.ShapeDtypeStruct((l, width), BF16), jax.ShapeDtypeStruct((SSM_CONV, width), F32),
                   jax.ShapeDtypeStruct((1, width), F32)],
        scratch_shapes=[pltpu.VMEM((tl + HALO, CONV_COLS), F32), pltpu.VMEM((tl + HALO, CONV_COLS), F32)],
        compiler_params=_params("parallel", "arbitrary"), name=name,
    )(dact, dact, pre, pre, proj, proj, cw)


DT_COL_BLOCK = (SSM_D_INNER + SSM_CONV_DIM) // SSM_DT_PAD


def _dt_fwd(proj, bias, name):
    l = proj.shape[0]
    tl = _row_tile(l)

    def body(p_ref, b_ref, o_ref):
        v = p_ref[...] + b_ref[...]
        o_ref[...] = jnp.maximum(v, 0.0) + jnp.log1p(jnp.exp(-jnp.abs(v)))

    return pl.pallas_call(
        body, grid=(l // tl,),
        in_specs=[pl.BlockSpec((tl, SSM_DT_PAD), lambda i: (i, DT_COL_BLOCK)),
                  pl.BlockSpec((1, SSM_DT_PAD), lambda i: (0, 0))],
        out_specs=pl.BlockSpec((tl, SSM_DT_PAD), lambda i: (i, 0)),
        out_shape=jax.ShapeDtypeStruct((l, SSM_DT_PAD), F32), compiler_params=_params("parallel"), name=name,
    )(proj, bias)


def _dt_bwd(ddt, proj, bias, name):
    l = proj.shape[0]
    tl = _row_tile(l)

    def body(g_ref, p_ref, b_ref, o_ref, db_ref):
        @pl.when(pl.program_id(0) == 0)
        def _():
            db_ref[...] = jnp.zeros_like(db_ref)

        d = g_ref[...] * _sigmoid(p_ref[...] + b_ref[...])
        o_ref[...] = d.astype(o_ref.dtype)
        db_ref[...] += jnp.sum(d, axis=0, keepdims=True)

    return pl.pallas_call(
        body, grid=(l // tl,),
        in_specs=[pl.BlockSpec((tl, SSM_DT_PAD), lambda i: (i, 0)),
                  pl.BlockSpec((tl, SSM_DT_PAD), lambda i: (i, DT_COL_BLOCK)),
                  pl.BlockSpec((1, SSM_DT_PAD), lambda i: (0, 0))],
        out_specs=[pl.BlockSpec((tl, SSM_DT_PAD), lambda i: (i, 0)), pl.BlockSpec((1, SSM_DT_PAD), lambda i: (0, 0))],
        out_shape=[jax.ShapeDtypeStruct((l, SSM_DT_PAD), BF16), jax.ShapeDtypeStruct((1, SSM_DT_PAD), F32)],
        compiler_params=_params("arbitrary"), name=name,
    )(ddt, proj, bias)


GP = SSM_HPG * SSM_HEAD_DIM
X_BLOCKS = SSM_D_INNER // GP
B_BLOCK0 = SSM_D_INNER // SSM_STATE
C_BLOCK0 = (SSM_D_INNER + SSM_BC_DIM) // SSM_STATE


def _chunk_decay_terms(dtc, dtr, a_row, a_col):
    ri = lax.broadcasted_iota(jnp.int32, (SSM_CHUNK, SSM_CHUNK), 0)
    cj = lax.broadcasted_iota(jnp.int32, (SSM_CHUNK, SSM_CHUNK), 1)
    tri = (ri >= cj).astype(F32)
    acs_c = jnp.dot(tri, dtc * a_row, preferred_element_type=F32, precision=lax.Precision.HIGHEST)
    acs_r = lax.dot_general(dtr * a_col, tri, NT_DIMS, preferred_element_type=F32, precision=lax.Precision.HIGHEST)
    return ri, cj, acs_c, acs_r


def _ssd_in_specs(nc, rev):
    def cidx(c):
        return nc - 1 - c if rev else c

    return [
        pl.BlockSpec((SSM_CHUNK, GP), lambda c, g: (cidx(c), g)),
        pl.BlockSpec((SSM_CHUNK, SSM_STATE), lambda c, g: (cidx(c), B_BLOCK0 + g)),
        pl.BlockSpec((SSM_CHUNK, SSM_STATE), lambda c, g: (cidx(c), C_BLOCK0 + g)),
        pl.BlockSpec((1, SSM_CHUNK, SSM_HPG), lambda c, g: (g, cidx(c), 0)),
        pl.BlockSpec((1, SSM_HPG, SSM_CHUNK), lambda c, g: (g, 0, cidx(c))),
        pl.BlockSpec((1, 1, SSM_HPG), lambda c, g: (g, 0, 0)),
        pl.BlockSpec((1, SSM_HPG, 1), lambda c, g: (g, 0, 0)),
        pl.BlockSpec((1, 1, SSM_HPG), lambda c, g: (g, 0, 0)),
    ]


def _ssd_fwd(xbc, dtc, dtr, a_log, d_skip, name):
    l = xbc.shape[0]
    nc = l // SSM_CHUNK

    def body(x_ref, b_ref, c_ref, dtc_ref, dtr_ref, alr_ref, alc_ref, d_ref, y_ref, hin_ref, h_ref):
        c = pl.program_id(0)
        g = pl.program_id(1)

        @pl.when(c == 0)
        def _():
            h_ref[g] = jnp.zeros((SSM_STATE, GP), F32)

        xv = x_ref[...]
        bb = b_ref[...].astype(BF16)
        cb16 = c_ref[...].astype(BF16)
        dtc_v = dtc_ref[0]
        a_row = -jnp.exp(alr_ref[0])
        ri, cj, acs_c, acs_r = _chunk_decay_terms(dtc_v, dtr_ref[0], a_row, -jnp.exp(alc_ref[0]))
        last = acs_c[SSM_CHUNK - 1:SSM_CHUNK, :]
        cb = lax.dot_general(cb16, bb, NT_DIMS, preferred_element_type=F32)
        hin = h_ref[g]
        hin_ref[0, 0] = hin
        yoff = jnp.dot(cb16, hin.astype(BF16), preferred_element_type=F32)
        ys, xws, decs = [], [], []
        for r in range(SSM_HPG):
            sl = slice(r * SSM_HEAD_DIM, (r + 1) * SSM_HEAD_DIM)
            col = acs_c[:, r:r + 1]
            decay = jnp.exp(jnp.where(ri >= cj, col - acs_r[r:r + 1, :], NEG_INF))
            xr = xv[:, sl]
            xd = xr * dtc_v[:, r:r + 1]
            ydiag = _bdot(cb * decay, xd)
            ys.append(ydiag + jnp.exp(col) * yoff[:, sl] + d_ref[0][:, r:r + 1] * xr)
            lastr = last[:, r:r + 1]
            xws.append(xd * jnp.exp(lastr - col))
            decs.append(jnp.broadcast_to(jnp.exp(lastr), (1, SSM_HEAD_DIM)))
        y_ref[...] = jnp.concatenate(ys, axis=1)
        h_ref[g] = hin * jnp.concatenate(decs, axis=1) + _bdot_tn(bb, jnp.concatenate(xws, axis=1))

    return pl.pallas_call(
        body, grid=(nc, SSM_GROUPS), in_specs=_ssd_in_specs(nc, False),
        out_specs=[pl.BlockSpec((SSM_CHUNK, GP), lambda c, g: (c, g)),
                   pl.BlockSpec((1, 1, SSM_STATE, GP), lambda c, g: (c, g, 0, 0))],
        out_shape=[jax.ShapeDtypeStruct((l, SSM_D_INNER), F32),
                   jax.ShapeDtypeStruct((nc, SSM_GROUPS, SSM_STATE, GP), F32)],
        scratch_shapes=[pltpu.VMEM((SSM_GROUPS, SSM_STATE, GP), F32)],
        compiler_params=_params("arbitrary", "arbitrary"), name=name,
    )(xbc, xbc, xbc, dtc, dtr, a_log.reshape(SSM_GROUPS, 1, SSM_HPG), a_log.reshape(SSM_GROUPS, SSM_HPG, 1),
      d_skip.reshape(SSM_GROUPS, 1, SSM_HPG))


def _ssd_bwd(xbc, dtc, dtr, a_log, d_skip, hin, dy, name):
    l = xbc.shape[0]
    nc = l // SSM_CHUNK

    def body(x_ref, b_ref, c_ref, dtc_ref, dtr_ref, alr_ref, alc_ref, d_ref, hin_ref, dy_ref,
             dx_ref, db_ref, dc_ref, ddt_ref, dal_ref, dd_ref, dh_ref):
        c = pl.program_id(0)
        g = pl.program_id(1)

        @pl.when(c == 0)
        def _():
            dh_ref[g] = jnp.zeros((SSM_STATE, GP), F32)

        @pl.when((c == 0) & (g == 0))
        def _():
            dal_ref[...] = jnp.zeros_like(dal_ref)
            dd_ref[...] = jnp.zeros_like(dd_ref)

        xv = x_ref[...]
        dyv = dy_ref[...]
        bb = b_ref[...].astype(BF16)
        cb16 = c_ref[...].astype(BF16)
        dtc_v = dtc_ref[0]
        a_row = -jnp.exp(alr_ref[0])
        d_row = d_ref[0]
        ri, cj, acs_c, acs_r = _chunk_decay_terms(dtc_v, dtr_ref[0], a_row, -jnp.exp(alc_ref[0]))
        tri_u = (ri <= cj).astype(F32)
        last = acs_c[SSM_CHUNK - 1:SSM_CHUNK, :]
        cb = lax.dot_general(cb16, bb, NT_DIMS, preferred_element_type=F32)
        hin_v = hin_ref[0, 0]
        dhn = dh_ref[g]
        h16 = hin_v.astype(BF16)
        dh16 = dhn.astype(BF16)
        ch = jnp.dot(cb16, h16, preferred_element_type=F32)
        bdh = jnp.dot(bb, dh16, preferred_element_type=F32)
        lane = lax.broadcasted_iota(jnp.int32, (1, SSM_HPG), 1)
        gl_sum = jnp.zeros((SSM_CHUNK, SSM_CHUNK), F32)
        da_rect = jnp.zeros((SSM_CHUNK, SSM_HPG), F32)
        ds_extra = jnp.zeros((SSM_CHUNK, SSM_HPG), F32)
        ddt_x = jnp.zeros((SSM_CHUNK, SSM_HPG), F32)
        e_last = jnp.zeros((1, SSM_HPG), F32)
        dd_acc = jnp.zeros((1, SSM_HPG), F32)
        ws, xws, dxs, decs = [], [], [], []
        for r in range(SSM_HPG):
            sl = slice(r * SSM_HEAD_DIM, (r + 1) * SSM_HEAD_DIM)
            col = acs_c[:, r:r + 1]
            decay = jnp.exp(jnp.where(ri >= cj, col - acs_r[r:r + 1, :], NEG_INF))
            m = cb * decay
            dtcol = dtc_v[:, r:r + 1]
            xr = xv[:, sl]
            dyr = dyv[:, sl]
            xd = xr * dtcol
            lastr = last[:, r:r + 1]
            dte = jnp.exp(lastr - col)
            ec = jnp.exp(col)
            xw = xd * dte
            gl = _bdot_nt(dyr, xd) * decay
            gl_sum = gl_sum + gl
            rect = _bdot(tri_u, gl * cb)
            da_r = jnp.sum(jnp.where(ri > cj, rect, 0.0), axis=1, keepdims=True)
            dxd = _bdot_tn(m, dyr) + dte * bdh[:, sl]
            t1 = jnp.sum(dyr * ec * ch[:, sl], axis=1, keepdims=True)
            e = jnp.sum(xw * bdh[:, sl], axis=1, keepdims=True)
            er = jnp.sum(e, axis=0, keepdims=True) + jnp.exp(lastr) * jnp.sum(dhn[:, sl] * hin_v[:, sl])
            hot = lane == r
            da_rect = da_rect + jnp.where(hot, da_r, 0.0)
            ds_extra = ds_extra + jnp.where(hot, t1 - e, 0.0)
            ddt_x = ddt_x + jnp.where(hot, jnp.sum(dxd * xr, axis=1, keepdims=True), 0.0)
            e_last = e_last + jnp.where(hot, er, 0.0)
            dd_acc = dd_acc + jnp.where(hot, jnp.sum(dyr * xr), 0.0)
            dxs.append(dxd * dtcol + d_row[:, r:r + 1] * dyr)
            ws.append(ec * dyr)
            xws.append(xw)
            decs.append(jnp.broadcast_to(jnp.exp(lastr), (1, SSM_HEAD_DIM)))
        w16 = jnp.concatenate(ws, axis=1).astype(BF16)
        xw16 = jnp.concatenate(xws, axis=1).astype(BF16)
        gl16 = gl_sum.astype(BF16)
        dx_ref[...] = jnp.concatenate(dxs, axis=1)
        dc_ref[...] = (jnp.dot(gl16, bb, preferred_element_type=F32)
                       + lax.dot_general(w16, h16, NT_DIMS, preferred_element_type=F32))
        db_ref[...] = (lax.dot_general(gl16, cb16, TN_DIMS, preferred_element_type=F32)
                       + lax.dot_general(xw16, dh16, NT_DIMS, preferred_element_type=F32))
        dh_ref[g] = (dhn * jnp.concatenate(decs, axis=1)
                     + lax.dot_general(cb16, w16, TN_DIMS, preferred_element_type=F32))
        da = (da_rect + e_last
              + jnp.dot(tri_u, ds_extra, preferred_element_type=F32, precision=lax.Precision.HIGHEST))
        ddt_ref[0] = da * a_row + ddt_x
        dal_ref[g] += a_row * jnp.sum(da * dtc_v, axis=0, keepdims=True)
        dd_ref[g] += dd_acc

    small = pl.BlockSpec((SSM_GROUPS, 1, SSM_HPG), lambda c, g: (0, 0, 0))
    return pl.pallas_call(
        body, grid=(nc, SSM_GROUPS),
        in_specs=_ssd_in_specs(nc, True) + [
            pl.BlockSpec((1, 1, SSM_STATE, GP), lambda c, g: (nc - 1 - c, g, 0, 0)),
            pl.BlockSpec((SSM_CHUNK, GP), lambda c, g: (nc - 1 - c, g))],
        out_specs=[pl.BlockSpec((SSM_CHUNK, GP), lambda c, g: (nc - 1 - c, g)),
                   pl.BlockSpec((SSM_CHUNK, SSM_STATE), lambda c, g: (nc - 1 - c, g)),
                   pl.BlockSpec((SSM_CHUNK, SSM_STATE), lambda c, g: (nc - 1 - c, g)),
                   pl.BlockSpec((1, SSM_CHUNK, SSM_HPG), lambda c, g: (g, nc - 1 - c, 0)),
                   small, small],
        out_shape=[jax.ShapeDtypeStruct((l, SSM_D_INNER), F32), jax.ShapeDtypeStruct((l, SSM_BC_DIM), F32),
                   jax.ShapeDtypeStruct((l, SSM_BC_DIM), F32), jax.ShapeDtypeStruct((SSM_GROUPS, l, SSM_HPG), F32),
                   jax.ShapeDtypeStruct((SSM_GROUPS, 1, SSM_HPG), F32),
                   jax.ShapeDtypeStruct((SSM_GROUPS, 1, SSM_HPG), F32)],
        scratch_shapes=[pltpu.VMEM((SSM_GROUPS, SSM_STATE, GP), F32)],
        compiler_params=_params("arbitrary", "arbitrary"), name=name,
    )(xbc, xbc, xbc, dtc, dtr, a_log.reshape(SSM_GROUPS, 1, SSM_HPG), a_log.reshape(SSM_GROUPS, SSM_HPG, 1),
      d_skip.reshape(SSM_GROUPS, 1, SSM_HPG), hin, dy)


def _gatenorm_fwd(y, proj, w, name):
    l = y.shape[0]
    tl = _pick(l, (256, 128))

    def body(y_ref, z_ref, w_ref, o_ref):
        z = z_ref[...]
        yg = y_ref[...] * (z * _sigmoid(z))
        r = lax.rsqrt(jnp.mean(yg * yg, axis=-1, keepdims=True) + EPS)
        o_ref[...] = (yg * r * w_ref[...]).astype(o_ref.dtype)

    row = pl.BlockSpec((tl, SSM_D_INNER), lambda i: (i, 0))
    return pl.pallas_call(
        body, grid=(l // tl,), in_specs=[row, row, pl.BlockSpec((1, SSM_D_INNER), lambda i: (0, 0))],
        out_specs=row, out_shape=jax.ShapeDtypeStruct((l, SSM_D_INNER), BF16),
        compiler_params=_params("parallel"), name=name,
    )(y, proj, w.reshape(1, SSM_D_INNER))


def _gatenorm_bwd(g, y, proj, w, name):
    l = y.shape[0]
    tl = _pick(l, (256, 128))
    nt = l // tl

    def body(g_ref, y_ref, z_ref, w_ref, dy_ref, dz_ref, dw_ref, acc_ref):
        i = pl.program_id(0)

        @pl.when(i == 0)
        def _():
            acc_ref[...] = jnp.zeros_like(acc_ref)

        z = z_ref[...]
        yv = y_ref[...]
        s = _sigmoid(z)
        sz = z * s
        yg = yv * sz
        r = lax.rsqrt(jnp.mean(yg * yg, axis=-1, keepdims=True) + EPS)
        nrm = yg * r
        gv = g_ref[...]
        gw = gv * w_ref[...]
        dyg = r * (gw - nrm * jnp.mean(gw * nrm, axis=-1, keepdims=True))
        dy_ref[...] = dyg * sz
        dz_ref[...] = (dyg * yv * (s * (1.0 + z * (1.0 - s)))).astype(dz_ref.dtype)
        acc_ref[...] += jnp.sum((gv * nrm).reshape(tl // 8, 8, SSM_D_INNER), axis=0)

        @pl.when(i == nt - 1)
        def _():
            dw_ref[...] = jnp.sum(acc_ref[...], axis=0, keepdims=True)

    row = pl.BlockSpec((tl, SSM_D_INNER), lambda i: (i, 0))
    vec = pl.BlockSpec((1, SSM_D_INNER), lambda i: (0, 0))
    return pl.pallas_call(
        body, grid=(nt,), in_specs=[row, row, row, vec], out_specs=[row, row, vec],
        out_shape=[jax.ShapeDtypeStruct((l, SSM_D_INNER), F32), jax.ShapeDtypeStruct((l, SSM_D_INNER), BF16),
                   jax.ShapeDtypeStruct((1, SSM_D_INNER), F32)],
        scratch_shapes=[pltpu.VMEM((8, SSM_D_INNER), F32)], compiler_params=_params("arbitrary"), name=name,
    )(g, y, proj, w.reshape(1, SSM_D_INNER))


LANES = 128
ROPE_Q_CHUNKS = ATT_WIDTH // LANES
ROPE_K_CHUNKS = ATT_KV_WIDTH // LANES


def _rope_tables(positions):
    inv = ROPE_THETA ** (-jnp.arange(0, ROPE_DIM, 2, dtype=F32) / ROPE_DIM)
    ang = positions.astype(F32)[:, None] * inv
    cos, sin = jnp.cos(ang), jnp.sin(ang)
    l = positions.shape[0]
    rest = ATT_HEAD_DIM - ROPE_DIM
    ones, zeros = jnp.ones((l, rest), F32), jnp.zeros((l, rest), F32)
    z8 = jnp.zeros((l, ROPE_HALF), F32)
    cos_f = jnp.concatenate([cos, cos, ones], axis=1)
    sin_a = jnp.concatenate([-sin, z8, zeros], axis=1)
    sin_b = jnp.concatenate([z8, sin, zeros], axis=1)
    reps = LANES // ATT_HEAD_DIM
    return tuple(jnp.tile(t, (1, reps)) for t in (cos_f, sin_a, sin_b))


def _rope_fwd(proj, tables, name):
    l = proj.shape[0]
    tl = _pick(l, (256, 128))

    def body(p_ref, c_ref, sa_ref, sb_ref, o_ref):
        cos_f, sin_a, sin_b = c_ref[...], sa_ref[...], sb_ref[...]
        for k in range(ATT_QKV // LANES):
            sl = slice(k * LANES, (k + 1) * LANES)
            t = p_ref[:, sl]
            if k < ROPE_Q_CHUNKS + ROPE_K_CHUNKS:
                t = (t * cos_f + pltpu.roll(t, LANES - ROPE_HALF, 1) * sin_a + pltpu.roll(t, ROPE_HALF, 1) * sin_b)
            if k < ROPE_Q_CHUNKS:
                t = t * Q_SCALE
            o_ref[:, sl] = t.astype(o_ref.dtype)

    tab = pl.BlockSpec((tl, LANES), lambda i: (i, 0))
    return pl.pallas_call(
        body, grid=(l // tl,), in_specs=[pl.BlockSpec((tl, ATT_IN_DIM), lambda i: (i, 0)), tab, tab, tab],
        out_specs=pl.BlockSpec((tl, ATT_QKV), lambda i: (i, 0)),
        out_shape=jax.ShapeDtypeStruct((l, ATT_QKV), BF16), compiler_params=_params("parallel"), name=name,
    )(proj, *tables)


def _rope_bwd(dq, dk, dv, dgate, tables, name):
    l = dq.shape[0]
    tl = _pick(l, (256, 128))

    def body(dq_ref, dk_ref, dv_ref, dg_ref, c_ref, sa_ref, sb_ref, o_ref):
        cos_f, sin_a, sin_b = c_ref[...], sa_ref[...], sb_ref[...]

        def unrope(t):
            return t * cos_f + pltpu.roll(t * sin_a, ROPE_HALF, 1) + pltpu.roll(t * sin_b, LANES - ROPE_HALF, 1)

        for k in range(ROPE_Q_CHUNKS):
            sl = slice(k * LANES, (k + 1) * LANES)
            o_ref[:, sl] = unrope(dq_ref[:, sl] * Q_SCALE).astype(o_ref.dtype)
        for k in range(ROPE_K_CHUNKS):
            sl = slice(k * LANES, (k + 1) * LANES)
            o_ref[:, ATT_WIDTH + k * LANES:ATT_WIDTH + (k + 1) * LANES] = unrope(dk_ref[:, sl]).astype(o_ref.dtype)
        o_ref[:, ATT_WIDTH + ATT_KV_WIDTH:ATT_QKV] = dv_ref[...].astype(o_ref.dtype)
        o_ref[:, ATT_QKV:ATT_IN_DIM] = dg_ref[...].astype(o_ref.dtype)

    tab = pl.BlockSpec((tl, LANES), lambda i: (i, 0))
    wide = pl.BlockSpec((tl, ATT_WIDTH), lambda i: (i, 0))
    kv = pl.BlockSpec((tl, ATT_KV_WIDTH), lambda i: (i, 0))
    return pl.pallas_call(
        body, grid=(l // tl,), in_specs=[wide, kv, kv, wide, tab, tab, tab],
        out_specs=pl.BlockSpec((tl, ATT_IN_DIM), lambda i: (i, 0)),
        out_shape=jax.ShapeDtypeStruct((l, ATT_IN_DIM), BF16), compiler_params=_params("parallel"), name=name,
    )(dq, dk, dv, dgate, *tables)


K_COL_BLOCK = ATT_WIDTH // ATT_KV_WIDTH
V_COL_BLOCK = K_COL_BLOCK + 1
GATE_HALF = ATT_WIDTH // 2
GATE_COL_BLOCK = ATT_QKV // GATE_HALF


def _band_masks():
    ri = lax.broadcasted_iota(jnp.int32, (ATT_BLOCK, ATT_BLOCK), 0)
    cj = lax.broadcasted_iota(jnp.int32, (ATT_BLOCK, ATT_BLOCK), 1)
    return cj > ri, cj <= ri


def _attn_fwd(qkv, proj, sinks, name):
    l = qkv.shape[0]
    nb = l // ATT_BLOCK

    def body(sink_ref, q_ref, kp_ref, kc_ref, vp_ref, vc_ref, g0_ref, g1_ref, og_ref, o_ref, lse_ref):
        n = pl.program_id(0)
        mask_p, mask_c = _band_masks()
        mask_p = mask_p & (n > 0)
        lane = lax.broadcasted_iota(jnp.int32, (1, ATT_Q_HEADS), 1)
        lse_acc = jnp.zeros((ATT_BLOCK, ATT_Q_HEADS), F32)
        for h in range(ATT_Q_HEADS):
            kv = slice((h // ATT_GQA) * ATT_HEAD_DIM, (h // ATT_GQA + 1) * ATT_HEAD_DIM)
            sl = slice(h * ATT_HEAD_DIM, (h + 1) * ATT_HEAD_DIM)
            qh = q_ref[:, sl]
            sp = jnp.where(mask_p, lax.dot_general(qh, kp_ref[:, kv], NT_DIMS, preferred_element_type=F32), NEG_INF)
            sc = jnp.where(mask_c, lax.dot_general(qh, kc_ref[:, kv], NT_DIMS, preferred_element_type=F32), NEG_INF)
            sink = sink_ref[h]
            m = jnp.maximum(jnp.maximum(jnp.max(sp, axis=1, keepdims=True), jnp.max(sc, axis=1, keepdims=True)), sink)
            pp = jnp.exp(sp - m)
            pc = jnp.exp(sc - m)
            den = jnp.sum(pp, axis=1, keepdims=True) + jnp.sum(pc, axis=1, keepdims=True) + jnp.exp(sink - m)
            oh = (jnp.dot(pp.astype(BF16), vp_ref[:, kv], preferred_element_type=F32)
                  + jnp.dot(pc.astype(BF16), vc_ref[:, kv], preferred_element_type=F32)) / den
            o_ref[:, sl] = oh
            lse_acc = lse_acc + jnp.where(lane == h, m + jnp.log(den), 0.0)
        lse_ref[...] = lse_acc
        for half, g_ref in enumerate((g0_ref, g1_ref)):
            sl = slice(half * GATE_HALF, (half + 1) * GATE_HALF)
            gate = g_ref[...]
            og_ref[:, sl] = (o_ref[:, sl] * (gate * _sigmoid(gate))).astype(og_ref.dtype)

    def prev(n):
        return jnp.maximum(n - 1, 0)

    wide = pl.BlockSpec((ATT_BLOCK, ATT_WIDTH), lambda n: (n, 0))
    return pl.pallas_call(
        body, grid=(nb,),
        in_specs=[pl.BlockSpec(memory_space=pltpu.SMEM), wide,
                  pl.BlockSpec((ATT_BLOCK, ATT_KV_WIDTH), lambda n: (prev(n), K_COL_BLOCK)),
                  pl.BlockSpec((ATT_BLOCK, ATT_KV_WIDTH), lambda n: (n, K_COL_BLOCK)),
                  pl.BlockSpec((ATT_BLOCK, ATT_KV_WIDTH), lambda n: (prev(n), V_COL_BLOCK)),
                  pl.BlockSpec((ATT_BLOCK, ATT_KV_WIDTH), lambda n: (n, V_COL_BLOCK)),
                  pl.BlockSpec((ATT_BLOCK, GATE_HALF), lambda n: (n, GATE_COL_BLOCK)),
                  pl.BlockSpec((ATT_BLOCK, GATE_HALF), lambda n: (n, GATE_COL_BLOCK + 1))],
        out_specs=[wide, wide, pl.BlockSpec((ATT_BLOCK, ATT_Q_HEADS), lambda n: (n, 0))],
        out_shape=[jax.ShapeDtypeStruct((l, ATT_WIDTH), BF16), jax.ShapeDtypeStruct((l, ATT_WIDTH), F32),
                   jax.ShapeDtypeStruct((l, ATT_Q_HEADS), F32)],
        compiler_params=_params("parallel"), name=name,
    )(sinks, qkv, qkv, qkv, qkv, qkv, proj, proj)


def _attn_bwd(qkv, proj, sinks, o, lse, dog, name):
    l = qkv.shape[0]
    nb = l // ATT_BLOCK

    def body(sink_ref, q_ref, kp_ref, kc_ref, vp_ref, vc_ref, g0_ref, g1_ref, o_ref, lse_ref, dog_ref,
             dq_ref, dk_ref, dv_ref, dg_ref, ds_ref, ck_ref, cv_ref, do_ref):
        n = pl.program_id(0)

        @pl.when(n == 0)
        def _():
            ds_ref[...] = jnp.zeros_like(ds_ref)
            ck_ref[...] = jnp.zeros_like(ck_ref)
            cv_ref[...] = jnp.zeros_like(cv_ref)

        @pl.when(n == nb)
        def _():
            dk_ref[...] = ck_ref[...]
            dv_ref[...] = cv_ref[...]

        @pl.when(n < nb)
        def _():
            mask_p, mask_c = _band_masks()
            mask_p = mask_p & (n > 0)
            lane = lax.broadcasted_iota(jnp.int32, (1, ATT_Q_HEADS), 1)
            for half, g_ref in enumerate((g0_ref, g1_ref)):
                sl = slice(half * GATE_HALF, (half + 1) * GATE_HALF)
                gate = g_ref[...]
                s = _sigmoid(gate)
                dogv = dog_ref[:, sl]
                do_ref[:, sl] = dogv * (gate * s)
                dg_ref[:, sl] = dogv * o_ref[:, sl] * (s * (1.0 + gate * (1.0 - s)))
            lse_v = lse_ref[...]
            ds_acc = jnp.zeros((1, ATT_Q_HEADS), F32)
            for kvh in range(ATT_KV_HEADS):
                kv = slice(kvh * ATT_HEAD_DIM, (kvh + 1) * ATT_HEAD_DIM)
                kp, kc, vp, vc = kp_ref[:, kv], kc_ref[:, kv], vp_ref[:, kv], vc_ref[:, kv]
                dkp = jnp.zeros((ATT_BLOCK, ATT_HEAD_DIM), F32)
                dkc, dvp, dvc = dkp, dkp, dkp
                for gq in range(ATT_GQA):
                    h = kvh * ATT_GQA + gq
                    sl = slice(h * ATT_HEAD_DIM, (h + 1) * ATT_HEAD_DIM)
                    qh = q_ref[:, sl]
                    doh = do_ref[:, sl]
                    do16 = doh.astype(BF16)
                    lse_h = lse_v[:, h:h + 1]
                    pp = jnp.exp(jnp.where(mask_p, lax.dot_general(qh, kp, NT_DIMS, preferred_element_type=F32) - lse_h, NEG_INF))
                    pc = jnp.exp(jnp.where(mask_c, lax.dot_general(qh, kc, NT_DIMS, preferred_element_type=F32) - lse_h, NEG_INF))
                    delta = jnp.sum(doh * o_ref[:, sl], axis=1, keepdims=True)
                    dsp = (pp * (lax.dot_general(do16, vp, NT_DIMS, preferred_element_type=F32) - delta)).astype(BF16)
                    dsc = (pc * (lax.dot_general(do16, vc, NT_DIMS, preferred_element_type=F32) - delta)).astype(BF16)
                    dq_ref[:, sl] = (jnp.dot(dsp, kp, preferred_element_type=F32)
                                     + jnp.dot(dsc, kc, preferred_element_type=F32))
                    dkp = dkp + lax.dot_general(dsp, qh, TN_DIMS, preferred_element_type=F32)
                    dkc = dkc + lax.dot_general(dsc, qh, TN_DIMS, preferred_element_type=F32)
                    dvp = dvp + lax.dot_general(pp.astype(BF16), do16, TN_DIMS, preferred_element_type=F32)
                    dvc = dvc + lax.dot_general(pc.astype(BF16), do16, TN_DIMS, preferred_element_type=F32)
                    dsink = -jnp.sum(jnp.exp(sink_ref[h] - lse_h) * delta)
                    ds_acc = ds_acc + jnp.where(lane == h, dsink, 0.0)
                dk_ref[:, kv] = ck_ref[:, kv] + dkp
                dv_ref[:, kv] = cv_ref[:, kv] + dvp
                ck_ref[:, kv] = dkc
                cv_ref[:, kv] = dvc
            ds_ref[...] += ds_acc

    def cur(n):
        return jnp.minimum(n, nb - 1)

    def prev(n):
        return jnp.maximum(n - 1, 0)

    wide = pl.BlockSpec((ATT_BLOCK, ATT_WIDTH), lambda n: (cur(n), 0))
    kvo = pl.BlockSpec((ATT_BLOCK, ATT_KV_WIDTH), lambda n: (prev(n), 0))
    return pl.pallas_call(
        body, grid=(nb + 1,),
        in_specs=[pl.BlockSpec(memory_space=pltpu.SMEM), wide,
                  pl.BlockSpec((ATT_BLOCK, ATT_KV_WIDTH), lambda n: (prev(cur(n)), K_COL_BLOCK)),
                  pl.BlockSpec((ATT_BLOCK, ATT_KV_WIDTH), lambda n: (cur(n), K_COL_BLOCK)),
                  pl.BlockSpec((ATT_BLOCK, ATT_KV_WIDTH), lambda n: (prev(cur(n)), V_COL_BLOCK)),
                  pl.BlockSpec((ATT_BLOCK, ATT_KV_WIDTH), lambda n: (cur(n), V_COL_BLOCK)),
                  pl.BlockSpec((ATT_BLOCK, GATE_HALF), lambda n: (cur(n), GATE_COL_BLOCK)),
                  pl.BlockSpec((ATT_BLOCK, GATE_HALF), lambda n: (cur(n), GATE_COL_BLOCK + 1)),
                  wide, pl.BlockSpec((ATT_BLOCK, ATT_Q_HEADS), lambda n: (cur(n), 0)), wide],
        out_specs=[wide, kvo, kvo, wide, pl.BlockSpec((1, ATT_Q_HEADS), lambda n: (0, 0))],
        out_shape=[jax.ShapeDtypeStruct((l, ATT_WIDTH), F32), jax.ShapeDtypeStruct((l, ATT_KV_WIDTH), F32),
                   jax.ShapeDtypeStruct((l, ATT_KV_WIDTH), F32), jax.ShapeDtypeStruct((l, ATT_WIDTH), F32),
                   jax.ShapeDtypeStruct((1, ATT_Q_HEADS), F32)],
        scratch_shapes=[pltpu.VMEM((ATT_BLOCK, ATT_KV_WIDTH), F32), pltpu.VMEM((ATT_BLOCK, ATT_KV_WIDTH), F32),
                        pltpu.VMEM((ATT_BLOCK, ATT_WIDTH), F32)],
        compiler_params=_params("arbitrary"), name=name,
    )(sinks, qkv, qkv, qkv, qkv, qkv, proj, proj, o, lse, dog)


def _group_major(dt):
    l = dt.shape[0]
    d3 = dt[:, :SSM_HEADS].reshape(l, SSM_GROUPS, SSM_HPG)
    return jnp.transpose(d3, (1, 0, 2)), jnp.transpose(d3, (1, 2, 0))


def _local_step(x, positions, pre_norm, post_norm, w_ssm_in, conv_w, conv_b, dt_bias, a_log, d_skip, gate_norm,
                w_ssm_out, w_att_in, sinks, w_att_out, target):
    tables = _rope_tables(positions)
    dt_bias_pad = jnp.pad(dt_bias, ((0, 0), (0, SSM_DT_PAD - SSM_HEADS)))
    saved = []
    cur = x
    for i in range(DEPTH):
        j = i // 2
        h = _rmsnorm_fwd(cur, pre_norm[i], f"prenorm_fwd_{i}")
        if i % 2 == 0:
            proj = _matmul(h, w_ssm_in[j], "nn", F32, f"ssm_in_{i}")
            pre, xbc = _conv_fwd(proj, conv_w[j], conv_b[j], f"conv_fwd_{i}")
            dt = _dt_fwd(proj, dt_bias_pad[j:j + 1], f"dt_fwd_{i}")
            dtc, dtr = _group_major(dt)
            y, hin = _ssd_fwd(xbc, dtc, dtr, a_log[j], d_skip[j], f"ssd_fwd_{i}")
            act = _gatenorm_fwd(y, proj, gate_norm[j], f"gatenorm_fwd_{i}")
            ymix = _matmul(act, w_ssm_out[j], "nn", F32, f"ssm_out_{i}")
            saved.append(dict(x=cur, h=h, proj=proj, pre=pre, xbc=xbc, dtc=dtc, dtr=dtr, y=y, hin=hin, act=act, ymix=ymix))
        else:
            proj = _matmul(h, w_att_in[j], "nn", F32, f"att_in_{i}")
            qkv = _rope_fwd(proj, tables, f"rope_fwd_{i}")
            act, o, lse = _attn_fwd(qkv, proj, sinks[j], f"attn_fwd_{i}")
            ymix = _matmul(act, w_att_out[j], "nn", F32, f"att_out_{i}")
            saved.append(dict(x=cur, h=h, proj=proj, qkv=qkv, o=o, lse=lse, act=act, ymix=ymix))
        cur = _post_fwd(cur, ymix, post_norm[i], f"post_fwd_{i}")

    g, loss_lanes = _loss_grad(cur, target, "loss")

    gr = {k: [None] * 2 for k in ("ssm_w_in", "ssm_conv_w", "ssm_conv_b", "ssm_dt_bias", "ssm_a_log", "ssm_d",
                                  "ssm_gate_norm", "ssm_w_out", "att_w_in", "att_sinks", "att_w_out")}
    gr["pre_norm"] = [None] * DEPTH
    gr["post_norm"] = [None] * DEPTH
    for i in reversed(range(DEPTH)):
        j = i // 2
        s = saved[i]
        dymix, gr["post_norm"][i] = _rmsnorm_bwd(g, s["ymix"], post_norm[i], None, BF16, f"post_bwd_{i}")
        if i % 2 == 0:
            dact = _matmul(dymix, w_ssm_out[j], "nt", F32, f"ssm_out_dx_{i}")
            gr["ssm_w_out"][j] = _matmul(s["act"], dymix, "tn", F32, f"ssm_out_dw_{i}")
            dy, dz, gr["ssm_gate_norm"][j] = _gatenorm_bwd(dact, s["y"], s["proj"], gate_norm[j], f"gatenorm_bwd_{i}")
            dxs, db, dc, ddt8, dal, dd = _ssd_bwd(s["xbc"], s["dtc"], s["dtr"], a_log[j], d_skip[j], s["hin"], dy,
                                                  f"ssd_bwd_{i}")
            gr["ssm_a_log"][j] = dal.reshape(SSM_HEADS)
            gr["ssm_d"][j] = dd.reshape(SSM_HEADS)
            l = x.shape[0]
            ddt = jnp.pad(jnp.transpose(ddt8, (1, 0, 2)).reshape(l, SSM_HEADS), ((0, 0), (0, SSM_DT_PAD - SSM_HEADS)))
            ddt_raw, dbias = _dt_bwd(ddt, s["proj"], dt_bias_pad[j:j + 1], f"dt_bwd_{i}")
            gr["ssm_dt_bias"][j] = dbias[0, :SSM_HEADS]
            pieces, dcw, dcb = [], [], []
            for c0, dpiece, tag in ((0, dxs, "x"), (SSM_D_INNER, db, "b"), (SSM_D_INNER + SSM_BC_DIM, dc, "c")):
                du, dw_, db_ = _conv_bwd(dpiece, s["pre"], s["proj"], conv_w[j], c0, f"conv_bwd_{tag}_{i}")
                pieces.append(du)
                dcw.append(dw_)
                dcb.append(db_)
            gr["ssm_conv_w"][j] = jnp.concatenate(dcw, axis=1)
            gr["ssm_conv_b"][j] = jnp.concatenate(dcb, axis=1)[0]
            dproj = jnp.concatenate([dz] + pieces + [ddt_raw], axis=1)
            w_in, key = w_ssm_in[j], "ssm_w_in"
        else:
            dog = _matmul(dymix, w_att_out[j], "nt", F32, f"att_out_dx_{i}")
            gr["att_w_out"][j] = _matmul(s["act"], dymix, "tn", F32, f"att_out_dw_{i}")
            dq, dk, dv, dgate, dsk = _attn_bwd(s["qkv"], s["proj"], sinks[j], s["o"], s["lse"], dog, f"attn_bwd_{i}")
            gr["att_sinks"][j] = dsk[0]
            dproj = _rope_bwd(dq, dk, dv, dgate, tables, f"rope_bwd_{i}")
            w_in, key = w_att_in[j], "att_w_in"
        dh = _matmul(dproj, w_in, "nt", F32, f"in_dx_{i}")
        gr[key][j] = _matmul(s["h"], dproj, "tn", F32, f"in_dw_{i}")
        g, gr["pre_norm"][i] = _rmsnorm_bwd(dh, s["x"], pre_norm[i], g, F32, f"prenorm_bwd_{i}")
    grads = {k: jnp.stack([v.reshape(v.shape[-1]) if k in ("pre_norm", "post_norm", "ssm_gate_norm") else v for v in vs])
             for k, vs in gr.items()}
    return loss_lanes, g, grads


N_CHIPS = 4
N_DEV = 8
MESH = pl.DeviceIdType.MESH
ANY = pl.BlockSpec(memory_space=pl.ANY)


def _place():
    x, y, c = lax.axis_index("x"), lax.axis_index("y"), lax.axis_index("c")
    return x, y, c, 2 * x + y


def _chip_gather(shards, name):
    n = len(shards)

    def body(*refs):
        ins, outs = refs[:n], refs[n:2 * n]
        send_sems, recv_sems, local_sems = refs[2 * n:]
        _, _, c, s = _place()
        local = [pltpu.make_async_copy(ins[w], outs[w].at[s], local_sems.at[w]) for w in range(n)]
        for cp in local:
            cp.start()

        def remote(w, t):
            return pltpu.make_async_remote_copy(
                src_ref=ins[w], dst_ref=outs[w].at[s], send_sem=send_sems.at[w, t], recv_sem=recv_sems.at[w, s],
                device_id=(t // 2, t % 2, c), device_id_type=MESH)

        def arrival(w, t):
            return pltpu.make_async_remote_copy(
                src_ref=ins[w], dst_ref=outs[w].at[t], send_sem=send_sems.at[w, t], recv_sem=recv_sems.at[w, t],
                device_id=(t // 2, t % 2, c), device_id_type=MESH)

        for t in range(N_CHIPS):
            @pl.when(s != t)
            def _():
                for w in range(n):
                    remote(w, t).start()
        for t in range(N_CHIPS):
            @pl.when(s != t)
            def _():
                for w in range(n):
                    remote(w, t).wait_send()
                    arrival(w, t).wait_recv()
        for cp in local:
            cp.wait()

    return pl.pallas_call(
        body, in_specs=[ANY] * n, out_specs=[ANY] * n,
        out_shape=[jax.ShapeDtypeStruct((N_CHIPS,) + a.shape, a.dtype) for a in shards],
        scratch_shapes=[pltpu.SemaphoreType.DMA((n, N_CHIPS)), pltpu.SemaphoreType.DMA((n, N_CHIPS)),
                        pltpu.SemaphoreType.DMA((n,))],
        name=name,
    )(*shards)


def _pair_swap(parts, name):
    n = len(parts)

    def body(*refs):
        ins, outs = refs[:n], refs[n:2 * n]
        send_sems, recv_sems = refs[2 * n:]
        x, y, c, _ = _place()
        cps = [pltpu.make_async_remote_copy(
            src_ref=ins[w].at[1 - c], dst_ref=outs[w], send_sem=send_sems.at[w], recv_sem=recv_sems.at[w],
            device_id=(x, y, 1 - c), device_id_type=MESH) for w in range(n)]
        for cp in cps:
            cp.start()
        for cp in cps:
            cp.wait()

    return pl.pallas_call(
        body, in_specs=[ANY] * n, out_specs=[ANY] * n,
        out_shape=[jax.ShapeDtypeStruct(a.shape[1:], a.dtype) for a in parts],
        scratch_shapes=[pltpu.SemaphoreType.DMA((n,)), pltpu.SemaphoreType.DMA((n,))],
        name=name,
    )(*parts)


def _chip_scatter(parts, name):
    n = len(parts)
    rows = [a.shape[0] // N_CHIPS for a in parts]

    def body(*refs):
        ins, outs = refs[:n], refs[n:2 * n]
        send_sems, recv_sems, local_sems = refs[2 * n:]
        _, _, c, s = _place()

        def block(w, t):
            return ins[w].at[pl.ds(t * rows[w], rows[w])]

        local = [pltpu.make_async_copy(block(w, s), outs[w].at[s], local_sems.at[w]) for w in range(n)]
        for cp in local:
            cp.start()

        def remote(w, t):
            return pltpu.make_async_remote_copy(
                src_ref=block(w, t), dst_ref=outs[w].at[s], send_sem=send_sems.at[w, t], recv_sem=recv_sems.at[w, s],
                device_id=(t // 2, t % 2, c), device_id_type=MESH)

        def arrival(w, t):
            return pltpu.make_async_remote_copy(
                src_ref=block(w, t), dst_ref=outs[w].at[t], send_sem=send_sems.at[w, t], recv_sem=recv_sems.at[w, t],
                device_id=(t // 2, t % 2, c), device_id_type=MESH)

        for t in range(N_CHIPS):
            @pl.when(s != t)
            def _():
                for w in range(n):
                    remote(w, t).start()
        for t in range(N_CHIPS):
            @pl.when(s != t)
            def _():
                for w in range(n):
                    remote(w, t).wait_send()
                    arrival(w, t).wait_recv()
        for cp in local:
            cp.wait()

    return pl.pallas_call(
        body, in_specs=[ANY] * n, out_specs=[ANY] * n,
        out_shape=[jax.ShapeDtypeStruct((N_CHIPS, r, a.shape[1]), a.dtype) for a, r in zip(parts, rows)],
        scratch_shapes=[pltpu.SemaphoreType.DMA((n, N_CHIPS)), pltpu.SemaphoreType.DMA((n, N_CHIPS)),
                        pltpu.SemaphoreType.DMA((n,))],
        name=name,
    )(*parts)


def _pair_merge(parts, name):
    n = len(parts)

    def body(*refs):
        ins, outs = refs[:n], refs[n:2 * n]
        send_sems, recv_sems, local_sems = refs[2 * n:]
        x, y, c, _ = _place()
        local = [pltpu.make_async_copy(ins[w], outs[w].at[c], local_sems.at[w]) for w in range(n)]
        cps = [pltpu.make_async_remote_copy(
            src_ref=ins[w], dst_ref=outs[w].at[c], send_sem=send_sems.at[w], recv_sem=recv_sems.at[w],
            device_id=(x, y, 1 - c), device_id_type=MESH) for w in range(n)]
        arrivals = [pltpu.make_async_remote_copy(
            src_ref=ins[w], dst_ref=outs[w].at[1 - c], send_sem=send_sems.at[w], recv_sem=recv_sems.at[w],
            device_id=(x, y, 1 - c), device_id_type=MESH) for w in range(n)]
        for cp in local + cps:
            cp.start()
        for w in range(n):
            cps[w].wait_send()
            arrivals[w].wait_recv()
        for cp in local:
            cp.wait()

    return pl.pallas_call(
        body, in_specs=[ANY] * n, out_specs=[ANY] * n,
        out_shape=[jax.ShapeDtypeStruct((2,) + a.shape, a.dtype) for a in parts],
        scratch_shapes=[pltpu.SemaphoreType.DMA((n,)), pltpu.SemaphoreType.DMA((n,)), pltpu.SemaphoreType.DMA((n,))],
        name=name,
    )(*parts)


def _all_gather_small(a, name):
    def body(in_ref, out_ref, send_sems, recv_sems, local_sem):
        x, y, c, _ = _place()
        me = 4 * x + 2 * y + c
        local = pltpu.make_async_copy(in_ref, out_ref.at[me], local_sem)
        local.start()

        def remote(d):
            return pltpu.make_async_remote_copy(
                src_ref=in_ref, dst_ref=out_ref.at[me], send_sem=send_sems.at[d], recv_sem=recv_sems.at[me],
                device_id=(d // 4, (d // 2) % 2, d % 2), device_id_type=MESH)

        def arrival(d):
            return pltpu.make_async_remote_copy(
                src_ref=in_ref, dst_ref=out_ref.at[d], send_sem=send_sems.at[d], recv_sem=recv_sems.at[d],
                device_id=(d // 4, (d // 2) % 2, d % 2), device_id_type=MESH)

        for d in range(N_DEV):
            @pl.when(me != d)
            def _():
                remote(d).start()
        for d in range(N_DEV):
            @pl.when(me != d)
            def _():
                remote(d).wait_send()
                arrival(d).wait_recv()
        local.wait()

    return pl.pallas_call(
        body, in_specs=[ANY], out_specs=ANY, out_shape=jax.ShapeDtypeStruct((N_DEV,) + a.shape, a.dtype),
        scratch_shapes=[pltpu.SemaphoreType.DMA((N_DEV,)), pltpu.SemaphoreType.DMA((N_DEV,)), pltpu.SemaphoreType.DMA],
        name=name,
    )(a)


def _reduce_tile(rows):
    return _pick(rows, (256, 16))


def _pair_add(full, other, layer, name):
    _, rows, cols = full.shape
    tr = _reduce_tile(rows)

    def body(layer_ref, a_ref, b_ref, o_ref):
        o_ref[...] = (a_ref[0] + b_ref[...]).astype(o_ref.dtype)

    return pl.pallas_call(
        body,
        grid_spec=pltpu.PrefetchScalarGridSpec(
            num_scalar_prefetch=1, grid=(rows // tr,),
            in_specs=[pl.BlockSpec((1, tr, cols), lambda i, lr: (lr[0], i, 0)), pl.BlockSpec((tr, cols), lambda i, lr: (i, 0))],
            out_specs=pl.BlockSpec((tr, cols), lambda i, lr: (i, 0))),
        out_shape=jax.ShapeDtypeStruct((rows, cols), BF16), compiler_params=_params("parallel"), name=name,
    )(layer, full, other)


def _sum_slots(a, name):
    n, rows, cols = a.shape
    tr = _reduce_tile(rows)

    def body(a_ref, o_ref):
        acc = a_ref[0].astype(F32)
        for k in range(1, n):
            acc = acc + a_ref[k].astype(F32)
        o_ref[...] = acc

    return pl.pallas_call(
        body, grid=(rows // tr,), in_specs=[pl.BlockSpec((n, tr, cols), lambda i: (0, i, 0))],
        out_specs=pl.BlockSpec((tr, cols), lambda i: (i, 0)),
        out_shape=jax.ShapeDtypeStruct((rows, cols), F32), compiler_params=_params("parallel"), name=name,
    )(a)


def _adamw(w, g, m, v, name):
    rows, cols = w.shape
    tr = _pick(rows, (256, 8))

    def body(w_ref, g_ref, m_ref, v_ref, d_ref, nm_ref, nv_ref):
        gv = g_ref[...]
        mn = ADAM_B1 * m_ref[...] + (1.0 - ADAM_B1) * gv
        vn = ADAM_B2 * v_ref[...] + (1.0 - ADAM_B2) * jnp.square(gv)
        m_hat = mn / (1.0 - ADAM_B1 ** ADAM_STEP)
        v_hat = vn / (1.0 - ADAM_B2 ** ADAM_STEP)
        d_ref[...] = -ADAM_LR * (m_hat / (jnp.sqrt(v_hat) + ADAM_EPS) + ADAM_WD * w_ref[...])
        nm_ref[...] = mn
        nv_ref[...] = vn

    blk = pl.BlockSpec((tr, cols), lambda i: (i, 0))
    return pl.pallas_call(
        body, grid=(rows // tr,), in_specs=[blk] * 4, out_specs=[blk] * 3,
        out_shape=[jax.ShapeDtypeStruct((rows, cols), F32)] * 3, compiler_params=_params("parallel"), name=name,
    )(w, g, m, v)


BIG = ("ssm_w_in", "ssm_w_out", "att_w_in", "att_w_out")
SHARDED = BIG + ("ssm_conv_w",)
SMALL = ("pre_norm", "post_norm", "ssm_conv_b", "ssm_dt_bias", "ssm_a_log", "ssm_d", "ssm_gate_norm", "att_sinks")
WEIGHTS = ("pre_norm", "post_norm", "ssm_w_in", "ssm_conv_w", "ssm_conv_b", "ssm_dt_bias", "ssm_a_log", "ssm_d",
           "ssm_gate_norm", "ssm_w_out", "att_w_in", "att_sinks", "att_w_out")


def _cols_to_whole(g):
    _, two, rows, cols = g.shape
    return jnp.transpose(g, (1, 2, 0, 3)).reshape(two, rows, N_CHIPS * cols)


def _rows_to_whole(g):
    _, two, rows, cols = g.shape
    return jnp.transpose(g, (1, 0, 2, 3)).reshape(two, N_CHIPS * rows, cols)


def _cols_by_chip(g):
    two, rows, cols = g.shape
    return jnp.transpose(g.reshape(two, rows, N_CHIPS, cols // N_CHIPS), (0, 2, 1, 3)).reshape(two, N_CHIPS * rows, cols // N_CHIPS)


def _pack_small(tree, keys):
    flat = jnp.concatenate([tree[k].reshape(-1) for k in keys])
    rows = -(-flat.shape[0] // (8 * LANES)) * 8
    return jnp.pad(flat, (0, rows * LANES - flat.shape[0])).reshape(rows, LANES)


def _unpack_small(packed, shapes, keys):
    flat = packed.reshape(-1)
    out, at = {}, 0
    for k in keys:
        n = 1
        for dim in shapes[k]:
            n *= dim
        out[k] = flat[at:at + n].reshape(shapes[k])
        at += n
    return out


def kernel(x, positions, pre_norm, post_norm, ssm_w_in, ssm_conv_w, ssm_conv_b, ssm_dt_bias, ssm_a_log, ssm_d, ssm_gate_norm, ssm_w_out, att_w_in, att_sinks, att_w_out, loss_target, m_pre_norm, m_post_norm, m_ssm_w_in, m_ssm_conv_w, m_ssm_conv_b, m_ssm_dt_bias, m_ssm_a_log, m_ssm_d, m_ssm_gate_norm, m_ssm_w_out, m_att_w_in, m_att_sinks, m_att_w_out, v_pre_norm, v_post_norm, v_ssm_w_in, v_ssm_conv_w, v_ssm_conv_b, v_ssm_dt_bias, v_ssm_a_log, v_ssm_d, v_ssm_gate_norm, v_ssm_w_out, v_att_w_in, v_att_sinks, v_att_w_out):
    w = dict(pre_norm=pre_norm, post_norm=post_norm, ssm_w_in=ssm_w_in, ssm_conv_w=ssm_conv_w, ssm_conv_b=ssm_conv_b,
             ssm_dt_bias=ssm_dt_bias, ssm_a_log=ssm_a_log, ssm_d=ssm_d, ssm_gate_norm=ssm_gate_norm, ssm_w_out=ssm_w_out,
             att_w_in=att_w_in, att_sinks=att_sinks, att_w_out=att_w_out)
    m = dict(pre_norm=m_pre_norm, post_norm=m_post_norm, ssm_w_in=m_ssm_w_in, ssm_conv_w=m_ssm_conv_w, ssm_conv_b=m_ssm_conv_b,
             ssm_dt_bias=m_ssm_dt_bias, ssm_a_log=m_ssm_a_log, ssm_d=m_ssm_d, ssm_gate_norm=m_ssm_gate_norm,
             ssm_w_out=m_ssm_w_out, att_w_in=m_att_w_in, att_sinks=m_att_sinks, att_w_out=m_att_w_out)
    v = dict(pre_norm=v_pre_norm, post_norm=v_post_norm, ssm_w_in=v_ssm_w_in, ssm_conv_w=v_ssm_conv_w, ssm_conv_b=v_ssm_conv_b,
             ssm_dt_bias=v_ssm_dt_bias, ssm_a_log=v_ssm_a_log, ssm_d=v_ssm_d, ssm_gate_norm=v_ssm_gate_norm,
             ssm_w_out=v_ssm_w_out, att_w_in=v_att_w_in, att_sinks=v_att_sinks, att_w_out=v_att_w_out)
    c = lax.axis_index("c")
    chip = 2 * lax.axis_index("x") + lax.axis_index("y")

    g_in, g_out, g_ain, g_aout, g_cw = _chip_gather(
        [ssm_w_in.astype(BF16), ssm_w_out.astype(BF16), att_w_in.astype(BF16), att_w_out.astype(BF16), ssm_conv_w],
        "gather_weights")
    w_in_full = jnp.pad(_cols_to_whole(g_in), ((0, 0), (0, 0), (0, SSM_IN_PAD - SSM_IN_DIM)))
    loss_lanes, grad_x, gr = _local_step(
        x[0], positions[0], pre_norm, post_norm, w_in_full, _cols_to_whole(g_cw), ssm_conv_b, ssm_dt_bias, ssm_a_log,
        ssm_d, ssm_gate_norm, _rows_to_whole(g_out), _cols_to_whole(g_ain), att_sinks, _rows_to_whole(g_aout),
        loss_target[0])
    loss = lax.psum(0.5 * jnp.sum(loss_lanes) / D_MODEL, ("x", "y", "c"))

    parts = [_cols_by_chip(gr["ssm_w_in"][:, :, :SSM_IN_DIM]), gr["ssm_w_out"], _cols_by_chip(gr["att_w_in"]),
             gr["att_w_out"]]
    from_sibling = _pair_swap(parts, "reduce_pair_swap")
    layer = jnp.reshape(c, (1,)).astype(jnp.int32)
    chip_sums = [_pair_add(p, o, layer, f"reduce_pair_add_{k}") for k, (p, o) in enumerate(zip(parts, from_sibling))]
    by_chip = _chip_scatter(chip_sums, "reduce_chip_scatter")
    mine = [_sum_slots(a, f"reduce_chip_sum_{k}") for k, a in enumerate(by_chip)]
    both = _pair_merge(mine, "reduce_pair_merge")
    grads = {k: g.reshape(w[k].shape) for k, g in zip(BIG, both)}

    small_keys = SMALL + ("ssm_conv_w",)
    small_shapes = {k: w[k].shape for k in SMALL}
    small_shapes["ssm_conv_w"] = gr["ssm_conv_w"].shape
    small_sum = _sum_slots(_all_gather_small(_pack_small(gr, small_keys), "reduce_small_gather"), "reduce_small_sum")
    grads.update(_unpack_small(small_sum, small_shapes, small_keys))
    conv_cols = ssm_conv_w.shape[2]
    grads["ssm_conv_w"] = lax.dynamic_slice_in_dim(grads["ssm_conv_w"], chip * conv_cols, conv_cols, axis=2)

    delta, new_m, new_v = {}, {}, {}
    for k in SHARDED:
        shp = w[k].shape
        two_d = (shp[0] * shp[1], shp[2])
        d_, m_, v_ = _adamw(w[k].reshape(two_d), grads[k].reshape(two_d), m[k].reshape(two_d), v[k].reshape(two_d),
                            f"adamw_{k}")
        delta[k], new_m[k], new_v[k] = d_.reshape(shp), m_.reshape(shp), v_.reshape(shp)
    d_, m_, v_ = _adamw(_pack_small(w, SMALL), _pack_small(grads, SMALL), _pack_small(m, SMALL), _pack_small(v, SMALL),
                        "adamw_small")
    delta.update(_unpack_small(d_, small_shapes, SMALL))
    new_m.update(_unpack_small(m_, small_shapes, SMALL))
    new_v.update(_unpack_small(v_, small_shapes, SMALL))

    return (loss, grad_x[None], *[grads[k] for k in WEIGHTS], *[delta[k] for k in WEIGHTS],
            *[new_m[k] for k in WEIGHTS], *[new_v[k] for k in WEIGHTS])
```

```python
import functools

import jax
import jax.numpy as jnp
from jax import lax
from jax.experimental import pallas as pl
from jax.experimental.pallas import tpu as pltpu

F32 = jnp.float32
BF16 = jnp.bfloat16
EPS = 1e-6
NEG_INF = float("-inf")

D_MODEL = 1024
DEPTH = 4
SSM_D_INNER = 2048
SSM_HEAD_DIM = 64
SSM_HEADS = 32
SSM_GROUPS = 8
SSM_HPG = 4
SSM_STATE = 128
SSM_CONV = 4
SSM_CHUNK = 128
SSM_BC_DIM = 1024
SSM_CONV_DIM = 4096
SSM_IN_DIM = 6176
SSM_IN_PAD = 6272
SSM_DT_PAD = 128
ATT_HEAD_DIM = 64
ATT_Q_HEADS = 16
ATT_KV_HEADS = 4
ATT_GQA = 4
ATT_WIDTH = 1024
ATT_KV_WIDTH = 256
ATT_IN_DIM = 2560
ATT_QKV = ATT_WIDTH + 2 * ATT_KV_WIDTH
ATT_BLOCK = 128
ROPE_THETA = 500000.0
ROPE_DIM = 16
ROPE_HALF = 8
Q_SCALE = ATT_HEAD_DIM ** -0.5

ADAM_LR = 0.001
ADAM_B1 = 0.9
ADAM_B2 = 0.999
ADAM_EPS = 1e-08
ADAM_WD = 0.01
ADAM_STEP = 10

VMEM_LIMIT_BYTES = 48 * 1024 * 1024
NT_DIMS = (((1,), (1,)), ((), ()))
TN_DIMS = (((0,), (0,)), ((), ()))


def _params(*sem):
    return pltpu.CompilerParams(dimension_semantics=sem, vmem_limit_bytes=VMEM_LIMIT_BYTES)


def _pick(n, cands):
    for c in cands:
        if n % c == 0:
            return c
    return n


def _sigmoid(v):
    return 1.0 / (1.0 + jnp.exp(-v))


def _bdot(a, b):
    return jnp.dot(a.astype(BF16), b.astype(BF16), preferred_element_type=F32)


def _bdot_nt(a, b):
    return lax.dot_general(a.astype(BF16), b.astype(BF16), NT_DIMS, preferred_element_type=F32)


def _bdot_tn(a, b):
    return lax.dot_general(a.astype(BF16), b.astype(BF16), TN_DIMS, preferred_element_type=F32)


def _matmul(a, b, mode, out_dtype, name):
    if mode == "nn":
        (m, k), n = a.shape, b.shape[1]
    elif mode == "nt":
        (m, k), n = a.shape, b.shape[0]
    else:
        (k, m), n = a.shape, b.shape[1]
    tm = _pick(m, (1024, 512) if mode == "tn" else (512,))
    tn = _pick(n, (896, 640, 512))
    tk = _pick(k, (2048, 1024, 896, 512))
    nk = k // tk
    dims = {"nn": (((1,), (0,)), ((), ())), "nt": NT_DIMS, "tn": TN_DIMS}[mode]

    def body(a_ref, b_ref, o_ref, acc_ref):
        kk = pl.program_id(2)
        part = lax.dot_general(a_ref[...], b_ref[...], dims, preferred_element_type=F32)
        if nk == 1:
            o_ref[...] = part.astype(o_ref.dtype)
        else:
            @pl.when(kk == 0)
            def _():
                acc_ref[...] = part

            @pl.when(kk > 0)
            def _():
                acc_ref[...] += part

            @pl.when(kk == nk - 1)
            def _():
                o_ref[...] = acc_ref[...].astype(o_ref.dtype)

    if mode == "nn":
        a_spec = pl.BlockSpec((tm, tk), lambda j, i, kk: (i, kk))
        b_spec = pl.BlockSpec((tk, tn), lambda j, i, kk: (kk, j))
    elif mode == "nt":
        a_spec = pl.BlockSpec((tm, tk), lambda j, i, kk: (i, kk))
        b_spec = pl.BlockSpec((tn, tk), lambda j, i, kk: (j, kk))
    else:
        a_spec = pl.BlockSpec((tk, tm), lambda j, i, kk: (kk, i))
        b_spec = pl.BlockSpec((tk, tn), lambda j, i, kk: (kk, j))
    return pl.pallas_call(
        body, grid=(n // tn, m // tm, nk), in_specs=[a_spec, b_spec],
        out_specs=pl.BlockSpec((tm, tn), lambda j, i, kk: (i, j)),
        out_shape=jax.ShapeDtypeStruct((m, n), out_dtype),
        scratch_shapes=[pltpu.VMEM((tm, tn), F32)],
        compiler_params=_params("parallel", "parallel", "arbitrary"), name=name,
    )(a, b)


def _row_tile(l):
    return _pick(l, (512, 256, 128))


def _rmsnorm_fwd(x, w, name):
    l, d = x.shape
    tl = _row_tile(l)

    def body(x_ref, w_ref, o_ref):
        xv = x_ref[...]
        r = lax.rsqrt(jnp.mean(xv * xv, axis=-1, keepdims=True) + EPS)
        o_ref[...] = (xv * r * w_ref[...]).astype(o_ref.dtype)

    return pl.pallas_call(
        body, grid=(l // tl,),
        in_specs=[pl.BlockSpec((tl, d), lambda i: (i, 0)), pl.BlockSpec((1, d), lambda i: (0, 0))],
        out_specs=pl.BlockSpec((tl, d), lambda i: (i, 0)),
        out_shape=jax.ShapeDtypeStruct((l, d), BF16), compiler_params=_params("parallel"), name=name,
    )(x, w.reshape(1, d))


def _post_fwd(x, y, w, name):
    l, d = x.shape
    tl = _row_tile(l)

    def body(x_ref, y_ref, w_ref, o_ref):
        yv = y_ref[...]
        r = lax.rsqrt(jnp.mean(yv * yv, axis=-1, keepdims=True) + EPS)
        o_ref[...] = x_ref[...] + yv * r * w_ref[...]

    return pl.pallas_call(
        body, grid=(l // tl,),
        in_specs=[pl.BlockSpec((tl, d), lambda i: (i, 0)), pl.BlockSpec((tl, d), lambda i: (i, 0)),
                  pl.BlockSpec((1, d), lambda i: (0, 0))],
        out_specs=pl.BlockSpec((tl, d), lambda i: (i, 0)),
        out_shape=jax.ShapeDtypeStruct((l, d), F32), compiler_params=_params("parallel"), name=name,
    )(x, y, w.reshape(1, d))


def _rmsnorm_bwd(g, y, w, resid, out_dtype, name):
    l, d = y.shape
    tl = _row_tile(l)
    nt = l // tl
    has_resid = resid is not None

    def body(*refs):
        if has_resid:
            g_ref, y_ref, w_ref, r_ref, dy_ref, dw_ref, acc_ref = refs
        else:
            g_ref, y_ref, w_ref, dy_ref, dw_ref, acc_ref = refs
        i = pl.program_id(0)

        @pl.when(i == 0)
        def _():
            acc_ref[...] = jnp.zeros_like(acc_ref)

        yv = y_ref[...]
        gv = g_ref[...].astype(F32)
        r = lax.rsqrt(jnp.mean(yv * yv, axis=-1, keepdims=True) + EPS)
        nrm = yv * r
        gw = gv * w_ref[...]
        dy = r * (gw - nrm * jnp.mean(gw * nrm, axis=-1, keepdims=True))
        if has_resid:
            dy = dy + r_ref[...]
        dy_ref[...] = dy.astype(dy_ref.dtype)
        acc_ref[...] += jnp.sum((gv * nrm).reshape(tl // 8, 8, d), axis=0)

        @pl.when(i == nt - 1)
        def _():
            dw_ref[...] = jnp.sum(acc_ref[...], axis=0, keepdims=True)

    row = pl.BlockSpec((tl, d), lambda i: (i, 0))
    vec = pl.BlockSpec((1, d), lambda i: (0, 0))
    ins = [g, y, w.reshape(1, d)] + ([resid] if has_resid else [])
    return pl.pallas_call(
        body, grid=(nt,), in_specs=[row, row, vec] + ([row] if has_resid else []),
        out_specs=[row, vec],
        out_shape=[jax.ShapeDtypeStruct((l, d), out_dtype), jax.ShapeDtypeStruct((1, d), F32)],
        scratch_shapes=[pltpu.VMEM((8, d), F32)], compiler_params=_params("arbitrary"), name=name,
    )(*ins)


def _loss_grad(y, t, name):
    l, d = y.shape
    tl = _row_tile(l)

    def body(y_ref, t_ref, dy_ref, ls_ref):
        @pl.when(pl.program_id(0) == 0)
        def _():
            ls_ref[...] = jnp.zeros_like(ls_ref)

        e = y_ref[...] - t_ref[...]
        dy_ref[...] = e * (1.0 / d)
        ls_ref[...] += jnp.sum((e * e).reshape(tl // 8, 8, d), axis=0)

    row = pl.BlockSpec((tl, d), lambda i: (i, 0))
    return pl.pallas_call(
        body, grid=(l // tl,), in_specs=[row, row],
        out_specs=[row, pl.BlockSpec((8, d), lambda i: (0, 0))],
        out_shape=[jax.ShapeDtypeStruct((l, d), F32), jax.ShapeDtypeStruct((8, d), F32)],
        compiler_params=_params("arbitrary"), name=name,
    )(y, t)


CONV_COLS = 512
HALO = 8


def _conv_fwd(proj, cw, cb, name):
    l = proj.shape[0]
    tl = _row_tile(l)
    off = SSM_D_INNER // CONV_COLS

    def body(u_ref, halo_ref, w_ref, b_ref, pre_ref, act_ref, ext_ref):
        i = pl.program_id(1)
        ext_ref[0:HALO, :] = jnp.where(i > 0, halo_ref[...], 0.0)
        ext_ref[HALO:HALO + tl, :] = u_ref[...]
        acc = jnp.broadcast_to(b_ref[...], (tl, CONV_COLS))
        for k in range(SSM_CONV):
            acc = acc + w_ref[k:k + 1, :] * ext_ref[pl.ds(HALO - SSM_CONV + 1 + k, tl), :]
        pre_ref[...] = acc
        act_ref[...] = acc * _sigmoid(acc)

    hb = tl // HALO
    out = pl.BlockSpec((tl, CONV_COLS), lambda j, i: (i, j))
    return pl.pallas_call(
        body, grid=(SSM_CONV_DIM // CONV_COLS, l // tl),
        in_specs=[pl.BlockSpec((tl, CONV_COLS), lambda j, i: (i, off + j)),
                  pl.BlockSpec((HALO, CONV_COLS), lambda j, i: (jnp.maximum(i * hb - 1, 0), off + j)),
                  pl.BlockSpec((SSM_CONV, CONV_COLS), lambda j, i: (0, j)),
                  pl.BlockSpec((1, CONV_COLS), lambda j, i: (0, j))],
        out_specs=[out, out],
        out_shape=[jax.ShapeDtypeStruct((l, SSM_CONV_DIM), F32)] * 2,
        scratch_shapes=[pltpu.VMEM((tl + HALO, CONV_COLS), F32)],
        compiler_params=_params("parallel", "arbitrary"), name=name,
    )(proj, proj, cw, cb.reshape(1, SSM_CONV_DIM))


def _conv_bwd(dact, pre, proj, cw, c0, name):
    l, width = dact.shape
    tl = _row_tile(l)
    nt = l // tl
    pre_off = c0 // CONV_COLS
    u_off = (SSM_D_INNER + c0) // CONV_COLS
    hb = tl // HALO
    last_hb = l // HALO - 1

    def body(da_ref, da_h_ref, p_ref, p_h_ref, u_ref, u_h_ref, w_ref, du_ref, dw_ref, db_ref, ext_ref, uext_ref):
        i = pl.program_id(1)

        @pl.when(i == 0)
        def _():
            dw_ref[...] = jnp.zeros_like(dw_ref)
            db_ref[...] = jnp.zeros_like(db_ref)

        def dpre_of(da, p):
            s = _sigmoid(p)
            return da * (s * (1.0 + p * (1.0 - s)))

        dp = dpre_of(da_ref[...], p_ref[...])
        ext_ref[0:tl, :] = dp
        ext_ref[tl:tl + HALO, :] = jnp.where(i < nt - 1, dpre_of(da_h_ref[...], p_h_ref[...]), 0.0)
        uext_ref[0:HALO, :] = jnp.where(i > 0, u_h_ref[...], 0.0)
        uext_ref[HALO:HALO + tl, :] = u_ref[...]
        du = jnp.zeros((tl, CONV_COLS), F32)
        for k in range(SSM_CONV):
            du = du + w_ref[k:k + 1, :] * ext_ref[pl.ds(SSM_CONV - 1 - k, tl), :]
            dw_ref[k:k + 1, :] += jnp.sum(dp * uext_ref[pl.ds(HALO - SSM_CONV + 1 + k, tl), :], axis=0, keepdims=True)
        du_ref[...] = du.astype(du_ref.dtype)
        db_ref[...] += jnp.sum(dp, axis=0, keepdims=True)

    return pl.pallas_call(
        body, grid=(width // CONV_COLS, nt),
        in_specs=[pl.BlockSpec((tl, CONV_COLS), lambda j, i: (i, j)),
                  pl.BlockSpec((HALO, CONV_COLS), lambda j, i: (jnp.minimum((i + 1) * hb, last_hb), j)),
                  pl.BlockSpec((tl, CONV_COLS), lambda j, i: (i, pre_off + j)),
                  pl.BlockSpec((HALO, CONV_COLS), lambda j, i: (jnp.minimum((i + 1) * hb, last_hb), pre_off + j)),
                  pl.BlockSpec((tl, CONV_COLS), lambda j, i: (i, u_off + j)),
                  pl.BlockSpec((HALO, CONV_COLS), lambda j, i: (jnp.maximum(i * hb - 1, 0), u_off + j)),
                  pl.BlockSpec((SSM_CONV, CONV_COLS), lambda j, i: (0, pre_off + j))],
        out_specs=[pl.BlockSpec((tl, CONV_COLS), lambda j, i: (i, j)),
                   pl.BlockSpec((SSM_CONV, CONV_COLS), lambda j, i: (0, j)),
                   pl.BlockSpec((1, CONV_COLS), lambda j, i: (0, j))],
        out_shape=[jax.ShapeDtypeStruct((l, width), BF16), jax.ShapeDtypeStruct((SSM_CONV, width), F32),
                   jax.ShapeDtypeStruct((1, width), F32)],
        scratch_shapes=[pltpu.VMEM((tl + HALO, CONV_COLS), F32), pltpu.VMEM((tl + HALO, CONV_COLS), F32)],
        compiler_params=_params("parallel", "arbitrary"), name=name,
    )(dact, dact, pre, pre, proj, proj, cw)


DT_COL_BLOCK = (SSM_D_INNER + SSM_CONV_DIM) // SSM_DT_PAD


def _split3(v):
    hi = v.astype(BF16)
    rest = v - hi.astype(F32)
    mid = rest.astype(BF16)
    lo = (rest - mid.astype(F32)).astype(BF16)
    return hi, mid, lo


def _ssd_prep(proj, bias, alog_lanes, name):
    l = proj.shape[0]
    nc = l // SSM_CHUNK
    head_dim_log2 = SSM_HEAD_DIM.bit_length() - 1

    def body(p_ref, b_ref, al_ref, dtb_ref, acsb_ref, dtr_ref, acsr_ref):
        v = p_ref[...] + b_ref[...]
        dt = jnp.maximum(v, 0.0) + jnp.log1p(jnp.exp(-jnp.abs(v)))
        head_of_lane = lax.shift_right_logical(lax.broadcasted_iota(jnp.int32, (SSM_DT_PAD, SSM_D_INNER), 1), head_dim_log2)
        spread = (head_of_lane == lax.broadcasted_iota(jnp.int32, (SSM_DT_PAD, SSM_D_INNER), 0)).astype(BF16)
        dtb = sum(jnp.dot(piece, spread, preferred_element_type=F32) for piece in _split3(dt))
        dtb_ref[...] = dtb
        ri = lax.broadcasted_iota(jnp.int32, (SSM_CHUNK, SSM_CHUNK), 0)
        cj = lax.broadcasted_iota(jnp.int32, (SSM_CHUNK, SSM_CHUNK), 1)
        tri = (ri >= cj).astype(BF16)
        acsb = sum(jnp.dot(tri, piece, preferred_element_type=F32) for piece in _split3(dtb * (-jnp.exp(al_ref[...]))))
        acsb_ref[...] = acsb
        gp = SSM_HPG * SSM_HEAD_DIM
        lane = lax.broadcasted_iota(jnp.int32, (SSM_HPG, gp), 1)
        pick = (lane == lax.broadcasted_iota(jnp.int32, (SSM_HPG, gp), 0) * SSM_HEAD_DIM).astype(BF16)
        for g in range(SSM_GROUPS):
            cols = slice(g * gp, (g + 1) * gp)
            dtr_ref[g] = sum(lax.dot_general(pick, piece, NT_DIMS, preferred_element_type=F32)
                             for piece in _split3(dtb[:, cols]))
            acsr_ref[g] = sum(lax.dot_general(pick, piece, NT_DIMS, preferred_element_type=F32)
                              for piece in _split3(acsb[:, cols]))

    rows = pl.BlockSpec((SSM_GROUPS, SSM_HPG, SSM_CHUNK), lambda c: (0, 0, c))
    dense = pl.BlockSpec((SSM_CHUNK, SSM_D_INNER), lambda c: (c, 0))
    return pl.pallas_call(
        body, grid=(nc,),
        in_specs=[pl.BlockSpec((SSM_CHUNK, SSM_DT_PAD), lambda c: (c, DT_COL_BLOCK)),
                  pl.BlockSpec((1, SSM_DT_PAD), lambda c: (0, 0)),
                  pl.BlockSpec((1, SSM_D_INNER), lambda c: (0, 0))],
        out_specs=[dense, dense, rows, rows],
        out_shape=[jax.ShapeDtypeStruct((l, SSM_D_INNER), F32), jax.ShapeDtypeStruct((l, SSM_D_INNER), F32),
                   jax.ShapeDtypeStruct((SSM_GROUPS, SSM_HPG, l), F32),
                   jax.ShapeDtypeStruct((SSM_GROUPS, SSM_HPG, l), F32)],
        compiler_params=_params("parallel"), name=name,
    )(proj, bias, alog_lanes)


def _dt_bwd(ddt, proj, bias, name):
    l = proj.shape[0]
    tl = _row_tile(l)

    def body(g_ref, p_ref, b_ref, o_ref, db_ref):
        @pl.when(pl.program_id(0) == 0)
        def _():
            db_ref[...] = jnp.zeros_like(db_ref)

        d = g_ref[...] * _sigmoid(p_ref[...] + b_ref[...])
        o_ref[...] = d.astype(o_ref.dtype)
        db_ref[...] += jnp.sum(d, axis=0, keepdims=True)

    return pl.pallas_call(
        body, grid=(l // tl,),
        in_specs=[pl.BlockSpec((tl, SSM_DT_PAD), lambda i: (i, 0)),
                  pl.BlockSpec((tl, SSM_DT_PAD), lambda i: (i, DT_COL_BLOCK)),
                  pl.BlockSpec((1, SSM_DT_PAD), lambda i: (0, 0))],
        out_specs=[pl.BlockSpec((tl, SSM_DT_PAD), lambda i: (i, 0)), pl.BlockSpec((1, SSM_DT_PAD), lambda i: (0, 0))],
        out_shape=[jax.ShapeDtypeStruct((l, SSM_DT_PAD), BF16), jax.ShapeDtypeStruct((1, SSM_DT_PAD), F32)],
        compiler_params=_params("arbitrary"), name=name,
    )(ddt, proj, bias)


GP = SSM_HPG * SSM_HEAD_DIM
HEAD_DIM_LOG2 = SSM_HEAD_DIM.bit_length() - 1
CHUNK_LOG2 = SSM_CHUNK.bit_length() - 1
X_BLOCKS = SSM_D_INNER // GP
B_BLOCK0 = SSM_D_INNER // SSM_STATE
C_BLOCK0 = (SSM_D_INNER + SSM_BC_DIM) // SSM_STATE


def _chunk_iotas():
    ri = lax.broadcasted_iota(jnp.int32, (SSM_CHUNK, SSM_CHUNK), 0)
    cj = lax.broadcasted_iota(jnp.int32, (SSM_CHUNK, SSM_CHUNK), 1)
    return ri, cj


def _head_decay(acsb, acs_r, r, ri, cj):
    pair = acsb[:, (r // 2) * LANES:(r // 2 + 1) * LANES]
    mine_low = r % 2 == 0
    lane = lax.broadcasted_iota(jnp.int32, (1, LANES), 1)
    col = jnp.where((lane < SSM_HEAD_DIM) == mine_low, pair, pltpu.roll(pair, SSM_HEAD_DIM, 1))
    return jnp.exp(jnp.where(ri >= cj, col - acs_r[r:r + 1, :], NEG_INF))


def _head_masked_rows(v, dtype):
    head_of_lane = lax.shift_right_logical(lax.broadcasted_iota(jnp.int32, (1, GP), 1), HEAD_DIM_LOG2)
    return jnp.concatenate([jnp.where(head_of_lane == r, v, 0.0).astype(dtype) for r in range(SSM_HPG)], axis=0)


def _ssd_fwd(xbc, dtb, acsb, acs_r, d_lanes, name):
    l = xbc.shape[0]
    nc = l // SSM_CHUNK

    def body(x_ref, b_ref, c_ref, dtb_ref, acsb_ref, acsr_ref, d_ref, y_ref, hin_ref, h_ref):
        c = pl.program_id(0)
        g = pl.program_id(1)

        @pl.when(c == 0)
        def _():
            h_ref[g] = jnp.zeros((SSM_STATE, GP), F32)

        xv = x_ref[...]
        bb = b_ref[...].astype(BF16)
        cb16 = c_ref[...].astype(BF16)
        acs_v = acsb_ref[...]
        acs_r_v = acsr_ref[0]
        ri, cj = _chunk_iotas()
        lastb = acs_v[SSM_CHUNK - 1:SSM_CHUNK, :]
        xd = xv * dtb_ref[...]
        cb = lax.dot_general(cb16, bb, NT_DIMS, preferred_element_type=F32)
        hin = h_ref[g]
        hin_ref[0, 0] = hin
        yoff = jnp.dot(cb16, hin.astype(BF16), preferred_element_type=F32)
        ms = [(cb * _head_decay(acs_v, acs_r_v, r, ri, cj)).astype(BF16) for r in range(SSM_HPG)]
        ydiag = jnp.dot(jnp.concatenate(ms, axis=1), _head_masked_rows(xd, BF16), preferred_element_type=F32)
        y_ref[...] = ydiag + jnp.exp(acs_v) * yoff + d_ref[0] * xv
        h_ref[g] = hin * jnp.exp(lastb) + _bdot_tn(bb, xd * jnp.exp(lastb - acs_v))

    lanes = pl.BlockSpec((SSM_CHUNK, GP), lambda c, g: (c, g))
    return pl.pallas_call(
        body, grid=(nc, SSM_GROUPS),
        in_specs=[lanes,
                  pl.BlockSpec((SSM_CHUNK, SSM_STATE), lambda c, g: (c, B_BLOCK0 + g)),
                  pl.BlockSpec((SSM_CHUNK, SSM_STATE), lambda c, g: (c, C_BLOCK0 + g)),
                  lanes, lanes,
                  pl.BlockSpec((1, SSM_HPG, SSM_CHUNK), lambda c, g: (g, 0, c)),
                  pl.BlockSpec((1, 1, GP), lambda c, g: (g, 0, 0))],
        out_specs=[lanes, pl.BlockSpec((1, 1, SSM_STATE, GP), lambda c, g: (c, g, 0, 0))],
        out_shape=[jax.ShapeDtypeStruct((l, SSM_D_INNER), F32),
                   jax.ShapeDtypeStruct((nc, SSM_GROUPS, SSM_STATE, GP), F32)],
        scratch_shapes=[pltpu.VMEM((SSM_GROUPS, SSM_STATE, GP), F32)],
        compiler_params=_params("arbitrary", "arbitrary"), name=name,
    )(xbc, xbc, xbc, dtb, acsb, acs_r, d_lanes)


def _ssd_bwd(xbc, dtb, acsb, dtr, acs_r, a_log, d_lanes, hin, dy, name):
    l = xbc.shape[0]
    nc = l // SSM_CHUNK

    def body(x_ref, b_ref, c_ref, dtb_ref, acsb_ref, dtr_ref, acsr_ref, alc_ref, d_ref, hin_ref, dy_ref,
             dx_ref, db_ref, dc_ref, ddt_ref, dal_ref, dd_ref, dh_ref):
        c = pl.program_id(0)
        g = pl.program_id(1)

        @pl.when(c == 0)
        def _():
            dh_ref[g] = jnp.zeros((SSM_STATE, GP), F32)

        @pl.when((c == 0) & (g == 0))
        def _():
            dal_ref[...] = jnp.zeros_like(dal_ref)
            dd_ref[...] = jnp.zeros_like(dd_ref)

        xv = x_ref[...]
        dyv = dy_ref[...]
        bb = b_ref[...].astype(BF16)
        cb16 = c_ref[...].astype(BF16)
        dtb = dtb_ref[...]
        acsb = acsb_ref[...]
        dtr_v = dtr_ref[0]
        acs_r = acsr_ref[0]
        a_col = -jnp.exp(alc_ref[0])
        ri, cj = _chunk_iotas()
        head_of_lane = lax.shift_right_logical(lax.broadcasted_iota(jnp.int32, (SSM_HPG, GP), 1), HEAD_DIM_LOG2)
        ind_t = (head_of_lane == lax.broadcasted_iota(jnp.int32, (SSM_HPG, GP), 0)).astype(BF16)
        lastb = acsb[SSM_CHUNK - 1:SSM_CHUNK, :]
        ecb = jnp.exp(acsb)
        dteb = jnp.exp(lastb - acsb)
        xd = xv * dtb
        xw = xd * dteb
        cb = lax.dot_general(cb16, bb, NT_DIMS, preferred_element_type=F32)
        hin_v = hin_ref[0, 0]
        dhn = dh_ref[g]
        h16 = hin_v.astype(BF16)
        dh16 = dhn.astype(BF16)
        ch = jnp.dot(cb16, h16, preferred_element_type=F32)
        bdh = jnp.dot(bb, dh16, preferred_element_type=F32)
        dym = _head_masked_rows(dyv, BF16)
        g_all = lax.dot_general(dym, xd.astype(BF16), NT_DIMS, preferred_element_type=F32)
        gl_sum = jnp.zeros((SSM_CHUNK, SSM_CHUNK), F32)
        ms, qs = [], []
        for r in range(SSM_HPG):
            decay = _head_decay(acsb, acs_r, r, ri, cj)
            gl = g_all[r * SSM_CHUNK:(r + 1) * SSM_CHUNK] * decay
            gl_sum = gl_sum + gl
            ms.append((cb * decay).astype(BF16))
            qs.append((gl * cb).astype(BF16))
        dxd = lax.dot_general(jnp.concatenate(ms, axis=0), dym, TN_DIMS, preferred_element_type=F32) + dteb * bdh
        cum = jnp.dot(jnp.concatenate(qs, axis=0), (ri < cj).astype(BF16), preferred_element_type=F32)
        sub4 = lax.broadcasted_iota(jnp.int32, (SSM_HPG, 1), 0)
        da = jnp.zeros((SSM_HPG, SSM_CHUNK), F32)
        for r in range(SSM_HPG):
            rect = jnp.sum(jnp.where(ri >= cj, cum[r * SSM_CHUNK:(r + 1) * SSM_CHUNK], 0.0), axis=0, keepdims=True)
            da = da + jnp.where(sub4 == r, rect, 0.0)
        z2 = xw * bdh
        sub8 = lax.broadcasted_iota(jnp.int32, (8, 1), 0)
        col_sums = (jnp.where(sub8 == 0, jnp.sum(z2, axis=0, keepdims=True), 0.0)
                    + jnp.where(sub8 == 1, jnp.sum(dhn * hin_v, axis=0, keepdims=True), 0.0)
                    + jnp.where(sub8 == 2, jnp.sum(dyv * xv, axis=0, keepdims=True), 0.0))
        summands = jnp.concatenate([dyv * ecb * ch - z2, dxd * xv, col_sums], axis=0)
        sums = sum(lax.dot_general(ind_t, piece, NT_DIMS, preferred_element_type=F32) for piece in _split3(summands))
        per_pos = sums[:, :2 * SSM_CHUNK]
        totals = sums[:, 2 * SSM_CHUNK:]
        e_last = totals[:, 0:1] + jnp.exp(acs_r[:, SSM_CHUNK - 1:SSM_CHUNK]) * totals[:, 1:2]
        da = (da + e_last + jnp.dot(per_pos[:, :SSM_CHUNK], (ri >= cj).astype(F32), preferred_element_type=F32,
                                    precision=lax.Precision.HIGHEST))
        ddt_ref[0] = a_col * da + per_pos[:, SSM_CHUNK:]
        dal_ref[g] += a_col * jnp.sum(da * dtr_v, axis=1, keepdims=True)
        dd_ref[g] += totals[:, 2:3]
        dx_ref[...] = dxd * dtb + d_ref[0] * dyv
        w16 = (ecb * dyv).astype(BF16)
        xw16 = xw.astype(BF16)
        gl16 = gl_sum.astype(BF16)
        dc_ref[...] = (jnp.dot(gl16, bb, preferred_element_type=F32)
                       + lax.dot_general(w16, h16, NT_DIMS, preferred_element_type=F32))
        db_ref[...] = (lax.dot_general(gl16, cb16, TN_DIMS, preferred_element_type=F32)
                       + lax.dot_general(xw16, dh16, NT_DIMS, preferred_element_type=F32))
        dh_ref[g] = dhn * jnp.exp(lastb) + lax.dot_general(cb16, w16, TN_DIMS, preferred_element_type=F32)

    def rev(c):
        return nc - 1 - c

    small = pl.BlockSpec((SSM_GROUPS, SSM_HPG, 1), lambda c, g: (0, 0, 0))
    return pl.pallas_call(
        body, grid=(nc, SSM_GROUPS),
        in_specs=[pl.BlockSpec((SSM_CHUNK, GP), lambda c, g: (rev(c), g)),
                  pl.BlockSpec((SSM_CHUNK, SSM_STATE), lambda c, g: (rev(c), B_BLOCK0 + g)),
                  pl.BlockSpec((SSM_CHUNK, SSM_STATE), lambda c, g: (rev(c), C_BLOCK0 + g)),
                  pl.BlockSpec((SSM_CHUNK, GP), lambda c, g: (rev(c), g)),
                  pl.BlockSpec((SSM_CHUNK, GP), lambda c, g: (rev(c), g)),
                  pl.BlockSpec((1, SSM_HPG, SSM_CHUNK), lambda c, g: (g, 0, rev(c))),
                  pl.BlockSpec((1, SSM_HPG, SSM_CHUNK), lambda c, g: (g, 0, rev(c))),
                  pl.BlockSpec((1, SSM_HPG, 1), lambda c, g: (g, 0, 0)),
                  pl.BlockSpec((1, 1, GP), lambda c, g: (g, 0, 0)),
                  pl.BlockSpec((1, 1, SSM_STATE, GP), lambda c, g: (rev(c), g, 0, 0)),
                  pl.BlockSpec((SSM_CHUNK, GP), lambda c, g: (rev(c), g))],
        out_specs=[pl.BlockSpec((SSM_CHUNK, GP), lambda c, g: (rev(c), g)),
                   pl.BlockSpec((SSM_CHUNK, SSM_STATE), lambda c, g: (rev(c), g)),
                   pl.BlockSpec((SSM_CHUNK, SSM_STATE), lambda c, g: (rev(c), g)),
                   pl.BlockSpec((1, SSM_HPG, SSM_CHUNK), lambda c, g: (g, 0, rev(c))),
                   small, small],
        out_shape=[jax.ShapeDtypeStruct((l, SSM_D_INNER), F32), jax.ShapeDtypeStruct((l, SSM_BC_DIM), F32),
                   jax.ShapeDtypeStruct((l, SSM_BC_DIM), F32), jax.ShapeDtypeStruct((SSM_GROUPS, SSM_HPG, l), F32),
                   jax.ShapeDtypeStruct((SSM_GROUPS, SSM_HPG, 1), F32),
                   jax.ShapeDtypeStruct((SSM_GROUPS, SSM_HPG, 1), F32)],
        scratch_shapes=[pltpu.VMEM((SSM_GROUPS, SSM_STATE, GP), F32)],
        compiler_params=_params("arbitrary", "arbitrary"), name=name,
    )(xbc, xbc, xbc, dtb, acsb, dtr, acs_r, a_log.reshape(SSM_GROUPS, SSM_HPG, 1), d_lanes, hin, dy)


def _gatenorm_fwd(y, proj, w, name):
    l = y.shape[0]
    tl = _pick(l, (256, 128))

    def body(y_ref, z_ref, w_ref, o_ref):
        z = z_ref[...]
        yg = y_ref[...] * (z * _sigmoid(z))
        r = lax.rsqrt(jnp.mean(yg * yg, axis=-1, keepdims=True) + EPS)
        o_ref[...] = (yg * r * w_ref[...]).astype(o_ref.dtype)

    row = pl.BlockSpec((tl, SSM_D_INNER), lambda i: (i, 0))
    return pl.pallas_call(
        body, grid=(l // tl,), in_specs=[row, row, pl.BlockSpec((1, SSM_D_INNER), lambda i: (0, 0))],
        out_specs=row, out_shape=jax.ShapeDtypeStruct((l, SSM_D_INNER), BF16),
        compiler_params=_params("parallel"), name=name,
    )(y, proj, w.reshape(1, SSM_D_INNER))


def _gatenorm_bwd(g, y, proj, w, name):
    l = y.shape[0]
    tl = _pick(l, (256, 128))
    nt = l // tl

    def body(g_ref, y_ref, z_ref, w_ref, dy_ref, dz_ref, dw_ref, acc_ref):
        i = pl.program_id(0)

        @pl.when(i == 0)
        def _():
            acc_ref[...] = jnp.zeros_like(acc_ref)

        z = z_ref[...]
        yv = y_ref[...]
        s = _sigmoid(z)
        sz = z * s
        yg = yv * sz
        r = lax.rsqrt(jnp.mean(yg * yg, axis=-1, keepdims=True) + EPS)
        nrm = yg * r
        gv = g_ref[...]
        gw = gv * w_ref[...]
        dyg = r * (gw - nrm * jnp.mean(gw * nrm, axis=-1, keepdims=True))
        dy_ref[...] = dyg * sz
        dz_ref[...] = (dyg * yv * (s * (1.0 + z * (1.0 - s)))).astype(dz_ref.dtype)
        acc_ref[...] += jnp.sum((gv * nrm).reshape(tl // 8, 8, SSM_D_INNER), axis=0)

        @pl.when(i == nt - 1)
        def _():
            dw_ref[...] = jnp.sum(acc_ref[...], axis=0, keepdims=True)

    row = pl.BlockSpec((tl, SSM_D_INNER), lambda i: (i, 0))
    vec = pl.BlockSpec((1, SSM_D_INNER), lambda i: (0, 0))
    return pl.pallas_call(
        body, grid=(nt,), in_specs=[row, row, row, vec], out_specs=[row, row, vec],
        out_shape=[jax.ShapeDtypeStruct((l, SSM_D_INNER), F32), jax.ShapeDtypeStruct((l, SSM_D_INNER), BF16),
                   jax.ShapeDtypeStruct((1, SSM_D_INNER), F32)],
        scratch_shapes=[pltpu.VMEM((8, SSM_D_INNER), F32)], compiler_params=_params("arbitrary"), name=name,
    )(g, y, proj, w.reshape(1, SSM_D_INNER))


LANES = 128
ROPE_Q_CHUNKS = ATT_WIDTH // LANES
ROPE_K_CHUNKS = ATT_KV_WIDTH // LANES


def _rope_tables(positions):
    inv = ROPE_THETA ** (-jnp.arange(0, ROPE_DIM, 2, dtype=F32) / ROPE_DIM)
    ang = positions.astype(F32)[:, None] * inv
    cos, sin = jnp.cos(ang), jnp.sin(ang)
    l = positions.shape[0]
    rest = ATT_HEAD_DIM - ROPE_DIM
    ones, zeros = jnp.ones((l, rest), F32), jnp.zeros((l, rest), F32)
    z8 = jnp.zeros((l, ROPE_HALF), F32)
    cos_f = jnp.concatenate([cos, cos, ones], axis=1)
    sin_a = jnp.concatenate([-sin, z8, zeros], axis=1)
    sin_b = jnp.concatenate([z8, sin, zeros], axis=1)
    reps = LANES // ATT_HEAD_DIM
    return tuple(jnp.tile(t, (1, reps)) for t in (cos_f, sin_a, sin_b))


def _rope_fwd(proj, tables, name):
    l = proj.shape[0]
    tl = _pick(l, (256, 128))

    def body(p_ref, c_ref, sa_ref, sb_ref, o_ref):
        cos_f, sin_a, sin_b = c_ref[...], sa_ref[...], sb_ref[...]
        for k in range(ATT_QKV // LANES):
            sl = slice(k * LANES, (k + 1) * LANES)
            t = p_ref[:, sl]
            if k < ROPE_Q_CHUNKS + ROPE_K_CHUNKS:
                t = (t * cos_f + pltpu.roll(t, LANES - ROPE_HALF, 1) * sin_a + pltpu.roll(t, ROPE_HALF, 1) * sin_b)
            if k < ROPE_Q_CHUNKS:
                t = t * Q_SCALE
            o_ref[:, sl] = t.astype(o_ref.dtype)

    tab = pl.BlockSpec((tl, LANES), lambda i: (i, 0))
    return pl.pallas_call(
        body, grid=(l // tl,), in_specs=[pl.BlockSpec((tl, ATT_IN_DIM), lambda i: (i, 0)), tab, tab, tab],
        out_specs=pl.BlockSpec((tl, ATT_QKV), lambda i: (i, 0)),
        out_shape=jax.ShapeDtypeStruct((l, ATT_QKV), BF16), compiler_params=_params("parallel"), name=name,
    )(proj, *tables)


def _rope_bwd(dq, dk, dv, dgate, tables, name):
    l = dq.shape[0]
    tl = _pick(l, (256, 128))

    def body(dq_ref, dk_ref, dv_ref, dg_ref, c_ref, sa_ref, sb_ref, o_ref):
        cos_f, sin_a, sin_b = c_ref[...], sa_ref[...], sb_ref[...]

        def unrope(t):
            return t * cos_f + pltpu.roll(t * sin_a, ROPE_HALF, 1) + pltpu.roll(t * sin_b, LANES - ROPE_HALF, 1)

        for k in range(ROPE_Q_CHUNKS):
            sl = slice(k * LANES, (k + 1) * LANES)
            o_ref[:, sl] = unrope(dq_ref[:, sl] * Q_SCALE).astype(o_ref.dtype)
        for k in range(ROPE_K_CHUNKS):
            sl = slice(k * LANES, (k + 1) * LANES)
            o_ref[:, ATT_WIDTH + k * LANES:ATT_WIDTH + (k + 1) * LANES] = unrope(dk_ref[:, sl]).astype(o_ref.dtype)
        o_ref[:, ATT_WIDTH + ATT_KV_WIDTH:ATT_QKV] = dv_ref[...].astype(o_ref.dtype)
        o_ref[:, ATT_QKV:ATT_IN_DIM] = dg_ref[...].astype(o_ref.dtype)

    tab = pl.BlockSpec((tl, LANES), lambda i: (i, 0))
    wide = pl.BlockSpec((tl, ATT_WIDTH), lambda i: (i, 0))
    kv = pl.BlockSpec((tl, ATT_KV_WIDTH), lambda i: (i, 0))
    return pl.pallas_call(
        body, grid=(l // tl,), in_specs=[wide, kv, kv, wide, tab, tab, tab],
        out_specs=pl.BlockSpec((tl, ATT_IN_DIM), lambda i: (i, 0)),
        out_shape=jax.ShapeDtypeStruct((l, ATT_IN_DIM), BF16), compiler_params=_params("parallel"), name=name,
    )(dq, dk, dv, dgate, *tables)


K_COL_BLOCK = ATT_WIDTH // ATT_KV_WIDTH
V_COL_BLOCK = K_COL_BLOCK + 1
GATE_HALF = ATT_WIDTH // 2
GATE_COL_BLOCK = ATT_QKV // GATE_HALF


def _band_masks():
    ri = lax.broadcasted_iota(jnp.int32, (ATT_BLOCK, ATT_BLOCK), 0)
    cj = lax.broadcasted_iota(jnp.int32, (ATT_BLOCK, ATT_BLOCK), 1)
    return cj > ri, cj <= ri


def _attn_fwd(qkv, proj, sinks, name):
    l = qkv.shape[0]
    nb = l // ATT_BLOCK

    def body(sink_ref, q_ref, kp_ref, kc_ref, vp_ref, vc_ref, g0_ref, g1_ref, og_ref, o_ref, lse_ref):
        n = pl.program_id(0)
        mask_p, mask_c = _band_masks()
        mask_p = mask_p & (n > 0)
        lane = lax.broadcasted_iota(jnp.int32, (1, ATT_Q_HEADS), 1)
        lse_acc = jnp.zeros((ATT_BLOCK, ATT_Q_HEADS), F32)
        for h in range(ATT_Q_HEADS):
            kv = slice((h // ATT_GQA) * ATT_HEAD_DIM, (h // ATT_GQA + 1) * ATT_HEAD_DIM)
            sl = slice(h * ATT_HEAD_DIM, (h + 1) * ATT_HEAD_DIM)
            qh = q_ref[:, sl]
            sp = jnp.where(mask_p, lax.dot_general(qh, kp_ref[:, kv], NT_DIMS, preferred_element_type=F32), NEG_INF)
            sc = jnp.where(mask_c, lax.dot_general(qh, kc_ref[:, kv], NT_DIMS, preferred_element_type=F32), NEG_INF)
            sink = sink_ref[h]
            m = jnp.maximum(jnp.maximum(jnp.max(sp, axis=1, keepdims=True), jnp.max(sc, axis=1, keepdims=True)), sink)
            pp = jnp.exp(sp - m)
            pc = jnp.exp(sc - m)
            den = jnp.sum(pp, axis=1, keepdims=True) + jnp.sum(pc, axis=1, keepdims=True) + jnp.exp(sink - m)
            oh = (jnp.dot(pp.astype(BF16), vp_ref[:, kv], preferred_element_type=F32)
                  + jnp.dot(pc.astype(BF16), vc_ref[:, kv], preferred_element_type=F32)) / den
            o_ref[:, sl] = oh
            lse_acc = lse_acc + jnp.where(lane == h, m + jnp.log(den), 0.0)
        lse_ref[...] = lse_acc
        for half, g_ref in enumerate((g0_ref, g1_ref)):
            sl = slice(half * GATE_HALF, (half + 1) * GATE_HALF)
            gate = g_ref[...]
            og_ref[:, sl] = (o_ref[:, sl] * (gate * _sigmoid(gate))).astype(og_ref.dtype)

    def prev(n):
        return jnp.maximum(n - 1, 0)

    wide = pl.BlockSpec((ATT_BLOCK, ATT_WIDTH), lambda n: (n, 0))
    return pl.pallas_call(
        body, grid=(nb,),
        in_specs=[pl.BlockSpec(memory_space=pltpu.SMEM), wide,
                  pl.BlockSpec((ATT_BLOCK, ATT_KV_WIDTH), lambda n: (prev(n), K_COL_BLOCK)),
                  pl.BlockSpec((ATT_BLOCK, ATT_KV_WIDTH), lambda n: (n, K_COL_BLOCK)),
                  pl.BlockSpec((ATT_BLOCK, ATT_KV_WIDTH), lambda n: (prev(n), V_COL_BLOCK)),
                  pl.BlockSpec((ATT_BLOCK, ATT_KV_WIDTH), lambda n: (n, V_COL_BLOCK)),
                  pl.BlockSpec((ATT_BLOCK, GATE_HALF), lambda n: (n, GATE_COL_BLOCK)),
                  pl.BlockSpec((ATT_BLOCK, GATE_HALF), lambda n: (n, GATE_COL_BLOCK + 1))],
        out_specs=[wide, wide, pl.BlockSpec((ATT_BLOCK, ATT_Q_HEADS), lambda n: (n, 0))],
        out_shape=[jax.ShapeDtypeStruct((l, ATT_WIDTH), BF16), jax.ShapeDtypeStruct((l, ATT_WIDTH), F32),
                   jax.ShapeDtypeStruct((l, ATT_Q_HEADS), F32)],
        compiler_params=_params("parallel"), name=name,
    )(sinks, qkv, qkv, qkv, qkv, qkv, proj, proj)


def _attn_bwd(qkv, proj, sinks, o, lse, dog, name):
    l = qkv.shape[0]
    nb = l // ATT_BLOCK

    def body(sink_ref, q_ref, kp_ref, kc_ref, vp_ref, vc_ref, g0_ref, g1_ref, o_ref, lse_ref, dog_ref,
             dq_ref, dk_ref, dv_ref, dg_ref, ds_ref, ck_ref, cv_ref, do_ref):
        n = pl.program_id(0)

        @pl.when(n == 0)
        def _():
            ds_ref[...] = jnp.zeros_like(ds_ref)
            ck_ref[...] = jnp.zeros_like(ck_ref)
            cv_ref[...] = jnp.zeros_like(cv_ref)

        @pl.when(n == nb)
        def _():
            dk_ref[...] = ck_ref[...]
            dv_ref[...] = cv_ref[...]

        @pl.when(n < nb)
        def _():
            mask_p, mask_c = _band_masks()
            mask_p = mask_p & (n > 0)
            lane = lax.broadcasted_iota(jnp.int32, (1, ATT_Q_HEADS), 1)
            for half, g_ref in enumerate((g0_ref, g1_ref)):
                sl = slice(half * GATE_HALF, (half + 1) * GATE_HALF)
                gate = g_ref[...]
                s = _sigmoid(gate)
                dogv = dog_ref[:, sl]
                do_ref[:, sl] = dogv * (gate * s)
                dg_ref[:, sl] = dogv * o_ref[:, sl] * (s * (1.0 + gate * (1.0 - s)))
            lse_v = lse_ref[...]
            ds_acc = jnp.zeros((1, ATT_Q_HEADS), F32)
            for kvh in range(ATT_KV_HEADS):
                kv = slice(kvh * ATT_HEAD_DIM, (kvh + 1) * ATT_HEAD_DIM)
                kp, kc, vp, vc = kp_ref[:, kv], kc_ref[:, kv], vp_ref[:, kv], vc_ref[:, kv]
                dkp = jnp.zeros((ATT_BLOCK, ATT_HEAD_DIM), F32)
                dkc, dvp, dvc = dkp, dkp, dkp
                for gq in range(ATT_GQA):
                    h = kvh * ATT_GQA + gq
                    sl = slice(h * ATT_HEAD_DIM, (h + 1) * ATT_HEAD_DIM)
                    qh = q_ref[:, sl]
                    doh = do_ref[:, sl]
                    do16 = doh.astype(BF16)
                    lse_h = lse_v[:, h:h + 1]
                    pp = jnp.exp(jnp.where(mask_p, lax.dot_general(qh, kp, NT_DIMS, preferred_element_type=F32) - lse_h, NEG_INF))
                    pc = jnp.exp(jnp.where(mask_c, lax.dot_general(qh, kc, NT_DIMS, preferred_element_type=F32) - lse_h, NEG_INF))
                    delta = jnp.sum(doh * o_ref[:, sl], axis=1, keepdims=True)
                    dsp = (pp * (lax.dot_general(do16, vp, NT_DIMS, preferred_element_type=F32) - delta)).astype(BF16)
                    dsc = (pc * (lax.dot_general(do16, vc, NT_DIMS, preferred_element_type=F32) - delta)).astype(BF16)
                    dq_ref[:, sl] = (jnp.dot(dsp, kp, preferred_element_type=F32)
                                     + jnp.dot(dsc, kc, preferred_element_type=F32))
                    dkp = dkp + lax.dot_general(dsp, qh, TN_DIMS, preferred_element_type=F32)
                    dkc = dkc + lax.dot_general(dsc, qh, TN_DIMS, preferred_element_type=F32)
                    dvp = dvp + lax.dot_general(pp.astype(BF16), do16, TN_DIMS, preferred_element_type=F32)
                    dvc = dvc + lax.dot_general(pc.astype(BF16), do16, TN_DIMS, preferred_element_type=F32)
                    dsink = -jnp.sum(jnp.exp(sink_ref[h] - lse_h) * delta)
                    ds_acc = ds_acc + jnp.where(lane == h, dsink, 0.0)
                dk_ref[:, kv] = ck_ref[:, kv] + dkp
                dv_ref[:, kv] = cv_ref[:, kv] + dvp
                ck_ref[:, kv] = dkc
                cv_ref[:, kv] = dvc
            ds_ref[...] += ds_acc

    def cur(n):
        return jnp.minimum(n, nb - 1)

    def prev(n):
        return jnp.maximum(n - 1, 0)

    wide = pl.BlockSpec((ATT_BLOCK, ATT_WIDTH), lambda n: (cur(n), 0))
    kvo = pl.BlockSpec((ATT_BLOCK, ATT_KV_WIDTH), lambda n: (prev(n), 0))
    return pl.pallas_call(
        body, grid=(nb + 1,),
        in_specs=[pl.BlockSpec(memory_space=pltpu.SMEM), wide,
                  pl.BlockSpec((ATT_BLOCK, ATT_KV_WIDTH), lambda n: (prev(cur(n)), K_COL_BLOCK)),
                  pl.BlockSpec((ATT_BLOCK, ATT_KV_WIDTH), lambda n: (cur(n), K_COL_BLOCK)),
                  pl.BlockSpec((ATT_BLOCK, ATT_KV_WIDTH), lambda n: (prev(cur(n)), V_COL_BLOCK)),
                  pl.BlockSpec((ATT_BLOCK, ATT_KV_WIDTH), lambda n: (cur(n), V_COL_BLOCK)),
                  pl.BlockSpec((ATT_BLOCK, GATE_HALF), lambda n: (cur(n), GATE_COL_BLOCK)),
                  pl.BlockSpec((ATT_BLOCK, GATE_HALF), lambda n: (cur(n), GATE_COL_BLOCK + 1)),
                  wide, pl.BlockSpec((ATT_BLOCK, ATT_Q_HEADS), lambda n: (cur(n), 0)), wide],
        out_specs=[wide, kvo, kvo, wide, pl.BlockSpec((1, ATT_Q_HEADS), lambda n: (0, 0))],
        out_shape=[jax.ShapeDtypeStruct((l, ATT_WIDTH), F32), jax.ShapeDtypeStruct((l, ATT_KV_WIDTH), F32),
                   jax.ShapeDtypeStruct((l, ATT_KV_WIDTH), F32), jax.ShapeDtypeStruct((l, ATT_WIDTH), F32),
                   jax.ShapeDtypeStruct((1, ATT_Q_HEADS), F32)],
        scratch_shapes=[pltpu.VMEM((ATT_BLOCK, ATT_KV_WIDTH), F32), pltpu.VMEM((ATT_BLOCK, ATT_KV_WIDTH), F32),
                        pltpu.VMEM((ATT_BLOCK, ATT_WIDTH), F32)],
        compiler_params=_params("arbitrary"), name=name,
    )(sinks, qkv, qkv, qkv, qkv, qkv, proj, proj, o, lse, dog)


def _local_step(x, positions, pre_norm, post_norm, w_ssm_in, conv_w, conv_b, dt_bias, a_log, d_skip, gate_norm,
                w_ssm_out, w_att_in, sinks, w_att_out, target):
    tables = _rope_tables(positions)
    dt_bias_pad = jnp.pad(dt_bias, ((0, 0), (0, SSM_DT_PAD - SSM_HEADS)))
    d_lanes = jnp.repeat(d_skip, SSM_HEAD_DIM, axis=1).reshape(-1, SSM_GROUPS, 1, GP)
    alog_lanes = jnp.repeat(a_log, SSM_HEAD_DIM, axis=1)
    saved = []
    cur = x
    for i in range(DEPTH):
        j = i // 2
        h = _rmsnorm_fwd(cur, pre_norm[i], f"prenorm_fwd_{i}")
        if i % 2 == 0:
            proj = _matmul(h, w_ssm_in[j], "nn", F32, f"ssm_in_{i}")
            pre, xbc = _conv_fwd(proj, conv_w[j], conv_b[j], f"conv_fwd_{i}")
            dtb, acsb, dtr, acs_r = _ssd_prep(proj, dt_bias_pad[j:j + 1], alog_lanes[j:j + 1], f"ssd_prep_{i}")
            y, hin = _ssd_fwd(xbc, dtb, acsb, acs_r, d_lanes[j], f"ssd_fwd_{i}")
            act = _gatenorm_fwd(y, proj, gate_norm[j], f"gatenorm_fwd_{i}")
            ymix = _matmul(act, w_ssm_out[j], "nn", F32, f"ssm_out_{i}")
            saved.append(dict(x=cur, h=h, proj=proj, pre=pre, xbc=xbc, dtb=dtb, acsb=acsb, dtr=dtr, acs_r=acs_r, y=y,
                              hin=hin, act=act, ymix=ymix))
        else:
            proj = _matmul(h, w_att_in[j], "nn", F32, f"att_in_{i}")
            qkv = _rope_fwd(proj, tables, f"rope_fwd_{i}")
            act, o, lse = _attn_fwd(qkv, proj, sinks[j], f"attn_fwd_{i}")
            ymix = _matmul(act, w_att_out[j], "nn", F32, f"att_out_{i}")
            saved.append(dict(x=cur, h=h, proj=proj, qkv=qkv, o=o, lse=lse, act=act, ymix=ymix))
        cur = _post_fwd(cur, ymix, post_norm[i], f"post_fwd_{i}")

    g, loss_lanes = _loss_grad(cur, target, "loss")

    gr = {k: [None] * 2 for k in ("ssm_w_in", "ssm_conv_w", "ssm_conv_b", "ssm_dt_bias", "ssm_a_log", "ssm_d",
                                  "ssm_gate_norm", "ssm_w_out", "att_w_in", "att_sinks", "att_w_out")}
    gr["pre_norm"] = [None] * DEPTH
    gr["post_norm"] = [None] * DEPTH
    for i in reversed(range(DEPTH)):
        j = i // 2
        s = saved[i]
        dymix, gr["post_norm"][i] = _rmsnorm_bwd(g, s["ymix"], post_norm[i], None, BF16, f"post_bwd_{i}")
        if i % 2 == 0:
            dact = _matmul(dymix, w_ssm_out[j], "nt", F32, f"ssm_out_dx_{i}")
            gr["ssm_w_out"][j] = _matmul(s["act"], dymix, "tn", F32, f"ssm_out_dw_{i}")
            dy, dz, gr["ssm_gate_norm"][j] = _gatenorm_bwd(dact, s["y"], s["proj"], gate_norm[j], f"gatenorm_bwd_{i}")
            dxs, db, dc, ddt8, dal, dd = _ssd_bwd(s["xbc"], s["dtb"], s["acsb"], s["dtr"], s["acs_r"], a_log[j],
                                                  d_lanes[j], s["hin"], dy, f"ssd_bwd_{i}")
            gr["ssm_a_log"][j] = dal.reshape(SSM_HEADS)
            gr["ssm_d"][j] = dd.reshape(SSM_HEADS)
            l = x.shape[0]
            ddt = jnp.pad(jnp.transpose(ddt8, (2, 0, 1)).reshape(l, SSM_HEADS), ((0, 0), (0, SSM_DT_PAD - SSM_HEADS)))
            ddt_raw, dbias = _dt_bwd(ddt, s["proj"], dt_bias_pad[j:j + 1], f"dt_bwd_{i}")
            gr["ssm_dt_bias"][j] = dbias[0, :SSM_HEADS]
            pieces, dcw, dcb = [], [], []
            for c0, dpiece, tag in ((0, dxs, "x"), (SSM_D_INNER, db, "b"), (SSM_D_INNER + SSM_BC_DIM, dc, "c")):
                du, dw_, db_ = _conv_bwd(dpiece, s["pre"], s["proj"], conv_w[j], c0, f"conv_bwd_{tag}_{i}")
                pieces.append(du)
                dcw.append(dw_)
                dcb.append(db_)
            gr["ssm_conv_w"][j] = jnp.concatenate(dcw, axis=1)
            gr["ssm_conv_b"][j] = jnp.concatenate(dcb, axis=1)[0]
            dproj = jnp.concatenate([dz] + pieces + [ddt_raw], axis=1)
            w_in, key = w_ssm_in[j], "ssm_w_in"
        else:
            dog = _matmul(dymix, w_att_out[j], "nt", F32, f"att_out_dx_{i}")
            gr["att_w_out"][j] = _matmul(s["act"], dymix, "tn", F32, f"att_out_dw_{i}")
            dq, dk, dv, dgate, dsk = _attn_bwd(s["qkv"], s["proj"], sinks[j], s["o"], s["lse"], dog, f"attn_bwd_{i}")
            gr["att_sinks"][j] = dsk[0]
            dproj = _rope_bwd(dq, dk, dv, dgate, tables, f"rope_bwd_{i}")
            w_in, key = w_att_in[j], "att_w_in"
        dh = _matmul(dproj, w_in, "nt", F32, f"in_dx_{i}")
        gr[key][j] = _matmul(s["h"], dproj, "tn", F32, f"in_dw_{i}")
        g, gr["pre_norm"][i] = _rmsnorm_bwd(dh, s["x"], pre_norm[i], g, F32, f"prenorm_bwd_{i}")
    grads = {k: jnp.stack([v.reshape(v.shape[-1]) if k in ("pre_norm", "post_norm", "ssm_gate_norm") else v for v in vs])
             for k, vs in gr.items()}
    return loss_lanes, g, grads


N_CHIPS = 4
N_DEV = 8
MESH = pl.DeviceIdType.MESH
ANY = pl.BlockSpec(memory_space=pl.ANY)


def _place():
    x, y, c = lax.axis_index("x"), lax.axis_index("y"), lax.axis_index("c")
    return x, y, c, 2 * x + y


def _chip_gather(shards, name):
    n = len(shards)

    def body(*refs):
        ins, outs = refs[:n], refs[n:2 * n]
        send_sems, recv_sems, local_sems = refs[2 * n:]
        _, _, c, s = _place()
        local = [pltpu.make_async_copy(ins[w], outs[w].at[s], local_sems.at[w]) for w in range(n)]
        for cp in local:
            cp.start()

        def remote(w, t):
            return pltpu.make_async_remote_copy(
                src_ref=ins[w], dst_ref=outs[w].at[s], send_sem=send_sems.at[w, t], recv_sem=recv_sems.at[w, s],
                device_id=(t // 2, t % 2, c), device_id_type=MESH)

        def arrival(w, t):
            return pltpu.make_async_remote_copy(
                src_ref=ins[w], dst_ref=outs[w].at[t], send_sem=send_sems.at[w, t], recv_sem=recv_sems.at[w, t],
                device_id=(t // 2, t % 2, c), device_id_type=MESH)

        for t in range(N_CHIPS):
            @pl.when(s != t)
            def _():
                for w in range(n):
                    remote(w, t).start()
        for t in range(N_CHIPS):
            @pl.when(s != t)
            def _():
                for w in range(n):
                    remote(w, t).wait_send()
                    arrival(w, t).wait_recv()
        for cp in local:
            cp.wait()

    return pl.pallas_call(
        body, in_specs=[ANY] * n, out_specs=[ANY] * n,
        out_shape=[jax.ShapeDtypeStruct((N_CHIPS,) + a.shape, a.dtype) for a in shards],
        scratch_shapes=[pltpu.SemaphoreType.DMA((n, N_CHIPS)), pltpu.SemaphoreType.DMA((n, N_CHIPS)),
                        pltpu.SemaphoreType.DMA((n,))],
        name=name,
    )(*shards)


def _pair_swap(parts, name):
    n = len(parts)

    def body(*refs):
        ins, outs = refs[:n], refs[n:2 * n]
        send_sems, recv_sems = refs[2 * n:]
        x, y, c, _ = _place()
        cps = [pltpu.make_async_remote_copy(
            src_ref=ins[w].at[1 - c], dst_ref=outs[w], send_sem=send_sems.at[w], recv_sem=recv_sems.at[w],
            device_id=(x, y, 1 - c), device_id_type=MESH) for w in range(n)]
        for cp in cps:
            cp.start()
        for cp in cps:
            cp.wait()

    return pl.pallas_call(
        body, in_specs=[ANY] * n, out_specs=[ANY] * n,
        out_shape=[jax.ShapeDtypeStruct(a.shape[1:], a.dtype) for a in parts],
        scratch_shapes=[pltpu.SemaphoreType.DMA((n,)), pltpu.SemaphoreType.DMA((n,))],
        name=name,
    )(*parts)


def _chip_scatter(parts, name):
    n = len(parts)
    rows = [a.shape[0] // N_CHIPS for a in parts]

    def body(*refs):
        ins, outs = refs[:n], refs[n:2 * n]
        send_sems, recv_sems, local_sems = refs[2 * n:]
        _, _, c, s = _place()

        def block(w, t):
            return ins[w].at[pl.ds(t * rows[w], rows[w])]

        local = [pltpu.make_async_copy(block(w, s), outs[w].at[s], local_sems.at[w]) for w in range(n)]
        for cp in local:
            cp.start()

        def remote(w, t):
            return pltpu.make_async_remote_copy(
                src_ref=block(w, t), dst_ref=outs[w].at[s], send_sem=send_sems.at[w, t], recv_sem=recv_sems.at[w, s],
                device_id=(t // 2, t % 2, c), device_id_type=MESH)

        def arrival(w, t):
            return pltpu.make_async_remote_copy(
                src_ref=block(w, t), dst_ref=outs[w].at[t], send_sem=send_sems.at[w, t], recv_sem=recv_sems.at[w, t],
                device_id=(t // 2, t % 2, c), device_id_type=MESH)

        for t in range(N_CHIPS):
            @pl.when(s != t)
            def _():
                for w in range(n):
                    remote(w, t).start()
        for t in range(N_CHIPS):
            @pl.when(s != t)
            def _():
                for w in range(n):
                    remote(w, t).wait_send()
                    arrival(w, t).wait_recv()
        for cp in local:
            cp.wait()

    return pl.pallas_call(
        body, in_specs=[ANY] * n, out_specs=[ANY] * n,
        out_shape=[jax.ShapeDtypeStruct((N_CHIPS, r, a.shape[1]), a.dtype) for a, r in zip(parts, rows)],
        scratch_shapes=[pltpu.SemaphoreType.DMA((n, N_CHIPS)), pltpu.SemaphoreType.DMA((n, N_CHIPS)),
                        pltpu.SemaphoreType.DMA((n,))],
        name=name,
    )(*parts)


def _pair_merge(parts, name):
    n = len(parts)

    def body(*refs):
        ins, outs = refs[:n], refs[n:2 * n]
        send_sems, recv_sems, local_sems = refs[2 * n:]
        x, y, c, _ = _place()
        local = [pltpu.make_async_copy(ins[w], outs[w].at[c], local_sems.at[w]) for w in range(n)]
        cps = [pltpu.make_async_remote_copy(
            src_ref=ins[w], dst_ref=outs[w].at[c], send_sem=send_sems.at[w], recv_sem=recv_sems.at[w],
            device_id=(x, y, 1 - c), device_id_type=MESH) for w in range(n)]
        arrivals = [pltpu.make_async_remote_copy(
            src_ref=ins[w], dst_ref=outs[w].at[1 - c], send_sem=send_sems.at[w], recv_sem=recv_sems.at[w],
            device_id=(x, y, 1 - c), device_id_type=MESH) for w in range(n)]
        for cp in local + cps:
            cp.start()
        for w in range(n):
            cps[w].wait_send()
            arrivals[w].wait_recv()
        for cp in local:
            cp.wait()

    return pl.pallas_call(
        body, in_specs=[ANY] * n, out_specs=[ANY] * n,
        out_shape=[jax.ShapeDtypeStruct((2,) + a.shape, a.dtype) for a in parts],
        scratch_shapes=[pltpu.SemaphoreType.DMA((n,)), pltpu.SemaphoreType.DMA((n,)), pltpu.SemaphoreType.DMA((n,))],
        name=name,
    )(*parts)


def _all_gather_small(a, name):
    def body(in_ref, out_ref, send_sems, recv_sems, local_sem):
        x, y, c, _ = _place()
        me = 4 * x + 2 * y + c
        local = pltpu.make_async_copy(in_ref, out_ref.at[me], local_sem)
        local.start()

        def remote(d):
            return pltpu.make_async_remote_copy(
                src_ref=in_ref, dst_ref=out_ref.at[me], send_sem=send_sems.at[d], recv_sem=recv_sems.at[me],
                device_id=(d // 4, (d // 2) % 2, d % 2), device_id_type=MESH)

        def arrival(d):
            return pltpu.make_async_remote_copy(
                src_ref=in_ref, dst_ref=out_ref.at[d], send_sem=send_sems.at[d], recv_sem=recv_sems.at[d],
                device_id=(d // 4, (d // 2) % 2, d % 2), device_id_type=MESH)

        for d in range(N_DEV):
            @pl.when(me != d)
            def _():
                remote(d).start()
        for d in range(N_DEV):
            @pl.when(me != d)
            def _():
                remote(d).wait_send()
                arrival(d).wait_recv()
        local.wait()

    return pl.pallas_call(
        body, in_specs=[ANY], out_specs=ANY, out_shape=jax.ShapeDtypeStruct((N_DEV,) + a.shape, a.dtype),
        scratch_shapes=[pltpu.SemaphoreType.DMA((N_DEV,)), pltpu.SemaphoreType.DMA((N_DEV,)), pltpu.SemaphoreType.DMA],
        name=name,
    )(a)


def _reduce_tile(rows):
    return _pick(rows, (256, 16))


def _pair_add(full, other, layer, name):
    _, rows, cols = full.shape
    tr = _reduce_tile(rows)

    def body(layer_ref, a_ref, b_ref, o_ref):
        o_ref[...] = (a_ref[0] + b_ref[...]).astype(o_ref.dtype)

    return pl.pallas_call(
        body,
        grid_spec=pltpu.PrefetchScalarGridSpec(
            num_scalar_prefetch=1, grid=(rows // tr,),
            in_specs=[pl.BlockSpec((1, tr, cols), lambda i, lr: (lr[0], i, 0)), pl.BlockSpec((tr, cols), lambda i, lr: (i, 0))],
            out_specs=pl.BlockSpec((tr, cols), lambda i, lr: (i, 0))),
        out_shape=jax.ShapeDtypeStruct((rows, cols), BF16), compiler_params=_params("parallel"), name=name,
    )(layer, full, other)


def _sum_slots(a, name):
    n, rows, cols = a.shape
    tr = _reduce_tile(rows)

    def body(a_ref, o_ref):
        acc = a_ref[0].astype(F32)
        for k in range(1, n):
            acc = acc + a_ref[k].astype(F32)
        o_ref[...] = acc

    return pl.pallas_call(
        body, grid=(rows // tr,), in_specs=[pl.BlockSpec((n, tr, cols), lambda i: (0, i, 0))],
        out_specs=pl.BlockSpec((tr, cols), lambda i: (i, 0)),
        out_shape=jax.ShapeDtypeStruct((rows, cols), F32), compiler_params=_params("parallel"), name=name,
    )(a)


def _adamw(w, g, m, v, name):
    rows, cols = w.shape
    tr = _pick(rows, (256, 8))

    def body(w_ref, g_ref, m_ref, v_ref, d_ref, nm_ref, nv_ref):
        gv = g_ref[...]
        mn = ADAM_B1 * m_ref[...] + (1.0 - ADAM_B1) * gv
        vn = ADAM_B2 * v_ref[...] + (1.0 - ADAM_B2) * jnp.square(gv)
        m_hat = mn / (1.0 - ADAM_B1 ** ADAM_STEP)
        v_hat = vn / (1.0 - ADAM_B2 ** ADAM_STEP)
        d_ref[...] = -ADAM_LR * (m_hat / (jnp.sqrt(v_hat) + ADAM_EPS) + ADAM_WD * w_ref[...])
        nm_ref[...] = mn
        nv_ref[...] = vn

    blk = pl.BlockSpec((tr, cols), lambda i: (i, 0))
    return pl.pallas_call(
        body, grid=(rows // tr,), in_specs=[blk] * 4, out_specs=[blk] * 3,
        out_shape=[jax.ShapeDtypeStruct((rows, cols), F32)] * 3, compiler_params=_params("parallel"), name=name,
    )(w, g, m, v)


BIG = ("ssm_w_in", "ssm_w_out", "att_w_in", "att_w_out")
SHARDED = BIG + ("ssm_conv_w",)
SMALL = ("pre_norm", "post_norm", "ssm_conv_b", "ssm_dt_bias", "ssm_a_log", "ssm_d", "ssm_gate_norm", "att_sinks")
WEIGHTS = ("pre_norm", "post_norm", "ssm_w_in", "ssm_conv_w", "ssm_conv_b", "ssm_dt_bias", "ssm_a_log", "ssm_d",
           "ssm_gate_norm", "ssm_w_out", "att_w_in", "att_sinks", "att_w_out")


def _cols_to_whole(g):
    _, two, rows, cols = g.shape
    return jnp.transpose(g, (1, 2, 0, 3)).reshape(two, rows, N_CHIPS * cols)


def _rows_to_whole(g):
    _, two, rows, cols = g.shape
    return jnp.transpose(g, (1, 0, 2, 3)).reshape(two, N_CHIPS * rows, cols)


def _cols_by_chip(g):
    two, rows, cols = g.shape
    return jnp.transpose(g.reshape(two, rows, N_CHIPS, cols // N_CHIPS), (0, 2, 1, 3)).reshape(two, N_CHIPS * rows, cols // N_CHIPS)


def _pack_small(tree, keys):
    flat = jnp.concatenate([tree[k].reshape(-1) for k in keys])
    rows = -(-flat.shape[0] // (8 * LANES)) * 8
    return jnp.pad(flat, (0, rows * LANES - flat.shape[0])).reshape(rows, LANES)


def _unpack_small(packed, shapes, keys):
    flat = packed.reshape(-1)
    out, at = {}, 0
    for k in keys:
        n = 1
        for dim in shapes[k]:
            n *= dim
        out[k] = flat[at:at + n].reshape(shapes[k])
        at += n
    return out


def kernel(x, positions, pre_norm, post_norm, ssm_w_in, ssm_conv_w, ssm_conv_b, ssm_dt_bias, ssm_a_log, ssm_d, ssm_gate_norm, ssm_w_out, att_w_in, att_sinks, att_w_out, loss_target, m_pre_norm, m_post_norm, m_ssm_w_in, m_ssm_conv_w, m_ssm_conv_b, m_ssm_dt_bias, m_ssm_a_log, m_ssm_d, m_ssm_gate_norm, m_ssm_w_out, m_att_w_in, m_att_sinks, m_att_w_out, v_pre_norm, v_post_norm, v_ssm_w_in, v_ssm_conv_w, v_ssm_conv_b, v_ssm_dt_bias, v_ssm_a_log, v_ssm_d, v_ssm_gate_norm, v_ssm_w_out, v_att_w_in, v_att_sinks, v_att_w_out):
    w = dict(pre_norm=pre_norm, post_norm=post_norm, ssm_w_in=ssm_w_in, ssm_conv_w=ssm_conv_w, ssm_conv_b=ssm_conv_b,
             ssm_dt_bias=ssm_dt_bias, ssm_a_log=ssm_a_log, ssm_d=ssm_d, ssm_gate_norm=ssm_gate_norm, ssm_w_out=ssm_w_out,
             att_w_in=att_w_in, att_sinks=att_sinks, att_w_out=att_w_out)
    m = dict(pre_norm=m_pre_norm, post_norm=m_post_norm, ssm_w_in=m_ssm_w_in, ssm_conv_w=m_ssm_conv_w, ssm_conv_b=m_ssm_conv_b,
             ssm_dt_bias=m_ssm_dt_bias, ssm_a_log=m_ssm_a_log, ssm_d=m_ssm_d, ssm_gate_norm=m_ssm_gate_norm,
             ssm_w_out=m_ssm_w_out, att_w_in=m_att_w_in, att_sinks=m_att_sinks, att_w_out=m_att_w_out)
    v = dict(pre_norm=v_pre_norm, post_norm=v_post_norm, ssm_w_in=v_ssm_w_in, ssm_conv_w=v_ssm_conv_w, ssm_conv_b=v_ssm_conv_b,
             ssm_dt_bias=v_ssm_dt_bias, ssm_a_log=v_ssm_a_log, ssm_d=v_ssm_d, ssm_gate_norm=v_ssm_gate_norm,
             ssm_w_out=v_ssm_w_out, att_w_in=v_att_w_in, att_sinks=v_att_sinks, att_w_out=v_att_w_out)
    c = lax.axis_index("c")
    chip = 2 * lax.axis_index("x") + lax.axis_index("y")

    g_in, g_out, g_ain, g_aout, g_cw = _chip_gather(
        [ssm_w_in.astype(BF16), ssm_w_out.astype(BF16), att_w_in.astype(BF16), att_w_out.astype(BF16), ssm_conv_w],
        "gather_weights")
    w_in_full = jnp.pad(_cols_to_whole(g_in), ((0, 0), (0, 0), (0, SSM_IN_PAD - SSM_IN_DIM)))
    loss_lanes, grad_x, gr = _local_step(
        x[0], positions[0], pre_norm, post_norm, w_in_full, _cols_to_whole(g_cw), ssm_conv_b, ssm_dt_bias, ssm_a_log,
        ssm_d, ssm_gate_norm, _rows_to_whole(g_out), _cols_to_whole(g_ain), att_sinks, _rows_to_whole(g_aout),
        loss_target[0])
    loss = lax.psum(0.5 * jnp.sum(loss_lanes) / D_MODEL, ("x", "y", "c"))

    parts = [_cols_by_chip(gr["ssm_w_in"][:, :, :SSM_IN_DIM]), gr["ssm_w_out"], _cols_by_chip(gr["att_w_in"]),
             gr["att_w_out"]]
    from_sibling = _pair_swap(parts, "reduce_pair_swap")
    layer = jnp.reshape(c, (1,)).astype(jnp.int32)
    chip_sums = [_pair_add(p, o, layer, f"reduce_pair_add_{k}") for k, (p, o) in enumerate(zip(parts, from_sibling))]
    by_chip = _chip_scatter(chip_sums, "reduce_chip_scatter")
    mine = [_sum_slots(a, f"reduce_chip_sum_{k}") for k, a in enumerate(by_chip)]
    both = _pair_merge(mine, "reduce_pair_merge")
    grads = {k: g.reshape(w[k].shape) for k, g in zip(BIG, both)}

    small_keys = SMALL + ("ssm_conv_w",)
    small_shapes = {k: w[k].shape for k in SMALL}
    small_shapes["ssm_conv_w"] = gr["ssm_conv_w"].shape
    small_sum = _sum_slots(_all_gather_small(_pack_small(gr, small_keys), "reduce_small_gather"), "reduce_small_sum")
    grads.update(_unpack_small(small_sum, small_shapes, small_keys))
    conv_cols = ssm_conv_w.shape[2]
    grads["ssm_conv_w"] = lax.dynamic_slice_in_dim(grads["ssm_conv_w"], chip * conv_cols, conv_cols, axis=2)

    delta, new_m, new_v = {}, {}, {}
    for k in SHARDED:
        shp = w[k].shape
        two_d = (shp[0] * shp[1], shp[2])
        d_, m_, v_ = _adamw(w[k].reshape(two_d), grads[k].reshape(two_d), m[k].reshape(two_d), v[k].reshape(two_d),
                            f"adamw_{k}")
        delta[k], new_m[k], new_v[k] = d_.reshape(shp), m_.reshape(shp), v_.reshape(shp)
    d_, m_, v_ = _adamw(_pack_small(w, SMALL), _pack_small(grads, SMALL), _pack_small(m, SMALL), _pack_small(v, SMALL),
                        "adamw_small")
    delta.update(_unpack_small(d_, small_shapes, SMALL))
    new_m.update(_unpack_small(m_, small_shapes, SMALL))
    new_v.update(_unpack_small(v_, small_shapes, SMALL))

    return (loss, grad_x[None], *[grads[k] for k in WEIGHTS], *[delta[k] for k in WEIGHTS],
            *[new_m[k] for k in WEIGHTS], *[new_v[k] for k in WEIGHTS])
```

```python
import functools

import jax
import jax.numpy as jnp
from jax import lax
from jax.experimental import pallas as pl
from jax.experimental.pallas import tpu as pltpu

F32 = jnp.float32
BF16 = jnp.bfloat16
EPS = 1e-6
NEG_INF = float("-inf")

D_MODEL = 1024
DEPTH = 4
SSM_D_INNER = 2048
SSM_HEAD_DIM = 64
SSM_HEADS = 32
SSM_GROUPS = 8
SSM_HPG = 4
SSM_STATE = 128
SSM_CONV = 4
SSM_CHUNK = 128
SSM_BC_DIM = 1024
SSM_CONV_DIM = 4096
SSM_IN_DIM = 6176
SSM_IN_PAD = 6272
SSM_DT_PAD = 128
ATT_HEAD_DIM = 64
ATT_Q_HEADS = 16
ATT_KV_HEADS = 4
ATT_GQA = 4
ATT_WIDTH = 1024
ATT_KV_WIDTH = 256
ATT_IN_DIM = 2560
ATT_QKV = ATT_WIDTH + 2 * ATT_KV_WIDTH
ATT_BLOCK = 128
ROPE_THETA = 500000.0
ROPE_DIM = 16
ROPE_HALF = 8
Q_SCALE = ATT_HEAD_DIM ** -0.5

ADAM_LR = 0.001
ADAM_B1 = 0.9
ADAM_B2 = 0.999
ADAM_EPS = 1e-08
ADAM_WD = 0.01
ADAM_STEP = 10

VMEM_LIMIT_BYTES = 48 * 1024 * 1024
NT_DIMS = (((1,), (1,)), ((), ()))
TN_DIMS = (((0,), (0,)), ((), ()))


def _params(*sem):
    return pltpu.CompilerParams(dimension_semantics=sem, vmem_limit_bytes=VMEM_LIMIT_BYTES)


def _pick(n, cands):
    for c in cands:
        if n % c == 0:
            return c
    return n


def _sigmoid(v):
    return 1.0 / (1.0 + jnp.exp(-v))


def _bdot(a, b):
    return jnp.dot(a.astype(BF16), b.astype(BF16), preferred_element_type=F32)


def _bdot_nt(a, b):
    return lax.dot_general(a.astype(BF16), b.astype(BF16), NT_DIMS, preferred_element_type=F32)


def _bdot_tn(a, b):
    return lax.dot_general(a.astype(BF16), b.astype(BF16), TN_DIMS, preferred_element_type=F32)


MATMUL_VMEM_BUDGET = 36 * 1024 * 1024


def _matmul_tiles(m, n, k, out_bytes, reduce_rows):
    best = None
    whole = [k] if (not reduce_rows or k <= 2048) else []
    for tk in whole + [c for c in (4096, 2048, 1024, 896, 512) if k % c == 0 and c < k]:
        for tm in (c for c in (1024, 512, 256) if m % c == 0):
            for tn in (c for c in (n, 1280, 1024, 896, 640, 512) if n % c == 0):
                acc = tm * tn * 4 if tk < k else 0
                need = 2 * (2 * tk * (tm + tn) + tm * tn * out_bytes) + acc
                if need <= MATMUL_VMEM_BUDGET and (best is None or tm * tn * min(tk, 2048) > best[0]):
                    best = (tm * tn * min(tk, 2048), tm, tn, tk)
        if best is not None and not reduce_rows:
            break
    return best[1:]


def _matmul(a, b, mode, out_dtype, name):
    if mode == "nn":
        (m, k), n = a.shape, b.shape[1]
    elif mode == "nt":
        (m, k), n = a.shape, b.shape[0]
    else:
        (k, m), n = a.shape, b.shape[1]
    tm, tn, tk = _matmul_tiles(m, n, k, jnp.dtype(out_dtype).itemsize, mode == "tn")
    nk = k // tk
    dims = {"nn": (((1,), (0,)), ((), ())), "nt": NT_DIMS, "tn": TN_DIMS}[mode]

    def body(a_ref, b_ref, o_ref, acc_ref):
        kk = pl.program_id(2)
        part = lax.dot_general(a_ref[...], b_ref[...], dims, preferred_element_type=F32)
        if nk == 1:
            o_ref[...] = part.astype(o_ref.dtype)
        else:
            @pl.when(kk == 0)
            def _():
                acc_ref[...] = part

            @pl.when(kk > 0)
            def _():
                acc_ref[...] += part

            @pl.when(kk == nk - 1)
            def _():
                o_ref[...] = acc_ref[...].astype(o_ref.dtype)

    if mode == "nn":
        a_spec = pl.BlockSpec((tm, tk), lambda j, i, kk: (i, kk))
        b_spec = pl.BlockSpec((tk, tn), lambda j, i, kk: (kk, j))
    elif mode == "nt":
        a_spec = pl.BlockSpec((tm, tk), lambda j, i, kk: (i, kk))
        b_spec = pl.BlockSpec((tn, tk), lambda j, i, kk: (j, kk))
    else:
        a_spec = pl.BlockSpec((tk, tm), lambda j, i, kk: (kk, i))
        b_spec = pl.BlockSpec((tk, tn), lambda j, i, kk: (kk, j))
    return pl.pallas_call(
        body, grid=(n // tn, m // tm, nk), in_specs=[a_spec, b_spec],
        out_specs=pl.BlockSpec((tm, tn), lambda j, i, kk: (i, j)),
        out_shape=jax.ShapeDtypeStruct((m, n), out_dtype),
        scratch_shapes=[pltpu.VMEM((tm, tn), F32)],
        compiler_params=_params("parallel", "parallel", "arbitrary"), name=name,
    )(a, b)


def _row_tile(l):
    return _pick(l, (512, 256, 128))


def _rmsnorm_fwd(x, w, name):
    l, d = x.shape
    tl = _row_tile(l)

    def body(x_ref, w_ref, o_ref):
        xv = x_ref[...]
        r = lax.rsqrt(jnp.mean(xv * xv, axis=-1, keepdims=True) + EPS)
        o_ref[...] = (xv * r * w_ref[...]).astype(o_ref.dtype)

    return pl.pallas_call(
        body, grid=(l // tl,),
        in_specs=[pl.BlockSpec((tl, d), lambda i: (i, 0)), pl.BlockSpec((1, d), lambda i: (0, 0))],
        out_specs=pl.BlockSpec((tl, d), lambda i: (i, 0)),
        out_shape=jax.ShapeDtypeStruct((l, d), BF16), compiler_params=_params("parallel"), name=name,
    )(x, w.reshape(1, d))


def _post_fwd(x, y, w, name):
    l, d = x.shape
    tl = _row_tile(l)

    def body(x_ref, y_ref, w_ref, o_ref):
        yv = y_ref[...]
        r = lax.rsqrt(jnp.mean(yv * yv, axis=-1, keepdims=True) + EPS)
        o_ref[...] = x_ref[...] + yv * r * w_ref[...]

    return pl.pallas_call(
        body, grid=(l // tl,),
        in_specs=[pl.BlockSpec((tl, d), lambda i: (i, 0)), pl.BlockSpec((tl, d), lambda i: (i, 0)),
                  pl.BlockSpec((1, d), lambda i: (0, 0))],
        out_specs=pl.BlockSpec((tl, d), lambda i: (i, 0)),
        out_shape=jax.ShapeDtypeStruct((l, d), F32), compiler_params=_params("parallel"), name=name,
    )(x, y, w.reshape(1, d))


def _rmsnorm_bwd(g, y, w, resid, out_dtype, name):
    l, d = y.shape
    tl = _row_tile(l)
    nt = l // tl
    has_resid = resid is not None

    def body(*refs):
        if has_resid:
            g_ref, y_ref, w_ref, r_ref, dy_ref, dw_ref, acc_ref = refs
        else:
            g_ref, y_ref, w_ref, dy_ref, dw_ref, acc_ref = refs
        i = pl.program_id(0)

        @pl.when(i == 0)
        def _():
            acc_ref[...] = jnp.zeros_like(acc_ref)

        yv = y_ref[...]
        gv = g_ref[...].astype(F32)
        r = lax.rsqrt(jnp.mean(yv * yv, axis=-1, keepdims=True) + EPS)
        nrm = yv * r
        gw = gv * w_ref[...]
        dy = r * (gw - nrm * jnp.mean(gw * nrm, axis=-1, keepdims=True))
        if has_resid:
            dy = dy + r_ref[...]
        dy_ref[...] = dy.astype(dy_ref.dtype)
        acc_ref[...] += jnp.sum((gv * nrm).reshape(tl // 8, 8, d), axis=0)

        @pl.when(i == nt - 1)
        def _():
            dw_ref[...] = jnp.sum(acc_ref[...], axis=0, keepdims=True)

    row = pl.BlockSpec((tl, d), lambda i: (i, 0))
    vec = pl.BlockSpec((1, d), lambda i: (0, 0))
    ins = [g, y, w.reshape(1, d)] + ([resid] if has_resid else [])
    return pl.pallas_call(
        body, grid=(nt,), in_specs=[row, row, vec] + ([row] if has_resid else []),
        out_specs=[row, vec],
        out_shape=[jax.ShapeDtypeStruct((l, d), out_dtype), jax.ShapeDtypeStruct((1, d), F32)],
        scratch_shapes=[pltpu.VMEM((8, d), F32)], compiler_params=_params("arbitrary"), name=name,
    )(*ins)


def _loss_grad(y, t, name):
    l, d = y.shape
    tl = _row_tile(l)

    def body(y_ref, t_ref, dy_ref, ls_ref):
        @pl.when(pl.program_id(0) == 0)
        def _():
            ls_ref[...] = jnp.zeros_like(ls_ref)

        e = y_ref[...] - t_ref[...]
        dy_ref[...] = e * (1.0 / d)
        ls_ref[...] += jnp.sum((e * e).reshape(tl // 8, 8, d), axis=0)

    row = pl.BlockSpec((tl, d), lambda i: (i, 0))
    return pl.pallas_call(
        body, grid=(l // tl,), in_specs=[row, row],
        out_specs=[row, pl.BlockSpec((8, d), lambda i: (0, 0))],
        out_shape=[jax.ShapeDtypeStruct((l, d), F32), jax.ShapeDtypeStruct((8, d), F32)],
        compiler_params=_params("arbitrary"), name=name,
    )(y, t)


CONV_COLS = 512
HALO = 8


def _conv_fwd(proj, cw, cb, name):
    l = proj.shape[0]
    tl = _row_tile(l)
    off = SSM_D_INNER // CONV_COLS

    def body(u_ref, halo_ref, w_ref, b_ref, pre_ref, act_ref, ext_ref):
        i = pl.program_id(1)
        ext_ref[0:HALO, :] = jnp.where(i > 0, halo_ref[...], 0.0)
        ext_ref[HALO:HALO + tl, :] = u_ref[...]
        acc = jnp.broadcast_to(b_ref[...], (tl, CONV_COLS))
        for k in range(SSM_CONV):
            acc = acc + w_ref[k:k + 1, :] * ext_ref[pl.ds(HALO - SSM_CONV + 1 + k, tl), :]
        pre_ref[...] = acc
        act_ref[...] = acc * _sigmoid(acc)

    hb = tl // HALO
    out = pl.BlockSpec((tl, CONV_COLS), lambda j, i: (i, j))
    return pl.pallas_call(
        body, grid=(SSM_CONV_DIM // CONV_COLS, l // tl),
        in_specs=[pl.BlockSpec((tl, CONV_COLS), lambda j, i: (i, off + j)),
                  pl.BlockSpec((HALO, CONV_COLS), lambda j, i: (jnp.maximum(i * hb - 1, 0), off + j)),
                  pl.BlockSpec((SSM_CONV, CONV_COLS), lambda j, i: (0, j)),
                  pl.BlockSpec((1, CONV_COLS), lambda j, i: (0, j))],
        out_specs=[out, out],
        out_shape=[jax.ShapeDtypeStruct((l, SSM_CONV_DIM), F32)] * 2,
        scratch_shapes=[pltpu.VMEM((tl + HALO, CONV_COLS), F32)],
        compiler_params=_params("parallel", "arbitrary"), name=name,
    )(proj, proj, cw, cb.reshape(1, SSM_CONV_DIM))


def _conv_bwd(dact, pre, proj, cw, c0, name):
    l, width = dact.shape
    tl = _row_tile(l)
    nt = l // tl
    pre_off = c0 // CONV_COLS
    u_off = (SSM_D_INNER + c0) // CONV_COLS
    hb = tl // HALO
    last_hb = l // HALO - 1

    def body(da_ref, da_h_ref, p_ref, p_h_ref, u_ref, u_h_ref, w_ref, du_ref, dw_ref, db_ref, ext_ref, uext_ref):
        i = pl.program_id(1)

        @pl.when(i == 0)
        def _():
            dw_ref[...] = jnp.zeros_like(dw_ref)
            db_ref[...] = jnp.zeros_like(db_ref)

        def dpre_of(da, p):
            s = _sigmoid(p)
            return da * (s * (1.0 + p * (1.0 - s)))

        dp = dpre_of(da_ref[...], p_ref[...])
        ext_ref[0:tl, :] = dp
        ext_ref[tl:tl + HALO, :] = jnp.where(i < nt - 1, dpre_of(da_h_ref[...], p_h_ref[...]), 0.0)
        uext_ref[0:HALO, :] = jnp.where(i > 0, u_h_ref[...], 0.0)
        uext_ref[HALO:HALO + tl, :] = u_ref[...]
        du = jnp.zeros((tl, CONV_COLS), F32)
        for k in range(SSM_CONV):
            du = du + w_ref[k:k + 1, :] * ext_ref[pl.ds(SSM_CONV - 1 - k, tl), :]
            dw_ref[k:k + 1, :] += jnp.sum(dp * uext_ref[pl.ds(HALO - SSM_CONV + 1 + k, tl), :], axis=0, keepdims=True)
        du_ref[...] = du.astype(du_ref.dtype)
        db_ref[...] += jnp.sum(dp, axis=0, keepdims=True)

    return pl.pallas_call(
        body, grid=(width // CONV_COLS, nt),
        in_specs=[pl.BlockSpec((tl, CONV_COLS), lambda j, i: (i, j)),
                  pl.BlockSpec((HALO, CONV_COLS), lambda j, i: (jnp.minimum((i + 1) * hb, last_hb), j)),
                  pl.BlockSpec((tl, CONV_COLS), lambda j, i: (i, pre_off + j)),
                  pl.BlockSpec((HALO, CONV_COLS), lambda j, i: (jnp.minimum((i + 1) * hb, last_hb), pre_off + j)),
                  pl.BlockSpec((tl, CONV_COLS), lambda j, i: (i, u_off + j)),
                  pl.BlockSpec((HALO, CONV_COLS), lambda j, i: (jnp.maximum(i * hb - 1, 0), u_off + j)),
                  pl.BlockSpec((SSM_CONV, CONV_COLS), lambda j, i: (0, pre_off + j))],
        out_specs=[pl.BlockSpec((tl, CONV_COLS), lambda j, i: (i, j)),
                   pl.BlockSpec((SSM_CONV, CONV_COLS), lambda j, i: (0, j)),
                   pl.BlockSpec((1, CONV_COLS), lambda j, i: (0, j))],
        out_shape=[jax.ShapeDtypeStruct((l, width), BF16), jax.ShapeDtypeStruct((SSM_CONV, width), F32),
                   jax.ShapeDtypeStruct((1, width), F32)],
        scratch_shapes=[pltpu.VMEM((tl + HALO, CONV_COLS), F32), pltpu.VMEM((tl + HALO, CONV_COLS), F32)],
        compiler_params=_params("parallel", "arbitrary"), name=name,
    )(dact, dact, pre, pre, proj, proj, cw)


DT_COL_BLOCK = (SSM_D_INNER + SSM_CONV_DIM) // SSM_DT_PAD


def _split3(v):
    hi = v.astype(BF16)
    rest = v - hi.astype(F32)
    mid = rest.astype(BF16)
    lo = (rest - mid.astype(F32)).astype(BF16)
    return hi, mid, lo


def _ssd_prep(proj, bias, alog_lanes, name):
    l = proj.shape[0]
    nc = l // SSM_CHUNK
    head_dim_log2 = SSM_HEAD_DIM.bit_length() - 1

    def body(p_ref, b_ref, al_ref, dtb_ref, acsb_ref, dtr_ref, acsr_ref):
        v = p_ref[...] + b_ref[...]
        dt = jnp.maximum(v, 0.0) + jnp.log1p(jnp.exp(-jnp.abs(v)))
        head_of_lane = lax.shift_right_logical(lax.broadcasted_iota(jnp.int32, (SSM_DT_PAD, SSM_D_INNER), 1), head_dim_log2)
        spread = (head_of_lane == lax.broadcasted_iota(jnp.int32, (SSM_DT_PAD, SSM_D_INNER), 0)).astype(BF16)
        dtb = sum(jnp.dot(piece, spread, preferred_element_type=F32) for piece in _split3(dt))
        dtb_ref[...] = dtb
        ri = lax.broadcasted_iota(jnp.int32, (SSM_CHUNK, SSM_CHUNK), 0)
        cj = lax.broadcasted_iota(jnp.int32, (SSM_CHUNK, SSM_CHUNK), 1)
        tri = (ri >= cj).astype(BF16)
        acsb = sum(jnp.dot(tri, piece, preferred_element_type=F32) for piece in _split3(dtb * (-jnp.exp(al_ref[...]))))
        acsb_ref[...] = acsb
        gp = SSM_HPG * SSM_HEAD_DIM
        lane = lax.broadcasted_iota(jnp.int32, (SSM_HPG, gp), 1)
        pick = (lane == lax.broadcasted_iota(jnp.int32, (SSM_HPG, gp), 0) * SSM_HEAD_DIM).astype(BF16)
        for g in range(SSM_GROUPS):
            cols = slice(g * gp, (g + 1) * gp)
            dtr_ref[g] = sum(lax.dot_general(pick, piece, NT_DIMS, preferred_element_type=F32)
                             for piece in _split3(dtb[:, cols]))
            acsr_ref[g] = sum(lax.dot_general(pick, piece, NT_DIMS, preferred_element_type=F32)
                              for piece in _split3(acsb[:, cols]))

    rows = pl.BlockSpec((SSM_GROUPS, SSM_HPG, SSM_CHUNK), lambda c: (0, 0, c))
    dense = pl.BlockSpec((SSM_CHUNK, SSM_D_INNER), lambda c: (c, 0))
    return pl.pallas_call(
        body, grid=(nc,),
        in_specs=[pl.BlockSpec((SSM_CHUNK, SSM_DT_PAD), lambda c: (c, DT_COL_BLOCK)),
                  pl.BlockSpec((1, SSM_DT_PAD), lambda c: (0, 0)),
                  pl.BlockSpec((1, SSM_D_INNER), lambda c: (0, 0))],
        out_specs=[dense, dense, rows, rows],
        out_shape=[jax.ShapeDtypeStruct((l, SSM_D_INNER), F32), jax.ShapeDtypeStruct((l, SSM_D_INNER), F32),
                   jax.ShapeDtypeStruct((SSM_GROUPS, SSM_HPG, l), F32),
                   jax.ShapeDtypeStruct((SSM_GROUPS, SSM_HPG, l), F32)],
        compiler_params=_params("parallel"), name=name,
    )(proj, bias, alog_lanes)


def _dt_bwd(ddt, proj, bias, name):
    l = proj.shape[0]
    tl = _row_tile(l)

    def body(g_ref, p_ref, b_ref, o_ref, db_ref):
        @pl.when(pl.program_id(0) == 0)
        def _():
            db_ref[...] = jnp.zeros_like(db_ref)

        d = g_ref[...] * _sigmoid(p_ref[...] + b_ref[...])
        o_ref[...] = d.astype(o_ref.dtype)
        db_ref[...] += jnp.sum(d, axis=0, keepdims=True)

    return pl.pallas_call(
        body, grid=(l // tl,),
        in_specs=[pl.BlockSpec((tl, SSM_DT_PAD), lambda i: (i, 0)),
                  pl.BlockSpec((tl, SSM_DT_PAD), lambda i: (i, DT_COL_BLOCK)),
                  pl.BlockSpec((1, SSM_DT_PAD), lambda i: (0, 0))],
        out_specs=[pl.BlockSpec((tl, SSM_DT_PAD), lambda i: (i, 0)), pl.BlockSpec((1, SSM_DT_PAD), lambda i: (0, 0))],
        out_shape=[jax.ShapeDtypeStruct((l, SSM_DT_PAD), BF16), jax.ShapeDtypeStruct((1, SSM_DT_PAD), F32)],
        compiler_params=_params("arbitrary"), name=name,
    )(ddt, proj, bias)


GP = SSM_HPG * SSM_HEAD_DIM
HEAD_DIM_LOG2 = SSM_HEAD_DIM.bit_length() - 1
CHUNK_LOG2 = SSM_CHUNK.bit_length() - 1
GPS = 4
B_BLOCK0 = SSM_D_INNER // SSM_STATE
C_BLOCK0 = (SSM_D_INNER + SSM_BC_DIM) // SSM_STATE


def _chunk_iotas():
    ri = lax.broadcasted_iota(jnp.int32, (SSM_CHUNK, SSM_CHUNK), 0)
    cj = lax.broadcasted_iota(jnp.int32, (SSM_CHUNK, SSM_CHUNK), 1)
    return ri, cj


def _head_decay(acsb, acs_r, r, ri, cj):
    pair = acsb[:, (r // 2) * LANES:(r // 2 + 1) * LANES]
    mine_low = r % 2 == 0
    lane = lax.broadcasted_iota(jnp.int32, (1, LANES), 1)
    col = jnp.where((lane < SSM_HEAD_DIM) == mine_low, pair, pltpu.roll(pair, SSM_HEAD_DIM, 1))
    return jnp.exp(jnp.where(ri >= cj, col - acs_r[r:r + 1, :], NEG_INF))


def _head_masked_rows(v, dtype):
    head_of_lane = lax.shift_right_logical(lax.broadcasted_iota(jnp.int32, (1, GP), 1), HEAD_DIM_LOG2)
    return jnp.concatenate([jnp.where(head_of_lane == r, v, 0.0).astype(dtype) for r in range(SSM_HPG)], axis=0)


def _ssd_fwd(xbc, dtb, acsb, acs_r, d_lanes, name):
    l = xbc.shape[0]
    nc = l // SSM_CHUNK

    def body(x_ref, b_ref, c_ref, dtb_ref, acsb_ref, acsr_ref, d_ref, y_ref, hin_ref, h_ref):
        c = pl.program_id(0)
        gi = pl.program_id(1)
        ri, cj = _chunk_iotas()
        for k in range(GPS):
            g = gi * GPS + k
            cols = slice(k * GP, (k + 1) * GP)
            ncols = slice(k * SSM_STATE, (k + 1) * SSM_STATE)

            @pl.when(c == 0)
            def _():
                h_ref[g] = jnp.zeros((SSM_STATE, GP), F32)

            xv = x_ref[:, cols]
            bb = b_ref[:, ncols].astype(BF16)
            cb16 = c_ref[:, ncols].astype(BF16)
            acs_v = acsb_ref[:, cols]
            acs_r_v = acsr_ref[k]
            lastb = acs_v[SSM_CHUNK - 1:SSM_CHUNK, :]
            xd = xv * dtb_ref[:, cols]
            cb = lax.dot_general(cb16, bb, NT_DIMS, preferred_element_type=F32)
            hin = h_ref[g]
            hin_ref[0, k] = hin
            yoff = jnp.dot(cb16, hin.astype(BF16), preferred_element_type=F32)
            ms = [(cb * _head_decay(acs_v, acs_r_v, r, ri, cj)).astype(BF16) for r in range(SSM_HPG)]
            ydiag = jnp.dot(jnp.concatenate(ms, axis=1), _head_masked_rows(xd, BF16), preferred_element_type=F32)
            y_ref[:, cols] = ydiag + jnp.exp(acs_v) * yoff + d_ref[k] * xv
            h_ref[g] = hin * jnp.exp(lastb) + _bdot_tn(bb, xd * jnp.exp(lastb - acs_v))

    lanes = pl.BlockSpec((SSM_CHUNK, GPS * GP), lambda c, g: (c, g))
    return pl.pallas_call(
        body, grid=(nc, SSM_GROUPS // GPS),
        in_specs=[lanes,
                  pl.BlockSpec((SSM_CHUNK, GPS * SSM_STATE), lambda c, g: (c, B_BLOCK0 // GPS + g)),
                  pl.BlockSpec((SSM_CHUNK, GPS * SSM_STATE), lambda c, g: (c, C_BLOCK0 // GPS + g)),
                  lanes, lanes,
                  pl.BlockSpec((GPS, SSM_HPG, SSM_CHUNK), lambda c, g: (g, 0, c)),
                  pl.BlockSpec((GPS, 1, GP), lambda c, g: (g, 0, 0))],
        out_specs=[lanes, pl.BlockSpec((1, GPS, SSM_STATE, GP), lambda c, g: (c, g, 0, 0))],
        out_shape=[jax.ShapeDtypeStruct((l, SSM_D_INNER), F32),
                   jax.ShapeDtypeStruct((nc, SSM_GROUPS, SSM_STATE, GP), F32)],
        scratch_shapes=[pltpu.VMEM((SSM_GROUPS, SSM_STATE, GP), F32)],
        compiler_params=_params("arbitrary", "arbitrary"), name=name,
    )(xbc, xbc, xbc, dtb, acsb, acs_r, d_lanes)


def _ssd_bwd(xbc, dtb, acsb, dtr, acs_r, a_log, d_lanes, hin, dy, name):
    l = xbc.shape[0]
    nc = l // SSM_CHUNK

    def body(x_ref, b_ref, c_ref, dtb_ref, acsb_ref, dtr_ref, acsr_ref, alc_ref, d_ref, hin_ref, dy_ref,
             dx_ref, db_ref, dc_ref, ddt_ref, dal_ref, dd_ref, dh_ref):
        c = pl.program_id(0)
        gi = pl.program_id(1)

        @pl.when((c == 0) & (gi == 0))
        def _():
            dal_ref[...] = jnp.zeros_like(dal_ref)
            dd_ref[...] = jnp.zeros_like(dd_ref)

        for k in range(GPS):
            one_group(c, gi * GPS + k, k, x_ref, b_ref, c_ref, dtb_ref, acsb_ref, dtr_ref, acsr_ref, alc_ref, d_ref,
                      hin_ref, dy_ref, dx_ref, db_ref, dc_ref, ddt_ref, dal_ref, dd_ref, dh_ref)

    def one_group(c, g, k, x_ref, b_ref, c_ref, dtb_ref, acsb_ref, dtr_ref, acsr_ref, alc_ref, d_ref, hin_ref, dy_ref,
                  dx_ref, db_ref, dc_ref, ddt_ref, dal_ref, dd_ref, dh_ref):
        cols = slice(k * GP, (k + 1) * GP)
        ncols = slice(k * SSM_STATE, (k + 1) * SSM_STATE)

        @pl.when(c == 0)
        def _():
            dh_ref[g] = jnp.zeros((SSM_STATE, GP), F32)

        xv = x_ref[:, cols]
        dyv = dy_ref[:, cols]
        bb = b_ref[:, ncols].astype(BF16)
        cb16 = c_ref[:, ncols].astype(BF16)
        dtb = dtb_ref[:, cols]
        acsb = acsb_ref[:, cols]
        dtr_v = dtr_ref[k]
        acs_r = acsr_ref[k]
        a_col = -jnp.exp(alc_ref[k])
        ri, cj = _chunk_iotas()
        head_of_lane = lax.shift_right_logical(lax.broadcasted_iota(jnp.int32, (SSM_HPG, GP), 1), HEAD_DIM_LOG2)
        ind_t = (head_of_lane == lax.broadcasted_iota(jnp.int32, (SSM_HPG, GP), 0)).astype(BF16)
        lastb = acsb[SSM_CHUNK - 1:SSM_CHUNK, :]
        ecb = jnp.exp(acsb)
        dteb = jnp.exp(lastb - acsb)
        xd = xv * dtb
        xw = xd * dteb
        cb = lax.dot_general(cb16, bb, NT_DIMS, preferred_element_type=F32)
        hin_v = hin_ref[0, k]
        dhn = dh_ref[g]
        h16 = hin_v.astype(BF16)
        dh16 = dhn.astype(BF16)
        ch = jnp.dot(cb16, h16, preferred_element_type=F32)
        bdh = jnp.dot(bb, dh16, preferred_element_type=F32)
        dym = _head_masked_rows(dyv, BF16)
        g_all = lax.dot_general(dym, xd.astype(BF16), NT_DIMS, preferred_element_type=F32)
        gl_sum = jnp.zeros((SSM_CHUNK, SSM_CHUNK), F32)
        ms, qs = [], []
        for r in range(SSM_HPG):
            decay = _head_decay(acsb, acs_r, r, ri, cj)
            gl = g_all[r * SSM_CHUNK:(r + 1) * SSM_CHUNK] * decay
            gl_sum = gl_sum + gl
            ms.append((cb * decay).astype(BF16))
            qs.append((gl * cb).astype(BF16))
        dxd = lax.dot_general(jnp.concatenate(ms, axis=0), dym, TN_DIMS, preferred_element_type=F32) + dteb * bdh
        cum = jnp.dot(jnp.concatenate(qs, axis=0), (ri < cj).astype(BF16), preferred_element_type=F32)
        sub4 = lax.broadcasted_iota(jnp.int32, (SSM_HPG, 1), 0)
        da = jnp.zeros((SSM_HPG, SSM_CHUNK), F32)
        for r in range(SSM_HPG):
            rect = jnp.sum(jnp.where(ri >= cj, cum[r * SSM_CHUNK:(r + 1) * SSM_CHUNK], 0.0), axis=0, keepdims=True)
            da = da + jnp.where(sub4 == r, rect, 0.0)
        z2 = xw * bdh
        sub8 = lax.broadcasted_iota(jnp.int32, (8, 1), 0)
        col_sums = (jnp.where(sub8 == 0, jnp.sum(z2, axis=0, keepdims=True), 0.0)
                    + jnp.where(sub8 == 1, jnp.sum(dhn * hin_v, axis=0, keepdims=True), 0.0)
                    + jnp.where(sub8 == 2, jnp.sum(dyv * xv, axis=0, keepdims=True), 0.0))
        summands = jnp.concatenate([dyv * ecb * ch - z2, dxd * xv, col_sums], axis=0)
        sums = sum(lax.dot_general(ind_t, piece, NT_DIMS, preferred_element_type=F32) for piece in _split3(summands))
        per_pos = sums[:, :2 * SSM_CHUNK]
        totals = sums[:, 2 * SSM_CHUNK:]
        e_last = totals[:, 0:1] + jnp.exp(acs_r[:, SSM_CHUNK - 1:SSM_CHUNK]) * totals[:, 1:2]
        da = (da + e_last + jnp.dot(per_pos[:, :SSM_CHUNK], (ri >= cj).astype(F32), preferred_element_type=F32,
                                    precision=lax.Precision.HIGHEST))
        ddt_ref[k] = a_col * da + per_pos[:, SSM_CHUNK:]
        dal_ref[g] += a_col * jnp.sum(da * dtr_v, axis=1, keepdims=True)
        dd_ref[g] += totals[:, 2:3]
        dx_ref[:, cols] = dxd * dtb + d_ref[k] * dyv
        w16 = (ecb * dyv).astype(BF16)
        xw16 = xw.astype(BF16)
        gl16 = gl_sum.astype(BF16)
        dc_ref[:, ncols] = (jnp.dot(gl16, bb, preferred_element_type=F32)
                            + lax.dot_general(w16, h16, NT_DIMS, preferred_element_type=F32))
        db_ref[:, ncols] = (lax.dot_general(gl16, cb16, TN_DIMS, preferred_element_type=F32)
                            + lax.dot_general(xw16, dh16, NT_DIMS, preferred_element_type=F32))
        dh_ref[g] = dhn * jnp.exp(lastb) + lax.dot_general(cb16, w16, TN_DIMS, preferred_element_type=F32)

    def rev(c):
        return nc - 1 - c

    small = pl.BlockSpec((SSM_GROUPS, SSM_HPG, 1), lambda c, g: (0, 0, 0))
    lanes = pl.BlockSpec((SSM_CHUNK, GPS * GP), lambda c, g: (rev(c), g))
    rows = pl.BlockSpec((GPS, SSM_HPG, SSM_CHUNK), lambda c, g: (g, 0, rev(c)))
    return pl.pallas_call(
        body, grid=(nc, SSM_GROUPS // GPS),
        in_specs=[lanes,
                  pl.BlockSpec((SSM_CHUNK, GPS * SSM_STATE), lambda c, g: (rev(c), B_BLOCK0 // GPS + g)),
                  pl.BlockSpec((SSM_CHUNK, GPS * SSM_STATE), lambda c, g: (rev(c), C_BLOCK0 // GPS + g)),
                  lanes, lanes, rows, rows,
                  pl.BlockSpec((GPS, SSM_HPG, 1), lambda c, g: (g, 0, 0)),
                  pl.BlockSpec((GPS, 1, GP), lambda c, g: (g, 0, 0)),
                  pl.BlockSpec((1, GPS, SSM_STATE, GP), lambda c, g: (rev(c), g, 0, 0)),
                  lanes],
        out_specs=[lanes,
                   pl.BlockSpec((SSM_CHUNK, GPS * SSM_STATE), lambda c, g: (rev(c), g)),
                   pl.BlockSpec((SSM_CHUNK, GPS * SSM_STATE), lambda c, g: (rev(c), g)),
                   rows, small, small],
        out_shape=[jax.ShapeDtypeStruct((l, SSM_D_INNER), F32), jax.ShapeDtypeStruct((l, SSM_BC_DIM), F32),
                   jax.ShapeDtypeStruct((l, SSM_BC_DIM), F32), jax.ShapeDtypeStruct((SSM_GROUPS, SSM_HPG, l), F32),
                   jax.ShapeDtypeStruct((SSM_GROUPS, SSM_HPG, 1), F32),
                   jax.ShapeDtypeStruct((SSM_GROUPS, SSM_HPG, 1), F32)],
        scratch_shapes=[pltpu.VMEM((SSM_GROUPS, SSM_STATE, GP), F32)],
        compiler_params=_params("arbitrary", "arbitrary"), name=name,
    )(xbc, xbc, xbc, dtb, acsb, dtr, acs_r, a_log.reshape(SSM_GROUPS, SSM_HPG, 1), d_lanes, hin, dy)


def _gatenorm_fwd(y, proj, w, name):
    l = y.shape[0]
    tl = _pick(l, (256, 128))

    def body(y_ref, z_ref, w_ref, o_ref):
        z = z_ref[...]
        yg = y_ref[...] * (z * _sigmoid(z))
        r = lax.rsqrt(jnp.mean(yg * yg, axis=-1, keepdims=True) + EPS)
        o_ref[...] = (yg * r * w_ref[...]).astype(o_ref.dtype)

    row = pl.BlockSpec((tl, SSM_D_INNER), lambda i: (i, 0))
    return pl.pallas_call(
        body, grid=(l // tl,), in_specs=[row, row, pl.BlockSpec((1, SSM_D_INNER), lambda i: (0, 0))],
        out_specs=row, out_shape=jax.ShapeDtypeStruct((l, SSM_D_INNER), BF16),
        compiler_params=_params("parallel"), name=name,
    )(y, proj, w.reshape(1, SSM_D_INNER))


def _gatenorm_bwd(g, y, proj, w, name):
    l = y.shape[0]
    tl = _pick(l, (256, 128))
    nt = l // tl

    def body(g_ref, y_ref, z_ref, w_ref, dy_ref, dz_ref, dw_ref, acc_ref):
        i = pl.program_id(0)

        @pl.when(i == 0)
        def _():
            acc_ref[...] = jnp.zeros_like(acc_ref)

        z = z_ref[...]
        yv = y_ref[...]
        s = _sigmoid(z)
        sz = z * s
        yg = yv * sz
        r = lax.rsqrt(jnp.mean(yg * yg, axis=-1, keepdims=True) + EPS)
        nrm = yg * r
        gv = g_ref[...]
        gw = gv * w_ref[...]
        dyg = r * (gw - nrm * jnp.mean(gw * nrm, axis=-1, keepdims=True))
        dy_ref[...] = dyg * sz
        dz_ref[...] = (dyg * yv * (s * (1.0 + z * (1.0 - s)))).astype(dz_ref.dtype)
        acc_ref[...] += jnp.sum((gv * nrm).reshape(tl // 8, 8, SSM_D_INNER), axis=0)

        @pl.when(i == nt - 1)
        def _():
            dw_ref[...] = jnp.sum(acc_ref[...], axis=0, keepdims=True)

    row = pl.BlockSpec((tl, SSM_D_INNER), lambda i: (i, 0))
    vec = pl.BlockSpec((1, SSM_D_INNER), lambda i: (0, 0))
    return pl.pallas_call(
        body, grid=(nt,), in_specs=[row, row, row, vec], out_specs=[row, row, vec],
        out_shape=[jax.ShapeDtypeStruct((l, SSM_D_INNER), F32), jax.ShapeDtypeStruct((l, SSM_D_INNER), BF16),
                   jax.ShapeDtypeStruct((1, SSM_D_INNER), F32)],
        scratch_shapes=[pltpu.VMEM((8, SSM_D_INNER), F32)], compiler_params=_params("arbitrary"), name=name,
    )(g, y, proj, w.reshape(1, SSM_D_INNER))


LANES = 128
ROPE_Q_CHUNKS = ATT_WIDTH // LANES
ROPE_K_CHUNKS = ATT_KV_WIDTH // LANES


def _rope_tables(positions):
    inv = ROPE_THETA ** (-jnp.arange(0, ROPE_DIM, 2, dtype=F32) / ROPE_DIM)
    ang = positions.astype(F32)[:, None] * inv
    cos, sin = jnp.cos(ang), jnp.sin(ang)
    l = positions.shape[0]
    rest = ATT_HEAD_DIM - ROPE_DIM
    ones, zeros = jnp.ones((l, rest), F32), jnp.zeros((l, rest), F32)
    z8 = jnp.zeros((l, ROPE_HALF), F32)
    cos_f = jnp.concatenate([cos, cos, ones], axis=1)
    sin_a = jnp.concatenate([-sin, z8, zeros], axis=1)
    sin_b = jnp.concatenate([z8, sin, zeros], axis=1)
    reps = LANES // ATT_HEAD_DIM
    return tuple(jnp.tile(t, (1, reps)) for t in (cos_f, sin_a, sin_b))


def _rope_fwd(proj, tables, name):
    l = proj.shape[0]
    tl = _pick(l, (256, 128))

    def body(p_ref, c_ref, sa_ref, sb_ref, o_ref):
        cos_f, sin_a, sin_b = c_ref[...], sa_ref[...], sb_ref[...]
        for k in range(ATT_QKV // LANES):
            sl = slice(k * LANES, (k + 1) * LANES)
            t = p_ref[:, sl]
            if k < ROPE_Q_CHUNKS + ROPE_K_CHUNKS:
                t = (t * cos_f + pltpu.roll(t, LANES - ROPE_HALF, 1) * sin_a + pltpu.roll(t, ROPE_HALF, 1) * sin_b)
            if k < ROPE_Q_CHUNKS:
                t = t * Q_SCALE
            o_ref[:, sl] = t.astype(o_ref.dtype)

    tab = pl.BlockSpec((tl, LANES), lambda i: (i, 0))
    return pl.pallas_call(
        body, grid=(l // tl,), in_specs=[pl.BlockSpec((tl, ATT_IN_DIM), lambda i: (i, 0)), tab, tab, tab],
        out_specs=pl.BlockSpec((tl, ATT_QKV), lambda i: (i, 0)),
        out_shape=jax.ShapeDtypeStruct((l, ATT_QKV), BF16), compiler_params=_params("parallel"), name=name,
    )(proj, *tables)


def _rope_bwd(dq, dk, dv, dgate, tables, name):
    l = dq.shape[0]
    tl = _pick(l, (256, 128))

    def body(dq_ref, dk_ref, dv_ref, dg_ref, c_ref, sa_ref, sb_ref, o_ref):
        cos_f, sin_a, sin_b = c_ref[...], sa_ref[...], sb_ref[...]

        def unrope(t):
            return t * cos_f + pltpu.roll(t * sin_a, ROPE_HALF, 1) + pltpu.roll(t * sin_b, LANES - ROPE_HALF, 1)

        for k in range(ROPE_Q_CHUNKS):
            sl = slice(k * LANES, (k + 1) * LANES)
            o_ref[:, sl] = unrope(dq_ref[:, sl] * Q_SCALE).astype(o_ref.dtype)
        for k in range(ROPE_K_CHUNKS):
            sl = slice(k * LANES, (k + 1) * LANES)
            o_ref[:, ATT_WIDTH + k * LANES:ATT_WIDTH + (k + 1) * LANES] = unrope(dk_ref[:, sl]).astype(o_ref.dtype)
        o_ref[:, ATT_WIDTH + ATT_KV_WIDTH:ATT_QKV] = dv_ref[...].astype(o_ref.dtype)
        o_ref[:, ATT_QKV:ATT_IN_DIM] = dg_ref[...].astype(o_ref.dtype)

    tab = pl.BlockSpec((tl, LANES), lambda i: (i, 0))
    wide = pl.BlockSpec((tl, ATT_WIDTH), lambda i: (i, 0))
    kv = pl.BlockSpec((tl, ATT_KV_WIDTH), lambda i: (i, 0))
    return pl.pallas_call(
        body, grid=(l // tl,), in_specs=[wide, kv, kv, wide, tab, tab, tab],
        out_specs=pl.BlockSpec((tl, ATT_IN_DIM), lambda i: (i, 0)),
        out_shape=jax.ShapeDtypeStruct((l, ATT_IN_DIM), BF16), compiler_params=_params("parallel"), name=name,
    )(dq, dk, dv, dgate, *tables)


K_COL_BLOCK = ATT_WIDTH // ATT_KV_WIDTH
V_COL_BLOCK = K_COL_BLOCK + 1
GATE_HALF = ATT_WIDTH // 2
GATE_COL_BLOCK = ATT_QKV // GATE_HALF


def _band_masks():
    ri = lax.broadcasted_iota(jnp.int32, (ATT_BLOCK, ATT_BLOCK), 0)
    cj = lax.broadcasted_iota(jnp.int32, (ATT_BLOCK, ATT_BLOCK), 1)
    return cj > ri, cj <= ri


def _attn_fwd(qkv, proj, sinks, name):
    l = qkv.shape[0]
    nb = l // ATT_BLOCK

    def body(sink_ref, q_ref, kp_ref, kc_ref, vp_ref, vc_ref, g0_ref, g1_ref, og_ref, o_ref, lse_ref):
        n = pl.program_id(0)
        mask_p, mask_c = _band_masks()
        mask_p = mask_p & (n > 0)
        lane = lax.broadcasted_iota(jnp.int32, (1, ATT_Q_HEADS), 1)
        lse_acc = jnp.zeros((ATT_BLOCK, ATT_Q_HEADS), F32)
        for h in range(ATT_Q_HEADS):
            kv = slice((h // ATT_GQA) * ATT_HEAD_DIM, (h // ATT_GQA + 1) * ATT_HEAD_DIM)
            sl = slice(h * ATT_HEAD_DIM, (h + 1) * ATT_HEAD_DIM)
            qh = q_ref[:, sl]
            sp = jnp.where(mask_p, lax.dot_general(qh, kp_ref[:, kv], NT_DIMS, preferred_element_type=F32), NEG_INF)
            sc = jnp.where(mask_c, lax.dot_general(qh, kc_ref[:, kv], NT_DIMS, preferred_element_type=F32), NEG_INF)
            sink = sink_ref[h]
            m = jnp.maximum(jnp.maximum(jnp.max(sp, axis=1, keepdims=True), jnp.max(sc, axis=1, keepdims=True)), sink)
            pp = jnp.exp(sp - m)
            pc = jnp.exp(sc - m)
            den = jnp.sum(pp, axis=1, keepdims=True) + jnp.sum(pc, axis=1, keepdims=True) + jnp.exp(sink - m)
            oh = (jnp.dot(pp.astype(BF16), vp_ref[:, kv], preferred_element_type=F32)
                  + jnp.dot(pc.astype(BF16), vc_ref[:, kv], preferred_element_type=F32)) / den
            o_ref[:, sl] = oh
            lse_acc = lse_acc + jnp.where(lane == h, m + jnp.log(den), 0.0)
        lse_ref[...] = lse_acc
        for half, g_ref in enumerate((g0_ref, g1_ref)):
            sl = slice(half * GATE_HALF, (half + 1) * GATE_HALF)
            gate = g_ref[...]
            og_ref[:, sl] = (o_ref[:, sl] * (gate * _sigmoid(gate))).astype(og_ref.dtype)

    def prev(n):
        return jnp.maximum(n - 1, 0)

    wide = pl.BlockSpec((ATT_BLOCK, ATT_WIDTH), lambda n: (n, 0))
    return pl.pallas_call(
        body, grid=(nb,),
        in_specs=[pl.BlockSpec(memory_space=pltpu.SMEM), wide,
                  pl.BlockSpec((ATT_BLOCK, ATT_KV_WIDTH), lambda n: (prev(n), K_COL_BLOCK)),
                  pl.BlockSpec((ATT_BLOCK, ATT_KV_WIDTH), lambda n: (n, K_COL_BLOCK)),
                  pl.BlockSpec((ATT_BLOCK, ATT_KV_WIDTH), lambda n: (prev(n), V_COL_BLOCK)),
                  pl.BlockSpec((ATT_BLOCK, ATT_KV_WIDTH), lambda n: (n, V_COL_BLOCK)),
                  pl.BlockSpec((ATT_BLOCK, GATE_HALF), lambda n: (n, GATE_COL_BLOCK)),
                  pl.BlockSpec((ATT_BLOCK, GATE_HALF), lambda n: (n, GATE_COL_BLOCK + 1))],
        out_specs=[wide, wide, pl.BlockSpec((ATT_BLOCK, ATT_Q_HEADS), lambda n: (n, 0))],
        out_shape=[jax.ShapeDtypeStruct((l, ATT_WIDTH), BF16), jax.ShapeDtypeStruct((l, ATT_WIDTH), F32),
                   jax.ShapeDtypeStruct((l, ATT_Q_HEADS), F32)],
        compiler_params=_params("parallel"), name=name,
    )(sinks, qkv, qkv, qkv, qkv, qkv, proj, proj)


def _attn_bwd(qkv, proj, sinks, o, lse, dog, name):
    l = qkv.shape[0]
    nb = l // ATT_BLOCK

    def body(sink_ref, q_ref, kp_ref, kc_ref, vp_ref, vc_ref, g0_ref, g1_ref, o_ref, lse_ref, dog_ref,
             dq_ref, dk_ref, dv_ref, dg_ref, ds_ref, ck_ref, cv_ref, do_ref):
        n = pl.program_id(0)

        @pl.when(n == 0)
        def _():
            ds_ref[...] = jnp.zeros_like(ds_ref)
            ck_ref[...] = jnp.zeros_like(ck_ref)
            cv_ref[...] = jnp.zeros_like(cv_ref)

        @pl.when(n == nb)
        def _():
            dk_ref[...] = ck_ref[...]
            dv_ref[...] = cv_ref[...]

        @pl.when(n < nb)
        def _():
            mask_p, mask_c = _band_masks()
            mask_p = mask_p & (n > 0)
            lane = lax.broadcasted_iota(jnp.int32, (1, ATT_Q_HEADS), 1)
            for half, g_ref in enumerate((g0_ref, g1_ref)):
                sl = slice(half * GATE_HALF, (half + 1) * GATE_HALF)
                gate = g_ref[...]
                s = _sigmoid(gate)
                dogv = dog_ref[:, sl]
                do_ref[:, sl] = dogv * (gate * s)
                dg_ref[:, sl] = dogv * o_ref[:, sl] * (s * (1.0 + gate * (1.0 - s)))
            lse_v = lse_ref[...]
            ds_acc = jnp.zeros((1, ATT_Q_HEADS), F32)
            for kvh in range(ATT_KV_HEADS):
                kv = slice(kvh * ATT_HEAD_DIM, (kvh + 1) * ATT_HEAD_DIM)
                kp, kc, vp, vc = kp_ref[:, kv], kc_ref[:, kv], vp_ref[:, kv], vc_ref[:, kv]
                dkp = jnp.zeros((ATT_BLOCK, ATT_HEAD_DIM), F32)
                dkc, dvp, dvc = dkp, dkp, dkp
                for gq in range(ATT_GQA):
                    h = kvh * ATT_GQA + gq
                    sl = slice(h * ATT_HEAD_DIM, (h + 1) * ATT_HEAD_DIM)
                    qh = q_ref[:, sl]
                    doh = do_ref[:, sl]
                    do16 = doh.astype(BF16)
                    lse_h = lse_v[:, h:h + 1]
                    pp = jnp.exp(jnp.where(mask_p, lax.dot_general(qh, kp, NT_DIMS, preferred_element_type=F32) - lse_h, NEG_INF))
                    pc = jnp.exp(jnp.where(mask_c, lax.dot_general(qh, kc, NT_DIMS, preferred_element_type=F32) - lse_h, NEG_INF))
                    delta = jnp.sum(doh * o_ref[:, sl], axis=1, keepdims=True)
                    dsp = (pp * (lax.dot_general(do16, vp, NT_DIMS, preferred_element_type=F32) - delta)).astype(BF16)
                    dsc = (pc * (lax.dot_general(do16, vc, NT_DIMS, preferred_element_type=F32) - delta)).astype(BF16)
                    dq_ref[:, sl] = (jnp.dot(dsp, kp, preferred_element_type=F32)
                                     + jnp.dot(dsc, kc, preferred_element_type=F32))
                    dkp = dkp + lax.dot_general(dsp, qh, TN_DIMS, preferred_element_type=F32)
                    dkc = dkc + lax.dot_general(dsc, qh, TN_DIMS, preferred_element_type=F32)
                    dvp = dvp + lax.dot_general(pp.astype(BF16), do16, TN_DIMS, preferred_element_type=F32)
                    dvc = dvc + lax.dot_general(pc.astype(BF16), do16, TN_DIMS, preferred_element_type=F32)
                    dsink = -jnp.sum(jnp.exp(sink_ref[h] - lse_h) * delta)
                    ds_acc = ds_acc + jnp.where(lane == h, dsink, 0.0)
                dk_ref[:, kv] = ck_ref[:, kv] + dkp
                dv_ref[:, kv] = cv_ref[:, kv] + dvp
                ck_ref[:, kv] = dkc
                cv_ref[:, kv] = dvc
            ds_ref[...] += ds_acc

    def cur(n):
        return jnp.minimum(n, nb - 1)

    def prev(n):
        return jnp.maximum(n - 1, 0)

    wide = pl.BlockSpec((ATT_BLOCK, ATT_WIDTH), lambda n: (cur(n), 0))
    kvo = pl.BlockSpec((ATT_BLOCK, ATT_KV_WIDTH), lambda n: (prev(n), 0))
    return pl.pallas_call(
        body, grid=(nb + 1,),
        in_specs=[pl.BlockSpec(memory_space=pltpu.SMEM), wide,
                  pl.BlockSpec((ATT_BLOCK, ATT_KV_WIDTH), lambda n: (prev(cur(n)), K_COL_BLOCK)),
                  pl.BlockSpec((ATT_BLOCK, ATT_KV_WIDTH), lambda n: (cur(n), K_COL_BLOCK)),
                  pl.BlockSpec((ATT_BLOCK, ATT_KV_WIDTH), lambda n: (prev(cur(n)), V_COL_BLOCK)),
                  pl.BlockSpec((ATT_BLOCK, ATT_KV_WIDTH), lambda n: (cur(n), V_COL_BLOCK)),
                  pl.BlockSpec((ATT_BLOCK, GATE_HALF), lambda n: (cur(n), GATE_COL_BLOCK)),
                  pl.BlockSpec((ATT_BLOCK, GATE_HALF), lambda n: (cur(n), GATE_COL_BLOCK + 1)),
                  wide, pl.BlockSpec((ATT_BLOCK, ATT_Q_HEADS), lambda n: (cur(n), 0)), wide],
        out_specs=[wide, kvo, kvo, wide, pl.BlockSpec((1, ATT_Q_HEADS), lambda n: (0, 0))],
        out_shape=[jax.ShapeDtypeStruct((l, ATT_WIDTH), F32), jax.ShapeDtypeStruct((l, ATT_KV_WIDTH), F32),
                   jax.ShapeDtypeStruct((l, ATT_KV_WIDTH), F32), jax.ShapeDtypeStruct((l, ATT_WIDTH), F32),
                   jax.ShapeDtypeStruct((1, ATT_Q_HEADS), F32)],
        scratch_shapes=[pltpu.VMEM((ATT_BLOCK, ATT_KV_WIDTH), F32), pltpu.VMEM((ATT_BLOCK, ATT_KV_WIDTH), F32),
                        pltpu.VMEM((ATT_BLOCK, ATT_WIDTH), F32)],
        compiler_params=_params("arbitrary"), name=name,
    )(sinks, qkv, qkv, qkv, qkv, qkv, proj, proj, o, lse, dog)


def _local_step(x, positions, pre_norm, post_norm, w_ssm_in, conv_w, conv_b, dt_bias, a_log, d_skip, gate_norm,
                w_ssm_out, w_att_in, sinks, w_att_out, target):
    tables = _rope_tables(positions)
    dt_bias_pad = jnp.pad(dt_bias, ((0, 0), (0, SSM_DT_PAD - SSM_HEADS)))
    d_lanes = jnp.repeat(d_skip, SSM_HEAD_DIM, axis=1).reshape(-1, SSM_GROUPS, 1, GP)
    alog_lanes = jnp.repeat(a_log, SSM_HEAD_DIM, axis=1)
    saved = []
    cur = x
    for i in range(DEPTH):
        j = i // 2
        h = _rmsnorm_fwd(cur, pre_norm[i], f"prenorm_fwd_{i}")
        if i % 2 == 0:
            proj = _matmul(h, w_ssm_in[j], "nn", F32, f"ssm_in_{i}")
            pre, xbc = _conv_fwd(proj, conv_w[j], conv_b[j], f"conv_fwd_{i}")
            dtb, acsb, dtr, acs_r = _ssd_prep(proj, dt_bias_pad[j:j + 1], alog_lanes[j:j + 1], f"ssd_prep_{i}")
            y, hin = _ssd_fwd(xbc, dtb, acsb, acs_r, d_lanes[j], f"ssd_fwd_{i}")
            act = _gatenorm_fwd(y, proj, gate_norm[j], f"gatenorm_fwd_{i}")
            ymix = _matmul(act, w_ssm_out[j], "nn", F32, f"ssm_out_{i}")
            saved.append(dict(x=cur, h=h, proj=proj, pre=pre, xbc=xbc, dtb=dtb, acsb=acsb, dtr=dtr, acs_r=acs_r, y=y,
                              hin=hin, act=act, ymix=ymix))
        else:
            proj = _matmul(h, w_att_in[j], "nn", F32, f"att_in_{i}")
            qkv = _rope_fwd(proj, tables, f"rope_fwd_{i}")
            act, o, lse = _attn_fwd(qkv, proj, sinks[j], f"attn_fwd_{i}")
            ymix = _matmul(act, w_att_out[j], "nn", F32, f"att_out_{i}")
            saved.append(dict(x=cur, h=h, proj=proj, qkv=qkv, o=o, lse=lse, act=act, ymix=ymix))
        cur = _post_fwd(cur, ymix, post_norm[i], f"post_fwd_{i}")

    g, loss_lanes = _loss_grad(cur, target, "loss")

    gr = {k: [None] * 2 for k in ("ssm_w_in", "ssm_conv_w", "ssm_conv_b", "ssm_dt_bias", "ssm_a_log", "ssm_d",
                                  "ssm_gate_norm", "ssm_w_out", "att_w_in", "att_sinks", "att_w_out")}
    gr["pre_norm"] = [None] * DEPTH
    gr["post_norm"] = [None] * DEPTH
    for i in reversed(range(DEPTH)):
        j = i // 2
        s = saved[i]
        dymix, gr["post_norm"][i] = _rmsnorm_bwd(g, s["ymix"], post_norm[i], None, BF16, f"post_bwd_{i}")
        if i % 2 == 0:
            dact = _matmul(dymix, w_ssm_out[j], "nt", F32, f"ssm_out_dx_{i}")
            gr["ssm_w_out"][j] = _matmul(s["act"], dymix, "tn", F32, f"ssm_out_dw_{i}")
            dy, dz, gr["ssm_gate_norm"][j] = _gatenorm_bwd(dact, s["y"], s["proj"], gate_norm[j], f"gatenorm_bwd_{i}")
            dxs, db, dc, ddt8, dal, dd = _ssd_bwd(s["xbc"], s["dtb"], s["acsb"], s["dtr"], s["acs_r"], a_log[j],
                                                  d_lanes[j], s["hin"], dy, f"ssd_bwd_{i}")
            gr["ssm_a_log"][j] = dal.reshape(SSM_HEADS)
            gr["ssm_d"][j] = dd.reshape(SSM_HEADS)
            l = x.shape[0]
            ddt = jnp.pad(jnp.transpose(ddt8, (2, 0, 1)).reshape(l, SSM_HEADS), ((0, 0), (0, SSM_DT_PAD - SSM_HEADS)))
            ddt_raw, dbias = _dt_bwd(ddt, s["proj"], dt_bias_pad[j:j + 1], f"dt_bwd_{i}")
            gr["ssm_dt_bias"][j] = dbias[0, :SSM_HEADS]
            pieces, dcw, dcb = [], [], []
            for c0, dpiece, tag in ((0, dxs, "x"), (SSM_D_INNER, db, "b"), (SSM_D_INNER + SSM_BC_DIM, dc, "c")):
                du, dw_, db_ = _conv_bwd(dpiece, s["pre"], s["proj"], conv_w[j], c0, f"conv_bwd_{tag}_{i}")
                pieces.append(du)
                dcw.append(dw_)
                dcb.append(db_)
            gr["ssm_conv_w"][j] = jnp.concatenate(dcw, axis=1)
            gr["ssm_conv_b"][j] = jnp.concatenate(dcb, axis=1)[0]
            dproj = jnp.concatenate([dz] + pieces + [ddt_raw], axis=1)
            w_in, key = w_ssm_in[j], "ssm_w_in"
        else:
            dog = _matmul(dymix, w_att_out[j], "nt", F32, f"att_out_dx_{i}")
            gr["att_w_out"][j] = _matmul(s["act"], dymix, "tn", F32, f"att_out_dw_{i}")
            dq, dk, dv, dgate, dsk = _attn_bwd(s["qkv"], s["proj"], sinks[j], s["o"], s["lse"], dog, f"attn_bwd_{i}")
            gr["att_sinks"][j] = dsk[0]
            dproj = _rope_bwd(dq, dk, dv, dgate, tables, f"rope_bwd_{i}")
            w_in, key = w_att_in[j], "att_w_in"
        dh = _matmul(dproj, w_in, "nt", F32, f"in_dx_{i}")
        gr[key][j] = _matmul(s["h"], dproj, "tn", F32, f"in_dw_{i}")
        g, gr["pre_norm"][i] = _rmsnorm_bwd(dh, s["x"], pre_norm[i], g, F32, f"prenorm_bwd_{i}")
    grads = {k: jnp.stack([v.reshape(v.shape[-1]) if k in ("pre_norm", "post_norm", "ssm_gate_norm") else v for v in vs])
             for k, vs in gr.items()}
    return loss_lanes, g, grads


N_CHIPS = 4
N_DEV = 8
MESH = pl.DeviceIdType.MESH
ANY = pl.BlockSpec(memory_space=pl.ANY)


def _place():
    x, y, c = lax.axis_index("x"), lax.axis_index("y"), lax.axis_index("c")
    return x, y, c, 2 * x + y


def _chip_gather(shards, name):
    n = len(shards)

    def body(*refs):
        ins, outs = refs[:n], refs[n:2 * n]
        send_sems, recv_sems, pass_send_sems, pass_recv_sems, local_sems = refs[2 * n:]
        x, y, c, s = _place()
        local = [pltpu.make_async_copy(ins[w], outs[w].at[s], local_sems.at[w]) for w in range(n)]
        for cp in local:
            cp.start()

        def remote(w, t):
            return pltpu.make_async_remote_copy(
                src_ref=ins[w].at[c], dst_ref=outs[w].at[s, c], send_sem=send_sems.at[w, t],
                recv_sem=recv_sems.at[w, s], device_id=(t // 2, t % 2, c), device_id_type=MESH)

        def arrival(w, t):
            return pltpu.make_async_remote_copy(
                src_ref=ins[w].at[c], dst_ref=outs[w].at[t, c], send_sem=send_sems.at[w, t],
                recv_sem=recv_sems.at[w, t], device_id=(t // 2, t % 2, c), device_id_type=MESH)

        def handed_on(w, t):
            return pltpu.make_async_remote_copy(
                src_ref=outs[w].at[t, c], dst_ref=outs[w].at[t, c], send_sem=pass_send_sems.at[w, t],
                recv_sem=pass_recv_sems.at[w, t], device_id=(x, y, 1 - c), device_id_type=MESH)

        def handed_in(w, t):
            return pltpu.make_async_remote_copy(
                src_ref=outs[w].at[t, 1 - c], dst_ref=outs[w].at[t, 1 - c], send_sem=pass_send_sems.at[w, t],
                recv_sem=pass_recv_sems.at[w, t], device_id=(x, y, 1 - c), device_id_type=MESH)

        for t in range(N_CHIPS):
            @pl.when(s != t)
            def _():
                for w in range(n):
                    remote(w, t).start()
        for t in range(N_CHIPS):
            @pl.when(s != t)
            def _():
                for w in range(n):
                    arrival(w, t).wait_recv()
                    handed_on(w, t).start()
        for t in range(N_CHIPS):
            @pl.when(s != t)
            def _():
                for w in range(n):
                    remote(w, t).wait_send()
                    handed_on(w, t).wait_send()
                    handed_in(w, t).wait_recv()
        for cp in local:
            cp.wait()

    return pl.pallas_call(
        body, in_specs=[ANY] * n, out_specs=[ANY] * n,
        out_shape=[jax.ShapeDtypeStruct((N_CHIPS,) + a.shape, a.dtype) for a in shards],
        scratch_shapes=[pltpu.SemaphoreType.DMA((n, N_CHIPS)), pltpu.SemaphoreType.DMA((n, N_CHIPS)),
                        pltpu.SemaphoreType.DMA((n, N_CHIPS)), pltpu.SemaphoreType.DMA((n, N_CHIPS)),
                        pltpu.SemaphoreType.DMA((n,))],
        name=name,
    )(*shards)


def _pair_swap(parts, name):
    n = len(parts)

    def body(*refs):
        ins, outs = refs[:n], refs[n:2 * n]
        send_sems, recv_sems = refs[2 * n:]
        x, y, c, _ = _place()
        cps = [pltpu.make_async_remote_copy(
            src_ref=ins[w].at[1 - c], dst_ref=outs[w], send_sem=send_sems.at[w], recv_sem=recv_sems.at[w],
            device_id=(x, y, 1 - c), device_id_type=MESH) for w in range(n)]
        for cp in cps:
            cp.start()
        for cp in cps:
            cp.wait()

    return pl.pallas_call(
        body, in_specs=[ANY] * n, out_specs=[ANY] * n,
        out_shape=[jax.ShapeDtypeStruct(a.shape[1:], a.dtype) for a in parts],
        scratch_shapes=[pltpu.SemaphoreType.DMA((n,)), pltpu.SemaphoreType.DMA((n,))],
        name=name,
    )(*parts)


def _chip_scatter(parts, name):
    n = len(parts)
    rows = [a.shape[0] // N_CHIPS for a in parts]

    def body(*refs):
        ins, outs = refs[:n], refs[n:2 * n]
        send_sems, recv_sems, local_sems = refs[2 * n:]
        _, _, c, s = _place()

        def block(w, t):
            return ins[w].at[pl.ds(t * rows[w], rows[w])]

        local = [pltpu.make_async_copy(block(w, s), outs[w].at[s], local_sems.at[w]) for w in range(n)]
        for cp in local:
            cp.start()

        def remote(w, t):
            return pltpu.make_async_remote_copy(
                src_ref=block(w, t), dst_ref=outs[w].at[s], send_sem=send_sems.at[w, t], recv_sem=recv_sems.at[w, s],
                device_id=(t // 2, t % 2, c), device_id_type=MESH)

        def arrival(w, t):
            return pltpu.make_async_remote_copy(
                src_ref=block(w, t), dst_ref=outs[w].at[t], send_sem=send_sems.at[w, t], recv_sem=recv_sems.at[w, t],
                device_id=(t // 2, t % 2, c), device_id_type=MESH)

        for t in range(N_CHIPS):
            @pl.when(s != t)
            def _():
                for w in range(n):
                    remote(w, t).start()
        for t in range(N_CHIPS):
            @pl.when(s != t)
            def _():
                for w in range(n):
                    remote(w, t).wait_send()
                    arrival(w, t).wait_recv()
        for cp in local:
            cp.wait()

    return pl.pallas_call(
        body, in_specs=[ANY] * n, out_specs=[ANY] * n,
        out_shape=[jax.ShapeDtypeStruct((N_CHIPS, r, a.shape[1]), a.dtype) for a, r in zip(parts, rows)],
        scratch_shapes=[pltpu.SemaphoreType.DMA((n, N_CHIPS)), pltpu.SemaphoreType.DMA((n, N_CHIPS)),
                        pltpu.SemaphoreType.DMA((n,))],
        name=name,
    )(*parts)


def _pair_merge(parts, name):
    n = len(parts)

    def body(*refs):
        ins, outs = refs[:n], refs[n:2 * n]
        send_sems, recv_sems = refs[2 * n:]
        x, y, c, _ = _place()
        cps = [pltpu.make_async_remote_copy(
            src_ref=ins[w], dst_ref=outs[w], send_sem=send_sems.at[w], recv_sem=recv_sems.at[w],
            device_id=(x, y, 1 - c), device_id_type=MESH) for w in range(n)]
        for cp in cps:
            cp.start()
        for cp in cps:
            cp.wait()

    return pl.pallas_call(
        body, in_specs=[ANY] * n, out_specs=[ANY] * n,
        out_shape=[jax.ShapeDtypeStruct(a.shape, a.dtype) for a in parts],
        scratch_shapes=[pltpu.SemaphoreType.DMA((n,)), pltpu.SemaphoreType.DMA((n,))],
        name=name,
    )(*parts)


def _all_gather_small(a, name):
    def body(in_ref, out_ref, send_sems, recv_sems, local_sem):
        x, y, c, _ = _place()
        me = 4 * x + 2 * y + c
        local = pltpu.make_async_copy(in_ref, out_ref.at[me], local_sem)
        local.start()

        def remote(d):
            return pltpu.make_async_remote_copy(
                src_ref=in_ref, dst_ref=out_ref.at[me], send_sem=send_sems.at[d], recv_sem=recv_sems.at[me],
                device_id=(d // 4, (d // 2) % 2, d % 2), device_id_type=MESH)

        def arrival(d):
            return pltpu.make_async_remote_copy(
                src_ref=in_ref, dst_ref=out_ref.at[d], send_sem=send_sems.at[d], recv_sem=recv_sems.at[d],
                device_id=(d // 4, (d // 2) % 2, d % 2), device_id_type=MESH)

        for d in range(N_DEV):
            @pl.when(me != d)
            def _():
                remote(d).start()
        for d in range(N_DEV):
            @pl.when(me != d)
            def _():
                remote(d).wait_send()
                arrival(d).wait_recv()
        local.wait()

    return pl.pallas_call(
        body, in_specs=[ANY], out_specs=ANY, out_shape=jax.ShapeDtypeStruct((N_DEV,) + a.shape, a.dtype),
        scratch_shapes=[pltpu.SemaphoreType.DMA((N_DEV,)), pltpu.SemaphoreType.DMA((N_DEV,)), pltpu.SemaphoreType.DMA],
        name=name,
    )(a)


def _reduce_tile(rows):
    return _pick(rows, (256, 16))


def _pair_add(full, other, layer, name):
    _, rows, cols = full.shape
    tr = _reduce_tile(rows)

    def body(layer_ref, a_ref, b_ref, o_ref):
        o_ref[...] = (a_ref[0] + b_ref[...]).astype(o_ref.dtype)

    return pl.pallas_call(
        body,
        grid_spec=pltpu.PrefetchScalarGridSpec(
            num_scalar_prefetch=1, grid=(rows // tr,),
            in_specs=[pl.BlockSpec((1, tr, cols), lambda i, lr: (lr[0], i, 0)), pl.BlockSpec((tr, cols), lambda i, lr: (i, 0))],
            out_specs=pl.BlockSpec((tr, cols), lambda i, lr: (i, 0))),
        out_shape=jax.ShapeDtypeStruct((rows, cols), BF16), compiler_params=_params("parallel"), name=name,
    )(layer, full, other)


def _sum_slots(a, name):
    n, rows, cols = a.shape
    tr = _reduce_tile(rows)

    def body(a_ref, o_ref):
        acc = a_ref[0].astype(F32)
        for k in range(1, n):
            acc = acc + a_ref[k].astype(F32)
        o_ref[...] = acc

    return pl.pallas_call(
        body, grid=(rows // tr,), in_specs=[pl.BlockSpec((n, tr, cols), lambda i: (0, i, 0))],
        out_specs=pl.BlockSpec((tr, cols), lambda i: (i, 0)),
        out_shape=jax.ShapeDtypeStruct((rows, cols), F32), compiler_params=_params("parallel"), name=name,
    )(a)


def _adamw(w, g, m, v, name):
    rows, cols = w.shape
    tr = _pick(rows, (256, 8))

    def body(w_ref, g_ref, m_ref, v_ref, d_ref, nm_ref, nv_ref):
        gv = g_ref[...]
        mn = ADAM_B1 * m_ref[...] + (1.0 - ADAM_B1) * gv
        vn = ADAM_B2 * v_ref[...] + (1.0 - ADAM_B2) * jnp.square(gv)
        m_hat = mn / (1.0 - ADAM_B1 ** ADAM_STEP)
        v_hat = vn / (1.0 - ADAM_B2 ** ADAM_STEP)
        d_ref[...] = -ADAM_LR * (m_hat / (jnp.sqrt(v_hat) + ADAM_EPS) + ADAM_WD * w_ref[...])
        nm_ref[...] = mn
        nv_ref[...] = vn

    blk = pl.BlockSpec((tr, cols), lambda i: (i, 0))
    return pl.pallas_call(
        body, grid=(rows // tr,), in_specs=[blk] * 4, out_specs=[blk] * 3,
        out_shape=[jax.ShapeDtypeStruct((rows, cols), F32)] * 3, compiler_params=_params("parallel"), name=name,
    )(w, g, m, v)


BIG = ("ssm_w_in", "ssm_w_out", "att_w_in", "att_w_out")
SHARDED = BIG + ("ssm_conv_w",)
SMALL = ("pre_norm", "post_norm", "ssm_conv_b", "ssm_dt_bias", "ssm_a_log", "ssm_d", "ssm_gate_norm", "att_sinks")
WEIGHTS = ("pre_norm", "post_norm", "ssm_w_in", "ssm_conv_w", "ssm_conv_b", "ssm_dt_bias", "ssm_a_log", "ssm_d",
           "ssm_gate_norm", "ssm_w_out", "att_w_in", "att_sinks", "att_w_out")


def _cols_to_whole(g):
    _, two, rows, cols = g.shape
    return jnp.transpose(g, (1, 2, 0, 3)).reshape(two, rows, N_CHIPS * cols)


def _rows_to_whole(g):
    _, two, rows, cols = g.shape
    return jnp.transpose(g, (1, 0, 2, 3)).reshape(two, N_CHIPS * rows, cols)


def _cols_by_chip(g):
    two, rows, cols = g.shape
    return jnp.transpose(g.reshape(two, rows, N_CHIPS, cols // N_CHIPS), (0, 2, 1, 3)).reshape(two, N_CHIPS * rows, cols // N_CHIPS)


def _pack_small(tree, keys):
    flat = jnp.concatenate([tree[k].reshape(-1) for k in keys])
    rows = -(-flat.shape[0] // (8 * LANES)) * 8
    return jnp.pad(flat, (0, rows * LANES - flat.shape[0])).reshape(rows, LANES)


def _unpack_small(packed, shapes, keys):
    flat = packed.reshape(-1)
    out, at = {}, 0
    for k in keys:
        n = 1
        for dim in shapes[k]:
            n *= dim
        out[k] = flat[at:at + n].reshape(shapes[k])
        at += n
    return out


def kernel(x, positions, pre_norm, post_norm, ssm_w_in, ssm_conv_w, ssm_conv_b, ssm_dt_bias, ssm_a_log, ssm_d, ssm_gate_norm, ssm_w_out, att_w_in, att_sinks, att_w_out, loss_target, m_pre_norm, m_post_norm, m_ssm_w_in, m_ssm_conv_w, m_ssm_conv_b, m_ssm_dt_bias, m_ssm_a_log, m_ssm_d, m_ssm_gate_norm, m_ssm_w_out, m_att_w_in, m_att_sinks, m_att_w_out, v_pre_norm, v_post_norm, v_ssm_w_in, v_ssm_conv_w, v_ssm_conv_b, v_ssm_dt_bias, v_ssm_a_log, v_ssm_d, v_ssm_gate_norm, v_ssm_w_out, v_att_w_in, v_att_sinks, v_att_w_out):
    w = dict(pre_norm=pre_norm, post_norm=post_norm, ssm_w_in=ssm_w_in, ssm_conv_w=ssm_conv_w, ssm_conv_b=ssm_conv_b,
             ssm_dt_bias=ssm_dt_bias, ssm_a_log=ssm_a_log, ssm_d=ssm_d, ssm_gate_norm=ssm_gate_norm, ssm_w_out=ssm_w_out,
             att_w_in=att_w_in, att_sinks=att_sinks, att_w_out=att_w_out)
    m = dict(pre_norm=m_pre_norm, post_norm=m_post_norm, ssm_w_in=m_ssm_w_in, ssm_conv_w=m_ssm_conv_w, ssm_conv_b=m_ssm_conv_b,
             ssm_dt_bias=m_ssm_dt_bias, ssm_a_log=m_ssm_a_log, ssm_d=m_ssm_d, ssm_gate_norm=m_ssm_gate_norm,
             ssm_w_out=m_ssm_w_out, att_w_in=m_att_w_in, att_sinks=m_att_sinks, att_w_out=m_att_w_out)
    v = dict(pre_norm=v_pre_norm, post_norm=v_post_norm, ssm_w_in=v_ssm_w_in, ssm_conv_w=v_ssm_conv_w, ssm_conv_b=v_ssm_conv_b,
             ssm_dt_bias=v_ssm_dt_bias, ssm_a_log=v_ssm_a_log, ssm_d=v_ssm_d, ssm_gate_norm=v_ssm_gate_norm,
             ssm_w_out=v_ssm_w_out, att_w_in=v_att_w_in, att_sinks=v_att_sinks, att_w_out=v_att_w_out)
    c = lax.axis_index("c")
    chip = 2 * lax.axis_index("x") + lax.axis_index("y")

    g_in, g_out, g_ain, g_aout, g_cw = _chip_gather(
        [ssm_w_in.astype(BF16), ssm_w_out.astype(BF16), att_w_in.astype(BF16), att_w_out.astype(BF16), ssm_conv_w],
        "gather_weights")
    w_in_full = jnp.pad(_cols_to_whole(g_in), ((0, 0), (0, 0), (0, SSM_IN_PAD - SSM_IN_DIM)))
    loss_lanes, grad_x, gr = _local_step(
        x[0], positions[0], pre_norm, post_norm, w_in_full, _cols_to_whole(g_cw), ssm_conv_b, ssm_dt_bias, ssm_a_log,
        ssm_d, ssm_gate_norm, _rows_to_whole(g_out), _cols_to_whole(g_ain), att_sinks, _rows_to_whole(g_aout),
        loss_target[0])
    loss = lax.psum(0.5 * jnp.sum(loss_lanes) / D_MODEL, ("x", "y", "c"))

    parts = [_cols_by_chip(gr["ssm_w_in"][:, :, :SSM_IN_DIM]), gr["ssm_w_out"], _cols_by_chip(gr["att_w_in"]),
             gr["att_w_out"]]
    from_sibling = _pair_swap(parts, "reduce_pair_swap")
    layer = jnp.reshape(c, (1,)).astype(jnp.int32)
    chip_sums = [_pair_add(p, o, layer, f"reduce_pair_add_{k}") for k, (p, o) in enumerate(zip(parts, from_sibling))]
    by_chip = _chip_scatter(chip_sums, "reduce_chip_scatter")
    mine = [_sum_slots(a, f"reduce_chip_sum_{k}") for k, a in enumerate(by_chip)]
    theirs = _pair_merge(mine, "reduce_pair_merge")
    grads = {k: jnp.stack([jnp.where(c == 0, a, b), jnp.where(c == 0, b, a)]).reshape(w[k].shape)
             for k, a, b in zip(BIG, mine, theirs)}

    small_keys = SMALL + ("ssm_conv_w",)
    small_shapes = {k: w[k].shape for k in SMALL}
    small_shapes["ssm_conv_w"] = gr["ssm_conv_w"].shape
    small_sum = _sum_slots(_all_gather_small(_pack_small(gr, small_keys), "reduce_small_gather"), "reduce_small_sum")
    grads.update(_unpack_small(small_sum, small_shapes, small_keys))
    conv_cols = ssm_conv_w.shape[2]
    grads["ssm_conv_w"] = lax.dynamic_slice_in_dim(grads["ssm_conv_w"], chip * conv_cols, conv_cols, axis=2)

    delta, new_m, new_v = {}, {}, {}
    for k in SHARDED:
        shp = w[k].shape
        two_d = (shp[0] * shp[1], shp[2])
        d_, m_, v_ = _adamw(w[k].reshape(two_d), grads[k].reshape(two_d), m[k].reshape(two_d), v[k].reshape(two_d),
                            f"adamw_{k}")
        delta[k], new_m[k], new_v[k] = d_.reshape(shp), m_.reshape(shp), v_.reshape(shp)
    d_, m_, v_ = _adamw(_pack_small(w, SMALL), _pack_small(grads, SMALL), _pack_small(m, SMALL), _pack_small(v, SMALL),
                        "adamw_small")
    delta.update(_unpack_small(d_, small_shapes, SMALL))
    new_m.update(_unpack_small(m_, small_shapes, SMALL))
    new_v.update(_unpack_small(v_, small_shapes, SMALL))

    return (loss, grad_x[None], *[grads[k] for k in WEIGHTS], *[delta[k] for k in WEIGHTS],
            *[new_m[k] for k in WEIGHTS], *[new_v[k] for k in WEIGHTS])
```

```python
import functools

import jax
import jax.numpy as jnp
from jax import lax
from jax.experimental import pallas as pl
from jax.experimental.pallas import tpu as pltpu

F32 = jnp.float32
BF16 = jnp.bfloat16
EPS = 1e-6
NEG_INF = float("-inf")

D_MODEL = 1024
DEPTH = 4
SSM_D_INNER = 2048
SSM_HEAD_DIM = 64
SSM_HEADS = 32
SSM_GROUPS = 8
SSM_HPG = 4
SSM_STATE = 128
SSM_CONV = 4
SSM_CHUNK = 128
SSM_BC_DIM = 1024
SSM_CONV_DIM = 4096
SSM_IN_DIM = 6176
SSM_IN_PAD = 6272
SSM_DT_PAD = 128
ATT_HEAD_DIM = 64
ATT_Q_HEADS = 16
ATT_KV_HEADS = 4
ATT_GQA = 4
ATT_WIDTH = 1024
ATT_KV_WIDTH = 256
ATT_IN_DIM = 2560
ATT_QKV = ATT_WIDTH + 2 * ATT_KV_WIDTH
ATT_BLOCK = 128
ROPE_THETA = 500000.0
ROPE_DIM = 16
ROPE_HALF = 8
Q_SCALE = ATT_HEAD_DIM ** -0.5

ADAM_LR = 0.001
ADAM_B1 = 0.9
ADAM_B2 = 0.999
ADAM_EPS = 1e-08
ADAM_WD = 0.01
ADAM_STEP = 10

VMEM_LIMIT_BYTES = 48 * 1024 * 1024
NT_DIMS = (((1,), (1,)), ((), ()))
TN_DIMS = (((0,), (0,)), ((), ()))


def _params(*sem):
    return pltpu.CompilerParams(dimension_semantics=sem, vmem_limit_bytes=VMEM_LIMIT_BYTES)


def _pick(n, cands):
    for c in cands:
        if n % c == 0:
            return c
    return n


def _sigmoid(v):
    return 1.0 / (1.0 + jnp.exp(-v))


def _bdot(a, b):
    return jnp.dot(a.astype(BF16), b.astype(BF16), preferred_element_type=F32)


def _bdot_nt(a, b):
    return lax.dot_general(a.astype(BF16), b.astype(BF16), NT_DIMS, preferred_element_type=F32)


def _bdot_tn(a, b):
    return lax.dot_general(a.astype(BF16), b.astype(BF16), TN_DIMS, preferred_element_type=F32)


MATMUL_VMEM_BUDGET = 36 * 1024 * 1024


def _matmul_tiles(m, n, k, out_bytes, reduce_rows):
    best = None
    whole = [k] if (not reduce_rows or k <= 2048) else []
    for tk in whole + [c for c in (4096, 2048, 1024, 896, 512) if k % c == 0 and c < k]:
        for tm in (c for c in (1024, 512, 256) if m % c == 0):
            for tn in (c for c in (n, 1280, 1024, 896, 640, 512) if n % c == 0):
                acc = tm * tn * 4 if tk < k else 0
                need = 2 * (2 * tk * (tm + tn) + tm * tn * out_bytes) + acc
                if need <= MATMUL_VMEM_BUDGET and (best is None or tm * tn * min(tk, 2048) > best[0]):
                    best = (tm * tn * min(tk, 2048), tm, tn, tk)
        if best is not None and not reduce_rows:
            break
    return best[1:]


def _matmul(a, b, mode, out_dtype, name):
    if mode == "nn":
        (m, k), n = a.shape, b.shape[1]
    elif mode == "nt":
        (m, k), n = a.shape, b.shape[0]
    else:
        (k, m), n = a.shape, b.shape[1]
    tm, tn, tk = _matmul_tiles(m, n, k, jnp.dtype(out_dtype).itemsize, mode == "tn")
    nk = k // tk
    dims = {"nn": (((1,), (0,)), ((), ())), "nt": NT_DIMS, "tn": TN_DIMS}[mode]

    def body(a_ref, b_ref, o_ref, acc_ref):
        kk = pl.program_id(2)
        part = lax.dot_general(a_ref[...], b_ref[...], dims, preferred_element_type=F32)
        if nk == 1:
            o_ref[...] = part.astype(o_ref.dtype)
        else:
            @pl.when(kk == 0)
            def _():
                acc_ref[...] = part

            @pl.when(kk > 0)
            def _():
                acc_ref[...] += part

            @pl.when(kk == nk - 1)
            def _():
                o_ref[...] = acc_ref[...].astype(o_ref.dtype)

    if mode == "nn":
        a_spec = pl.BlockSpec((tm, tk), lambda j, i, kk: (i, kk))
        b_spec = pl.BlockSpec((tk, tn), lambda j, i, kk: (kk, j))
    elif mode == "nt":
        a_spec = pl.BlockSpec((tm, tk), lambda j, i, kk: (i, kk))
        b_spec = pl.BlockSpec((tn, tk), lambda j, i, kk: (j, kk))
    else:
        a_spec = pl.BlockSpec((tk, tm), lambda j, i, kk: (kk, i))
        b_spec = pl.BlockSpec((tk, tn), lambda j, i, kk: (kk, j))
    return pl.pallas_call(
        body, grid=(n // tn, m // tm, nk), in_specs=[a_spec, b_spec],
        out_specs=pl.BlockSpec((tm, tn), lambda j, i, kk: (i, j)),
        out_shape=jax.ShapeDtypeStruct((m, n), out_dtype),
        scratch_shapes=[pltpu.VMEM((tm, tn), F32)],
        compiler_params=_params("parallel", "parallel", "arbitrary"), name=name,
    )(a, b)


def _row_tile(l):
    return _pick(l, (512, 256, 128))


def _rmsnorm_fwd(x, w, name):
    l, d = x.shape
    tl = _row_tile(l)

    def body(x_ref, w_ref, o_ref):
        xv = x_ref[...]
        r = lax.rsqrt(jnp.mean(xv * xv, axis=-1, keepdims=True) + EPS)
        o_ref[...] = (xv * r * w_ref[...]).astype(o_ref.dtype)

    return pl.pallas_call(
        body, grid=(l // tl,),
        in_specs=[pl.BlockSpec((tl, d), lambda i: (i, 0)), pl.BlockSpec((1, d), lambda i: (0, 0))],
        out_specs=pl.BlockSpec((tl, d), lambda i: (i, 0)),
        out_shape=jax.ShapeDtypeStruct((l, d), BF16), compiler_params=_params("parallel"), name=name,
    )(x, w.reshape(1, d))


def _post_fwd(x, y, w, name):
    l, d = x.shape
    tl = _row_tile(l)

    def body(x_ref, y_ref, w_ref, o_ref):
        yv = y_ref[...]
        r = lax.rsqrt(jnp.mean(yv * yv, axis=-1, keepdims=True) + EPS)
        o_ref[...] = x_ref[...] + yv * r * w_ref[...]

    return pl.pallas_call(
        body, grid=(l // tl,),
        in_specs=[pl.BlockSpec((tl, d), lambda i: (i, 0)), pl.BlockSpec((tl, d), lambda i: (i, 0)),
                  pl.BlockSpec((1, d), lambda i: (0, 0))],
        out_specs=pl.BlockSpec((tl, d), lambda i: (i, 0)),
        out_shape=jax.ShapeDtypeStruct((l, d), F32), compiler_params=_params("parallel"), name=name,
    )(x, y, w.reshape(1, d))


def _rmsnorm_bwd(g, y, w, resid, out_dtype, name):
    l, d = y.shape
    tl = _row_tile(l)
    nt = l // tl
    has_resid = resid is not None

    def body(*refs):
        if has_resid:
            g_ref, y_ref, w_ref, r_ref, dy_ref, dw_ref, acc_ref = refs
        else:
            g_ref, y_ref, w_ref, dy_ref, dw_ref, acc_ref = refs
        i = pl.program_id(0)

        @pl.when(i == 0)
        def _():
            acc_ref[...] = jnp.zeros_like(acc_ref)

        yv = y_ref[...]
        gv = g_ref[...].astype(F32)
        r = lax.rsqrt(jnp.mean(yv * yv, axis=-1, keepdims=True) + EPS)
        nrm = yv * r
        gw = gv * w_ref[...]
        dy = r * (gw - nrm * jnp.mean(gw * nrm, axis=-1, keepdims=True))
        if has_resid:
            dy = dy + r_ref[...]
        dy_ref[...] = dy.astype(dy_ref.dtype)
        acc_ref[...] += jnp.sum((gv * nrm).reshape(tl // 8, 8, d), axis=0)

        @pl.when(i == nt - 1)
        def _():
            dw_ref[...] = jnp.sum(acc_ref[...], axis=0, keepdims=True)

    row = pl.BlockSpec((tl, d), lambda i: (i, 0))
    vec = pl.BlockSpec((1, d), lambda i: (0, 0))
    ins = [g, y, w.reshape(1, d)] + ([resid] if has_resid else [])
    return pl.pallas_call(
        body, grid=(nt,), in_specs=[row, row, vec] + ([row] if has_resid else []),
        out_specs=[row, vec],
        out_shape=[jax.ShapeDtypeStruct((l, d), out_dtype), jax.ShapeDtypeStruct((1, d), F32)],
        scratch_shapes=[pltpu.VMEM((8, d), F32)], compiler_params=_params("arbitrary"), name=name,
    )(*ins)


def _loss_grad(y, t, name):
    l, d = y.shape
    tl = _row_tile(l)

    def body(y_ref, t_ref, dy_ref, ls_ref):
        @pl.when(pl.program_id(0) == 0)
        def _():
            ls_ref[...] = jnp.zeros_like(ls_ref)

        e = y_ref[...] - t_ref[...]
        dy_ref[...] = e * (1.0 / d)
        ls_ref[...] += jnp.sum((e * e).reshape(tl // 8, 8, d), axis=0)

    row = pl.BlockSpec((tl, d), lambda i: (i, 0))
    return pl.pallas_call(
        body, grid=(l // tl,), in_specs=[row, row],
        out_specs=[row, pl.BlockSpec((8, d), lambda i: (0, 0))],
        out_shape=[jax.ShapeDtypeStruct((l, d), F32), jax.ShapeDtypeStruct((8, d), F32)],
        compiler_params=_params("arbitrary"), name=name,
    )(y, t)


CONV_COLS = 512
HALO = 8


def _conv_fwd(proj, cw, cb, name):
    l = proj.shape[0]
    tl = _row_tile(l)
    off = SSM_D_INNER // CONV_COLS

    def body(u_ref, halo_ref, w_ref, b_ref, pre_ref, act_ref, ext_ref):
        i = pl.program_id(1)
        ext_ref[0:HALO, :] = jnp.where(i > 0, halo_ref[...], 0.0)
        ext_ref[HALO:HALO + tl, :] = u_ref[...]
        acc = jnp.broadcast_to(b_ref[...], (tl, CONV_COLS))
        for k in range(SSM_CONV):
            acc = acc + w_ref[k:k + 1, :] * ext_ref[pl.ds(HALO - SSM_CONV + 1 + k, tl), :]
        pre_ref[...] = acc
        act_ref[...] = acc * _sigmoid(acc)

    hb = tl // HALO
    out = pl.BlockSpec((tl, CONV_COLS), lambda j, i: (i, j))
    return pl.pallas_call(
        body, grid=(SSM_CONV_DIM // CONV_COLS, l // tl),
        in_specs=[pl.BlockSpec((tl, CONV_COLS), lambda j, i: (i, off + j)),
                  pl.BlockSpec((HALO, CONV_COLS), lambda j, i: (jnp.maximum(i * hb - 1, 0), off + j)),
                  pl.BlockSpec((SSM_CONV, CONV_COLS), lambda j, i: (0, j)),
                  pl.BlockSpec((1, CONV_COLS), lambda j, i: (0, j))],
        out_specs=[out, out],
        out_shape=[jax.ShapeDtypeStruct((l, SSM_CONV_DIM), F32)] * 2,
        scratch_shapes=[pltpu.VMEM((tl + HALO, CONV_COLS), F32)],
        compiler_params=_params("parallel", "arbitrary"), name=name,
    )(proj, proj, cw, cb.reshape(1, SSM_CONV_DIM))


def _conv_bwd(dact, pre, proj, cw, c0, name):
    l, width = dact.shape
    tl = _row_tile(l)
    nt = l // tl
    pre_off = c0 // CONV_COLS
    u_off = (SSM_D_INNER + c0) // CONV_COLS
    hb = tl // HALO
    last_hb = l // HALO - 1

    def body(da_ref, da_h_ref, p_ref, p_h_ref, u_ref, u_h_ref, w_ref, du_ref, dw_ref, db_ref, ext_ref, uext_ref):
        i = pl.program_id(1)

        @pl.when(i == 0)
        def _():
            dw_ref[...] = jnp.zeros_like(dw_ref)
            db_ref[...] = jnp.zeros_like(db_ref)

        def dpre_of(da, p):
            s = _sigmoid(p)
            return da * (s * (1.0 + p * (1.0 - s)))

        dp = dpre_of(da_ref[...], p_ref[...])
        ext_ref[0:tl, :] = dp
        ext_ref[tl:tl + HALO, :] = jnp.where(i < nt - 1, dpre_of(da_h_ref[...], p_h_ref[...]), 0.0)
        uext_ref[0:HALO, :] = jnp.where(i > 0, u_h_ref[...], 0.0)
        uext_ref[HALO:HALO + tl, :] = u_ref[...]
        du = jnp.zeros((tl, CONV_COLS), F32)
        for k in range(SSM_CONV):
            du = du + w_ref[k:k + 1, :] * ext_ref[pl.ds(SSM_CONV - 1 - k, tl), :]
            dw_ref[k:k + 1, :] += jnp.sum(dp * uext_ref[pl.ds(HALO - SSM_CONV + 1 + k, tl), :], axis=0, keepdims=True)
        du_ref[...] = du.astype(du_ref.dtype)
        db_ref[...] += jnp.sum(dp, axis=0, keepdims=True)

    return pl.pallas_call(
        body, grid=(width // CONV_COLS, nt),
        in_specs=[pl.BlockSpec((tl, CONV_COLS), lambda j, i: (i, j)),
                  pl.BlockSpec((HALO, CONV_COLS), lambda j, i: (jnp.minimum((i + 1) * hb, last_hb), j)),
                  pl.BlockSpec((tl, CONV_COLS), lambda j, i: (i, pre_off + j)),
                  pl.BlockSpec((HALO, CONV_COLS), lambda j, i: (jnp.minimum((i + 1) * hb, last_hb), pre_off + j)),
                  pl.BlockSpec((tl, CONV_COLS), lambda j, i: (i, u_off + j)),
                  pl.BlockSpec((HALO, CONV_COLS), lambda j, i: (jnp.maximum(i * hb - 1, 0), u_off + j)),
                  pl.BlockSpec((SSM_CONV, CONV_COLS), lambda j, i: (0, pre_off + j))],
        out_specs=[pl.BlockSpec((tl, CONV_COLS), lambda j, i: (i, j)),
                   pl.BlockSpec((SSM_CONV, CONV_COLS), lambda j, i: (0, j)),
                   pl.BlockSpec((1, CONV_COLS), lambda j, i: (0, j))],
        out_shape=[jax.ShapeDtypeStruct((l, width), BF16), jax.ShapeDtypeStruct((SSM_CONV, width), F32),
                   jax.ShapeDtypeStruct((1, width), F32)],
        scratch_shapes=[pltpu.VMEM((tl + HALO, CONV_COLS), F32), pltpu.VMEM((tl + HALO, CONV_COLS), F32)],
        compiler_params=_params("parallel", "arbitrary"), name=name,
    )(dact, dact, pre, pre, proj, proj, cw)


DT_COL_BLOCK = (SSM_D_INNER + SSM_CONV_DIM) // SSM_DT_PAD


def _split3(v):
    hi = v.astype(BF16)
    rest = v - hi.astype(F32)
    mid = rest.astype(BF16)
    lo = (rest - mid.astype(F32)).astype(BF16)
    return hi, mid, lo


def _ssd_prep(proj, bias, alog_lanes, name):
    l = proj.shape[0]
    nc = l // SSM_CHUNK
    head_dim_log2 = SSM_HEAD_DIM.bit_length() - 1

    def body(p_ref, b_ref, al_ref, dtb_ref, acsb_ref, dtr_ref, acsr_ref):
        v = p_ref[...] + b_ref[...]
        dt = jnp.maximum(v, 0.0) + jnp.log1p(jnp.exp(-jnp.abs(v)))
        head_of_lane = lax.shift_right_logical(lax.broadcasted_iota(jnp.int32, (SSM_DT_PAD, SSM_D_INNER), 1), head_dim_log2)
        spread = (head_of_lane == lax.broadcasted_iota(jnp.int32, (SSM_DT_PAD, SSM_D_INNER), 0)).astype(BF16)
        dtb = sum(jnp.dot(piece, spread, preferred_element_type=F32) for piece in _split3(dt))
        dtb_ref[...] = dtb
        ri = lax.broadcasted_iota(jnp.int32, (SSM_CHUNK, SSM_CHUNK), 0)
        cj = lax.broadcasted_iota(jnp.int32, (SSM_CHUNK, SSM_CHUNK), 1)
        tri = (ri >= cj).astype(BF16)
        acsb = sum(jnp.dot(tri, piece, preferred_element_type=F32) for piece in _split3(dtb * (-jnp.exp(al_ref[...]))))
        acsb_ref[...] = acsb
        gp = SSM_HPG * SSM_HEAD_DIM
        lane = lax.broadcasted_iota(jnp.int32, (SSM_HPG, gp), 1)
        pick = (lane == lax.broadcasted_iota(jnp.int32, (SSM_HPG, gp), 0) * SSM_HEAD_DIM).astype(BF16)
        for g in range(SSM_GROUPS):
            cols = slice(g * gp, (g + 1) * gp)
            dtr_ref[g] = sum(lax.dot_general(pick, piece, NT_DIMS, preferred_element_type=F32)
                             for piece in _split3(dtb[:, cols]))
            acsr_ref[g] = sum(lax.dot_general(pick, piece, NT_DIMS, preferred_element_type=F32)
                              for piece in _split3(acsb[:, cols]))

    rows = pl.BlockSpec((SSM_GROUPS, SSM_HPG, SSM_CHUNK), lambda c: (0, 0, c))
    dense = pl.BlockSpec((SSM_CHUNK, SSM_D_INNER), lambda c: (c, 0))
    return pl.pallas_call(
        body, grid=(nc,),
        in_specs=[pl.BlockSpec((SSM_CHUNK, SSM_DT_PAD), lambda c: (c, DT_COL_BLOCK)),
                  pl.BlockSpec((1, SSM_DT_PAD), lambda c: (0, 0)),
                  pl.BlockSpec((1, SSM_D_INNER), lambda c: (0, 0))],
        out_specs=[dense, dense, rows, rows],
        out_shape=[jax.ShapeDtypeStruct((l, SSM_D_INNER), F32), jax.ShapeDtypeStruct((l, SSM_D_INNER), F32),
                   jax.ShapeDtypeStruct((SSM_GROUPS, SSM_HPG, l), F32),
                   jax.ShapeDtypeStruct((SSM_GROUPS, SSM_HPG, l), F32)],
        compiler_params=_params("parallel"), name=name,
    )(proj, bias, alog_lanes)


def _dt_bwd(ddt, proj, bias, name):
    l = proj.shape[0]
    tl = _row_tile(l)

    def body(g_ref, p_ref, b_ref, o_ref, db_ref):
        @pl.when(pl.program_id(0) == 0)
        def _():
            db_ref[...] = jnp.zeros_like(db_ref)

        d = g_ref[...] * _sigmoid(p_ref[...] + b_ref[...])
        o_ref[...] = d.astype(o_ref.dtype)
        db_ref[...] += jnp.sum(d, axis=0, keepdims=True)

    return pl.pallas_call(
        body, grid=(l // tl,),
        in_specs=[pl.BlockSpec((tl, SSM_DT_PAD), lambda i: (i, 0)),
                  pl.BlockSpec((tl, SSM_DT_PAD), lambda i: (i, DT_COL_BLOCK)),
                  pl.BlockSpec((1, SSM_DT_PAD), lambda i: (0, 0))],
        out_specs=[pl.BlockSpec((tl, SSM_DT_PAD), lambda i: (i, 0)), pl.BlockSpec((1, SSM_DT_PAD), lambda i: (0, 0))],
        out_shape=[jax.ShapeDtypeStruct((l, SSM_DT_PAD), BF16), jax.ShapeDtypeStruct((1, SSM_DT_PAD), F32)],
        compiler_params=_params("arbitrary"), name=name,
    )(ddt, proj, bias)


GP = SSM_HPG * SSM_HEAD_DIM
HEAD_DIM_LOG2 = SSM_HEAD_DIM.bit_length() - 1
CHUNK_LOG2 = SSM_CHUNK.bit_length() - 1
GPS = 4
B_BLOCK0 = SSM_D_INNER // SSM_STATE
C_BLOCK0 = (SSM_D_INNER + SSM_BC_DIM) // SSM_STATE


def _chunk_iotas():
    ri = lax.broadcasted_iota(jnp.int32, (SSM_CHUNK, SSM_CHUNK), 0)
    cj = lax.broadcasted_iota(jnp.int32, (SSM_CHUNK, SSM_CHUNK), 1)
    return ri, cj


def _head_decay(acsb, acs_r, r, ri, cj):
    pair = acsb[:, (r // 2) * LANES:(r // 2 + 1) * LANES]
    mine_low = r % 2 == 0
    lane = lax.broadcasted_iota(jnp.int32, (1, LANES), 1)
    col = jnp.where((lane < SSM_HEAD_DIM) == mine_low, pair, pltpu.roll(pair, SSM_HEAD_DIM, 1))
    return jnp.exp(jnp.where(ri >= cj, col - acs_r[r:r + 1, :], NEG_INF))


def _head_masked_rows(v, dtype):
    head_of_lane = lax.shift_right_logical(lax.broadcasted_iota(jnp.int32, (1, GP), 1), HEAD_DIM_LOG2)
    return jnp.concatenate([jnp.where(head_of_lane == r, v, 0.0).astype(dtype) for r in range(SSM_HPG)], axis=0)


def _ssd_fwd(xbc, dtb, acsb, acs_r, d_lanes, name):
    l = xbc.shape[0]
    nc = l // SSM_CHUNK

    def body(x_ref, b_ref, c_ref, dtb_ref, acsb_ref, acsr_ref, d_ref, y_ref, hin_ref, h_ref):
        c = pl.program_id(0)
        gi = pl.program_id(1)
        ri, cj = _chunk_iotas()
        for k in range(GPS):
            g = gi * GPS + k
            cols = slice(k * GP, (k + 1) * GP)
            ncols = slice(k * SSM_STATE, (k + 1) * SSM_STATE)

            @pl.when(c == 0)
            def _():
                h_ref[g] = jnp.zeros((SSM_STATE, GP), F32)

            xv = x_ref[:, cols]
            bb = b_ref[:, ncols].astype(BF16)
            cb16 = c_ref[:, ncols].astype(BF16)
            acs_v = acsb_ref[:, cols]
            acs_r_v = acsr_ref[k]
            lastb = acs_v[SSM_CHUNK - 1:SSM_CHUNK, :]
            xd = xv * dtb_ref[:, cols]
            cb = lax.dot_general(cb16, bb, NT_DIMS, preferred_element_type=F32)
            hin = h_ref[g]
            hin_ref[0, k] = hin
            yoff = jnp.dot(cb16, hin.astype(BF16), preferred_element_type=F32)
            ms = [(cb * _head_decay(acs_v, acs_r_v, r, ri, cj)).astype(BF16) for r in range(SSM_HPG)]
            ydiag = jnp.dot(jnp.concatenate(ms, axis=1), _head_masked_rows(xd, BF16), preferred_element_type=F32)
            y_ref[:, cols] = ydiag + jnp.exp(acs_v) * yoff + d_ref[k] * xv
            h_ref[g] = hin * jnp.exp(lastb) + _bdot_tn(bb, xd * jnp.exp(lastb - acs_v))

    lanes = pl.BlockSpec((SSM_CHUNK, GPS * GP), lambda c, g: (c, g))
    return pl.pallas_call(
        body, grid=(nc, SSM_GROUPS // GPS),
        in_specs=[lanes,
                  pl.BlockSpec((SSM_CHUNK, GPS * SSM_STATE), lambda c, g: (c, B_BLOCK0 // GPS + g)),
                  pl.BlockSpec((SSM_CHUNK, GPS * SSM_STATE), lambda c, g: (c, C_BLOCK0 // GPS + g)),
                  lanes, lanes,
                  pl.BlockSpec((GPS, SSM_HPG, SSM_CHUNK), lambda c, g: (g, 0, c)),
                  pl.BlockSpec((GPS, 1, GP), lambda c, g: (g, 0, 0))],
        out_specs=[lanes, pl.BlockSpec((1, GPS, SSM_STATE, GP), lambda c, g: (c, g, 0, 0))],
        out_shape=[jax.ShapeDtypeStruct((l, SSM_D_INNER), F32),
                   jax.ShapeDtypeStruct((nc, SSM_GROUPS, SSM_STATE, GP), F32)],
        scratch_shapes=[pltpu.VMEM((SSM_GROUPS, SSM_STATE, GP), F32)],
        compiler_params=_params("arbitrary", "arbitrary"), name=name,
    )(xbc, xbc, xbc, dtb, acsb, acs_r, d_lanes)


def _ssd_bwd(xbc, dtb, acsb, dtr, acs_r, a_log, d_lanes, hin, dy, name):
    l = xbc.shape[0]
    nc = l // SSM_CHUNK

    def body(x_ref, b_ref, c_ref, dtb_ref, acsb_ref, dtr_ref, acsr_ref, alc_ref, d_ref, hin_ref, dy_ref,
             dx_ref, db_ref, dc_ref, ddt_ref, dal_ref, dd_ref, dh_ref):
        c = pl.program_id(0)
        gi = pl.program_id(1)

        @pl.when((c == 0) & (gi == 0))
        def _():
            dal_ref[...] = jnp.zeros_like(dal_ref)
            dd_ref[...] = jnp.zeros_like(dd_ref)

        for k in range(GPS):
            one_group(c, gi * GPS + k, k, x_ref, b_ref, c_ref, dtb_ref, acsb_ref, dtr_ref, acsr_ref, alc_ref, d_ref,
                      hin_ref, dy_ref, dx_ref, db_ref, dc_ref, ddt_ref, dal_ref, dd_ref, dh_ref)

    def one_group(c, g, k, x_ref, b_ref, c_ref, dtb_ref, acsb_ref, dtr_ref, acsr_ref, alc_ref, d_ref, hin_ref, dy_ref,
                  dx_ref, db_ref, dc_ref, ddt_ref, dal_ref, dd_ref, dh_ref):
        cols = slice(k * GP, (k + 1) * GP)
        ncols = slice(k * SSM_STATE, (k + 1) * SSM_STATE)

        @pl.when(c == 0)
        def _():
            dh_ref[g] = jnp.zeros((SSM_STATE, GP), F32)

        xv = x_ref[:, cols]
        dyv = dy_ref[:, cols]
        bb = b_ref[:, ncols].astype(BF16)
        cb16 = c_ref[:, ncols].astype(BF16)
        dtb = dtb_ref[:, cols]
        acsb = acsb_ref[:, cols]
        dtr_v = dtr_ref[k]
        acs_r = acsr_ref[k]
        a_col = -jnp.exp(alc_ref[k])
        ri, cj = _chunk_iotas()
        head_of_lane = lax.shift_right_logical(lax.broadcasted_iota(jnp.int32, (SSM_HPG, GP), 1), HEAD_DIM_LOG2)
        ind_t = (head_of_lane == lax.broadcasted_iota(jnp.int32, (SSM_HPG, GP), 0)).astype(BF16)
        lastb = acsb[SSM_CHUNK - 1:SSM_CHUNK, :]
        ecb = jnp.exp(acsb)
        dteb = jnp.exp(lastb - acsb)
        xd = xv * dtb
        xw = xd * dteb
        cb = lax.dot_general(cb16, bb, NT_DIMS, preferred_element_type=F32)
        hin_v = hin_ref[0, k]
        dhn = dh_ref[g]
        h16 = hin_v.astype(BF16)
        dh16 = dhn.astype(BF16)
        ch = jnp.dot(cb16, h16, preferred_element_type=F32)
        bdh = jnp.dot(bb, dh16, preferred_element_type=F32)
        dym = _head_masked_rows(dyv, BF16)
        g_all = lax.dot_general(dym, xd.astype(BF16), NT_DIMS, preferred_element_type=F32)
        gl_sum = jnp.zeros((SSM_CHUNK, SSM_CHUNK), F32)
        ms, qs = [], []
        for r in range(SSM_HPG):
            decay = _head_decay(acsb, acs_r, r, ri, cj)
            gl = g_all[r * SSM_CHUNK:(r + 1) * SSM_CHUNK] * decay
            gl_sum = gl_sum + gl
            ms.append((cb * decay).astype(BF16))
            qs.append((gl * cb).astype(BF16))
        dxd = lax.dot_general(jnp.concatenate(ms, axis=0), dym, TN_DIMS, preferred_element_type=F32) + dteb * bdh
        cum = jnp.dot(jnp.concatenate(qs, axis=0), (ri < cj).astype(BF16), preferred_element_type=F32)
        sub4 = lax.broadcasted_iota(jnp.int32, (SSM_HPG, 1), 0)
        da = jnp.zeros((SSM_HPG, SSM_CHUNK), F32)
        for r in range(SSM_HPG):
            rect = jnp.sum(jnp.where(ri >= cj, cum[r * SSM_CHUNK:(r + 1) * SSM_CHUNK], 0.0), axis=0, keepdims=True)
            da = da + jnp.where(sub4 == r, rect, 0.0)
        z2 = xw * bdh
        sub8 = lax.broadcasted_iota(jnp.int32, (8, 1), 0)
        col_sums = (jnp.where(sub8 == 0, jnp.sum(z2, axis=0, keepdims=True), 0.0)
                    + jnp.where(sub8 == 1, jnp.sum(dhn * hin_v, axis=0, keepdims=True), 0.0)
                    + jnp.where(sub8 == 2, jnp.sum(dyv * xv, axis=0, keepdims=True), 0.0))
        summands = jnp.concatenate([dyv * ecb * ch - z2, dxd * xv, col_sums], axis=0)
        sums = sum(lax.dot_general(ind_t, piece, NT_DIMS, preferred_element_type=F32) for piece in _split3(summands))
        per_pos = sums[:, :2 * SSM_CHUNK]
        totals = sums[:, 2 * SSM_CHUNK:]
        e_last = totals[:, 0:1] + jnp.exp(acs_r[:, SSM_CHUNK - 1:SSM_CHUNK]) * totals[:, 1:2]
        da = (da + e_last + jnp.dot(per_pos[:, :SSM_CHUNK], (ri >= cj).astype(F32), preferred_element_type=F32,
                                    precision=lax.Precision.HIGHEST))
        ddt_ref[k] = a_col * da + per_pos[:, SSM_CHUNK:]
        dal_ref[g] += a_col * jnp.sum(da * dtr_v, axis=1, keepdims=True)
        dd_ref[g] += totals[:, 2:3]
        dx_ref[:, cols] = dxd * dtb + d_ref[k] * dyv
        w16 = (ecb * dyv).astype(BF16)
        xw16 = xw.astype(BF16)
        gl16 = gl_sum.astype(BF16)
        dc_ref[:, ncols] = (jnp.dot(gl16, bb, preferred_element_type=F32)
                            + lax.dot_general(w16, h16, NT_DIMS, preferred_element_type=F32))
        db_ref[:, ncols] = (lax.dot_general(gl16, cb16, TN_DIMS, preferred_element_type=F32)
                            + lax.dot_general(xw16, dh16, NT_DIMS, preferred_element_type=F32))
        dh_ref[g] = dhn * jnp.exp(lastb) + lax.dot_general(cb16, w16, TN_DIMS, preferred_element_type=F32)

    def rev(c):
        return nc - 1 - c

    small = pl.BlockSpec((SSM_GROUPS, SSM_HPG, 1), lambda c, g: (0, 0, 0))
    lanes = pl.BlockSpec((SSM_CHUNK, GPS * GP), lambda c, g: (rev(c), g))
    rows = pl.BlockSpec((GPS, SSM_HPG, SSM_CHUNK), lambda c, g: (g, 0, rev(c)))
    return pl.pallas_call(
        body, grid=(nc, SSM_GROUPS // GPS),
        in_specs=[lanes,
                  pl.BlockSpec((SSM_CHUNK, GPS * SSM_STATE), lambda c, g: (rev(c), B_BLOCK0 // GPS + g)),
                  pl.BlockSpec((SSM_CHUNK, GPS * SSM_STATE), lambda c, g: (rev(c), C_BLOCK0 // GPS + g)),
                  lanes, lanes, rows, rows,
                  pl.BlockSpec((GPS, SSM_HPG, 1), lambda c, g: (g, 0, 0)),
                  pl.BlockSpec((GPS, 1, GP), lambda c, g: (g, 0, 0)),
                  pl.BlockSpec((1, GPS, SSM_STATE, GP), lambda c, g: (rev(c), g, 0, 0)),
                  lanes],
        out_specs=[lanes,
                   pl.BlockSpec((SSM_CHUNK, GPS * SSM_STATE), lambda c, g: (rev(c), g)),
                   pl.BlockSpec((SSM_CHUNK, GPS * SSM_STATE), lambda c, g: (rev(c), g)),
                   rows, small, small],
        out_shape=[jax.ShapeDtypeStruct((l, SSM_D_INNER), F32), jax.ShapeDtypeStruct((l, SSM_BC_DIM), F32),
                   jax.ShapeDtypeStruct((l, SSM_BC_DIM), F32), jax.ShapeDtypeStruct((SSM_GROUPS, SSM_HPG, l), F32),
                   jax.ShapeDtypeStruct((SSM_GROUPS, SSM_HPG, 1), F32),
                   jax.ShapeDtypeStruct((SSM_GROUPS, SSM_HPG, 1), F32)],
        scratch_shapes=[pltpu.VMEM((SSM_GROUPS, SSM_STATE, GP), F32)],
        compiler_params=_params("arbitrary", "arbitrary"), name=name,
    )(xbc, xbc, xbc, dtb, acsb, dtr, acs_r, a_log.reshape(SSM_GROUPS, SSM_HPG, 1), d_lanes, hin, dy)


def _gatenorm_fwd(y, proj, w, name):
    l = y.shape[0]
    tl = _pick(l, (256, 128))

    def body(y_ref, z_ref, w_ref, o_ref):
        z = z_ref[...]
        yg = y_ref[...] * (z * _sigmoid(z))
        r = lax.rsqrt(jnp.mean(yg * yg, axis=-1, keepdims=True) + EPS)
        o_ref[...] = (yg * r * w_ref[...]).astype(o_ref.dtype)

    row = pl.BlockSpec((tl, SSM_D_INNER), lambda i: (i, 0))
    return pl.pallas_call(
        body, grid=(l // tl,), in_specs=[row, row, pl.BlockSpec((1, SSM_D_INNER), lambda i: (0, 0))],
        out_specs=row, out_shape=jax.ShapeDtypeStruct((l, SSM_D_INNER), BF16),
        compiler_params=_params("parallel"), name=name,
    )(y, proj, w.reshape(1, SSM_D_INNER))


def _gatenorm_bwd(g, y, proj, w, name):
    l = y.shape[0]
    tl = _pick(l, (256, 128))
    nt = l // tl

    def body(g_ref, y_ref, z_ref, w_ref, dy_ref, dz_ref, dw_ref, acc_ref):
        i = pl.program_id(0)

        @pl.when(i == 0)
        def _():
            acc_ref[...] = jnp.zeros_like(acc_ref)

        z = z_ref[...]
        yv = y_ref[...]
        s = _sigmoid(z)
        sz = z * s
        yg = yv * sz
        r = lax.rsqrt(jnp.mean(yg * yg, axis=-1, keepdims=True) + EPS)
        nrm = yg * r
        gv = g_ref[...]
        gw = gv * w_ref[...]
        dyg = r * (gw - nrm * jnp.mean(gw * nrm, axis=-1, keepdims=True))
        dy_ref[...] = dyg * sz
        dz_ref[...] = (dyg * yv * (s * (1.0 + z * (1.0 - s)))).astype(dz_ref.dtype)
        acc_ref[...] += jnp.sum((gv * nrm).reshape(tl // 8, 8, SSM_D_INNER), axis=0)

        @pl.when(i == nt - 1)
        def _():
            dw_ref[...] = jnp.sum(acc_ref[...], axis=0, keepdims=True)

    row = pl.BlockSpec((tl, SSM_D_INNER), lambda i: (i, 0))
    vec = pl.BlockSpec((1, SSM_D_INNER), lambda i: (0, 0))
    return pl.pallas_call(
        body, grid=(nt,), in_specs=[row, row, row, vec], out_specs=[row, row, vec],
        out_shape=[jax.ShapeDtypeStruct((l, SSM_D_INNER), F32), jax.ShapeDtypeStruct((l, SSM_D_INNER), BF16),
                   jax.ShapeDtypeStruct((1, SSM_D_INNER), F32)],
        scratch_shapes=[pltpu.VMEM((8, SSM_D_INNER), F32)], compiler_params=_params("arbitrary"), name=name,
    )(g, y, proj, w.reshape(1, SSM_D_INNER))


LANES = 128
ROPE_Q_CHUNKS = ATT_WIDTH // LANES
ROPE_K_CHUNKS = ATT_KV_WIDTH // LANES


def _rope_tables(positions):
    inv = ROPE_THETA ** (-jnp.arange(0, ROPE_DIM, 2, dtype=F32) / ROPE_DIM)
    ang = positions.astype(F32)[:, None] * inv
    cos, sin = jnp.cos(ang), jnp.sin(ang)
    l = positions.shape[0]
    rest = ATT_HEAD_DIM - ROPE_DIM
    ones, zeros = jnp.ones((l, rest), F32), jnp.zeros((l, rest), F32)
    z8 = jnp.zeros((l, ROPE_HALF), F32)
    cos_f = jnp.concatenate([cos, cos, ones], axis=1)
    sin_a = jnp.concatenate([-sin, z8, zeros], axis=1)
    sin_b = jnp.concatenate([z8, sin, zeros], axis=1)
    reps = LANES // ATT_HEAD_DIM
    return tuple(jnp.tile(t, (1, reps)) for t in (cos_f, sin_a, sin_b))


ATT_QKV4 = 3 * ATT_WIDTH


def _both_halves(chunk):
    lane = lax.broadcasted_iota(jnp.int32, (1, LANES), 1)
    swapped = pltpu.roll(chunk, ATT_HEAD_DIM, 1)
    return jnp.where(lane < ATT_HEAD_DIM, chunk, swapped), jnp.where(lane < ATT_HEAD_DIM, swapped, chunk)


def _rope_fwd(proj, tables, name):
    l = proj.shape[0]
    tl = _pick(l, (256, 128))

    def body(p_ref, c_ref, sa_ref, sb_ref, o_ref):
        cos_f, sin_a, sin_b = c_ref[...], sa_ref[...], sb_ref[...]

        def rope(t):
            return t * cos_f + pltpu.roll(t, LANES - ROPE_HALF, 1) * sin_a + pltpu.roll(t, ROPE_HALF, 1) * sin_b

        for k in range(ROPE_Q_CHUNKS):
            sl = slice(k * LANES, (k + 1) * LANES)
            o_ref[:, sl] = (rope(p_ref[:, sl]) * Q_SCALE).astype(o_ref.dtype)
        for part in range(2):
            for k in range(ROPE_K_CHUNKS):
                src = ATT_WIDTH + part * ATT_KV_WIDTH + k * LANES
                t = p_ref[:, src:src + LANES]
                if part == 0:
                    t = rope(t)
                for head, dup in enumerate(_both_halves(t.astype(o_ref.dtype))):
                    dst = (1 + part) * ATT_WIDTH + (2 * k + head) * ATT_GQA * ATT_HEAD_DIM
                    o_ref[:, dst:dst + LANES] = dup
                    o_ref[:, dst + LANES:dst + 2 * LANES] = dup

    tab = pl.BlockSpec((tl, LANES), lambda i: (i, 0))
    return pl.pallas_call(
        body, grid=(l // tl,), in_specs=[pl.BlockSpec((tl, ATT_IN_DIM), lambda i: (i, 0)), tab, tab, tab],
        out_specs=pl.BlockSpec((tl, ATT_QKV4), lambda i: (i, 0)),
        out_shape=jax.ShapeDtypeStruct((l, ATT_QKV4), BF16), compiler_params=_params("parallel"), name=name,
    )(proj, *tables)


def _rope_bwd(dq, dk4, dv4, dgate, tables, name):
    l = dq.shape[0]
    tl = _pick(l, (256, 128))

    def body(dq_ref, dk_ref, dv_ref, dg_ref, c_ref, sa_ref, sb_ref, o_ref):
        cos_f, sin_a, sin_b = c_ref[...], sa_ref[...], sb_ref[...]
        lane = lax.broadcasted_iota(jnp.int32, (1, LANES), 1)

        def unrope(t):
            return t * cos_f + pltpu.roll(t * sin_a, ROPE_HALF, 1) + pltpu.roll(t * sin_b, LANES - ROPE_HALF, 1)

        def head_total(ref, kvh):
            base = kvh * ATT_GQA * ATT_HEAD_DIM
            s = ref[:, base:base + LANES] + ref[:, base + LANES:base + 2 * LANES]
            return s + pltpu.roll(s, ATT_HEAD_DIM, 1)

        for k in range(ROPE_Q_CHUNKS):
            sl = slice(k * LANES, (k + 1) * LANES)
            o_ref[:, sl] = unrope(dq_ref[:, sl] * Q_SCALE).astype(o_ref.dtype)
        for k in range(ROPE_K_CHUNKS):
            dk = jnp.where(lane < ATT_HEAD_DIM, head_total(dk_ref, 2 * k), head_total(dk_ref, 2 * k + 1))
            dv = jnp.where(lane < ATT_HEAD_DIM, head_total(dv_ref, 2 * k), head_total(dv_ref, 2 * k + 1))
            o_ref[:, ATT_WIDTH + k * LANES:ATT_WIDTH + (k + 1) * LANES] = unrope(dk).astype(o_ref.dtype)
            at = ATT_WIDTH + ATT_KV_WIDTH + k * LANES
            o_ref[:, at:at + LANES] = dv.astype(o_ref.dtype)
        o_ref[:, ATT_QKV:ATT_IN_DIM] = dg_ref[...].astype(o_ref.dtype)

    tab = pl.BlockSpec((tl, LANES), lambda i: (i, 0))
    wide = pl.BlockSpec((tl, ATT_WIDTH), lambda i: (i, 0))
    return pl.pallas_call(
        body, grid=(l // tl,), in_specs=[wide, wide, wide, wide, tab, tab, tab],
        out_specs=pl.BlockSpec((tl, ATT_IN_DIM), lambda i: (i, 0)),
        out_shape=jax.ShapeDtypeStruct((l, ATT_IN_DIM), BF16), compiler_params=_params("parallel"), name=name,
    )(dq, dk4, dv4, dgate, *tables)


GATE_HALF = ATT_WIDTH // 2
GATE_COL_BLOCK = ATT_QKV // GATE_HALF


ATT_STACK = ATT_GQA * ATT_BLOCK
BLOCK_LOG2 = ATT_BLOCK.bit_length() - 1


def _stack_masks(n):
    ri = lax.broadcasted_iota(jnp.int32, (ATT_STACK, ATT_BLOCK), 0) & (ATT_BLOCK - 1)
    cj = lax.broadcasted_iota(jnp.int32, (ATT_STACK, ATT_BLOCK), 1)
    return (cj > ri) & (n > 0), cj <= ri


def _stack_sinks(sink_ref, kvh):
    blk = lax.shift_right_logical(lax.broadcasted_iota(jnp.int32, (ATT_STACK, 1), 0), BLOCK_LOG2)
    col = jnp.zeros((ATT_STACK, 1), F32)
    for r in range(ATT_GQA):
        col = jnp.where(blk == r, sink_ref[kvh * ATT_GQA + r], col)
    return col


def _stack_fold(stack):
    head_of_lane = lax.shift_right_logical(lax.broadcasted_iota(jnp.int32, (1, GP), 1), HEAD_DIM_LOG2)
    out = jnp.zeros((ATT_BLOCK, GP), F32)
    for r in range(ATT_GQA):
        out = jnp.where(head_of_lane == r, stack[r * ATT_BLOCK:(r + 1) * ATT_BLOCK], out)
    return out


def _attn_fwd(qkv, proj, sinks, name):
    l = qkv.shape[0]
    nb = l // ATT_BLOCK

    def body(sink_ref, q_ref, kp_ref, kc_ref, vp_ref, vc_ref, g0_ref, g1_ref, og_ref, o_ref, lse_ref):
        n = pl.program_id(0)
        mask_p, mask_c = _stack_masks(n)
        ones = jnp.ones((ATT_BLOCK, LANES), BF16)
        for kvh in range(ATT_KV_HEADS):
            cols = slice(kvh * GP, (kvh + 1) * GP)
            q_stack = _head_masked_rows(q_ref[:, cols], BF16)
            sp = jnp.where(mask_p, lax.dot_general(q_stack, kp_ref[:, cols], NT_DIMS, preferred_element_type=F32), NEG_INF)
            sc = jnp.where(mask_c, lax.dot_general(q_stack, kc_ref[:, cols], NT_DIMS, preferred_element_type=F32), NEG_INF)
            sink = _stack_sinks(sink_ref, kvh)
            m = jnp.maximum(jnp.max(jnp.maximum(sp, sc), axis=1, keepdims=True), sink)
            pp = jnp.exp(sp - m).astype(BF16)
            pc = jnp.exp(sc - m).astype(BF16)
            acc = (jnp.dot(pp, jnp.concatenate([vp_ref[:, cols], ones], axis=1), preferred_element_type=F32)
                   + jnp.dot(pc, jnp.concatenate([vc_ref[:, cols], ones], axis=1), preferred_element_type=F32))
            den = acc[:, GP:] + jnp.exp(sink - m)
            inv = 1.0 / den
            o_ref[:, cols] = _stack_fold(acc[:, :GP] * jnp.concatenate([inv, inv], axis=1))
            lse = m + jnp.log(den)
            lse_ref[:, cols] = _stack_fold(jnp.concatenate([lse, lse], axis=1))
        for half, g_ref in enumerate((g0_ref, g1_ref)):
            sl = slice(half * GATE_HALF, (half + 1) * GATE_HALF)
            gate = g_ref[...]
            og_ref[:, sl] = (o_ref[:, sl] * (gate * _sigmoid(gate))).astype(og_ref.dtype)

    def prev(n):
        return jnp.maximum(n - 1, 0)

    wide = pl.BlockSpec((ATT_BLOCK, ATT_WIDTH), lambda n: (n, 0))
    return pl.pallas_call(
        body, grid=(nb,),
        in_specs=[pl.BlockSpec(memory_space=pltpu.SMEM), wide,
                  pl.BlockSpec((ATT_BLOCK, ATT_WIDTH), lambda n: (prev(n), 1)),
                  pl.BlockSpec((ATT_BLOCK, ATT_WIDTH), lambda n: (n, 1)),
                  pl.BlockSpec((ATT_BLOCK, ATT_WIDTH), lambda n: (prev(n), 2)),
                  pl.BlockSpec((ATT_BLOCK, ATT_WIDTH), lambda n: (n, 2)),
                  pl.BlockSpec((ATT_BLOCK, GATE_HALF), lambda n: (n, GATE_COL_BLOCK)),
                  pl.BlockSpec((ATT_BLOCK, GATE_HALF), lambda n: (n, GATE_COL_BLOCK + 1))],
        out_specs=[wide, wide, wide],
        out_shape=[jax.ShapeDtypeStruct((l, ATT_WIDTH), BF16), jax.ShapeDtypeStruct((l, ATT_WIDTH), F32),
                   jax.ShapeDtypeStruct((l, ATT_WIDTH), F32)],
        compiler_params=_params("parallel"), name=name,
    )(sinks, qkv, qkv, qkv, qkv, qkv, proj, proj)


def _attn_bwd(qkv, proj, sinks, o, lse, dog, name):
    l = qkv.shape[0]
    nb = l // ATT_BLOCK

    def body(sink_ref, q_ref, kp_ref, kc_ref, vp_ref, vc_ref, g0_ref, g1_ref, o_ref, lse_ref, dog_ref,
             dq_ref, dk_ref, dv_ref, dg_ref, ds_ref, ck_ref, cv_ref, do_ref):
        n = pl.program_id(0)

        @pl.when(n == 0)
        def _():
            ds_ref[...] = jnp.zeros_like(ds_ref)
            ck_ref[...] = jnp.zeros_like(ck_ref)
            cv_ref[...] = jnp.zeros_like(cv_ref)

        @pl.when(n == nb)
        def _():
            dk_ref[...] = ck_ref[...]
            dv_ref[...] = cv_ref[...]

        @pl.when(n < nb)
        def _():
            mask_p, mask_c = _stack_masks(n)
            lane = lax.broadcasted_iota(jnp.int32, (1, ATT_Q_HEADS), 1)
            for half, g_ref in enumerate((g0_ref, g1_ref)):
                sl = slice(half * GATE_HALF, (half + 1) * GATE_HALF)
                gate = g_ref[...]
                s = _sigmoid(gate)
                dogv = dog_ref[:, sl]
                do_ref[:, sl] = dogv * (gate * s)
                dg_ref[:, sl] = dogv * o_ref[:, sl] * (s * (1.0 + gate * (1.0 - s)))
            ds_acc = jnp.zeros((1, ATT_Q_HEADS), F32)
            for kvh in range(ATT_KV_HEADS):
                cols = slice(kvh * GP, (kvh + 1) * GP)
                kp, kc, vp, vc = kp_ref[:, cols], kc_ref[:, cols], vp_ref[:, cols], vc_ref[:, cols]
                q_stack = _head_masked_rows(q_ref[:, cols], BF16)
                do_g = do_ref[:, cols]
                do_stack = _head_masked_rows(do_g, BF16)
                lse_g = lse_ref[:, cols]
                lse_stack = jnp.concatenate(
                    [_both_halves(lse_g[:, (r // 2) * LANES:(r // 2 + 1) * LANES])[r % 2] for r in range(ATT_GQA)], axis=0)
                pp = jnp.exp(jnp.where(
                    mask_p, lax.dot_general(q_stack, kp, NT_DIMS, preferred_element_type=F32) - lse_stack, NEG_INF))
                pc = jnp.exp(jnp.where(
                    mask_c, lax.dot_general(q_stack, kc, NT_DIMS, preferred_element_type=F32) - lse_stack, NEG_INF))
                dpp = lax.dot_general(do_stack, vp, NT_DIMS, preferred_element_type=F32)
                dpc = lax.dot_general(do_stack, vc, NT_DIMS, preferred_element_type=F32)
                delta = jnp.sum(pp * dpp + pc * dpc, axis=1, keepdims=True)
                dsp = (pp * (dpp - delta)).astype(BF16)
                dsc = (pc * (dpc - delta)).astype(BF16)
                dq_ref[:, cols] = _stack_fold(jnp.dot(dsp, kp, preferred_element_type=F32)
                                              + jnp.dot(dsc, kc, preferred_element_type=F32))
                dk_ref[:, cols] = ck_ref[:, cols] + lax.dot_general(dsp, q_stack, TN_DIMS, preferred_element_type=F32)
                dv_ref[:, cols] = cv_ref[:, cols] + lax.dot_general(pp.astype(BF16), do_stack, TN_DIMS,
                                                                    preferred_element_type=F32)
                ck_ref[:, cols] = lax.dot_general(dsc, q_stack, TN_DIMS, preferred_element_type=F32)
                cv_ref[:, cols] = lax.dot_general(pc.astype(BF16), do_stack, TN_DIMS, preferred_element_type=F32)
                t = jnp.exp(_stack_sinks(sink_ref, kvh) - lse_stack) * delta
                for r in range(ATT_GQA):
                    tot = jnp.sum(t[r * ATT_BLOCK:(r + 1) * ATT_BLOCK], axis=0, keepdims=True)
                    ds_acc = ds_acc - jnp.where(lane == kvh * ATT_GQA + r, tot[:, :ATT_Q_HEADS], 0.0)
            ds_ref[...] += ds_acc

    def cur(n):
        return jnp.minimum(n, nb - 1)

    def prev(n):
        return jnp.maximum(n - 1, 0)

    wide = pl.BlockSpec((ATT_BLOCK, ATT_WIDTH), lambda n: (cur(n), 0))
    late = pl.BlockSpec((ATT_BLOCK, ATT_WIDTH), lambda n: (prev(n), 0))
    return pl.pallas_call(
        body, grid=(nb + 1,),
        in_specs=[pl.BlockSpec(memory_space=pltpu.SMEM), wide,
                  pl.BlockSpec((ATT_BLOCK, ATT_WIDTH), lambda n: (prev(cur(n)), 1)),
                  pl.BlockSpec((ATT_BLOCK, ATT_WIDTH), lambda n: (cur(n), 1)),
                  pl.BlockSpec((ATT_BLOCK, ATT_WIDTH), lambda n: (prev(cur(n)), 2)),
                  pl.BlockSpec((ATT_BLOCK, ATT_WIDTH), lambda n: (cur(n), 2)),
                  pl.BlockSpec((ATT_BLOCK, GATE_HALF), lambda n: (cur(n), GATE_COL_BLOCK)),
                  pl.BlockSpec((ATT_BLOCK, GATE_HALF), lambda n: (cur(n), GATE_COL_BLOCK + 1)),
                  wide, wide, wide],
        out_specs=[wide, late, late, wide, pl.BlockSpec((1, ATT_Q_HEADS), lambda n: (0, 0))],
        out_shape=[jax.ShapeDtypeStruct((l, ATT_WIDTH), F32), jax.ShapeDtypeStruct((l, ATT_WIDTH), F32),
                   jax.ShapeDtypeStruct((l, ATT_WIDTH), F32), jax.ShapeDtypeStruct((l, ATT_WIDTH), F32),
                   jax.ShapeDtypeStruct((1, ATT_Q_HEADS), F32)],
        scratch_shapes=[pltpu.VMEM((ATT_BLOCK, ATT_WIDTH), F32), pltpu.VMEM((ATT_BLOCK, ATT_WIDTH), F32),
                        pltpu.VMEM((ATT_BLOCK, ATT_WIDTH), F32)],
        compiler_params=_params("arbitrary"), name=name,
    )(sinks, qkv, qkv, qkv, qkv, qkv, proj, proj, o, lse, dog)


def _local_step(x, positions, pre_norm, post_norm, w_ssm_in, conv_w, conv_b, dt_bias, a_log, d_skip, gate_norm,
                w_ssm_out, w_att_in, sinks, w_att_out, target):
    tables = _rope_tables(positions)
    dt_bias_pad = jnp.pad(dt_bias, ((0, 0), (0, SSM_DT_PAD - SSM_HEADS)))
    d_lanes = jnp.repeat(d_skip, SSM_HEAD_DIM, axis=1).reshape(-1, SSM_GROUPS, 1, GP)
    alog_lanes = jnp.repeat(a_log, SSM_HEAD_DIM, axis=1)
    saved = []
    cur = x
    for i in range(DEPTH):
        j = i // 2
        h = _rmsnorm_fwd(cur, pre_norm[i], f"prenorm_fwd_{i}")
        if i % 2 == 0:
            proj = _matmul(h, w_ssm_in[j], "nn", F32, f"ssm_in_{i}")
            pre, xbc = _conv_fwd(proj, conv_w[j], conv_b[j], f"conv_fwd_{i}")
            dtb, acsb, dtr, acs_r = _ssd_prep(proj, dt_bias_pad[j:j + 1], alog_lanes[j:j + 1], f"ssd_prep_{i}")
            y, hin = _ssd_fwd(xbc, dtb, acsb, acs_r, d_lanes[j], f"ssd_fwd_{i}")
            act = _gatenorm_fwd(y, proj, gate_norm[j], f"gatenorm_fwd_{i}")
            ymix = _matmul(act, w_ssm_out[j], "nn", F32, f"ssm_out_{i}")
            saved.append(dict(x=cur, h=h, proj=proj, pre=pre, xbc=xbc, dtb=dtb, acsb=acsb, dtr=dtr, acs_r=acs_r, y=y,
                              hin=hin, act=act, ymix=ymix))
        else:
            proj = _matmul(h, w_att_in[j], "nn", F32, f"att_in_{i}")
            qkv = _rope_fwd(proj, tables, f"rope_fwd_{i}")
            act, o, lse = _attn_fwd(qkv, proj, sinks[j], f"attn_fwd_{i}")
            ymix = _matmul(act, w_att_out[j], "nn", F32, f"att_out_{i}")
            saved.append(dict(x=cur, h=h, proj=proj, qkv=qkv, o=o, lse=lse, act=act, ymix=ymix))
        cur = _post_fwd(cur, ymix, post_norm[i], f"post_fwd_{i}")

    g, loss_lanes = _loss_grad(cur, target, "loss")

    gr = {k: [None] * 2 for k in ("ssm_w_in", "ssm_conv_w", "ssm_conv_b", "ssm_dt_bias", "ssm_a_log", "ssm_d",
                                  "ssm_gate_norm", "ssm_w_out", "att_w_in", "att_sinks", "att_w_out")}
    gr["pre_norm"] = [None] * DEPTH
    gr["post_norm"] = [None] * DEPTH
    for i in reversed(range(DEPTH)):
        j = i // 2
        s = saved[i]
        dymix, gr["post_norm"][i] = _rmsnorm_bwd(g, s["ymix"], post_norm[i], None, BF16, f"post_bwd_{i}")
        if i % 2 == 0:
            dact = _matmul(dymix, w_ssm_out[j], "nt", F32, f"ssm_out_dx_{i}")
            gr["ssm_w_out"][j] = _matmul(s["act"], dymix, "tn", F32, f"ssm_out_dw_{i}")
            dy, dz, gr["ssm_gate_norm"][j] = _gatenorm_bwd(dact, s["y"], s["proj"], gate_norm[j], f"gatenorm_bwd_{i}")
            dxs, db, dc, ddt8, dal, dd = _ssd_bwd(s["xbc"], s["dtb"], s["acsb"], s["dtr"], s["acs_r"], a_log[j],
                                                  d_lanes[j], s["hin"], dy, f"ssd_bwd_{i}")
            gr["ssm_a_log"][j] = dal.reshape(SSM_HEADS)
            gr["ssm_d"][j] = dd.reshape(SSM_HEADS)
            l = x.shape[0]
            ddt = jnp.pad(jnp.transpose(ddt8, (2, 0, 1)).reshape(l, SSM_HEADS), ((0, 0), (0, SSM_DT_PAD - SSM_HEADS)))
            ddt_raw, dbias = _dt_bwd(ddt, s["proj"], dt_bias_pad[j:j + 1], f"dt_bwd_{i}")
            gr["ssm_dt_bias"][j] = dbias[0, :SSM_HEADS]
            pieces, dcw, dcb = [], [], []
            for c0, dpiece, tag in ((0, dxs, "x"), (SSM_D_INNER, db, "b"), (SSM_D_INNER + SSM_BC_DIM, dc, "c")):
                du, dw_, db_ = _conv_bwd(dpiece, s["pre"], s["proj"], conv_w[j], c0, f"conv_bwd_{tag}_{i}")
                pieces.append(du)
                dcw.append(dw_)
                dcb.append(db_)
            gr["ssm_conv_w"][j] = jnp.concatenate(dcw, axis=1)
            gr["ssm_conv_b"][j] = jnp.concatenate(dcb, axis=1)[0]
            dproj = jnp.concatenate([dz] + pieces + [ddt_raw], axis=1)
            w_in, key = w_ssm_in[j], "ssm_w_in"
        else:
            dog = _matmul(dymix, w_att_out[j], "nt", F32, f"att_out_dx_{i}")
            gr["att_w_out"][j] = _matmul(s["act"], dymix, "tn", F32, f"att_out_dw_{i}")
            dq, dk, dv, dgate, dsk = _attn_bwd(s["qkv"], s["proj"], sinks[j], s["o"], s["lse"], dog, f"attn_bwd_{i}")
            gr["att_sinks"][j] = dsk[0]
            dproj = _rope_bwd(dq, dk, dv, dgate, tables, f"rope_bwd_{i}")
            w_in, key = w_att_in[j], "att_w_in"
        dh = _matmul(dproj, w_in, "nt", F32, f"in_dx_{i}")
        gr[key][j] = _matmul(s["h"], dproj, "tn", F32, f"in_dw_{i}")
        g, gr["pre_norm"][i] = _rmsnorm_bwd(dh, s["x"], pre_norm[i], g, F32, f"prenorm_bwd_{i}")
    grads = {k: jnp.stack([v.reshape(v.shape[-1]) if k in ("pre_norm", "post_norm", "ssm_gate_norm") else v for v in vs])
             for k, vs in gr.items()}
    return loss_lanes, g, grads


N_CHIPS = 4
N_DEV = 8
MESH = pl.DeviceIdType.MESH
ANY = pl.BlockSpec(memory_space=pl.ANY)


def _place():
    x, y, c = lax.axis_index("x"), lax.axis_index("y"), lax.axis_index("c")
    return x, y, c, 2 * x + y


def _chip_gather(shards, name):
    n = len(shards)

    def body(*refs):
        ins, outs = refs[:n], refs[n:2 * n]
        send_sems, recv_sems, pass_send_sems, pass_recv_sems, local_sems = refs[2 * n:]
        x, y, c, s = _place()
        local = [pltpu.make_async_copy(ins[w], outs[w].at[s], local_sems.at[w]) for w in range(n)]
        for cp in local:
            cp.start()

        def remote(w, t):
            return pltpu.make_async_remote_copy(
                src_ref=ins[w].at[c], dst_ref=outs[w].at[s, c], send_sem=send_sems.at[w, t],
                recv_sem=recv_sems.at[w, s], device_id=(t // 2, t % 2, c), device_id_type=MESH)

        def arrival(w, t):
            return pltpu.make_async_remote_copy(
                src_ref=ins[w].at[c], dst_ref=outs[w].at[t, c], send_sem=send_sems.at[w, t],
                recv_sem=recv_sems.at[w, t], device_id=(t // 2, t % 2, c), device_id_type=MESH)

        def handed_on(w, t):
            return pltpu.make_async_remote_copy(
                src_ref=outs[w].at[t, c], dst_ref=outs[w].at[t, c], send_sem=pass_send_sems.at[w, t],
                recv_sem=pass_recv_sems.at[w, t], device_id=(x, y, 1 - c), device_id_type=MESH)

        def handed_in(w, t):
            return pltpu.make_async_remote_copy(
                src_ref=outs[w].at[t, 1 - c], dst_ref=outs[w].at[t, 1 - c], send_sem=pass_send_sems.at[w, t],
                recv_sem=pass_recv_sems.at[w, t], device_id=(x, y, 1 - c), device_id_type=MESH)

        for t in range(N_CHIPS):
            @pl.when(s != t)
            def _():
                for w in range(n):
                    remote(w, t).start()
        for t in range(N_CHIPS):
            @pl.when(s != t)
            def _():
                for w in range(n):
                    arrival(w, t).wait_recv()
                    handed_on(w, t).start()
        for t in range(N_CHIPS):
            @pl.when(s != t)
            def _():
                for w in range(n):
                    remote(w, t).wait_send()
                    handed_on(w, t).wait_send()
                    handed_in(w, t).wait_recv()
        for cp in local:
            cp.wait()

    return pl.pallas_call(
        body, in_specs=[ANY] * n, out_specs=[ANY] * n,
        out_shape=[jax.ShapeDtypeStruct((N_CHIPS,) + a.shape, a.dtype) for a in shards],
        scratch_shapes=[pltpu.SemaphoreType.DMA((n, N_CHIPS)), pltpu.SemaphoreType.DMA((n, N_CHIPS)),
                        pltpu.SemaphoreType.DMA((n, N_CHIPS)), pltpu.SemaphoreType.DMA((n, N_CHIPS)),
                        pltpu.SemaphoreType.DMA((n,))],
        name=name,
    )(*shards)


def _pair_swap(parts, name):
    n = len(parts)

    def body(*refs):
        ins, outs = refs[:n], refs[n:2 * n]
        send_sems, recv_sems = refs[2 * n:]
        x, y, c, _ = _place()
        cps = [pltpu.make_async_remote_copy(
            src_ref=ins[w].at[1 - c], dst_ref=outs[w], send_sem=send_sems.at[w], recv_sem=recv_sems.at[w],
            device_id=(x, y, 1 - c), device_id_type=MESH) for w in range(n)]
        for cp in cps:
            cp.start()
        for cp in cps:
            cp.wait()

    return pl.pallas_call(
        body, in_specs=[ANY] * n, out_specs=[ANY] * n,
        out_shape=[jax.ShapeDtypeStruct(a.shape[1:], a.dtype) for a in parts],
        scratch_shapes=[pltpu.SemaphoreType.DMA((n,)), pltpu.SemaphoreType.DMA((n,))],
        name=name,
    )(*parts)


def _chip_scatter(parts, name):
    n = len(parts)
    rows = [a.shape[0] // N_CHIPS for a in parts]

    def body(*refs):
        ins, outs = refs[:n], refs[n:2 * n]
        send_sems, recv_sems, local_sems = refs[2 * n:]
        _, _, c, s = _place()

        def block(w, t):
            return ins[w].at[pl.ds(t * rows[w], rows[w])]

        local = [pltpu.make_async_copy(block(w, s), outs[w].at[s], local_sems.at[w]) for w in range(n)]
        for cp in local:
            cp.start()

        def remote(w, t):
            return pltpu.make_async_remote_copy(
                src_ref=block(w, t), dst_ref=outs[w].at[s], send_sem=send_sems.at[w, t], recv_sem=recv_sems.at[w, s],
                device_id=(t // 2, t % 2, c), device_id_type=MESH)

        def arrival(w, t):
            return pltpu.make_async_remote_copy(
                src_ref=block(w, t), dst_ref=outs[w].at[t], send_sem=send_sems.at[w, t], recv_sem=recv_sems.at[w, t],
                device_id=(t // 2, t % 2, c), device_id_type=MESH)

        for t in range(N_CHIPS):
            @pl.when(s != t)
            def _():
                for w in range(n):
                    remote(w, t).start()
        for t in range(N_CHIPS):
            @pl.when(s != t)
            def _():
                for w in range(n):
                    remote(w, t).wait_send()
                    arrival(w, t).wait_recv()
        for cp in local:
            cp.wait()

    return pl.pallas_call(
        body, in_specs=[ANY] * n, out_specs=[ANY] * n,
        out_shape=[jax.ShapeDtypeStruct((N_CHIPS, r, a.shape[1]), a.dtype) for a, r in zip(parts, rows)],
        scratch_shapes=[pltpu.SemaphoreType.DMA((n, N_CHIPS)), pltpu.SemaphoreType.DMA((n, N_CHIPS)),
                        pltpu.SemaphoreType.DMA((n,))],
        name=name,
    )(*parts)


def _pair_merge(parts, name):
    n = len(parts)

    def body(*refs):
        ins, outs = refs[:n], refs[n:2 * n]
        send_sems, recv_sems = refs[2 * n:]
        x, y, c, _ = _place()
        cps = [pltpu.make_async_remote_copy(
            src_ref=ins[w], dst_ref=outs[w], send_sem=send_sems.at[w], recv_sem=recv_sems.at[w],
            device_id=(x, y, 1 - c), device_id_type=MESH) for w in range(n)]
        for cp in cps:
            cp.start()
        for cp in cps:
            cp.wait()

    return pl.pallas_call(
        body, in_specs=[ANY] * n, out_specs=[ANY] * n,
        out_shape=[jax.ShapeDtypeStruct(a.shape, a.dtype) for a in parts],
        scratch_shapes=[pltpu.SemaphoreType.DMA((n,)), pltpu.SemaphoreType.DMA((n,))],
        name=name,
    )(*parts)


def _all_gather_small(a, name):
    def body(in_ref, out_ref, send_sems, recv_sems, local_sem):
        x, y, c, _ = _place()
        me = 4 * x + 2 * y + c
        local = pltpu.make_async_copy(in_ref, out_ref.at[me], local_sem)
        local.start()

        def remote(d):
            return pltpu.make_async_remote_copy(
                src_ref=in_ref, dst_ref=out_ref.at[me], send_sem=send_sems.at[d], recv_sem=recv_sems.at[me],
                device_id=(d // 4, (d // 2) % 2, d % 2), device_id_type=MESH)

        def arrival(d):
            return pltpu.make_async_remote_copy(
                src_ref=in_ref, dst_ref=out_ref.at[d], send_sem=send_sems.at[d], recv_sem=recv_sems.at[d],
                device_id=(d // 4, (d // 2) % 2, d % 2), device_id_type=MESH)

        for d in range(N_DEV):
            @pl.when(me != d)
            def _():
                remote(d).start()
        for d in range(N_DEV):
            @pl.when(me != d)
            def _():
                remote(d).wait_send()
                arrival(d).wait_recv()
        local.wait()

    return pl.pallas_call(
        body, in_specs=[ANY], out_specs=ANY, out_shape=jax.ShapeDtypeStruct((N_DEV,) + a.shape, a.dtype),
        scratch_shapes=[pltpu.SemaphoreType.DMA((N_DEV,)), pltpu.SemaphoreType.DMA((N_DEV,)), pltpu.SemaphoreType.DMA],
        name=name,
    )(a)


def _reduce_tile(rows):
    return _pick(rows, (256, 16))


def _pair_add(full, other, layer, name):
    _, rows, cols = full.shape
    tr = _reduce_tile(rows)

    def body(layer_ref, a_ref, b_ref, o_ref):
        o_ref[...] = (a_ref[0] + b_ref[...]).astype(o_ref.dtype)

    return pl.pallas_call(
        body,
        grid_spec=pltpu.PrefetchScalarGridSpec(
            num_scalar_prefetch=1, grid=(rows // tr,),
            in_specs=[pl.BlockSpec((1, tr, cols), lambda i, lr: (lr[0], i, 0)), pl.BlockSpec((tr, cols), lambda i, lr: (i, 0))],
            out_specs=pl.BlockSpec((tr, cols), lambda i, lr: (i, 0))),
        out_shape=jax.ShapeDtypeStruct((rows, cols), BF16), compiler_params=_params("parallel"), name=name,
    )(layer, full, other)


def _sum_slots(a, name):
    n, rows, cols = a.shape
    tr = _reduce_tile(rows)

    def body(a_ref, o_ref):
        acc = a_ref[0].astype(F32)
        for k in range(1, n):
            acc = acc + a_ref[k].astype(F32)
        o_ref[...] = acc

    return pl.pallas_call(
        body, grid=(rows // tr,), in_specs=[pl.BlockSpec((n, tr, cols), lambda i: (0, i, 0))],
        out_specs=pl.BlockSpec((tr, cols), lambda i: (i, 0)),
        out_shape=jax.ShapeDtypeStruct((rows, cols), F32), compiler_params=_params("parallel"), name=name,
    )(a)


def _adamw(w, g, m, v, name):
    rows, cols = w.shape
    tr = _pick(rows, (256, 8))

    def body(w_ref, g_ref, m_ref, v_ref, d_ref, nm_ref, nv_ref):
        gv = g_ref[...]
        mn = ADAM_B1 * m_ref[...] + (1.0 - ADAM_B1) * gv
        vn = ADAM_B2 * v_ref[...] + (1.0 - ADAM_B2) * jnp.square(gv)
        m_hat = mn / (1.0 - ADAM_B1 ** ADAM_STEP)
        v_hat = vn / (1.0 - ADAM_B2 ** ADAM_STEP)
        d_ref[...] = -ADAM_LR * (m_hat / (jnp.sqrt(v_hat) + ADAM_EPS) + ADAM_WD * w_ref[...])
        nm_ref[...] = mn
        nv_ref[...] = vn

    blk = pl.BlockSpec((tr, cols), lambda i: (i, 0))
    return pl.pallas_call(
        body, grid=(rows // tr,), in_specs=[blk] * 4, out_specs=[blk] * 3,
        out_shape=[jax.ShapeDtypeStruct((rows, cols), F32)] * 3, compiler_params=_params("parallel"), name=name,
    )(w, g, m, v)


BIG = ("ssm_w_in", "ssm_w_out", "att_w_in", "att_w_out")
SHARDED = BIG + ("ssm_conv_w",)
SMALL = ("pre_norm", "post_norm", "ssm_conv_b", "ssm_dt_bias", "ssm_a_log", "ssm_d", "ssm_gate_norm", "att_sinks")
WEIGHTS = ("pre_norm", "post_norm", "ssm_w_in", "ssm_conv_w", "ssm_conv_b", "ssm_dt_bias", "ssm_a_log", "ssm_d",
           "ssm_gate_norm", "ssm_w_out", "att_w_in", "att_sinks", "att_w_out")


def _cols_to_whole(g):
    _, two, rows, cols = g.shape
    return jnp.transpose(g, (1, 2, 0, 3)).reshape(two, rows, N_CHIPS * cols)


def _rows_to_whole(g):
    _, two, rows, cols = g.shape
    return jnp.transpose(g, (1, 0, 2, 3)).reshape(two, N_CHIPS * rows, cols)


def _cols_by_chip(g):
    two, rows, cols = g.shape
    return jnp.transpose(g.reshape(two, rows, N_CHIPS, cols // N_CHIPS), (0, 2, 1, 3)).reshape(two, N_CHIPS * rows, cols // N_CHIPS)


def _pack_small(tree, keys):
    flat = jnp.concatenate([tree[k].reshape(-1) for k in keys])
    rows = -(-flat.shape[0] // (8 * LANES)) * 8
    return jnp.pad(flat, (0, rows * LANES - flat.shape[0])).reshape(rows, LANES)


def _unpack_small(packed, shapes, keys):
    flat = packed.reshape(-1)
    out, at = {}, 0
    for k in keys:
        n = 1
        for dim in shapes[k]:
            n *= dim
        out[k] = flat[at:at + n].reshape(shapes[k])
        at += n
    return out


def kernel(x, positions, pre_norm, post_norm, ssm_w_in, ssm_conv_w, ssm_conv_b, ssm_dt_bias, ssm_a_log, ssm_d, ssm_gate_norm, ssm_w_out, att_w_in, att_sinks, att_w_out, loss_target, m_pre_norm, m_post_norm, m_ssm_w_in, m_ssm_conv_w, m_ssm_conv_b, m_ssm_dt_bias, m_ssm_a_log, m_ssm_d, m_ssm_gate_norm, m_ssm_w_out, m_att_w_in, m_att_sinks, m_att_w_out, v_pre_norm, v_post_norm, v_ssm_w_in, v_ssm_conv_w, v_ssm_conv_b, v_ssm_dt_bias, v_ssm_a_log, v_ssm_d, v_ssm_gate_norm, v_ssm_w_out, v_att_w_in, v_att_sinks, v_att_w_out):
    w = dict(pre_norm=pre_norm, post_norm=post_norm, ssm_w_in=ssm_w_in, ssm_conv_w=ssm_conv_w, ssm_conv_b=ssm_conv_b,
             ssm_dt_bias=ssm_dt_bias, ssm_a_log=ssm_a_log, ssm_d=ssm_d, ssm_gate_norm=ssm_gate_norm, ssm_w_out=ssm_w_out,
             att_w_in=att_w_in, att_sinks=att_sinks, att_w_out=att_w_out)
    m = dict(pre_norm=m_pre_norm, post_norm=m_post_norm, ssm_w_in=m_ssm_w_in, ssm_conv_w=m_ssm_conv_w, ssm_conv_b=m_ssm_conv_b,
             ssm_dt_bias=m_ssm_dt_bias, ssm_a_log=m_ssm_a_log, ssm_d=m_ssm_d, ssm_gate_norm=m_ssm_gate_norm,
             ssm_w_out=m_ssm_w_out, att_w_in=m_att_w_in, att_sinks=m_att_sinks, att_w_out=m_att_w_out)
    v = dict(pre_norm=v_pre_norm, post_norm=v_post_norm, ssm_w_in=v_ssm_w_in, ssm_conv_w=v_ssm_conv_w, ssm_conv_b=v_ssm_conv_b,
             ssm_dt_bias=v_ssm_dt_bias, ssm_a_log=v_ssm_a_log, ssm_d=v_ssm_d, ssm_gate_norm=v_ssm_gate_norm,
             ssm_w_out=v_ssm_w_out, att_w_in=v_att_w_in, att_sinks=v_att_sinks, att_w_out=v_att_w_out)
    c = lax.axis_index("c")
    chip = 2 * lax.axis_index("x") + lax.axis_index("y")

    g_in, g_out, g_ain, g_aout, g_cw = _chip_gather(
        [ssm_w_in.astype(BF16), ssm_w_out.astype(BF16), att_w_in.astype(BF16), att_w_out.astype(BF16), ssm_conv_w],
        "gather_weights")
    w_in_full = jnp.pad(_cols_to_whole(g_in), ((0, 0), (0, 0), (0, SSM_IN_PAD - SSM_IN_DIM)))
    loss_lanes, grad_x, gr = _local_step(
        x[0], positions[0], pre_norm, post_norm, w_in_full, _cols_to_whole(g_cw), ssm_conv_b, ssm_dt_bias, ssm_a_log,
        ssm_d, ssm_gate_norm, _rows_to_whole(g_out), _cols_to_whole(g_ain), att_sinks, _rows_to_whole(g_aout),
        loss_target[0])
    loss = lax.psum(0.5 * jnp.sum(loss_lanes) / D_MODEL, ("x", "y", "c"))

    parts = [_cols_by_chip(gr["ssm_w_in"][:, :, :SSM_IN_DIM]), gr["ssm_w_out"], _cols_by_chip(gr["att_w_in"]),
             gr["att_w_out"]]
    from_sibling = _pair_swap(parts, "reduce_pair_swap")
    layer = jnp.reshape(c, (1,)).astype(jnp.int32)
    chip_sums = [_pair_add(p, o, layer, f"reduce_pair_add_{k}") for k, (p, o) in enumerate(zip(parts, from_sibling))]
    by_chip = _chip_scatter(chip_sums, "reduce_chip_scatter")
    mine = [_sum_slots(a, f"reduce_chip_sum_{k}") for k, a in enumerate(by_chip)]
    theirs = _pair_merge(mine, "reduce_pair_merge")
    grads = {k: jnp.stack([jnp.where(c == 0, a, b), jnp.where(c == 0, b, a)]).reshape(w[k].shape)
             for k, a, b in zip(BIG, mine, theirs)}

    small_keys = SMALL + ("ssm_conv_w",)
    small_shapes = {k: w[k].shape for k in SMALL}
    small_shapes["ssm_conv_w"] = gr["ssm_conv_w"].shape
    small_sum = _sum_slots(_all_gather_small(_pack_small(gr, small_keys), "reduce_small_gather"), "reduce_small_sum")
    grads.update(_unpack_small(small_sum, small_shapes, small_keys))
    conv_cols = ssm_conv_w.shape[2]
    grads["ssm_conv_w"] = lax.dynamic_slice_in_dim(grads["ssm_conv_w"], chip * conv_cols, conv_cols, axis=2)

    delta, new_m, new_v = {}, {}, {}
    for k in SHARDED:
        shp = w[k].shape
        two_d = (shp[0] * shp[1], shp[2])
        d_, m_, v_ = _adamw(w[k].reshape(two_d), grads[k].reshape(two_d), m[k].reshape(two_d), v[k].reshape(two_d),
                            f"adamw_{k}")
        delta[k], new_m[k], new_v[k] = d_.reshape(shp), m_.reshape(shp), v_.reshape(shp)
    d_, m_, v_ = _adamw(_pack_small(w, SMALL), _pack_small(grads, SMALL), _pack_small(m, SMALL), _pack_small(v, SMALL),
                        "adamw_small")
    delta.update(_unpack_small(d_, small_shapes, SMALL))
    new_m.update(_unpack_small(m_, small_shapes, SMALL))
    new_v.update(_unpack_small(v_, small_shapes, SMALL))

    return (loss, grad_x[None], *[grads[k] for k in WEIGHTS], *[delta[k] for k in WEIGHTS],
            *[new_m[k] for k in WEIGHTS], *[new_v[k] for k in WEIGHTS])
```

```python
import functools

import jax
import jax.numpy as jnp
from jax import lax
from jax.experimental import pallas as pl
from jax.experimental.pallas import tpu as pltpu

F32 = jnp.float32
BF16 = jnp.bfloat16
EPS = 1e-6
NEG_INF = float("-inf")

D_MODEL = 1024
DEPTH = 4
SSM_D_INNER = 2048
SSM_HEAD_DIM = 64
SSM_HEADS = 32
SSM_GROUPS = 8
SSM_HPG = 4
SSM_STATE = 128
SSM_CONV = 4
SSM_CHUNK = 128
SSM_BC_DIM = 1024
SSM_CONV_DIM = 4096
SSM_IN_DIM = 6176
SSM_IN_PAD = 6272
SSM_DT_PAD = 128
ATT_HEAD_DIM = 64
ATT_Q_HEADS = 16
ATT_KV_HEADS = 4
ATT_GQA = 4
ATT_WIDTH = 1024
ATT_KV_WIDTH = 256
ATT_IN_DIM = 2560
ATT_QKV = ATT_WIDTH + 2 * ATT_KV_WIDTH
ATT_BLOCK = 128
ROPE_THETA = 500000.0
ROPE_DIM = 16
ROPE_HALF = 8
Q_SCALE = ATT_HEAD_DIM ** -0.5

ADAM_LR = 0.001
ADAM_B1 = 0.9
ADAM_B2 = 0.999
ADAM_EPS = 1e-08
ADAM_WD = 0.01
ADAM_STEP = 10

VMEM_LIMIT_BYTES = 48 * 1024 * 1024
NT_DIMS = (((1,), (1,)), ((), ()))
TN_DIMS = (((0,), (0,)), ((), ()))


def _params(*sem):
    return pltpu.CompilerParams(dimension_semantics=sem, vmem_limit_bytes=VMEM_LIMIT_BYTES)


def _pick(n, cands):
    for c in cands:
        if n % c == 0:
            return c
    return n


def _sigmoid(v):
    return 0.5 * jnp.tanh(0.5 * v) + 0.5


def _bdot(a, b):
    return jnp.dot(a.astype(BF16), b.astype(BF16), preferred_element_type=F32)


def _bdot_nt(a, b):
    return lax.dot_general(a.astype(BF16), b.astype(BF16), NT_DIMS, preferred_element_type=F32)


def _bdot_tn(a, b):
    return lax.dot_general(a.astype(BF16), b.astype(BF16), TN_DIMS, preferred_element_type=F32)


MATMUL_VMEM_BUDGET = 36 * 1024 * 1024


def _matmul_tiles(m, n, k, out_bytes, reduce_rows):
    best = None
    whole = [k] if (not reduce_rows or k <= 2048) else []
    for tk in whole + [c for c in (4096, 2048, 1024, 896, 512) if k % c == 0 and c < k]:
        for tm in (c for c in (1024, 512, 256) if m % c == 0):
            for tn in (c for c in (n, 1280, 1024, 896, 640, 512) if n % c == 0):
                acc = tm * tn * 4 if tk < k else 0
                need = 2 * (2 * tk * (tm + tn) + tm * tn * out_bytes) + acc
                if need <= MATMUL_VMEM_BUDGET and (best is None or tm * tn * min(tk, 2048) > best[0]):
                    best = (tm * tn * min(tk, 2048), tm, tn, tk)
        if best is not None and not reduce_rows:
            break
    return best[1:]


def _matmul(a, b, mode, out_dtype, name):
    if mode == "nn":
        (m, k), n = a.shape, b.shape[1]
    elif mode == "nt":
        (m, k), n = a.shape, b.shape[0]
    else:
        (k, m), n = a.shape, b.shape[1]
    tm, tn, tk = _matmul_tiles(m, n, k, jnp.dtype(out_dtype).itemsize, mode == "tn")
    nk = k // tk
    dims = {"nn": (((1,), (0,)), ((), ())), "nt": NT_DIMS, "tn": TN_DIMS}[mode]

    def body(a_ref, b_ref, o_ref, acc_ref):
        kk = pl.program_id(2)
        part = lax.dot_general(a_ref[...], b_ref[...], dims, preferred_element_type=F32)
        if nk == 1:
            o_ref[...] = part.astype(o_ref.dtype)
        else:
            @pl.when(kk == 0)
            def _():
                acc_ref[...] = part

            @pl.when(kk > 0)
            def _():
                acc_ref[...] += part

            @pl.when(kk == nk - 1)
            def _():
                o_ref[...] = acc_ref[...].astype(o_ref.dtype)

    if mode == "nn":
        a_spec = pl.BlockSpec((tm, tk), lambda j, i, kk: (i, kk))
        b_spec = pl.BlockSpec((tk, tn), lambda j, i, kk: (kk, j))
    elif mode == "nt":
        a_spec = pl.BlockSpec((tm, tk), lambda j, i, kk: (i, kk))
        b_spec = pl.BlockSpec((tn, tk), lambda j, i, kk: (j, kk))
    else:
        a_spec = pl.BlockSpec((tk, tm), lambda j, i, kk: (kk, i))
        b_spec = pl.BlockSpec((tk, tn), lambda j, i, kk: (kk, j))
    return pl.pallas_call(
        body, grid=(n // tn, m // tm, nk), in_specs=[a_spec, b_spec],
        out_specs=pl.BlockSpec((tm, tn), lambda j, i, kk: (i, j)),
        out_shape=jax.ShapeDtypeStruct((m, n), out_dtype),
        scratch_shapes=[pltpu.VMEM((tm, tn), F32)],
        compiler_params=_params("parallel", "parallel", "arbitrary"), name=name,
    )(a, b)


def _row_tile(l):
    return _pick(l, (512, 256, 128))


def _rmsnorm_fwd(x, w, name):
    l, d = x.shape
    tl = _row_tile(l)

    def body(x_ref, w_ref, o_ref):
        xv = x_ref[...]
        r = lax.rsqrt(jnp.mean(xv * xv, axis=-1, keepdims=True) + EPS)
        o_ref[...] = (xv * r * w_ref[...]).astype(o_ref.dtype)

    return pl.pallas_call(
        body, grid=(l // tl,),
        in_specs=[pl.BlockSpec((tl, d), lambda i: (i, 0)), pl.BlockSpec((1, d), lambda i: (0, 0))],
        out_specs=pl.BlockSpec((tl, d), lambda i: (i, 0)),
        out_shape=jax.ShapeDtypeStruct((l, d), BF16), compiler_params=_params("parallel"), name=name,
    )(x, w.reshape(1, d))


def _post_fwd(x, y, w, name):
    l, d = x.shape
    tl = _row_tile(l)

    def body(x_ref, y_ref, w_ref, o_ref):
        yv = y_ref[...]
        r = lax.rsqrt(jnp.mean(yv * yv, axis=-1, keepdims=True) + EPS)
        o_ref[...] = x_ref[...] + yv * r * w_ref[...]

    return pl.pallas_call(
        body, grid=(l // tl,),
        in_specs=[pl.BlockSpec((tl, d), lambda i: (i, 0)), pl.BlockSpec((tl, d), lambda i: (i, 0)),
                  pl.BlockSpec((1, d), lambda i: (0, 0))],
        out_specs=pl.BlockSpec((tl, d), lambda i: (i, 0)),
        out_shape=jax.ShapeDtypeStruct((l, d), F32), compiler_params=_params("parallel"), name=name,
    )(x, y, w.reshape(1, d))


def _rmsnorm_bwd(g, y, w, resid, out_dtype, name):
    l, d = y.shape
    tl = _row_tile(l)
    nt = l // tl
    has_resid = resid is not None

    def body(*refs):
        if has_resid:
            g_ref, y_ref, w_ref, r_ref, dy_ref, dw_ref, acc_ref = refs
        else:
            g_ref, y_ref, w_ref, dy_ref, dw_ref, acc_ref = refs
        i = pl.program_id(0)

        @pl.when(i == 0)
        def _():
            acc_ref[...] = jnp.zeros_like(acc_ref)

        yv = y_ref[...]
        gv = g_ref[...].astype(F32)
        r = lax.rsqrt(jnp.mean(yv * yv, axis=-1, keepdims=True) + EPS)
        nrm = yv * r
        gw = gv * w_ref[...]
        dy = r * (gw - nrm * jnp.mean(gw * nrm, axis=-1, keepdims=True))
        if has_resid:
            dy = dy + r_ref[...]
        dy_ref[...] = dy.astype(dy_ref.dtype)
        acc_ref[...] += jnp.sum((gv * nrm).reshape(tl // 8, 8, d), axis=0)

        @pl.when(i == nt - 1)
        def _():
            dw_ref[...] = jnp.sum(acc_ref[...], axis=0, keepdims=True)

    row = pl.BlockSpec((tl, d), lambda i: (i, 0))
    vec = pl.BlockSpec((1, d), lambda i: (0, 0))
    ins = [g, y, w.reshape(1, d)] + ([resid] if has_resid else [])
    return pl.pallas_call(
        body, grid=(nt,), in_specs=[row, row, vec] + ([row] if has_resid else []),
        out_specs=[row, vec],
        out_shape=[jax.ShapeDtypeStruct((l, d), out_dtype), jax.ShapeDtypeStruct((1, d), F32)],
        scratch_shapes=[pltpu.VMEM((8, d), F32)], compiler_params=_params("arbitrary"), name=name,
    )(*ins)


def _loss_grad(y, t, name):
    l, d = y.shape
    tl = _row_tile(l)

    def body(y_ref, t_ref, dy_ref, ls_ref):
        @pl.when(pl.program_id(0) == 0)
        def _():
            ls_ref[...] = jnp.zeros_like(ls_ref)

        e = y_ref[...] - t_ref[...]
        dy_ref[...] = e * (1.0 / d)
        ls_ref[...] += jnp.sum((e * e).reshape(tl // 8, 8, d), axis=0)

    row = pl.BlockSpec((tl, d), lambda i: (i, 0))
    return pl.pallas_call(
        body, grid=(l // tl,), in_specs=[row, row],
        out_specs=[row, pl.BlockSpec((8, d), lambda i: (0, 0))],
        out_shape=[jax.ShapeDtypeStruct((l, d), F32), jax.ShapeDtypeStruct((8, d), F32)],
        compiler_params=_params("arbitrary"), name=name,
    )(y, t)


CONV_COLS = 512
HALO = 8
CONV_SUB_ROWS = 64
CONV_SUB_COLS = 256


def _conv_fwd(proj, cw, cb, name):
    l = proj.shape[0]
    tl = _row_tile(l)
    off = SSM_D_INNER // CONV_COLS

    def body(u_ref, halo_ref, w_ref, b_ref, pre_ref, act_ref, ext_ref):
        i = pl.program_id(1)
        ext_ref[0:HALO, :] = jnp.where(i > 0, halo_ref[...], 0.0)
        ext_ref[HALO:HALO + tl, :] = u_ref[...]
        for r0 in range(0, tl, CONV_SUB_ROWS):
            for c0 in range(0, CONV_COLS, CONV_SUB_COLS):
                cs = slice(c0, c0 + CONV_SUB_COLS)
                ext = ext_ref[r0:r0 + CONV_SUB_ROWS + HALO, cs]
                acc = b_ref[:, cs] + w_ref[SSM_CONV - 1:SSM_CONV, cs] * ext[HALO:]
                for k in range(SSM_CONV - 1):
                    acc = acc + w_ref[k:k + 1, cs] * pltpu.roll(ext, SSM_CONV - 1 - k, 0)[HALO:]
                pre_ref[r0:r0 + CONV_SUB_ROWS, cs] = acc
                act_ref[r0:r0 + CONV_SUB_ROWS, cs] = acc * _sigmoid(acc)

    hb = tl // HALO
    out = pl.BlockSpec((tl, CONV_COLS), lambda j, i: (i, j))
    return pl.pallas_call(
        body, grid=(SSM_CONV_DIM // CONV_COLS, l // tl),
        in_specs=[pl.BlockSpec((tl, CONV_COLS), lambda j, i: (i, off + j)),
                  pl.BlockSpec((HALO, CONV_COLS), lambda j, i: (jnp.maximum(i * hb - 1, 0), off + j)),
                  pl.BlockSpec((SSM_CONV, CONV_COLS), lambda j, i: (0, j)),
                  pl.BlockSpec((1, CONV_COLS), lambda j, i: (0, j))],
        out_specs=[out, out],
        out_shape=[jax.ShapeDtypeStruct((l, SSM_CONV_DIM), F32)] * 2,
        scratch_shapes=[pltpu.VMEM((tl + HALO, CONV_COLS), F32)],
        compiler_params=_params("parallel", "arbitrary"), name=name,
    )(proj, proj, cw, cb.reshape(1, SSM_CONV_DIM))


def _conv_bwd(dact, pre, proj, cw, c0, dproj, name):
    l, width = dact.shape
    tl = _row_tile(l)
    nt = l // tl
    pre_off = c0 // CONV_COLS
    u_off = (SSM_D_INNER + c0) // CONV_COLS
    hb = tl // HALO
    last_hb = l // HALO - 1

    def body(da_ref, da_h_ref, p_ref, p_h_ref, u_ref, u_h_ref, w_ref, _, du_ref, dw_ref, db_ref, ext_ref, uext_ref):
        i = pl.program_id(1)

        @pl.when(i == 0)
        def _():
            dw_ref[...] = jnp.zeros_like(dw_ref)
            db_ref[...] = jnp.zeros_like(db_ref)

        def dpre_of(da, p):
            s = _sigmoid(p)
            return da * (s * (1.0 + p * (1.0 - s)))

        ext_ref[0:tl, :] = dpre_of(da_ref[...], p_ref[...])
        ext_ref[tl:tl + HALO, :] = jnp.where(i < nt - 1, dpre_of(da_h_ref[...], p_h_ref[...]), 0.0)
        uext_ref[0:HALO, :] = jnp.where(i > 0, u_h_ref[...], 0.0)
        uext_ref[HALO:HALO + tl, :] = u_ref[...]
        sub = CONV_SUB_ROWS
        for c0 in range(0, CONV_COLS, CONV_SUB_COLS):
            cs = slice(c0, c0 + CONV_SUB_COLS)
            dws = [jnp.zeros((1, CONV_SUB_COLS), F32) for _ in range(SSM_CONV)]
            dbs = jnp.zeros((1, CONV_SUB_COLS), F32)
            for r0 in range(0, tl, sub):
                dext = ext_ref[r0:r0 + sub + HALO, cs]
                uext = uext_ref[r0:r0 + sub + HALO, cs]
                dp = dext[:sub]
                du = w_ref[SSM_CONV - 1:SSM_CONV, cs] * dp
                dws[SSM_CONV - 1] = dws[SSM_CONV - 1] + jnp.sum(dp * uext[HALO:], axis=0, keepdims=True)
                for k in range(SSM_CONV - 1):
                    j = SSM_CONV - 1 - k
                    du = du + w_ref[k:k + 1, cs] * pltpu.roll(dext, sub + HALO - j, 0)[:sub]
                    dws[k] = dws[k] + jnp.sum(dp * pltpu.roll(uext, j, 0)[HALO:], axis=0, keepdims=True)
                dbs = dbs + jnp.sum(dp, axis=0, keepdims=True)
                du_ref[r0:r0 + sub, cs] = du.astype(du_ref.dtype)
            for k in range(SSM_CONV):
                dw_ref[k:k + 1, cs] += dws[k]
            db_ref[:, cs] += dbs

    return pl.pallas_call(
        body, grid=(width // CONV_COLS, nt),
        in_specs=[pl.BlockSpec((tl, CONV_COLS), lambda j, i: (i, j)),
                  pl.BlockSpec((HALO, CONV_COLS), lambda j, i: (jnp.minimum((i + 1) * hb, last_hb), j)),
                  pl.BlockSpec((tl, CONV_COLS), lambda j, i: (i, pre_off + j)),
                  pl.BlockSpec((HALO, CONV_COLS), lambda j, i: (jnp.minimum((i + 1) * hb, last_hb), pre_off + j)),
                  pl.BlockSpec((tl, CONV_COLS), lambda j, i: (i, u_off + j)),
                  pl.BlockSpec((HALO, CONV_COLS), lambda j, i: (jnp.maximum(i * hb - 1, 0), u_off + j)),
                  pl.BlockSpec((SSM_CONV, CONV_COLS), lambda j, i: (0, pre_off + j)),
                  pl.BlockSpec(memory_space=pl.ANY)],
        out_specs=[pl.BlockSpec((tl, CONV_COLS), lambda j, i: (i, u_off + j)),
                   pl.BlockSpec((SSM_CONV, CONV_COLS), lambda j, i: (0, j)),
                   pl.BlockSpec((1, CONV_COLS), lambda j, i: (0, j))],
        out_shape=[jax.ShapeDtypeStruct(dproj.shape, dproj.dtype), jax.ShapeDtypeStruct((SSM_CONV, width), F32),
                   jax.ShapeDtypeStruct((1, width), F32)],
        scratch_shapes=[pltpu.VMEM((tl + HALO, CONV_COLS), F32), pltpu.VMEM((tl + HALO, CONV_COLS), F32)],
        input_output_aliases={7: 0}, compiler_params=_params("parallel", "arbitrary"), name=name,
    )(dact, dact, pre, pre, proj, proj, cw, dproj)


DT_COL_BLOCK = (SSM_D_INNER + SSM_CONV_DIM) // SSM_DT_PAD


def _split3(v):
    hi = v.astype(BF16)
    rest = v - hi.astype(F32)
    mid = rest.astype(BF16)
    lo = (rest - mid.astype(F32)).astype(BF16)
    return hi, mid, lo


def _ssd_prep(proj, bias, alog_lanes, name):
    l = proj.shape[0]
    nc = l // SSM_CHUNK
    head_dim_log2 = SSM_HEAD_DIM.bit_length() - 1

    def body(p_ref, b_ref, al_ref, dtb_ref, acsb_ref, dtr_ref, acsr_ref):
        v = p_ref[...] + b_ref[...]
        dt = jnp.maximum(v, 0.0) + jnp.log1p(jnp.exp(-jnp.abs(v)))
        head_of_lane = lax.shift_right_logical(lax.broadcasted_iota(jnp.int32, (SSM_DT_PAD, SSM_D_INNER), 1), head_dim_log2)
        spread = (head_of_lane == lax.broadcasted_iota(jnp.int32, (SSM_DT_PAD, SSM_D_INNER), 0)).astype(BF16)
        dtb = sum(jnp.dot(piece, spread, preferred_element_type=F32) for piece in _split3(dt))
        dtb_ref[...] = dtb
        ri = lax.broadcasted_iota(jnp.int32, (SSM_CHUNK, SSM_CHUNK), 0)
        cj = lax.broadcasted_iota(jnp.int32, (SSM_CHUNK, SSM_CHUNK), 1)
        tri = (ri >= cj).astype(BF16)
        acsb = sum(jnp.dot(tri, piece, preferred_element_type=F32) for piece in _split3(dtb * (-jnp.exp(al_ref[...]))))
        acsb_ref[...] = acsb
        gp = SSM_HPG * SSM_HEAD_DIM
        lane = lax.broadcasted_iota(jnp.int32, (SSM_HPG, gp), 1)
        pick = (lane == lax.broadcasted_iota(jnp.int32, (SSM_HPG, gp), 0) * SSM_HEAD_DIM).astype(BF16)
        for g in range(SSM_GROUPS):
            cols = slice(g * gp, (g + 1) * gp)
            dtr_ref[g] = sum(lax.dot_general(pick, piece, NT_DIMS, preferred_element_type=F32)
                             for piece in _split3(dtb[:, cols]))
            acsr_ref[g] = sum(lax.dot_general(pick, piece, NT_DIMS, preferred_element_type=F32)
                              for piece in _split3(acsb[:, cols]))

    rows = pl.BlockSpec((SSM_GROUPS, SSM_HPG, SSM_CHUNK), lambda c: (0, 0, c))
    dense = pl.BlockSpec((SSM_CHUNK, SSM_D_INNER), lambda c: (c, 0))
    return pl.pallas_call(
        body, grid=(nc,),
        in_specs=[pl.BlockSpec((SSM_CHUNK, SSM_DT_PAD), lambda c: (c, DT_COL_BLOCK)),
                  pl.BlockSpec((1, SSM_DT_PAD), lambda c: (0, 0)),
                  pl.BlockSpec((1, SSM_D_INNER), lambda c: (0, 0))],
        out_specs=[dense, dense, rows, rows],
        out_shape=[jax.ShapeDtypeStruct((l, SSM_D_INNER), F32), jax.ShapeDtypeStruct((l, SSM_D_INNER), F32),
                   jax.ShapeDtypeStruct((SSM_GROUPS, SSM_HPG, l), F32),
                   jax.ShapeDtypeStruct((SSM_GROUPS, SSM_HPG, l), F32)],
        compiler_params=_params("parallel"), name=name,
    )(proj, bias, alog_lanes)


def _dt_bwd(ddt, proj, bias, dproj, name):
    l = proj.shape[0]
    tl = _row_tile(l)

    def body(g_ref, p_ref, b_ref, _, o_ref, db_ref):
        @pl.when(pl.program_id(0) == 0)
        def _():
            db_ref[...] = jnp.zeros_like(db_ref)

        d = g_ref[...] * _sigmoid(p_ref[...] + b_ref[...])
        o_ref[...] = d.astype(o_ref.dtype)
        db_ref[...] += jnp.sum(d, axis=0, keepdims=True)

    return pl.pallas_call(
        body, grid=(l // tl,),
        in_specs=[pl.BlockSpec((tl, SSM_DT_PAD), lambda i: (i, 0)),
                  pl.BlockSpec((tl, SSM_DT_PAD), lambda i: (i, DT_COL_BLOCK)),
                  pl.BlockSpec((1, SSM_DT_PAD), lambda i: (0, 0)),
                  pl.BlockSpec(memory_space=pl.ANY)],
        out_specs=[pl.BlockSpec((tl, SSM_DT_PAD), lambda i: (i, DT_COL_BLOCK)),
                   pl.BlockSpec((1, SSM_DT_PAD), lambda i: (0, 0))],
        out_shape=[jax.ShapeDtypeStruct(dproj.shape, dproj.dtype), jax.ShapeDtypeStruct((1, SSM_DT_PAD), F32)],
        input_output_aliases={3: 0}, compiler_params=_params("arbitrary"), name=name,
    )(ddt, proj, bias, dproj)


GP = SSM_HPG * SSM_HEAD_DIM
HEAD_DIM_LOG2 = SSM_HEAD_DIM.bit_length() - 1
CHUNK_LOG2 = SSM_CHUNK.bit_length() - 1
GPS = 4
B_BLOCK0 = SSM_D_INNER // SSM_STATE
C_BLOCK0 = (SSM_D_INNER + SSM_BC_DIM) // SSM_STATE


def _chunk_iotas():
    ri = lax.broadcasted_iota(jnp.int32, (SSM_CHUNK, SSM_CHUNK), 0)
    cj = lax.broadcasted_iota(jnp.int32, (SSM_CHUNK, SSM_CHUNK), 1)
    return ri, cj


def _head_decay(acsb, acs_r, r, ri, cj):
    pair = acsb[:, (r // 2) * LANES:(r // 2 + 1) * LANES]
    mine_low = r % 2 == 0
    lane = lax.broadcasted_iota(jnp.int32, (1, LANES), 1)
    col = jnp.where((lane < SSM_HEAD_DIM) == mine_low, pair, pltpu.roll(pair, SSM_HEAD_DIM, 1))
    return jnp.exp(jnp.where(ri >= cj, col - acs_r[r:r + 1, :], NEG_INF))


def _head_masked_rows(v, dtype):
    head_of_lane = lax.shift_right_logical(lax.broadcasted_iota(jnp.int32, (1, GP), 1), HEAD_DIM_LOG2)
    return jnp.concatenate([jnp.where(head_of_lane == r, v, 0.0).astype(dtype) for r in range(SSM_HPG)], axis=0)


def _ssd_fwd(xbc, dtb, acsb, acs_r, d_lanes, name):
    l = xbc.shape[0]
    nc = l // SSM_CHUNK

    def body(x_ref, b_ref, c_ref, dtb_ref, acsb_ref, acsr_ref, d_ref, y_ref, hin_ref, h_ref):
        c = pl.program_id(0)
        gi = pl.program_id(1)
        ri, cj = _chunk_iotas()
        for k in range(GPS):
            g = gi * GPS + k
            cols = slice(k * GP, (k + 1) * GP)
            ncols = slice(k * SSM_STATE, (k + 1) * SSM_STATE)

            @pl.when(c == 0)
            def _():
                h_ref[g] = jnp.zeros((SSM_STATE, GP), F32)

            xv = x_ref[:, cols]
            bb = b_ref[:, ncols].astype(BF16)
            cb16 = c_ref[:, ncols].astype(BF16)
            acs_v = acsb_ref[:, cols]
            acs_r_v = acsr_ref[k]
            lastb = acs_v[SSM_CHUNK - 1:SSM_CHUNK, :]
            xd = xv * dtb_ref[:, cols]
            cb = lax.dot_general(cb16, bb, NT_DIMS, preferred_element_type=F32)
            hin = h_ref[g]
            hin_ref[0, k] = hin
            yoff = jnp.dot(cb16, hin.astype(BF16), preferred_element_type=F32)
            ms = [(cb * _head_decay(acs_v, acs_r_v, r, ri, cj)).astype(BF16) for r in range(SSM_HPG)]
            ydiag = jnp.dot(jnp.concatenate(ms, axis=1), _head_masked_rows(xd, BF16), preferred_element_type=F32)
            y_ref[:, cols] = ydiag + jnp.exp(acs_v) * yoff + d_ref[k] * xv
            h_ref[g] = hin * jnp.exp(lastb) + _bdot_tn(bb, xd * jnp.exp(lastb - acs_v))

    lanes = pl.BlockSpec((SSM_CHUNK, GPS * GP), lambda c, g: (c, g))
    return pl.pallas_call(
        body, grid=(nc, SSM_GROUPS // GPS),
        in_specs=[lanes,
                  pl.BlockSpec((SSM_CHUNK, GPS * SSM_STATE), lambda c, g: (c, B_BLOCK0 // GPS + g)),
                  pl.BlockSpec((SSM_CHUNK, GPS * SSM_STATE), lambda c, g: (c, C_BLOCK0 // GPS + g)),
                  lanes, lanes,
                  pl.BlockSpec((GPS, SSM_HPG, SSM_CHUNK), lambda c, g: (g, 0, c)),
                  pl.BlockSpec((GPS, 1, GP), lambda c, g: (g, 0, 0))],
        out_specs=[lanes, pl.BlockSpec((1, GPS, SSM_STATE, GP), lambda c, g: (c, g, 0, 0))],
        out_shape=[jax.ShapeDtypeStruct((l, SSM_D_INNER), F32),
                   jax.ShapeDtypeStruct((nc, SSM_GROUPS, SSM_STATE, GP), F32)],
        scratch_shapes=[pltpu.VMEM((SSM_GROUPS, SSM_STATE, GP), F32)],
        compiler_params=_params("arbitrary", "arbitrary"), name=name,
    )(xbc, xbc, xbc, dtb, acsb, acs_r, d_lanes)


def _ssd_bwd(xbc, dtb, acsb, dtr, acs_r, a_log, d_lanes, hin, dy, name):
    l = xbc.shape[0]
    nc = l // SSM_CHUNK

    def body(x_ref, b_ref, c_ref, dtb_ref, acsb_ref, dtr_ref, acsr_ref, alc_ref, d_ref, hin_ref, dy_ref,
             dx_ref, db_ref, dc_ref, ddt_ref, dal_ref, dd_ref, dh_ref):
        c = pl.program_id(0)
        gi = pl.program_id(1)

        @pl.when((c == 0) & (gi == 0))
        def _():
            dal_ref[...] = jnp.zeros_like(dal_ref)
            dd_ref[...] = jnp.zeros_like(dd_ref)

        for k in range(GPS):
            one_group(c, gi * GPS + k, k, x_ref, b_ref, c_ref, dtb_ref, acsb_ref, dtr_ref, acsr_ref, alc_ref, d_ref,
                      hin_ref, dy_ref, dx_ref, db_ref, dc_ref, ddt_ref, dal_ref, dd_ref, dh_ref)

    def one_group(c, g, k, x_ref, b_ref, c_ref, dtb_ref, acsb_ref, dtr_ref, acsr_ref, alc_ref, d_ref, hin_ref, dy_ref,
                  dx_ref, db_ref, dc_ref, ddt_ref, dal_ref, dd_ref, dh_ref):
        cols = slice(k * GP, (k + 1) * GP)
        ncols = slice(k * SSM_STATE, (k + 1) * SSM_STATE)

        @pl.when(c == 0)
        def _():
            dh_ref[g] = jnp.zeros((SSM_STATE, GP), F32)

        xv = x_ref[:, cols]
        dyv = dy_ref[:, cols]
        bb = b_ref[:, ncols].astype(BF16)
        cb16 = c_ref[:, ncols].astype(BF16)
        dtb = dtb_ref[:, cols]
        acsb = acsb_ref[:, cols]
        dtr_v = dtr_ref[k]
        acs_r = acsr_ref[k]
        a_col = -jnp.exp(alc_ref[k])
        ri, cj = _chunk_iotas()
        head_of_lane = lax.shift_right_logical(lax.broadcasted_iota(jnp.int32, (SSM_HPG, GP), 1), HEAD_DIM_LOG2)
        ind_t = (head_of_lane == lax.broadcasted_iota(jnp.int32, (SSM_HPG, GP), 0)).astype(BF16)
        lastb = acsb[SSM_CHUNK - 1:SSM_CHUNK, :]
        ecb = jnp.exp(acsb)
        dteb = jnp.exp(lastb - acsb)
        xd = xv * dtb
        xw = xd * dteb
        cb = lax.dot_general(cb16, bb, NT_DIMS, preferred_element_type=F32)
        hin_v = hin_ref[0, k]
        dhn = dh_ref[g]
        h16 = hin_v.astype(BF16)
        dh16 = dhn.astype(BF16)
        ch = jnp.dot(cb16, h16, preferred_element_type=F32)
        bdh = jnp.dot(bb, dh16, preferred_element_type=F32)
        dym = _head_masked_rows(dyv, BF16)
        g_all = lax.dot_general(dym, xd.astype(BF16), NT_DIMS, preferred_element_type=F32)
        gl_sum = jnp.zeros((SSM_CHUNK, SSM_CHUNK), F32)
        ms, qs = [], []
        for r in range(SSM_HPG):
            decay = _head_decay(acsb, acs_r, r, ri, cj)
            gl = g_all[r * SSM_CHUNK:(r + 1) * SSM_CHUNK] * decay
            gl_sum = gl_sum + gl
            ms.append((cb * decay).astype(BF16))
            qs.append((gl * cb).astype(BF16))
        dxd = lax.dot_general(jnp.concatenate(ms, axis=0), dym, TN_DIMS, preferred_element_type=F32) + dteb * bdh
        cum = jnp.dot(jnp.concatenate(qs, axis=0), (ri < cj).astype(BF16), preferred_element_type=F32)
        sub4 = lax.broadcasted_iota(jnp.int32, (SSM_HPG, 1), 0)
        da = jnp.zeros((SSM_HPG, SSM_CHUNK), F32)
        for r in range(SSM_HPG):
            rect = jnp.sum(jnp.where(ri >= cj, cum[r * SSM_CHUNK:(r + 1) * SSM_CHUNK], 0.0), axis=0, keepdims=True)
            da = da + jnp.where(sub4 == r, rect, 0.0)
        z2 = xw * bdh
        sub8 = lax.broadcasted_iota(jnp.int32, (8, 1), 0)
        col_sums = (jnp.where(sub8 == 0, jnp.sum(z2, axis=0, keepdims=True), 0.0)
                    + jnp.where(sub8 == 1, jnp.sum(dhn * hin_v, axis=0, keepdims=True), 0.0)
                    + jnp.where(sub8 == 2, jnp.sum(dyv * xv, axis=0, keepdims=True), 0.0))
        summands = jnp.concatenate([dyv * ecb * ch - z2, dxd * xv, col_sums], axis=0)
        sums = sum(lax.dot_general(ind_t, piece, NT_DIMS, preferred_element_type=F32) for piece in _split3(summands))
        per_pos = sums[:, :2 * SSM_CHUNK]
        totals = sums[:, 2 * SSM_CHUNK:]
        e_last = totals[:, 0:1] + jnp.exp(acs_r[:, SSM_CHUNK - 1:SSM_CHUNK]) * totals[:, 1:2]
        da = (da + e_last + jnp.dot(per_pos[:, :SSM_CHUNK], (ri >= cj).astype(F32), preferred_element_type=F32,
                                    precision=lax.Precision.HIGHEST))
        ddt_ref[k] = a_col * da + per_pos[:, SSM_CHUNK:]
        dal_ref[g] += a_col * jnp.sum(da * dtr_v, axis=1, keepdims=True)
        dd_ref[g] += totals[:, 2:3]
        dx_ref[:, cols] = dxd * dtb + d_ref[k] * dyv
        w16 = (ecb * dyv).astype(BF16)
        xw16 = xw.astype(BF16)
        gl16 = gl_sum.astype(BF16)
        dc_ref[:, ncols] = (jnp.dot(gl16, bb, preferred_element_type=F32)
                            + lax.dot_general(w16, h16, NT_DIMS, preferred_element_type=F32))
        db_ref[:, ncols] = (lax.dot_general(gl16, cb16, TN_DIMS, preferred_element_type=F32)
                            + lax.dot_general(xw16, dh16, NT_DIMS, preferred_element_type=F32))
        dh_ref[g] = dhn * jnp.exp(lastb) + lax.dot_general(cb16, w16, TN_DIMS, preferred_element_type=F32)

    def rev(c):
        return nc - 1 - c

    small = pl.BlockSpec((SSM_GROUPS, SSM_HPG, 1), lambda c, g: (0, 0, 0))
    lanes = pl.BlockSpec((SSM_CHUNK, GPS * GP), lambda c, g: (rev(c), g))
    rows = pl.BlockSpec((GPS, SSM_HPG, SSM_CHUNK), lambda c, g: (g, 0, rev(c)))
    return pl.pallas_call(
        body, grid=(nc, SSM_GROUPS // GPS),
        in_specs=[lanes,
                  pl.BlockSpec((SSM_CHUNK, GPS * SSM_STATE), lambda c, g: (rev(c), B_BLOCK0 // GPS + g)),
                  pl.BlockSpec((SSM_CHUNK, GPS * SSM_STATE), lambda c, g: (rev(c), C_BLOCK0 // GPS + g)),
                  lanes, lanes, rows, rows,
                  pl.BlockSpec((GPS, SSM_HPG, 1), lambda c, g: (g, 0, 0)),
                  pl.BlockSpec((GPS, 1, GP), lambda c, g: (g, 0, 0)),
                  pl.BlockSpec((1, GPS, SSM_STATE, GP), lambda c, g: (rev(c), g, 0, 0)),
                  lanes],
        out_specs=[lanes,
                   pl.BlockSpec((SSM_CHUNK, GPS * SSM_STATE), lambda c, g: (rev(c), g)),
                   pl.BlockSpec((SSM_CHUNK, GPS * SSM_STATE), lambda c, g: (rev(c), g)),
                   rows, small, small],
        out_shape=[jax.ShapeDtypeStruct((l, SSM_D_INNER), F32), jax.ShapeDtypeStruct((l, SSM_BC_DIM), F32),
                   jax.ShapeDtypeStruct((l, SSM_BC_DIM), F32), jax.ShapeDtypeStruct((SSM_GROUPS, SSM_HPG, l), F32),
                   jax.ShapeDtypeStruct((SSM_GROUPS, SSM_HPG, 1), F32),
                   jax.ShapeDtypeStruct((SSM_GROUPS, SSM_HPG, 1), F32)],
        scratch_shapes=[pltpu.VMEM((SSM_GROUPS, SSM_STATE, GP), F32)],
        compiler_params=_params("arbitrary", "arbitrary"), name=name,
    )(xbc, xbc, xbc, dtb, acsb, dtr, acs_r, a_log.reshape(SSM_GROUPS, SSM_HPG, 1), d_lanes, hin, dy)


def _gatenorm_fwd(y, proj, w, name):
    l = y.shape[0]
    tl = _pick(l, (256, 128))

    def body(y_ref, z_ref, w_ref, o_ref):
        z = z_ref[...]
        yg = y_ref[...] * (z * _sigmoid(z))
        r = lax.rsqrt(jnp.mean(yg * yg, axis=-1, keepdims=True) + EPS)
        o_ref[...] = (yg * r * w_ref[...]).astype(o_ref.dtype)

    row = pl.BlockSpec((tl, SSM_D_INNER), lambda i: (i, 0))
    return pl.pallas_call(
        body, grid=(l // tl,), in_specs=[row, row, pl.BlockSpec((1, SSM_D_INNER), lambda i: (0, 0))],
        out_specs=row, out_shape=jax.ShapeDtypeStruct((l, SSM_D_INNER), BF16),
        compiler_params=_params("parallel"), name=name,
    )(y, proj, w.reshape(1, SSM_D_INNER))


def _gatenorm_bwd(g, y, proj, w, name):
    l = y.shape[0]
    tl = _pick(l, (256, 128))
    nt = l // tl

    def body(g_ref, y_ref, z_ref, w_ref, dy_ref, dz_ref, dw_ref, acc_ref):
        i = pl.program_id(0)

        @pl.when(i == 0)
        def _():
            acc_ref[...] = jnp.zeros_like(acc_ref)

        z = z_ref[...]
        yv = y_ref[...]
        s = _sigmoid(z)
        sz = z * s
        yg = yv * sz
        r = lax.rsqrt(jnp.mean(yg * yg, axis=-1, keepdims=True) + EPS)
        nrm = yg * r
        gv = g_ref[...]
        gw = gv * w_ref[...]
        dyg = r * (gw - nrm * jnp.mean(gw * nrm, axis=-1, keepdims=True))
        dy_ref[...] = dyg * sz
        dz_ref[...] = (dyg * yv * (s * (1.0 + z * (1.0 - s)))).astype(dz_ref.dtype)
        acc_ref[...] += jnp.sum((gv * nrm).reshape(tl // 8, 8, SSM_D_INNER), axis=0)

        @pl.when(i == nt - 1)
        def _():
            dw_ref[...] = jnp.sum(acc_ref[...], axis=0, keepdims=True)

    row = pl.BlockSpec((tl, SSM_D_INNER), lambda i: (i, 0))
    vec = pl.BlockSpec((1, SSM_D_INNER), lambda i: (0, 0))
    return pl.pallas_call(
        body, grid=(nt,), in_specs=[row, row, row, vec], out_specs=[row, row, vec],
        out_shape=[jax.ShapeDtypeStruct((l, SSM_D_INNER), F32), jax.ShapeDtypeStruct((l, SSM_IN_PAD), BF16),
                   jax.ShapeDtypeStruct((1, SSM_D_INNER), F32)],
        scratch_shapes=[pltpu.VMEM((8, SSM_D_INNER), F32)], compiler_params=_params("arbitrary"), name=name,
    )(g, y, proj, w.reshape(1, SSM_D_INNER))


LANES = 128
ROPE_Q_CHUNKS = ATT_WIDTH // LANES
ROPE_K_CHUNKS = ATT_KV_WIDTH // LANES


def _rope_tables(positions):
    inv = ROPE_THETA ** (-jnp.arange(0, ROPE_DIM, 2, dtype=F32) / ROPE_DIM)
    ang = positions.astype(F32)[:, None] * inv
    cos, sin = jnp.cos(ang), jnp.sin(ang)
    l = positions.shape[0]
    rest = ATT_HEAD_DIM - ROPE_DIM
    ones, zeros = jnp.ones((l, rest), F32), jnp.zeros((l, rest), F32)
    z8 = jnp.zeros((l, ROPE_HALF), F32)
    cos_f = jnp.concatenate([cos, cos, ones], axis=1)
    sin_a = jnp.concatenate([-sin, z8, zeros], axis=1)
    sin_b = jnp.concatenate([z8, sin, zeros], axis=1)
    reps = LANES // ATT_HEAD_DIM
    return tuple(jnp.tile(t, (1, reps)) for t in (cos_f, sin_a, sin_b))


ATT_QKV4 = 3 * ATT_WIDTH


def _both_halves(chunk):
    lane = lax.broadcasted_iota(jnp.int32, (1, LANES), 1)
    swapped = pltpu.roll(chunk, ATT_HEAD_DIM, 1)
    return jnp.where(lane < ATT_HEAD_DIM, chunk, swapped), jnp.where(lane < ATT_HEAD_DIM, swapped, chunk)


def _rope_fwd(proj, tables, name):
    l = proj.shape[0]
    tl = _pick(l, (256, 128))

    def body(p_ref, c_ref, sa_ref, sb_ref, o_ref):
        cos_f, sin_a, sin_b = c_ref[...], sa_ref[...], sb_ref[...]

        def rope(t):
            return t * cos_f + pltpu.roll(t, LANES - ROPE_HALF, 1) * sin_a + pltpu.roll(t, ROPE_HALF, 1) * sin_b

        for k in range(ROPE_Q_CHUNKS):
            sl = slice(k * LANES, (k + 1) * LANES)
            o_ref[:, sl] = (rope(p_ref[:, sl]) * Q_SCALE).astype(o_ref.dtype)
        for part in range(2):
            for k in range(ROPE_K_CHUNKS):
                src = ATT_WIDTH + part * ATT_KV_WIDTH + k * LANES
                t = p_ref[:, src:src + LANES]
                if part == 0:
                    t = rope(t)
                for head, dup in enumerate(_both_halves(t.astype(o_ref.dtype))):
                    dst = (1 + part) * ATT_WIDTH + (2 * k + head) * ATT_GQA * ATT_HEAD_DIM
                    o_ref[:, dst:dst + LANES] = dup
                    o_ref[:, dst + LANES:dst + 2 * LANES] = dup

    tab = pl.BlockSpec((tl, LANES), lambda i: (i, 0))
    return pl.pallas_call(
        body, grid=(l // tl,), in_specs=[pl.BlockSpec((tl, ATT_IN_DIM), lambda i: (i, 0)), tab, tab, tab],
        out_specs=pl.BlockSpec((tl, ATT_QKV4), lambda i: (i, 0)),
        out_shape=jax.ShapeDtypeStruct((l, ATT_QKV4), BF16), compiler_params=_params("parallel"), name=name,
    )(proj, *tables)


def _rope_bwd(dq, dk4, dv4, dgate, tables, name):
    l = dq.shape[0]
    tl = _pick(l, (256, 128))

    def body(dq_ref, dk_ref, dv_ref, dg_ref, c_ref, sa_ref, sb_ref, o_ref):
        cos_f, sin_a, sin_b = c_ref[...], sa_ref[...], sb_ref[...]
        lane = lax.broadcasted_iota(jnp.int32, (1, LANES), 1)

        def unrope(t):
            return t * cos_f + pltpu.roll(t * sin_a, ROPE_HALF, 1) + pltpu.roll(t * sin_b, LANES - ROPE_HALF, 1)

        def head_total(ref, kvh):
            base = kvh * ATT_GQA * ATT_HEAD_DIM
            s = ref[:, base:base + LANES] + ref[:, base + LANES:base + 2 * LANES]
            return s + pltpu.roll(s, ATT_HEAD_DIM, 1)

        for k in range(ROPE_Q_CHUNKS):
            sl = slice(k * LANES, (k + 1) * LANES)
            o_ref[:, sl] = unrope(dq_ref[:, sl] * Q_SCALE).astype(o_ref.dtype)
        for k in range(ROPE_K_CHUNKS):
            dk = jnp.where(lane < ATT_HEAD_DIM, head_total(dk_ref, 2 * k), head_total(dk_ref, 2 * k + 1))
            dv = jnp.where(lane < ATT_HEAD_DIM, head_total(dv_ref, 2 * k), head_total(dv_ref, 2 * k + 1))
            o_ref[:, ATT_WIDTH + k * LANES:ATT_WIDTH + (k + 1) * LANES] = unrope(dk).astype(o_ref.dtype)
            at = ATT_WIDTH + ATT_KV_WIDTH + k * LANES
            o_ref[:, at:at + LANES] = dv.astype(o_ref.dtype)
        o_ref[:, ATT_QKV:ATT_IN_DIM] = dg_ref[...].astype(o_ref.dtype)

    tab = pl.BlockSpec((tl, LANES), lambda i: (i, 0))
    wide = pl.BlockSpec((tl, ATT_WIDTH), lambda i: (i, 0))
    return pl.pallas_call(
        body, grid=(l // tl,), in_specs=[wide, wide, wide, wide, tab, tab, tab],
        out_specs=pl.BlockSpec((tl, ATT_IN_DIM), lambda i: (i, 0)),
        out_shape=jax.ShapeDtypeStruct((l, ATT_IN_DIM), BF16), compiler_params=_params("parallel"), name=name,
    )(dq, dk4, dv4, dgate, *tables)


GATE_HALF = ATT_WIDTH // 2
GATE_COL_BLOCK = ATT_QKV // GATE_HALF


ATT_STACK = ATT_GQA * ATT_BLOCK
BLOCK_LOG2 = ATT_BLOCK.bit_length() - 1


def _stack_masks(n):
    ri = lax.broadcasted_iota(jnp.int32, (ATT_STACK, ATT_BLOCK), 0) & (ATT_BLOCK - 1)
    cj = lax.broadcasted_iota(jnp.int32, (ATT_STACK, ATT_BLOCK), 1)
    return (cj > ri) & (n > 0), cj <= ri


def _stack_sinks(sink_ref, kvh):
    blk = lax.shift_right_logical(lax.broadcasted_iota(jnp.int32, (ATT_STACK, 1), 0), BLOCK_LOG2)
    col = jnp.zeros((ATT_STACK, 1), F32)
    for r in range(ATT_GQA):
        col = jnp.where(blk == r, sink_ref[kvh * ATT_GQA + r], col)
    return col


def _stack_fold(stack):
    head_of_lane = lax.shift_right_logical(lax.broadcasted_iota(jnp.int32, (1, GP), 1), HEAD_DIM_LOG2)
    out = jnp.zeros((ATT_BLOCK, GP), F32)
    for r in range(ATT_GQA):
        out = jnp.where(head_of_lane == r, stack[r * ATT_BLOCK:(r + 1) * ATT_BLOCK], out)
    return out


def _attn_fwd(qkv, proj, sinks, name):
    l = qkv.shape[0]
    nb = l // ATT_BLOCK

    def body(sink_ref, q_ref, kp_ref, kc_ref, vp_ref, vc_ref, g0_ref, g1_ref, og_ref, o_ref, lse_ref):
        n = pl.program_id(0)
        mask_p, mask_c = _stack_masks(n)
        ones = jnp.ones((ATT_BLOCK, LANES), BF16)
        for kvh in range(ATT_KV_HEADS):
            cols = slice(kvh * GP, (kvh + 1) * GP)
            q_stack = _head_masked_rows(q_ref[:, cols], BF16)
            sp = jnp.where(mask_p, lax.dot_general(q_stack, kp_ref[:, cols], NT_DIMS, preferred_element_type=F32), NEG_INF)
            sc = jnp.where(mask_c, lax.dot_general(q_stack, kc_ref[:, cols], NT_DIMS, preferred_element_type=F32), NEG_INF)
            sink = _stack_sinks(sink_ref, kvh)
            m = jnp.maximum(jnp.max(jnp.maximum(sp, sc), axis=1, keepdims=True), sink)
            pp = jnp.exp(sp - m).astype(BF16)
            pc = jnp.exp(sc - m).astype(BF16)
            acc = (jnp.dot(pp, jnp.concatenate([vp_ref[:, cols], ones], axis=1), preferred_element_type=F32)
                   + jnp.dot(pc, jnp.concatenate([vc_ref[:, cols], ones], axis=1), preferred_element_type=F32))
            den = acc[:, GP:] + jnp.exp(sink - m)
            inv = 1.0 / den
            o_ref[:, cols] = _stack_fold(acc[:, :GP] * jnp.concatenate([inv, inv], axis=1))
            lse = m + jnp.log(den)
            lse_ref[:, cols] = _stack_fold(jnp.concatenate([lse, lse], axis=1))
        for half, g_ref in enumerate((g0_ref, g1_ref)):
            sl = slice(half * GATE_HALF, (half + 1) * GATE_HALF)
            gate = g_ref[...]
            og_ref[:, sl] = (o_ref[:, sl] * (gate * _sigmoid(gate))).astype(og_ref.dtype)

    def prev(n):
        return jnp.maximum(n - 1, 0)

    wide = pl.BlockSpec((ATT_BLOCK, ATT_WIDTH), lambda n: (n, 0))
    return pl.pallas_call(
        body, grid=(nb,),
        in_specs=[pl.BlockSpec(memory_space=pltpu.SMEM), wide,
                  pl.BlockSpec((ATT_BLOCK, ATT_WIDTH), lambda n: (prev(n), 1)),
                  pl.BlockSpec((ATT_BLOCK, ATT_WIDTH), lambda n: (n, 1)),
                  pl.BlockSpec((ATT_BLOCK, ATT_WIDTH), lambda n: (prev(n), 2)),
                  pl.BlockSpec((ATT_BLOCK, ATT_WIDTH), lambda n: (n, 2)),
                  pl.BlockSpec((ATT_BLOCK, GATE_HALF), lambda n: (n, GATE_COL_BLOCK)),
                  pl.BlockSpec((ATT_BLOCK, GATE_HALF), lambda n: (n, GATE_COL_BLOCK + 1))],
        out_specs=[wide, wide, wide],
        out_shape=[jax.ShapeDtypeStruct((l, ATT_WIDTH), BF16), jax.ShapeDtypeStruct((l, ATT_WIDTH), F32),
                   jax.ShapeDtypeStruct((l, ATT_WIDTH), F32)],
        compiler_params=_params("parallel"), name=name,
    )(sinks, qkv, qkv, qkv, qkv, qkv, proj, proj)


def _attn_bwd(qkv, proj, sinks, o, lse, dog, name):
    l = qkv.shape[0]
    nb = l // ATT_BLOCK

    def body(sink_ref, q_ref, kp_ref, kc_ref, vp_ref, vc_ref, g0_ref, g1_ref, o_ref, lse_ref, dog_ref,
             dq_ref, dk_ref, dv_ref, dg_ref, ds_ref, ck_ref, cv_ref, do_ref):
        n = pl.program_id(0)

        @pl.when(n == 0)
        def _():
            ds_ref[...] = jnp.zeros_like(ds_ref)
            ck_ref[...] = jnp.zeros_like(ck_ref)
            cv_ref[...] = jnp.zeros_like(cv_ref)

        @pl.when(n == nb)
        def _():
            dk_ref[...] = ck_ref[...]
            dv_ref[...] = cv_ref[...]

        @pl.when(n < nb)
        def _():
            mask_p, mask_c = _stack_masks(n)
            lane = lax.broadcasted_iota(jnp.int32, (1, ATT_Q_HEADS), 1)
            for half, g_ref in enumerate((g0_ref, g1_ref)):
                sl = slice(half * GATE_HALF, (half + 1) * GATE_HALF)
                gate = g_ref[...]
                s = _sigmoid(gate)
                dogv = dog_ref[:, sl]
                do_ref[:, sl] = dogv * (gate * s)
                dg_ref[:, sl] = dogv * o_ref[:, sl] * (s * (1.0 + gate * (1.0 - s)))
            ds_acc = jnp.zeros((1, ATT_Q_HEADS), F32)
            for kvh in range(ATT_KV_HEADS):
                cols = slice(kvh * GP, (kvh + 1) * GP)
                kp, kc, vp, vc = kp_ref[:, cols], kc_ref[:, cols], vp_ref[:, cols], vc_ref[:, cols]
                q_stack = _head_masked_rows(q_ref[:, cols], BF16)
                do_g = do_ref[:, cols]
                do_stack = _head_masked_rows(do_g, BF16)
                lse_g = lse_ref[:, cols]
                lse_stack = jnp.concatenate(
                    [_both_halves(lse_g[:, (r // 2) * LANES:(r // 2 + 1) * LANES])[r % 2] for r in range(ATT_GQA)], axis=0)
                pp = jnp.exp(jnp.where(
                    mask_p, lax.dot_general(q_stack, kp, NT_DIMS, preferred_element_type=F32) - lse_stack, NEG_INF))
                pc = jnp.exp(jnp.where(
                    mask_c, lax.dot_general(q_stack, kc, NT_DIMS, preferred_element_type=F32) - lse_stack, NEG_INF))
                dpp = lax.dot_general(do_stack, vp, NT_DIMS, preferred_element_type=F32)
                dpc = lax.dot_general(do_stack, vc, NT_DIMS, preferred_element_type=F32)
                delta = jnp.sum(pp * dpp + pc * dpc, axis=1, keepdims=True)
                dsp = (pp * (dpp - delta)).astype(BF16)
                dsc = (pc * (dpc - delta)).astype(BF16)
                dq_ref[:, cols] = _stack_fold(jnp.dot(dsp, kp, preferred_element_type=F32)
                                              + jnp.dot(dsc, kc, preferred_element_type=F32))
                dk_ref[:, cols] = ck_ref[:, cols] + lax.dot_general(dsp, q_stack, TN_DIMS, preferred_element_type=F32)
                dv_ref[:, cols] = cv_ref[:, cols] + lax.dot_general(pp.astype(BF16), do_stack, TN_DIMS,
                                                                    preferred_element_type=F32)
                ck_ref[:, cols] = lax.dot_general(dsc, q_stack, TN_DIMS, preferred_element_type=F32)
                cv_ref[:, cols] = lax.dot_general(pc.astype(BF16), do_stack, TN_DIMS, preferred_element_type=F32)
                t = jnp.exp(_stack_sinks(sink_ref, kvh) - lse_stack) * delta
                for r in range(ATT_GQA):
                    tot = jnp.sum(t[r * ATT_BLOCK:(r + 1) * ATT_BLOCK], axis=0, keepdims=True)
                    ds_acc = ds_acc - jnp.where(lane == kvh * ATT_GQA + r, tot[:, :ATT_Q_HEADS], 0.0)
            ds_ref[...] += ds_acc

    def cur(n):
        return jnp.minimum(n, nb - 1)

    def prev(n):
        return jnp.maximum(n - 1, 0)

    wide = pl.BlockSpec((ATT_BLOCK, ATT_WIDTH), lambda n: (cur(n), 0))
    late = pl.BlockSpec((ATT_BLOCK, ATT_WIDTH), lambda n: (prev(n), 0))
    return pl.pallas_call(
        body, grid=(nb + 1,),
        in_specs=[pl.BlockSpec(memory_space=pltpu.SMEM), wide,
                  pl.BlockSpec((ATT_BLOCK, ATT_WIDTH), lambda n: (prev(cur(n)), 1)),
                  pl.BlockSpec((ATT_BLOCK, ATT_WIDTH), lambda n: (cur(n), 1)),
                  pl.BlockSpec((ATT_BLOCK, ATT_WIDTH), lambda n: (prev(cur(n)), 2)),
                  pl.BlockSpec((ATT_BLOCK, ATT_WIDTH), lambda n: (cur(n), 2)),
                  pl.BlockSpec((ATT_BLOCK, GATE_HALF), lambda n: (cur(n), GATE_COL_BLOCK)),
                  pl.BlockSpec((ATT_BLOCK, GATE_HALF), lambda n: (cur(n), GATE_COL_BLOCK + 1)),
                  wide, wide, wide],
        out_specs=[wide, late, late, wide, pl.BlockSpec((1, ATT_Q_HEADS), lambda n: (0, 0))],
        out_shape=[jax.ShapeDtypeStruct((l, ATT_WIDTH), F32), jax.ShapeDtypeStruct((l, ATT_WIDTH), F32),
                   jax.ShapeDtypeStruct((l, ATT_WIDTH), F32), jax.ShapeDtypeStruct((l, ATT_WIDTH), F32),
                   jax.ShapeDtypeStruct((1, ATT_Q_HEADS), F32)],
        scratch_shapes=[pltpu.VMEM((ATT_BLOCK, ATT_WIDTH), F32), pltpu.VMEM((ATT_BLOCK, ATT_WIDTH), F32),
                        pltpu.VMEM((ATT_BLOCK, ATT_WIDTH), F32)],
        compiler_params=_params("arbitrary"), name=name,
    )(sinks, qkv, qkv, qkv, qkv, qkv, proj, proj, o, lse, dog)


def _local_step(x, positions, pre_norm, post_norm, w_ssm_in, conv_w, conv_b, dt_bias, a_log, d_skip, gate_norm,
                w_ssm_out, w_att_in, sinks, w_att_out, target):
    tables = _rope_tables(positions)
    dt_bias_pad = jnp.pad(dt_bias, ((0, 0), (0, SSM_DT_PAD - SSM_HEADS)))
    d_lanes = jnp.repeat(d_skip, SSM_HEAD_DIM, axis=1).reshape(-1, SSM_GROUPS, 1, GP)
    alog_lanes = jnp.repeat(a_log, SSM_HEAD_DIM, axis=1)
    saved = []
    cur = x
    for i in range(DEPTH):
        j = i // 2
        h = _rmsnorm_fwd(cur, pre_norm[i], f"prenorm_fwd_{i}")
        if i % 2 == 0:
            proj = _matmul(h, w_ssm_in[j], "nn", F32, f"ssm_in_{i}")
            pre, xbc = _conv_fwd(proj, conv_w[j], conv_b[j], f"conv_fwd_{i}")
            dtb, acsb, dtr, acs_r = _ssd_prep(proj, dt_bias_pad[j:j + 1], alog_lanes[j:j + 1], f"ssd_prep_{i}")
            y, hin = _ssd_fwd(xbc, dtb, acsb, acs_r, d_lanes[j], f"ssd_fwd_{i}")
            act = _gatenorm_fwd(y, proj, gate_norm[j], f"gatenorm_fwd_{i}")
            ymix = _matmul(act, w_ssm_out[j], "nn", F32, f"ssm_out_{i}")
            saved.append(dict(x=cur, h=h, proj=proj, pre=pre, xbc=xbc, dtb=dtb, acsb=acsb, dtr=dtr, acs_r=acs_r, y=y,
                              hin=hin, act=act, ymix=ymix))
        else:
            proj = _matmul(h, w_att_in[j], "nn", F32, f"att_in_{i}")
            qkv = _rope_fwd(proj, tables, f"rope_fwd_{i}")
            act, o, lse = _attn_fwd(qkv, proj, sinks[j], f"attn_fwd_{i}")
            ymix = _matmul(act, w_att_out[j], "nn", F32, f"att_out_{i}")
            saved.append(dict(x=cur, h=h, proj=proj, qkv=qkv, o=o, lse=lse, act=act, ymix=ymix))
        cur = _post_fwd(cur, ymix, post_norm[i], f"post_fwd_{i}")

    g, loss_lanes = _loss_grad(cur, target, "loss")

    gr = {k: [None] * 2 for k in ("ssm_w_in", "ssm_conv_w", "ssm_conv_b", "ssm_dt_bias", "ssm_a_log", "ssm_d",
                                  "ssm_gate_norm", "ssm_w_out", "att_w_in", "att_sinks", "att_w_out")}
    gr["pre_norm"] = [None] * DEPTH
    gr["post_norm"] = [None] * DEPTH
    for i in reversed(range(DEPTH)):
        j = i // 2
        s = saved[i]
        dymix, gr["post_norm"][i] = _rmsnorm_bwd(g, s["ymix"], post_norm[i], None, BF16, f"post_bwd_{i}")
        if i % 2 == 0:
            dact = _matmul(dymix, w_ssm_out[j], "nt", F32, f"ssm_out_dx_{i}")
            gr["ssm_w_out"][j] = _matmul(s["act"], dymix, "tn", F32, f"ssm_out_dw_{i}")
            dy, dproj, gr["ssm_gate_norm"][j] = _gatenorm_bwd(dact, s["y"], s["proj"], gate_norm[j], f"gatenorm_bwd_{i}")
            dxs, db, dc, ddt8, dal, dd = _ssd_bwd(s["xbc"], s["dtb"], s["acsb"], s["dtr"], s["acs_r"], a_log[j],
                                                  d_lanes[j], s["hin"], dy, f"ssd_bwd_{i}")
            gr["ssm_a_log"][j] = dal.reshape(SSM_HEADS)
            gr["ssm_d"][j] = dd.reshape(SSM_HEADS)
            l = x.shape[0]
            ddt = jnp.pad(jnp.transpose(ddt8, (2, 0, 1)).reshape(l, SSM_HEADS), ((0, 0), (0, SSM_DT_PAD - SSM_HEADS)))
            dproj, dbias = _dt_bwd(ddt, s["proj"], dt_bias_pad[j:j + 1], dproj, f"dt_bwd_{i}")
            gr["ssm_dt_bias"][j] = dbias[0, :SSM_HEADS]
            dcw, dcb = [], []
            for c0, dpiece, tag in ((0, dxs, "x"), (SSM_D_INNER, db, "b"), (SSM_D_INNER + SSM_BC_DIM, dc, "c")):
                dproj, dw_, db_ = _conv_bwd(dpiece, s["pre"], s["proj"], conv_w[j], c0, dproj, f"conv_bwd_{tag}_{i}")
                dcw.append(dw_)
                dcb.append(db_)
            gr["ssm_conv_w"][j] = jnp.concatenate(dcw, axis=1)
            gr["ssm_conv_b"][j] = jnp.concatenate(dcb, axis=1)[0]
            w_in, key = w_ssm_in[j], "ssm_w_in"
        else:
            dog = _matmul(dymix, w_att_out[j], "nt", F32, f"att_out_dx_{i}")
            gr["att_w_out"][j] = _matmul(s["act"], dymix, "tn", F32, f"att_out_dw_{i}")
            dq, dk, dv, dgate, dsk = _attn_bwd(s["qkv"], s["proj"], sinks[j], s["o"], s["lse"], dog, f"attn_bwd_{i}")
            gr["att_sinks"][j] = dsk[0]
            dproj = _rope_bwd(dq, dk, dv, dgate, tables, f"rope_bwd_{i}")
            w_in, key = w_att_in[j], "att_w_in"
        dh = _matmul(dproj, w_in, "nt", F32, f"in_dx_{i}")
        gr[key][j] = _matmul(s["h"], dproj, "tn", F32, f"in_dw_{i}")
        g, gr["pre_norm"][i] = _rmsnorm_bwd(dh, s["x"], pre_norm[i], g, F32, f"prenorm_bwd_{i}")
    grads = {k: jnp.stack([v.reshape(v.shape[-1]) if k in ("pre_norm", "post_norm", "ssm_gate_norm") else v for v in vs])
             for k, vs in gr.items()}
    return loss_lanes, g, grads


N_CHIPS = 4
N_DEV = 8
MESH = pl.DeviceIdType.MESH
ANY = pl.BlockSpec(memory_space=pl.ANY)


def _place():
    x, y, c = lax.axis_index("x"), lax.axis_index("y"), lax.axis_index("c")
    return x, y, c, 2 * x + y


def _chip_gather(shards, name):
    n = len(shards)

    def body(*refs):
        ins, outs = refs[:n], refs[n:2 * n]
        send_sems, recv_sems, pass_send_sems, pass_recv_sems, local_sems = refs[2 * n:]
        x, y, c, s = _place()
        local = [pltpu.make_async_copy(ins[w], outs[w].at[s], local_sems.at[w]) for w in range(n)]
        for cp in local:
            cp.start()

        def remote(w, t):
            return pltpu.make_async_remote_copy(
                src_ref=ins[w].at[c], dst_ref=outs[w].at[s, c], send_sem=send_sems.at[w, t],
                recv_sem=recv_sems.at[w, s], device_id=(t // 2, t % 2, c), device_id_type=MESH)

        def arrival(w, t):
            return pltpu.make_async_remote_copy(
                src_ref=ins[w].at[c], dst_ref=outs[w].at[t, c], send_sem=send_sems.at[w, t],
                recv_sem=recv_sems.at[w, t], device_id=(t // 2, t % 2, c), device_id_type=MESH)

        def handed_on(w, t):
            return pltpu.make_async_remote_copy(
                src_ref=outs[w].at[t, c], dst_ref=outs[w].at[t, c], send_sem=pass_send_sems.at[w, t],
                recv_sem=pass_recv_sems.at[w, t], device_id=(x, y, 1 - c), device_id_type=MESH)

        def handed_in(w, t):
            return pltpu.make_async_remote_copy(
                src_ref=outs[w].at[t, 1 - c], dst_ref=outs[w].at[t, 1 - c], send_sem=pass_send_sems.at[w, t],
                recv_sem=pass_recv_sems.at[w, t], device_id=(x, y, 1 - c), device_id_type=MESH)

        for t in range(N_CHIPS):
            @pl.when(s != t)
            def _():
                for w in range(n):
                    remote(w, t).start()
        for t in range(N_CHIPS):
            @pl.when(s != t)
            def _():
                for w in range(n):
                    arrival(w, t).wait_recv()
                    handed_on(w, t).start()
        for t in range(N_CHIPS):
            @pl.when(s != t)
            def _():
                for w in range(n):
                    remote(w, t).wait_send()
                    handed_on(w, t).wait_send()
                    handed_in(w, t).wait_recv()
        for cp in local:
            cp.wait()

    return pl.pallas_call(
        body, in_specs=[ANY] * n, out_specs=[ANY] * n,
        out_shape=[jax.ShapeDtypeStruct((N_CHIPS,) + a.shape, a.dtype) for a in shards],
        scratch_shapes=[pltpu.SemaphoreType.DMA((n, N_CHIPS)), pltpu.SemaphoreType.DMA((n, N_CHIPS)),
                        pltpu.SemaphoreType.DMA((n, N_CHIPS)), pltpu.SemaphoreType.DMA((n, N_CHIPS)),
                        pltpu.SemaphoreType.DMA((n,))],
        name=name,
    )(*shards)


def _pair_swap(parts, name):
    n = len(parts)

    def body(*refs):
        ins, outs = refs[:n], refs[n:2 * n]
        send_sems, recv_sems = refs[2 * n:]
        x, y, c, _ = _place()
        cps = [pltpu.make_async_remote_copy(
            src_ref=ins[w].at[1 - c], dst_ref=outs[w], send_sem=send_sems.at[w], recv_sem=recv_sems.at[w],
            device_id=(x, y, 1 - c), device_id_type=MESH) for w in range(n)]
        for cp in cps:
            cp.start()
        for cp in cps:
            cp.wait()

    return pl.pallas_call(
        body, in_specs=[ANY] * n, out_specs=[ANY] * n,
        out_shape=[jax.ShapeDtypeStruct(a.shape[1:], a.dtype) for a in parts],
        scratch_shapes=[pltpu.SemaphoreType.DMA((n,)), pltpu.SemaphoreType.DMA((n,))],
        name=name,
    )(*parts)


def _chip_scatter(parts, name):
    n = len(parts)
    rows = [a.shape[0] // N_CHIPS for a in parts]

    def body(*refs):
        ins, outs = refs[:n], refs[n:2 * n]
        send_sems, recv_sems, local_sems = refs[2 * n:]
        _, _, c, s = _place()

        def block(w, t):
            return ins[w].at[pl.ds(t * rows[w], rows[w])]

        local = [pltpu.make_async_copy(block(w, s), outs[w].at[s], local_sems.at[w]) for w in range(n)]
        for cp in local:
            cp.start()

        def remote(w, t):
            return pltpu.make_async_remote_copy(
                src_ref=block(w, t), dst_ref=outs[w].at[s], send_sem=send_sems.at[w, t], recv_sem=recv_sems.at[w, s],
                device_id=(t // 2, t % 2, c), device_id_type=MESH)

        def arrival(w, t):
            return pltpu.make_async_remote_copy(
                src_ref=block(w, t), dst_ref=outs[w].at[t], send_sem=send_sems.at[w, t], recv_sem=recv_sems.at[w, t],
                device_id=(t // 2, t % 2, c), device_id_type=MESH)

        for t in range(N_CHIPS):
            @pl.when(s != t)
            def _():
                for w in range(n):
                    remote(w, t).start()
        for t in range(N_CHIPS):
            @pl.when(s != t)
            def _():
                for w in range(n):
                    remote(w, t).wait_send()
                    arrival(w, t).wait_recv()
        for cp in local:
            cp.wait()

    return pl.pallas_call(
        body, in_specs=[ANY] * n, out_specs=[ANY] * n,
        out_shape=[jax.ShapeDtypeStruct((N_CHIPS, r, a.shape[1]), a.dtype) for a, r in zip(parts, rows)],
        scratch_shapes=[pltpu.SemaphoreType.DMA((n, N_CHIPS)), pltpu.SemaphoreType.DMA((n, N_CHIPS)),
                        pltpu.SemaphoreType.DMA((n,))],
        name=name,
    )(*parts)


def _pair_merge(parts, name):
    n = len(parts)

    def body(*refs):
        ins, outs = refs[:n], refs[n:2 * n]
        send_sems, recv_sems = refs[2 * n:]
        x, y, c, _ = _place()
        cps = [pltpu.make_async_remote_copy(
            src_ref=ins[w], dst_ref=outs[w], send_sem=send_sems.at[w], recv_sem=recv_sems.at[w],
            device_id=(x, y, 1 - c), device_id_type=MESH) for w in range(n)]
        for cp in cps:
            cp.start()
        for cp in cps:
            cp.wait()

    return pl.pallas_call(
        body, in_specs=[ANY] * n, out_specs=[ANY] * n,
        out_shape=[jax.ShapeDtypeStruct(a.shape, a.dtype) for a in parts],
        scratch_shapes=[pltpu.SemaphoreType.DMA((n,)), pltpu.SemaphoreType.DMA((n,))],
        name=name,
    )(*parts)


def _all_gather_small(a, name):
    def body(in_ref, out_ref, send_sems, recv_sems, local_sem):
        x, y, c, _ = _place()
        me = 4 * x + 2 * y + c
        local = pltpu.make_async_copy(in_ref, out_ref.at[me], local_sem)
        local.start()

        def remote(d):
            return pltpu.make_async_remote_copy(
                src_ref=in_ref, dst_ref=out_ref.at[me], send_sem=send_sems.at[d], recv_sem=recv_sems.at[me],
                device_id=(d // 4, (d // 2) % 2, d % 2), device_id_type=MESH)

        def arrival(d):
            return pltpu.make_async_remote_copy(
                src_ref=in_ref, dst_ref=out_ref.at[d], send_sem=send_sems.at[d], recv_sem=recv_sems.at[d],
                device_id=(d // 4, (d // 2) % 2, d % 2), device_id_type=MESH)

        for d in range(N_DEV):
            @pl.when(me != d)
            def _():
                remote(d).start()
        for d in range(N_DEV):
            @pl.when(me != d)
            def _():
                remote(d).wait_send()
                arrival(d).wait_recv()
        local.wait()

    return pl.pallas_call(
        body, in_specs=[ANY], out_specs=ANY, out_shape=jax.ShapeDtypeStruct((N_DEV,) + a.shape, a.dtype),
        scratch_shapes=[pltpu.SemaphoreType.DMA((N_DEV,)), pltpu.SemaphoreType.DMA((N_DEV,)), pltpu.SemaphoreType.DMA],
        name=name,
    )(a)


def _reduce_tile(rows):
    return _pick(rows, (256, 16))


def _pair_add(full, other, layer, name):
    _, rows, cols = full.shape
    tr = _reduce_tile(rows)

    def body(layer_ref, a_ref, b_ref, o_ref):
        o_ref[...] = (a_ref[0] + b_ref[...]).astype(o_ref.dtype)

    return pl.pallas_call(
        body,
        grid_spec=pltpu.PrefetchScalarGridSpec(
            num_scalar_prefetch=1, grid=(rows // tr,),
            in_specs=[pl.BlockSpec((1, tr, cols), lambda i, lr: (lr[0], i, 0)), pl.BlockSpec((tr, cols), lambda i, lr: (i, 0))],
            out_specs=pl.BlockSpec((tr, cols), lambda i, lr: (i, 0))),
        out_shape=jax.ShapeDtypeStruct((rows, cols), BF16), compiler_params=_params("parallel"), name=name,
    )(layer, full, other)


def _sum_slots(a, name):
    n, rows, cols = a.shape
    tr = _reduce_tile(rows)

    def body(a_ref, o_ref):
        acc = a_ref[0].astype(F32)
        for k in range(1, n):
            acc = acc + a_ref[k].astype(F32)
        o_ref[...] = acc

    return pl.pallas_call(
        body, grid=(rows // tr,), in_specs=[pl.BlockSpec((n, tr, cols), lambda i: (0, i, 0))],
        out_specs=pl.BlockSpec((tr, cols), lambda i: (i, 0)),
        out_shape=jax.ShapeDtypeStruct((rows, cols), F32), compiler_params=_params("parallel"), name=name,
    )(a)


def _adamw(w, g, m, v, name):
    rows, cols = w.shape
    tr = _pick(rows, (256, 8))

    def body(w_ref, g_ref, m_ref, v_ref, d_ref, nm_ref, nv_ref):
        gv = g_ref[...]
        mn = ADAM_B1 * m_ref[...] + (1.0 - ADAM_B1) * gv
        vn = ADAM_B2 * v_ref[...] + (1.0 - ADAM_B2) * jnp.square(gv)
        m_hat = mn / (1.0 - ADAM_B1 ** ADAM_STEP)
        v_hat = vn / (1.0 - ADAM_B2 ** ADAM_STEP)
        d_ref[...] = -ADAM_LR * (m_hat / (jnp.sqrt(v_hat) + ADAM_EPS) + ADAM_WD * w_ref[...])
        nm_ref[...] = mn
        nv_ref[...] = vn

    blk = pl.BlockSpec((tr, cols), lambda i: (i, 0))
    return pl.pallas_call(
        body, grid=(rows // tr,), in_specs=[blk] * 4, out_specs=[blk] * 3,
        out_shape=[jax.ShapeDtypeStruct((rows, cols), F32)] * 3, compiler_params=_params("parallel"), name=name,
    )(w, g, m, v)


BIG = ("ssm_w_in", "ssm_w_out", "att_w_in", "att_w_out")
SHARDED = BIG + ("ssm_conv_w",)
SMALL = ("pre_norm", "post_norm", "ssm_conv_b", "ssm_dt_bias", "ssm_a_log", "ssm_d", "ssm_gate_norm", "att_sinks")
WEIGHTS = ("pre_norm", "post_norm", "ssm_w_in", "ssm_conv_w", "ssm_conv_b", "ssm_dt_bias", "ssm_a_log", "ssm_d",
           "ssm_gate_norm", "ssm_w_out", "att_w_in", "att_sinks", "att_w_out")


def _cols_to_whole(g):
    _, two, rows, cols = g.shape
    return jnp.transpose(g, (1, 2, 0, 3)).reshape(two, rows, N_CHIPS * cols)


def _rows_to_whole(g):
    _, two, rows, cols = g.shape
    return jnp.transpose(g, (1, 0, 2, 3)).reshape(two, N_CHIPS * rows, cols)


def _cols_by_chip(g):
    two, rows, cols = g.shape
    return jnp.transpose(g.reshape(two, rows, N_CHIPS, cols // N_CHIPS), (0, 2, 1, 3)).reshape(two, N_CHIPS * rows, cols // N_CHIPS)


def _pack_small(tree, keys):
    flat = jnp.concatenate([tree[k].reshape(-1) for k in keys])
    rows = -(-flat.shape[0] // (8 * LANES)) * 8
    return jnp.pad(flat, (0, rows * LANES - flat.shape[0])).reshape(rows, LANES)


def _unpack_small(packed, shapes, keys):
    flat = packed.reshape(-1)
    out, at = {}, 0
    for k in keys:
        n = 1
        for dim in shapes[k]:
            n *= dim
        out[k] = flat[at:at + n].reshape(shapes[k])
        at += n
    return out


def kernel(x, positions, pre_norm, post_norm, ssm_w_in, ssm_conv_w, ssm_conv_b, ssm_dt_bias, ssm_a_log, ssm_d, ssm_gate_norm, ssm_w_out, att_w_in, att_sinks, att_w_out, loss_target, m_pre_norm, m_post_norm, m_ssm_w_in, m_ssm_conv_w, m_ssm_conv_b, m_ssm_dt_bias, m_ssm_a_log, m_ssm_d, m_ssm_gate_norm, m_ssm_w_out, m_att_w_in, m_att_sinks, m_att_w_out, v_pre_norm, v_post_norm, v_ssm_w_in, v_ssm_conv_w, v_ssm_conv_b, v_ssm_dt_bias, v_ssm_a_log, v_ssm_d, v_ssm_gate_norm, v_ssm_w_out, v_att_w_in, v_att_sinks, v_att_w_out):
    w = dict(pre_norm=pre_norm, post_norm=post_norm, ssm_w_in=ssm_w_in, ssm_conv_w=ssm_conv_w, ssm_conv_b=ssm_conv_b,
             ssm_dt_bias=ssm_dt_bias, ssm_a_log=ssm_a_log, ssm_d=ssm_d, ssm_gate_norm=ssm_gate_norm, ssm_w_out=ssm_w_out,
             att_w_in=att_w_in, att_sinks=att_sinks, att_w_out=att_w_out)
    m = dict(pre_norm=m_pre_norm, post_norm=m_post_norm, ssm_w_in=m_ssm_w_in, ssm_conv_w=m_ssm_conv_w, ssm_conv_b=m_ssm_conv_b,
             ssm_dt_bias=m_ssm_dt_bias, ssm_a_log=m_ssm_a_log, ssm_d=m_ssm_d, ssm_gate_norm=m_ssm_gate_norm,
             ssm_w_out=m_ssm_w_out, att_w_in=m_att_w_in, att_sinks=m_att_sinks, att_w_out=m_att_w_out)
    v = dict(pre_norm=v_pre_norm, post_norm=v_post_norm, ssm_w_in=v_ssm_w_in, ssm_conv_w=v_ssm_conv_w, ssm_conv_b=v_ssm_conv_b,
             ssm_dt_bias=v_ssm_dt_bias, ssm_a_log=v_ssm_a_log, ssm_d=v_ssm_d, ssm_gate_norm=v_ssm_gate_norm,
             ssm_w_out=v_ssm_w_out, att_w_in=v_att_w_in, att_sinks=v_att_sinks, att_w_out=v_att_w_out)
    c = lax.axis_index("c")
    chip = 2 * lax.axis_index("x") + lax.axis_index("y")

    g_in, g_out, g_ain, g_aout, g_cw = _chip_gather(
        [ssm_w_in.astype(BF16), ssm_w_out.astype(BF16), att_w_in.astype(BF16), att_w_out.astype(BF16), ssm_conv_w],
        "gather_weights")
    w_in_full = jnp.pad(_cols_to_whole(g_in), ((0, 0), (0, 0), (0, SSM_IN_PAD - SSM_IN_DIM)))
    loss_lanes, grad_x, gr = _local_step(
        x[0], positions[0], pre_norm, post_norm, w_in_full, _cols_to_whole(g_cw), ssm_conv_b, ssm_dt_bias, ssm_a_log,
        ssm_d, ssm_gate_norm, _rows_to_whole(g_out), _cols_to_whole(g_ain), att_sinks, _rows_to_whole(g_aout),
        loss_target[0])
    loss = lax.psum(0.5 * jnp.sum(loss_lanes) / D_MODEL, ("x", "y", "c"))

    parts = [_cols_by_chip(gr["ssm_w_in"][:, :, :SSM_IN_DIM]), gr["ssm_w_out"], _cols_by_chip(gr["att_w_in"]),
             gr["att_w_out"]]
    from_sibling = _pair_swap(parts, "reduce_pair_swap")
    layer = jnp.reshape(c, (1,)).astype(jnp.int32)
    chip_sums = [_pair_add(p, o, layer, f"reduce_pair_add_{k}") for k, (p, o) in enumerate(zip(parts, from_sibling))]
    by_chip = _chip_scatter(chip_sums, "reduce_chip_scatter")
    mine = [_sum_slots(a, f"reduce_chip_sum_{k}") for k, a in enumerate(by_chip)]
    theirs = _pair_merge(mine, "reduce_pair_merge")
    grads = {k: jnp.stack([jnp.where(c == 0, a, b), jnp.where(c == 0, b, a)]).reshape(w[k].shape)
             for k, a, b in zip(BIG, mine, theirs)}

    small_keys = SMALL + ("ssm_conv_w",)
    small_shapes = {k: w[k].shape for k in SMALL}
    small_shapes["ssm_conv_w"] = gr["ssm_conv_w"].shape
    small_sum = _sum_slots(_all_gather_small(_pack_small(gr, small_keys), "reduce_small_gather"), "reduce_small_sum")
    grads.update(_unpack_small(small_sum, small_shapes, small_keys))
    conv_cols = ssm_conv_w.shape[2]
    grads["ssm_conv_w"] = lax.dynamic_slice_in_dim(grads["ssm_conv_w"], chip * conv_cols, conv_cols, axis=2)

    delta, new_m, new_v = {}, {}, {}
    for k in SHARDED:
        shp = w[k].shape
        two_d = (shp[0] * shp[1], shp[2])
        d_, m_, v_ = _adamw(w[k].reshape(two_d), grads[k].reshape(two_d), m[k].reshape(two_d), v[k].reshape(two_d),
                            f"adamw_{k}")
        delta[k], new_m[k], new_v[k] = d_.reshape(shp), m_.reshape(shp), v_.reshape(shp)
    d_, m_, v_ = _adamw(_pack_small(w, SMALL), _pack_small(grads, SMALL), _pack_small(m, SMALL), _pack_small(v, SMALL),
                        "adamw_small")
    delta.update(_unpack_small(d_, small_shapes, SMALL))
    new_m.update(_unpack_small(m_, small_shapes, SMALL))
    new_v.update(_unpack_small(v_, small_shapes, SMALL))

    return (loss, grad_x[None], *[grads[k] for k in WEIGHTS], *[delta[k] for k in WEIGHTS],
            *[new_m[k] for k in WEIGHTS], *[new_v[k] for k in WEIGHTS])
```

```python
import functools

import jax
import jax.numpy as jnp
from jax import lax
from jax.experimental import pallas as pl
from jax.experimental.pallas import tpu as pltpu

F32 = jnp.float32
BF16 = jnp.bfloat16
EPS = 1e-6
NEG_INF = float("-inf")

D_MODEL = 1024
DEPTH = 4
SSM_D_INNER = 2048
SSM_HEAD_DIM = 64
SSM_HEADS = 32
SSM_GROUPS = 8
SSM_HPG = 4
SSM_STATE = 128
SSM_CONV = 4
SSM_CHUNK = 128
SSM_BC_DIM = 1024
SSM_CONV_DIM = 4096
SSM_IN_DIM = 6176
SSM_IN_PAD = 6272
SSM_DT_PAD = 128
ATT_HEAD_DIM = 64
ATT_Q_HEADS = 16
ATT_KV_HEADS = 4
ATT_GQA = 4
ATT_WIDTH = 1024
ATT_KV_WIDTH = 256
ATT_IN_DIM = 2560
ATT_QKV = ATT_WIDTH + 2 * ATT_KV_WIDTH
ATT_BLOCK = 128
ROPE_THETA = 500000.0
ROPE_DIM = 16
ROPE_HALF = 8
Q_SCALE = ATT_HEAD_DIM ** -0.5

ADAM_LR = 0.001
ADAM_B1 = 0.9
ADAM_B2 = 0.999
ADAM_EPS = 1e-08
ADAM_WD = 0.01
ADAM_STEP = 10

VMEM_LIMIT_BYTES = 48 * 1024 * 1024
NT_DIMS = (((1,), (1,)), ((), ()))
TN_DIMS = (((0,), (0,)), ((), ()))


def _params(*sem):
    return pltpu.CompilerParams(dimension_semantics=sem, vmem_limit_bytes=VMEM_LIMIT_BYTES)


def _pick(n, cands):
    for c in cands:
        if n % c == 0:
            return c
    return n


def _sigmoid(v):
    return 0.5 * jnp.tanh(0.5 * v) + 0.5


def _bdot(a, b):
    return jnp.dot(a.astype(BF16), b.astype(BF16), preferred_element_type=F32)


def _bdot_nt(a, b):
    return lax.dot_general(a.astype(BF16), b.astype(BF16), NT_DIMS, preferred_element_type=F32)


def _bdot_tn(a, b):
    return lax.dot_general(a.astype(BF16), b.astype(BF16), TN_DIMS, preferred_element_type=F32)


MATMUL_VMEM_BUDGET = 36 * 1024 * 1024


def _matmul_tiles(m, n, k, out_bytes, reduce_rows):
    best = None
    whole = [k] if (not reduce_rows or k <= 2048) else []
    for tk in whole + [c for c in (4096, 2048, 1024, 896, 512) if k % c == 0 and c < k]:
        for tm in (c for c in (1024, 512, 256) if m % c == 0):
            for tn in (c for c in (n, 1280, 1024, 896, 640, 512) if n % c == 0):
                acc = tm * tn * 4 if tk < k else 0
                need = 2 * (2 * tk * (tm + tn) + tm * tn * out_bytes) + acc
                if need <= MATMUL_VMEM_BUDGET and (best is None or tm * tn * min(tk, 2048) > best[0]):
                    best = (tm * tn * min(tk, 2048), tm, tn, tk)
        if best is not None and not reduce_rows:
            break
    return best[1:]


def _matmul(a, b, mode, out_dtype, name):
    if mode == "nn":
        (m, k), n = a.shape, b.shape[1]
    elif mode == "nt":
        (m, k), n = a.shape, b.shape[0]
    else:
        (k, m), n = a.shape, b.shape[1]
    tm, tn, tk = _matmul_tiles(m, n, k, jnp.dtype(out_dtype).itemsize, mode == "tn")
    nk = k // tk
    dims = {"nn": (((1,), (0,)), ((), ())), "nt": NT_DIMS, "tn": TN_DIMS}[mode]

    def body(a_ref, b_ref, o_ref, acc_ref):
        kk = pl.program_id(2)
        part = lax.dot_general(a_ref[...], b_ref[...], dims, preferred_element_type=F32)
        if nk == 1:
            o_ref[...] = part.astype(o_ref.dtype)
        else:
            @pl.when(kk == 0)
            def _():
                acc_ref[...] = part

            @pl.when(kk > 0)
            def _():
                acc_ref[...] += part

            @pl.when(kk == nk - 1)
            def _():
                o_ref[...] = acc_ref[...].astype(o_ref.dtype)

    if mode == "nn":
        a_spec = pl.BlockSpec((tm, tk), lambda j, i, kk: (i, kk))
        b_spec = pl.BlockSpec((tk, tn), lambda j, i, kk: (kk, j))
    elif mode == "nt":
        a_spec = pl.BlockSpec((tm, tk), lambda j, i, kk: (i, kk))
        b_spec = pl.BlockSpec((tn, tk), lambda j, i, kk: (j, kk))
    else:
        a_spec = pl.BlockSpec((tk, tm), lambda j, i, kk: (kk, i))
        b_spec = pl.BlockSpec((tk, tn), lambda j, i, kk: (kk, j))
    return pl.pallas_call(
        body, grid=(n // tn, m // tm, nk), in_specs=[a_spec, b_spec],
        out_specs=pl.BlockSpec((tm, tn), lambda j, i, kk: (i, j)),
        out_shape=jax.ShapeDtypeStruct((m, n), out_dtype),
        scratch_shapes=[pltpu.VMEM((tm, tn), F32)],
        compiler_params=_params("parallel", "parallel", "arbitrary"), name=name,
    )(a, b)


def _row_tile(l):
    return _pick(l, (512, 256, 128))


def _rmsnorm_fwd(x, w, name):
    l, d = x.shape
    tl = _row_tile(l)

    def body(x_ref, w_ref, o_ref):
        xv = x_ref[...]
        r = lax.rsqrt(jnp.mean(xv * xv, axis=-1, keepdims=True) + EPS)
        o_ref[...] = (xv * r * w_ref[...]).astype(o_ref.dtype)

    return pl.pallas_call(
        body, grid=(l // tl,),
        in_specs=[pl.BlockSpec((tl, d), lambda i: (i, 0)), pl.BlockSpec((1, d), lambda i: (0, 0))],
        out_specs=pl.BlockSpec((tl, d), lambda i: (i, 0)),
        out_shape=jax.ShapeDtypeStruct((l, d), BF16), compiler_params=_params("parallel"), name=name,
    )(x, w.reshape(1, d))


def _post_fwd(x, y, w, w_next, name):
    l, d = x.shape
    tl = _row_tile(l)

    def body(x_ref, y_ref, w_ref, wn_ref, o_ref, h_ref):
        yv = y_ref[...]
        r = lax.rsqrt(jnp.mean(yv * yv, axis=-1, keepdims=True) + EPS)
        out = x_ref[...] + yv * r * w_ref[...]
        o_ref[...] = out
        rn = lax.rsqrt(jnp.mean(out * out, axis=-1, keepdims=True) + EPS)
        h_ref[...] = (out * rn * wn_ref[...]).astype(h_ref.dtype)

    row = pl.BlockSpec((tl, d), lambda i: (i, 0))
    vec = pl.BlockSpec((1, d), lambda i: (0, 0))
    return pl.pallas_call(
        body, grid=(l // tl,), in_specs=[row, row, vec, vec], out_specs=[row, row],
        out_shape=[jax.ShapeDtypeStruct((l, d), F32), jax.ShapeDtypeStruct((l, d), BF16)],
        compiler_params=_params("parallel"), name=name,
    )(x, y, w.reshape(1, d), w_next.reshape(1, d))


def _post_loss(x, y, w, t, name):
    l, d = x.shape
    tl = _row_tile(l)
    nt = l // tl

    def body(x_ref, y_ref, w_ref, t_ref, g_ref, dy_ref, ls_ref, dw_ref, acc_ref):
        i = pl.program_id(0)

        @pl.when(i == 0)
        def _():
            ls_ref[...] = jnp.zeros_like(ls_ref)
            acc_ref[...] = jnp.zeros_like(acc_ref)

        yv = y_ref[...]
        r = lax.rsqrt(jnp.mean(yv * yv, axis=-1, keepdims=True) + EPS)
        nrm = yv * r
        e = x_ref[...] + nrm * w_ref[...] - t_ref[...]
        gv = e * (1.0 / d)
        g_ref[...] = gv
        ls_ref[...] += jnp.sum((e * e).reshape(tl // 8, 8, d), axis=0)
        gw = gv * w_ref[...]
        dy_ref[...] = (r * (gw - nrm * jnp.mean(gw * nrm, axis=-1, keepdims=True))).astype(dy_ref.dtype)
        acc_ref[...] += jnp.sum((gv * nrm).reshape(tl // 8, 8, d), axis=0)

        @pl.when(i == nt - 1)
        def _():
            dw_ref[...] = jnp.sum(acc_ref[...], axis=0, keepdims=True)

    row = pl.BlockSpec((tl, d), lambda i: (i, 0))
    vec = pl.BlockSpec((1, d), lambda i: (0, 0))
    return pl.pallas_call(
        body, grid=(nt,), in_specs=[row, row, vec, row],
        out_specs=[row, row, pl.BlockSpec((8, d), lambda i: (0, 0)), vec],
        out_shape=[jax.ShapeDtypeStruct((l, d), F32), jax.ShapeDtypeStruct((l, d), BF16),
                   jax.ShapeDtypeStruct((8, d), F32), jax.ShapeDtypeStruct((1, d), F32)],
        scratch_shapes=[pltpu.VMEM((8, d), F32)], compiler_params=_params("arbitrary"), name=name,
    )(x, y, w.reshape(1, d), t)


def _norm_bwd_chain(dh, x, w_pre, resid, y_prev, w_post_prev, name):
    l, d = x.shape
    tl = _row_tile(l)
    nt = l // tl

    def body(dh_ref, x_ref, wp_ref, r_ref, y_ref, wq_ref, g_ref, dy_ref, dwp_ref, dwq_ref, accp_ref, accq_ref):
        i = pl.program_id(0)

        @pl.when(i == 0)
        def _():
            accp_ref[...] = jnp.zeros_like(accp_ref)
            accq_ref[...] = jnp.zeros_like(accq_ref)

        xv = x_ref[...]
        dhv = dh_ref[...]
        rx = lax.rsqrt(jnp.mean(xv * xv, axis=-1, keepdims=True) + EPS)
        nx = xv * rx
        gw = dhv * wp_ref[...]
        gv = rx * (gw - nx * jnp.mean(gw * nx, axis=-1, keepdims=True)) + r_ref[...]
        g_ref[...] = gv
        accp_ref[...] += jnp.sum((dhv * nx).reshape(tl // 8, 8, d), axis=0)
        yv = y_ref[...]
        ry = lax.rsqrt(jnp.mean(yv * yv, axis=-1, keepdims=True) + EPS)
        ny = yv * ry
        gq = gv * wq_ref[...]
        dy_ref[...] = (ry * (gq - ny * jnp.mean(gq * ny, axis=-1, keepdims=True))).astype(dy_ref.dtype)
        accq_ref[...] += jnp.sum((gv * ny).reshape(tl // 8, 8, d), axis=0)

        @pl.when(i == nt - 1)
        def _():
            dwp_ref[...] = jnp.sum(accp_ref[...], axis=0, keepdims=True)
            dwq_ref[...] = jnp.sum(accq_ref[...], axis=0, keepdims=True)

    row = pl.BlockSpec((tl, d), lambda i: (i, 0))
    vec = pl.BlockSpec((1, d), lambda i: (0, 0))
    return pl.pallas_call(
        body, grid=(nt,), in_specs=[row, row, vec, row, row, vec], out_specs=[row, row, vec, vec],
        out_shape=[jax.ShapeDtypeStruct((l, d), F32), jax.ShapeDtypeStruct((l, d), BF16),
                   jax.ShapeDtypeStruct((1, d), F32), jax.ShapeDtypeStruct((1, d), F32)],
        scratch_shapes=[pltpu.VMEM((8, d), F32), pltpu.VMEM((8, d), F32)],
        compiler_params=_params("arbitrary"), name=name,
    )(dh, x, w_pre.reshape(1, d), resid, y_prev, w_post_prev.reshape(1, d))


def _rmsnorm_bwd(g, y, w, resid, out_dtype, name):
    l, d = y.shape
    tl = _row_tile(l)
    nt = l // tl
    has_resid = resid is not None

    def body(*refs):
        if has_resid:
            g_ref, y_ref, w_ref, r_ref, dy_ref, dw_ref, acc_ref = refs
        else:
            g_ref, y_ref, w_ref, dy_ref, dw_ref, acc_ref = refs
        i = pl.program_id(0)

        @pl.when(i == 0)
        def _():
            acc_ref[...] = jnp.zeros_like(acc_ref)

        yv = y_ref[...]
        gv = g_ref[...].astype(F32)
        r = lax.rsqrt(jnp.mean(yv * yv, axis=-1, keepdims=True) + EPS)
        nrm = yv * r
        gw = gv * w_ref[...]
        dy = r * (gw - nrm * jnp.mean(gw * nrm, axis=-1, keepdims=True))
        if has_resid:
            dy = dy + r_ref[...]
        dy_ref[...] = dy.astype(dy_ref.dtype)
        acc_ref[...] += jnp.sum((gv * nrm).reshape(tl // 8, 8, d), axis=0)

        @pl.when(i == nt - 1)
        def _():
            dw_ref[...] = jnp.sum(acc_ref[...], axis=0, keepdims=True)

    row = pl.BlockSpec((tl, d), lambda i: (i, 0))
    vec = pl.BlockSpec((1, d), lambda i: (0, 0))
    ins = [g, y, w.reshape(1, d)] + ([resid] if has_resid else [])
    return pl.pallas_call(
        body, grid=(nt,), in_specs=[row, row, vec] + ([row] if has_resid else []),
        out_specs=[row, vec],
        out_shape=[jax.ShapeDtypeStruct((l, d), out_dtype), jax.ShapeDtypeStruct((1, d), F32)],
        scratch_shapes=[pltpu.VMEM((8, d), F32)], compiler_params=_params("arbitrary"), name=name,
    )(*ins)


CONV_COLS = 512
HALO = 8
CONV_SUB_ROWS = 64
CONV_SUB_COLS = 256


def _conv_fwd(proj, cw, cb, name):
    l = proj.shape[0]
    tl = _row_tile(l)
    off = SSM_D_INNER // CONV_COLS

    def body(u_ref, halo_ref, w_ref, b_ref, pre_ref, act_ref, ext_ref):
        i = pl.program_id(1)
        ext_ref[0:HALO, :] = jnp.where(i > 0, halo_ref[...], 0.0)
        ext_ref[HALO:HALO + tl, :] = u_ref[...]
        for r0 in range(0, tl, CONV_SUB_ROWS):
            for c0 in range(0, CONV_COLS, CONV_SUB_COLS):
                cs = slice(c0, c0 + CONV_SUB_COLS)
                ext = ext_ref[r0:r0 + CONV_SUB_ROWS + HALO, cs]
                acc = b_ref[:, cs] + w_ref[SSM_CONV - 1:SSM_CONV, cs] * ext[HALO:]
                for k in range(SSM_CONV - 1):
                    acc = acc + w_ref[k:k + 1, cs] * pltpu.roll(ext, SSM_CONV - 1 - k, 0)[HALO:]
                pre_ref[r0:r0 + CONV_SUB_ROWS, cs] = acc
                act_ref[r0:r0 + CONV_SUB_ROWS, cs] = acc * _sigmoid(acc)

    hb = tl // HALO
    out = pl.BlockSpec((tl, CONV_COLS), lambda j, i: (i, j))
    return pl.pallas_call(
        body, grid=(SSM_CONV_DIM // CONV_COLS, l // tl),
        in_specs=[pl.BlockSpec((tl, CONV_COLS), lambda j, i: (i, off + j)),
                  pl.BlockSpec((HALO, CONV_COLS), lambda j, i: (jnp.maximum(i * hb - 1, 0), off + j)),
                  pl.BlockSpec((SSM_CONV, CONV_COLS), lambda j, i: (0, j)),
                  pl.BlockSpec((1, CONV_COLS), lambda j, i: (0, j))],
        out_specs=[out, out],
        out_shape=[jax.ShapeDtypeStruct((l, SSM_CONV_DIM), F32)] * 2,
        scratch_shapes=[pltpu.VMEM((tl + HALO, CONV_COLS), F32)],
        compiler_params=_params("parallel", "arbitrary"), name=name,
    )(proj, proj, cw, cb.reshape(1, SSM_CONV_DIM))


def _conv_bwd(dact, pre, proj, cw, c0, dproj, name):
    l, width = dact.shape
    tl = _row_tile(l)
    nt = l // tl
    pre_off = c0 // CONV_COLS
    u_off = (SSM_D_INNER + c0) // CONV_COLS
    hb = tl // HALO
    last_hb = l // HALO - 1

    def body(da_ref, da_h_ref, p_ref, p_h_ref, u_ref, u_h_ref, w_ref, _, du_ref, dw_ref, db_ref, ext_ref, uext_ref):
        i = pl.program_id(1)

        @pl.when(i == 0)
        def _():
            dw_ref[...] = jnp.zeros_like(dw_ref)
            db_ref[...] = jnp.zeros_like(db_ref)

        def dpre_of(da, p):
            s = _sigmoid(p)
            return da * (s * (1.0 + p * (1.0 - s)))

        ext_ref[0:tl, :] = dpre_of(da_ref[...], p_ref[...])
        ext_ref[tl:tl + HALO, :] = jnp.where(i < nt - 1, dpre_of(da_h_ref[...], p_h_ref[...]), 0.0)
        uext_ref[0:HALO, :] = jnp.where(i > 0, u_h_ref[...], 0.0)
        uext_ref[HALO:HALO + tl, :] = u_ref[...]
        sub = CONV_SUB_ROWS
        for c0 in range(0, CONV_COLS, CONV_SUB_COLS):
            cs = slice(c0, c0 + CONV_SUB_COLS)
            dws = [jnp.zeros((1, CONV_SUB_COLS), F32) for _ in range(SSM_CONV)]
            dbs = jnp.zeros((1, CONV_SUB_COLS), F32)
            for r0 in range(0, tl, sub):
                dext = ext_ref[r0:r0 + sub + HALO, cs]
                uext = uext_ref[r0:r0 + sub + HALO, cs]
                dp = dext[:sub]
                du = w_ref[SSM_CONV - 1:SSM_CONV, cs] * dp
                dws[SSM_CONV - 1] = dws[SSM_CONV - 1] + jnp.sum(dp * uext[HALO:], axis=0, keepdims=True)
                for k in range(SSM_CONV - 1):
                    j = SSM_CONV - 1 - k
                    du = du + w_ref[k:k + 1, cs] * pltpu.roll(dext, sub + HALO - j, 0)[:sub]
                    dws[k] = dws[k] + jnp.sum(dp * pltpu.roll(uext, j, 0)[HALO:], axis=0, keepdims=True)
                dbs = dbs + jnp.sum(dp, axis=0, keepdims=True)
                du_ref[r0:r0 + sub, cs] = du.astype(du_ref.dtype)
            for k in range(SSM_CONV):
                dw_ref[k:k + 1, cs] += dws[k]
            db_ref[:, cs] += dbs

    return pl.pallas_call(
        body, grid=(width // CONV_COLS, nt),
        in_specs=[pl.BlockSpec((tl, CONV_COLS), lambda j, i: (i, j)),
                  pl.BlockSpec((HALO, CONV_COLS), lambda j, i: (jnp.minimum((i + 1) * hb, last_hb), j)),
                  pl.BlockSpec((tl, CONV_COLS), lambda j, i: (i, pre_off + j)),
                  pl.BlockSpec((HALO, CONV_COLS), lambda j, i: (jnp.minimum((i + 1) * hb, last_hb), pre_off + j)),
                  pl.BlockSpec((tl, CONV_COLS), lambda j, i: (i, u_off + j)),
                  pl.BlockSpec((HALO, CONV_COLS), lambda j, i: (jnp.maximum(i * hb - 1, 0), u_off + j)),
                  pl.BlockSpec((SSM_CONV, CONV_COLS), lambda j, i: (0, pre_off + j)),
                  pl.BlockSpec(memory_space=pl.ANY)],
        out_specs=[pl.BlockSpec((tl, CONV_COLS), lambda j, i: (i, u_off + j)),
                   pl.BlockSpec((SSM_CONV, CONV_COLS), lambda j, i: (0, j)),
                   pl.BlockSpec((1, CONV_COLS), lambda j, i: (0, j))],
        out_shape=[jax.ShapeDtypeStruct(dproj.shape, dproj.dtype), jax.ShapeDtypeStruct((SSM_CONV, width), F32),
                   jax.ShapeDtypeStruct((1, width), F32)],
        scratch_shapes=[pltpu.VMEM((tl + HALO, CONV_COLS), F32), pltpu.VMEM((tl + HALO, CONV_COLS), F32)],
        input_output_aliases={7: 0}, compiler_params=_params("parallel", "arbitrary"), name=name,
    )(dact, dact, pre, pre, proj, proj, cw, dproj)


DT_COL_BLOCK = (SSM_D_INNER + SSM_CONV_DIM) // SSM_DT_PAD


def _split3(v):
    hi = v.astype(BF16)
    rest = v - hi.astype(F32)
    mid = rest.astype(BF16)
    lo = (rest - mid.astype(F32)).astype(BF16)
    return hi, mid, lo


def _ssd_prep(proj, bias, alog_lanes, name):
    l = proj.shape[0]
    nc = l // SSM_CHUNK
    head_dim_log2 = SSM_HEAD_DIM.bit_length() - 1

    def body(p_ref, b_ref, al_ref, dtb_ref, acsb_ref, dtr_ref, acsr_ref):
        v = p_ref[...] + b_ref[...]
        dt = jnp.maximum(v, 0.0) + jnp.log1p(jnp.exp(-jnp.abs(v)))
        head_of_lane = lax.shift_right_logical(lax.broadcasted_iota(jnp.int32, (SSM_DT_PAD, SSM_D_INNER), 1), head_dim_log2)
        spread = (head_of_lane == lax.broadcasted_iota(jnp.int32, (SSM_DT_PAD, SSM_D_INNER), 0)).astype(BF16)
        dtb = sum(jnp.dot(piece, spread, preferred_element_type=F32) for piece in _split3(dt)[:2])
        dtb_ref[...] = dtb
        ri = lax.broadcasted_iota(jnp.int32, (SSM_CHUNK, SSM_CHUNK), 0)
        cj = lax.broadcasted_iota(jnp.int32, (SSM_CHUNK, SSM_CHUNK), 1)
        tri = (ri >= cj).astype(BF16)
        acsb = sum(jnp.dot(tri, piece, preferred_element_type=F32) for piece in _split3(dtb * (-jnp.exp(al_ref[...]))))
        acsb_ref[...] = acsb
        gp = SSM_HPG * SSM_HEAD_DIM
        lane = lax.broadcasted_iota(jnp.int32, (SSM_HPG, gp), 1)
        pick = (lane == lax.broadcasted_iota(jnp.int32, (SSM_HPG, gp), 0) * SSM_HEAD_DIM).astype(BF16)
        for g in range(SSM_GROUPS):
            cols = slice(g * gp, (g + 1) * gp)
            dtr_ref[g] = sum(lax.dot_general(pick, piece, NT_DIMS, preferred_element_type=F32)
                             for piece in _split3(dtb[:, cols]))
            acsr_ref[g] = sum(lax.dot_general(pick, piece, NT_DIMS, preferred_element_type=F32)
                              for piece in _split3(acsb[:, cols]))

    rows = pl.BlockSpec((SSM_GROUPS, SSM_HPG, SSM_CHUNK), lambda c: (0, 0, c))
    dense = pl.BlockSpec((SSM_CHUNK, SSM_D_INNER), lambda c: (c, 0))
    return pl.pallas_call(
        body, grid=(nc,),
        in_specs=[pl.BlockSpec((SSM_CHUNK, SSM_DT_PAD), lambda c: (c, DT_COL_BLOCK)),
                  pl.BlockSpec((1, SSM_DT_PAD), lambda c: (0, 0)),
                  pl.BlockSpec((1, SSM_D_INNER), lambda c: (0, 0))],
        out_specs=[dense, dense, rows, rows],
        out_shape=[jax.ShapeDtypeStruct((l, SSM_D_INNER), F32), jax.ShapeDtypeStruct((l, SSM_D_INNER), F32),
                   jax.ShapeDtypeStruct((SSM_GROUPS, SSM_HPG, l), F32),
                   jax.ShapeDtypeStruct((SSM_GROUPS, SSM_HPG, l), F32)],
        compiler_params=_params("parallel"), name=name,
    )(proj, bias, alog_lanes)


def _dt_bwd(ddt, proj, bias, dproj, name):
    l = proj.shape[0]
    tl = _row_tile(l)

    def body(g_ref, p_ref, b_ref, _, o_ref, db_ref):
        @pl.when(pl.program_id(0) == 0)
        def _():
            db_ref[...] = jnp.zeros_like(db_ref)

        d = g_ref[...] * _sigmoid(p_ref[...] + b_ref[...])
        o_ref[...] = d.astype(o_ref.dtype)
        db_ref[...] += jnp.sum(d, axis=0, keepdims=True)

    return pl.pallas_call(
        body, grid=(l // tl,),
        in_specs=[pl.BlockSpec((tl, SSM_DT_PAD), lambda i: (i, 0)),
                  pl.BlockSpec((tl, SSM_DT_PAD), lambda i: (i, DT_COL_BLOCK)),
                  pl.BlockSpec((1, SSM_DT_PAD), lambda i: (0, 0)),
                  pl.BlockSpec(memory_space=pl.ANY)],
        out_specs=[pl.BlockSpec((tl, SSM_DT_PAD), lambda i: (i, DT_COL_BLOCK)),
                   pl.BlockSpec((1, SSM_DT_PAD), lambda i: (0, 0))],
        out_shape=[jax.ShapeDtypeStruct(dproj.shape, dproj.dtype), jax.ShapeDtypeStruct((1, SSM_DT_PAD), F32)],
        input_output_aliases={3: 0}, compiler_params=_params("arbitrary"), name=name,
    )(ddt, proj, bias, dproj)


GP = SSM_HPG * SSM_HEAD_DIM
HEAD_DIM_LOG2 = SSM_HEAD_DIM.bit_length() - 1
CHUNK_LOG2 = SSM_CHUNK.bit_length() - 1
GPS = 8
B_BLOCK0 = SSM_D_INNER // SSM_STATE
C_BLOCK0 = (SSM_D_INNER + SSM_BC_DIM) // SSM_STATE


def _chunk_iotas():
    ri = lax.broadcasted_iota(jnp.int32, (SSM_CHUNK, SSM_CHUNK), 0)
    cj = lax.broadcasted_iota(jnp.int32, (SSM_CHUNK, SSM_CHUNK), 1)
    return ri, cj


def _head_decay(acsb, acs_r, r, ri, cj):
    pair = acsb[:, (r // 2) * LANES:(r // 2 + 1) * LANES]
    mine_low = r % 2 == 0
    lane = lax.broadcasted_iota(jnp.int32, (1, LANES), 1)
    col = jnp.where((lane < SSM_HEAD_DIM) == mine_low, pair, pltpu.roll(pair, SSM_HEAD_DIM, 1))
    return jnp.exp(jnp.where(ri >= cj, col - acs_r[r:r + 1, :], NEG_INF))


def _head_masked_rows(v, dtype):
    head_of_lane = lax.shift_right_logical(lax.broadcasted_iota(jnp.int32, (1, GP), 1), HEAD_DIM_LOG2)
    return jnp.concatenate([jnp.where(head_of_lane == r, v, 0.0).astype(dtype) for r in range(SSM_HPG)], axis=0)


def _ssd_fwd(xbc, dtb, acsb, acs_r, d_lanes, proj, gate_w, name):
    l = xbc.shape[0]
    nc = l // SSM_CHUNK
    assert GPS == SSM_GROUPS

    def body(x_ref, b_ref, c_ref, dtb_ref, acsb_ref, acsr_ref, d_ref, z_ref, gw_ref, y_ref, act_ref, hin_ref, h_ref):
        c = pl.program_id(0)
        ri, cj = _chunk_iotas()
        for k in range(GPS):
            g = k
            cols = slice(k * GP, (k + 1) * GP)
            ncols = slice(k * SSM_STATE, (k + 1) * SSM_STATE)

            @pl.when(c == 0)
            def _():
                h_ref[g] = jnp.zeros((SSM_STATE, GP), F32)

            xv = x_ref[:, cols]
            bb = b_ref[:, ncols].astype(BF16)
            cb16 = c_ref[:, ncols].astype(BF16)
            acs_v = acsb_ref[:, cols]
            acs_r_v = acsr_ref[k]
            lastb = acs_v[SSM_CHUNK - 1:SSM_CHUNK, :]
            xd = xv * dtb_ref[:, cols]
            cb = lax.dot_general(cb16, bb, NT_DIMS, preferred_element_type=F32)
            hin = h_ref[g]
            hin_ref[0, k] = hin
            yoff = jnp.dot(cb16, hin.astype(BF16), preferred_element_type=F32)
            ms = [(cb * _head_decay(acs_v, acs_r_v, r, ri, cj)).astype(BF16) for r in range(SSM_HPG)]
            ydiag = jnp.dot(jnp.concatenate(ms, axis=1), _head_masked_rows(xd, BF16), preferred_element_type=F32)
            y_ref[:, cols] = ydiag + jnp.exp(acs_v) * yoff + d_ref[k] * xv
            h_ref[g] = hin * jnp.exp(lastb) + _bdot_tn(bb, xd * jnp.exp(lastb - acs_v))
        z = z_ref[...]
        yg = y_ref[...] * (z * _sigmoid(z))
        r = lax.rsqrt(jnp.mean(yg * yg, axis=-1, keepdims=True) + EPS)
        act_ref[...] = (yg * r * gw_ref[...]).astype(act_ref.dtype)

    lanes = pl.BlockSpec((SSM_CHUNK, SSM_D_INNER), lambda c: (c, 0))
    return pl.pallas_call(
        body, grid=(nc,),
        in_specs=[lanes,
                  pl.BlockSpec((SSM_CHUNK, SSM_BC_DIM), lambda c: (c, B_BLOCK0 // GPS)),
                  pl.BlockSpec((SSM_CHUNK, SSM_BC_DIM), lambda c: (c, C_BLOCK0 // GPS)),
                  lanes, lanes,
                  pl.BlockSpec((SSM_GROUPS, SSM_HPG, SSM_CHUNK), lambda c: (0, 0, c)),
                  pl.BlockSpec((SSM_GROUPS, 1, GP), lambda c: (0, 0, 0)),
                  lanes, pl.BlockSpec((1, SSM_D_INNER), lambda c: (0, 0))],
        out_specs=[lanes, lanes, pl.BlockSpec((1, SSM_GROUPS, SSM_STATE, GP), lambda c: (c, 0, 0, 0))],
        out_shape=[jax.ShapeDtypeStruct((l, SSM_D_INNER), F32), jax.ShapeDtypeStruct((l, SSM_D_INNER), BF16),
                   jax.ShapeDtypeStruct((nc, SSM_GROUPS, SSM_STATE, GP), F32)],
        scratch_shapes=[pltpu.VMEM((SSM_GROUPS, SSM_STATE, GP), F32)],
        compiler_params=_params("arbitrary"), name=name,
    )(xbc, xbc, xbc, dtb, acsb, acs_r, d_lanes, proj, gate_w.reshape(1, SSM_D_INNER))


def _ssd_bwd(xbc, dtb, acsb, dtr, acs_r, a_log, d_lanes, hin, dact, y, proj, gate_w, name):
    l = xbc.shape[0]
    nc = l // SSM_CHUNK

    def body(x_ref, b_ref, c_ref, dtb_ref, acsb_ref, dtr_ref, acsr_ref, alc_ref, d_ref, hin_ref,
             dact_ref, y_ref, z_ref, gw_ref,
             dx_ref, db_ref, dc_ref, ddt_ref, dal_ref, dd_ref, dproj_ref, dgw_ref, dh_ref, dy_ref, acc_ref):
        c = pl.program_id(0)

        @pl.when(c == 0)
        def _():
            dal_ref[...] = jnp.zeros_like(dal_ref)
            dd_ref[...] = jnp.zeros_like(dd_ref)
            acc_ref[...] = jnp.zeros_like(acc_ref)

        z = z_ref[...]
        yv = y_ref[...]
        s = _sigmoid(z)
        sz = z * s
        yg = yv * sz
        r = lax.rsqrt(jnp.mean(yg * yg, axis=-1, keepdims=True) + EPS)
        nrm = yg * r
        gv = dact_ref[...]
        gw = gv * gw_ref[...]
        dyg = r * (gw - nrm * jnp.mean(gw * nrm, axis=-1, keepdims=True))
        dy_ref[...] = dyg * sz
        dproj_ref[...] = (dyg * yv * (s * (1.0 + z * (1.0 - s)))).astype(dproj_ref.dtype)
        acc_ref[...] += jnp.sum((gv * nrm).reshape(SSM_CHUNK // 8, 8, SSM_D_INNER), axis=0)

        @pl.when(c == nc - 1)
        def _():
            dgw_ref[...] = jnp.sum(acc_ref[...], axis=0, keepdims=True)

        for k in range(GPS):
            one_group(c, k, k, x_ref, b_ref, c_ref, dtb_ref, acsb_ref, dtr_ref, acsr_ref, alc_ref, d_ref,
                      hin_ref, dy_ref, dx_ref, db_ref, dc_ref, ddt_ref, dal_ref, dd_ref, dh_ref)

    def one_group(c, g, k, x_ref, b_ref, c_ref, dtb_ref, acsb_ref, dtr_ref, acsr_ref, alc_ref, d_ref, hin_ref, dy_ref,
                  dx_ref, db_ref, dc_ref, ddt_ref, dal_ref, dd_ref, dh_ref):
        cols = slice(k * GP, (k + 1) * GP)
        ncols = slice(k * SSM_STATE, (k + 1) * SSM_STATE)

        @pl.when(c == 0)
        def _():
            dh_ref[g] = jnp.zeros((SSM_STATE, GP), F32)

        xv = x_ref[:, cols]
        dyv = dy_ref[:, cols]
        bb = b_ref[:, ncols].astype(BF16)
        cb16 = c_ref[:, ncols].astype(BF16)
        dtb = dtb_ref[:, cols]
        acsb = acsb_ref[:, cols]
        dtr_v = dtr_ref[k]
        acs_r = acsr_ref[k]
        a_col = -jnp.exp(alc_ref[k])
        ri, cj = _chunk_iotas()
        head_of_lane = lax.shift_right_logical(lax.broadcasted_iota(jnp.int32, (SSM_HPG, GP), 1), HEAD_DIM_LOG2)
        ind_t = (head_of_lane == lax.broadcasted_iota(jnp.int32, (SSM_HPG, GP), 0)).astype(BF16)
        lastb = acsb[SSM_CHUNK - 1:SSM_CHUNK, :]
        ecb = jnp.exp(acsb)
        dteb = jnp.exp(lastb - acsb)
        xd = xv * dtb
        xw = xd * dteb
        cb = lax.dot_general(cb16, bb, NT_DIMS, preferred_element_type=F32)
        hin_v = hin_ref[0, k]
        dhn = dh_ref[g]
        h16 = hin_v.astype(BF16)
        dh16 = dhn.astype(BF16)
        ch = jnp.dot(cb16, h16, preferred_element_type=F32)
        bdh = jnp.dot(bb, dh16, preferred_element_type=F32)
        dym = _head_masked_rows(dyv, BF16)
        g_all = lax.dot_general(dym, xd.astype(BF16), NT_DIMS, preferred_element_type=F32)
        gl_sum = jnp.zeros((SSM_CHUNK, SSM_CHUNK), F32)
        ms, qs = [], []
        for r in range(SSM_HPG):
            decay = _head_decay(acsb, acs_r, r, ri, cj)
            gl = g_all[r * SSM_CHUNK:(r + 1) * SSM_CHUNK] * decay
            gl_sum = gl_sum + gl
            ms.append((cb * decay).astype(BF16))
            qs.append((gl * cb).astype(BF16))
        dxd = lax.dot_general(jnp.concatenate(ms, axis=0), dym, TN_DIMS, preferred_element_type=F32) + dteb * bdh
        cum = jnp.dot(jnp.concatenate(qs, axis=0), (ri < cj).astype(BF16), preferred_element_type=F32)
        sub4 = lax.broadcasted_iota(jnp.int32, (SSM_HPG, 1), 0)
        da = jnp.zeros((SSM_HPG, SSM_CHUNK), F32)
        for r in range(SSM_HPG):
            rect = jnp.sum(jnp.where(ri >= cj, cum[r * SSM_CHUNK:(r + 1) * SSM_CHUNK], 0.0), axis=0, keepdims=True)
            da = da + jnp.where(sub4 == r, rect, 0.0)
        z2 = xw * bdh
        sub8 = lax.broadcasted_iota(jnp.int32, (8, 1), 0)
        col_sums = (jnp.where(sub8 == 0, jnp.sum(z2, axis=0, keepdims=True), 0.0)
                    + jnp.where(sub8 == 1, jnp.sum(dhn * hin_v, axis=0, keepdims=True), 0.0)
                    + jnp.where(sub8 == 2, jnp.sum(dyv * xv, axis=0, keepdims=True), 0.0))
        summands = jnp.concatenate([dyv * ecb * ch - z2, dxd * xv, col_sums], axis=0)
        sums = sum(lax.dot_general(ind_t, piece, NT_DIMS, preferred_element_type=F32) for piece in _split3(summands))
        per_pos = sums[:, :2 * SSM_CHUNK]
        totals = sums[:, 2 * SSM_CHUNK:]
        e_last = totals[:, 0:1] + jnp.exp(acs_r[:, SSM_CHUNK - 1:SSM_CHUNK]) * totals[:, 1:2]
        da = (da + e_last + jnp.dot(per_pos[:, :SSM_CHUNK], (ri >= cj).astype(F32), preferred_element_type=F32,
                                    precision=lax.Precision.HIGHEST))
        ddt_ref[k] = a_col * da + per_pos[:, SSM_CHUNK:]
        dal_ref[g] += a_col * jnp.sum(da * dtr_v, axis=1, keepdims=True)
        dd_ref[g] += totals[:, 2:3]
        dx_ref[:, cols] = dxd * dtb + d_ref[k] * dyv
        w16 = (ecb * dyv).astype(BF16)
        xw16 = xw.astype(BF16)
        gl16 = gl_sum.astype(BF16)
        dc_ref[:, ncols] = (jnp.dot(gl16, bb, preferred_element_type=F32)
                            + lax.dot_general(w16, h16, NT_DIMS, preferred_element_type=F32))
        db_ref[:, ncols] = (lax.dot_general(gl16, cb16, TN_DIMS, preferred_element_type=F32)
                            + lax.dot_general(xw16, dh16, NT_DIMS, preferred_element_type=F32))
        dh_ref[g] = dhn * jnp.exp(lastb) + lax.dot_general(cb16, w16, TN_DIMS, preferred_element_type=F32)

    def rev(c):
        return nc - 1 - c

    small = pl.BlockSpec((SSM_GROUPS, SSM_HPG, 1), lambda c: (0, 0, 0))
    lanes = pl.BlockSpec((SSM_CHUNK, SSM_D_INNER), lambda c: (rev(c), 0))
    rows = pl.BlockSpec((SSM_GROUPS, SSM_HPG, SSM_CHUNK), lambda c: (0, 0, rev(c)))
    vec = pl.BlockSpec((1, SSM_D_INNER), lambda c: (0, 0))
    return pl.pallas_call(
        body, grid=(nc,),
        in_specs=[lanes,
                  pl.BlockSpec((SSM_CHUNK, SSM_BC_DIM), lambda c: (rev(c), B_BLOCK0 // GPS)),
                  pl.BlockSpec((SSM_CHUNK, SSM_BC_DIM), lambda c: (rev(c), C_BLOCK0 // GPS)),
                  lanes, lanes, rows, rows,
                  pl.BlockSpec((SSM_GROUPS, SSM_HPG, 1), lambda c: (0, 0, 0)),
                  pl.BlockSpec((SSM_GROUPS, 1, GP), lambda c: (0, 0, 0)),
                  pl.BlockSpec((1, SSM_GROUPS, SSM_STATE, GP), lambda c: (rev(c), 0, 0, 0)),
                  lanes, lanes, lanes, vec],
        out_specs=[lanes,
                   pl.BlockSpec((SSM_CHUNK, SSM_BC_DIM), lambda c: (rev(c), 0)),
                   pl.BlockSpec((SSM_CHUNK, SSM_BC_DIM), lambda c: (rev(c), 0)),
                   rows, small, small, lanes, vec],
        out_shape=[jax.ShapeDtypeStruct((l, SSM_D_INNER), F32), jax.ShapeDtypeStruct((l, SSM_BC_DIM), F32),
                   jax.ShapeDtypeStruct((l, SSM_BC_DIM), F32), jax.ShapeDtypeStruct((SSM_GROUPS, SSM_HPG, l), F32),
                   jax.ShapeDtypeStruct((SSM_GROUPS, SSM_HPG, 1), F32),
                   jax.ShapeDtypeStruct((SSM_GROUPS, SSM_HPG, 1), F32),
                   jax.ShapeDtypeStruct((l, SSM_IN_PAD), BF16), jax.ShapeDtypeStruct((1, SSM_D_INNER), F32)],
        scratch_shapes=[pltpu.VMEM((SSM_GROUPS, SSM_STATE, GP), F32), pltpu.VMEM((SSM_CHUNK, SSM_D_INNER), F32),
                        pltpu.VMEM((8, SSM_D_INNER), F32)],
        compiler_params=_params("arbitrary"), name=name,
    )(xbc, xbc, xbc, dtb, acsb, dtr, acs_r, a_log.reshape(SSM_GROUPS, SSM_HPG, 1), d_lanes, hin, dact, y, proj,
      gate_w.reshape(1, SSM_D_INNER))


LANES = 128
ROPE_Q_CHUNKS = ATT_WIDTH // LANES
ROPE_K_CHUNKS = ATT_KV_WIDTH // LANES


def _rope_tables(positions):
    inv = ROPE_THETA ** (-jnp.arange(0, ROPE_DIM, 2, dtype=F32) / ROPE_DIM)
    ang = positions.astype(F32)[:, None] * inv
    cos, sin = jnp.cos(ang), jnp.sin(ang)
    l = positions.shape[0]
    rest = ATT_HEAD_DIM - ROPE_DIM
    ones, zeros = jnp.ones((l, rest), F32), jnp.zeros((l, rest), F32)
    z8 = jnp.zeros((l, ROPE_HALF), F32)
    cos_f = jnp.concatenate([cos, cos, ones], axis=1)
    sin_a = jnp.concatenate([-sin, z8, zeros], axis=1)
    sin_b = jnp.concatenate([z8, sin, zeros], axis=1)
    reps = LANES // ATT_HEAD_DIM
    return tuple(jnp.tile(t, (1, reps)) for t in (cos_f, sin_a, sin_b))


ATT_QKV4 = 3 * ATT_WIDTH


def _both_halves(chunk):
    lane = lax.broadcasted_iota(jnp.int32, (1, LANES), 1)
    swapped = pltpu.roll(chunk, ATT_HEAD_DIM, 1)
    return jnp.where(lane < ATT_HEAD_DIM, chunk, swapped), jnp.where(lane < ATT_HEAD_DIM, swapped, chunk)


def _rope_fwd(proj, tables, name):
    l = proj.shape[0]
    tl = _pick(l, (256, 128))

    def body(p_ref, c_ref, sa_ref, sb_ref, o_ref):
        cos_f, sin_a, sin_b = c_ref[...], sa_ref[...], sb_ref[...]

        def rope(t):
            return t * cos_f + pltpu.roll(t, LANES - ROPE_HALF, 1) * sin_a + pltpu.roll(t, ROPE_HALF, 1) * sin_b

        for k in range(ROPE_Q_CHUNKS):
            sl = slice(k * LANES, (k + 1) * LANES)
            o_ref[:, sl] = (rope(p_ref[:, sl]) * Q_SCALE).astype(o_ref.dtype)
        for part in range(2):
            for k in range(ROPE_K_CHUNKS):
                src = ATT_WIDTH + part * ATT_KV_WIDTH + k * LANES
                t = p_ref[:, src:src + LANES]
                if part == 0:
                    t = rope(t)
                for head, dup in enumerate(_both_halves(t.astype(o_ref.dtype))):
                    dst = (1 + part) * ATT_WIDTH + (2 * k + head) * ATT_GQA * ATT_HEAD_DIM
                    o_ref[:, dst:dst + LANES] = dup
                    o_ref[:, dst + LANES:dst + 2 * LANES] = dup

    tab = pl.BlockSpec((tl, LANES), lambda i: (i, 0))
    return pl.pallas_call(
        body, grid=(l // tl,), in_specs=[pl.BlockSpec((tl, ATT_IN_DIM), lambda i: (i, 0)), tab, tab, tab],
        out_specs=pl.BlockSpec((tl, ATT_QKV4), lambda i: (i, 0)),
        out_shape=jax.ShapeDtypeStruct((l, ATT_QKV4), BF16), compiler_params=_params("parallel"), name=name,
    )(proj, *tables)


def _rope_bwd(dq, dk4, dv4, dgate, tables, name):
    l = dq.shape[0]
    tl = _pick(l, (256, 128))

    def body(dq_ref, dk_ref, dv_ref, dg_ref, c_ref, sa_ref, sb_ref, o_ref):
        cos_f, sin_a, sin_b = c_ref[...], sa_ref[...], sb_ref[...]
        lane = lax.broadcasted_iota(jnp.int32, (1, LANES), 1)

        def unrope(t):
            return t * cos_f + pltpu.roll(t * sin_a, ROPE_HALF, 1) + pltpu.roll(t * sin_b, LANES - ROPE_HALF, 1)

        def head_total(ref, kvh):
            base = kvh * ATT_GQA * ATT_HEAD_DIM
            s = ref[:, base:base + LANES] + ref[:, base + LANES:base + 2 * LANES]
            return s + pltpu.roll(s, ATT_HEAD_DIM, 1)

        for k in range(ROPE_Q_CHUNKS):
            sl = slice(k * LANES, (k + 1) * LANES)
            o_ref[:, sl] = unrope(dq_ref[:, sl] * Q_SCALE).astype(o_ref.dtype)
        for k in range(ROPE_K_CHUNKS):
            dk = jnp.where(lane < ATT_HEAD_DIM, head_total(dk_ref, 2 * k), head_total(dk_ref, 2 * k + 1))
            dv = jnp.where(lane < ATT_HEAD_DIM, head_total(dv_ref, 2 * k), head_total(dv_ref, 2 * k + 1))
            o_ref[:, ATT_WIDTH + k * LANES:ATT_WIDTH + (k + 1) * LANES] = unrope(dk).astype(o_ref.dtype)
            at = ATT_WIDTH + ATT_KV_WIDTH + k * LANES
            o_ref[:, at:at + LANES] = dv.astype(o_ref.dtype)
        o_ref[:, ATT_QKV:ATT_IN_DIM] = dg_ref[...].astype(o_ref.dtype)

    tab = pl.BlockSpec((tl, LANES), lambda i: (i, 0))
    wide = pl.BlockSpec((tl, ATT_WIDTH), lambda i: (i, 0))
    return pl.pallas_call(
        body, grid=(l // tl,), in_specs=[wide, wide, wide, wide, tab, tab, tab],
        out_specs=pl.BlockSpec((tl, ATT_IN_DIM), lambda i: (i, 0)),
        out_shape=jax.ShapeDtypeStruct((l, ATT_IN_DIM), BF16), compiler_params=_params("parallel"), name=name,
    )(dq, dk4, dv4, dgate, *tables)


GATE_HALF = ATT_WIDTH // 2
GATE_COL_BLOCK = ATT_QKV // GATE_HALF


ATT_STACK = ATT_GQA * ATT_BLOCK
BLOCK_LOG2 = ATT_BLOCK.bit_length() - 1


def _stack_masks(n):
    ri = lax.broadcasted_iota(jnp.int32, (ATT_STACK, ATT_BLOCK), 0) & (ATT_BLOCK - 1)
    cj = lax.broadcasted_iota(jnp.int32, (ATT_STACK, ATT_BLOCK), 1)
    return (cj > ri) & (n > 0), cj <= ri


def _stack_sinks(sink_ref, kvh):
    blk = lax.shift_right_logical(lax.broadcasted_iota(jnp.int32, (ATT_STACK, 1), 0), BLOCK_LOG2)
    col = jnp.zeros((ATT_STACK, 1), F32)
    for r in range(ATT_GQA):
        col = jnp.where(blk == r, sink_ref[kvh * ATT_GQA + r], col)
    return col


def _stack_fold(stack):
    head_of_lane = lax.shift_right_logical(lax.broadcasted_iota(jnp.int32, (1, GP), 1), HEAD_DIM_LOG2)
    out = jnp.zeros((ATT_BLOCK, GP), F32)
    for r in range(ATT_GQA):
        out = jnp.where(head_of_lane == r, stack[r * ATT_BLOCK:(r + 1) * ATT_BLOCK], out)
    return out


def _attn_fwd(qkv, proj, sinks, name):
    l = qkv.shape[0]
    nb = l // ATT_BLOCK

    def body(sink_ref, q_ref, kp_ref, kc_ref, vp_ref, vc_ref, g0_ref, g1_ref, og_ref, o_ref, lse_ref):
        n = pl.program_id(0)
        mask_p, mask_c = _stack_masks(n)
        ones = jnp.ones((ATT_BLOCK, LANES), BF16)
        for kvh in range(ATT_KV_HEADS):
            cols = slice(kvh * GP, (kvh + 1) * GP)
            q_stack = _head_masked_rows(q_ref[:, cols], BF16)
            sp = jnp.where(mask_p, lax.dot_general(q_stack, kp_ref[:, cols], NT_DIMS, preferred_element_type=F32), NEG_INF)
            sc = jnp.where(mask_c, lax.dot_general(q_stack, kc_ref[:, cols], NT_DIMS, preferred_element_type=F32), NEG_INF)
            sink = _stack_sinks(sink_ref, kvh)
            m = jnp.maximum(jnp.max(jnp.maximum(sp, sc), axis=1, keepdims=True), sink)
            pp = jnp.exp(sp - m).astype(BF16)
            pc = jnp.exp(sc - m).astype(BF16)
            acc = (jnp.dot(pp, jnp.concatenate([vp_ref[:, cols], ones], axis=1), preferred_element_type=F32)
                   + jnp.dot(pc, jnp.concatenate([vc_ref[:, cols], ones], axis=1), preferred_element_type=F32))
            den = acc[:, GP:] + jnp.exp(sink - m)
            inv = 1.0 / den
            o_ref[:, cols] = _stack_fold(acc[:, :GP] * jnp.concatenate([inv, inv], axis=1))
            lse = m + jnp.log(den)
            lse_ref[:, cols] = _stack_fold(jnp.concatenate([lse, lse], axis=1))
        for half, g_ref in enumerate((g0_ref, g1_ref)):
            sl = slice(half * GATE_HALF, (half + 1) * GATE_HALF)
            gate = g_ref[...]
            og_ref[:, sl] = (o_ref[:, sl] * (gate * _sigmoid(gate))).astype(og_ref.dtype)

    def prev(n):
        return jnp.maximum(n - 1, 0)

    wide = pl.BlockSpec((ATT_BLOCK, ATT_WIDTH), lambda n: (n, 0))
    return pl.pallas_call(
        body, grid=(nb,),
        in_specs=[pl.BlockSpec(memory_space=pltpu.SMEM), wide,
                  pl.BlockSpec((ATT_BLOCK, ATT_WIDTH), lambda n: (prev(n), 1)),
                  pl.BlockSpec((ATT_BLOCK, ATT_WIDTH), lambda n: (n, 1)),
                  pl.BlockSpec((ATT_BLOCK, ATT_WIDTH), lambda n: (prev(n), 2)),
                  pl.BlockSpec((ATT_BLOCK, ATT_WIDTH), lambda n: (n, 2)),
                  pl.BlockSpec((ATT_BLOCK, GATE_HALF), lambda n: (n, GATE_COL_BLOCK)),
                  pl.BlockSpec((ATT_BLOCK, GATE_HALF), lambda n: (n, GATE_COL_BLOCK + 1))],
        out_specs=[wide, wide, wide],
        out_shape=[jax.ShapeDtypeStruct((l, ATT_WIDTH), BF16), jax.ShapeDtypeStruct((l, ATT_WIDTH), F32),
                   jax.ShapeDtypeStruct((l, ATT_WIDTH), F32)],
        compiler_params=_params("parallel"), name=name,
    )(sinks, qkv, qkv, qkv, qkv, qkv, proj, proj)


def _attn_bwd(qkv, proj, sinks, o, lse, dog, name):
    l = qkv.shape[0]
    nb = l // ATT_BLOCK

    def body(sink_ref, q_ref, kp_ref, kc_ref, vp_ref, vc_ref, g0_ref, g1_ref, o_ref, lse_ref, dog_ref,
             dq_ref, dk_ref, dv_ref, dg_ref, ds_ref, ck_ref, cv_ref, do_ref):
        n = pl.program_id(0)

        @pl.when(n == 0)
        def _():
            ds_ref[...] = jnp.zeros_like(ds_ref)
            ck_ref[...] = jnp.zeros_like(ck_ref)
            cv_ref[...] = jnp.zeros_like(cv_ref)

        @pl.when(n == nb)
        def _():
            dk_ref[...] = ck_ref[...]
            dv_ref[...] = cv_ref[...]

        @pl.when(n < nb)
        def _():
            mask_p, mask_c = _stack_masks(n)
            lane = lax.broadcasted_iota(jnp.int32, (1, ATT_Q_HEADS), 1)
            for half, g_ref in enumerate((g0_ref, g1_ref)):
                sl = slice(half * GATE_HALF, (half + 1) * GATE_HALF)
                gate = g_ref[...]
                s = _sigmoid(gate)
                dogv = dog_ref[:, sl]
                do_ref[:, sl] = dogv * (gate * s)
                dg_ref[:, sl] = dogv * o_ref[:, sl] * (s * (1.0 + gate * (1.0 - s)))
            ds_acc = jnp.zeros((1, ATT_Q_HEADS), F32)
            for kvh in range(ATT_KV_HEADS):
                cols = slice(kvh * GP, (kvh + 1) * GP)
                kp, kc, vp, vc = kp_ref[:, cols], kc_ref[:, cols], vp_ref[:, cols], vc_ref[:, cols]
                q_stack = _head_masked_rows(q_ref[:, cols], BF16)
                do_g = do_ref[:, cols]
                do_stack = _head_masked_rows(do_g, BF16)
                lse_g = lse_ref[:, cols]
                lse_stack = jnp.concatenate(
                    [_both_halves(lse_g[:, (r // 2) * LANES:(r // 2 + 1) * LANES])[r % 2] for r in range(ATT_GQA)], axis=0)
                pp = jnp.exp(jnp.where(
                    mask_p, lax.dot_general(q_stack, kp, NT_DIMS, preferred_element_type=F32) - lse_stack, NEG_INF))
                pc = jnp.exp(jnp.where(
                    mask_c, lax.dot_general(q_stack, kc, NT_DIMS, preferred_element_type=F32) - lse_stack, NEG_INF))
                dpp = lax.dot_general(do_stack, vp, NT_DIMS, preferred_element_type=F32)
                dpc = lax.dot_general(do_stack, vc, NT_DIMS, preferred_element_type=F32)
                delta = jnp.sum(pp * dpp + pc * dpc, axis=1, keepdims=True)
                dsp = (pp * (dpp - delta)).astype(BF16)
                dsc = (pc * (dpc - delta)).astype(BF16)
                dq_ref[:, cols] = _stack_fold(jnp.dot(dsp, kp, preferred_element_type=F32)
                                              + jnp.dot(dsc, kc, preferred_element_type=F32))
                dk_ref[:, cols] = ck_ref[:, cols] + lax.dot_general(dsp, q_stack, TN_DIMS, preferred_element_type=F32)
                dv_ref[:, cols] = cv_ref[:, cols] + lax.dot_general(pp.astype(BF16), do_stack, TN_DIMS,
                                                                    preferred_element_type=F32)
                ck_ref[:, cols] = lax.dot_general(dsc, q_stack, TN_DIMS, preferred_element_type=F32)
                cv_ref[:, cols] = lax.dot_general(pc.astype(BF16), do_stack, TN_DIMS, preferred_element_type=F32)
                t = jnp.exp(_stack_sinks(sink_ref, kvh) - lse_stack) * delta
                for r in range(ATT_GQA):
                    tot = jnp.sum(t[r * ATT_BLOCK:(r + 1) * ATT_BLOCK], axis=0, keepdims=True)
                    ds_acc = ds_acc - jnp.where(lane == kvh * ATT_GQA + r, tot[:, :ATT_Q_HEADS], 0.0)
            ds_ref[...] += ds_acc

    def cur(n):
        return jnp.minimum(n, nb - 1)

    def prev(n):
        return jnp.maximum(n - 1, 0)

    wide = pl.BlockSpec((ATT_BLOCK, ATT_WIDTH), lambda n: (cur(n), 0))
    late = pl.BlockSpec((ATT_BLOCK, ATT_WIDTH), lambda n: (prev(n), 0))
    return pl.pallas_call(
        body, grid=(nb + 1,),
        in_specs=[pl.BlockSpec(memory_space=pltpu.SMEM), wide,
                  pl.BlockSpec((ATT_BLOCK, ATT_WIDTH), lambda n: (prev(cur(n)), 1)),
                  pl.BlockSpec((ATT_BLOCK, ATT_WIDTH), lambda n: (cur(n), 1)),
                  pl.BlockSpec((ATT_BLOCK, ATT_WIDTH), lambda n: (prev(cur(n)), 2)),
                  pl.BlockSpec((ATT_BLOCK, ATT_WIDTH), lambda n: (cur(n), 2)),
                  pl.BlockSpec((ATT_BLOCK, GATE_HALF), lambda n: (cur(n), GATE_COL_BLOCK)),
                  pl.BlockSpec((ATT_BLOCK, GATE_HALF), lambda n: (cur(n), GATE_COL_BLOCK + 1)),
                  wide, wide, wide],
        out_specs=[wide, late, late, wide, pl.BlockSpec((1, ATT_Q_HEADS), lambda n: (0, 0))],
        out_shape=[jax.ShapeDtypeStruct((l, ATT_WIDTH), F32), jax.ShapeDtypeStruct((l, ATT_WIDTH), F32),
                   jax.ShapeDtypeStruct((l, ATT_WIDTH), F32), jax.ShapeDtypeStruct((l, ATT_WIDTH), F32),
                   jax.ShapeDtypeStruct((1, ATT_Q_HEADS), F32)],
        scratch_shapes=[pltpu.VMEM((ATT_BLOCK, ATT_WIDTH), F32), pltpu.VMEM((ATT_BLOCK, ATT_WIDTH), F32),
                        pltpu.VMEM((ATT_BLOCK, ATT_WIDTH), F32)],
        compiler_params=_params("arbitrary"), name=name,
    )(sinks, qkv, qkv, qkv, qkv, qkv, proj, proj, o, lse, dog)


def _local_step(x, positions, pre_norm, post_norm, w_ssm_in, conv_w, conv_b, dt_bias, a_log, d_skip, gate_norm,
                w_ssm_out, w_att_in, sinks, w_att_out, target):
    tables = _rope_tables(positions)
    dt_bias_pad = jnp.pad(dt_bias, ((0, 0), (0, SSM_DT_PAD - SSM_HEADS)))
    d_lanes = jnp.repeat(d_skip, SSM_HEAD_DIM, axis=1).reshape(-1, SSM_GROUPS, 1, GP)
    alog_lanes = jnp.repeat(a_log, SSM_HEAD_DIM, axis=1)
    saved = []
    cur = x
    h = _rmsnorm_fwd(cur, pre_norm[0], "prenorm_fwd_0")
    for i in range(DEPTH):
        j = i // 2
        if i % 2 == 0:
            proj = _matmul(h, w_ssm_in[j], "nn", F32, f"ssm_in_{i}")
            pre, xbc = _conv_fwd(proj, conv_w[j], conv_b[j], f"conv_fwd_{i}")
            dtb, acsb, dtr, acs_r = _ssd_prep(proj, dt_bias_pad[j:j + 1], alog_lanes[j:j + 1], f"ssd_prep_{i}")
            y, act, hin = _ssd_fwd(xbc, dtb, acsb, acs_r, d_lanes[j], proj, gate_norm[j], f"ssd_fwd_{i}")
            ymix = _matmul(act, w_ssm_out[j], "nn", F32, f"ssm_out_{i}")
            saved.append(dict(x=cur, h=h, proj=proj, pre=pre, xbc=xbc, dtb=dtb, acsb=acsb, dtr=dtr, acs_r=acs_r, y=y,
                              hin=hin, act=act, ymix=ymix))
        else:
            proj = _matmul(h, w_att_in[j], "nn", F32, f"att_in_{i}")
            qkv = _rope_fwd(proj, tables, f"rope_fwd_{i}")
            act, o, lse = _attn_fwd(qkv, proj, sinks[j], f"attn_fwd_{i}")
            ymix = _matmul(act, w_att_out[j], "nn", F32, f"att_out_{i}")
            saved.append(dict(x=cur, h=h, proj=proj, qkv=qkv, o=o, lse=lse, act=act, ymix=ymix))
        if i + 1 < DEPTH:
            cur, h = _post_fwd(cur, ymix, post_norm[i], pre_norm[i + 1], f"post_fwd_{i}")

    gr = {k: [None] * 2 for k in ("ssm_w_in", "ssm_conv_w", "ssm_conv_b", "ssm_dt_bias", "ssm_a_log", "ssm_d",
                                  "ssm_gate_norm", "ssm_w_out", "att_w_in", "att_sinks", "att_w_out")}
    gr["pre_norm"] = [None] * DEPTH
    gr["post_norm"] = [None] * DEPTH
    last = DEPTH - 1
    g, dymix, loss_lanes, gr["post_norm"][last] = _post_loss(cur, ymix, post_norm[last], target, "post_loss")
    for i in reversed(range(DEPTH)):
        j = i // 2
        s = saved[i]
        if i % 2 == 0:
            dact = _matmul(dymix, w_ssm_out[j], "nt", F32, f"ssm_out_dx_{i}")
            gr["ssm_w_out"][j] = _matmul(s["act"], dymix, "tn", F32, f"ssm_out_dw_{i}")
            dxs, db, dc, ddt8, dal, dd, dproj, gr["ssm_gate_norm"][j] = _ssd_bwd(
                s["xbc"], s["dtb"], s["acsb"], s["dtr"], s["acs_r"], a_log[j], d_lanes[j], s["hin"], dact, s["y"],
                s["proj"], gate_norm[j], f"ssd_bwd_{i}")
            gr["ssm_a_log"][j] = dal.reshape(SSM_HEADS)
            gr["ssm_d"][j] = dd.reshape(SSM_HEADS)
            l = x.shape[0]
            ddt = jnp.pad(jnp.transpose(ddt8, (2, 0, 1)).reshape(l, SSM_HEADS), ((0, 0), (0, SSM_DT_PAD - SSM_HEADS)))
            dproj, dbias = _dt_bwd(ddt, s["proj"], dt_bias_pad[j:j + 1], dproj, f"dt_bwd_{i}")
            gr["ssm_dt_bias"][j] = dbias[0, :SSM_HEADS]
            dcw, dcb = [], []
            for c0, dpiece, tag in ((0, dxs, "x"), (SSM_D_INNER, db, "b"), (SSM_D_INNER + SSM_BC_DIM, dc, "c")):
                dproj, dw_, db_ = _conv_bwd(dpiece, s["pre"], s["proj"], conv_w[j], c0, dproj, f"conv_bwd_{tag}_{i}")
                dcw.append(dw_)
                dcb.append(db_)
            gr["ssm_conv_w"][j] = jnp.concatenate(dcw, axis=1)
            gr["ssm_conv_b"][j] = jnp.concatenate(dcb, axis=1)[0]
            w_in, key = w_ssm_in[j], "ssm_w_in"
        else:
            dog = _matmul(dymix, w_att_out[j], "nt", F32, f"att_out_dx_{i}")
            gr["att_w_out"][j] = _matmul(s["act"], dymix, "tn", F32, f"att_out_dw_{i}")
            dq, dk, dv, dgate, dsk = _attn_bwd(s["qkv"], s["proj"], sinks[j], s["o"], s["lse"], dog, f"attn_bwd_{i}")
            gr["att_sinks"][j] = dsk[0]
            dproj = _rope_bwd(dq, dk, dv, dgate, tables, f"rope_bwd_{i}")
            w_in, key = w_att_in[j], "att_w_in"
        dh = _matmul(dproj, w_in, "nt", F32, f"in_dx_{i}")
        gr[key][j] = _matmul(s["h"], dproj, "tn", F32, f"in_dw_{i}")
        if i > 0:
            g, dymix, gr["pre_norm"][i], gr["post_norm"][i - 1] = _norm_bwd_chain(
                dh, s["x"], pre_norm[i], g, saved[i - 1]["ymix"], post_norm[i - 1], f"norm_bwd_{i}")
        else:
            g, gr["pre_norm"][i] = _rmsnorm_bwd(dh, s["x"], pre_norm[i], g, F32, f"prenorm_bwd_{i}")
    grads = {k: jnp.stack([v.reshape(v.shape[-1]) if k in ("pre_norm", "post_norm", "ssm_gate_norm") else v for v in vs])
             for k, vs in gr.items()}
    return loss_lanes, g, grads


N_CHIPS = 4
N_DEV = 8
MESH = pl.DeviceIdType.MESH
ANY = pl.BlockSpec(memory_space=pl.ANY)


def _place():
    x, y, c = lax.axis_index("x"), lax.axis_index("y"), lax.axis_index("c")
    return x, y, c, 2 * x + y


def _chip_gather(shards, name):
    n = len(shards)

    def body(*refs):
        ins, outs = refs[:n], refs[n:2 * n]
        send_sems, recv_sems, pass_send_sems, pass_recv_sems, local_sems = refs[2 * n:]
        x, y, c, s = _place()
        local = [pltpu.make_async_copy(ins[w], outs[w].at[s], local_sems.at[w]) for w in range(n)]
        for cp in local:
            cp.start()

        def remote(w, t):
            return pltpu.make_async_remote_copy(
                src_ref=ins[w].at[c], dst_ref=outs[w].at[s, c], send_sem=send_sems.at[w, t],
                recv_sem=recv_sems.at[w, s], device_id=(t // 2, t % 2, c), device_id_type=MESH)

        def arrival(w, t):
            return pltpu.make_async_remote_copy(
                src_ref=ins[w].at[c], dst_ref=outs[w].at[t, c], send_sem=send_sems.at[w, t],
                recv_sem=recv_sems.at[w, t], device_id=(t // 2, t % 2, c), device_id_type=MESH)

        def handed_on(w, t):
            return pltpu.make_async_remote_copy(
                src_ref=outs[w].at[t, c], dst_ref=outs[w].at[t, c], send_sem=pass_send_sems.at[w, t],
                recv_sem=pass_recv_sems.at[w, t], device_id=(x, y, 1 - c), device_id_type=MESH)

        def handed_in(w, t):
            return pltpu.make_async_remote_copy(
                src_ref=outs[w].at[t, 1 - c], dst_ref=outs[w].at[t, 1 - c], send_sem=pass_send_sems.at[w, t],
                recv_sem=pass_recv_sems.at[w, t], device_id=(x, y, 1 - c), device_id_type=MESH)

        for t in range(N_CHIPS):
            @pl.when(s != t)
            def _():
                for w in range(n):
                    remote(w, t).start()
        for t in range(N_CHIPS):
            @pl.when(s != t)
            def _():
                for w in range(n):
                    arrival(w, t).wait_recv()
                    handed_on(w, t).start()
        for t in range(N_CHIPS):
            @pl.when(s != t)
            def _():
                for w in range(n):
                    remote(w, t).wait_send()
                    handed_on(w, t).wait_send()
                    handed_in(w, t).wait_recv()
        for cp in local:
            cp.wait()

    return pl.pallas_call(
        body, in_specs=[ANY] * n, out_specs=[ANY] * n,
        out_shape=[jax.ShapeDtypeStruct((N_CHIPS,) + a.shape, a.dtype) for a in shards],
        scratch_shapes=[pltpu.SemaphoreType.DMA((n, N_CHIPS)), pltpu.SemaphoreType.DMA((n, N_CHIPS)),
                        pltpu.SemaphoreType.DMA((n, N_CHIPS)), pltpu.SemaphoreType.DMA((n, N_CHIPS)),
                        pltpu.SemaphoreType.DMA((n,))],
        name=name,
    )(*shards)


def _pair_swap(parts, name):
    n = len(parts)

    def body(*refs):
        ins, outs = refs[:n], refs[n:2 * n]
        send_sems, recv_sems = refs[2 * n:]
        x, y, c, _ = _place()
        cps = [pltpu.make_async_remote_copy(
            src_ref=ins[w].at[1 - c], dst_ref=outs[w], send_sem=send_sems.at[w], recv_sem=recv_sems.at[w],
            device_id=(x, y, 1 - c), device_id_type=MESH) for w in range(n)]
        for cp in cps:
            cp.start()
        for cp in cps:
            cp.wait()

    return pl.pallas_call(
        body, in_specs=[ANY] * n, out_specs=[ANY] * n,
        out_shape=[jax.ShapeDtypeStruct(a.shape[1:], a.dtype) for a in parts],
        scratch_shapes=[pltpu.SemaphoreType.DMA((n,)), pltpu.SemaphoreType.DMA((n,))],
        name=name,
    )(*parts)


def _chip_scatter(parts, name):
    n = len(parts)
    rows = [a.shape[0] // N_CHIPS for a in parts]

    def body(*refs):
        ins, outs = refs[:n], refs[n:2 * n]
        send_sems, recv_sems, local_sems = refs[2 * n:]
        _, _, c, s = _place()

        def block(w, t):
            return ins[w].at[pl.ds(t * rows[w], rows[w])]

        local = [pltpu.make_async_copy(block(w, s), outs[w].at[s], local_sems.at[w]) for w in range(n)]
        for cp in local:
            cp.start()

        def remote(w, t):
            return pltpu.make_async_remote_copy(
                src_ref=block(w, t), dst_ref=outs[w].at[s], send_sem=send_sems.at[w, t], recv_sem=recv_sems.at[w, s],
                device_id=(t // 2, t % 2, c), device_id_type=MESH)

        def arrival(w, t):
            return pltpu.make_async_remote_copy(
                src_ref=block(w, t), dst_ref=outs[w].at[t], send_sem=send_sems.at[w, t], recv_sem=recv_sems.at[w, t],
                device_id=(t // 2, t % 2, c), device_id_type=MESH)

        for t in range(N_CHIPS):
            @pl.when(s != t)
            def _():
                for w in range(n):
                    remote(w, t).start()
        for t in range(N_CHIPS):
            @pl.when(s != t)
            def _():
                for w in range(n):
                    remote(w, t).wait_send()
                    arrival(w, t).wait_recv()
        for cp in local:
            cp.wait()

    return pl.pallas_call(
        body, in_specs=[ANY] * n, out_specs=[ANY] * n,
        out_shape=[jax.ShapeDtypeStruct((N_CHIPS, r, a.shape[1]), a.dtype) for a, r in zip(parts, rows)],
        scratch_shapes=[pltpu.SemaphoreType.DMA((n, N_CHIPS)), pltpu.SemaphoreType.DMA((n, N_CHIPS)),
                        pltpu.SemaphoreType.DMA((n,))],
        name=name,
    )(*parts)


def _pair_merge(parts, name):
    n = len(parts)

    def body(*refs):
        ins, outs = refs[:n], refs[n:2 * n]
        send_sems, recv_sems = refs[2 * n:]
        x, y, c, _ = _place()
        cps = [pltpu.make_async_remote_copy(
            src_ref=ins[w], dst_ref=outs[w], send_sem=send_sems.at[w], recv_sem=recv_sems.at[w],
            device_id=(x, y, 1 - c), device_id_type=MESH) for w in range(n)]
        for cp in cps:
            cp.start()
        for cp in cps:
            cp.wait()

    return pl.pallas_call(
        body, in_specs=[ANY] * n, out_specs=[ANY] * n,
        out_shape=[jax.ShapeDtypeStruct(a.shape, a.dtype) for a in parts],
        scratch_shapes=[pltpu.SemaphoreType.DMA((n,)), pltpu.SemaphoreType.DMA((n,))],
        name=name,
    )(*parts)


def _all_gather_small(a, name):
    def body(in_ref, out_ref, send_sems, recv_sems, local_sem):
        x, y, c, _ = _place()
        me = 4 * x + 2 * y + c
        local = pltpu.make_async_copy(in_ref, out_ref.at[me], local_sem)
        local.start()

        def remote(d):
            return pltpu.make_async_remote_copy(
                src_ref=in_ref, dst_ref=out_ref.at[me], send_sem=send_sems.at[d], recv_sem=recv_sems.at[me],
                device_id=(d // 4, (d // 2) % 2, d % 2), device_id_type=MESH)

        def arrival(d):
            return pltpu.make_async_remote_copy(
                src_ref=in_ref, dst_ref=out_ref.at[d], send_sem=send_sems.at[d], recv_sem=recv_sems.at[d],
                device_id=(d // 4, (d // 2) % 2, d % 2), device_id_type=MESH)

        for d in range(N_DEV):
            @pl.when(me != d)
            def _():
                remote(d).start()
        for d in range(N_DEV):
            @pl.when(me != d)
            def _():
                remote(d).wait_send()
                arrival(d).wait_recv()
        local.wait()

    return pl.pallas_call(
        body, in_specs=[ANY], out_specs=ANY, out_shape=jax.ShapeDtypeStruct((N_DEV,) + a.shape, a.dtype),
        scratch_shapes=[pltpu.SemaphoreType.DMA((N_DEV,)), pltpu.SemaphoreType.DMA((N_DEV,)), pltpu.SemaphoreType.DMA],
        name=name,
    )(a)


def _reduce_tile(rows):
    return _pick(rows, (256, 16))


def _pair_add(full, other, layer, name):
    _, rows, cols = full.shape
    tr = _reduce_tile(rows)

    def body(layer_ref, a_ref, b_ref, o_ref):
        o_ref[...] = (a_ref[0] + b_ref[...]).astype(o_ref.dtype)

    return pl.pallas_call(
        body,
        grid_spec=pltpu.PrefetchScalarGridSpec(
            num_scalar_prefetch=1, grid=(rows // tr,),
            in_specs=[pl.BlockSpec((1, tr, cols), lambda i, lr: (lr[0], i, 0)), pl.BlockSpec((tr, cols), lambda i, lr: (i, 0))],
            out_specs=pl.BlockSpec((tr, cols), lambda i, lr: (i, 0))),
        out_shape=jax.ShapeDtypeStruct((rows, cols), BF16), compiler_params=_params("parallel"), name=name,
    )(layer, full, other)


def _sum_slots(a, name):
    n, rows, cols = a.shape
    tr = _reduce_tile(rows)

    def body(a_ref, o_ref):
        acc = a_ref[0].astype(F32)
        for k in range(1, n):
            acc = acc + a_ref[k].astype(F32)
        o_ref[...] = acc

    return pl.pallas_call(
        body, grid=(rows // tr,), in_specs=[pl.BlockSpec((n, tr, cols), lambda i: (0, i, 0))],
        out_specs=pl.BlockSpec((tr, cols), lambda i: (i, 0)),
        out_shape=jax.ShapeDtypeStruct((rows, cols), F32), compiler_params=_params("parallel"), name=name,
    )(a)


def _adamw(w, g, m, v, name):
    rows, cols = w.shape
    tr = _pick(rows, (256, 8))

    def body(w_ref, g_ref, m_ref, v_ref, d_ref, nm_ref, nv_ref):
        gv = g_ref[...]
        mn = ADAM_B1 * m_ref[...] + (1.0 - ADAM_B1) * gv
        vn = ADAM_B2 * v_ref[...] + (1.0 - ADAM_B2) * jnp.square(gv)
        m_hat = mn / (1.0 - ADAM_B1 ** ADAM_STEP)
        v_hat = vn / (1.0 - ADAM_B2 ** ADAM_STEP)
        d_ref[...] = -ADAM_LR * (m_hat / (jnp.sqrt(v_hat) + ADAM_EPS) + ADAM_WD * w_ref[...])
        nm_ref[...] = mn
        nv_ref[...] = vn

    blk = pl.BlockSpec((tr, cols), lambda i: (i, 0))
    return pl.pallas_call(
        body, grid=(rows // tr,), in_specs=[blk] * 4, out_specs=[blk] * 3,
        out_shape=[jax.ShapeDtypeStruct((rows, cols), F32)] * 3, compiler_params=_params("parallel"), name=name,
    )(w, g, m, v)


BIG = ("ssm_w_in", "ssm_w_out", "att_w_in", "att_w_out")
SHARDED = BIG + ("ssm_conv_w",)
SMALL = ("pre_norm", "post_norm", "ssm_conv_b", "ssm_dt_bias", "ssm_a_log", "ssm_d", "ssm_gate_norm", "att_sinks")
WEIGHTS = ("pre_norm", "post_norm", "ssm_w_in", "ssm_conv_w", "ssm_conv_b", "ssm_dt_bias", "ssm_a_log", "ssm_d",
           "ssm_gate_norm", "ssm_w_out", "att_w_in", "att_sinks", "att_w_out")


def _cols_to_whole(g):
    _, two, rows, cols = g.shape
    return jnp.transpose(g, (1, 2, 0, 3)).reshape(two, rows, N_CHIPS * cols)


def _rows_to_whole(g):
    _, two, rows, cols = g.shape
    return jnp.transpose(g, (1, 0, 2, 3)).reshape(two, N_CHIPS * rows, cols)


def _cols_by_chip(g):
    two, rows, cols = g.shape
    return jnp.transpose(g.reshape(two, rows, N_CHIPS, cols // N_CHIPS), (0, 2, 1, 3)).reshape(two, N_CHIPS * rows, cols // N_CHIPS)


def _pack_small(tree, keys):
    flat = jnp.concatenate([tree[k].reshape(-1) for k in keys])
    rows = -(-flat.shape[0] // (8 * LANES)) * 8
    return jnp.pad(flat, (0, rows * LANES - flat.shape[0])).reshape(rows, LANES)


def _unpack_small(packed, shapes, keys):
    flat = packed.reshape(-1)
    out, at = {}, 0
    for k in keys:
        n = 1
        for dim in shapes[k]:
            n *= dim
        out[k] = flat[at:at + n].reshape(shapes[k])
        at += n
    return out


def kernel(x, positions, pre_norm, post_norm, ssm_w_in, ssm_conv_w, ssm_conv_b, ssm_dt_bias, ssm_a_log, ssm_d, ssm_gate_norm, ssm_w_out, att_w_in, att_sinks, att_w_out, loss_target, m_pre_norm, m_post_norm, m_ssm_w_in, m_ssm_conv_w, m_ssm_conv_b, m_ssm_dt_bias, m_ssm_a_log, m_ssm_d, m_ssm_gate_norm, m_ssm_w_out, m_att_w_in, m_att_sinks, m_att_w_out, v_pre_norm, v_post_norm, v_ssm_w_in, v_ssm_conv_w, v_ssm_conv_b, v_ssm_dt_bias, v_ssm_a_log, v_ssm_d, v_ssm_gate_norm, v_ssm_w_out, v_att_w_in, v_att_sinks, v_att_w_out):
    w = dict(pre_norm=pre_norm, post_norm=post_norm, ssm_w_in=ssm_w_in, ssm_conv_w=ssm_conv_w, ssm_conv_b=ssm_conv_b,
             ssm_dt_bias=ssm_dt_bias, ssm_a_log=ssm_a_log, ssm_d=ssm_d, ssm_gate_norm=ssm_gate_norm, ssm_w_out=ssm_w_out,
             att_w_in=att_w_in, att_sinks=att_sinks, att_w_out=att_w_out)
    m = dict(pre_norm=m_pre_norm, post_norm=m_post_norm, ssm_w_in=m_ssm_w_in, ssm_conv_w=m_ssm_conv_w, ssm_conv_b=m_ssm_conv_b,
             ssm_dt_bias=m_ssm_dt_bias, ssm_a_log=m_ssm_a_log, ssm_d=m_ssm_d, ssm_gate_norm=m_ssm_gate_norm,
             ssm_w_out=m_ssm_w_out, att_w_in=m_att_w_in, att_sinks=m_att_sinks, att_w_out=m_att_w_out)
    v = dict(pre_norm=v_pre_norm, post_norm=v_post_norm, ssm_w_in=v_ssm_w_in, ssm_conv_w=v_ssm_conv_w, ssm_conv_b=v_ssm_conv_b,
             ssm_dt_bias=v_ssm_dt_bias, ssm_a_log=v_ssm_a_log, ssm_d=v_ssm_d, ssm_gate_norm=v_ssm_gate_norm,
             ssm_w_out=v_ssm_w_out, att_w_in=v_att_w_in, att_sinks=v_att_sinks, att_w_out=v_att_w_out)
    c = lax.axis_index("c")
    chip = 2 * lax.axis_index("x") + lax.axis_index("y")

    g_in, g_out, g_ain, g_aout, g_cw = _chip_gather(
        [ssm_w_in.astype(BF16), ssm_w_out.astype(BF16), att_w_in.astype(BF16), att_w_out.astype(BF16), ssm_conv_w],
        "gather_weights")
    w_in_full = jnp.pad(_cols_to_whole(g_in), ((0, 0), (0, 0), (0, SSM_IN_PAD - SSM_IN_DIM)))
    loss_lanes, grad_x, gr = _local_step(
        x[0], positions[0], pre_norm, post_norm, w_in_full, _cols_to_whole(g_cw), ssm_conv_b, ssm_dt_bias, ssm_a_log,
        ssm_d, ssm_gate_norm, _rows_to_whole(g_out), _cols_to_whole(g_ain), att_sinks, _rows_to_whole(g_aout),
        loss_target[0])
    loss = lax.psum(0.5 * jnp.sum(loss_lanes) / D_MODEL, ("x", "y", "c"))

    parts = [_cols_by_chip(gr["ssm_w_in"][:, :, :SSM_IN_DIM]), gr["ssm_w_out"], _cols_by_chip(gr["att_w_in"]),
             gr["att_w_out"]]
    from_sibling = _pair_swap(parts, "reduce_pair_swap")
    layer = jnp.reshape(c, (1,)).astype(jnp.int32)
    chip_sums = [_pair_add(p, o, layer, f"reduce_pair_add_{k}") for k, (p, o) in enumerate(zip(parts, from_sibling))]
    by_chip = _chip_scatter(chip_sums, "reduce_chip_scatter")
    mine = [_sum_slots(a, f"reduce_chip_sum_{k}") for k, a in enumerate(by_chip)]
    theirs = _pair_merge(mine, "reduce_pair_merge")
    grads = {k: jnp.stack([jnp.where(c == 0, a, b), jnp.where(c == 0, b, a)]).reshape(w[k].shape)
             for k, a, b in zip(BIG, mine, theirs)}

    small_keys = SMALL + ("ssm_conv_w",)
    small_shapes = {k: w[k].shape for k in SMALL}
    small_shapes["ssm_conv_w"] = gr["ssm_conv_w"].shape
    small_sum = _sum_slots(_all_gather_small(_pack_small(gr, small_keys), "reduce_small_gather"), "reduce_small_sum")
    grads.update(_unpack_small(small_sum, small_shapes, small_keys))
    conv_cols = ssm_conv_w.shape[2]
    grads["ssm_conv_w"] = lax.dynamic_slice_in_dim(grads["ssm_conv_w"], chip * conv_cols, conv_cols, axis=2)

    delta, new_m, new_v = {}, {}, {}
    for k in SHARDED:
        shp = w[k].shape
        two_d = (shp[0] * shp[1], shp[2])
        d_, m_, v_ = _adamw(w[k].reshape(two_d), grads[k].reshape(two_d), m[k].reshape(two_d), v[k].reshape(two_d),
                            f"adamw_{k}")
        delta[k], new_m[k], new_v[k] = d_.reshape(shp), m_.reshape(shp), v_.reshape(shp)
    d_, m_, v_ = _adamw(_pack_small(w, SMALL), _pack_small(grads, SMALL), _pack_small(m, SMALL), _pack_small(v, SMALL),
                        "adamw_small")
    delta.update(_unpack_small(d_, small_shapes, SMALL))
    new_m.update(_unpack_small(m_, small_shapes, SMALL))
    new_v.update(_unpack_small(v_, small_shapes, SMALL))

    return (loss, grad_x[None], *[grads[k] for k in WEIGHTS], *[delta[k] for k in WEIGHTS],
            *[new_m[k] for k in WEIGHTS], *[new_v[k] for k in WEIGHTS])
```

```python
import functools

import jax
import jax.numpy as jnp
from jax import lax
from jax.experimental import pallas as pl
from jax.experimental.pallas import tpu as pltpu

F32 = jnp.float32
BF16 = jnp.bfloat16
EPS = 1e-6
NEG_INF = float("-inf")

D_MODEL = 1024
DEPTH = 4
SSM_D_INNER = 2048
SSM_HEAD_DIM = 64
SSM_HEADS = 32
SSM_GROUPS = 8
SSM_HPG = 4
SSM_STATE = 128
SSM_CONV = 4
SSM_CHUNK = 128
SSM_BC_DIM = 1024
SSM_CONV_DIM = 4096
SSM_IN_DIM = 6176
SSM_IN_PAD = 6272
SSM_DT_PAD = 128
ATT_HEAD_DIM = 64
ATT_Q_HEADS = 16
ATT_KV_HEADS = 4
ATT_GQA = 4
ATT_WIDTH = 1024
ATT_KV_WIDTH = 256
ATT_IN_DIM = 2560
ATT_QKV = ATT_WIDTH + 2 * ATT_KV_WIDTH
ATT_BLOCK = 128
ROPE_THETA = 500000.0
ROPE_DIM = 16
ROPE_HALF = 8
Q_SCALE = ATT_HEAD_DIM ** -0.5

ADAM_LR = 0.001
ADAM_B1 = 0.9
ADAM_B2 = 0.999
ADAM_EPS = 1e-08
ADAM_WD = 0.01
ADAM_STEP = 10

VMEM_LIMIT_BYTES = 48 * 1024 * 1024
NT_DIMS = (((1,), (1,)), ((), ()))
TN_DIMS = (((0,), (0,)), ((), ()))


def _params(*sem):
    return pltpu.CompilerParams(dimension_semantics=sem, vmem_limit_bytes=VMEM_LIMIT_BYTES)


def _pick(n, cands):
    for c in cands:
        if n % c == 0:
            return c
    return n


def _sigmoid(v):
    return 0.5 * jnp.tanh(0.5 * v) + 0.5


def _bdot(a, b):
    return jnp.dot(a.astype(BF16), b.astype(BF16), preferred_element_type=F32)


def _bdot_nt(a, b):
    return lax.dot_general(a.astype(BF16), b.astype(BF16), NT_DIMS, preferred_element_type=F32)


def _bdot_tn(a, b):
    return lax.dot_general(a.astype(BF16), b.astype(BF16), TN_DIMS, preferred_element_type=F32)


MATMUL_VMEM_BUDGET = 36 * 1024 * 1024


def _matmul_tiles(m, n, k, out_bytes, reduce_rows):
    best = None
    whole = [k] if (not reduce_rows or k <= 2048) else []
    for tk in whole + [c for c in (4096, 2048, 1024, 896, 512) if k % c == 0 and c < k]:
        for tm in (c for c in (2048, 1024, 512, 256) if m % c == 0):
            for tn in (c for c in (n, 1280, 1024, 896, 640, 512) if n % c == 0):
                acc = tm * tn * 4 if tk < k else 0
                need = 2 * (2 * tk * (tm + tn) + tm * tn * out_bytes) + acc
                if need <= MATMUL_VMEM_BUDGET and (best is None or tm * tn * min(tk, 2048) > best[0]):
                    best = (tm * tn * min(tk, 2048), tm, tn, tk)
        if best is not None and not reduce_rows:
            break
    return best[1:]


def _matmul(a, b, mode, out_dtype, name):
    if mode == "nn":
        (m, k), n = a.shape, b.shape[1]
    elif mode == "nt":
        (m, k), n = a.shape, b.shape[0]
    else:
        (k, m), n = a.shape, b.shape[1]
    tm, tn, tk = _matmul_tiles(m, n, k, jnp.dtype(out_dtype).itemsize, mode == "tn")
    nk = k // tk
    dims = {"nn": (((1,), (0,)), ((), ())), "nt": NT_DIMS, "tn": TN_DIMS}[mode]

    def body(a_ref, b_ref, o_ref, acc_ref):
        kk = pl.program_id(2)
        part = lax.dot_general(a_ref[...], b_ref[...], dims, preferred_element_type=F32)
        if nk == 1:
            o_ref[...] = part.astype(o_ref.dtype)
        else:
            @pl.when(kk == 0)
            def _():
                acc_ref[...] = part

            @pl.when(kk > 0)
            def _():
                acc_ref[...] += part

            @pl.when(kk == nk - 1)
            def _():
                o_ref[...] = acc_ref[...].astype(o_ref.dtype)

    if mode == "nn":
        a_spec = pl.BlockSpec((tm, tk), lambda j, i, kk: (i, kk))
        b_spec = pl.BlockSpec((tk, tn), lambda j, i, kk: (kk, j))
    elif mode == "nt":
        a_spec = pl.BlockSpec((tm, tk), lambda j, i, kk: (i, kk))
        b_spec = pl.BlockSpec((tn, tk), lambda j, i, kk: (j, kk))
    else:
        a_spec = pl.BlockSpec((tk, tm), lambda j, i, kk: (kk, i))
        b_spec = pl.BlockSpec((tk, tn), lambda j, i, kk: (kk, j))
    return pl.pallas_call(
        body, grid=(n // tn, m // tm, nk), in_specs=[a_spec, b_spec],
        out_specs=pl.BlockSpec((tm, tn), lambda j, i, kk: (i, j)),
        out_shape=jax.ShapeDtypeStruct((m, n), out_dtype),
        scratch_shapes=[pltpu.VMEM((tm, tn), F32)],
        compiler_params=_params("parallel", "parallel", "arbitrary"), name=name,
    )(a, b)


def _row_tile(l):
    return _pick(l, (512, 256, 128))


def _rmsnorm_fwd(x, w, name):
    l, d = x.shape
    tl = _row_tile(l)

    def body(x_ref, w_ref, o_ref):
        xv = x_ref[...]
        r = lax.rsqrt(jnp.mean(xv * xv, axis=-1, keepdims=True) + EPS)
        o_ref[...] = (xv * r * w_ref[...]).astype(o_ref.dtype)

    return pl.pallas_call(
        body, grid=(l // tl,),
        in_specs=[pl.BlockSpec((tl, d), lambda i: (i, 0)), pl.BlockSpec((1, d), lambda i: (0, 0))],
        out_specs=pl.BlockSpec((tl, d), lambda i: (i, 0)),
        out_shape=jax.ShapeDtypeStruct((l, d), BF16), compiler_params=_params("parallel"), name=name,
    )(x, w.reshape(1, d))


def _post_fwd(x, y, w, w_next, name):
    l, d = x.shape
    tl = _row_tile(l)

    def body(x_ref, y_ref, w_ref, wn_ref, o_ref, h_ref):
        yv = y_ref[...]
        r = lax.rsqrt(jnp.mean(yv * yv, axis=-1, keepdims=True) + EPS)
        out = x_ref[...] + yv * r * w_ref[...]
        o_ref[...] = out
        rn = lax.rsqrt(jnp.mean(out * out, axis=-1, keepdims=True) + EPS)
        h_ref[...] = (out * rn * wn_ref[...]).astype(h_ref.dtype)

    row = pl.BlockSpec((tl, d), lambda i: (i, 0))
    vec = pl.BlockSpec((1, d), lambda i: (0, 0))
    return pl.pallas_call(
        body, grid=(l // tl,), in_specs=[row, row, vec, vec], out_specs=[row, row],
        out_shape=[jax.ShapeDtypeStruct((l, d), F32), jax.ShapeDtypeStruct((l, d), BF16)],
        compiler_params=_params("parallel"), name=name,
    )(x, y, w.reshape(1, d), w_next.reshape(1, d))


def _post_loss(x, y, w, t, name):
    l, d = x.shape
    tl = _row_tile(l)
    nt = l // tl

    def body(x_ref, y_ref, w_ref, t_ref, g_ref, dy_ref, ls_ref, dw_ref, acc_ref):
        i = pl.program_id(0)

        @pl.when(i == 0)
        def _():
            ls_ref[...] = jnp.zeros_like(ls_ref)
            acc_ref[...] = jnp.zeros_like(acc_ref)

        yv = y_ref[...]
        r = lax.rsqrt(jnp.mean(yv * yv, axis=-1, keepdims=True) + EPS)
        nrm = yv * r
        e = x_ref[...] + nrm * w_ref[...] - t_ref[...]
        gv = e * (1.0 / d)
        g_ref[...] = gv
        ls_ref[...] += jnp.sum((e * e).reshape(tl // 8, 8, d), axis=0)
        gw = gv * w_ref[...]
        dy_ref[...] = (r * (gw - nrm * jnp.mean(gw * nrm, axis=-1, keepdims=True))).astype(dy_ref.dtype)
        acc_ref[...] += jnp.sum((gv * nrm).reshape(tl // 8, 8, d), axis=0)

        @pl.when(i == nt - 1)
        def _():
            dw_ref[...] = jnp.sum(acc_ref[...], axis=0, keepdims=True)

    row = pl.BlockSpec((tl, d), lambda i: (i, 0))
    vec = pl.BlockSpec((1, d), lambda i: (0, 0))
    return pl.pallas_call(
        body, grid=(nt,), in_specs=[row, row, vec, row],
        out_specs=[row, row, pl.BlockSpec((8, d), lambda i: (0, 0)), vec],
        out_shape=[jax.ShapeDtypeStruct((l, d), F32), jax.ShapeDtypeStruct((l, d), BF16),
                   jax.ShapeDtypeStruct((8, d), F32), jax.ShapeDtypeStruct((1, d), F32)],
        scratch_shapes=[pltpu.VMEM((8, d), F32)], compiler_params=_params("arbitrary"), name=name,
    )(x, y, w.reshape(1, d), t)


def _norm_bwd_chain(dh, x, w_pre, resid, y_prev, w_post_prev, name):
    l, d = x.shape
    tl = _row_tile(l)
    nt = l // tl

    def body(dh_ref, x_ref, wp_ref, r_ref, y_ref, wq_ref, g_ref, dy_ref, dwp_ref, dwq_ref, accp_ref, accq_ref):
        i = pl.program_id(0)

        @pl.when(i == 0)
        def _():
            accp_ref[...] = jnp.zeros_like(accp_ref)
            accq_ref[...] = jnp.zeros_like(accq_ref)

        xv = x_ref[...]
        dhv = dh_ref[...]
        rx = lax.rsqrt(jnp.mean(xv * xv, axis=-1, keepdims=True) + EPS)
        nx = xv * rx
        gw = dhv * wp_ref[...]
        gv = rx * (gw - nx * jnp.mean(gw * nx, axis=-1, keepdims=True)) + r_ref[...]
        g_ref[...] = gv
        accp_ref[...] += jnp.sum((dhv * nx).reshape(tl // 8, 8, d), axis=0)
        yv = y_ref[...]
        ry = lax.rsqrt(jnp.mean(yv * yv, axis=-1, keepdims=True) + EPS)
        ny = yv * ry
        gq = gv * wq_ref[...]
        dy_ref[...] = (ry * (gq - ny * jnp.mean(gq * ny, axis=-1, keepdims=True))).astype(dy_ref.dtype)
        accq_ref[...] += jnp.sum((gv * ny).reshape(tl // 8, 8, d), axis=0)

        @pl.when(i == nt - 1)
        def _():
            dwp_ref[...] = jnp.sum(accp_ref[...], axis=0, keepdims=True)
            dwq_ref[...] = jnp.sum(accq_ref[...], axis=0, keepdims=True)

    row = pl.BlockSpec((tl, d), lambda i: (i, 0))
    vec = pl.BlockSpec((1, d), lambda i: (0, 0))
    return pl.pallas_call(
        body, grid=(nt,), in_specs=[row, row, vec, row, row, vec], out_specs=[row, row, vec, vec],
        out_shape=[jax.ShapeDtypeStruct((l, d), F32), jax.ShapeDtypeStruct((l, d), BF16),
                   jax.ShapeDtypeStruct((1, d), F32), jax.ShapeDtypeStruct((1, d), F32)],
        scratch_shapes=[pltpu.VMEM((8, d), F32), pltpu.VMEM((8, d), F32)],
        compiler_params=_params("arbitrary"), name=name,
    )(dh, x, w_pre.reshape(1, d), resid, y_prev, w_post_prev.reshape(1, d))


def _rmsnorm_bwd(g, y, w, resid, out_dtype, name):
    l, d = y.shape
    tl = _row_tile(l)
    nt = l // tl
    has_resid = resid is not None

    def body(*refs):
        if has_resid:
            g_ref, y_ref, w_ref, r_ref, dy_ref, dw_ref, acc_ref = refs
        else:
            g_ref, y_ref, w_ref, dy_ref, dw_ref, acc_ref = refs
        i = pl.program_id(0)

        @pl.when(i == 0)
        def _():
            acc_ref[...] = jnp.zeros_like(acc_ref)

        yv = y_ref[...]
        gv = g_ref[...].astype(F32)
        r = lax.rsqrt(jnp.mean(yv * yv, axis=-1, keepdims=True) + EPS)
        nrm = yv * r
        gw = gv * w_ref[...]
        dy = r * (gw - nrm * jnp.mean(gw * nrm, axis=-1, keepdims=True))
        if has_resid:
            dy = dy + r_ref[...]
        dy_ref[...] = dy.astype(dy_ref.dtype)
        acc_ref[...] += jnp.sum((gv * nrm).reshape(tl // 8, 8, d), axis=0)

        @pl.when(i == nt - 1)
        def _():
            dw_ref[...] = jnp.sum(acc_ref[...], axis=0, keepdims=True)

    row = pl.BlockSpec((tl, d), lambda i: (i, 0))
    vec = pl.BlockSpec((1, d), lambda i: (0, 0))
    ins = [g, y, w.reshape(1, d)] + ([resid] if has_resid else [])
    return pl.pallas_call(
        body, grid=(nt,), in_specs=[row, row, vec] + ([row] if has_resid else []),
        out_specs=[row, vec],
        out_shape=[jax.ShapeDtypeStruct((l, d), out_dtype), jax.ShapeDtypeStruct((1, d), F32)],
        scratch_shapes=[pltpu.VMEM((8, d), F32)], compiler_params=_params("arbitrary"), name=name,
    )(*ins)


CONV_COLS = 512
HALO = 8
HALO16 = 16
CONV_SUB_ROWS = 64
CONV_SUB_COLS = 256


def _conv_fwd(proj, cw, cb, name):
    l = proj.shape[0]
    tl = _row_tile(l)
    off = SSM_D_INNER // CONV_COLS

    def body(u_ref, halo_ref, w_ref, b_ref, pre_ref, act_ref, ext_ref):
        i = pl.program_id(1)
        ext_ref[0:HALO, :] = jnp.where(i > 0, halo_ref[...], 0.0)
        ext_ref[HALO:HALO + tl, :] = u_ref[...]
        for r0 in range(0, tl, CONV_SUB_ROWS):
            for c0 in range(0, CONV_COLS, CONV_SUB_COLS):
                cs = slice(c0, c0 + CONV_SUB_COLS)
                ext = ext_ref[r0:r0 + CONV_SUB_ROWS + HALO, cs]
                acc = b_ref[:, cs] + w_ref[SSM_CONV - 1:SSM_CONV, cs] * ext[HALO:]
                for k in range(SSM_CONV - 1):
                    acc = acc + w_ref[k:k + 1, cs] * pltpu.roll(ext, SSM_CONV - 1 - k, 0)[HALO:]
                pre_ref[r0:r0 + CONV_SUB_ROWS, cs] = acc.astype(pre_ref.dtype)
                act_ref[r0:r0 + CONV_SUB_ROWS, cs] = (acc * _sigmoid(acc)).astype(act_ref.dtype)

    hb = tl // HALO
    out = pl.BlockSpec((tl, CONV_COLS), lambda j, i: (i, j))
    return pl.pallas_call(
        body, grid=(SSM_CONV_DIM // CONV_COLS, l // tl),
        in_specs=[pl.BlockSpec((tl, CONV_COLS), lambda j, i: (i, off + j)),
                  pl.BlockSpec((HALO, CONV_COLS), lambda j, i: (jnp.maximum(i * hb - 1, 0), off + j)),
                  pl.BlockSpec((SSM_CONV, CONV_COLS), lambda j, i: (0, j)),
                  pl.BlockSpec((1, CONV_COLS), lambda j, i: (0, j))],
        out_specs=[out, out],
        out_shape=[jax.ShapeDtypeStruct((l, SSM_CONV_DIM), BF16)] * 2,
        scratch_shapes=[pltpu.VMEM((tl + HALO, CONV_COLS), F32)],
        compiler_params=_params("parallel", "arbitrary"), name=name,
    )(proj, proj, cw, cb.reshape(1, SSM_CONV_DIM))


def _conv_bwd(dact, pre, proj, cw, c0, dproj, name):
    l, width = dact.shape
    tl = _row_tile(l)
    nt = l // tl
    pre_off = c0 // CONV_COLS
    u_off = (SSM_D_INNER + c0) // CONV_COLS
    hb = tl // HALO
    hb16 = tl // HALO16
    last_hb16 = l // HALO16 - 1

    def body(da_ref, da_h_ref, p_ref, p_h_ref, u_ref, u_h_ref, w_ref, _, du_ref, dw_ref, db_ref, ext_ref, uext_ref):
        i = pl.program_id(1)

        @pl.when(i == 0)
        def _():
            dw_ref[...] = jnp.zeros_like(dw_ref)
            db_ref[...] = jnp.zeros_like(db_ref)

        def dpre_of(da, p):
            s = _sigmoid(p)
            return da * (s * (1.0 + p * (1.0 - s)))

        ext_ref[0:tl, :] = dpre_of(da_ref[...].astype(F32), p_ref[...].astype(F32))
        ext_ref[tl:tl + HALO, :] = jnp.where(
            i < nt - 1, dpre_of(da_h_ref[...].astype(F32)[:HALO], p_h_ref[...].astype(F32)[:HALO]), 0.0)
        uext_ref[0:HALO, :] = jnp.where(i > 0, u_h_ref[...], 0.0)
        uext_ref[HALO:HALO + tl, :] = u_ref[...]
        sub = CONV_SUB_ROWS
        for c0 in range(0, CONV_COLS, CONV_SUB_COLS):
            cs = slice(c0, c0 + CONV_SUB_COLS)
            dws = [jnp.zeros((1, CONV_SUB_COLS), F32) for _ in range(SSM_CONV)]
            dbs = jnp.zeros((1, CONV_SUB_COLS), F32)
            for r0 in range(0, tl, sub):
                dext = ext_ref[r0:r0 + sub + HALO, cs]
                uext = uext_ref[r0:r0 + sub + HALO, cs]
                dp = dext[:sub]
                du = w_ref[SSM_CONV - 1:SSM_CONV, cs] * dp
                dws[SSM_CONV - 1] = dws[SSM_CONV - 1] + jnp.sum(dp * uext[HALO:], axis=0, keepdims=True)
                for k in range(SSM_CONV - 1):
                    j = SSM_CONV - 1 - k
                    du = du + w_ref[k:k + 1, cs] * pltpu.roll(dext, sub + HALO - j, 0)[:sub]
                    dws[k] = dws[k] + jnp.sum(dp * pltpu.roll(uext, j, 0)[HALO:], axis=0, keepdims=True)
                dbs = dbs + jnp.sum(dp, axis=0, keepdims=True)
                du_ref[r0:r0 + sub, cs] = du.astype(du_ref.dtype)
            for k in range(SSM_CONV):
                dw_ref[k:k + 1, cs] += dws[k]
            db_ref[:, cs] += dbs

    return pl.pallas_call(
        body, grid=(width // CONV_COLS, nt),
        in_specs=[pl.BlockSpec((tl, CONV_COLS), lambda j, i: (i, j)),
                  pl.BlockSpec((HALO16, CONV_COLS), lambda j, i: (jnp.minimum((i + 1) * hb16, last_hb16), j)),
                  pl.BlockSpec((tl, CONV_COLS), lambda j, i: (i, pre_off + j)),
                  pl.BlockSpec((HALO16, CONV_COLS), lambda j, i: (jnp.minimum((i + 1) * hb16, last_hb16), pre_off + j)),
                  pl.BlockSpec((tl, CONV_COLS), lambda j, i: (i, u_off + j)),
                  pl.BlockSpec((HALO, CONV_COLS), lambda j, i: (jnp.maximum(i * hb - 1, 0), u_off + j)),
                  pl.BlockSpec((SSM_CONV, CONV_COLS), lambda j, i: (0, pre_off + j)),
                  pl.BlockSpec(memory_space=pl.ANY)],
        out_specs=[pl.BlockSpec((tl, CONV_COLS), lambda j, i: (i, u_off + j)),
                   pl.BlockSpec((SSM_CONV, CONV_COLS), lambda j, i: (0, j)),
                   pl.BlockSpec((1, CONV_COLS), lambda j, i: (0, j))],
        out_shape=[jax.ShapeDtypeStruct(dproj.shape, dproj.dtype), jax.ShapeDtypeStruct((SSM_CONV, width), F32),
                   jax.ShapeDtypeStruct((1, width), F32)],
        scratch_shapes=[pltpu.VMEM((tl + HALO, CONV_COLS), F32), pltpu.VMEM((tl + HALO, CONV_COLS), F32)],
        input_output_aliases={7: 0}, compiler_params=_params("parallel", "arbitrary"), name=name,
    )(dact, dact, pre, pre, proj, proj, cw, dproj)


DT_COL_BLOCK = (SSM_D_INNER + SSM_CONV_DIM) // SSM_DT_PAD


def _split3(v):
    hi = v.astype(BF16)
    rest = v - hi.astype(F32)
    mid = rest.astype(BF16)
    lo = (rest - mid.astype(F32)).astype(BF16)
    return hi, mid, lo


def _ssd_prep(proj, bias, alog_lanes, name):
    l = proj.shape[0]
    nc = l // SSM_CHUNK
    head_dim_log2 = SSM_HEAD_DIM.bit_length() - 1

    def body(p_ref, b_ref, al_ref, dtb_ref, acsb_ref, dtr_ref, acsr_ref):
        v = p_ref[...] + b_ref[...]
        dt = jnp.maximum(v, 0.0) + jnp.log1p(jnp.exp(-jnp.abs(v)))
        head_of_lane = lax.shift_right_logical(lax.broadcasted_iota(jnp.int32, (SSM_DT_PAD, SSM_D_INNER), 1), head_dim_log2)
        spread = (head_of_lane == lax.broadcasted_iota(jnp.int32, (SSM_DT_PAD, SSM_D_INNER), 0)).astype(BF16)
        dtb = sum(jnp.dot(piece, spread, preferred_element_type=F32) for piece in _split3(dt)[:2])
        dtb_ref[...] = dtb
        ri = lax.broadcasted_iota(jnp.int32, (SSM_CHUNK, SSM_CHUNK), 0)
        cj = lax.broadcasted_iota(jnp.int32, (SSM_CHUNK, SSM_CHUNK), 1)
        tri = (ri >= cj).astype(BF16)
        acsb = sum(jnp.dot(tri, piece, preferred_element_type=F32) for piece in _split3(dtb * (-jnp.exp(al_ref[...]))))
        acsb_ref[...] = acsb
        gp = SSM_HPG * SSM_HEAD_DIM
        lane = lax.broadcasted_iota(jnp.int32, (SSM_HPG, gp), 1)
        pick = (lane == lax.broadcasted_iota(jnp.int32, (SSM_HPG, gp), 0) * SSM_HEAD_DIM).astype(BF16)
        for g in range(SSM_GROUPS):
            cols = slice(g * gp, (g + 1) * gp)
            dtr_ref[g] = sum(lax.dot_general(pick, piece, NT_DIMS, preferred_element_type=F32)
                             for piece in _split3(dtb[:, cols]))
            acsr_ref[g] = sum(lax.dot_general(pick, piece, NT_DIMS, preferred_element_type=F32)
                              for piece in _split3(acsb[:, cols]))

    rows = pl.BlockSpec((SSM_GROUPS, SSM_HPG, SSM_CHUNK), lambda c: (0, 0, c))
    dense = pl.BlockSpec((SSM_CHUNK, SSM_D_INNER), lambda c: (c, 0))
    return pl.pallas_call(
        body, grid=(nc,),
        in_specs=[pl.BlockSpec((SSM_CHUNK, SSM_DT_PAD), lambda c: (c, DT_COL_BLOCK)),
                  pl.BlockSpec((1, SSM_DT_PAD), lambda c: (0, 0)),
                  pl.BlockSpec((1, SSM_D_INNER), lambda c: (0, 0))],
        out_specs=[dense, dense, rows, rows],
        out_shape=[jax.ShapeDtypeStruct((l, SSM_D_INNER), F32), jax.ShapeDtypeStruct((l, SSM_D_INNER), F32),
                   jax.ShapeDtypeStruct((SSM_GROUPS, SSM_HPG, l), F32),
                   jax.ShapeDtypeStruct((SSM_GROUPS, SSM_HPG, l), F32)],
        compiler_params=_params("parallel"), name=name,
    )(proj, bias, alog_lanes)


def _dt_bwd(ddt, proj, bias, dproj, name):
    l = proj.shape[0]
    tl = _row_tile(l)

    def body(g_ref, p_ref, b_ref, _, o_ref, db_ref):
        @pl.when(pl.program_id(0) == 0)
        def _():
            db_ref[...] = jnp.zeros_like(db_ref)

        d = g_ref[...] * _sigmoid(p_ref[...] + b_ref[...])
        o_ref[...] = d.astype(o_ref.dtype)
        db_ref[...] += jnp.sum(d, axis=0, keepdims=True)

    return pl.pallas_call(
        body, grid=(l // tl,),
        in_specs=[pl.BlockSpec((tl, SSM_DT_PAD), lambda i: (i, 0)),
                  pl.BlockSpec((tl, SSM_DT_PAD), lambda i: (i, DT_COL_BLOCK)),
                  pl.BlockSpec((1, SSM_DT_PAD), lambda i: (0, 0)),
                  pl.BlockSpec(memory_space=pl.ANY)],
        out_specs=[pl.BlockSpec((tl, SSM_DT_PAD), lambda i: (i, DT_COL_BLOCK)),
                   pl.BlockSpec((1, SSM_DT_PAD), lambda i: (0, 0))],
        out_shape=[jax.ShapeDtypeStruct(dproj.shape, dproj.dtype), jax.ShapeDtypeStruct((1, SSM_DT_PAD), F32)],
        input_output_aliases={3: 0}, compiler_params=_params("arbitrary"), name=name,
    )(ddt, proj, bias, dproj)


GP = SSM_HPG * SSM_HEAD_DIM
HEAD_DIM_LOG2 = SSM_HEAD_DIM.bit_length() - 1
CHUNK_LOG2 = SSM_CHUNK.bit_length() - 1
GPS = 8
B_BLOCK0 = SSM_D_INNER // SSM_STATE
C_BLOCK0 = (SSM_D_INNER + SSM_BC_DIM) // SSM_STATE


def _chunk_iotas():
    ri = lax.broadcasted_iota(jnp.int32, (SSM_CHUNK, SSM_CHUNK), 0)
    cj = lax.broadcasted_iota(jnp.int32, (SSM_CHUNK, SSM_CHUNK), 1)
    return ri, cj


def _head_decay(acsb, acs_r, r, ri, cj):
    pair = acsb[:, (r // 2) * LANES:(r // 2 + 1) * LANES]
    mine_low = r % 2 == 0
    lane = lax.broadcasted_iota(jnp.int32, (1, LANES), 1)
    col = jnp.where((lane < SSM_HEAD_DIM) == mine_low, pair, pltpu.roll(pair, SSM_HEAD_DIM, 1))
    return jnp.exp(jnp.where(ri >= cj, col - acs_r[r:r + 1, :], NEG_INF))


def _head_masked_rows(v, dtype):
    head_of_lane = lax.shift_right_logical(lax.broadcasted_iota(jnp.int32, (1, GP), 1), HEAD_DIM_LOG2)
    return jnp.concatenate([jnp.where(head_of_lane == r, v, 0.0).astype(dtype) for r in range(SSM_HPG)], axis=0)


def _ssd_fwd(xbc, dtb, acsb, acs_r, d_lanes, proj, gate_w, name):
    l = xbc.shape[0]
    nc = l // SSM_CHUNK
    assert GPS == SSM_GROUPS

    def body(x_ref, b_ref, c_ref, dtb_ref, acsb_ref, acsr_ref, d_ref, z_ref, gw_ref, y_ref, act_ref, hin_ref, h_ref):
        c = pl.program_id(0)
        ri, cj = _chunk_iotas()
        for k in range(GPS):
            g = k
            cols = slice(k * GP, (k + 1) * GP)
            ncols = slice(k * SSM_STATE, (k + 1) * SSM_STATE)

            @pl.when(c == 0)
            def _():
                h_ref[g] = jnp.zeros((SSM_STATE, GP), F32)

            xv = x_ref[:, cols].astype(F32)
            bb = b_ref[:, ncols].astype(BF16)
            cb16 = c_ref[:, ncols].astype(BF16)
            acs_v = acsb_ref[:, cols]
            acs_r_v = acsr_ref[k]
            lastb = acs_v[SSM_CHUNK - 1:SSM_CHUNK, :]
            xd = xv * dtb_ref[:, cols]
            cb = lax.dot_general(cb16, bb, NT_DIMS, preferred_element_type=F32)
            hin = h_ref[g]
            hin_ref[0, k] = hin
            yoff = jnp.dot(cb16, hin.astype(BF16), preferred_element_type=F32)
            ms = [(cb * _head_decay(acs_v, acs_r_v, r, ri, cj)).astype(BF16) for r in range(SSM_HPG)]
            ydiag = jnp.dot(jnp.concatenate(ms, axis=1), _head_masked_rows(xd, BF16), preferred_element_type=F32)
            y_ref[:, cols] = ydiag + jnp.exp(acs_v) * yoff + d_ref[k] * xv
            h_ref[g] = hin * jnp.exp(lastb) + _bdot_tn(bb, xd * jnp.exp(lastb - acs_v))
        z = z_ref[...]
        yg = y_ref[...] * (z * _sigmoid(z))
        r = lax.rsqrt(jnp.mean(yg * yg, axis=-1, keepdims=True) + EPS)
        act_ref[...] = (yg * r * gw_ref[...]).astype(act_ref.dtype)

    lanes = pl.BlockSpec((SSM_CHUNK, SSM_D_INNER), lambda c: (c, 0))
    return pl.pallas_call(
        body, grid=(nc,),
        in_specs=[lanes,
                  pl.BlockSpec((SSM_CHUNK, SSM_BC_DIM), lambda c: (c, B_BLOCK0 // GPS)),
                  pl.BlockSpec((SSM_CHUNK, SSM_BC_DIM), lambda c: (c, C_BLOCK0 // GPS)),
                  lanes, lanes,
                  pl.BlockSpec((SSM_GROUPS, SSM_HPG, SSM_CHUNK), lambda c: (0, 0, c)),
                  pl.BlockSpec((SSM_GROUPS, 1, GP), lambda c: (0, 0, 0)),
                  lanes, pl.BlockSpec((1, SSM_D_INNER), lambda c: (0, 0))],
        out_specs=[lanes, lanes, pl.BlockSpec((1, SSM_GROUPS, SSM_STATE, GP), lambda c: (c, 0, 0, 0))],
        out_shape=[jax.ShapeDtypeStruct((l, SSM_D_INNER), F32), jax.ShapeDtypeStruct((l, SSM_D_INNER), BF16),
                   jax.ShapeDtypeStruct((nc, SSM_GROUPS, SSM_STATE, GP), F32)],
        scratch_shapes=[pltpu.VMEM((SSM_GROUPS, SSM_STATE, GP), F32)],
        compiler_params=_params("arbitrary"), name=name,
    )(xbc, xbc, xbc, dtb, acsb, acs_r, d_lanes, proj, gate_w.reshape(1, SSM_D_INNER))


def _ssd_bwd(xbc, dtb, acsb, dtr, acs_r, a_log, d_lanes, hin, dact, y, proj, gate_w, name):
    l = xbc.shape[0]
    nc = l // SSM_CHUNK

    def body(x_ref, b_ref, c_ref, dtb_ref, acsb_ref, dtr_ref, acsr_ref, alc_ref, d_ref, hin_ref,
             dact_ref, y_ref, z_ref, gw_ref,
             dx_ref, db_ref, dc_ref, ddt_ref, dal_ref, dd_ref, dproj_ref, dgw_ref, dh_ref, dy_ref, acc_ref):
        c = pl.program_id(0)

        @pl.when(c == 0)
        def _():
            dal_ref[...] = jnp.zeros_like(dal_ref)
            dd_ref[...] = jnp.zeros_like(dd_ref)
            acc_ref[...] = jnp.zeros_like(acc_ref)

        z = z_ref[...]
        yv = y_ref[...]
        s = _sigmoid(z)
        sz = z * s
        yg = yv * sz
        r = lax.rsqrt(jnp.mean(yg * yg, axis=-1, keepdims=True) + EPS)
        nrm = yg * r
        gv = dact_ref[...]
        gw = gv * gw_ref[...]
        dyg = r * (gw - nrm * jnp.mean(gw * nrm, axis=-1, keepdims=True))
        dy_ref[...] = dyg * sz
        dproj_ref[...] = (dyg * yv * (s * (1.0 + z * (1.0 - s)))).astype(dproj_ref.dtype)
        acc_ref[...] += jnp.sum((gv * nrm).reshape(SSM_CHUNK // 8, 8, SSM_D_INNER), axis=0)

        @pl.when(c == nc - 1)
        def _():
            dgw_ref[...] = jnp.sum(acc_ref[...], axis=0, keepdims=True)

        for k in range(GPS):
            one_group(c, k, k, x_ref, b_ref, c_ref, dtb_ref, acsb_ref, dtr_ref, acsr_ref, alc_ref, d_ref,
                      hin_ref, dy_ref, dx_ref, db_ref, dc_ref, ddt_ref, dal_ref, dd_ref, dh_ref)

    def one_group(c, g, k, x_ref, b_ref, c_ref, dtb_ref, acsb_ref, dtr_ref, acsr_ref, alc_ref, d_ref, hin_ref, dy_ref,
                  dx_ref, db_ref, dc_ref, ddt_ref, dal_ref, dd_ref, dh_ref):
        cols = slice(k * GP, (k + 1) * GP)
        ncols = slice(k * SSM_STATE, (k + 1) * SSM_STATE)

        @pl.when(c == 0)
        def _():
            dh_ref[g] = jnp.zeros((SSM_STATE, GP), F32)

        xv = x_ref[:, cols].astype(F32)
        dyv = dy_ref[:, cols]
        bb = b_ref[:, ncols].astype(BF16)
        cb16 = c_ref[:, ncols].astype(BF16)
        dtb = dtb_ref[:, cols]
        acsb = acsb_ref[:, cols]
        dtr_v = dtr_ref[k]
        acs_r = acsr_ref[k]
        a_col = -jnp.exp(alc_ref[k])
        ri, cj = _chunk_iotas()
        head_of_lane = lax.shift_right_logical(lax.broadcasted_iota(jnp.int32, (SSM_HPG, GP), 1), HEAD_DIM_LOG2)
        ind_t = (head_of_lane == lax.broadcasted_iota(jnp.int32, (SSM_HPG, GP), 0)).astype(BF16)
        lastb = acsb[SSM_CHUNK - 1:SSM_CHUNK, :]
        ecb = jnp.exp(acsb)
        dteb = jnp.exp(lastb - acsb)
        xd = xv * dtb
        xw = xd * dteb
        cb = lax.dot_general(cb16, bb, NT_DIMS, preferred_element_type=F32)
        hin_v = hin_ref[0, k]
        dhn = dh_ref[g]
        h16 = hin_v.astype(BF16)
        dh16 = dhn.astype(BF16)
        ch = jnp.dot(cb16, h16, preferred_element_type=F32)
        bdh = jnp.dot(bb, dh16, preferred_element_type=F32)
        dym = _head_masked_rows(dyv, BF16)
        g_all = lax.dot_general(dym, xd.astype(BF16), NT_DIMS, preferred_element_type=F32)
        gl_sum = jnp.zeros((SSM_CHUNK, SSM_CHUNK), F32)
        ms, qs = [], []
        for r in range(SSM_HPG):
            decay = _head_decay(acsb, acs_r, r, ri, cj)
            gl = g_all[r * SSM_CHUNK:(r + 1) * SSM_CHUNK] * decay
            gl_sum = gl_sum + gl
            ms.append((cb * decay).astype(BF16))
            qs.append((gl * cb).astype(BF16))
        dxd = lax.dot_general(jnp.concatenate(ms, axis=0), dym, TN_DIMS, preferred_element_type=F32) + dteb * bdh
        cum = jnp.dot(jnp.concatenate(qs, axis=0), (ri < cj).astype(BF16), preferred_element_type=F32)
        sub4 = lax.broadcasted_iota(jnp.int32, (SSM_HPG, 1), 0)
        da = jnp.zeros((SSM_HPG, SSM_CHUNK), F32)
        for r in range(SSM_HPG):
            rect = jnp.sum(jnp.where(ri >= cj, cum[r * SSM_CHUNK:(r + 1) * SSM_CHUNK], 0.0), axis=0, keepdims=True)
            da = da + jnp.where(sub4 == r, rect, 0.0)
        z2 = xw * bdh
        sub8 = lax.broadcasted_iota(jnp.int32, (8, 1), 0)
        col_sums = (jnp.where(sub8 == 0, jnp.sum(z2, axis=0, keepdims=True), 0.0)
                    + jnp.where(sub8 == 1, jnp.sum(dhn * hin_v, axis=0, keepdims=True), 0.0)
                    + jnp.where(sub8 == 2, jnp.sum(dyv * xv, axis=0, keepdims=True), 0.0))
        summands = jnp.concatenate([dyv * ecb * ch - z2, dxd * xv, col_sums], axis=0)
        sums = sum(lax.dot_general(ind_t, piece, NT_DIMS, preferred_element_type=F32) for piece in _split3(summands)[:2])
        per_pos = sums[:, :2 * SSM_CHUNK]
        totals = sums[:, 2 * SSM_CHUNK:]
        e_last = totals[:, 0:1] + jnp.exp(acs_r[:, SSM_CHUNK - 1:SSM_CHUNK]) * totals[:, 1:2]
        da = (da + e_last + jnp.dot(per_pos[:, :SSM_CHUNK], (ri >= cj).astype(F32), preferred_element_type=F32,
                                    precision=lax.Precision.HIGHEST))
        ddt_ref[k] = a_col * da + per_pos[:, SSM_CHUNK:]
        dal_ref[g] += a_col * jnp.sum(da * dtr_v, axis=1, keepdims=True)
        dd_ref[g] += totals[:, 2:3]
        dx_ref[:, cols] = (dxd * dtb + d_ref[k] * dyv).astype(dx_ref.dtype)
        w16 = (ecb * dyv).astype(BF16)
        xw16 = xw.astype(BF16)
        gl16 = gl_sum.astype(BF16)
        dc_ref[:, ncols] = (jnp.dot(gl16, bb, preferred_element_type=F32)
                            + lax.dot_general(w16, h16, NT_DIMS, preferred_element_type=F32)).astype(dc_ref.dtype)
        db_ref[:, ncols] = (lax.dot_general(gl16, cb16, TN_DIMS, preferred_element_type=F32)
                            + lax.dot_general(xw16, dh16, NT_DIMS, preferred_element_type=F32)).astype(db_ref.dtype)
        dh_ref[g] = dhn * jnp.exp(lastb) + lax.dot_general(cb16, w16, TN_DIMS, preferred_element_type=F32)

    def rev(c):
        return nc - 1 - c

    small = pl.BlockSpec((SSM_GROUPS, SSM_HPG, 1), lambda c: (0, 0, 0))
    lanes = pl.BlockSpec((SSM_CHUNK, SSM_D_INNER), lambda c: (rev(c), 0))
    rows = pl.BlockSpec((SSM_GROUPS, SSM_HPG, SSM_CHUNK), lambda c: (0, 0, rev(c)))
    vec = pl.BlockSpec((1, SSM_D_INNER), lambda c: (0, 0))
    return pl.pallas_call(
        body, grid=(nc,),
        in_specs=[lanes,
                  pl.BlockSpec((SSM_CHUNK, SSM_BC_DIM), lambda c: (rev(c), B_BLOCK0 // GPS)),
                  pl.BlockSpec((SSM_CHUNK, SSM_BC_DIM), lambda c: (rev(c), C_BLOCK0 // GPS)),
                  lanes, lanes, rows, rows,
                  pl.BlockSpec((SSM_GROUPS, SSM_HPG, 1), lambda c: (0, 0, 0)),
                  pl.BlockSpec((SSM_GROUPS, 1, GP), lambda c: (0, 0, 0)),
                  pl.BlockSpec((1, SSM_GROUPS, SSM_STATE, GP), lambda c: (rev(c), 0, 0, 0)),
                  lanes, lanes, lanes, vec],
        out_specs=[lanes,
                   pl.BlockSpec((SSM_CHUNK, SSM_BC_DIM), lambda c: (rev(c), 0)),
                   pl.BlockSpec((SSM_CHUNK, SSM_BC_DIM), lambda c: (rev(c), 0)),
                   rows, small, small, lanes, vec],
        out_shape=[jax.ShapeDtypeStruct((l, SSM_D_INNER), BF16), jax.ShapeDtypeStruct((l, SSM_BC_DIM), BF16),
                   jax.ShapeDtypeStruct((l, SSM_BC_DIM), BF16), jax.ShapeDtypeStruct((SSM_GROUPS, SSM_HPG, l), F32),
                   jax.ShapeDtypeStruct((SSM_GROUPS, SSM_HPG, 1), F32),
                   jax.ShapeDtypeStruct((SSM_GROUPS, SSM_HPG, 1), F32),
                   jax.ShapeDtypeStruct((l, SSM_IN_PAD), BF16), jax.ShapeDtypeStruct((1, SSM_D_INNER), F32)],
        scratch_shapes=[pltpu.VMEM((SSM_GROUPS, SSM_STATE, GP), F32), pltpu.VMEM((SSM_CHUNK, SSM_D_INNER), F32),
                        pltpu.VMEM((8, SSM_D_INNER), F32)],
        compiler_params=_params("arbitrary"), name=name,
    )(xbc, xbc, xbc, dtb, acsb, dtr, acs_r, a_log.reshape(SSM_GROUPS, SSM_HPG, 1), d_lanes, hin, dact, y, proj,
      gate_w.reshape(1, SSM_D_INNER))


LANES = 128
ROPE_Q_CHUNKS = ATT_WIDTH // LANES
ROPE_K_CHUNKS = ATT_KV_WIDTH // LANES


def _rope_tables(positions):
    inv = ROPE_THETA ** (-jnp.arange(0, ROPE_DIM, 2, dtype=F32) / ROPE_DIM)
    ang = positions.astype(F32)[:, None] * inv
    cos, sin = jnp.cos(ang), jnp.sin(ang)
    l = positions.shape[0]
    rest = ATT_HEAD_DIM - ROPE_DIM
    ones, zeros = jnp.ones((l, rest), F32), jnp.zeros((l, rest), F32)
    z8 = jnp.zeros((l, ROPE_HALF), F32)
    cos_f = jnp.concatenate([cos, cos, ones], axis=1)
    sin_a = jnp.concatenate([-sin, z8, zeros], axis=1)
    sin_b = jnp.concatenate([z8, sin, zeros], axis=1)
    reps = LANES // ATT_HEAD_DIM
    return tuple(jnp.tile(t, (1, reps)) for t in (cos_f, sin_a, sin_b))


ATT_QKV4 = 3 * ATT_WIDTH


def _both_halves(chunk):
    lane = lax.broadcasted_iota(jnp.int32, (1, LANES), 1)
    swapped = pltpu.roll(chunk, ATT_HEAD_DIM, 1)
    return jnp.where(lane < ATT_HEAD_DIM, chunk, swapped), jnp.where(lane < ATT_HEAD_DIM, swapped, chunk)


def _rope_fwd(proj, tables, name):
    l = proj.shape[0]
    tl = _pick(l, (256, 128))

    def body(p_ref, c_ref, sa_ref, sb_ref, o_ref):
        cos_f, sin_a, sin_b = c_ref[...], sa_ref[...], sb_ref[...]

        def rope(t):
            return t * cos_f + pltpu.roll(t, LANES - ROPE_HALF, 1) * sin_a + pltpu.roll(t, ROPE_HALF, 1) * sin_b

        for k in range(ROPE_Q_CHUNKS):
            sl = slice(k * LANES, (k + 1) * LANES)
            o_ref[:, sl] = (rope(p_ref[:, sl]) * Q_SCALE).astype(o_ref.dtype)
        for part in range(2):
            for k in range(ROPE_K_CHUNKS):
                src = ATT_WIDTH + part * ATT_KV_WIDTH + k * LANES
                t = p_ref[:, src:src + LANES]
                if part == 0:
                    t = rope(t)
                for head, dup in enumerate(_both_halves(t.astype(o_ref.dtype))):
                    dst = (1 + part) * ATT_WIDTH + (2 * k + head) * ATT_GQA * ATT_HEAD_DIM
                    o_ref[:, dst:dst + LANES] = dup
                    o_ref[:, dst + LANES:dst + 2 * LANES] = dup

    tab = pl.BlockSpec((tl, LANES), lambda i: (i, 0))
    return pl.pallas_call(
        body, grid=(l // tl,), in_specs=[pl.BlockSpec((tl, ATT_IN_DIM), lambda i: (i, 0)), tab, tab, tab],
        out_specs=pl.BlockSpec((tl, ATT_QKV4), lambda i: (i, 0)),
        out_shape=jax.ShapeDtypeStruct((l, ATT_QKV4), BF16), compiler_params=_params("parallel"), name=name,
    )(proj, *tables)


def _rope_bwd(dq, dk4, dv4, dgate, tables, name):
    l = dq.shape[0]
    tl = _pick(l, (256, 128))

    def body(dq_ref, dk_ref, dv_ref, dg_ref, c_ref, sa_ref, sb_ref, o_ref):
        cos_f, sin_a, sin_b = c_ref[...], sa_ref[...], sb_ref[...]
        lane = lax.broadcasted_iota(jnp.int32, (1, LANES), 1)

        def unrope(t):
            return t * cos_f + pltpu.roll(t * sin_a, ROPE_HALF, 1) + pltpu.roll(t * sin_b, LANES - ROPE_HALF, 1)

        def head_total(ref, kvh):
            base = kvh * ATT_GQA * ATT_HEAD_DIM
            s = ref[:, base:base + LANES] + ref[:, base + LANES:base + 2 * LANES]
            return s + pltpu.roll(s, ATT_HEAD_DIM, 1)

        for k in range(ROPE_Q_CHUNKS):
            sl = slice(k * LANES, (k + 1) * LANES)
            o_ref[:, sl] = unrope(dq_ref[:, sl] * Q_SCALE).astype(o_ref.dtype)
        for k in range(ROPE_K_CHUNKS):
            dk = jnp.where(lane < ATT_HEAD_DIM, head_total(dk_ref, 2 * k), head_total(dk_ref, 2 * k + 1))
            dv = jnp.where(lane < ATT_HEAD_DIM, head_total(dv_ref, 2 * k), head_total(dv_ref, 2 * k + 1))
            o_ref[:, ATT_WIDTH + k * LANES:ATT_WIDTH + (k + 1) * LANES] = unrope(dk).astype(o_ref.dtype)
            at = ATT_WIDTH + ATT_KV_WIDTH + k * LANES
            o_ref[:, at:at + LANES] = dv.astype(o_ref.dtype)
        o_ref[:, ATT_QKV:ATT_IN_DIM] = dg_ref[...].astype(o_ref.dtype)

    tab = pl.BlockSpec((tl, LANES), lambda i: (i, 0))
    wide = pl.BlockSpec((tl, ATT_WIDTH), lambda i: (i, 0))
    return pl.pallas_call(
        body, grid=(l // tl,), in_specs=[wide, wide, wide, wide, tab, tab, tab],
        out_specs=pl.BlockSpec((tl, ATT_IN_DIM), lambda i: (i, 0)),
        out_shape=jax.ShapeDtypeStruct((l, ATT_IN_DIM), BF16), compiler_params=_params("parallel"), name=name,
    )(dq, dk4, dv4, dgate, *tables)


GATE_HALF = ATT_WIDTH // 2
GATE_COL_BLOCK = ATT_QKV // GATE_HALF


ATT_STACK = ATT_GQA * ATT_BLOCK
BLOCK_LOG2 = ATT_BLOCK.bit_length() - 1


def _stack_masks(n):
    ri = lax.broadcasted_iota(jnp.int32, (ATT_STACK, ATT_BLOCK), 0) & (ATT_BLOCK - 1)
    cj = lax.broadcasted_iota(jnp.int32, (ATT_STACK, ATT_BLOCK), 1)
    return (cj > ri) & (n > 0), cj <= ri


def _stack_sinks(sink_ref, kvh):
    blk = lax.shift_right_logical(lax.broadcasted_iota(jnp.int32, (ATT_STACK, 1), 0), BLOCK_LOG2)
    col = jnp.zeros((ATT_STACK, 1), F32)
    for r in range(ATT_GQA):
        col = jnp.where(blk == r, sink_ref[kvh * ATT_GQA + r], col)
    return col


def _stack_fold(stack):
    head_of_lane = lax.shift_right_logical(lax.broadcasted_iota(jnp.int32, (1, GP), 1), HEAD_DIM_LOG2)
    out = jnp.zeros((ATT_BLOCK, GP), F32)
    for r in range(ATT_GQA):
        out = jnp.where(head_of_lane == r, stack[r * ATT_BLOCK:(r + 1) * ATT_BLOCK], out)
    return out


def _attn_fwd(qkv, proj, sinks, name):
    l = qkv.shape[0]
    nb = l // ATT_BLOCK

    def body(sink_ref, q_ref, kp_ref, kc_ref, vp_ref, vc_ref, g0_ref, g1_ref, og_ref, o_ref, lse_ref):
        n = pl.program_id(0)
        mask_p, mask_c = _stack_masks(n)
        ones = jnp.ones((ATT_BLOCK, LANES), BF16)
        for kvh in range(ATT_KV_HEADS):
            cols = slice(kvh * GP, (kvh + 1) * GP)
            q_stack = _head_masked_rows(q_ref[:, cols], BF16)
            sp = jnp.where(mask_p, lax.dot_general(q_stack, kp_ref[:, cols], NT_DIMS, preferred_element_type=F32), NEG_INF)
            sc = jnp.where(mask_c, lax.dot_general(q_stack, kc_ref[:, cols], NT_DIMS, preferred_element_type=F32), NEG_INF)
            sink = _stack_sinks(sink_ref, kvh)
            m = jnp.maximum(jnp.max(jnp.maximum(sp, sc), axis=1, keepdims=True), sink)
            pp = jnp.exp(sp - m).astype(BF16)
            pc = jnp.exp(sc - m).astype(BF16)
            acc = (jnp.dot(pp, jnp.concatenate([vp_ref[:, cols], ones], axis=1), preferred_element_type=F32)
                   + jnp.dot(pc, jnp.concatenate([vc_ref[:, cols], ones], axis=1), preferred_element_type=F32))
            den = acc[:, GP:] + jnp.exp(sink - m)
            inv = 1.0 / den
            o_ref[:, cols] = _stack_fold(acc[:, :GP] * jnp.concatenate([inv, inv], axis=1))
            lse = m + jnp.log(den)
            lse_ref[:, cols] = _stack_fold(jnp.concatenate([lse, lse], axis=1))
        for half, g_ref in enumerate((g0_ref, g1_ref)):
            sl = slice(half * GATE_HALF, (half + 1) * GATE_HALF)
            gate = g_ref[...]
            og_ref[:, sl] = (o_ref[:, sl] * (gate * _sigmoid(gate))).astype(og_ref.dtype)

    def prev(n):
        return jnp.maximum(n - 1, 0)

    wide = pl.BlockSpec((ATT_BLOCK, ATT_WIDTH), lambda n: (n, 0))
    return pl.pallas_call(
        body, grid=(nb,),
        in_specs=[pl.BlockSpec(memory_space=pltpu.SMEM), wide,
                  pl.BlockSpec((ATT_BLOCK, ATT_WIDTH), lambda n: (prev(n), 1)),
                  pl.BlockSpec((ATT_BLOCK, ATT_WIDTH), lambda n: (n, 1)),
                  pl.BlockSpec((ATT_BLOCK, ATT_WIDTH), lambda n: (prev(n), 2)),
                  pl.BlockSpec((ATT_BLOCK, ATT_WIDTH), lambda n: (n, 2)),
                  pl.BlockSpec((ATT_BLOCK, GATE_HALF), lambda n: (n, GATE_COL_BLOCK)),
                  pl.BlockSpec((ATT_BLOCK, GATE_HALF), lambda n: (n, GATE_COL_BLOCK + 1))],
        out_specs=[wide, wide, wide],
        out_shape=[jax.ShapeDtypeStruct((l, ATT_WIDTH), BF16), jax.ShapeDtypeStruct((l, ATT_WIDTH), F32),
                   jax.ShapeDtypeStruct((l, ATT_WIDTH), F32)],
        compiler_params=_params("parallel"), name=name,
    )(sinks, qkv, qkv, qkv, qkv, qkv, proj, proj)


def _attn_bwd(qkv, proj, sinks, o, lse, dog, name):
    l = qkv.shape[0]
    nb = l // ATT_BLOCK

    def body(sink_ref, q_ref, kp_ref, kc_ref, vp_ref, vc_ref, g0_ref, g1_ref, o_ref, lse_ref, dog_ref,
             dq_ref, dk_ref, dv_ref, dg_ref, ds_ref, ck_ref, cv_ref, do_ref):
        n = pl.program_id(0)

        @pl.when(n == 0)
        def _():
            ds_ref[...] = jnp.zeros_like(ds_ref)
            ck_ref[...] = jnp.zeros_like(ck_ref)
            cv_ref[...] = jnp.zeros_like(cv_ref)

        @pl.when(n == nb)
        def _():
            dk_ref[...] = ck_ref[...]
            dv_ref[...] = cv_ref[...]

        @pl.when(n < nb)
        def _():
            mask_p, mask_c = _stack_masks(n)
            lane = lax.broadcasted_iota(jnp.int32, (1, ATT_Q_HEADS), 1)
            for half, g_ref in enumerate((g0_ref, g1_ref)):
                sl = slice(half * GATE_HALF, (half + 1) * GATE_HALF)
                gate = g_ref[...]
                s = _sigmoid(gate)
                dogv = dog_ref[:, sl]
                do_ref[:, sl] = dogv * (gate * s)
                dg_ref[:, sl] = dogv * o_ref[:, sl] * (s * (1.0 + gate * (1.0 - s)))
            ds_acc = jnp.zeros((1, ATT_Q_HEADS), F32)
            for kvh in range(ATT_KV_HEADS):
                cols = slice(kvh * GP, (kvh + 1) * GP)
                kp, kc, vp, vc = kp_ref[:, cols], kc_ref[:, cols], vp_ref[:, cols], vc_ref[:, cols]
                q_stack = _head_masked_rows(q_ref[:, cols], BF16)
                do_g = do_ref[:, cols]
                do_stack = _head_masked_rows(do_g, BF16)
                lse_g = lse_ref[:, cols]
                lse_stack = jnp.concatenate(
                    [_both_halves(lse_g[:, (r // 2) * LANES:(r // 2 + 1) * LANES])[r % 2] for r in range(ATT_GQA)], axis=0)
                pp = jnp.exp(jnp.where(
                    mask_p, lax.dot_general(q_stack, kp, NT_DIMS, preferred_element_type=F32) - lse_stack, NEG_INF))
                pc = jnp.exp(jnp.where(
                    mask_c, lax.dot_general(q_stack, kc, NT_DIMS, preferred_element_type=F32) - lse_stack, NEG_INF))
                dpp = lax.dot_general(do_stack, vp, NT_DIMS, preferred_element_type=F32)
                dpc = lax.dot_general(do_stack, vc, NT_DIMS, preferred_element_type=F32)
                delta = jnp.sum(pp * dpp + pc * dpc, axis=1, keepdims=True)
                dsp = (pp * (dpp - delta)).astype(BF16)
                dsc = (pc * (dpc - delta)).astype(BF16)
                dq_ref[:, cols] = _stack_fold(jnp.dot(dsp, kp, preferred_element_type=F32)
                                              + jnp.dot(dsc, kc, preferred_element_type=F32))
                dk_ref[:, cols] = ck_ref[:, cols] + lax.dot_general(dsp, q_stack, TN_DIMS, preferred_element_type=F32)
                dv_ref[:, cols] = cv_ref[:, cols] + lax.dot_general(pp.astype(BF16), do_stack, TN_DIMS,
                                                                    preferred_element_type=F32)
                ck_ref[:, cols] = lax.dot_general(dsc, q_stack, TN_DIMS, preferred_element_type=F32)
                cv_ref[:, cols] = lax.dot_general(pc.astype(BF16), do_stack, TN_DIMS, preferred_element_type=F32)
                t = jnp.exp(_stack_sinks(sink_ref, kvh) - lse_stack) * delta
                for r in range(ATT_GQA):
                    tot = jnp.sum(t[r * ATT_BLOCK:(r + 1) * ATT_BLOCK], axis=0, keepdims=True)
                    ds_acc = ds_acc - jnp.where(lane == kvh * ATT_GQA + r, tot[:, :ATT_Q_HEADS], 0.0)
            ds_ref[...] += ds_acc

    def cur(n):
        return jnp.minimum(n, nb - 1)

    def prev(n):
        return jnp.maximum(n - 1, 0)

    wide = pl.BlockSpec((ATT_BLOCK, ATT_WIDTH), lambda n: (cur(n), 0))
    late = pl.BlockSpec((ATT_BLOCK, ATT_WIDTH), lambda n: (prev(n), 0))
    return pl.pallas_call(
        body, grid=(nb + 1,),
        in_specs=[pl.BlockSpec(memory_space=pltpu.SMEM), wide,
                  pl.BlockSpec((ATT_BLOCK, ATT_WIDTH), lambda n: (prev(cur(n)), 1)),
                  pl.BlockSpec((ATT_BLOCK, ATT_WIDTH), lambda n: (cur(n), 1)),
                  pl.BlockSpec((ATT_BLOCK, ATT_WIDTH), lambda n: (prev(cur(n)), 2)),
                  pl.BlockSpec((ATT_BLOCK, ATT_WIDTH), lambda n: (cur(n), 2)),
                  pl.BlockSpec((ATT_BLOCK, GATE_HALF), lambda n: (cur(n), GATE_COL_BLOCK)),
                  pl.BlockSpec((ATT_BLOCK, GATE_HALF), lambda n: (cur(n), GATE_COL_BLOCK + 1)),
                  wide, wide, wide],
        out_specs=[wide, late, late, wide, pl.BlockSpec((1, ATT_Q_HEADS), lambda n: (0, 0))],
        out_shape=[jax.ShapeDtypeStruct((l, ATT_WIDTH), F32), jax.ShapeDtypeStruct((l, ATT_WIDTH), F32),
                   jax.ShapeDtypeStruct((l, ATT_WIDTH), F32), jax.ShapeDtypeStruct((l, ATT_WIDTH), F32),
                   jax.ShapeDtypeStruct((1, ATT_Q_HEADS), F32)],
        scratch_shapes=[pltpu.VMEM((ATT_BLOCK, ATT_WIDTH), F32), pltpu.VMEM((ATT_BLOCK, ATT_WIDTH), F32),
                        pltpu.VMEM((ATT_BLOCK, ATT_WIDTH), F32)],
        compiler_params=_params("arbitrary"), name=name,
    )(sinks, qkv, qkv, qkv, qkv, qkv, proj, proj, o, lse, dog)


def _local_step(x, positions, pre_norm, post_norm, w_ssm_in, conv_w, conv_b, dt_bias, a_log, d_skip, gate_norm,
                w_ssm_out, w_att_in, sinks, w_att_out, target):
    tables = _rope_tables(positions)
    dt_bias_pad = jnp.pad(dt_bias, ((0, 0), (0, SSM_DT_PAD - SSM_HEADS)))
    d_lanes = jnp.repeat(d_skip, SSM_HEAD_DIM, axis=1).reshape(-1, SSM_GROUPS, 1, GP)
    alog_lanes = jnp.repeat(a_log, SSM_HEAD_DIM, axis=1)
    saved = []
    cur = x
    h = _rmsnorm_fwd(cur, pre_norm[0], "prenorm_fwd_0")
    for i in range(DEPTH):
        j = i // 2
        if i % 2 == 0:
            proj = _matmul(h, w_ssm_in[j], "nn", F32, f"ssm_in_{i}")
            pre, xbc = _conv_fwd(proj, conv_w[j], conv_b[j], f"conv_fwd_{i}")
            dtb, acsb, dtr, acs_r = _ssd_prep(proj, dt_bias_pad[j:j + 1], alog_lanes[j:j + 1], f"ssd_prep_{i}")
            y, act, hin = _ssd_fwd(xbc, dtb, acsb, acs_r, d_lanes[j], proj, gate_norm[j], f"ssd_fwd_{i}")
            ymix = _matmul(act, w_ssm_out[j], "nn", F32, f"ssm_out_{i}")
            saved.append(dict(x=cur, h=h, proj=proj, pre=pre, xbc=xbc, dtb=dtb, acsb=acsb, dtr=dtr, acs_r=acs_r, y=y,
                              hin=hin, act=act, ymix=ymix))
        else:
            proj = _matmul(h, w_att_in[j], "nn", F32, f"att_in_{i}")
            qkv = _rope_fwd(proj, tables, f"rope_fwd_{i}")
            act, o, lse = _attn_fwd(qkv, proj, sinks[j], f"attn_fwd_{i}")
            ymix = _matmul(act, w_att_out[j], "nn", F32, f"att_out_{i}")
            saved.append(dict(x=cur, h=h, proj=proj, qkv=qkv, o=o, lse=lse, act=act, ymix=ymix))
        if i + 1 < DEPTH:
            cur, h = _post_fwd(cur, ymix, post_norm[i], pre_norm[i + 1], f"post_fwd_{i}")

    gr = {k: [None] * 2 for k in ("ssm_w_in", "ssm_conv_w", "ssm_conv_b", "ssm_dt_bias", "ssm_a_log", "ssm_d",
                                  "ssm_gate_norm", "ssm_w_out", "att_w_in", "att_sinks", "att_w_out")}
    gr["pre_norm"] = [None] * DEPTH
    gr["post_norm"] = [None] * DEPTH
    last = DEPTH - 1
    g, dymix, loss_lanes, gr["post_norm"][last] = _post_loss(cur, ymix, post_norm[last], target, "post_loss")
    for i in reversed(range(DEPTH)):
        j = i // 2
        s = saved[i]
        if i % 2 == 0:
            dact = _matmul(dymix, w_ssm_out[j], "nt", F32, f"ssm_out_dx_{i}")
            gr["ssm_w_out"][j] = _matmul(s["act"], dymix, "tn", F32, f"ssm_out_dw_{i}")
            dxs, db, dc, ddt8, dal, dd, dproj, gr["ssm_gate_norm"][j] = _ssd_bwd(
                s["xbc"], s["dtb"], s["acsb"], s["dtr"], s["acs_r"], a_log[j], d_lanes[j], s["hin"], dact, s["y"],
                s["proj"], gate_norm[j], f"ssd_bwd_{i}")
            gr["ssm_a_log"][j] = dal.reshape(SSM_HEADS)
            gr["ssm_d"][j] = dd.reshape(SSM_HEADS)
            l = x.shape[0]
            ddt = jnp.pad(jnp.transpose(ddt8, (2, 0, 1)).reshape(l, SSM_HEADS), ((0, 0), (0, SSM_DT_PAD - SSM_HEADS)))
            dproj, dbias = _dt_bwd(ddt, s["proj"], dt_bias_pad[j:j + 1], dproj, f"dt_bwd_{i}")
            gr["ssm_dt_bias"][j] = dbias[0, :SSM_HEADS]
            dcw, dcb = [], []
            for c0, dpiece, tag in ((0, dxs, "x"), (SSM_D_INNER, db, "b"), (SSM_D_INNER + SSM_BC_DIM, dc, "c")):
                dproj, dw_, db_ = _conv_bwd(dpiece, s["pre"], s["proj"], conv_w[j], c0, dproj, f"conv_bwd_{tag}_{i}")
                dcw.append(dw_)
                dcb.append(db_)
            gr["ssm_conv_w"][j] = jnp.concatenate(dcw, axis=1)
            gr["ssm_conv_b"][j] = jnp.concatenate(dcb, axis=1)[0]
            w_in, key = w_ssm_in[j], "ssm_w_in"
        else:
            dog = _matmul(dymix, w_att_out[j], "nt", F32, f"att_out_dx_{i}")
            gr["att_w_out"][j] = _matmul(s["act"], dymix, "tn", F32, f"att_out_dw_{i}")
            dq, dk, dv, dgate, dsk = _attn_bwd(s["qkv"], s["proj"], sinks[j], s["o"], s["lse"], dog, f"attn_bwd_{i}")
            gr["att_sinks"][j] = dsk[0]
            dproj = _rope_bwd(dq, dk, dv, dgate, tables, f"rope_bwd_{i}")
            w_in, key = w_att_in[j], "att_w_in"
        dh = _matmul(dproj, w_in, "nt", F32, f"in_dx_{i}")
        gr[key][j] = _matmul(s["h"], dproj, "tn", F32, f"in_dw_{i}")
        if i > 0:
            g, dymix, gr["pre_norm"][i], gr["post_norm"][i - 1] = _norm_bwd_chain(
                dh, s["x"], pre_norm[i], g, saved[i - 1]["ymix"], post_norm[i - 1], f"norm_bwd_{i}")
        else:
            g, gr["pre_norm"][i] = _rmsnorm_bwd(dh, s["x"], pre_norm[i], g, F32, f"prenorm_bwd_{i}")
    grads = {k: jnp.stack([v.reshape(v.shape[-1]) if k in ("pre_norm", "post_norm", "ssm_gate_norm") else v for v in vs])
             for k, vs in gr.items()}
    return loss_lanes, g, grads


N_CHIPS = 4
N_DEV = 8
MESH = pl.DeviceIdType.MESH
ANY = pl.BlockSpec(memory_space=pl.ANY)


def _place():
    x, y, c = lax.axis_index("x"), lax.axis_index("y"), lax.axis_index("c")
    return x, y, c, 2 * x + y


def _chip_gather(shards, name):
    n = len(shards)

    def body(*refs):
        ins, outs = refs[:n], refs[n:2 * n]
        send_sems, recv_sems, pass_send_sems, pass_recv_sems, local_sems = refs[2 * n:]
        x, y, c, s = _place()
        local = [pltpu.make_async_copy(ins[w], outs[w].at[s], local_sems.at[w]) for w in range(n)]
        for cp in local:
            cp.start()

        def remote(w, t):
            return pltpu.make_async_remote_copy(
                src_ref=ins[w].at[c], dst_ref=outs[w].at[s, c], send_sem=send_sems.at[w, t],
                recv_sem=recv_sems.at[w, s], device_id=(t // 2, t % 2, c), device_id_type=MESH)

        def arrival(w, t):
            return pltpu.make_async_remote_copy(
                src_ref=ins[w].at[c], dst_ref=outs[w].at[t, c], send_sem=send_sems.at[w, t],
                recv_sem=recv_sems.at[w, t], device_id=(t // 2, t % 2, c), device_id_type=MESH)

        def handed_on(w, t):
            return pltpu.make_async_remote_copy(
                src_ref=outs[w].at[t, c], dst_ref=outs[w].at[t, c], send_sem=pass_send_sems.at[w, t],
                recv_sem=pass_recv_sems.at[w, t], device_id=(x, y, 1 - c), device_id_type=MESH)

        def handed_in(w, t):
            return pltpu.make_async_remote_copy(
                src_ref=outs[w].at[t, 1 - c], dst_ref=outs[w].at[t, 1 - c], send_sem=pass_send_sems.at[w, t],
                recv_sem=pass_recv_sems.at[w, t], device_id=(x, y, 1 - c), device_id_type=MESH)

        for t in range(N_CHIPS):
            @pl.when(s != t)
            def _():
                for w in range(n):
                    remote(w, t).start()
        for t in range(N_CHIPS):
            @pl.when(s != t)
            def _():
                for w in range(n):
                    arrival(w, t).wait_recv()
                    handed_on(w, t).start()
        for t in range(N_CHIPS):
            @pl.when(s != t)
            def _():
                for w in range(n):
                    remote(w, t).wait_send()
                    handed_on(w, t).wait_send()
                    handed_in(w, t).wait_recv()
        for cp in local:
            cp.wait()

    return pl.pallas_call(
        body, in_specs=[ANY] * n, out_specs=[ANY] * n,
        out_shape=[jax.ShapeDtypeStruct((N_CHIPS,) + a.shape, a.dtype) for a in shards],
        scratch_shapes=[pltpu.SemaphoreType.DMA((n, N_CHIPS)), pltpu.SemaphoreType.DMA((n, N_CHIPS)),
                        pltpu.SemaphoreType.DMA((n, N_CHIPS)), pltpu.SemaphoreType.DMA((n, N_CHIPS)),
                        pltpu.SemaphoreType.DMA((n,))],
        name=name,
    )(*shards)


def _pair_swap(parts, name):
    n = len(parts)

    def body(*refs):
        ins, outs = refs[:n], refs[n:2 * n]
        send_sems, recv_sems = refs[2 * n:]
        x, y, c, _ = _place()
        cps = [pltpu.make_async_remote_copy(
            src_ref=ins[w].at[1 - c], dst_ref=outs[w], send_sem=send_sems.at[w], recv_sem=recv_sems.at[w],
            device_id=(x, y, 1 - c), device_id_type=MESH) for w in range(n)]
        for cp in cps:
            cp.start()
        for cp in cps:
            cp.wait()

    return pl.pallas_call(
        body, in_specs=[ANY] * n, out_specs=[ANY] * n,
        out_shape=[jax.ShapeDtypeStruct(a.shape[1:], a.dtype) for a in parts],
        scratch_shapes=[pltpu.SemaphoreType.DMA((n,)), pltpu.SemaphoreType.DMA((n,))],
        name=name,
    )(*parts)


def _chip_scatter(parts, name):
    n = len(parts)
    rows = [a.shape[0] // N_CHIPS for a in parts]

    def body(*refs):
        ins, outs = refs[:n], refs[n:2 * n]
        send_sems, recv_sems, local_sems = refs[2 * n:]
        _, _, c, s = _place()

        def block(w, t):
            return ins[w].at[pl.ds(t * rows[w], rows[w])]

        local = [pltpu.make_async_copy(block(w, s), outs[w].at[s], local_sems.at[w]) for w in range(n)]
        for cp in local:
            cp.start()

        def remote(w, t):
            return pltpu.make_async_remote_copy(
                src_ref=block(w, t), dst_ref=outs[w].at[s], send_sem=send_sems.at[w, t], recv_sem=recv_sems.at[w, s],
                device_id=(t // 2, t % 2, c), device_id_type=MESH)

        def arrival(w, t):
            return pltpu.make_async_remote_copy(
                src_ref=block(w, t), dst_ref=outs[w].at[t], send_sem=send_sems.at[w, t], recv_sem=recv_sems.at[w, t],
                device_id=(t // 2, t % 2, c), device_id_type=MESH)

        for t in range(N_CHIPS):
            @pl.when(s != t)
            def _():
                for w in range(n):
                    remote(w, t).start()
        for t in range(N_CHIPS):
            @pl.when(s != t)
            def _():
                for w in range(n):
                    remote(w, t).wait_send()
                    arrival(w, t).wait_recv()
        for cp in local:
            cp.wait()

    return pl.pallas_call(
        body, in_specs=[ANY] * n, out_specs=[ANY] * n,
        out_shape=[jax.ShapeDtypeStruct((N_CHIPS, r, a.shape[1]), a.dtype) for a, r in zip(parts, rows)],
        scratch_shapes=[pltpu.SemaphoreType.DMA((n, N_CHIPS)), pltpu.SemaphoreType.DMA((n, N_CHIPS)),
                        pltpu.SemaphoreType.DMA((n,))],
        name=name,
    )(*parts)


def _pair_merge(parts, name):
    n = len(parts)

    def body(*refs):
        ins, outs = refs[:n], refs[n:2 * n]
        send_sems, recv_sems = refs[2 * n:]
        x, y, c, _ = _place()
        cps = [pltpu.make_async_remote_copy(
            src_ref=ins[w], dst_ref=outs[w], send_sem=send_sems.at[w], recv_sem=recv_sems.at[w],
            device_id=(x, y, 1 - c), device_id_type=MESH) for w in range(n)]
        for cp in cps:
            cp.start()
        for cp in cps:
            cp.wait()

    return pl.pallas_call(
        body, in_specs=[ANY] * n, out_specs=[ANY] * n,
        out_shape=[jax.ShapeDtypeStruct(a.shape, a.dtype) for a in parts],
        scratch_shapes=[pltpu.SemaphoreType.DMA((n,)), pltpu.SemaphoreType.DMA((n,))],
        name=name,
    )(*parts)


def _all_gather_small(a, name):
    def body(in_ref, out_ref, send_sems, recv_sems, local_sem):
        x, y, c, _ = _place()
        me = 4 * x + 2 * y + c
        local = pltpu.make_async_copy(in_ref, out_ref.at[me], local_sem)
        local.start()

        def remote(d):
            return pltpu.make_async_remote_copy(
                src_ref=in_ref, dst_ref=out_ref.at[me], send_sem=send_sems.at[d], recv_sem=recv_sems.at[me],
                device_id=(d // 4, (d // 2) % 2, d % 2), device_id_type=MESH)

        def arrival(d):
            return pltpu.make_async_remote_copy(
                src_ref=in_ref, dst_ref=out_ref.at[d], send_sem=send_sems.at[d], recv_sem=recv_sems.at[d],
                device_id=(d // 4, (d // 2) % 2, d % 2), device_id_type=MESH)

        for d in range(N_DEV):
            @pl.when(me != d)
            def _():
                remote(d).start()
        for d in range(N_DEV):
            @pl.when(me != d)
            def _():
                remote(d).wait_send()
                arrival(d).wait_recv()
        local.wait()

    return pl.pallas_call(
        body, in_specs=[ANY], out_specs=ANY, out_shape=jax.ShapeDtypeStruct((N_DEV,) + a.shape, a.dtype),
        scratch_shapes=[pltpu.SemaphoreType.DMA((N_DEV,)), pltpu.SemaphoreType.DMA((N_DEV,)), pltpu.SemaphoreType.DMA],
        name=name,
    )(a)


def _reduce_tile(rows):
    return _pick(rows, (256, 16))


def _pair_add(full, other, layer, name):
    _, rows, cols = full.shape
    tr = _reduce_tile(rows)

    def body(layer_ref, a_ref, b_ref, o_ref):
        o_ref[...] = (a_ref[0] + b_ref[...]).astype(o_ref.dtype)

    return pl.pallas_call(
        body,
        grid_spec=pltpu.PrefetchScalarGridSpec(
            num_scalar_prefetch=1, grid=(rows // tr,),
            in_specs=[pl.BlockSpec((1, tr, cols), lambda i, lr: (lr[0], i, 0)), pl.BlockSpec((tr, cols), lambda i, lr: (i, 0))],
            out_specs=pl.BlockSpec((tr, cols), lambda i, lr: (i, 0))),
        out_shape=jax.ShapeDtypeStruct((rows, cols), BF16), compiler_params=_params("parallel"), name=name,
    )(layer, full, other)


def _sum_slots(a, name):
    n, rows, cols = a.shape
    tr = _reduce_tile(rows)

    def body(a_ref, o_ref):
        acc = a_ref[0].astype(F32)
        for k in range(1, n):
            acc = acc + a_ref[k].astype(F32)
        o_ref[...] = acc

    return pl.pallas_call(
        body, grid=(rows // tr,), in_specs=[pl.BlockSpec((n, tr, cols), lambda i: (0, i, 0))],
        out_specs=pl.BlockSpec((tr, cols), lambda i: (i, 0)),
        out_shape=jax.ShapeDtypeStruct((rows, cols), F32), compiler_params=_params("parallel"), name=name,
    )(a)


def _adamw(w, g, m, v, name):
    rows, cols = w.shape
    tr = _pick(rows, (256, 8))

    def body(w_ref, g_ref, m_ref, v_ref, d_ref, nm_ref, nv_ref):
        gv = g_ref[...]
        mn = ADAM_B1 * m_ref[...] + (1.0 - ADAM_B1) * gv
        vn = ADAM_B2 * v_ref[...] + (1.0 - ADAM_B2) * jnp.square(gv)
        m_hat = mn / (1.0 - ADAM_B1 ** ADAM_STEP)
        v_hat = vn / (1.0 - ADAM_B2 ** ADAM_STEP)
        d_ref[...] = -ADAM_LR * (m_hat / (jnp.sqrt(v_hat) + ADAM_EPS) + ADAM_WD * w_ref[...])
        nm_ref[...] = mn
        nv_ref[...] = vn

    blk = pl.BlockSpec((tr, cols), lambda i: (i, 0))
    return pl.pallas_call(
        body, grid=(rows // tr,), in_specs=[blk] * 4, out_specs=[blk] * 3,
        out_shape=[jax.ShapeDtypeStruct((rows, cols), F32)] * 3, compiler_params=_params("parallel"), name=name,
    )(w, g, m, v)


BIG = ("ssm_w_in", "ssm_w_out", "att_w_in", "att_w_out")
SHARDED = BIG + ("ssm_conv_w",)
SMALL = ("pre_norm", "post_norm", "ssm_conv_b", "ssm_dt_bias", "ssm_a_log", "ssm_d", "ssm_gate_norm", "att_sinks")
WEIGHTS = ("pre_norm", "post_norm", "ssm_w_in", "ssm_conv_w", "ssm_conv_b", "ssm_dt_bias", "ssm_a_log", "ssm_d",
           "ssm_gate_norm", "ssm_w_out", "att_w_in", "att_sinks", "att_w_out")


def _cols_to_whole(g):
    _, two, rows, cols = g.shape
    return jnp.transpose(g, (1, 2, 0, 3)).reshape(two, rows, N_CHIPS * cols)


def _rows_to_whole(g):
    _, two, rows, cols = g.shape
    return jnp.transpose(g, (1, 0, 2, 3)).reshape(two, N_CHIPS * rows, cols)


def _cols_by_chip(g):
    two, rows, cols = g.shape
    return jnp.transpose(g.reshape(two, rows, N_CHIPS, cols // N_CHIPS), (0, 2, 1, 3)).reshape(two, N_CHIPS * rows, cols // N_CHIPS)


def _pack_small(tree, keys):
    flat = jnp.concatenate([tree[k].reshape(-1) for k in keys])
    rows = -(-flat.shape[0] // (8 * LANES)) * 8
    return jnp.pad(flat, (0, rows * LANES - flat.shape[0])).reshape(rows, LANES)


def _unpack_small(packed, shapes, keys):
    flat = packed.reshape(-1)
    out, at = {}, 0
    for k in keys:
        n = 1
        for dim in shapes[k]:
            n *= dim
        out[k] = flat[at:at + n].reshape(shapes[k])
        at += n
    return out


def kernel(x, positions, pre_norm, post_norm, ssm_w_in, ssm_conv_w, ssm_conv_b, ssm_dt_bias, ssm_a_log, ssm_d, ssm_gate_norm, ssm_w_out, att_w_in, att_sinks, att_w_out, loss_target, m_pre_norm, m_post_norm, m_ssm_w_in, m_ssm_conv_w, m_ssm_conv_b, m_ssm_dt_bias, m_ssm_a_log, m_ssm_d, m_ssm_gate_norm, m_ssm_w_out, m_att_w_in, m_att_sinks, m_att_w_out, v_pre_norm, v_post_norm, v_ssm_w_in, v_ssm_conv_w, v_ssm_conv_b, v_ssm_dt_bias, v_ssm_a_log, v_ssm_d, v_ssm_gate_norm, v_ssm_w_out, v_att_w_in, v_att_sinks, v_att_w_out):
    w = dict(pre_norm=pre_norm, post_norm=post_norm, ssm_w_in=ssm_w_in, ssm_conv_w=ssm_conv_w, ssm_conv_b=ssm_conv_b,
             ssm_dt_bias=ssm_dt_bias, ssm_a_log=ssm_a_log, ssm_d=ssm_d, ssm_gate_norm=ssm_gate_norm, ssm_w_out=ssm_w_out,
             att_w_in=att_w_in, att_sinks=att_sinks, att_w_out=att_w_out)
    m = dict(pre_norm=m_pre_norm, post_norm=m_post_norm, ssm_w_in=m_ssm_w_in, ssm_conv_w=m_ssm_conv_w, ssm_conv_b=m_ssm_conv_b,
             ssm_dt_bias=m_ssm_dt_bias, ssm_a_log=m_ssm_a_log, ssm_d=m_ssm_d, ssm_gate_norm=m_ssm_gate_norm,
             ssm_w_out=m_ssm_w_out, att_w_in=m_att_w_in, att_sinks=m_att_sinks, att_w_out=m_att_w_out)
    v = dict(pre_norm=v_pre_norm, post_norm=v_post_norm, ssm_w_in=v_ssm_w_in, ssm_conv_w=v_ssm_conv_w, ssm_conv_b=v_ssm_conv_b,
             ssm_dt_bias=v_ssm_dt_bias, ssm_a_log=v_ssm_a_log, ssm_d=v_ssm_d, ssm_gate_norm=v_ssm_gate_norm,
             ssm_w_out=v_ssm_w_out, att_w_in=v_att_w_in, att_sinks=v_att_sinks, att_w_out=v_att_w_out)
    c = lax.axis_index("c")
    chip = 2 * lax.axis_index("x") + lax.axis_index("y")

    g_in, g_out, g_ain, g_aout, g_cw = _chip_gather(
        [ssm_w_in.astype(BF16), ssm_w_out.astype(BF16), att_w_in.astype(BF16), att_w_out.astype(BF16), ssm_conv_w],
        "gather_weights")
    w_in_full = jnp.pad(_cols_to_whole(g_in), ((0, 0), (0, 0), (0, SSM_IN_PAD - SSM_IN_DIM)))
    loss_lanes, grad_x, gr = _local_step(
        x[0], positions[0], pre_norm, post_norm, w_in_full, _cols_to_whole(g_cw), ssm_conv_b, ssm_dt_bias, ssm_a_log,
        ssm_d, ssm_gate_norm, _rows_to_whole(g_out), _cols_to_whole(g_ain), att_sinks, _rows_to_whole(g_aout),
        loss_target[0])
    loss = lax.psum(0.5 * jnp.sum(loss_lanes) / D_MODEL, ("x", "y", "c"))

    parts = [_cols_by_chip(gr["ssm_w_in"][:, :, :SSM_IN_DIM]), gr["ssm_w_out"], _cols_by_chip(gr["att_w_in"]),
             gr["att_w_out"]]
    from_sibling = _pair_swap(parts, "reduce_pair_swap")
    layer = jnp.reshape(c, (1,)).astype(jnp.int32)
    chip_sums = [_pair_add(p, o, layer, f"reduce_pair_add_{k}") for k, (p, o) in enumerate(zip(parts, from_sibling))]
    by_chip = _chip_scatter(chip_sums, "reduce_chip_scatter")
    mine = [_sum_slots(a, f"reduce_chip_sum_{k}") for k, a in enumerate(by_chip)]
    theirs = _pair_merge(mine, "reduce_pair_merge")
    grads = {k: jnp.stack([jnp.where(c == 0, a, b), jnp.where(c == 0, b, a)]).reshape(w[k].shape)
             for k, a, b in zip(BIG, mine, theirs)}

    small_keys = SMALL + ("ssm_conv_w",)
    small_shapes = {k: w[k].shape for k in SMALL}
    small_shapes["ssm_conv_w"] = gr["ssm_conv_w"].shape
    small_sum = _sum_slots(_all_gather_small(_pack_small(gr, small_keys), "reduce_small_gather"), "reduce_small_sum")
    grads.update(_unpack_small(small_sum, small_shapes, small_keys))
    conv_cols = ssm_conv_w.shape[2]
    grads["ssm_conv_w"] = lax.dynamic_slice_in_dim(grads["ssm_conv_w"], chip * conv_cols, conv_cols, axis=2)

    delta, new_m, new_v = {}, {}, {}
    for k in SHARDED:
        shp = w[k].shape
        two_d = (shp[0] * shp[1], shp[2])
        d_, m_, v_ = _adamw(w[k].reshape(two_d), grads[k].reshape(two_d), m[k].reshape(two_d), v[k].reshape(two_d),
                            f"adamw_{k}")
        delta[k], new_m[k], new_v[k] = d_.reshape(shp), m_.reshape(shp), v_.reshape(shp)
    d_, m_, v_ = _adamw(_pack_small(w, SMALL), _pack_small(grads, SMALL), _pack_small(m, SMALL), _pack_small(v, SMALL),
                        "adamw_small")
    delta.update(_unpack_small(d_, small_shapes, SMALL))
    new_m.update(_unpack_small(m_, small_shapes, SMALL))
    new_v.update(_unpack_small(v_, small_shapes, SMALL))

    return (loss, grad_x[None], *[grads[k] for k in WEIGHTS], *[delta[k] for k in WEIGHTS],
            *[new_m[k] for k in WEIGHTS], *[new_v[k] for k in WEIGHTS])
```

```python
import functools

import jax
import jax.numpy as jnp
from jax import lax
from jax.experimental import pallas as pl
from jax.experimental.pallas import tpu as pltpu

F32 = jnp.float32
BF16 = jnp.bfloat16
EPS = 1e-6
NEG_INF = float("-inf")

D_MODEL = 1024
DEPTH = 4
SSM_D_INNER = 2048
SSM_HEAD_DIM = 64
SSM_HEADS = 32
SSM_GROUPS = 8
SSM_HPG = 4
SSM_STATE = 128
SSM_CONV = 4
SSM_CHUNK = 128
SSM_BC_DIM = 1024
SSM_CONV_DIM = 4096
SSM_IN_DIM = 6176
SSM_IN_PAD = 6272
SSM_DT_PAD = 128
ATT_HEAD_DIM = 64
ATT_Q_HEADS = 16
ATT_KV_HEADS = 4
ATT_GQA = 4
ATT_WIDTH = 1024
ATT_KV_WIDTH = 256
ATT_IN_DIM = 2560
ATT_QKV = ATT_WIDTH + 2 * ATT_KV_WIDTH
ATT_BLOCK = 128
ROPE_THETA = 500000.0
ROPE_DIM = 16
ROPE_HALF = 8
Q_SCALE = ATT_HEAD_DIM ** -0.5

ADAM_LR = 0.001
ADAM_B1 = 0.9
ADAM_B2 = 0.999
ADAM_EPS = 1e-08
ADAM_WD = 0.01
ADAM_STEP = 10

VMEM_LIMIT_BYTES = 48 * 1024 * 1024
NT_DIMS = (((1,), (1,)), ((), ()))
TN_DIMS = (((0,), (0,)), ((), ()))


def _params(*sem):
    return pltpu.CompilerParams(dimension_semantics=sem, vmem_limit_bytes=VMEM_LIMIT_BYTES)


def _pick(n, cands):
    for c in cands:
        if n % c == 0:
            return c
    return n


def _sigmoid(v):
    return 0.5 * jnp.tanh(0.5 * v) + 0.5


def _bdot(a, b):
    return jnp.dot(a.astype(BF16), b.astype(BF16), preferred_element_type=F32)


def _bdot_nt(a, b):
    return lax.dot_general(a.astype(BF16), b.astype(BF16), NT_DIMS, preferred_element_type=F32)


def _bdot_tn(a, b):
    return lax.dot_general(a.astype(BF16), b.astype(BF16), TN_DIMS, preferred_element_type=F32)


MATMUL_VMEM_BUDGET = 36 * 1024 * 1024


def _matmul_tiles(m, n, k, out_bytes, reduce_rows):
    best = None
    whole = [k] if (not reduce_rows or k <= 2048) else []
    for tk in whole + [c for c in (4096, 2048, 1024, 896, 512) if k % c == 0 and c < k]:
        for tm in (c for c in (2048, 1024, 512, 256) if m % c == 0):
            for tn in (c for c in (n, 1280, 1024, 896, 640, 512) if n % c == 0):
                acc = tm * tn * 4 if tk < k else 0
                need = 2 * (2 * tk * (tm + tn) + tm * tn * out_bytes) + acc
                if need <= MATMUL_VMEM_BUDGET and (best is None or tm * tn * min(tk, 2048) > best[0]):
                    best = (tm * tn * min(tk, 2048), tm, tn, tk)
        if best is not None and not reduce_rows:
            break
    return best[1:]


def _matmul(a, b, mode, out_dtype, name):
    if mode == "nn":
        (m, k), n = a.shape, b.shape[1]
    elif mode == "nt":
        (m, k), n = a.shape, b.shape[0]
    else:
        (k, m), n = a.shape, b.shape[1]
    tm, tn, tk = _matmul_tiles(m, n, k, jnp.dtype(out_dtype).itemsize, mode == "tn")
    nk = k // tk
    dims = {"nn": (((1,), (0,)), ((), ())), "nt": NT_DIMS, "tn": TN_DIMS}[mode]

    def body(a_ref, b_ref, o_ref, acc_ref):
        kk = pl.program_id(2)
        part = lax.dot_general(a_ref[...], b_ref[...], dims, preferred_element_type=F32)
        if nk == 1:
            o_ref[...] = part.astype(o_ref.dtype)
        else:
            @pl.when(kk == 0)
            def _():
                acc_ref[...] = part

            @pl.when(kk > 0)
            def _():
                acc_ref[...] += part

            @pl.when(kk == nk - 1)
            def _():
                o_ref[...] = acc_ref[...].astype(o_ref.dtype)

    if mode == "nn":
        a_spec = pl.BlockSpec((tm, tk), lambda j, i, kk: (i, kk))
        b_spec = pl.BlockSpec((tk, tn), lambda j, i, kk: (kk, j))
    elif mode == "nt":
        a_spec = pl.BlockSpec((tm, tk), lambda j, i, kk: (i, kk))
        b_spec = pl.BlockSpec((tn, tk), lambda j, i, kk: (j, kk))
    else:
        a_spec = pl.BlockSpec((tk, tm), lambda j, i, kk: (kk, i))
        b_spec = pl.BlockSpec((tk, tn), lambda j, i, kk: (kk, j))
    return pl.pallas_call(
        body, grid=(n // tn, m // tm, nk), in_specs=[a_spec, b_spec],
        out_specs=pl.BlockSpec((tm, tn), lambda j, i, kk: (i, j)),
        out_shape=jax.ShapeDtypeStruct((m, n), out_dtype),
        scratch_shapes=[pltpu.VMEM((tm, tn), F32)],
        compiler_params=_params("parallel", "parallel", "arbitrary"), name=name,
    )(a, b)


def _row_tile(l):
    return _pick(l, (512, 256, 128))


def _rmsnorm_fwd(x, w, name):
    l, d = x.shape
    tl = _row_tile(l)

    def body(x_ref, w_ref, o_ref):
        xv = x_ref[...]
        r = lax.rsqrt(jnp.mean(xv * xv, axis=-1, keepdims=True) + EPS)
        o_ref[...] = (xv * r * w_ref[...]).astype(o_ref.dtype)

    return pl.pallas_call(
        body, grid=(l // tl,),
        in_specs=[pl.BlockSpec((tl, d), lambda i: (i, 0)), pl.BlockSpec((1, d), lambda i: (0, 0))],
        out_specs=pl.BlockSpec((tl, d), lambda i: (i, 0)),
        out_shape=jax.ShapeDtypeStruct((l, d), BF16), compiler_params=_params("parallel"), name=name,
    )(x, w.reshape(1, d))


def _post_fwd(x, y, w, w_next, name):
    l, d = x.shape
    tl = _row_tile(l)

    def body(x_ref, y_ref, w_ref, wn_ref, o_ref, h_ref):
        yv = y_ref[...]
        r = lax.rsqrt(jnp.mean(yv * yv, axis=-1, keepdims=True) + EPS)
        out = x_ref[...] + yv * r * w_ref[...]
        o_ref[...] = out
        rn = lax.rsqrt(jnp.mean(out * out, axis=-1, keepdims=True) + EPS)
        h_ref[...] = (out * rn * wn_ref[...]).astype(h_ref.dtype)

    row = pl.BlockSpec((tl, d), lambda i: (i, 0))
    vec = pl.BlockSpec((1, d), lambda i: (0, 0))
    return pl.pallas_call(
        body, grid=(l // tl,), in_specs=[row, row, vec, vec], out_specs=[row, row],
        out_shape=[jax.ShapeDtypeStruct((l, d), F32), jax.ShapeDtypeStruct((l, d), BF16)],
        compiler_params=_params("parallel"), name=name,
    )(x, y, w.reshape(1, d), w_next.reshape(1, d))


def _post_loss(x, y, w, t, name):
    l, d = x.shape
    tl = _row_tile(l)
    nt = l // tl

    def body(x_ref, y_ref, w_ref, t_ref, g_ref, dy_ref, ls_ref, dw_ref, acc_ref):
        i = pl.program_id(0)

        @pl.when(i == 0)
        def _():
            ls_ref[...] = jnp.zeros_like(ls_ref)
            acc_ref[...] = jnp.zeros_like(acc_ref)

        yv = y_ref[...]
        r = lax.rsqrt(jnp.mean(yv * yv, axis=-1, keepdims=True) + EPS)
        nrm = yv * r
        e = x_ref[...] + nrm * w_ref[...] - t_ref[...]
        gv = e * (1.0 / d)
        g_ref[...] = gv
        ls_ref[...] += jnp.sum((e * e).reshape(tl // 8, 8, d), axis=0)
        gw = gv * w_ref[...]
        dy_ref[...] = (r * (gw - nrm * jnp.mean(gw * nrm, axis=-1, keepdims=True))).astype(dy_ref.dtype)
        acc_ref[...] += jnp.sum((gv * nrm).reshape(tl // 8, 8, d), axis=0)

        @pl.when(i == nt - 1)
        def _():
            dw_ref[...] = jnp.sum(acc_ref[...], axis=0, keepdims=True)

    row = pl.BlockSpec((tl, d), lambda i: (i, 0))
    vec = pl.BlockSpec((1, d), lambda i: (0, 0))
    return pl.pallas_call(
        body, grid=(nt,), in_specs=[row, row, vec, row],
        out_specs=[row, row, pl.BlockSpec((8, d), lambda i: (0, 0)), vec],
        out_shape=[jax.ShapeDtypeStruct((l, d), F32), jax.ShapeDtypeStruct((l, d), BF16),
                   jax.ShapeDtypeStruct((8, d), F32), jax.ShapeDtypeStruct((1, d), F32)],
        scratch_shapes=[pltpu.VMEM((8, d), F32)], compiler_params=_params("arbitrary"), name=name,
    )(x, y, w.reshape(1, d), t)


def _norm_bwd_chain(dh, x, w_pre, resid, y_prev, w_post_prev, name):
    l, d = x.shape
    tl = _row_tile(l)
    nt = l // tl

    def body(dh_ref, x_ref, wp_ref, r_ref, y_ref, wq_ref, g_ref, dy_ref, dwp_ref, dwq_ref, accp_ref, accq_ref):
        i = pl.program_id(0)

        @pl.when(i == 0)
        def _():
            accp_ref[...] = jnp.zeros_like(accp_ref)
            accq_ref[...] = jnp.zeros_like(accq_ref)

        xv = x_ref[...]
        dhv = dh_ref[...]
        rx = lax.rsqrt(jnp.mean(xv * xv, axis=-1, keepdims=True) + EPS)
        nx = xv * rx
        gw = dhv * wp_ref[...]
        gv = rx * (gw - nx * jnp.mean(gw * nx, axis=-1, keepdims=True)) + r_ref[...]
        g_ref[...] = gv
        accp_ref[...] += jnp.sum((dhv * nx).reshape(tl // 8, 8, d), axis=0)
        yv = y_ref[...]
        ry = lax.rsqrt(jnp.mean(yv * yv, axis=-1, keepdims=True) + EPS)
        ny = yv * ry
        gq = gv * wq_ref[...]
        dy_ref[...] = (ry * (gq - ny * jnp.mean(gq * ny, axis=-1, keepdims=True))).astype(dy_ref.dtype)
        accq_ref[...] += jnp.sum((gv * ny).reshape(tl // 8, 8, d), axis=0)

        @pl.when(i == nt - 1)
        def _():
            dwp_ref[...] = jnp.sum(accp_ref[...], axis=0, keepdims=True)
            dwq_ref[...] = jnp.sum(accq_ref[...], axis=0, keepdims=True)

    row = pl.BlockSpec((tl, d), lambda i: (i, 0))
    vec = pl.BlockSpec((1, d), lambda i: (0, 0))
    return pl.pallas_call(
        body, grid=(nt,), in_specs=[row, row, vec, row, row, vec], out_specs=[row, row, vec, vec],
        out_shape=[jax.ShapeDtypeStruct((l, d), F32), jax.ShapeDtypeStruct((l, d), BF16),
                   jax.ShapeDtypeStruct((1, d), F32), jax.ShapeDtypeStruct((1, d), F32)],
        scratch_shapes=[pltpu.VMEM((8, d), F32), pltpu.VMEM((8, d), F32)],
        compiler_params=_params("arbitrary"), name=name,
    )(dh, x, w_pre.reshape(1, d), resid, y_prev, w_post_prev.reshape(1, d))


def _rmsnorm_bwd(g, y, w, resid, out_dtype, name):
    l, d = y.shape
    tl = _row_tile(l)
    nt = l // tl
    has_resid = resid is not None

    def body(*refs):
        if has_resid:
            g_ref, y_ref, w_ref, r_ref, dy_ref, dw_ref, acc_ref = refs
        else:
            g_ref, y_ref, w_ref, dy_ref, dw_ref, acc_ref = refs
        i = pl.program_id(0)

        @pl.when(i == 0)
        def _():
            acc_ref[...] = jnp.zeros_like(acc_ref)

        yv = y_ref[...]
        gv = g_ref[...].astype(F32)
        r = lax.rsqrt(jnp.mean(yv * yv, axis=-1, keepdims=True) + EPS)
        nrm = yv * r
        gw = gv * w_ref[...]
        dy = r * (gw - nrm * jnp.mean(gw * nrm, axis=-1, keepdims=True))
        if has_resid:
            dy = dy + r_ref[...]
        dy_ref[...] = dy.astype(dy_ref.dtype)
        acc_ref[...] += jnp.sum((gv * nrm).reshape(tl // 8, 8, d), axis=0)

        @pl.when(i == nt - 1)
        def _():
            dw_ref[...] = jnp.sum(acc_ref[...], axis=0, keepdims=True)

    row = pl.BlockSpec((tl, d), lambda i: (i, 0))
    vec = pl.BlockSpec((1, d), lambda i: (0, 0))
    ins = [g, y, w.reshape(1, d)] + ([resid] if has_resid else [])
    return pl.pallas_call(
        body, grid=(nt,), in_specs=[row, row, vec] + ([row] if has_resid else []),
        out_specs=[row, vec],
        out_shape=[jax.ShapeDtypeStruct((l, d), out_dtype), jax.ShapeDtypeStruct((1, d), F32)],
        scratch_shapes=[pltpu.VMEM((8, d), F32)], compiler_params=_params("arbitrary"), name=name,
    )(*ins)


CONV_COLS = 512
HALO = 8
HALO16 = 16
CONV_SUB_ROWS = 64
CONV_SUB_COLS = 256


def _conv_fwd(proj, cw, cb, name):
    l = proj.shape[0]
    tl = _row_tile(l)
    off = SSM_D_INNER // CONV_COLS

    def body(u_ref, halo_ref, w_ref, b_ref, pre_ref, act_ref, ext_ref):
        i = pl.program_id(1)
        ext_ref[0:HALO, :] = jnp.where(i > 0, halo_ref[...], 0.0)
        ext_ref[HALO:HALO + tl, :] = u_ref[...]
        for r0 in range(0, tl, CONV_SUB_ROWS):
            for c0 in range(0, CONV_COLS, CONV_SUB_COLS):
                cs = slice(c0, c0 + CONV_SUB_COLS)
                ext = ext_ref[r0:r0 + CONV_SUB_ROWS + HALO, cs]
                acc = b_ref[:, cs] + w_ref[SSM_CONV - 1:SSM_CONV, cs] * ext[HALO:]
                for k in range(SSM_CONV - 1):
                    acc = acc + w_ref[k:k + 1, cs] * pltpu.roll(ext, SSM_CONV - 1 - k, 0)[HALO:]
                pre_ref[r0:r0 + CONV_SUB_ROWS, cs] = acc.astype(pre_ref.dtype)
                act_ref[r0:r0 + CONV_SUB_ROWS, cs] = (acc * _sigmoid(acc)).astype(act_ref.dtype)

    hb = tl // HALO
    out = pl.BlockSpec((tl, CONV_COLS), lambda j, i: (i, j))
    return pl.pallas_call(
        body, grid=(SSM_CONV_DIM // CONV_COLS, l // tl),
        in_specs=[pl.BlockSpec((tl, CONV_COLS), lambda j, i: (i, off + j)),
                  pl.BlockSpec((HALO, CONV_COLS), lambda j, i: (jnp.maximum(i * hb - 1, 0), off + j)),
                  pl.BlockSpec((SSM_CONV, CONV_COLS), lambda j, i: (0, j)),
                  pl.BlockSpec((1, CONV_COLS), lambda j, i: (0, j))],
        out_specs=[out, out],
        out_shape=[jax.ShapeDtypeStruct((l, SSM_CONV_DIM), BF16)] * 2,
        scratch_shapes=[pltpu.VMEM((tl + HALO, CONV_COLS), F32)],
        compiler_params=_params("parallel", "arbitrary"), name=name,
    )(proj, proj, cw, cb.reshape(1, SSM_CONV_DIM))


def _conv_bwd(dact, pre, proj, cw, c0, dproj, name):
    l, width = dact.shape
    tl = _row_tile(l)
    nt = l // tl
    pre_off = c0 // CONV_COLS
    u_off = (SSM_D_INNER + c0) // CONV_COLS
    hb = tl // HALO
    hb16 = tl // HALO16
    last_hb16 = l // HALO16 - 1

    def body(da_ref, da_h_ref, p_ref, p_h_ref, u_ref, u_h_ref, w_ref, _, du_ref, dw_ref, db_ref, ext_ref, uext_ref):
        i = pl.program_id(1)

        @pl.when(i == 0)
        def _():
            dw_ref[...] = jnp.zeros_like(dw_ref)
            db_ref[...] = jnp.zeros_like(db_ref)

        def dpre_of(da, p):
            s = _sigmoid(p)
            return da * (s * (1.0 + p * (1.0 - s)))

        ext_ref[0:tl, :] = dpre_of(da_ref[...].astype(F32), p_ref[...].astype(F32))
        ext_ref[tl:tl + HALO, :] = jnp.where(
            i < nt - 1, dpre_of(da_h_ref[...].astype(F32)[:HALO], p_h_ref[...].astype(F32)[:HALO]), 0.0)
        uext_ref[0:HALO, :] = jnp.where(i > 0, u_h_ref[...], 0.0)
        uext_ref[HALO:HALO + tl, :] = u_ref[...]
        sub = CONV_SUB_ROWS
        for c0 in range(0, CONV_COLS, CONV_SUB_COLS):
            cs = slice(c0, c0 + CONV_SUB_COLS)
            dws = [jnp.zeros((1, CONV_SUB_COLS), F32) for _ in range(SSM_CONV)]
            dbs = jnp.zeros((1, CONV_SUB_COLS), F32)
            for r0 in range(0, tl, sub):
                dext = ext_ref[r0:r0 + sub + HALO, cs]
                uext = uext_ref[r0:r0 + sub + HALO, cs]
                dp = dext[:sub]
                du = w_ref[SSM_CONV - 1:SSM_CONV, cs] * dp
                dws[SSM_CONV - 1] = dws[SSM_CONV - 1] + jnp.sum(dp * uext[HALO:], axis=0, keepdims=True)
                for k in range(SSM_CONV - 1):
                    j = SSM_CONV - 1 - k
                    du = du + w_ref[k:k + 1, cs] * pltpu.roll(dext, sub + HALO - j, 0)[:sub]
                    dws[k] = dws[k] + jnp.sum(dp * pltpu.roll(uext, j, 0)[HALO:], axis=0, keepdims=True)
                dbs = dbs + jnp.sum(dp, axis=0, keepdims=True)
                du_ref[r0:r0 + sub, cs] = du.astype(du_ref.dtype)
            for k in range(SSM_CONV):
                dw_ref[k:k + 1, cs] += dws[k]
            db_ref[:, cs] += dbs

    return pl.pallas_call(
        body, grid=(width // CONV_COLS, nt),
        in_specs=[pl.BlockSpec((tl, CONV_COLS), lambda j, i: (i, j)),
                  pl.BlockSpec((HALO16, CONV_COLS), lambda j, i: (jnp.minimum((i + 1) * hb16, last_hb16), j)),
                  pl.BlockSpec((tl, CONV_COLS), lambda j, i: (i, pre_off + j)),
                  pl.BlockSpec((HALO16, CONV_COLS), lambda j, i: (jnp.minimum((i + 1) * hb16, last_hb16), pre_off + j)),
                  pl.BlockSpec((tl, CONV_COLS), lambda j, i: (i, u_off + j)),
                  pl.BlockSpec((HALO, CONV_COLS), lambda j, i: (jnp.maximum(i * hb - 1, 0), u_off + j)),
                  pl.BlockSpec((SSM_CONV, CONV_COLS), lambda j, i: (0, pre_off + j)),
                  pl.BlockSpec(memory_space=pl.ANY)],
        out_specs=[pl.BlockSpec((tl, CONV_COLS), lambda j, i: (i, u_off + j)),
                   pl.BlockSpec((SSM_CONV, CONV_COLS), lambda j, i: (0, j)),
                   pl.BlockSpec((1, CONV_COLS), lambda j, i: (0, j))],
        out_shape=[jax.ShapeDtypeStruct(dproj.shape, dproj.dtype), jax.ShapeDtypeStruct((SSM_CONV, width), F32),
                   jax.ShapeDtypeStruct((1, width), F32)],
        scratch_shapes=[pltpu.VMEM((tl + HALO, CONV_COLS), F32), pltpu.VMEM((tl + HALO, CONV_COLS), F32)],
        input_output_aliases={7: 0}, compiler_params=_params("parallel", "arbitrary"), name=name,
    )(dact, dact, pre, pre, proj, proj, cw, dproj)


DT_COL_BLOCK = (SSM_D_INNER + SSM_CONV_DIM) // SSM_DT_PAD


def _split3(v):
    hi = v.astype(BF16)
    rest = v - hi.astype(F32)
    mid = rest.astype(BF16)
    lo = (rest - mid.astype(F32)).astype(BF16)
    return hi, mid, lo


def _ssd_prep(proj, bias, alog_lanes, name):
    l = proj.shape[0]
    nc = l // SSM_CHUNK
    head_dim_log2 = SSM_HEAD_DIM.bit_length() - 1

    def body(p_ref, b_ref, al_ref, dtb_ref, acsb_ref, dtr_ref, acsr_ref):
        v = p_ref[...] + b_ref[...]
        dt = jnp.maximum(v, 0.0) + jnp.log1p(jnp.exp(-jnp.abs(v)))
        head_of_lane = lax.shift_right_logical(lax.broadcasted_iota(jnp.int32, (SSM_DT_PAD, SSM_D_INNER), 1), head_dim_log2)
        spread = (head_of_lane == lax.broadcasted_iota(jnp.int32, (SSM_DT_PAD, SSM_D_INNER), 0)).astype(BF16)
        dtb = sum(jnp.dot(piece, spread, preferred_element_type=F32) for piece in _split3(dt)[:2])
        dtb_ref[...] = dtb
        ri = lax.broadcasted_iota(jnp.int32, (SSM_CHUNK, SSM_CHUNK), 0)
        cj = lax.broadcasted_iota(jnp.int32, (SSM_CHUNK, SSM_CHUNK), 1)
        tri = (ri >= cj).astype(BF16)
        acsb = sum(jnp.dot(tri, piece, preferred_element_type=F32) for piece in _split3(dtb * (-jnp.exp(al_ref[...]))))
        acsb_ref[...] = acsb
        gp = SSM_HPG * SSM_HEAD_DIM
        lane = lax.broadcasted_iota(jnp.int32, (SSM_HPG, gp), 1)
        pick = (lane == lax.broadcasted_iota(jnp.int32, (SSM_HPG, gp), 0) * SSM_HEAD_DIM).astype(BF16)
        for g in range(SSM_GROUPS):
            cols = slice(g * gp, (g + 1) * gp)
            dtr_ref[g] = sum(lax.dot_general(pick, piece, NT_DIMS, preferred_element_type=F32)
                             for piece in _split3(dtb[:, cols]))
            acsr_ref[g] = sum(lax.dot_general(pick, piece, NT_DIMS, preferred_element_type=F32)
                              for piece in _split3(acsb[:, cols]))

    rows = pl.BlockSpec((SSM_GROUPS, SSM_HPG, SSM_CHUNK), lambda c: (0, 0, c))
    dense = pl.BlockSpec((SSM_CHUNK, SSM_D_INNER), lambda c: (c, 0))
    return pl.pallas_call(
        body, grid=(nc,),
        in_specs=[pl.BlockSpec((SSM_CHUNK, SSM_DT_PAD), lambda c: (c, DT_COL_BLOCK)),
                  pl.BlockSpec((1, SSM_DT_PAD), lambda c: (0, 0)),
                  pl.BlockSpec((1, SSM_D_INNER), lambda c: (0, 0))],
        out_specs=[dense, dense, rows, rows],
        out_shape=[jax.ShapeDtypeStruct((l, SSM_D_INNER), F32), jax.ShapeDtypeStruct((l, SSM_D_INNER), F32),
                   jax.ShapeDtypeStruct((SSM_GROUPS, SSM_HPG, l), F32),
                   jax.ShapeDtypeStruct((SSM_GROUPS, SSM_HPG, l), F32)],
        compiler_params=_params("parallel"), name=name,
    )(proj, bias, alog_lanes)


def _dt_bwd(ddt, proj, bias, dproj, name):
    l = proj.shape[0]
    tl = _row_tile(l)

    def body(g_ref, p_ref, b_ref, _, o_ref, db_ref):
        @pl.when(pl.program_id(0) == 0)
        def _():
            db_ref[...] = jnp.zeros_like(db_ref)

        d = g_ref[...] * _sigmoid(p_ref[...] + b_ref[...])
        o_ref[...] = d.astype(o_ref.dtype)
        db_ref[...] += jnp.sum(d, axis=0, keepdims=True)

    return pl.pallas_call(
        body, grid=(l // tl,),
        in_specs=[pl.BlockSpec((tl, SSM_DT_PAD), lambda i: (i, 0)),
                  pl.BlockSpec((tl, SSM_DT_PAD), lambda i: (i, DT_COL_BLOCK)),
                  pl.BlockSpec((1, SSM_DT_PAD), lambda i: (0, 0)),
                  pl.BlockSpec(memory_space=pl.ANY)],
        out_specs=[pl.BlockSpec((tl, SSM_DT_PAD), lambda i: (i, DT_COL_BLOCK)),
                   pl.BlockSpec((1, SSM_DT_PAD), lambda i: (0, 0))],
        out_shape=[jax.ShapeDtypeStruct(dproj.shape, dproj.dtype), jax.ShapeDtypeStruct((1, SSM_DT_PAD), F32)],
        input_output_aliases={3: 0}, compiler_params=_params("arbitrary"), name=name,
    )(ddt, proj, bias, dproj)


GP = SSM_HPG * SSM_HEAD_DIM
HEAD_DIM_LOG2 = SSM_HEAD_DIM.bit_length() - 1
CHUNK_LOG2 = SSM_CHUNK.bit_length() - 1
GPS = 8
B_BLOCK0 = SSM_D_INNER // SSM_STATE
C_BLOCK0 = (SSM_D_INNER + SSM_BC_DIM) // SSM_STATE


def _chunk_iotas():
    ri = lax.broadcasted_iota(jnp.int32, (SSM_CHUNK, SSM_CHUNK), 0)
    cj = lax.broadcasted_iota(jnp.int32, (SSM_CHUNK, SSM_CHUNK), 1)
    return ri, cj


def _head_decay(acsb, acs_r, r, ri, cj):
    pair = acsb[:, (r // 2) * LANES:(r // 2 + 1) * LANES]
    mine_low = r % 2 == 0
    lane = lax.broadcasted_iota(jnp.int32, (1, LANES), 1)
    col = jnp.where((lane < SSM_HEAD_DIM) == mine_low, pair, pltpu.roll(pair, SSM_HEAD_DIM, 1))
    return jnp.exp(jnp.where(ri >= cj, col - acs_r[r:r + 1, :], NEG_INF))


def _head_masked_rows(v, dtype):
    head_of_lane = lax.shift_right_logical(lax.broadcasted_iota(jnp.int32, (1, GP), 1), HEAD_DIM_LOG2)
    return jnp.concatenate([jnp.where(head_of_lane == r, v, 0.0).astype(dtype) for r in range(SSM_HPG)], axis=0)


def _ssd_fwd(xbc, dtb, acsb, acs_r, d_lanes, proj, gate_w, name, ride=()):
    l = xbc.shape[0]
    nc = l // SSM_CHUNK
    assert GPS == SSM_GROUPS

    n_ride = len(ride)

    def body(*refs):
        x_ref, b_ref, c_ref, dtb_ref, acsb_ref, acsr_ref, d_ref, z_ref, gw_ref = refs[:9]
        ride_in = refs[9:9 + n_ride]
        y_ref, act_ref, hin_ref = refs[9 + n_ride:12 + n_ride]
        ride_out = refs[12 + n_ride:12 + 2 * n_ride]
        h_ref = refs[12 + 2 * n_ride]
        ride_sems = refs[13 + 2 * n_ride:]
        c = pl.program_id(0)
        if n_ride:
            @pl.when(c == 0)
            def _():
                _gather_between_chips(ride_in, ride_out, *ride_sems, wait=False)

            @pl.when(c == nc - 1)
            def _():
                _gather_between_chips(ride_in, ride_out, *ride_sems, wait=True)

        ri, cj = _chunk_iotas()
        for k in range(GPS):
            g = k
            cols = slice(k * GP, (k + 1) * GP)
            ncols = slice(k * SSM_STATE, (k + 1) * SSM_STATE)

            @pl.when(c == 0)
            def _():
                h_ref[g] = jnp.zeros((SSM_STATE, GP), F32)

            xv = x_ref[:, cols].astype(F32)
            bb = b_ref[:, ncols].astype(BF16)
            cb16 = c_ref[:, ncols].astype(BF16)
            acs_v = acsb_ref[:, cols]
            acs_r_v = acsr_ref[k]
            lastb = acs_v[SSM_CHUNK - 1:SSM_CHUNK, :]
            xd = xv * dtb_ref[:, cols]
            cb = lax.dot_general(cb16, bb, NT_DIMS, preferred_element_type=F32)
            hin = h_ref[g]
            hin_ref[0, k] = hin
            yoff = jnp.dot(cb16, hin.astype(BF16), preferred_element_type=F32)
            ms = [(cb * _head_decay(acs_v, acs_r_v, r, ri, cj)).astype(BF16) for r in range(SSM_HPG)]
            ydiag = jnp.dot(jnp.concatenate(ms, axis=1), _head_masked_rows(xd, BF16), preferred_element_type=F32)
            y_ref[:, cols] = ydiag + jnp.exp(acs_v) * yoff + d_ref[k] * xv
            h_ref[g] = hin * jnp.exp(lastb) + _bdot_tn(bb, xd * jnp.exp(lastb - acs_v))
        z = z_ref[...]
        yg = y_ref[...] * (z * _sigmoid(z))
        r = lax.rsqrt(jnp.mean(yg * yg, axis=-1, keepdims=True) + EPS)
        act_ref[...] = (yg * r * gw_ref[...]).astype(act_ref.dtype)

    lanes = pl.BlockSpec((SSM_CHUNK, SSM_D_INNER), lambda c: (c, 0))
    return pl.pallas_call(
        body, grid=(nc,),
        in_specs=[lanes,
                  pl.BlockSpec((SSM_CHUNK, SSM_BC_DIM), lambda c: (c, B_BLOCK0 // GPS)),
                  pl.BlockSpec((SSM_CHUNK, SSM_BC_DIM), lambda c: (c, C_BLOCK0 // GPS)),
                  lanes, lanes,
                  pl.BlockSpec((SSM_GROUPS, SSM_HPG, SSM_CHUNK), lambda c: (0, 0, c)),
                  pl.BlockSpec((SSM_GROUPS, 1, GP), lambda c: (0, 0, 0)),
                  lanes, pl.BlockSpec((1, SSM_D_INNER), lambda c: (0, 0))] + [ANY] * n_ride,
        out_specs=[lanes, lanes, pl.BlockSpec((1, SSM_GROUPS, SSM_STATE, GP), lambda c: (c, 0, 0, 0))] + [ANY] * n_ride,
        out_shape=[jax.ShapeDtypeStruct((l, SSM_D_INNER), F32), jax.ShapeDtypeStruct((l, SSM_D_INNER), BF16),
                   jax.ShapeDtypeStruct((nc, SSM_GROUPS, SSM_STATE, GP), F32)]
        + [jax.ShapeDtypeStruct((N_CHIPS,) + a.shape, a.dtype) for a in ride],
        scratch_shapes=[pltpu.VMEM((SSM_GROUPS, SSM_STATE, GP), F32)] + (_gather_sems(n_ride) if n_ride else []),
        compiler_params=_params("arbitrary"), name=name,
    )(xbc, xbc, xbc, dtb, acsb, acs_r, d_lanes, proj, gate_w.reshape(1, SSM_D_INNER), *ride)


def _ssd_bwd(xbc, dtb, acsb, dtr, acs_r, a_log, d_lanes, hin, dact, y, proj, gate_w, name):
    l = xbc.shape[0]
    nc = l // SSM_CHUNK

    def body(x_ref, b_ref, c_ref, dtb_ref, acsb_ref, dtr_ref, acsr_ref, alc_ref, d_ref, hin_ref,
             dact_ref, y_ref, z_ref, gw_ref,
             dx_ref, db_ref, dc_ref, ddt_ref, dal_ref, dd_ref, dproj_ref, dgw_ref, dh_ref, dy_ref, acc_ref):
        c = pl.program_id(0)

        @pl.when(c == 0)
        def _():
            dal_ref[...] = jnp.zeros_like(dal_ref)
            dd_ref[...] = jnp.zeros_like(dd_ref)
            acc_ref[...] = jnp.zeros_like(acc_ref)

        z = z_ref[...]
        yv = y_ref[...]
        s = _sigmoid(z)
        sz = z * s
        yg = yv * sz
        r = lax.rsqrt(jnp.mean(yg * yg, axis=-1, keepdims=True) + EPS)
        nrm = yg * r
        gv = dact_ref[...]
        gw = gv * gw_ref[...]
        dyg = r * (gw - nrm * jnp.mean(gw * nrm, axis=-1, keepdims=True))
        dy_ref[...] = dyg * sz
        dproj_ref[...] = (dyg * yv * (s * (1.0 + z * (1.0 - s)))).astype(dproj_ref.dtype)
        acc_ref[...] += jnp.sum((gv * nrm).reshape(SSM_CHUNK // 8, 8, SSM_D_INNER), axis=0)

        @pl.when(c == nc - 1)
        def _():
            dgw_ref[...] = jnp.sum(acc_ref[...], axis=0, keepdims=True)

        for k in range(GPS):
            one_group(c, k, k, x_ref, b_ref, c_ref, dtb_ref, acsb_ref, dtr_ref, acsr_ref, alc_ref, d_ref,
                      hin_ref, dy_ref, dx_ref, db_ref, dc_ref, ddt_ref, dal_ref, dd_ref, dh_ref)

    def one_group(c, g, k, x_ref, b_ref, c_ref, dtb_ref, acsb_ref, dtr_ref, acsr_ref, alc_ref, d_ref, hin_ref, dy_ref,
                  dx_ref, db_ref, dc_ref, ddt_ref, dal_ref, dd_ref, dh_ref):
        cols = slice(k * GP, (k + 1) * GP)
        ncols = slice(k * SSM_STATE, (k + 1) * SSM_STATE)

        @pl.when(c == 0)
        def _():
            dh_ref[g] = jnp.zeros((SSM_STATE, GP), F32)

        xv = x_ref[:, cols].astype(F32)
        dyv = dy_ref[:, cols]
        bb = b_ref[:, ncols].astype(BF16)
        cb16 = c_ref[:, ncols].astype(BF16)
        dtb = dtb_ref[:, cols]
        acsb = acsb_ref[:, cols]
        dtr_v = dtr_ref[k]
        acs_r = acsr_ref[k]
        a_col = -jnp.exp(alc_ref[k])
        ri, cj = _chunk_iotas()
        head_of_lane = lax.shift_right_logical(lax.broadcasted_iota(jnp.int32, (SSM_HPG, GP), 1), HEAD_DIM_LOG2)
        ind_t = (head_of_lane == lax.broadcasted_iota(jnp.int32, (SSM_HPG, GP), 0)).astype(BF16)
        lastb = acsb[SSM_CHUNK - 1:SSM_CHUNK, :]
        ecb = jnp.exp(acsb)
        dteb = jnp.exp(lastb - acsb)
        xd = xv * dtb
        xw = xd * dteb
        cb = lax.dot_general(cb16, bb, NT_DIMS, preferred_element_type=F32)
        hin_v = hin_ref[0, k]
        dhn = dh_ref[g]
        h16 = hin_v.astype(BF16)
        dh16 = dhn.astype(BF16)
        ch = jnp.dot(cb16, h16, preferred_element_type=F32)
        bdh = jnp.dot(bb, dh16, preferred_element_type=F32)
        dym = _head_masked_rows(dyv, BF16)
        g_all = lax.dot_general(dym, xd.astype(BF16), NT_DIMS, preferred_element_type=F32)
        gl_sum = jnp.zeros((SSM_CHUNK, SSM_CHUNK), F32)
        ms, qs = [], []
        for r in range(SSM_HPG):
            decay = _head_decay(acsb, acs_r, r, ri, cj)
            gl = g_all[r * SSM_CHUNK:(r + 1) * SSM_CHUNK] * decay
            gl_sum = gl_sum + gl
            ms.append((cb * decay).astype(BF16))
            qs.append((gl * cb).astype(BF16))
        dxd = lax.dot_general(jnp.concatenate(ms, axis=0), dym, TN_DIMS, preferred_element_type=F32) + dteb * bdh
        cum = jnp.dot(jnp.concatenate(qs, axis=0), (ri < cj).astype(BF16), preferred_element_type=F32)
        sub4 = lax.broadcasted_iota(jnp.int32, (SSM_HPG, 1), 0)
        da = jnp.zeros((SSM_HPG, SSM_CHUNK), F32)
        for r in range(SSM_HPG):
            rect = jnp.sum(jnp.where(ri >= cj, cum[r * SSM_CHUNK:(r + 1) * SSM_CHUNK], 0.0), axis=0, keepdims=True)
            da = da + jnp.where(sub4 == r, rect, 0.0)
        z2 = xw * bdh
        sub8 = lax.broadcasted_iota(jnp.int32, (8, 1), 0)
        col_sums = (jnp.where(sub8 == 0, jnp.sum(z2, axis=0, keepdims=True), 0.0)
                    + jnp.where(sub8 == 1, jnp.sum(dhn * hin_v, axis=0, keepdims=True), 0.0)
                    + jnp.where(sub8 == 2, jnp.sum(dyv * xv, axis=0, keepdims=True), 0.0))
        summands = jnp.concatenate([dyv * ecb * ch - z2, dxd * xv, col_sums], axis=0)
        sums = sum(lax.dot_general(ind_t, piece, NT_DIMS, preferred_element_type=F32) for piece in _split3(summands)[:2])
        per_pos = sums[:, :2 * SSM_CHUNK]
        totals = sums[:, 2 * SSM_CHUNK:]
        e_last = totals[:, 0:1] + jnp.exp(acs_r[:, SSM_CHUNK - 1:SSM_CHUNK]) * totals[:, 1:2]
        da = (da + e_last + jnp.dot(per_pos[:, :SSM_CHUNK], (ri >= cj).astype(F32), preferred_element_type=F32,
                                    precision=lax.Precision.HIGHEST))
        ddt_ref[k] = a_col * da + per_pos[:, SSM_CHUNK:]
        dal_ref[g] += a_col * jnp.sum(da * dtr_v, axis=1, keepdims=True)
        dd_ref[g] += totals[:, 2:3]
        dx_ref[:, cols] = (dxd * dtb + d_ref[k] * dyv).astype(dx_ref.dtype)
        w16 = (ecb * dyv).astype(BF16)
        xw16 = xw.astype(BF16)
        gl16 = gl_sum.astype(BF16)
        dc_ref[:, ncols] = (jnp.dot(gl16, bb, preferred_element_type=F32)
                            + lax.dot_general(w16, h16, NT_DIMS, preferred_element_type=F32)).astype(dc_ref.dtype)
        db_ref[:, ncols] = (lax.dot_general(gl16, cb16, TN_DIMS, preferred_element_type=F32)
                            + lax.dot_general(xw16, dh16, NT_DIMS, preferred_element_type=F32)).astype(db_ref.dtype)
        dh_ref[g] = dhn * jnp.exp(lastb) + lax.dot_general(cb16, w16, TN_DIMS, preferred_element_type=F32)

    def rev(c):
        return nc - 1 - c

    small = pl.BlockSpec((SSM_GROUPS, SSM_HPG, 1), lambda c: (0, 0, 0))
    lanes = pl.BlockSpec((SSM_CHUNK, SSM_D_INNER), lambda c: (rev(c), 0))
    rows = pl.BlockSpec((SSM_GROUPS, SSM_HPG, SSM_CHUNK), lambda c: (0, 0, rev(c)))
    vec = pl.BlockSpec((1, SSM_D_INNER), lambda c: (0, 0))
    return pl.pallas_call(
        body, grid=(nc,),
        in_specs=[lanes,
                  pl.BlockSpec((SSM_CHUNK, SSM_BC_DIM), lambda c: (rev(c), B_BLOCK0 // GPS)),
                  pl.BlockSpec((SSM_CHUNK, SSM_BC_DIM), lambda c: (rev(c), C_BLOCK0 // GPS)),
                  lanes, lanes, rows, rows,
                  pl.BlockSpec((SSM_GROUPS, SSM_HPG, 1), lambda c: (0, 0, 0)),
                  pl.BlockSpec((SSM_GROUPS, 1, GP), lambda c: (0, 0, 0)),
                  pl.BlockSpec((1, SSM_GROUPS, SSM_STATE, GP), lambda c: (rev(c), 0, 0, 0)),
                  lanes, lanes, lanes, vec],
        out_specs=[lanes,
                   pl.BlockSpec((SSM_CHUNK, SSM_BC_DIM), lambda c: (rev(c), 0)),
                   pl.BlockSpec((SSM_CHUNK, SSM_BC_DIM), lambda c: (rev(c), 0)),
                   rows, small, small, lanes, vec],
        out_shape=[jax.ShapeDtypeStruct((l, SSM_D_INNER), BF16), jax.ShapeDtypeStruct((l, SSM_BC_DIM), BF16),
                   jax.ShapeDtypeStruct((l, SSM_BC_DIM), BF16), jax.ShapeDtypeStruct((SSM_GROUPS, SSM_HPG, l), F32),
                   jax.ShapeDtypeStruct((SSM_GROUPS, SSM_HPG, 1), F32),
                   jax.ShapeDtypeStruct((SSM_GROUPS, SSM_HPG, 1), F32),
                   jax.ShapeDtypeStruct((l, SSM_IN_PAD), BF16), jax.ShapeDtypeStruct((1, SSM_D_INNER), F32)],
        scratch_shapes=[pltpu.VMEM((SSM_GROUPS, SSM_STATE, GP), F32), pltpu.VMEM((SSM_CHUNK, SSM_D_INNER), F32),
                        pltpu.VMEM((8, SSM_D_INNER), F32)],
        compiler_params=_params("arbitrary"), name=name,
    )(xbc, xbc, xbc, dtb, acsb, dtr, acs_r, a_log.reshape(SSM_GROUPS, SSM_HPG, 1), d_lanes, hin, dact, y, proj,
      gate_w.reshape(1, SSM_D_INNER))


LANES = 128
ROPE_Q_CHUNKS = ATT_WIDTH // LANES
ROPE_K_CHUNKS = ATT_KV_WIDTH // LANES


def _rope_tables(positions):
    inv = ROPE_THETA ** (-jnp.arange(0, ROPE_DIM, 2, dtype=F32) / ROPE_DIM)
    ang = positions.astype(F32)[:, None] * inv
    cos, sin = jnp.cos(ang), jnp.sin(ang)
    l = positions.shape[0]
    rest = ATT_HEAD_DIM - ROPE_DIM
    ones, zeros = jnp.ones((l, rest), F32), jnp.zeros((l, rest), F32)
    z8 = jnp.zeros((l, ROPE_HALF), F32)
    cos_f = jnp.concatenate([cos, cos, ones], axis=1)
    sin_a = jnp.concatenate([-sin, z8, zeros], axis=1)
    sin_b = jnp.concatenate([z8, sin, zeros], axis=1)
    reps = LANES // ATT_HEAD_DIM
    return tuple(jnp.tile(t, (1, reps)) for t in (cos_f, sin_a, sin_b))


ATT_QKV4 = 3 * ATT_WIDTH


def _both_halves(chunk):
    lane = lax.broadcasted_iota(jnp.int32, (1, LANES), 1)
    swapped = pltpu.roll(chunk, ATT_HEAD_DIM, 1)
    return jnp.where(lane < ATT_HEAD_DIM, chunk, swapped), jnp.where(lane < ATT_HEAD_DIM, swapped, chunk)


def _rope_fwd(proj, tables, name):
    l = proj.shape[0]
    tl = _pick(l, (256, 128))

    def body(p_ref, c_ref, sa_ref, sb_ref, o_ref):
        cos_f, sin_a, sin_b = c_ref[...], sa_ref[...], sb_ref[...]

        def rope(t):
            return t * cos_f + pltpu.roll(t, LANES - ROPE_HALF, 1) * sin_a + pltpu.roll(t, ROPE_HALF, 1) * sin_b

        for k in range(ROPE_Q_CHUNKS):
            sl = slice(k * LANES, (k + 1) * LANES)
            o_ref[:, sl] = (rope(p_ref[:, sl]) * Q_SCALE).astype(o_ref.dtype)
        for part in range(2):
            for k in range(ROPE_K_CHUNKS):
                src = ATT_WIDTH + part * ATT_KV_WIDTH + k * LANES
                t = p_ref[:, src:src + LANES]
                if part == 0:
                    t = rope(t)
                for head, dup in enumerate(_both_halves(t.astype(o_ref.dtype))):
                    dst = (1 + part) * ATT_WIDTH + (2 * k + head) * ATT_GQA * ATT_HEAD_DIM
                    o_ref[:, dst:dst + LANES] = dup
                    o_ref[:, dst + LANES:dst + 2 * LANES] = dup

    tab = pl.BlockSpec((tl, LANES), lambda i: (i, 0))
    return pl.pallas_call(
        body, grid=(l // tl,), in_specs=[pl.BlockSpec((tl, ATT_IN_DIM), lambda i: (i, 0)), tab, tab, tab],
        out_specs=pl.BlockSpec((tl, ATT_QKV4), lambda i: (i, 0)),
        out_shape=jax.ShapeDtypeStruct((l, ATT_QKV4), BF16), compiler_params=_params("parallel"), name=name,
    )(proj, *tables)


def _rope_bwd(dq, dk4, dv4, dgate, tables, name):
    l = dq.shape[0]
    tl = _pick(l, (256, 128))

    def body(dq_ref, dk_ref, dv_ref, dg_ref, c_ref, sa_ref, sb_ref, o_ref):
        cos_f, sin_a, sin_b = c_ref[...], sa_ref[...], sb_ref[...]
        lane = lax.broadcasted_iota(jnp.int32, (1, LANES), 1)

        def unrope(t):
            return t * cos_f + pltpu.roll(t * sin_a, ROPE_HALF, 1) + pltpu.roll(t * sin_b, LANES - ROPE_HALF, 1)

        def head_total(ref, kvh):
            base = kvh * ATT_GQA * ATT_HEAD_DIM
            s = ref[:, base:base + LANES] + ref[:, base + LANES:base + 2 * LANES]
            return s + pltpu.roll(s, ATT_HEAD_DIM, 1)

        for k in range(ROPE_Q_CHUNKS):
            sl = slice(k * LANES, (k + 1) * LANES)
            o_ref[:, sl] = unrope(dq_ref[:, sl] * Q_SCALE).astype(o_ref.dtype)
        for k in range(ROPE_K_CHUNKS):
            dk = jnp.where(lane < ATT_HEAD_DIM, head_total(dk_ref, 2 * k), head_total(dk_ref, 2 * k + 1))
            dv = jnp.where(lane < ATT_HEAD_DIM, head_total(dv_ref, 2 * k), head_total(dv_ref, 2 * k + 1))
            o_ref[:, ATT_WIDTH + k * LANES:ATT_WIDTH + (k + 1) * LANES] = unrope(dk).astype(o_ref.dtype)
            at = ATT_WIDTH + ATT_KV_WIDTH + k * LANES
            o_ref[:, at:at + LANES] = dv.astype(o_ref.dtype)
        o_ref[:, ATT_QKV:ATT_IN_DIM] = dg_ref[...].astype(o_ref.dtype)

    tab = pl.BlockSpec((tl, LANES), lambda i: (i, 0))
    wide = pl.BlockSpec((tl, ATT_WIDTH), lambda i: (i, 0))
    return pl.pallas_call(
        body, grid=(l // tl,), in_specs=[wide, wide, wide, wide, tab, tab, tab],
        out_specs=pl.BlockSpec((tl, ATT_IN_DIM), lambda i: (i, 0)),
        out_shape=jax.ShapeDtypeStruct((l, ATT_IN_DIM), BF16), compiler_params=_params("parallel"), name=name,
    )(dq, dk4, dv4, dgate, *tables)


GATE_HALF = ATT_WIDTH // 2
GATE_COL_BLOCK = ATT_QKV // GATE_HALF


ATT_STACK = ATT_GQA * ATT_BLOCK
BLOCK_LOG2 = ATT_BLOCK.bit_length() - 1


def _stack_masks(n):
    ri = lax.broadcasted_iota(jnp.int32, (ATT_STACK, ATT_BLOCK), 0) & (ATT_BLOCK - 1)
    cj = lax.broadcasted_iota(jnp.int32, (ATT_STACK, ATT_BLOCK), 1)
    return (cj > ri) & (n > 0), cj <= ri


def _stack_sinks(sink_ref, kvh):
    blk = lax.shift_right_logical(lax.broadcasted_iota(jnp.int32, (ATT_STACK, 1), 0), BLOCK_LOG2)
    col = jnp.zeros((ATT_STACK, 1), F32)
    for r in range(ATT_GQA):
        col = jnp.where(blk == r, sink_ref[kvh * ATT_GQA + r], col)
    return col


def _stack_fold(stack):
    head_of_lane = lax.shift_right_logical(lax.broadcasted_iota(jnp.int32, (1, GP), 1), HEAD_DIM_LOG2)
    out = jnp.zeros((ATT_BLOCK, GP), F32)
    for r in range(ATT_GQA):
        out = jnp.where(head_of_lane == r, stack[r * ATT_BLOCK:(r + 1) * ATT_BLOCK], out)
    return out


def _attn_fwd(qkv, proj, sinks, name):
    l = qkv.shape[0]
    nb = l // ATT_BLOCK

    def body(sink_ref, q_ref, kp_ref, kc_ref, vp_ref, vc_ref, g0_ref, g1_ref, og_ref, o_ref, lse_ref):
        n = pl.program_id(0)
        mask_p, mask_c = _stack_masks(n)
        ones = jnp.ones((ATT_BLOCK, LANES), BF16)
        for kvh in range(ATT_KV_HEADS):
            cols = slice(kvh * GP, (kvh + 1) * GP)
            q_stack = _head_masked_rows(q_ref[:, cols], BF16)
            sp = jnp.where(mask_p, lax.dot_general(q_stack, kp_ref[:, cols], NT_DIMS, preferred_element_type=F32), NEG_INF)
            sc = jnp.where(mask_c, lax.dot_general(q_stack, kc_ref[:, cols], NT_DIMS, preferred_element_type=F32), NEG_INF)
            sink = _stack_sinks(sink_ref, kvh)
            m = jnp.maximum(jnp.max(jnp.maximum(sp, sc), axis=1, keepdims=True), sink)
            pp = jnp.exp(sp - m).astype(BF16)
            pc = jnp.exp(sc - m).astype(BF16)
            acc = (jnp.dot(pp, jnp.concatenate([vp_ref[:, cols], ones], axis=1), preferred_element_type=F32)
                   + jnp.dot(pc, jnp.concatenate([vc_ref[:, cols], ones], axis=1), preferred_element_type=F32))
            den = acc[:, GP:] + jnp.exp(sink - m)
            inv = 1.0 / den
            o_ref[:, cols] = _stack_fold(acc[:, :GP] * jnp.concatenate([inv, inv], axis=1))
            lse = m + jnp.log(den)
            lse_ref[:, cols] = _stack_fold(jnp.concatenate([lse, lse], axis=1))
        for half, g_ref in enumerate((g0_ref, g1_ref)):
            sl = slice(half * GATE_HALF, (half + 1) * GATE_HALF)
            gate = g_ref[...]
            og_ref[:, sl] = (o_ref[:, sl] * (gate * _sigmoid(gate))).astype(og_ref.dtype)

    def prev(n):
        return jnp.maximum(n - 1, 0)

    wide = pl.BlockSpec((ATT_BLOCK, ATT_WIDTH), lambda n: (n, 0))
    return pl.pallas_call(
        body, grid=(nb,),
        in_specs=[pl.BlockSpec(memory_space=pltpu.SMEM), wide,
                  pl.BlockSpec((ATT_BLOCK, ATT_WIDTH), lambda n: (prev(n), 1)),
                  pl.BlockSpec((ATT_BLOCK, ATT_WIDTH), lambda n: (n, 1)),
                  pl.BlockSpec((ATT_BLOCK, ATT_WIDTH), lambda n: (prev(n), 2)),
                  pl.BlockSpec((ATT_BLOCK, ATT_WIDTH), lambda n: (n, 2)),
                  pl.BlockSpec((ATT_BLOCK, GATE_HALF), lambda n: (n, GATE_COL_BLOCK)),
                  pl.BlockSpec((ATT_BLOCK, GATE_HALF), lambda n: (n, GATE_COL_BLOCK + 1))],
        out_specs=[wide, wide, wide],
        out_shape=[jax.ShapeDtypeStruct((l, ATT_WIDTH), BF16), jax.ShapeDtypeStruct((l, ATT_WIDTH), F32),
                   jax.ShapeDtypeStruct((l, ATT_WIDTH), F32)],
        compiler_params=_params("parallel"), name=name,
    )(sinks, qkv, qkv, qkv, qkv, qkv, proj, proj)


def _attn_bwd(qkv, proj, sinks, o, lse, dog, name):
    l = qkv.shape[0]
    nb = l // ATT_BLOCK

    def body(sink_ref, q_ref, kp_ref, kc_ref, vp_ref, vc_ref, g0_ref, g1_ref, o_ref, lse_ref, dog_ref,
             dq_ref, dk_ref, dv_ref, dg_ref, ds_ref, ck_ref, cv_ref, do_ref):
        n = pl.program_id(0)

        @pl.when(n == 0)
        def _():
            ds_ref[...] = jnp.zeros_like(ds_ref)
            ck_ref[...] = jnp.zeros_like(ck_ref)
            cv_ref[...] = jnp.zeros_like(cv_ref)

        @pl.when(n == nb)
        def _():
            dk_ref[...] = ck_ref[...]
            dv_ref[...] = cv_ref[...]

        @pl.when(n < nb)
        def _():
            mask_p, mask_c = _stack_masks(n)
            lane = lax.broadcasted_iota(jnp.int32, (1, ATT_Q_HEADS), 1)
            for half, g_ref in enumerate((g0_ref, g1_ref)):
                sl = slice(half * GATE_HALF, (half + 1) * GATE_HALF)
                gate = g_ref[...]
                s = _sigmoid(gate)
                dogv = dog_ref[:, sl]
                do_ref[:, sl] = dogv * (gate * s)
                dg_ref[:, sl] = dogv * o_ref[:, sl] * (s * (1.0 + gate * (1.0 - s)))
            ds_acc = jnp.zeros((1, ATT_Q_HEADS), F32)
            for kvh in range(ATT_KV_HEADS):
                cols = slice(kvh * GP, (kvh + 1) * GP)
                kp, kc, vp, vc = kp_ref[:, cols], kc_ref[:, cols], vp_ref[:, cols], vc_ref[:, cols]
                q_stack = _head_masked_rows(q_ref[:, cols], BF16)
                do_g = do_ref[:, cols]
                do_stack = _head_masked_rows(do_g, BF16)
                lse_g = lse_ref[:, cols]
                lse_stack = jnp.concatenate(
                    [_both_halves(lse_g[:, (r // 2) * LANES:(r // 2 + 1) * LANES])[r % 2] for r in range(ATT_GQA)], axis=0)
                pp = jnp.exp(jnp.where(
                    mask_p, lax.dot_general(q_stack, kp, NT_DIMS, preferred_element_type=F32) - lse_stack, NEG_INF))
                pc = jnp.exp(jnp.where(
                    mask_c, lax.dot_general(q_stack, kc, NT_DIMS, preferred_element_type=F32) - lse_stack, NEG_INF))
                dpp = lax.dot_general(do_stack, vp, NT_DIMS, preferred_element_type=F32)
                dpc = lax.dot_general(do_stack, vc, NT_DIMS, preferred_element_type=F32)
                delta = jnp.sum(pp * dpp + pc * dpc, axis=1, keepdims=True)
                dsp = (pp * (dpp - delta)).astype(BF16)
                dsc = (pc * (dpc - delta)).astype(BF16)
                dq_ref[:, cols] = _stack_fold(jnp.dot(dsp, kp, preferred_element_type=F32)
                                              + jnp.dot(dsc, kc, preferred_element_type=F32))
                dk_ref[:, cols] = ck_ref[:, cols] + lax.dot_general(dsp, q_stack, TN_DIMS, preferred_element_type=F32)
                dv_ref[:, cols] = cv_ref[:, cols] + lax.dot_general(pp.astype(BF16), do_stack, TN_DIMS,
                                                                    preferred_element_type=F32)
                ck_ref[:, cols] = lax.dot_general(dsc, q_stack, TN_DIMS, preferred_element_type=F32)
                cv_ref[:, cols] = lax.dot_general(pc.astype(BF16), do_stack, TN_DIMS, preferred_element_type=F32)
                t = jnp.exp(_stack_sinks(sink_ref, kvh) - lse_stack) * delta
                for r in range(ATT_GQA):
                    tot = jnp.sum(t[r * ATT_BLOCK:(r + 1) * ATT_BLOCK], axis=0, keepdims=True)
                    ds_acc = ds_acc - jnp.where(lane == kvh * ATT_GQA + r, tot[:, :ATT_Q_HEADS], 0.0)
            ds_ref[...] += ds_acc

    def cur(n):
        return jnp.minimum(n, nb - 1)

    def prev(n):
        return jnp.maximum(n - 1, 0)

    wide = pl.BlockSpec((ATT_BLOCK, ATT_WIDTH), lambda n: (cur(n), 0))
    late = pl.BlockSpec((ATT_BLOCK, ATT_WIDTH), lambda n: (prev(n), 0))
    return pl.pallas_call(
        body, grid=(nb + 1,),
        in_specs=[pl.BlockSpec(memory_space=pltpu.SMEM), wide,
                  pl.BlockSpec((ATT_BLOCK, ATT_WIDTH), lambda n: (prev(cur(n)), 1)),
                  pl.BlockSpec((ATT_BLOCK, ATT_WIDTH), lambda n: (cur(n), 1)),
                  pl.BlockSpec((ATT_BLOCK, ATT_WIDTH), lambda n: (prev(cur(n)), 2)),
                  pl.BlockSpec((ATT_BLOCK, ATT_WIDTH), lambda n: (cur(n), 2)),
                  pl.BlockSpec((ATT_BLOCK, GATE_HALF), lambda n: (cur(n), GATE_COL_BLOCK)),
                  pl.BlockSpec((ATT_BLOCK, GATE_HALF), lambda n: (cur(n), GATE_COL_BLOCK + 1)),
                  wide, wide, wide],
        out_specs=[wide, late, late, wide, pl.BlockSpec((1, ATT_Q_HEADS), lambda n: (0, 0))],
        out_shape=[jax.ShapeDtypeStruct((l, ATT_WIDTH), F32), jax.ShapeDtypeStruct((l, ATT_WIDTH), F32),
                   jax.ShapeDtypeStruct((l, ATT_WIDTH), F32), jax.ShapeDtypeStruct((l, ATT_WIDTH), F32),
                   jax.ShapeDtypeStruct((1, ATT_Q_HEADS), F32)],
        scratch_shapes=[pltpu.VMEM((ATT_BLOCK, ATT_WIDTH), F32), pltpu.VMEM((ATT_BLOCK, ATT_WIDTH), F32),
                        pltpu.VMEM((ATT_BLOCK, ATT_WIDTH), F32)],
        compiler_params=_params("arbitrary"), name=name,
    )(sinks, qkv, qkv, qkv, qkv, qkv, proj, proj, o, lse, dog)


def _local_step(x, positions, pre_norm, post_norm, conv_b, dt_bias, a_log, d_skip, gate_norm, sinks, target,
                first_pair, scan_with_second_pair):
    tables = _rope_tables(positions)
    dt_bias_pad = jnp.pad(dt_bias, ((0, 0), (0, SSM_DT_PAD - SSM_HEADS)))
    d_lanes = jnp.repeat(d_skip, SSM_HEAD_DIM, axis=1).reshape(-1, SSM_GROUPS, 1, GP)
    alog_lanes = jnp.repeat(a_log, SSM_HEAD_DIM, axis=1)
    pairs = [first_pair, None]
    saved = []
    cur = x
    h = _rmsnorm_fwd(cur, pre_norm[0], "prenorm_fwd_0")
    for i in range(DEPTH):
        j = i // 2
        if i % 2 == 0:
            proj = _matmul(h, pairs[j]["ssm_w_in"], "nn", F32, f"ssm_in_{i}")
            pre, xbc = _conv_fwd(proj, pairs[j]["ssm_conv_w"], conv_b[j], f"conv_fwd_{i}")
            dtb, acsb, dtr, acs_r = _ssd_prep(proj, dt_bias_pad[j:j + 1], alog_lanes[j:j + 1], f"ssd_prep_{i}")
            scan = functools.partial(_ssd_fwd, xbc, dtb, acsb, acs_r, d_lanes[j], proj, gate_norm[j], f"ssd_fwd_{i}")
            if i == 0:
                y, act, hin, pairs[1] = scan_with_second_pair(scan)
            else:
                y, act, hin = scan()
            w_ssm_in = [p["ssm_w_in"] for p in pairs]
            w_ssm_out = [p["ssm_w_out"] for p in pairs]
            w_att_in = [p["att_w_in"] for p in pairs]
            w_att_out = [p["att_w_out"] for p in pairs]
            conv_w = [p["ssm_conv_w"] for p in pairs]
            ymix = _matmul(act, w_ssm_out[j], "nn", F32, f"ssm_out_{i}")
            saved.append(dict(x=cur, h=h, proj=proj, pre=pre, xbc=xbc, dtb=dtb, acsb=acsb, dtr=dtr, acs_r=acs_r, y=y,
                              hin=hin, act=act, ymix=ymix))
        else:
            proj = _matmul(h, w_att_in[j], "nn", F32, f"att_in_{i}")
            qkv = _rope_fwd(proj, tables, f"rope_fwd_{i}")
            act, o, lse = _attn_fwd(qkv, proj, sinks[j], f"attn_fwd_{i}")
            ymix = _matmul(act, w_att_out[j], "nn", F32, f"att_out_{i}")
            saved.append(dict(x=cur, h=h, proj=proj, qkv=qkv, o=o, lse=lse, act=act, ymix=ymix))
        if i + 1 < DEPTH:
            cur, h = _post_fwd(cur, ymix, post_norm[i], pre_norm[i + 1], f"post_fwd_{i}")

    gr = {k: [None] * 2 for k in ("ssm_w_in", "ssm_conv_w", "ssm_conv_b", "ssm_dt_bias", "ssm_a_log", "ssm_d",
                                  "ssm_gate_norm", "ssm_w_out", "att_w_in", "att_sinks", "att_w_out")}
    gr["pre_norm"] = [None] * DEPTH
    gr["post_norm"] = [None] * DEPTH
    last = DEPTH - 1
    g, dymix, loss_lanes, gr["post_norm"][last] = _post_loss(cur, ymix, post_norm[last], target, "post_loss")
    for i in reversed(range(DEPTH)):
        j = i // 2
        s = saved[i]
        if i % 2 == 0:
            dact = _matmul(dymix, w_ssm_out[j], "nt", F32, f"ssm_out_dx_{i}")
            gr["ssm_w_out"][j] = _matmul(s["act"], dymix, "tn", F32, f"ssm_out_dw_{i}")
            dxs, db, dc, ddt8, dal, dd, dproj, gr["ssm_gate_norm"][j] = _ssd_bwd(
                s["xbc"], s["dtb"], s["acsb"], s["dtr"], s["acs_r"], a_log[j], d_lanes[j], s["hin"], dact, s["y"],
                s["proj"], gate_norm[j], f"ssd_bwd_{i}")
            gr["ssm_a_log"][j] = dal.reshape(SSM_HEADS)
            gr["ssm_d"][j] = dd.reshape(SSM_HEADS)
            l = x.shape[0]
            ddt = jnp.pad(jnp.transpose(ddt8, (2, 0, 1)).reshape(l, SSM_HEADS), ((0, 0), (0, SSM_DT_PAD - SSM_HEADS)))
            dproj, dbias = _dt_bwd(ddt, s["proj"], dt_bias_pad[j:j + 1], dproj, f"dt_bwd_{i}")
            gr["ssm_dt_bias"][j] = dbias[0, :SSM_HEADS]
            dcw, dcb = [], []
            for c0, dpiece, tag in ((0, dxs, "x"), (SSM_D_INNER, db, "b"), (SSM_D_INNER + SSM_BC_DIM, dc, "c")):
                dproj, dw_, db_ = _conv_bwd(dpiece, s["pre"], s["proj"], conv_w[j], c0, dproj, f"conv_bwd_{tag}_{i}")
                dcw.append(dw_)
                dcb.append(db_)
            gr["ssm_conv_w"][j] = jnp.concatenate(dcw, axis=1)
            gr["ssm_conv_b"][j] = jnp.concatenate(dcb, axis=1)[0]
            w_in, key = w_ssm_in[j], "ssm_w_in"
        else:
            dog = _matmul(dymix, w_att_out[j], "nt", F32, f"att_out_dx_{i}")
            gr["att_w_out"][j] = _matmul(s["act"], dymix, "tn", F32, f"att_out_dw_{i}")
            dq, dk, dv, dgate, dsk = _attn_bwd(s["qkv"], s["proj"], sinks[j], s["o"], s["lse"], dog, f"attn_bwd_{i}")
            gr["att_sinks"][j] = dsk[0]
            dproj = _rope_bwd(dq, dk, dv, dgate, tables, f"rope_bwd_{i}")
            w_in, key = w_att_in[j], "att_w_in"
        dh = _matmul(dproj, w_in, "nt", F32, f"in_dx_{i}")
        gr[key][j] = _matmul(s["h"], dproj, "tn", F32, f"in_dw_{i}")
        if i > 0:
            g, dymix, gr["pre_norm"][i], gr["post_norm"][i - 1] = _norm_bwd_chain(
                dh, s["x"], pre_norm[i], g, saved[i - 1]["ymix"], post_norm[i - 1], f"norm_bwd_{i}")
        else:
            g, gr["pre_norm"][i] = _rmsnorm_bwd(dh, s["x"], pre_norm[i], g, F32, f"prenorm_bwd_{i}")
    grads = {k: jnp.stack([v.reshape(v.shape[-1]) if k in ("pre_norm", "post_norm", "ssm_gate_norm") else v for v in vs])
             for k, vs in gr.items()}
    return loss_lanes, g, grads


N_CHIPS = 4
N_DEV = 8
MESH = pl.DeviceIdType.MESH
ANY = pl.BlockSpec(memory_space=pl.ANY)


def _place():
    x, y, c = lax.axis_index("x"), lax.axis_index("y"), lax.axis_index("c")
    return x, y, c, 2 * x + y


def _gather_sems(n):
    return [pltpu.SemaphoreType.DMA((n, N_CHIPS)), pltpu.SemaphoreType.DMA((n, N_CHIPS)), pltpu.SemaphoreType.DMA((n,))]


def _gather_between_chips(ins, outs, send_sems, recv_sems, local_sems, wait):
    n = len(ins)
    _, _, c, s = _place()
    local = [pltpu.make_async_copy(ins[w], outs[w].at[s], local_sems.at[w]) for w in range(n)]

    def remote(w, t):
        return pltpu.make_async_remote_copy(
            src_ref=ins[w].at[c], dst_ref=outs[w].at[s, c], send_sem=send_sems.at[w, t],
            recv_sem=recv_sems.at[w, s], device_id=(t // 2, t % 2, c), device_id_type=MESH)

    def arrival(w, t):
        return pltpu.make_async_remote_copy(
            src_ref=ins[w].at[c], dst_ref=outs[w].at[t, c], send_sem=send_sems.at[w, t],
            recv_sem=recv_sems.at[w, t], device_id=(t // 2, t % 2, c), device_id_type=MESH)

    if not wait:
        for cp in local:
            cp.start()
    for t in range(N_CHIPS):
        @pl.when(s != t)
        def _():
            for w in range(n):
                if wait:
                    remote(w, t).wait_send()
                    arrival(w, t).wait_recv()
                else:
                    remote(w, t).start()
    if wait:
        for cp in local:
            cp.wait()


def _pair_handoff(bufs, name):
    n = len(bufs)

    def body(*refs):
        outs = refs[n:2 * n]
        send_sems, recv_sems = refs[2 * n:]
        x, y, c, s = _place()

        def handed_on(w, t):
            return pltpu.make_async_remote_copy(
                src_ref=outs[w].at[t, c], dst_ref=outs[w].at[t, c], send_sem=send_sems.at[w, t],
                recv_sem=recv_sems.at[w, t], device_id=(x, y, 1 - c), device_id_type=MESH)

        def handed_in(w, t):
            return pltpu.make_async_remote_copy(
                src_ref=outs[w].at[t, 1 - c], dst_ref=outs[w].at[t, 1 - c], send_sem=send_sems.at[w, t],
                recv_sem=recv_sems.at[w, t], device_id=(x, y, 1 - c), device_id_type=MESH)

        for t in range(N_CHIPS):
            @pl.when(s != t)
            def _():
                for w in range(n):
                    handed_on(w, t).start()
        for t in range(N_CHIPS):
            @pl.when(s != t)
            def _():
                for w in range(n):
                    handed_on(w, t).wait_send()
                    handed_in(w, t).wait_recv()

    return pl.pallas_call(
        body, in_specs=[ANY] * n, out_specs=[ANY] * n,
        out_shape=[jax.ShapeDtypeStruct(a.shape, a.dtype) for a in bufs],
        scratch_shapes=[pltpu.SemaphoreType.DMA((n, N_CHIPS)), pltpu.SemaphoreType.DMA((n, N_CHIPS))],
        input_output_aliases={w: w for w in range(n)}, name=name,
    )(*bufs)


def _chip_gather(shards, name):
    n = len(shards)

    def body(*refs):
        ins, outs = refs[:n], refs[n:2 * n]
        _gather_between_chips(ins, outs, *refs[2 * n:], wait=False)
        _gather_between_chips(ins, outs, *refs[2 * n:], wait=True)

    bufs = pl.pallas_call(
        body, in_specs=[ANY] * n, out_specs=[ANY] * n,
        out_shape=[jax.ShapeDtypeStruct((N_CHIPS,) + a.shape, a.dtype) for a in shards],
        scratch_shapes=_gather_sems(n), name=name,
    )(*shards)
    return _pair_handoff(bufs, name + "_handoff")


def _pair_swap(parts, name):
    n = len(parts)

    def body(*refs):
        ins, outs = refs[:n], refs[n:2 * n]
        send_sems, recv_sems = refs[2 * n:]
        x, y, c, _ = _place()
        cps = [pltpu.make_async_remote_copy(
            src_ref=ins[w].at[1 - c], dst_ref=outs[w], send_sem=send_sems.at[w], recv_sem=recv_sems.at[w],
            device_id=(x, y, 1 - c), device_id_type=MESH) for w in range(n)]
        for cp in cps:
            cp.start()
        for cp in cps:
            cp.wait()

    return pl.pallas_call(
        body, in_specs=[ANY] * n, out_specs=[ANY] * n,
        out_shape=[jax.ShapeDtypeStruct(a.shape[1:], a.dtype) for a in parts],
        scratch_shapes=[pltpu.SemaphoreType.DMA((n,)), pltpu.SemaphoreType.DMA((n,))],
        name=name,
    )(*parts)


def _chip_scatter(parts, name):
    n = len(parts)
    rows = [a.shape[0] // N_CHIPS for a in parts]

    def body(*refs):
        ins, outs = refs[:n], refs[n:2 * n]
        send_sems, recv_sems, local_sems = refs[2 * n:]
        _, _, c, s = _place()

        def block(w, t):
            return ins[w].at[pl.ds(t * rows[w], rows[w])]

        local = [pltpu.make_async_copy(block(w, s), outs[w].at[s], local_sems.at[w]) for w in range(n)]
        for cp in local:
            cp.start()

        def remote(w, t):
            return pltpu.make_async_remote_copy(
                src_ref=block(w, t), dst_ref=outs[w].at[s], send_sem=send_sems.at[w, t], recv_sem=recv_sems.at[w, s],
                device_id=(t // 2, t % 2, c), device_id_type=MESH)

        def arrival(w, t):
            return pltpu.make_async_remote_copy(
                src_ref=block(w, t), dst_ref=outs[w].at[t], send_sem=send_sems.at[w, t], recv_sem=recv_sems.at[w, t],
                device_id=(t // 2, t % 2, c), device_id_type=MESH)

        for t in range(N_CHIPS):
            @pl.when(s != t)
            def _():
                for w in range(n):
                    remote(w, t).start()
        for t in range(N_CHIPS):
            @pl.when(s != t)
            def _():
                for w in range(n):
                    remote(w, t).wait_send()
                    arrival(w, t).wait_recv()
        for cp in local:
            cp.wait()

    return pl.pallas_call(
        body, in_specs=[ANY] * n, out_specs=[ANY] * n,
        out_shape=[jax.ShapeDtypeStruct((N_CHIPS, r, a.shape[1]), a.dtype) for a, r in zip(parts, rows)],
        scratch_shapes=[pltpu.SemaphoreType.DMA((n, N_CHIPS)), pltpu.SemaphoreType.DMA((n, N_CHIPS)),
                        pltpu.SemaphoreType.DMA((n,))],
        name=name,
    )(*parts)


def _pair_merge(parts, name):
    n = len(parts)

    def body(*refs):
        ins, outs = refs[:n], refs[n:2 * n]
        send_sems, recv_sems = refs[2 * n:]
        x, y, c, _ = _place()
        cps = [pltpu.make_async_remote_copy(
            src_ref=ins[w], dst_ref=outs[w], send_sem=send_sems.at[w], recv_sem=recv_sems.at[w],
            device_id=(x, y, 1 - c), device_id_type=MESH) for w in range(n)]
        for cp in cps:
            cp.start()
        for cp in cps:
            cp.wait()

    return pl.pallas_call(
        body, in_specs=[ANY] * n, out_specs=[ANY] * n,
        out_shape=[jax.ShapeDtypeStruct(a.shape, a.dtype) for a in parts],
        scratch_shapes=[pltpu.SemaphoreType.DMA((n,)), pltpu.SemaphoreType.DMA((n,))],
        name=name,
    )(*parts)


def _all_gather_small(a, name):
    def body(in_ref, out_ref, send_sems, recv_sems, local_sem):
        x, y, c, _ = _place()
        me = 4 * x + 2 * y + c
        local = pltpu.make_async_copy(in_ref, out_ref.at[me], local_sem)
        local.start()

        def remote(d):
            return pltpu.make_async_remote_copy(
                src_ref=in_ref, dst_ref=out_ref.at[me], send_sem=send_sems.at[d], recv_sem=recv_sems.at[me],
                device_id=(d // 4, (d // 2) % 2, d % 2), device_id_type=MESH)

        def arrival(d):
            return pltpu.make_async_remote_copy(
                src_ref=in_ref, dst_ref=out_ref.at[d], send_sem=send_sems.at[d], recv_sem=recv_sems.at[d],
                device_id=(d // 4, (d // 2) % 2, d % 2), device_id_type=MESH)

        for d in range(N_DEV):
            @pl.when(me != d)
            def _():
                remote(d).start()
        for d in range(N_DEV):
            @pl.when(me != d)
            def _():
                remote(d).wait_send()
                arrival(d).wait_recv()
        local.wait()

    return pl.pallas_call(
        body, in_specs=[ANY], out_specs=ANY, out_shape=jax.ShapeDtypeStruct((N_DEV,) + a.shape, a.dtype),
        scratch_shapes=[pltpu.SemaphoreType.DMA((N_DEV,)), pltpu.SemaphoreType.DMA((N_DEV,)), pltpu.SemaphoreType.DMA],
        name=name,
    )(a)


def _reduce_tile(rows):
    return _pick(rows, (256, 16))


def _pair_add(full, other, layer, name):
    _, rows, cols = full.shape
    tr = _reduce_tile(rows)

    def body(layer_ref, a_ref, b_ref, o_ref):
        o_ref[...] = (a_ref[0] + b_ref[...]).astype(o_ref.dtype)

    return pl.pallas_call(
        body,
        grid_spec=pltpu.PrefetchScalarGridSpec(
            num_scalar_prefetch=1, grid=(rows // tr,),
            in_specs=[pl.BlockSpec((1, tr, cols), lambda i, lr: (lr[0], i, 0)), pl.BlockSpec((tr, cols), lambda i, lr: (i, 0))],
            out_specs=pl.BlockSpec((tr, cols), lambda i, lr: (i, 0))),
        out_shape=jax.ShapeDtypeStruct((rows, cols), BF16), compiler_params=_params("parallel"), name=name,
    )(layer, full, other)


def _sum_slots(a, name):
    n, rows, cols = a.shape
    tr = _reduce_tile(rows)

    def body(a_ref, o_ref):
        acc = a_ref[0].astype(F32)
        for k in range(1, n):
            acc = acc + a_ref[k].astype(F32)
        o_ref[...] = acc

    return pl.pallas_call(
        body, grid=(rows // tr,), in_specs=[pl.BlockSpec((n, tr, cols), lambda i: (0, i, 0))],
        out_specs=pl.BlockSpec((tr, cols), lambda i: (i, 0)),
        out_shape=jax.ShapeDtypeStruct((rows, cols), F32), compiler_params=_params("parallel"), name=name,
    )(a)


def _adamw(w, g, m, v, name):
    rows, cols = w.shape
    tr = _pick(rows, (256, 8))

    def body(w_ref, g_ref, m_ref, v_ref, d_ref, nm_ref, nv_ref):
        gv = g_ref[...]
        mn = ADAM_B1 * m_ref[...] + (1.0 - ADAM_B1) * gv
        vn = ADAM_B2 * v_ref[...] + (1.0 - ADAM_B2) * jnp.square(gv)
        m_hat = mn / (1.0 - ADAM_B1 ** ADAM_STEP)
        v_hat = vn / (1.0 - ADAM_B2 ** ADAM_STEP)
        d_ref[...] = -ADAM_LR * (m_hat / (jnp.sqrt(v_hat) + ADAM_EPS) + ADAM_WD * w_ref[...])
        nm_ref[...] = mn
        nv_ref[...] = vn

    blk = pl.BlockSpec((tr, cols), lambda i: (i, 0))
    return pl.pallas_call(
        body, grid=(rows // tr,), in_specs=[blk] * 4, out_specs=[blk] * 3,
        out_shape=[jax.ShapeDtypeStruct((rows, cols), F32)] * 3, compiler_params=_params("parallel"), name=name,
    )(w, g, m, v)


BIG = ("ssm_w_in", "ssm_w_out", "att_w_in", "att_w_out")
SHARDED = BIG + ("ssm_conv_w",)
SMALL = ("pre_norm", "post_norm", "ssm_conv_b", "ssm_dt_bias", "ssm_a_log", "ssm_d", "ssm_gate_norm", "att_sinks")
WEIGHTS = ("pre_norm", "post_norm", "ssm_w_in", "ssm_conv_w", "ssm_conv_b", "ssm_dt_bias", "ssm_a_log", "ssm_d",
           "ssm_gate_norm", "ssm_w_out", "att_w_in", "att_sinks", "att_w_out")


def _halves(a):
    return a.reshape(2, a.shape[0] // 2, a.shape[1])


def _layer_shards(j, ssm_w_in, ssm_w_out, att_w_in, att_w_out, ssm_conv_w):
    return [_halves(ssm_w_in[j].astype(BF16)), _halves(ssm_w_out[j].astype(BF16)), _halves(att_w_in[j].astype(BF16)),
            _halves(att_w_out[j].astype(BF16)), _halves(ssm_conv_w[j])]


def _whole_weights(gathered):
    g_in, g_out, g_ain, g_aout, g_cw = [g.reshape((N_CHIPS, 2 * g.shape[2], g.shape[3])) for g in gathered]

    def by_cols(g):
        return jnp.transpose(g, (1, 0, 2)).reshape(g.shape[1], N_CHIPS * g.shape[2])

    def by_rows(g):
        return g.reshape(N_CHIPS * g.shape[1], g.shape[2])

    return dict(ssm_w_in=jnp.pad(by_cols(g_in), ((0, 0), (0, SSM_IN_PAD - SSM_IN_DIM))), ssm_w_out=by_rows(g_out),
                att_w_in=by_cols(g_ain), att_w_out=by_rows(g_aout), ssm_conv_w=by_cols(g_cw))


def _cols_by_chip(g):
    two, rows, cols = g.shape
    return jnp.transpose(g.reshape(two, rows, N_CHIPS, cols // N_CHIPS), (0, 2, 1, 3)).reshape(two, N_CHIPS * rows, cols // N_CHIPS)


def _pack_small(tree, keys):
    flat = jnp.concatenate([tree[k].reshape(-1) for k in keys])
    rows = -(-flat.shape[0] // (8 * LANES)) * 8
    return jnp.pad(flat, (0, rows * LANES - flat.shape[0])).reshape(rows, LANES)


def _unpack_small(packed, shapes, keys):
    flat = packed.reshape(-1)
    out, at = {}, 0
    for k in keys:
        n = 1
        for dim in shapes[k]:
            n *= dim
        out[k] = flat[at:at + n].reshape(shapes[k])
        at += n
    return out


def kernel(x, positions, pre_norm, post_norm, ssm_w_in, ssm_conv_w, ssm_conv_b, ssm_dt_bias, ssm_a_log, ssm_d, ssm_gate_norm, ssm_w_out, att_w_in, att_sinks, att_w_out, loss_target, m_pre_norm, m_post_norm, m_ssm_w_in, m_ssm_conv_w, m_ssm_conv_b, m_ssm_dt_bias, m_ssm_a_log, m_ssm_d, m_ssm_gate_norm, m_ssm_w_out, m_att_w_in, m_att_sinks, m_att_w_out, v_pre_norm, v_post_norm, v_ssm_w_in, v_ssm_conv_w, v_ssm_conv_b, v_ssm_dt_bias, v_ssm_a_log, v_ssm_d, v_ssm_gate_norm, v_ssm_w_out, v_att_w_in, v_att_sinks, v_att_w_out):
    w = dict(pre_norm=pre_norm, post_norm=post_norm, ssm_w_in=ssm_w_in, ssm_conv_w=ssm_conv_w, ssm_conv_b=ssm_conv_b,
             ssm_dt_bias=ssm_dt_bias, ssm_a_log=ssm_a_log, ssm_d=ssm_d, ssm_gate_norm=ssm_gate_norm, ssm_w_out=ssm_w_out,
             att_w_in=att_w_in, att_sinks=att_sinks, att_w_out=att_w_out)
    m = dict(pre_norm=m_pre_norm, post_norm=m_post_norm, ssm_w_in=m_ssm_w_in, ssm_conv_w=m_ssm_conv_w, ssm_conv_b=m_ssm_conv_b,
             ssm_dt_bias=m_ssm_dt_bias, ssm_a_log=m_ssm_a_log, ssm_d=m_ssm_d, ssm_gate_norm=m_ssm_gate_norm,
             ssm_w_out=m_ssm_w_out, att_w_in=m_att_w_in, att_sinks=m_att_sinks, att_w_out=m_att_w_out)
    v = dict(pre_norm=v_pre_norm, post_norm=v_post_norm, ssm_w_in=v_ssm_w_in, ssm_conv_w=v_ssm_conv_w, ssm_conv_b=v_ssm_conv_b,
             ssm_dt_bias=v_ssm_dt_bias, ssm_a_log=v_ssm_a_log, ssm_d=v_ssm_d, ssm_gate_norm=v_ssm_gate_norm,
             ssm_w_out=v_ssm_w_out, att_w_in=v_att_w_in, att_sinks=v_att_sinks, att_w_out=v_att_w_out)
    c = lax.axis_index("c")
    chip = 2 * lax.axis_index("x") + lax.axis_index("y")

    sharded = (ssm_w_in, ssm_w_out, att_w_in, att_w_out, ssm_conv_w)
    first_pair = _whole_weights(_chip_gather(_layer_shards(0, *sharded), "gather_weights_0"))

    def scan_with_second_pair(scan):
        y, act, hin, *arrived = scan(ride=_layer_shards(1, *sharded))
        return y, act, hin, _whole_weights(_pair_handoff(arrived, "gather_weights_1_handoff"))

    loss_lanes, grad_x, gr = _local_step(
        x[0], positions[0], pre_norm, post_norm, ssm_conv_b, ssm_dt_bias, ssm_a_log, ssm_d, ssm_gate_norm, att_sinks,
        loss_target[0], first_pair, scan_with_second_pair)
    loss = lax.psum(0.5 * jnp.sum(loss_lanes) / D_MODEL, ("x", "y", "c"))

    parts = [_cols_by_chip(gr["ssm_w_in"][:, :, :SSM_IN_DIM]), gr["ssm_w_out"], _cols_by_chip(gr["att_w_in"]),
             gr["att_w_out"]]
    from_sibling = _pair_swap(parts, "reduce_pair_swap")
    layer = jnp.reshape(c, (1,)).astype(jnp.int32)
    chip_sums = [_pair_add(p, o, layer, f"reduce_pair_add_{k}") for k, (p, o) in enumerate(zip(parts, from_sibling))]
    by_chip = _chip_scatter(chip_sums, "reduce_chip_scatter")
    mine = [_sum_slots(a, f"reduce_chip_sum_{k}") for k, a in enumerate(by_chip)]
    theirs = _pair_merge(mine, "reduce_pair_merge")
    grads = {k: jnp.stack([jnp.where(c == 0, a, b), jnp.where(c == 0, b, a)]).reshape(w[k].shape)
             for k, a, b in zip(BIG, mine, theirs)}

    small_keys = SMALL + ("ssm_conv_w",)
    small_shapes = {k: w[k].shape for k in SMALL}
    small_shapes["ssm_conv_w"] = gr["ssm_conv_w"].shape
    small_sum = _sum_slots(_all_gather_small(_pack_small(gr, small_keys), "reduce_small_gather"), "reduce_small_sum")
    grads.update(_unpack_small(small_sum, small_shapes, small_keys))
    conv_cols = ssm_conv_w.shape[2]
    grads["ssm_conv_w"] = lax.dynamic_slice_in_dim(grads["ssm_conv_w"], chip * conv_cols, conv_cols, axis=2)

    delta, new_m, new_v = {}, {}, {}
    for k in SHARDED:
        shp = w[k].shape
        two_d = (shp[0] * shp[1], shp[2])
        d_, m_, v_ = _adamw(w[k].reshape(two_d), grads[k].reshape(two_d), m[k].reshape(two_d), v[k].reshape(two_d),
                            f"adamw_{k}")
        delta[k], new_m[k], new_v[k] = d_.reshape(shp), m_.reshape(shp), v_.reshape(shp)
    d_, m_, v_ = _adamw(_pack_small(w, SMALL), _pack_small(grads, SMALL), _pack_small(m, SMALL), _pack_small(v, SMALL),
                        "adamw_small")
    delta.update(_unpack_small(d_, small_shapes, SMALL))
    new_m.update(_unpack_small(m_, small_shapes, SMALL))
    new_v.update(_unpack_small(v_, small_shapes, SMALL))

    return (loss, grad_x[None], *[grads[k] for k in WEIGHTS], *[delta[k] for k in WEIGHTS],
            *[new_m[k] for k in WEIGHTS], *[new_v[k] for k in WEIGHTS])
```

```python
import functools

import jax
import jax.numpy as jnp
from jax import lax
from jax.experimental import pallas as pl
from jax.experimental.pallas import tpu as pltpu

F32 = jnp.float32
BF16 = jnp.bfloat16
EPS = 1e-6
NEG_INF = float("-inf")

D_MODEL = 1024
DEPTH = 4
SSM_D_INNER = 2048
SSM_HEAD_DIM = 64
SSM_HEADS = 32
SSM_GROUPS = 8
SSM_HPG = 4
SSM_STATE = 128
SSM_CONV = 4
SSM_CHUNK = 128
SSM_BC_DIM = 1024
SSM_CONV_DIM = 4096
SSM_IN_DIM = 6176
SSM_IN_PAD = 6272
SSM_DT_PAD = 128
ATT_HEAD_DIM = 64
ATT_Q_HEADS = 16
ATT_KV_HEADS = 4
ATT_GQA = 4
ATT_WIDTH = 1024
ATT_KV_WIDTH = 256
ATT_IN_DIM = 2560
ATT_QKV = ATT_WIDTH + 2 * ATT_KV_WIDTH
ATT_BLOCK = 128
ROPE_THETA = 500000.0
ROPE_DIM = 16
ROPE_HALF = 8
Q_SCALE = ATT_HEAD_DIM ** -0.5

ADAM_LR = 0.001
ADAM_B1 = 0.9
ADAM_B2 = 0.999
ADAM_EPS = 1e-08
ADAM_WD = 0.01
ADAM_STEP = 10

VMEM_LIMIT_BYTES = 48 * 1024 * 1024
NT_DIMS = (((1,), (1,)), ((), ()))
TN_DIMS = (((0,), (0,)), ((), ()))


def _params(*sem):
    return pltpu.CompilerParams(dimension_semantics=sem, vmem_limit_bytes=VMEM_LIMIT_BYTES)


def _pick(n, cands):
    for c in cands:
        if n % c == 0:
            return c
    return n


def _sigmoid(v):
    return 0.5 * jnp.tanh(0.5 * v) + 0.5


def _bdot(a, b):
    return jnp.dot(a.astype(BF16), b.astype(BF16), preferred_element_type=F32)


def _bdot_nt(a, b):
    return lax.dot_general(a.astype(BF16), b.astype(BF16), NT_DIMS, preferred_element_type=F32)


def _bdot_tn(a, b):
    return lax.dot_general(a.astype(BF16), b.astype(BF16), TN_DIMS, preferred_element_type=F32)


MATMUL_VMEM_BUDGET = 36 * 1024 * 1024


def _matmul_tiles(m, n, k, out_bytes, reduce_rows):
    best = None
    whole = [k] if (not reduce_rows or k <= 2048) else []
    for tk in whole + [c for c in (4096, 2048, 1024, 896, 512) if k % c == 0 and c < k]:
        for tm in (c for c in (2048, 1024, 512, 256) if m % c == 0):
            for tn in (c for c in (n, 1280, 1024, 896, 640, 512) if n % c == 0):
                acc = tm * tn * 4 if tk < k else 0
                need = 2 * (2 * tk * (tm + tn) + tm * tn * out_bytes) + acc
                if need <= MATMUL_VMEM_BUDGET and (best is None or tm * tn * min(tk, 2048) > best[0]):
                    best = (tm * tn * min(tk, 2048), tm, tn, tk)
        if best is not None and not reduce_rows:
            break
    return best[1:]


def _matmul(a, b, mode, out_dtype, name, ride=()):
    if mode == "nn":
        (m, k), n = a.shape, b.shape[1]
    elif mode == "nt":
        (m, k), n = a.shape, b.shape[0]
    else:
        (k, m), n = a.shape, b.shape[1]
    tm, tn, tk = _matmul_tiles(m, n, k, jnp.dtype(out_dtype).itemsize, mode == "tn")
    nk = k // tk
    steps = (n // tn, m // tm, nk)
    dims = {"nn": (((1,), (0,)), ((), ())), "nt": NT_DIMS, "tn": TN_DIMS}[mode]
    n_ride = len(ride)

    def body(*refs):
        a_ref, b_ref = refs[:2]
        ride_in = refs[2:2 + n_ride]
        o_ref = refs[2 + n_ride]
        ride_out = refs[3 + n_ride:3 + 2 * n_ride]
        acc_ref = refs[3 + 2 * n_ride]
        ride_sems = refs[4 + 2 * n_ride:]
        kk = pl.program_id(2)
        at = [pl.program_id(d) for d in range(3)]
        if n_ride:
            @pl.when((at[0] == 0) & (at[1] == 0) & (at[2] == 0))
            def _():
                _gather_between_chips(ride_in, ride_out, *ride_sems, wait=False)

        part = lax.dot_general(a_ref[...], b_ref[...], dims, preferred_element_type=F32)
        if nk == 1:
            o_ref[...] = part.astype(o_ref.dtype)
        else:
            @pl.when(kk == 0)
            def _():
                acc_ref[...] = part

            @pl.when(kk > 0)
            def _():
                acc_ref[...] += part

            @pl.when(kk == nk - 1)
            def _():
                o_ref[...] = acc_ref[...].astype(o_ref.dtype)

        if n_ride:
            @pl.when((at[0] == steps[0] - 1) & (at[1] == steps[1] - 1) & (at[2] == steps[2] - 1))
            def _():
                _gather_between_chips(ride_in, ride_out, *ride_sems, wait=True)

    if mode == "nn":
        a_spec = pl.BlockSpec((tm, tk), lambda j, i, kk: (i, kk))
        b_spec = pl.BlockSpec((tk, tn), lambda j, i, kk: (kk, j))
    elif mode == "nt":
        a_spec = pl.BlockSpec((tm, tk), lambda j, i, kk: (i, kk))
        b_spec = pl.BlockSpec((tn, tk), lambda j, i, kk: (j, kk))
    else:
        a_spec = pl.BlockSpec((tk, tm), lambda j, i, kk: (kk, i))
        b_spec = pl.BlockSpec((tk, tn), lambda j, i, kk: (kk, j))
    out = pl.pallas_call(
        body, grid=steps, in_specs=[a_spec, b_spec] + [ANY] * n_ride,
        out_specs=[pl.BlockSpec((tm, tn), lambda j, i, kk: (i, j))] + [ANY] * n_ride,
        out_shape=[jax.ShapeDtypeStruct((m, n), out_dtype)]
        + [jax.ShapeDtypeStruct((N_CHIPS,) + r.shape, r.dtype) for r in ride],
        scratch_shapes=[pltpu.VMEM((tm, tn), F32)] + (_gather_sems(n_ride) if n_ride else []),
        compiler_params=_params(*(["arbitrary"] * 3 if n_ride else ["parallel", "parallel", "arbitrary"])), name=name,
    )(a, b, *ride)
    return out if n_ride else out[0]


def _row_tile(l):
    return _pick(l, (512, 256, 128))


def _rmsnorm_fwd(x, w, name):
    l, d = x.shape
    tl = _row_tile(l)

    def body(x_ref, w_ref, o_ref):
        xv = x_ref[...]
        r = lax.rsqrt(jnp.mean(xv * xv, axis=-1, keepdims=True) + EPS)
        o_ref[...] = (xv * r * w_ref[...]).astype(o_ref.dtype)

    return pl.pallas_call(
        body, grid=(l // tl,),
        in_specs=[pl.BlockSpec((tl, d), lambda i: (i, 0)), pl.BlockSpec((1, d), lambda i: (0, 0))],
        out_specs=pl.BlockSpec((tl, d), lambda i: (i, 0)),
        out_shape=jax.ShapeDtypeStruct((l, d), BF16), compiler_params=_params("parallel"), name=name,
    )(x, w.reshape(1, d))


def _post_fwd(x, y, w, w_next, name):
    l, d = x.shape
    tl = _row_tile(l)

    def body(x_ref, y_ref, w_ref, wn_ref, o_ref, h_ref):
        yv = y_ref[...]
        r = lax.rsqrt(jnp.mean(yv * yv, axis=-1, keepdims=True) + EPS)
        out = x_ref[...] + yv * r * w_ref[...]
        o_ref[...] = out
        rn = lax.rsqrt(jnp.mean(out * out, axis=-1, keepdims=True) + EPS)
        h_ref[...] = (out * rn * wn_ref[...]).astype(h_ref.dtype)

    row = pl.BlockSpec((tl, d), lambda i: (i, 0))
    vec = pl.BlockSpec((1, d), lambda i: (0, 0))
    return pl.pallas_call(
        body, grid=(l // tl,), in_specs=[row, row, vec, vec], out_specs=[row, row],
        out_shape=[jax.ShapeDtypeStruct((l, d), F32), jax.ShapeDtypeStruct((l, d), BF16)],
        compiler_params=_params("parallel"), name=name,
    )(x, y, w.reshape(1, d), w_next.reshape(1, d))


def _post_loss(x, y, w, t, name):
    l, d = x.shape
    tl = _row_tile(l)
    nt = l // tl

    def body(x_ref, y_ref, w_ref, t_ref, g_ref, dy_ref, ls_ref, dw_ref, acc_ref):
        i = pl.program_id(0)

        @pl.when(i == 0)
        def _():
            ls_ref[...] = jnp.zeros_like(ls_ref)
            acc_ref[...] = jnp.zeros_like(acc_ref)

        yv = y_ref[...]
        r = lax.rsqrt(jnp.mean(yv * yv, axis=-1, keepdims=True) + EPS)
        nrm = yv * r
        e = x_ref[...] + nrm * w_ref[...] - t_ref[...]
        gv = e * (1.0 / d)
        g_ref[...] = gv
        ls_ref[...] += jnp.sum((e * e).reshape(tl // 8, 8, d), axis=0)
        gw = gv * w_ref[...]
        dy_ref[...] = (r * (gw - nrm * jnp.mean(gw * nrm, axis=-1, keepdims=True))).astype(dy_ref.dtype)
        acc_ref[...] += jnp.sum((gv * nrm).reshape(tl // 8, 8, d), axis=0)

        @pl.when(i == nt - 1)
        def _():
            dw_ref[...] = jnp.sum(acc_ref[...], axis=0, keepdims=True)

    row = pl.BlockSpec((tl, d), lambda i: (i, 0))
    vec = pl.BlockSpec((1, d), lambda i: (0, 0))
    return pl.pallas_call(
        body, grid=(nt,), in_specs=[row, row, vec, row],
        out_specs=[row, row, pl.BlockSpec((8, d), lambda i: (0, 0)), vec],
        out_shape=[jax.ShapeDtypeStruct((l, d), F32), jax.ShapeDtypeStruct((l, d), BF16),
                   jax.ShapeDtypeStruct((8, d), F32), jax.ShapeDtypeStruct((1, d), F32)],
        scratch_shapes=[pltpu.VMEM((8, d), F32)], compiler_params=_params("arbitrary"), name=name,
    )(x, y, w.reshape(1, d), t)


def _norm_bwd_chain(dh, x, w_pre, resid, y_prev, w_post_prev, name):
    l, d = x.shape
    tl = _row_tile(l)
    nt = l // tl

    def body(dh_ref, x_ref, wp_ref, r_ref, y_ref, wq_ref, g_ref, dy_ref, dwp_ref, dwq_ref, accp_ref, accq_ref):
        i = pl.program_id(0)

        @pl.when(i == 0)
        def _():
            accp_ref[...] = jnp.zeros_like(accp_ref)
            accq_ref[...] = jnp.zeros_like(accq_ref)

        xv = x_ref[...]
        dhv = dh_ref[...]
        rx = lax.rsqrt(jnp.mean(xv * xv, axis=-1, keepdims=True) + EPS)
        nx = xv * rx
        gw = dhv * wp_ref[...]
        gv = rx * (gw - nx * jnp.mean(gw * nx, axis=-1, keepdims=True)) + r_ref[...]
        g_ref[...] = gv
        accp_ref[...] += jnp.sum((dhv * nx).reshape(tl // 8, 8, d), axis=0)
        yv = y_ref[...]
        ry = lax.rsqrt(jnp.mean(yv * yv, axis=-1, keepdims=True) + EPS)
        ny = yv * ry
        gq = gv * wq_ref[...]
        dy_ref[...] = (ry * (gq - ny * jnp.mean(gq * ny, axis=-1, keepdims=True))).astype(dy_ref.dtype)
        accq_ref[...] += jnp.sum((gv * ny).reshape(tl // 8, 8, d), axis=0)

        @pl.when(i == nt - 1)
        def _():
            dwp_ref[...] = jnp.sum(accp_ref[...], axis=0, keepdims=True)
            dwq_ref[...] = jnp.sum(accq_ref[...], axis=0, keepdims=True)

    row = pl.BlockSpec((tl, d), lambda i: (i, 0))
    vec = pl.BlockSpec((1, d), lambda i: (0, 0))
    return pl.pallas_call(
        body, grid=(nt,), in_specs=[row, row, vec, row, row, vec], out_specs=[row, row, vec, vec],
        out_shape=[jax.ShapeDtypeStruct((l, d), F32), jax.ShapeDtypeStruct((l, d), BF16),
                   jax.ShapeDtypeStruct((1, d), F32), jax.ShapeDtypeStruct((1, d), F32)],
        scratch_shapes=[pltpu.VMEM((8, d), F32), pltpu.VMEM((8, d), F32)],
        compiler_params=_params("arbitrary"), name=name,
    )(dh, x, w_pre.reshape(1, d), resid, y_prev, w_post_prev.reshape(1, d))


def _rmsnorm_bwd(g, y, w, resid, out_dtype, name):
    l, d = y.shape
    tl = _row_tile(l)
    nt = l // tl
    has_resid = resid is not None

    def body(*refs):
        if has_resid:
            g_ref, y_ref, w_ref, r_ref, dy_ref, dw_ref, acc_ref = refs
        else:
            g_ref, y_ref, w_ref, dy_ref, dw_ref, acc_ref = refs
        i = pl.program_id(0)

        @pl.when(i == 0)
        def _():
            acc_ref[...] = jnp.zeros_like(acc_ref)

        yv = y_ref[...]
        gv = g_ref[...].astype(F32)
        r = lax.rsqrt(jnp.mean(yv * yv, axis=-1, keepdims=True) + EPS)
        nrm = yv * r
        gw = gv * w_ref[...]
        dy = r * (gw - nrm * jnp.mean(gw * nrm, axis=-1, keepdims=True))
        if has_resid:
            dy = dy + r_ref[...]
        dy_ref[...] = dy.astype(dy_ref.dtype)
        acc_ref[...] += jnp.sum((gv * nrm).reshape(tl // 8, 8, d), axis=0)

        @pl.when(i == nt - 1)
        def _():
            dw_ref[...] = jnp.sum(acc_ref[...], axis=0, keepdims=True)

    row = pl.BlockSpec((tl, d), lambda i: (i, 0))
    vec = pl.BlockSpec((1, d), lambda i: (0, 0))
    ins = [g, y, w.reshape(1, d)] + ([resid] if has_resid else [])
    return pl.pallas_call(
        body, grid=(nt,), in_specs=[row, row, vec] + ([row] if has_resid else []),
        out_specs=[row, vec],
        out_shape=[jax.ShapeDtypeStruct((l, d), out_dtype), jax.ShapeDtypeStruct((1, d), F32)],
        scratch_shapes=[pltpu.VMEM((8, d), F32)], compiler_params=_params("arbitrary"), name=name,
    )(*ins)


CONV_COLS = 512
HALO = 8
HALO16 = 16
CONV_SUB_ROWS = 64
CONV_SUB_COLS = 256


def _conv_rows(l):
    return _pick(l, (1024, 512, 256, 128))


def _conv_fwd(proj, cw, cb, name):
    l = proj.shape[0]
    tl = _conv_rows(l)
    off = SSM_D_INNER // CONV_COLS

    def body(u_ref, halo_ref, w_ref, b_ref, pre_ref, act_ref, ext_ref):
        i = pl.program_id(1)
        ext_ref[0:HALO, :] = jnp.where(i > 0, halo_ref[...], 0.0)
        ext_ref[HALO:HALO + tl, :] = u_ref[...]
        for r0 in range(0, tl, CONV_SUB_ROWS):
            for c0 in range(0, CONV_COLS, CONV_SUB_COLS):
                cs = slice(c0, c0 + CONV_SUB_COLS)
                ext = ext_ref[r0:r0 + CONV_SUB_ROWS + HALO, cs]
                acc = b_ref[:, cs] + w_ref[SSM_CONV - 1:SSM_CONV, cs] * ext[HALO:]
                for k in range(SSM_CONV - 1):
                    acc = acc + w_ref[k:k + 1, cs] * pltpu.roll(ext, SSM_CONV - 1 - k, 0)[HALO:]
                pre_ref[r0:r0 + CONV_SUB_ROWS, cs] = acc.astype(pre_ref.dtype)
                act_ref[r0:r0 + CONV_SUB_ROWS, cs] = (acc * _sigmoid(acc)).astype(act_ref.dtype)

    hb = tl // HALO
    out = pl.BlockSpec((tl, CONV_COLS), lambda j, i: (i, j))
    return pl.pallas_call(
        body, grid=(SSM_CONV_DIM // CONV_COLS, l // tl),
        in_specs=[pl.BlockSpec((tl, CONV_COLS), lambda j, i: (i, off + j)),
                  pl.BlockSpec((HALO, CONV_COLS), lambda j, i: (jnp.maximum(i * hb - 1, 0), off + j)),
                  pl.BlockSpec((SSM_CONV, CONV_COLS), lambda j, i: (0, j)),
                  pl.BlockSpec((1, CONV_COLS), lambda j, i: (0, j))],
        out_specs=[out, out],
        out_shape=[jax.ShapeDtypeStruct((l, SSM_CONV_DIM), BF16)] * 2,
        scratch_shapes=[pltpu.VMEM((tl + HALO, CONV_COLS), F32)],
        compiler_params=_params("parallel", "arbitrary"), name=name,
    )(proj, proj, cw, cb.reshape(1, SSM_CONV_DIM))


def _conv_bwd(dact, pre, proj, cw, dproj, name):
    l, width = dact.shape
    tl = _conv_rows(l)
    nt = l // tl
    pre_off = 0
    u_off = SSM_D_INNER // CONV_COLS
    hb = tl // HALO
    hb16 = tl // HALO16
    last_hb16 = l // HALO16 - 1

    def body(da_ref, da_h_ref, p_ref, p_h_ref, u_ref, u_h_ref, w_ref, _, du_ref, dw_ref, db_ref, ext_ref, uext_ref):
        i = pl.program_id(1)

        @pl.when(i == 0)
        def _():
            dw_ref[...] = jnp.zeros_like(dw_ref)
            db_ref[...] = jnp.zeros_like(db_ref)

        def dpre_of(da, p):
            s = _sigmoid(p)
            return da * (s * (1.0 + p * (1.0 - s)))

        ext_ref[0:tl, :] = dpre_of(da_ref[...].astype(F32), p_ref[...].astype(F32))
        ext_ref[tl:tl + HALO, :] = jnp.where(
            i < nt - 1, dpre_of(da_h_ref[...].astype(F32)[:HALO], p_h_ref[...].astype(F32)[:HALO]), 0.0)
        uext_ref[0:HALO, :] = jnp.where(i > 0, u_h_ref[...], 0.0)
        uext_ref[HALO:HALO + tl, :] = u_ref[...]
        sub = CONV_SUB_ROWS
        for c0 in range(0, CONV_COLS, CONV_SUB_COLS):
            cs = slice(c0, c0 + CONV_SUB_COLS)
            dws = [jnp.zeros((1, CONV_SUB_COLS), F32) for _ in range(SSM_CONV)]
            dbs = jnp.zeros((1, CONV_SUB_COLS), F32)
            for r0 in range(0, tl, sub):
                dext = ext_ref[r0:r0 + sub + HALO, cs]
                uext = uext_ref[r0:r0 + sub + HALO, cs]
                dp = dext[:sub]
                du = w_ref[SSM_CONV - 1:SSM_CONV, cs] * dp
                dws[SSM_CONV - 1] = dws[SSM_CONV - 1] + jnp.sum(dp * uext[HALO:], axis=0, keepdims=True)
                for k in range(SSM_CONV - 1):
                    j = SSM_CONV - 1 - k
                    du = du + w_ref[k:k + 1, cs] * pltpu.roll(dext, sub + HALO - j, 0)[:sub]
                    dws[k] = dws[k] + jnp.sum(dp * pltpu.roll(uext, j, 0)[HALO:], axis=0, keepdims=True)
                dbs = dbs + jnp.sum(dp, axis=0, keepdims=True)
                du_ref[r0:r0 + sub, cs] = du.astype(du_ref.dtype)
            for k in range(SSM_CONV):
                dw_ref[k:k + 1, cs] += dws[k]
            db_ref[:, cs] += dbs

    return pl.pallas_call(
        body, grid=(width // CONV_COLS, nt),
        in_specs=[pl.BlockSpec((tl, CONV_COLS), lambda j, i: (i, j)),
                  pl.BlockSpec((HALO16, CONV_COLS), lambda j, i: (jnp.minimum((i + 1) * hb16, last_hb16), j)),
                  pl.BlockSpec((tl, CONV_COLS), lambda j, i: (i, pre_off + j)),
                  pl.BlockSpec((HALO16, CONV_COLS), lambda j, i: (jnp.minimum((i + 1) * hb16, last_hb16), pre_off + j)),
                  pl.BlockSpec((tl, CONV_COLS), lambda j, i: (i, u_off + j)),
                  pl.BlockSpec((HALO, CONV_COLS), lambda j, i: (jnp.maximum(i * hb - 1, 0), u_off + j)),
                  pl.BlockSpec((SSM_CONV, CONV_COLS), lambda j, i: (0, pre_off + j)),
                  pl.BlockSpec(memory_space=pl.ANY)],
        out_specs=[pl.BlockSpec((tl, CONV_COLS), lambda j, i: (i, u_off + j)),
                   pl.BlockSpec((SSM_CONV, CONV_COLS), lambda j, i: (0, j)),
                   pl.BlockSpec((1, CONV_COLS), lambda j, i: (0, j))],
        out_shape=[jax.ShapeDtypeStruct(dproj.shape, dproj.dtype), jax.ShapeDtypeStruct((SSM_CONV, width), F32),
                   jax.ShapeDtypeStruct((1, width), F32)],
        scratch_shapes=[pltpu.VMEM((tl + HALO, CONV_COLS), F32), pltpu.VMEM((tl + HALO, CONV_COLS), F32)],
        input_output_aliases={7: 0}, compiler_params=_params("parallel", "arbitrary"), name=name,
    )(dact, dact, pre, pre, proj, proj, cw, dproj)


DT_COL_BLOCK = (SSM_D_INNER + SSM_CONV_DIM) // SSM_DT_PAD


def _split3(v):
    hi = v.astype(BF16)
    rest = v - hi.astype(F32)
    mid = rest.astype(BF16)
    lo = (rest - mid.astype(F32)).astype(BF16)
    return hi, mid, lo


def _ssd_prep(proj, bias, alog_lanes, name):
    l = proj.shape[0]
    nc = l // SSM_CHUNK
    head_dim_log2 = SSM_HEAD_DIM.bit_length() - 1

    def body(p_ref, b_ref, al_ref, dtb_ref, acsb_ref, dtr_ref, acsr_ref):
        v = p_ref[...] + b_ref[...]
        dt = jnp.maximum(v, 0.0) + jnp.log1p(jnp.exp(-jnp.abs(v)))
        head_of_lane = lax.shift_right_logical(lax.broadcasted_iota(jnp.int32, (SSM_DT_PAD, SSM_D_INNER), 1), head_dim_log2)
        spread = (head_of_lane == lax.broadcasted_iota(jnp.int32, (SSM_DT_PAD, SSM_D_INNER), 0)).astype(BF16)
        dtb = sum(jnp.dot(piece, spread, preferred_element_type=F32) for piece in _split3(dt)[:2])
        dtb_ref[...] = dtb
        ri = lax.broadcasted_iota(jnp.int32, (SSM_CHUNK, SSM_CHUNK), 0)
        cj = lax.broadcasted_iota(jnp.int32, (SSM_CHUNK, SSM_CHUNK), 1)
        tri = (ri >= cj).astype(BF16)
        acsb = sum(jnp.dot(tri, piece, preferred_element_type=F32) for piece in _split3(dtb * (-jnp.exp(al_ref[...]))))
        acsb_ref[...] = acsb
        gp = SSM_HPG * SSM_HEAD_DIM
        lane = lax.broadcasted_iota(jnp.int32, (SSM_HPG, gp), 1)
        pick = (lane == lax.broadcasted_iota(jnp.int32, (SSM_HPG, gp), 0) * SSM_HEAD_DIM).astype(BF16)
        for g in range(SSM_GROUPS):
            cols = slice(g * gp, (g + 1) * gp)
            dtr_ref[g] = sum(lax.dot_general(pick, piece, NT_DIMS, preferred_element_type=F32)
                             for piece in _split3(dtb[:, cols]))
            acsr_ref[g] = sum(lax.dot_general(pick, piece, NT_DIMS, preferred_element_type=F32)
                              for piece in _split3(acsb[:, cols]))

    rows = pl.BlockSpec((SSM_GROUPS, SSM_HPG, SSM_CHUNK), lambda c: (0, 0, c))
    dense = pl.BlockSpec((SSM_CHUNK, SSM_D_INNER), lambda c: (c, 0))
    return pl.pallas_call(
        body, grid=(nc,),
        in_specs=[pl.BlockSpec((SSM_CHUNK, SSM_DT_PAD), lambda c: (c, DT_COL_BLOCK)),
                  pl.BlockSpec((1, SSM_DT_PAD), lambda c: (0, 0)),
                  pl.BlockSpec((1, SSM_D_INNER), lambda c: (0, 0))],
        out_specs=[dense, dense, rows, rows],
        out_shape=[jax.ShapeDtypeStruct((l, SSM_D_INNER), F32), jax.ShapeDtypeStruct((l, SSM_D_INNER), F32),
                   jax.ShapeDtypeStruct((SSM_GROUPS, SSM_HPG, l), F32),
                   jax.ShapeDtypeStruct((SSM_GROUPS, SSM_HPG, l), F32)],
        compiler_params=_params("parallel"), name=name,
    )(proj, bias, alog_lanes)


def _dt_bwd(ddt, proj, bias, dproj, name):
    l = proj.shape[0]
    tl = _row_tile(l)

    def body(g_ref, p_ref, b_ref, _, o_ref, db_ref):
        @pl.when(pl.program_id(0) == 0)
        def _():
            db_ref[...] = jnp.zeros_like(db_ref)

        d = g_ref[...] * _sigmoid(p_ref[...] + b_ref[...])
        o_ref[...] = d.astype(o_ref.dtype)
        db_ref[...] += jnp.sum(d, axis=0, keepdims=True)

    return pl.pallas_call(
        body, grid=(l // tl,),
        in_specs=[pl.BlockSpec((tl, SSM_DT_PAD), lambda i: (i, 0)),
                  pl.BlockSpec((tl, SSM_DT_PAD), lambda i: (i, DT_COL_BLOCK)),
                  pl.BlockSpec((1, SSM_DT_PAD), lambda i: (0, 0)),
                  pl.BlockSpec(memory_space=pl.ANY)],
        out_specs=[pl.BlockSpec((tl, SSM_DT_PAD), lambda i: (i, DT_COL_BLOCK)),
                   pl.BlockSpec((1, SSM_DT_PAD), lambda i: (0, 0))],
        out_shape=[jax.ShapeDtypeStruct(dproj.shape, dproj.dtype), jax.ShapeDtypeStruct((1, SSM_DT_PAD), F32)],
        input_output_aliases={3: 0}, compiler_params=_params("arbitrary"), name=name,
    )(ddt, proj, bias, dproj)


GP = SSM_HPG * SSM_HEAD_DIM
HEAD_DIM_LOG2 = SSM_HEAD_DIM.bit_length() - 1
CHUNK_LOG2 = SSM_CHUNK.bit_length() - 1
GPS = 8
B_BLOCK0 = SSM_D_INNER // SSM_STATE
C_BLOCK0 = (SSM_D_INNER + SSM_BC_DIM) // SSM_STATE


def _chunk_iotas():
    ri = lax.broadcasted_iota(jnp.int32, (SSM_CHUNK, SSM_CHUNK), 0)
    cj = lax.broadcasted_iota(jnp.int32, (SSM_CHUNK, SSM_CHUNK), 1)
    return ri, cj


def _head_decay(acsb, acs_r, r, ri, cj):
    pair = acsb[:, (r // 2) * LANES:(r // 2 + 1) * LANES]
    mine_low = r % 2 == 0
    lane = lax.broadcasted_iota(jnp.int32, (1, LANES), 1)
    col = jnp.where((lane < SSM_HEAD_DIM) == mine_low, pair, pltpu.roll(pair, SSM_HEAD_DIM, 1))
    return jnp.exp(jnp.where(ri >= cj, col - acs_r[r:r + 1, :], NEG_INF))


def _head_masked_rows(v, dtype):
    head_of_lane = lax.shift_right_logical(lax.broadcasted_iota(jnp.int32, (1, GP), 1), HEAD_DIM_LOG2)
    return jnp.concatenate([jnp.where(head_of_lane == r, v, 0.0).astype(dtype) for r in range(SSM_HPG)], axis=0)


def _ssd_fwd(xbc, dtb, acsb, acs_r, d_lanes, proj, gate_w, name, ride=()):
    l = xbc.shape[0]
    nc = l // SSM_CHUNK
    assert GPS == SSM_GROUPS

    n_ride = len(ride)

    def body(*refs):
        x_ref, b_ref, c_ref, dtb_ref, acsb_ref, acsr_ref, d_ref, z_ref, gw_ref = refs[:9]
        ride_in = refs[9:9 + n_ride]
        y_ref, act_ref, hin_ref = refs[9 + n_ride:12 + n_ride]
        ride_out = refs[12 + n_ride:12 + 2 * n_ride]
        h_ref = refs[12 + 2 * n_ride]
        ride_sems = refs[13 + 2 * n_ride:]
        c = pl.program_id(0)
        if n_ride:
            @pl.when(c == 0)
            def _():
                _gather_between_chips(ride_in, ride_out, *ride_sems, wait=False)

            @pl.when(c == nc - 1)
            def _():
                _gather_between_chips(ride_in, ride_out, *ride_sems, wait=True)

        ri, cj = _chunk_iotas()
        for k in range(GPS):
            g = k
            cols = slice(k * GP, (k + 1) * GP)
            ncols = slice(k * SSM_STATE, (k + 1) * SSM_STATE)

            @pl.when(c == 0)
            def _():
                h_ref[g] = jnp.zeros((SSM_STATE, GP), F32)

            xv = x_ref[:, cols].astype(F32)
            bb = b_ref[:, ncols].astype(BF16)
            cb16 = c_ref[:, ncols].astype(BF16)
            acs_v = acsb_ref[:, cols]
            acs_r_v = acsr_ref[k]
            lastb = acs_v[SSM_CHUNK - 1:SSM_CHUNK, :]
            xd = xv * dtb_ref[:, cols]
            cb = lax.dot_general(cb16, bb, NT_DIMS, preferred_element_type=F32)
            hin = h_ref[g]
            hin_ref[0, k] = hin
            yoff = jnp.dot(cb16, hin.astype(BF16), preferred_element_type=F32)
            ms = [(cb * _head_decay(acs_v, acs_r_v, r, ri, cj)).astype(BF16) for r in range(SSM_HPG)]
            ydiag = jnp.dot(jnp.concatenate(ms, axis=1), _head_masked_rows(xd, BF16), preferred_element_type=F32)
            y_ref[:, cols] = ydiag + jnp.exp(acs_v) * yoff + d_ref[k] * xv
            h_ref[g] = hin * jnp.exp(lastb) + _bdot_tn(bb, xd * jnp.exp(lastb - acs_v))
        z = z_ref[...]
        yg = y_ref[...] * (z * _sigmoid(z))
        r = lax.rsqrt(jnp.mean(yg * yg, axis=-1, keepdims=True) + EPS)
        act_ref[...] = (yg * r * gw_ref[...]).astype(act_ref.dtype)

    lanes = pl.BlockSpec((SSM_CHUNK, SSM_D_INNER), lambda c: (c, 0))
    return pl.pallas_call(
        body, grid=(nc,),
        in_specs=[lanes,
                  pl.BlockSpec((SSM_CHUNK, SSM_BC_DIM), lambda c: (c, B_BLOCK0 // GPS)),
                  pl.BlockSpec((SSM_CHUNK, SSM_BC_DIM), lambda c: (c, C_BLOCK0 // GPS)),
                  lanes, lanes,
                  pl.BlockSpec((SSM_GROUPS, SSM_HPG, SSM_CHUNK), lambda c: (0, 0, c)),
                  pl.BlockSpec((SSM_GROUPS, 1, GP), lambda c: (0, 0, 0)),
                  lanes, pl.BlockSpec((1, SSM_D_INNER), lambda c: (0, 0))] + [ANY] * n_ride,
        out_specs=[lanes, lanes, pl.BlockSpec((1, SSM_GROUPS, SSM_STATE, GP), lambda c: (c, 0, 0, 0))] + [ANY] * n_ride,
        out_shape=[jax.ShapeDtypeStruct((l, SSM_D_INNER), F32), jax.ShapeDtypeStruct((l, SSM_D_INNER), BF16),
                   jax.ShapeDtypeStruct((nc, SSM_GROUPS, SSM_STATE, GP), F32)]
        + [jax.ShapeDtypeStruct((N_CHIPS,) + a.shape, a.dtype) for a in ride],
        scratch_shapes=[pltpu.VMEM((SSM_GROUPS, SSM_STATE, GP), F32)] + (_gather_sems(n_ride) if n_ride else []),
        compiler_params=_params("arbitrary"), name=name,
    )(xbc, xbc, xbc, dtb, acsb, acs_r, d_lanes, proj, gate_w.reshape(1, SSM_D_INNER), *ride)


def _ssd_bwd(xbc, dtb, acsb, dtr, acs_r, a_log, d_lanes, hin, dact, y, proj, gate_w, name):
    l = xbc.shape[0]
    nc = l // SSM_CHUNK

    def body(x_ref, b_ref, c_ref, dtb_ref, acsb_ref, dtr_ref, acsr_ref, alc_ref, d_ref, hin_ref,
             dact_ref, y_ref, z_ref, gw_ref,
             dxbc_ref, ddt_ref, dal_ref, dd_ref, dproj_ref, dgw_ref, dh_ref, dy_ref, acc_ref):
        c = pl.program_id(0)
        dx_ref = dxbc_ref.at[:, 0:SSM_D_INNER]
        db_ref = dxbc_ref.at[:, SSM_D_INNER:SSM_D_INNER + SSM_BC_DIM]
        dc_ref = dxbc_ref.at[:, SSM_D_INNER + SSM_BC_DIM:SSM_CONV_DIM]

        @pl.when(c == 0)
        def _():
            dal_ref[...] = jnp.zeros_like(dal_ref)
            dd_ref[...] = jnp.zeros_like(dd_ref)
            acc_ref[...] = jnp.zeros_like(acc_ref)

        z = z_ref[...]
        yv = y_ref[...]
        s = _sigmoid(z)
        sz = z * s
        yg = yv * sz
        r = lax.rsqrt(jnp.mean(yg * yg, axis=-1, keepdims=True) + EPS)
        nrm = yg * r
        gv = dact_ref[...]
        gw = gv * gw_ref[...]
        dyg = r * (gw - nrm * jnp.mean(gw * nrm, axis=-1, keepdims=True))
        dy_ref[...] = dyg * sz
        dproj_ref[...] = (dyg * yv * (s * (1.0 + z * (1.0 - s)))).astype(dproj_ref.dtype)
        acc_ref[...] += jnp.sum((gv * nrm).reshape(SSM_CHUNK // 8, 8, SSM_D_INNER), axis=0)

        @pl.when(c == nc - 1)
        def _():
            dgw_ref[...] = jnp.sum(acc_ref[...], axis=0, keepdims=True)

        for k in range(GPS):
            one_group(c, k, k, x_ref, b_ref, c_ref, dtb_ref, acsb_ref, dtr_ref, acsr_ref, alc_ref, d_ref,
                      hin_ref, dy_ref, dx_ref, db_ref, dc_ref, ddt_ref, dal_ref, dd_ref, dh_ref)

    def one_group(c, g, k, x_ref, b_ref, c_ref, dtb_ref, acsb_ref, dtr_ref, acsr_ref, alc_ref, d_ref, hin_ref, dy_ref,
                  dx_ref, db_ref, dc_ref, ddt_ref, dal_ref, dd_ref, dh_ref):
        cols = slice(k * GP, (k + 1) * GP)
        ncols = slice(k * SSM_STATE, (k + 1) * SSM_STATE)

        @pl.when(c == 0)
        def _():
            dh_ref[g] = jnp.zeros((SSM_STATE, GP), F32)

        xv = x_ref[:, cols].astype(F32)
        dyv = dy_ref[:, cols]
        bb = b_ref[:, ncols].astype(BF16)
        cb16 = c_ref[:, ncols].astype(BF16)
        dtb = dtb_ref[:, cols]
        acsb = acsb_ref[:, cols]
        dtr_v = dtr_ref[k]
        acs_r = acsr_ref[k]
        a_col = -jnp.exp(alc_ref[k])
        ri, cj = _chunk_iotas()
        head_of_lane = lax.shift_right_logical(lax.broadcasted_iota(jnp.int32, (SSM_HPG, GP), 1), HEAD_DIM_LOG2)
        ind_t = (head_of_lane == lax.broadcasted_iota(jnp.int32, (SSM_HPG, GP), 0)).astype(BF16)
        lastb = acsb[SSM_CHUNK - 1:SSM_CHUNK, :]
        ecb = jnp.exp(acsb)
        dteb = jnp.exp(lastb - acsb)
        xd = xv * dtb
        xw = xd * dteb
        cb = lax.dot_general(cb16, bb, NT_DIMS, preferred_element_type=F32)
        hin_v = hin_ref[0, k]
        dhn = dh_ref[g]
        h16 = hin_v.astype(BF16)
        dh16 = dhn.astype(BF16)
        ch = jnp.dot(cb16, h16, preferred_element_type=F32)
        bdh = jnp.dot(bb, dh16, preferred_element_type=F32)
        dym = _head_masked_rows(dyv, BF16)
        g_all = lax.dot_general(dym, xd.astype(BF16), NT_DIMS, preferred_element_type=F32)
        gl_sum = jnp.zeros((SSM_CHUNK, SSM_CHUNK), F32)
        ms, qs = [], []
        for r in range(SSM_HPG):
            decay = _head_decay(acsb, acs_r, r, ri, cj)
            gl = g_all[r * SSM_CHUNK:(r + 1) * SSM_CHUNK] * decay
            gl_sum = gl_sum + gl
            ms.append((cb * decay).astype(BF16))
            qs.append((gl * cb).astype(BF16))
        dxd = lax.dot_general(jnp.concatenate(ms, axis=0), dym, TN_DIMS, preferred_element_type=F32) + dteb * bdh
        cum = jnp.dot(jnp.concatenate(qs, axis=0), (ri < cj).astype(BF16), preferred_element_type=F32)
        sub4 = lax.broadcasted_iota(jnp.int32, (SSM_HPG, 1), 0)
        da = jnp.zeros((SSM_HPG, SSM_CHUNK), F32)
        for r in range(SSM_HPG):
            rect = jnp.sum(jnp.where(ri >= cj, cum[r * SSM_CHUNK:(r + 1) * SSM_CHUNK], 0.0), axis=0, keepdims=True)
            da = da + jnp.where(sub4 == r, rect, 0.0)
        z2 = xw * bdh
        sub8 = lax.broadcasted_iota(jnp.int32, (8, 1), 0)
        col_sums = (jnp.where(sub8 == 0, jnp.sum(z2, axis=0, keepdims=True), 0.0)
                    + jnp.where(sub8 == 1, jnp.sum(dhn * hin_v, axis=0, keepdims=True), 0.0)
                    + jnp.where(sub8 == 2, jnp.sum(dyv * xv, axis=0, keepdims=True), 0.0))
        summands = jnp.concatenate([dyv * ecb * ch - z2, dxd * xv, col_sums], axis=0)
        sums = sum(lax.dot_general(ind_t, piece, NT_DIMS, preferred_element_type=F32) for piece in _split3(summands)[:2])
        per_pos = sums[:, :2 * SSM_CHUNK]
        totals = sums[:, 2 * SSM_CHUNK:]
        e_last = totals[:, 0:1] + jnp.exp(acs_r[:, SSM_CHUNK - 1:SSM_CHUNK]) * totals[:, 1:2]
        da = (da + e_last + jnp.dot(per_pos[:, :SSM_CHUNK], (ri >= cj).astype(F32), preferred_element_type=F32,
                                    precision=lax.Precision.HIGHEST))
        ddt_ref[k] = a_col * da + per_pos[:, SSM_CHUNK:]
        dal_ref[g] += a_col * jnp.sum(da * dtr_v, axis=1, keepdims=True)
        dd_ref[g] += totals[:, 2:3]
        dx_ref[:, cols] = (dxd * dtb + d_ref[k] * dyv).astype(dx_ref.dtype)
        w16 = (ecb * dyv).astype(BF16)
        xw16 = xw.astype(BF16)
        gl16 = gl_sum.astype(BF16)
        dc_ref[:, ncols] = (jnp.dot(gl16, bb, preferred_element_type=F32)
                            + lax.dot_general(w16, h16, NT_DIMS, preferred_element_type=F32)).astype(dc_ref.dtype)
        db_ref[:, ncols] = (lax.dot_general(gl16, cb16, TN_DIMS, preferred_element_type=F32)
                            + lax.dot_general(xw16, dh16, NT_DIMS, preferred_element_type=F32)).astype(db_ref.dtype)
        dh_ref[g] = dhn * jnp.exp(lastb) + lax.dot_general(cb16, w16, TN_DIMS, preferred_element_type=F32)

    def rev(c):
        return nc - 1 - c

    small = pl.BlockSpec((SSM_GROUPS, SSM_HPG, 1), lambda c: (0, 0, 0))
    lanes = pl.BlockSpec((SSM_CHUNK, SSM_D_INNER), lambda c: (rev(c), 0))
    rows = pl.BlockSpec((SSM_GROUPS, SSM_HPG, SSM_CHUNK), lambda c: (0, 0, rev(c)))
    vec = pl.BlockSpec((1, SSM_D_INNER), lambda c: (0, 0))
    return pl.pallas_call(
        body, grid=(nc,),
        in_specs=[lanes,
                  pl.BlockSpec((SSM_CHUNK, SSM_BC_DIM), lambda c: (rev(c), B_BLOCK0 // GPS)),
                  pl.BlockSpec((SSM_CHUNK, SSM_BC_DIM), lambda c: (rev(c), C_BLOCK0 // GPS)),
                  lanes, lanes, rows, rows,
                  pl.BlockSpec((SSM_GROUPS, SSM_HPG, 1), lambda c: (0, 0, 0)),
                  pl.BlockSpec((SSM_GROUPS, 1, GP), lambda c: (0, 0, 0)),
                  pl.BlockSpec((1, SSM_GROUPS, SSM_STATE, GP), lambda c: (rev(c), 0, 0, 0)),
                  lanes, lanes, lanes, vec],
        out_specs=[pl.BlockSpec((SSM_CHUNK, SSM_CONV_DIM), lambda c: (rev(c), 0)),
                   rows, small, small, lanes, vec],
        out_shape=[jax.ShapeDtypeStruct((l, SSM_CONV_DIM), BF16), jax.ShapeDtypeStruct((SSM_GROUPS, SSM_HPG, l), F32),
                   jax.ShapeDtypeStruct((SSM_GROUPS, SSM_HPG, 1), F32),
                   jax.ShapeDtypeStruct((SSM_GROUPS, SSM_HPG, 1), F32),
                   jax.ShapeDtypeStruct((l, SSM_IN_PAD), BF16), jax.ShapeDtypeStruct((1, SSM_D_INNER), F32)],
        scratch_shapes=[pltpu.VMEM((SSM_GROUPS, SSM_STATE, GP), F32), pltpu.VMEM((SSM_CHUNK, SSM_D_INNER), F32),
                        pltpu.VMEM((8, SSM_D_INNER), F32)],
        compiler_params=_params("arbitrary"), name=name,
    )(xbc, xbc, xbc, dtb, acsb, dtr, acs_r, a_log.reshape(SSM_GROUPS, SSM_HPG, 1), d_lanes, hin, dact, y, proj,
      gate_w.reshape(1, SSM_D_INNER))


LANES = 128
ROPE_Q_CHUNKS = ATT_WIDTH // LANES
ROPE_K_CHUNKS = ATT_KV_WIDTH // LANES


def _rope_tables(positions):
    inv = ROPE_THETA ** (-jnp.arange(0, ROPE_DIM, 2, dtype=F32) / ROPE_DIM)
    ang = positions.astype(F32)[:, None] * inv
    cos, sin = jnp.cos(ang), jnp.sin(ang)
    l = positions.shape[0]
    rest = ATT_HEAD_DIM - ROPE_DIM
    ones, zeros = jnp.ones((l, rest), F32), jnp.zeros((l, rest), F32)
    z8 = jnp.zeros((l, ROPE_HALF), F32)
    cos_f = jnp.concatenate([cos, cos, ones], axis=1)
    sin_a = jnp.concatenate([-sin, z8, zeros], axis=1)
    sin_b = jnp.concatenate([z8, sin, zeros], axis=1)
    reps = LANES // ATT_HEAD_DIM
    return tuple(jnp.tile(t, (1, reps)) for t in (cos_f, sin_a, sin_b))


ATT_QKV4 = 3 * ATT_WIDTH


def _both_halves(chunk):
    lane = lax.broadcasted_iota(jnp.int32, (1, LANES), 1)
    swapped = pltpu.roll(chunk, ATT_HEAD_DIM, 1)
    return jnp.where(lane < ATT_HEAD_DIM, chunk, swapped), jnp.where(lane < ATT_HEAD_DIM, swapped, chunk)


def _rope_fwd(proj, tables, name):
    l = proj.shape[0]
    tl = _pick(l, (256, 128))

    def body(p_ref, c_ref, sa_ref, sb_ref, o_ref):
        cos_f, sin_a, sin_b = c_ref[...], sa_ref[...], sb_ref[...]

        def rope(t):
            return t * cos_f + pltpu.roll(t, LANES - ROPE_HALF, 1) * sin_a + pltpu.roll(t, ROPE_HALF, 1) * sin_b

        for k in range(ROPE_Q_CHUNKS):
            sl = slice(k * LANES, (k + 1) * LANES)
            o_ref[:, sl] = (rope(p_ref[:, sl]) * Q_SCALE).astype(o_ref.dtype)
        for part in range(2):
            for k in range(ROPE_K_CHUNKS):
                src = ATT_WIDTH + part * ATT_KV_WIDTH + k * LANES
                t = p_ref[:, src:src + LANES]
                if part == 0:
                    t = rope(t)
                for head, dup in enumerate(_both_halves(t.astype(o_ref.dtype))):
                    dst = (1 + part) * ATT_WIDTH + (2 * k + head) * ATT_GQA * ATT_HEAD_DIM
                    o_ref[:, dst:dst + LANES] = dup
                    o_ref[:, dst + LANES:dst + 2 * LANES] = dup

    tab = pl.BlockSpec((tl, LANES), lambda i: (i, 0))
    return pl.pallas_call(
        body, grid=(l // tl,), in_specs=[pl.BlockSpec((tl, ATT_IN_DIM), lambda i: (i, 0)), tab, tab, tab],
        out_specs=pl.BlockSpec((tl, ATT_QKV4), lambda i: (i, 0)),
        out_shape=jax.ShapeDtypeStruct((l, ATT_QKV4), BF16), compiler_params=_params("parallel"), name=name,
    )(proj, *tables)


def _rope_bwd(dq, dk4, dv4, dgate, tables, name):
    l = dq.shape[0]
    tl = _pick(l, (256, 128))

    def body(dq_ref, dk_ref, dv_ref, dg_ref, c_ref, sa_ref, sb_ref, o_ref):
        cos_f, sin_a, sin_b = c_ref[...], sa_ref[...], sb_ref[...]
        lane = lax.broadcasted_iota(jnp.int32, (1, LANES), 1)

        def unrope(t):
            return t * cos_f + pltpu.roll(t * sin_a, ROPE_HALF, 1) + pltpu.roll(t * sin_b, LANES - ROPE_HALF, 1)

        def head_total(ref, kvh):
            base = kvh * ATT_GQA * ATT_HEAD_DIM
            s = ref[:, base:base + LANES] + ref[:, base + LANES:base + 2 * LANES]
            return s + pltpu.roll(s, ATT_HEAD_DIM, 1)

        for k in range(ROPE_Q_CHUNKS):
            sl = slice(k * LANES, (k + 1) * LANES)
            o_ref[:, sl] = unrope(dq_ref[:, sl] * Q_SCALE).astype(o_ref.dtype)
        for k in range(ROPE_K_CHUNKS):
            dk = jnp.where(lane < ATT_HEAD_DIM, head_total(dk_ref, 2 * k), head_total(dk_ref, 2 * k + 1))
            dv = jnp.where(lane < ATT_HEAD_DIM, head_total(dv_ref, 2 * k), head_total(dv_ref, 2 * k + 1))
            o_ref[:, ATT_WIDTH + k * LANES:ATT_WIDTH + (k + 1) * LANES] = unrope(dk).astype(o_ref.dtype)
            at = ATT_WIDTH + ATT_KV_WIDTH + k * LANES
            o_ref[:, at:at + LANES] = dv.astype(o_ref.dtype)
        o_ref[:, ATT_QKV:ATT_IN_DIM] = dg_ref[...].astype(o_ref.dtype)

    tab = pl.BlockSpec((tl, LANES), lambda i: (i, 0))
    wide = pl.BlockSpec((tl, ATT_WIDTH), lambda i: (i, 0))
    return pl.pallas_call(
        body, grid=(l // tl,), in_specs=[wide, wide, wide, wide, tab, tab, tab],
        out_specs=pl.BlockSpec((tl, ATT_IN_DIM), lambda i: (i, 0)),
        out_shape=jax.ShapeDtypeStruct((l, ATT_IN_DIM), BF16), compiler_params=_params("parallel"), name=name,
    )(dq, dk4, dv4, dgate, *tables)


GATE_HALF = ATT_WIDTH // 2
GATE_COL_BLOCK = ATT_QKV // GATE_HALF


ATT_STACK = ATT_GQA * ATT_BLOCK
BLOCK_LOG2 = ATT_BLOCK.bit_length() - 1


def _stack_masks(n):
    ri = lax.broadcasted_iota(jnp.int32, (ATT_STACK, ATT_BLOCK), 0) & (ATT_BLOCK - 1)
    cj = lax.broadcasted_iota(jnp.int32, (ATT_STACK, ATT_BLOCK), 1)
    return (cj > ri) & (n > 0), cj <= ri


def _stack_sinks(sink_ref, kvh):
    blk = lax.shift_right_logical(lax.broadcasted_iota(jnp.int32, (ATT_STACK, 1), 0), BLOCK_LOG2)
    col = jnp.zeros((ATT_STACK, 1), F32)
    for r in range(ATT_GQA):
        col = jnp.where(blk == r, sink_ref[kvh * ATT_GQA + r], col)
    return col


def _stack_fold(stack):
    head_of_lane = lax.shift_right_logical(lax.broadcasted_iota(jnp.int32, (1, GP), 1), HEAD_DIM_LOG2)
    out = jnp.zeros((ATT_BLOCK, GP), F32)
    for r in range(ATT_GQA):
        out = jnp.where(head_of_lane == r, stack[r * ATT_BLOCK:(r + 1) * ATT_BLOCK], out)
    return out


def _attn_fwd(qkv, proj, sinks, name):
    l = qkv.shape[0]
    nb = l // ATT_BLOCK

    def body(sink_ref, q_ref, kp_ref, kc_ref, vp_ref, vc_ref, g0_ref, g1_ref, og_ref, o_ref, lse_ref):
        n = pl.program_id(0)
        mask_p, mask_c = _stack_masks(n)
        ones = jnp.ones((ATT_BLOCK, LANES), BF16)
        for kvh in range(ATT_KV_HEADS):
            cols = slice(kvh * GP, (kvh + 1) * GP)
            q_stack = _head_masked_rows(q_ref[:, cols], BF16)
            sp = jnp.where(mask_p, lax.dot_general(q_stack, kp_ref[:, cols], NT_DIMS, preferred_element_type=F32), NEG_INF)
            sc = jnp.where(mask_c, lax.dot_general(q_stack, kc_ref[:, cols], NT_DIMS, preferred_element_type=F32), NEG_INF)
            sink = _stack_sinks(sink_ref, kvh)
            m = jnp.maximum(jnp.max(jnp.maximum(sp, sc), axis=1, keepdims=True), sink)
            pp = jnp.exp(sp - m).astype(BF16)
            pc = jnp.exp(sc - m).astype(BF16)
            acc = (jnp.dot(pp, jnp.concatenate([vp_ref[:, cols], ones], axis=1), preferred_element_type=F32)
                   + jnp.dot(pc, jnp.concatenate([vc_ref[:, cols], ones], axis=1), preferred_element_type=F32))
            den = acc[:, GP:] + jnp.exp(sink - m)
            inv = 1.0 / den
            o_ref[:, cols] = _stack_fold(acc[:, :GP] * jnp.concatenate([inv, inv], axis=1))
            lse = m + jnp.log(den)
            lse_ref[:, cols] = _stack_fold(jnp.concatenate([lse, lse], axis=1))
        for half, g_ref in enumerate((g0_ref, g1_ref)):
            sl = slice(half * GATE_HALF, (half + 1) * GATE_HALF)
            gate = g_ref[...]
            og_ref[:, sl] = (o_ref[:, sl] * (gate * _sigmoid(gate))).astype(og_ref.dtype)

    def prev(n):
        return jnp.maximum(n - 1, 0)

    wide = pl.BlockSpec((ATT_BLOCK, ATT_WIDTH), lambda n: (n, 0))
    return pl.pallas_call(
        body, grid=(nb,),
        in_specs=[pl.BlockSpec(memory_space=pltpu.SMEM), wide,
                  pl.BlockSpec((ATT_BLOCK, ATT_WIDTH), lambda n: (prev(n), 1)),
                  pl.BlockSpec((ATT_BLOCK, ATT_WIDTH), lambda n: (n, 1)),
                  pl.BlockSpec((ATT_BLOCK, ATT_WIDTH), lambda n: (prev(n), 2)),
                  pl.BlockSpec((ATT_BLOCK, ATT_WIDTH), lambda n: (n, 2)),
                  pl.BlockSpec((ATT_BLOCK, GATE_HALF), lambda n: (n, GATE_COL_BLOCK)),
                  pl.BlockSpec((ATT_BLOCK, GATE_HALF), lambda n: (n, GATE_COL_BLOCK + 1))],
        out_specs=[wide, wide, wide],
        out_shape=[jax.ShapeDtypeStruct((l, ATT_WIDTH), BF16), jax.ShapeDtypeStruct((l, ATT_WIDTH), F32),
                   jax.ShapeDtypeStruct((l, ATT_WIDTH), F32)],
        compiler_params=_params("parallel"), name=name,
    )(sinks, qkv, qkv, qkv, qkv, qkv, proj, proj)


def _attn_bwd(qkv, proj, sinks, o, lse, dog, name):
    l = qkv.shape[0]
    nb = l // ATT_BLOCK

    def body(sink_ref, q_ref, kp_ref, kc_ref, vp_ref, vc_ref, g0_ref, g1_ref, o_ref, lse_ref, dog_ref,
             dq_ref, dk_ref, dv_ref, dg_ref, ds_ref, ck_ref, cv_ref, do_ref):
        n = pl.program_id(0)

        @pl.when(n == 0)
        def _():
            ds_ref[...] = jnp.zeros_like(ds_ref)
            ck_ref[...] = jnp.zeros_like(ck_ref)
            cv_ref[...] = jnp.zeros_like(cv_ref)

        @pl.when(n == nb)
        def _():
            dk_ref[...] = ck_ref[...]
            dv_ref[...] = cv_ref[...]

        @pl.when(n < nb)
        def _():
            mask_p, mask_c = _stack_masks(n)
            lane = lax.broadcasted_iota(jnp.int32, (1, ATT_Q_HEADS), 1)
            for half, g_ref in enumerate((g0_ref, g1_ref)):
                sl = slice(half * GATE_HALF, (half + 1) * GATE_HALF)
                gate = g_ref[...]
                s = _sigmoid(gate)
                dogv = dog_ref[:, sl]
                do_ref[:, sl] = dogv * (gate * s)
                dg_ref[:, sl] = dogv * o_ref[:, sl] * (s * (1.0 + gate * (1.0 - s)))
            ds_acc = jnp.zeros((1, ATT_Q_HEADS), F32)
            for kvh in range(ATT_KV_HEADS):
                cols = slice(kvh * GP, (kvh + 1) * GP)
                kp, kc, vp, vc = kp_ref[:, cols], kc_ref[:, cols], vp_ref[:, cols], vc_ref[:, cols]
                q_stack = _head_masked_rows(q_ref[:, cols], BF16)
                do_g = do_ref[:, cols]
                do_stack = _head_masked_rows(do_g, BF16)
                lse_g = lse_ref[:, cols]
                lse_stack = jnp.concatenate(
                    [_both_halves(lse_g[:, (r // 2) * LANES:(r // 2 + 1) * LANES])[r % 2] for r in range(ATT_GQA)], axis=0)
                pp = jnp.exp(jnp.where(
                    mask_p, lax.dot_general(q_stack, kp, NT_DIMS, preferred_element_type=F32) - lse_stack, NEG_INF))
                pc = jnp.exp(jnp.where(
                    mask_c, lax.dot_general(q_stack, kc, NT_DIMS, preferred_element_type=F32) - lse_stack, NEG_INF))
                dpp = lax.dot_general(do_stack, vp, NT_DIMS, preferred_element_type=F32)
                dpc = lax.dot_general(do_stack, vc, NT_DIMS, preferred_element_type=F32)
                delta = jnp.sum(pp * dpp + pc * dpc, axis=1, keepdims=True)
                dsp = (pp * (dpp - delta)).astype(BF16)
                dsc = (pc * (dpc - delta)).astype(BF16)
                dq_ref[:, cols] = _stack_fold(jnp.dot(dsp, kp, preferred_element_type=F32)
                                              + jnp.dot(dsc, kc, preferred_element_type=F32))
                dk_ref[:, cols] = ck_ref[:, cols] + lax.dot_general(dsp, q_stack, TN_DIMS, preferred_element_type=F32)
                dv_ref[:, cols] = cv_ref[:, cols] + lax.dot_general(pp.astype(BF16), do_stack, TN_DIMS,
                                                                    preferred_element_type=F32)
                ck_ref[:, cols] = lax.dot_general(dsc, q_stack, TN_DIMS, preferred_element_type=F32)
                cv_ref[:, cols] = lax.dot_general(pc.astype(BF16), do_stack, TN_DIMS, preferred_element_type=F32)
                t = jnp.exp(_stack_sinks(sink_ref, kvh) - lse_stack) * delta
                for r in range(ATT_GQA):
                    tot = jnp.sum(t[r * ATT_BLOCK:(r + 1) * ATT_BLOCK], axis=0, keepdims=True)
                    ds_acc = ds_acc - jnp.where(lane == kvh * ATT_GQA + r, tot[:, :ATT_Q_HEADS], 0.0)
            ds_ref[...] += ds_acc

    def cur(n):
        return jnp.minimum(n, nb - 1)

    def prev(n):
        return jnp.maximum(n - 1, 0)

    wide = pl.BlockSpec((ATT_BLOCK, ATT_WIDTH), lambda n: (cur(n), 0))
    late = pl.BlockSpec((ATT_BLOCK, ATT_WIDTH), lambda n: (prev(n), 0))
    return pl.pallas_call(
        body, grid=(nb + 1,),
        in_specs=[pl.BlockSpec(memory_space=pltpu.SMEM), wide,
                  pl.BlockSpec((ATT_BLOCK, ATT_WIDTH), lambda n: (prev(cur(n)), 1)),
                  pl.BlockSpec((ATT_BLOCK, ATT_WIDTH), lambda n: (cur(n), 1)),
                  pl.BlockSpec((ATT_BLOCK, ATT_WIDTH), lambda n: (prev(cur(n)), 2)),
                  pl.BlockSpec((ATT_BLOCK, ATT_WIDTH), lambda n: (cur(n), 2)),
                  pl.BlockSpec((ATT_BLOCK, GATE_HALF), lambda n: (cur(n), GATE_COL_BLOCK)),
                  pl.BlockSpec((ATT_BLOCK, GATE_HALF), lambda n: (cur(n), GATE_COL_BLOCK + 1)),
                  wide, wide, wide],
        out_specs=[wide, late, late, wide, pl.BlockSpec((1, ATT_Q_HEADS), lambda n: (0, 0))],
        out_shape=[jax.ShapeDtypeStruct((l, ATT_WIDTH), F32), jax.ShapeDtypeStruct((l, ATT_WIDTH), F32),
                   jax.ShapeDtypeStruct((l, ATT_WIDTH), F32), jax.ShapeDtypeStruct((l, ATT_WIDTH), F32),
                   jax.ShapeDtypeStruct((1, ATT_Q_HEADS), F32)],
        scratch_shapes=[pltpu.VMEM((ATT_BLOCK, ATT_WIDTH), F32), pltpu.VMEM((ATT_BLOCK, ATT_WIDTH), F32),
                        pltpu.VMEM((ATT_BLOCK, ATT_WIDTH), F32)],
        compiler_params=_params("arbitrary"), name=name,
    )(sinks, qkv, qkv, qkv, qkv, qkv, proj, proj, o, lse, dog)


def _local_step(x, positions, pre_norm, post_norm, conv_b, dt_bias, a_log, d_skip, gate_norm, sinks, target,
                first_in, in_proj_with_first_pair, scan_with_second_pair):
    tables = _rope_tables(positions)
    dt_bias_pad = jnp.pad(dt_bias, ((0, 0), (0, SSM_DT_PAD - SSM_HEADS)))
    d_lanes = jnp.repeat(d_skip, SSM_HEAD_DIM, axis=1).reshape(-1, SSM_GROUPS, 1, GP)
    alog_lanes = jnp.repeat(a_log, SSM_HEAD_DIM, axis=1)
    pairs = [first_in, None]
    saved = []
    cur = x
    h = _rmsnorm_fwd(cur, pre_norm[0], "prenorm_fwd_0")
    for i in range(DEPTH):
        j = i // 2
        if i % 2 == 0:
            in_proj = functools.partial(_matmul, h, pairs[j]["ssm_w_in"], "nn", F32, f"ssm_in_{i}")
            if i == 0:
                proj, rest = in_proj_with_first_pair(in_proj)
                pairs[0] = {**first_in, **rest}
            else:
                proj = in_proj()
            pre, xbc = _conv_fwd(proj, pairs[j]["ssm_conv_w"], conv_b[j], f"conv_fwd_{i}")
            dtb, acsb, dtr, acs_r = _ssd_prep(proj, dt_bias_pad[j:j + 1], alog_lanes[j:j + 1], f"ssd_prep_{i}")
            scan = functools.partial(_ssd_fwd, xbc, dtb, acsb, acs_r, d_lanes[j], proj, gate_norm[j], f"ssd_fwd_{i}")
            if i == 0:
                y, act, hin, pairs[1] = scan_with_second_pair(scan)
            else:
                y, act, hin = scan()
            w_ssm_in = [p["ssm_w_in"] for p in pairs]
            w_ssm_out = [p["ssm_w_out"] for p in pairs]
            w_att_in = [p["att_w_in"] for p in pairs]
            w_att_out = [p["att_w_out"] for p in pairs]
            conv_w = [p["ssm_conv_w"] for p in pairs]
            ymix = _matmul(act, w_ssm_out[j], "nn", F32, f"ssm_out_{i}")
            saved.append(dict(x=cur, h=h, proj=proj, pre=pre, xbc=xbc, dtb=dtb, acsb=acsb, dtr=dtr, acs_r=acs_r, y=y,
                              hin=hin, act=act, ymix=ymix))
        else:
            proj = _matmul(h, w_att_in[j], "nn", F32, f"att_in_{i}")
            qkv = _rope_fwd(proj, tables, f"rope_fwd_{i}")
            act, o, lse = _attn_fwd(qkv, proj, sinks[j], f"attn_fwd_{i}")
            ymix = _matmul(act, w_att_out[j], "nn", F32, f"att_out_{i}")
            saved.append(dict(x=cur, h=h, proj=proj, qkv=qkv, o=o, lse=lse, act=act, ymix=ymix))
        if i + 1 < DEPTH:
            cur, h = _post_fwd(cur, ymix, post_norm[i], pre_norm[i + 1], f"post_fwd_{i}")

    gr = {k: [None] * 2 for k in ("ssm_w_in", "ssm_conv_w", "ssm_conv_b", "ssm_dt_bias", "ssm_a_log", "ssm_d",
                                  "ssm_gate_norm", "ssm_w_out", "att_w_in", "att_sinks", "att_w_out")}
    gr["pre_norm"] = [None] * DEPTH
    gr["post_norm"] = [None] * DEPTH
    last = DEPTH - 1
    g, dymix, loss_lanes, gr["post_norm"][last] = _post_loss(cur, ymix, post_norm[last], target, "post_loss")
    for i in reversed(range(DEPTH)):
        j = i // 2
        s = saved[i]
        if i % 2 == 0:
            dact = _matmul(dymix, w_ssm_out[j], "nt", F32, f"ssm_out_dx_{i}")
            gr["ssm_w_out"][j] = _matmul(s["act"], dymix, "tn", F32, f"ssm_out_dw_{i}")
            dxbc, ddt8, dal, dd, dproj, gr["ssm_gate_norm"][j] = _ssd_bwd(
                s["xbc"], s["dtb"], s["acsb"], s["dtr"], s["acs_r"], a_log[j], d_lanes[j], s["hin"], dact, s["y"],
                s["proj"], gate_norm[j], f"ssd_bwd_{i}")
            gr["ssm_a_log"][j] = dal.reshape(SSM_HEADS)
            gr["ssm_d"][j] = dd.reshape(SSM_HEADS)
            l = x.shape[0]
            ddt = jnp.pad(jnp.transpose(ddt8, (2, 0, 1)).reshape(l, SSM_HEADS), ((0, 0), (0, SSM_DT_PAD - SSM_HEADS)))
            dproj, dbias = _dt_bwd(ddt, s["proj"], dt_bias_pad[j:j + 1], dproj, f"dt_bwd_{i}")
            gr["ssm_dt_bias"][j] = dbias[0, :SSM_HEADS]
            dproj, gr["ssm_conv_w"][j], dcb = _conv_bwd(dxbc, s["pre"], s["proj"], conv_w[j], dproj, f"conv_bwd_{i}")
            gr["ssm_conv_b"][j] = dcb[0]
            w_in, key = w_ssm_in[j], "ssm_w_in"
        else:
            dog = _matmul(dymix, w_att_out[j], "nt", F32, f"att_out_dx_{i}")
            gr["att_w_out"][j] = _matmul(s["act"], dymix, "tn", F32, f"att_out_dw_{i}")
            dq, dk, dv, dgate, dsk = _attn_bwd(s["qkv"], s["proj"], sinks[j], s["o"], s["lse"], dog, f"attn_bwd_{i}")
            gr["att_sinks"][j] = dsk[0]
            dproj = _rope_bwd(dq, dk, dv, dgate, tables, f"rope_bwd_{i}")
            w_in, key = w_att_in[j], "att_w_in"
        dh = _matmul(dproj, w_in, "nt", F32, f"in_dx_{i}")
        gr[key][j] = _matmul(s["h"], dproj, "tn", F32, f"in_dw_{i}")
        if i > 0:
            g, dymix, gr["pre_norm"][i], gr["post_norm"][i - 1] = _norm_bwd_chain(
                dh, s["x"], pre_norm[i], g, saved[i - 1]["ymix"], post_norm[i - 1], f"norm_bwd_{i}")
        else:
            g, gr["pre_norm"][i] = _rmsnorm_bwd(dh, s["x"], pre_norm[i], g, F32, f"prenorm_bwd_{i}")
    grads = {k: jnp.stack([v.reshape(v.shape[-1]) if k in ("pre_norm", "post_norm", "ssm_gate_norm") else v for v in vs])
             for k, vs in gr.items()}
    return loss_lanes, g, grads


N_CHIPS = 4
N_DEV = 8
MESH = pl.DeviceIdType.MESH
ANY = pl.BlockSpec(memory_space=pl.ANY)


def _place():
    x, y, c = lax.axis_index("x"), lax.axis_index("y"), lax.axis_index("c")
    return x, y, c, 2 * x + y


def _gather_sems(n):
    return [pltpu.SemaphoreType.DMA((n, N_CHIPS)), pltpu.SemaphoreType.DMA((n, N_CHIPS)), pltpu.SemaphoreType.DMA((n,))]


def _gather_between_chips(ins, outs, send_sems, recv_sems, local_sems, wait):
    n = len(ins)
    _, _, c, s = _place()
    local = [pltpu.make_async_copy(ins[w], outs[w].at[s], local_sems.at[w]) for w in range(n)]

    def remote(w, t):
        return pltpu.make_async_remote_copy(
            src_ref=ins[w].at[c], dst_ref=outs[w].at[s, c], send_sem=send_sems.at[w, t],
            recv_sem=recv_sems.at[w, s], device_id=(t // 2, t % 2, c), device_id_type=MESH)

    def arrival(w, t):
        return pltpu.make_async_remote_copy(
            src_ref=ins[w].at[c], dst_ref=outs[w].at[t, c], send_sem=send_sems.at[w, t],
            recv_sem=recv_sems.at[w, t], device_id=(t // 2, t % 2, c), device_id_type=MESH)

    if not wait:
        for cp in local:
            cp.start()
    for t in range(N_CHIPS):
        @pl.when(s != t)
        def _():
            for w in range(n):
                if wait:
                    remote(w, t).wait_send()
                    arrival(w, t).wait_recv()
                else:
                    remote(w, t).start()
    if wait:
        for cp in local:
            cp.wait()


def _pair_handoff(bufs, name):
    n = len(bufs)

    def body(*refs):
        outs = refs[n:2 * n]
        send_sems, recv_sems = refs[2 * n:]
        x, y, c, s = _place()

        def handed_on(w, t):
            return pltpu.make_async_remote_copy(
                src_ref=outs[w].at[t, c], dst_ref=outs[w].at[t, c], send_sem=send_sems.at[w, t],
                recv_sem=recv_sems.at[w, t], device_id=(x, y, 1 - c), device_id_type=MESH)

        def handed_in(w, t):
            return pltpu.make_async_remote_copy(
                src_ref=outs[w].at[t, 1 - c], dst_ref=outs[w].at[t, 1 - c], send_sem=send_sems.at[w, t],
                recv_sem=recv_sems.at[w, t], device_id=(x, y, 1 - c), device_id_type=MESH)

        for t in range(N_CHIPS):
            @pl.when(s != t)
            def _():
                for w in range(n):
                    handed_on(w, t).start()
        for t in range(N_CHIPS):
            @pl.when(s != t)
            def _():
                for w in range(n):
                    handed_on(w, t).wait_send()
                    handed_in(w, t).wait_recv()

    return pl.pallas_call(
        body, in_specs=[ANY] * n, out_specs=[ANY] * n,
        out_shape=[jax.ShapeDtypeStruct(a.shape, a.dtype) for a in bufs],
        scratch_shapes=[pltpu.SemaphoreType.DMA((n, N_CHIPS)), pltpu.SemaphoreType.DMA((n, N_CHIPS))],
        input_output_aliases={w: w for w in range(n)}, name=name,
    )(*bufs)


def _chip_gather(shards, name):
    n = len(shards)

    def body(*refs):
        ins, outs = refs[:n], refs[n:2 * n]
        _gather_between_chips(ins, outs, *refs[2 * n:], wait=False)
        _gather_between_chips(ins, outs, *refs[2 * n:], wait=True)

    bufs = pl.pallas_call(
        body, in_specs=[ANY] * n, out_specs=[ANY] * n,
        out_shape=[jax.ShapeDtypeStruct((N_CHIPS,) + a.shape, a.dtype) for a in shards],
        scratch_shapes=_gather_sems(n), name=name,
    )(*shards)
    return _pair_handoff(bufs, name + "_handoff")


def _pair_swap(parts, name):
    n = len(parts)

    def body(*refs):
        ins, outs = refs[:n], refs[n:2 * n]
        send_sems, recv_sems = refs[2 * n:]
        x, y, c, _ = _place()
        cps = [pltpu.make_async_remote_copy(
            src_ref=ins[w].at[1 - c], dst_ref=outs[w], send_sem=send_sems.at[w], recv_sem=recv_sems.at[w],
            device_id=(x, y, 1 - c), device_id_type=MESH) for w in range(n)]
        for cp in cps:
            cp.start()
        for cp in cps:
            cp.wait()

    return pl.pallas_call(
        body, in_specs=[ANY] * n, out_specs=[ANY] * n,
        out_shape=[jax.ShapeDtypeStruct(a.shape[1:], a.dtype) for a in parts],
        scratch_shapes=[pltpu.SemaphoreType.DMA((n,)), pltpu.SemaphoreType.DMA((n,))],
        name=name,
    )(*parts)


def _chip_scatter(parts, name):
    n = len(parts)
    rows = [a.shape[0] // N_CHIPS for a in parts]

    def body(*refs):
        ins, outs = refs[:n], refs[n:2 * n]
        send_sems, recv_sems, local_sems = refs[2 * n:]
        _, _, c, s = _place()

        def block(w, t):
            return ins[w].at[pl.ds(t * rows[w], rows[w])]

        local = [pltpu.make_async_copy(block(w, s), outs[w].at[s], local_sems.at[w]) for w in range(n)]
        for cp in local:
            cp.start()

        def remote(w, t):
            return pltpu.make_async_remote_copy(
                src_ref=block(w, t), dst_ref=outs[w].at[s], send_sem=send_sems.at[w, t], recv_sem=recv_sems.at[w, s],
                device_id=(t // 2, t % 2, c), device_id_type=MESH)

        def arrival(w, t):
            return pltpu.make_async_remote_copy(
                src_ref=block(w, t), dst_ref=outs[w].at[t], send_sem=send_sems.at[w, t], recv_sem=recv_sems.at[w, t],
                device_id=(t // 2, t % 2, c), device_id_type=MESH)

        for t in range(N_CHIPS):
            @pl.when(s != t)
            def _():
                for w in range(n):
                    remote(w, t).start()
        for t in range(N_CHIPS):
            @pl.when(s != t)
            def _():
                for w in range(n):
                    remote(w, t).wait_send()
                    arrival(w, t).wait_recv()
        for cp in local:
            cp.wait()

    return pl.pallas_call(
        body, in_specs=[ANY] * n, out_specs=[ANY] * n,
        out_shape=[jax.ShapeDtypeStruct((N_CHIPS, r, a.shape[1]), a.dtype) for a, r in zip(parts, rows)],
        scratch_shapes=[pltpu.SemaphoreType.DMA((n, N_CHIPS)), pltpu.SemaphoreType.DMA((n, N_CHIPS)),
                        pltpu.SemaphoreType.DMA((n,))],
        name=name,
    )(*parts)


def _pair_merge(parts, name):
    n = len(parts)

    def body(*refs):
        ins, outs = refs[:n], refs[n:2 * n]
        send_sems, recv_sems = refs[2 * n:]
        x, y, c, _ = _place()
        cps = [pltpu.make_async_remote_copy(
            src_ref=ins[w], dst_ref=outs[w], send_sem=send_sems.at[w], recv_sem=recv_sems.at[w],
            device_id=(x, y, 1 - c), device_id_type=MESH) for w in range(n)]
        for cp in cps:
            cp.start()
        for cp in cps:
            cp.wait()

    return pl.pallas_call(
        body, in_specs=[ANY] * n, out_specs=[ANY] * n,
        out_shape=[jax.ShapeDtypeStruct(a.shape, a.dtype) for a in parts],
        scratch_shapes=[pltpu.SemaphoreType.DMA((n,)), pltpu.SemaphoreType.DMA((n,))],
        name=name,
    )(*parts)


def _all_gather_small(a, name):
    def body(in_ref, out_ref, send_sems, recv_sems, local_sem):
        x, y, c, _ = _place()
        me = 4 * x + 2 * y + c
        local = pltpu.make_async_copy(in_ref, out_ref.at[me], local_sem)
        local.start()

        def remote(d):
            return pltpu.make_async_remote_copy(
                src_ref=in_ref, dst_ref=out_ref.at[me], send_sem=send_sems.at[d], recv_sem=recv_sems.at[me],
                device_id=(d // 4, (d // 2) % 2, d % 2), device_id_type=MESH)

        def arrival(d):
            return pltpu.make_async_remote_copy(
                src_ref=in_ref, dst_ref=out_ref.at[d], send_sem=send_sems.at[d], recv_sem=recv_sems.at[d],
                device_id=(d // 4, (d // 2) % 2, d % 2), device_id_type=MESH)

        for d in range(N_DEV):
            @pl.when(me != d)
            def _():
                remote(d).start()
        for d in range(N_DEV):
            @pl.when(me != d)
            def _():
                remote(d).wait_send()
                arrival(d).wait_recv()
        local.wait()

    return pl.pallas_call(
        body, in_specs=[ANY], out_specs=ANY, out_shape=jax.ShapeDtypeStruct((N_DEV,) + a.shape, a.dtype),
        scratch_shapes=[pltpu.SemaphoreType.DMA((N_DEV,)), pltpu.SemaphoreType.DMA((N_DEV,)), pltpu.SemaphoreType.DMA],
        name=name,
    )(a)


def _reduce_tile(rows):
    return _pick(rows, (256, 16))


def _pair_add(full, other, layer, name):
    _, rows, cols = full.shape
    tr = _reduce_tile(rows)

    def body(layer_ref, a_ref, b_ref, o_ref):
        o_ref[...] = (a_ref[0] + b_ref[...]).astype(o_ref.dtype)

    return pl.pallas_call(
        body,
        grid_spec=pltpu.PrefetchScalarGridSpec(
            num_scalar_prefetch=1, grid=(rows // tr,),
            in_specs=[pl.BlockSpec((1, tr, cols), lambda i, lr: (lr[0], i, 0)), pl.BlockSpec((tr, cols), lambda i, lr: (i, 0))],
            out_specs=pl.BlockSpec((tr, cols), lambda i, lr: (i, 0))),
        out_shape=jax.ShapeDtypeStruct((rows, cols), BF16), compiler_params=_params("parallel"), name=name,
    )(layer, full, other)


def _sum_slots(a, name):
    n, rows, cols = a.shape
    tr = _reduce_tile(rows)

    def body(a_ref, o_ref):
        acc = a_ref[0].astype(F32)
        for k in range(1, n):
            acc = acc + a_ref[k].astype(F32)
        o_ref[...] = acc

    return pl.pallas_call(
        body, grid=(rows // tr,), in_specs=[pl.BlockSpec((n, tr, cols), lambda i: (0, i, 0))],
        out_specs=pl.BlockSpec((tr, cols), lambda i: (i, 0)),
        out_shape=jax.ShapeDtypeStruct((rows, cols), F32), compiler_params=_params("parallel"), name=name,
    )(a)


def _adamw(w, g, m, v, name):
    rows, cols = w.shape
    tr = _pick(rows, (256, 8))

    def body(w_ref, g_ref, m_ref, v_ref, d_ref, nm_ref, nv_ref):
        gv = g_ref[...]
        mn = ADAM_B1 * m_ref[...] + (1.0 - ADAM_B1) * gv
        vn = ADAM_B2 * v_ref[...] + (1.0 - ADAM_B2) * jnp.square(gv)
        m_hat = mn / (1.0 - ADAM_B1 ** ADAM_STEP)
        v_hat = vn / (1.0 - ADAM_B2 ** ADAM_STEP)
        d_ref[...] = -ADAM_LR * (m_hat / (jnp.sqrt(v_hat) + ADAM_EPS) + ADAM_WD * w_ref[...])
        nm_ref[...] = mn
        nv_ref[...] = vn

    blk = pl.BlockSpec((tr, cols), lambda i: (i, 0))
    return pl.pallas_call(
        body, grid=(rows // tr,), in_specs=[blk] * 4, out_specs=[blk] * 3,
        out_shape=[jax.ShapeDtypeStruct((rows, cols), F32)] * 3, compiler_params=_params("parallel"), name=name,
    )(w, g, m, v)


BIG = ("ssm_w_in", "ssm_w_out", "att_w_in", "att_w_out")
SHARDED = BIG + ("ssm_conv_w",)
SMALL = ("pre_norm", "post_norm", "ssm_conv_b", "ssm_dt_bias", "ssm_a_log", "ssm_d", "ssm_gate_norm", "att_sinks")
WEIGHTS = ("pre_norm", "post_norm", "ssm_w_in", "ssm_conv_w", "ssm_conv_b", "ssm_dt_bias", "ssm_a_log", "ssm_d",
           "ssm_gate_norm", "ssm_w_out", "att_w_in", "att_sinks", "att_w_out")


def _halves(a):
    return a.reshape(2, a.shape[0] // 2, a.shape[1])


def _layer_shards(j, ssm_w_in, ssm_w_out, att_w_in, att_w_out, ssm_conv_w):
    return [_halves(ssm_w_in[j].astype(BF16)), _halves(ssm_w_out[j].astype(BF16)), _halves(att_w_in[j].astype(BF16)),
            _halves(att_w_out[j].astype(BF16)), _halves(ssm_conv_w[j])]


SHARD_KEYS = ("ssm_w_in", "ssm_w_out", "att_w_in", "att_w_out", "ssm_conv_w")


def _whole_weights(keys, gathered):
    out = {}
    for k, g in zip(keys, gathered):
        g = g.reshape((N_CHIPS, 2 * g.shape[2], g.shape[3]))
        if k in ("ssm_w_out", "att_w_out"):
            out[k] = g.reshape(N_CHIPS * g.shape[1], g.shape[2])
        else:
            out[k] = jnp.transpose(g, (1, 0, 2)).reshape(g.shape[1], N_CHIPS * g.shape[2])
    if "ssm_w_in" in out:
        out["ssm_w_in"] = jnp.pad(out["ssm_w_in"], ((0, 0), (0, SSM_IN_PAD - SSM_IN_DIM)))
    return out


def _cols_by_chip(g):
    two, rows, cols = g.shape
    return jnp.transpose(g.reshape(two, rows, N_CHIPS, cols // N_CHIPS), (0, 2, 1, 3)).reshape(two, N_CHIPS * rows, cols // N_CHIPS)


def _pack_small(tree, keys):
    flat = jnp.concatenate([tree[k].reshape(-1) for k in keys])
    rows = -(-flat.shape[0] // (8 * LANES)) * 8
    return jnp.pad(flat, (0, rows * LANES - flat.shape[0])).reshape(rows, LANES)


def _unpack_small(packed, shapes, keys):
    flat = packed.reshape(-1)
    out, at = {}, 0
    for k in keys:
        n = 1
        for dim in shapes[k]:
            n *= dim
        out[k] = flat[at:at + n].reshape(shapes[k])
        at += n
    return out


def kernel(x, positions, pre_norm, post_norm, ssm_w_in, ssm_conv_w, ssm_conv_b, ssm_dt_bias, ssm_a_log, ssm_d, ssm_gate_norm, ssm_w_out, att_w_in, att_sinks, att_w_out, loss_target, m_pre_norm, m_post_norm, m_ssm_w_in, m_ssm_conv_w, m_ssm_conv_b, m_ssm_dt_bias, m_ssm_a_log, m_ssm_d, m_ssm_gate_norm, m_ssm_w_out, m_att_w_in, m_att_sinks, m_att_w_out, v_pre_norm, v_post_norm, v_ssm_w_in, v_ssm_conv_w, v_ssm_conv_b, v_ssm_dt_bias, v_ssm_a_log, v_ssm_d, v_ssm_gate_norm, v_ssm_w_out, v_att_w_in, v_att_sinks, v_att_w_out):
    w = dict(pre_norm=pre_norm, post_norm=post_norm, ssm_w_in=ssm_w_in, ssm_conv_w=ssm_conv_w, ssm_conv_b=ssm_conv_b,
             ssm_dt_bias=ssm_dt_bias, ssm_a_log=ssm_a_log, ssm_d=ssm_d, ssm_gate_norm=ssm_gate_norm, ssm_w_out=ssm_w_out,
             att_w_in=att_w_in, att_sinks=att_sinks, att_w_out=att_w_out)
    m = dict(pre_norm=m_pre_norm, post_norm=m_post_norm, ssm_w_in=m_ssm_w_in, ssm_conv_w=m_ssm_conv_w, ssm_conv_b=m_ssm_conv_b,
             ssm_dt_bias=m_ssm_dt_bias, ssm_a_log=m_ssm_a_log, ssm_d=m_ssm_d, ssm_gate_norm=m_ssm_gate_norm,
             ssm_w_out=m_ssm_w_out, att_w_in=m_att_w_in, att_sinks=m_att_sinks, att_w_out=m_att_w_out)
    v = dict(pre_norm=v_pre_norm, post_norm=v_post_norm, ssm_w_in=v_ssm_w_in, ssm_conv_w=v_ssm_conv_w, ssm_conv_b=v_ssm_conv_b,
             ssm_dt_bias=v_ssm_dt_bias, ssm_a_log=v_ssm_a_log, ssm_d=v_ssm_d, ssm_gate_norm=v_ssm_gate_norm,
             ssm_w_out=v_ssm_w_out, att_w_in=v_att_w_in, att_sinks=v_att_sinks, att_w_out=v_att_w_out)
    c = lax.axis_index("c")
    chip = 2 * lax.axis_index("x") + lax.axis_index("y")

    sharded = (ssm_w_in, ssm_w_out, att_w_in, att_w_out, ssm_conv_w)
    own = [dict(zip(SHARD_KEYS, _layer_shards(j, *sharded))) for j in range(2)]
    now_keys = ("ssm_w_in", "ssm_conv_w")
    later_keys = ("ssm_w_out", "att_w_in", "att_w_out")
    first_in = _whole_weights(now_keys, _chip_gather([own[0][k] for k in now_keys], "gather_weights_0"))

    def in_proj_with_first_pair(matmul):
        proj, *arrived = matmul(ride=[own[0][k] for k in later_keys])
        return proj, _whole_weights(later_keys, _pair_handoff(arrived, "gather_weights_0_rest_handoff"))

    def scan_with_second_pair(scan):
        y, act, hin, *arrived = scan(ride=[own[1][k] for k in SHARD_KEYS])
        return y, act, hin, _whole_weights(SHARD_KEYS, _pair_handoff(arrived, "gather_weights_1_handoff"))

    loss_lanes, grad_x, gr = _local_step(
        x[0], positions[0], pre_norm, post_norm, ssm_conv_b, ssm_dt_bias, ssm_a_log, ssm_d, ssm_gate_norm, att_sinks,
        loss_target[0], first_in, in_proj_with_first_pair, scan_with_second_pair)
    loss = lax.psum(0.5 * jnp.sum(loss_lanes) / D_MODEL, ("x", "y", "c"))

    parts = [_cols_by_chip(gr["ssm_w_in"][:, :, :SSM_IN_DIM]), gr["ssm_w_out"], _cols_by_chip(gr["att_w_in"]),
             gr["att_w_out"]]
    from_sibling = _pair_swap(parts, "reduce_pair_swap")
    layer = jnp.reshape(c, (1,)).astype(jnp.int32)
    chip_sums = [_pair_add(p, o, layer, f"reduce_pair_add_{k}") for k, (p, o) in enumerate(zip(parts, from_sibling))]
    by_chip = _chip_scatter(chip_sums, "reduce_chip_scatter")
    mine = [_sum_slots(a, f"reduce_chip_sum_{k}") for k, a in enumerate(by_chip)]
    theirs = _pair_merge(mine, "reduce_pair_merge")
    grads = {k: jnp.stack([jnp.where(c == 0, a, b), jnp.where(c == 0, b, a)]).reshape(w[k].shape)
             for k, a, b in zip(BIG, mine, theirs)}

    small_keys = SMALL + ("ssm_conv_w",)
    small_shapes = {k: w[k].shape for k in SMALL}
    small_shapes["ssm_conv_w"] = gr["ssm_conv_w"].shape
    small_sum = _sum_slots(_all_gather_small(_pack_small(gr, small_keys), "reduce_small_gather"), "reduce_small_sum")
    grads.update(_unpack_small(small_sum, small_shapes, small_keys))
    conv_cols = ssm_conv_w.shape[2]
    grads["ssm_conv_w"] = lax.dynamic_slice_in_dim(grads["ssm_conv_w"], chip * conv_cols, conv_cols, axis=2)

    delta, new_m, new_v = {}, {}, {}
    for k in SHARDED:
        shp = w[k].shape
        two_d = (shp[0] * shp[1], shp[2])
        d_, m_, v_ = _adamw(w[k].reshape(two_d), grads[k].reshape(two_d), m[k].reshape(two_d), v[k].reshape(two_d),
                            f"adamw_{k}")
        delta[k], new_m[k], new_v[k] = d_.reshape(shp), m_.reshape(shp), v_.reshape(shp)
    d_, m_, v_ = _adamw(_pack_small(w, SMALL), _pack_small(grads, SMALL), _pack_small(m, SMALL), _pack_small(v, SMALL),
                        "adamw_small")
    delta.update(_unpack_small(d_, small_shapes, SMALL))
    new_m.update(_unpack_small(m_, small_shapes, SMALL))
    new_v.update(_unpack_small(v_, small_shapes, SMALL))

    return (loss, grad_x[None], *[grads[k] for k in WEIGHTS], *[delta[k] for k in WEIGHTS],
            *[new_m[k] for k in WEIGHTS], *[new_v[k] for k in WEIGHTS])
```

```python
import functools

import jax
import jax.numpy as jnp
from jax import lax
from jax.experimental import pallas as pl
from jax.experimental.pallas import tpu as pltpu

F32 = jnp.float32
BF16 = jnp.bfloat16
EPS = 1e-6
NEG_INF = float("-inf")

D_MODEL = 1024
DEPTH = 4
SSM_D_INNER = 2048
SSM_HEAD_DIM = 64
SSM_HEADS = 32
SSM_GROUPS = 8
SSM_HPG = 4
SSM_STATE = 128
SSM_CONV = 4
SSM_CHUNK = 128
SSM_BC_DIM = 1024
SSM_CONV_DIM = 4096
SSM_IN_DIM = 6176
SSM_IN_PAD = 6272
SSM_DT_PAD = 128
ATT_HEAD_DIM = 64
ATT_Q_HEADS = 16
ATT_KV_HEADS = 4
ATT_GQA = 4
ATT_WIDTH = 1024
ATT_KV_WIDTH = 256
ATT_IN_DIM = 2560
ATT_QKV = ATT_WIDTH + 2 * ATT_KV_WIDTH
ATT_BLOCK = 128
ROPE_THETA = 500000.0
ROPE_DIM = 16
ROPE_HALF = 8
Q_SCALE = ATT_HEAD_DIM ** -0.5

ADAM_LR = 0.001
ADAM_B1 = 0.9
ADAM_B2 = 0.999
ADAM_EPS = 1e-08
ADAM_WD = 0.01
ADAM_STEP = 10

VMEM_LIMIT_BYTES = 48 * 1024 * 1024
NT_DIMS = (((1,), (1,)), ((), ()))
TN_DIMS = (((0,), (0,)), ((), ()))


def _params(*sem):
    return pltpu.CompilerParams(dimension_semantics=sem, vmem_limit_bytes=VMEM_LIMIT_BYTES)


def _pick(n, cands):
    for c in cands:
        if n % c == 0:
            return c
    return n


def _sigmoid(v):
    return 0.5 * jnp.tanh(0.5 * v) + 0.5


def _bdot(a, b):
    return jnp.dot(a.astype(BF16), b.astype(BF16), preferred_element_type=F32)


def _bdot_nt(a, b):
    return lax.dot_general(a.astype(BF16), b.astype(BF16), NT_DIMS, preferred_element_type=F32)


def _bdot_tn(a, b):
    return lax.dot_general(a.astype(BF16), b.astype(BF16), TN_DIMS, preferred_element_type=F32)


MATMUL_VMEM_BUDGET = 36 * 1024 * 1024


def _matmul_tiles(m, n, k, out_bytes, reduce_rows):
    best = None
    whole = [k] if (not reduce_rows or k <= 2048) else []
    for tk in whole + [c for c in (4096, 2048, 1024, 896, 512) if k % c == 0 and c < k]:
        for tm in (c for c in (2048, 1024, 512, 256) if m % c == 0):
            for tn in (c for c in (n, 1280, 1024, 896, 640, 512) if n % c == 0):
                acc = tm * tn * 4 if tk < k else 0
                need = 2 * (2 * tk * (tm + tn) + tm * tn * out_bytes) + acc
                if need <= MATMUL_VMEM_BUDGET and (best is None or tm * tn * min(tk, 2048) > best[0]):
                    best = (tm * tn * min(tk, 2048), tm, tn, tk)
        if best is not None and not reduce_rows:
            break
    return best[1:]


def _matmul(a, b, mode, out_dtype, name, ride=()):
    if mode == "nn":
        (m, k), n = a.shape, b.shape[1]
    elif mode == "nt":
        (m, k), n = a.shape, b.shape[0]
    else:
        (k, m), n = a.shape, b.shape[1]
    tm, tn, tk = _matmul_tiles(m, n, k, jnp.dtype(out_dtype).itemsize, mode == "tn")
    nk = k // tk
    steps = (n // tn, m // tm, nk)
    dims = {"nn": (((1,), (0,)), ((), ())), "nt": NT_DIMS, "tn": TN_DIMS}[mode]
    n_ride = len(ride)

    def body(*refs):
        a_ref, b_ref = refs[:2]
        ride_in = refs[2:2 + n_ride]
        o_ref = refs[2 + n_ride]
        ride_out = refs[3 + n_ride:3 + 2 * n_ride]
        acc_ref = refs[3 + 2 * n_ride]
        ride_sems = refs[4 + 2 * n_ride:]
        kk = pl.program_id(2)
        at = [pl.program_id(d) for d in range(3)]
        if n_ride:
            @pl.when((at[0] == 0) & (at[1] == 0) & (at[2] == 0))
            def _():
                _gather_between_chips(ride_in, ride_out, *ride_sems, wait=False)

        part = lax.dot_general(a_ref[...], b_ref[...], dims, preferred_element_type=F32)
        if nk == 1:
            o_ref[...] = part.astype(o_ref.dtype)
        else:
            @pl.when(kk == 0)
            def _():
                acc_ref[...] = part

            @pl.when(kk > 0)
            def _():
                acc_ref[...] += part

            @pl.when(kk == nk - 1)
            def _():
                o_ref[...] = acc_ref[...].astype(o_ref.dtype)

        if n_ride:
            @pl.when((at[0] == steps[0] - 1) & (at[1] == steps[1] - 1) & (at[2] == steps[2] - 1))
            def _():
                _gather_between_chips(ride_in, ride_out, *ride_sems, wait=True)

    if mode == "nn":
        a_spec = pl.BlockSpec((tm, tk), lambda j, i, kk: (i, kk))
        b_spec = pl.BlockSpec((tk, tn), lambda j, i, kk: (kk, j))
    elif mode == "nt":
        a_spec = pl.BlockSpec((tm, tk), lambda j, i, kk: (i, kk))
        b_spec = pl.BlockSpec((tn, tk), lambda j, i, kk: (j, kk))
    else:
        a_spec = pl.BlockSpec((tk, tm), lambda j, i, kk: (kk, i))
        b_spec = pl.BlockSpec((tk, tn), lambda j, i, kk: (kk, j))
    out = pl.pallas_call(
        body, grid=steps, in_specs=[a_spec, b_spec] + [ANY] * n_ride,
        out_specs=[pl.BlockSpec((tm, tn), lambda j, i, kk: (i, j))] + [ANY] * n_ride,
        out_shape=[jax.ShapeDtypeStruct((m, n), out_dtype)]
        + [jax.ShapeDtypeStruct((N_CHIPS,) + r.shape, r.dtype) for r in ride],
        scratch_shapes=[pltpu.VMEM((tm, tn), F32)] + (_gather_sems(n_ride) if n_ride else []),
        compiler_params=_params(*(["arbitrary"] * 3 if n_ride else ["parallel", "parallel", "arbitrary"])), name=name,
    )(a, b, *ride)
    return out if n_ride else out[0]


def _row_tile(l):
    return _pick(l, (512, 256, 128))


def _rmsnorm_fwd(x, w, name):
    l, d = x.shape
    tl = _row_tile(l)

    def body(x_ref, w_ref, o_ref):
        xv = x_ref[...]
        r = lax.rsqrt(jnp.mean(xv * xv, axis=-1, keepdims=True) + EPS)
        o_ref[...] = (xv * r * w_ref[...]).astype(o_ref.dtype)

    return pl.pallas_call(
        body, grid=(l // tl,),
        in_specs=[pl.BlockSpec((tl, d), lambda i: (i, 0)), pl.BlockSpec((1, d), lambda i: (0, 0))],
        out_specs=pl.BlockSpec((tl, d), lambda i: (i, 0)),
        out_shape=jax.ShapeDtypeStruct((l, d), BF16), compiler_params=_params("parallel"), name=name,
    )(x, w.reshape(1, d))


def _post_fwd(x, y, w, w_next, name):
    l, d = x.shape
    tl = _row_tile(l)

    def body(x_ref, y_ref, w_ref, wn_ref, o_ref, h_ref):
        yv = y_ref[...]
        r = lax.rsqrt(jnp.mean(yv * yv, axis=-1, keepdims=True) + EPS)
        out = x_ref[...] + yv * r * w_ref[...]
        o_ref[...] = out
        rn = lax.rsqrt(jnp.mean(out * out, axis=-1, keepdims=True) + EPS)
        h_ref[...] = (out * rn * wn_ref[...]).astype(h_ref.dtype)

    row = pl.BlockSpec((tl, d), lambda i: (i, 0))
    vec = pl.BlockSpec((1, d), lambda i: (0, 0))
    return pl.pallas_call(
        body, grid=(l // tl,), in_specs=[row, row, vec, vec], out_specs=[row, row],
        out_shape=[jax.ShapeDtypeStruct((l, d), F32), jax.ShapeDtypeStruct((l, d), BF16)],
        compiler_params=_params("parallel"), name=name,
    )(x, y, w.reshape(1, d), w_next.reshape(1, d))


def _post_loss(x, y, w, t, name):
    l, d = x.shape
    tl = _row_tile(l)
    nt = l // tl

    def body(x_ref, y_ref, w_ref, t_ref, g_ref, dy_ref, ls_ref, dw_ref, acc_ref):
        i = pl.program_id(0)

        @pl.when(i == 0)
        def _():
            ls_ref[...] = jnp.zeros_like(ls_ref)
            acc_ref[...] = jnp.zeros_like(acc_ref)

        yv = y_ref[...]
        r = lax.rsqrt(jnp.mean(yv * yv, axis=-1, keepdims=True) + EPS)
        nrm = yv * r
        e = x_ref[...] + nrm * w_ref[...] - t_ref[...]
        gv = e * (1.0 / d)
        g_ref[...] = gv
        ls_ref[...] += jnp.sum((e * e).reshape(tl // 8, 8, d), axis=0)
        gw = gv * w_ref[...]
        dy_ref[...] = (r * (gw - nrm * jnp.mean(gw * nrm, axis=-1, keepdims=True))).astype(dy_ref.dtype)
        acc_ref[...] += jnp.sum((gv * nrm).reshape(tl // 8, 8, d), axis=0)

        @pl.when(i == nt - 1)
        def _():
            dw_ref[...] = jnp.sum(acc_ref[...], axis=0, keepdims=True)

    row = pl.BlockSpec((tl, d), lambda i: (i, 0))
    vec = pl.BlockSpec((1, d), lambda i: (0, 0))
    return pl.pallas_call(
        body, grid=(nt,), in_specs=[row, row, vec, row],
        out_specs=[row, row, pl.BlockSpec((8, d), lambda i: (0, 0)), vec],
        out_shape=[jax.ShapeDtypeStruct((l, d), F32), jax.ShapeDtypeStruct((l, d), BF16),
                   jax.ShapeDtypeStruct((8, d), F32), jax.ShapeDtypeStruct((1, d), F32)],
        scratch_shapes=[pltpu.VMEM((8, d), F32)], compiler_params=_params("arbitrary"), name=name,
    )(x, y, w.reshape(1, d), t)


def _norm_bwd_chain(dh, x, w_pre, resid, y_prev, w_post_prev, name):
    l, d = x.shape
    tl = _row_tile(l)
    nt = l // tl

    def body(dh_ref, x_ref, wp_ref, r_ref, y_ref, wq_ref, g_ref, dy_ref, dwp_ref, dwq_ref, accp_ref, accq_ref):
        i = pl.program_id(0)

        @pl.when(i == 0)
        def _():
            accp_ref[...] = jnp.zeros_like(accp_ref)
            accq_ref[...] = jnp.zeros_like(accq_ref)

        xv = x_ref[...]
        dhv = dh_ref[...]
        rx = lax.rsqrt(jnp.mean(xv * xv, axis=-1, keepdims=True) + EPS)
        nx = xv * rx
        gw = dhv * wp_ref[...]
        gv = rx * (gw - nx * jnp.mean(gw * nx, axis=-1, keepdims=True)) + r_ref[...]
        g_ref[...] = gv
        accp_ref[...] += jnp.sum((dhv * nx).reshape(tl // 8, 8, d), axis=0)
        yv = y_ref[...]
        ry = lax.rsqrt(jnp.mean(yv * yv, axis=-1, keepdims=True) + EPS)
        ny = yv * ry
        gq = gv * wq_ref[...]
        dy_ref[...] = (ry * (gq - ny * jnp.mean(gq * ny, axis=-1, keepdims=True))).astype(dy_ref.dtype)
        accq_ref[...] += jnp.sum((gv * ny).reshape(tl // 8, 8, d), axis=0)

        @pl.when(i == nt - 1)
        def _():
            dwp_ref[...] = jnp.sum(accp_ref[...], axis=0, keepdims=True)
            dwq_ref[...] = jnp.sum(accq_ref[...], axis=0, keepdims=True)

    row = pl.BlockSpec((tl, d), lambda i: (i, 0))
    vec = pl.BlockSpec((1, d), lambda i: (0, 0))
    return pl.pallas_call(
        body, grid=(nt,), in_specs=[row, row, vec, row, row, vec], out_specs=[row, row, vec, vec],
        out_shape=[jax.ShapeDtypeStruct((l, d), F32), jax.ShapeDtypeStruct((l, d), BF16),
                   jax.ShapeDtypeStruct((1, d), F32), jax.ShapeDtypeStruct((1, d), F32)],
        scratch_shapes=[pltpu.VMEM((8, d), F32), pltpu.VMEM((8, d), F32)],
        compiler_params=_params("arbitrary"), name=name,
    )(dh, x, w_pre.reshape(1, d), resid, y_prev, w_post_prev.reshape(1, d))


def _rmsnorm_bwd(g, y, w, resid, out_dtype, name):
    l, d = y.shape
    tl = _row_tile(l)
    nt = l // tl
    has_resid = resid is not None

    def body(*refs):
        if has_resid:
            g_ref, y_ref, w_ref, r_ref, dy_ref, dw_ref, acc_ref = refs
        else:
            g_ref, y_ref, w_ref, dy_ref, dw_ref, acc_ref = refs
        i = pl.program_id(0)

        @pl.when(i == 0)
        def _():
            acc_ref[...] = jnp.zeros_like(acc_ref)

        yv = y_ref[...]
        gv = g_ref[...].astype(F32)
        r = lax.rsqrt(jnp.mean(yv * yv, axis=-1, keepdims=True) + EPS)
        nrm = yv * r
        gw = gv * w_ref[...]
        dy = r * (gw - nrm * jnp.mean(gw * nrm, axis=-1, keepdims=True))
        if has_resid:
            dy = dy + r_ref[...]
        dy_ref[...] = dy.astype(dy_ref.dtype)
        acc_ref[...] += jnp.sum((gv * nrm).reshape(tl // 8, 8, d), axis=0)

        @pl.when(i == nt - 1)
        def _():
            dw_ref[...] = jnp.sum(acc_ref[...], axis=0, keepdims=True)

    row = pl.BlockSpec((tl, d), lambda i: (i, 0))
    vec = pl.BlockSpec((1, d), lambda i: (0, 0))
    ins = [g, y, w.reshape(1, d)] + ([resid] if has_resid else [])
    return pl.pallas_call(
        body, grid=(nt,), in_specs=[row, row, vec] + ([row] if has_resid else []),
        out_specs=[row, vec],
        out_shape=[jax.ShapeDtypeStruct((l, d), out_dtype), jax.ShapeDtypeStruct((1, d), F32)],
        scratch_shapes=[pltpu.VMEM((8, d), F32)], compiler_params=_params("arbitrary"), name=name,
    )(*ins)


CONV_COLS = 512
HALO = 8
HALO16 = 16
CONV_SUB_ROWS = 64
CONV_SUB_COLS = 256


def _conv_rows(l):
    return _pick(l, (1024, 512, 256, 128))


def _conv_fwd(proj, cw, cb, name):
    l = proj.shape[0]
    tl = _conv_rows(l)
    off = SSM_D_INNER // CONV_COLS

    def body(u_ref, halo_ref, w_ref, b_ref, pre_ref, act_ref, ext_ref):
        i = pl.program_id(1)
        ext_ref[0:HALO, :] = jnp.where(i > 0, halo_ref[...], 0.0)
        ext_ref[HALO:HALO + tl, :] = u_ref[...]
        for r0 in range(0, tl, CONV_SUB_ROWS):
            for c0 in range(0, CONV_COLS, CONV_SUB_COLS):
                cs = slice(c0, c0 + CONV_SUB_COLS)
                ext = ext_ref[r0:r0 + CONV_SUB_ROWS + HALO, cs]
                acc = b_ref[:, cs] + w_ref[SSM_CONV - 1:SSM_CONV, cs] * ext[HALO:]
                for k in range(SSM_CONV - 1):
                    acc = acc + w_ref[k:k + 1, cs] * pltpu.roll(ext, SSM_CONV - 1 - k, 0)[HALO:]
                pre_ref[r0:r0 + CONV_SUB_ROWS, cs] = acc.astype(pre_ref.dtype)
                act_ref[r0:r0 + CONV_SUB_ROWS, cs] = (acc * _sigmoid(acc)).astype(act_ref.dtype)

    hb = tl // HALO
    out = pl.BlockSpec((tl, CONV_COLS), lambda j, i: (i, j))
    return pl.pallas_call(
        body, grid=(SSM_CONV_DIM // CONV_COLS, l // tl),
        in_specs=[pl.BlockSpec((tl, CONV_COLS), lambda j, i: (i, off + j)),
                  pl.BlockSpec((HALO, CONV_COLS), lambda j, i: (jnp.maximum(i * hb - 1, 0), off + j)),
                  pl.BlockSpec((SSM_CONV, CONV_COLS), lambda j, i: (0, j)),
                  pl.BlockSpec((1, CONV_COLS), lambda j, i: (0, j))],
        out_specs=[out, out],
        out_shape=[jax.ShapeDtypeStruct((l, SSM_CONV_DIM), BF16)] * 2,
        scratch_shapes=[pltpu.VMEM((tl + HALO, CONV_COLS), F32)],
        compiler_params=_params("parallel", "arbitrary"), name=name,
    )(proj, proj, cw, cb.reshape(1, SSM_CONV_DIM))


def _conv_bwd(dact, pre, proj, cw, dproj, name):
    l, width = dact.shape
    tl = _conv_rows(l)
    nt = l // tl
    pre_off = 0
    u_off = SSM_D_INNER // CONV_COLS
    hb = tl // HALO
    hb16 = tl // HALO16
    last_hb16 = l // HALO16 - 1

    def body(da_ref, da_h_ref, p_ref, p_h_ref, u_ref, u_h_ref, w_ref, _, du_ref, dw_ref, db_ref, ext_ref, uext_ref):
        i = pl.program_id(1)

        @pl.when(i == 0)
        def _():
            dw_ref[...] = jnp.zeros_like(dw_ref)
            db_ref[...] = jnp.zeros_like(db_ref)

        def dpre_of(da, p):
            s = _sigmoid(p)
            return da * (s * (1.0 + p * (1.0 - s)))

        ext_ref[0:tl, :] = dpre_of(da_ref[...].astype(F32), p_ref[...].astype(F32))
        ext_ref[tl:tl + HALO, :] = jnp.where(
            i < nt - 1, dpre_of(da_h_ref[...].astype(F32)[:HALO], p_h_ref[...].astype(F32)[:HALO]), 0.0)
        uext_ref[0:HALO, :] = jnp.where(i > 0, u_h_ref[...], 0.0)
        uext_ref[HALO:HALO + tl, :] = u_ref[...]
        sub = CONV_SUB_ROWS
        for c0 in range(0, CONV_COLS, CONV_SUB_COLS):
            cs = slice(c0, c0 + CONV_SUB_COLS)
            dws = [jnp.zeros((1, CONV_SUB_COLS), F32) for _ in range(SSM_CONV)]
            dbs = jnp.zeros((1, CONV_SUB_COLS), F32)
            for r0 in range(0, tl, sub):
                dext = ext_ref[r0:r0 + sub + HALO, cs]
                uext = uext_ref[r0:r0 + sub + HALO, cs]
                dp = dext[:sub]
                du = w_ref[SSM_CONV - 1:SSM_CONV, cs] * dp
                dws[SSM_CONV - 1] = dws[SSM_CONV - 1] + jnp.sum(dp * uext[HALO:], axis=0, keepdims=True)
                for k in range(SSM_CONV - 1):
                    j = SSM_CONV - 1 - k
                    du = du + w_ref[k:k + 1, cs] * pltpu.roll(dext, sub + HALO - j, 0)[:sub]
                    dws[k] = dws[k] + jnp.sum(dp * pltpu.roll(uext, j, 0)[HALO:], axis=0, keepdims=True)
                dbs = dbs + jnp.sum(dp, axis=0, keepdims=True)
                du_ref[r0:r0 + sub, cs] = du.astype(du_ref.dtype)
            for k in range(SSM_CONV):
                dw_ref[k:k + 1, cs] += dws[k]
            db_ref[:, cs] += dbs

    return pl.pallas_call(
        body, grid=(width // CONV_COLS, nt),
        in_specs=[pl.BlockSpec((tl, CONV_COLS), lambda j, i: (i, j)),
                  pl.BlockSpec((HALO16, CONV_COLS), lambda j, i: (jnp.minimum((i + 1) * hb16, last_hb16), j)),
                  pl.BlockSpec((tl, CONV_COLS), lambda j, i: (i, pre_off + j)),
                  pl.BlockSpec((HALO16, CONV_COLS), lambda j, i: (jnp.minimum((i + 1) * hb16, last_hb16), pre_off + j)),
                  pl.BlockSpec((tl, CONV_COLS), lambda j, i: (i, u_off + j)),
                  pl.BlockSpec((HALO, CONV_COLS), lambda j, i: (jnp.maximum(i * hb - 1, 0), u_off + j)),
                  pl.BlockSpec((SSM_CONV, CONV_COLS), lambda j, i: (0, pre_off + j)),
                  pl.BlockSpec(memory_space=pl.ANY)],
        out_specs=[pl.BlockSpec((tl, CONV_COLS), lambda j, i: (i, u_off + j)),
                   pl.BlockSpec((SSM_CONV, CONV_COLS), lambda j, i: (0, j)),
                   pl.BlockSpec((1, CONV_COLS), lambda j, i: (0, j))],
        out_shape=[jax.ShapeDtypeStruct(dproj.shape, dproj.dtype), jax.ShapeDtypeStruct((SSM_CONV, width), F32),
                   jax.ShapeDtypeStruct((1, width), F32)],
        scratch_shapes=[pltpu.VMEM((tl + HALO, CONV_COLS), F32), pltpu.VMEM((tl + HALO, CONV_COLS), F32)],
        input_output_aliases={7: 0}, compiler_params=_params("parallel", "arbitrary"), name=name,
    )(dact, dact, pre, pre, proj, proj, cw, dproj)


DT_COL_BLOCK = (SSM_D_INNER + SSM_CONV_DIM) // SSM_DT_PAD


def _split3(v):
    hi = v.astype(BF16)
    rest = v - hi.astype(F32)
    mid = rest.astype(BF16)
    lo = (rest - mid.astype(F32)).astype(BF16)
    return hi, mid, lo


def _ssd_prep(proj, bias, alog_lanes, name):
    l = proj.shape[0]
    nc = l // SSM_CHUNK
    head_dim_log2 = SSM_HEAD_DIM.bit_length() - 1

    def body(p_ref, b_ref, al_ref, dtb_ref, acsb_ref, dtr_ref, acsr_ref):
        v = p_ref[...] + b_ref[...]
        dt = jnp.maximum(v, 0.0) + jnp.log1p(jnp.exp(-jnp.abs(v)))
        head_of_lane = lax.shift_right_logical(lax.broadcasted_iota(jnp.int32, (SSM_DT_PAD, SSM_D_INNER), 1), head_dim_log2)
        spread = (head_of_lane == lax.broadcasted_iota(jnp.int32, (SSM_DT_PAD, SSM_D_INNER), 0)).astype(BF16)
        dtb = sum(jnp.dot(piece, spread, preferred_element_type=F32) for piece in _split3(dt)[:2])
        dtb_ref[...] = dtb
        ri = lax.broadcasted_iota(jnp.int32, (SSM_CHUNK, SSM_CHUNK), 0)
        cj = lax.broadcasted_iota(jnp.int32, (SSM_CHUNK, SSM_CHUNK), 1)
        tri = (ri >= cj).astype(BF16)
        acsb = sum(jnp.dot(tri, piece, preferred_element_type=F32) for piece in _split3(dtb * (-jnp.exp(al_ref[...]))))
        acsb_ref[...] = acsb
        gp = SSM_HPG * SSM_HEAD_DIM
        lane = lax.broadcasted_iota(jnp.int32, (SSM_HPG, gp), 1)
        pick = (lane == lax.broadcasted_iota(jnp.int32, (SSM_HPG, gp), 0) * SSM_HEAD_DIM).astype(BF16)
        for g in range(SSM_GROUPS):
            cols = slice(g * gp, (g + 1) * gp)
            dtr_ref[g] = sum(lax.dot_general(pick, piece, NT_DIMS, preferred_element_type=F32)
                             for piece in _split3(dtb[:, cols]))
            acsr_ref[g] = sum(lax.dot_general(pick, piece, NT_DIMS, preferred_element_type=F32)
                              for piece in _split3(acsb[:, cols]))

    rows = pl.BlockSpec((SSM_GROUPS, SSM_HPG, SSM_CHUNK), lambda c: (0, 0, c))
    dense = pl.BlockSpec((SSM_CHUNK, SSM_D_INNER), lambda c: (c, 0))
    return pl.pallas_call(
        body, grid=(nc,),
        in_specs=[pl.BlockSpec((SSM_CHUNK, SSM_DT_PAD), lambda c: (c, DT_COL_BLOCK)),
                  pl.BlockSpec((1, SSM_DT_PAD), lambda c: (0, 0)),
                  pl.BlockSpec((1, SSM_D_INNER), lambda c: (0, 0))],
        out_specs=[dense, dense, rows, rows],
        out_shape=[jax.ShapeDtypeStruct((l, SSM_D_INNER), F32), jax.ShapeDtypeStruct((l, SSM_D_INNER), F32),
                   jax.ShapeDtypeStruct((SSM_GROUPS, SSM_HPG, l), F32),
                   jax.ShapeDtypeStruct((SSM_GROUPS, SSM_HPG, l), F32)],
        compiler_params=_params("parallel"), name=name,
    )(proj, bias, alog_lanes)


def _dt_bwd(ddt, proj, bias, dproj, name):
    l = proj.shape[0]
    tl = _row_tile(l)

    def body(g_ref, p_ref, b_ref, _, o_ref, db_ref):
        @pl.when(pl.program_id(0) == 0)
        def _():
            db_ref[...] = jnp.zeros_like(db_ref)

        d = g_ref[...] * _sigmoid(p_ref[...] + b_ref[...])
        o_ref[...] = d.astype(o_ref.dtype)
        db_ref[...] += jnp.sum(d, axis=0, keepdims=True)

    return pl.pallas_call(
        body, grid=(l // tl,),
        in_specs=[pl.BlockSpec((tl, SSM_DT_PAD), lambda i: (i, 0)),
                  pl.BlockSpec((tl, SSM_DT_PAD), lambda i: (i, DT_COL_BLOCK)),
                  pl.BlockSpec((1, SSM_DT_PAD), lambda i: (0, 0)),
                  pl.BlockSpec(memory_space=pl.ANY)],
        out_specs=[pl.BlockSpec((tl, SSM_DT_PAD), lambda i: (i, DT_COL_BLOCK)),
                   pl.BlockSpec((1, SSM_DT_PAD), lambda i: (0, 0))],
        out_shape=[jax.ShapeDtypeStruct(dproj.shape, dproj.dtype), jax.ShapeDtypeStruct((1, SSM_DT_PAD), F32)],
        input_output_aliases={3: 0}, compiler_params=_params("arbitrary"), name=name,
    )(ddt, proj, bias, dproj)


GP = SSM_HPG * SSM_HEAD_DIM
HEAD_DIM_LOG2 = SSM_HEAD_DIM.bit_length() - 1
CHUNK_LOG2 = SSM_CHUNK.bit_length() - 1
GPS = 8
B_BLOCK0 = SSM_D_INNER // SSM_STATE
C_BLOCK0 = (SSM_D_INNER + SSM_BC_DIM) // SSM_STATE


def _chunk_iotas():
    ri = lax.broadcasted_iota(jnp.int32, (SSM_CHUNK, SSM_CHUNK), 0)
    cj = lax.broadcasted_iota(jnp.int32, (SSM_CHUNK, SSM_CHUNK), 1)
    return ri, cj


def _head_decay(acsb, acs_r, r, ri, cj):
    pair = acsb[:, (r // 2) * LANES:(r // 2 + 1) * LANES]
    mine_low = r % 2 == 0
    lane = lax.broadcasted_iota(jnp.int32, (1, LANES), 1)
    col = jnp.where((lane < SSM_HEAD_DIM) == mine_low, pair, pltpu.roll(pair, SSM_HEAD_DIM, 1))
    return jnp.exp(jnp.where(ri >= cj, col - acs_r[r:r + 1, :], NEG_INF))


def _head_masked_rows(v, dtype):
    head_of_lane = lax.shift_right_logical(lax.broadcasted_iota(jnp.int32, (1, GP), 1), HEAD_DIM_LOG2)
    return jnp.concatenate([jnp.where(head_of_lane == r, v, 0.0).astype(dtype) for r in range(SSM_HPG)], axis=0)


def _ssd_fwd(xbc, dtb, acsb, acs_r, d_lanes, proj, gate_w, name, ride=()):
    l = xbc.shape[0]
    nc = l // SSM_CHUNK
    assert GPS == SSM_GROUPS

    n_ride = len(ride)

    def body(*refs):
        x_ref, b_ref, c_ref, dtb_ref, acsb_ref, acsr_ref, d_ref, z_ref, gw_ref = refs[:9]
        ride_in = refs[9:9 + n_ride]
        y_ref, act_ref, hin_ref = refs[9 + n_ride:12 + n_ride]
        ride_out = refs[12 + n_ride:12 + 2 * n_ride]
        h_ref = refs[12 + 2 * n_ride]
        ride_sems = refs[13 + 2 * n_ride:]
        c = pl.program_id(0)
        if n_ride:
            @pl.when(c == 0)
            def _():
                _gather_between_chips(ride_in, ride_out, *ride_sems, wait=False)

            @pl.when(c == nc - 1)
            def _():
                _gather_between_chips(ride_in, ride_out, *ride_sems, wait=True)

        ri, cj = _chunk_iotas()
        for k in range(GPS):
            g = k
            cols = slice(k * GP, (k + 1) * GP)
            ncols = slice(k * SSM_STATE, (k + 1) * SSM_STATE)

            @pl.when(c == 0)
            def _():
                h_ref[g] = jnp.zeros((SSM_STATE, GP), F32)

            xv = x_ref[:, cols].astype(F32)
            bb = b_ref[:, ncols].astype(BF16)
            cb16 = c_ref[:, ncols].astype(BF16)
            acs_v = acsb_ref[:, cols]
            acs_r_v = acsr_ref[k]
            lastb = acs_v[SSM_CHUNK - 1:SSM_CHUNK, :]
            xd = xv * dtb_ref[:, cols]
            cb = lax.dot_general(cb16, bb, NT_DIMS, preferred_element_type=F32)
            hin = h_ref[g]
            hin_ref[0, k] = hin
            yoff = jnp.dot(cb16, hin.astype(BF16), preferred_element_type=F32)
            ms = [(cb * _head_decay(acs_v, acs_r_v, r, ri, cj)).astype(BF16) for r in range(SSM_HPG)]
            ydiag = jnp.dot(jnp.concatenate(ms, axis=1), _head_masked_rows(xd, BF16), preferred_element_type=F32)
            y_ref[:, cols] = ydiag + jnp.exp(acs_v) * yoff + d_ref[k] * xv
            h_ref[g] = hin * jnp.exp(lastb) + _bdot_tn(bb, xd * jnp.exp(lastb - acs_v))
        z = z_ref[...]
        yg = y_ref[...] * (z * _sigmoid(z))
        r = lax.rsqrt(jnp.mean(yg * yg, axis=-1, keepdims=True) + EPS)
        act_ref[...] = (yg * r * gw_ref[...]).astype(act_ref.dtype)

    lanes = pl.BlockSpec((SSM_CHUNK, SSM_D_INNER), lambda c: (c, 0))
    return pl.pallas_call(
        body, grid=(nc,),
        in_specs=[lanes,
                  pl.BlockSpec((SSM_CHUNK, SSM_BC_DIM), lambda c: (c, B_BLOCK0 // GPS)),
                  pl.BlockSpec((SSM_CHUNK, SSM_BC_DIM), lambda c: (c, C_BLOCK0 // GPS)),
                  lanes, lanes,
                  pl.BlockSpec((SSM_GROUPS, SSM_HPG, SSM_CHUNK), lambda c: (0, 0, c)),
                  pl.BlockSpec((SSM_GROUPS, 1, GP), lambda c: (0, 0, 0)),
                  lanes, pl.BlockSpec((1, SSM_D_INNER), lambda c: (0, 0))] + [ANY] * n_ride,
        out_specs=[lanes, lanes, pl.BlockSpec((1, SSM_GROUPS, SSM_STATE, GP), lambda c: (c, 0, 0, 0))] + [ANY] * n_ride,
        out_shape=[jax.ShapeDtypeStruct((l, SSM_D_INNER), F32), jax.ShapeDtypeStruct((l, SSM_D_INNER), BF16),
                   jax.ShapeDtypeStruct((nc, SSM_GROUPS, SSM_STATE, GP), F32)]
        + [jax.ShapeDtypeStruct((N_CHIPS,) + a.shape, a.dtype) for a in ride],
        scratch_shapes=[pltpu.VMEM((SSM_GROUPS, SSM_STATE, GP), F32)] + (_gather_sems(n_ride) if n_ride else []),
        compiler_params=_params("arbitrary"), name=name,
    )(xbc, xbc, xbc, dtb, acsb, acs_r, d_lanes, proj, gate_w.reshape(1, SSM_D_INNER), *ride)


def _ssd_bwd(xbc, dtb, acsb, dtr, acs_r, a_log, d_lanes, hin, dact, y, proj, gate_w, name):
    l = xbc.shape[0]
    nc = l // SSM_CHUNK

    def body(x_ref, b_ref, c_ref, dtb_ref, acsb_ref, dtr_ref, acsr_ref, alc_ref, d_ref, hin_ref,
             dact_ref, y_ref, z_ref, gw_ref,
             dxbc_ref, ddt_ref, dal_ref, dd_ref, dproj_ref, dgw_ref, dh_ref, dy_ref, acc_ref):
        c = pl.program_id(0)
        dx_ref = dxbc_ref.at[:, 0:SSM_D_INNER]
        db_ref = dxbc_ref.at[:, SSM_D_INNER:SSM_D_INNER + SSM_BC_DIM]
        dc_ref = dxbc_ref.at[:, SSM_D_INNER + SSM_BC_DIM:SSM_CONV_DIM]

        @pl.when(c == 0)
        def _():
            dal_ref[...] = jnp.zeros_like(dal_ref)
            dd_ref[...] = jnp.zeros_like(dd_ref)
            acc_ref[...] = jnp.zeros_like(acc_ref)

        z = z_ref[...]
        yv = y_ref[...]
        s = _sigmoid(z)
        sz = z * s
        yg = yv * sz
        r = lax.rsqrt(jnp.mean(yg * yg, axis=-1, keepdims=True) + EPS)
        nrm = yg * r
        gv = dact_ref[...]
        gw = gv * gw_ref[...]
        dyg = r * (gw - nrm * jnp.mean(gw * nrm, axis=-1, keepdims=True))
        dy_ref[...] = dyg * sz
        dproj_ref[...] = (dyg * yv * (s * (1.0 + z * (1.0 - s)))).astype(dproj_ref.dtype)
        acc_ref[...] += jnp.sum((gv * nrm).reshape(SSM_CHUNK // 8, 8, SSM_D_INNER), axis=0)

        @pl.when(c == nc - 1)
        def _():
            dgw_ref[...] = jnp.sum(acc_ref[...], axis=0, keepdims=True)

        for k in range(GPS):
            one_group(c, k, k, x_ref, b_ref, c_ref, dtb_ref, acsb_ref, dtr_ref, acsr_ref, alc_ref, d_ref,
                      hin_ref, dy_ref, dx_ref, db_ref, dc_ref, ddt_ref, dal_ref, dd_ref, dh_ref)

    def one_group(c, g, k, x_ref, b_ref, c_ref, dtb_ref, acsb_ref, dtr_ref, acsr_ref, alc_ref, d_ref, hin_ref, dy_ref,
                  dx_ref, db_ref, dc_ref, ddt_ref, dal_ref, dd_ref, dh_ref):
        cols = slice(k * GP, (k + 1) * GP)
        ncols = slice(k * SSM_STATE, (k + 1) * SSM_STATE)

        @pl.when(c == 0)
        def _():
            dh_ref[g] = jnp.zeros((SSM_STATE, GP), F32)

        xv = x_ref[:, cols].astype(F32)
        dyv = dy_ref[:, cols]
        bb = b_ref[:, ncols].astype(BF16)
        cb16 = c_ref[:, ncols].astype(BF16)
        dtb = dtb_ref[:, cols]
        acsb = acsb_ref[:, cols]
        dtr_v = dtr_ref[k]
        acs_r = acsr_ref[k]
        a_col = -jnp.exp(alc_ref[k])
        ri, cj = _chunk_iotas()
        head_of_lane = lax.shift_right_logical(lax.broadcasted_iota(jnp.int32, (SSM_HPG, GP), 1), HEAD_DIM_LOG2)
        ind_t = (head_of_lane == lax.broadcasted_iota(jnp.int32, (SSM_HPG, GP), 0)).astype(BF16)
        lastb = acsb[SSM_CHUNK - 1:SSM_CHUNK, :]
        ecb = jnp.exp(acsb)
        dteb = jnp.exp(lastb - acsb)
        xd = xv * dtb
        xw = xd * dteb
        cb = lax.dot_general(cb16, bb, NT_DIMS, preferred_element_type=F32)
        hin_v = hin_ref[0, k]
        dhn = dh_ref[g]
        h16 = hin_v.astype(BF16)
        dh16 = dhn.astype(BF16)
        ch = jnp.dot(cb16, h16, preferred_element_type=F32)
        bdh = jnp.dot(bb, dh16, preferred_element_type=F32)
        dym = _head_masked_rows(dyv, BF16)
        g_all = lax.dot_general(dym, xd.astype(BF16), NT_DIMS, preferred_element_type=F32)
        gl_sum = jnp.zeros((SSM_CHUNK, SSM_CHUNK), F32)
        ms, qs = [], []
        for r in range(SSM_HPG):
            decay = _head_decay(acsb, acs_r, r, ri, cj)
            gl = g_all[r * SSM_CHUNK:(r + 1) * SSM_CHUNK] * decay
            gl_sum = gl_sum + gl
            ms.append((cb * decay).astype(BF16))
            qs.append((gl * cb).astype(BF16))
        dxd = lax.dot_general(jnp.concatenate(ms, axis=0), dym, TN_DIMS, preferred_element_type=F32) + dteb * bdh
        cum = jnp.dot(jnp.concatenate(qs, axis=0), (ri < cj).astype(BF16), preferred_element_type=F32)
        sub4 = lax.broadcasted_iota(jnp.int32, (SSM_HPG, 1), 0)
        da = jnp.zeros((SSM_HPG, SSM_CHUNK), F32)
        for r in range(SSM_HPG):
            rect = jnp.sum(jnp.where(ri >= cj, cum[r * SSM_CHUNK:(r + 1) * SSM_CHUNK], 0.0), axis=0, keepdims=True)
            da = da + jnp.where(sub4 == r, rect, 0.0)
        z2 = xw * bdh
        sub8 = lax.broadcasted_iota(jnp.int32, (8, 1), 0)
        col_sums = (jnp.where(sub8 == 0, jnp.sum(z2, axis=0, keepdims=True), 0.0)
                    + jnp.where(sub8 == 1, jnp.sum(dhn * hin_v, axis=0, keepdims=True), 0.0)
                    + jnp.where(sub8 == 2, jnp.sum(dyv * xv, axis=0, keepdims=True), 0.0))
        summands = jnp.concatenate([dyv * ecb * ch - z2, dxd * xv, col_sums], axis=0)
        sums = sum(lax.dot_general(ind_t, piece, NT_DIMS, preferred_element_type=F32) for piece in _split3(summands)[:2])
        per_pos = sums[:, :2 * SSM_CHUNK]
        totals = sums[:, 2 * SSM_CHUNK:]
        e_last = totals[:, 0:1] + jnp.exp(acs_r[:, SSM_CHUNK - 1:SSM_CHUNK]) * totals[:, 1:2]
        da = (da + e_last + jnp.dot(per_pos[:, :SSM_CHUNK], (ri >= cj).astype(F32), preferred_element_type=F32,
                                    precision=lax.Precision.HIGHEST))
        ddt_ref[k] = a_col * da + per_pos[:, SSM_CHUNK:]
        dal_ref[g] += a_col * jnp.sum(da * dtr_v, axis=1, keepdims=True)
        dd_ref[g] += totals[:, 2:3]
        dx_ref[:, cols] = (dxd * dtb + d_ref[k] * dyv).astype(dx_ref.dtype)
        w16 = (ecb * dyv).astype(BF16)
        xw16 = xw.astype(BF16)
        gl16 = gl_sum.astype(BF16)
        dc_ref[:, ncols] = (jnp.dot(gl16, bb, preferred_element_type=F32)
                            + lax.dot_general(w16, h16, NT_DIMS, preferred_element_type=F32)).astype(dc_ref.dtype)
        db_ref[:, ncols] = (lax.dot_general(gl16, cb16, TN_DIMS, preferred_element_type=F32)
                            + lax.dot_general(xw16, dh16, NT_DIMS, preferred_element_type=F32)).astype(db_ref.dtype)
        dh_ref[g] = dhn * jnp.exp(lastb) + lax.dot_general(cb16, w16, TN_DIMS, preferred_element_type=F32)

    def rev(c):
        return nc - 1 - c

    small = pl.BlockSpec((SSM_GROUPS, SSM_HPG, 1), lambda c: (0, 0, 0))
    lanes = pl.BlockSpec((SSM_CHUNK, SSM_D_INNER), lambda c: (rev(c), 0))
    rows = pl.BlockSpec((SSM_GROUPS, SSM_HPG, SSM_CHUNK), lambda c: (0, 0, rev(c)))
    vec = pl.BlockSpec((1, SSM_D_INNER), lambda c: (0, 0))
    return pl.pallas_call(
        body, grid=(nc,),
        in_specs=[lanes,
                  pl.BlockSpec((SSM_CHUNK, SSM_BC_DIM), lambda c: (rev(c), B_BLOCK0 // GPS)),
                  pl.BlockSpec((SSM_CHUNK, SSM_BC_DIM), lambda c: (rev(c), C_BLOCK0 // GPS)),
                  lanes, lanes, rows, rows,
                  pl.BlockSpec((SSM_GROUPS, SSM_HPG, 1), lambda c: (0, 0, 0)),
                  pl.BlockSpec((SSM_GROUPS, 1, GP), lambda c: (0, 0, 0)),
                  pl.BlockSpec((1, SSM_GROUPS, SSM_STATE, GP), lambda c: (rev(c), 0, 0, 0)),
                  lanes, lanes, lanes, vec],
        out_specs=[pl.BlockSpec((SSM_CHUNK, SSM_CONV_DIM), lambda c: (rev(c), 0)),
                   rows, small, small, lanes, vec],
        out_shape=[jax.ShapeDtypeStruct((l, SSM_CONV_DIM), BF16), jax.ShapeDtypeStruct((SSM_GROUPS, SSM_HPG, l), F32),
                   jax.ShapeDtypeStruct((SSM_GROUPS, SSM_HPG, 1), F32),
                   jax.ShapeDtypeStruct((SSM_GROUPS, SSM_HPG, 1), F32),
                   jax.ShapeDtypeStruct((l, SSM_IN_PAD), BF16), jax.ShapeDtypeStruct((1, SSM_D_INNER), F32)],
        scratch_shapes=[pltpu.VMEM((SSM_GROUPS, SSM_STATE, GP), F32), pltpu.VMEM((SSM_CHUNK, SSM_D_INNER), F32),
                        pltpu.VMEM((8, SSM_D_INNER), F32)],
        compiler_params=_params("arbitrary"), name=name,
    )(xbc, xbc, xbc, dtb, acsb, dtr, acs_r, a_log.reshape(SSM_GROUPS, SSM_HPG, 1), d_lanes, hin, dact, y, proj,
      gate_w.reshape(1, SSM_D_INNER))


LANES = 128
ROPE_Q_CHUNKS = ATT_WIDTH // LANES
ROPE_K_CHUNKS = ATT_KV_WIDTH // LANES


def _rope_tables(positions):
    inv = ROPE_THETA ** (-jnp.arange(0, ROPE_DIM, 2, dtype=F32) / ROPE_DIM)
    ang = positions.astype(F32)[:, None] * inv
    cos, sin = jnp.cos(ang), jnp.sin(ang)
    l = positions.shape[0]
    rest = ATT_HEAD_DIM - ROPE_DIM
    ones, zeros = jnp.ones((l, rest), F32), jnp.zeros((l, rest), F32)
    z8 = jnp.zeros((l, ROPE_HALF), F32)
    cos_f = jnp.concatenate([cos, cos, ones], axis=1)
    sin_a = jnp.concatenate([-sin, z8, zeros], axis=1)
    sin_b = jnp.concatenate([z8, sin, zeros], axis=1)
    reps = LANES // ATT_HEAD_DIM
    return tuple(jnp.tile(t, (1, reps)) for t in (cos_f, sin_a, sin_b))


ATT_QKV4 = 3 * ATT_WIDTH


def _both_halves(chunk):
    lane = lax.broadcasted_iota(jnp.int32, (1, LANES), 1)
    swapped = pltpu.roll(chunk, ATT_HEAD_DIM, 1)
    return jnp.where(lane < ATT_HEAD_DIM, chunk, swapped), jnp.where(lane < ATT_HEAD_DIM, swapped, chunk)


def _rope_fwd(proj, tables, name):
    l = proj.shape[0]
    tl = _pick(l, (256, 128))

    def body(p_ref, c_ref, sa_ref, sb_ref, o_ref):
        cos_f, sin_a, sin_b = c_ref[...], sa_ref[...], sb_ref[...]

        def rope(t):
            return t * cos_f + pltpu.roll(t, LANES - ROPE_HALF, 1) * sin_a + pltpu.roll(t, ROPE_HALF, 1) * sin_b

        for k in range(ROPE_Q_CHUNKS):
            sl = slice(k * LANES, (k + 1) * LANES)
            o_ref[:, sl] = (rope(p_ref[:, sl]) * Q_SCALE).astype(o_ref.dtype)
        for part in range(2):
            for k in range(ROPE_K_CHUNKS):
                src = ATT_WIDTH + part * ATT_KV_WIDTH + k * LANES
                t = p_ref[:, src:src + LANES]
                if part == 0:
                    t = rope(t)
                for head, dup in enumerate(_both_halves(t.astype(o_ref.dtype))):
                    dst = (1 + part) * ATT_WIDTH + (2 * k + head) * ATT_GQA * ATT_HEAD_DIM
                    o_ref[:, dst:dst + LANES] = dup
                    o_ref[:, dst + LANES:dst + 2 * LANES] = dup

    tab = pl.BlockSpec((tl, LANES), lambda i: (i, 0))
    return pl.pallas_call(
        body, grid=(l // tl,), in_specs=[pl.BlockSpec((tl, ATT_IN_DIM), lambda i: (i, 0)), tab, tab, tab],
        out_specs=pl.BlockSpec((tl, ATT_QKV4), lambda i: (i, 0)),
        out_shape=jax.ShapeDtypeStruct((l, ATT_QKV4), BF16), compiler_params=_params("parallel"), name=name,
    )(proj, *tables)


def _rope_bwd(dq, dk4, dv4, dgate, tables, name):
    l = dq.shape[0]
    tl = _pick(l, (256, 128))

    def body(dq_ref, dk_ref, dv_ref, dg_ref, c_ref, sa_ref, sb_ref, o_ref):
        cos_f, sin_a, sin_b = c_ref[...], sa_ref[...], sb_ref[...]
        lane = lax.broadcasted_iota(jnp.int32, (1, LANES), 1)

        def unrope(t):
            return t * cos_f + pltpu.roll(t * sin_a, ROPE_HALF, 1) + pltpu.roll(t * sin_b, LANES - ROPE_HALF, 1)

        def head_total(ref, kvh):
            base = kvh * ATT_GQA * ATT_HEAD_DIM
            s = ref[:, base:base + LANES] + ref[:, base + LANES:base + 2 * LANES]
            return s + pltpu.roll(s, ATT_HEAD_DIM, 1)

        for k in range(ROPE_Q_CHUNKS):
            sl = slice(k * LANES, (k + 1) * LANES)
            o_ref[:, sl] = unrope(dq_ref[:, sl] * Q_SCALE).astype(o_ref.dtype)
        for k in range(ROPE_K_CHUNKS):
            dk = jnp.where(lane < ATT_HEAD_DIM, head_total(dk_ref, 2 * k), head_total(dk_ref, 2 * k + 1))
            dv = jnp.where(lane < ATT_HEAD_DIM, head_total(dv_ref, 2 * k), head_total(dv_ref, 2 * k + 1))
            o_ref[:, ATT_WIDTH + k * LANES:ATT_WIDTH + (k + 1) * LANES] = unrope(dk).astype(o_ref.dtype)
            at = ATT_WIDTH + ATT_KV_WIDTH + k * LANES
            o_ref[:, at:at + LANES] = dv.astype(o_ref.dtype)
        o_ref[:, ATT_QKV:ATT_IN_DIM] = dg_ref[...].astype(o_ref.dtype)

    tab = pl.BlockSpec((tl, LANES), lambda i: (i, 0))
    wide = pl.BlockSpec((tl, ATT_WIDTH), lambda i: (i, 0))
    return pl.pallas_call(
        body, grid=(l // tl,), in_specs=[wide, wide, wide, wide, tab, tab, tab],
        out_specs=pl.BlockSpec((tl, ATT_IN_DIM), lambda i: (i, 0)),
        out_shape=jax.ShapeDtypeStruct((l, ATT_IN_DIM), BF16), compiler_params=_params("parallel"), name=name,
    )(dq, dk4, dv4, dgate, *tables)


GATE_HALF = ATT_WIDTH // 2
GATE_COL_BLOCK = ATT_QKV // GATE_HALF


ATT_STACK = ATT_GQA * ATT_BLOCK
BLOCK_LOG2 = ATT_BLOCK.bit_length() - 1


def _stack_masks(n):
    ri = lax.broadcasted_iota(jnp.int32, (ATT_STACK, ATT_BLOCK), 0) & (ATT_BLOCK - 1)
    cj = lax.broadcasted_iota(jnp.int32, (ATT_STACK, ATT_BLOCK), 1)
    return (cj > ri) & (n > 0), cj <= ri


def _stack_sinks(sink_ref, kvh):
    blk = lax.shift_right_logical(lax.broadcasted_iota(jnp.int32, (ATT_STACK, 1), 0), BLOCK_LOG2)
    col = jnp.zeros((ATT_STACK, 1), F32)
    for r in range(ATT_GQA):
        col = jnp.where(blk == r, sink_ref[kvh * ATT_GQA + r], col)
    return col


def _stack_fold(stack):
    head_of_lane = lax.shift_right_logical(lax.broadcasted_iota(jnp.int32, (1, GP), 1), HEAD_DIM_LOG2)
    out = jnp.zeros((ATT_BLOCK, GP), F32)
    for r in range(ATT_GQA):
        out = jnp.where(head_of_lane == r, stack[r * ATT_BLOCK:(r + 1) * ATT_BLOCK], out)
    return out


def _attn_fwd(qkv, proj, sinks, name):
    l = qkv.shape[0]
    nb = l // ATT_BLOCK

    def body(sink_ref, q_ref, kp_ref, kc_ref, vp_ref, vc_ref, g0_ref, g1_ref, og_ref, o_ref, lse_ref):
        n = pl.program_id(0)
        mask_p, mask_c = _stack_masks(n)
        ones = jnp.ones((ATT_BLOCK, LANES), BF16)
        for kvh in range(ATT_KV_HEADS):
            cols = slice(kvh * GP, (kvh + 1) * GP)
            q_stack = _head_masked_rows(q_ref[:, cols], BF16)
            sp = jnp.where(mask_p, lax.dot_general(q_stack, kp_ref[:, cols], NT_DIMS, preferred_element_type=F32), NEG_INF)
            sc = jnp.where(mask_c, lax.dot_general(q_stack, kc_ref[:, cols], NT_DIMS, preferred_element_type=F32), NEG_INF)
            sink = _stack_sinks(sink_ref, kvh)
            m = jnp.maximum(jnp.max(jnp.maximum(sp, sc), axis=1, keepdims=True), sink)
            pp = jnp.exp(sp - m).astype(BF16)
            pc = jnp.exp(sc - m).astype(BF16)
            acc = (jnp.dot(pp, jnp.concatenate([vp_ref[:, cols], ones], axis=1), preferred_element_type=F32)
                   + jnp.dot(pc, jnp.concatenate([vc_ref[:, cols], ones], axis=1), preferred_element_type=F32))
            den = acc[:, GP:] + jnp.exp(sink - m)
            inv = 1.0 / den
            o_ref[:, cols] = _stack_fold(acc[:, :GP] * jnp.concatenate([inv, inv], axis=1))
            lse = m + jnp.log(den)
            lse_ref[:, cols] = _stack_fold(jnp.concatenate([lse, lse], axis=1))
        for half, g_ref in enumerate((g0_ref, g1_ref)):
            sl = slice(half * GATE_HALF, (half + 1) * GATE_HALF)
            gate = g_ref[...]
            og_ref[:, sl] = (o_ref[:, sl] * (gate * _sigmoid(gate))).astype(og_ref.dtype)

    def prev(n):
        return jnp.maximum(n - 1, 0)

    wide = pl.BlockSpec((ATT_BLOCK, ATT_WIDTH), lambda n: (n, 0))
    return pl.pallas_call(
        body, grid=(nb,),
        in_specs=[pl.BlockSpec(memory_space=pltpu.SMEM), wide,
                  pl.BlockSpec((ATT_BLOCK, ATT_WIDTH), lambda n: (prev(n), 1)),
                  pl.BlockSpec((ATT_BLOCK, ATT_WIDTH), lambda n: (n, 1)),
                  pl.BlockSpec((ATT_BLOCK, ATT_WIDTH), lambda n: (prev(n), 2)),
                  pl.BlockSpec((ATT_BLOCK, ATT_WIDTH), lambda n: (n, 2)),
                  pl.BlockSpec((ATT_BLOCK, GATE_HALF), lambda n: (n, GATE_COL_BLOCK)),
                  pl.BlockSpec((ATT_BLOCK, GATE_HALF), lambda n: (n, GATE_COL_BLOCK + 1))],
        out_specs=[wide, wide, wide],
        out_shape=[jax.ShapeDtypeStruct((l, ATT_WIDTH), BF16), jax.ShapeDtypeStruct((l, ATT_WIDTH), F32),
                   jax.ShapeDtypeStruct((l, ATT_WIDTH), F32)],
        compiler_params=_params("parallel"), name=name,
    )(sinks, qkv, qkv, qkv, qkv, qkv, proj, proj)


def _attn_bwd(qkv, proj, sinks, o, lse, dog, name, ride=()):
    l = qkv.shape[0]
    nb = l // ATT_BLOCK
    n_ride = len(ride)

    def body(*refs):
        sink_ref, q_ref, kp_ref, kc_ref, vp_ref, vc_ref, g0_ref, g1_ref, o_ref, lse_ref, dog_ref = refs[:11]
        ride_in = refs[11:11 + n_ride]
        dq_ref, dk_ref, dv_ref, dg_ref, ds_ref = refs[11 + n_ride:16 + n_ride]
        ride_out = refs[16 + n_ride:16 + 2 * n_ride]
        ck_ref, cv_ref, do_ref = refs[16 + 2 * n_ride:19 + 2 * n_ride]
        ride_sems = refs[19 + 2 * n_ride:]
        n = pl.program_id(0)

        @pl.when(n == 0)
        def _():
            ds_ref[...] = jnp.zeros_like(ds_ref)
            ck_ref[...] = jnp.zeros_like(ck_ref)
            cv_ref[...] = jnp.zeros_like(cv_ref)
            if n_ride:
                _scatter_between_chips(ride_in, ride_out, *ride_sems, wait=False)

        @pl.when(n == nb)
        def _():
            dk_ref[...] = ck_ref[...]
            dv_ref[...] = cv_ref[...]
            if n_ride:
                _scatter_between_chips(ride_in, ride_out, *ride_sems, wait=True)

        @pl.when(n < nb)
        def _():
            mask_p, mask_c = _stack_masks(n)
            lane = lax.broadcasted_iota(jnp.int32, (1, ATT_Q_HEADS), 1)
            for half, g_ref in enumerate((g0_ref, g1_ref)):
                sl = slice(half * GATE_HALF, (half + 1) * GATE_HALF)
                gate = g_ref[...]
                s = _sigmoid(gate)
                dogv = dog_ref[:, sl]
                do_ref[:, sl] = dogv * (gate * s)
                dg_ref[:, sl] = dogv * o_ref[:, sl] * (s * (1.0 + gate * (1.0 - s)))
            ds_acc = jnp.zeros((1, ATT_Q_HEADS), F32)
            for kvh in range(ATT_KV_HEADS):
                cols = slice(kvh * GP, (kvh + 1) * GP)
                kp, kc, vp, vc = kp_ref[:, cols], kc_ref[:, cols], vp_ref[:, cols], vc_ref[:, cols]
                q_stack = _head_masked_rows(q_ref[:, cols], BF16)
                do_g = do_ref[:, cols]
                do_stack = _head_masked_rows(do_g, BF16)
                lse_g = lse_ref[:, cols]
                lse_stack = jnp.concatenate(
                    [_both_halves(lse_g[:, (r // 2) * LANES:(r // 2 + 1) * LANES])[r % 2] for r in range(ATT_GQA)], axis=0)
                pp = jnp.exp(jnp.where(
                    mask_p, lax.dot_general(q_stack, kp, NT_DIMS, preferred_element_type=F32) - lse_stack, NEG_INF))
                pc = jnp.exp(jnp.where(
                    mask_c, lax.dot_general(q_stack, kc, NT_DIMS, preferred_element_type=F32) - lse_stack, NEG_INF))
                dpp = lax.dot_general(do_stack, vp, NT_DIMS, preferred_element_type=F32)
                dpc = lax.dot_general(do_stack, vc, NT_DIMS, preferred_element_type=F32)
                delta = jnp.sum(pp * dpp + pc * dpc, axis=1, keepdims=True)
                dsp = (pp * (dpp - delta)).astype(BF16)
                dsc = (pc * (dpc - delta)).astype(BF16)
                dq_ref[:, cols] = _stack_fold(jnp.dot(dsp, kp, preferred_element_type=F32)
                                              + jnp.dot(dsc, kc, preferred_element_type=F32))
                dk_ref[:, cols] = ck_ref[:, cols] + lax.dot_general(dsp, q_stack, TN_DIMS, preferred_element_type=F32)
                dv_ref[:, cols] = cv_ref[:, cols] + lax.dot_general(pp.astype(BF16), do_stack, TN_DIMS,
                                                                    preferred_element_type=F32)
                ck_ref[:, cols] = lax.dot_general(dsc, q_stack, TN_DIMS, preferred_element_type=F32)
                cv_ref[:, cols] = lax.dot_general(pc.astype(BF16), do_stack, TN_DIMS, preferred_element_type=F32)
                t = jnp.exp(_stack_sinks(sink_ref, kvh) - lse_stack) * delta
                for r in range(ATT_GQA):
                    tot = jnp.sum(t[r * ATT_BLOCK:(r + 1) * ATT_BLOCK], axis=0, keepdims=True)
                    ds_acc = ds_acc - jnp.where(lane == kvh * ATT_GQA + r, tot[:, :ATT_Q_HEADS], 0.0)
            ds_ref[...] += ds_acc

    def cur(n):
        return jnp.minimum(n, nb - 1)

    def prev(n):
        return jnp.maximum(n - 1, 0)

    wide = pl.BlockSpec((ATT_BLOCK, ATT_WIDTH), lambda n: (cur(n), 0))
    late = pl.BlockSpec((ATT_BLOCK, ATT_WIDTH), lambda n: (prev(n), 0))
    return pl.pallas_call(
        body, grid=(nb + 1,),
        in_specs=[pl.BlockSpec(memory_space=pltpu.SMEM), wide,
                  pl.BlockSpec((ATT_BLOCK, ATT_WIDTH), lambda n: (prev(cur(n)), 1)),
                  pl.BlockSpec((ATT_BLOCK, ATT_WIDTH), lambda n: (cur(n), 1)),
                  pl.BlockSpec((ATT_BLOCK, ATT_WIDTH), lambda n: (prev(cur(n)), 2)),
                  pl.BlockSpec((ATT_BLOCK, ATT_WIDTH), lambda n: (cur(n), 2)),
                  pl.BlockSpec((ATT_BLOCK, GATE_HALF), lambda n: (cur(n), GATE_COL_BLOCK)),
                  pl.BlockSpec((ATT_BLOCK, GATE_HALF), lambda n: (cur(n), GATE_COL_BLOCK + 1)),
                  wide, wide, wide] + [ANY] * n_ride,
        out_specs=[wide, late, late, wide, pl.BlockSpec((1, ATT_Q_HEADS), lambda n: (0, 0))] + [ANY] * n_ride,
        out_shape=[jax.ShapeDtypeStruct((l, ATT_WIDTH), F32), jax.ShapeDtypeStruct((l, ATT_WIDTH), F32),
                   jax.ShapeDtypeStruct((l, ATT_WIDTH), F32), jax.ShapeDtypeStruct((l, ATT_WIDTH), F32),
                   jax.ShapeDtypeStruct((1, ATT_Q_HEADS), F32)] + _scatter_shapes(ride),
        scratch_shapes=[pltpu.VMEM((ATT_BLOCK, ATT_WIDTH), F32), pltpu.VMEM((ATT_BLOCK, ATT_WIDTH), F32),
                        pltpu.VMEM((ATT_BLOCK, ATT_WIDTH), F32)] + (_gather_sems(n_ride) if n_ride else []),
        compiler_params=_params("arbitrary"), name=name,
    )(sinks, qkv, qkv, qkv, qkv, qkv, proj, proj, o, lse, dog, *ride)


def _local_step(x, positions, pre_norm, post_norm, conv_b, dt_bias, a_log, d_skip, gate_norm, sinks, target,
                first_in, in_proj_with_first_pair, scan_with_second_pair, attn_bwd_with_second_pair_grads):
    tables = _rope_tables(positions)
    dt_bias_pad = jnp.pad(dt_bias, ((0, 0), (0, SSM_DT_PAD - SSM_HEADS)))
    d_lanes = jnp.repeat(d_skip, SSM_HEAD_DIM, axis=1).reshape(-1, SSM_GROUPS, 1, GP)
    alog_lanes = jnp.repeat(a_log, SSM_HEAD_DIM, axis=1)
    pairs = [first_in, None]
    saved = []
    cur = x
    h = _rmsnorm_fwd(cur, pre_norm[0], "prenorm_fwd_0")
    for i in range(DEPTH):
        j = i // 2
        if i % 2 == 0:
            in_proj = functools.partial(_matmul, h, pairs[j]["ssm_w_in"], "nn", F32, f"ssm_in_{i}")
            if i == 0:
                proj, rest = in_proj_with_first_pair(in_proj)
                pairs[0] = {**first_in, **rest}
            else:
                proj = in_proj()
            pre, xbc = _conv_fwd(proj, pairs[j]["ssm_conv_w"], conv_b[j], f"conv_fwd_{i}")
            dtb, acsb, dtr, acs_r = _ssd_prep(proj, dt_bias_pad[j:j + 1], alog_lanes[j:j + 1], f"ssd_prep_{i}")
            scan = functools.partial(_ssd_fwd, xbc, dtb, acsb, acs_r, d_lanes[j], proj, gate_norm[j], f"ssd_fwd_{i}")
            if i == 0:
                y, act, hin, pairs[1] = scan_with_second_pair(scan)
            else:
                y, act, hin = scan()
            w_ssm_in = [p["ssm_w_in"] for p in pairs]
            w_ssm_out = [p["ssm_w_out"] for p in pairs]
            w_att_in = [p["att_w_in"] for p in pairs]
            w_att_out = [p["att_w_out"] for p in pairs]
            conv_w = [p["ssm_conv_w"] for p in pairs]
            ymix = _matmul(act, w_ssm_out[j], "nn", F32, f"ssm_out_{i}")
            saved.append(dict(x=cur, h=h, proj=proj, pre=pre, xbc=xbc, dtb=dtb, acsb=acsb, dtr=dtr, acs_r=acs_r, y=y,
                              hin=hin, act=act, ymix=ymix))
        else:
            proj = _matmul(h, w_att_in[j], "nn", F32, f"att_in_{i}")
            qkv = _rope_fwd(proj, tables, f"rope_fwd_{i}")
            act, o, lse = _attn_fwd(qkv, proj, sinks[j], f"attn_fwd_{i}")
            ymix = _matmul(act, w_att_out[j], "nn", F32, f"att_out_{i}")
            saved.append(dict(x=cur, h=h, proj=proj, qkv=qkv, o=o, lse=lse, act=act, ymix=ymix))
        if i + 1 < DEPTH:
            cur, h = _post_fwd(cur, ymix, post_norm[i], pre_norm[i + 1], f"post_fwd_{i}")

    gr = {k: [None] * 2 for k in ("ssm_w_in", "ssm_conv_w", "ssm_conv_b", "ssm_dt_bias", "ssm_a_log", "ssm_d",
                                  "ssm_gate_norm", "ssm_w_out", "att_w_in", "att_sinks", "att_w_out")}
    gr["pre_norm"] = [None] * DEPTH
    gr["post_norm"] = [None] * DEPTH
    last = DEPTH - 1
    g, dymix, loss_lanes, gr["post_norm"][last] = _post_loss(cur, ymix, post_norm[last], target, "post_loss")
    for i in reversed(range(DEPTH)):
        j = i // 2
        s = saved[i]
        if i % 2 == 0:
            dact = _matmul(dymix, w_ssm_out[j], "nt", F32, f"ssm_out_dx_{i}")
            gr["ssm_w_out"][j] = _matmul(s["act"], dymix, "tn", F32, f"ssm_out_dw_{i}")
            dxbc, ddt8, dal, dd, dproj, gr["ssm_gate_norm"][j] = _ssd_bwd(
                s["xbc"], s["dtb"], s["acsb"], s["dtr"], s["acs_r"], a_log[j], d_lanes[j], s["hin"], dact, s["y"],
                s["proj"], gate_norm[j], f"ssd_bwd_{i}")
            gr["ssm_a_log"][j] = dal.reshape(SSM_HEADS)
            gr["ssm_d"][j] = dd.reshape(SSM_HEADS)
            l = x.shape[0]
            ddt = jnp.pad(jnp.transpose(ddt8, (2, 0, 1)).reshape(l, SSM_HEADS), ((0, 0), (0, SSM_DT_PAD - SSM_HEADS)))
            dproj, dbias = _dt_bwd(ddt, s["proj"], dt_bias_pad[j:j + 1], dproj, f"dt_bwd_{i}")
            gr["ssm_dt_bias"][j] = dbias[0, :SSM_HEADS]
            dproj, gr["ssm_conv_w"][j], dcb = _conv_bwd(dxbc, s["pre"], s["proj"], conv_w[j], dproj, f"conv_bwd_{i}")
            gr["ssm_conv_b"][j] = dcb[0]
            w_in, key = w_ssm_in[j], "ssm_w_in"
        else:
            dog = _matmul(dymix, w_att_out[j], "nt", F32, f"att_out_dx_{i}")
            gr["att_w_out"][j] = _matmul(s["act"], dymix, "tn", F32, f"att_out_dw_{i}")
            attn_bwd = functools.partial(_attn_bwd, s["qkv"], s["proj"], sinks[j], s["o"], s["lse"], dog, f"attn_bwd_{i}")
            if i == 1:
                (dq, dk, dv, dgate, dsk), second_pair_reduced = attn_bwd_with_second_pair_grads(
                    attn_bwd, {k: gr[k][1] for k in BIG})
            else:
                dq, dk, dv, dgate, dsk = attn_bwd()
            gr["att_sinks"][j] = dsk[0]
            dproj = _rope_bwd(dq, dk, dv, dgate, tables, f"rope_bwd_{i}")
            w_in, key = w_att_in[j], "att_w_in"
        dh = _matmul(dproj, w_in, "nt", F32, f"in_dx_{i}")
        gr[key][j] = _matmul(s["h"], dproj, "tn", F32, f"in_dw_{i}")
        if i > 0:
            g, dymix, gr["pre_norm"][i], gr["post_norm"][i - 1] = _norm_bwd_chain(
                dh, s["x"], pre_norm[i], g, saved[i - 1]["ymix"], post_norm[i - 1], f"norm_bwd_{i}")
        else:
            g, gr["pre_norm"][i] = _rmsnorm_bwd(dh, s["x"], pre_norm[i], g, F32, f"prenorm_bwd_{i}")
    first_pair_grads = {k: gr.pop(k)[0] for k in BIG}
    grads = {k: jnp.stack([v.reshape(v.shape[-1]) if k in ("pre_norm", "post_norm", "ssm_gate_norm") else v for v in vs])
             for k, vs in gr.items()}
    return loss_lanes, g, grads, first_pair_grads, second_pair_reduced


N_CHIPS = 4
N_DEV = 8
MESH = pl.DeviceIdType.MESH
ANY = pl.BlockSpec(memory_space=pl.ANY)


def _place():
    x, y, c = lax.axis_index("x"), lax.axis_index("y"), lax.axis_index("c")
    return x, y, c, 2 * x + y


def _gather_sems(n):
    return [pltpu.SemaphoreType.DMA((n, N_CHIPS)), pltpu.SemaphoreType.DMA((n, N_CHIPS)), pltpu.SemaphoreType.DMA((n,))]


def _gather_between_chips(ins, outs, send_sems, recv_sems, local_sems, wait):
    n = len(ins)
    _, _, c, s = _place()
    local = [pltpu.make_async_copy(ins[w], outs[w].at[s], local_sems.at[w]) for w in range(n)]

    def remote(w, t):
        return pltpu.make_async_remote_copy(
            src_ref=ins[w].at[c], dst_ref=outs[w].at[s, c], send_sem=send_sems.at[w, t],
            recv_sem=recv_sems.at[w, s], device_id=(t // 2, t % 2, c), device_id_type=MESH)

    def arrival(w, t):
        return pltpu.make_async_remote_copy(
            src_ref=ins[w].at[c], dst_ref=outs[w].at[t, c], send_sem=send_sems.at[w, t],
            recv_sem=recv_sems.at[w, t], device_id=(t // 2, t % 2, c), device_id_type=MESH)

    if not wait:
        for cp in local:
            cp.start()
    for t in range(N_CHIPS):
        @pl.when(s != t)
        def _():
            for w in range(n):
                if wait:
                    remote(w, t).wait_send()
                    arrival(w, t).wait_recv()
                else:
                    remote(w, t).start()
    if wait:
        for cp in local:
            cp.wait()


def _pair_handoff(bufs, name):
    n = len(bufs)

    def body(*refs):
        outs = refs[n:2 * n]
        send_sems, recv_sems = refs[2 * n:]
        x, y, c, s = _place()

        def handed_on(w, t):
            return pltpu.make_async_remote_copy(
                src_ref=outs[w].at[t, c], dst_ref=outs[w].at[t, c], send_sem=send_sems.at[w, t],
                recv_sem=recv_sems.at[w, t], device_id=(x, y, 1 - c), device_id_type=MESH)

        def handed_in(w, t):
            return pltpu.make_async_remote_copy(
                src_ref=outs[w].at[t, 1 - c], dst_ref=outs[w].at[t, 1 - c], send_sem=send_sems.at[w, t],
                recv_sem=recv_sems.at[w, t], device_id=(x, y, 1 - c), device_id_type=MESH)

        for t in range(N_CHIPS):
            @pl.when(s != t)
            def _():
                for w in range(n):
                    handed_on(w, t).start()
        for t in range(N_CHIPS):
            @pl.when(s != t)
            def _():
                for w in range(n):
                    handed_on(w, t).wait_send()
                    handed_in(w, t).wait_recv()

    return pl.pallas_call(
        body, in_specs=[ANY] * n, out_specs=[ANY] * n,
        out_shape=[jax.ShapeDtypeStruct(a.shape, a.dtype) for a in bufs],
        scratch_shapes=[pltpu.SemaphoreType.DMA((n, N_CHIPS)), pltpu.SemaphoreType.DMA((n, N_CHIPS))],
        input_output_aliases={w: w for w in range(n)}, name=name,
    )(*bufs)


def _chip_gather(shards, name):
    n = len(shards)

    def body(*refs):
        ins, outs = refs[:n], refs[n:2 * n]
        _gather_between_chips(ins, outs, *refs[2 * n:], wait=False)
        _gather_between_chips(ins, outs, *refs[2 * n:], wait=True)

    bufs = pl.pallas_call(
        body, in_specs=[ANY] * n, out_specs=[ANY] * n,
        out_shape=[jax.ShapeDtypeStruct((N_CHIPS,) + a.shape, a.dtype) for a in shards],
        scratch_shapes=_gather_sems(n), name=name,
    )(*shards)
    return _pair_handoff(bufs, name + "_handoff")


def _pair_swap(parts, name):
    n = len(parts)

    def body(*refs):
        ins, outs = refs[:n], refs[n:2 * n]
        send_sems, recv_sems = refs[2 * n:]
        x, y, c, _ = _place()
        cps = [pltpu.make_async_remote_copy(
            src_ref=ins[w].at[1 - c], dst_ref=outs[w], send_sem=send_sems.at[w], recv_sem=recv_sems.at[w],
            device_id=(x, y, 1 - c), device_id_type=MESH) for w in range(n)]
        for cp in cps:
            cp.start()
        for cp in cps:
            cp.wait()

    return pl.pallas_call(
        body, in_specs=[ANY] * n, out_specs=[ANY] * n,
        out_shape=[jax.ShapeDtypeStruct(a.shape[1:], a.dtype) for a in parts],
        scratch_shapes=[pltpu.SemaphoreType.DMA((n,)), pltpu.SemaphoreType.DMA((n,))],
        name=name,
    )(*parts)


def _scatter_between_chips(ins, outs, send_sems, recv_sems, local_sems, wait):
    n = len(ins)
    _, _, c, s = _place()

    def block(w, t):
        rows = ins[w].shape[0] // N_CHIPS
        return ins[w].at[pl.ds(t * rows, rows)]

    local = [pltpu.make_async_copy(block(w, s), outs[w].at[s], local_sems.at[w]) for w in range(n)]

    def remote(w, t):
        return pltpu.make_async_remote_copy(
            src_ref=block(w, t), dst_ref=outs[w].at[s], send_sem=send_sems.at[w, t], recv_sem=recv_sems.at[w, s],
            device_id=(t // 2, t % 2, c), device_id_type=MESH)

    def arrival(w, t):
        return pltpu.make_async_remote_copy(
            src_ref=block(w, t), dst_ref=outs[w].at[t], send_sem=send_sems.at[w, t], recv_sem=recv_sems.at[w, t],
            device_id=(t // 2, t % 2, c), device_id_type=MESH)

    if not wait:
        for cp in local:
            cp.start()
    for t in range(N_CHIPS):
        @pl.when(s != t)
        def _():
            for w in range(n):
                if wait:
                    remote(w, t).wait_send()
                    arrival(w, t).wait_recv()
                else:
                    remote(w, t).start()
    if wait:
        for cp in local:
            cp.wait()


def _scatter_shapes(parts):
    return [jax.ShapeDtypeStruct((N_CHIPS, a.shape[0] // N_CHIPS, a.shape[1]), a.dtype) for a in parts]


def _chip_scatter(parts, name):
    n = len(parts)

    def body(*refs):
        ins, outs = refs[:n], refs[n:2 * n]
        _scatter_between_chips(ins, outs, *refs[2 * n:], wait=False)
        _scatter_between_chips(ins, outs, *refs[2 * n:], wait=True)

    return pl.pallas_call(
        body, in_specs=[ANY] * n, out_specs=[ANY] * n, out_shape=_scatter_shapes(parts),
        scratch_shapes=_gather_sems(n), name=name,
    )(*parts)


def _pair_merge(parts, name):
    n = len(parts)

    def body(*refs):
        ins, outs = refs[:n], refs[n:2 * n]
        send_sems, recv_sems = refs[2 * n:]
        x, y, c, _ = _place()
        cps = [pltpu.make_async_remote_copy(
            src_ref=ins[w], dst_ref=outs[w], send_sem=send_sems.at[w], recv_sem=recv_sems.at[w],
            device_id=(x, y, 1 - c), device_id_type=MESH) for w in range(n)]
        for cp in cps:
            cp.start()
        for cp in cps:
            cp.wait()

    return pl.pallas_call(
        body, in_specs=[ANY] * n, out_specs=[ANY] * n,
        out_shape=[jax.ShapeDtypeStruct(a.shape, a.dtype) for a in parts],
        scratch_shapes=[pltpu.SemaphoreType.DMA((n,)), pltpu.SemaphoreType.DMA((n,))],
        name=name,
    )(*parts)


def _all_gather_small(a, name):
    def body(in_ref, out_ref, send_sems, recv_sems, local_sem):
        x, y, c, _ = _place()
        me = 4 * x + 2 * y + c
        local = pltpu.make_async_copy(in_ref, out_ref.at[me], local_sem)
        local.start()

        def remote(d):
            return pltpu.make_async_remote_copy(
                src_ref=in_ref, dst_ref=out_ref.at[me], send_sem=send_sems.at[d], recv_sem=recv_sems.at[me],
                device_id=(d // 4, (d // 2) % 2, d % 2), device_id_type=MESH)

        def arrival(d):
            return pltpu.make_async_remote_copy(
                src_ref=in_ref, dst_ref=out_ref.at[d], send_sem=send_sems.at[d], recv_sem=recv_sems.at[d],
                device_id=(d // 4, (d // 2) % 2, d % 2), device_id_type=MESH)

        for d in range(N_DEV):
            @pl.when(me != d)
            def _():
                remote(d).start()
        for d in range(N_DEV):
            @pl.when(me != d)
            def _():
                remote(d).wait_send()
                arrival(d).wait_recv()
        local.wait()

    return pl.pallas_call(
        body, in_specs=[ANY], out_specs=ANY, out_shape=jax.ShapeDtypeStruct((N_DEV,) + a.shape, a.dtype),
        scratch_shapes=[pltpu.SemaphoreType.DMA((N_DEV,)), pltpu.SemaphoreType.DMA((N_DEV,)), pltpu.SemaphoreType.DMA],
        name=name,
    )(a)


def _reduce_tile(rows):
    return _pick(rows, (256, 128, 16))


def _pair_add(full, other, layer, name):
    _, rows, cols = full.shape
    tr = _reduce_tile(rows)

    def body(layer_ref, a_ref, b_ref, o_ref):
        o_ref[...] = (a_ref[0] + b_ref[...]).astype(o_ref.dtype)

    return pl.pallas_call(
        body,
        grid_spec=pltpu.PrefetchScalarGridSpec(
            num_scalar_prefetch=1, grid=(rows // tr,),
            in_specs=[pl.BlockSpec((1, tr, cols), lambda i, lr: (lr[0], i, 0)), pl.BlockSpec((tr, cols), lambda i, lr: (i, 0))],
            out_specs=pl.BlockSpec((tr, cols), lambda i, lr: (i, 0))),
        out_shape=jax.ShapeDtypeStruct((rows, cols), BF16), compiler_params=_params("parallel"), name=name,
    )(layer, full, other)


def _sum_slots(a, name):
    n, rows, cols = a.shape
    tr = _reduce_tile(rows)

    def body(a_ref, o_ref):
        acc = a_ref[0].astype(F32)
        for k in range(1, n):
            acc = acc + a_ref[k].astype(F32)
        o_ref[...] = acc

    return pl.pallas_call(
        body, grid=(rows // tr,), in_specs=[pl.BlockSpec((n, tr, cols), lambda i: (0, i, 0))],
        out_specs=pl.BlockSpec((tr, cols), lambda i: (i, 0)),
        out_shape=jax.ShapeDtypeStruct((rows, cols), F32), compiler_params=_params("parallel"), name=name,
    )(a)


def _adamw(w, g, m, v, name):
    rows, cols = w.shape
    tr = _pick(rows, (256, 8))

    def body(w_ref, g_ref, m_ref, v_ref, d_ref, nm_ref, nv_ref):
        gv = g_ref[...]
        mn = ADAM_B1 * m_ref[...] + (1.0 - ADAM_B1) * gv
        vn = ADAM_B2 * v_ref[...] + (1.0 - ADAM_B2) * jnp.square(gv)
        m_hat = mn / (1.0 - ADAM_B1 ** ADAM_STEP)
        v_hat = vn / (1.0 - ADAM_B2 ** ADAM_STEP)
        d_ref[...] = -ADAM_LR * (m_hat / (jnp.sqrt(v_hat) + ADAM_EPS) + ADAM_WD * w_ref[...])
        nm_ref[...] = mn
        nv_ref[...] = vn

    blk = pl.BlockSpec((tr, cols), lambda i: (i, 0))
    return pl.pallas_call(
        body, grid=(rows // tr,), in_specs=[blk] * 4, out_specs=[blk] * 3,
        out_shape=[jax.ShapeDtypeStruct((rows, cols), F32)] * 3, compiler_params=_params("parallel"), name=name,
    )(w, g, m, v)


BIG = ("ssm_w_in", "ssm_w_out", "att_w_in", "att_w_out")
SHARDED = BIG + ("ssm_conv_w",)
SMALL = ("pre_norm", "post_norm", "ssm_conv_b", "ssm_dt_bias", "ssm_a_log", "ssm_d", "ssm_gate_norm", "att_sinks")
WEIGHTS = ("pre_norm", "post_norm", "ssm_w_in", "ssm_conv_w", "ssm_conv_b", "ssm_dt_bias", "ssm_a_log", "ssm_d",
           "ssm_gate_norm", "ssm_w_out", "att_w_in", "att_sinks", "att_w_out")


def _halves(a):
    return a.reshape(2, a.shape[0] // 2, a.shape[1])


def _layer_shards(j, ssm_w_in, ssm_w_out, att_w_in, att_w_out, ssm_conv_w):
    return [_halves(ssm_w_in[j].astype(BF16)), _halves(ssm_w_out[j].astype(BF16)), _halves(att_w_in[j].astype(BF16)),
            _halves(att_w_out[j].astype(BF16)), _halves(ssm_conv_w[j])]


SHARD_KEYS = ("ssm_w_in", "ssm_w_out", "att_w_in", "att_w_out", "ssm_conv_w")


def _whole_weights(keys, gathered):
    out = {}
    for k, g in zip(keys, gathered):
        g = g.reshape((N_CHIPS, 2 * g.shape[2], g.shape[3]))
        if k in ("ssm_w_out", "att_w_out"):
            out[k] = g.reshape(N_CHIPS * g.shape[1], g.shape[2])
        else:
            out[k] = jnp.transpose(g, (1, 0, 2)).reshape(g.shape[1], N_CHIPS * g.shape[2])
    if "ssm_w_in" in out:
        out["ssm_w_in"] = jnp.pad(out["ssm_w_in"], ((0, 0), (0, SSM_IN_PAD - SSM_IN_DIM)))
    return out


def _halves_by_chip(key, g):
    if key in ("ssm_w_out", "att_w_out"):
        rows = g.shape[0] // N_CHIPS
        blocks = g.reshape(N_CHIPS, 2, rows // 2, g.shape[1])
        return jnp.transpose(blocks, (1, 0, 2, 3)).reshape(2, N_CHIPS * (rows // 2), g.shape[1])
    cols = (SSM_IN_DIM if key == "ssm_w_in" else g.shape[1]) // N_CHIPS
    rows = g.shape[0]
    blocks = g[:, :N_CHIPS * cols].reshape(2, rows // 2, N_CHIPS, cols)
    return jnp.transpose(blocks, (0, 2, 1, 3)).reshape(2, N_CHIPS * (rows // 2), cols)


def _pack_small(tree, keys):
    flat = jnp.concatenate([tree[k].reshape(-1) for k in keys])
    rows = -(-flat.shape[0] // (8 * LANES)) * 8
    return jnp.pad(flat, (0, rows * LANES - flat.shape[0])).reshape(rows, LANES)


def _unpack_small(packed, shapes, keys):
    flat = packed.reshape(-1)
    out, at = {}, 0
    for k in keys:
        n = 1
        for dim in shapes[k]:
            n *= dim
        out[k] = flat[at:at + n].reshape(shapes[k])
        at += n
    return out


def kernel(x, positions, pre_norm, post_norm, ssm_w_in, ssm_conv_w, ssm_conv_b, ssm_dt_bias, ssm_a_log, ssm_d, ssm_gate_norm, ssm_w_out, att_w_in, att_sinks, att_w_out, loss_target, m_pre_norm, m_post_norm, m_ssm_w_in, m_ssm_conv_w, m_ssm_conv_b, m_ssm_dt_bias, m_ssm_a_log, m_ssm_d, m_ssm_gate_norm, m_ssm_w_out, m_att_w_in, m_att_sinks, m_att_w_out, v_pre_norm, v_post_norm, v_ssm_w_in, v_ssm_conv_w, v_ssm_conv_b, v_ssm_dt_bias, v_ssm_a_log, v_ssm_d, v_ssm_gate_norm, v_ssm_w_out, v_att_w_in, v_att_sinks, v_att_w_out):
    w = dict(pre_norm=pre_norm, post_norm=post_norm, ssm_w_in=ssm_w_in, ssm_conv_w=ssm_conv_w, ssm_conv_b=ssm_conv_b,
             ssm_dt_bias=ssm_dt_bias, ssm_a_log=ssm_a_log, ssm_d=ssm_d, ssm_gate_norm=ssm_gate_norm, ssm_w_out=ssm_w_out,
             att_w_in=att_w_in, att_sinks=att_sinks, att_w_out=att_w_out)
    m = dict(pre_norm=m_pre_norm, post_norm=m_post_norm, ssm_w_in=m_ssm_w_in, ssm_conv_w=m_ssm_conv_w, ssm_conv_b=m_ssm_conv_b,
             ssm_dt_bias=m_ssm_dt_bias, ssm_a_log=m_ssm_a_log, ssm_d=m_ssm_d, ssm_gate_norm=m_ssm_gate_norm,
             ssm_w_out=m_ssm_w_out, att_w_in=m_att_w_in, att_sinks=m_att_sinks, att_w_out=m_att_w_out)
    v = dict(pre_norm=v_pre_norm, post_norm=v_post_norm, ssm_w_in=v_ssm_w_in, ssm_conv_w=v_ssm_conv_w, ssm_conv_b=v_ssm_conv_b,
             ssm_dt_bias=v_ssm_dt_bias, ssm_a_log=v_ssm_a_log, ssm_d=v_ssm_d, ssm_gate_norm=v_ssm_gate_norm,
             ssm_w_out=v_ssm_w_out, att_w_in=v_att_w_in, att_sinks=v_att_sinks, att_w_out=v_att_w_out)
    c = lax.axis_index("c")
    chip = 2 * lax.axis_index("x") + lax.axis_index("y")

    sharded = (ssm_w_in, ssm_w_out, att_w_in, att_w_out, ssm_conv_w)
    own = [dict(zip(SHARD_KEYS, _layer_shards(j, *sharded))) for j in range(2)]
    now_keys = ("ssm_w_in", "ssm_conv_w")
    later_keys = ("ssm_w_out", "att_w_in", "att_w_out")
    first_in = _whole_weights(now_keys, _chip_gather([own[0][k] for k in now_keys], "gather_weights_0"))

    def in_proj_with_first_pair(matmul):
        proj, *arrived = matmul(ride=[own[0][k] for k in later_keys])
        return proj, _whole_weights(later_keys, _pair_handoff(arrived, "gather_weights_0_rest_handoff"))

    def scan_with_second_pair(scan):
        y, act, hin, *arrived = scan(ride=[own[1][k] for k in SHARD_KEYS])
        return y, act, hin, _whole_weights(SHARD_KEYS, _pair_handoff(arrived, "gather_weights_1_handoff"))

    half = jnp.reshape(c, (1,)).astype(jnp.int32)

    def reduce_begin(pair_grads, tag):
        parts = [_halves_by_chip(k, pair_grads[k]) for k in BIG]
        from_sibling = _pair_swap(parts, f"reduce_pair_swap_{tag}")
        return [_pair_add(p, o, half, f"reduce_pair_add_{tag}_{n}") for n, (p, o) in enumerate(zip(parts, from_sibling))]

    def reduce_end(by_chip, tag):
        mine = [_sum_slots(a, f"reduce_chip_sum_{tag}_{n}") for n, a in enumerate(by_chip)]
        theirs = _pair_merge(mine, f"reduce_pair_merge_{tag}")
        return {k: jnp.where(c == 0, jnp.concatenate([a, b]), jnp.concatenate([b, a])) for k, a, b in zip(BIG, mine, theirs)}

    def attn_bwd_with_second_pair_grads(attn_bwd, pair_grads):
        dq, dk, dv, dgate, dsk, *by_chip = attn_bwd(ride=reduce_begin(pair_grads, "1"))
        return (dq, dk, dv, dgate, dsk), reduce_end(by_chip, "1")

    loss_lanes, grad_x, gr, first_pair_grads, reduced_1 = _local_step(
        x[0], positions[0], pre_norm, post_norm, ssm_conv_b, ssm_dt_bias, ssm_a_log, ssm_d, ssm_gate_norm, att_sinks,
        loss_target[0], first_in, in_proj_with_first_pair, scan_with_second_pair, attn_bwd_with_second_pair_grads)
    loss = lax.psum(0.5 * jnp.sum(loss_lanes) / D_MODEL, ("x", "y", "c"))
    reduced_0 = reduce_end(_chip_scatter(reduce_begin(first_pair_grads, "0"), "reduce_chip_scatter_0"), "0")
    grads = {k: jnp.stack([reduced_0[k], reduced_1[k]]) for k in BIG}

    small_keys = SMALL + ("ssm_conv_w",)
    small_shapes = {k: w[k].shape for k in SMALL}
    small_shapes["ssm_conv_w"] = gr["ssm_conv_w"].shape
    small_sum = _sum_slots(_all_gather_small(_pack_small(gr, small_keys), "reduce_small_gather"), "reduce_small_sum")
    grads.update(_unpack_small(small_sum, small_shapes, small_keys))
    conv_cols = ssm_conv_w.shape[2]
    grads["ssm_conv_w"] = lax.dynamic_slice_in_dim(grads["ssm_conv_w"], chip * conv_cols, conv_cols, axis=2)

    delta, new_m, new_v = {}, {}, {}
    for k in SHARDED:
        shp = w[k].shape
        two_d = (shp[0] * shp[1], shp[2])
        d_, m_, v_ = _adamw(w[k].reshape(two_d), grads[k].reshape(two_d), m[k].reshape(two_d), v[k].reshape(two_d),
                            f"adamw_{k}")
        delta[k], new_m[k], new_v[k] = d_.reshape(shp), m_.reshape(shp), v_.reshape(shp)
    d_, m_, v_ = _adamw(_pack_small(w, SMALL), _pack_small(grads, SMALL), _pack_small(m, SMALL), _pack_small(v, SMALL),
                        "adamw_small")
    delta.update(_unpack_small(d_, small_shapes, SMALL))
    new_m.update(_unpack_small(m_, small_shapes, SMALL))
    new_v.update(_unpack_small(v_, small_shapes, SMALL))

    return (loss, grad_x[None], *[grads[k] for k in WEIGHTS], *[delta[k] for k in WEIGHTS],
            *[new_m[k] for k in WEIGHTS], *[new_v[k] for k in WEIGHTS])
```

```python
import functools

import jax
import jax.numpy as jnp
from jax import lax
from jax.experimental import pallas as pl
from jax.experimental.pallas import tpu as pltpu

F32 = jnp.float32
BF16 = jnp.bfloat16
EPS = 1e-6
NEG_INF = float("-inf")

D_MODEL = 1024
DEPTH = 4
SSM_D_INNER = 2048
SSM_HEAD_DIM = 64
SSM_HEADS = 32
SSM_GROUPS = 8
SSM_HPG = 4
SSM_STATE = 128
SSM_CONV = 4
SSM_CHUNK = 128
SSM_BC_DIM = 1024
SSM_CONV_DIM = 4096
SSM_IN_DIM = 6176
SSM_IN_PAD = 6272
SSM_DT_PAD = 128
ATT_HEAD_DIM = 64
ATT_Q_HEADS = 16
ATT_KV_HEADS = 4
ATT_GQA = 4
ATT_WIDTH = 1024
ATT_KV_WIDTH = 256
ATT_IN_DIM = 2560
ATT_QKV = ATT_WIDTH + 2 * ATT_KV_WIDTH
ATT_BLOCK = 128
ROPE_THETA = 500000.0
ROPE_DIM = 16
ROPE_HALF = 8
Q_SCALE = ATT_HEAD_DIM ** -0.5

ADAM_LR = 0.001
ADAM_B1 = 0.9
ADAM_B2 = 0.999
ADAM_EPS = 1e-08
ADAM_WD = 0.01
ADAM_STEP = 10

VMEM_LIMIT_BYTES = 48 * 1024 * 1024
NT_DIMS = (((1,), (1,)), ((), ()))
TN_DIMS = (((0,), (0,)), ((), ()))


def _params(*sem):
    return pltpu.CompilerParams(dimension_semantics=sem, vmem_limit_bytes=VMEM_LIMIT_BYTES)


def _pick(n, cands):
    for c in cands:
        if n % c == 0:
            return c
    return n


def _sigmoid(v):
    return 0.5 * jnp.tanh(0.5 * v) + 0.5


def _bdot(a, b):
    return jnp.dot(a.astype(BF16), b.astype(BF16), preferred_element_type=F32)


def _bdot_nt(a, b):
    return lax.dot_general(a.astype(BF16), b.astype(BF16), NT_DIMS, preferred_element_type=F32)


def _bdot_tn(a, b):
    return lax.dot_general(a.astype(BF16), b.astype(BF16), TN_DIMS, preferred_element_type=F32)


MATMUL_VMEM_BUDGET = 36 * 1024 * 1024


def _matmul_tiles(m, n, k, out_bytes, reduce_rows):
    best = None
    whole = [k] if (not reduce_rows or k <= 2048) else []
    for tk in whole + [c for c in (4096, 2048, 1024, 896, 512) if k % c == 0 and c < k]:
        for tm in (c for c in (2048, 1024, 512, 256) if m % c == 0):
            for tn in (c for c in (n, 1280, 1024, 896, 640, 512) if n % c == 0):
                acc = tm * tn * 4 if tk < k else 0
                need = 2 * (2 * tk * (tm + tn) + tm * tn * out_bytes) + acc
                if need <= MATMUL_VMEM_BUDGET and (best is None or tm * tn * min(tk, 2048) > best[0]):
                    best = (tm * tn * min(tk, 2048), tm, tn, tk)
        if best is not None and not reduce_rows:
            break
    return best[1:]


def _matmul(a, b, mode, out_dtype, name, ride=(), ride_scatters=False):
    if mode == "nn":
        (m, k), n = a.shape, b.shape[1]
    elif mode == "nt":
        (m, k), n = a.shape, b.shape[0]
    else:
        (k, m), n = a.shape, b.shape[1]
    tm, tn, tk = _matmul_tiles(m, n, k, jnp.dtype(out_dtype).itemsize, mode == "tn")
    nk = k // tk
    steps = (n // tn, m // tm, nk)
    dims = {"nn": (((1,), (0,)), ((), ())), "nt": NT_DIMS, "tn": TN_DIMS}[mode]
    n_ride = len(ride)
    exchange = _scatter_between_chips if ride_scatters else _gather_between_chips
    arrived = _scatter_shapes(ride) if ride_scatters else [jax.ShapeDtypeStruct((N_CHIPS,) + r.shape, r.dtype) for r in ride]

    def body(*refs):
        a_ref, b_ref = refs[:2]
        ride_in = refs[2:2 + n_ride]
        o_ref = refs[2 + n_ride]
        ride_out = refs[3 + n_ride:3 + 2 * n_ride]
        acc_ref = refs[3 + 2 * n_ride]
        ride_sems = refs[4 + 2 * n_ride:]
        kk = pl.program_id(2)
        at = [pl.program_id(d) for d in range(3)]
        if n_ride:
            @pl.when((at[0] == 0) & (at[1] == 0) & (at[2] == 0))
            def _():
                exchange(ride_in, ride_out, *ride_sems, wait=False)

        part = lax.dot_general(a_ref[...], b_ref[...], dims, preferred_element_type=F32)
        if nk == 1:
            o_ref[...] = part.astype(o_ref.dtype)
        else:
            @pl.when(kk == 0)
            def _():
                acc_ref[...] = part

            @pl.when(kk > 0)
            def _():
                acc_ref[...] += part

            @pl.when(kk == nk - 1)
            def _():
                o_ref[...] = acc_ref[...].astype(o_ref.dtype)

        if n_ride:
            @pl.when((at[0] == steps[0] - 1) & (at[1] == steps[1] - 1) & (at[2] == steps[2] - 1))
            def _():
                exchange(ride_in, ride_out, *ride_sems, wait=True)

    if mode == "nn":
        a_spec = pl.BlockSpec((tm, tk), lambda j, i, kk: (i, kk))
        b_spec = pl.BlockSpec((tk, tn), lambda j, i, kk: (kk, j))
    elif mode == "nt":
        a_spec = pl.BlockSpec((tm, tk), lambda j, i, kk: (i, kk))
        b_spec = pl.BlockSpec((tn, tk), lambda j, i, kk: (j, kk))
    else:
        a_spec = pl.BlockSpec((tk, tm), lambda j, i, kk: (kk, i))
        b_spec = pl.BlockSpec((tk, tn), lambda j, i, kk: (kk, j))
    out = pl.pallas_call(
        body, grid=steps, in_specs=[a_spec, b_spec] + [ANY] * n_ride,
        out_specs=[pl.BlockSpec((tm, tn), lambda j, i, kk: (i, j))] + [ANY] * n_ride,
        out_shape=[jax.ShapeDtypeStruct((m, n), out_dtype)] + arrived,
        scratch_shapes=[pltpu.VMEM((tm, tn), F32)] + (_gather_sems(n_ride) if n_ride else []),
        compiler_params=_params(*(["arbitrary"] * 3 if n_ride else ["parallel", "parallel", "arbitrary"])), name=name,
    )(a, b, *ride)
    return out if n_ride else out[0]


def _row_tile(l):
    return _pick(l, (512, 256, 128))


def _rmsnorm_fwd(x, w, name):
    l, d = x.shape
    tl = _row_tile(l)

    def body(x_ref, w_ref, o_ref):
        xv = x_ref[...]
        r = lax.rsqrt(jnp.mean(xv * xv, axis=-1, keepdims=True) + EPS)
        o_ref[...] = (xv * r * w_ref[...]).astype(o_ref.dtype)

    return pl.pallas_call(
        body, grid=(l // tl,),
        in_specs=[pl.BlockSpec((tl, d), lambda i: (i, 0)), pl.BlockSpec((1, d), lambda i: (0, 0))],
        out_specs=pl.BlockSpec((tl, d), lambda i: (i, 0)),
        out_shape=jax.ShapeDtypeStruct((l, d), BF16), compiler_params=_params("parallel"), name=name,
    )(x, w.reshape(1, d))


def _post_fwd(x, y, w, w_next, name):
    l, d = x.shape
    tl = _row_tile(l)

    def body(x_ref, y_ref, w_ref, wn_ref, o_ref, h_ref):
        yv = y_ref[...]
        r = lax.rsqrt(jnp.mean(yv * yv, axis=-1, keepdims=True) + EPS)
        out = x_ref[...] + yv * r * w_ref[...]
        o_ref[...] = out
        rn = lax.rsqrt(jnp.mean(out * out, axis=-1, keepdims=True) + EPS)
        h_ref[...] = (out * rn * wn_ref[...]).astype(h_ref.dtype)

    row = pl.BlockSpec((tl, d), lambda i: (i, 0))
    vec = pl.BlockSpec((1, d), lambda i: (0, 0))
    return pl.pallas_call(
        body, grid=(l // tl,), in_specs=[row, row, vec, vec], out_specs=[row, row],
        out_shape=[jax.ShapeDtypeStruct((l, d), F32), jax.ShapeDtypeStruct((l, d), BF16)],
        compiler_params=_params("parallel"), name=name,
    )(x, y, w.reshape(1, d), w_next.reshape(1, d))


def _post_loss(x, y, w, t, name):
    l, d = x.shape
    tl = _row_tile(l)
    nt = l // tl

    def body(x_ref, y_ref, w_ref, t_ref, g_ref, dy_ref, ls_ref, dw_ref, acc_ref):
        i = pl.program_id(0)

        @pl.when(i == 0)
        def _():
            ls_ref[...] = jnp.zeros_like(ls_ref)
            acc_ref[...] = jnp.zeros_like(acc_ref)

        yv = y_ref[...]
        r = lax.rsqrt(jnp.mean(yv * yv, axis=-1, keepdims=True) + EPS)
        nrm = yv * r
        e = x_ref[...] + nrm * w_ref[...] - t_ref[...]
        gv = e * (1.0 / d)
        g_ref[...] = gv
        ls_ref[...] += jnp.sum((e * e).reshape(tl // 8, 8, d), axis=0)
        gw = gv * w_ref[...]
        dy_ref[...] = (r * (gw - nrm * jnp.mean(gw * nrm, axis=-1, keepdims=True))).astype(dy_ref.dtype)
        acc_ref[...] += jnp.sum((gv * nrm).reshape(tl // 8, 8, d), axis=0)

        @pl.when(i == nt - 1)
        def _():
            dw_ref[...] = jnp.sum(acc_ref[...], axis=0, keepdims=True)

    row = pl.BlockSpec((tl, d), lambda i: (i, 0))
    vec = pl.BlockSpec((1, d), lambda i: (0, 0))
    return pl.pallas_call(
        body, grid=(nt,), in_specs=[row, row, vec, row],
        out_specs=[row, row, pl.BlockSpec((8, d), lambda i: (0, 0)), vec],
        out_shape=[jax.ShapeDtypeStruct((l, d), F32), jax.ShapeDtypeStruct((l, d), BF16),
                   jax.ShapeDtypeStruct((8, d), F32), jax.ShapeDtypeStruct((1, d), F32)],
        scratch_shapes=[pltpu.VMEM((8, d), F32)], compiler_params=_params("arbitrary"), name=name,
    )(x, y, w.reshape(1, d), t)


def _norm_bwd_chain(dh, x, w_pre, resid, y_prev, w_post_prev, name):
    l, d = x.shape
    tl = _row_tile(l)
    nt = l // tl

    def body(dh_ref, x_ref, wp_ref, r_ref, y_ref, wq_ref, g_ref, dy_ref, dwp_ref, dwq_ref, accp_ref, accq_ref):
        i = pl.program_id(0)

        @pl.when(i == 0)
        def _():
            accp_ref[...] = jnp.zeros_like(accp_ref)
            accq_ref[...] = jnp.zeros_like(accq_ref)

        xv = x_ref[...]
        dhv = dh_ref[...]
        rx = lax.rsqrt(jnp.mean(xv * xv, axis=-1, keepdims=True) + EPS)
        nx = xv * rx
        gw = dhv * wp_ref[...]
        gv = rx * (gw - nx * jnp.mean(gw * nx, axis=-1, keepdims=True)) + r_ref[...]
        g_ref[...] = gv
        accp_ref[...] += jnp.sum((dhv * nx).reshape(tl // 8, 8, d), axis=0)
        yv = y_ref[...]
        ry = lax.rsqrt(jnp.mean(yv * yv, axis=-1, keepdims=True) + EPS)
        ny = yv * ry
        gq = gv * wq_ref[...]
        dy_ref[...] = (ry * (gq - ny * jnp.mean(gq * ny, axis=-1, keepdims=True))).astype(dy_ref.dtype)
        accq_ref[...] += jnp.sum((gv * ny).reshape(tl // 8, 8, d), axis=0)

        @pl.when(i == nt - 1)
        def _():
            dwp_ref[...] = jnp.sum(accp_ref[...], axis=0, keepdims=True)
            dwq_ref[...] = jnp.sum(accq_ref[...], axis=0, keepdims=True)

    row = pl.BlockSpec((tl, d), lambda i: (i, 0))
    vec = pl.BlockSpec((1, d), lambda i: (0, 0))
    return pl.pallas_call(
        body, grid=(nt,), in_specs=[row, row, vec, row, row, vec], out_specs=[row, row, vec, vec],
        out_shape=[jax.ShapeDtypeStruct((l, d), F32), jax.ShapeDtypeStruct((l, d), BF16),
                   jax.ShapeDtypeStruct((1, d), F32), jax.ShapeDtypeStruct((1, d), F32)],
        scratch_shapes=[pltpu.VMEM((8, d), F32), pltpu.VMEM((8, d), F32)],
        compiler_params=_params("arbitrary"), name=name,
    )(dh, x, w_pre.reshape(1, d), resid, y_prev, w_post_prev.reshape(1, d))


def _rmsnorm_bwd(g, y, w, resid, out_dtype, name):
    l, d = y.shape
    tl = _row_tile(l)
    nt = l // tl
    has_resid = resid is not None

    def body(*refs):
        if has_resid:
            g_ref, y_ref, w_ref, r_ref, dy_ref, dw_ref, acc_ref = refs
        else:
            g_ref, y_ref, w_ref, dy_ref, dw_ref, acc_ref = refs
        i = pl.program_id(0)

        @pl.when(i == 0)
        def _():
            acc_ref[...] = jnp.zeros_like(acc_ref)

        yv = y_ref[...]
        gv = g_ref[...].astype(F32)
        r = lax.rsqrt(jnp.mean(yv * yv, axis=-1, keepdims=True) + EPS)
        nrm = yv * r
        gw = gv * w_ref[...]
        dy = r * (gw - nrm * jnp.mean(gw * nrm, axis=-1, keepdims=True))
        if has_resid:
            dy = dy + r_ref[...]
        dy_ref[...] = dy.astype(dy_ref.dtype)
        acc_ref[...] += jnp.sum((gv * nrm).reshape(tl // 8, 8, d), axis=0)

        @pl.when(i == nt - 1)
        def _():
            dw_ref[...] = jnp.sum(acc_ref[...], axis=0, keepdims=True)

    row = pl.BlockSpec((tl, d), lambda i: (i, 0))
    vec = pl.BlockSpec((1, d), lambda i: (0, 0))
    ins = [g, y, w.reshape(1, d)] + ([resid] if has_resid else [])
    return pl.pallas_call(
        body, grid=(nt,), in_specs=[row, row, vec] + ([row] if has_resid else []),
        out_specs=[row, vec],
        out_shape=[jax.ShapeDtypeStruct((l, d), out_dtype), jax.ShapeDtypeStruct((1, d), F32)],
        scratch_shapes=[pltpu.VMEM((8, d), F32)], compiler_params=_params("arbitrary"), name=name,
    )(*ins)


CONV_COLS = 512
HALO = 8
HALO16 = 16
CONV_SUB_ROWS = 64
CONV_SUB_COLS = 256


def _conv_rows(l):
    return _pick(l, (1024, 512, 256, 128))


def _conv_fwd(proj, cw, cb, name):
    l = proj.shape[0]
    tl = _conv_rows(l)
    off = SSM_D_INNER // CONV_COLS

    def body(u_ref, halo_ref, w_ref, b_ref, pre_ref, act_ref, ext_ref):
        i = pl.program_id(1)
        ext_ref[0:HALO, :] = jnp.where(i > 0, halo_ref[...], 0.0)
        ext_ref[HALO:HALO + tl, :] = u_ref[...]
        for r0 in range(0, tl, CONV_SUB_ROWS):
            for c0 in range(0, CONV_COLS, CONV_SUB_COLS):
                cs = slice(c0, c0 + CONV_SUB_COLS)
                ext = ext_ref[r0:r0 + CONV_SUB_ROWS + HALO, cs]
                acc = b_ref[:, cs] + w_ref[SSM_CONV - 1:SSM_CONV, cs] * ext[HALO:]
                for k in range(SSM_CONV - 1):
                    acc = acc + w_ref[k:k + 1, cs] * pltpu.roll(ext, SSM_CONV - 1 - k, 0)[HALO:]
                pre_ref[r0:r0 + CONV_SUB_ROWS, cs] = acc.astype(pre_ref.dtype)
                act_ref[r0:r0 + CONV_SUB_ROWS, cs] = (acc * _sigmoid(acc)).astype(act_ref.dtype)

    hb = tl // HALO
    out = pl.BlockSpec((tl, CONV_COLS), lambda j, i: (i, j))
    return pl.pallas_call(
        body, grid=(SSM_CONV_DIM // CONV_COLS, l // tl),
        in_specs=[pl.BlockSpec((tl, CONV_COLS), lambda j, i: (i, off + j)),
                  pl.BlockSpec((HALO, CONV_COLS), lambda j, i: (jnp.maximum(i * hb - 1, 0), off + j)),
                  pl.BlockSpec((SSM_CONV, CONV_COLS), lambda j, i: (0, j)),
                  pl.BlockSpec((1, CONV_COLS), lambda j, i: (0, j))],
        out_specs=[out, out],
        out_shape=[jax.ShapeDtypeStruct((l, SSM_CONV_DIM), BF16)] * 2,
        scratch_shapes=[pltpu.VMEM((tl + HALO, CONV_COLS), F32)],
        compiler_params=_params("parallel", "arbitrary"), name=name,
    )(proj, proj, cw, cb.reshape(1, SSM_CONV_DIM))


def _conv_bwd(dact, pre, proj, cw, dproj, name):
    l, width = dact.shape
    tl = _conv_rows(l)
    nt = l // tl
    pre_off = 0
    u_off = SSM_D_INNER // CONV_COLS
    hb = tl // HALO
    hb16 = tl // HALO16
    last_hb16 = l // HALO16 - 1

    def body(da_ref, da_h_ref, p_ref, p_h_ref, u_ref, u_h_ref, w_ref, _, du_ref, dw_ref, db_ref, ext_ref, uext_ref):
        i = pl.program_id(1)

        @pl.when(i == 0)
        def _():
            dw_ref[...] = jnp.zeros_like(dw_ref)
            db_ref[...] = jnp.zeros_like(db_ref)

        def dpre_of(da, p):
            s = _sigmoid(p)
            return da * (s * (1.0 + p * (1.0 - s)))

        ext_ref[0:tl, :] = dpre_of(da_ref[...].astype(F32), p_ref[...].astype(F32))
        ext_ref[tl:tl + HALO, :] = jnp.where(
            i < nt - 1, dpre_of(da_h_ref[...].astype(F32)[:HALO], p_h_ref[...].astype(F32)[:HALO]), 0.0)
        uext_ref[0:HALO, :] = jnp.where(i > 0, u_h_ref[...], 0.0)
        uext_ref[HALO:HALO + tl, :] = u_ref[...]
        sub = CONV_SUB_ROWS
        for c0 in range(0, CONV_COLS, CONV_SUB_COLS):
            cs = slice(c0, c0 + CONV_SUB_COLS)
            dws = [jnp.zeros((1, CONV_SUB_COLS), F32) for _ in range(SSM_CONV)]
            dbs = jnp.zeros((1, CONV_SUB_COLS), F32)
            for r0 in range(0, tl, sub):
                dext = ext_ref[r0:r0 + sub + HALO, cs]
                uext = uext_ref[r0:r0 + sub + HALO, cs]
                dp = dext[:sub]
                du = w_ref[SSM_CONV - 1:SSM_CONV, cs] * dp
                dws[SSM_CONV - 1] = dws[SSM_CONV - 1] + jnp.sum(dp * uext[HALO:], axis=0, keepdims=True)
                for k in range(SSM_CONV - 1):
                    j = SSM_CONV - 1 - k
                    du = du + w_ref[k:k + 1, cs] * pltpu.roll(dext, sub + HALO - j, 0)[:sub]
                    dws[k] = dws[k] + jnp.sum(dp * pltpu.roll(uext, j, 0)[HALO:], axis=0, keepdims=True)
                dbs = dbs + jnp.sum(dp, axis=0, keepdims=True)
                du_ref[r0:r0 + sub, cs] = du.astype(du_ref.dtype)
            for k in range(SSM_CONV):
                dw_ref[k:k + 1, cs] += dws[k]
            db_ref[:, cs] += dbs

    return pl.pallas_call(
        body, grid=(width // CONV_COLS, nt),
        in_specs=[pl.BlockSpec((tl, CONV_COLS), lambda j, i: (i, j)),
                  pl.BlockSpec((HALO16, CONV_COLS), lambda j, i: (jnp.minimum((i + 1) * hb16, last_hb16), j)),
                  pl.BlockSpec((tl, CONV_COLS), lambda j, i: (i, pre_off + j)),
                  pl.BlockSpec((HALO16, CONV_COLS), lambda j, i: (jnp.minimum((i + 1) * hb16, last_hb16), pre_off + j)),
                  pl.BlockSpec((tl, CONV_COLS), lambda j, i: (i, u_off + j)),
                  pl.BlockSpec((HALO, CONV_COLS), lambda j, i: (jnp.maximum(i * hb - 1, 0), u_off + j)),
                  pl.BlockSpec((SSM_CONV, CONV_COLS), lambda j, i: (0, pre_off + j)),
                  pl.BlockSpec(memory_space=pl.ANY)],
        out_specs=[pl.BlockSpec((tl, CONV_COLS), lambda j, i: (i, u_off + j)),
                   pl.BlockSpec((SSM_CONV, CONV_COLS), lambda j, i: (0, j)),
                   pl.BlockSpec((1, CONV_COLS), lambda j, i: (0, j))],
        out_shape=[jax.ShapeDtypeStruct(dproj.shape, dproj.dtype), jax.ShapeDtypeStruct((SSM_CONV, width), F32),
                   jax.ShapeDtypeStruct((1, width), F32)],
        scratch_shapes=[pltpu.VMEM((tl + HALO, CONV_COLS), F32), pltpu.VMEM((tl + HALO, CONV_COLS), F32)],
        input_output_aliases={7: 0}, compiler_params=_params("parallel", "arbitrary"), name=name,
    )(dact, dact, pre, pre, proj, proj, cw, dproj)


DT_COL_BLOCK = (SSM_D_INNER + SSM_CONV_DIM) // SSM_DT_PAD


def _split3(v):
    hi = v.astype(BF16)
    rest = v - hi.astype(F32)
    mid = rest.astype(BF16)
    lo = (rest - mid.astype(F32)).astype(BF16)
    return hi, mid, lo


def _ssd_prep(proj, bias, alog_lanes, name):
    l = proj.shape[0]
    nc = l // SSM_CHUNK
    head_dim_log2 = SSM_HEAD_DIM.bit_length() - 1

    def body(p_ref, b_ref, al_ref, dtb_ref, acsb_ref, dtr_ref, acsr_ref):
        v = p_ref[...] + b_ref[...]
        dt = jnp.maximum(v, 0.0) + jnp.log1p(jnp.exp(-jnp.abs(v)))
        head_of_lane = lax.shift_right_logical(lax.broadcasted_iota(jnp.int32, (SSM_DT_PAD, SSM_D_INNER), 1), head_dim_log2)
        spread = (head_of_lane == lax.broadcasted_iota(jnp.int32, (SSM_DT_PAD, SSM_D_INNER), 0)).astype(BF16)
        dtb = sum(jnp.dot(piece, spread, preferred_element_type=F32) for piece in _split3(dt)[:2])
        dtb_ref[...] = dtb
        ri = lax.broadcasted_iota(jnp.int32, (SSM_CHUNK, SSM_CHUNK), 0)
        cj = lax.broadcasted_iota(jnp.int32, (SSM_CHUNK, SSM_CHUNK), 1)
        tri = (ri >= cj).astype(BF16)
        acsb = sum(jnp.dot(tri, piece, preferred_element_type=F32) for piece in _split3(dtb * (-jnp.exp(al_ref[...]))))
        acsb_ref[...] = acsb
        gp = SSM_HPG * SSM_HEAD_DIM
        lane = lax.broadcasted_iota(jnp.int32, (SSM_HPG, gp), 1)
        pick = (lane == lax.broadcasted_iota(jnp.int32, (SSM_HPG, gp), 0) * SSM_HEAD_DIM).astype(BF16)
        for g in range(SSM_GROUPS):
            cols = slice(g * gp, (g + 1) * gp)
            dtr_ref[g] = sum(lax.dot_general(pick, piece, NT_DIMS, preferred_element_type=F32)
                             for piece in _split3(dtb[:, cols]))
            acsr_ref[g] = sum(lax.dot_general(pick, piece, NT_DIMS, preferred_element_type=F32)
                              for piece in _split3(acsb[:, cols]))

    rows = pl.BlockSpec((SSM_GROUPS, SSM_HPG, SSM_CHUNK), lambda c: (0, 0, c))
    dense = pl.BlockSpec((SSM_CHUNK, SSM_D_INNER), lambda c: (c, 0))
    return pl.pallas_call(
        body, grid=(nc,),
        in_specs=[pl.BlockSpec((SSM_CHUNK, SSM_DT_PAD), lambda c: (c, DT_COL_BLOCK)),
                  pl.BlockSpec((1, SSM_DT_PAD), lambda c: (0, 0)),
                  pl.BlockSpec((1, SSM_D_INNER), lambda c: (0, 0))],
        out_specs=[dense, dense, rows, rows],
        out_shape=[jax.ShapeDtypeStruct((l, SSM_D_INNER), F32), jax.ShapeDtypeStruct((l, SSM_D_INNER), F32),
                   jax.ShapeDtypeStruct((SSM_GROUPS, SSM_HPG, l), F32),
                   jax.ShapeDtypeStruct((SSM_GROUPS, SSM_HPG, l), F32)],
        compiler_params=_params("parallel"), name=name,
    )(proj, bias, alog_lanes)


def _dt_bwd(ddt, proj, bias, dproj, name):
    l = proj.shape[0]
    tl = _row_tile(l)

    def body(g_ref, p_ref, b_ref, _, o_ref, db_ref):
        @pl.when(pl.program_id(0) == 0)
        def _():
            db_ref[...] = jnp.zeros_like(db_ref)

        d = g_ref[...] * _sigmoid(p_ref[...] + b_ref[...])
        o_ref[...] = d.astype(o_ref.dtype)
        db_ref[...] += jnp.sum(d, axis=0, keepdims=True)

    return pl.pallas_call(
        body, grid=(l // tl,),
        in_specs=[pl.BlockSpec((tl, SSM_DT_PAD), lambda i: (i, 0)),
                  pl.BlockSpec((tl, SSM_DT_PAD), lambda i: (i, DT_COL_BLOCK)),
                  pl.BlockSpec((1, SSM_DT_PAD), lambda i: (0, 0)),
                  pl.BlockSpec(memory_space=pl.ANY)],
        out_specs=[pl.BlockSpec((tl, SSM_DT_PAD), lambda i: (i, DT_COL_BLOCK)),
                   pl.BlockSpec((1, SSM_DT_PAD), lambda i: (0, 0))],
        out_shape=[jax.ShapeDtypeStruct(dproj.shape, dproj.dtype), jax.ShapeDtypeStruct((1, SSM_DT_PAD), F32)],
        input_output_aliases={3: 0}, compiler_params=_params("arbitrary"), name=name,
    )(ddt, proj, bias, dproj)


GP = SSM_HPG * SSM_HEAD_DIM
HEAD_DIM_LOG2 = SSM_HEAD_DIM.bit_length() - 1
CHUNK_LOG2 = SSM_CHUNK.bit_length() - 1
GPS = 8
B_BLOCK0 = SSM_D_INNER // SSM_STATE
C_BLOCK0 = (SSM_D_INNER + SSM_BC_DIM) // SSM_STATE


def _chunk_iotas():
    ri = lax.broadcasted_iota(jnp.int32, (SSM_CHUNK, SSM_CHUNK), 0)
    cj = lax.broadcasted_iota(jnp.int32, (SSM_CHUNK, SSM_CHUNK), 1)
    return ri, cj


def _head_decay(acsb, acs_r, r, ri, cj):
    pair = acsb[:, (r // 2) * LANES:(r // 2 + 1) * LANES]
    mine_low = r % 2 == 0
    lane = lax.broadcasted_iota(jnp.int32, (1, LANES), 1)
    col = jnp.where((lane < SSM_HEAD_DIM) == mine_low, pair, pltpu.roll(pair, SSM_HEAD_DIM, 1))
    return jnp.exp(jnp.where(ri >= cj, col - acs_r[r:r + 1, :], NEG_INF))


def _head_masked_rows(v, dtype):
    head_of_lane = lax.shift_right_logical(lax.broadcasted_iota(jnp.int32, (1, GP), 1), HEAD_DIM_LOG2)
    return jnp.concatenate([jnp.where(head_of_lane == r, v, 0.0).astype(dtype) for r in range(SSM_HPG)], axis=0)


def _ssd_fwd(xbc, dtb, acsb, acs_r, d_lanes, proj, gate_w, name, ride=()):
    l = xbc.shape[0]
    nc = l // SSM_CHUNK
    assert GPS == SSM_GROUPS

    n_ride = len(ride)

    def body(*refs):
        x_ref, b_ref, c_ref, dtb_ref, acsb_ref, acsr_ref, d_ref, z_ref, gw_ref = refs[:9]
        ride_in = refs[9:9 + n_ride]
        y_ref, act_ref, hin_ref = refs[9 + n_ride:12 + n_ride]
        ride_out = refs[12 + n_ride:12 + 2 * n_ride]
        h_ref = refs[12 + 2 * n_ride]
        ride_sems = refs[13 + 2 * n_ride:]
        c = pl.program_id(0)
        if n_ride:
            @pl.when(c == 0)
            def _():
                _gather_between_chips(ride_in, ride_out, *ride_sems, wait=False)

            @pl.when(c == nc - 1)
            def _():
                _gather_between_chips(ride_in, ride_out, *ride_sems, wait=True)

        ri, cj = _chunk_iotas()
        for k in range(GPS):
            g = k
            cols = slice(k * GP, (k + 1) * GP)
            ncols = slice(k * SSM_STATE, (k + 1) * SSM_STATE)

            @pl.when(c == 0)
            def _():
                h_ref[g] = jnp.zeros((SSM_STATE, GP), F32)

            xv = x_ref[:, cols].astype(F32)
            bb = b_ref[:, ncols].astype(BF16)
            cb16 = c_ref[:, ncols].astype(BF16)
            acs_v = acsb_ref[:, cols]
            acs_r_v = acsr_ref[k]
            lastb = acs_v[SSM_CHUNK - 1:SSM_CHUNK, :]
            xd = xv * dtb_ref[:, cols]
            cb = lax.dot_general(cb16, bb, NT_DIMS, preferred_element_type=F32)
            hin = h_ref[g]
            hin_ref[0, k] = hin
            yoff = jnp.dot(cb16, hin.astype(BF16), preferred_element_type=F32)
            ms = [(cb * _head_decay(acs_v, acs_r_v, r, ri, cj)).astype(BF16) for r in range(SSM_HPG)]
            ydiag = jnp.dot(jnp.concatenate(ms, axis=1), _head_masked_rows(xd, BF16), preferred_element_type=F32)
            y_ref[:, cols] = ydiag + jnp.exp(acs_v) * yoff + d_ref[k] * xv
            h_ref[g] = hin * jnp.exp(lastb) + _bdot_tn(bb, xd * jnp.exp(lastb - acs_v))
        z = z_ref[...]
        yg = y_ref[...] * (z * _sigmoid(z))
        r = lax.rsqrt(jnp.mean(yg * yg, axis=-1, keepdims=True) + EPS)
        act_ref[...] = (yg * r * gw_ref[...]).astype(act_ref.dtype)

    lanes = pl.BlockSpec((SSM_CHUNK, SSM_D_INNER), lambda c: (c, 0))
    return pl.pallas_call(
        body, grid=(nc,),
        in_specs=[lanes,
                  pl.BlockSpec((SSM_CHUNK, SSM_BC_DIM), lambda c: (c, B_BLOCK0 // GPS)),
                  pl.BlockSpec((SSM_CHUNK, SSM_BC_DIM), lambda c: (c, C_BLOCK0 // GPS)),
                  lanes, lanes,
                  pl.BlockSpec((SSM_GROUPS, SSM_HPG, SSM_CHUNK), lambda c: (0, 0, c)),
                  pl.BlockSpec((SSM_GROUPS, 1, GP), lambda c: (0, 0, 0)),
                  lanes, pl.BlockSpec((1, SSM_D_INNER), lambda c: (0, 0))] + [ANY] * n_ride,
        out_specs=[lanes, lanes, pl.BlockSpec((1, SSM_GROUPS, SSM_STATE, GP), lambda c: (c, 0, 0, 0))] + [ANY] * n_ride,
        out_shape=[jax.ShapeDtypeStruct((l, SSM_D_INNER), F32), jax.ShapeDtypeStruct((l, SSM_D_INNER), BF16),
                   jax.ShapeDtypeStruct((nc, SSM_GROUPS, SSM_STATE, GP), F32)]
        + [jax.ShapeDtypeStruct((N_CHIPS,) + a.shape, a.dtype) for a in ride],
        scratch_shapes=[pltpu.VMEM((SSM_GROUPS, SSM_STATE, GP), F32)] + (_gather_sems(n_ride) if n_ride else []),
        compiler_params=_params("arbitrary"), name=name,
    )(xbc, xbc, xbc, dtb, acsb, acs_r, d_lanes, proj, gate_w.reshape(1, SSM_D_INNER), *ride)


def _ssd_bwd(xbc, dtb, acsb, dtr, acs_r, a_log, d_lanes, hin, dact, y, proj, gate_w, name):
    l = xbc.shape[0]
    nc = l // SSM_CHUNK

    def body(x_ref, b_ref, c_ref, dtb_ref, acsb_ref, dtr_ref, acsr_ref, alc_ref, d_ref, hin_ref,
             dact_ref, y_ref, z_ref, gw_ref,
             dxbc_ref, ddt_ref, dal_ref, dd_ref, dproj_ref, dgw_ref, dh_ref, dy_ref, acc_ref):
        c = pl.program_id(0)
        dx_ref = dxbc_ref.at[:, 0:SSM_D_INNER]
        db_ref = dxbc_ref.at[:, SSM_D_INNER:SSM_D_INNER + SSM_BC_DIM]
        dc_ref = dxbc_ref.at[:, SSM_D_INNER + SSM_BC_DIM:SSM_CONV_DIM]

        @pl.when(c == 0)
        def _():
            dal_ref[...] = jnp.zeros_like(dal_ref)
            dd_ref[...] = jnp.zeros_like(dd_ref)
            acc_ref[...] = jnp.zeros_like(acc_ref)

        z = z_ref[...]
        yv = y_ref[...]
        s = _sigmoid(z)
        sz = z * s
        yg = yv * sz
        r = lax.rsqrt(jnp.mean(yg * yg, axis=-1, keepdims=True) + EPS)
        nrm = yg * r
        gv = dact_ref[...]
        gw = gv * gw_ref[...]
        dyg = r * (gw - nrm * jnp.mean(gw * nrm, axis=-1, keepdims=True))
        dy_ref[...] = dyg * sz
        dproj_ref[...] = (dyg * yv * (s * (1.0 + z * (1.0 - s)))).astype(dproj_ref.dtype)
        acc_ref[...] += jnp.sum((gv * nrm).reshape(SSM_CHUNK // 8, 8, SSM_D_INNER), axis=0)

        @pl.when(c == nc - 1)
        def _():
            dgw_ref[...] = jnp.sum(acc_ref[...], axis=0, keepdims=True)

        for k in range(GPS):
            one_group(c, k, k, x_ref, b_ref, c_ref, dtb_ref, acsb_ref, dtr_ref, acsr_ref, alc_ref, d_ref,
                      hin_ref, dy_ref, dx_ref, db_ref, dc_ref, ddt_ref, dal_ref, dd_ref, dh_ref)

    def one_group(c, g, k, x_ref, b_ref, c_ref, dtb_ref, acsb_ref, dtr_ref, acsr_ref, alc_ref, d_ref, hin_ref, dy_ref,
                  dx_ref, db_ref, dc_ref, ddt_ref, dal_ref, dd_ref, dh_ref):
        cols = slice(k * GP, (k + 1) * GP)
        ncols = slice(k * SSM_STATE, (k + 1) * SSM_STATE)

        @pl.when(c == 0)
        def _():
            dh_ref[g] = jnp.zeros((SSM_STATE, GP), F32)

        xv = x_ref[:, cols].astype(F32)
        dyv = dy_ref[:, cols]
        bb = b_ref[:, ncols].astype(BF16)
        cb16 = c_ref[:, ncols].astype(BF16)
        dtb = dtb_ref[:, cols]
        acsb = acsb_ref[:, cols]
        dtr_v = dtr_ref[k]
        acs_r = acsr_ref[k]
        a_col = -jnp.exp(alc_ref[k])
        ri, cj = _chunk_iotas()
        head_of_lane = lax.shift_right_logical(lax.broadcasted_iota(jnp.int32, (SSM_HPG, GP), 1), HEAD_DIM_LOG2)
        ind_t = (head_of_lane == lax.broadcasted_iota(jnp.int32, (SSM_HPG, GP), 0)).astype(BF16)
        lastb = acsb[SSM_CHUNK - 1:SSM_CHUNK, :]
        ecb = jnp.exp(acsb)
        dteb = jnp.exp(lastb - acsb)
        xd = xv * dtb
        xw = xd * dteb
        cb = lax.dot_general(cb16, bb, NT_DIMS, preferred_element_type=F32)
        hin_v = hin_ref[0, k]
        dhn = dh_ref[g]
        h16 = hin_v.astype(BF16)
        dh16 = dhn.astype(BF16)
        ch = jnp.dot(cb16, h16, preferred_element_type=F32)
        bdh = jnp.dot(bb, dh16, preferred_element_type=F32)
        dym = _head_masked_rows(dyv, BF16)
        g_all = lax.dot_general(dym, xd.astype(BF16), NT_DIMS, preferred_element_type=F32)
        gl_sum = jnp.zeros((SSM_CHUNK, SSM_CHUNK), F32)
        ms, qs = [], []
        for r in range(SSM_HPG):
            decay = _head_decay(acsb, acs_r, r, ri, cj)
            gl = g_all[r * SSM_CHUNK:(r + 1) * SSM_CHUNK] * decay
            gl_sum = gl_sum + gl
            ms.append((cb * decay).astype(BF16))
            qs.append((gl * cb).astype(BF16))
        dxd = lax.dot_general(jnp.concatenate(ms, axis=0), dym, TN_DIMS, preferred_element_type=F32) + dteb * bdh
        cum = jnp.dot(jnp.concatenate(qs, axis=0), (ri < cj).astype(BF16), preferred_element_type=F32)
        sub4 = lax.broadcasted_iota(jnp.int32, (SSM_HPG, 1), 0)
        da = jnp.zeros((SSM_HPG, SSM_CHUNK), F32)
        for r in range(SSM_HPG):
            rect = jnp.sum(jnp.where(ri >= cj, cum[r * SSM_CHUNK:(r + 1) * SSM_CHUNK], 0.0), axis=0, keepdims=True)
            da = da + jnp.where(sub4 == r, rect, 0.0)
        z2 = xw * bdh
        sub8 = lax.broadcasted_iota(jnp.int32, (8, 1), 0)
        col_sums = (jnp.where(sub8 == 0, jnp.sum(z2, axis=0, keepdims=True), 0.0)
                    + jnp.where(sub8 == 1, jnp.sum(dhn * hin_v, axis=0, keepdims=True), 0.0)
                    + jnp.where(sub8 == 2, jnp.sum(dyv * xv, axis=0, keepdims=True), 0.0))
        summands = jnp.concatenate([dyv * ecb * ch - z2, dxd * xv, col_sums], axis=0)
        sums = sum(lax.dot_general(ind_t, piece, NT_DIMS, preferred_element_type=F32) for piece in _split3(summands)[:2])
        per_pos = sums[:, :2 * SSM_CHUNK]
        totals = sums[:, 2 * SSM_CHUNK:]
        e_last = totals[:, 0:1] + jnp.exp(acs_r[:, SSM_CHUNK - 1:SSM_CHUNK]) * totals[:, 1:2]
        da = (da + e_last + jnp.dot(per_pos[:, :SSM_CHUNK], (ri >= cj).astype(F32), preferred_element_type=F32,
                                    precision=lax.Precision.HIGHEST))
        ddt_ref[k] = a_col * da + per_pos[:, SSM_CHUNK:]
        dal_ref[g] += a_col * jnp.sum(da * dtr_v, axis=1, keepdims=True)
        dd_ref[g] += totals[:, 2:3]
        dx_ref[:, cols] = (dxd * dtb + d_ref[k] * dyv).astype(dx_ref.dtype)
        w16 = (ecb * dyv).astype(BF16)
        xw16 = xw.astype(BF16)
        gl16 = gl_sum.astype(BF16)
        dc_ref[:, ncols] = (jnp.dot(gl16, bb, preferred_element_type=F32)
                            + lax.dot_general(w16, h16, NT_DIMS, preferred_element_type=F32)).astype(dc_ref.dtype)
        db_ref[:, ncols] = (lax.dot_general(gl16, cb16, TN_DIMS, preferred_element_type=F32)
                            + lax.dot_general(xw16, dh16, NT_DIMS, preferred_element_type=F32)).astype(db_ref.dtype)
        dh_ref[g] = dhn * jnp.exp(lastb) + lax.dot_general(cb16, w16, TN_DIMS, preferred_element_type=F32)

    def rev(c):
        return nc - 1 - c

    small = pl.BlockSpec((SSM_GROUPS, SSM_HPG, 1), lambda c: (0, 0, 0))
    lanes = pl.BlockSpec((SSM_CHUNK, SSM_D_INNER), lambda c: (rev(c), 0))
    rows = pl.BlockSpec((SSM_GROUPS, SSM_HPG, SSM_CHUNK), lambda c: (0, 0, rev(c)))
    vec = pl.BlockSpec((1, SSM_D_INNER), lambda c: (0, 0))
    return pl.pallas_call(
        body, grid=(nc,),
        in_specs=[lanes,
                  pl.BlockSpec((SSM_CHUNK, SSM_BC_DIM), lambda c: (rev(c), B_BLOCK0 // GPS)),
                  pl.BlockSpec((SSM_CHUNK, SSM_BC_DIM), lambda c: (rev(c), C_BLOCK0 // GPS)),
                  lanes, lanes, rows, rows,
                  pl.BlockSpec((SSM_GROUPS, SSM_HPG, 1), lambda c: (0, 0, 0)),
                  pl.BlockSpec((SSM_GROUPS, 1, GP), lambda c: (0, 0, 0)),
                  pl.BlockSpec((1, SSM_GROUPS, SSM_STATE, GP), lambda c: (rev(c), 0, 0, 0)),
                  lanes, lanes, lanes, vec],
        out_specs=[pl.BlockSpec((SSM_CHUNK, SSM_CONV_DIM), lambda c: (rev(c), 0)),
                   rows, small, small, lanes, vec],
        out_shape=[jax.ShapeDtypeStruct((l, SSM_CONV_DIM), BF16), jax.ShapeDtypeStruct((SSM_GROUPS, SSM_HPG, l), F32),
                   jax.ShapeDtypeStruct((SSM_GROUPS, SSM_HPG, 1), F32),
                   jax.ShapeDtypeStruct((SSM_GROUPS, SSM_HPG, 1), F32),
                   jax.ShapeDtypeStruct((l, SSM_IN_PAD), BF16), jax.ShapeDtypeStruct((1, SSM_D_INNER), F32)],
        scratch_shapes=[pltpu.VMEM((SSM_GROUPS, SSM_STATE, GP), F32), pltpu.VMEM((SSM_CHUNK, SSM_D_INNER), F32),
                        pltpu.VMEM((8, SSM_D_INNER), F32)],
        compiler_params=_params("arbitrary"), name=name,
    )(xbc, xbc, xbc, dtb, acsb, dtr, acs_r, a_log.reshape(SSM_GROUPS, SSM_HPG, 1), d_lanes, hin, dact, y, proj,
      gate_w.reshape(1, SSM_D_INNER))


LANES = 128
ROPE_Q_CHUNKS = ATT_WIDTH // LANES
ROPE_K_CHUNKS = ATT_KV_WIDTH // LANES


def _rope_tables(positions):
    inv = ROPE_THETA ** (-jnp.arange(0, ROPE_DIM, 2, dtype=F32) / ROPE_DIM)
    ang = positions.astype(F32)[:, None] * inv
    cos, sin = jnp.cos(ang), jnp.sin(ang)
    l = positions.shape[0]
    rest = ATT_HEAD_DIM - ROPE_DIM
    ones, zeros = jnp.ones((l, rest), F32), jnp.zeros((l, rest), F32)
    z8 = jnp.zeros((l, ROPE_HALF), F32)
    cos_f = jnp.concatenate([cos, cos, ones], axis=1)
    sin_a = jnp.concatenate([-sin, z8, zeros], axis=1)
    sin_b = jnp.concatenate([z8, sin, zeros], axis=1)
    reps = LANES // ATT_HEAD_DIM
    return tuple(jnp.tile(t, (1, reps)) for t in (cos_f, sin_a, sin_b))


ATT_QKV4 = 3 * ATT_WIDTH


def _both_halves(chunk):
    lane = lax.broadcasted_iota(jnp.int32, (1, LANES), 1)
    swapped = pltpu.roll(chunk, ATT_HEAD_DIM, 1)
    return jnp.where(lane < ATT_HEAD_DIM, chunk, swapped), jnp.where(lane < ATT_HEAD_DIM, swapped, chunk)


def _rope_fwd(proj, tables, name):
    l = proj.shape[0]
    tl = _pick(l, (256, 128))

    def body(p_ref, c_ref, sa_ref, sb_ref, o_ref):
        cos_f, sin_a, sin_b = c_ref[...], sa_ref[...], sb_ref[...]

        def rope(t):
            return t * cos_f + pltpu.roll(t, LANES - ROPE_HALF, 1) * sin_a + pltpu.roll(t, ROPE_HALF, 1) * sin_b

        for k in range(ROPE_Q_CHUNKS):
            sl = slice(k * LANES, (k + 1) * LANES)
            o_ref[:, sl] = (rope(p_ref[:, sl]) * Q_SCALE).astype(o_ref.dtype)
        for part in range(2):
            for k in range(ROPE_K_CHUNKS):
                src = ATT_WIDTH + part * ATT_KV_WIDTH + k * LANES
                t = p_ref[:, src:src + LANES]
                if part == 0:
                    t = rope(t)
                for head, dup in enumerate(_both_halves(t.astype(o_ref.dtype))):
                    dst = (1 + part) * ATT_WIDTH + (2 * k + head) * ATT_GQA * ATT_HEAD_DIM
                    o_ref[:, dst:dst + LANES] = dup
                    o_ref[:, dst + LANES:dst + 2 * LANES] = dup

    tab = pl.BlockSpec((tl, LANES), lambda i: (i, 0))
    return pl.pallas_call(
        body, grid=(l // tl,), in_specs=[pl.BlockSpec((tl, ATT_IN_DIM), lambda i: (i, 0)), tab, tab, tab],
        out_specs=pl.BlockSpec((tl, ATT_QKV4), lambda i: (i, 0)),
        out_shape=jax.ShapeDtypeStruct((l, ATT_QKV4), BF16), compiler_params=_params("parallel"), name=name,
    )(proj, *tables)


def _rope_bwd(dq, dk4, dv4, dgate, tables, name):
    l = dq.shape[0]
    tl = _pick(l, (256, 128))

    def body(dq_ref, dk_ref, dv_ref, dg_ref, c_ref, sa_ref, sb_ref, o_ref):
        cos_f, sin_a, sin_b = c_ref[...], sa_ref[...], sb_ref[...]
        lane = lax.broadcasted_iota(jnp.int32, (1, LANES), 1)

        def unrope(t):
            return t * cos_f + pltpu.roll(t * sin_a, ROPE_HALF, 1) + pltpu.roll(t * sin_b, LANES - ROPE_HALF, 1)

        def head_total(ref, kvh):
            base = kvh * ATT_GQA * ATT_HEAD_DIM
            s = ref[:, base:base + LANES] + ref[:, base + LANES:base + 2 * LANES]
            return s + pltpu.roll(s, ATT_HEAD_DIM, 1)

        for k in range(ROPE_Q_CHUNKS):
            sl = slice(k * LANES, (k + 1) * LANES)
            o_ref[:, sl] = unrope(dq_ref[:, sl] * Q_SCALE).astype(o_ref.dtype)
        for k in range(ROPE_K_CHUNKS):
            dk = jnp.where(lane < ATT_HEAD_DIM, head_total(dk_ref, 2 * k), head_total(dk_ref, 2 * k + 1))
            dv = jnp.where(lane < ATT_HEAD_DIM, head_total(dv_ref, 2 * k), head_total(dv_ref, 2 * k + 1))
            o_ref[:, ATT_WIDTH + k * LANES:ATT_WIDTH + (k + 1) * LANES] = unrope(dk).astype(o_ref.dtype)
            at = ATT_WIDTH + ATT_KV_WIDTH + k * LANES
            o_ref[:, at:at + LANES] = dv.astype(o_ref.dtype)
        o_ref[:, ATT_QKV:ATT_IN_DIM] = dg_ref[...].astype(o_ref.dtype)

    tab = pl.BlockSpec((tl, LANES), lambda i: (i, 0))
    wide = pl.BlockSpec((tl, ATT_WIDTH), lambda i: (i, 0))
    return pl.pallas_call(
        body, grid=(l // tl,), in_specs=[wide, wide, wide, wide, tab, tab, tab],
        out_specs=pl.BlockSpec((tl, ATT_IN_DIM), lambda i: (i, 0)),
        out_shape=jax.ShapeDtypeStruct((l, ATT_IN_DIM), BF16), compiler_params=_params("parallel"), name=name,
    )(dq, dk4, dv4, dgate, *tables)


GATE_HALF = ATT_WIDTH // 2
GATE_COL_BLOCK = ATT_QKV // GATE_HALF


ATT_STACK = ATT_GQA * ATT_BLOCK
BLOCK_LOG2 = ATT_BLOCK.bit_length() - 1


def _stack_masks(n):
    ri = lax.broadcasted_iota(jnp.int32, (ATT_STACK, ATT_BLOCK), 0) & (ATT_BLOCK - 1)
    cj = lax.broadcasted_iota(jnp.int32, (ATT_STACK, ATT_BLOCK), 1)
    return (cj > ri) & (n > 0), cj <= ri


def _stack_sinks(sink_ref, kvh):
    blk = lax.shift_right_logical(lax.broadcasted_iota(jnp.int32, (ATT_STACK, 1), 0), BLOCK_LOG2)
    col = jnp.zeros((ATT_STACK, 1), F32)
    for r in range(ATT_GQA):
        col = jnp.where(blk == r, sink_ref[kvh * ATT_GQA + r], col)
    return col


def _stack_fold(stack):
    head_of_lane = lax.shift_right_logical(lax.broadcasted_iota(jnp.int32, (1, GP), 1), HEAD_DIM_LOG2)
    out = jnp.zeros((ATT_BLOCK, GP), F32)
    for r in range(ATT_GQA):
        out = jnp.where(head_of_lane == r, stack[r * ATT_BLOCK:(r + 1) * ATT_BLOCK], out)
    return out


def _attn_fwd(qkv, proj, sinks, name):
    l = qkv.shape[0]
    nb = l // ATT_BLOCK

    def body(sink_ref, q_ref, kp_ref, kc_ref, vp_ref, vc_ref, g0_ref, g1_ref, og_ref, o_ref, lse_ref):
        n = pl.program_id(0)
        mask_p, mask_c = _stack_masks(n)
        ones = jnp.ones((ATT_BLOCK, LANES), BF16)
        for kvh in range(ATT_KV_HEADS):
            cols = slice(kvh * GP, (kvh + 1) * GP)
            q_stack = _head_masked_rows(q_ref[:, cols], BF16)
            sp = jnp.where(mask_p, lax.dot_general(q_stack, kp_ref[:, cols], NT_DIMS, preferred_element_type=F32), NEG_INF)
            sc = jnp.where(mask_c, lax.dot_general(q_stack, kc_ref[:, cols], NT_DIMS, preferred_element_type=F32), NEG_INF)
            sink = _stack_sinks(sink_ref, kvh)
            m = jnp.maximum(jnp.max(jnp.maximum(sp, sc), axis=1, keepdims=True), sink)
            pp = jnp.exp(sp - m).astype(BF16)
            pc = jnp.exp(sc - m).astype(BF16)
            acc = (jnp.dot(pp, jnp.concatenate([vp_ref[:, cols], ones], axis=1), preferred_element_type=F32)
                   + jnp.dot(pc, jnp.concatenate([vc_ref[:, cols], ones], axis=1), preferred_element_type=F32))
            den = acc[:, GP:] + jnp.exp(sink - m)
            inv = 1.0 / den
            o_ref[:, cols] = _stack_fold(acc[:, :GP] * jnp.concatenate([inv, inv], axis=1))
            lse = m + jnp.log(den)
            lse_ref[:, cols] = _stack_fold(jnp.concatenate([lse, lse], axis=1))
        for half, g_ref in enumerate((g0_ref, g1_ref)):
            sl = slice(half * GATE_HALF, (half + 1) * GATE_HALF)
            gate = g_ref[...]
            og_ref[:, sl] = (o_ref[:, sl] * (gate * _sigmoid(gate))).astype(og_ref.dtype)

    def prev(n):
        return jnp.maximum(n - 1, 0)

    wide = pl.BlockSpec((ATT_BLOCK, ATT_WIDTH), lambda n: (n, 0))
    return pl.pallas_call(
        body, grid=(nb,),
        in_specs=[pl.BlockSpec(memory_space=pltpu.SMEM), wide,
                  pl.BlockSpec((ATT_BLOCK, ATT_WIDTH), lambda n: (prev(n), 1)),
                  pl.BlockSpec((ATT_BLOCK, ATT_WIDTH), lambda n: (n, 1)),
                  pl.BlockSpec((ATT_BLOCK, ATT_WIDTH), lambda n: (prev(n), 2)),
                  pl.BlockSpec((ATT_BLOCK, ATT_WIDTH), lambda n: (n, 2)),
                  pl.BlockSpec((ATT_BLOCK, GATE_HALF), lambda n: (n, GATE_COL_BLOCK)),
                  pl.BlockSpec((ATT_BLOCK, GATE_HALF), lambda n: (n, GATE_COL_BLOCK + 1))],
        out_specs=[wide, wide, wide],
        out_shape=[jax.ShapeDtypeStruct((l, ATT_WIDTH), BF16), jax.ShapeDtypeStruct((l, ATT_WIDTH), F32),
                   jax.ShapeDtypeStruct((l, ATT_WIDTH), F32)],
        compiler_params=_params("parallel"), name=name,
    )(sinks, qkv, qkv, qkv, qkv, qkv, proj, proj)


def _attn_bwd(qkv, proj, sinks, o, lse, dog, name, ride=()):
    l = qkv.shape[0]
    nb = l // ATT_BLOCK
    n_ride = len(ride)

    def body(*refs):
        sink_ref, q_ref, kp_ref, kc_ref, vp_ref, vc_ref, g0_ref, g1_ref, o_ref, lse_ref, dog_ref = refs[:11]
        ride_in = refs[11:11 + n_ride]
        dq_ref, dk_ref, dv_ref, dg_ref, ds_ref = refs[11 + n_ride:16 + n_ride]
        ride_out = refs[16 + n_ride:16 + 2 * n_ride]
        ck_ref, cv_ref, do_ref = refs[16 + 2 * n_ride:19 + 2 * n_ride]
        ride_sems = refs[19 + 2 * n_ride:]
        n = pl.program_id(0)

        @pl.when(n == 0)
        def _():
            ds_ref[...] = jnp.zeros_like(ds_ref)
            ck_ref[...] = jnp.zeros_like(ck_ref)
            cv_ref[...] = jnp.zeros_like(cv_ref)
            if n_ride:
                _scatter_between_chips(ride_in, ride_out, *ride_sems, wait=False)

        @pl.when(n == nb)
        def _():
            dk_ref[...] = ck_ref[...]
            dv_ref[...] = cv_ref[...]
            if n_ride:
                _scatter_between_chips(ride_in, ride_out, *ride_sems, wait=True)

        @pl.when(n < nb)
        def _():
            mask_p, mask_c = _stack_masks(n)
            lane = lax.broadcasted_iota(jnp.int32, (1, ATT_Q_HEADS), 1)
            for half, g_ref in enumerate((g0_ref, g1_ref)):
                sl = slice(half * GATE_HALF, (half + 1) * GATE_HALF)
                gate = g_ref[...]
                s = _sigmoid(gate)
                dogv = dog_ref[:, sl]
                do_ref[:, sl] = dogv * (gate * s)
                dg_ref[:, sl] = dogv * o_ref[:, sl] * (s * (1.0 + gate * (1.0 - s)))
            ds_acc = jnp.zeros((1, ATT_Q_HEADS), F32)
            for kvh in range(ATT_KV_HEADS):
                cols = slice(kvh * GP, (kvh + 1) * GP)
                kp, kc, vp, vc = kp_ref[:, cols], kc_ref[:, cols], vp_ref[:, cols], vc_ref[:, cols]
                q_stack = _head_masked_rows(q_ref[:, cols], BF16)
                do_g = do_ref[:, cols]
                do_stack = _head_masked_rows(do_g, BF16)
                lse_g = lse_ref[:, cols]
                lse_stack = jnp.concatenate(
                    [_both_halves(lse_g[:, (r // 2) * LANES:(r // 2 + 1) * LANES])[r % 2] for r in range(ATT_GQA)], axis=0)
                pp = jnp.exp(jnp.where(
                    mask_p, lax.dot_general(q_stack, kp, NT_DIMS, preferred_element_type=F32) - lse_stack, NEG_INF))
                pc = jnp.exp(jnp.where(
                    mask_c, lax.dot_general(q_stack, kc, NT_DIMS, preferred_element_type=F32) - lse_stack, NEG_INF))
                dpp = lax.dot_general(do_stack, vp, NT_DIMS, preferred_element_type=F32)
                dpc = lax.dot_general(do_stack, vc, NT_DIMS, preferred_element_type=F32)
                delta = jnp.sum(pp * dpp + pc * dpc, axis=1, keepdims=True)
                dsp = (pp * (dpp - delta)).astype(BF16)
                dsc = (pc * (dpc - delta)).astype(BF16)
                dq_ref[:, cols] = _stack_fold(jnp.dot(dsp, kp, preferred_element_type=F32)
                                              + jnp.dot(dsc, kc, preferred_element_type=F32))
                dk_ref[:, cols] = ck_ref[:, cols] + lax.dot_general(dsp, q_stack, TN_DIMS, preferred_element_type=F32)
                dv_ref[:, cols] = cv_ref[:, cols] + lax.dot_general(pp.astype(BF16), do_stack, TN_DIMS,
                                                                    preferred_element_type=F32)
                ck_ref[:, cols] = lax.dot_general(dsc, q_stack, TN_DIMS, preferred_element_type=F32)
                cv_ref[:, cols] = lax.dot_general(pc.astype(BF16), do_stack, TN_DIMS, preferred_element_type=F32)
                t = jnp.exp(_stack_sinks(sink_ref, kvh) - lse_stack) * delta
                for r in range(ATT_GQA):
                    tot = jnp.sum(t[r * ATT_BLOCK:(r + 1) * ATT_BLOCK], axis=0, keepdims=True)
                    ds_acc = ds_acc - jnp.where(lane == kvh * ATT_GQA + r, tot[:, :ATT_Q_HEADS], 0.0)
            ds_ref[...] += ds_acc

    def cur(n):
        return jnp.minimum(n, nb - 1)

    def prev(n):
        return jnp.maximum(n - 1, 0)

    wide = pl.BlockSpec((ATT_BLOCK, ATT_WIDTH), lambda n: (cur(n), 0))
    late = pl.BlockSpec((ATT_BLOCK, ATT_WIDTH), lambda n: (prev(n), 0))
    return pl.pallas_call(
        body, grid=(nb + 1,),
        in_specs=[pl.BlockSpec(memory_space=pltpu.SMEM), wide,
                  pl.BlockSpec((ATT_BLOCK, ATT_WIDTH), lambda n: (prev(cur(n)), 1)),
                  pl.BlockSpec((ATT_BLOCK, ATT_WIDTH), lambda n: (cur(n), 1)),
                  pl.BlockSpec((ATT_BLOCK, ATT_WIDTH), lambda n: (prev(cur(n)), 2)),
                  pl.BlockSpec((ATT_BLOCK, ATT_WIDTH), lambda n: (cur(n), 2)),
                  pl.BlockSpec((ATT_BLOCK, GATE_HALF), lambda n: (cur(n), GATE_COL_BLOCK)),
                  pl.BlockSpec((ATT_BLOCK, GATE_HALF), lambda n: (cur(n), GATE_COL_BLOCK + 1)),
                  wide, wide, wide] + [ANY] * n_ride,
        out_specs=[wide, late, late, wide, pl.BlockSpec((1, ATT_Q_HEADS), lambda n: (0, 0))] + [ANY] * n_ride,
        out_shape=[jax.ShapeDtypeStruct((l, ATT_WIDTH), F32), jax.ShapeDtypeStruct((l, ATT_WIDTH), F32),
                   jax.ShapeDtypeStruct((l, ATT_WIDTH), F32), jax.ShapeDtypeStruct((l, ATT_WIDTH), F32),
                   jax.ShapeDtypeStruct((1, ATT_Q_HEADS), F32)] + _scatter_shapes(ride),
        scratch_shapes=[pltpu.VMEM((ATT_BLOCK, ATT_WIDTH), F32), pltpu.VMEM((ATT_BLOCK, ATT_WIDTH), F32),
                        pltpu.VMEM((ATT_BLOCK, ATT_WIDTH), F32)] + (_gather_sems(n_ride) if n_ride else []),
        compiler_params=_params("arbitrary"), name=name,
    )(sinks, qkv, qkv, qkv, qkv, qkv, proj, proj, o, lse, dog, *ride)


def _local_step(x, positions, pre_norm, post_norm, conv_b, dt_bias, a_log, d_skip, gate_norm, sinks, target,
                first_in, in_proj_with_first_pair, scan_with_second_pair, attn_bwd_with_second_pair_grads,
                in_dx_with_first_pair_grads):
    tables = _rope_tables(positions)
    dt_bias_pad = jnp.pad(dt_bias, ((0, 0), (0, SSM_DT_PAD - SSM_HEADS)))
    d_lanes = jnp.repeat(d_skip, SSM_HEAD_DIM, axis=1).reshape(-1, SSM_GROUPS, 1, GP)
    alog_lanes = jnp.repeat(a_log, SSM_HEAD_DIM, axis=1)
    pairs = [first_in, None]
    saved = []
    cur = x
    h = _rmsnorm_fwd(cur, pre_norm[0], "prenorm_fwd_0")
    for i in range(DEPTH):
        j = i // 2
        if i % 2 == 0:
            in_proj = functools.partial(_matmul, h, pairs[j]["ssm_w_in"], "nn", F32, f"ssm_in_{i}")
            if i == 0:
                proj, rest = in_proj_with_first_pair(in_proj)
                pairs[0] = {**first_in, **rest}
            else:
                proj = in_proj()
            pre, xbc = _conv_fwd(proj, pairs[j]["ssm_conv_w"], conv_b[j], f"conv_fwd_{i}")
            dtb, acsb, dtr, acs_r = _ssd_prep(proj, dt_bias_pad[j:j + 1], alog_lanes[j:j + 1], f"ssd_prep_{i}")
            scan = functools.partial(_ssd_fwd, xbc, dtb, acsb, acs_r, d_lanes[j], proj, gate_norm[j], f"ssd_fwd_{i}")
            if i == 0:
                y, act, hin, pairs[1] = scan_with_second_pair(scan)
            else:
                y, act, hin = scan()
            w_ssm_in = [p["ssm_w_in"] for p in pairs]
            w_ssm_out = [p["ssm_w_out"] for p in pairs]
            w_att_in = [p["att_w_in"] for p in pairs]
            w_att_out = [p["att_w_out"] for p in pairs]
            conv_w = [p["ssm_conv_w"] for p in pairs]
            ymix = _matmul(act, w_ssm_out[j], "nn", F32, f"ssm_out_{i}")
            saved.append(dict(x=cur, h=h, proj=proj, pre=pre, xbc=xbc, dtb=dtb, acsb=acsb, dtr=dtr, acs_r=acs_r, y=y,
                              hin=hin, act=act, ymix=ymix))
        else:
            proj = _matmul(h, w_att_in[j], "nn", F32, f"att_in_{i}")
            qkv = _rope_fwd(proj, tables, f"rope_fwd_{i}")
            act, o, lse = _attn_fwd(qkv, proj, sinks[j], f"attn_fwd_{i}")
            ymix = _matmul(act, w_att_out[j], "nn", F32, f"att_out_{i}")
            saved.append(dict(x=cur, h=h, proj=proj, qkv=qkv, o=o, lse=lse, act=act, ymix=ymix))
        if i + 1 < DEPTH:
            cur, h = _post_fwd(cur, ymix, post_norm[i], pre_norm[i + 1], f"post_fwd_{i}")

    gr = {k: [None] * 2 for k in ("ssm_w_in", "ssm_conv_w", "ssm_conv_b", "ssm_dt_bias", "ssm_a_log", "ssm_d",
                                  "ssm_gate_norm", "ssm_w_out", "att_w_in", "att_sinks", "att_w_out")}
    gr["pre_norm"] = [None] * DEPTH
    gr["post_norm"] = [None] * DEPTH
    last = DEPTH - 1
    g, dymix, loss_lanes, gr["post_norm"][last] = _post_loss(cur, ymix, post_norm[last], target, "post_loss")
    for i in reversed(range(DEPTH)):
        j = i // 2
        s = saved[i]
        if i % 2 == 0:
            dact = _matmul(dymix, w_ssm_out[j], "nt", F32, f"ssm_out_dx_{i}")
            gr["ssm_w_out"][j] = _matmul(s["act"], dymix, "tn", F32, f"ssm_out_dw_{i}")
            dxbc, ddt8, dal, dd, dproj, gr["ssm_gate_norm"][j] = _ssd_bwd(
                s["xbc"], s["dtb"], s["acsb"], s["dtr"], s["acs_r"], a_log[j], d_lanes[j], s["hin"], dact, s["y"],
                s["proj"], gate_norm[j], f"ssd_bwd_{i}")
            gr["ssm_a_log"][j] = dal.reshape(SSM_HEADS)
            gr["ssm_d"][j] = dd.reshape(SSM_HEADS)
            l = x.shape[0]
            ddt = jnp.pad(jnp.transpose(ddt8, (2, 0, 1)).reshape(l, SSM_HEADS), ((0, 0), (0, SSM_DT_PAD - SSM_HEADS)))
            dproj, dbias = _dt_bwd(ddt, s["proj"], dt_bias_pad[j:j + 1], dproj, f"dt_bwd_{i}")
            gr["ssm_dt_bias"][j] = dbias[0, :SSM_HEADS]
            dproj, gr["ssm_conv_w"][j], dcb = _conv_bwd(dxbc, s["pre"], s["proj"], conv_w[j], dproj, f"conv_bwd_{i}")
            gr["ssm_conv_b"][j] = dcb[0]
            w_in, key = w_ssm_in[j], "ssm_w_in"
        else:
            dog = _matmul(dymix, w_att_out[j], "nt", F32, f"att_out_dx_{i}")
            gr["att_w_out"][j] = _matmul(s["act"], dymix, "tn", F32, f"att_out_dw_{i}")
            attn_bwd = functools.partial(_attn_bwd, s["qkv"], s["proj"], sinks[j], s["o"], s["lse"], dog, f"attn_bwd_{i}")
            if i == 1:
                (dq, dk, dv, dgate, dsk), second_pair_reduced = attn_bwd_with_second_pair_grads(
                    attn_bwd, {k: gr[k][1] for k in BIG})
            else:
                dq, dk, dv, dgate, dsk = attn_bwd()
            gr["att_sinks"][j] = dsk[0]
            dproj = _rope_bwd(dq, dk, dv, dgate, tables, f"rope_bwd_{i}")
            w_in, key = w_att_in[j], "att_w_in"
        gr[key][j] = _matmul(s["h"], dproj, "tn", F32, f"in_dw_{i}")
        in_dx = functools.partial(_matmul, dproj, w_in, "nt", F32, f"in_dx_{i}")
        if i == 0:
            dh, first_pair_reduced = in_dx_with_first_pair_grads(in_dx, {k: gr[k][0] for k in BIG})
        else:
            dh = in_dx()
        if i > 0:
            g, dymix, gr["pre_norm"][i], gr["post_norm"][i - 1] = _norm_bwd_chain(
                dh, s["x"], pre_norm[i], g, saved[i - 1]["ymix"], post_norm[i - 1], f"norm_bwd_{i}")
        else:
            g, gr["pre_norm"][i] = _rmsnorm_bwd(dh, s["x"], pre_norm[i], g, F32, f"prenorm_bwd_{i}")
    grads = {k: jnp.stack([v.reshape(v.shape[-1]) if k in ("pre_norm", "post_norm", "ssm_gate_norm") else v for v in vs])
             for k, vs in gr.items() if k not in BIG}
    return loss_lanes, g, grads, first_pair_reduced, second_pair_reduced


N_CHIPS = 4
N_DEV = 8
MESH = pl.DeviceIdType.MESH
ANY = pl.BlockSpec(memory_space=pl.ANY)


def _place():
    x, y, c = lax.axis_index("x"), lax.axis_index("y"), lax.axis_index("c")
    return x, y, c, 2 * x + y


def _gather_sems(n):
    return [pltpu.SemaphoreType.DMA((n, N_CHIPS)), pltpu.SemaphoreType.DMA((n, N_CHIPS)), pltpu.SemaphoreType.DMA((n,))]


def _gather_between_chips(ins, outs, send_sems, recv_sems, local_sems, wait):
    n = len(ins)
    _, _, c, s = _place()
    local = [pltpu.make_async_copy(ins[w], outs[w].at[s], local_sems.at[w]) for w in range(n)]

    def remote(w, t):
        return pltpu.make_async_remote_copy(
            src_ref=ins[w].at[c], dst_ref=outs[w].at[s, c], send_sem=send_sems.at[w, t],
            recv_sem=recv_sems.at[w, s], device_id=(t // 2, t % 2, c), device_id_type=MESH)

    def arrival(w, t):
        return pltpu.make_async_remote_copy(
            src_ref=ins[w].at[c], dst_ref=outs[w].at[t, c], send_sem=send_sems.at[w, t],
            recv_sem=recv_sems.at[w, t], device_id=(t // 2, t % 2, c), device_id_type=MESH)

    if not wait:
        for cp in local:
            cp.start()
    for t in range(N_CHIPS):
        @pl.when(s != t)
        def _():
            for w in range(n):
                if wait:
                    remote(w, t).wait_send()
                    arrival(w, t).wait_recv()
                else:
                    remote(w, t).start()
    if wait:
        for cp in local:
            cp.wait()


def _pair_handoff(bufs, name):
    n = len(bufs)

    def body(*refs):
        outs = refs[n:2 * n]
        send_sems, recv_sems = refs[2 * n:]
        x, y, c, s = _place()

        def handed_on(w, t):
            return pltpu.make_async_remote_copy(
                src_ref=outs[w].at[t, c], dst_ref=outs[w].at[t, c], send_sem=send_sems.at[w, t],
                recv_sem=recv_sems.at[w, t], device_id=(x, y, 1 - c), device_id_type=MESH)

        def handed_in(w, t):
            return pltpu.make_async_remote_copy(
                src_ref=outs[w].at[t, 1 - c], dst_ref=outs[w].at[t, 1 - c], send_sem=send_sems.at[w, t],
                recv_sem=recv_sems.at[w, t], device_id=(x, y, 1 - c), device_id_type=MESH)

        for t in range(N_CHIPS):
            @pl.when(s != t)
            def _():
                for w in range(n):
                    handed_on(w, t).start()
        for t in range(N_CHIPS):
            @pl.when(s != t)
            def _():
                for w in range(n):
                    handed_on(w, t).wait_send()
                    handed_in(w, t).wait_recv()

    return pl.pallas_call(
        body, in_specs=[ANY] * n, out_specs=[ANY] * n,
        out_shape=[jax.ShapeDtypeStruct(a.shape, a.dtype) for a in bufs],
        scratch_shapes=[pltpu.SemaphoreType.DMA((n, N_CHIPS)), pltpu.SemaphoreType.DMA((n, N_CHIPS))],
        input_output_aliases={w: w for w in range(n)}, name=name,
    )(*bufs)


def _chip_gather(shards, name):
    n = len(shards)

    def body(*refs):
        ins, outs = refs[:n], refs[n:2 * n]
        _gather_between_chips(ins, outs, *refs[2 * n:], wait=False)
        _gather_between_chips(ins, outs, *refs[2 * n:], wait=True)

    bufs = pl.pallas_call(
        body, in_specs=[ANY] * n, out_specs=[ANY] * n,
        out_shape=[jax.ShapeDtypeStruct((N_CHIPS,) + a.shape, a.dtype) for a in shards],
        scratch_shapes=_gather_sems(n), name=name,
    )(*shards)
    return _pair_handoff(bufs, name + "_handoff")


def _pair_swap(parts, name):
    n = len(parts)

    def body(*refs):
        ins, outs = refs[:n], refs[n:2 * n]
        send_sems, recv_sems = refs[2 * n:]
        x, y, c, _ = _place()
        cps = [pltpu.make_async_remote_copy(
            src_ref=ins[w].at[1 - c], dst_ref=outs[w], send_sem=send_sems.at[w], recv_sem=recv_sems.at[w],
            device_id=(x, y, 1 - c), device_id_type=MESH) for w in range(n)]
        for cp in cps:
            cp.start()
        for cp in cps:
            cp.wait()

    return pl.pallas_call(
        body, in_specs=[ANY] * n, out_specs=[ANY] * n,
        out_shape=[jax.ShapeDtypeStruct(a.shape[1:], a.dtype) for a in parts],
        scratch_shapes=[pltpu.SemaphoreType.DMA((n,)), pltpu.SemaphoreType.DMA((n,))],
        name=name,
    )(*parts)


def _scatter_between_chips(ins, outs, send_sems, recv_sems, local_sems, wait):
    n = len(ins)
    _, _, c, s = _place()

    def block(w, t):
        rows = ins[w].shape[0] // N_CHIPS
        return ins[w].at[pl.ds(t * rows, rows)]

    local = [pltpu.make_async_copy(block(w, s), outs[w].at[s], local_sems.at[w]) for w in range(n)]

    def remote(w, t):
        return pltpu.make_async_remote_copy(
            src_ref=block(w, t), dst_ref=outs[w].at[s], send_sem=send_sems.at[w, t], recv_sem=recv_sems.at[w, s],
            device_id=(t // 2, t % 2, c), device_id_type=MESH)

    def arrival(w, t):
        return pltpu.make_async_remote_copy(
            src_ref=block(w, t), dst_ref=outs[w].at[t], send_sem=send_sems.at[w, t], recv_sem=recv_sems.at[w, t],
            device_id=(t // 2, t % 2, c), device_id_type=MESH)

    if not wait:
        for cp in local:
            cp.start()
    for t in range(N_CHIPS):
        @pl.when(s != t)
        def _():
            for w in range(n):
                if wait:
                    remote(w, t).wait_send()
                    arrival(w, t).wait_recv()
                else:
                    remote(w, t).start()
    if wait:
        for cp in local:
            cp.wait()


def _scatter_shapes(parts):
    return [jax.ShapeDtypeStruct((N_CHIPS, a.shape[0] // N_CHIPS, a.shape[1]), a.dtype) for a in parts]


def _pair_merge(parts, name):
    n = len(parts)

    def body(*refs):
        ins, outs = refs[:n], refs[n:2 * n]
        send_sems, recv_sems = refs[2 * n:]
        x, y, c, _ = _place()
        cps = [pltpu.make_async_remote_copy(
            src_ref=ins[w], dst_ref=outs[w], send_sem=send_sems.at[w], recv_sem=recv_sems.at[w],
            device_id=(x, y, 1 - c), device_id_type=MESH) for w in range(n)]
        for cp in cps:
            cp.start()
        for cp in cps:
            cp.wait()

    return pl.pallas_call(
        body, in_specs=[ANY] * n, out_specs=[ANY] * n,
        out_shape=[jax.ShapeDtypeStruct(a.shape, a.dtype) for a in parts],
        scratch_shapes=[pltpu.SemaphoreType.DMA((n,)), pltpu.SemaphoreType.DMA((n,))],
        name=name,
    )(*parts)


def _all_gather_small(a, name):
    def body(in_ref, out_ref, send_sems, recv_sems, local_sem):
        x, y, c, _ = _place()
        me = 4 * x + 2 * y + c
        local = pltpu.make_async_copy(in_ref, out_ref.at[me], local_sem)
        local.start()

        def remote(d):
            return pltpu.make_async_remote_copy(
                src_ref=in_ref, dst_ref=out_ref.at[me], send_sem=send_sems.at[d], recv_sem=recv_sems.at[me],
                device_id=(d // 4, (d // 2) % 2, d % 2), device_id_type=MESH)

        def arrival(d):
            return pltpu.make_async_remote_copy(
                src_ref=in_ref, dst_ref=out_ref.at[d], send_sem=send_sems.at[d], recv_sem=recv_sems.at[d],
                device_id=(d // 4, (d // 2) % 2, d % 2), device_id_type=MESH)

        for d in range(N_DEV):
            @pl.when(me != d)
            def _():
                remote(d).start()
        for d in range(N_DEV):
            @pl.when(me != d)
            def _():
                remote(d).wait_send()
                arrival(d).wait_recv()
        local.wait()

    return pl.pallas_call(
        body, in_specs=[ANY], out_specs=ANY, out_shape=jax.ShapeDtypeStruct((N_DEV,) + a.shape, a.dtype),
        scratch_shapes=[pltpu.SemaphoreType.DMA((N_DEV,)), pltpu.SemaphoreType.DMA((N_DEV,)), pltpu.SemaphoreType.DMA],
        name=name,
    )(a)


def _reduce_tile(rows):
    return _pick(rows, (256, 128, 16))


def _pair_add(full, other, layer, name):
    _, rows, cols = full.shape
    tr = _reduce_tile(rows)

    def body(layer_ref, a_ref, b_ref, o_ref):
        o_ref[...] = (a_ref[0] + b_ref[...]).astype(o_ref.dtype)

    return pl.pallas_call(
        body,
        grid_spec=pltpu.PrefetchScalarGridSpec(
            num_scalar_prefetch=1, grid=(rows // tr,),
            in_specs=[pl.BlockSpec((1, tr, cols), lambda i, lr: (lr[0], i, 0)), pl.BlockSpec((tr, cols), lambda i, lr: (i, 0))],
            out_specs=pl.BlockSpec((tr, cols), lambda i, lr: (i, 0))),
        out_shape=jax.ShapeDtypeStruct((rows, cols), BF16), compiler_params=_params("parallel"), name=name,
    )(layer, full, other)


def _sum_slots(a, name):
    n, rows, cols = a.shape
    tr = _reduce_tile(rows)

    def body(a_ref, o_ref):
        acc = a_ref[0].astype(F32)
        for k in range(1, n):
            acc = acc + a_ref[k].astype(F32)
        o_ref[...] = acc

    return pl.pallas_call(
        body, grid=(rows // tr,), in_specs=[pl.BlockSpec((n, tr, cols), lambda i: (0, i, 0))],
        out_specs=pl.BlockSpec((tr, cols), lambda i: (i, 0)),
        out_shape=jax.ShapeDtypeStruct((rows, cols), F32), compiler_params=_params("parallel"), name=name,
    )(a)


def _adamw(w, g, m, v, name):
    rows, cols = w.shape
    tr = _pick(rows, (256, 8))

    def body(w_ref, g_ref, m_ref, v_ref, d_ref, nm_ref, nv_ref):
        gv = g_ref[...]
        mn = ADAM_B1 * m_ref[...] + (1.0 - ADAM_B1) * gv
        vn = ADAM_B2 * v_ref[...] + (1.0 - ADAM_B2) * jnp.square(gv)
        m_hat = mn / (1.0 - ADAM_B1 ** ADAM_STEP)
        v_hat = vn / (1.0 - ADAM_B2 ** ADAM_STEP)
        d_ref[...] = -ADAM_LR * (m_hat / (jnp.sqrt(v_hat) + ADAM_EPS) + ADAM_WD * w_ref[...])
        nm_ref[...] = mn
        nv_ref[...] = vn

    blk = pl.BlockSpec((tr, cols), lambda i: (i, 0))
    return pl.pallas_call(
        body, grid=(rows // tr,), in_specs=[blk] * 4, out_specs=[blk] * 3,
        out_shape=[jax.ShapeDtypeStruct((rows, cols), F32)] * 3, compiler_params=_params("parallel"), name=name,
    )(w, g, m, v)


BIG = ("ssm_w_in", "ssm_w_out", "att_w_in", "att_w_out")
SHARDED = BIG + ("ssm_conv_w",)
SMALL = ("pre_norm", "post_norm", "ssm_conv_b", "ssm_dt_bias", "ssm_a_log", "ssm_d", "ssm_gate_norm", "att_sinks")
WEIGHTS = ("pre_norm", "post_norm", "ssm_w_in", "ssm_conv_w", "ssm_conv_b", "ssm_dt_bias", "ssm_a_log", "ssm_d",
           "ssm_gate_norm", "ssm_w_out", "att_w_in", "att_sinks", "att_w_out")


def _halves(a):
    return a.reshape(2, a.shape[0] // 2, a.shape[1])


def _layer_shards(j, ssm_w_in, ssm_w_out, att_w_in, att_w_out, ssm_conv_w):
    return [_halves(ssm_w_in[j].astype(BF16)), _halves(ssm_w_out[j].astype(BF16)), _halves(att_w_in[j].astype(BF16)),
            _halves(att_w_out[j].astype(BF16)), _halves(ssm_conv_w[j])]


SHARD_KEYS = ("ssm_w_in", "ssm_w_out", "att_w_in", "att_w_out", "ssm_conv_w")


def _whole_weights(keys, gathered):
    out = {}
    for k, g in zip(keys, gathered):
        g = g.reshape((N_CHIPS, 2 * g.shape[2], g.shape[3]))
        if k in ("ssm_w_out", "att_w_out"):
            out[k] = g.reshape(N_CHIPS * g.shape[1], g.shape[2])
        else:
            out[k] = jnp.transpose(g, (1, 0, 2)).reshape(g.shape[1], N_CHIPS * g.shape[2])
    if "ssm_w_in" in out:
        out["ssm_w_in"] = jnp.pad(out["ssm_w_in"], ((0, 0), (0, SSM_IN_PAD - SSM_IN_DIM)))
    return out


def _halves_by_chip(key, g):
    if key in ("ssm_w_out", "att_w_out"):
        rows = g.shape[0] // N_CHIPS
        blocks = g.reshape(N_CHIPS, 2, rows // 2, g.shape[1])
        return jnp.transpose(blocks, (1, 0, 2, 3)).reshape(2, N_CHIPS * (rows // 2), g.shape[1])
    cols = (SSM_IN_DIM if key == "ssm_w_in" else g.shape[1]) // N_CHIPS
    rows = g.shape[0]
    blocks = g[:, :N_CHIPS * cols].reshape(2, rows // 2, N_CHIPS, cols)
    return jnp.transpose(blocks, (0, 2, 1, 3)).reshape(2, N_CHIPS * (rows // 2), cols)


def _pack_small(tree, keys):
    flat = jnp.concatenate([tree[k].reshape(-1) for k in keys])
    rows = -(-flat.shape[0] // (8 * LANES)) * 8
    return jnp.pad(flat, (0, rows * LANES - flat.shape[0])).reshape(rows, LANES)


def _unpack_small(packed, shapes, keys):
    flat = packed.reshape(-1)
    out, at = {}, 0
    for k in keys:
        n = 1
        for dim in shapes[k]:
            n *= dim
        out[k] = flat[at:at + n].reshape(shapes[k])
        at += n
    return out


def kernel(x, positions, pre_norm, post_norm, ssm_w_in, ssm_conv_w, ssm_conv_b, ssm_dt_bias, ssm_a_log, ssm_d, ssm_gate_norm, ssm_w_out, att_w_in, att_sinks, att_w_out, loss_target, m_pre_norm, m_post_norm, m_ssm_w_in, m_ssm_conv_w, m_ssm_conv_b, m_ssm_dt_bias, m_ssm_a_log, m_ssm_d, m_ssm_gate_norm, m_ssm_w_out, m_att_w_in, m_att_sinks, m_att_w_out, v_pre_norm, v_post_norm, v_ssm_w_in, v_ssm_conv_w, v_ssm_conv_b, v_ssm_dt_bias, v_ssm_a_log, v_ssm_d, v_ssm_gate_norm, v_ssm_w_out, v_att_w_in, v_att_sinks, v_att_w_out):
    w = dict(pre_norm=pre_norm, post_norm=post_norm, ssm_w_in=ssm_w_in, ssm_conv_w=ssm_conv_w, ssm_conv_b=ssm_conv_b,
             ssm_dt_bias=ssm_dt_bias, ssm_a_log=ssm_a_log, ssm_d=ssm_d, ssm_gate_norm=ssm_gate_norm, ssm_w_out=ssm_w_out,
             att_w_in=att_w_in, att_sinks=att_sinks, att_w_out=att_w_out)
    m = dict(pre_norm=m_pre_norm, post_norm=m_post_norm, ssm_w_in=m_ssm_w_in, ssm_conv_w=m_ssm_conv_w, ssm_conv_b=m_ssm_conv_b,
             ssm_dt_bias=m_ssm_dt_bias, ssm_a_log=m_ssm_a_log, ssm_d=m_ssm_d, ssm_gate_norm=m_ssm_gate_norm,
             ssm_w_out=m_ssm_w_out, att_w_in=m_att_w_in, att_sinks=m_att_sinks, att_w_out=m_att_w_out)
    v = dict(pre_norm=v_pre_norm, post_norm=v_post_norm, ssm_w_in=v_ssm_w_in, ssm_conv_w=v_ssm_conv_w, ssm_conv_b=v_ssm_conv_b,
             ssm_dt_bias=v_ssm_dt_bias, ssm_a_log=v_ssm_a_log, ssm_d=v_ssm_d, ssm_gate_norm=v_ssm_gate_norm,
             ssm_w_out=v_ssm_w_out, att_w_in=v_att_w_in, att_sinks=v_att_sinks, att_w_out=v_att_w_out)
    c = lax.axis_index("c")
    chip = 2 * lax.axis_index("x") + lax.axis_index("y")

    sharded = (ssm_w_in, ssm_w_out, att_w_in, att_w_out, ssm_conv_w)
    own = [dict(zip(SHARD_KEYS, _layer_shards(j, *sharded))) for j in range(2)]
    now_keys = ("ssm_w_in", "ssm_conv_w")
    later_keys = ("ssm_w_out", "att_w_in", "att_w_out")
    first_in = _whole_weights(now_keys, _chip_gather([own[0][k] for k in now_keys], "gather_weights_0"))

    def in_proj_with_first_pair(matmul):
        proj, *arrived = matmul(ride=[own[0][k] for k in later_keys])
        return proj, _whole_weights(later_keys, _pair_handoff(arrived, "gather_weights_0_rest_handoff"))

    def scan_with_second_pair(scan):
        y, act, hin, *arrived = scan(ride=[own[1][k] for k in SHARD_KEYS])
        return y, act, hin, _whole_weights(SHARD_KEYS, _pair_handoff(arrived, "gather_weights_1_handoff"))

    half = jnp.reshape(c, (1,)).astype(jnp.int32)

    def reduce_begin(pair_grads, tag):
        parts = [_halves_by_chip(k, pair_grads[k]) for k in BIG]
        from_sibling = _pair_swap(parts, f"reduce_pair_swap_{tag}")
        return [_pair_add(p, o, half, f"reduce_pair_add_{tag}_{n}") for n, (p, o) in enumerate(zip(parts, from_sibling))]

    def reduce_end(by_chip, tag):
        mine = [_sum_slots(a, f"reduce_chip_sum_{tag}_{n}") for n, a in enumerate(by_chip)]
        theirs = _pair_merge(mine, f"reduce_pair_merge_{tag}")
        return {k: jnp.where(c == 0, jnp.concatenate([a, b]), jnp.concatenate([b, a])) for k, a, b in zip(BIG, mine, theirs)}

    def attn_bwd_with_second_pair_grads(attn_bwd, pair_grads):
        dq, dk, dv, dgate, dsk, *by_chip = attn_bwd(ride=reduce_begin(pair_grads, "1"))
        return (dq, dk, dv, dgate, dsk), reduce_end(by_chip, "1")

    def in_dx_with_first_pair_grads(matmul, pair_grads):
        dh, *by_chip = matmul(ride=reduce_begin(pair_grads, "0"), ride_scatters=True)
        return dh, reduce_end(by_chip, "0")

    loss_lanes, grad_x, gr, reduced_0, reduced_1 = _local_step(
        x[0], positions[0], pre_norm, post_norm, ssm_conv_b, ssm_dt_bias, ssm_a_log, ssm_d, ssm_gate_norm, att_sinks,
        loss_target[0], first_in, in_proj_with_first_pair, scan_with_second_pair, attn_bwd_with_second_pair_grads,
        in_dx_with_first_pair_grads)
    loss = lax.psum(0.5 * jnp.sum(loss_lanes) / D_MODEL, ("x", "y", "c"))
    grads = {k: jnp.stack([reduced_0[k], reduced_1[k]]) for k in BIG}

    small_keys = SMALL + ("ssm_conv_w",)
    small_shapes = {k: w[k].shape for k in SMALL}
    small_shapes["ssm_conv_w"] = gr["ssm_conv_w"].shape
    small_sum = _sum_slots(_all_gather_small(_pack_small(gr, small_keys), "reduce_small_gather"), "reduce_small_sum")
    grads.update(_unpack_small(small_sum, small_shapes, small_keys))
    conv_cols = ssm_conv_w.shape[2]
    grads["ssm_conv_w"] = lax.dynamic_slice_in_dim(grads["ssm_conv_w"], chip * conv_cols, conv_cols, axis=2)

    delta, new_m, new_v = {}, {}, {}
    for k in SHARDED:
        shp = w[k].shape
        two_d = (shp[0] * shp[1], shp[2])
        d_, m_, v_ = _adamw(w[k].reshape(two_d), grads[k].reshape(two_d), m[k].reshape(two_d), v[k].reshape(two_d),
                            f"adamw_{k}")
        delta[k], new_m[k], new_v[k] = d_.reshape(shp), m_.reshape(shp), v_.reshape(shp)
    d_, m_, v_ = _adamw(_pack_small(w, SMALL), _pack_small(grads, SMALL), _pack_small(m, SMALL), _pack_small(v, SMALL),
                        "adamw_small")
    delta.update(_unpack_small(d_, small_shapes, SMALL))
    new_m.update(_unpack_small(m_, small_shapes, SMALL))
    new_v.update(_unpack_small(v_, small_shapes, SMALL))

    return (loss, grad_x[None], *[grads[k] for k in WEIGHTS], *[delta[k] for k in WEIGHTS],
            *[new_m[k] for k in WEIGHTS], *[new_v[k] for k in WEIGHTS])
```

```python
import functools

import jax
import jax.numpy as jnp
from jax import lax
from jax.experimental import pallas as pl
from jax.experimental.pallas import tpu as pltpu

F32 = jnp.float32
BF16 = jnp.bfloat16
EPS = 1e-6
NEG_INF = float("-inf")

D_MODEL = 1024
DEPTH = 4
SSM_D_INNER = 2048
SSM_HEAD_DIM = 64
SSM_HEADS = 32
SSM_GROUPS = 8
SSM_HPG = 4
SSM_STATE = 128
SSM_CONV = 4
SSM_CHUNK = 128
SSM_BC_DIM = 1024
SSM_CONV_DIM = 4096
SSM_IN_DIM = 6176
SSM_IN_PAD = 6272
SSM_DT_PAD = 128
ATT_HEAD_DIM = 64
ATT_Q_HEADS = 16
ATT_KV_HEADS = 4
ATT_GQA = 4
ATT_WIDTH = 1024
ATT_KV_WIDTH = 256
ATT_IN_DIM = 2560
ATT_QKV = ATT_WIDTH + 2 * ATT_KV_WIDTH
ATT_BLOCK = 128
ROPE_THETA = 500000.0
ROPE_DIM = 16
ROPE_HALF = 8
Q_SCALE = ATT_HEAD_DIM ** -0.5

ADAM_LR = 0.001
ADAM_B1 = 0.9
ADAM_B2 = 0.999
ADAM_EPS = 1e-08
ADAM_WD = 0.01
ADAM_STEP = 10

VMEM_LIMIT_BYTES = 48 * 1024 * 1024
NT_DIMS = (((1,), (1,)), ((), ()))
TN_DIMS = (((0,), (0,)), ((), ()))


def _params(*sem):
    return pltpu.CompilerParams(dimension_semantics=sem, vmem_limit_bytes=VMEM_LIMIT_BYTES)


def _pick(n, cands):
    for c in cands:
        if n % c == 0:
            return c
    return n


def _sigmoid(v):
    return 0.5 * jnp.tanh(0.5 * v) + 0.5


def _bdot(a, b):
    return jnp.dot(a.astype(BF16), b.astype(BF16), preferred_element_type=F32)


def _bdot_nt(a, b):
    return lax.dot_general(a.astype(BF16), b.astype(BF16), NT_DIMS, preferred_element_type=F32)


def _bdot_tn(a, b):
    return lax.dot_general(a.astype(BF16), b.astype(BF16), TN_DIMS, preferred_element_type=F32)


MATMUL_VMEM_BUDGET = 36 * 1024 * 1024


def _matmul_tiles(m, n, k, out_bytes, reduce_rows):
    best = None
    whole = [k] if (not reduce_rows or k <= 2048) else []
    for tk in whole + [c for c in (4096, 2048, 1024, 896, 512) if k % c == 0 and c < k]:
        for tm in (c for c in (2048, 1024, 512, 256) if m % c == 0):
            for tn in (c for c in (n, 1280, 1024, 896, 640, 512) if n % c == 0):
                acc = tm * tn * 4 if tk < k else 0
                need = 2 * (2 * tk * (tm + tn) + tm * tn * out_bytes) + acc
                if need <= MATMUL_VMEM_BUDGET and (best is None or tm * tn * min(tk, 2048) > best[0]):
                    best = (tm * tn * min(tk, 2048), tm, tn, tk)
        if best is not None and not reduce_rows:
            break
    return best[1:]


def _matmul(a, b, mode, out_dtype, name, ride=(), ride_scatters=False):
    if mode == "nn":
        (m, k), n = a.shape, b.shape[1]
    elif mode == "nt":
        (m, k), n = a.shape, b.shape[0]
    else:
        (k, m), n = a.shape, b.shape[1]
    tm, tn, tk = _matmul_tiles(m, n, k, jnp.dtype(out_dtype).itemsize, mode == "tn")
    nk = k // tk
    steps = (n // tn, m // tm, nk)
    dims = {"nn": (((1,), (0,)), ((), ())), "nt": NT_DIMS, "tn": TN_DIMS}[mode]
    n_ride = len(ride)
    exchange = _scatter_between_chips if ride_scatters else _gather_between_chips
    arrived = _scatter_shapes(ride) if ride_scatters else [jax.ShapeDtypeStruct((N_CHIPS,) + r.shape, r.dtype) for r in ride]

    def body(*refs):
        a_ref, b_ref = refs[:2]
        ride_in = refs[2:2 + n_ride]
        o_ref = refs[2 + n_ride]
        ride_out = refs[3 + n_ride:3 + 2 * n_ride]
        acc_ref = refs[3 + 2 * n_ride]
        ride_sems = refs[4 + 2 * n_ride:]
        kk = pl.program_id(2)
        at = [pl.program_id(d) for d in range(3)]
        if n_ride:
            @pl.when((at[0] == 0) & (at[1] == 0) & (at[2] == 0))
            def _():
                exchange(ride_in, ride_out, *ride_sems, wait=False)

        part = lax.dot_general(a_ref[...], b_ref[...], dims, preferred_element_type=F32)
        if nk == 1:
            o_ref[...] = part.astype(o_ref.dtype)
        else:
            @pl.when(kk == 0)
            def _():
                acc_ref[...] = part

            @pl.when(kk > 0)
            def _():
                acc_ref[...] += part

            @pl.when(kk == nk - 1)
            def _():
                o_ref[...] = acc_ref[...].astype(o_ref.dtype)

        if n_ride:
            @pl.when((at[0] == steps[0] - 1) & (at[1] == steps[1] - 1) & (at[2] == steps[2] - 1))
            def _():
                exchange(ride_in, ride_out, *ride_sems, wait=True)

    if mode == "nn":
        a_spec = pl.BlockSpec((tm, tk), lambda j, i, kk: (i, kk))
        b_spec = pl.BlockSpec((tk, tn), lambda j, i, kk: (kk, j))
    elif mode == "nt":
        a_spec = pl.BlockSpec((tm, tk), lambda j, i, kk: (i, kk))
        b_spec = pl.BlockSpec((tn, tk), lambda j, i, kk: (j, kk))
    else:
        a_spec = pl.BlockSpec((tk, tm), lambda j, i, kk: (kk, i))
        b_spec = pl.BlockSpec((tk, tn), lambda j, i, kk: (kk, j))
    out = pl.pallas_call(
        body, grid=steps, in_specs=[a_spec, b_spec] + [ANY] * n_ride,
        out_specs=[pl.BlockSpec((tm, tn), lambda j, i, kk: (i, j))] + [ANY] * n_ride,
        out_shape=[jax.ShapeDtypeStruct((m, n), out_dtype)] + arrived,
        scratch_shapes=[pltpu.VMEM((tm, tn), F32)] + (_gather_sems(n_ride) if n_ride else []),
        compiler_params=_params(*(["arbitrary"] * 3 if n_ride else ["parallel", "parallel", "arbitrary"])), name=name,
    )(a, b, *ride)
    return out if n_ride else out[0]


def _row_tile(l):
    return _pick(l, (512, 256, 128))


def _rmsnorm_fwd(x, w, name):
    l, d = x.shape
    tl = _row_tile(l)

    def body(x_ref, w_ref, o_ref):
        xv = x_ref[...]
        r = lax.rsqrt(jnp.mean(xv * xv, axis=-1, keepdims=True) + EPS)
        o_ref[...] = (xv * r * w_ref[...]).astype(o_ref.dtype)

    return pl.pallas_call(
        body, grid=(l // tl,),
        in_specs=[pl.BlockSpec((tl, d), lambda i: (i, 0)), pl.BlockSpec((1, d), lambda i: (0, 0))],
        out_specs=pl.BlockSpec((tl, d), lambda i: (i, 0)),
        out_shape=jax.ShapeDtypeStruct((l, d), BF16), compiler_params=_params("parallel"), name=name,
    )(x, w.reshape(1, d))


def _post_fwd(x, y, w, w_next, name):
    l, d = x.shape
    tl = _row_tile(l)

    def body(x_ref, y_ref, w_ref, wn_ref, o_ref, h_ref):
        yv = y_ref[...]
        r = lax.rsqrt(jnp.mean(yv * yv, axis=-1, keepdims=True) + EPS)
        out = x_ref[...] + yv * r * w_ref[...]
        o_ref[...] = out
        rn = lax.rsqrt(jnp.mean(out * out, axis=-1, keepdims=True) + EPS)
        h_ref[...] = (out * rn * wn_ref[...]).astype(h_ref.dtype)

    row = pl.BlockSpec((tl, d), lambda i: (i, 0))
    vec = pl.BlockSpec((1, d), lambda i: (0, 0))
    return pl.pallas_call(
        body, grid=(l // tl,), in_specs=[row, row, vec, vec], out_specs=[row, row],
        out_shape=[jax.ShapeDtypeStruct((l, d), F32), jax.ShapeDtypeStruct((l, d), BF16)],
        compiler_params=_params("parallel"), name=name,
    )(x, y, w.reshape(1, d), w_next.reshape(1, d))


def _post_loss(x, y, w, t, name):
    l, d = x.shape
    tl = _row_tile(l)
    nt = l // tl

    def body(x_ref, y_ref, w_ref, t_ref, g_ref, dy_ref, ls_ref, dw_ref, acc_ref):
        i = pl.program_id(0)

        @pl.when(i == 0)
        def _():
            ls_ref[...] = jnp.zeros_like(ls_ref)
            acc_ref[...] = jnp.zeros_like(acc_ref)

        yv = y_ref[...]
        r = lax.rsqrt(jnp.mean(yv * yv, axis=-1, keepdims=True) + EPS)
        nrm = yv * r
        e = x_ref[...] + nrm * w_ref[...] - t_ref[...]
        gv = e * (1.0 / d)
        g_ref[...] = gv
        ls_ref[...] += jnp.sum((e * e).reshape(tl // 8, 8, d), axis=0)
        gw = gv * w_ref[...]
        dy_ref[...] = (r * (gw - nrm * jnp.mean(gw * nrm, axis=-1, keepdims=True))).astype(dy_ref.dtype)
        acc_ref[...] += jnp.sum((gv * nrm).reshape(tl // 8, 8, d), axis=0)

        @pl.when(i == nt - 1)
        def _():
            dw_ref[...] = jnp.sum(acc_ref[...], axis=0, keepdims=True)

    row = pl.BlockSpec((tl, d), lambda i: (i, 0))
    vec = pl.BlockSpec((1, d), lambda i: (0, 0))
    return pl.pallas_call(
        body, grid=(nt,), in_specs=[row, row, vec, row],
        out_specs=[row, row, pl.BlockSpec((8, d), lambda i: (0, 0)), vec],
        out_shape=[jax.ShapeDtypeStruct((l, d), F32), jax.ShapeDtypeStruct((l, d), BF16),
                   jax.ShapeDtypeStruct((8, d), F32), jax.ShapeDtypeStruct((1, d), F32)],
        scratch_shapes=[pltpu.VMEM((8, d), F32)], compiler_params=_params("arbitrary"), name=name,
    )(x, y, w.reshape(1, d), t)


def _norm_bwd_chain(dh, x, w_pre, resid, y_prev, w_post_prev, name):
    l, d = x.shape
    tl = _row_tile(l)
    nt = l // tl

    def body(dh_ref, x_ref, wp_ref, r_ref, y_ref, wq_ref, g_ref, dy_ref, dwp_ref, dwq_ref, accp_ref, accq_ref):
        i = pl.program_id(0)

        @pl.when(i == 0)
        def _():
            accp_ref[...] = jnp.zeros_like(accp_ref)
            accq_ref[...] = jnp.zeros_like(accq_ref)

        xv = x_ref[...]
        dhv = dh_ref[...]
        rx = lax.rsqrt(jnp.mean(xv * xv, axis=-1, keepdims=True) + EPS)
        nx = xv * rx
        gw = dhv * wp_ref[...]
        gv = rx * (gw - nx * jnp.mean(gw * nx, axis=-1, keepdims=True)) + r_ref[...]
        g_ref[...] = gv
        accp_ref[...] += jnp.sum((dhv * nx).reshape(tl // 8, 8, d), axis=0)
        yv = y_ref[...]
        ry = lax.rsqrt(jnp.mean(yv * yv, axis=-1, keepdims=True) + EPS)
        ny = yv * ry
        gq = gv * wq_ref[...]
        dy_ref[...] = (ry * (gq - ny * jnp.mean(gq * ny, axis=-1, keepdims=True))).astype(dy_ref.dtype)
        accq_ref[...] += jnp.sum((gv * ny).reshape(tl // 8, 8, d), axis=0)

        @pl.when(i == nt - 1)
        def _():
            dwp_ref[...] = jnp.sum(accp_ref[...], axis=0, keepdims=True)
            dwq_ref[...] = jnp.sum(accq_ref[...], axis=0, keepdims=True)

    row = pl.BlockSpec((tl, d), lambda i: (i, 0))
    vec = pl.BlockSpec((1, d), lambda i: (0, 0))
    return pl.pallas_call(
        body, grid=(nt,), in_specs=[row, row, vec, row, row, vec], out_specs=[row, row, vec, vec],
        out_shape=[jax.ShapeDtypeStruct((l, d), F32), jax.ShapeDtypeStruct((l, d), BF16),
                   jax.ShapeDtypeStruct((1, d), F32), jax.ShapeDtypeStruct((1, d), F32)],
        scratch_shapes=[pltpu.VMEM((8, d), F32), pltpu.VMEM((8, d), F32)],
        compiler_params=_params("arbitrary"), name=name,
    )(dh, x, w_pre.reshape(1, d), resid, y_prev, w_post_prev.reshape(1, d))


def _rmsnorm_bwd(g, y, w, resid, name):
    l, d = y.shape
    tl = _row_tile(l)
    nt = l // tl

    def body(g_ref, y_ref, w_ref, r_ref, dy_ref, dw_ref, acc_ref):
        i = pl.program_id(0)

        @pl.when(i == 0)
        def _():
            acc_ref[...] = jnp.zeros_like(acc_ref)

        yv = y_ref[...]
        gv = g_ref[...]
        r = lax.rsqrt(jnp.mean(yv * yv, axis=-1, keepdims=True) + EPS)
        nrm = yv * r
        gw = gv * w_ref[...]
        dy_ref[...] = r * (gw - nrm * jnp.mean(gw * nrm, axis=-1, keepdims=True)) + r_ref[...]
        acc_ref[...] += jnp.sum((gv * nrm).reshape(tl // 8, 8, d), axis=0)

        @pl.when(i == nt - 1)
        def _():
            dw_ref[...] = jnp.sum(acc_ref[...], axis=0, keepdims=True)

    row = pl.BlockSpec((tl, d), lambda i: (i, 0))
    vec = pl.BlockSpec((1, d), lambda i: (0, 0))
    return pl.pallas_call(
        body, grid=(nt,), in_specs=[row, row, vec, row], out_specs=[row, vec],
        out_shape=[jax.ShapeDtypeStruct((l, d), F32), jax.ShapeDtypeStruct((1, d), F32)],
        scratch_shapes=[pltpu.VMEM((8, d), F32)], compiler_params=_params("arbitrary"), name=name,
    )(g, y, w.reshape(1, d), resid)


CONV_COLS = 512
HALO = 8
HALO16 = 16
CONV_SUB_ROWS = 64
CONV_SUB_COLS = 256


def _conv_rows(l):
    return _pick(l, (1024, 512, 256, 128))


def _conv_fwd(proj, cw, cb, name):
    l = proj.shape[0]
    tl = _conv_rows(l)
    off = SSM_D_INNER // CONV_COLS

    def body(u_ref, halo_ref, w_ref, b_ref, pre_ref, act_ref, ext_ref):
        i = pl.program_id(1)
        ext_ref[0:HALO, :] = jnp.where(i > 0, halo_ref[...], 0.0)
        ext_ref[HALO:HALO + tl, :] = u_ref[...]
        for r0 in range(0, tl, CONV_SUB_ROWS):
            for c0 in range(0, CONV_COLS, CONV_SUB_COLS):
                cs = slice(c0, c0 + CONV_SUB_COLS)
                ext = ext_ref[r0:r0 + CONV_SUB_ROWS + HALO, cs]
                acc = b_ref[:, cs] + w_ref[SSM_CONV - 1:SSM_CONV, cs] * ext[HALO:]
                for k in range(SSM_CONV - 1):
                    acc = acc + w_ref[k:k + 1, cs] * pltpu.roll(ext, SSM_CONV - 1 - k, 0)[HALO:]
                pre_ref[r0:r0 + CONV_SUB_ROWS, cs] = acc.astype(pre_ref.dtype)
                act_ref[r0:r0 + CONV_SUB_ROWS, cs] = (acc * _sigmoid(acc)).astype(act_ref.dtype)

    hb = tl // HALO
    out = pl.BlockSpec((tl, CONV_COLS), lambda j, i: (i, j))
    return pl.pallas_call(
        body, grid=(SSM_CONV_DIM // CONV_COLS, l // tl),
        in_specs=[pl.BlockSpec((tl, CONV_COLS), lambda j, i: (i, off + j)),
                  pl.BlockSpec((HALO, CONV_COLS), lambda j, i: (jnp.maximum(i * hb - 1, 0), off + j)),
                  pl.BlockSpec((SSM_CONV, CONV_COLS), lambda j, i: (0, j)),
                  pl.BlockSpec((1, CONV_COLS), lambda j, i: (0, j))],
        out_specs=[out, out],
        out_shape=[jax.ShapeDtypeStruct((l, SSM_CONV_DIM), BF16)] * 2,
        scratch_shapes=[pltpu.VMEM((tl + HALO, CONV_COLS), F32)],
        compiler_params=_params("parallel", "arbitrary"), name=name,
    )(proj, proj, cw, cb.reshape(1, SSM_CONV_DIM))


def _conv_bwd(dact, pre, proj, cw, dproj, name):
    l, width = dact.shape
    tl = _conv_rows(l)
    nt = l // tl
    pre_off = 0
    u_off = SSM_D_INNER // CONV_COLS
    hb = tl // HALO
    hb16 = tl // HALO16
    last_hb16 = l // HALO16 - 1

    def body(da_ref, da_h_ref, p_ref, p_h_ref, u_ref, u_h_ref, w_ref, _, du_ref, dw_ref, db_ref, ext_ref, uext_ref):
        i = pl.program_id(1)

        @pl.when(i == 0)
        def _():
            dw_ref[...] = jnp.zeros_like(dw_ref)
            db_ref[...] = jnp.zeros_like(db_ref)

        def dpre_of(da, p):
            s = _sigmoid(p)
            return da * (s * (1.0 + p * (1.0 - s)))

        ext_ref[0:tl, :] = dpre_of(da_ref[...].astype(F32), p_ref[...].astype(F32))
        ext_ref[tl:tl + HALO, :] = jnp.where(
            i < nt - 1, dpre_of(da_h_ref[...].astype(F32)[:HALO], p_h_ref[...].astype(F32)[:HALO]), 0.0)
        uext_ref[0:HALO, :] = jnp.where(i > 0, u_h_ref[...], 0.0)
        uext_ref[HALO:HALO + tl, :] = u_ref[...]
        sub = CONV_SUB_ROWS
        for c0 in range(0, CONV_COLS, CONV_SUB_COLS):
            cs = slice(c0, c0 + CONV_SUB_COLS)
            dws = [jnp.zeros((1, CONV_SUB_COLS), F32) for _ in range(SSM_CONV)]
            dbs = jnp.zeros((1, CONV_SUB_COLS), F32)
            for r0 in range(0, tl, sub):
                dext = ext_ref[r0:r0 + sub + HALO, cs]
                uext = uext_ref[r0:r0 + sub + HALO, cs]
                dp = dext[:sub]
                du = w_ref[SSM_CONV - 1:SSM_CONV, cs] * dp
                dws[SSM_CONV - 1] = dws[SSM_CONV - 1] + jnp.sum(dp * uext[HALO:], axis=0, keepdims=True)
                for k in range(SSM_CONV - 1):
                    j = SSM_CONV - 1 - k
                    du = du + w_ref[k:k + 1, cs] * pltpu.roll(dext, sub + HALO - j, 0)[:sub]
                    dws[k] = dws[k] + jnp.sum(dp * pltpu.roll(uext, j, 0)[HALO:], axis=0, keepdims=True)
                dbs = dbs + jnp.sum(dp, axis=0, keepdims=True)
                du_ref[r0:r0 + sub, cs] = du.astype(du_ref.dtype)
            for k in range(SSM_CONV):
                dw_ref[k:k + 1, cs] += dws[k]
            db_ref[:, cs] += dbs

    return pl.pallas_call(
        body, grid=(width // CONV_COLS, nt),
        in_specs=[pl.BlockSpec((tl, CONV_COLS), lambda j, i: (i, j)),
                  pl.BlockSpec((HALO16, CONV_COLS), lambda j, i: (jnp.minimum((i + 1) * hb16, last_hb16), j)),
                  pl.BlockSpec((tl, CONV_COLS), lambda j, i: (i, pre_off + j)),
                  pl.BlockSpec((HALO16, CONV_COLS), lambda j, i: (jnp.minimum((i + 1) * hb16, last_hb16), pre_off + j)),
                  pl.BlockSpec((tl, CONV_COLS), lambda j, i: (i, u_off + j)),
                  pl.BlockSpec((HALO, CONV_COLS), lambda j, i: (jnp.maximum(i * hb - 1, 0), u_off + j)),
                  pl.BlockSpec((SSM_CONV, CONV_COLS), lambda j, i: (0, pre_off + j)),
                  pl.BlockSpec(memory_space=pl.ANY)],
        out_specs=[pl.BlockSpec((tl, CONV_COLS), lambda j, i: (i, u_off + j)),
                   pl.BlockSpec((SSM_CONV, CONV_COLS), lambda j, i: (0, j)),
                   pl.BlockSpec((1, CONV_COLS), lambda j, i: (0, j))],
        out_shape=[jax.ShapeDtypeStruct(dproj.shape, dproj.dtype), jax.ShapeDtypeStruct((SSM_CONV, width), F32),
                   jax.ShapeDtypeStruct((1, width), F32)],
        scratch_shapes=[pltpu.VMEM((tl + HALO, CONV_COLS), F32), pltpu.VMEM((tl + HALO, CONV_COLS), F32)],
        input_output_aliases={7: 0}, compiler_params=_params("parallel", "arbitrary"), name=name,
    )(dact, dact, pre, pre, proj, proj, cw, dproj)


DT_COL_BLOCK = (SSM_D_INNER + SSM_CONV_DIM) // SSM_DT_PAD


def _split3(v):
    hi = v.astype(BF16)
    rest = v - hi.astype(F32)
    mid = rest.astype(BF16)
    lo = (rest - mid.astype(F32)).astype(BF16)
    return hi, mid, lo


def _ssd_prep(proj, bias, a_log, name):
    l = proj.shape[0]
    nc = l // SSM_CHUNK
    head_dim_log2 = SSM_HEAD_DIM.bit_length() - 1

    def body(p_ref, b_ref, al_ref, dtb_ref, acsb_ref, dtr_ref, acsr_ref):
        v = p_ref[...] + b_ref[...]
        dt_hi, dt_mid, _ = _split3(jnp.maximum(v, 0.0) + jnp.log1p(jnp.exp(-jnp.abs(v))))
        dt = dt_hi.astype(F32) + dt_mid.astype(F32)
        ri = lax.broadcasted_iota(jnp.int32, (SSM_CHUNK, SSM_CHUNK), 0)
        cj = lax.broadcasted_iota(jnp.int32, (SSM_CHUNK, SSM_CHUNK), 1)
        tri = (ri >= cj).astype(BF16)
        acs_pieces = _split3(sum(jnp.dot(tri, piece, preferred_element_type=F32)
                                 for piece in _split3(dt * (-jnp.exp(al_ref[...])))))
        acs = sum(piece.astype(F32) for piece in acs_pieces)
        head_of_lane = lax.shift_right_logical(lax.broadcasted_iota(jnp.int32, (SSM_DT_PAD, SSM_D_INNER), 1), head_dim_log2)
        spread = (head_of_lane == lax.broadcasted_iota(jnp.int32, (SSM_DT_PAD, SSM_D_INNER), 0)).astype(BF16)
        dtb_ref[...] = sum(jnp.dot(piece, spread, preferred_element_type=F32) for piece in (dt_hi, dt_mid))
        acsb_ref[...] = sum(jnp.dot(piece, spread, preferred_element_type=F32) for piece in acs_pieces)
        dt_rows = dt.T
        acs_rows = acs.T
        for g in range(SSM_GROUPS):
            heads = slice(g * SSM_HPG, (g + 1) * SSM_HPG)
            dtr_ref[g] = dt_rows[heads, :]
            acsr_ref[g] = acs_rows[heads, :]

    rows = pl.BlockSpec((SSM_GROUPS, SSM_HPG, SSM_CHUNK), lambda c: (0, 0, c))
    dense = pl.BlockSpec((SSM_CHUNK, SSM_D_INNER), lambda c: (c, 0))
    return pl.pallas_call(
        body, grid=(nc,),
        in_specs=[pl.BlockSpec((SSM_CHUNK, SSM_DT_PAD), lambda c: (c, DT_COL_BLOCK)),
                  pl.BlockSpec((1, SSM_DT_PAD), lambda c: (0, 0)),
                  pl.BlockSpec((1, SSM_DT_PAD), lambda c: (0, 0))],
        out_specs=[dense, dense, rows, rows],
        out_shape=[jax.ShapeDtypeStruct((l, SSM_D_INNER), F32), jax.ShapeDtypeStruct((l, SSM_D_INNER), F32),
                   jax.ShapeDtypeStruct((SSM_GROUPS, SSM_HPG, l), F32),
                   jax.ShapeDtypeStruct((SSM_GROUPS, SSM_HPG, l), F32)],
        compiler_params=_params("parallel"), name=name,
    )(proj, bias, a_log)


def _dt_bwd(ddt, proj, bias, dproj, name):
    l = proj.shape[0]
    tl = _row_tile(l)

    def body(g_ref, p_ref, b_ref, _, o_ref, db_ref):
        @pl.when(pl.program_id(0) == 0)
        def _():
            db_ref[...] = jnp.zeros_like(db_ref)

        d = g_ref[...] * _sigmoid(p_ref[...] + b_ref[...])
        o_ref[...] = d.astype(o_ref.dtype)
        db_ref[...] += jnp.sum(d, axis=0, keepdims=True)

    return pl.pallas_call(
        body, grid=(l // tl,),
        in_specs=[pl.BlockSpec((tl, SSM_DT_PAD), lambda i: (i, 0)),
                  pl.BlockSpec((tl, SSM_DT_PAD), lambda i: (i, DT_COL_BLOCK)),
                  pl.BlockSpec((1, SSM_DT_PAD), lambda i: (0, 0)),
                  pl.BlockSpec(memory_space=pl.ANY)],
        out_specs=[pl.BlockSpec((tl, SSM_DT_PAD), lambda i: (i, DT_COL_BLOCK)),
                   pl.BlockSpec((1, SSM_DT_PAD), lambda i: (0, 0))],
        out_shape=[jax.ShapeDtypeStruct(dproj.shape, dproj.dtype), jax.ShapeDtypeStruct((1, SSM_DT_PAD), F32)],
        input_output_aliases={3: 0}, compiler_params=_params("arbitrary"), name=name,
    )(ddt, proj, bias, dproj)


GP = SSM_HPG * SSM_HEAD_DIM
HEAD_DIM_LOG2 = SSM_HEAD_DIM.bit_length() - 1
CHUNK_LOG2 = SSM_CHUNK.bit_length() - 1
GPS = 8
B_BLOCK0 = SSM_D_INNER // SSM_STATE
C_BLOCK0 = (SSM_D_INNER + SSM_BC_DIM) // SSM_STATE


def _chunk_iotas():
    ri = lax.broadcasted_iota(jnp.int32, (SSM_CHUNK, SSM_CHUNK), 0)
    cj = lax.broadcasted_iota(jnp.int32, (SSM_CHUNK, SSM_CHUNK), 1)
    return ri, cj


def _head_decay(acsb, acs_r, r, ri, cj):
    pair = acsb[:, (r // 2) * LANES:(r // 2 + 1) * LANES]
    mine_low = r % 2 == 0
    lane = lax.broadcasted_iota(jnp.int32, (1, LANES), 1)
    col = jnp.where((lane < SSM_HEAD_DIM) == mine_low, pair, pltpu.roll(pair, SSM_HEAD_DIM, 1))
    return jnp.exp(jnp.where(ri >= cj, col - acs_r[r:r + 1, :], NEG_INF))


def _head_masked_rows(v, dtype):
    head_of_lane = lax.shift_right_logical(lax.broadcasted_iota(jnp.int32, (1, GP), 1), HEAD_DIM_LOG2)
    return jnp.concatenate([jnp.where(head_of_lane == r, v, 0.0).astype(dtype) for r in range(SSM_HPG)], axis=0)


def _ssd_fwd(xbc, dtb, acsb, acs_r, d_lanes, proj, gate_w, name, ride=()):
    l = xbc.shape[0]
    nc = l // SSM_CHUNK
    assert GPS == SSM_GROUPS

    n_ride = len(ride)

    def body(*refs):
        x_ref, b_ref, c_ref, dtb_ref, acsb_ref, acsr_ref, d_ref, z_ref, gw_ref = refs[:9]
        ride_in = refs[9:9 + n_ride]
        y_ref, act_ref, hin_ref = refs[9 + n_ride:12 + n_ride]
        ride_out = refs[12 + n_ride:12 + 2 * n_ride]
        h_ref = refs[12 + 2 * n_ride]
        ride_sems = refs[13 + 2 * n_ride:]
        c = pl.program_id(0)
        if n_ride:
            @pl.when(c == 0)
            def _():
                _gather_between_chips(ride_in, ride_out, *ride_sems, wait=False)

            @pl.when(c == nc - 1)
            def _():
                _gather_between_chips(ride_in, ride_out, *ride_sems, wait=True)

        ri, cj = _chunk_iotas()
        for k in range(GPS):
            g = k
            cols = slice(k * GP, (k + 1) * GP)
            ncols = slice(k * SSM_STATE, (k + 1) * SSM_STATE)

            @pl.when(c == 0)
            def _():
                h_ref[g] = jnp.zeros((SSM_STATE, GP), F32)

            xv = x_ref[:, cols].astype(F32)
            bb = b_ref[:, ncols].astype(BF16)
            cb16 = c_ref[:, ncols].astype(BF16)
            acs_v = acsb_ref[:, cols]
            acs_r_v = acsr_ref[k]
            lastb = acs_v[SSM_CHUNK - 1:SSM_CHUNK, :]
            xd = xv * dtb_ref[:, cols]
            cb = lax.dot_general(cb16, bb, NT_DIMS, preferred_element_type=F32)
            hin = h_ref[g]
            hin_ref[0, k] = hin
            yoff = jnp.dot(cb16, hin.astype(BF16), preferred_element_type=F32)
            ms = [(cb * _head_decay(acs_v, acs_r_v, r, ri, cj)).astype(BF16) for r in range(SSM_HPG)]
            ydiag = jnp.dot(jnp.concatenate(ms, axis=1), _head_masked_rows(xd, BF16), preferred_element_type=F32)
            y_ref[:, cols] = ydiag + jnp.exp(acs_v) * yoff + d_ref[k] * xv
            h_ref[g] = hin * jnp.exp(lastb) + _bdot_tn(bb, xd * jnp.exp(lastb - acs_v))
        z = z_ref[...]
        yg = y_ref[...] * (z * _sigmoid(z))
        r = lax.rsqrt(jnp.mean(yg * yg, axis=-1, keepdims=True) + EPS)
        act_ref[...] = (yg * r * gw_ref[...]).astype(act_ref.dtype)

    lanes = pl.BlockSpec((SSM_CHUNK, SSM_D_INNER), lambda c: (c, 0))
    return pl.pallas_call(
        body, grid=(nc,),
        in_specs=[lanes,
                  pl.BlockSpec((SSM_CHUNK, SSM_BC_DIM), lambda c: (c, B_BLOCK0 // GPS)),
                  pl.BlockSpec((SSM_CHUNK, SSM_BC_DIM), lambda c: (c, C_BLOCK0 // GPS)),
                  lanes, lanes,
                  pl.BlockSpec((SSM_GROUPS, SSM_HPG, SSM_CHUNK), lambda c: (0, 0, c)),
                  pl.BlockSpec((SSM_GROUPS, 1, GP), lambda c: (0, 0, 0)),
                  lanes, pl.BlockSpec((1, SSM_D_INNER), lambda c: (0, 0))] + [ANY] * n_ride,
        out_specs=[lanes, lanes, pl.BlockSpec((1, SSM_GROUPS, SSM_STATE, GP), lambda c: (c, 0, 0, 0))] + [ANY] * n_ride,
        out_shape=[jax.ShapeDtypeStruct((l, SSM_D_INNER), F32), jax.ShapeDtypeStruct((l, SSM_D_INNER), BF16),
                   jax.ShapeDtypeStruct((nc, SSM_GROUPS, SSM_STATE, GP), F32)]
        + [jax.ShapeDtypeStruct((N_CHIPS,) + a.shape, a.dtype) for a in ride],
        scratch_shapes=[pltpu.VMEM((SSM_GROUPS, SSM_STATE, GP), F32)] + (_gather_sems(n_ride) if n_ride else []),
        compiler_params=_params("arbitrary"), name=name,
    )(xbc, xbc, xbc, dtb, acsb, acs_r, d_lanes, proj, gate_w.reshape(1, SSM_D_INNER), *ride)


def _ssd_bwd(xbc, dtb, acsb, dtr, acs_r, a_log, d_lanes, hin, dact, y, proj, gate_w, name):
    l = xbc.shape[0]
    nc = l // SSM_CHUNK

    def body(x_ref, b_ref, c_ref, dtb_ref, acsb_ref, dtr_ref, acsr_ref, alc_ref, d_ref, hin_ref,
             dact_ref, y_ref, z_ref, gw_ref,
             dxbc_ref, ddt_ref, dal_ref, dd_ref, dproj_ref, dgw_ref, dh_ref, dy_ref, acc_ref):
        c = pl.program_id(0)
        dx_ref = dxbc_ref.at[:, 0:SSM_D_INNER]
        db_ref = dxbc_ref.at[:, SSM_D_INNER:SSM_D_INNER + SSM_BC_DIM]
        dc_ref = dxbc_ref.at[:, SSM_D_INNER + SSM_BC_DIM:SSM_CONV_DIM]

        @pl.when(c == 0)
        def _():
            dal_ref[...] = jnp.zeros_like(dal_ref)
            dd_ref[...] = jnp.zeros_like(dd_ref)
            acc_ref[...] = jnp.zeros_like(acc_ref)

        z = z_ref[...]
        yv = y_ref[...]
        s = _sigmoid(z)
        sz = z * s
        yg = yv * sz
        r = lax.rsqrt(jnp.mean(yg * yg, axis=-1, keepdims=True) + EPS)
        nrm = yg * r
        gv = dact_ref[...]
        gw = gv * gw_ref[...]
        dyg = r * (gw - nrm * jnp.mean(gw * nrm, axis=-1, keepdims=True))
        dy_ref[...] = dyg * sz
        dproj_ref[...] = (dyg * yv * (s * (1.0 + z * (1.0 - s)))).astype(dproj_ref.dtype)
        acc_ref[...] += jnp.sum((gv * nrm).reshape(SSM_CHUNK // 8, 8, SSM_D_INNER), axis=0)

        @pl.when(c == nc - 1)
        def _():
            dgw_ref[...] = jnp.sum(acc_ref[...], axis=0, keepdims=True)

        for k in range(GPS):
            one_group(c, k, k, x_ref, b_ref, c_ref, dtb_ref, acsb_ref, dtr_ref, acsr_ref, alc_ref, d_ref,
                      hin_ref, dy_ref, dx_ref, db_ref, dc_ref, ddt_ref, dal_ref, dd_ref, dh_ref)

    def one_group(c, g, k, x_ref, b_ref, c_ref, dtb_ref, acsb_ref, dtr_ref, acsr_ref, alc_ref, d_ref, hin_ref, dy_ref,
                  dx_ref, db_ref, dc_ref, ddt_ref, dal_ref, dd_ref, dh_ref):
        cols = slice(k * GP, (k + 1) * GP)
        ncols = slice(k * SSM_STATE, (k + 1) * SSM_STATE)

        @pl.when(c == 0)
        def _():
            dh_ref[g] = jnp.zeros((SSM_STATE, GP), F32)

        xv = x_ref[:, cols].astype(F32)
        dyv = dy_ref[:, cols]
        bb = b_ref[:, ncols].astype(BF16)
        cb16 = c_ref[:, ncols].astype(BF16)
        dtb = dtb_ref[:, cols]
        acsb = acsb_ref[:, cols]
        dtr_v = dtr_ref[k]
        acs_r = acsr_ref[k]
        a_col = -jnp.exp(alc_ref[k])
        ri, cj = _chunk_iotas()
        head_of_lane = lax.shift_right_logical(lax.broadcasted_iota(jnp.int32, (SSM_HPG, GP), 1), HEAD_DIM_LOG2)
        ind_t = (head_of_lane == lax.broadcasted_iota(jnp.int32, (SSM_HPG, GP), 0)).astype(BF16)
        lastb = acsb[SSM_CHUNK - 1:SSM_CHUNK, :]
        ecb = jnp.exp(acsb)
        dteb = jnp.exp(lastb - acsb)
        xd = xv * dtb
        xw = xd * dteb
        cb = lax.dot_general(cb16, bb, NT_DIMS, preferred_element_type=F32)
        hin_v = hin_ref[0, k]
        dhn = dh_ref[g]
        h16 = hin_v.astype(BF16)
        dh16 = dhn.astype(BF16)
        ch = jnp.dot(cb16, h16, preferred_element_type=F32)
        bdh = jnp.dot(bb, dh16, preferred_element_type=F32)
        dym = _head_masked_rows(dyv, BF16)
        g_all = lax.dot_general(dym, xd.astype(BF16), NT_DIMS, preferred_element_type=F32)
        gl_sum = jnp.zeros((SSM_CHUNK, SSM_CHUNK), F32)
        ms, qs = [], []
        for r in range(SSM_HPG):
            decay = _head_decay(acsb, acs_r, r, ri, cj)
            gl = g_all[r * SSM_CHUNK:(r + 1) * SSM_CHUNK] * decay
            gl_sum = gl_sum + gl
            ms.append((cb * decay).astype(BF16))
            qs.append((gl * cb).astype(BF16))
        dxd = lax.dot_general(jnp.concatenate(ms, axis=0), dym, TN_DIMS, preferred_element_type=F32) + dteb * bdh
        cum = jnp.dot(jnp.concatenate(qs, axis=0), (ri < cj).astype(BF16), preferred_element_type=F32)
        sub4 = lax.broadcasted_iota(jnp.int32, (SSM_HPG, 1), 0)
        da = jnp.zeros((SSM_HPG, SSM_CHUNK), F32)
        for r in range(SSM_HPG):
            rect = jnp.sum(jnp.where(ri >= cj, cum[r * SSM_CHUNK:(r + 1) * SSM_CHUNK], 0.0), axis=0, keepdims=True)
            da = da + jnp.where(sub4 == r, rect, 0.0)
        z2 = xw * bdh
        sub8 = lax.broadcasted_iota(jnp.int32, (8, 1), 0)
        col_sums = (jnp.where(sub8 == 0, jnp.sum(z2, axis=0, keepdims=True), 0.0)
                    + jnp.where(sub8 == 1, jnp.sum(dhn * hin_v, axis=0, keepdims=True), 0.0)
                    + jnp.where(sub8 == 2, jnp.sum(dyv * xv, axis=0, keepdims=True), 0.0))
        summands = jnp.concatenate([dyv * ecb * ch - z2, dxd * xv, col_sums], axis=0)
        sums = lax.dot_general(ind_t, summands.astype(BF16), NT_DIMS, preferred_element_type=F32)
        per_pos = sums[:, :2 * SSM_CHUNK]
        totals = sums[:, 2 * SSM_CHUNK:]
        e_last = totals[:, 0:1] + jnp.exp(acs_r[:, SSM_CHUNK - 1:SSM_CHUNK]) * totals[:, 1:2]
        da = (da + e_last + jnp.dot(per_pos[:, :SSM_CHUNK], (ri >= cj).astype(F32), preferred_element_type=F32,
                                    precision=lax.Precision.HIGHEST))
        ddt_ref[k] = a_col * da + per_pos[:, SSM_CHUNK:]
        dal_ref[g] += a_col * jnp.sum(da * dtr_v, axis=1, keepdims=True)
        dd_ref[g] += totals[:, 2:3]
        dx_ref[:, cols] = (dxd * dtb + d_ref[k] * dyv).astype(dx_ref.dtype)
        w16 = (ecb * dyv).astype(BF16)
        xw16 = xw.astype(BF16)
        gl16 = gl_sum.astype(BF16)
        dc_ref[:, ncols] = (jnp.dot(gl16, bb, preferred_element_type=F32)
                            + lax.dot_general(w16, h16, NT_DIMS, preferred_element_type=F32)).astype(dc_ref.dtype)
        db_ref[:, ncols] = (lax.dot_general(gl16, cb16, TN_DIMS, preferred_element_type=F32)
                            + lax.dot_general(xw16, dh16, NT_DIMS, preferred_element_type=F32)).astype(db_ref.dtype)
        dh_ref[g] = dhn * jnp.exp(lastb) + lax.dot_general(cb16, w16, TN_DIMS, preferred_element_type=F32)

    def rev(c):
        return nc - 1 - c

    small = pl.BlockSpec((SSM_GROUPS, SSM_HPG, 1), lambda c: (0, 0, 0))
    lanes = pl.BlockSpec((SSM_CHUNK, SSM_D_INNER), lambda c: (rev(c), 0))
    rows = pl.BlockSpec((SSM_GROUPS, SSM_HPG, SSM_CHUNK), lambda c: (0, 0, rev(c)))
    vec = pl.BlockSpec((1, SSM_D_INNER), lambda c: (0, 0))
    return pl.pallas_call(
        body, grid=(nc,),
        in_specs=[lanes,
                  pl.BlockSpec((SSM_CHUNK, SSM_BC_DIM), lambda c: (rev(c), B_BLOCK0 // GPS)),
                  pl.BlockSpec((SSM_CHUNK, SSM_BC_DIM), lambda c: (rev(c), C_BLOCK0 // GPS)),
                  lanes, lanes, rows, rows,
                  pl.BlockSpec((SSM_GROUPS, SSM_HPG, 1), lambda c: (0, 0, 0)),
                  pl.BlockSpec((SSM_GROUPS, 1, GP), lambda c: (0, 0, 0)),
                  pl.BlockSpec((1, SSM_GROUPS, SSM_STATE, GP), lambda c: (rev(c), 0, 0, 0)),
                  lanes, lanes, lanes, vec],
        out_specs=[pl.BlockSpec((SSM_CHUNK, SSM_CONV_DIM), lambda c: (rev(c), 0)),
                   rows, small, small, lanes, vec],
        out_shape=[jax.ShapeDtypeStruct((l, SSM_CONV_DIM), BF16), jax.ShapeDtypeStruct((SSM_GROUPS, SSM_HPG, l), F32),
                   jax.ShapeDtypeStruct((SSM_GROUPS, SSM_HPG, 1), F32),
                   jax.ShapeDtypeStruct((SSM_GROUPS, SSM_HPG, 1), F32),
                   jax.ShapeDtypeStruct((l, SSM_IN_PAD), BF16), jax.ShapeDtypeStruct((1, SSM_D_INNER), F32)],
        scratch_shapes=[pltpu.VMEM((SSM_GROUPS, SSM_STATE, GP), F32), pltpu.VMEM((SSM_CHUNK, SSM_D_INNER), F32),
                        pltpu.VMEM((8, SSM_D_INNER), F32)],
        compiler_params=_params("arbitrary"), name=name,
    )(xbc, xbc, xbc, dtb, acsb, dtr, acs_r, a_log.reshape(SSM_GROUPS, SSM_HPG, 1), d_lanes, hin, dact, y, proj,
      gate_w.reshape(1, SSM_D_INNER))


LANES = 128
ROPE_Q_CHUNKS = ATT_WIDTH // LANES
ROPE_K_CHUNKS = ATT_KV_WIDTH // LANES


def _rope_tables(positions):
    inv = ROPE_THETA ** (-jnp.arange(0, ROPE_DIM, 2, dtype=F32) / ROPE_DIM)
    ang = positions.astype(F32)[:, None] * inv
    cos, sin = jnp.cos(ang), jnp.sin(ang)
    l = positions.shape[0]
    rest = ATT_HEAD_DIM - ROPE_DIM
    ones, zeros = jnp.ones((l, rest), F32), jnp.zeros((l, rest), F32)
    z8 = jnp.zeros((l, ROPE_HALF), F32)
    cos_f = jnp.concatenate([cos, cos, ones], axis=1)
    sin_a = jnp.concatenate([-sin, z8, zeros], axis=1)
    sin_b = jnp.concatenate([z8, sin, zeros], axis=1)
    reps = LANES // ATT_HEAD_DIM
    return tuple(jnp.tile(t, (1, reps)) for t in (cos_f, sin_a, sin_b))


ATT_QKV4 = 3 * ATT_WIDTH


def _both_halves(chunk):
    lane = lax.broadcasted_iota(jnp.int32, (1, LANES), 1)
    swapped = pltpu.roll(chunk, ATT_HEAD_DIM, 1)
    return jnp.where(lane < ATT_HEAD_DIM, chunk, swapped), jnp.where(lane < ATT_HEAD_DIM, swapped, chunk)


def _rope_fwd(proj, tables, name):
    l = proj.shape[0]
    tl = _pick(l, (256, 128))

    def body(p_ref, c_ref, sa_ref, sb_ref, o_ref):
        cos_f, sin_a, sin_b = c_ref[...], sa_ref[...], sb_ref[...]

        def rope(t):
            return t * cos_f + pltpu.roll(t, LANES - ROPE_HALF, 1) * sin_a + pltpu.roll(t, ROPE_HALF, 1) * sin_b

        for k in range(ROPE_Q_CHUNKS):
            sl = slice(k * LANES, (k + 1) * LANES)
            o_ref[:, sl] = (rope(p_ref[:, sl]) * Q_SCALE).astype(o_ref.dtype)
        for part in range(2):
            for k in range(ROPE_K_CHUNKS):
                src = ATT_WIDTH + part * ATT_KV_WIDTH + k * LANES
                t = p_ref[:, src:src + LANES]
                if part == 0:
                    t = rope(t)
                for head, dup in enumerate(_both_halves(t.astype(o_ref.dtype))):
                    dst = (1 + part) * ATT_WIDTH + (2 * k + head) * ATT_GQA * ATT_HEAD_DIM
                    o_ref[:, dst:dst + LANES] = dup
                    o_ref[:, dst + LANES:dst + 2 * LANES] = dup

    tab = pl.BlockSpec((tl, LANES), lambda i: (i, 0))
    return pl.pallas_call(
        body, grid=(l // tl,), in_specs=[pl.BlockSpec((tl, ATT_IN_DIM), lambda i: (i, 0)), tab, tab, tab],
        out_specs=pl.BlockSpec((tl, ATT_QKV4), lambda i: (i, 0)),
        out_shape=jax.ShapeDtypeStruct((l, ATT_QKV4), BF16), compiler_params=_params("parallel"), name=name,
    )(proj, *tables)


def _rope_bwd(dq, dk4, dv4, dgate, tables, name):
    l = dq.shape[0]
    tl = _pick(l, (256, 128))

    def body(dq_ref, dk_ref, dv_ref, dg_ref, c_ref, sa_ref, sb_ref, o_ref):
        cos_f, sin_a, sin_b = c_ref[...], sa_ref[...], sb_ref[...]
        lane = lax.broadcasted_iota(jnp.int32, (1, LANES), 1)

        def unrope(t):
            return t * cos_f + pltpu.roll(t * sin_a, ROPE_HALF, 1) + pltpu.roll(t * sin_b, LANES - ROPE_HALF, 1)

        def head_total(ref, kvh):
            base = kvh * ATT_GQA * ATT_HEAD_DIM
            s = ref[:, base:base + LANES] + ref[:, base + LANES:base + 2 * LANES]
            return s + pltpu.roll(s, ATT_HEAD_DIM, 1)

        for k in range(ROPE_Q_CHUNKS):
            sl = slice(k * LANES, (k + 1) * LANES)
            o_ref[:, sl] = unrope(dq_ref[:, sl] * Q_SCALE).astype(o_ref.dtype)
        for k in range(ROPE_K_CHUNKS):
            dk = jnp.where(lane < ATT_HEAD_DIM, head_total(dk_ref, 2 * k), head_total(dk_ref, 2 * k + 1))
            dv = jnp.where(lane < ATT_HEAD_DIM, head_total(dv_ref, 2 * k), head_total(dv_ref, 2 * k + 1))
            o_ref[:, ATT_WIDTH + k * LANES:ATT_WIDTH + (k + 1) * LANES] = unrope(dk).astype(o_ref.dtype)
            at = ATT_WIDTH + ATT_KV_WIDTH + k * LANES
            o_ref[:, at:at + LANES] = dv.astype(o_ref.dtype)
        o_ref[:, ATT_QKV:ATT_IN_DIM] = dg_ref[...].astype(o_ref.dtype)

    tab = pl.BlockSpec((tl, LANES), lambda i: (i, 0))
    wide = pl.BlockSpec((tl, ATT_WIDTH), lambda i: (i, 0))
    return pl.pallas_call(
        body, grid=(l // tl,), in_specs=[wide, wide, wide, wide, tab, tab, tab],
        out_specs=pl.BlockSpec((tl, ATT_IN_DIM), lambda i: (i, 0)),
        out_shape=jax.ShapeDtypeStruct((l, ATT_IN_DIM), BF16), compiler_params=_params("parallel"), name=name,
    )(dq, dk4, dv4, dgate, *tables)


GATE_HALF = ATT_WIDTH // 2
GATE_COL_BLOCK = ATT_QKV // GATE_HALF


ATT_STACK = ATT_GQA * ATT_BLOCK
BLOCK_LOG2 = ATT_BLOCK.bit_length() - 1


def _stack_masks(n):
    ri = lax.broadcasted_iota(jnp.int32, (ATT_STACK, ATT_BLOCK), 0) & (ATT_BLOCK - 1)
    cj = lax.broadcasted_iota(jnp.int32, (ATT_STACK, ATT_BLOCK), 1)
    return (cj > ri) & (n > 0), cj <= ri


def _stack_sinks(sink_ref, kvh):
    blk = lax.shift_right_logical(lax.broadcasted_iota(jnp.int32, (ATT_STACK, 1), 0), BLOCK_LOG2)
    col = jnp.zeros((ATT_STACK, 1), F32)
    for r in range(ATT_GQA):
        col = jnp.where(blk == r, sink_ref[kvh * ATT_GQA + r], col)
    return col


def _stack_fold(stack):
    head_of_lane = lax.shift_right_logical(lax.broadcasted_iota(jnp.int32, (1, GP), 1), HEAD_DIM_LOG2)
    out = jnp.zeros((ATT_BLOCK, GP), F32)
    for r in range(ATT_GQA):
        out = jnp.where(head_of_lane == r, stack[r * ATT_BLOCK:(r + 1) * ATT_BLOCK], out)
    return out


def _attn_fwd(qkv, proj, sinks, name):
    l = qkv.shape[0]
    nb = l // ATT_BLOCK

    def body(sink_ref, q_ref, kp_ref, kc_ref, vp_ref, vc_ref, g0_ref, g1_ref, og_ref, o_ref, lse_ref):
        n = pl.program_id(0)
        mask_p, mask_c = _stack_masks(n)
        ones = jnp.ones((ATT_BLOCK, LANES), BF16)
        for kvh in range(ATT_KV_HEADS):
            cols = slice(kvh * GP, (kvh + 1) * GP)
            q_stack = _head_masked_rows(q_ref[:, cols], BF16)
            sp = jnp.where(mask_p, lax.dot_general(q_stack, kp_ref[:, cols], NT_DIMS, preferred_element_type=F32), NEG_INF)
            sc = jnp.where(mask_c, lax.dot_general(q_stack, kc_ref[:, cols], NT_DIMS, preferred_element_type=F32), NEG_INF)
            sink = _stack_sinks(sink_ref, kvh)
            m = jnp.maximum(jnp.max(jnp.maximum(sp, sc), axis=1, keepdims=True), sink)
            pp = jnp.exp(sp - m).astype(BF16)
            pc = jnp.exp(sc - m).astype(BF16)
            acc = (jnp.dot(pp, jnp.concatenate([vp_ref[:, cols], ones], axis=1), preferred_element_type=F32)
                   + jnp.dot(pc, jnp.concatenate([vc_ref[:, cols], ones], axis=1), preferred_element_type=F32))
            den = acc[:, GP:] + jnp.exp(sink - m)
            inv = 1.0 / den
            o_ref[:, cols] = _stack_fold(acc[:, :GP] * jnp.concatenate([inv, inv], axis=1))
            lse = m + jnp.log(den)
            lse_ref[:, cols] = _stack_fold(jnp.concatenate([lse, lse], axis=1))
        for half, g_ref in enumerate((g0_ref, g1_ref)):
            sl = slice(half * GATE_HALF, (half + 1) * GATE_HALF)
            gate = g_ref[...]
            og_ref[:, sl] = (o_ref[:, sl] * (gate * _sigmoid(gate))).astype(og_ref.dtype)

    def prev(n):
        return jnp.maximum(n - 1, 0)

    wide = pl.BlockSpec((ATT_BLOCK, ATT_WIDTH), lambda n: (n, 0))
    return pl.pallas_call(
        body, grid=(nb,),
        in_specs=[pl.BlockSpec(memory_space=pltpu.SMEM), wide,
                  pl.BlockSpec((ATT_BLOCK, ATT_WIDTH), lambda n: (prev(n), 1)),
                  pl.BlockSpec((ATT_BLOCK, ATT_WIDTH), lambda n: (n, 1)),
                  pl.BlockSpec((ATT_BLOCK, ATT_WIDTH), lambda n: (prev(n), 2)),
                  pl.BlockSpec((ATT_BLOCK, ATT_WIDTH), lambda n: (n, 2)),
                  pl.BlockSpec((ATT_BLOCK, GATE_HALF), lambda n: (n, GATE_COL_BLOCK)),
                  pl.BlockSpec((ATT_BLOCK, GATE_HALF), lambda n: (n, GATE_COL_BLOCK + 1))],
        out_specs=[wide, wide, wide],
        out_shape=[jax.ShapeDtypeStruct((l, ATT_WIDTH), BF16), jax.ShapeDtypeStruct((l, ATT_WIDTH), F32),
                   jax.ShapeDtypeStruct((l, ATT_WIDTH), F32)],
        compiler_params=_params("parallel"), name=name,
    )(sinks, qkv, qkv, qkv, qkv, qkv, proj, proj)


def _attn_bwd(qkv, proj, sinks, o, lse, dog, name, ride=()):
    l = qkv.shape[0]
    nb = l // ATT_BLOCK
    n_ride = len(ride)

    def body(*refs):
        sink_ref, q_ref, kp_ref, kc_ref, vp_ref, vc_ref, g0_ref, g1_ref, o_ref, lse_ref, dog_ref = refs[:11]
        ride_in = refs[11:11 + n_ride]
        dq_ref, dk_ref, dv_ref, dg_ref, ds_ref = refs[11 + n_ride:16 + n_ride]
        ride_out = refs[16 + n_ride:16 + 2 * n_ride]
        ck_ref, cv_ref, do_ref = refs[16 + 2 * n_ride:19 + 2 * n_ride]
        ride_sems = refs[19 + 2 * n_ride:]
        n = pl.program_id(0)

        @pl.when(n == 0)
        def _():
            ds_ref[...] = jnp.zeros_like(ds_ref)
            ck_ref[...] = jnp.zeros_like(ck_ref)
            cv_ref[...] = jnp.zeros_like(cv_ref)
            if n_ride:
                _scatter_between_chips(ride_in, ride_out, *ride_sems, wait=False)

        @pl.when(n == nb)
        def _():
            dk_ref[...] = ck_ref[...]
            dv_ref[...] = cv_ref[...]
            if n_ride:
                _scatter_between_chips(ride_in, ride_out, *ride_sems, wait=True)

        @pl.when(n < nb)
        def _():
            mask_p, mask_c = _stack_masks(n)
            lane = lax.broadcasted_iota(jnp.int32, (1, ATT_Q_HEADS), 1)
            for half, g_ref in enumerate((g0_ref, g1_ref)):
                sl = slice(half * GATE_HALF, (half + 1) * GATE_HALF)
                gate = g_ref[...]
                s = _sigmoid(gate)
                dogv = dog_ref[:, sl]
                do_ref[:, sl] = dogv * (gate * s)
                dg_ref[:, sl] = dogv * o_ref[:, sl] * (s * (1.0 + gate * (1.0 - s)))
            ds_acc = jnp.zeros((1, ATT_Q_HEADS), F32)
            for kvh in range(ATT_KV_HEADS):
                cols = slice(kvh * GP, (kvh + 1) * GP)
                kp, kc, vp, vc = kp_ref[:, cols], kc_ref[:, cols], vp_ref[:, cols], vc_ref[:, cols]
                q_stack = _head_masked_rows(q_ref[:, cols], BF16)
                do_g = do_ref[:, cols]
                do_stack = _head_masked_rows(do_g, BF16)
                lse_g = lse_ref[:, cols]
                lse_stack = jnp.concatenate(
                    [_both_halves(lse_g[:, (r // 2) * LANES:(r // 2 + 1) * LANES])[r % 2] for r in range(ATT_GQA)], axis=0)
                pp = jnp.exp(jnp.where(
                    mask_p, lax.dot_general(q_stack, kp, NT_DIMS, preferred_element_type=F32) - lse_stack, NEG_INF))
                pc = jnp.exp(jnp.where(
                    mask_c, lax.dot_general(q_stack, kc, NT_DIMS, preferred_element_type=F32) - lse_stack, NEG_INF))
                dpp = lax.dot_general(do_stack, vp, NT_DIMS, preferred_element_type=F32)
                dpc = lax.dot_general(do_stack, vc, NT_DIMS, preferred_element_type=F32)
                delta = jnp.sum(pp * dpp + pc * dpc, axis=1, keepdims=True)
                dsp = (pp * (dpp - delta)).astype(BF16)
                dsc = (pc * (dpc - delta)).astype(BF16)
                dq_ref[:, cols] = _stack_fold(jnp.dot(dsp, kp, preferred_element_type=F32)
                                              + jnp.dot(dsc, kc, preferred_element_type=F32))
                dk_ref[:, cols] = ck_ref[:, cols] + lax.dot_general(dsp, q_stack, TN_DIMS, preferred_element_type=F32)
                dv_ref[:, cols] = cv_ref[:, cols] + lax.dot_general(pp.astype(BF16), do_stack, TN_DIMS,
                                                                    preferred_element_type=F32)
                ck_ref[:, cols] = lax.dot_general(dsc, q_stack, TN_DIMS, preferred_element_type=F32)
                cv_ref[:, cols] = lax.dot_general(pc.astype(BF16), do_stack, TN_DIMS, preferred_element_type=F32)
                t = jnp.exp(_stack_sinks(sink_ref, kvh) - lse_stack) * delta
                for r in range(ATT_GQA):
                    tot = jnp.sum(t[r * ATT_BLOCK:(r + 1) * ATT_BLOCK], axis=0, keepdims=True)
                    ds_acc = ds_acc - jnp.where(lane == kvh * ATT_GQA + r, tot[:, :ATT_Q_HEADS], 0.0)
            ds_ref[...] += ds_acc

    def cur(n):
        return jnp.minimum(n, nb - 1)

    def prev(n):
        return jnp.maximum(n - 1, 0)

    wide = pl.BlockSpec((ATT_BLOCK, ATT_WIDTH), lambda n: (cur(n), 0))
    late = pl.BlockSpec((ATT_BLOCK, ATT_WIDTH), lambda n: (prev(n), 0))
    return pl.pallas_call(
        body, grid=(nb + 1,),
        in_specs=[pl.BlockSpec(memory_space=pltpu.SMEM), wide,
                  pl.BlockSpec((ATT_BLOCK, ATT_WIDTH), lambda n: (prev(cur(n)), 1)),
                  pl.BlockSpec((ATT_BLOCK, ATT_WIDTH), lambda n: (cur(n), 1)),
                  pl.BlockSpec((ATT_BLOCK, ATT_WIDTH), lambda n: (prev(cur(n)), 2)),
                  pl.BlockSpec((ATT_BLOCK, ATT_WIDTH), lambda n: (cur(n), 2)),
                  pl.BlockSpec((ATT_BLOCK, GATE_HALF), lambda n: (cur(n), GATE_COL_BLOCK)),
                  pl.BlockSpec((ATT_BLOCK, GATE_HALF), lambda n: (cur(n), GATE_COL_BLOCK + 1)),
                  wide, wide, wide] + [ANY] * n_ride,
        out_specs=[wide, late, late, wide, pl.BlockSpec((1, ATT_Q_HEADS), lambda n: (0, 0))] + [ANY] * n_ride,
        out_shape=[jax.ShapeDtypeStruct((l, ATT_WIDTH), F32), jax.ShapeDtypeStruct((l, ATT_WIDTH), F32),
                   jax.ShapeDtypeStruct((l, ATT_WIDTH), F32), jax.ShapeDtypeStruct((l, ATT_WIDTH), F32),
                   jax.ShapeDtypeStruct((1, ATT_Q_HEADS), F32)] + _scatter_shapes(ride),
        scratch_shapes=[pltpu.VMEM((ATT_BLOCK, ATT_WIDTH), F32), pltpu.VMEM((ATT_BLOCK, ATT_WIDTH), F32),
                        pltpu.VMEM((ATT_BLOCK, ATT_WIDTH), F32)] + (_gather_sems(n_ride) if n_ride else []),
        compiler_params=_params("arbitrary"), name=name,
    )(sinks, qkv, qkv, qkv, qkv, qkv, proj, proj, o, lse, dog, *ride)


def _local_step(x, positions, pre_norm, post_norm, conv_b, dt_bias, a_log, d_skip, gate_norm, sinks, target,
                first_in, in_proj_with_first_pair, scan_with_second_pair, attn_bwd_with_second_pair_grads,
                in_dx_with_first_pair_grads):
    tables = _rope_tables(positions)
    dt_bias_pad = jnp.pad(dt_bias, ((0, 0), (0, SSM_DT_PAD - SSM_HEADS)))
    d_lanes = jnp.repeat(d_skip, SSM_HEAD_DIM, axis=1).reshape(-1, SSM_GROUPS, 1, GP)
    a_log_pad = jnp.pad(a_log, ((0, 0), (0, SSM_DT_PAD - SSM_HEADS)))
    pairs = [first_in, None]
    saved = []
    cur = x
    h = _rmsnorm_fwd(cur, pre_norm[0], "prenorm_fwd_0")
    for i in range(DEPTH):
        j = i // 2
        if i % 2 == 0:
            in_proj = functools.partial(_matmul, h, pairs[j]["ssm_w_in"], "nn", F32, f"ssm_in_{i}")
            if i == 0:
                proj, rest = in_proj_with_first_pair(in_proj)
                pairs[0] = {**first_in, **rest}
            else:
                proj = in_proj()
            pre, xbc = _conv_fwd(proj, pairs[j]["ssm_conv_w"], conv_b[j], f"conv_fwd_{i}")
            dtb, acsb, dtr, acs_r = _ssd_prep(proj, dt_bias_pad[j:j + 1], a_log_pad[j:j + 1], f"ssd_prep_{i}")
            scan = functools.partial(_ssd_fwd, xbc, dtb, acsb, acs_r, d_lanes[j], proj, gate_norm[j], f"ssd_fwd_{i}")
            if i == 0:
                y, act, hin, pairs[1] = scan_with_second_pair(scan)
            else:
                y, act, hin = scan()
            w_ssm_in = [p["ssm_w_in"] for p in pairs]
            w_ssm_out = [p["ssm_w_out"] for p in pairs]
            w_att_in = [p["att_w_in"] for p in pairs]
            w_att_out = [p["att_w_out"] for p in pairs]
            conv_w = [p["ssm_conv_w"] for p in pairs]
            ymix = _matmul(act, w_ssm_out[j], "nn", F32, f"ssm_out_{i}")
            saved.append(dict(x=cur, h=h, proj=proj, pre=pre, xbc=xbc, dtb=dtb, acsb=acsb, dtr=dtr, acs_r=acs_r, y=y,
                              hin=hin, act=act, ymix=ymix))
        else:
            proj = _matmul(h, w_att_in[j], "nn", F32, f"att_in_{i}")
            qkv = _rope_fwd(proj, tables, f"rope_fwd_{i}")
            act, o, lse = _attn_fwd(qkv, proj, sinks[j], f"attn_fwd_{i}")
            ymix = _matmul(act, w_att_out[j], "nn", F32, f"att_out_{i}")
            saved.append(dict(x=cur, h=h, proj=proj, qkv=qkv, o=o, lse=lse, act=act, ymix=ymix))
        if i + 1 < DEPTH:
            cur, h = _post_fwd(cur, ymix, post_norm[i], pre_norm[i + 1], f"post_fwd_{i}")

    gr = {k: [None] * 2 for k in ("ssm_w_in", "ssm_conv_w", "ssm_conv_b", "ssm_dt_bias", "ssm_a_log", "ssm_d",
                                  "ssm_gate_norm", "ssm_w_out", "att_w_in", "att_sinks", "att_w_out")}
    gr["pre_norm"] = [None] * DEPTH
    gr["post_norm"] = [None] * DEPTH
    last = DEPTH - 1
    g, dymix, loss_lanes, gr["post_norm"][last] = _post_loss(cur, ymix, post_norm[last], target, "post_loss")
    for i in reversed(range(DEPTH)):
        j = i // 2
        s = saved[i]
        if i % 2 == 0:
            dact = _matmul(dymix, w_ssm_out[j], "nt", F32, f"ssm_out_dx_{i}")
            gr["ssm_w_out"][j] = _matmul(s["act"], dymix, "tn", F32, f"ssm_out_dw_{i}")
            dxbc, ddt8, dal, dd, dproj, gr["ssm_gate_norm"][j] = _ssd_bwd(
                s["xbc"], s["dtb"], s["acsb"], s["dtr"], s["acs_r"], a_log[j], d_lanes[j], s["hin"], dact, s["y"],
                s["proj"], gate_norm[j], f"ssd_bwd_{i}")
            gr["ssm_a_log"][j] = dal.reshape(SSM_HEADS)
            gr["ssm_d"][j] = dd.reshape(SSM_HEADS)
            l = x.shape[0]
            ddt = jnp.pad(jnp.transpose(ddt8, (2, 0, 1)).reshape(l, SSM_HEADS), ((0, 0), (0, SSM_DT_PAD - SSM_HEADS)))
            dproj, dbias = _dt_bwd(ddt, s["proj"], dt_bias_pad[j:j + 1], dproj, f"dt_bwd_{i}")
            gr["ssm_dt_bias"][j] = dbias[0, :SSM_HEADS]
            dproj, gr["ssm_conv_w"][j], dcb = _conv_bwd(dxbc, s["pre"], s["proj"], conv_w[j], dproj, f"conv_bwd_{i}")
            gr["ssm_conv_b"][j] = dcb[0]
            w_in, key = w_ssm_in[j], "ssm_w_in"
        else:
            dog = _matmul(dymix, w_att_out[j], "nt", F32, f"att_out_dx_{i}")
            gr["att_w_out"][j] = _matmul(s["act"], dymix, "tn", F32, f"att_out_dw_{i}")
            attn_bwd = functools.partial(_attn_bwd, s["qkv"], s["proj"], sinks[j], s["o"], s["lse"], dog, f"attn_bwd_{i}")
            if i == 1:
                (dq, dk, dv, dgate, dsk), second_pair_reduced = attn_bwd_with_second_pair_grads(
                    attn_bwd, {k: gr[k][1] for k in BIG})
            else:
                dq, dk, dv, dgate, dsk = attn_bwd()
            gr["att_sinks"][j] = dsk[0]
            dproj = _rope_bwd(dq, dk, dv, dgate, tables, f"rope_bwd_{i}")
            w_in, key = w_att_in[j], "att_w_in"
        gr[key][j] = _matmul(s["h"], dproj, "tn", F32, f"in_dw_{i}")
        in_dx = functools.partial(_matmul, dproj, w_in, "nt", F32, f"in_dx_{i}")
        if i == 0:
            dh, first_pair_reduced = in_dx_with_first_pair_grads(in_dx, {k: gr[k][0] for k in BIG})
        else:
            dh = in_dx()
        if i > 0:
            g, dymix, gr["pre_norm"][i], gr["post_norm"][i - 1] = _norm_bwd_chain(
                dh, s["x"], pre_norm[i], g, saved[i - 1]["ymix"], post_norm[i - 1], f"norm_bwd_{i}")
        else:
            g, gr["pre_norm"][i] = _rmsnorm_bwd(dh, s["x"], pre_norm[i], g, f"prenorm_bwd_{i}")
    grads = {k: jnp.stack([v.reshape(v.shape[-1]) if k in ("pre_norm", "post_norm", "ssm_gate_norm") else v for v in vs])
             for k, vs in gr.items() if k not in BIG}
    return loss_lanes, g, grads, first_pair_reduced, second_pair_reduced


N_CHIPS = 4
N_DEV = 8
MESH = pl.DeviceIdType.MESH
ANY = pl.BlockSpec(memory_space=pl.ANY)


def _place():
    x, y, c = lax.axis_index("x"), lax.axis_index("y"), lax.axis_index("c")
    return x, y, c, 2 * x + y


def _gather_sems(n):
    return [pltpu.SemaphoreType.DMA((n, N_CHIPS)), pltpu.SemaphoreType.DMA((n, N_CHIPS)), pltpu.SemaphoreType.DMA((n,))]


def _gather_between_chips(ins, outs, send_sems, recv_sems, local_sems, wait):
    n = len(ins)
    _, _, c, s = _place()
    local = [pltpu.make_async_copy(ins[w], outs[w].at[s], local_sems.at[w]) for w in range(n)]

    def remote(w, t):
        return pltpu.make_async_remote_copy(
            src_ref=ins[w].at[c], dst_ref=outs[w].at[s, c], send_sem=send_sems.at[w, t],
            recv_sem=recv_sems.at[w, s], device_id=(t // 2, t % 2, c), device_id_type=MESH)

    def arrival(w, t):
        return pltpu.make_async_remote_copy(
            src_ref=ins[w].at[c], dst_ref=outs[w].at[t, c], send_sem=send_sems.at[w, t],
            recv_sem=recv_sems.at[w, t], device_id=(t // 2, t % 2, c), device_id_type=MESH)

    if not wait:
        for cp in local:
            cp.start()
    for t in range(N_CHIPS):
        @pl.when(s != t)
        def _():
            for w in range(n):
                if wait:
                    remote(w, t).wait_send()
                    arrival(w, t).wait_recv()
                else:
                    remote(w, t).start()
    if wait:
        for cp in local:
            cp.wait()


def _pair_handoff(bufs, name):
    n = len(bufs)

    def body(*refs):
        outs = refs[n:2 * n]
        send_sems, recv_sems = refs[2 * n:]
        x, y, c, s = _place()

        def handed_on(w, t):
            return pltpu.make_async_remote_copy(
                src_ref=outs[w].at[t, c], dst_ref=outs[w].at[t, c], send_sem=send_sems.at[w, t],
                recv_sem=recv_sems.at[w, t], device_id=(x, y, 1 - c), device_id_type=MESH)

        def handed_in(w, t):
            return pltpu.make_async_remote_copy(
                src_ref=outs[w].at[t, 1 - c], dst_ref=outs[w].at[t, 1 - c], send_sem=send_sems.at[w, t],
                recv_sem=recv_sems.at[w, t], device_id=(x, y, 1 - c), device_id_type=MESH)

        for t in range(N_CHIPS):
            @pl.when(s != t)
            def _():
                for w in range(n):
                    handed_on(w, t).start()
        for t in range(N_CHIPS):
            @pl.when(s != t)
            def _():
                for w in range(n):
                    handed_on(w, t).wait_send()
                    handed_in(w, t).wait_recv()

    return pl.pallas_call(
        body, in_specs=[ANY] * n, out_specs=[ANY] * n,
        out_shape=[jax.ShapeDtypeStruct(a.shape, a.dtype) for a in bufs],
        scratch_shapes=[pltpu.SemaphoreType.DMA((n, N_CHIPS)), pltpu.SemaphoreType.DMA((n, N_CHIPS))],
        input_output_aliases={w: w for w in range(n)}, name=name,
    )(*bufs)


def _chip_gather(shards, name):
    n = len(shards)

    def body(*refs):
        ins, outs = refs[:n], refs[n:2 * n]
        _gather_between_chips(ins, outs, *refs[2 * n:], wait=False)
        _gather_between_chips(ins, outs, *refs[2 * n:], wait=True)

    bufs = pl.pallas_call(
        body, in_specs=[ANY] * n, out_specs=[ANY] * n,
        out_shape=[jax.ShapeDtypeStruct((N_CHIPS,) + a.shape, a.dtype) for a in shards],
        scratch_shapes=_gather_sems(n), name=name,
    )(*shards)
    return _pair_handoff(bufs, name + "_handoff")


def _pair_swap(parts, name):
    n = len(parts)

    def body(*refs):
        ins, outs = refs[:n], refs[n:2 * n]
        send_sems, recv_sems = refs[2 * n:]
        x, y, c, _ = _place()
        cps = [pltpu.make_async_remote_copy(
            src_ref=ins[w].at[1 - c], dst_ref=outs[w], send_sem=send_sems.at[w], recv_sem=recv_sems.at[w],
            device_id=(x, y, 1 - c), device_id_type=MESH) for w in range(n)]
        for cp in cps:
            cp.start()
        for cp in cps:
            cp.wait()

    return pl.pallas_call(
        body, in_specs=[ANY] * n, out_specs=[ANY] * n,
        out_shape=[jax.ShapeDtypeStruct(a.shape[1:], a.dtype) for a in parts],
        scratch_shapes=[pltpu.SemaphoreType.DMA((n,)), pltpu.SemaphoreType.DMA((n,))],
        name=name,
    )(*parts)


def _scatter_between_chips(ins, outs, send_sems, recv_sems, local_sems, wait):
    n = len(ins)
    _, _, c, s = _place()

    def block(w, t):
        rows = ins[w].shape[0] // N_CHIPS
        return ins[w].at[pl.ds(t * rows, rows)]

    local = [pltpu.make_async_copy(block(w, s), outs[w].at[s], local_sems.at[w]) for w in range(n)]

    def remote(w, t):
        return pltpu.make_async_remote_copy(
            src_ref=block(w, t), dst_ref=outs[w].at[s], send_sem=send_sems.at[w, t], recv_sem=recv_sems.at[w, s],
            device_id=(t // 2, t % 2, c), device_id_type=MESH)

    def arrival(w, t):
        return pltpu.make_async_remote_copy(
            src_ref=block(w, t), dst_ref=outs[w].at[t], send_sem=send_sems.at[w, t], recv_sem=recv_sems.at[w, t],
            device_id=(t // 2, t % 2, c), device_id_type=MESH)

    if not wait:
        for cp in local:
            cp.start()
    for t in range(N_CHIPS):
        @pl.when(s != t)
        def _():
            for w in range(n):
                if wait:
                    remote(w, t).wait_send()
                    arrival(w, t).wait_recv()
                else:
                    remote(w, t).start()
    if wait:
        for cp in local:
            cp.wait()


def _scatter_shapes(parts):
    return [jax.ShapeDtypeStruct((N_CHIPS, a.shape[0] // N_CHIPS, a.shape[1]), a.dtype) for a in parts]


def _pair_merge(parts, name):
    n = len(parts)

    def body(*refs):
        ins, outs = refs[:n], refs[n:2 * n]
        send_sems, recv_sems = refs[2 * n:]
        x, y, c, _ = _place()
        cps = [pltpu.make_async_remote_copy(
            src_ref=ins[w], dst_ref=outs[w], send_sem=send_sems.at[w], recv_sem=recv_sems.at[w],
            device_id=(x, y, 1 - c), device_id_type=MESH) for w in range(n)]
        for cp in cps:
            cp.start()
        for cp in cps:
            cp.wait()

    return pl.pallas_call(
        body, in_specs=[ANY] * n, out_specs=[ANY] * n,
        out_shape=[jax.ShapeDtypeStruct(a.shape, a.dtype) for a in parts],
        scratch_shapes=[pltpu.SemaphoreType.DMA((n,)), pltpu.SemaphoreType.DMA((n,))],
        name=name,
    )(*parts)


def _all_gather_small(a, name):
    def body(in_ref, out_ref, send_sems, recv_sems, local_sem):
        x, y, c, _ = _place()
        me = 4 * x + 2 * y + c
        local = pltpu.make_async_copy(in_ref, out_ref.at[me], local_sem)
        local.start()

        def remote(d):
            return pltpu.make_async_remote_copy(
                src_ref=in_ref, dst_ref=out_ref.at[me], send_sem=send_sems.at[d], recv_sem=recv_sems.at[me],
                device_id=(d // 4, (d // 2) % 2, d % 2), device_id_type=MESH)

        def arrival(d):
            return pltpu.make_async_remote_copy(
                src_ref=in_ref, dst_ref=out_ref.at[d], send_sem=send_sems.at[d], recv_sem=recv_sems.at[d],
                device_id=(d // 4, (d // 2) % 2, d % 2), device_id_type=MESH)

        for d in range(N_DEV):
            @pl.when(me != d)
            def _():
                remote(d).start()
        for d in range(N_DEV):
            @pl.when(me != d)
            def _():
                remote(d).wait_send()
                arrival(d).wait_recv()
        local.wait()

    return pl.pallas_call(
        body, in_specs=[ANY], out_specs=ANY, out_shape=jax.ShapeDtypeStruct((N_DEV,) + a.shape, a.dtype),
        scratch_shapes=[pltpu.SemaphoreType.DMA((N_DEV,)), pltpu.SemaphoreType.DMA((N_DEV,)), pltpu.SemaphoreType.DMA],
        name=name,
    )(a)


def _reduce_tile(rows):
    return _pick(rows, (256, 128, 16))


def _pair_add(full, other, layer, name):
    _, rows, cols = full.shape
    tr = _reduce_tile(rows)

    def body(layer_ref, a_ref, b_ref, o_ref):
        o_ref[...] = (a_ref[0] + b_ref[...]).astype(o_ref.dtype)

    return pl.pallas_call(
        body,
        grid_spec=pltpu.PrefetchScalarGridSpec(
            num_scalar_prefetch=1, grid=(rows // tr,),
            in_specs=[pl.BlockSpec((1, tr, cols), lambda i, lr: (lr[0], i, 0)), pl.BlockSpec((tr, cols), lambda i, lr: (i, 0))],
            out_specs=pl.BlockSpec((tr, cols), lambda i, lr: (i, 0))),
        out_shape=jax.ShapeDtypeStruct((rows, cols), BF16), compiler_params=_params("parallel"), name=name,
    )(layer, full, other)


def _sum_slots(a, name):
    n, rows, cols = a.shape
    tr = _reduce_tile(rows)

    def body(a_ref, o_ref):
        acc = a_ref[0].astype(F32)
        for k in range(1, n):
            acc = acc + a_ref[k].astype(F32)
        o_ref[...] = acc

    return pl.pallas_call(
        body, grid=(rows // tr,), in_specs=[pl.BlockSpec((n, tr, cols), lambda i: (0, i, 0))],
        out_specs=pl.BlockSpec((tr, cols), lambda i: (i, 0)),
        out_shape=jax.ShapeDtypeStruct((rows, cols), F32), compiler_params=_params("parallel"), name=name,
    )(a)


def _adamw(w, g, m, v, name):
    rows, cols = w.shape
    tr = _pick(rows, (256, 8))

    def body(w_ref, g_ref, m_ref, v_ref, d_ref, nm_ref, nv_ref):
        gv = g_ref[...]
        mn = ADAM_B1 * m_ref[...] + (1.0 - ADAM_B1) * gv
        vn = ADAM_B2 * v_ref[...] + (1.0 - ADAM_B2) * jnp.square(gv)
        m_hat = mn / (1.0 - ADAM_B1 ** ADAM_STEP)
        v_hat = vn / (1.0 - ADAM_B2 ** ADAM_STEP)
        d_ref[...] = -ADAM_LR * (m_hat / (jnp.sqrt(v_hat) + ADAM_EPS) + ADAM_WD * w_ref[...])
        nm_ref[...] = mn
        nv_ref[...] = vn

    blk = pl.BlockSpec((tr, cols), lambda i: (i, 0))
    return pl.pallas_call(
        body, grid=(rows // tr,), in_specs=[blk] * 4, out_specs=[blk] * 3,
        out_shape=[jax.ShapeDtypeStruct((rows, cols), F32)] * 3, compiler_params=_params("parallel"), name=name,
    )(w, g, m, v)


BIG = ("ssm_w_in", "ssm_w_out", "att_w_in", "att_w_out")
SHARDED = BIG + ("ssm_conv_w",)
SMALL = ("pre_norm", "post_norm", "ssm_conv_b", "ssm_dt_bias", "ssm_a_log", "ssm_d", "ssm_gate_norm", "att_sinks")
WEIGHTS = ("pre_norm", "post_norm", "ssm_w_in", "ssm_conv_w", "ssm_conv_b", "ssm_dt_bias", "ssm_a_log", "ssm_d",
           "ssm_gate_norm", "ssm_w_out", "att_w_in", "att_sinks", "att_w_out")


def _halves(a):
    return a.reshape(2, a.shape[0] // 2, a.shape[1])


def _layer_shards(j, ssm_w_in, ssm_w_out, att_w_in, att_w_out, ssm_conv_w):
    return [_halves(ssm_w_in[j].astype(BF16)), _halves(ssm_w_out[j].astype(BF16)), _halves(att_w_in[j].astype(BF16)),
            _halves(att_w_out[j].astype(BF16)), _halves(ssm_conv_w[j])]


SHARD_KEYS = ("ssm_w_in", "ssm_w_out", "att_w_in", "att_w_out", "ssm_conv_w")


def _whole_weights(keys, gathered):
    out = {}
    for k, g in zip(keys, gathered):
        g = g.reshape((N_CHIPS, 2 * g.shape[2], g.shape[3]))
        if k in ("ssm_w_out", "att_w_out"):
            out[k] = g.reshape(N_CHIPS * g.shape[1], g.shape[2])
        else:
            out[k] = jnp.transpose(g, (1, 0, 2)).reshape(g.shape[1], N_CHIPS * g.shape[2])
    if "ssm_w_in" in out:
        out["ssm_w_in"] = jnp.pad(out["ssm_w_in"], ((0, 0), (0, SSM_IN_PAD - SSM_IN_DIM)))
    return out


def _halves_by_chip(key, g):
    if key in ("ssm_w_out", "att_w_out"):
        rows = g.shape[0] // N_CHIPS
        blocks = g.reshape(N_CHIPS, 2, rows // 2, g.shape[1])
        return jnp.transpose(blocks, (1, 0, 2, 3)).reshape(2, N_CHIPS * (rows // 2), g.shape[1])
    cols = (SSM_IN_DIM if key == "ssm_w_in" else g.shape[1]) // N_CHIPS
    rows = g.shape[0]
    blocks = g[:, :N_CHIPS * cols].reshape(2, rows // 2, N_CHIPS, cols)
    return jnp.transpose(blocks, (0, 2, 1, 3)).reshape(2, N_CHIPS * (rows // 2), cols)


def _pack_small(tree, keys):
    flat = jnp.concatenate([tree[k].reshape(-1) for k in keys])
    rows = -(-flat.shape[0] // (8 * LANES)) * 8
    return jnp.pad(flat, (0, rows * LANES - flat.shape[0])).reshape(rows, LANES)


def _unpack_small(packed, shapes, keys):
    flat = packed.reshape(-1)
    out, at = {}, 0
    for k in keys:
        n = 1
        for dim in shapes[k]:
            n *= dim
        out[k] = flat[at:at + n].reshape(shapes[k])
        at += n
    return out


def kernel(x, positions, pre_norm, post_norm, ssm_w_in, ssm_conv_w, ssm_conv_b, ssm_dt_bias, ssm_a_log, ssm_d, ssm_gate_norm, ssm_w_out, att_w_in, att_sinks, att_w_out, loss_target, m_pre_norm, m_post_norm, m_ssm_w_in, m_ssm_conv_w, m_ssm_conv_b, m_ssm_dt_bias, m_ssm_a_log, m_ssm_d, m_ssm_gate_norm, m_ssm_w_out, m_att_w_in, m_att_sinks, m_att_w_out, v_pre_norm, v_post_norm, v_ssm_w_in, v_ssm_conv_w, v_ssm_conv_b, v_ssm_dt_bias, v_ssm_a_log, v_ssm_d, v_ssm_gate_norm, v_ssm_w_out, v_att_w_in, v_att_sinks, v_att_w_out):
    w = dict(pre_norm=pre_norm, post_norm=post_norm, ssm_w_in=ssm_w_in, ssm_conv_w=ssm_conv_w, ssm_conv_b=ssm_conv_b,
             ssm_dt_bias=ssm_dt_bias, ssm_a_log=ssm_a_log, ssm_d=ssm_d, ssm_gate_norm=ssm_gate_norm, ssm_w_out=ssm_w_out,
             att_w_in=att_w_in, att_sinks=att_sinks, att_w_out=att_w_out)
    m = dict(pre_norm=m_pre_norm, post_norm=m_post_norm, ssm_w_in=m_ssm_w_in, ssm_conv_w=m_ssm_conv_w, ssm_conv_b=m_ssm_conv_b,
             ssm_dt_bias=m_ssm_dt_bias, ssm_a_log=m_ssm_a_log, ssm_d=m_ssm_d, ssm_gate_norm=m_ssm_gate_norm,
             ssm_w_out=m_ssm_w_out, att_w_in=m_att_w_in, att_sinks=m_att_sinks, att_w_out=m_att_w_out)
    v = dict(pre_norm=v_pre_norm, post_norm=v_post_norm, ssm_w_in=v_ssm_w_in, ssm_conv_w=v_ssm_conv_w, ssm_conv_b=v_ssm_conv_b,
             ssm_dt_bias=v_ssm_dt_bias, ssm_a_log=v_ssm_a_log, ssm_d=v_ssm_d, ssm_gate_norm=v_ssm_gate_norm,
             ssm_w_out=v_ssm_w_out, att_w_in=v_att_w_in, att_sinks=v_att_sinks, att_w_out=v_att_w_out)
    c = lax.axis_index("c")
    chip = 2 * lax.axis_index("x") + lax.axis_index("y")

    sharded = (ssm_w_in, ssm_w_out, att_w_in, att_w_out, ssm_conv_w)
    own = [dict(zip(SHARD_KEYS, _layer_shards(j, *sharded))) for j in range(2)]
    now_keys = ("ssm_w_in", "ssm_conv_w")
    later_keys = ("ssm_w_out", "att_w_in", "att_w_out")
    first_in = _whole_weights(now_keys, _chip_gather([own[0][k] for k in now_keys], "gather_weights_0"))

    def in_proj_with_first_pair(matmul):
        proj, *arrived = matmul(ride=[own[0][k] for k in later_keys])
        return proj, _whole_weights(later_keys, _pair_handoff(arrived, "gather_weights_0_rest_handoff"))

    def scan_with_second_pair(scan):
        y, act, hin, *arrived = scan(ride=[own[1][k] for k in SHARD_KEYS])
        return y, act, hin, _whole_weights(SHARD_KEYS, _pair_handoff(arrived, "gather_weights_1_handoff"))

    half = jnp.reshape(c, (1,)).astype(jnp.int32)

    def reduce_begin(pair_grads, tag):
        parts = [_halves_by_chip(k, pair_grads[k]) for k in BIG]
        from_sibling = _pair_swap(parts, f"reduce_pair_swap_{tag}")
        return [_pair_add(p, o, half, f"reduce_pair_add_{tag}_{n}") for n, (p, o) in enumerate(zip(parts, from_sibling))]

    def reduce_end(by_chip, tag):
        mine = [_sum_slots(a, f"reduce_chip_sum_{tag}_{n}") for n, a in enumerate(by_chip)]
        theirs = _pair_merge(mine, f"reduce_pair_merge_{tag}")
        return {k: jnp.where(c == 0, jnp.concatenate([a, b]), jnp.concatenate([b, a])) for k, a, b in zip(BIG, mine, theirs)}

    def attn_bwd_with_second_pair_grads(attn_bwd, pair_grads):
        dq, dk, dv, dgate, dsk, *by_chip = attn_bwd(ride=reduce_begin(pair_grads, "1"))
        return (dq, dk, dv, dgate, dsk), reduce_end(by_chip, "1")

    def in_dx_with_first_pair_grads(matmul, pair_grads):
        dh, *by_chip = matmul(ride=reduce_begin(pair_grads, "0"), ride_scatters=True)
        return dh, reduce_end(by_chip, "0")

    loss_lanes, grad_x, gr, reduced_0, reduced_1 = _local_step(
        x[0], positions[0], pre_norm, post_norm, ssm_conv_b, ssm_dt_bias, ssm_a_log, ssm_d, ssm_gate_norm, att_sinks,
        loss_target[0], first_in, in_proj_with_first_pair, scan_with_second_pair, attn_bwd_with_second_pair_grads,
        in_dx_with_first_pair_grads)
    loss = lax.psum(0.5 * jnp.sum(loss_lanes) / D_MODEL, ("x", "y", "c"))
    grads = {k: jnp.stack([reduced_0[k], reduced_1[k]]) for k in BIG}

    small_keys = SMALL + ("ssm_conv_w",)
    small_shapes = {k: w[k].shape for k in SMALL}
    small_shapes["ssm_conv_w"] = gr["ssm_conv_w"].shape
    small_sum = _sum_slots(_all_gather_small(_pack_small(gr, small_keys), "reduce_small_gather"), "reduce_small_sum")
    grads.update(_unpack_small(small_sum, small_shapes, small_keys))
    conv_cols = ssm_conv_w.shape[2]
    grads["ssm_conv_w"] = lax.dynamic_slice_in_dim(grads["ssm_conv_w"], chip * conv_cols, conv_cols, axis=2)

    delta, new_m, new_v = {}, {}, {}
    for k in SHARDED:
        shp = w[k].shape
        two_d = (shp[0] * shp[1], shp[2])
        d_, m_, v_ = _adamw(w[k].reshape(two_d), grads[k].reshape(two_d), m[k].reshape(two_d), v[k].reshape(two_d),
                            f"adamw_{k}")
        delta[k], new_m[k], new_v[k] = d_.reshape(shp), m_.reshape(shp), v_.reshape(shp)
    d_, m_, v_ = _adamw(_pack_small(w, SMALL), _pack_small(grads, SMALL), _pack_small(m, SMALL), _pack_small(v, SMALL),
                        "adamw_small")
    delta.update(_unpack_small(d_, small_shapes, SMALL))
    new_m.update(_unpack_small(m_, small_shapes, SMALL))
    new_v.update(_unpack_small(v_, small_shapes, SMALL))

    return (loss, grad_x[None], *[grads[k] for k in WEIGHTS], *[delta[k] for k in WEIGHTS],
            *[new_m[k] for k in WEIGHTS], *[new_v[k] for k in WEIGHTS])
```

```python
import functools

import jax
import jax.numpy as jnp
from jax import lax
from jax.experimental import pallas as pl
from jax.experimental.pallas import tpu as pltpu

F32 = jnp.float32
BF16 = jnp.bfloat16
EPS = 1e-6
NEG_INF = float("-inf")

D_MODEL = 1024
DEPTH = 4
SSM_D_INNER = 2048
SSM_HEAD_DIM = 64
SSM_HEADS = 32
SSM_GROUPS = 8
SSM_HPG = 4
SSM_STATE = 128
SSM_CONV = 4
SSM_CHUNK = 128
SSM_BC_DIM = 1024
SSM_CONV_DIM = 4096
SSM_IN_DIM = 6176
SSM_IN_PAD = 6272
SSM_DT_PAD = 128
ATT_HEAD_DIM = 64
ATT_Q_HEADS = 16
ATT_KV_HEADS = 4
ATT_GQA = 4
ATT_WIDTH = 1024
ATT_KV_WIDTH = 256
ATT_IN_DIM = 2560
ATT_QKV = ATT_WIDTH + 2 * ATT_KV_WIDTH
ATT_BLOCK = 128
ROPE_THETA = 500000.0
ROPE_DIM = 16
ROPE_HALF = 8
Q_SCALE = ATT_HEAD_DIM ** -0.5

ADAM_LR = 0.001
ADAM_B1 = 0.9
ADAM_B2 = 0.999
ADAM_EPS = 1e-08
ADAM_WD = 0.01
ADAM_STEP = 10

VMEM_LIMIT_BYTES = 48 * 1024 * 1024
NT_DIMS = (((1,), (1,)), ((), ()))
TN_DIMS = (((0,), (0,)), ((), ()))


def _params(*sem):
    return pltpu.CompilerParams(dimension_semantics=sem, vmem_limit_bytes=VMEM_LIMIT_BYTES)


def _pick(n, cands):
    for c in cands:
        if n % c == 0:
            return c
    return n


def _sigmoid(v):
    return 0.5 * jnp.tanh(0.5 * v) + 0.5


def _bdot(a, b):
    return jnp.dot(a.astype(BF16), b.astype(BF16), preferred_element_type=F32)


def _bdot_nt(a, b):
    return lax.dot_general(a.astype(BF16), b.astype(BF16), NT_DIMS, preferred_element_type=F32)


def _bdot_tn(a, b):
    return lax.dot_general(a.astype(BF16), b.astype(BF16), TN_DIMS, preferred_element_type=F32)


MATMUL_VMEM_BUDGET = 36 * 1024 * 1024


def _matmul_tiles(m, n, k, out_bytes, reduce_rows):
    best = None
    whole = [k] if (not reduce_rows or k <= 2048) else []
    for tk in whole + [c for c in (4096, 2048, 1024, 896, 512) if k % c == 0 and c < k]:
        for tm in (c for c in (2048, 1024, 512, 256) if m % c == 0):
            for tn in (c for c in (n, 1280, 1024, 896, 640, 512) if n % c == 0):
                acc = tm * tn * 4 if tk < k else 0
                need = 2 * (2 * tk * (tm + tn) + tm * tn * out_bytes) + acc
                if need <= MATMUL_VMEM_BUDGET and (best is None or tm * tn * min(tk, 2048) > best[0]):
                    best = (tm * tn * min(tk, 2048), tm, tn, tk)
        if best is not None and not reduce_rows:
            break
    return best[1:]


def _matmul(a, b, mode, out_dtype, name, ride=(), ride_scatters=False):
    if mode == "nn":
        (m, k), n = a.shape, b.shape[1]
    elif mode == "nt":
        (m, k), n = a.shape, b.shape[0]
    else:
        (k, m), n = a.shape, b.shape[1]
    tm, tn, tk = _matmul_tiles(m, n, k, jnp.dtype(out_dtype).itemsize, mode == "tn")
    nk = k // tk
    steps = (n // tn, m // tm, nk)
    dims = {"nn": (((1,), (0,)), ((), ())), "nt": NT_DIMS, "tn": TN_DIMS}[mode]
    n_ride = len(ride)
    exchange = _scatter_between_chips if ride_scatters else _gather_between_chips
    arrived = _scatter_shapes(ride) if ride_scatters else [jax.ShapeDtypeStruct((N_CHIPS,) + r.shape, r.dtype) for r in ride]

    def body(*refs):
        a_ref, b_ref = refs[:2]
        ride_in = refs[2:2 + n_ride]
        o_ref = refs[2 + n_ride]
        ride_out = refs[3 + n_ride:3 + 2 * n_ride]
        acc_ref = refs[3 + 2 * n_ride]
        ride_sems = refs[4 + 2 * n_ride:]
        kk = pl.program_id(2)
        at = [pl.program_id(d) for d in range(3)]
        if n_ride:
            @pl.when((at[0] == 0) & (at[1] == 0) & (at[2] == 0))
            def _():
                exchange(ride_in, ride_out, *ride_sems, wait=False)

        part = lax.dot_general(a_ref[...], b_ref[...], dims, preferred_element_type=F32)
        if nk == 1:
            o_ref[...] = part.astype(o_ref.dtype)
        else:
            @pl.when(kk == 0)
            def _():
                acc_ref[...] = part

            @pl.when(kk > 0)
            def _():
                acc_ref[...] += part

            @pl.when(kk == nk - 1)
            def _():
                o_ref[...] = acc_ref[...].astype(o_ref.dtype)

        if n_ride:
            @pl.when((at[0] == steps[0] - 1) & (at[1] == steps[1] - 1) & (at[2] == steps[2] - 1))
            def _():
                exchange(ride_in, ride_out, *ride_sems, wait=True)

    if mode == "nn":
        a_spec = pl.BlockSpec((tm, tk), lambda j, i, kk: (i, kk))
        b_spec = pl.BlockSpec((tk, tn), lambda j, i, kk: (kk, j))
    elif mode == "nt":
        a_spec = pl.BlockSpec((tm, tk), lambda j, i, kk: (i, kk))
        b_spec = pl.BlockSpec((tn, tk), lambda j, i, kk: (j, kk))
    else:
        a_spec = pl.BlockSpec((tk, tm), lambda j, i, kk: (kk, i))
        b_spec = pl.BlockSpec((tk, tn), lambda j, i, kk: (kk, j))
    out = pl.pallas_call(
        body, grid=steps, in_specs=[a_spec, b_spec] + [ANY] * n_ride,
        out_specs=[pl.BlockSpec((tm, tn), lambda j, i, kk: (i, j))] + [ANY] * n_ride,
        out_shape=[jax.ShapeDtypeStruct((m, n), out_dtype)] + arrived,
        scratch_shapes=[pltpu.VMEM((tm, tn), F32)] + (_gather_sems(n_ride) if n_ride else []),
        compiler_params=_params(*(["arbitrary"] * 3 if n_ride else ["parallel", "parallel", "arbitrary"])), name=name,
    )(a, b, *ride)
    return out if n_ride else out[0]


def _row_tile(l):
    return _pick(l, (512, 256, 128))


def _rmsnorm_fwd(x, w, name):
    l, d = x.shape
    tl = _row_tile(l)

    def body(x_ref, w_ref, o_ref):
        xv = x_ref[...]
        r = lax.rsqrt(jnp.mean(xv * xv, axis=-1, keepdims=True) + EPS)
        o_ref[...] = (xv * r * w_ref[...]).astype(o_ref.dtype)

    return pl.pallas_call(
        body, grid=(l // tl,),
        in_specs=[pl.BlockSpec((tl, d), lambda i: (i, 0)), pl.BlockSpec((1, d), lambda i: (0, 0))],
        out_specs=pl.BlockSpec((tl, d), lambda i: (i, 0)),
        out_shape=jax.ShapeDtypeStruct((l, d), BF16), compiler_params=_params("parallel"), name=name,
    )(x, w.reshape(1, d))


def _post_fwd(x, y, w, w_next, name):
    l, d = x.shape
    tl = _row_tile(l)

    def body(x_ref, y_ref, w_ref, wn_ref, o_ref, h_ref):
        yv = y_ref[...]
        r = lax.rsqrt(jnp.mean(yv * yv, axis=-1, keepdims=True) + EPS)
        out = x_ref[...] + yv * r * w_ref[...]
        o_ref[...] = out
        rn = lax.rsqrt(jnp.mean(out * out, axis=-1, keepdims=True) + EPS)
        h_ref[...] = (out * rn * wn_ref[...]).astype(h_ref.dtype)

    row = pl.BlockSpec((tl, d), lambda i: (i, 0))
    vec = pl.BlockSpec((1, d), lambda i: (0, 0))
    return pl.pallas_call(
        body, grid=(l // tl,), in_specs=[row, row, vec, vec], out_specs=[row, row],
        out_shape=[jax.ShapeDtypeStruct((l, d), F32), jax.ShapeDtypeStruct((l, d), BF16)],
        compiler_params=_params("parallel"), name=name,
    )(x, y, w.reshape(1, d), w_next.reshape(1, d))


def _post_loss(x, y, w, t, name):
    l, d = x.shape
    tl = _row_tile(l)
    nt = l // tl

    def body(x_ref, y_ref, w_ref, t_ref, g_ref, dy_ref, ls_ref, dw_ref, acc_ref):
        i = pl.program_id(0)

        @pl.when(i == 0)
        def _():
            ls_ref[...] = jnp.zeros_like(ls_ref)
            acc_ref[...] = jnp.zeros_like(acc_ref)

        yv = y_ref[...]
        r = lax.rsqrt(jnp.mean(yv * yv, axis=-1, keepdims=True) + EPS)
        nrm = yv * r
        e = x_ref[...] + nrm * w_ref[...] - t_ref[...]
        gv = e * (1.0 / d)
        g_ref[...] = gv
        ls_ref[...] += jnp.sum((e * e).reshape(tl // 8, 8, d), axis=0)
        gw = gv * w_ref[...]
        dy_ref[...] = (r * (gw - nrm * jnp.mean(gw * nrm, axis=-1, keepdims=True))).astype(dy_ref.dtype)
        acc_ref[...] += jnp.sum((gv * nrm).reshape(tl // 8, 8, d), axis=0)

        @pl.when(i == nt - 1)
        def _():
            dw_ref[...] = jnp.sum(acc_ref[...], axis=0, keepdims=True)

    row = pl.BlockSpec((tl, d), lambda i: (i, 0))
    vec = pl.BlockSpec((1, d), lambda i: (0, 0))
    return pl.pallas_call(
        body, grid=(nt,), in_specs=[row, row, vec, row],
        out_specs=[row, row, pl.BlockSpec((8, d), lambda i: (0, 0)), vec],
        out_shape=[jax.ShapeDtypeStruct((l, d), F32), jax.ShapeDtypeStruct((l, d), BF16),
                   jax.ShapeDtypeStruct((8, d), F32), jax.ShapeDtypeStruct((1, d), F32)],
        scratch_shapes=[pltpu.VMEM((8, d), F32)], compiler_params=_params("arbitrary"), name=name,
    )(x, y, w.reshape(1, d), t)


def _norm_bwd_chain(dh, x, w_pre, resid, y_prev, w_post_prev, name):
    l, d = x.shape
    tl = _row_tile(l)
    nt = l // tl

    def body(dh_ref, x_ref, wp_ref, r_ref, y_ref, wq_ref, g_ref, dy_ref, dwp_ref, dwq_ref, accp_ref, accq_ref):
        i = pl.program_id(0)

        @pl.when(i == 0)
        def _():
            accp_ref[...] = jnp.zeros_like(accp_ref)
            accq_ref[...] = jnp.zeros_like(accq_ref)

        xv = x_ref[...]
        dhv = dh_ref[...]
        rx = lax.rsqrt(jnp.mean(xv * xv, axis=-1, keepdims=True) + EPS)
        nx = xv * rx
        gw = dhv * wp_ref[...]
        gv = rx * (gw - nx * jnp.mean(gw * nx, axis=-1, keepdims=True)) + r_ref[...]
        g_ref[...] = gv
        accp_ref[...] += jnp.sum((dhv * nx).reshape(tl // 8, 8, d), axis=0)
        yv = y_ref[...]
        ry = lax.rsqrt(jnp.mean(yv * yv, axis=-1, keepdims=True) + EPS)
        ny = yv * ry
        gq = gv * wq_ref[...]
        dy_ref[...] = (ry * (gq - ny * jnp.mean(gq * ny, axis=-1, keepdims=True))).astype(dy_ref.dtype)
        accq_ref[...] += jnp.sum((gv * ny).reshape(tl // 8, 8, d), axis=0)

        @pl.when(i == nt - 1)
        def _():
            dwp_ref[...] = jnp.sum(accp_ref[...], axis=0, keepdims=True)
            dwq_ref[...] = jnp.sum(accq_ref[...], axis=0, keepdims=True)

    row = pl.BlockSpec((tl, d), lambda i: (i, 0))
    vec = pl.BlockSpec((1, d), lambda i: (0, 0))
    return pl.pallas_call(
        body, grid=(nt,), in_specs=[row, row, vec, row, row, vec], out_specs=[row, row, vec, vec],
        out_shape=[jax.ShapeDtypeStruct((l, d), F32), jax.ShapeDtypeStruct((l, d), BF16),
                   jax.ShapeDtypeStruct((1, d), F32), jax.ShapeDtypeStruct((1, d), F32)],
        scratch_shapes=[pltpu.VMEM((8, d), F32), pltpu.VMEM((8, d), F32)],
        compiler_params=_params("arbitrary"), name=name,
    )(dh, x, w_pre.reshape(1, d), resid, y_prev, w_post_prev.reshape(1, d))


def _rmsnorm_bwd(g, y, w, resid, name):
    l, d = y.shape
    tl = _row_tile(l)
    nt = l // tl

    def body(g_ref, y_ref, w_ref, r_ref, dy_ref, dw_ref, acc_ref):
        i = pl.program_id(0)

        @pl.when(i == 0)
        def _():
            acc_ref[...] = jnp.zeros_like(acc_ref)

        yv = y_ref[...]
        gv = g_ref[...]
        r = lax.rsqrt(jnp.mean(yv * yv, axis=-1, keepdims=True) + EPS)
        nrm = yv * r
        gw = gv * w_ref[...]
        dy_ref[...] = r * (gw - nrm * jnp.mean(gw * nrm, axis=-1, keepdims=True)) + r_ref[...]
        acc_ref[...] += jnp.sum((gv * nrm).reshape(tl // 8, 8, d), axis=0)

        @pl.when(i == nt - 1)
        def _():
            dw_ref[...] = jnp.sum(acc_ref[...], axis=0, keepdims=True)

    row = pl.BlockSpec((tl, d), lambda i: (i, 0))
    vec = pl.BlockSpec((1, d), lambda i: (0, 0))
    return pl.pallas_call(
        body, grid=(nt,), in_specs=[row, row, vec, row], out_specs=[row, vec],
        out_shape=[jax.ShapeDtypeStruct((l, d), F32), jax.ShapeDtypeStruct((1, d), F32)],
        scratch_shapes=[pltpu.VMEM((8, d), F32)], compiler_params=_params("arbitrary"), name=name,
    )(g, y, w.reshape(1, d), resid)


CONV_COLS = 512
HALO = 8
HALO16 = 16
CONV_SUB_ROWS = 64
CONV_SUB_COLS = 256


def _conv_rows(l):
    return _pick(l, (1024, 512, 256, 128))


def _conv_fwd(proj, cw, cb, name):
    l = proj.shape[0]
    tl = _conv_rows(l)
    off = SSM_D_INNER // CONV_COLS

    def body(u_ref, halo_ref, w_ref, b_ref, pre_ref, act_ref, ext_ref):
        i = pl.program_id(1)
        ext_ref[0:HALO, :] = jnp.where(i > 0, halo_ref[...], 0.0)
        ext_ref[HALO:HALO + tl, :] = u_ref[...]
        for r0 in range(0, tl, CONV_SUB_ROWS):
            for c0 in range(0, CONV_COLS, CONV_SUB_COLS):
                cs = slice(c0, c0 + CONV_SUB_COLS)
                ext = ext_ref[r0:r0 + CONV_SUB_ROWS + HALO, cs]
                acc = b_ref[:, cs] + w_ref[SSM_CONV - 1:SSM_CONV, cs] * ext[HALO:]
                for k in range(SSM_CONV - 1):
                    acc = acc + w_ref[k:k + 1, cs] * pltpu.roll(ext, SSM_CONV - 1 - k, 0)[HALO:]
                pre_ref[r0:r0 + CONV_SUB_ROWS, cs] = acc.astype(pre_ref.dtype)
                act_ref[r0:r0 + CONV_SUB_ROWS, cs] = (acc * _sigmoid(acc)).astype(act_ref.dtype)

    hb = tl // HALO
    out = pl.BlockSpec((tl, CONV_COLS), lambda j, i: (i, j))
    return pl.pallas_call(
        body, grid=(SSM_CONV_DIM // CONV_COLS, l // tl),
        in_specs=[pl.BlockSpec((tl, CONV_COLS), lambda j, i: (i, off + j)),
                  pl.BlockSpec((HALO, CONV_COLS), lambda j, i: (jnp.maximum(i * hb - 1, 0), off + j)),
                  pl.BlockSpec((SSM_CONV, CONV_COLS), lambda j, i: (0, j)),
                  pl.BlockSpec((1, CONV_COLS), lambda j, i: (0, j))],
        out_specs=[out, out],
        out_shape=[jax.ShapeDtypeStruct((l, SSM_CONV_DIM), BF16)] * 2,
        scratch_shapes=[pltpu.VMEM((tl + HALO, CONV_COLS), F32)],
        compiler_params=_params("parallel", "arbitrary"), name=name,
    )(proj, proj, cw, cb.reshape(1, SSM_CONV_DIM))


def _conv_bwd(dact, pre, proj, cw, dproj, name):
    l, width = dact.shape
    tl = _conv_rows(l)
    nt = l // tl
    pre_off = 0
    u_off = SSM_D_INNER // CONV_COLS
    hb16 = tl // HALO16
    last_hb16 = l // HALO16 - 1

    def body(da_ref, da_h_ref, p_ref, p_h_ref, u_ref, w_ref, _, du_ref, dw_ref, db_ref, ext_ref):
        i = pl.program_id(1)

        @pl.when(i == 0)
        def _():
            dw_ref[...] = jnp.zeros_like(dw_ref)
            db_ref[...] = jnp.zeros_like(db_ref)

        def dpre_of(da, p):
            s = _sigmoid(p)
            return da * (s * (1.0 + p * (1.0 - s)))

        ext_ref[0:tl, :] = dpre_of(da_ref[...].astype(F32), p_ref[...].astype(F32))
        ext_ref[tl:tl + HALO, :] = jnp.where(
            i < nt - 1, dpre_of(da_h_ref[...].astype(F32)[:HALO], p_h_ref[...].astype(F32)[:HALO]), 0.0)
        sub = CONV_SUB_ROWS

        def fold(v):
            return jnp.sum(v.reshape(sub // 8, 8, CONV_SUB_COLS), axis=0)

        for c0 in range(0, CONV_COLS, CONV_SUB_COLS):
            cs = slice(c0, c0 + CONV_SUB_COLS)
            dws = [jnp.zeros((8, CONV_SUB_COLS), F32) for _ in range(SSM_CONV)]
            dbs = jnp.zeros((8, CONV_SUB_COLS), F32)
            for r0 in range(0, tl, sub):
                dext = ext_ref[r0:r0 + sub + HALO, cs]
                uv = u_ref[r0:r0 + sub, cs]
                for k in range(SSM_CONV):
                    j = SSM_CONV - 1 - k
                    ahead = dext[:sub] if j == 0 else pltpu.roll(dext, sub + HALO - j, 0)[:sub]
                    term = w_ref[k:k + 1, cs] * ahead
                    du = term if k == 0 else du + term
                    dws[k] = dws[k] + fold(ahead * uv)
                dbs = dbs + fold(dext[:sub])
                du_ref[r0:r0 + sub, cs] = du.astype(du_ref.dtype)
            for k in range(SSM_CONV):
                dw_ref[k:k + 1, cs] += jnp.sum(dws[k], axis=0, keepdims=True)
            db_ref[:, cs] += jnp.sum(dbs, axis=0, keepdims=True)

    return pl.pallas_call(
        body, grid=(width // CONV_COLS, nt),
        in_specs=[pl.BlockSpec((tl, CONV_COLS), lambda j, i: (i, j)),
                  pl.BlockSpec((HALO16, CONV_COLS), lambda j, i: (jnp.minimum((i + 1) * hb16, last_hb16), j)),
                  pl.BlockSpec((tl, CONV_COLS), lambda j, i: (i, pre_off + j)),
                  pl.BlockSpec((HALO16, CONV_COLS), lambda j, i: (jnp.minimum((i + 1) * hb16, last_hb16), pre_off + j)),
                  pl.BlockSpec((tl, CONV_COLS), lambda j, i: (i, u_off + j)),
                  pl.BlockSpec((SSM_CONV, CONV_COLS), lambda j, i: (0, pre_off + j)),
                  pl.BlockSpec(memory_space=pl.ANY)],
        out_specs=[pl.BlockSpec((tl, CONV_COLS), lambda j, i: (i, u_off + j)),
                   pl.BlockSpec((SSM_CONV, CONV_COLS), lambda j, i: (0, j)),
                   pl.BlockSpec((1, CONV_COLS), lambda j, i: (0, j))],
        out_shape=[jax.ShapeDtypeStruct(dproj.shape, dproj.dtype), jax.ShapeDtypeStruct((SSM_CONV, width), F32),
                   jax.ShapeDtypeStruct((1, width), F32)],
        scratch_shapes=[pltpu.VMEM((tl + HALO, CONV_COLS), F32)],
        input_output_aliases={6: 0}, compiler_params=_params("parallel", "arbitrary"), name=name,
    )(dact, dact, pre, pre, proj, cw, dproj)


DT_COL_BLOCK = (SSM_D_INNER + SSM_CONV_DIM) // SSM_DT_PAD


def _split3(v):
    hi = v.astype(BF16)
    rest = v - hi.astype(F32)
    mid = rest.astype(BF16)
    lo = (rest - mid.astype(F32)).astype(BF16)
    return hi, mid, lo


def _ssd_prep(proj, bias, a_log, name):
    l = proj.shape[0]
    nc = l // SSM_CHUNK
    head_dim_log2 = SSM_HEAD_DIM.bit_length() - 1

    def body(p_ref, b_ref, al_ref, dtb_ref, acsb_ref, dtr_ref, acsr_ref):
        v = p_ref[...] + b_ref[...]
        dt_hi, dt_mid, _ = _split3(jnp.maximum(v, 0.0) + jnp.log1p(jnp.exp(-jnp.abs(v))))
        dt = dt_hi.astype(F32) + dt_mid.astype(F32)
        ri = lax.broadcasted_iota(jnp.int32, (SSM_CHUNK, SSM_CHUNK), 0)
        cj = lax.broadcasted_iota(jnp.int32, (SSM_CHUNK, SSM_CHUNK), 1)
        tri = (ri >= cj).astype(BF16)
        acs_pieces = _split3(sum(jnp.dot(tri, piece, preferred_element_type=F32)
                                 for piece in _split3(dt * (-jnp.exp(al_ref[...])))))
        acs = sum(piece.astype(F32) for piece in acs_pieces)
        head_of_lane = lax.shift_right_logical(lax.broadcasted_iota(jnp.int32, (SSM_DT_PAD, SSM_D_INNER), 1), head_dim_log2)
        spread = (head_of_lane == lax.broadcasted_iota(jnp.int32, (SSM_DT_PAD, SSM_D_INNER), 0)).astype(BF16)
        dtb_ref[...] = sum(jnp.dot(piece, spread, preferred_element_type=F32) for piece in (dt_hi, dt_mid))
        acsb_ref[...] = sum(jnp.dot(piece, spread, preferred_element_type=F32) for piece in acs_pieces)
        dt_rows = dt.T
        acs_rows = acs.T
        for g in range(SSM_GROUPS):
            heads = slice(g * SSM_HPG, (g + 1) * SSM_HPG)
            dtr_ref[g] = dt_rows[heads, :]
            acsr_ref[g] = acs_rows[heads, :]

    rows = pl.BlockSpec((SSM_GROUPS, SSM_HPG, SSM_CHUNK), lambda c: (0, 0, c))
    dense = pl.BlockSpec((SSM_CHUNK, SSM_D_INNER), lambda c: (c, 0))
    return pl.pallas_call(
        body, grid=(nc,),
        in_specs=[pl.BlockSpec((SSM_CHUNK, SSM_DT_PAD), lambda c: (c, DT_COL_BLOCK)),
                  pl.BlockSpec((1, SSM_DT_PAD), lambda c: (0, 0)),
                  pl.BlockSpec((1, SSM_DT_PAD), lambda c: (0, 0))],
        out_specs=[dense, dense, rows, rows],
        out_shape=[jax.ShapeDtypeStruct((l, SSM_D_INNER), F32), jax.ShapeDtypeStruct((l, SSM_D_INNER), F32),
                   jax.ShapeDtypeStruct((SSM_GROUPS, SSM_HPG, l), F32),
                   jax.ShapeDtypeStruct((SSM_GROUPS, SSM_HPG, l), F32)],
        compiler_params=_params("parallel"), name=name,
    )(proj, bias, a_log)


def _dt_bwd(ddt, proj, bias, dproj, name):
    l = proj.shape[0]
    tl = _row_tile(l)

    def body(g_ref, p_ref, b_ref, _, o_ref, db_ref):
        @pl.when(pl.program_id(0) == 0)
        def _():
            db_ref[...] = jnp.zeros_like(db_ref)

        d = g_ref[...] * _sigmoid(p_ref[...] + b_ref[...])
        o_ref[...] = d.astype(o_ref.dtype)
        db_ref[...] += jnp.sum(d, axis=0, keepdims=True)

    return pl.pallas_call(
        body, grid=(l // tl,),
        in_specs=[pl.BlockSpec((tl, SSM_DT_PAD), lambda i: (i, 0)),
                  pl.BlockSpec((tl, SSM_DT_PAD), lambda i: (i, DT_COL_BLOCK)),
                  pl.BlockSpec((1, SSM_DT_PAD), lambda i: (0, 0)),
                  pl.BlockSpec(memory_space=pl.ANY)],
        out_specs=[pl.BlockSpec((tl, SSM_DT_PAD), lambda i: (i, DT_COL_BLOCK)),
                   pl.BlockSpec((1, SSM_DT_PAD), lambda i: (0, 0))],
        out_shape=[jax.ShapeDtypeStruct(dproj.shape, dproj.dtype), jax.ShapeDtypeStruct((1, SSM_DT_PAD), F32)],
        input_output_aliases={3: 0}, compiler_params=_params("arbitrary"), name=name,
    )(ddt, proj, bias, dproj)


GP = SSM_HPG * SSM_HEAD_DIM
HEAD_DIM_LOG2 = SSM_HEAD_DIM.bit_length() - 1
CHUNK_LOG2 = SSM_CHUNK.bit_length() - 1
GPS = 8
B_BLOCK0 = SSM_D_INNER // SSM_STATE
C_BLOCK0 = (SSM_D_INNER + SSM_BC_DIM) // SSM_STATE


def _chunk_iotas():
    ri = lax.broadcasted_iota(jnp.int32, (SSM_CHUNK, SSM_CHUNK), 0)
    cj = lax.broadcasted_iota(jnp.int32, (SSM_CHUNK, SSM_CHUNK), 1)
    return ri, cj


def _head_decay(acsb, acs_r, r, ri, cj):
    pair = acsb[:, (r // 2) * LANES:(r // 2 + 1) * LANES]
    mine_low = r % 2 == 0
    lane = lax.broadcasted_iota(jnp.int32, (1, LANES), 1)
    col = jnp.where((lane < SSM_HEAD_DIM) == mine_low, pair, pltpu.roll(pair, SSM_HEAD_DIM, 1))
    return jnp.exp(jnp.where(ri >= cj, col - acs_r[r:r + 1, :], NEG_INF))


def _head_masked_rows(v, dtype):
    head_of_lane = lax.shift_right_logical(lax.broadcasted_iota(jnp.int32, (1, GP), 1), HEAD_DIM_LOG2)
    return jnp.concatenate([jnp.where(head_of_lane == r, v, 0.0).astype(dtype) for r in range(SSM_HPG)], axis=0)


def _ssd_fwd(xbc, dtb, acsb, acs_r, d_lanes, proj, gate_w, name, ride=()):
    l = xbc.shape[0]
    nc = l // SSM_CHUNK
    assert GPS == SSM_GROUPS

    n_ride = len(ride)

    def body(*refs):
        x_ref, b_ref, c_ref, dtb_ref, acsb_ref, acsr_ref, d_ref, z_ref, gw_ref = refs[:9]
        ride_in = refs[9:9 + n_ride]
        y_ref, act_ref, hin_ref = refs[9 + n_ride:12 + n_ride]
        ride_out = refs[12 + n_ride:12 + 2 * n_ride]
        h_ref = refs[12 + 2 * n_ride]
        ride_sems = refs[13 + 2 * n_ride:]
        c = pl.program_id(0)
        if n_ride:
            @pl.when(c == 0)
            def _():
                _gather_between_chips(ride_in, ride_out, *ride_sems, wait=False)

            @pl.when(c == nc - 1)
            def _():
                _gather_between_chips(ride_in, ride_out, *ride_sems, wait=True)

        ri, cj = _chunk_iotas()
        for k in range(GPS):
            g = k
            cols = slice(k * GP, (k + 1) * GP)
            ncols = slice(k * SSM_STATE, (k + 1) * SSM_STATE)

            @pl.when(c == 0)
            def _():
                h_ref[g] = jnp.zeros((SSM_STATE, GP), F32)

            xv = x_ref[:, cols].astype(F32)
            bb = b_ref[:, ncols].astype(BF16)
            cb16 = c_ref[:, ncols].astype(BF16)
            acs_v = acsb_ref[:, cols]
            acs_r_v = acsr_ref[k]
            lastb = acs_v[SSM_CHUNK - 1:SSM_CHUNK, :]
            xd = xv * dtb_ref[:, cols]
            cb = lax.dot_general(cb16, bb, NT_DIMS, preferred_element_type=F32)
            hin = h_ref[g]
            hin_ref[0, k] = hin
            yoff = jnp.dot(cb16, hin.astype(BF16), preferred_element_type=F32)
            ms = [(cb * _head_decay(acs_v, acs_r_v, r, ri, cj)).astype(BF16) for r in range(SSM_HPG)]
            ydiag = jnp.dot(jnp.concatenate(ms, axis=1), _head_masked_rows(xd, BF16), preferred_element_type=F32)
            y_ref[:, cols] = ydiag + jnp.exp(acs_v) * yoff + d_ref[k] * xv
            h_ref[g] = hin * jnp.exp(lastb) + _bdot_tn(bb, xd * jnp.exp(lastb - acs_v))
        z = z_ref[...]
        yg = y_ref[...] * (z * _sigmoid(z))
        r = lax.rsqrt(jnp.mean(yg * yg, axis=-1, keepdims=True) + EPS)
        act_ref[...] = (yg * r * gw_ref[...]).astype(act_ref.dtype)

    lanes = pl.BlockSpec((SSM_CHUNK, SSM_D_INNER), lambda c: (c, 0))
    return pl.pallas_call(
        body, grid=(nc,),
        in_specs=[lanes,
                  pl.BlockSpec((SSM_CHUNK, SSM_BC_DIM), lambda c: (c, B_BLOCK0 // GPS)),
                  pl.BlockSpec((SSM_CHUNK, SSM_BC_DIM), lambda c: (c, C_BLOCK0 // GPS)),
                  lanes, lanes,
                  pl.BlockSpec((SSM_GROUPS, SSM_HPG, SSM_CHUNK), lambda c: (0, 0, c)),
                  pl.BlockSpec((SSM_GROUPS, 1, GP), lambda c: (0, 0, 0)),
                  lanes, pl.BlockSpec((1, SSM_D_INNER), lambda c: (0, 0))] + [ANY] * n_ride,
        out_specs=[lanes, lanes, pl.BlockSpec((1, SSM_GROUPS, SSM_STATE, GP), lambda c: (c, 0, 0, 0))] + [ANY] * n_ride,
        out_shape=[jax.ShapeDtypeStruct((l, SSM_D_INNER), F32), jax.ShapeDtypeStruct((l, SSM_D_INNER), BF16),
                   jax.ShapeDtypeStruct((nc, SSM_GROUPS, SSM_STATE, GP), F32)]
        + [jax.ShapeDtypeStruct((N_CHIPS,) + a.shape, a.dtype) for a in ride],
        scratch_shapes=[pltpu.VMEM((SSM_GROUPS, SSM_STATE, GP), F32)] + (_gather_sems(n_ride) if n_ride else []),
        compiler_params=_params("arbitrary"), name=name,
    )(xbc, xbc, xbc, dtb, acsb, acs_r, d_lanes, proj, gate_w.reshape(1, SSM_D_INNER), *ride)


def _ssd_bwd(xbc, dtb, acsb, dtr, acs_r, a_log, d_lanes, hin, dact, y, proj, gate_w, name):
    l = xbc.shape[0]
    nc = l // SSM_CHUNK

    def body(x_ref, b_ref, c_ref, dtb_ref, acsb_ref, dtr_ref, acsr_ref, alc_ref, d_ref, hin_ref,
             dact_ref, y_ref, z_ref, gw_ref,
             dxbc_ref, ddt_ref, dal_ref, dd_ref, dproj_ref, dgw_ref, dh_ref, dy_ref, acc_ref):
        c = pl.program_id(0)
        dx_ref = dxbc_ref.at[:, 0:SSM_D_INNER]
        db_ref = dxbc_ref.at[:, SSM_D_INNER:SSM_D_INNER + SSM_BC_DIM]
        dc_ref = dxbc_ref.at[:, SSM_D_INNER + SSM_BC_DIM:SSM_CONV_DIM]

        @pl.when(c == 0)
        def _():
            dal_ref[...] = jnp.zeros_like(dal_ref)
            dd_ref[...] = jnp.zeros_like(dd_ref)
            acc_ref[...] = jnp.zeros_like(acc_ref)

        z = z_ref[...]
        yv = y_ref[...]
        s = _sigmoid(z)
        sz = z * s
        yg = yv * sz
        r = lax.rsqrt(jnp.mean(yg * yg, axis=-1, keepdims=True) + EPS)
        nrm = yg * r
        gv = dact_ref[...]
        gw = gv * gw_ref[...]
        dyg = r * (gw - nrm * jnp.mean(gw * nrm, axis=-1, keepdims=True))
        dy_ref[...] = dyg * sz
        dproj_ref[...] = (dyg * yv * (s * (1.0 + z * (1.0 - s)))).astype(dproj_ref.dtype)
        acc_ref[...] += jnp.sum((gv * nrm).reshape(SSM_CHUNK // 8, 8, SSM_D_INNER), axis=0)

        @pl.when(c == nc - 1)
        def _():
            dgw_ref[...] = jnp.sum(acc_ref[...], axis=0, keepdims=True)

        for k in range(GPS):
            one_group(c, k, k, x_ref, b_ref, c_ref, dtb_ref, acsb_ref, dtr_ref, acsr_ref, alc_ref, d_ref,
                      hin_ref, dy_ref, dx_ref, db_ref, dc_ref, ddt_ref, dal_ref, dd_ref, dh_ref)

    def one_group(c, g, k, x_ref, b_ref, c_ref, dtb_ref, acsb_ref, dtr_ref, acsr_ref, alc_ref, d_ref, hin_ref, dy_ref,
                  dx_ref, db_ref, dc_ref, ddt_ref, dal_ref, dd_ref, dh_ref):
        cols = slice(k * GP, (k + 1) * GP)
        ncols = slice(k * SSM_STATE, (k + 1) * SSM_STATE)

        @pl.when(c == 0)
        def _():
            dh_ref[g] = jnp.zeros((SSM_STATE, GP), F32)

        xv = x_ref[:, cols].astype(F32)
        dyv = dy_ref[:, cols]
        bb = b_ref[:, ncols].astype(BF16)
        cb16 = c_ref[:, ncols].astype(BF16)
        dtb = dtb_ref[:, cols]
        acsb = acsb_ref[:, cols]
        dtr_v = dtr_ref[k]
        acs_r = acsr_ref[k]
        a_col = -jnp.exp(alc_ref[k])
        ri, cj = _chunk_iotas()
        head_of_lane = lax.shift_right_logical(lax.broadcasted_iota(jnp.int32, (SSM_HPG, GP), 1), HEAD_DIM_LOG2)
        ind_t = (head_of_lane == lax.broadcasted_iota(jnp.int32, (SSM_HPG, GP), 0)).astype(BF16)
        lastb = acsb[SSM_CHUNK - 1:SSM_CHUNK, :]
        ecb = jnp.exp(acsb)
        dteb = jnp.exp(lastb - acsb)
        xd = xv * dtb
        xw = xd * dteb
        cb = lax.dot_general(cb16, bb, NT_DIMS, preferred_element_type=F32)
        hin_v = hin_ref[0, k]
        dhn = dh_ref[g]
        h16 = hin_v.astype(BF16)
        dh16 = dhn.astype(BF16)
        ch = jnp.dot(cb16, h16, preferred_element_type=F32)
        bdh = jnp.dot(bb, dh16, preferred_element_type=F32)
        dym = _head_masked_rows(dyv, BF16)
        g_all = lax.dot_general(dym, xd.astype(BF16), NT_DIMS, preferred_element_type=F32)
        gl_sum = jnp.zeros((SSM_CHUNK, SSM_CHUNK), F32)
        ms, qs = [], []
        for r in range(SSM_HPG):
            decay = _head_decay(acsb, acs_r, r, ri, cj)
            gl = g_all[r * SSM_CHUNK:(r + 1) * SSM_CHUNK] * decay
            gl_sum = gl_sum + gl
            ms.append((cb * decay).astype(BF16))
            qs.append((gl * cb).astype(BF16))
        dxd = lax.dot_general(jnp.concatenate(ms, axis=0), dym, TN_DIMS, preferred_element_type=F32) + dteb * bdh
        cum = jnp.dot(jnp.concatenate(qs, axis=0), (ri < cj).astype(BF16), preferred_element_type=F32)
        sub4 = lax.broadcasted_iota(jnp.int32, (SSM_HPG, 1), 0)
        da = jnp.zeros((SSM_HPG, SSM_CHUNK), F32)
        for r in range(SSM_HPG):
            rect = jnp.sum(jnp.where(ri >= cj, cum[r * SSM_CHUNK:(r + 1) * SSM_CHUNK], 0.0), axis=0, keepdims=True)
            da = da + jnp.where(sub4 == r, rect, 0.0)
        z2 = xw * bdh
        sub8 = lax.broadcasted_iota(jnp.int32, (8, 1), 0)
        col_sums = (jnp.where(sub8 == 0, jnp.sum(z2, axis=0, keepdims=True), 0.0)
                    + jnp.where(sub8 == 1, jnp.sum(dhn * hin_v, axis=0, keepdims=True), 0.0)
                    + jnp.where(sub8 == 2, jnp.sum(dyv * xv, axis=0, keepdims=True), 0.0))
        summands = jnp.concatenate([dyv * ecb * ch - z2, dxd * xv, col_sums], axis=0)
        sums = lax.dot_general(ind_t, summands.astype(BF16), NT_DIMS, preferred_element_type=F32)
        per_pos = sums[:, :2 * SSM_CHUNK]
        totals = sums[:, 2 * SSM_CHUNK:]
        e_last = totals[:, 0:1] + jnp.exp(acs_r[:, SSM_CHUNK - 1:SSM_CHUNK]) * totals[:, 1:2]
        later = (ri >= cj).astype(BF16)
        da = da + e_last + sum(jnp.dot(piece, later, preferred_element_type=F32)
                               for piece in _split3(per_pos[:, :SSM_CHUNK]))
        ddt_ref[k] = a_col * da + per_pos[:, SSM_CHUNK:]
        dal_ref[g] += a_col * jnp.sum(da * dtr_v, axis=1, keepdims=True)
        dd_ref[g] += totals[:, 2:3]
        dx_ref[:, cols] = (dxd * dtb + d_ref[k] * dyv).astype(dx_ref.dtype)
        w16 = (ecb * dyv).astype(BF16)
        xw16 = xw.astype(BF16)
        gl16 = gl_sum.astype(BF16)
        dc_ref[:, ncols] = (jnp.dot(gl16, bb, preferred_element_type=F32)
                            + lax.dot_general(w16, h16, NT_DIMS, preferred_element_type=F32)).astype(dc_ref.dtype)
        db_ref[:, ncols] = (lax.dot_general(gl16, cb16, TN_DIMS, preferred_element_type=F32)
                            + lax.dot_general(xw16, dh16, NT_DIMS, preferred_element_type=F32)).astype(db_ref.dtype)
        dh_ref[g] = dhn * jnp.exp(lastb) + lax.dot_general(cb16, w16, TN_DIMS, preferred_element_type=F32)

    def rev(c):
        return nc - 1 - c

    small = pl.BlockSpec((SSM_GROUPS, SSM_HPG, 1), lambda c: (0, 0, 0))
    lanes = pl.BlockSpec((SSM_CHUNK, SSM_D_INNER), lambda c: (rev(c), 0))
    rows = pl.BlockSpec((SSM_GROUPS, SSM_HPG, SSM_CHUNK), lambda c: (0, 0, rev(c)))
    vec = pl.BlockSpec((1, SSM_D_INNER), lambda c: (0, 0))
    return pl.pallas_call(
        body, grid=(nc,),
        in_specs=[lanes,
                  pl.BlockSpec((SSM_CHUNK, SSM_BC_DIM), lambda c: (rev(c), B_BLOCK0 // GPS)),
                  pl.BlockSpec((SSM_CHUNK, SSM_BC_DIM), lambda c: (rev(c), C_BLOCK0 // GPS)),
                  lanes, lanes, rows, rows,
                  pl.BlockSpec((SSM_GROUPS, SSM_HPG, 1), lambda c: (0, 0, 0)),
                  pl.BlockSpec((SSM_GROUPS, 1, GP), lambda c: (0, 0, 0)),
                  pl.BlockSpec((1, SSM_GROUPS, SSM_STATE, GP), lambda c: (rev(c), 0, 0, 0)),
                  lanes, lanes, lanes, vec],
        out_specs=[pl.BlockSpec((SSM_CHUNK, SSM_CONV_DIM), lambda c: (rev(c), 0)),
                   rows, small, small, lanes, vec],
        out_shape=[jax.ShapeDtypeStruct((l, SSM_CONV_DIM), BF16), jax.ShapeDtypeStruct((SSM_GROUPS, SSM_HPG, l), F32),
                   jax.ShapeDtypeStruct((SSM_GROUPS, SSM_HPG, 1), F32),
                   jax.ShapeDtypeStruct((SSM_GROUPS, SSM_HPG, 1), F32),
                   jax.ShapeDtypeStruct((l, SSM_IN_PAD), BF16), jax.ShapeDtypeStruct((1, SSM_D_INNER), F32)],
        scratch_shapes=[pltpu.VMEM((SSM_GROUPS, SSM_STATE, GP), F32), pltpu.VMEM((SSM_CHUNK, SSM_D_INNER), F32),
                        pltpu.VMEM((8, SSM_D_INNER), F32)],
        compiler_params=_params("arbitrary"), name=name,
    )(xbc, xbc, xbc, dtb, acsb, dtr, acs_r, a_log.reshape(SSM_GROUPS, SSM_HPG, 1), d_lanes, hin, dact, y, proj,
      gate_w.reshape(1, SSM_D_INNER))


LANES = 128
ROPE_Q_CHUNKS = ATT_WIDTH // LANES
ROPE_K_CHUNKS = ATT_KV_WIDTH // LANES


def _rope_tables(positions):
    inv = ROPE_THETA ** (-jnp.arange(0, ROPE_DIM, 2, dtype=F32) / ROPE_DIM)
    ang = positions.astype(F32)[:, None] * inv
    cos, sin = jnp.cos(ang), jnp.sin(ang)
    l = positions.shape[0]
    rest = ATT_HEAD_DIM - ROPE_DIM
    ones, zeros = jnp.ones((l, rest), F32), jnp.zeros((l, rest), F32)
    z8 = jnp.zeros((l, ROPE_HALF), F32)
    cos_f = jnp.concatenate([cos, cos, ones], axis=1)
    sin_a = jnp.concatenate([-sin, z8, zeros], axis=1)
    sin_b = jnp.concatenate([z8, sin, zeros], axis=1)
    reps = LANES // ATT_HEAD_DIM
    return tuple(jnp.tile(t, (1, reps)) for t in (cos_f, sin_a, sin_b))


ATT_QKV4 = 3 * ATT_WIDTH


def _both_halves(chunk):
    lane = lax.broadcasted_iota(jnp.int32, (1, LANES), 1)
    swapped = pltpu.roll(chunk, ATT_HEAD_DIM, 1)
    return jnp.where(lane < ATT_HEAD_DIM, chunk, swapped), jnp.where(lane < ATT_HEAD_DIM, swapped, chunk)


def _rope_fwd(proj, tables, name):
    l = proj.shape[0]
    tl = _pick(l, (256, 128))

    def body(p_ref, c_ref, sa_ref, sb_ref, o_ref):
        cos_f, sin_a, sin_b = c_ref[...], sa_ref[...], sb_ref[...]

        def rope(t):
            return t * cos_f + pltpu.roll(t, LANES - ROPE_HALF, 1) * sin_a + pltpu.roll(t, ROPE_HALF, 1) * sin_b

        for k in range(ROPE_Q_CHUNKS):
            sl = slice(k * LANES, (k + 1) * LANES)
            o_ref[:, sl] = (rope(p_ref[:, sl]) * Q_SCALE).astype(o_ref.dtype)
        for part in range(2):
            for k in range(ROPE_K_CHUNKS):
                src = ATT_WIDTH + part * ATT_KV_WIDTH + k * LANES
                t = p_ref[:, src:src + LANES]
                if part == 0:
                    t = rope(t)
                for head, dup in enumerate(_both_halves(t.astype(o_ref.dtype))):
                    dst = (1 + part) * ATT_WIDTH + (2 * k + head) * ATT_GQA * ATT_HEAD_DIM
                    o_ref[:, dst:dst + LANES] = dup
                    o_ref[:, dst + LANES:dst + 2 * LANES] = dup

    tab = pl.BlockSpec((tl, LANES), lambda i: (i, 0))
    return pl.pallas_call(
        body, grid=(l // tl,), in_specs=[pl.BlockSpec((tl, ATT_IN_DIM), lambda i: (i, 0)), tab, tab, tab],
        out_specs=pl.BlockSpec((tl, ATT_QKV4), lambda i: (i, 0)),
        out_shape=jax.ShapeDtypeStruct((l, ATT_QKV4), BF16), compiler_params=_params("parallel"), name=name,
    )(proj, *tables)


def _rope_bwd(dq, dk4, dv4, dgate, tables, name):
    l = dq.shape[0]
    tl = _pick(l, (256, 128))

    def body(dq_ref, dk_ref, dv_ref, dg_ref, c_ref, sa_ref, sb_ref, o_ref):
        cos_f, sin_a, sin_b = c_ref[...], sa_ref[...], sb_ref[...]
        lane = lax.broadcasted_iota(jnp.int32, (1, LANES), 1)

        def unrope(t):
            return t * cos_f + pltpu.roll(t * sin_a, ROPE_HALF, 1) + pltpu.roll(t * sin_b, LANES - ROPE_HALF, 1)

        def head_total(ref, kvh):
            base = kvh * ATT_GQA * ATT_HEAD_DIM
            s = ref[:, base:base + LANES] + ref[:, base + LANES:base + 2 * LANES]
            return s + pltpu.roll(s, ATT_HEAD_DIM, 1)

        for k in range(ROPE_Q_CHUNKS):
            sl = slice(k * LANES, (k + 1) * LANES)
            o_ref[:, sl] = unrope(dq_ref[:, sl] * Q_SCALE).astype(o_ref.dtype)
        for k in range(ROPE_K_CHUNKS):
            dk = jnp.where(lane < ATT_HEAD_DIM, head_total(dk_ref, 2 * k), head_total(dk_ref, 2 * k + 1))
            dv = jnp.where(lane < ATT_HEAD_DIM, head_total(dv_ref, 2 * k), head_total(dv_ref, 2 * k + 1))
            o_ref[:, ATT_WIDTH + k * LANES:ATT_WIDTH + (k + 1) * LANES] = unrope(dk).astype(o_ref.dtype)
            at = ATT_WIDTH + ATT_KV_WIDTH + k * LANES
            o_ref[:, at:at + LANES] = dv.astype(o_ref.dtype)
        o_ref[:, ATT_QKV:ATT_IN_DIM] = dg_ref[...].astype(o_ref.dtype)

    tab = pl.BlockSpec((tl, LANES), lambda i: (i, 0))
    wide = pl.BlockSpec((tl, ATT_WIDTH), lambda i: (i, 0))
    return pl.pallas_call(
        body, grid=(l // tl,), in_specs=[wide, wide, wide, wide, tab, tab, tab],
        out_specs=pl.BlockSpec((tl, ATT_IN_DIM), lambda i: (i, 0)),
        out_shape=jax.ShapeDtypeStruct((l, ATT_IN_DIM), BF16), compiler_params=_params("parallel"), name=name,
    )(dq, dk4, dv4, dgate, *tables)


GATE_HALF = ATT_WIDTH // 2
GATE_COL_BLOCK = ATT_QKV // GATE_HALF


ATT_STACK = ATT_GQA * ATT_BLOCK
BLOCK_LOG2 = ATT_BLOCK.bit_length() - 1


def _stack_masks(n):
    ri = lax.broadcasted_iota(jnp.int32, (ATT_STACK, ATT_BLOCK), 0) & (ATT_BLOCK - 1)
    cj = lax.broadcasted_iota(jnp.int32, (ATT_STACK, ATT_BLOCK), 1)
    return (cj > ri) & (n > 0), cj <= ri


def _stack_sinks(sink_ref, kvh):
    blk = lax.shift_right_logical(lax.broadcasted_iota(jnp.int32, (ATT_STACK, 1), 0), BLOCK_LOG2)
    col = jnp.zeros((ATT_STACK, 1), F32)
    for r in range(ATT_GQA):
        col = jnp.where(blk == r, sink_ref[kvh * ATT_GQA + r], col)
    return col


def _stack_fold(stack):
    head_of_lane = lax.shift_right_logical(lax.broadcasted_iota(jnp.int32, (1, GP), 1), HEAD_DIM_LOG2)
    out = jnp.zeros((ATT_BLOCK, GP), F32)
    for r in range(ATT_GQA):
        out = jnp.where(head_of_lane == r, stack[r * ATT_BLOCK:(r + 1) * ATT_BLOCK], out)
    return out


def _attn_fwd(qkv, proj, sinks, name):
    l = qkv.shape[0]
    nb = l // ATT_BLOCK

    def body(sink_ref, q_ref, kp_ref, kc_ref, vp_ref, vc_ref, g0_ref, g1_ref, og_ref, o_ref, lse_ref):
        n = pl.program_id(0)
        mask_p, mask_c = _stack_masks(n)
        ones = jnp.ones((ATT_BLOCK, LANES), BF16)
        for kvh in range(ATT_KV_HEADS):
            cols = slice(kvh * GP, (kvh + 1) * GP)
            q_stack = _head_masked_rows(q_ref[:, cols], BF16)
            sp = jnp.where(mask_p, lax.dot_general(q_stack, kp_ref[:, cols], NT_DIMS, preferred_element_type=F32), NEG_INF)
            sc = jnp.where(mask_c, lax.dot_general(q_stack, kc_ref[:, cols], NT_DIMS, preferred_element_type=F32), NEG_INF)
            sink = _stack_sinks(sink_ref, kvh)
            m = jnp.maximum(jnp.max(jnp.maximum(sp, sc), axis=1, keepdims=True), sink)
            pp = jnp.exp(sp - m).astype(BF16)
            pc = jnp.exp(sc - m).astype(BF16)
            acc = (jnp.dot(pp, jnp.concatenate([vp_ref[:, cols], ones], axis=1), preferred_element_type=F32)
                   + jnp.dot(pc, jnp.concatenate([vc_ref[:, cols], ones], axis=1), preferred_element_type=F32))
            den = acc[:, GP:] + jnp.exp(sink - m)
            inv = 1.0 / den
            o_ref[:, cols] = _stack_fold(acc[:, :GP] * jnp.concatenate([inv, inv], axis=1))
            lse = m + jnp.log(den)
            lse_ref[:, cols] = _stack_fold(jnp.concatenate([lse, lse], axis=1))
        for half, g_ref in enumerate((g0_ref, g1_ref)):
            sl = slice(half * GATE_HALF, (half + 1) * GATE_HALF)
            gate = g_ref[...]
            og_ref[:, sl] = (o_ref[:, sl] * (gate * _sigmoid(gate))).astype(og_ref.dtype)

    def prev(n):
        return jnp.maximum(n - 1, 0)

    wide = pl.BlockSpec((ATT_BLOCK, ATT_WIDTH), lambda n: (n, 0))
    return pl.pallas_call(
        body, grid=(nb,),
        in_specs=[pl.BlockSpec(memory_space=pltpu.SMEM), wide,
                  pl.BlockSpec((ATT_BLOCK, ATT_WIDTH), lambda n: (prev(n), 1)),
                  pl.BlockSpec((ATT_BLOCK, ATT_WIDTH), lambda n: (n, 1)),
                  pl.BlockSpec((ATT_BLOCK, ATT_WIDTH), lambda n: (prev(n), 2)),
                  pl.BlockSpec((ATT_BLOCK, ATT_WIDTH), lambda n: (n, 2)),
                  pl.BlockSpec((ATT_BLOCK, GATE_HALF), lambda n: (n, GATE_COL_BLOCK)),
                  pl.BlockSpec((ATT_BLOCK, GATE_HALF), lambda n: (n, GATE_COL_BLOCK + 1))],
        out_specs=[wide, wide, wide],
        out_shape=[jax.ShapeDtypeStruct((l, ATT_WIDTH), BF16), jax.ShapeDtypeStruct((l, ATT_WIDTH), F32),
                   jax.ShapeDtypeStruct((l, ATT_WIDTH), F32)],
        compiler_params=_params("parallel"), name=name,
    )(sinks, qkv, qkv, qkv, qkv, qkv, proj, proj)


def _attn_bwd(qkv, proj, sinks, o, lse, dog, name, ride=()):
    l = qkv.shape[0]
    nb = l // ATT_BLOCK
    n_ride = len(ride)

    def body(*refs):
        sink_ref, q_ref, kp_ref, kc_ref, vp_ref, vc_ref, g0_ref, g1_ref, o_ref, lse_ref, dog_ref = refs[:11]
        ride_in = refs[11:11 + n_ride]
        dq_ref, dk_ref, dv_ref, dg_ref, ds_ref = refs[11 + n_ride:16 + n_ride]
        ride_out = refs[16 + n_ride:16 + 2 * n_ride]
        ck_ref, cv_ref, do_ref = refs[16 + 2 * n_ride:19 + 2 * n_ride]
        ride_sems = refs[19 + 2 * n_ride:]
        n = pl.program_id(0)

        @pl.when(n == 0)
        def _():
            ds_ref[...] = jnp.zeros_like(ds_ref)
            ck_ref[...] = jnp.zeros_like(ck_ref)
            cv_ref[...] = jnp.zeros_like(cv_ref)
            if n_ride:
                _scatter_between_chips(ride_in, ride_out, *ride_sems, wait=False)

        @pl.when(n == nb)
        def _():
            dk_ref[...] = ck_ref[...]
            dv_ref[...] = cv_ref[...]
            if n_ride:
                _scatter_between_chips(ride_in, ride_out, *ride_sems, wait=True)

        @pl.when(n < nb)
        def _():
            mask_p, mask_c = _stack_masks(n)
            lane = lax.broadcasted_iota(jnp.int32, (1, ATT_Q_HEADS), 1)
            for half, g_ref in enumerate((g0_ref, g1_ref)):
                sl = slice(half * GATE_HALF, (half + 1) * GATE_HALF)
                gate = g_ref[...]
                s = _sigmoid(gate)
                dogv = dog_ref[:, sl]
                do_ref[:, sl] = dogv * (gate * s)
                dg_ref[:, sl] = dogv * o_ref[:, sl] * (s * (1.0 + gate * (1.0 - s)))
            ds_acc = jnp.zeros((1, ATT_Q_HEADS), F32)
            for kvh in range(ATT_KV_HEADS):
                cols = slice(kvh * GP, (kvh + 1) * GP)
                kp, kc, vp, vc = kp_ref[:, cols], kc_ref[:, cols], vp_ref[:, cols], vc_ref[:, cols]
                q_stack = _head_masked_rows(q_ref[:, cols], BF16)
                do_g = do_ref[:, cols]
                do_stack = _head_masked_rows(do_g, BF16)
                lse_g = lse_ref[:, cols]
                lse_stack = jnp.concatenate(
                    [_both_halves(lse_g[:, (r // 2) * LANES:(r // 2 + 1) * LANES])[r % 2] for r in range(ATT_GQA)], axis=0)
                pp = jnp.exp(jnp.where(
                    mask_p, lax.dot_general(q_stack, kp, NT_DIMS, preferred_element_type=F32) - lse_stack, NEG_INF))
                pc = jnp.exp(jnp.where(
                    mask_c, lax.dot_general(q_stack, kc, NT_DIMS, preferred_element_type=F32) - lse_stack, NEG_INF))
                dpp = lax.dot_general(do_stack, vp, NT_DIMS, preferred_element_type=F32)
                dpc = lax.dot_general(do_stack, vc, NT_DIMS, preferred_element_type=F32)
                delta = jnp.sum(pp * dpp + pc * dpc, axis=1, keepdims=True)
                dsp = (pp * (dpp - delta)).astype(BF16)
                dsc = (pc * (dpc - delta)).astype(BF16)
                dq_ref[:, cols] = _stack_fold(jnp.dot(dsp, kp, preferred_element_type=F32)
                                              + jnp.dot(dsc, kc, preferred_element_type=F32))
                dk_ref[:, cols] = ck_ref[:, cols] + lax.dot_general(dsp, q_stack, TN_DIMS, preferred_element_type=F32)
                dv_ref[:, cols] = cv_ref[:, cols] + lax.dot_general(pp.astype(BF16), do_stack, TN_DIMS,
                                                                    preferred_element_type=F32)
                ck_ref[:, cols] = lax.dot_general(dsc, q_stack, TN_DIMS, preferred_element_type=F32)
                cv_ref[:, cols] = lax.dot_general(pc.astype(BF16), do_stack, TN_DIMS, preferred_element_type=F32)
                t = jnp.exp(_stack_sinks(sink_ref, kvh) - lse_stack) * delta
                for r in range(ATT_GQA):
                    tot = jnp.sum(t[r * ATT_BLOCK:(r + 1) * ATT_BLOCK], axis=0, keepdims=True)
                    ds_acc = ds_acc - jnp.where(lane == kvh * ATT_GQA + r, tot[:, :ATT_Q_HEADS], 0.0)
            ds_ref[...] += ds_acc

    def cur(n):
        return jnp.minimum(n, nb - 1)

    def prev(n):
        return jnp.maximum(n - 1, 0)

    wide = pl.BlockSpec((ATT_BLOCK, ATT_WIDTH), lambda n: (cur(n), 0))
    late = pl.BlockSpec((ATT_BLOCK, ATT_WIDTH), lambda n: (prev(n), 0))
    return pl.pallas_call(
        body, grid=(nb + 1,),
        in_specs=[pl.BlockSpec(memory_space=pltpu.SMEM), wide,
                  pl.BlockSpec((ATT_BLOCK, ATT_WIDTH), lambda n: (prev(cur(n)), 1)),
                  pl.BlockSpec((ATT_BLOCK, ATT_WIDTH), lambda n: (cur(n), 1)),
                  pl.BlockSpec((ATT_BLOCK, ATT_WIDTH), lambda n: (prev(cur(n)), 2)),
                  pl.BlockSpec((ATT_BLOCK, ATT_WIDTH), lambda n: (cur(n), 2)),
                  pl.BlockSpec((ATT_BLOCK, GATE_HALF), lambda n: (cur(n), GATE_COL_BLOCK)),
                  pl.BlockSpec((ATT_BLOCK, GATE_HALF), lambda n: (cur(n), GATE_COL_BLOCK + 1)),
                  wide, wide, wide] + [ANY] * n_ride,
        out_specs=[wide, late, late, wide, pl.BlockSpec((1, ATT_Q_HEADS), lambda n: (0, 0))] + [ANY] * n_ride,
        out_shape=[jax.ShapeDtypeStruct((l, ATT_WIDTH), F32), jax.ShapeDtypeStruct((l, ATT_WIDTH), F32),
                   jax.ShapeDtypeStruct((l, ATT_WIDTH), F32), jax.ShapeDtypeStruct((l, ATT_WIDTH), F32),
                   jax.ShapeDtypeStruct((1, ATT_Q_HEADS), F32)] + _scatter_shapes(ride),
        scratch_shapes=[pltpu.VMEM((ATT_BLOCK, ATT_WIDTH), F32), pltpu.VMEM((ATT_BLOCK, ATT_WIDTH), F32),
                        pltpu.VMEM((ATT_BLOCK, ATT_WIDTH), F32)] + (_gather_sems(n_ride) if n_ride else []),
        compiler_params=_params("arbitrary"), name=name,
    )(sinks, qkv, qkv, qkv, qkv, qkv, proj, proj, o, lse, dog, *ride)


def _local_step(x, positions, pre_norm, post_norm, conv_b, dt_bias, a_log, d_skip, gate_norm, sinks, target,
                first_in, in_proj_with_first_pair, scan_with_second_pair, attn_bwd_with_second_pair_grads,
                in_dx_with_first_pair_grads):
    tables = _rope_tables(positions)
    dt_bias_pad = jnp.pad(dt_bias, ((0, 0), (0, SSM_DT_PAD - SSM_HEADS)))
    d_lanes = jnp.repeat(d_skip, SSM_HEAD_DIM, axis=1).reshape(-1, SSM_GROUPS, 1, GP)
    a_log_pad = jnp.pad(a_log, ((0, 0), (0, SSM_DT_PAD - SSM_HEADS)))
    pairs = [first_in, None]
    saved = []
    cur = x
    h = _rmsnorm_fwd(cur, pre_norm[0], "prenorm_fwd_0")
    for i in range(DEPTH):
        j = i // 2
        if i % 2 == 0:
            in_proj = functools.partial(_matmul, h, pairs[j]["ssm_w_in"], "nn", F32, f"ssm_in_{i}")
            if i == 0:
                proj, rest = in_proj_with_first_pair(in_proj)
                pairs[0] = {**first_in, **rest}
            else:
                proj = in_proj()
            pre, xbc = _conv_fwd(proj, pairs[j]["ssm_conv_w"], conv_b[j], f"conv_fwd_{i}")
            dtb, acsb, dtr, acs_r = _ssd_prep(proj, dt_bias_pad[j:j + 1], a_log_pad[j:j + 1], f"ssd_prep_{i}")
            scan = functools.partial(_ssd_fwd, xbc, dtb, acsb, acs_r, d_lanes[j], proj, gate_norm[j], f"ssd_fwd_{i}")
            if i == 0:
                y, act, hin, pairs[1] = scan_with_second_pair(scan)
            else:
                y, act, hin = scan()
            w_ssm_in = [p["ssm_w_in"] for p in pairs]
            w_ssm_out = [p["ssm_w_out"] for p in pairs]
            w_att_in = [p["att_w_in"] for p in pairs]
            w_att_out = [p["att_w_out"] for p in pairs]
            conv_w = [p["ssm_conv_w"] for p in pairs]
            ymix = _matmul(act, w_ssm_out[j], "nn", F32, f"ssm_out_{i}")
            saved.append(dict(x=cur, h=h, proj=proj, pre=pre, xbc=xbc, dtb=dtb, acsb=acsb, dtr=dtr, acs_r=acs_r, y=y,
                              hin=hin, act=act, ymix=ymix))
        else:
            proj = _matmul(h, w_att_in[j], "nn", F32, f"att_in_{i}")
            qkv = _rope_fwd(proj, tables, f"rope_fwd_{i}")
            act, o, lse = _attn_fwd(qkv, proj, sinks[j], f"attn_fwd_{i}")
            ymix = _matmul(act, w_att_out[j], "nn", F32, f"att_out_{i}")
            saved.append(dict(x=cur, h=h, proj=proj, qkv=qkv, o=o, lse=lse, act=act, ymix=ymix))
        if i + 1 < DEPTH:
            cur, h = _post_fwd(cur, ymix, post_norm[i], pre_norm[i + 1], f"post_fwd_{i}")

    gr = {k: [None] * 2 for k in ("ssm_w_in", "ssm_conv_w", "ssm_conv_b", "ssm_dt_bias", "ssm_a_log", "ssm_d",
                                  "ssm_gate_norm", "ssm_w_out", "att_w_in", "att_sinks", "att_w_out")}
    gr["pre_norm"] = [None] * DEPTH
    gr["post_norm"] = [None] * DEPTH
    last = DEPTH - 1
    g, dymix, loss_lanes, gr["post_norm"][last] = _post_loss(cur, ymix, post_norm[last], target, "post_loss")
    for i in reversed(range(DEPTH)):
        j = i // 2
        s = saved[i]
        if i % 2 == 0:
            dact = _matmul(dymix, w_ssm_out[j], "nt", F32, f"ssm_out_dx_{i}")
            gr["ssm_w_out"][j] = _matmul(s["act"], dymix, "tn", F32, f"ssm_out_dw_{i}")
            dxbc, ddt8, dal, dd, dproj, gr["ssm_gate_norm"][j] = _ssd_bwd(
                s["xbc"], s["dtb"], s["acsb"], s["dtr"], s["acs_r"], a_log[j], d_lanes[j], s["hin"], dact, s["y"],
                s["proj"], gate_norm[j], f"ssd_bwd_{i}")
            gr["ssm_a_log"][j] = dal.reshape(SSM_HEADS)
            gr["ssm_d"][j] = dd.reshape(SSM_HEADS)
            l = x.shape[0]
            ddt = jnp.pad(jnp.transpose(ddt8, (2, 0, 1)).reshape(l, SSM_HEADS), ((0, 0), (0, SSM_DT_PAD - SSM_HEADS)))
            dproj, dbias = _dt_bwd(ddt, s["proj"], dt_bias_pad[j:j + 1], dproj, f"dt_bwd_{i}")
            gr["ssm_dt_bias"][j] = dbias[0, :SSM_HEADS]
            dproj, gr["ssm_conv_w"][j], dcb = _conv_bwd(dxbc, s["pre"], s["proj"], conv_w[j], dproj, f"conv_bwd_{i}")
            gr["ssm_conv_b"][j] = dcb[0]
            w_in, key = w_ssm_in[j], "ssm_w_in"
        else:
            dog = _matmul(dymix, w_att_out[j], "nt", F32, f"att_out_dx_{i}")
            gr["att_w_out"][j] = _matmul(s["act"], dymix, "tn", F32, f"att_out_dw_{i}")
            attn_bwd = functools.partial(_attn_bwd, s["qkv"], s["proj"], sinks[j], s["o"], s["lse"], dog, f"attn_bwd_{i}")
            if i == 1:
                (dq, dk, dv, dgate, dsk), second_pair_reduced = attn_bwd_with_second_pair_grads(
                    attn_bwd, {k: gr[k][1] for k in BIG})
            else:
                dq, dk, dv, dgate, dsk = attn_bwd()
            gr["att_sinks"][j] = dsk[0]
            dproj = _rope_bwd(dq, dk, dv, dgate, tables, f"rope_bwd_{i}")
            w_in, key = w_att_in[j], "att_w_in"
        gr[key][j] = _matmul(s["h"], dproj, "tn", F32, f"in_dw_{i}")
        in_dx = functools.partial(_matmul, dproj, w_in, "nt", F32, f"in_dx_{i}")
        if i == 0:
            dh, first_pair_reduced = in_dx_with_first_pair_grads(in_dx, {k: gr[k][0] for k in BIG})
        else:
            dh = in_dx()
        if i > 0:
            g, dymix, gr["pre_norm"][i], gr["post_norm"][i - 1] = _norm_bwd_chain(
                dh, s["x"], pre_norm[i], g, saved[i - 1]["ymix"], post_norm[i - 1], f"norm_bwd_{i}")
        else:
            g, gr["pre_norm"][i] = _rmsnorm_bwd(dh, s["x"], pre_norm[i], g, f"prenorm_bwd_{i}")
    grads = {k: jnp.stack([v.reshape(v.shape[-1]) if k in ("pre_norm", "post_norm", "ssm_gate_norm") else v for v in vs])
             for k, vs in gr.items() if k not in BIG}
    return loss_lanes, g, grads, first_pair_reduced, second_pair_reduced


N_CHIPS = 4
N_DEV = 8
MESH = pl.DeviceIdType.MESH
ANY = pl.BlockSpec(memory_space=pl.ANY)


def _place():
    x, y, c = lax.axis_index("x"), lax.axis_index("y"), lax.axis_index("c")
    return x, y, c, 2 * x + y


def _gather_sems(n):
    return [pltpu.SemaphoreType.DMA((n, N_CHIPS)), pltpu.SemaphoreType.DMA((n, N_CHIPS)), pltpu.SemaphoreType.DMA((n,))]


def _gather_between_chips(ins, outs, send_sems, recv_sems, local_sems, wait):
    n = len(ins)
    _, _, c, s = _place()
    local = [pltpu.make_async_copy(ins[w], outs[w].at[s], local_sems.at[w]) for w in range(n)]

    def remote(w, t):
        return pltpu.make_async_remote_copy(
            src_ref=ins[w].at[c], dst_ref=outs[w].at[s, c], send_sem=send_sems.at[w, t],
            recv_sem=recv_sems.at[w, s], device_id=(t // 2, t % 2, c), device_id_type=MESH)

    def arrival(w, t):
        return pltpu.make_async_remote_copy(
            src_ref=ins[w].at[c], dst_ref=outs[w].at[t, c], send_sem=send_sems.at[w, t],
            recv_sem=recv_sems.at[w, t], device_id=(t // 2, t % 2, c), device_id_type=MESH)

    if not wait:
        for cp in local:
            cp.start()
    for t in range(N_CHIPS):
        @pl.when(s != t)
        def _():
            for w in range(n):
                if wait:
                    remote(w, t).wait_send()
                    arrival(w, t).wait_recv()
                else:
                    remote(w, t).start()
    if wait:
        for cp in local:
            cp.wait()


def _pair_handoff(bufs, name):
    n = len(bufs)

    def body(*refs):
        outs = refs[n:2 * n]
        send_sems, recv_sems = refs[2 * n:]
        x, y, c, s = _place()

        def handed_on(w, t):
            return pltpu.make_async_remote_copy(
                src_ref=outs[w].at[t, c], dst_ref=outs[w].at[t, c], send_sem=send_sems.at[w, t],
                recv_sem=recv_sems.at[w, t], device_id=(x, y, 1 - c), device_id_type=MESH)

        def handed_in(w, t):
            return pltpu.make_async_remote_copy(
                src_ref=outs[w].at[t, 1 - c], dst_ref=outs[w].at[t, 1 - c], send_sem=send_sems.at[w, t],
                recv_sem=recv_sems.at[w, t], device_id=(x, y, 1 - c), device_id_type=MESH)

        for t in range(N_CHIPS):
            @pl.when(s != t)
            def _():
                for w in range(n):
                    handed_on(w, t).start()
        for t in range(N_CHIPS):
            @pl.when(s != t)
            def _():
                for w in range(n):
                    handed_on(w, t).wait_send()
                    handed_in(w, t).wait_recv()

    return pl.pallas_call(
        body, in_specs=[ANY] * n, out_specs=[ANY] * n,
        out_shape=[jax.ShapeDtypeStruct(a.shape, a.dtype) for a in bufs],
        scratch_shapes=[pltpu.SemaphoreType.DMA((n, N_CHIPS)), pltpu.SemaphoreType.DMA((n, N_CHIPS))],
        input_output_aliases={w: w for w in range(n)}, name=name,
    )(*bufs)


def _chip_gather(shards, name):
    n = len(shards)

    def body(*refs):
        ins, outs = refs[:n], refs[n:2 * n]
        _gather_between_chips(ins, outs, *refs[2 * n:], wait=False)
        _gather_between_chips(ins, outs, *refs[2 * n:], wait=True)

    bufs = pl.pallas_call(
        body, in_specs=[ANY] * n, out_specs=[ANY] * n,
        out_shape=[jax.ShapeDtypeStruct((N_CHIPS,) + a.shape, a.dtype) for a in shards],
        scratch_shapes=_gather_sems(n), name=name,
    )(*shards)
    return _pair_handoff(bufs, name + "_handoff")


def _pair_swap(parts, name):
    n = len(parts)

    def body(*refs):
        ins, outs = refs[:n], refs[n:2 * n]
        send_sems, recv_sems = refs[2 * n:]
        x, y, c, _ = _place()
        cps = [pltpu.make_async_remote_copy(
            src_ref=ins[w].at[1 - c], dst_ref=outs[w], send_sem=send_sems.at[w], recv_sem=recv_sems.at[w],
            device_id=(x, y, 1 - c), device_id_type=MESH) for w in range(n)]
        for cp in cps:
            cp.start()
        for cp in cps:
            cp.wait()

    return pl.pallas_call(
        body, in_specs=[ANY] * n, out_specs=[ANY] * n,
        out_shape=[jax.ShapeDtypeStruct(a.shape[1:], a.dtype) for a in parts],
        scratch_shapes=[pltpu.SemaphoreType.DMA((n,)), pltpu.SemaphoreType.DMA((n,))],
        name=name,
    )(*parts)


def _scatter_between_chips(ins, outs, send_sems, recv_sems, local_sems, wait):
    n = len(ins)
    _, _, c, s = _place()

    def block(w, t):
        rows = ins[w].shape[0] // N_CHIPS
        return ins[w].at[pl.ds(t * rows, rows)]

    local = [pltpu.make_async_copy(block(w, s), outs[w].at[s], local_sems.at[w]) for w in range(n)]

    def remote(w, t):
        return pltpu.make_async_remote_copy(
            src_ref=block(w, t), dst_ref=outs[w].at[s], send_sem=send_sems.at[w, t], recv_sem=recv_sems.at[w, s],
            device_id=(t // 2, t % 2, c), device_id_type=MESH)

    def arrival(w, t):
        return pltpu.make_async_remote_copy(
            src_ref=block(w, t), dst_ref=outs[w].at[t], send_sem=send_sems.at[w, t], recv_sem=recv_sems.at[w, t],
            device_id=(t // 2, t % 2, c), device_id_type=MESH)

    if not wait:
        for cp in local:
            cp.start()
    for t in range(N_CHIPS):
        @pl.when(s != t)
        def _():
            for w in range(n):
                if wait:
                    remote(w, t).wait_send()
                    arrival(w, t).wait_recv()
                else:
                    remote(w, t).start()
    if wait:
        for cp in local:
            cp.wait()


def _scatter_shapes(parts):
    return [jax.ShapeDtypeStruct((N_CHIPS, a.shape[0] // N_CHIPS, a.shape[1]), a.dtype) for a in parts]


def _pair_merge(parts, name):
    n = len(parts)

    def body(*refs):
        ins, outs = refs[:n], refs[n:2 * n]
        send_sems, recv_sems = refs[2 * n:]
        x, y, c, _ = _place()
        cps = [pltpu.make_async_remote_copy(
            src_ref=ins[w], dst_ref=outs[w], send_sem=send_sems.at[w], recv_sem=recv_sems.at[w],
            device_id=(x, y, 1 - c), device_id_type=MESH) for w in range(n)]
        for cp in cps:
            cp.start()
        for cp in cps:
            cp.wait()

    return pl.pallas_call(
        body, in_specs=[ANY] * n, out_specs=[ANY] * n,
        out_shape=[jax.ShapeDtypeStruct(a.shape, a.dtype) for a in parts],
        scratch_shapes=[pltpu.SemaphoreType.DMA((n,)), pltpu.SemaphoreType.DMA((n,))],
        name=name,
    )(*parts)


def _all_gather_small(a, name):
    def body(in_ref, out_ref, send_sems, recv_sems, local_sem):
        x, y, c, _ = _place()
        me = 4 * x + 2 * y + c
        local = pltpu.make_async_copy(in_ref, out_ref.at[me], local_sem)
        local.start()

        def remote(d):
            return pltpu.make_async_remote_copy(
                src_ref=in_ref, dst_ref=out_ref.at[me], send_sem=send_sems.at[d], recv_sem=recv_sems.at[me],
                device_id=(d // 4, (d // 2) % 2, d % 2), device_id_type=MESH)

        def arrival(d):
            return pltpu.make_async_remote_copy(
                src_ref=in_ref, dst_ref=out_ref.at[d], send_sem=send_sems.at[d], recv_sem=recv_sems.at[d],
                device_id=(d // 4, (d // 2) % 2, d % 2), device_id_type=MESH)

        for d in range(N_DEV):
            @pl.when(me != d)
            def _():
                remote(d).start()
        for d in range(N_DEV):
            @pl.when(me != d)
            def _():
                remote(d).wait_send()
                arrival(d).wait_recv()
        local.wait()

    return pl.pallas_call(
        body, in_specs=[ANY], out_specs=ANY, out_shape=jax.ShapeDtypeStruct((N_DEV,) + a.shape, a.dtype),
        scratch_shapes=[pltpu.SemaphoreType.DMA((N_DEV,)), pltpu.SemaphoreType.DMA((N_DEV,)), pltpu.SemaphoreType.DMA],
        name=name,
    )(a)


def _reduce_tile(rows):
    return _pick(rows, (256, 128, 16))


def _pair_add(full, other, layer, name):
    _, rows, cols = full.shape
    tr = _reduce_tile(rows)

    def body(layer_ref, a_ref, b_ref, o_ref):
        o_ref[...] = (a_ref[0] + b_ref[...]).astype(o_ref.dtype)

    return pl.pallas_call(
        body,
        grid_spec=pltpu.PrefetchScalarGridSpec(
            num_scalar_prefetch=1, grid=(rows // tr,),
            in_specs=[pl.BlockSpec((1, tr, cols), lambda i, lr: (lr[0], i, 0)), pl.BlockSpec((tr, cols), lambda i, lr: (i, 0))],
            out_specs=pl.BlockSpec((tr, cols), lambda i, lr: (i, 0))),
        out_shape=jax.ShapeDtypeStruct((rows, cols), BF16), compiler_params=_params("parallel"), name=name,
    )(layer, full, other)


def _sum_slots(a, name):
    n, rows, cols = a.shape
    tr = _reduce_tile(rows)

    def body(a_ref, o_ref):
        acc = a_ref[0].astype(F32)
        for k in range(1, n):
            acc = acc + a_ref[k].astype(F32)
        o_ref[...] = acc

    return pl.pallas_call(
        body, grid=(rows // tr,), in_specs=[pl.BlockSpec((n, tr, cols), lambda i: (0, i, 0))],
        out_specs=pl.BlockSpec((tr, cols), lambda i: (i, 0)),
        out_shape=jax.ShapeDtypeStruct((rows, cols), F32), compiler_params=_params("parallel"), name=name,
    )(a)


def _adamw(w, g, m, v, name):
    rows, cols = w.shape
    tr = _pick(rows, (256, 8))

    def body(w_ref, g_ref, m_ref, v_ref, d_ref, nm_ref, nv_ref):
        gv = g_ref[...]
        mn = ADAM_B1 * m_ref[...] + (1.0 - ADAM_B1) * gv
        vn = ADAM_B2 * v_ref[...] + (1.0 - ADAM_B2) * jnp.square(gv)
        m_hat = mn / (1.0 - ADAM_B1 ** ADAM_STEP)
        v_hat = vn / (1.0 - ADAM_B2 ** ADAM_STEP)
        d_ref[...] = -ADAM_LR * (m_hat / (jnp.sqrt(v_hat) + ADAM_EPS) + ADAM_WD * w_ref[...])
        nm_ref[...] = mn
        nv_ref[...] = vn

    blk = pl.BlockSpec((tr, cols), lambda i: (i, 0))
    return pl.pallas_call(
        body, grid=(rows // tr,), in_specs=[blk] * 4, out_specs=[blk] * 3,
        out_shape=[jax.ShapeDtypeStruct((rows, cols), F32)] * 3, compiler_params=_params("parallel"), name=name,
    )(w, g, m, v)


BIG = ("ssm_w_in", "ssm_w_out", "att_w_in", "att_w_out")
SHARDED = BIG + ("ssm_conv_w",)
SMALL = ("pre_norm", "post_norm", "ssm_conv_b", "ssm_dt_bias", "ssm_a_log", "ssm_d", "ssm_gate_norm", "att_sinks")
WEIGHTS = ("pre_norm", "post_norm", "ssm_w_in", "ssm_conv_w", "ssm_conv_b", "ssm_dt_bias", "ssm_a_log", "ssm_d",
           "ssm_gate_norm", "ssm_w_out", "att_w_in", "att_sinks", "att_w_out")


def _halves(a):
    return a.reshape(2, a.shape[0] // 2, a.shape[1])


def _layer_shards(j, ssm_w_in, ssm_w_out, att_w_in, att_w_out, ssm_conv_w):
    return [_halves(ssm_w_in[j].astype(BF16)), _halves(ssm_w_out[j].astype(BF16)), _halves(att_w_in[j].astype(BF16)),
            _halves(att_w_out[j].astype(BF16)), _halves(ssm_conv_w[j])]


SHARD_KEYS = ("ssm_w_in", "ssm_w_out", "att_w_in", "att_w_out", "ssm_conv_w")


def _whole_weights(keys, gathered):
    out = {}
    for k, g in zip(keys, gathered):
        g = g.reshape((N_CHIPS, 2 * g.shape[2], g.shape[3]))
        if k in ("ssm_w_out", "att_w_out"):
            out[k] = g.reshape(N_CHIPS * g.shape[1], g.shape[2])
        else:
            out[k] = jnp.transpose(g, (1, 0, 2)).reshape(g.shape[1], N_CHIPS * g.shape[2])
    if "ssm_w_in" in out:
        out["ssm_w_in"] = jnp.pad(out["ssm_w_in"], ((0, 0), (0, SSM_IN_PAD - SSM_IN_DIM)))
    return out


def _halves_by_chip(key, g):
    if key in ("ssm_w_out", "att_w_out"):
        rows = g.shape[0] // N_CHIPS
        blocks = g.reshape(N_CHIPS, 2, rows // 2, g.shape[1])
        return jnp.transpose(blocks, (1, 0, 2, 3)).reshape(2, N_CHIPS * (rows // 2), g.shape[1])
    cols = (SSM_IN_DIM if key == "ssm_w_in" else g.shape[1]) // N_CHIPS
    rows = g.shape[0]
    blocks = g[:, :N_CHIPS * cols].reshape(2, rows // 2, N_CHIPS, cols)
    return jnp.transpose(blocks, (0, 2, 1, 3)).reshape(2, N_CHIPS * (rows // 2), cols)


def _pack_small(tree, keys):
    flat = jnp.concatenate([tree[k].reshape(-1) for k in keys])
    rows = -(-flat.shape[0] // (8 * LANES)) * 8
    return jnp.pad(flat, (0, rows * LANES - flat.shape[0])).reshape(rows, LANES)


def _unpack_small(packed, shapes, keys):
    flat = packed.reshape(-1)
    out, at = {}, 0
    for k in keys:
        n = 1
        for dim in shapes[k]:
            n *= dim
        out[k] = flat[at:at + n].reshape(shapes[k])
        at += n
    return out


def kernel(x, positions, pre_norm, post_norm, ssm_w_in, ssm_conv_w, ssm_conv_b, ssm_dt_bias, ssm_a_log, ssm_d, ssm_gate_norm, ssm_w_out, att_w_in, att_sinks, att_w_out, loss_target, m_pre_norm, m_post_norm, m_ssm_w_in, m_ssm_conv_w, m_ssm_conv_b, m_ssm_dt_bias, m_ssm_a_log, m_ssm_d, m_ssm_gate_norm, m_ssm_w_out, m_att_w_in, m_att_sinks, m_att_w_out, v_pre_norm, v_post_norm, v_ssm_w_in, v_ssm_conv_w, v_ssm_conv_b, v_ssm_dt_bias, v_ssm_a_log, v_ssm_d, v_ssm_gate_norm, v_ssm_w_out, v_att_w_in, v_att_sinks, v_att_w_out):
    w = dict(pre_norm=pre_norm, post_norm=post_norm, ssm_w_in=ssm_w_in, ssm_conv_w=ssm_conv_w, ssm_conv_b=ssm_conv_b,
             ssm_dt_bias=ssm_dt_bias, ssm_a_log=ssm_a_log, ssm_d=ssm_d, ssm_gate_norm=ssm_gate_norm, ssm_w_out=ssm_w_out,
             att_w_in=att_w_in, att_sinks=att_sinks, att_w_out=att_w_out)
    m = dict(pre_norm=m_pre_norm, post_norm=m_post_norm, ssm_w_in=m_ssm_w_in, ssm_conv_w=m_ssm_conv_w, ssm_conv_b=m_ssm_conv_b,
             ssm_dt_bias=m_ssm_dt_bias, ssm_a_log=m_ssm_a_log, ssm_d=m_ssm_d, ssm_gate_norm=m_ssm_gate_norm,
             ssm_w_out=m_ssm_w_out, att_w_in=m_att_w_in, att_sinks=m_att_sinks, att_w_out=m_att_w_out)
    v = dict(pre_norm=v_pre_norm, post_norm=v_post_norm, ssm_w_in=v_ssm_w_in, ssm_conv_w=v_ssm_conv_w, ssm_conv_b=v_ssm_conv_b,
             ssm_dt_bias=v_ssm_dt_bias, ssm_a_log=v_ssm_a_log, ssm_d=v_ssm_d, ssm_gate_norm=v_ssm_gate_norm,
             ssm_w_out=v_ssm_w_out, att_w_in=v_att_w_in, att_sinks=v_att_sinks, att_w_out=v_att_w_out)
    c = lax.axis_index("c")
    chip = 2 * lax.axis_index("x") + lax.axis_index("y")

    sharded = (ssm_w_in, ssm_w_out, att_w_in, att_w_out, ssm_conv_w)
    own = [dict(zip(SHARD_KEYS, _layer_shards(j, *sharded))) for j in range(2)]
    now_keys = ("ssm_w_in", "ssm_conv_w")
    later_keys = ("ssm_w_out", "att_w_in", "att_w_out")
    first_in = _whole_weights(now_keys, _chip_gather([own[0][k] for k in now_keys], "gather_weights_0"))

    def in_proj_with_first_pair(matmul):
        proj, *arrived = matmul(ride=[own[0][k] for k in later_keys])
        return proj, _whole_weights(later_keys, _pair_handoff(arrived, "gather_weights_0_rest_handoff"))

    def scan_with_second_pair(scan):
        y, act, hin, *arrived = scan(ride=[own[1][k] for k in SHARD_KEYS])
        return y, act, hin, _whole_weights(SHARD_KEYS, _pair_handoff(arrived, "gather_weights_1_handoff"))

    half = jnp.reshape(c, (1,)).astype(jnp.int32)

    def reduce_begin(pair_grads, tag):
        parts = [_halves_by_chip(k, pair_grads[k]) for k in BIG]
        from_sibling = _pair_swap(parts, f"reduce_pair_swap_{tag}")
        return [_pair_add(p, o, half, f"reduce_pair_add_{tag}_{n}") for n, (p, o) in enumerate(zip(parts, from_sibling))]

    def reduce_end(by_chip, tag):
        mine = [_sum_slots(a, f"reduce_chip_sum_{tag}_{n}") for n, a in enumerate(by_chip)]
        theirs = _pair_merge(mine, f"reduce_pair_merge_{tag}")
        return {k: jnp.where(c == 0, jnp.concatenate([a, b]), jnp.concatenate([b, a])) for k, a, b in zip(BIG, mine, theirs)}

    def attn_bwd_with_second_pair_grads(attn_bwd, pair_grads):
        dq, dk, dv, dgate, dsk, *by_chip = attn_bwd(ride=reduce_begin(pair_grads, "1"))
        return (dq, dk, dv, dgate, dsk), reduce_end(by_chip, "1")

    def in_dx_with_first_pair_grads(matmul, pair_grads):
        dh, *by_chip = matmul(ride=reduce_begin(pair_grads, "0"), ride_scatters=True)
        return dh, reduce_end(by_chip, "0")

    loss_lanes, grad_x, gr, reduced_0, reduced_1 = _local_step(
        x[0], positions[0], pre_norm, post_norm, ssm_conv_b, ssm_dt_bias, ssm_a_log, ssm_d, ssm_gate_norm, att_sinks,
        loss_target[0], first_in, in_proj_with_first_pair, scan_with_second_pair, attn_bwd_with_second_pair_grads,
        in_dx_with_first_pair_grads)
    loss = lax.psum(0.5 * jnp.sum(loss_lanes) / D_MODEL, ("x", "y", "c"))
    grads = {k: jnp.stack([reduced_0[k], reduced_1[k]]) for k in BIG}

    small_keys = SMALL + ("ssm_conv_w",)
    small_shapes = {k: w[k].shape for k in SMALL}
    small_shapes["ssm_conv_w"] = gr["ssm_conv_w"].shape
    small_sum = _sum_slots(_all_gather_small(_pack_small(gr, small_keys), "reduce_small_gather"), "reduce_small_sum")
    grads.update(_unpack_small(small_sum, small_shapes, small_keys))
    conv_cols = ssm_conv_w.shape[2]
    grads["ssm_conv_w"] = lax.dynamic_slice_in_dim(grads["ssm_conv_w"], chip * conv_cols, conv_cols, axis=2)

    delta, new_m, new_v = {}, {}, {}
    for k in SHARDED:
        shp = w[k].shape
        two_d = (shp[0] * shp[1], shp[2])
        d_, m_, v_ = _adamw(w[k].reshape(two_d), grads[k].reshape(two_d), m[k].reshape(two_d), v[k].reshape(two_d),
                            f"adamw_{k}")
        delta[k], new_m[k], new_v[k] = d_.reshape(shp), m_.reshape(shp), v_.reshape(shp)
    d_, m_, v_ = _adamw(_pack_small(w, SMALL), _pack_small(grads, SMALL), _pack_small(m, SMALL), _pack_small(v, SMALL),
                        "adamw_small")
    delta.update(_unpack_small(d_, small_shapes, SMALL))
    new_m.update(_unpack_small(m_, small_shapes, SMALL))
    new_v.update(_unpack_small(v_, small_shapes, SMALL))

    return (loss, grad_x[None], *[grads[k] for k in WEIGHTS], *[delta[k] for k in WEIGHTS],
            *[new_m[k] for k in WEIGHTS], *[new_v[k] for k in WEIGHTS])
```

```python
import functools

import jax
import jax.numpy as jnp
from jax import lax
from jax.experimental import pallas as pl
from jax.experimental.pallas import tpu as pltpu

F32 = jnp.float32
BF16 = jnp.bfloat16
EPS = 1e-6
NEG_INF = float("-inf")

D_MODEL = 1024
DEPTH = 4
SSM_D_INNER = 2048
SSM_HEAD_DIM = 64
SSM_HEADS = 32
SSM_GROUPS = 8
SSM_HPG = 4
SSM_STATE = 128
SSM_CONV = 4
SSM_CHUNK = 128
SSM_BC_DIM = 1024
SSM_CONV_DIM = 4096
SSM_IN_DIM = 6176
SSM_IN_PAD = 6272
SSM_DT_PAD = 128
ATT_HEAD_DIM = 64
ATT_Q_HEADS = 16
ATT_KV_HEADS = 4
ATT_GQA = 4
ATT_WIDTH = 1024
ATT_KV_WIDTH = 256
ATT_IN_DIM = 2560
ATT_QKV = ATT_WIDTH + 2 * ATT_KV_WIDTH
ATT_BLOCK = 128
ROPE_THETA = 500000.0
ROPE_DIM = 16
ROPE_HALF = 8
Q_SCALE = ATT_HEAD_DIM ** -0.5

ADAM_LR = 0.001
ADAM_B1 = 0.9
ADAM_B2 = 0.999
ADAM_EPS = 1e-08
ADAM_WD = 0.01
ADAM_STEP = 10

VMEM_LIMIT_BYTES = 48 * 1024 * 1024
NT_DIMS = (((1,), (1,)), ((), ()))
TN_DIMS = (((0,), (0,)), ((), ()))


def _params(*sem):
    return pltpu.CompilerParams(dimension_semantics=sem, vmem_limit_bytes=VMEM_LIMIT_BYTES)


def _pick(n, cands):
    for c in cands:
        if n % c == 0:
            return c
    return n


def _sigmoid(v):
    return 0.5 * jnp.tanh(0.5 * v) + 0.5


def _bdot_tn(a, b):
    return lax.dot_general(a.astype(BF16), b.astype(BF16), TN_DIMS, preferred_element_type=F32)


MATMUL_VMEM_BUDGET = 36 * 1024 * 1024


def _matmul_tiles(m, n, k, out_bytes, reduce_rows):
    best = None
    whole = [k] if (not reduce_rows or k <= 2048) else []
    for tk in whole + [c for c in (4096, 2048, 1024, 896, 512) if k % c == 0 and c < k]:
        for tm in (c for c in (2048, 1024, 512, 256) if m % c == 0):
            for tn in (c for c in (n, 1280, 1024, 896, 640, 512) if n % c == 0):
                acc = tm * tn * 4 if tk < k else 0
                need = 2 * (2 * tk * (tm + tn) + tm * tn * out_bytes) + acc
                if need <= MATMUL_VMEM_BUDGET and (best is None or tm * tn * min(tk, 2048) > best[0]):
                    best = (tm * tn * min(tk, 2048), tm, tn, tk)
        if best is not None and not reduce_rows:
            break
    return best[1:]


def _matmul(a, b, mode, out_dtype, name, ride=(), ride_scatters=False):
    if mode == "nn":
        (m, k), n = a.shape, b.shape[1]
    elif mode == "nt":
        (m, k), n = a.shape, b.shape[0]
    else:
        (k, m), n = a.shape, b.shape[1]
    tm, tn, tk = _matmul_tiles(m, n, k, jnp.dtype(out_dtype).itemsize, mode == "tn")
    nk = k // tk
    steps = (n // tn, m // tm, nk)
    dims = {"nn": (((1,), (0,)), ((), ())), "nt": NT_DIMS, "tn": TN_DIMS}[mode]
    n_ride = len(ride)
    exchange = _scatter_between_chips if ride_scatters else _gather_between_chips
    arrived = _scatter_shapes(ride) if ride_scatters else [jax.ShapeDtypeStruct((N_CHIPS,) + r.shape, r.dtype) for r in ride]

    def body(*refs):
        a_ref, b_ref = refs[:2]
        ride_in = refs[2:2 + n_ride]
        o_ref = refs[2 + n_ride]
        ride_out = refs[3 + n_ride:3 + 2 * n_ride]
        acc_ref = refs[3 + 2 * n_ride]
        ride_sems = refs[4 + 2 * n_ride:]
        kk = pl.program_id(2)
        at = [pl.program_id(d) for d in range(3)]
        if n_ride:
            @pl.when((at[0] == 0) & (at[1] == 0) & (at[2] == 0))
            def _():
                exchange(ride_in, ride_out, *ride_sems, wait=False)

        part = lax.dot_general(a_ref[...], b_ref[...], dims, preferred_element_type=F32)
        if nk == 1:
            o_ref[...] = part.astype(o_ref.dtype)
        else:
            @pl.when(kk == 0)
            def _():
                acc_ref[...] = part

            @pl.when(kk > 0)
            def _():
                acc_ref[...] += part

            @pl.when(kk == nk - 1)
            def _():
                o_ref[...] = acc_ref[...].astype(o_ref.dtype)

        if n_ride:
            @pl.when((at[0] == steps[0] - 1) & (at[1] == steps[1] - 1) & (at[2] == steps[2] - 1))
            def _():
                exchange(ride_in, ride_out, *ride_sems, wait=True)

    if mode == "nn":
        a_spec = pl.BlockSpec((tm, tk), lambda j, i, kk: (i, kk))
        b_spec = pl.BlockSpec((tk, tn), lambda j, i, kk: (kk, j))
    elif mode == "nt":
        a_spec = pl.BlockSpec((tm, tk), lambda j, i, kk: (i, kk))
        b_spec = pl.BlockSpec((tn, tk), lambda j, i, kk: (j, kk))
    else:
        a_spec = pl.BlockSpec((tk, tm), lambda j, i, kk: (kk, i))
        b_spec = pl.BlockSpec((tk, tn), lambda j, i, kk: (kk, j))
    out = pl.pallas_call(
        body, grid=steps, in_specs=[a_spec, b_spec] + [ANY] * n_ride,
        out_specs=[pl.BlockSpec((tm, tn), lambda j, i, kk: (i, j))] + [ANY] * n_ride,
        out_shape=[jax.ShapeDtypeStruct((m, n), out_dtype)] + arrived,
        scratch_shapes=[pltpu.VMEM((tm, tn), F32)] + (_gather_sems(n_ride) if n_ride else []),
        compiler_params=_params(*(["arbitrary"] * 3 if n_ride else ["parallel", "parallel", "arbitrary"])), name=name,
    )(a, b, *ride)
    return out if n_ride else out[0]


def _row_tile(l):
    return _pick(l, (512, 256, 128))


def _rmsnorm_fwd(x, w, name):
    l, d = x.shape
    tl = _row_tile(l)

    def body(x_ref, w_ref, o_ref):
        xv = x_ref[...]
        r = lax.rsqrt(jnp.mean(xv * xv, axis=-1, keepdims=True) + EPS)
        o_ref[...] = (xv * r * w_ref[...]).astype(o_ref.dtype)

    return pl.pallas_call(
        body, grid=(l // tl,),
        in_specs=[pl.BlockSpec((tl, d), lambda i: (i, 0)), pl.BlockSpec((1, d), lambda i: (0, 0))],
        out_specs=pl.BlockSpec((tl, d), lambda i: (i, 0)),
        out_shape=jax.ShapeDtypeStruct((l, d), BF16), compiler_params=_params("parallel"), name=name,
    )(x, w.reshape(1, d))


def _post_fwd(x, y, w, w_next, name):
    l, d = x.shape
    tl = _row_tile(l)

    def body(x_ref, y_ref, w_ref, wn_ref, o_ref, h_ref):
        yv = y_ref[...]
        r = lax.rsqrt(jnp.mean(yv * yv, axis=-1, keepdims=True) + EPS)
        out = x_ref[...] + yv * r * w_ref[...]
        o_ref[...] = out
        rn = lax.rsqrt(jnp.mean(out * out, axis=-1, keepdims=True) + EPS)
        h_ref[...] = (out * rn * wn_ref[...]).astype(h_ref.dtype)

    row = pl.BlockSpec((tl, d), lambda i: (i, 0))
    vec = pl.BlockSpec((1, d), lambda i: (0, 0))
    return pl.pallas_call(
        body, grid=(l // tl,), in_specs=[row, row, vec, vec], out_specs=[row, row],
        out_shape=[jax.ShapeDtypeStruct((l, d), F32), jax.ShapeDtypeStruct((l, d), BF16)],
        compiler_params=_params("parallel"), name=name,
    )(x, y, w.reshape(1, d), w_next.reshape(1, d))


def _post_loss(x, y, w, t, name):
    l, d = x.shape
    tl = _row_tile(l)
    nt = l // tl

    def body(x_ref, y_ref, w_ref, t_ref, g_ref, dy_ref, ls_ref, dw_ref, acc_ref):
        i = pl.program_id(0)

        @pl.when(i == 0)
        def _():
            ls_ref[...] = jnp.zeros_like(ls_ref)
            acc_ref[...] = jnp.zeros_like(acc_ref)

        yv = y_ref[...]
        r = lax.rsqrt(jnp.mean(yv * yv, axis=-1, keepdims=True) + EPS)
        nrm = yv * r
        e = x_ref[...] + nrm * w_ref[...] - t_ref[...]
        gv = e * (1.0 / d)
        g_ref[...] = gv
        ls_ref[...] += jnp.sum((e * e).reshape(tl // 8, 8, d), axis=0)
        gw = gv * w_ref[...]
        dy_ref[...] = (r * (gw - nrm * jnp.mean(gw * nrm, axis=-1, keepdims=True))).astype(dy_ref.dtype)
        acc_ref[...] += jnp.sum((gv * nrm).reshape(tl // 8, 8, d), axis=0)

        @pl.when(i == nt - 1)
        def _():
            dw_ref[...] = jnp.sum(acc_ref[...], axis=0, keepdims=True)

    row = pl.BlockSpec((tl, d), lambda i: (i, 0))
    vec = pl.BlockSpec((1, d), lambda i: (0, 0))
    return pl.pallas_call(
        body, grid=(nt,), in_specs=[row, row, vec, row],
        out_specs=[row, row, pl.BlockSpec((8, d), lambda i: (0, 0)), vec],
        out_shape=[jax.ShapeDtypeStruct((l, d), F32), jax.ShapeDtypeStruct((l, d), BF16),
                   jax.ShapeDtypeStruct((8, d), F32), jax.ShapeDtypeStruct((1, d), F32)],
        scratch_shapes=[pltpu.VMEM((8, d), F32)], compiler_params=_params("arbitrary"), name=name,
    )(x, y, w.reshape(1, d), t)


def _norm_bwd_chain(dh, x, w_pre, resid, y_prev, w_post_prev, name):
    l, d = x.shape
    tl = _row_tile(l)
    nt = l // tl

    def body(dh_ref, x_ref, wp_ref, r_ref, y_ref, wq_ref, g_ref, dy_ref, dwp_ref, dwq_ref, accp_ref, accq_ref):
        i = pl.program_id(0)

        @pl.when(i == 0)
        def _():
            accp_ref[...] = jnp.zeros_like(accp_ref)
            accq_ref[...] = jnp.zeros_like(accq_ref)

        xv = x_ref[...]
        dhv = dh_ref[...]
        rx = lax.rsqrt(jnp.mean(xv * xv, axis=-1, keepdims=True) + EPS)
        nx = xv * rx
        gw = dhv * wp_ref[...]
        gv = rx * (gw - nx * jnp.mean(gw * nx, axis=-1, keepdims=True)) + r_ref[...]
        g_ref[...] = gv
        accp_ref[...] += jnp.sum((dhv * nx).reshape(tl // 8, 8, d), axis=0)
        yv = y_ref[...]
        ry = lax.rsqrt(jnp.mean(yv * yv, axis=-1, keepdims=True) + EPS)
        ny = yv * ry
        gq = gv * wq_ref[...]
        dy_ref[...] = (ry * (gq - ny * jnp.mean(gq * ny, axis=-1, keepdims=True))).astype(dy_ref.dtype)
        accq_ref[...] += jnp.sum((gv * ny).reshape(tl // 8, 8, d), axis=0)

        @pl.when(i == nt - 1)
        def _():
            dwp_ref[...] = jnp.sum(accp_ref[...], axis=0, keepdims=True)
            dwq_ref[...] = jnp.sum(accq_ref[...], axis=0, keepdims=True)

    row = pl.BlockSpec((tl, d), lambda i: (i, 0))
    vec = pl.BlockSpec((1, d), lambda i: (0, 0))
    return pl.pallas_call(
        body, grid=(nt,), in_specs=[row, row, vec, row, row, vec], out_specs=[row, row, vec, vec],
        out_shape=[jax.ShapeDtypeStruct((l, d), F32), jax.ShapeDtypeStruct((l, d), BF16),
                   jax.ShapeDtypeStruct((1, d), F32), jax.ShapeDtypeStruct((1, d), F32)],
        scratch_shapes=[pltpu.VMEM((8, d), F32), pltpu.VMEM((8, d), F32)],
        compiler_params=_params("arbitrary"), name=name,
    )(dh, x, w_pre.reshape(1, d), resid, y_prev, w_post_prev.reshape(1, d))


def _rmsnorm_bwd(g, y, w, resid, name):
    l, d = y.shape
    tl = _row_tile(l)
    nt = l // tl

    def body(g_ref, y_ref, w_ref, r_ref, dy_ref, dw_ref, acc_ref):
        i = pl.program_id(0)

        @pl.when(i == 0)
        def _():
            acc_ref[...] = jnp.zeros_like(acc_ref)

        yv = y_ref[...]
        gv = g_ref[...]
        r = lax.rsqrt(jnp.mean(yv * yv, axis=-1, keepdims=True) + EPS)
        nrm = yv * r
        gw = gv * w_ref[...]
        dy_ref[...] = r * (gw - nrm * jnp.mean(gw * nrm, axis=-1, keepdims=True)) + r_ref[...]
        acc_ref[...] += jnp.sum((gv * nrm).reshape(tl // 8, 8, d), axis=0)

        @pl.when(i == nt - 1)
        def _():
            dw_ref[...] = jnp.sum(acc_ref[...], axis=0, keepdims=True)

    row = pl.BlockSpec((tl, d), lambda i: (i, 0))
    vec = pl.BlockSpec((1, d), lambda i: (0, 0))
    return pl.pallas_call(
        body, grid=(nt,), in_specs=[row, row, vec, row], out_specs=[row, vec],
        out_shape=[jax.ShapeDtypeStruct((l, d), F32), jax.ShapeDtypeStruct((1, d), F32)],
        scratch_shapes=[pltpu.VMEM((8, d), F32)], compiler_params=_params("arbitrary"), name=name,
    )(g, y, w.reshape(1, d), resid)


CONV_COLS = 512
HALO = 8
HALO16 = 16
CONV_SUB_ROWS = 64
CONV_SUB_COLS = 256


def _conv_rows(l):
    return _pick(l, (1024, 512, 256, 128))


def _conv_fwd(proj, cw, cb, name):
    l = proj.shape[0]
    tl = _conv_rows(l)
    off = SSM_D_INNER // CONV_COLS

    def body(u_ref, halo_ref, w_ref, b_ref, pre_ref, act_ref, ext_ref):
        i = pl.program_id(1)
        ext_ref[0:HALO, :] = jnp.where(i > 0, halo_ref[...], 0.0)
        ext_ref[HALO:HALO + tl, :] = u_ref[...]
        for r0 in range(0, tl, CONV_SUB_ROWS):
            for c0 in range(0, CONV_COLS, CONV_SUB_COLS):
                cs = slice(c0, c0 + CONV_SUB_COLS)
                ext = ext_ref[r0:r0 + CONV_SUB_ROWS + HALO, cs]
                acc = b_ref[:, cs] + w_ref[SSM_CONV - 1:SSM_CONV, cs] * ext[HALO:]
                for k in range(SSM_CONV - 1):
                    acc = acc + w_ref[k:k + 1, cs] * pltpu.roll(ext, SSM_CONV - 1 - k, 0)[HALO:]
                pre_ref[r0:r0 + CONV_SUB_ROWS, cs] = acc.astype(pre_ref.dtype)
                act_ref[r0:r0 + CONV_SUB_ROWS, cs] = (acc * _sigmoid(acc)).astype(act_ref.dtype)

    hb = tl // HALO
    out = pl.BlockSpec((tl, CONV_COLS), lambda j, i: (i, j))
    return pl.pallas_call(
        body, grid=(SSM_CONV_DIM // CONV_COLS, l // tl),
        in_specs=[pl.BlockSpec((tl, CONV_COLS), lambda j, i: (i, off + j)),
                  pl.BlockSpec((HALO, CONV_COLS), lambda j, i: (jnp.maximum(i * hb - 1, 0), off + j)),
                  pl.BlockSpec((SSM_CONV, CONV_COLS), lambda j, i: (0, j)),
                  pl.BlockSpec((1, CONV_COLS), lambda j, i: (0, j))],
        out_specs=[out, out],
        out_shape=[jax.ShapeDtypeStruct((l, SSM_CONV_DIM), BF16)] * 2,
        scratch_shapes=[pltpu.VMEM((tl + HALO, CONV_COLS), F32)],
        compiler_params=_params("parallel", "arbitrary"), name=name,
    )(proj, proj, cw, cb.reshape(1, SSM_CONV_DIM))


def _conv_bwd(dact, pre, proj, cw, dproj, name):
    l, width = dact.shape
    tl = _conv_rows(l)
    nt = l // tl
    pre_off = 0
    u_off = SSM_D_INNER // CONV_COLS
    hb16 = tl // HALO16
    last_hb16 = l // HALO16 - 1

    def body(da_ref, da_h_ref, p_ref, p_h_ref, u_ref, w_ref, _, du_ref, dw_ref, db_ref, ext_ref):
        i = pl.program_id(1)

        @pl.when(i == 0)
        def _():
            dw_ref[...] = jnp.zeros_like(dw_ref)
            db_ref[...] = jnp.zeros_like(db_ref)

        def dpre_of(da, p):
            s = _sigmoid(p)
            return da * (s * (1.0 + p * (1.0 - s)))

        ext_ref[0:tl, :] = dpre_of(da_ref[...].astype(F32), p_ref[...].astype(F32))
        ext_ref[tl:tl + HALO, :] = jnp.where(
            i < nt - 1, dpre_of(da_h_ref[...].astype(F32)[:HALO], p_h_ref[...].astype(F32)[:HALO]), 0.0)
        sub = CONV_SUB_ROWS

        def fold(v):
            return jnp.sum(v.reshape(sub // 8, 8, CONV_SUB_COLS), axis=0)

        for c0 in range(0, CONV_COLS, CONV_SUB_COLS):
            cs = slice(c0, c0 + CONV_SUB_COLS)
            dws = [jnp.zeros((8, CONV_SUB_COLS), F32) for _ in range(SSM_CONV)]
            dbs = jnp.zeros((8, CONV_SUB_COLS), F32)
            for r0 in range(0, tl, sub):
                dext = ext_ref[r0:r0 + sub + HALO, cs]
                uv = u_ref[r0:r0 + sub, cs]
                for k in range(SSM_CONV):
                    j = SSM_CONV - 1 - k
                    ahead = dext[:sub] if j == 0 else pltpu.roll(dext, sub + HALO - j, 0)[:sub]
                    term = w_ref[k:k + 1, cs] * ahead
                    du = term if k == 0 else du + term
                    dws[k] = dws[k] + fold(ahead * uv)
                dbs = dbs + fold(dext[:sub])
                du_ref[r0:r0 + sub, cs] = du.astype(du_ref.dtype)
            for k in range(SSM_CONV):
                dw_ref[k:k + 1, cs] += jnp.sum(dws[k], axis=0, keepdims=True)
            db_ref[:, cs] += jnp.sum(dbs, axis=0, keepdims=True)

    return pl.pallas_call(
        body, grid=(width // CONV_COLS, nt),
        in_specs=[pl.BlockSpec((tl, CONV_COLS), lambda j, i: (i, j)),
                  pl.BlockSpec((HALO16, CONV_COLS), lambda j, i: (jnp.minimum((i + 1) * hb16, last_hb16), j)),
                  pl.BlockSpec((tl, CONV_COLS), lambda j, i: (i, pre_off + j)),
                  pl.BlockSpec((HALO16, CONV_COLS), lambda j, i: (jnp.minimum((i + 1) * hb16, last_hb16), pre_off + j)),
                  pl.BlockSpec((tl, CONV_COLS), lambda j, i: (i, u_off + j)),
                  pl.BlockSpec((SSM_CONV, CONV_COLS), lambda j, i: (0, pre_off + j)),
                  pl.BlockSpec(memory_space=pl.ANY)],
        out_specs=[pl.BlockSpec((tl, CONV_COLS), lambda j, i: (i, u_off + j)),
                   pl.BlockSpec((SSM_CONV, CONV_COLS), lambda j, i: (0, j)),
                   pl.BlockSpec((1, CONV_COLS), lambda j, i: (0, j))],
        out_shape=[jax.ShapeDtypeStruct(dproj.shape, dproj.dtype), jax.ShapeDtypeStruct((SSM_CONV, width), F32),
                   jax.ShapeDtypeStruct((1, width), F32)],
        scratch_shapes=[pltpu.VMEM((tl + HALO, CONV_COLS), F32)],
        input_output_aliases={6: 0}, compiler_params=_params("parallel", "arbitrary"), name=name,
    )(dact, dact, pre, pre, proj, cw, dproj)


DT_COL_BLOCK = (SSM_D_INNER + SSM_CONV_DIM) // SSM_DT_PAD


def _split3(v):
    hi = v.astype(BF16)
    rest = v - hi.astype(F32)
    mid = rest.astype(BF16)
    lo = (rest - mid.astype(F32)).astype(BF16)
    return hi, mid, lo


def _ssd_prep(proj, bias, a_log, name):
    l = proj.shape[0]
    nc = l // SSM_CHUNK
    head_dim_log2 = SSM_HEAD_DIM.bit_length() - 1

    def body(p_ref, b_ref, al_ref, dtb_ref, acsb_ref, dtr_ref, acsr_ref):
        v = p_ref[...] + b_ref[...]
        dt_hi, dt_mid, _ = _split3(jnp.maximum(v, 0.0) + jnp.log1p(jnp.exp(-jnp.abs(v))))
        dt = dt_hi.astype(F32) + dt_mid.astype(F32)
        ri = lax.broadcasted_iota(jnp.int32, (SSM_CHUNK, SSM_CHUNK), 0)
        cj = lax.broadcasted_iota(jnp.int32, (SSM_CHUNK, SSM_CHUNK), 1)
        tri = (ri >= cj).astype(BF16)
        acs_pieces = _split3(sum(jnp.dot(tri, piece, preferred_element_type=F32)
                                 for piece in _split3(dt * (-jnp.exp(al_ref[...])))))
        acs = sum(piece.astype(F32) for piece in acs_pieces)
        head_of_lane = lax.shift_right_logical(lax.broadcasted_iota(jnp.int32, (SSM_DT_PAD, SSM_D_INNER), 1), head_dim_log2)
        spread = (head_of_lane == lax.broadcasted_iota(jnp.int32, (SSM_DT_PAD, SSM_D_INNER), 0)).astype(BF16)
        dtb_ref[...] = sum(jnp.dot(piece, spread, preferred_element_type=F32) for piece in (dt_hi, dt_mid))
        acsb_ref[...] = sum(jnp.dot(piece, spread, preferred_element_type=F32) for piece in acs_pieces)
        dt_rows = dt.T
        acs_rows = acs.T
        for g in range(SSM_GROUPS):
            heads = slice(g * SSM_HPG, (g + 1) * SSM_HPG)
            dtr_ref[g] = dt_rows[heads, :]
            acsr_ref[g] = acs_rows[heads, :]

    rows = pl.BlockSpec((SSM_GROUPS, SSM_HPG, SSM_CHUNK), lambda c: (0, 0, c))
    dense = pl.BlockSpec((SSM_CHUNK, SSM_D_INNER), lambda c: (c, 0))
    return pl.pallas_call(
        body, grid=(nc,),
        in_specs=[pl.BlockSpec((SSM_CHUNK, SSM_DT_PAD), lambda c: (c, DT_COL_BLOCK)),
                  pl.BlockSpec((1, SSM_DT_PAD), lambda c: (0, 0)),
                  pl.BlockSpec((1, SSM_DT_PAD), lambda c: (0, 0))],
        out_specs=[dense, dense, rows, rows],
        out_shape=[jax.ShapeDtypeStruct((l, SSM_D_INNER), F32), jax.ShapeDtypeStruct((l, SSM_D_INNER), F32),
                   jax.ShapeDtypeStruct((SSM_GROUPS, SSM_HPG, l), F32),
                   jax.ShapeDtypeStruct((SSM_GROUPS, SSM_HPG, l), F32)],
        compiler_params=_params("parallel"), name=name,
    )(proj, bias, a_log)


def _dt_bwd(ddt, proj, bias, dproj, name):
    l = proj.shape[0]
    tl = _row_tile(l)

    def body(g_ref, p_ref, b_ref, _, o_ref, db_ref):
        @pl.when(pl.program_id(0) == 0)
        def _():
            db_ref[...] = jnp.zeros_like(db_ref)

        d = g_ref[...] * _sigmoid(p_ref[...] + b_ref[...])
        o_ref[...] = d.astype(o_ref.dtype)
        db_ref[...] += jnp.sum(d, axis=0, keepdims=True)

    return pl.pallas_call(
        body, grid=(l // tl,),
        in_specs=[pl.BlockSpec((tl, SSM_DT_PAD), lambda i: (i, 0)),
                  pl.BlockSpec((tl, SSM_DT_PAD), lambda i: (i, DT_COL_BLOCK)),
                  pl.BlockSpec((1, SSM_DT_PAD), lambda i: (0, 0)),
                  pl.BlockSpec(memory_space=pl.ANY)],
        out_specs=[pl.BlockSpec((tl, SSM_DT_PAD), lambda i: (i, DT_COL_BLOCK)),
                   pl.BlockSpec((1, SSM_DT_PAD), lambda i: (0, 0))],
        out_shape=[jax.ShapeDtypeStruct(dproj.shape, dproj.dtype), jax.ShapeDtypeStruct((1, SSM_DT_PAD), F32)],
        input_output_aliases={3: 0}, compiler_params=_params("arbitrary"), name=name,
    )(ddt, proj, bias, dproj)


GP = SSM_HPG * SSM_HEAD_DIM
HEAD_DIM_LOG2 = SSM_HEAD_DIM.bit_length() - 1
GPS = SSM_GROUPS
B_BLOCK0 = SSM_D_INNER // SSM_STATE
C_BLOCK0 = (SSM_D_INNER + SSM_BC_DIM) // SSM_STATE


def _chunk_iotas():
    ri = lax.broadcasted_iota(jnp.int32, (SSM_CHUNK, SSM_CHUNK), 0)
    cj = lax.broadcasted_iota(jnp.int32, (SSM_CHUNK, SSM_CHUNK), 1)
    return ri, cj


def _head_decay(acsb, acs_r, r, ri, cj):
    pair = acsb[:, (r // 2) * LANES:(r // 2 + 1) * LANES]
    mine_low = r % 2 == 0
    lane = lax.broadcasted_iota(jnp.int32, (1, LANES), 1)
    col = jnp.where((lane < SSM_HEAD_DIM) == mine_low, pair, pltpu.roll(pair, SSM_HEAD_DIM, 1))
    return jnp.exp(jnp.where(ri >= cj, col - acs_r[r:r + 1, :], NEG_INF))


def _head_masked_rows(v, dtype):
    head_of_lane = lax.shift_right_logical(lax.broadcasted_iota(jnp.int32, (1, GP), 1), HEAD_DIM_LOG2)
    narrow = v.astype(dtype)
    return jnp.concatenate([jnp.where(head_of_lane == r, narrow, jnp.zeros_like(narrow)) for r in range(SSM_HPG)], axis=0)


def _ssd_fwd(xbc, dtb, acsb, acs_r, d_lanes, proj, gate_w, name, ride=()):
    l = xbc.shape[0]
    nc = l // SSM_CHUNK
    assert GPS == SSM_GROUPS

    n_ride = len(ride)

    def body(*refs):
        x_ref, b_ref, c_ref, dtb_ref, acsb_ref, acsr_ref, d_ref, z_ref, gw_ref = refs[:9]
        ride_in = refs[9:9 + n_ride]
        y_ref, act_ref, hin_ref = refs[9 + n_ride:12 + n_ride]
        ride_out = refs[12 + n_ride:12 + 2 * n_ride]
        h_ref = refs[12 + 2 * n_ride]
        ride_sems = refs[13 + 2 * n_ride:]
        c = pl.program_id(0)
        if n_ride:
            @pl.when(c == 0)
            def _():
                _gather_between_chips(ride_in, ride_out, *ride_sems, wait=False)

            @pl.when(c == nc - 1)
            def _():
                _gather_between_chips(ride_in, ride_out, *ride_sems, wait=True)

        ri, cj = _chunk_iotas()
        for k in range(GPS):
            g = k
            cols = slice(k * GP, (k + 1) * GP)
            ncols = slice(k * SSM_STATE, (k + 1) * SSM_STATE)

            @pl.when(c == 0)
            def _():
                h_ref[g] = jnp.zeros((SSM_STATE, GP), F32)

            xv = x_ref[:, cols].astype(F32)
            bb = b_ref[:, ncols].astype(BF16)
            cb16 = c_ref[:, ncols].astype(BF16)
            acs_v = acsb_ref[:, cols]
            acs_r_v = acsr_ref[k]
            lastb = acs_v[SSM_CHUNK - 1:SSM_CHUNK, :]
            xd = xv * dtb_ref[:, cols]
            cb = lax.dot_general(cb16, bb, NT_DIMS, preferred_element_type=F32)
            hin = h_ref[g]
            hin_ref[0, k] = hin
            yoff = jnp.dot(cb16, hin.astype(BF16), preferred_element_type=F32)
            ms = [(cb * _head_decay(acs_v, acs_r_v, r, ri, cj)).astype(BF16) for r in range(SSM_HPG)]
            ydiag = jnp.dot(jnp.concatenate(ms, axis=1), _head_masked_rows(xd, BF16), preferred_element_type=F32)
            y_ref[:, cols] = ydiag + jnp.exp(acs_v) * yoff + d_ref[k] * xv
            h_ref[g] = hin * jnp.exp(lastb) + _bdot_tn(bb, xd * jnp.exp(lastb - acs_v))
        z = z_ref[...]
        yg = y_ref[...] * (z * _sigmoid(z))
        r = lax.rsqrt(jnp.mean(yg * yg, axis=-1, keepdims=True) + EPS)
        act_ref[...] = (yg * r * gw_ref[...]).astype(act_ref.dtype)

    lanes = pl.BlockSpec((SSM_CHUNK, SSM_D_INNER), lambda c: (c, 0))
    return pl.pallas_call(
        body, grid=(nc,),
        in_specs=[lanes,
                  pl.BlockSpec((SSM_CHUNK, SSM_BC_DIM), lambda c: (c, B_BLOCK0 // GPS)),
                  pl.BlockSpec((SSM_CHUNK, SSM_BC_DIM), lambda c: (c, C_BLOCK0 // GPS)),
                  lanes, lanes,
                  pl.BlockSpec((SSM_GROUPS, SSM_HPG, SSM_CHUNK), lambda c: (0, 0, c)),
                  pl.BlockSpec((SSM_GROUPS, 1, GP), lambda c: (0, 0, 0)),
                  lanes, pl.BlockSpec((1, SSM_D_INNER), lambda c: (0, 0))] + [ANY] * n_ride,
        out_specs=[lanes, lanes, pl.BlockSpec((1, SSM_GROUPS, SSM_STATE, GP), lambda c: (c, 0, 0, 0))] + [ANY] * n_ride,
        out_shape=[jax.ShapeDtypeStruct((l, SSM_D_INNER), F32), jax.ShapeDtypeStruct((l, SSM_D_INNER), BF16),
                   jax.ShapeDtypeStruct((nc, SSM_GROUPS, SSM_STATE, GP), F32)]
        + [jax.ShapeDtypeStruct((N_CHIPS,) + a.shape, a.dtype) for a in ride],
        scratch_shapes=[pltpu.VMEM((SSM_GROUPS, SSM_STATE, GP), F32)] + (_gather_sems(n_ride) if n_ride else []),
        compiler_params=_params("arbitrary"), name=name,
    )(xbc, xbc, xbc, dtb, acsb, acs_r, d_lanes, proj, gate_w.reshape(1, SSM_D_INNER), *ride)


def _ssd_bwd(xbc, dtb, acsb, dtr, acs_r, a_log, d_lanes, hin, dact, y, proj, gate_w, name):
    l = xbc.shape[0]
    nc = l // SSM_CHUNK

    def body(x_ref, b_ref, c_ref, dtb_ref, acsb_ref, dtr_ref, acsr_ref, alc_ref, d_ref, hin_ref,
             dact_ref, y_ref, z_ref, gw_ref,
             dxbc_ref, ddt_ref, dal_ref, dd_ref, dproj_ref, dgw_ref, dh_ref, dy_ref, acc_ref):
        c = pl.program_id(0)
        dx_ref = dxbc_ref.at[:, 0:SSM_D_INNER]
        db_ref = dxbc_ref.at[:, SSM_D_INNER:SSM_D_INNER + SSM_BC_DIM]
        dc_ref = dxbc_ref.at[:, SSM_D_INNER + SSM_BC_DIM:SSM_CONV_DIM]

        @pl.when(c == 0)
        def _():
            dal_ref[...] = jnp.zeros_like(dal_ref)
            dd_ref[...] = jnp.zeros_like(dd_ref)
            acc_ref[...] = jnp.zeros_like(acc_ref)

        z = z_ref[...]
        yv = y_ref[...]
        s = _sigmoid(z)
        sz = z * s
        yg = yv * sz
        r = lax.rsqrt(jnp.mean(yg * yg, axis=-1, keepdims=True) + EPS)
        nrm = yg * r
        gv = dact_ref[...]
        gw = gv * gw_ref[...]
        dyg = r * (gw - nrm * jnp.mean(gw * nrm, axis=-1, keepdims=True))
        dy_ref[...] = dyg * sz
        dproj_ref[...] = (dyg * yv * (s * (1.0 + z * (1.0 - s)))).astype(dproj_ref.dtype)
        acc_ref[...] += jnp.sum((gv * nrm).reshape(SSM_CHUNK // 8, 8, SSM_D_INNER), axis=0)

        @pl.when(c == nc - 1)
        def _():
            dgw_ref[...] = jnp.sum(acc_ref[...], axis=0, keepdims=True)

        for k in range(GPS):
            one_group(c, k, k, x_ref, b_ref, c_ref, dtb_ref, acsb_ref, dtr_ref, acsr_ref, alc_ref, d_ref,
                      hin_ref, dy_ref, dx_ref, db_ref, dc_ref, ddt_ref, dal_ref, dd_ref, dh_ref)

    def one_group(c, g, k, x_ref, b_ref, c_ref, dtb_ref, acsb_ref, dtr_ref, acsr_ref, alc_ref, d_ref, hin_ref, dy_ref,
                  dx_ref, db_ref, dc_ref, ddt_ref, dal_ref, dd_ref, dh_ref):
        cols = slice(k * GP, (k + 1) * GP)
        ncols = slice(k * SSM_STATE, (k + 1) * SSM_STATE)

        @pl.when(c == 0)
        def _():
            dh_ref[g] = jnp.zeros((SSM_STATE, GP), F32)

        xv = x_ref[:, cols].astype(F32)
        dyv = dy_ref[:, cols]
        bb = b_ref[:, ncols].astype(BF16)
        cb16 = c_ref[:, ncols].astype(BF16)
        dtb = dtb_ref[:, cols]
        acsb = acsb_ref[:, cols]
        dtr_v = dtr_ref[k]
        acs_r = acsr_ref[k]
        a_col = -jnp.exp(alc_ref[k])
        ri, cj = _chunk_iotas()
        head_of_lane = lax.shift_right_logical(lax.broadcasted_iota(jnp.int32, (SSM_HPG, GP), 1), HEAD_DIM_LOG2)
        ind_t = (head_of_lane == lax.broadcasted_iota(jnp.int32, (SSM_HPG, GP), 0)).astype(BF16)
        lastb = acsb[SSM_CHUNK - 1:SSM_CHUNK, :]
        ecb = jnp.exp(acsb)
        dteb = jnp.exp(lastb - acsb)
        xd = xv * dtb
        xw = xd * dteb
        cb = lax.dot_general(cb16, bb, NT_DIMS, preferred_element_type=F32)
        hin_v = hin_ref[0, k]
        dhn = dh_ref[g]
        h16 = hin_v.astype(BF16)
        dh16 = dhn.astype(BF16)
        ch = jnp.dot(cb16, h16, preferred_element_type=F32)
        bdh = jnp.dot(bb, dh16, preferred_element_type=F32)
        dym = _head_masked_rows(dyv, BF16)
        g_all = lax.dot_general(dym, xd.astype(BF16), NT_DIMS, preferred_element_type=F32)
        gl_sum = jnp.zeros((SSM_CHUNK, SSM_CHUNK), F32)
        ms, qs = [], []
        for r in range(SSM_HPG):
            decay = _head_decay(acsb, acs_r, r, ri, cj)
            gl = g_all[r * SSM_CHUNK:(r + 1) * SSM_CHUNK] * decay
            gl_sum = gl_sum + gl
            ms.append((cb * decay).astype(BF16))
            qs.append((gl * cb).astype(BF16))
        dxd = lax.dot_general(jnp.concatenate(ms, axis=0), dym, TN_DIMS, preferred_element_type=F32) + dteb * bdh
        cum = jnp.dot(jnp.concatenate(qs, axis=0), (ri < cj).astype(BF16), preferred_element_type=F32)
        sub4 = lax.broadcasted_iota(jnp.int32, (SSM_HPG, 1), 0)
        da = jnp.zeros((SSM_HPG, SSM_CHUNK), F32)
        for r in range(SSM_HPG):
            rect = jnp.sum(jnp.where(ri >= cj, cum[r * SSM_CHUNK:(r + 1) * SSM_CHUNK], 0.0), axis=0, keepdims=True)
            da = da + jnp.where(sub4 == r, rect, 0.0)
        z2 = xw * bdh
        sub8 = lax.broadcasted_iota(jnp.int32, (8, 1), 0)
        col_sums = (jnp.where(sub8 == 0, jnp.sum(z2, axis=0, keepdims=True), 0.0)
                    + jnp.where(sub8 == 1, jnp.sum(dhn * hin_v, axis=0, keepdims=True), 0.0)
                    + jnp.where(sub8 == 2, jnp.sum(dyv * xv, axis=0, keepdims=True), 0.0))
        wv = ecb * dyv
        summands = jnp.concatenate([wv * ch - z2, dxd * xv, col_sums], axis=0)
        sums = lax.dot_general(ind_t, summands.astype(BF16), NT_DIMS, preferred_element_type=F32)
        per_pos = sums[:, :2 * SSM_CHUNK]
        totals = sums[:, 2 * SSM_CHUNK:]
        e_last = totals[:, 0:1] + jnp.exp(acs_r[:, SSM_CHUNK - 1:SSM_CHUNK]) * totals[:, 1:2]
        da = (da + e_last + jnp.dot(per_pos[:, :SSM_CHUNK], (ri >= cj).astype(F32), preferred_element_type=F32,
                                    precision=lax.Precision.HIGHEST))
        ddt_ref[k] = a_col * da + per_pos[:, SSM_CHUNK:]
        dal_ref[g] += a_col * jnp.sum(da * dtr_v, axis=1, keepdims=True)
        dd_ref[g] += totals[:, 2:3]
        dx_ref[:, cols] = (dxd * dtb + d_ref[k] * dyv).astype(dx_ref.dtype)
        w16 = wv.astype(BF16)
        xw16 = xw.astype(BF16)
        gl16 = gl_sum.astype(BF16)
        dc_ref[:, ncols] = (jnp.dot(gl16, bb, preferred_element_type=F32)
                            + lax.dot_general(w16, h16, NT_DIMS, preferred_element_type=F32)).astype(dc_ref.dtype)
        db_ref[:, ncols] = (lax.dot_general(gl16, cb16, TN_DIMS, preferred_element_type=F32)
                            + lax.dot_general(xw16, dh16, NT_DIMS, preferred_element_type=F32)).astype(db_ref.dtype)
        dh_ref[g] = dhn * jnp.exp(lastb) + lax.dot_general(cb16, w16, TN_DIMS, preferred_element_type=F32)

    def rev(c):
        return nc - 1 - c

    small = pl.BlockSpec((SSM_GROUPS, SSM_HPG, 1), lambda c: (0, 0, 0))
    lanes = pl.BlockSpec((SSM_CHUNK, SSM_D_INNER), lambda c: (rev(c), 0))
    rows = pl.BlockSpec((SSM_GROUPS, SSM_HPG, SSM_CHUNK), lambda c: (0, 0, rev(c)))
    vec = pl.BlockSpec((1, SSM_D_INNER), lambda c: (0, 0))
    return pl.pallas_call(
        body, grid=(nc,),
        in_specs=[lanes,
                  pl.BlockSpec((SSM_CHUNK, SSM_BC_DIM), lambda c: (rev(c), B_BLOCK0 // GPS)),
                  pl.BlockSpec((SSM_CHUNK, SSM_BC_DIM), lambda c: (rev(c), C_BLOCK0 // GPS)),
                  lanes, lanes, rows, rows,
                  pl.BlockSpec((SSM_GROUPS, SSM_HPG, 1), lambda c: (0, 0, 0)),
                  pl.BlockSpec((SSM_GROUPS, 1, GP), lambda c: (0, 0, 0)),
                  pl.BlockSpec((1, SSM_GROUPS, SSM_STATE, GP), lambda c: (rev(c), 0, 0, 0)),
                  lanes, lanes, lanes, vec],
        out_specs=[pl.BlockSpec((SSM_CHUNK, SSM_CONV_DIM), lambda c: (rev(c), 0)),
                   rows, small, small, lanes, vec],
        out_shape=[jax.ShapeDtypeStruct((l, SSM_CONV_DIM), BF16), jax.ShapeDtypeStruct((SSM_GROUPS, SSM_HPG, l), F32),
                   jax.ShapeDtypeStruct((SSM_GROUPS, SSM_HPG, 1), F32),
                   jax.ShapeDtypeStruct((SSM_GROUPS, SSM_HPG, 1), F32),
                   jax.ShapeDtypeStruct((l, SSM_IN_PAD), BF16), jax.ShapeDtypeStruct((1, SSM_D_INNER), F32)],
        scratch_shapes=[pltpu.VMEM((SSM_GROUPS, SSM_STATE, GP), F32), pltpu.VMEM((SSM_CHUNK, SSM_D_INNER), F32),
                        pltpu.VMEM((8, SSM_D_INNER), F32)],
        compiler_params=_params("arbitrary"), name=name,
    )(xbc, xbc, xbc, dtb, acsb, dtr, acs_r, a_log.reshape(SSM_GROUPS, SSM_HPG, 1), d_lanes, hin, dact, y, proj,
      gate_w.reshape(1, SSM_D_INNER))


LANES = 128
ROPE_Q_CHUNKS = ATT_WIDTH // LANES
ROPE_K_CHUNKS = ATT_KV_WIDTH // LANES


def _rope_tables(positions):
    inv = ROPE_THETA ** (-jnp.arange(0, ROPE_DIM, 2, dtype=F32) / ROPE_DIM)
    ang = positions.astype(F32)[:, None] * inv
    cos, sin = jnp.cos(ang), jnp.sin(ang)
    l = positions.shape[0]
    rest = ATT_HEAD_DIM - ROPE_DIM
    ones, zeros = jnp.ones((l, rest), F32), jnp.zeros((l, rest), F32)
    z8 = jnp.zeros((l, ROPE_HALF), F32)
    cos_f = jnp.concatenate([cos, cos, ones], axis=1)
    sin_a = jnp.concatenate([-sin, z8, zeros], axis=1)
    sin_b = jnp.concatenate([z8, sin, zeros], axis=1)
    reps = LANES // ATT_HEAD_DIM
    return tuple(jnp.tile(t, (1, reps)) for t in (cos_f, sin_a, sin_b))


ATT_QKV4 = 3 * ATT_WIDTH


def _both_halves(chunk):
    lane = lax.broadcasted_iota(jnp.int32, (1, LANES), 1)
    swapped = pltpu.roll(chunk, ATT_HEAD_DIM, 1)
    return jnp.where(lane < ATT_HEAD_DIM, chunk, swapped), jnp.where(lane < ATT_HEAD_DIM, swapped, chunk)


def _rope_fwd(proj, tables, name):
    l = proj.shape[0]
    tl = _pick(l, (256, 128))

    def body(p_ref, c_ref, sa_ref, sb_ref, o_ref):
        cos_f, sin_a, sin_b = c_ref[...], sa_ref[...], sb_ref[...]

        def rope(t):
            return t * cos_f + pltpu.roll(t, LANES - ROPE_HALF, 1) * sin_a + pltpu.roll(t, ROPE_HALF, 1) * sin_b

        for k in range(ROPE_Q_CHUNKS):
            sl = slice(k * LANES, (k + 1) * LANES)
            o_ref[:, sl] = (rope(p_ref[:, sl]) * Q_SCALE).astype(o_ref.dtype)
        for part in range(2):
            for k in range(ROPE_K_CHUNKS):
                src = ATT_WIDTH + part * ATT_KV_WIDTH + k * LANES
                t = p_ref[:, src:src + LANES]
                if part == 0:
                    t = rope(t)
                for head, dup in enumerate(_both_halves(t.astype(o_ref.dtype))):
                    dst = (1 + part) * ATT_WIDTH + (2 * k + head) * ATT_GQA * ATT_HEAD_DIM
                    o_ref[:, dst:dst + LANES] = dup
                    o_ref[:, dst + LANES:dst + 2 * LANES] = dup

    tab = pl.BlockSpec((tl, LANES), lambda i: (i, 0))
    return pl.pallas_call(
        body, grid=(l // tl,), in_specs=[pl.BlockSpec((tl, ATT_IN_DIM), lambda i: (i, 0)), tab, tab, tab],
        out_specs=pl.BlockSpec((tl, ATT_QKV4), lambda i: (i, 0)),
        out_shape=jax.ShapeDtypeStruct((l, ATT_QKV4), BF16), compiler_params=_params("parallel"), name=name,
    )(proj, *tables)


def _rope_bwd(dq, dk4, dv4, dgate, tables, name):
    l = dq.shape[0]
    tl = _pick(l, (256, 128))

    def body(dq_ref, dk_ref, dv_ref, dg_ref, c_ref, sa_ref, sb_ref, o_ref):
        cos_f, sin_a, sin_b = c_ref[...], sa_ref[...], sb_ref[...]
        lane = lax.broadcasted_iota(jnp.int32, (1, LANES), 1)

        def unrope(t):
            return t * cos_f + pltpu.roll(t * sin_a, ROPE_HALF, 1) + pltpu.roll(t * sin_b, LANES - ROPE_HALF, 1)

        def head_total(ref, kvh):
            base = kvh * ATT_GQA * ATT_HEAD_DIM
            s = ref[:, base:base + LANES] + ref[:, base + LANES:base + 2 * LANES]
            return s + pltpu.roll(s, ATT_HEAD_DIM, 1)

        for k in range(ROPE_Q_CHUNKS):
            sl = slice(k * LANES, (k + 1) * LANES)
            o_ref[:, sl] = unrope(dq_ref[:, sl] * Q_SCALE).astype(o_ref.dtype)
        for k in range(ROPE_K_CHUNKS):
            dk = jnp.where(lane < ATT_HEAD_DIM, head_total(dk_ref, 2 * k), head_total(dk_ref, 2 * k + 1))
            dv = jnp.where(lane < ATT_HEAD_DIM, head_total(dv_ref, 2 * k), head_total(dv_ref, 2 * k + 1))
            o_ref[:, ATT_WIDTH + k * LANES:ATT_WIDTH + (k + 1) * LANES] = unrope(dk).astype(o_ref.dtype)
            at = ATT_WIDTH + ATT_KV_WIDTH + k * LANES
            o_ref[:, at:at + LANES] = dv.astype(o_ref.dtype)
        o_ref[:, ATT_QKV:ATT_IN_DIM] = dg_ref[...].astype(o_ref.dtype)

    tab = pl.BlockSpec((tl, LANES), lambda i: (i, 0))
    wide = pl.BlockSpec((tl, ATT_WIDTH), lambda i: (i, 0))
    return pl.pallas_call(
        body, grid=(l // tl,), in_specs=[wide, wide, wide, wide, tab, tab, tab],
        out_specs=pl.BlockSpec((tl, ATT_IN_DIM), lambda i: (i, 0)),
        out_shape=jax.ShapeDtypeStruct((l, ATT_IN_DIM), BF16), compiler_params=_params("parallel"), name=name,
    )(dq, dk4, dv4, dgate, *tables)


GATE_HALF = ATT_WIDTH // 2
GATE_COL_BLOCK = ATT_QKV // GATE_HALF


ATT_STACK = ATT_GQA * ATT_BLOCK
BLOCK_LOG2 = ATT_BLOCK.bit_length() - 1


def _stack_masks(n):
    ri = lax.broadcasted_iota(jnp.int32, (ATT_STACK, ATT_BLOCK), 0) & (ATT_BLOCK - 1)
    cj = lax.broadcasted_iota(jnp.int32, (ATT_STACK, ATT_BLOCK), 1)
    return (cj > ri) & (n > 0), cj <= ri


def _stack_sinks(sink_ref, kvh):
    blk = lax.shift_right_logical(lax.broadcasted_iota(jnp.int32, (ATT_STACK, 1), 0), BLOCK_LOG2)
    col = jnp.zeros((ATT_STACK, 1), F32)
    for r in range(ATT_GQA):
        col = jnp.where(blk == r, sink_ref[kvh * ATT_GQA + r], col)
    return col


def _stack_fold(stack):
    head_of_lane = lax.shift_right_logical(lax.broadcasted_iota(jnp.int32, (1, GP), 1), HEAD_DIM_LOG2)
    out = jnp.zeros((ATT_BLOCK, GP), F32)
    for r in range(ATT_GQA):
        out = jnp.where(head_of_lane == r, stack[r * ATT_BLOCK:(r + 1) * ATT_BLOCK], out)
    return out


def _attn_fwd(qkv, proj, sinks, name):
    l = qkv.shape[0]
    nb = l // ATT_BLOCK

    def body(sink_ref, q_ref, kp_ref, kc_ref, vp_ref, vc_ref, g0_ref, g1_ref, og_ref, o_ref, lse_ref):
        n = pl.program_id(0)
        mask_p, mask_c = _stack_masks(n)
        ones = jnp.ones((ATT_BLOCK, LANES), BF16)
        for kvh in range(ATT_KV_HEADS):
            cols = slice(kvh * GP, (kvh + 1) * GP)
            q_stack = _head_masked_rows(q_ref[:, cols], BF16)
            sp = jnp.where(mask_p, lax.dot_general(q_stack, kp_ref[:, cols], NT_DIMS, preferred_element_type=F32), NEG_INF)
            sc = jnp.where(mask_c, lax.dot_general(q_stack, kc_ref[:, cols], NT_DIMS, preferred_element_type=F32), NEG_INF)
            sink = _stack_sinks(sink_ref, kvh)
            m = jnp.maximum(jnp.max(jnp.maximum(sp, sc), axis=1, keepdims=True), sink)
            pp = jnp.exp(sp - m).astype(BF16)
            pc = jnp.exp(sc - m).astype(BF16)
            acc = (jnp.dot(pp, jnp.concatenate([vp_ref[:, cols], ones], axis=1), preferred_element_type=F32)
                   + jnp.dot(pc, jnp.concatenate([vc_ref[:, cols], ones], axis=1), preferred_element_type=F32))
            den = acc[:, GP:] + jnp.exp(sink - m)
            inv = 1.0 / den
            o_ref[:, cols] = _stack_fold(acc[:, :GP] * jnp.concatenate([inv, inv], axis=1))
            lse = m + jnp.log(den)
            lse_ref[:, cols] = _stack_fold(jnp.concatenate([lse, lse], axis=1))
        for half, g_ref in enumerate((g0_ref, g1_ref)):
            sl = slice(half * GATE_HALF, (half + 1) * GATE_HALF)
            gate = g_ref[...]
            og_ref[:, sl] = (o_ref[:, sl] * (gate * _sigmoid(gate))).astype(og_ref.dtype)

    def prev(n):
        return jnp.maximum(n - 1, 0)

    wide = pl.BlockSpec((ATT_BLOCK, ATT_WIDTH), lambda n: (n, 0))
    return pl.pallas_call(
        body, grid=(nb,),
        in_specs=[pl.BlockSpec(memory_space=pltpu.SMEM), wide,
                  pl.BlockSpec((ATT_BLOCK, ATT_WIDTH), lambda n: (prev(n), 1)),
                  pl.BlockSpec((ATT_BLOCK, ATT_WIDTH), lambda n: (n, 1)),
                  pl.BlockSpec((ATT_BLOCK, ATT_WIDTH), lambda n: (prev(n), 2)),
                  pl.BlockSpec((ATT_BLOCK, ATT_WIDTH), lambda n: (n, 2)),
                  pl.BlockSpec((ATT_BLOCK, GATE_HALF), lambda n: (n, GATE_COL_BLOCK)),
                  pl.BlockSpec((ATT_BLOCK, GATE_HALF), lambda n: (n, GATE_COL_BLOCK + 1))],
        out_specs=[wide, wide, wide],
        out_shape=[jax.ShapeDtypeStruct((l, ATT_WIDTH), BF16), jax.ShapeDtypeStruct((l, ATT_WIDTH), F32),
                   jax.ShapeDtypeStruct((l, ATT_WIDTH), F32)],
        compiler_params=_params("parallel"), name=name,
    )(sinks, qkv, qkv, qkv, qkv, qkv, proj, proj)


def _attn_bwd(qkv, proj, sinks, o, lse, dog, name, ride=()):
    l = qkv.shape[0]
    nb = l // ATT_BLOCK
    n_ride = len(ride)

    def body(*refs):
        sink_ref, q_ref, kp_ref, kc_ref, vp_ref, vc_ref, g0_ref, g1_ref, o_ref, lse_ref, dog_ref = refs[:11]
        ride_in = refs[11:11 + n_ride]
        dq_ref, dk_ref, dv_ref, dg_ref, ds_ref = refs[11 + n_ride:16 + n_ride]
        ride_out = refs[16 + n_ride:16 + 2 * n_ride]
        ck_ref, cv_ref, do_ref = refs[16 + 2 * n_ride:19 + 2 * n_ride]
        ride_sems = refs[19 + 2 * n_ride:]
        n = pl.program_id(0)

        @pl.when(n == 0)
        def _():
            ds_ref[...] = jnp.zeros_like(ds_ref)
            ck_ref[...] = jnp.zeros_like(ck_ref)
            cv_ref[...] = jnp.zeros_like(cv_ref)
            if n_ride:
                _scatter_between_chips(ride_in, ride_out, *ride_sems, wait=False)

        @pl.when(n == nb)
        def _():
            dk_ref[...] = ck_ref[...]
            dv_ref[...] = cv_ref[...]
            if n_ride:
                _scatter_between_chips(ride_in, ride_out, *ride_sems, wait=True)

        @pl.when(n < nb)
        def _():
            mask_p, mask_c = _stack_masks(n)
            lane = lax.broadcasted_iota(jnp.int32, (1, ATT_Q_HEADS), 1)
            for half, g_ref in enumerate((g0_ref, g1_ref)):
                sl = slice(half * GATE_HALF, (half + 1) * GATE_HALF)
                gate = g_ref[...]
                s = _sigmoid(gate)
                dogv = dog_ref[:, sl]
                do_ref[:, sl] = dogv * (gate * s)
                dg_ref[:, sl] = dogv * o_ref[:, sl] * (s * (1.0 + gate * (1.0 - s)))
            ds_acc = jnp.zeros((1, ATT_Q_HEADS), F32)
            for kvh in range(ATT_KV_HEADS):
                cols = slice(kvh * GP, (kvh + 1) * GP)
                kp, kc, vp, vc = kp_ref[:, cols], kc_ref[:, cols], vp_ref[:, cols], vc_ref[:, cols]
                q_stack = _head_masked_rows(q_ref[:, cols], BF16)
                do_g = do_ref[:, cols]
                do_stack = _head_masked_rows(do_g, BF16)
                lse_g = lse_ref[:, cols]
                lse_stack = jnp.concatenate(
                    [_both_halves(lse_g[:, (r // 2) * LANES:(r // 2 + 1) * LANES])[r % 2] for r in range(ATT_GQA)], axis=0)
                pp = jnp.exp(jnp.where(
                    mask_p, lax.dot_general(q_stack, kp, NT_DIMS, preferred_element_type=F32) - lse_stack, NEG_INF))
                pc = jnp.exp(jnp.where(
                    mask_c, lax.dot_general(q_stack, kc, NT_DIMS, preferred_element_type=F32) - lse_stack, NEG_INF))
                dpp = lax.dot_general(do_stack, vp, NT_DIMS, preferred_element_type=F32)
                dpc = lax.dot_general(do_stack, vc, NT_DIMS, preferred_element_type=F32)
                delta = jnp.sum(pp * dpp + pc * dpc, axis=1, keepdims=True)
                dsp = (pp * (dpp - delta)).astype(BF16)
                dsc = (pc * (dpc - delta)).astype(BF16)
                dq_ref[:, cols] = _stack_fold(jnp.dot(dsp, kp, preferred_element_type=F32)
                                              + jnp.dot(dsc, kc, preferred_element_type=F32))
                dk_ref[:, cols] = ck_ref[:, cols] + lax.dot_general(dsp, q_stack, TN_DIMS, preferred_element_type=F32)
                dv_ref[:, cols] = cv_ref[:, cols] + lax.dot_general(pp.astype(BF16), do_stack, TN_DIMS,
                                                                    preferred_element_type=F32)
                ck_ref[:, cols] = lax.dot_general(dsc, q_stack, TN_DIMS, preferred_element_type=F32)
                cv_ref[:, cols] = lax.dot_general(pc.astype(BF16), do_stack, TN_DIMS, preferred_element_type=F32)
                t = jnp.exp(_stack_sinks(sink_ref, kvh) - lse_stack) * delta
                for r in range(ATT_GQA):
                    tot = jnp.sum(t[r * ATT_BLOCK:(r + 1) * ATT_BLOCK], axis=0, keepdims=True)
                    ds_acc = ds_acc - jnp.where(lane == kvh * ATT_GQA + r, tot[:, :ATT_Q_HEADS], 0.0)
            ds_ref[...] += ds_acc

    def cur(n):
        return jnp.minimum(n, nb - 1)

    def prev(n):
        return jnp.maximum(n - 1, 0)

    wide = pl.BlockSpec((ATT_BLOCK, ATT_WIDTH), lambda n: (cur(n), 0))
    late = pl.BlockSpec((ATT_BLOCK, ATT_WIDTH), lambda n: (prev(n), 0))
    return pl.pallas_call(
        body, grid=(nb + 1,),
        in_specs=[pl.BlockSpec(memory_space=pltpu.SMEM), wide,
                  pl.BlockSpec((ATT_BLOCK, ATT_WIDTH), lambda n: (prev(cur(n)), 1)),
                  pl.BlockSpec((ATT_BLOCK, ATT_WIDTH), lambda n: (cur(n), 1)),
                  pl.BlockSpec((ATT_BLOCK, ATT_WIDTH), lambda n: (prev(cur(n)), 2)),
                  pl.BlockSpec((ATT_BLOCK, ATT_WIDTH), lambda n: (cur(n), 2)),
                  pl.BlockSpec((ATT_BLOCK, GATE_HALF), lambda n: (cur(n), GATE_COL_BLOCK)),
                  pl.BlockSpec((ATT_BLOCK, GATE_HALF), lambda n: (cur(n), GATE_COL_BLOCK + 1)),
                  wide, wide, wide] + [ANY] * n_ride,
        out_specs=[wide, late, late, wide, pl.BlockSpec((1, ATT_Q_HEADS), lambda n: (0, 0))] + [ANY] * n_ride,
        out_shape=[jax.ShapeDtypeStruct((l, ATT_WIDTH), F32), jax.ShapeDtypeStruct((l, ATT_WIDTH), F32),
                   jax.ShapeDtypeStruct((l, ATT_WIDTH), F32), jax.ShapeDtypeStruct((l, ATT_WIDTH), F32),
                   jax.ShapeDtypeStruct((1, ATT_Q_HEADS), F32)] + _scatter_shapes(ride),
        scratch_shapes=[pltpu.VMEM((ATT_BLOCK, ATT_WIDTH), F32), pltpu.VMEM((ATT_BLOCK, ATT_WIDTH), F32),
                        pltpu.VMEM((ATT_BLOCK, ATT_WIDTH), F32)] + (_gather_sems(n_ride) if n_ride else []),
        compiler_params=_params("arbitrary"), name=name,
    )(sinks, qkv, qkv, qkv, qkv, qkv, proj, proj, o, lse, dog, *ride)


def _local_step(x, positions, pre_norm, post_norm, conv_b, dt_bias, a_log, d_skip, gate_norm, sinks, target,
                first_in, in_proj_with_first_pair, scan_with_second_pair, attn_bwd_with_second_pair_grads,
                in_dx_with_first_pair_grads):
    tables = _rope_tables(positions)
    dt_bias_pad = jnp.pad(dt_bias, ((0, 0), (0, SSM_DT_PAD - SSM_HEADS)))
    d_lanes = jnp.repeat(d_skip, SSM_HEAD_DIM, axis=1).reshape(-1, SSM_GROUPS, 1, GP)
    a_log_pad = jnp.pad(a_log, ((0, 0), (0, SSM_DT_PAD - SSM_HEADS)))
    pairs = [first_in, None]
    saved = []
    cur = x
    h = _rmsnorm_fwd(cur, pre_norm[0], "prenorm_fwd_0")
    for i in range(DEPTH):
        j = i // 2
        if i % 2 == 0:
            in_proj = functools.partial(_matmul, h, pairs[j]["ssm_w_in"], "nn", F32, f"ssm_in_{i}")
            if i == 0:
                proj, rest = in_proj_with_first_pair(in_proj)
                pairs[0] = {**first_in, **rest}
            else:
                proj = in_proj()
            pre, xbc = _conv_fwd(proj, pairs[j]["ssm_conv_w"], conv_b[j], f"conv_fwd_{i}")
            dtb, acsb, dtr, acs_r = _ssd_prep(proj, dt_bias_pad[j:j + 1], a_log_pad[j:j + 1], f"ssd_prep_{i}")
            scan = functools.partial(_ssd_fwd, xbc, dtb, acsb, acs_r, d_lanes[j], proj, gate_norm[j], f"ssd_fwd_{i}")
            if i == 0:
                y, act, hin, pairs[1] = scan_with_second_pair(scan)
            else:
                y, act, hin = scan()
            w_ssm_in = [p["ssm_w_in"] for p in pairs]
            w_ssm_out = [p["ssm_w_out"] for p in pairs]
            w_att_in = [p["att_w_in"] for p in pairs]
            w_att_out = [p["att_w_out"] for p in pairs]
            conv_w = [p["ssm_conv_w"] for p in pairs]
            ymix = _matmul(act, w_ssm_out[j], "nn", F32, f"ssm_out_{i}")
            saved.append(dict(x=cur, h=h, proj=proj, pre=pre, xbc=xbc, dtb=dtb, acsb=acsb, dtr=dtr, acs_r=acs_r, y=y,
                              hin=hin, act=act, ymix=ymix))
        else:
            proj = _matmul(h, w_att_in[j], "nn", F32, f"att_in_{i}")
            qkv = _rope_fwd(proj, tables, f"rope_fwd_{i}")
            act, o, lse = _attn_fwd(qkv, proj, sinks[j], f"attn_fwd_{i}")
            ymix = _matmul(act, w_att_out[j], "nn", F32, f"att_out_{i}")
            saved.append(dict(x=cur, h=h, proj=proj, qkv=qkv, o=o, lse=lse, act=act, ymix=ymix))
        if i + 1 < DEPTH:
            cur, h = _post_fwd(cur, ymix, post_norm[i], pre_norm[i + 1], f"post_fwd_{i}")

    gr = {k: [None] * 2 for k in ("ssm_w_in", "ssm_conv_w", "ssm_conv_b", "ssm_dt_bias", "ssm_a_log", "ssm_d",
                                  "ssm_gate_norm", "ssm_w_out", "att_w_in", "att_sinks", "att_w_out")}
    gr["pre_norm"] = [None] * DEPTH
    gr["post_norm"] = [None] * DEPTH
    last = DEPTH - 1
    g, dymix, loss_lanes, gr["post_norm"][last] = _post_loss(cur, ymix, post_norm[last], target, "post_loss")
    for i in reversed(range(DEPTH)):
        j = i // 2
        s = saved[i]
        if i % 2 == 0:
            dact = _matmul(dymix, w_ssm_out[j], "nt", F32, f"ssm_out_dx_{i}")
            gr["ssm_w_out"][j] = _matmul(s["act"], dymix, "tn", F32, f"ssm_out_dw_{i}")
            dxbc, ddt8, dal, dd, dproj, gr["ssm_gate_norm"][j] = _ssd_bwd(
                s["xbc"], s["dtb"], s["acsb"], s["dtr"], s["acs_r"], a_log[j], d_lanes[j], s["hin"], dact, s["y"],
                s["proj"], gate_norm[j], f"ssd_bwd_{i}")
            gr["ssm_a_log"][j] = dal.reshape(SSM_HEADS)
            gr["ssm_d"][j] = dd.reshape(SSM_HEADS)
            l = x.shape[0]
            ddt = jnp.pad(jnp.transpose(ddt8, (2, 0, 1)).reshape(l, SSM_HEADS), ((0, 0), (0, SSM_DT_PAD - SSM_HEADS)))
            dproj, dbias = _dt_bwd(ddt, s["proj"], dt_bias_pad[j:j + 1], dproj, f"dt_bwd_{i}")
            gr["ssm_dt_bias"][j] = dbias[0, :SSM_HEADS]
            dproj, gr["ssm_conv_w"][j], dcb = _conv_bwd(dxbc, s["pre"], s["proj"], conv_w[j], dproj, f"conv_bwd_{i}")
            gr["ssm_conv_b"][j] = dcb[0]
            w_in, key = w_ssm_in[j], "ssm_w_in"
        else:
            dog = _matmul(dymix, w_att_out[j], "nt", F32, f"att_out_dx_{i}")
            gr["att_w_out"][j] = _matmul(s["act"], dymix, "tn", F32, f"att_out_dw_{i}")
            attn_bwd = functools.partial(_attn_bwd, s["qkv"], s["proj"], sinks[j], s["o"], s["lse"], dog, f"attn_bwd_{i}")
            if i == 1:
                (dq, dk, dv, dgate, dsk), second_pair_reduced = attn_bwd_with_second_pair_grads(
                    attn_bwd, {k: gr[k][1] for k in BIG})
            else:
                dq, dk, dv, dgate, dsk = attn_bwd()
            gr["att_sinks"][j] = dsk[0]
            dproj = _rope_bwd(dq, dk, dv, dgate, tables, f"rope_bwd_{i}")
            w_in, key = w_att_in[j], "att_w_in"
        gr[key][j] = _matmul(s["h"], dproj, "tn", F32, f"in_dw_{i}")
        in_dx = functools.partial(_matmul, dproj, w_in, "nt", F32, f"in_dx_{i}")
        if i == 0:
            dh, first_pair_reduced = in_dx_with_first_pair_grads(in_dx, {k: gr[k][0] for k in BIG})
        else:
            dh = in_dx()
        if i > 0:
            g, dymix, gr["pre_norm"][i], gr["post_norm"][i - 1] = _norm_bwd_chain(
                dh, s["x"], pre_norm[i], g, saved[i - 1]["ymix"], post_norm[i - 1], f"norm_bwd_{i}")
        else:
            g, gr["pre_norm"][i] = _rmsnorm_bwd(dh, s["x"], pre_norm[i], g, f"prenorm_bwd_{i}")
    grads = {k: jnp.stack([v.reshape(v.shape[-1]) if k in ("pre_norm", "post_norm", "ssm_gate_norm") else v for v in vs])
             for k, vs in gr.items() if k not in BIG}
    return loss_lanes, g, grads, first_pair_reduced, second_pair_reduced


N_CHIPS = 4
N_DEV = 8
MESH = pl.DeviceIdType.MESH
ANY = pl.BlockSpec(memory_space=pl.ANY)


def _place():
    x, y, c = lax.axis_index("x"), lax.axis_index("y"), lax.axis_index("c")
    return x, y, c, 2 * x + y


def _gather_sems(n):
    return [pltpu.SemaphoreType.DMA((n, N_CHIPS)), pltpu.SemaphoreType.DMA((n, N_CHIPS)), pltpu.SemaphoreType.DMA((n,))]


def _gather_between_chips(ins, outs, send_sems, recv_sems, local_sems, wait):
    n = len(ins)
    _, _, c, s = _place()
    local = [pltpu.make_async_copy(ins[w], outs[w].at[s], local_sems.at[w]) for w in range(n)]

    def remote(w, t):
        return pltpu.make_async_remote_copy(
            src_ref=ins[w].at[c], dst_ref=outs[w].at[s, c], send_sem=send_sems.at[w, t],
            recv_sem=recv_sems.at[w, s], device_id=(t // 2, t % 2, c), device_id_type=MESH)

    def arrival(w, t):
        return pltpu.make_async_remote_copy(
            src_ref=ins[w].at[c], dst_ref=outs[w].at[t, c], send_sem=send_sems.at[w, t],
            recv_sem=recv_sems.at[w, t], device_id=(t // 2, t % 2, c), device_id_type=MESH)

    if not wait:
        for cp in local:
            cp.start()
    for t in range(N_CHIPS):
        @pl.when(s != t)
        def _():
            for w in range(n):
                if wait:
                    remote(w, t).wait_send()
                    arrival(w, t).wait_recv()
                else:
                    remote(w, t).start()
    if wait:
        for cp in local:
            cp.wait()


def _pair_handoff(bufs, name):
    n = len(bufs)

    def body(*refs):
        outs = refs[n:2 * n]
        send_sems, recv_sems = refs[2 * n:]
        x, y, c, s = _place()

        def handed_on(w, t):
            return pltpu.make_async_remote_copy(
                src_ref=outs[w].at[t, c], dst_ref=outs[w].at[t, c], send_sem=send_sems.at[w, t],
                recv_sem=recv_sems.at[w, t], device_id=(x, y, 1 - c), device_id_type=MESH)

        def handed_in(w, t):
            return pltpu.make_async_remote_copy(
                src_ref=outs[w].at[t, 1 - c], dst_ref=outs[w].at[t, 1 - c], send_sem=send_sems.at[w, t],
                recv_sem=recv_sems.at[w, t], device_id=(x, y, 1 - c), device_id_type=MESH)

        for t in range(N_CHIPS):
            @pl.when(s != t)
            def _():
                for w in range(n):
                    handed_on(w, t).start()
        for t in range(N_CHIPS):
            @pl.when(s != t)
            def _():
                for w in range(n):
                    handed_on(w, t).wait_send()
                    handed_in(w, t).wait_recv()

    return pl.pallas_call(
        body, in_specs=[ANY] * n, out_specs=[ANY] * n,
        out_shape=[jax.ShapeDtypeStruct(a.shape, a.dtype) for a in bufs],
        scratch_shapes=[pltpu.SemaphoreType.DMA((n, N_CHIPS)), pltpu.SemaphoreType.DMA((n, N_CHIPS))],
        input_output_aliases={w: w for w in range(n)}, name=name,
    )(*bufs)


def _chip_gather(shards, name):
    n = len(shards)

    def body(*refs):
        ins, outs = refs[:n], refs[n:2 * n]
        _gather_between_chips(ins, outs, *refs[2 * n:], wait=False)
        _gather_between_chips(ins, outs, *refs[2 * n:], wait=True)

    bufs = pl.pallas_call(
        body, in_specs=[ANY] * n, out_specs=[ANY] * n,
        out_shape=[jax.ShapeDtypeStruct((N_CHIPS,) + a.shape, a.dtype) for a in shards],
        scratch_shapes=_gather_sems(n), name=name,
    )(*shards)
    return _pair_handoff(bufs, name + "_handoff")


def _pair_swap(parts, name):
    n = len(parts)

    def body(*refs):
        ins, outs = refs[:n], refs[n:2 * n]
        send_sems, recv_sems = refs[2 * n:]
        x, y, c, _ = _place()
        cps = [pltpu.make_async_remote_copy(
            src_ref=ins[w].at[1 - c], dst_ref=outs[w], send_sem=send_sems.at[w], recv_sem=recv_sems.at[w],
            device_id=(x, y, 1 - c), device_id_type=MESH) for w in range(n)]
        for cp in cps:
            cp.start()
        for cp in cps:
            cp.wait()

    return pl.pallas_call(
        body, in_specs=[ANY] * n, out_specs=[ANY] * n,
        out_shape=[jax.ShapeDtypeStruct(a.shape[1:], a.dtype) for a in parts],
        scratch_shapes=[pltpu.SemaphoreType.DMA((n,)), pltpu.SemaphoreType.DMA((n,))],
        name=name,
    )(*parts)


def _scatter_between_chips(ins, outs, send_sems, recv_sems, local_sems, wait):
    n = len(ins)
    _, _, c, s = _place()

    def block(w, t):
        rows = ins[w].shape[0] // N_CHIPS
        return ins[w].at[pl.ds(t * rows, rows)]

    local = [pltpu.make_async_copy(block(w, s), outs[w].at[s], local_sems.at[w]) for w in range(n)]

    def remote(w, t):
        return pltpu.make_async_remote_copy(
            src_ref=block(w, t), dst_ref=outs[w].at[s], send_sem=send_sems.at[w, t], recv_sem=recv_sems.at[w, s],
            device_id=(t // 2, t % 2, c), device_id_type=MESH)

    def arrival(w, t):
        return pltpu.make_async_remote_copy(
            src_ref=block(w, t), dst_ref=outs[w].at[t], send_sem=send_sems.at[w, t], recv_sem=recv_sems.at[w, t],
            device_id=(t // 2, t % 2, c), device_id_type=MESH)

    if not wait:
        for cp in local:
            cp.start()
    for t in range(N_CHIPS):
        @pl.when(s != t)
        def _():
            for w in range(n):
                if wait:
                    remote(w, t).wait_send()
                    arrival(w, t).wait_recv()
                else:
                    remote(w, t).start()
    if wait:
        for cp in local:
            cp.wait()


def _scatter_shapes(parts):
    return [jax.ShapeDtypeStruct((N_CHIPS, a.shape[0] // N_CHIPS, a.shape[1]), a.dtype) for a in parts]


def _pair_merge(parts, name):
    n = len(parts)

    def body(*refs):
        ins, outs = refs[:n], refs[n:2 * n]
        send_sems, recv_sems = refs[2 * n:]
        x, y, c, _ = _place()
        cps = [pltpu.make_async_remote_copy(
            src_ref=ins[w], dst_ref=outs[w], send_sem=send_sems.at[w], recv_sem=recv_sems.at[w],
            device_id=(x, y, 1 - c), device_id_type=MESH) for w in range(n)]
        for cp in cps:
            cp.start()
        for cp in cps:
            cp.wait()

    return pl.pallas_call(
        body, in_specs=[ANY] * n, out_specs=[ANY] * n,
        out_shape=[jax.ShapeDtypeStruct(a.shape, a.dtype) for a in parts],
        scratch_shapes=[pltpu.SemaphoreType.DMA((n,)), pltpu.SemaphoreType.DMA((n,))],
        name=name,
    )(*parts)


def _all_gather_small(a, name):
    def body(in_ref, out_ref, send_sems, recv_sems, local_sem):
        x, y, c, _ = _place()
        me = 4 * x + 2 * y + c
        local = pltpu.make_async_copy(in_ref, out_ref.at[me], local_sem)
        local.start()

        def remote(d):
            return pltpu.make_async_remote_copy(
                src_ref=in_ref, dst_ref=out_ref.at[me], send_sem=send_sems.at[d], recv_sem=recv_sems.at[me],
                device_id=(d // 4, (d // 2) % 2, d % 2), device_id_type=MESH)

        def arrival(d):
            return pltpu.make_async_remote_copy(
                src_ref=in_ref, dst_ref=out_ref.at[d], send_sem=send_sems.at[d], recv_sem=recv_sems.at[d],
                device_id=(d // 4, (d // 2) % 2, d % 2), device_id_type=MESH)

        for d in range(N_DEV):
            @pl.when(me != d)
            def _():
                remote(d).start()
        for d in range(N_DEV):
            @pl.when(me != d)
            def _():
                remote(d).wait_send()
                arrival(d).wait_recv()
        local.wait()

    return pl.pallas_call(
        body, in_specs=[ANY], out_specs=ANY, out_shape=jax.ShapeDtypeStruct((N_DEV,) + a.shape, a.dtype),
        scratch_shapes=[pltpu.SemaphoreType.DMA((N_DEV,)), pltpu.SemaphoreType.DMA((N_DEV,)), pltpu.SemaphoreType.DMA],
        name=name,
    )(a)


def _reduce_tile(rows):
    return _pick(rows, (256, 128, 16))


def _pair_add(full, other, layer, name):
    _, rows, cols = full.shape
    tr = _reduce_tile(rows)

    def body(layer_ref, a_ref, b_ref, o_ref):
        o_ref[...] = (a_ref[0] + b_ref[...]).astype(o_ref.dtype)

    return pl.pallas_call(
        body,
        grid_spec=pltpu.PrefetchScalarGridSpec(
            num_scalar_prefetch=1, grid=(rows // tr,),
            in_specs=[pl.BlockSpec((1, tr, cols), lambda i, lr: (lr[0], i, 0)), pl.BlockSpec((tr, cols), lambda i, lr: (i, 0))],
            out_specs=pl.BlockSpec((tr, cols), lambda i, lr: (i, 0))),
        out_shape=jax.ShapeDtypeStruct((rows, cols), BF16), compiler_params=_params("parallel"), name=name,
    )(layer, full, other)


def _sum_slots(a, name):
    n, rows, cols = a.shape
    tr = _reduce_tile(rows)

    def body(a_ref, o_ref):
        acc = a_ref[0].astype(F32)
        for k in range(1, n):
            acc = acc + a_ref[k].astype(F32)
        o_ref[...] = acc

    return pl.pallas_call(
        body, grid=(rows // tr,), in_specs=[pl.BlockSpec((n, tr, cols), lambda i: (0, i, 0))],
        out_specs=pl.BlockSpec((tr, cols), lambda i: (i, 0)),
        out_shape=jax.ShapeDtypeStruct((rows, cols), F32), compiler_params=_params("parallel"), name=name,
    )(a)


def _adamw(w, g, m, v, name):
    rows, cols = w.shape
    tr = _pick(rows, (256, 8))

    def body(w_ref, g_ref, m_ref, v_ref, d_ref, nm_ref, nv_ref):
        gv = g_ref[...]
        mn = ADAM_B1 * m_ref[...] + (1.0 - ADAM_B1) * gv
        vn = ADAM_B2 * v_ref[...] + (1.0 - ADAM_B2) * jnp.square(gv)
        m_hat = mn / (1.0 - ADAM_B1 ** ADAM_STEP)
        v_hat = vn / (1.0 - ADAM_B2 ** ADAM_STEP)
        d_ref[...] = -ADAM_LR * (m_hat / (jnp.sqrt(v_hat) + ADAM_EPS) + ADAM_WD * w_ref[...])
        nm_ref[...] = mn
        nv_ref[...] = vn

    blk = pl.BlockSpec((tr, cols), lambda i: (i, 0))
    return pl.pallas_call(
        body, grid=(rows // tr,), in_specs=[blk] * 4, out_specs=[blk] * 3,
        out_shape=[jax.ShapeDtypeStruct((rows, cols), F32)] * 3, compiler_params=_params("parallel"), name=name,
    )(w, g, m, v)


BIG = ("ssm_w_in", "ssm_w_out", "att_w_in", "att_w_out")
SHARDED = BIG + ("ssm_conv_w",)
SMALL = ("pre_norm", "post_norm", "ssm_conv_b", "ssm_dt_bias", "ssm_a_log", "ssm_d", "ssm_gate_norm", "att_sinks")
WEIGHTS = ("pre_norm", "post_norm", "ssm_w_in", "ssm_conv_w", "ssm_conv_b", "ssm_dt_bias", "ssm_a_log", "ssm_d",
           "ssm_gate_norm", "ssm_w_out", "att_w_in", "att_sinks", "att_w_out")


def _halves(a):
    return a.reshape(2, a.shape[0] // 2, a.shape[1])


def _layer_shards(j, ssm_w_in, ssm_w_out, att_w_in, att_w_out, ssm_conv_w):
    return [_halves(ssm_w_in[j].astype(BF16)), _halves(ssm_w_out[j].astype(BF16)), _halves(att_w_in[j].astype(BF16)),
            _halves(att_w_out[j].astype(BF16)), _halves(ssm_conv_w[j])]


SHARD_KEYS = ("ssm_w_in", "ssm_w_out", "att_w_in", "att_w_out", "ssm_conv_w")


def _whole_weights(keys, gathered):
    out = {}
    for k, g in zip(keys, gathered):
        g = g.reshape((N_CHIPS, 2 * g.shape[2], g.shape[3]))
        if k in ("ssm_w_out", "att_w_out"):
            out[k] = g.reshape(N_CHIPS * g.shape[1], g.shape[2])
        else:
            out[k] = jnp.transpose(g, (1, 0, 2)).reshape(g.shape[1], N_CHIPS * g.shape[2])
    if "ssm_w_in" in out:
        out["ssm_w_in"] = jnp.pad(out["ssm_w_in"], ((0, 0), (0, SSM_IN_PAD - SSM_IN_DIM)))
    return out


def _halves_by_chip(key, g):
    if key in ("ssm_w_out", "att_w_out"):
        rows = g.shape[0] // N_CHIPS
        blocks = g.reshape(N_CHIPS, 2, rows // 2, g.shape[1])
        return jnp.transpose(blocks, (1, 0, 2, 3)).reshape(2, N_CHIPS * (rows // 2), g.shape[1])
    cols = (SSM_IN_DIM if key == "ssm_w_in" else g.shape[1]) // N_CHIPS
    rows = g.shape[0]
    blocks = g[:, :N_CHIPS * cols].reshape(2, rows // 2, N_CHIPS, cols)
    return jnp.transpose(blocks, (0, 2, 1, 3)).reshape(2, N_CHIPS * (rows // 2), cols)


def _pack_small(tree, keys):
    flat = jnp.concatenate([tree[k].reshape(-1) for k in keys])
    rows = -(-flat.shape[0] // (8 * LANES)) * 8
    return jnp.pad(flat, (0, rows * LANES - flat.shape[0])).reshape(rows, LANES)


def _unpack_small(packed, shapes, keys):
    flat = packed.reshape(-1)
    out, at = {}, 0
    for k in keys:
        n = 1
        for dim in shapes[k]:
            n *= dim
        out[k] = flat[at:at + n].reshape(shapes[k])
        at += n
    return out


def kernel(x, positions, pre_norm, post_norm, ssm_w_in, ssm_conv_w, ssm_conv_b, ssm_dt_bias, ssm_a_log, ssm_d, ssm_gate_norm, ssm_w_out, att_w_in, att_sinks, att_w_out, loss_target, m_pre_norm, m_post_norm, m_ssm_w_in, m_ssm_conv_w, m_ssm_conv_b, m_ssm_dt_bias, m_ssm_a_log, m_ssm_d, m_ssm_gate_norm, m_ssm_w_out, m_att_w_in, m_att_sinks, m_att_w_out, v_pre_norm, v_post_norm, v_ssm_w_in, v_ssm_conv_w, v_ssm_conv_b, v_ssm_dt_bias, v_ssm_a_log, v_ssm_d, v_ssm_gate_norm, v_ssm_w_out, v_att_w_in, v_att_sinks, v_att_w_out):
    w = dict(pre_norm=pre_norm, post_norm=post_norm, ssm_w_in=ssm_w_in, ssm_conv_w=ssm_conv_w, ssm_conv_b=ssm_conv_b,
             ssm_dt_bias=ssm_dt_bias, ssm_a_log=ssm_a_log, ssm_d=ssm_d, ssm_gate_norm=ssm_gate_norm, ssm_w_out=ssm_w_out,
             att_w_in=att_w_in, att_sinks=att_sinks, att_w_out=att_w_out)
    m = dict(pre_norm=m_pre_norm, post_norm=m_post_norm, ssm_w_in=m_ssm_w_in, ssm_conv_w=m_ssm_conv_w, ssm_conv_b=m_ssm_conv_b,
             ssm_dt_bias=m_ssm_dt_bias, ssm_a_log=m_ssm_a_log, ssm_d=m_ssm_d, ssm_gate_norm=m_ssm_gate_norm,
             ssm_w_out=m_ssm_w_out, att_w_in=m_att_w_in, att_sinks=m_att_sinks, att_w_out=m_att_w_out)
    v = dict(pre_norm=v_pre_norm, post_norm=v_post_norm, ssm_w_in=v_ssm_w_in, ssm_conv_w=v_ssm_conv_w, ssm_conv_b=v_ssm_conv_b,
             ssm_dt_bias=v_ssm_dt_bias, ssm_a_log=v_ssm_a_log, ssm_d=v_ssm_d, ssm_gate_norm=v_ssm_gate_norm,
             ssm_w_out=v_ssm_w_out, att_w_in=v_att_w_in, att_sinks=v_att_sinks, att_w_out=v_att_w_out)
    c = lax.axis_index("c")
    chip = 2 * lax.axis_index("x") + lax.axis_index("y")

    sharded = (ssm_w_in, ssm_w_out, att_w_in, att_w_out, ssm_conv_w)
    own = [dict(zip(SHARD_KEYS, _layer_shards(j, *sharded))) for j in range(2)]
    now_keys = ("ssm_w_in", "ssm_conv_w")
    later_keys = ("ssm_w_out", "att_w_in", "att_w_out")
    first_in = _whole_weights(now_keys, _chip_gather([own[0][k] for k in now_keys], "gather_weights_0"))

    def in_proj_with_first_pair(matmul):
        proj, *arrived = matmul(ride=[own[0][k] for k in later_keys])
        return proj, _whole_weights(later_keys, _pair_handoff(arrived, "gather_weights_0_rest_handoff"))

    def scan_with_second_pair(scan):
        y, act, hin, *arrived = scan(ride=[own[1][k] for k in SHARD_KEYS])
        return y, act, hin, _whole_weights(SHARD_KEYS, _pair_handoff(arrived, "gather_weights_1_handoff"))

    half = jnp.reshape(c, (1,)).astype(jnp.int32)

    def reduce_begin(pair_grads, tag):
        parts = [_halves_by_chip(k, pair_grads[k]) for k in BIG]
        from_sibling = _pair_swap(parts, f"reduce_pair_swap_{tag}")
        return [_pair_add(p, o, half, f"reduce_pair_add_{tag}_{n}") for n, (p, o) in enumerate(zip(parts, from_sibling))]

    def reduce_end(by_chip, tag):
        mine = [_sum_slots(a, f"reduce_chip_sum_{tag}_{n}") for n, a in enumerate(by_chip)]
        theirs = _pair_merge(mine, f"reduce_pair_merge_{tag}")
        return {k: jnp.where(c == 0, jnp.concatenate([a, b]), jnp.concatenate([b, a])) for k, a, b in zip(BIG, mine, theirs)}

    def attn_bwd_with_second_pair_grads(attn_bwd, pair_grads):
        dq, dk, dv, dgate, dsk, *by_chip = attn_bwd(ride=reduce_begin(pair_grads, "1"))
        return (dq, dk, dv, dgate, dsk), reduce_end(by_chip, "1")

    def in_dx_with_first_pair_grads(matmul, pair_grads):
        dh, *by_chip = matmul(ride=reduce_begin(pair_grads, "0"), ride_scatters=True)
        return dh, reduce_end(by_chip, "0")

    loss_lanes, grad_x, gr, reduced_0, reduced_1 = _local_step(
        x[0], positions[0], pre_norm, post_norm, ssm_conv_b, ssm_dt_bias, ssm_a_log, ssm_d, ssm_gate_norm, att_sinks,
        loss_target[0], first_in, in_proj_with_first_pair, scan_with_second_pair, attn_bwd_with_second_pair_grads,
        in_dx_with_first_pair_grads)
    loss = lax.psum(0.5 * jnp.sum(loss_lanes) / D_MODEL, ("x", "y", "c"))
    grads = {k: jnp.stack([reduced_0[k], reduced_1[k]]) for k in BIG}

    small_keys = SMALL + ("ssm_conv_w",)
    small_shapes = {k: w[k].shape for k in SMALL}
    small_shapes["ssm_conv_w"] = gr["ssm_conv_w"].shape
    small_sum = _sum_slots(_all_gather_small(_pack_small(gr, small_keys), "reduce_small_gather"), "reduce_small_sum")
    grads.update(_unpack_small(small_sum, small_shapes, small_keys))
    conv_cols = ssm_conv_w.shape[2]
    grads["ssm_conv_w"] = lax.dynamic_slice_in_dim(grads["ssm_conv_w"], chip * conv_cols, conv_cols, axis=2)

    delta, new_m, new_v = {}, {}, {}
    for k in SHARDED:
        shp = w[k].shape
        two_d = (shp[0] * shp[1], shp[2])
        d_, m_, v_ = _adamw(w[k].reshape(two_d), grads[k].reshape(two_d), m[k].reshape(two_d), v[k].reshape(two_d),
                            f"adamw_{k}")
        delta[k], new_m[k], new_v[k] = d_.reshape(shp), m_.reshape(shp), v_.reshape(shp)
    d_, m_, v_ = _adamw(_pack_small(w, SMALL), _pack_small(grads, SMALL), _pack_small(m, SMALL), _pack_small(v, SMALL),
                        "adamw_small")
    delta.update(_unpack_small(d_, small_shapes, SMALL))
    new_m.update(_unpack_small(m_, small_shapes, SMALL))
    new_v.update(_unpack_small(v_, small_shapes, SMALL))

    return (loss, grad_x[None], *[grads[k] for k in WEIGHTS], *[delta[k] for k in WEIGHTS],
            *[new_m[k] for k in WEIGHTS], *[new_v[k] for k in WEIGHTS])
```

```python
import functools

import jax
import jax.numpy as jnp
from jax import lax
from jax.experimental import pallas as pl
from jax.experimental.pallas import tpu as pltpu

F32 = jnp.float32
BF16 = jnp.bfloat16
EPS = 1e-6
NEG_INF = float("-inf")

D_MODEL = 1024
DEPTH = 4
SSM_D_INNER = 2048
SSM_HEAD_DIM = 64
SSM_HEADS = 32
SSM_GROUPS = 8
SSM_HPG = 4
SSM_STATE = 128
SSM_CONV = 4
SSM_CHUNK = 128
SSM_BC_DIM = 1024
SSM_CONV_DIM = 4096
SSM_IN_DIM = 6176
SSM_IN_PAD = 6272
SSM_DT_PAD = 128
ATT_HEAD_DIM = 64
ATT_Q_HEADS = 16
ATT_KV_HEADS = 4
ATT_GQA = 4
ATT_WIDTH = 1024
ATT_KV_WIDTH = 256
ATT_IN_DIM = 2560
ATT_QKV = ATT_WIDTH + 2 * ATT_KV_WIDTH
ATT_BLOCK = 128
ROPE_THETA = 500000.0
ROPE_DIM = 16
ROPE_HALF = 8
Q_SCALE = ATT_HEAD_DIM ** -0.5

ADAM_LR = 0.001
ADAM_B1 = 0.9
ADAM_B2 = 0.999
ADAM_EPS = 1e-08
ADAM_WD = 0.01
ADAM_STEP = 10

VMEM_LIMIT_BYTES = 48 * 1024 * 1024
NT_DIMS = (((1,), (1,)), ((), ()))
TN_DIMS = (((0,), (0,)), ((), ()))


def _params(*sem):
    return pltpu.CompilerParams(dimension_semantics=sem, vmem_limit_bytes=VMEM_LIMIT_BYTES)


def _pick(n, cands):
    for c in cands:
        if n % c == 0:
            return c
    return n


def _sigmoid(v):
    return 0.5 * jnp.tanh(0.5 * v) + 0.5


def _bdot_tn(a, b):
    return lax.dot_general(a.astype(BF16), b.astype(BF16), TN_DIMS, preferred_element_type=F32)


MATMUL_VMEM_BUDGET = 36 * 1024 * 1024


def _matmul_tiles(m, n, k, out_bytes, reduce_rows):
    best = None
    whole = [k] if (not reduce_rows or k <= 2048) else []
    for tk in whole + [c for c in (4096, 2048, 1024, 896, 512) if k % c == 0 and c < k]:
        for tm in (c for c in (2048, 1024, 512, 256) if m % c == 0):
            for tn in (c for c in (n, 1280, 1024, 896, 640, 512) if n % c == 0):
                acc = tm * tn * 4 if tk < k else 0
                need = 2 * (2 * tk * (tm + tn) + tm * tn * out_bytes) + acc
                if need <= MATMUL_VMEM_BUDGET and (best is None or tm * tn * min(tk, 2048) > best[0]):
                    best = (tm * tn * min(tk, 2048), tm, tn, tk)
        if best is not None and not reduce_rows:
            break
    return best[1:]


def _matmul(a, b, mode, out_dtype, name, ride=(), ride_scatters=False):
    if mode == "nn":
        (m, k), n = a.shape, b.shape[1]
    elif mode == "nt":
        (m, k), n = a.shape, b.shape[0]
    else:
        (k, m), n = a.shape, b.shape[1]
    tm, tn, tk = _matmul_tiles(m, n, k, jnp.dtype(out_dtype).itemsize, mode == "tn")
    nk = k // tk
    steps = (n // tn, m // tm, nk)
    dims = {"nn": (((1,), (0,)), ((), ())), "nt": NT_DIMS, "tn": TN_DIMS}[mode]
    n_ride = len(ride)
    exchange = _scatter_between_chips if ride_scatters else _gather_between_chips
    arrived = _scatter_shapes(ride) if ride_scatters else [jax.ShapeDtypeStruct((N_CHIPS,) + r.shape, r.dtype) for r in ride]

    def body(*refs):
        a_ref, b_ref = refs[:2]
        ride_in = refs[2:2 + n_ride]
        o_ref = refs[2 + n_ride]
        ride_out = refs[3 + n_ride:3 + 2 * n_ride]
        acc_ref = refs[3 + 2 * n_ride]
        ride_sems = refs[4 + 2 * n_ride:]
        kk = pl.program_id(2)
        at = [pl.program_id(d) for d in range(3)]
        if n_ride:
            @pl.when((at[0] == 0) & (at[1] == 0) & (at[2] == 0))
            def _():
                exchange(ride_in, ride_out, *ride_sems, wait=False)

        part = lax.dot_general(a_ref[...], b_ref[...], dims, preferred_element_type=F32)
        if nk == 1:
            o_ref[...] = part.astype(o_ref.dtype)
        else:
            @pl.when(kk == 0)
            def _():
                acc_ref[...] = part

            @pl.when(kk > 0)
            def _():
                acc_ref[...] += part

            @pl.when(kk == nk - 1)
            def _():
                o_ref[...] = acc_ref[...].astype(o_ref.dtype)

        if n_ride:
            @pl.when((at[0] == steps[0] - 1) & (at[1] == steps[1] - 1) & (at[2] == steps[2] - 1))
            def _():
                exchange(ride_in, ride_out, *ride_sems, wait=True)

    if mode == "nn":
        a_spec = pl.BlockSpec((tm, tk), lambda j, i, kk: (i, kk))
        b_spec = pl.BlockSpec((tk, tn), lambda j, i, kk: (kk, j))
    elif mode == "nt":
        a_spec = pl.BlockSpec((tm, tk), lambda j, i, kk: (i, kk))
        b_spec = pl.BlockSpec((tn, tk), lambda j, i, kk: (j, kk))
    else:
        a_spec = pl.BlockSpec((tk, tm), lambda j, i, kk: (kk, i))
        b_spec = pl.BlockSpec((tk, tn), lambda j, i, kk: (kk, j))
    out = pl.pallas_call(
        body, grid=steps, in_specs=[a_spec, b_spec] + [ANY] * n_ride,
        out_specs=[pl.BlockSpec((tm, tn), lambda j, i, kk: (i, j))] + [ANY] * n_ride,
        out_shape=[jax.ShapeDtypeStruct((m, n), out_dtype)] + arrived,
        scratch_shapes=[pltpu.VMEM((tm, tn), F32)] + (_gather_sems(n_ride) if n_ride else []),
        compiler_params=_params(*(["arbitrary"] * 3 if n_ride else ["parallel", "parallel", "arbitrary"])), name=name,
    )(a, b, *ride)
    return out if n_ride else out[0]


def _row_tile(l):
    return _pick(l, (512, 256, 128))


def _rmsnorm_fwd(x, w, name):
    l, d = x.shape
    tl = _row_tile(l)

    def body(x_ref, w_ref, o_ref):
        xv = x_ref[...]
        r = lax.rsqrt(jnp.mean(xv * xv, axis=-1, keepdims=True) + EPS)
        o_ref[...] = (xv * r * w_ref[...]).astype(o_ref.dtype)

    return pl.pallas_call(
        body, grid=(l // tl,),
        in_specs=[pl.BlockSpec((tl, d), lambda i: (i, 0)), pl.BlockSpec((1, d), lambda i: (0, 0))],
        out_specs=pl.BlockSpec((tl, d), lambda i: (i, 0)),
        out_shape=jax.ShapeDtypeStruct((l, d), BF16), compiler_params=_params("parallel"), name=name,
    )(x, w.reshape(1, d))


def _post_fwd(x, y, w, w_next, name):
    l, d = x.shape
    tl = _row_tile(l)

    def body(x_ref, y_ref, w_ref, wn_ref, o_ref, h_ref):
        yv = y_ref[...]
        r = lax.rsqrt(jnp.mean(yv * yv, axis=-1, keepdims=True) + EPS)
        out = x_ref[...] + yv * r * w_ref[...]
        o_ref[...] = out
        rn = lax.rsqrt(jnp.mean(out * out, axis=-1, keepdims=True) + EPS)
        h_ref[...] = (out * rn * wn_ref[...]).astype(h_ref.dtype)

    row = pl.BlockSpec((tl, d), lambda i: (i, 0))
    vec = pl.BlockSpec((1, d), lambda i: (0, 0))
    return pl.pallas_call(
        body, grid=(l // tl,), in_specs=[row, row, vec, vec], out_specs=[row, row],
        out_shape=[jax.ShapeDtypeStruct((l, d), F32), jax.ShapeDtypeStruct((l, d), BF16)],
        compiler_params=_params("parallel"), name=name,
    )(x, y, w.reshape(1, d), w_next.reshape(1, d))


def _post_loss(x, y, w, t, name):
    l, d = x.shape
    tl = _row_tile(l)
    nt = l // tl

    def body(x_ref, y_ref, w_ref, t_ref, g_ref, dy_ref, ls_ref, dw_ref, acc_ref):
        i = pl.program_id(0)

        @pl.when(i == 0)
        def _():
            ls_ref[...] = jnp.zeros_like(ls_ref)
            acc_ref[...] = jnp.zeros_like(acc_ref)

        yv = y_ref[...]
        r = lax.rsqrt(jnp.mean(yv * yv, axis=-1, keepdims=True) + EPS)
        nrm = yv * r
        e = x_ref[...] + nrm * w_ref[...] - t_ref[...]
        gv = e * (1.0 / d)
        g_ref[...] = gv
        ls_ref[...] += jnp.sum((e * e).reshape(tl // 8, 8, d), axis=0)
        gw = gv * w_ref[...]
        dy_ref[...] = (r * (gw - nrm * jnp.mean(gw * nrm, axis=-1, keepdims=True))).astype(dy_ref.dtype)
        acc_ref[...] += jnp.sum((gv * nrm).reshape(tl // 8, 8, d), axis=0)

        @pl.when(i == nt - 1)
        def _():
            dw_ref[...] = jnp.sum(acc_ref[...], axis=0, keepdims=True)

    row = pl.BlockSpec((tl, d), lambda i: (i, 0))
    vec = pl.BlockSpec((1, d), lambda i: (0, 0))
    return pl.pallas_call(
        body, grid=(nt,), in_specs=[row, row, vec, row],
        out_specs=[row, row, pl.BlockSpec((8, d), lambda i: (0, 0)), vec],
        out_shape=[jax.ShapeDtypeStruct((l, d), F32), jax.ShapeDtypeStruct((l, d), BF16),
                   jax.ShapeDtypeStruct((8, d), F32), jax.ShapeDtypeStruct((1, d), F32)],
        scratch_shapes=[pltpu.VMEM((8, d), F32)], compiler_params=_params("arbitrary"), name=name,
    )(x, y, w.reshape(1, d), t)


def _norm_bwd_chain(dh, x, w_pre, resid, y_prev, w_post_prev, name):
    l, d = x.shape
    tl = _row_tile(l)
    nt = l // tl

    def body(dh_ref, x_ref, wp_ref, r_ref, y_ref, wq_ref, g_ref, dy_ref, dwp_ref, dwq_ref, accp_ref, accq_ref):
        i = pl.program_id(0)

        @pl.when(i == 0)
        def _():
            accp_ref[...] = jnp.zeros_like(accp_ref)
            accq_ref[...] = jnp.zeros_like(accq_ref)

        xv = x_ref[...]
        dhv = dh_ref[...]
        rx = lax.rsqrt(jnp.mean(xv * xv, axis=-1, keepdims=True) + EPS)
        nx = xv * rx
        gw = dhv * wp_ref[...]
        gv = rx * (gw - nx * jnp.mean(gw * nx, axis=-1, keepdims=True)) + r_ref[...]
        g_ref[...] = gv
        accp_ref[...] += jnp.sum((dhv * nx).reshape(tl // 8, 8, d), axis=0)
        yv = y_ref[...]
        ry = lax.rsqrt(jnp.mean(yv * yv, axis=-1, keepdims=True) + EPS)
        ny = yv * ry
        gq = gv * wq_ref[...]
        dy_ref[...] = (ry * (gq - ny * jnp.mean(gq * ny, axis=-1, keepdims=True))).astype(dy_ref.dtype)
        accq_ref[...] += jnp.sum((gv * ny).reshape(tl // 8, 8, d), axis=0)

        @pl.when(i == nt - 1)
        def _():
            dwp_ref[...] = jnp.sum(accp_ref[...], axis=0, keepdims=True)
            dwq_ref[...] = jnp.sum(accq_ref[...], axis=0, keepdims=True)

    row = pl.BlockSpec((tl, d), lambda i: (i, 0))
    vec = pl.BlockSpec((1, d), lambda i: (0, 0))
    return pl.pallas_call(
        body, grid=(nt,), in_specs=[row, row, vec, row, row, vec], out_specs=[row, row, vec, vec],
        out_shape=[jax.ShapeDtypeStruct((l, d), F32), jax.ShapeDtypeStruct((l, d), BF16),
                   jax.ShapeDtypeStruct((1, d), F32), jax.ShapeDtypeStruct((1, d), F32)],
        scratch_shapes=[pltpu.VMEM((8, d), F32), pltpu.VMEM((8, d), F32)],
        compiler_params=_params("arbitrary"), name=name,
    )(dh, x, w_pre.reshape(1, d), resid, y_prev, w_post_prev.reshape(1, d))


def _rmsnorm_bwd(g, y, w, resid, name):
    l, d = y.shape
    tl = _row_tile(l)
    nt = l // tl

    def body(g_ref, y_ref, w_ref, r_ref, dy_ref, dw_ref, acc_ref):
        i = pl.program_id(0)

        @pl.when(i == 0)
        def _():
            acc_ref[...] = jnp.zeros_like(acc_ref)

        yv = y_ref[...]
        gv = g_ref[...]
        r = lax.rsqrt(jnp.mean(yv * yv, axis=-1, keepdims=True) + EPS)
        nrm = yv * r
        gw = gv * w_ref[...]
        dy_ref[...] = r * (gw - nrm * jnp.mean(gw * nrm, axis=-1, keepdims=True)) + r_ref[...]
        acc_ref[...] += jnp.sum((gv * nrm).reshape(tl // 8, 8, d), axis=0)

        @pl.when(i == nt - 1)
        def _():
            dw_ref[...] = jnp.sum(acc_ref[...], axis=0, keepdims=True)

    row = pl.BlockSpec((tl, d), lambda i: (i, 0))
    vec = pl.BlockSpec((1, d), lambda i: (0, 0))
    return pl.pallas_call(
        body, grid=(nt,), in_specs=[row, row, vec, row], out_specs=[row, vec],
        out_shape=[jax.ShapeDtypeStruct((l, d), F32), jax.ShapeDtypeStruct((1, d), F32)],
        scratch_shapes=[pltpu.VMEM((8, d), F32)], compiler_params=_params("arbitrary"), name=name,
    )(g, y, w.reshape(1, d), resid)


CONV_COLS = 512
HALO = 8
HALO16 = 16
CONV_SUB_ROWS = 64
CONV_SUB_COLS = 256


def _conv_rows(l):
    return _pick(l, (1024, 512, 256, 128))


def _conv_bwd(dact, pre, proj, cw, dproj, name):
    l, width = dact.shape
    tl = _conv_rows(l)
    nt = l // tl
    pre_off = 0
    u_off = SSM_D_INNER // CONV_COLS
    hb16 = tl // HALO16
    last_hb16 = l // HALO16 - 1

    def body(da_ref, da_h_ref, p_ref, p_h_ref, u_ref, w_ref, _, du_ref, dw_ref, db_ref, ext_ref):
        i = pl.program_id(1)

        @pl.when(i == 0)
        def _():
            dw_ref[...] = jnp.zeros_like(dw_ref)
            db_ref[...] = jnp.zeros_like(db_ref)

        def dpre_of(da, p):
            s = _sigmoid(p)
            return da * (s * (1.0 + p * (1.0 - s)))

        ext_ref[0:tl, :] = dpre_of(da_ref[...].astype(F32), p_ref[...].astype(F32))
        ext_ref[tl:tl + HALO, :] = jnp.where(
            i < nt - 1, dpre_of(da_h_ref[...].astype(F32)[:HALO], p_h_ref[...].astype(F32)[:HALO]), 0.0)
        sub = CONV_SUB_ROWS

        def fold(v):
            return jnp.sum(v.reshape(sub // 8, 8, CONV_SUB_COLS), axis=0)

        for c0 in range(0, CONV_COLS, CONV_SUB_COLS):
            cs = slice(c0, c0 + CONV_SUB_COLS)
            dws = [jnp.zeros((8, CONV_SUB_COLS), F32) for _ in range(SSM_CONV)]
            dbs = jnp.zeros((8, CONV_SUB_COLS), F32)
            for r0 in range(0, tl, sub):
                dext = ext_ref[r0:r0 + sub + HALO, cs]
                uv = u_ref[r0:r0 + sub, cs]
                for k in range(SSM_CONV):
                    j = SSM_CONV - 1 - k
                    ahead = dext[:sub] if j == 0 else pltpu.roll(dext, sub + HALO - j, 0)[:sub]
                    term = w_ref[k:k + 1, cs] * ahead
                    du = term if k == 0 else du + term
                    dws[k] = dws[k] + fold(ahead * uv)
                dbs = dbs + fold(dext[:sub])
                du_ref[r0:r0 + sub, cs] = du.astype(du_ref.dtype)
            for k in range(SSM_CONV):
                dw_ref[k:k + 1, cs] += jnp.sum(dws[k], axis=0, keepdims=True)
            db_ref[:, cs] += jnp.sum(dbs, axis=0, keepdims=True)

    return pl.pallas_call(
        body, grid=(width // CONV_COLS, nt),
        in_specs=[pl.BlockSpec((tl, CONV_COLS), lambda j, i: (i, j)),
                  pl.BlockSpec((HALO16, CONV_COLS), lambda j, i: (jnp.minimum((i + 1) * hb16, last_hb16), j)),
                  pl.BlockSpec((tl, CONV_COLS), lambda j, i: (i, pre_off + j)),
                  pl.BlockSpec((HALO16, CONV_COLS), lambda j, i: (jnp.minimum((i + 1) * hb16, last_hb16), pre_off + j)),
                  pl.BlockSpec((tl, CONV_COLS), lambda j, i: (i, u_off + j)),
                  pl.BlockSpec((SSM_CONV, CONV_COLS), lambda j, i: (0, pre_off + j)),
                  pl.BlockSpec(memory_space=pl.ANY)],
        out_specs=[pl.BlockSpec((tl, CONV_COLS), lambda j, i: (i, u_off + j)),
                   pl.BlockSpec((SSM_CONV, CONV_COLS), lambda j, i: (0, j)),
                   pl.BlockSpec((1, CONV_COLS), lambda j, i: (0, j))],
        out_shape=[jax.ShapeDtypeStruct(dproj.shape, dproj.dtype), jax.ShapeDtypeStruct((SSM_CONV, width), F32),
                   jax.ShapeDtypeStruct((1, width), F32)],
        scratch_shapes=[pltpu.VMEM((tl + HALO, CONV_COLS), F32)],
        input_output_aliases={6: 0}, compiler_params=_params("parallel", "arbitrary"), name=name,
    )(dact, dact, pre, pre, proj, cw, dproj)


DT_COL_BLOCK = (SSM_D_INNER + SSM_CONV_DIM) // SSM_DT_PAD


def _split3(v):
    hi = v.astype(BF16)
    rest = v - hi.astype(F32)
    mid = rest.astype(BF16)
    lo = (rest - mid.astype(F32)).astype(BF16)
    return hi, mid, lo


def _ssd_prep(proj, bias, a_log, name):
    l = proj.shape[0]
    nc = l // SSM_CHUNK
    head_dim_log2 = SSM_HEAD_DIM.bit_length() - 1

    def body(p_ref, b_ref, al_ref, dtb_ref, acsb_ref, dtr_ref, acsr_ref):
        v = p_ref[...] + b_ref[...]
        dt_hi, dt_mid, _ = _split3(jnp.maximum(v, 0.0) + jnp.log1p(jnp.exp(-jnp.abs(v))))
        dt = dt_hi.astype(F32) + dt_mid.astype(F32)
        ri = lax.broadcasted_iota(jnp.int32, (SSM_CHUNK, SSM_CHUNK), 0)
        cj = lax.broadcasted_iota(jnp.int32, (SSM_CHUNK, SSM_CHUNK), 1)
        tri = (ri >= cj).astype(BF16)
        acs_pieces = _split3(sum(jnp.dot(tri, piece, preferred_element_type=F32)
                                 for piece in _split3(dt * (-jnp.exp(al_ref[...])))))
        acs = sum(piece.astype(F32) for piece in acs_pieces)
        head_of_lane = lax.shift_right_logical(lax.broadcasted_iota(jnp.int32, (SSM_DT_PAD, SSM_D_INNER), 1), head_dim_log2)
        spread = (head_of_lane == lax.broadcasted_iota(jnp.int32, (SSM_DT_PAD, SSM_D_INNER), 0)).astype(BF16)
        dtb_ref[...] = sum(jnp.dot(piece, spread, preferred_element_type=F32) for piece in (dt_hi, dt_mid))
        acsb_ref[...] = sum(jnp.dot(piece, spread, preferred_element_type=F32) for piece in acs_pieces)
        dt_rows = dt.T
        acs_rows = acs.T
        for g in range(SSM_GROUPS):
            heads = slice(g * SSM_HPG, (g + 1) * SSM_HPG)
            dtr_ref[g] = dt_rows[heads, :]
            acsr_ref[g] = acs_rows[heads, :]

    rows = pl.BlockSpec((SSM_GROUPS, SSM_HPG, SSM_CHUNK), lambda c: (0, 0, c))
    dense = pl.BlockSpec((SSM_CHUNK, SSM_D_INNER), lambda c: (c, 0))
    return pl.pallas_call(
        body, grid=(nc,),
        in_specs=[pl.BlockSpec((SSM_CHUNK, SSM_DT_PAD), lambda c: (c, DT_COL_BLOCK)),
                  pl.BlockSpec((1, SSM_DT_PAD), lambda c: (0, 0)),
                  pl.BlockSpec((1, SSM_DT_PAD), lambda c: (0, 0))],
        out_specs=[dense, dense, rows, rows],
        out_shape=[jax.ShapeDtypeStruct((l, SSM_D_INNER), F32), jax.ShapeDtypeStruct((l, SSM_D_INNER), F32),
                   jax.ShapeDtypeStruct((SSM_GROUPS, SSM_HPG, l), F32),
                   jax.ShapeDtypeStruct((SSM_GROUPS, SSM_HPG, l), F32)],
        compiler_params=_params("parallel"), name=name,
    )(proj, bias, a_log)


def _dt_bwd(ddt, proj, bias, dproj, name):
    l = proj.shape[0]
    tl = _row_tile(l)

    def body(g_ref, p_ref, b_ref, _, o_ref, db_ref):
        @pl.when(pl.program_id(0) == 0)
        def _():
            db_ref[...] = jnp.zeros_like(db_ref)

        d = g_ref[...] * _sigmoid(p_ref[...] + b_ref[...])
        o_ref[...] = d.astype(o_ref.dtype)
        db_ref[...] += jnp.sum(d, axis=0, keepdims=True)

    return pl.pallas_call(
        body, grid=(l // tl,),
        in_specs=[pl.BlockSpec((tl, SSM_DT_PAD), lambda i: (i, 0)),
                  pl.BlockSpec((tl, SSM_DT_PAD), lambda i: (i, DT_COL_BLOCK)),
                  pl.BlockSpec((1, SSM_DT_PAD), lambda i: (0, 0)),
                  pl.BlockSpec(memory_space=pl.ANY)],
        out_specs=[pl.BlockSpec((tl, SSM_DT_PAD), lambda i: (i, DT_COL_BLOCK)),
                   pl.BlockSpec((1, SSM_DT_PAD), lambda i: (0, 0))],
        out_shape=[jax.ShapeDtypeStruct(dproj.shape, dproj.dtype), jax.ShapeDtypeStruct((1, SSM_DT_PAD), F32)],
        input_output_aliases={3: 0}, compiler_params=_params("arbitrary"), name=name,
    )(ddt, proj, bias, dproj)


GP = SSM_HPG * SSM_HEAD_DIM
HEAD_DIM_LOG2 = SSM_HEAD_DIM.bit_length() - 1
GPS = SSM_GROUPS
B_BLOCK0 = SSM_D_INNER // SSM_STATE
C_BLOCK0 = (SSM_D_INNER + SSM_BC_DIM) // SSM_STATE


def _chunk_iotas():
    ri = lax.broadcasted_iota(jnp.int32, (SSM_CHUNK, SSM_CHUNK), 0)
    cj = lax.broadcasted_iota(jnp.int32, (SSM_CHUNK, SSM_CHUNK), 1)
    return ri, cj


def _head_decay(acsb, acs_r, r, ri, cj):
    pair = acsb[:, (r // 2) * LANES:(r // 2 + 1) * LANES]
    mine_low = r % 2 == 0
    lane = lax.broadcasted_iota(jnp.int32, (1, LANES), 1)
    col = jnp.where((lane < SSM_HEAD_DIM) == mine_low, pair, pltpu.roll(pair, SSM_HEAD_DIM, 1))
    return jnp.exp(jnp.where(ri >= cj, col - acs_r[r:r + 1, :], NEG_INF))


def _head_masked_rows(v, dtype):
    head_of_lane = lax.shift_right_logical(lax.broadcasted_iota(jnp.int32, (1, GP), 1), HEAD_DIM_LOG2)
    narrow = v.astype(dtype)
    return jnp.concatenate([jnp.where(head_of_lane == r, narrow, jnp.zeros_like(narrow)) for r in range(SSM_HPG)], axis=0)


def _ssd_fwd(proj, cw, cb, dtb, acsb, acs_r, d_lanes, gate_w, name, ride=()):
    l = proj.shape[0]
    nc = l // SSM_CHUNK
    assert GPS == SSM_GROUPS
    n_ride = len(ride)
    halo_blocks = SSM_CHUNK // HALO
    x_block = 1

    def body(*refs):
        u0_ref, u1_ref, h0_ref, h1_ref, cw_ref, cb_ref, dtb_ref, acsb_ref, acsr_ref, d_ref, z_ref, gw_ref = refs[:12]
        ride_in = refs[12:12 + n_ride]
        y_ref, act_ref, hin_ref, pre_ref, xbc_ref = refs[12 + n_ride:17 + n_ride]
        ride_out = refs[17 + n_ride:17 + 2 * n_ride]
        h_ref, ext_ref, conv_ref = refs[17 + 2 * n_ride:20 + 2 * n_ride]
        ride_sems = refs[20 + 2 * n_ride:]
        s = pl.program_id(0)
        if n_ride:
            @pl.when(s == 0)
            def _():
                _gather_between_chips(ride_in, ride_out, *ride_sems, wait=False)

            @pl.when(s == nc)
            def _():
                _gather_between_chips(ride_in, ride_out, *ride_sems, wait=True)

        @pl.when(s <= 1)
        def _():
            h_ref[...] = jnp.zeros_like(h_ref)

        @pl.when(s == 0)
        def _():
            conv_ref[1] = jnp.zeros((SSM_CHUNK, SSM_CONV_DIM), BF16)

        conv_slot = s & 1
        scan_slot = (s - 1) & 1
        for half, (u_ref, hl_ref) in enumerate(((u0_ref, h0_ref), (u1_ref, h1_ref))):
            hc = slice(half * SSM_D_INNER, (half + 1) * SSM_D_INNER)
            ext_ref[0:HALO, hc] = jnp.where(s > 0, hl_ref[...], 0.0)
            ext_ref[HALO:HALO + SSM_CHUNK, hc] = u_ref[...]

        def conv_columns(c_lo, c_hi):
            for r0 in range(0, SSM_CHUNK, CONV_SUB_ROWS):
                for c0 in range(c_lo, c_hi, CONV_SUB_COLS):
                    cs = slice(c0, c0 + CONV_SUB_COLS)
                    ext = ext_ref[r0:r0 + CONV_SUB_ROWS + HALO, cs]
                    acc = cb_ref[:, cs] + cw_ref[SSM_CONV - 1:SSM_CONV, cs] * ext[HALO:]
                    for k in range(SSM_CONV - 1):
                        acc = acc + cw_ref[k:k + 1, cs] * pltpu.roll(ext, SSM_CONV - 1 - k, 0)[HALO:]
                    act = (acc * _sigmoid(acc)).astype(BF16)
                    pre_ref[r0:r0 + CONV_SUB_ROWS, cs] = acc.astype(pre_ref.dtype)
                    xbc_ref[r0:r0 + CONV_SUB_ROWS, cs] = act
                    conv_ref[conv_slot, r0:r0 + CONV_SUB_ROWS, cs] = act

        ri, cj = _chunk_iotas()
        conv_share = SSM_CONV_DIM // GPS
        for k in range(GPS):
            conv_columns(k * conv_share, (k + 1) * conv_share)
            g = k
            cols = slice(k * GP, (k + 1) * GP)
            bcols = slice(SSM_D_INNER + k * SSM_STATE, SSM_D_INNER + (k + 1) * SSM_STATE)
            ccols = slice(SSM_D_INNER + SSM_BC_DIM + k * SSM_STATE, SSM_D_INNER + SSM_BC_DIM + (k + 1) * SSM_STATE)
            xv = conv_ref[scan_slot, :, cols].astype(F32)
            bb = conv_ref[scan_slot, :, bcols]
            cb16 = conv_ref[scan_slot, :, ccols]
            acs_v = acsb_ref[:, cols]
            acs_r_v = acsr_ref[k]
            lastb = acs_v[SSM_CHUNK - 1:SSM_CHUNK, :]
            xd = xv * dtb_ref[:, cols]
            cbm = lax.dot_general(cb16, bb, NT_DIMS, preferred_element_type=F32)
            hin = h_ref[g]
            hin_ref[0, k] = hin
            yoff = jnp.dot(cb16, hin.astype(BF16), preferred_element_type=F32)
            ms = [(cbm * _head_decay(acs_v, acs_r_v, r, ri, cj)).astype(BF16) for r in range(SSM_HPG)]
            ydiag = jnp.dot(jnp.concatenate(ms, axis=1), _head_masked_rows(xd, BF16), preferred_element_type=F32)
            y_ref[:, cols] = ydiag + jnp.exp(acs_v) * yoff + d_ref[k] * xv
            h_ref[g] = hin * jnp.exp(lastb) + _bdot_tn(bb, xd * jnp.exp(lastb - acs_v))
        z = z_ref[...]
        yg = y_ref[...] * (z * _sigmoid(z))
        r = lax.rsqrt(jnp.mean(yg * yg, axis=-1, keepdims=True) + EPS)
        act_ref[...] = (yg * r * gw_ref[...]).astype(act_ref.dtype)

    def conv_at(s):
        return jnp.minimum(s, nc - 1)

    def scan_at(s):
        return jnp.maximum(s - 1, 0)

    lanes = pl.BlockSpec((SSM_CHUNK, SSM_D_INNER), lambda s: (scan_at(s), 0))
    conv_out = pl.BlockSpec((SSM_CHUNK, SSM_CONV_DIM), lambda s: (conv_at(s), 0))
    return pl.pallas_call(
        body, grid=(nc + 1,),
        in_specs=[pl.BlockSpec((SSM_CHUNK, SSM_D_INNER), lambda s: (conv_at(s), x_block)),
                  pl.BlockSpec((SSM_CHUNK, SSM_D_INNER), lambda s: (conv_at(s), x_block + 1)),
                  pl.BlockSpec((HALO, SSM_D_INNER), lambda s: (jnp.maximum(conv_at(s) * halo_blocks - 1, 0), x_block)),
                  pl.BlockSpec((HALO, SSM_D_INNER), lambda s: (jnp.maximum(conv_at(s) * halo_blocks - 1, 0), x_block + 1)),
                  pl.BlockSpec((SSM_CONV, SSM_CONV_DIM), lambda s: (0, 0)),
                  pl.BlockSpec((1, SSM_CONV_DIM), lambda s: (0, 0)),
                  lanes, lanes,
                  pl.BlockSpec((SSM_GROUPS, SSM_HPG, SSM_CHUNK), lambda s: (0, 0, scan_at(s))),
                  pl.BlockSpec((SSM_GROUPS, 1, GP), lambda s: (0, 0, 0)),
                  lanes, pl.BlockSpec((1, SSM_D_INNER), lambda s: (0, 0))] + [ANY] * n_ride,
        out_specs=[lanes, lanes, pl.BlockSpec((1, SSM_GROUPS, SSM_STATE, GP), lambda s: (scan_at(s), 0, 0, 0)),
                   conv_out, conv_out] + [ANY] * n_ride,
        out_shape=[jax.ShapeDtypeStruct((l, SSM_D_INNER), F32), jax.ShapeDtypeStruct((l, SSM_D_INNER), BF16),
                   jax.ShapeDtypeStruct((nc, SSM_GROUPS, SSM_STATE, GP), F32),
                   jax.ShapeDtypeStruct((l, SSM_CONV_DIM), BF16), jax.ShapeDtypeStruct((l, SSM_CONV_DIM), BF16)]
        + [jax.ShapeDtypeStruct((N_CHIPS,) + a.shape, a.dtype) for a in ride],
        scratch_shapes=[pltpu.VMEM((SSM_GROUPS, SSM_STATE, GP), F32),
                        pltpu.VMEM((SSM_CHUNK + HALO, SSM_CONV_DIM), F32),
                        pltpu.VMEM((2, SSM_CHUNK, SSM_CONV_DIM), BF16)] + (_gather_sems(n_ride) if n_ride else []),
        compiler_params=_params("arbitrary"), name=name,
    )(proj, proj, proj, proj, cw, cb.reshape(1, SSM_CONV_DIM), dtb, acsb, acs_r, d_lanes, proj,
      gate_w.reshape(1, SSM_D_INNER), *ride)


def _ssd_bwd(xbc, dtb, acsb, dtr, acs_r, a_log, d_lanes, hin, dact, y, proj, gate_w, name):
    l = xbc.shape[0]
    nc = l // SSM_CHUNK

    def body(x_ref, b_ref, c_ref, dtb_ref, acsb_ref, dtr_ref, acsr_ref, alc_ref, d_ref, hin_ref,
             dact_ref, y_ref, z_ref, gw_ref,
             dxbc_ref, ddt_ref, dal_ref, dd_ref, dproj_ref, dgw_ref, dh_ref, dy_ref, acc_ref):
        c = pl.program_id(0)
        dx_ref = dxbc_ref.at[:, 0:SSM_D_INNER]
        db_ref = dxbc_ref.at[:, SSM_D_INNER:SSM_D_INNER + SSM_BC_DIM]
        dc_ref = dxbc_ref.at[:, SSM_D_INNER + SSM_BC_DIM:SSM_CONV_DIM]

        @pl.when(c == 0)
        def _():
            dal_ref[...] = jnp.zeros_like(dal_ref)
            dd_ref[...] = jnp.zeros_like(dd_ref)
            acc_ref[...] = jnp.zeros_like(acc_ref)

        z = z_ref[...]
        yv = y_ref[...]
        s = _sigmoid(z)
        sz = z * s
        yg = yv * sz
        r = lax.rsqrt(jnp.mean(yg * yg, axis=-1, keepdims=True) + EPS)
        nrm = yg * r
        gv = dact_ref[...]
        gw = gv * gw_ref[...]
        dyg = r * (gw - nrm * jnp.mean(gw * nrm, axis=-1, keepdims=True))
        dy_ref[...] = dyg * sz
        dproj_ref[...] = (dyg * yv * (s * (1.0 + z * (1.0 - s)))).astype(dproj_ref.dtype)
        acc_ref[...] += jnp.sum((gv * nrm).reshape(SSM_CHUNK // 8, 8, SSM_D_INNER), axis=0)

        @pl.when(c == nc - 1)
        def _():
            dgw_ref[...] = jnp.sum(acc_ref[...], axis=0, keepdims=True)

        for k in range(GPS):
            one_group(c, k, k, x_ref, b_ref, c_ref, dtb_ref, acsb_ref, dtr_ref, acsr_ref, alc_ref, d_ref,
                      hin_ref, dy_ref, dx_ref, db_ref, dc_ref, ddt_ref, dal_ref, dd_ref, dh_ref)

    def one_group(c, g, k, x_ref, b_ref, c_ref, dtb_ref, acsb_ref, dtr_ref, acsr_ref, alc_ref, d_ref, hin_ref, dy_ref,
                  dx_ref, db_ref, dc_ref, ddt_ref, dal_ref, dd_ref, dh_ref):
        cols = slice(k * GP, (k + 1) * GP)
        ncols = slice(k * SSM_STATE, (k + 1) * SSM_STATE)

        @pl.when(c == 0)
        def _():
            dh_ref[g] = jnp.zeros((SSM_STATE, GP), F32)

        xv = x_ref[:, cols].astype(F32)
        dyv = dy_ref[:, cols]
        bb = b_ref[:, ncols].astype(BF16)
        cb16 = c_ref[:, ncols].astype(BF16)
        dtb = dtb_ref[:, cols]
        acsb = acsb_ref[:, cols]
        dtr_v = dtr_ref[k]
        acs_r = acsr_ref[k]
        a_col = -jnp.exp(alc_ref[k])
        ri, cj = _chunk_iotas()
        head_of_lane = lax.shift_right_logical(lax.broadcasted_iota(jnp.int32, (SSM_HPG, GP), 1), HEAD_DIM_LOG2)
        ind_t = (head_of_lane == lax.broadcasted_iota(jnp.int32, (SSM_HPG, GP), 0)).astype(BF16)
        lastb = acsb[SSM_CHUNK - 1:SSM_CHUNK, :]
        ecb = jnp.exp(acsb)
        dteb = jnp.exp(lastb - acsb)
        xd = xv * dtb
        xw = xd * dteb
        cb = lax.dot_general(cb16, bb, NT_DIMS, preferred_element_type=F32)
        hin_v = hin_ref[0, k]
        dhn = dh_ref[g]
        h16 = hin_v.astype(BF16)
        dh16 = dhn.astype(BF16)
        ch = jnp.dot(cb16, h16, preferred_element_type=F32)
        bdh = jnp.dot(bb, dh16, preferred_element_type=F32)
        dym = _head_masked_rows(dyv, BF16)
        g_all = lax.dot_general(dym, xd.astype(BF16), NT_DIMS, preferred_element_type=F32)
        gl_sum = jnp.zeros((SSM_CHUNK, SSM_CHUNK), F32)
        ms, qs = [], []
        for r in range(SSM_HPG):
            decay = _head_decay(acsb, acs_r, r, ri, cj)
            gl = g_all[r * SSM_CHUNK:(r + 1) * SSM_CHUNK] * decay
            gl_sum = gl_sum + gl
            ms.append((cb * decay).astype(BF16))
            qs.append((gl * cb).astype(BF16))
        dxd = lax.dot_general(jnp.concatenate(ms, axis=0), dym, TN_DIMS, preferred_element_type=F32) + dteb * bdh
        cum = jnp.dot(jnp.concatenate(qs, axis=0), (ri < cj).astype(BF16), preferred_element_type=F32)
        sub4 = lax.broadcasted_iota(jnp.int32, (SSM_HPG, 1), 0)
        da = jnp.zeros((SSM_HPG, SSM_CHUNK), F32)
        for r in range(SSM_HPG):
            rect = jnp.sum(jnp.where(ri >= cj, cum[r * SSM_CHUNK:(r + 1) * SSM_CHUNK], 0.0), axis=0, keepdims=True)
            da = da + jnp.where(sub4 == r, rect, 0.0)
        z2 = xw * bdh
        sub8 = lax.broadcasted_iota(jnp.int32, (8, 1), 0)
        col_sums = (jnp.where(sub8 == 0, jnp.sum(z2, axis=0, keepdims=True), 0.0)
                    + jnp.where(sub8 == 1, jnp.sum(dhn * hin_v, axis=0, keepdims=True), 0.0)
                    + jnp.where(sub8 == 2, jnp.sum(dyv * xv, axis=0, keepdims=True), 0.0))
        wv = ecb * dyv
        summands = jnp.concatenate([wv * ch - z2, dxd * xv, col_sums], axis=0)
        sums = lax.dot_general(ind_t, summands.astype(BF16), NT_DIMS, preferred_element_type=F32)
        per_pos = sums[:, :2 * SSM_CHUNK]
        totals = sums[:, 2 * SSM_CHUNK:]
        e_last = totals[:, 0:1] + jnp.exp(acs_r[:, SSM_CHUNK - 1:SSM_CHUNK]) * totals[:, 1:2]
        da = (da + e_last + jnp.dot(per_pos[:, :SSM_CHUNK], (ri >= cj).astype(F32), preferred_element_type=F32,
                                    precision=lax.Precision.HIGHEST))
        ddt_ref[k] = a_col * da + per_pos[:, SSM_CHUNK:]
        dal_ref[g] += a_col * jnp.sum(da * dtr_v, axis=1, keepdims=True)
        dd_ref[g] += totals[:, 2:3]
        dx_ref[:, cols] = (dxd * dtb + d_ref[k] * dyv).astype(dx_ref.dtype)
        w16 = wv.astype(BF16)
        xw16 = xw.astype(BF16)
        gl16 = gl_sum.astype(BF16)
        dc_ref[:, ncols] = (jnp.dot(gl16, bb, preferred_element_type=F32)
                            + lax.dot_general(w16, h16, NT_DIMS, preferred_element_type=F32)).astype(dc_ref.dtype)
        db_ref[:, ncols] = (lax.dot_general(gl16, cb16, TN_DIMS, preferred_element_type=F32)
                            + lax.dot_general(xw16, dh16, NT_DIMS, preferred_element_type=F32)).astype(db_ref.dtype)
        dh_ref[g] = dhn * jnp.exp(lastb) + lax.dot_general(cb16, w16, TN_DIMS, preferred_element_type=F32)

    def rev(c):
        return nc - 1 - c

    small = pl.BlockSpec((SSM_GROUPS, SSM_HPG, 1), lambda c: (0, 0, 0))
    lanes = pl.BlockSpec((SSM_CHUNK, SSM_D_INNER), lambda c: (rev(c), 0))
    rows = pl.BlockSpec((SSM_GROUPS, SSM_HPG, SSM_CHUNK), lambda c: (0, 0, rev(c)))
    vec = pl.BlockSpec((1, SSM_D_INNER), lambda c: (0, 0))
    return pl.pallas_call(
        body, grid=(nc,),
        in_specs=[lanes,
                  pl.BlockSpec((SSM_CHUNK, SSM_BC_DIM), lambda c: (rev(c), B_BLOCK0 // GPS)),
                  pl.BlockSpec((SSM_CHUNK, SSM_BC_DIM), lambda c: (rev(c), C_BLOCK0 // GPS)),
                  lanes, lanes, rows, rows,
                  pl.BlockSpec((SSM_GROUPS, SSM_HPG, 1), lambda c: (0, 0, 0)),
                  pl.BlockSpec((SSM_GROUPS, 1, GP), lambda c: (0, 0, 0)),
                  pl.BlockSpec((1, SSM_GROUPS, SSM_STATE, GP), lambda c: (rev(c), 0, 0, 0)),
                  lanes, lanes, lanes, vec],
        out_specs=[pl.BlockSpec((SSM_CHUNK, SSM_CONV_DIM), lambda c: (rev(c), 0)),
                   rows, small, small, lanes, vec],
        out_shape=[jax.ShapeDtypeStruct((l, SSM_CONV_DIM), BF16), jax.ShapeDtypeStruct((SSM_GROUPS, SSM_HPG, l), F32),
                   jax.ShapeDtypeStruct((SSM_GROUPS, SSM_HPG, 1), F32),
                   jax.ShapeDtypeStruct((SSM_GROUPS, SSM_HPG, 1), F32),
                   jax.ShapeDtypeStruct((l, SSM_IN_PAD), BF16), jax.ShapeDtypeStruct((1, SSM_D_INNER), F32)],
        scratch_shapes=[pltpu.VMEM((SSM_GROUPS, SSM_STATE, GP), F32), pltpu.VMEM((SSM_CHUNK, SSM_D_INNER), F32),
                        pltpu.VMEM((8, SSM_D_INNER), F32)],
        compiler_params=_params("arbitrary"), name=name,
    )(xbc, xbc, xbc, dtb, acsb, dtr, acs_r, a_log.reshape(SSM_GROUPS, SSM_HPG, 1), d_lanes, hin, dact, y, proj,
      gate_w.reshape(1, SSM_D_INNER))


LANES = 128
ROPE_Q_CHUNKS = ATT_WIDTH // LANES
ROPE_K_CHUNKS = ATT_KV_WIDTH // LANES


def _rope_tables(positions):
    inv = ROPE_THETA ** (-jnp.arange(0, ROPE_DIM, 2, dtype=F32) / ROPE_DIM)
    ang = positions.astype(F32)[:, None] * inv
    cos, sin = jnp.cos(ang), jnp.sin(ang)
    l = positions.shape[0]
    rest = ATT_HEAD_DIM - ROPE_DIM
    ones, zeros = jnp.ones((l, rest), F32), jnp.zeros((l, rest), F32)
    z8 = jnp.zeros((l, ROPE_HALF), F32)
    cos_f = jnp.concatenate([cos, cos, ones], axis=1)
    sin_a = jnp.concatenate([-sin, z8, zeros], axis=1)
    sin_b = jnp.concatenate([z8, sin, zeros], axis=1)
    reps = LANES // ATT_HEAD_DIM
    return tuple(jnp.tile(t, (1, reps)) for t in (cos_f, sin_a, sin_b))


ATT_QKV4 = 3 * ATT_WIDTH


def _both_halves(chunk):
    lane = lax.broadcasted_iota(jnp.int32, (1, LANES), 1)
    swapped = pltpu.roll(chunk, ATT_HEAD_DIM, 1)
    return jnp.where(lane < ATT_HEAD_DIM, chunk, swapped), jnp.where(lane < ATT_HEAD_DIM, swapped, chunk)


def _rope_fwd(proj, tables, name):
    l = proj.shape[0]
    tl = _pick(l, (256, 128))

    def body(p_ref, c_ref, sa_ref, sb_ref, o_ref):
        cos_f, sin_a, sin_b = c_ref[...], sa_ref[...], sb_ref[...]

        def rope(t):
            return t * cos_f + pltpu.roll(t, LANES - ROPE_HALF, 1) * sin_a + pltpu.roll(t, ROPE_HALF, 1) * sin_b

        for k in range(ROPE_Q_CHUNKS):
            sl = slice(k * LANES, (k + 1) * LANES)
            o_ref[:, sl] = (rope(p_ref[:, sl]) * Q_SCALE).astype(o_ref.dtype)
        for part in range(2):
            for k in range(ROPE_K_CHUNKS):
                src = ATT_WIDTH + part * ATT_KV_WIDTH + k * LANES
                t = p_ref[:, src:src + LANES]
                if part == 0:
                    t = rope(t)
                for head, dup in enumerate(_both_halves(t.astype(o_ref.dtype))):
                    dst = (1 + part) * ATT_WIDTH + (2 * k + head) * ATT_GQA * ATT_HEAD_DIM
                    o_ref[:, dst:dst + LANES] = dup
                    o_ref[:, dst + LANES:dst + 2 * LANES] = dup

    tab = pl.BlockSpec((tl, LANES), lambda i: (i, 0))
    return pl.pallas_call(
        body, grid=(l // tl,), in_specs=[pl.BlockSpec((tl, ATT_IN_DIM), lambda i: (i, 0)), tab, tab, tab],
        out_specs=pl.BlockSpec((tl, ATT_QKV4), lambda i: (i, 0)),
        out_shape=jax.ShapeDtypeStruct((l, ATT_QKV4), BF16), compiler_params=_params("parallel"), name=name,
    )(proj, *tables)


def _rope_bwd(dq, dk4, dv4, dgate, tables, name):
    l = dq.shape[0]
    tl = _pick(l, (256, 128))

    def body(dq_ref, dk_ref, dv_ref, dg_ref, c_ref, sa_ref, sb_ref, o_ref):
        cos_f, sin_a, sin_b = c_ref[...], sa_ref[...], sb_ref[...]
        lane = lax.broadcasted_iota(jnp.int32, (1, LANES), 1)

        def unrope(t):
            return t * cos_f + pltpu.roll(t * sin_a, ROPE_HALF, 1) + pltpu.roll(t * sin_b, LANES - ROPE_HALF, 1)

        def head_total(ref, kvh):
            base = kvh * ATT_GQA * ATT_HEAD_DIM
            s = ref[:, base:base + LANES] + ref[:, base + LANES:base + 2 * LANES]
            return s + pltpu.roll(s, ATT_HEAD_DIM, 1)

        for k in range(ROPE_Q_CHUNKS):
            sl = slice(k * LANES, (k + 1) * LANES)
            o_ref[:, sl] = unrope(dq_ref[:, sl] * Q_SCALE).astype(o_ref.dtype)
        for k in range(ROPE_K_CHUNKS):
            dk = jnp.where(lane < ATT_HEAD_DIM, head_total(dk_ref, 2 * k), head_total(dk_ref, 2 * k + 1))
            dv = jnp.where(lane < ATT_HEAD_DIM, head_total(dv_ref, 2 * k), head_total(dv_ref, 2 * k + 1))
            o_ref[:, ATT_WIDTH + k * LANES:ATT_WIDTH + (k + 1) * LANES] = unrope(dk).astype(o_ref.dtype)
            at = ATT_WIDTH + ATT_KV_WIDTH + k * LANES
            o_ref[:, at:at + LANES] = dv.astype(o_ref.dtype)
        o_ref[:, ATT_QKV:ATT_IN_DIM] = dg_ref[...].astype(o_ref.dtype)

    tab = pl.BlockSpec((tl, LANES), lambda i: (i, 0))
    wide = pl.BlockSpec((tl, ATT_WIDTH), lambda i: (i, 0))
    return pl.pallas_call(
        body, grid=(l // tl,), in_specs=[wide, wide, wide, wide, tab, tab, tab],
        out_specs=pl.BlockSpec((tl, ATT_IN_DIM), lambda i: (i, 0)),
        out_shape=jax.ShapeDtypeStruct((l, ATT_IN_DIM), BF16), compiler_params=_params("parallel"), name=name,
    )(dq, dk4, dv4, dgate, *tables)


GATE_HALF = ATT_WIDTH // 2
GATE_COL_BLOCK = ATT_QKV // GATE_HALF


ATT_STACK = ATT_GQA * ATT_BLOCK
BLOCK_LOG2 = ATT_BLOCK.bit_length() - 1


def _stack_masks(n):
    ri = lax.broadcasted_iota(jnp.int32, (ATT_STACK, ATT_BLOCK), 0) & (ATT_BLOCK - 1)
    cj = lax.broadcasted_iota(jnp.int32, (ATT_STACK, ATT_BLOCK), 1)
    return (cj > ri) & (n > 0), cj <= ri


def _stack_sinks(sink_ref, kvh):
    blk = lax.shift_right_logical(lax.broadcasted_iota(jnp.int32, (ATT_STACK, 1), 0), BLOCK_LOG2)
    col = jnp.zeros((ATT_STACK, 1), F32)
    for r in range(ATT_GQA):
        col = jnp.where(blk == r, sink_ref[kvh * ATT_GQA + r], col)
    return col


def _stack_fold(stack):
    head_of_lane = lax.shift_right_logical(lax.broadcasted_iota(jnp.int32, (1, GP), 1), HEAD_DIM_LOG2)
    out = jnp.zeros((ATT_BLOCK, GP), F32)
    for r in range(ATT_GQA):
        out = jnp.where(head_of_lane == r, stack[r * ATT_BLOCK:(r + 1) * ATT_BLOCK], out)
    return out


def _attn_fwd(qkv, proj, sinks, name):
    l = qkv.shape[0]
    nb = l // ATT_BLOCK

    def body(sink_ref, q_ref, kp_ref, kc_ref, vp_ref, vc_ref, g0_ref, g1_ref, og_ref, o_ref, lse_ref):
        n = pl.program_id(0)
        mask_p, mask_c = _stack_masks(n)
        ones = jnp.ones((ATT_BLOCK, LANES), BF16)
        for kvh in range(ATT_KV_HEADS):
            cols = slice(kvh * GP, (kvh + 1) * GP)
            q_stack = _head_masked_rows(q_ref[:, cols], BF16)
            sp = jnp.where(mask_p, lax.dot_general(q_stack, kp_ref[:, cols], NT_DIMS, preferred_element_type=F32), NEG_INF)
            sc = jnp.where(mask_c, lax.dot_general(q_stack, kc_ref[:, cols], NT_DIMS, preferred_element_type=F32), NEG_INF)
            sink = _stack_sinks(sink_ref, kvh)
            m = jnp.maximum(jnp.max(jnp.maximum(sp, sc), axis=1, keepdims=True), sink)
            pp = jnp.exp(sp - m).astype(BF16)
            pc = jnp.exp(sc - m).astype(BF16)
            acc = (jnp.dot(pp, jnp.concatenate([vp_ref[:, cols], ones], axis=1), preferred_element_type=F32)
                   + jnp.dot(pc, jnp.concatenate([vc_ref[:, cols], ones], axis=1), preferred_element_type=F32))
            den = acc[:, GP:] + jnp.exp(sink - m)
            inv = 1.0 / den
            o_ref[:, cols] = _stack_fold(acc[:, :GP] * jnp.concatenate([inv, inv], axis=1))
            lse = m + jnp.log(den)
            lse_ref[:, cols] = _stack_fold(jnp.concatenate([lse, lse], axis=1))
        for half, g_ref in enumerate((g0_ref, g1_ref)):
            sl = slice(half * GATE_HALF, (half + 1) * GATE_HALF)
            gate = g_ref[...]
            og_ref[:, sl] = (o_ref[:, sl] * (gate * _sigmoid(gate))).astype(og_ref.dtype)

    def prev(n):
        return jnp.maximum(n - 1, 0)

    wide = pl.BlockSpec((ATT_BLOCK, ATT_WIDTH), lambda n: (n, 0))
    return pl.pallas_call(
        body, grid=(nb,),
        in_specs=[pl.BlockSpec(memory_space=pltpu.SMEM), wide,
                  pl.BlockSpec((ATT_BLOCK, ATT_WIDTH), lambda n: (prev(n), 1)),
                  pl.BlockSpec((ATT_BLOCK, ATT_WIDTH), lambda n: (n, 1)),
                  pl.BlockSpec((ATT_BLOCK, ATT_WIDTH), lambda n: (prev(n), 2)),
                  pl.BlockSpec((ATT_BLOCK, ATT_WIDTH), lambda n: (n, 2)),
                  pl.BlockSpec((ATT_BLOCK, GATE_HALF), lambda n: (n, GATE_COL_BLOCK)),
                  pl.BlockSpec((ATT_BLOCK, GATE_HALF), lambda n: (n, GATE_COL_BLOCK + 1))],
        out_specs=[wide, wide, wide],
        out_shape=[jax.ShapeDtypeStruct((l, ATT_WIDTH), BF16), jax.ShapeDtypeStruct((l, ATT_WIDTH), F32),
                   jax.ShapeDtypeStruct((l, ATT_WIDTH), F32)],
        compiler_params=_params("parallel"), name=name,
    )(sinks, qkv, qkv, qkv, qkv, qkv, proj, proj)


def _attn_bwd(qkv, proj, sinks, o, lse, dog, name, ride=()):
    l = qkv.shape[0]
    nb = l // ATT_BLOCK
    n_ride = len(ride)

    def body(*refs):
        sink_ref, q_ref, kp_ref, kc_ref, vp_ref, vc_ref, g0_ref, g1_ref, o_ref, lse_ref, dog_ref = refs[:11]
        ride_in = refs[11:11 + n_ride]
        dq_ref, dk_ref, dv_ref, dg_ref, ds_ref = refs[11 + n_ride:16 + n_ride]
        ride_out = refs[16 + n_ride:16 + 2 * n_ride]
        ck_ref, cv_ref, do_ref = refs[16 + 2 * n_ride:19 + 2 * n_ride]
        ride_sems = refs[19 + 2 * n_ride:]
        n = pl.program_id(0)

        @pl.when(n == 0)
        def _():
            ds_ref[...] = jnp.zeros_like(ds_ref)
            ck_ref[...] = jnp.zeros_like(ck_ref)
            cv_ref[...] = jnp.zeros_like(cv_ref)
            if n_ride:
                _scatter_between_chips(ride_in, ride_out, *ride_sems, wait=False)

        @pl.when(n == nb)
        def _():
            dk_ref[...] = ck_ref[...]
            dv_ref[...] = cv_ref[...]
            if n_ride:
                _scatter_between_chips(ride_in, ride_out, *ride_sems, wait=True)

        @pl.when(n < nb)
        def _():
            mask_p, mask_c = _stack_masks(n)
            lane = lax.broadcasted_iota(jnp.int32, (1, ATT_Q_HEADS), 1)
            for half, g_ref in enumerate((g0_ref, g1_ref)):
                sl = slice(half * GATE_HALF, (half + 1) * GATE_HALF)
                gate = g_ref[...]
                s = _sigmoid(gate)
                dogv = dog_ref[:, sl]
                do_ref[:, sl] = dogv * (gate * s)
                dg_ref[:, sl] = dogv * o_ref[:, sl] * (s * (1.0 + gate * (1.0 - s)))
            ds_acc = jnp.zeros((1, ATT_Q_HEADS), F32)
            for kvh in range(ATT_KV_HEADS):
                cols = slice(kvh * GP, (kvh + 1) * GP)
                kp, kc, vp, vc = kp_ref[:, cols], kc_ref[:, cols], vp_ref[:, cols], vc_ref[:, cols]
                q_stack = _head_masked_rows(q_ref[:, cols], BF16)
                do_g = do_ref[:, cols]
                do_stack = _head_masked_rows(do_g, BF16)
                lse_g = lse_ref[:, cols]
                lse_stack = jnp.concatenate(
                    [_both_halves(lse_g[:, (r // 2) * LANES:(r // 2 + 1) * LANES])[r % 2] for r in range(ATT_GQA)], axis=0)
                pp = jnp.exp(jnp.where(
                    mask_p, lax.dot_general(q_stack, kp, NT_DIMS, preferred_element_type=F32) - lse_stack, NEG_INF))
                pc = jnp.exp(jnp.where(
                    mask_c, lax.dot_general(q_stack, kc, NT_DIMS, preferred_element_type=F32) - lse_stack, NEG_INF))
                dpp = lax.dot_general(do_stack, vp, NT_DIMS, preferred_element_type=F32)
                dpc = lax.dot_general(do_stack, vc, NT_DIMS, preferred_element_type=F32)
                delta = jnp.sum(pp * dpp + pc * dpc, axis=1, keepdims=True)
                dsp = (pp * (dpp - delta)).astype(BF16)
                dsc = (pc * (dpc - delta)).astype(BF16)
                dq_ref[:, cols] = _stack_fold(jnp.dot(dsp, kp, preferred_element_type=F32)
                                              + jnp.dot(dsc, kc, preferred_element_type=F32))
                dk_ref[:, cols] = ck_ref[:, cols] + lax.dot_general(dsp, q_stack, TN_DIMS, preferred_element_type=F32)
                dv_ref[:, cols] = cv_ref[:, cols] + lax.dot_general(pp.astype(BF16), do_stack, TN_DIMS,
                                                                    preferred_element_type=F32)
                ck_ref[:, cols] = lax.dot_general(dsc, q_stack, TN_DIMS, preferred_element_type=F32)
                cv_ref[:, cols] = lax.dot_general(pc.astype(BF16), do_stack, TN_DIMS, preferred_element_type=F32)
                t = jnp.exp(_stack_sinks(sink_ref, kvh) - lse_stack) * delta
                for r in range(ATT_GQA):
                    tot = jnp.sum(t[r * ATT_BLOCK:(r + 1) * ATT_BLOCK], axis=0, keepdims=True)
                    ds_acc = ds_acc - jnp.where(lane == kvh * ATT_GQA + r, tot[:, :ATT_Q_HEADS], 0.0)
            ds_ref[...] += ds_acc

    def cur(n):
        return jnp.minimum(n, nb - 1)

    def prev(n):
        return jnp.maximum(n - 1, 0)

    wide = pl.BlockSpec((ATT_BLOCK, ATT_WIDTH), lambda n: (cur(n), 0))
    late = pl.BlockSpec((ATT_BLOCK, ATT_WIDTH), lambda n: (prev(n), 0))
    return pl.pallas_call(
        body, grid=(nb + 1,),
        in_specs=[pl.BlockSpec(memory_space=pltpu.SMEM), wide,
                  pl.BlockSpec((ATT_BLOCK, ATT_WIDTH), lambda n: (prev(cur(n)), 1)),
                  pl.BlockSpec((ATT_BLOCK, ATT_WIDTH), lambda n: (cur(n), 1)),
                  pl.BlockSpec((ATT_BLOCK, ATT_WIDTH), lambda n: (prev(cur(n)), 2)),
                  pl.BlockSpec((ATT_BLOCK, ATT_WIDTH), lambda n: (cur(n), 2)),
                  pl.BlockSpec((ATT_BLOCK, GATE_HALF), lambda n: (cur(n), GATE_COL_BLOCK)),
                  pl.BlockSpec((ATT_BLOCK, GATE_HALF), lambda n: (cur(n), GATE_COL_BLOCK + 1)),
                  wide, wide, wide] + [ANY] * n_ride,
        out_specs=[wide, late, late, wide, pl.BlockSpec((1, ATT_Q_HEADS), lambda n: (0, 0))] + [ANY] * n_ride,
        out_shape=[jax.ShapeDtypeStruct((l, ATT_WIDTH), F32), jax.ShapeDtypeStruct((l, ATT_WIDTH), F32),
                   jax.ShapeDtypeStruct((l, ATT_WIDTH), F32), jax.ShapeDtypeStruct((l, ATT_WIDTH), F32),
                   jax.ShapeDtypeStruct((1, ATT_Q_HEADS), F32)] + _scatter_shapes(ride),
        scratch_shapes=[pltpu.VMEM((ATT_BLOCK, ATT_WIDTH), F32), pltpu.VMEM((ATT_BLOCK, ATT_WIDTH), F32),
                        pltpu.VMEM((ATT_BLOCK, ATT_WIDTH), F32)] + (_gather_sems(n_ride) if n_ride else []),
        compiler_params=_params("arbitrary"), name=name,
    )(sinks, qkv, qkv, qkv, qkv, qkv, proj, proj, o, lse, dog, *ride)


def _local_step(x, positions, pre_norm, post_norm, conv_b, dt_bias, a_log, d_skip, gate_norm, sinks, target,
                first_in, in_proj_with_first_pair, scan_with_second_pair, attn_bwd_with_second_pair_grads,
                in_dx_with_first_pair_grads):
    tables = _rope_tables(positions)
    dt_bias_pad = jnp.pad(dt_bias, ((0, 0), (0, SSM_DT_PAD - SSM_HEADS)))
    d_lanes = jnp.repeat(d_skip, SSM_HEAD_DIM, axis=1).reshape(-1, SSM_GROUPS, 1, GP)
    a_log_pad = jnp.pad(a_log, ((0, 0), (0, SSM_DT_PAD - SSM_HEADS)))
    pairs = [first_in, None]
    saved = []
    cur = x
    h = _rmsnorm_fwd(cur, pre_norm[0], "prenorm_fwd_0")
    for i in range(DEPTH):
        j = i // 2
        if i % 2 == 0:
            in_proj = functools.partial(_matmul, h, pairs[j]["ssm_w_in"], "nn", F32, f"ssm_in_{i}")
            if i == 0:
                proj, rest = in_proj_with_first_pair(in_proj)
                pairs[0] = {**first_in, **rest}
            else:
                proj = in_proj()
            dtb, acsb, dtr, acs_r = _ssd_prep(proj, dt_bias_pad[j:j + 1], a_log_pad[j:j + 1], f"ssd_prep_{i}")
            scan = functools.partial(_ssd_fwd, proj, pairs[j]["ssm_conv_w"], conv_b[j], dtb, acsb, acs_r, d_lanes[j],
                                     gate_norm[j], f"ssd_fwd_{i}")
            if i == 0:
                y, act, hin, pre, xbc, pairs[1] = scan_with_second_pair(scan)
            else:
                y, act, hin, pre, xbc = scan()
            w_ssm_in = [p["ssm_w_in"] for p in pairs]
            w_ssm_out = [p["ssm_w_out"] for p in pairs]
            w_att_in = [p["att_w_in"] for p in pairs]
            w_att_out = [p["att_w_out"] for p in pairs]
            conv_w = [p["ssm_conv_w"] for p in pairs]
            ymix = _matmul(act, w_ssm_out[j], "nn", F32, f"ssm_out_{i}")
            saved.append(dict(x=cur, h=h, proj=proj, pre=pre, xbc=xbc, dtb=dtb, acsb=acsb, dtr=dtr, acs_r=acs_r, y=y,
                              hin=hin, act=act, ymix=ymix))
        else:
            proj = _matmul(h, w_att_in[j], "nn", F32, f"att_in_{i}")
            qkv = _rope_fwd(proj, tables, f"rope_fwd_{i}")
            act, o, lse = _attn_fwd(qkv, proj, sinks[j], f"attn_fwd_{i}")
            ymix = _matmul(act, w_att_out[j], "nn", F32, f"att_out_{i}")
            saved.append(dict(x=cur, h=h, proj=proj, qkv=qkv, o=o, lse=lse, act=act, ymix=ymix))
        if i + 1 < DEPTH:
            cur, h = _post_fwd(cur, ymix, post_norm[i], pre_norm[i + 1], f"post_fwd_{i}")

    gr = {k: [None] * 2 for k in ("ssm_w_in", "ssm_conv_w", "ssm_conv_b", "ssm_dt_bias", "ssm_a_log", "ssm_d",
                                  "ssm_gate_norm", "ssm_w_out", "att_w_in", "att_sinks", "att_w_out")}
    gr["pre_norm"] = [None] * DEPTH
    gr["post_norm"] = [None] * DEPTH
    last = DEPTH - 1
    g, dymix, loss_lanes, gr["post_norm"][last] = _post_loss(cur, ymix, post_norm[last], target, "post_loss")
    for i in reversed(range(DEPTH)):
        j = i // 2
        s = saved[i]
        if i % 2 == 0:
            dact = _matmul(dymix, w_ssm_out[j], "nt", F32, f"ssm_out_dx_{i}")
            gr["ssm_w_out"][j] = _matmul(s["act"], dymix, "tn", F32, f"ssm_out_dw_{i}")
            dxbc, ddt8, dal, dd, dproj, gr["ssm_gate_norm"][j] = _ssd_bwd(
                s["xbc"], s["dtb"], s["acsb"], s["dtr"], s["acs_r"], a_log[j], d_lanes[j], s["hin"], dact, s["y"],
                s["proj"], gate_norm[j], f"ssd_bwd_{i}")
            gr["ssm_a_log"][j] = dal.reshape(SSM_HEADS)
            gr["ssm_d"][j] = dd.reshape(SSM_HEADS)
            l = x.shape[0]
            ddt = jnp.pad(jnp.transpose(ddt8, (2, 0, 1)).reshape(l, SSM_HEADS), ((0, 0), (0, SSM_DT_PAD - SSM_HEADS)))
            dproj, dbias = _dt_bwd(ddt, s["proj"], dt_bias_pad[j:j + 1], dproj, f"dt_bwd_{i}")
            gr["ssm_dt_bias"][j] = dbias[0, :SSM_HEADS]
            dproj, gr["ssm_conv_w"][j], dcb = _conv_bwd(dxbc, s["pre"], s["proj"], conv_w[j], dproj, f"conv_bwd_{i}")
            gr["ssm_conv_b"][j] = dcb[0]
            w_in, key = w_ssm_in[j], "ssm_w_in"
        else:
            dog = _matmul(dymix, w_att_out[j], "nt", F32, f"att_out_dx_{i}")
            gr["att_w_out"][j] = _matmul(s["act"], dymix, "tn", F32, f"att_out_dw_{i}")
            attn_bwd = functools.partial(_attn_bwd, s["qkv"], s["proj"], sinks[j], s["o"], s["lse"], dog, f"attn_bwd_{i}")
            if i == 1:
                (dq, dk, dv, dgate, dsk), second_pair_reduced = attn_bwd_with_second_pair_grads(
                    attn_bwd, {k: gr[k][1] for k in BIG})
            else:
                dq, dk, dv, dgate, dsk = attn_bwd()
            gr["att_sinks"][j] = dsk[0]
            dproj = _rope_bwd(dq, dk, dv, dgate, tables, f"rope_bwd_{i}")
            w_in, key = w_att_in[j], "att_w_in"
        gr[key][j] = _matmul(s["h"], dproj, "tn", F32, f"in_dw_{i}")
        in_dx = functools.partial(_matmul, dproj, w_in, "nt", F32, f"in_dx_{i}")
        if i == 0:
            dh, first_pair_reduced = in_dx_with_first_pair_grads(in_dx, {k: gr[k][0] for k in BIG})
        else:
            dh = in_dx()
        if i > 0:
            g, dymix, gr["pre_norm"][i], gr["post_norm"][i - 1] = _norm_bwd_chain(
                dh, s["x"], pre_norm[i], g, saved[i - 1]["ymix"], post_norm[i - 1], f"norm_bwd_{i}")
        else:
            g, gr["pre_norm"][i] = _rmsnorm_bwd(dh, s["x"], pre_norm[i], g, f"prenorm_bwd_{i}")
    grads = {k: jnp.stack([v.reshape(v.shape[-1]) if k in ("pre_norm", "post_norm", "ssm_gate_norm") else v for v in vs])
             for k, vs in gr.items() if k not in BIG}
    return loss_lanes, g, grads, first_pair_reduced, second_pair_reduced


N_CHIPS = 4
N_DEV = 8
MESH = pl.DeviceIdType.MESH
ANY = pl.BlockSpec(memory_space=pl.ANY)


def _place():
    x, y, c = lax.axis_index("x"), lax.axis_index("y"), lax.axis_index("c")
    return x, y, c, 2 * x + y


def _gather_sems(n):
    return [pltpu.SemaphoreType.DMA((n, N_CHIPS)), pltpu.SemaphoreType.DMA((n, N_CHIPS)), pltpu.SemaphoreType.DMA((n,))]


def _gather_between_chips(ins, outs, send_sems, recv_sems, local_sems, wait):
    n = len(ins)
    _, _, c, s = _place()
    local = [pltpu.make_async_copy(ins[w], outs[w].at[s], local_sems.at[w]) for w in range(n)]

    def remote(w, t):
        return pltpu.make_async_remote_copy(
            src_ref=ins[w].at[c], dst_ref=outs[w].at[s, c], send_sem=send_sems.at[w, t],
            recv_sem=recv_sems.at[w, s], device_id=(t // 2, t % 2, c), device_id_type=MESH)

    def arrival(w, t):
        return pltpu.make_async_remote_copy(
            src_ref=ins[w].at[c], dst_ref=outs[w].at[t, c], send_sem=send_sems.at[w, t],
            recv_sem=recv_sems.at[w, t], device_id=(t // 2, t % 2, c), device_id_type=MESH)

    if not wait:
        for cp in local:
            cp.start()
    for t in range(N_CHIPS):
        @pl.when(s != t)
        def _():
            for w in range(n):
                if wait:
                    remote(w, t).wait_send()
                    arrival(w, t).wait_recv()
                else:
                    remote(w, t).start()
    if wait:
        for cp in local:
            cp.wait()


def _pair_handoff(bufs, name):
    n = len(bufs)

    def body(*refs):
        outs = refs[n:2 * n]
        send_sems, recv_sems = refs[2 * n:]
        x, y, c, s = _place()

        def handed_on(w, t):
            return pltpu.make_async_remote_copy(
                src_ref=outs[w].at[t, c], dst_ref=outs[w].at[t, c], send_sem=send_sems.at[w, t],
                recv_sem=recv_sems.at[w, t], device_id=(x, y, 1 - c), device_id_type=MESH)

        def handed_in(w, t):
            return pltpu.make_async_remote_copy(
                src_ref=outs[w].at[t, 1 - c], dst_ref=outs[w].at[t, 1 - c], send_sem=send_sems.at[w, t],
                recv_sem=recv_sems.at[w, t], device_id=(x, y, 1 - c), device_id_type=MESH)

        for t in range(N_CHIPS):
            @pl.when(s != t)
            def _():
                for w in range(n):
                    handed_on(w, t).start()
        for t in range(N_CHIPS):
            @pl.when(s != t)
            def _():
                for w in range(n):
                    handed_on(w, t).wait_send()
                    handed_in(w, t).wait_recv()

    return pl.pallas_call(
        body, in_specs=[ANY] * n, out_specs=[ANY] * n,
        out_shape=[jax.ShapeDtypeStruct(a.shape, a.dtype) for a in bufs],
        scratch_shapes=[pltpu.SemaphoreType.DMA((n, N_CHIPS)), pltpu.SemaphoreType.DMA((n, N_CHIPS))],
        input_output_aliases={w: w for w in range(n)}, name=name,
    )(*bufs)


def _chip_gather(shards, name):
    n = len(shards)

    def body(*refs):
        ins, outs = refs[:n], refs[n:2 * n]
        _gather_between_chips(ins, outs, *refs[2 * n:], wait=False)
        _gather_between_chips(ins, outs, *refs[2 * n:], wait=True)

    bufs = pl.pallas_call(
        body, in_specs=[ANY] * n, out_specs=[ANY] * n,
        out_shape=[jax.ShapeDtypeStruct((N_CHIPS,) + a.shape, a.dtype) for a in shards],
        scratch_shapes=_gather_sems(n), name=name,
    )(*shards)
    return _pair_handoff(bufs, name + "_handoff")


def _pair_swap(parts, name):
    n = len(parts)

    def body(*refs):
        ins, outs = refs[:n], refs[n:2 * n]
        send_sems, recv_sems = refs[2 * n:]
        x, y, c, _ = _place()
        cps = [pltpu.make_async_remote_copy(
            src_ref=ins[w].at[1 - c], dst_ref=outs[w], send_sem=send_sems.at[w], recv_sem=recv_sems.at[w],
            device_id=(x, y, 1 - c), device_id_type=MESH) for w in range(n)]
        for cp in cps:
            cp.start()
        for cp in cps:
            cp.wait()

    return pl.pallas_call(
        body, in_specs=[ANY] * n, out_specs=[ANY] * n,
        out_shape=[jax.ShapeDtypeStruct(a.shape[1:], a.dtype) for a in parts],
        scratch_shapes=[pltpu.SemaphoreType.DMA((n,)), pltpu.SemaphoreType.DMA((n,))],
        name=name,
    )(*parts)


def _scatter_between_chips(ins, outs, send_sems, recv_sems, local_sems, wait):
    n = len(ins)
    _, _, c, s = _place()

    def block(w, t):
        rows = ins[w].shape[0] // N_CHIPS
        return ins[w].at[pl.ds(t * rows, rows)]

    local = [pltpu.make_async_copy(block(w, s), outs[w].at[s], local_sems.at[w]) for w in range(n)]

    def remote(w, t):
        return pltpu.make_async_remote_copy(
            src_ref=block(w, t), dst_ref=outs[w].at[s], send_sem=send_sems.at[w, t], recv_sem=recv_sems.at[w, s],
            device_id=(t // 2, t % 2, c), device_id_type=MESH)

    def arrival(w, t):
        return pltpu.make_async_remote_copy(
            src_ref=block(w, t), dst_ref=outs[w].at[t], send_sem=send_sems.at[w, t], recv_sem=recv_sems.at[w, t],
            device_id=(t // 2, t % 2, c), device_id_type=MESH)

    if not wait:
        for cp in local:
            cp.start()
    for t in range(N_CHIPS):
        @pl.when(s != t)
        def _():
            for w in range(n):
                if wait:
                    remote(w, t).wait_send()
                    arrival(w, t).wait_recv()
                else:
                    remote(w, t).start()
    if wait:
        for cp in local:
            cp.wait()


def _scatter_shapes(parts):
    return [jax.ShapeDtypeStruct((N_CHIPS, a.shape[0] // N_CHIPS, a.shape[1]), a.dtype) for a in parts]


def _pair_merge(parts, name):
    n = len(parts)

    def body(*refs):
        ins, outs = refs[:n], refs[n:2 * n]
        send_sems, recv_sems = refs[2 * n:]
        x, y, c, _ = _place()
        cps = [pltpu.make_async_remote_copy(
            src_ref=ins[w], dst_ref=outs[w], send_sem=send_sems.at[w], recv_sem=recv_sems.at[w],
            device_id=(x, y, 1 - c), device_id_type=MESH) for w in range(n)]
        for cp in cps:
            cp.start()
        for cp in cps:
            cp.wait()

    return pl.pallas_call(
        body, in_specs=[ANY] * n, out_specs=[ANY] * n,
        out_shape=[jax.ShapeDtypeStruct(a.shape, a.dtype) for a in parts],
        scratch_shapes=[pltpu.SemaphoreType.DMA((n,)), pltpu.SemaphoreType.DMA((n,))],
        name=name,
    )(*parts)


def _all_gather_small(a, name):
    def body(in_ref, out_ref, send_sems, recv_sems, local_sem):
        x, y, c, _ = _place()
        me = 4 * x + 2 * y + c
        local = pltpu.make_async_copy(in_ref, out_ref.at[me], local_sem)
        local.start()

        def remote(d):
            return pltpu.make_async_remote_copy(
                src_ref=in_ref, dst_ref=out_ref.at[me], send_sem=send_sems.at[d], recv_sem=recv_sems.at[me],
                device_id=(d // 4, (d // 2) % 2, d % 2), device_id_type=MESH)

        def arrival(d):
            return pltpu.make_async_remote_copy(
                src_ref=in_ref, dst_ref=out_ref.at[d], send_sem=send_sems.at[d], recv_sem=recv_sems.at[d],
                device_id=(d // 4, (d // 2) % 2, d % 2), device_id_type=MESH)

        for d in range(N_DEV):
            @pl.when(me != d)
            def _():
                remote(d).start()
        for d in range(N_DEV):
            @pl.when(me != d)
            def _():
                remote(d).wait_send()
                arrival(d).wait_recv()
        local.wait()

    return pl.pallas_call(
        body, in_specs=[ANY], out_specs=ANY, out_shape=jax.ShapeDtypeStruct((N_DEV,) + a.shape, a.dtype),
        scratch_shapes=[pltpu.SemaphoreType.DMA((N_DEV,)), pltpu.SemaphoreType.DMA((N_DEV,)), pltpu.SemaphoreType.DMA],
        name=name,
    )(a)


def _reduce_tile(rows):
    return _pick(rows, (256, 128, 16))


def _pair_add(full, other, layer, name):
    _, rows, cols = full.shape
    tr = _reduce_tile(rows)

    def body(layer_ref, a_ref, b_ref, o_ref):
        o_ref[...] = (a_ref[0] + b_ref[...]).astype(o_ref.dtype)

    return pl.pallas_call(
        body,
        grid_spec=pltpu.PrefetchScalarGridSpec(
            num_scalar_prefetch=1, grid=(rows // tr,),
            in_specs=[pl.BlockSpec((1, tr, cols), lambda i, lr: (lr[0], i, 0)), pl.BlockSpec((tr, cols), lambda i, lr: (i, 0))],
            out_specs=pl.BlockSpec((tr, cols), lambda i, lr: (i, 0))),
        out_shape=jax.ShapeDtypeStruct((rows, cols), BF16), compiler_params=_params("parallel"), name=name,
    )(layer, full, other)


def _sum_slots(a, name):
    n, rows, cols = a.shape
    tr = _reduce_tile(rows)

    def body(a_ref, o_ref):
        acc = a_ref[0].astype(F32)
        for k in range(1, n):
            acc = acc + a_ref[k].astype(F32)
        o_ref[...] = acc

    return pl.pallas_call(
        body, grid=(rows // tr,), in_specs=[pl.BlockSpec((n, tr, cols), lambda i: (0, i, 0))],
        out_specs=pl.BlockSpec((tr, cols), lambda i: (i, 0)),
        out_shape=jax.ShapeDtypeStruct((rows, cols), F32), compiler_params=_params("parallel"), name=name,
    )(a)


def _adamw(w, g, m, v, name):
    rows, cols = w.shape
    tr = _pick(rows, (256, 8))

    def body(w_ref, g_ref, m_ref, v_ref, d_ref, nm_ref, nv_ref):
        gv = g_ref[...]
        mn = ADAM_B1 * m_ref[...] + (1.0 - ADAM_B1) * gv
        vn = ADAM_B2 * v_ref[...] + (1.0 - ADAM_B2) * jnp.square(gv)
        m_hat = mn / (1.0 - ADAM_B1 ** ADAM_STEP)
        v_hat = vn / (1.0 - ADAM_B2 ** ADAM_STEP)
        d_ref[...] = -ADAM_LR * (m_hat / (jnp.sqrt(v_hat) + ADAM_EPS) + ADAM_WD * w_ref[...])
        nm_ref[...] = mn
        nv_ref[...] = vn

    blk = pl.BlockSpec((tr, cols), lambda i: (i, 0))
    return pl.pallas_call(
        body, grid=(rows // tr,), in_specs=[blk] * 4, out_specs=[blk] * 3,
        out_shape=[jax.ShapeDtypeStruct((rows, cols), F32)] * 3, compiler_params=_params("parallel"), name=name,
    )(w, g, m, v)


BIG = ("ssm_w_in", "ssm_w_out", "att_w_in", "att_w_out")
SHARDED = BIG + ("ssm_conv_w",)
SMALL = ("pre_norm", "post_norm", "ssm_conv_b", "ssm_dt_bias", "ssm_a_log", "ssm_d", "ssm_gate_norm", "att_sinks")
WEIGHTS = ("pre_norm", "post_norm", "ssm_w_in", "ssm_conv_w", "ssm_conv_b", "ssm_dt_bias", "ssm_a_log", "ssm_d",
           "ssm_gate_norm", "ssm_w_out", "att_w_in", "att_sinks", "att_w_out")


def _halves(a):
    return a.reshape(2, a.shape[0] // 2, a.shape[1])


def _layer_shards(j, ssm_w_in, ssm_w_out, att_w_in, att_w_out, ssm_conv_w):
    return [_halves(ssm_w_in[j].astype(BF16)), _halves(ssm_w_out[j].astype(BF16)), _halves(att_w_in[j].astype(BF16)),
            _halves(att_w_out[j].astype(BF16)), _halves(ssm_conv_w[j])]


SHARD_KEYS = ("ssm_w_in", "ssm_w_out", "att_w_in", "att_w_out", "ssm_conv_w")


def _whole_weights(keys, gathered):
    out = {}
    for k, g in zip(keys, gathered):
        g = g.reshape((N_CHIPS, 2 * g.shape[2], g.shape[3]))
        if k in ("ssm_w_out", "att_w_out"):
            out[k] = g.reshape(N_CHIPS * g.shape[1], g.shape[2])
        else:
            out[k] = jnp.transpose(g, (1, 0, 2)).reshape(g.shape[1], N_CHIPS * g.shape[2])
    if "ssm_w_in" in out:
        out["ssm_w_in"] = jnp.pad(out["ssm_w_in"], ((0, 0), (0, SSM_IN_PAD - SSM_IN_DIM)))
    return out


def _halves_by_chip(key, g):
    if key in ("ssm_w_out", "att_w_out"):
        rows = g.shape[0] // N_CHIPS
        blocks = g.reshape(N_CHIPS, 2, rows // 2, g.shape[1])
        return jnp.transpose(blocks, (1, 0, 2, 3)).reshape(2, N_CHIPS * (rows // 2), g.shape[1])
    cols = (SSM_IN_DIM if key == "ssm_w_in" else g.shape[1]) // N_CHIPS
    rows = g.shape[0]
    blocks = g[:, :N_CHIPS * cols].reshape(2, rows // 2, N_CHIPS, cols)
    return jnp.transpose(blocks, (0, 2, 1, 3)).reshape(2, N_CHIPS * (rows // 2), cols)


def _pack_small(tree, keys):
    flat = jnp.concatenate([tree[k].reshape(-1) for k in keys])
    rows = -(-flat.shape[0] // (8 * LANES)) * 8
    return jnp.pad(flat, (0, rows * LANES - flat.shape[0])).reshape(rows, LANES)


def _unpack_small(packed, shapes, keys):
    flat = packed.reshape(-1)
    out, at = {}, 0
    for k in keys:
        n = 1
        for dim in shapes[k]:
            n *= dim
        out[k] = flat[at:at + n].reshape(shapes[k])
        at += n
    return out


def kernel(x, positions, pre_norm, post_norm, ssm_w_in, ssm_conv_w, ssm_conv_b, ssm_dt_bias, ssm_a_log, ssm_d, ssm_gate_norm, ssm_w_out, att_w_in, att_sinks, att_w_out, loss_target, m_pre_norm, m_post_norm, m_ssm_w_in, m_ssm_conv_w, m_ssm_conv_b, m_ssm_dt_bias, m_ssm_a_log, m_ssm_d, m_ssm_gate_norm, m_ssm_w_out, m_att_w_in, m_att_sinks, m_att_w_out, v_pre_norm, v_post_norm, v_ssm_w_in, v_ssm_conv_w, v_ssm_conv_b, v_ssm_dt_bias, v_ssm_a_log, v_ssm_d, v_ssm_gate_norm, v_ssm_w_out, v_att_w_in, v_att_sinks, v_att_w_out):
    w = dict(pre_norm=pre_norm, post_norm=post_norm, ssm_w_in=ssm_w_in, ssm_conv_w=ssm_conv_w, ssm_conv_b=ssm_conv_b,
             ssm_dt_bias=ssm_dt_bias, ssm_a_log=ssm_a_log, ssm_d=ssm_d, ssm_gate_norm=ssm_gate_norm, ssm_w_out=ssm_w_out,
             att_w_in=att_w_in, att_sinks=att_sinks, att_w_out=att_w_out)
    m = dict(pre_norm=m_pre_norm, post_norm=m_post_norm, ssm_w_in=m_ssm_w_in, ssm_conv_w=m_ssm_conv_w, ssm_conv_b=m_ssm_conv_b,
             ssm_dt_bias=m_ssm_dt_bias, ssm_a_log=m_ssm_a_log, ssm_d=m_ssm_d, ssm_gate_norm=m_ssm_gate_norm,
             ssm_w_out=m_ssm_w_out, att_w_in=m_att_w_in, att_sinks=m_att_sinks, att_w_out=m_att_w_out)
    v = dict(pre_norm=v_pre_norm, post_norm=v_post_norm, ssm_w_in=v_ssm_w_in, ssm_conv_w=v_ssm_conv_w, ssm_conv_b=v_ssm_conv_b,
             ssm_dt_bias=v_ssm_dt_bias, ssm_a_log=v_ssm_a_log, ssm_d=v_ssm_d, ssm_gate_norm=v_ssm_gate_norm,
             ssm_w_out=v_ssm_w_out, att_w_in=v_att_w_in, att_sinks=v_att_sinks, att_w_out=v_att_w_out)
    c = lax.axis_index("c")
    chip = 2 * lax.axis_index("x") + lax.axis_index("y")

    sharded = (ssm_w_in, ssm_w_out, att_w_in, att_w_out, ssm_conv_w)
    own = [dict(zip(SHARD_KEYS, _layer_shards(j, *sharded))) for j in range(2)]
    now_keys = ("ssm_w_in", "ssm_conv_w")
    later_keys = ("ssm_w_out", "att_w_in", "att_w_out")
    first_in = _whole_weights(now_keys, _chip_gather([own[0][k] for k in now_keys], "gather_weights_0"))

    def in_proj_with_first_pair(matmul):
        proj, *arrived = matmul(ride=[own[0][k] for k in later_keys])
        return proj, _whole_weights(later_keys, _pair_handoff(arrived, "gather_weights_0_rest_handoff"))

    def scan_with_second_pair(scan):
        y, act, hin, pre, xbc, *arrived = scan(ride=[own[1][k] for k in SHARD_KEYS])
        return y, act, hin, pre, xbc, _whole_weights(SHARD_KEYS, _pair_handoff(arrived, "gather_weights_1_handoff"))

    half = jnp.reshape(c, (1,)).astype(jnp.int32)

    def reduce_begin(pair_grads, tag):
        parts = [_halves_by_chip(k, pair_grads[k]) for k in BIG]
        from_sibling = _pair_swap(parts, f"reduce_pair_swap_{tag}")
        return [_pair_add(p, o, half, f"reduce_pair_add_{tag}_{n}") for n, (p, o) in enumerate(zip(parts, from_sibling))]

    def reduce_end(by_chip, tag):
        mine = [_sum_slots(a, f"reduce_chip_sum_{tag}_{n}") for n, a in enumerate(by_chip)]
        theirs = _pair_merge(mine, f"reduce_pair_merge_{tag}")
        return {k: jnp.where(c == 0, jnp.concatenate([a, b]), jnp.concatenate([b, a])) for k, a, b in zip(BIG, mine, theirs)}

    def attn_bwd_with_second_pair_grads(attn_bwd, pair_grads):
        dq, dk, dv, dgate, dsk, *by_chip = attn_bwd(ride=reduce_begin(pair_grads, "1"))
        return (dq, dk, dv, dgate, dsk), reduce_end(by_chip, "1")

    def in_dx_with_first_pair_grads(matmul, pair_grads):
        dh, *by_chip = matmul(ride=reduce_begin(pair_grads, "0"), ride_scatters=True)
        return dh, reduce_end(by_chip, "0")

    loss_lanes, grad_x, gr, reduced_0, reduced_1 = _local_step(
        x[0], positions[0], pre_norm, post_norm, ssm_conv_b, ssm_dt_bias, ssm_a_log, ssm_d, ssm_gate_norm, att_sinks,
        loss_target[0], first_in, in_proj_with_first_pair, scan_with_second_pair, attn_bwd_with_second_pair_grads,
        in_dx_with_first_pair_grads)
    loss = lax.psum(0.5 * jnp.sum(loss_lanes) / D_MODEL, ("x", "y", "c"))
    grads = {k: jnp.stack([reduced_0[k], reduced_1[k]]) for k in BIG}

    small_keys = SMALL + ("ssm_conv_w",)
    small_shapes = {k: w[k].shape for k in SMALL}
    small_shapes["ssm_conv_w"] = gr["ssm_conv_w"].shape
    small_sum = _sum_slots(_all_gather_small(_pack_small(gr, small_keys), "reduce_small_gather"), "reduce_small_sum")
    grads.update(_unpack_small(small_sum, small_shapes, small_keys))
    conv_cols = ssm_conv_w.shape[2]
    grads["ssm_conv_w"] = lax.dynamic_slice_in_dim(grads["ssm_conv_w"], chip * conv_cols, conv_cols, axis=2)

    delta, new_m, new_v = {}, {}, {}
    for k in SHARDED:
        shp = w[k].shape
        two_d = (shp[0] * shp[1], shp[2])
        d_, m_, v_ = _adamw(w[k].reshape(two_d), grads[k].reshape(two_d), m[k].reshape(two_d), v[k].reshape(two_d),
                            f"adamw_{k}")
        delta[k], new_m[k], new_v[k] = d_.reshape(shp), m_.reshape(shp), v_.reshape(shp)
    d_, m_, v_ = _adamw(_pack_small(w, SMALL), _pack_small(grads, SMALL), _pack_small(m, SMALL), _pack_small(v, SMALL),
                        "adamw_small")
    delta.update(_unpack_small(d_, small_shapes, SMALL))
    new_m.update(_unpack_small(m_, small_shapes, SMALL))
    new_v.update(_unpack_small(v_, small_shapes, SMALL))

    return (loss, grad_x[None], *[grads[k] for k in WEIGHTS], *[delta[k] for k in WEIGHTS],
            *[new_m[k] for k in WEIGHTS], *[new_v[k] for k in WEIGHTS])
```

```python
import functools

import jax
import jax.numpy as jnp
from jax import lax
from jax.experimental import pallas as pl
from jax.experimental.pallas import tpu as pltpu

F32 = jnp.float32
BF16 = jnp.bfloat16
EPS = 1e-6
NEG_INF = float("-inf")

D_MODEL = 1024
DEPTH = 4
SSM_D_INNER = 2048
SSM_HEAD_DIM = 64
SSM_HEADS = 32
SSM_GROUPS = 8
SSM_HPG = 4
SSM_STATE = 128
SSM_CONV = 4
SSM_CHUNK = 128
SSM_BC_DIM = 1024
SSM_CONV_DIM = 4096
SSM_IN_DIM = 6176
SSM_IN_PAD = 6272
SSM_DT_PAD = 128
ATT_HEAD_DIM = 64
ATT_Q_HEADS = 16
ATT_KV_HEADS = 4
ATT_GQA = 4
ATT_WIDTH = 1024
ATT_KV_WIDTH = 256
ATT_IN_DIM = 2560
ATT_QKV = ATT_WIDTH + 2 * ATT_KV_WIDTH
ATT_BLOCK = 128
ROPE_THETA = 500000.0
ROPE_DIM = 16
ROPE_HALF = 8
Q_SCALE = ATT_HEAD_DIM ** -0.5

ADAM_LR = 0.001
ADAM_B1 = 0.9
ADAM_B2 = 0.999
ADAM_EPS = 1e-08
ADAM_WD = 0.01
ADAM_STEP = 10

VMEM_LIMIT_BYTES = 48 * 1024 * 1024
NT_DIMS = (((1,), (1,)), ((), ()))
TN_DIMS = (((0,), (0,)), ((), ()))


def _params(*sem):
    return pltpu.CompilerParams(dimension_semantics=sem, vmem_limit_bytes=VMEM_LIMIT_BYTES)


def _pick(n, cands):
    for c in cands:
        if n % c == 0:
            return c
    return n


def _sigmoid(v):
    return 0.5 * jnp.tanh(0.5 * v) + 0.5


def _bdot_tn(a, b):
    return lax.dot_general(a.astype(BF16), b.astype(BF16), TN_DIMS, preferred_element_type=F32)


MATMUL_VMEM_BUDGET = 36 * 1024 * 1024


def _matmul_tiles(m, n, k, out_bytes, reduce_rows):
    best = None
    whole = [k] if (not reduce_rows or k <= 2048) else []
    for tk in whole + [c for c in (4096, 2048, 1024, 896, 512) if k % c == 0 and c < k]:
        for tm in (c for c in (2048, 1024, 512, 256) if m % c == 0):
            for tn in (c for c in (n, 1280, 1024, 896, 640, 512) if n % c == 0):
                acc = tm * tn * 4 if tk < k else 0
                need = 2 * (2 * tk * (tm + tn) + tm * tn * out_bytes) + acc
                if need <= MATMUL_VMEM_BUDGET and (best is None or tm * tn * min(tk, 2048) > best[0]):
                    best = (tm * tn * min(tk, 2048), tm, tn, tk)
        if best is not None and not reduce_rows:
            break
    return best[1:]


def _matmul(a, b, mode, out_dtype, name, ride=(), ride_scatters=False):
    if mode == "nn":
        (m, k), n = a.shape, b.shape[1]
    elif mode == "nt":
        (m, k), n = a.shape, b.shape[0]
    else:
        (k, m), n = a.shape, b.shape[1]
    tm, tn, tk = _matmul_tiles(m, n, k, jnp.dtype(out_dtype).itemsize, mode == "tn")
    nk = k // tk
    steps = (n // tn, m // tm, nk)
    dims = {"nn": (((1,), (0,)), ((), ())), "nt": NT_DIMS, "tn": TN_DIMS}[mode]
    n_ride = len(ride)
    exchange = _scatter_between_chips if ride_scatters else _gather_between_chips
    arrived = _scatter_shapes(ride) if ride_scatters else [jax.ShapeDtypeStruct((N_CHIPS,) + r.shape, r.dtype) for r in ride]

    def body(*refs):
        a_ref, b_ref = refs[:2]
        ride_in = refs[2:2 + n_ride]
        o_ref = refs[2 + n_ride]
        ride_out = refs[3 + n_ride:3 + 2 * n_ride]
        acc_ref = refs[3 + 2 * n_ride]
        ride_sems = refs[4 + 2 * n_ride:]
        kk = pl.program_id(2)
        at = [pl.program_id(d) for d in range(3)]
        if n_ride:
            @pl.when((at[0] == 0) & (at[1] == 0) & (at[2] == 0))
            def _():
                exchange(ride_in, ride_out, *ride_sems, wait=False)

        part = lax.dot_general(a_ref[...], b_ref[...], dims, preferred_element_type=F32)
        if nk == 1:
            o_ref[...] = part.astype(o_ref.dtype)
        else:
            @pl.when(kk == 0)
            def _():
                acc_ref[...] = part

            @pl.when(kk > 0)
            def _():
                acc_ref[...] += part

            @pl.when(kk == nk - 1)
            def _():
                o_ref[...] = acc_ref[...].astype(o_ref.dtype)

        if n_ride:
            @pl.when((at[0] == steps[0] - 1) & (at[1] == steps[1] - 1) & (at[2] == steps[2] - 1))
            def _():
                exchange(ride_in, ride_out, *ride_sems, wait=True)

    if mode == "nn":
        a_spec = pl.BlockSpec((tm, tk), lambda j, i, kk: (i, kk))
        b_spec = pl.BlockSpec((tk, tn), lambda j, i, kk: (kk, j))
    elif mode == "nt":
        a_spec = pl.BlockSpec((tm, tk), lambda j, i, kk: (i, kk))
        b_spec = pl.BlockSpec((tn, tk), lambda j, i, kk: (j, kk))
    else:
        a_spec = pl.BlockSpec((tk, tm), lambda j, i, kk: (kk, i))
        b_spec = pl.BlockSpec((tk, tn), lambda j, i, kk: (kk, j))
    out = pl.pallas_call(
        body, grid=steps, in_specs=[a_spec, b_spec] + [ANY] * n_ride,
        out_specs=[pl.BlockSpec((tm, tn), lambda j, i, kk: (i, j))] + [ANY] * n_ride,
        out_shape=[jax.ShapeDtypeStruct((m, n), out_dtype)] + arrived,
        scratch_shapes=[pltpu.VMEM((tm, tn), F32)] + (_gather_sems(n_ride) if n_ride else []),
        compiler_params=_params(*(["arbitrary"] * 3 if n_ride else ["parallel", "parallel", "arbitrary"])), name=name,
    )(a, b, *ride)
    return out if n_ride else out[0]


def _row_tile(l):
    return _pick(l, (512, 256, 128))


def _rmsnorm_fwd(x, w, name):
    l, d = x.shape
    tl = _row_tile(l)

    def body(x_ref, w_ref, o_ref):
        xv = x_ref[...]
        r = lax.rsqrt(jnp.mean(xv * xv, axis=-1, keepdims=True) + EPS)
        o_ref[...] = (xv * r * w_ref[...]).astype(o_ref.dtype)

    return pl.pallas_call(
        body, grid=(l // tl,),
        in_specs=[pl.BlockSpec((tl, d), lambda i: (i, 0)), pl.BlockSpec((1, d), lambda i: (0, 0))],
        out_specs=pl.BlockSpec((tl, d), lambda i: (i, 0)),
        out_shape=jax.ShapeDtypeStruct((l, d), BF16), compiler_params=_params("parallel"), name=name,
    )(x, w.reshape(1, d))


def _post_fwd(x, y, w, w_next, name):
    l, d = x.shape
    tl = _row_tile(l)

    def body(x_ref, y_ref, w_ref, wn_ref, o_ref, h_ref):
        yv = y_ref[...]
        r = lax.rsqrt(jnp.mean(yv * yv, axis=-1, keepdims=True) + EPS)
        out = x_ref[...] + yv * r * w_ref[...]
        o_ref[...] = out
        rn = lax.rsqrt(jnp.mean(out * out, axis=-1, keepdims=True) + EPS)
        h_ref[...] = (out * rn * wn_ref[...]).astype(h_ref.dtype)

    row = pl.BlockSpec((tl, d), lambda i: (i, 0))
    vec = pl.BlockSpec((1, d), lambda i: (0, 0))
    return pl.pallas_call(
        body, grid=(l // tl,), in_specs=[row, row, vec, vec], out_specs=[row, row],
        out_shape=[jax.ShapeDtypeStruct((l, d), F32), jax.ShapeDtypeStruct((l, d), BF16)],
        compiler_params=_params("parallel"), name=name,
    )(x, y, w.reshape(1, d), w_next.reshape(1, d))


def _post_loss(x, y, w, t, name):
    l, d = x.shape
    tl = _row_tile(l)
    nt = l // tl

    def body(x_ref, y_ref, w_ref, t_ref, g_ref, dy_ref, ls_ref, dw_ref, acc_ref):
        i = pl.program_id(0)

        @pl.when(i == 0)
        def _():
            ls_ref[...] = jnp.zeros_like(ls_ref)
            acc_ref[...] = jnp.zeros_like(acc_ref)

        yv = y_ref[...]
        r = lax.rsqrt(jnp.mean(yv * yv, axis=-1, keepdims=True) + EPS)
        nrm = yv * r
        e = x_ref[...] + nrm * w_ref[...] - t_ref[...]
        gv = e * (1.0 / d)
        g_ref[...] = gv
        ls_ref[...] += jnp.sum((e * e).reshape(tl // 8, 8, d), axis=0)
        gw = gv * w_ref[...]
        dy_ref[...] = (r * (gw - nrm * jnp.mean(gw * nrm, axis=-1, keepdims=True))).astype(dy_ref.dtype)
        acc_ref[...] += jnp.sum((gv * nrm).reshape(tl // 8, 8, d), axis=0)

        @pl.when(i == nt - 1)
        def _():
            dw_ref[...] = jnp.sum(acc_ref[...], axis=0, keepdims=True)

    row = pl.BlockSpec((tl, d), lambda i: (i, 0))
    vec = pl.BlockSpec((1, d), lambda i: (0, 0))
    return pl.pallas_call(
        body, grid=(nt,), in_specs=[row, row, vec, row],
        out_specs=[row, row, pl.BlockSpec((8, d), lambda i: (0, 0)), vec],
        out_shape=[jax.ShapeDtypeStruct((l, d), F32), jax.ShapeDtypeStruct((l, d), BF16),
                   jax.ShapeDtypeStruct((8, d), F32), jax.ShapeDtypeStruct((1, d), F32)],
        scratch_shapes=[pltpu.VMEM((8, d), F32)], compiler_params=_params("arbitrary"), name=name,
    )(x, y, w.reshape(1, d), t)


def _norm_bwd_chain(dh, x, w_pre, resid, y_prev, w_post_prev, name):
    l, d = x.shape
    tl = _row_tile(l)
    nt = l // tl

    def body(dh_ref, x_ref, wp_ref, r_ref, y_ref, wq_ref, g_ref, dy_ref, dwp_ref, dwq_ref, accp_ref, accq_ref):
        i = pl.program_id(0)

        @pl.when(i == 0)
        def _():
            accp_ref[...] = jnp.zeros_like(accp_ref)
            accq_ref[...] = jnp.zeros_like(accq_ref)

        xv = x_ref[...]
        dhv = dh_ref[...]
        rx = lax.rsqrt(jnp.mean(xv * xv, axis=-1, keepdims=True) + EPS)
        nx = xv * rx
        gw = dhv * wp_ref[...]
        gv = rx * (gw - nx * jnp.mean(gw * nx, axis=-1, keepdims=True)) + r_ref[...]
        g_ref[...] = gv
        accp_ref[...] += jnp.sum((dhv * nx).reshape(tl // 8, 8, d), axis=0)
        yv = y_ref[...]
        ry = lax.rsqrt(jnp.mean(yv * yv, axis=-1, keepdims=True) + EPS)
        ny = yv * ry
        gq = gv * wq_ref[...]
        dy_ref[...] = (ry * (gq - ny * jnp.mean(gq * ny, axis=-1, keepdims=True))).astype(dy_ref.dtype)
        accq_ref[...] += jnp.sum((gv * ny).reshape(tl // 8, 8, d), axis=0)

        @pl.when(i == nt - 1)
        def _():
            dwp_ref[...] = jnp.sum(accp_ref[...], axis=0, keepdims=True)
            dwq_ref[...] = jnp.sum(accq_ref[...], axis=0, keepdims=True)

    row = pl.BlockSpec((tl, d), lambda i: (i, 0))
    vec = pl.BlockSpec((1, d), lambda i: (0, 0))
    return pl.pallas_call(
        body, grid=(nt,), in_specs=[row, row, vec, row, row, vec], out_specs=[row, row, vec, vec],
        out_shape=[jax.ShapeDtypeStruct((l, d), F32), jax.ShapeDtypeStruct((l, d), BF16),
                   jax.ShapeDtypeStruct((1, d), F32), jax.ShapeDtypeStruct((1, d), F32)],
        scratch_shapes=[pltpu.VMEM((8, d), F32), pltpu.VMEM((8, d), F32)],
        compiler_params=_params("arbitrary"), name=name,
    )(dh, x, w_pre.reshape(1, d), resid, y_prev, w_post_prev.reshape(1, d))


def _rmsnorm_bwd(g, y, w, resid, name):
    l, d = y.shape
    tl = _row_tile(l)
    nt = l // tl

    def body(g_ref, y_ref, w_ref, r_ref, dy_ref, dw_ref, acc_ref):
        i = pl.program_id(0)

        @pl.when(i == 0)
        def _():
            acc_ref[...] = jnp.zeros_like(acc_ref)

        yv = y_ref[...]
        gv = g_ref[...]
        r = lax.rsqrt(jnp.mean(yv * yv, axis=-1, keepdims=True) + EPS)
        nrm = yv * r
        gw = gv * w_ref[...]
        dy_ref[...] = r * (gw - nrm * jnp.mean(gw * nrm, axis=-1, keepdims=True)) + r_ref[...]
        acc_ref[...] += jnp.sum((gv * nrm).reshape(tl // 8, 8, d), axis=0)

        @pl.when(i == nt - 1)
        def _():
            dw_ref[...] = jnp.sum(acc_ref[...], axis=0, keepdims=True)

    row = pl.BlockSpec((tl, d), lambda i: (i, 0))
    vec = pl.BlockSpec((1, d), lambda i: (0, 0))
    return pl.pallas_call(
        body, grid=(nt,), in_specs=[row, row, vec, row], out_specs=[row, vec],
        out_shape=[jax.ShapeDtypeStruct((l, d), F32), jax.ShapeDtypeStruct((1, d), F32)],
        scratch_shapes=[pltpu.VMEM((8, d), F32)], compiler_params=_params("arbitrary"), name=name,
    )(g, y, w.reshape(1, d), resid)


CONV_COLS = 512
HALO = 8
HALO16 = 16
CONV_SUB_ROWS = 64
CONV_SUB_COLS = 256


def _conv_rows(l):
    return _pick(l, (1024, 512, 256, 128))


def _conv_bwd(dact, pre, proj, cw, dproj, name):
    l, width = dact.shape
    tl = _conv_rows(l)
    nt = l // tl
    pre_off = 0
    u_off = SSM_D_INNER // CONV_COLS
    hb16 = tl // HALO16
    last_hb16 = l // HALO16 - 1

    def body(da_ref, da_h_ref, p_ref, p_h_ref, u_ref, w_ref, _, du_ref, dw_ref, db_ref, ext_ref):
        i = pl.program_id(1)

        @pl.when(i == 0)
        def _():
            dw_ref[...] = jnp.zeros_like(dw_ref)
            db_ref[...] = jnp.zeros_like(db_ref)

        def dpre_of(da, p):
            s = _sigmoid(p)
            return da * (s * (1.0 + p * (1.0 - s)))

        ext_ref[0:tl, :] = dpre_of(da_ref[...].astype(F32), p_ref[...].astype(F32))
        ext_ref[tl:tl + HALO, :] = jnp.where(
            i < nt - 1, dpre_of(da_h_ref[...].astype(F32)[:HALO], p_h_ref[...].astype(F32)[:HALO]), 0.0)
        sub = CONV_SUB_ROWS

        def fold(v):
            return jnp.sum(v.reshape(sub // 8, 8, CONV_SUB_COLS), axis=0)

        for c0 in range(0, CONV_COLS, CONV_SUB_COLS):
            cs = slice(c0, c0 + CONV_SUB_COLS)
            dws = [jnp.zeros((8, CONV_SUB_COLS), F32) for _ in range(SSM_CONV)]
            dbs = jnp.zeros((8, CONV_SUB_COLS), F32)
            for r0 in range(0, tl, sub):
                dext = ext_ref[r0:r0 + sub + HALO, cs]
                uv = u_ref[r0:r0 + sub, cs]
                for k in range(SSM_CONV):
                    j = SSM_CONV - 1 - k
                    ahead = dext[:sub] if j == 0 else pltpu.roll(dext, sub + HALO - j, 0)[:sub]
                    term = w_ref[k:k + 1, cs] * ahead
                    du = term if k == 0 else du + term
                    dws[k] = dws[k] + fold(ahead * uv)
                dbs = dbs + fold(dext[:sub])
                du_ref[r0:r0 + sub, cs] = du.astype(du_ref.dtype)
            for k in range(SSM_CONV):
                dw_ref[k:k + 1, cs] += jnp.sum(dws[k], axis=0, keepdims=True)
            db_ref[:, cs] += jnp.sum(dbs, axis=0, keepdims=True)

    return pl.pallas_call(
        body, grid=(width // CONV_COLS, nt),
        in_specs=[pl.BlockSpec((tl, CONV_COLS), lambda j, i: (i, j)),
                  pl.BlockSpec((HALO16, CONV_COLS), lambda j, i: (jnp.minimum((i + 1) * hb16, last_hb16), j)),
                  pl.BlockSpec((tl, CONV_COLS), lambda j, i: (i, pre_off + j)),
                  pl.BlockSpec((HALO16, CONV_COLS), lambda j, i: (jnp.minimum((i + 1) * hb16, last_hb16), pre_off + j)),
                  pl.BlockSpec((tl, CONV_COLS), lambda j, i: (i, u_off + j)),
                  pl.BlockSpec((SSM_CONV, CONV_COLS), lambda j, i: (0, pre_off + j)),
                  pl.BlockSpec(memory_space=pl.ANY)],
        out_specs=[pl.BlockSpec((tl, CONV_COLS), lambda j, i: (i, u_off + j)),
                   pl.BlockSpec((SSM_CONV, CONV_COLS), lambda j, i: (0, j)),
                   pl.BlockSpec((1, CONV_COLS), lambda j, i: (0, j))],
        out_shape=[jax.ShapeDtypeStruct(dproj.shape, dproj.dtype), jax.ShapeDtypeStruct((SSM_CONV, width), F32),
                   jax.ShapeDtypeStruct((1, width), F32)],
        scratch_shapes=[pltpu.VMEM((tl + HALO, CONV_COLS), F32)],
        input_output_aliases={6: 0}, compiler_params=_params("parallel", "arbitrary"), name=name,
    )(dact, dact, pre, pre, proj, cw, dproj)


DT_COL_BLOCK = (SSM_D_INNER + SSM_CONV_DIM) // SSM_DT_PAD


def _split3(v):
    hi = v.astype(BF16)
    rest = v - hi.astype(F32)
    mid = rest.astype(BF16)
    lo = (rest - mid.astype(F32)).astype(BF16)
    return hi, mid, lo


def _dt_and_decay(v, a_log):
    head_dim_log2 = SSM_HEAD_DIM.bit_length() - 1
    dt_hi, dt_mid, _ = _split3(jnp.maximum(v, 0.0) + jnp.log1p(jnp.exp(-jnp.abs(v))))
    dt = dt_hi.astype(F32) + dt_mid.astype(F32)
    ri = lax.broadcasted_iota(jnp.int32, (SSM_CHUNK, SSM_CHUNK), 0)
    cj = lax.broadcasted_iota(jnp.int32, (SSM_CHUNK, SSM_CHUNK), 1)
    tri = (ri >= cj).astype(BF16)
    acs_pieces = _split3(sum(jnp.dot(tri, piece, preferred_element_type=F32)
                             for piece in _split3(dt * (-jnp.exp(a_log)))))
    acs = sum(piece.astype(F32) for piece in acs_pieces)
    head_of_lane = lax.shift_right_logical(lax.broadcasted_iota(jnp.int32, (SSM_DT_PAD, SSM_D_INNER), 1), head_dim_log2)
    spread = (head_of_lane == lax.broadcasted_iota(jnp.int32, (SSM_DT_PAD, SSM_D_INNER), 0)).astype(BF16)
    dtb = sum(jnp.dot(piece, spread, preferred_element_type=F32) for piece in (dt_hi, dt_mid))
    acsb = sum(jnp.dot(piece, spread, preferred_element_type=F32) for piece in acs_pieces)
    return dtb, acsb, dt.T, acs.T


def _dt_bwd(ddt, proj, bias, dproj, name):
    l = proj.shape[0]
    tl = _row_tile(l)

    def body(g_ref, p_ref, b_ref, _, o_ref, db_ref):
        @pl.when(pl.program_id(0) == 0)
        def _():
            db_ref[...] = jnp.zeros_like(db_ref)

        d = g_ref[...] * _sigmoid(p_ref[...] + b_ref[...])
        o_ref[...] = d.astype(o_ref.dtype)
        db_ref[...] += jnp.sum(d, axis=0, keepdims=True)

    return pl.pallas_call(
        body, grid=(l // tl,),
        in_specs=[pl.BlockSpec((tl, SSM_DT_PAD), lambda i: (i, 0)),
                  pl.BlockSpec((tl, SSM_DT_PAD), lambda i: (i, DT_COL_BLOCK)),
                  pl.BlockSpec((1, SSM_DT_PAD), lambda i: (0, 0)),
                  pl.BlockSpec(memory_space=pl.ANY)],
        out_specs=[pl.BlockSpec((tl, SSM_DT_PAD), lambda i: (i, DT_COL_BLOCK)),
                   pl.BlockSpec((1, SSM_DT_PAD), lambda i: (0, 0))],
        out_shape=[jax.ShapeDtypeStruct(dproj.shape, dproj.dtype), jax.ShapeDtypeStruct((1, SSM_DT_PAD), F32)],
        input_output_aliases={3: 0}, compiler_params=_params("arbitrary"), name=name,
    )(ddt, proj, bias, dproj)


GP = SSM_HPG * SSM_HEAD_DIM
HEAD_DIM_LOG2 = SSM_HEAD_DIM.bit_length() - 1
GPS = SSM_GROUPS
B_BLOCK0 = SSM_D_INNER // SSM_STATE
C_BLOCK0 = (SSM_D_INNER + SSM_BC_DIM) // SSM_STATE


def _chunk_iotas():
    ri = lax.broadcasted_iota(jnp.int32, (SSM_CHUNK, SSM_CHUNK), 0)
    cj = lax.broadcasted_iota(jnp.int32, (SSM_CHUNK, SSM_CHUNK), 1)
    return ri, cj


def _head_decay(acsb, acs_r, r, ri, cj):
    pair = acsb[:, (r // 2) * LANES:(r // 2 + 1) * LANES]
    mine_low = r % 2 == 0
    lane = lax.broadcasted_iota(jnp.int32, (1, LANES), 1)
    col = jnp.where((lane < SSM_HEAD_DIM) == mine_low, pair, pltpu.roll(pair, SSM_HEAD_DIM, 1))
    return jnp.exp(jnp.where(ri >= cj, col - acs_r[r:r + 1, :], NEG_INF))


def _head_masked_rows(v, dtype):
    head_of_lane = lax.shift_right_logical(lax.broadcasted_iota(jnp.int32, (1, GP), 1), HEAD_DIM_LOG2)
    narrow = v.astype(dtype)
    return jnp.concatenate([jnp.where(head_of_lane == r, narrow, jnp.zeros_like(narrow)) for r in range(SSM_HPG)], axis=0)


def _ssd_fwd(proj, cw, cb, dt_bias, a_log, d_lanes, gate_w, name, ride=()):
    l = proj.shape[0]
    nc = l // SSM_CHUNK
    assert GPS == SSM_GROUPS
    n_ride = len(ride)
    halo_blocks = SSM_CHUNK // HALO
    x_block = 1

    def body(*refs):
        u0_ref, u1_ref, h0_ref, h1_ref, cw_ref, cb_ref, dtraw_ref, bias_ref, alog_ref, d_ref, z_ref, gw_ref = refs[:12]
        ride_in = refs[12:12 + n_ride]
        (y_ref, act_ref, hin_ref, pre_ref, xbc_ref, dtb_out, acsb_out, dtr_out, acsr_out) = refs[12 + n_ride:21 + n_ride]
        ride_out = refs[21 + n_ride:21 + 2 * n_ride]
        h_ref, ext_ref, conv_ref, dtb_ref, acsb_ref, acsr_ref = refs[21 + 2 * n_ride:27 + 2 * n_ride]
        ride_sems = refs[27 + 2 * n_ride:]
        s = pl.program_id(0)
        if n_ride:
            @pl.when(s == 0)
            def _():
                _gather_between_chips(ride_in, ride_out, *ride_sems, wait=False)

            @pl.when(s == nc)
            def _():
                _gather_between_chips(ride_in, ride_out, *ride_sems, wait=True)

        @pl.when(s <= 1)
        def _():
            h_ref[...] = jnp.zeros_like(h_ref)

        @pl.when(s == 0)
        def _():
            conv_ref[1] = jnp.zeros((SSM_CHUNK, SSM_CONV_DIM), BF16)
            dtb_ref[1] = jnp.zeros((SSM_CHUNK, SSM_D_INNER), F32)
            acsb_ref[1] = jnp.zeros((SSM_CHUNK, SSM_D_INNER), F32)
            acsr_ref[1] = jnp.zeros((SSM_GROUPS, SSM_HPG, SSM_CHUNK), F32)

        conv_slot = s & 1
        scan_slot = (s - 1) & 1
        for half, (u_ref, hl_ref) in enumerate(((u0_ref, h0_ref), (u1_ref, h1_ref))):
            hc = slice(half * SSM_D_INNER, (half + 1) * SSM_D_INNER)
            ext_ref[0:HALO, hc] = jnp.where(s > 0, hl_ref[...], 0.0)
            ext_ref[HALO:HALO + SSM_CHUNK, hc] = u_ref[...]

        def conv_columns(c_lo, c_hi):
            for r0 in range(0, SSM_CHUNK, CONV_SUB_ROWS):
                for c0 in range(c_lo, c_hi, CONV_SUB_COLS):
                    cs = slice(c0, c0 + CONV_SUB_COLS)
                    ext = ext_ref[r0:r0 + CONV_SUB_ROWS + HALO, cs]
                    acc = cb_ref[:, cs] + cw_ref[SSM_CONV - 1:SSM_CONV, cs] * ext[HALO:]
                    for k in range(SSM_CONV - 1):
                        acc = acc + cw_ref[k:k + 1, cs] * pltpu.roll(ext, SSM_CONV - 1 - k, 0)[HALO:]
                    act = (acc * _sigmoid(acc)).astype(BF16)
                    pre_ref[r0:r0 + CONV_SUB_ROWS, cs] = acc.astype(pre_ref.dtype)
                    xbc_ref[r0:r0 + CONV_SUB_ROWS, cs] = act
                    conv_ref[conv_slot, r0:r0 + CONV_SUB_ROWS, cs] = act

        dtb, acsb, dt_rows, acs_rows = _dt_and_decay(dtraw_ref[...] + bias_ref[...], alog_ref[...])
        dtb_out[...] = dtb
        acsb_out[...] = acsb
        dtb_ref[conv_slot] = dtb
        acsb_ref[conv_slot] = acsb
        for g in range(SSM_GROUPS):
            heads = slice(g * SSM_HPG, (g + 1) * SSM_HPG)
            dtr_out[g] = dt_rows[heads, :]
            acsr_out[g] = acs_rows[heads, :]
            acsr_ref[conv_slot, g] = acs_rows[heads, :]

        ri, cj = _chunk_iotas()
        conv_share = SSM_CONV_DIM // GPS
        for k in range(GPS):
            conv_columns(k * conv_share, (k + 1) * conv_share)
            g = k
            cols = slice(k * GP, (k + 1) * GP)
            bcols = slice(SSM_D_INNER + k * SSM_STATE, SSM_D_INNER + (k + 1) * SSM_STATE)
            ccols = slice(SSM_D_INNER + SSM_BC_DIM + k * SSM_STATE, SSM_D_INNER + SSM_BC_DIM + (k + 1) * SSM_STATE)
            xv = conv_ref[scan_slot, :, cols].astype(F32)
            bb = conv_ref[scan_slot, :, bcols]
            cb16 = conv_ref[scan_slot, :, ccols]
            acs_v = acsb_ref[scan_slot, :, cols]
            acs_r_v = acsr_ref[scan_slot, k]
            lastb = acs_v[SSM_CHUNK - 1:SSM_CHUNK, :]
            xd = xv * dtb_ref[scan_slot, :, cols]
            cbm = lax.dot_general(cb16, bb, NT_DIMS, preferred_element_type=F32)
            hin = h_ref[g]
            hin_ref[0, k] = hin
            yoff = jnp.dot(cb16, hin.astype(BF16), preferred_element_type=F32)
            ms = [(cbm * _head_decay(acs_v, acs_r_v, r, ri, cj)).astype(BF16) for r in range(SSM_HPG)]
            ydiag = jnp.dot(jnp.concatenate(ms, axis=1), _head_masked_rows(xd, BF16), preferred_element_type=F32)
            y_ref[:, cols] = ydiag + jnp.exp(acs_v) * yoff + d_ref[k] * xv
            h_ref[g] = hin * jnp.exp(lastb) + _bdot_tn(bb, xd * jnp.exp(lastb - acs_v))
        z = z_ref[...]
        yg = y_ref[...] * (z * _sigmoid(z))
        r = lax.rsqrt(jnp.mean(yg * yg, axis=-1, keepdims=True) + EPS)
        act_ref[...] = (yg * r * gw_ref[...]).astype(act_ref.dtype)

    def conv_at(s):
        return jnp.minimum(s, nc - 1)

    def scan_at(s):
        return jnp.maximum(s - 1, 0)

    lanes = pl.BlockSpec((SSM_CHUNK, SSM_D_INNER), lambda s: (scan_at(s), 0))
    conv_out = pl.BlockSpec((SSM_CHUNK, SSM_CONV_DIM), lambda s: (conv_at(s), 0))
    lanes_ahead = pl.BlockSpec((SSM_CHUNK, SSM_D_INNER), lambda s: (conv_at(s), 0))
    rows_ahead = pl.BlockSpec((SSM_GROUPS, SSM_HPG, SSM_CHUNK), lambda s: (0, 0, conv_at(s)))
    return pl.pallas_call(
        body, grid=(nc + 1,),
        in_specs=[pl.BlockSpec((SSM_CHUNK, SSM_D_INNER), lambda s: (conv_at(s), x_block)),
                  pl.BlockSpec((SSM_CHUNK, SSM_D_INNER), lambda s: (conv_at(s), x_block + 1)),
                  pl.BlockSpec((HALO, SSM_D_INNER), lambda s: (jnp.maximum(conv_at(s) * halo_blocks - 1, 0), x_block)),
                  pl.BlockSpec((HALO, SSM_D_INNER), lambda s: (jnp.maximum(conv_at(s) * halo_blocks - 1, 0), x_block + 1)),
                  pl.BlockSpec((SSM_CONV, SSM_CONV_DIM), lambda s: (0, 0)),
                  pl.BlockSpec((1, SSM_CONV_DIM), lambda s: (0, 0)),
                  pl.BlockSpec((SSM_CHUNK, SSM_DT_PAD), lambda s: (conv_at(s), DT_COL_BLOCK)),
                  pl.BlockSpec((1, SSM_DT_PAD), lambda s: (0, 0)),
                  pl.BlockSpec((1, SSM_DT_PAD), lambda s: (0, 0)),
                  pl.BlockSpec((SSM_GROUPS, 1, GP), lambda s: (0, 0, 0)),
                  lanes, pl.BlockSpec((1, SSM_D_INNER), lambda s: (0, 0))] + [ANY] * n_ride,
        out_specs=[lanes, lanes, pl.BlockSpec((1, SSM_GROUPS, SSM_STATE, GP), lambda s: (scan_at(s), 0, 0, 0)),
                   conv_out, conv_out, lanes_ahead, lanes_ahead, rows_ahead, rows_ahead] + [ANY] * n_ride,
        out_shape=[jax.ShapeDtypeStruct((l, SSM_D_INNER), F32), jax.ShapeDtypeStruct((l, SSM_D_INNER), BF16),
                   jax.ShapeDtypeStruct((nc, SSM_GROUPS, SSM_STATE, GP), F32),
                   jax.ShapeDtypeStruct((l, SSM_CONV_DIM), BF16), jax.ShapeDtypeStruct((l, SSM_CONV_DIM), BF16),
                   jax.ShapeDtypeStruct((l, SSM_D_INNER), F32), jax.ShapeDtypeStruct((l, SSM_D_INNER), F32),
                   jax.ShapeDtypeStruct((SSM_GROUPS, SSM_HPG, l), F32),
                   jax.ShapeDtypeStruct((SSM_GROUPS, SSM_HPG, l), F32)]
        + [jax.ShapeDtypeStruct((N_CHIPS,) + a.shape, a.dtype) for a in ride],
        scratch_shapes=[pltpu.VMEM((SSM_GROUPS, SSM_STATE, GP), F32),
                        pltpu.VMEM((SSM_CHUNK + HALO, SSM_CONV_DIM), F32),
                        pltpu.VMEM((2, SSM_CHUNK, SSM_CONV_DIM), BF16),
                        pltpu.VMEM((2, SSM_CHUNK, SSM_D_INNER), F32), pltpu.VMEM((2, SSM_CHUNK, SSM_D_INNER), F32),
                        pltpu.VMEM((2, SSM_GROUPS, SSM_HPG, SSM_CHUNK), F32)] + (_gather_sems(n_ride) if n_ride else []),
        compiler_params=_params("arbitrary"), name=name,
    )(proj, proj, proj, proj, cw, cb.reshape(1, SSM_CONV_DIM), proj, dt_bias, a_log, d_lanes, proj,
      gate_w.reshape(1, SSM_D_INNER), *ride)


def _ssd_bwd(xbc, dtb, acsb, dtr, acs_r, a_log, d_lanes, hin, dact, y, proj, gate_w, name):
    l = xbc.shape[0]
    nc = l // SSM_CHUNK

    def body(x_ref, b_ref, c_ref, dtb_ref, acsb_ref, dtr_ref, acsr_ref, alc_ref, d_ref, hin_ref,
             dact_ref, y_ref, z_ref, gw_ref,
             dxbc_ref, ddt_ref, dal_ref, dd_ref, dproj_ref, dgw_ref, dh_ref, dy_ref, acc_ref):
        c = pl.program_id(0)
        dx_ref = dxbc_ref.at[:, 0:SSM_D_INNER]
        db_ref = dxbc_ref.at[:, SSM_D_INNER:SSM_D_INNER + SSM_BC_DIM]
        dc_ref = dxbc_ref.at[:, SSM_D_INNER + SSM_BC_DIM:SSM_CONV_DIM]

        @pl.when(c == 0)
        def _():
            dal_ref[...] = jnp.zeros_like(dal_ref)
            dd_ref[...] = jnp.zeros_like(dd_ref)
            acc_ref[...] = jnp.zeros_like(acc_ref)

        z = z_ref[...]
        yv = y_ref[...]
        s = _sigmoid(z)
        sz = z * s
        yg = yv * sz
        r = lax.rsqrt(jnp.mean(yg * yg, axis=-1, keepdims=True) + EPS)
        nrm = yg * r
        gv = dact_ref[...]
        gw = gv * gw_ref[...]
        dyg = r * (gw - nrm * jnp.mean(gw * nrm, axis=-1, keepdims=True))
        dy_ref[...] = dyg * sz
        dproj_ref[...] = (dyg * yv * (s * (1.0 + z * (1.0 - s)))).astype(dproj_ref.dtype)
        acc_ref[...] += jnp.sum((gv * nrm).reshape(SSM_CHUNK // 8, 8, SSM_D_INNER), axis=0)

        @pl.when(c == nc - 1)
        def _():
            dgw_ref[...] = jnp.sum(acc_ref[...], axis=0, keepdims=True)

        for k in range(GPS):
            one_group(c, k, k, x_ref, b_ref, c_ref, dtb_ref, acsb_ref, dtr_ref, acsr_ref, alc_ref, d_ref,
                      hin_ref, dy_ref, dx_ref, db_ref, dc_ref, ddt_ref, dal_ref, dd_ref, dh_ref)

    def one_group(c, g, k, x_ref, b_ref, c_ref, dtb_ref, acsb_ref, dtr_ref, acsr_ref, alc_ref, d_ref, hin_ref, dy_ref,
                  dx_ref, db_ref, dc_ref, ddt_ref, dal_ref, dd_ref, dh_ref):
        cols = slice(k * GP, (k + 1) * GP)
        ncols = slice(k * SSM_STATE, (k + 1) * SSM_STATE)

        @pl.when(c == 0)
        def _():
            dh_ref[g] = jnp.zeros((SSM_STATE, GP), F32)

        xv = x_ref[:, cols].astype(F32)
        dyv = dy_ref[:, cols]
        bb = b_ref[:, ncols].astype(BF16)
        cb16 = c_ref[:, ncols].astype(BF16)
        dtb = dtb_ref[:, cols]
        acsb = acsb_ref[:, cols]
        dtr_v = dtr_ref[k]
        acs_r = acsr_ref[k]
        a_col = -jnp.exp(alc_ref[k])
        ri, cj = _chunk_iotas()
        head_of_lane = lax.shift_right_logical(lax.broadcasted_iota(jnp.int32, (SSM_HPG, GP), 1), HEAD_DIM_LOG2)
        ind_t = (head_of_lane == lax.broadcasted_iota(jnp.int32, (SSM_HPG, GP), 0)).astype(BF16)
        lastb = acsb[SSM_CHUNK - 1:SSM_CHUNK, :]
        ecb = jnp.exp(acsb)
        dteb = jnp.exp(lastb - acsb)
        xd = xv * dtb
        xw = xd * dteb
        cb = lax.dot_general(cb16, bb, NT_DIMS, preferred_element_type=F32)
        hin_v = hin_ref[0, k]
        dhn = dh_ref[g]
        h16 = hin_v.astype(BF16)
        dh16 = dhn.astype(BF16)
        ch = jnp.dot(cb16, h16, preferred_element_type=F32)
        bdh = jnp.dot(bb, dh16, preferred_element_type=F32)
        dym = _head_masked_rows(dyv, BF16)
        g_all = lax.dot_general(dym, xd.astype(BF16), NT_DIMS, preferred_element_type=F32)
        gl_sum = jnp.zeros((SSM_CHUNK, SSM_CHUNK), F32)
        ms, qs = [], []
        for r in range(SSM_HPG):
            decay = _head_decay(acsb, acs_r, r, ri, cj)
            gl = g_all[r * SSM_CHUNK:(r + 1) * SSM_CHUNK] * decay
            gl_sum = gl_sum + gl
            ms.append((cb * decay).astype(BF16))
            qs.append((gl * cb).astype(BF16))
        dxd = lax.dot_general(jnp.concatenate(ms, axis=0), dym, TN_DIMS, preferred_element_type=F32) + dteb * bdh
        cum = jnp.dot(jnp.concatenate(qs, axis=0), (ri < cj).astype(BF16), preferred_element_type=F32)
        sub4 = lax.broadcasted_iota(jnp.int32, (SSM_HPG, 1), 0)
        da = jnp.zeros((SSM_HPG, SSM_CHUNK), F32)
        for r in range(SSM_HPG):
            rect = jnp.sum(jnp.where(ri >= cj, cum[r * SSM_CHUNK:(r + 1) * SSM_CHUNK], 0.0), axis=0, keepdims=True)
            da = da + jnp.where(sub4 == r, rect, 0.0)
        z2 = xw * bdh
        sub8 = lax.broadcasted_iota(jnp.int32, (8, 1), 0)
        col_sums = (jnp.where(sub8 == 0, jnp.sum(z2, axis=0, keepdims=True), 0.0)
                    + jnp.where(sub8 == 1, jnp.sum(dhn * hin_v, axis=0, keepdims=True), 0.0)
                    + jnp.where(sub8 == 2, jnp.sum(dyv * xv, axis=0, keepdims=True), 0.0))
        wv = ecb * dyv
        summands = jnp.concatenate([wv * ch - z2, dxd * xv, col_sums], axis=0)
        sums = lax.dot_general(ind_t, summands.astype(BF16), NT_DIMS, preferred_element_type=F32)
        per_pos = sums[:, :2 * SSM_CHUNK]
        totals = sums[:, 2 * SSM_CHUNK:]
        e_last = totals[:, 0:1] + jnp.exp(acs_r[:, SSM_CHUNK - 1:SSM_CHUNK]) * totals[:, 1:2]
        da = (da + e_last + jnp.dot(per_pos[:, :SSM_CHUNK], (ri >= cj).astype(F32), preferred_element_type=F32,
                                    precision=lax.Precision.HIGHEST))
        ddt_ref[k] = a_col * da + per_pos[:, SSM_CHUNK:]
        dal_ref[g] += a_col * jnp.sum(da * dtr_v, axis=1, keepdims=True)
        dd_ref[g] += totals[:, 2:3]
        dx_ref[:, cols] = (dxd * dtb + d_ref[k] * dyv).astype(dx_ref.dtype)
        w16 = wv.astype(BF16)
        xw16 = xw.astype(BF16)
        gl16 = gl_sum.astype(BF16)
        dc_ref[:, ncols] = (jnp.dot(gl16, bb, preferred_element_type=F32)
                            + lax.dot_general(w16, h16, NT_DIMS, preferred_element_type=F32)).astype(dc_ref.dtype)
        db_ref[:, ncols] = (lax.dot_general(gl16, cb16, TN_DIMS, preferred_element_type=F32)
                            + lax.dot_general(xw16, dh16, NT_DIMS, preferred_element_type=F32)).astype(db_ref.dtype)
        dh_ref[g] = dhn * jnp.exp(lastb) + lax.dot_general(cb16, w16, TN_DIMS, preferred_element_type=F32)

    def rev(c):
        return nc - 1 - c

    small = pl.BlockSpec((SSM_GROUPS, SSM_HPG, 1), lambda c: (0, 0, 0))
    lanes = pl.BlockSpec((SSM_CHUNK, SSM_D_INNER), lambda c: (rev(c), 0))
    rows = pl.BlockSpec((SSM_GROUPS, SSM_HPG, SSM_CHUNK), lambda c: (0, 0, rev(c)))
    vec = pl.BlockSpec((1, SSM_D_INNER), lambda c: (0, 0))
    return pl.pallas_call(
        body, grid=(nc,),
        in_specs=[lanes,
                  pl.BlockSpec((SSM_CHUNK, SSM_BC_DIM), lambda c: (rev(c), B_BLOCK0 // GPS)),
                  pl.BlockSpec((SSM_CHUNK, SSM_BC_DIM), lambda c: (rev(c), C_BLOCK0 // GPS)),
                  lanes, lanes, rows, rows,
                  pl.BlockSpec((SSM_GROUPS, SSM_HPG, 1), lambda c: (0, 0, 0)),
                  pl.BlockSpec((SSM_GROUPS, 1, GP), lambda c: (0, 0, 0)),
                  pl.BlockSpec((1, SSM_GROUPS, SSM_STATE, GP), lambda c: (rev(c), 0, 0, 0)),
                  lanes, lanes, lanes, vec],
        out_specs=[pl.BlockSpec((SSM_CHUNK, SSM_CONV_DIM), lambda c: (rev(c), 0)),
                   rows, small, small, lanes, vec],
        out_shape=[jax.ShapeDtypeStruct((l, SSM_CONV_DIM), BF16), jax.ShapeDtypeStruct((SSM_GROUPS, SSM_HPG, l), F32),
                   jax.ShapeDtypeStruct((SSM_GROUPS, SSM_HPG, 1), F32),
                   jax.ShapeDtypeStruct((SSM_GROUPS, SSM_HPG, 1), F32),
                   jax.ShapeDtypeStruct((l, SSM_IN_PAD), BF16), jax.ShapeDtypeStruct((1, SSM_D_INNER), F32)],
        scratch_shapes=[pltpu.VMEM((SSM_GROUPS, SSM_STATE, GP), F32), pltpu.VMEM((SSM_CHUNK, SSM_D_INNER), F32),
                        pltpu.VMEM((8, SSM_D_INNER), F32)],
        compiler_params=_params("arbitrary"), name=name,
    )(xbc, xbc, xbc, dtb, acsb, dtr, acs_r, a_log.reshape(SSM_GROUPS, SSM_HPG, 1), d_lanes, hin, dact, y, proj,
      gate_w.reshape(1, SSM_D_INNER))


LANES = 128
ROPE_Q_CHUNKS = ATT_WIDTH // LANES
ROPE_K_CHUNKS = ATT_KV_WIDTH // LANES


def _rope_tables(positions):
    inv = ROPE_THETA ** (-jnp.arange(0, ROPE_DIM, 2, dtype=F32) / ROPE_DIM)
    ang = positions.astype(F32)[:, None] * inv
    cos, sin = jnp.cos(ang), jnp.sin(ang)
    l = positions.shape[0]
    rest = ATT_HEAD_DIM - ROPE_DIM
    ones, zeros = jnp.ones((l, rest), F32), jnp.zeros((l, rest), F32)
    z8 = jnp.zeros((l, ROPE_HALF), F32)
    cos_f = jnp.concatenate([cos, cos, ones], axis=1)
    sin_a = jnp.concatenate([-sin, z8, zeros], axis=1)
    sin_b = jnp.concatenate([z8, sin, zeros], axis=1)
    reps = LANES // ATT_HEAD_DIM
    return tuple(jnp.tile(t, (1, reps)) for t in (cos_f, sin_a, sin_b))


ATT_QKV4 = 3 * ATT_WIDTH


def _both_halves(chunk):
    lane = lax.broadcasted_iota(jnp.int32, (1, LANES), 1)
    swapped = pltpu.roll(chunk, ATT_HEAD_DIM, 1)
    return jnp.where(lane < ATT_HEAD_DIM, chunk, swapped), jnp.where(lane < ATT_HEAD_DIM, swapped, chunk)


def _rope_fwd(proj, tables, name):
    l = proj.shape[0]
    tl = _pick(l, (256, 128))

    def body(p_ref, c_ref, sa_ref, sb_ref, o_ref):
        cos_f, sin_a, sin_b = c_ref[...], sa_ref[...], sb_ref[...]

        def rope(t):
            return t * cos_f + pltpu.roll(t, LANES - ROPE_HALF, 1) * sin_a + pltpu.roll(t, ROPE_HALF, 1) * sin_b

        for k in range(ROPE_Q_CHUNKS):
            sl = slice(k * LANES, (k + 1) * LANES)
            o_ref[:, sl] = (rope(p_ref[:, sl]) * Q_SCALE).astype(o_ref.dtype)
        for part in range(2):
            for k in range(ROPE_K_CHUNKS):
                src = ATT_WIDTH + part * ATT_KV_WIDTH + k * LANES
                t = p_ref[:, src:src + LANES]
                if part == 0:
                    t = rope(t)
                for head, dup in enumerate(_both_halves(t.astype(o_ref.dtype))):
                    dst = (1 + part) * ATT_WIDTH + (2 * k + head) * ATT_GQA * ATT_HEAD_DIM
                    o_ref[:, dst:dst + LANES] = dup
                    o_ref[:, dst + LANES:dst + 2 * LANES] = dup

    tab = pl.BlockSpec((tl, LANES), lambda i: (i, 0))
    return pl.pallas_call(
        body, grid=(l // tl,), in_specs=[pl.BlockSpec((tl, ATT_IN_DIM), lambda i: (i, 0)), tab, tab, tab],
        out_specs=pl.BlockSpec((tl, ATT_QKV4), lambda i: (i, 0)),
        out_shape=jax.ShapeDtypeStruct((l, ATT_QKV4), BF16), compiler_params=_params("parallel"), name=name,
    )(proj, *tables)


def _rope_bwd(dq, dk4, dv4, dgate, tables, name):
    l = dq.shape[0]
    tl = _pick(l, (256, 128))

    def body(dq_ref, dk_ref, dv_ref, dg_ref, c_ref, sa_ref, sb_ref, o_ref):
        cos_f, sin_a, sin_b = c_ref[...], sa_ref[...], sb_ref[...]
        lane = lax.broadcasted_iota(jnp.int32, (1, LANES), 1)

        def unrope(t):
            return t * cos_f + pltpu.roll(t * sin_a, ROPE_HALF, 1) + pltpu.roll(t * sin_b, LANES - ROPE_HALF, 1)

        def head_total(ref, kvh):
            base = kvh * ATT_GQA * ATT_HEAD_DIM
            s = ref[:, base:base + LANES] + ref[:, base + LANES:base + 2 * LANES]
            return s + pltpu.roll(s, ATT_HEAD_DIM, 1)

        for k in range(ROPE_Q_CHUNKS):
            sl = slice(k * LANES, (k + 1) * LANES)
            o_ref[:, sl] = unrope(dq_ref[:, sl] * Q_SCALE).astype(o_ref.dtype)
        for k in range(ROPE_K_CHUNKS):
            dk = jnp.where(lane < ATT_HEAD_DIM, head_total(dk_ref, 2 * k), head_total(dk_ref, 2 * k + 1))
            dv = jnp.where(lane < ATT_HEAD_DIM, head_total(dv_ref, 2 * k), head_total(dv_ref, 2 * k + 1))
            o_ref[:, ATT_WIDTH + k * LANES:ATT_WIDTH + (k + 1) * LANES] = unrope(dk).astype(o_ref.dtype)
            at = ATT_WIDTH + ATT_KV_WIDTH + k * LANES
            o_ref[:, at:at + LANES] = dv.astype(o_ref.dtype)
        o_ref[:, ATT_QKV:ATT_IN_DIM] = dg_ref[...].astype(o_ref.dtype)

    tab = pl.BlockSpec((tl, LANES), lambda i: (i, 0))
    wide = pl.BlockSpec((tl, ATT_WIDTH), lambda i: (i, 0))
    return pl.pallas_call(
        body, grid=(l // tl,), in_specs=[wide, wide, wide, wide, tab, tab, tab],
        out_specs=pl.BlockSpec((tl, ATT_IN_DIM), lambda i: (i, 0)),
        out_shape=jax.ShapeDtypeStruct((l, ATT_IN_DIM), BF16), compiler_params=_params("parallel"), name=name,
    )(dq, dk4, dv4, dgate, *tables)


GATE_HALF = ATT_WIDTH // 2
GATE_COL_BLOCK = ATT_QKV // GATE_HALF


ATT_STACK = ATT_GQA * ATT_BLOCK
BLOCK_LOG2 = ATT_BLOCK.bit_length() - 1


def _stack_masks(n):
    ri = lax.broadcasted_iota(jnp.int32, (ATT_STACK, ATT_BLOCK), 0) & (ATT_BLOCK - 1)
    cj = lax.broadcasted_iota(jnp.int32, (ATT_STACK, ATT_BLOCK), 1)
    return (cj > ri) & (n > 0), cj <= ri


def _stack_sinks(sink_ref, kvh):
    blk = lax.shift_right_logical(lax.broadcasted_iota(jnp.int32, (ATT_STACK, 1), 0), BLOCK_LOG2)
    col = jnp.zeros((ATT_STACK, 1), F32)
    for r in range(ATT_GQA):
        col = jnp.where(blk == r, sink_ref[kvh * ATT_GQA + r], col)
    return col


def _stack_fold(stack):
    head_of_lane = lax.shift_right_logical(lax.broadcasted_iota(jnp.int32, (1, GP), 1), HEAD_DIM_LOG2)
    out = jnp.zeros((ATT_BLOCK, GP), F32)
    for r in range(ATT_GQA):
        out = jnp.where(head_of_lane == r, stack[r * ATT_BLOCK:(r + 1) * ATT_BLOCK], out)
    return out


def _attn_fwd(qkv, proj, sinks, name):
    l = qkv.shape[0]
    nb = l // ATT_BLOCK

    def body(sink_ref, q_ref, kp_ref, kc_ref, vp_ref, vc_ref, g0_ref, g1_ref, og_ref, o_ref, lse_ref):
        n = pl.program_id(0)
        mask_p, mask_c = _stack_masks(n)
        ones = jnp.ones((ATT_BLOCK, LANES), BF16)
        for kvh in range(ATT_KV_HEADS):
            cols = slice(kvh * GP, (kvh + 1) * GP)
            q_stack = _head_masked_rows(q_ref[:, cols], BF16)
            sp = jnp.where(mask_p, lax.dot_general(q_stack, kp_ref[:, cols], NT_DIMS, preferred_element_type=F32), NEG_INF)
            sc = jnp.where(mask_c, lax.dot_general(q_stack, kc_ref[:, cols], NT_DIMS, preferred_element_type=F32), NEG_INF)
            sink = _stack_sinks(sink_ref, kvh)
            m = jnp.maximum(jnp.max(jnp.maximum(sp, sc), axis=1, keepdims=True), sink)
            pp = jnp.exp(sp - m).astype(BF16)
            pc = jnp.exp(sc - m).astype(BF16)
            acc = (jnp.dot(pp, jnp.concatenate([vp_ref[:, cols], ones], axis=1), preferred_element_type=F32)
                   + jnp.dot(pc, jnp.concatenate([vc_ref[:, cols], ones], axis=1), preferred_element_type=F32))
            den = acc[:, GP:] + jnp.exp(sink - m)
            inv = 1.0 / den
            o_ref[:, cols] = _stack_fold(acc[:, :GP] * jnp.concatenate([inv, inv], axis=1))
            lse = m + jnp.log(den)
            lse_ref[:, cols] = _stack_fold(jnp.concatenate([lse, lse], axis=1))
        for half, g_ref in enumerate((g0_ref, g1_ref)):
            sl = slice(half * GATE_HALF, (half + 1) * GATE_HALF)
            gate = g_ref[...]
            og_ref[:, sl] = (o_ref[:, sl] * (gate * _sigmoid(gate))).astype(og_ref.dtype)

    def prev(n):
        return jnp.maximum(n - 1, 0)

    wide = pl.BlockSpec((ATT_BLOCK, ATT_WIDTH), lambda n: (n, 0))
    return pl.pallas_call(
        body, grid=(nb,),
        in_specs=[pl.BlockSpec(memory_space=pltpu.SMEM), wide,
                  pl.BlockSpec((ATT_BLOCK, ATT_WIDTH), lambda n: (prev(n), 1)),
                  pl.BlockSpec((ATT_BLOCK, ATT_WIDTH), lambda n: (n, 1)),
                  pl.BlockSpec((ATT_BLOCK, ATT_WIDTH), lambda n: (prev(n), 2)),
                  pl.BlockSpec((ATT_BLOCK, ATT_WIDTH), lambda n: (n, 2)),
                  pl.BlockSpec((ATT_BLOCK, GATE_HALF), lambda n: (n, GATE_COL_BLOCK)),
                  pl.BlockSpec((ATT_BLOCK, GATE_HALF), lambda n: (n, GATE_COL_BLOCK + 1))],
        out_specs=[wide, wide, wide],
        out_shape=[jax.ShapeDtypeStruct((l, ATT_WIDTH), BF16), jax.ShapeDtypeStruct((l, ATT_WIDTH), F32),
                   jax.ShapeDtypeStruct((l, ATT_WIDTH), F32)],
        compiler_params=_params("parallel"), name=name,
    )(sinks, qkv, qkv, qkv, qkv, qkv, proj, proj)


def _attn_bwd(qkv, proj, sinks, o, lse, dog, name, ride=()):
    l = qkv.shape[0]
    nb = l // ATT_BLOCK
    n_ride = len(ride)

    def body(*refs):
        sink_ref, q_ref, kp_ref, kc_ref, vp_ref, vc_ref, g0_ref, g1_ref, o_ref, lse_ref, dog_ref = refs[:11]
        ride_in = refs[11:11 + n_ride]
        dq_ref, dk_ref, dv_ref, dg_ref, ds_ref = refs[11 + n_ride:16 + n_ride]
        ride_out = refs[16 + n_ride:16 + 2 * n_ride]
        ck_ref, cv_ref, do_ref = refs[16 + 2 * n_ride:19 + 2 * n_ride]
        ride_sems = refs[19 + 2 * n_ride:]
        n = pl.program_id(0)

        @pl.when(n == 0)
        def _():
            ds_ref[...] = jnp.zeros_like(ds_ref)
            ck_ref[...] = jnp.zeros_like(ck_ref)
            cv_ref[...] = jnp.zeros_like(cv_ref)
            if n_ride:
                _scatter_between_chips(ride_in, ride_out, *ride_sems, wait=False)

        @pl.when(n == nb)
        def _():
            dk_ref[...] = ck_ref[...]
            dv_ref[...] = cv_ref[...]
            if n_ride:
                _scatter_between_chips(ride_in, ride_out, *ride_sems, wait=True)

        @pl.when(n < nb)
        def _():
            mask_p, mask_c = _stack_masks(n)
            lane = lax.broadcasted_iota(jnp.int32, (1, ATT_Q_HEADS), 1)
            for half, g_ref in enumerate((g0_ref, g1_ref)):
                sl = slice(half * GATE_HALF, (half + 1) * GATE_HALF)
                gate = g_ref[...]
                s = _sigmoid(gate)
                dogv = dog_ref[:, sl]
                do_ref[:, sl] = dogv * (gate * s)
                dg_ref[:, sl] = dogv * o_ref[:, sl] * (s * (1.0 + gate * (1.0 - s)))
            ds_acc = jnp.zeros((1, ATT_Q_HEADS), F32)
            for kvh in range(ATT_KV_HEADS):
                cols = slice(kvh * GP, (kvh + 1) * GP)
                kp, kc, vp, vc = kp_ref[:, cols], kc_ref[:, cols], vp_ref[:, cols], vc_ref[:, cols]
                q_stack = _head_masked_rows(q_ref[:, cols], BF16)
                do_g = do_ref[:, cols]
                do_stack = _head_masked_rows(do_g, BF16)
                lse_g = lse_ref[:, cols]
                lse_stack = jnp.concatenate(
                    [_both_halves(lse_g[:, (r // 2) * LANES:(r // 2 + 1) * LANES])[r % 2] for r in range(ATT_GQA)], axis=0)
                pp = jnp.exp(jnp.where(
                    mask_p, lax.dot_general(q_stack, kp, NT_DIMS, preferred_element_type=F32) - lse_stack, NEG_INF))
                pc = jnp.exp(jnp.where(
                    mask_c, lax.dot_general(q_stack, kc, NT_DIMS, preferred_element_type=F32) - lse_stack, NEG_INF))
                dpp = lax.dot_general(do_stack, vp, NT_DIMS, preferred_element_type=F32)
                dpc = lax.dot_general(do_stack, vc, NT_DIMS, preferred_element_type=F32)
                delta = jnp.sum(pp * dpp + pc * dpc, axis=1, keepdims=True)
                dsp = (pp * (dpp - delta)).astype(BF16)
                dsc = (pc * (dpc - delta)).astype(BF16)
                dq_ref[:, cols] = _stack_fold(jnp.dot(dsp, kp, preferred_element_type=F32)
                                              + jnp.dot(dsc, kc, preferred_element_type=F32))
                dk_ref[:, cols] = ck_ref[:, cols] + lax.dot_general(dsp, q_stack, TN_DIMS, preferred_element_type=F32)
                dv_ref[:, cols] = cv_ref[:, cols] + lax.dot_general(pp.astype(BF16), do_stack, TN_DIMS,
                                                                    preferred_element_type=F32)
                ck_ref[:, cols] = lax.dot_general(dsc, q_stack, TN_DIMS, preferred_element_type=F32)
                cv_ref[:, cols] = lax.dot_general(pc.astype(BF16), do_stack, TN_DIMS, preferred_element_type=F32)
                t = jnp.exp(_stack_sinks(sink_ref, kvh) - lse_stack) * delta
                for r in range(ATT_GQA):
                    tot = jnp.sum(t[r * ATT_BLOCK:(r + 1) * ATT_BLOCK], axis=0, keepdims=True)
                    ds_acc = ds_acc - jnp.where(lane == kvh * ATT_GQA + r, tot[:, :ATT_Q_HEADS], 0.0)
            ds_ref[...] += ds_acc

    def cur(n):
        return jnp.minimum(n, nb - 1)

    def prev(n):
        return jnp.maximum(n - 1, 0)

    wide = pl.BlockSpec((ATT_BLOCK, ATT_WIDTH), lambda n: (cur(n), 0))
    late = pl.BlockSpec((ATT_BLOCK, ATT_WIDTH), lambda n: (prev(n), 0))
    return pl.pallas_call(
        body, grid=(nb + 1,),
        in_specs=[pl.BlockSpec(memory_space=pltpu.SMEM), wide,
                  pl.BlockSpec((ATT_BLOCK, ATT_WIDTH), lambda n: (prev(cur(n)), 1)),
                  pl.BlockSpec((ATT_BLOCK, ATT_WIDTH), lambda n: (cur(n), 1)),
                  pl.BlockSpec((ATT_BLOCK, ATT_WIDTH), lambda n: (prev(cur(n)), 2)),
                  pl.BlockSpec((ATT_BLOCK, ATT_WIDTH), lambda n: (cur(n), 2)),
                  pl.BlockSpec((ATT_BLOCK, GATE_HALF), lambda n: (cur(n), GATE_COL_BLOCK)),
                  pl.BlockSpec((ATT_BLOCK, GATE_HALF), lambda n: (cur(n), GATE_COL_BLOCK + 1)),
                  wide, wide, wide] + [ANY] * n_ride,
        out_specs=[wide, late, late, wide, pl.BlockSpec((1, ATT_Q_HEADS), lambda n: (0, 0))] + [ANY] * n_ride,
        out_shape=[jax.ShapeDtypeStruct((l, ATT_WIDTH), F32), jax.ShapeDtypeStruct((l, ATT_WIDTH), F32),
                   jax.ShapeDtypeStruct((l, ATT_WIDTH), F32), jax.ShapeDtypeStruct((l, ATT_WIDTH), F32),
                   jax.ShapeDtypeStruct((1, ATT_Q_HEADS), F32)] + _scatter_shapes(ride),
        scratch_shapes=[pltpu.VMEM((ATT_BLOCK, ATT_WIDTH), F32), pltpu.VMEM((ATT_BLOCK, ATT_WIDTH), F32),
                        pltpu.VMEM((ATT_BLOCK, ATT_WIDTH), F32)] + (_gather_sems(n_ride) if n_ride else []),
        compiler_params=_params("arbitrary"), name=name,
    )(sinks, qkv, qkv, qkv, qkv, qkv, proj, proj, o, lse, dog, *ride)


def _local_step(x, positions, pre_norm, post_norm, conv_b, dt_bias, a_log, d_skip, gate_norm, sinks, target,
                first_in, in_proj_with_first_pair, scan_with_second_pair, attn_bwd_with_second_pair_grads,
                in_dx_with_first_pair_grads):
    tables = _rope_tables(positions)
    dt_bias_pad = jnp.pad(dt_bias, ((0, 0), (0, SSM_DT_PAD - SSM_HEADS)))
    d_lanes = jnp.repeat(d_skip, SSM_HEAD_DIM, axis=1).reshape(-1, SSM_GROUPS, 1, GP)
    a_log_pad = jnp.pad(a_log, ((0, 0), (0, SSM_DT_PAD - SSM_HEADS)))
    pairs = [first_in, None]
    saved = []
    cur = x
    h = _rmsnorm_fwd(cur, pre_norm[0], "prenorm_fwd_0")
    for i in range(DEPTH):
        j = i // 2
        if i % 2 == 0:
            in_proj = functools.partial(_matmul, h, pairs[j]["ssm_w_in"], "nn", F32, f"ssm_in_{i}")
            if i == 0:
                proj, rest = in_proj_with_first_pair(in_proj)
                pairs[0] = {**first_in, **rest}
            else:
                proj = in_proj()
            scan = functools.partial(_ssd_fwd, proj, pairs[j]["ssm_conv_w"], conv_b[j], dt_bias_pad[j:j + 1],
                                     a_log_pad[j:j + 1], d_lanes[j], gate_norm[j], f"ssd_fwd_{i}")
            if i == 0:
                *scanned, pairs[1] = scan_with_second_pair(scan)
            else:
                scanned = scan()
            y, act, hin, pre, xbc, dtb, acsb, dtr, acs_r = scanned
            w_ssm_in = [p["ssm_w_in"] for p in pairs]
            w_ssm_out = [p["ssm_w_out"] for p in pairs]
            w_att_in = [p["att_w_in"] for p in pairs]
            w_att_out = [p["att_w_out"] for p in pairs]
            conv_w = [p["ssm_conv_w"] for p in pairs]
            ymix = _matmul(act, w_ssm_out[j], "nn", F32, f"ssm_out_{i}")
            saved.append(dict(x=cur, h=h, proj=proj, pre=pre, xbc=xbc, dtb=dtb, acsb=acsb, dtr=dtr, acs_r=acs_r, y=y,
                              hin=hin, act=act, ymix=ymix))
        else:
            proj = _matmul(h, w_att_in[j], "nn", F32, f"att_in_{i}")
            qkv = _rope_fwd(proj, tables, f"rope_fwd_{i}")
            act, o, lse = _attn_fwd(qkv, proj, sinks[j], f"attn_fwd_{i}")
            ymix = _matmul(act, w_att_out[j], "nn", F32, f"att_out_{i}")
            saved.append(dict(x=cur, h=h, proj=proj, qkv=qkv, o=o, lse=lse, act=act, ymix=ymix))
        if i + 1 < DEPTH:
            cur, h = _post_fwd(cur, ymix, post_norm[i], pre_norm[i + 1], f"post_fwd_{i}")

    gr = {k: [None] * 2 for k in ("ssm_w_in", "ssm_conv_w", "ssm_conv_b", "ssm_dt_bias", "ssm_a_log", "ssm_d",
                                  "ssm_gate_norm", "ssm_w_out", "att_w_in", "att_sinks", "att_w_out")}
    gr["pre_norm"] = [None] * DEPTH
    gr["post_norm"] = [None] * DEPTH
    last = DEPTH - 1
    g, dymix, loss_lanes, gr["post_norm"][last] = _post_loss(cur, ymix, post_norm[last], target, "post_loss")
    for i in reversed(range(DEPTH)):
        j = i // 2
        s = saved[i]
        if i % 2 == 0:
            dact = _matmul(dymix, w_ssm_out[j], "nt", F32, f"ssm_out_dx_{i}")
            gr["ssm_w_out"][j] = _matmul(s["act"], dymix, "tn", F32, f"ssm_out_dw_{i}")
            dxbc, ddt8, dal, dd, dproj, gr["ssm_gate_norm"][j] = _ssd_bwd(
                s["xbc"], s["dtb"], s["acsb"], s["dtr"], s["acs_r"], a_log[j], d_lanes[j], s["hin"], dact, s["y"],
                s["proj"], gate_norm[j], f"ssd_bwd_{i}")
            gr["ssm_a_log"][j] = dal.reshape(SSM_HEADS)
            gr["ssm_d"][j] = dd.reshape(SSM_HEADS)
            l = x.shape[0]
            ddt = jnp.pad(jnp.transpose(ddt8, (2, 0, 1)).reshape(l, SSM_HEADS), ((0, 0), (0, SSM_DT_PAD - SSM_HEADS)))
            dproj, dbias = _dt_bwd(ddt, s["proj"], dt_bias_pad[j:j + 1], dproj, f"dt_bwd_{i}")
            gr["ssm_dt_bias"][j] = dbias[0, :SSM_HEADS]
            dproj, gr["ssm_conv_w"][j], dcb = _conv_bwd(dxbc, s["pre"], s["proj"], conv_w[j], dproj, f"conv_bwd_{i}")
            gr["ssm_conv_b"][j] = dcb[0]
            w_in, key = w_ssm_in[j], "ssm_w_in"
        else:
            dog = _matmul(dymix, w_att_out[j], "nt", F32, f"att_out_dx_{i}")
            gr["att_w_out"][j] = _matmul(s["act"], dymix, "tn", F32, f"att_out_dw_{i}")
            attn_bwd = functools.partial(_attn_bwd, s["qkv"], s["proj"], sinks[j], s["o"], s["lse"], dog, f"attn_bwd_{i}")
            if i == 1:
                (dq, dk, dv, dgate, dsk), second_pair_reduced = attn_bwd_with_second_pair_grads(
                    attn_bwd, {k: gr[k][1] for k in BIG})
            else:
                dq, dk, dv, dgate, dsk = attn_bwd()
            gr["att_sinks"][j] = dsk[0]
            dproj = _rope_bwd(dq, dk, dv, dgate, tables, f"rope_bwd_{i}")
            w_in, key = w_att_in[j], "att_w_in"
        gr[key][j] = _matmul(s["h"], dproj, "tn", F32, f"in_dw_{i}")
        in_dx = functools.partial(_matmul, dproj, w_in, "nt", F32, f"in_dx_{i}")
        if i == 0:
            dh, first_pair_reduced = in_dx_with_first_pair_grads(in_dx, {k: gr[k][0] for k in BIG})
        else:
            dh = in_dx()
        if i > 0:
            g, dymix, gr["pre_norm"][i], gr["post_norm"][i - 1] = _norm_bwd_chain(
                dh, s["x"], pre_norm[i], g, saved[i - 1]["ymix"], post_norm[i - 1], f"norm_bwd_{i}")
        else:
            g, gr["pre_norm"][i] = _rmsnorm_bwd(dh, s["x"], pre_norm[i], g, f"prenorm_bwd_{i}")
    grads = {k: jnp.stack([v.reshape(v.shape[-1]) if k in ("pre_norm", "post_norm", "ssm_gate_norm") else v for v in vs])
             for k, vs in gr.items() if k not in BIG}
    return loss_lanes, g, grads, first_pair_reduced, second_pair_reduced


N_CHIPS = 4
N_DEV = 8
MESH = pl.DeviceIdType.MESH
ANY = pl.BlockSpec(memory_space=pl.ANY)


def _place():
    x, y, c = lax.axis_index("x"), lax.axis_index("y"), lax.axis_index("c")
    return x, y, c, 2 * x + y


def _gather_sems(n):
    return [pltpu.SemaphoreType.DMA((n, N_CHIPS)), pltpu.SemaphoreType.DMA((n, N_CHIPS)), pltpu.SemaphoreType.DMA((n,))]


def _gather_between_chips(ins, outs, send_sems, recv_sems, local_sems, wait):
    n = len(ins)
    _, _, c, s = _place()
    local = [pltpu.make_async_copy(ins[w], outs[w].at[s], local_sems.at[w]) for w in range(n)]

    def remote(w, t):
        return pltpu.make_async_remote_copy(
            src_ref=ins[w].at[c], dst_ref=outs[w].at[s, c], send_sem=send_sems.at[w, t],
            recv_sem=recv_sems.at[w, s], device_id=(t // 2, t % 2, c), device_id_type=MESH)

    def arrival(w, t):
        return pltpu.make_async_remote_copy(
            src_ref=ins[w].at[c], dst_ref=outs[w].at[t, c], send_sem=send_sems.at[w, t],
            recv_sem=recv_sems.at[w, t], device_id=(t // 2, t % 2, c), device_id_type=MESH)

    if not wait:
        for cp in local:
            cp.start()
    for t in range(N_CHIPS):
        @pl.when(s != t)
        def _():
            for w in range(n):
                if wait:
                    remote(w, t).wait_send()
                    arrival(w, t).wait_recv()
                else:
                    remote(w, t).start()
    if wait:
        for cp in local:
            cp.wait()


def _pair_handoff(bufs, name):
    n = len(bufs)

    def body(*refs):
        outs = refs[n:2 * n]
        send_sems, recv_sems = refs[2 * n:]
        x, y, c, s = _place()

        def handed_on(w, t):
            return pltpu.make_async_remote_copy(
                src_ref=outs[w].at[t, c], dst_ref=outs[w].at[t, c], send_sem=send_sems.at[w, t],
                recv_sem=recv_sems.at[w, t], device_id=(x, y, 1 - c), device_id_type=MESH)

        def handed_in(w, t):
            return pltpu.make_async_remote_copy(
                src_ref=outs[w].at[t, 1 - c], dst_ref=outs[w].at[t, 1 - c], send_sem=send_sems.at[w, t],
                recv_sem=recv_sems.at[w, t], device_id=(x, y, 1 - c), device_id_type=MESH)

        for t in range(N_CHIPS):
            @pl.when(s != t)
            def _():
                for w in range(n):
                    handed_on(w, t).start()
        for t in range(N_CHIPS):
            @pl.when(s != t)
            def _():
                for w in range(n):
                    handed_on(w, t).wait_send()
                    handed_in(w, t).wait_recv()

    return pl.pallas_call(
        body, in_specs=[ANY] * n, out_specs=[ANY] * n,
        out_shape=[jax.ShapeDtypeStruct(a.shape, a.dtype) for a in bufs],
        scratch_shapes=[pltpu.SemaphoreType.DMA((n, N_CHIPS)), pltpu.SemaphoreType.DMA((n, N_CHIPS))],
        input_output_aliases={w: w for w in range(n)}, name=name,
    )(*bufs)


def _chip_gather(shards, name):
    n = len(shards)

    def body(*refs):
        ins, outs = refs[:n], refs[n:2 * n]
        _gather_between_chips(ins, outs, *refs[2 * n:], wait=False)
        _gather_between_chips(ins, outs, *refs[2 * n:], wait=True)

    bufs = pl.pallas_call(
        body, in_specs=[ANY] * n, out_specs=[ANY] * n,
        out_shape=[jax.ShapeDtypeStruct((N_CHIPS,) + a.shape, a.dtype) for a in shards],
        scratch_shapes=_gather_sems(n), name=name,
    )(*shards)
    return _pair_handoff(bufs, name + "_handoff")


def _pair_swap(parts, name):
    n = len(parts)

    def body(*refs):
        ins, outs = refs[:n], refs[n:2 * n]
        send_sems, recv_sems = refs[2 * n:]
        x, y, c, _ = _place()
        cps = [pltpu.make_async_remote_copy(
            src_ref=ins[w].at[1 - c], dst_ref=outs[w], send_sem=send_sems.at[w], recv_sem=recv_sems.at[w],
            device_id=(x, y, 1 - c), device_id_type=MESH) for w in range(n)]
        for cp in cps:
            cp.start()
        for cp in cps:
            cp.wait()

    return pl.pallas_call(
        body, in_specs=[ANY] * n, out_specs=[ANY] * n,
        out_shape=[jax.ShapeDtypeStruct(a.shape[1:], a.dtype) for a in parts],
        scratch_shapes=[pltpu.SemaphoreType.DMA((n,)), pltpu.SemaphoreType.DMA((n,))],
        name=name,
    )(*parts)


def _scatter_between_chips(ins, outs, send_sems, recv_sems, local_sems, wait):
    n = len(ins)
    _, _, c, s = _place()

    def block(w, t):
        rows = ins[w].shape[0] // N_CHIPS
        return ins[w].at[pl.ds(t * rows, rows)]

    local = [pltpu.make_async_copy(block(w, s), outs[w].at[s], local_sems.at[w]) for w in range(n)]

    def remote(w, t):
        return pltpu.make_async_remote_copy(
            src_ref=block(w, t), dst_ref=outs[w].at[s], send_sem=send_sems.at[w, t], recv_sem=recv_sems.at[w, s],
            device_id=(t // 2, t % 2, c), device_id_type=MESH)

    def arrival(w, t):
        return pltpu.make_async_remote_copy(
            src_ref=block(w, t), dst_ref=outs[w].at[t], send_sem=send_sems.at[w, t], recv_sem=recv_sems.at[w, t],
            device_id=(t // 2, t % 2, c), device_id_type=MESH)

    if not wait:
        for cp in local:
            cp.start()
    for t in range(N_CHIPS):
        @pl.when(s != t)
        def _():
            for w in range(n):
                if wait:
                    remote(w, t).wait_send()
                    arrival(w, t).wait_recv()
                else:
                    remote(w, t).start()
    if wait:
        for cp in local:
            cp.wait()


def _scatter_shapes(parts):
    return [jax.ShapeDtypeStruct((N_CHIPS, a.shape[0] // N_CHIPS, a.shape[1]), a.dtype) for a in parts]


def _pair_merge(parts, name):
    n = len(parts)

    def body(*refs):
        ins, outs = refs[:n], refs[n:2 * n]
        send_sems, recv_sems = refs[2 * n:]
        x, y, c, _ = _place()
        cps = [pltpu.make_async_remote_copy(
            src_ref=ins[w], dst_ref=outs[w], send_sem=send_sems.at[w], recv_sem=recv_sems.at[w],
            device_id=(x, y, 1 - c), device_id_type=MESH) for w in range(n)]
        for cp in cps:
            cp.start()
        for cp in cps:
            cp.wait()

    return pl.pallas_call(
        body, in_specs=[ANY] * n, out_specs=[ANY] * n,
        out_shape=[jax.ShapeDtypeStruct(a.shape, a.dtype) for a in parts],
        scratch_shapes=[pltpu.SemaphoreType.DMA((n,)), pltpu.SemaphoreType.DMA((n,))],
        name=name,
    )(*parts)


def _all_gather_small(a, name):
    def body(in_ref, out_ref, send_sems, recv_sems, local_sem):
        x, y, c, _ = _place()
        me = 4 * x + 2 * y + c
        local = pltpu.make_async_copy(in_ref, out_ref.at[me], local_sem)
        local.start()

        def remote(d):
            return pltpu.make_async_remote_copy(
                src_ref=in_ref, dst_ref=out_ref.at[me], send_sem=send_sems.at[d], recv_sem=recv_sems.at[me],
                device_id=(d // 4, (d // 2) % 2, d % 2), device_id_type=MESH)

        def arrival(d):
            return pltpu.make_async_remote_copy(
                src_ref=in_ref, dst_ref=out_ref.at[d], send_sem=send_sems.at[d], recv_sem=recv_sems.at[d],
                device_id=(d // 4, (d // 2) % 2, d % 2), device_id_type=MESH)

        for d in range(N_DEV):
            @pl.when(me != d)
            def _():
                remote(d).start()
        for d in range(N_DEV):
            @pl.when(me != d)
            def _():
                remote(d).wait_send()
                arrival(d).wait_recv()
        local.wait()

    return pl.pallas_call(
        body, in_specs=[ANY], out_specs=ANY, out_shape=jax.ShapeDtypeStruct((N_DEV,) + a.shape, a.dtype),
        scratch_shapes=[pltpu.SemaphoreType.DMA((N_DEV,)), pltpu.SemaphoreType.DMA((N_DEV,)), pltpu.SemaphoreType.DMA],
        name=name,
    )(a)


def _reduce_tile(rows):
    return _pick(rows, (256, 128, 16))


def _pair_add(full, other, layer, name):
    _, rows, cols = full.shape
    tr = _reduce_tile(rows)

    def body(layer_ref, a_ref, b_ref, o_ref):
        o_ref[...] = (a_ref[0] + b_ref[...]).astype(o_ref.dtype)

    return pl.pallas_call(
        body,
        grid_spec=pltpu.PrefetchScalarGridSpec(
            num_scalar_prefetch=1, grid=(rows // tr,),
            in_specs=[pl.BlockSpec((1, tr, cols), lambda i, lr: (lr[0], i, 0)), pl.BlockSpec((tr, cols), lambda i, lr: (i, 0))],
            out_specs=pl.BlockSpec((tr, cols), lambda i, lr: (i, 0))),
        out_shape=jax.ShapeDtypeStruct((rows, cols), BF16), compiler_params=_params("parallel"), name=name,
    )(layer, full, other)


def _sum_slots(a, name):
    n, rows, cols = a.shape
    tr = _reduce_tile(rows)

    def body(a_ref, o_ref):
        acc = a_ref[0].astype(F32)
        for k in range(1, n):
            acc = acc + a_ref[k].astype(F32)
        o_ref[...] = acc

    return pl.pallas_call(
        body, grid=(rows // tr,), in_specs=[pl.BlockSpec((n, tr, cols), lambda i: (0, i, 0))],
        out_specs=pl.BlockSpec((tr, cols), lambda i: (i, 0)),
        out_shape=jax.ShapeDtypeStruct((rows, cols), F32), compiler_params=_params("parallel"), name=name,
    )(a)


def _adamw(w, g, m, v, name):
    rows, cols = w.shape
    tr = _pick(rows, (256, 8))

    def body(w_ref, g_ref, m_ref, v_ref, d_ref, nm_ref, nv_ref):
        gv = g_ref[...]
        mn = ADAM_B1 * m_ref[...] + (1.0 - ADAM_B1) * gv
        vn = ADAM_B2 * v_ref[...] + (1.0 - ADAM_B2) * jnp.square(gv)
        m_hat = mn / (1.0 - ADAM_B1 ** ADAM_STEP)
        v_hat = vn / (1.0 - ADAM_B2 ** ADAM_STEP)
        d_ref[...] = -ADAM_LR * (m_hat / (jnp.sqrt(v_hat) + ADAM_EPS) + ADAM_WD * w_ref[...])
        nm_ref[...] = mn
        nv_ref[...] = vn

    blk = pl.BlockSpec((tr, cols), lambda i: (i, 0))
    return pl.pallas_call(
        body, grid=(rows // tr,), in_specs=[blk] * 4, out_specs=[blk] * 3,
        out_shape=[jax.ShapeDtypeStruct((rows, cols), F32)] * 3, compiler_params=_params("parallel"), name=name,
    )(w, g, m, v)


BIG = ("ssm_w_in", "ssm_w_out", "att_w_in", "att_w_out")
SHARDED = BIG + ("ssm_conv_w",)
SMALL = ("pre_norm", "post_norm", "ssm_conv_b", "ssm_dt_bias", "ssm_a_log", "ssm_d", "ssm_gate_norm", "att_sinks")
WEIGHTS = ("pre_norm", "post_norm", "ssm_w_in", "ssm_conv_w", "ssm_conv_b", "ssm_dt_bias", "ssm_a_log", "ssm_d",
           "ssm_gate_norm", "ssm_w_out", "att_w_in", "att_sinks", "att_w_out")


def _halves(a):
    return a.reshape(2, a.shape[0] // 2, a.shape[1])


def _layer_shards(j, ssm_w_in, ssm_w_out, att_w_in, att_w_out, ssm_conv_w):
    return [_halves(ssm_w_in[j].astype(BF16)), _halves(ssm_w_out[j].astype(BF16)), _halves(att_w_in[j].astype(BF16)),
            _halves(att_w_out[j].astype(BF16)), _halves(ssm_conv_w[j])]


SHARD_KEYS = ("ssm_w_in", "ssm_w_out", "att_w_in", "att_w_out", "ssm_conv_w")


def _whole_weights(keys, gathered):
    out = {}
    for k, g in zip(keys, gathered):
        g = g.reshape((N_CHIPS, 2 * g.shape[2], g.shape[3]))
        if k in ("ssm_w_out", "att_w_out"):
            out[k] = g.reshape(N_CHIPS * g.shape[1], g.shape[2])
        else:
            out[k] = jnp.transpose(g, (1, 0, 2)).reshape(g.shape[1], N_CHIPS * g.shape[2])
    if "ssm_w_in" in out:
        out["ssm_w_in"] = jnp.pad(out["ssm_w_in"], ((0, 0), (0, SSM_IN_PAD - SSM_IN_DIM)))
    return out


def _halves_by_chip(key, g):
    if key in ("ssm_w_out", "att_w_out"):
        rows = g.shape[0] // N_CHIPS
        blocks = g.reshape(N_CHIPS, 2, rows // 2, g.shape[1])
        return jnp.transpose(blocks, (1, 0, 2, 3)).reshape(2, N_CHIPS * (rows // 2), g.shape[1])
    cols = (SSM_IN_DIM if key == "ssm_w_in" else g.shape[1]) // N_CHIPS
    rows = g.shape[0]
    blocks = g[:, :N_CHIPS * cols].reshape(2, rows // 2, N_CHIPS, cols)
    return jnp.transpose(blocks, (0, 2, 1, 3)).reshape(2, N_CHIPS * (rows // 2), cols)


def _pack_small(tree, keys):
    flat = jnp.concatenate([tree[k].reshape(-1) for k in keys])
    rows = -(-flat.shape[0] // (8 * LANES)) * 8
    return jnp.pad(flat, (0, rows * LANES - flat.shape[0])).reshape(rows, LANES)


def _unpack_small(packed, shapes, keys):
    flat = packed.reshape(-1)
    out, at = {}, 0
    for k in keys:
        n = 1
        for dim in shapes[k]:
            n *= dim
        out[k] = flat[at:at + n].reshape(shapes[k])
        at += n
    return out


def kernel(x, positions, pre_norm, post_norm, ssm_w_in, ssm_conv_w, ssm_conv_b, ssm_dt_bias, ssm_a_log, ssm_d, ssm_gate_norm, ssm_w_out, att_w_in, att_sinks, att_w_out, loss_target, m_pre_norm, m_post_norm, m_ssm_w_in, m_ssm_conv_w, m_ssm_conv_b, m_ssm_dt_bias, m_ssm_a_log, m_ssm_d, m_ssm_gate_norm, m_ssm_w_out, m_att_w_in, m_att_sinks, m_att_w_out, v_pre_norm, v_post_norm, v_ssm_w_in, v_ssm_conv_w, v_ssm_conv_b, v_ssm_dt_bias, v_ssm_a_log, v_ssm_d, v_ssm_gate_norm, v_ssm_w_out, v_att_w_in, v_att_sinks, v_att_w_out):
    w = dict(pre_norm=pre_norm, post_norm=post_norm, ssm_w_in=ssm_w_in, ssm_conv_w=ssm_conv_w, ssm_conv_b=ssm_conv_b,
             ssm_dt_bias=ssm_dt_bias, ssm_a_log=ssm_a_log, ssm_d=ssm_d, ssm_gate_norm=ssm_gate_norm, ssm_w_out=ssm_w_out,
             att_w_in=att_w_in, att_sinks=att_sinks, att_w_out=att_w_out)
    m = dict(pre_norm=m_pre_norm, post_norm=m_post_norm, ssm_w_in=m_ssm_w_in, ssm_conv_w=m_ssm_conv_w, ssm_conv_b=m_ssm_conv_b,
             ssm_dt_bias=m_ssm_dt_bias, ssm_a_log=m_ssm_a_log, ssm_d=m_ssm_d, ssm_gate_norm=m_ssm_gate_norm,
             ssm_w_out=m_ssm_w_out, att_w_in=m_att_w_in, att_sinks=m_att_sinks, att_w_out=m_att_w_out)
    v = dict(pre_norm=v_pre_norm, post_norm=v_post_norm, ssm_w_in=v_ssm_w_in, ssm_conv_w=v_ssm_conv_w, ssm_conv_b=v_ssm_conv_b,
             ssm_dt_bias=v_ssm_dt_bias, ssm_a_log=v_ssm_a_log, ssm_d=v_ssm_d, ssm_gate_norm=v_ssm_gate_norm,
             ssm_w_out=v_ssm_w_out, att_w_in=v_att_w_in, att_sinks=v_att_sinks, att_w_out=v_att_w_out)
    c = lax.axis_index("c")
    chip = 2 * lax.axis_index("x") + lax.axis_index("y")

    sharded = (ssm_w_in, ssm_w_out, att_w_in, att_w_out, ssm_conv_w)
    own = [dict(zip(SHARD_KEYS, _layer_shards(j, *sharded))) for j in range(2)]
    now_keys = ("ssm_w_in", "ssm_conv_w")
    later_keys = ("ssm_w_out", "att_w_in", "att_w_out")
    first_in = _whole_weights(now_keys, _chip_gather([own[0][k] for k in now_keys], "gather_weights_0"))

    def in_proj_with_first_pair(matmul):
        proj, *arrived = matmul(ride=[own[0][k] for k in later_keys])
        return proj, _whole_weights(later_keys, _pair_handoff(arrived, "gather_weights_0_rest_handoff"))

    def scan_with_second_pair(scan):
        results = scan(ride=[own[1][k] for k in SHARD_KEYS])
        scanned, arrived = results[:-len(SHARD_KEYS)], results[-len(SHARD_KEYS):]
        return (*scanned, _whole_weights(SHARD_KEYS, _pair_handoff(arrived, "gather_weights_1_handoff")))

    half = jnp.reshape(c, (1,)).astype(jnp.int32)

    def reduce_begin(pair_grads, tag):
        parts = [_halves_by_chip(k, pair_grads[k]) for k in BIG]
        from_sibling = _pair_swap(parts, f"reduce_pair_swap_{tag}")
        return [_pair_add(p, o, half, f"reduce_pair_add_{tag}_{n}") for n, (p, o) in enumerate(zip(parts, from_sibling))]

    def reduce_end(by_chip, tag):
        mine = [_sum_slots(a, f"reduce_chip_sum_{tag}_{n}") for n, a in enumerate(by_chip)]
        theirs = _pair_merge(mine, f"reduce_pair_merge_{tag}")
        return {k: jnp.where(c == 0, jnp.concatenate([a, b]), jnp.concatenate([b, a])) for k, a, b in zip(BIG, mine, theirs)}

    def attn_bwd_with_second_pair_grads(attn_bwd, pair_grads):
        dq, dk, dv, dgate, dsk, *by_chip = attn_bwd(ride=reduce_begin(pair_grads, "1"))
        return (dq, dk, dv, dgate, dsk), reduce_end(by_chip, "1")

    def in_dx_with_first_pair_grads(matmul, pair_grads):
        dh, *by_chip = matmul(ride=reduce_begin(pair_grads, "0"), ride_scatters=True)
        return dh, reduce_end(by_chip, "0")

    loss_lanes, grad_x, gr, reduced_0, reduced_1 = _local_step(
        x[0], positions[0], pre_norm, post_norm, ssm_conv_b, ssm_dt_bias, ssm_a_log, ssm_d, ssm_gate_norm, att_sinks,
        loss_target[0], first_in, in_proj_with_first_pair, scan_with_second_pair, attn_bwd_with_second_pair_grads,
        in_dx_with_first_pair_grads)
    loss = lax.psum(0.5 * jnp.sum(loss_lanes) / D_MODEL, ("x", "y", "c"))
    grads = {k: jnp.stack([reduced_0[k], reduced_1[k]]) for k in BIG}

    small_keys = SMALL + ("ssm_conv_w",)
    small_shapes = {k: w[k].shape for k in SMALL}
    small_shapes["ssm_conv_w"] = gr["ssm_conv_w"].shape
    small_sum = _sum_slots(_all_gather_small(_pack_small(gr, small_keys), "reduce_small_gather"), "reduce_small_sum")
    grads.update(_unpack_small(small_sum, small_shapes, small_keys))
    conv_cols = ssm_conv_w.shape[2]
    grads["ssm_conv_w"] = lax.dynamic_slice_in_dim(grads["ssm_conv_w"], chip * conv_cols, conv_cols, axis=2)

    delta, new_m, new_v = {}, {}, {}
    for k in SHARDED:
        shp = w[k].shape
        two_d = (shp[0] * shp[1], shp[2])
        d_, m_, v_ = _adamw(w[k].reshape(two_d), grads[k].reshape(two_d), m[k].reshape(two_d), v[k].reshape(two_d),
                            f"adamw_{k}")
        delta[k], new_m[k], new_v[k] = d_.reshape(shp), m_.reshape(shp), v_.reshape(shp)
    d_, m_, v_ = _adamw(_pack_small(w, SMALL), _pack_small(grads, SMALL), _pack_small(m, SMALL), _pack_small(v, SMALL),
                        "adamw_small")
    delta.update(_unpack_small(d_, small_shapes, SMALL))
    new_m.update(_unpack_small(m_, small_shapes, SMALL))
    new_v.update(_unpack_small(v_, small_shapes, SMALL))

    return (loss, grad_x[None], *[grads[k] for k in WEIGHTS], *[delta[k] for k in WEIGHTS],
            *[new_m[k] for k in WEIGHTS], *[new_v[k] for k in WEIGHTS])
```

```python
import functools

import jax
import jax.numpy as jnp
from jax import lax
from jax.experimental import pallas as pl
from jax.experimental.pallas import tpu as pltpu

F32 = jnp.float32
BF16 = jnp.bfloat16
EPS = 1e-6
NEG_INF = float("-inf")

D_MODEL = 1024
DEPTH = 4
SSM_D_INNER = 2048
SSM_HEAD_DIM = 64
SSM_HEADS = 32
SSM_GROUPS = 8
SSM_HPG = 4
SSM_STATE = 128
SSM_CONV = 4
SSM_CHUNK = 128
SSM_BC_DIM = 1024
SSM_CONV_DIM = 4096
SSM_IN_DIM = 6176
SSM_IN_PAD = 6272
SSM_DT_PAD = 128
ATT_HEAD_DIM = 64
ATT_Q_HEADS = 16
ATT_KV_HEADS = 4
ATT_GQA = 4
ATT_WIDTH = 1024
ATT_KV_WIDTH = 256
ATT_IN_DIM = 2560
ATT_QKV = ATT_WIDTH + 2 * ATT_KV_WIDTH
ATT_BLOCK = 128
ROPE_THETA = 500000.0
ROPE_DIM = 16
ROPE_HALF = 8
Q_SCALE = ATT_HEAD_DIM ** -0.5

ADAM_LR = 0.001
ADAM_B1 = 0.9
ADAM_B2 = 0.999
ADAM_EPS = 1e-08
ADAM_WD = 0.01
ADAM_STEP = 10

VMEM_LIMIT_BYTES = 48 * 1024 * 1024
NT_DIMS = (((1,), (1,)), ((), ()))
TN_DIMS = (((0,), (0,)), ((), ()))


def _params(*sem):
    return pltpu.CompilerParams(dimension_semantics=sem, vmem_limit_bytes=VMEM_LIMIT_BYTES)


def _pick(n, cands):
    for c in cands:
        if n % c == 0:
            return c
    return n


def _sigmoid(v):
    return 0.5 * jnp.tanh(0.5 * v) + 0.5


def _bdot_tn(a, b):
    return lax.dot_general(a.astype(BF16), b.astype(BF16), TN_DIMS, preferred_element_type=F32)


MATMUL_VMEM_BUDGET = 36 * 1024 * 1024


def _matmul_tiles(m, n, k, out_bytes, reduce_rows):
    best = None
    whole = [k] if (not reduce_rows or k <= 2048) else []
    for tk in whole + [c for c in (4096, 2048, 1024, 896, 512) if k % c == 0 and c < k]:
        for tm in (c for c in (2048, 1024, 512, 256) if m % c == 0):
            for tn in (c for c in (n, 1280, 1024, 896, 640, 512) if n % c == 0):
                acc = tm * tn * 4 if tk < k else 0
                need = 2 * (2 * tk * (tm + tn) + tm * tn * out_bytes) + acc
                if need <= MATMUL_VMEM_BUDGET and (best is None or tm * tn * min(tk, 2048) > best[0]):
                    best = (tm * tn * min(tk, 2048), tm, tn, tk)
        if best is not None and not reduce_rows:
            break
    return best[1:]


def _matmul(a, b, mode, out_dtype, name, ride=(), ride_scatters=False):
    if mode == "nn":
        (m, k), n = a.shape, b.shape[1]
    elif mode == "nt":
        (m, k), n = a.shape, b.shape[0]
    else:
        (k, m), n = a.shape, b.shape[1]
    tm, tn, tk = _matmul_tiles(m, n, k, jnp.dtype(out_dtype).itemsize, mode == "tn")
    nk = k // tk
    steps = (n // tn, m // tm, nk)
    dims = {"nn": (((1,), (0,)), ((), ())), "nt": NT_DIMS, "tn": TN_DIMS}[mode]
    n_ride = len(ride)
    exchange = _scatter_between_chips if ride_scatters else _gather_between_chips
    arrived = _scatter_shapes(ride) if ride_scatters else [jax.ShapeDtypeStruct((N_CHIPS,) + r.shape, r.dtype) for r in ride]

    def body(*refs):
        a_ref, b_ref = refs[:2]
        ride_in = refs[2:2 + n_ride]
        o_ref = refs[2 + n_ride]
        ride_out = refs[3 + n_ride:3 + 2 * n_ride]
        acc_ref = refs[3 + 2 * n_ride]
        ride_sems = refs[4 + 2 * n_ride:]
        kk = pl.program_id(2)
        at = [pl.program_id(d) for d in range(3)]
        if n_ride:
            @pl.when((at[0] == 0) & (at[1] == 0) & (at[2] == 0))
            def _():
                exchange(ride_in, ride_out, *ride_sems, wait=False)

        part = lax.dot_general(a_ref[...], b_ref[...], dims, preferred_element_type=F32)
        if nk == 1:
            o_ref[...] = part.astype(o_ref.dtype)
        else:
            @pl.when(kk == 0)
            def _():
                acc_ref[...] = part

            @pl.when(kk > 0)
            def _():
                acc_ref[...] += part

            @pl.when(kk == nk - 1)
            def _():
                o_ref[...] = acc_ref[...].astype(o_ref.dtype)

        if n_ride:
            @pl.when((at[0] == steps[0] - 1) & (at[1] == steps[1] - 1) & (at[2] == steps[2] - 1))
            def _():
                exchange(ride_in, ride_out, *ride_sems, wait=True)

    if mode == "nn":
        a_spec = pl.BlockSpec((tm, tk), lambda j, i, kk: (i, kk))
        b_spec = pl.BlockSpec((tk, tn), lambda j, i, kk: (kk, j))
    elif mode == "nt":
        a_spec = pl.BlockSpec((tm, tk), lambda j, i, kk: (i, kk))
        b_spec = pl.BlockSpec((tn, tk), lambda j, i, kk: (j, kk))
    else:
        a_spec = pl.BlockSpec((tk, tm), lambda j, i, kk: (kk, i))
        b_spec = pl.BlockSpec((tk, tn), lambda j, i, kk: (kk, j))
    out = pl.pallas_call(
        body, grid=steps, in_specs=[a_spec, b_spec] + [ANY] * n_ride,
        out_specs=[pl.BlockSpec((tm, tn), lambda j, i, kk: (i, j))] + [ANY] * n_ride,
        out_shape=[jax.ShapeDtypeStruct((m, n), out_dtype)] + arrived,
        scratch_shapes=[pltpu.VMEM((tm, tn), F32)] + (_gather_sems(n_ride) if n_ride else []),
        compiler_params=_params(*(["arbitrary"] * 3 if n_ride else ["parallel", "parallel", "arbitrary"])), name=name,
    )(a, b, *ride)
    return out if n_ride else out[0]


def _row_tile(l):
    return _pick(l, (512, 256, 128))


def _rmsnorm_fwd(x, w, name):
    l, d = x.shape
    tl = _row_tile(l)

    def body(x_ref, w_ref, o_ref):
        xv = x_ref[...]
        r = lax.rsqrt(jnp.mean(xv * xv, axis=-1, keepdims=True) + EPS)
        o_ref[...] = (xv * r * w_ref[...]).astype(o_ref.dtype)

    return pl.pallas_call(
        body, grid=(l // tl,),
        in_specs=[pl.BlockSpec((tl, d), lambda i: (i, 0)), pl.BlockSpec((1, d), lambda i: (0, 0))],
        out_specs=pl.BlockSpec((tl, d), lambda i: (i, 0)),
        out_shape=jax.ShapeDtypeStruct((l, d), BF16), compiler_params=_params("parallel"), name=name,
    )(x, w.reshape(1, d))


def _post_fwd(x, y, w, w_next, name):
    l, d = x.shape
    tl = _row_tile(l)

    def body(x_ref, y_ref, w_ref, wn_ref, o_ref, h_ref):
        yv = y_ref[...]
        r = lax.rsqrt(jnp.mean(yv * yv, axis=-1, keepdims=True) + EPS)
        out = x_ref[...] + yv * r * w_ref[...]
        o_ref[...] = out
        rn = lax.rsqrt(jnp.mean(out * out, axis=-1, keepdims=True) + EPS)
        h_ref[...] = (out * rn * wn_ref[...]).astype(h_ref.dtype)

    row = pl.BlockSpec((tl, d), lambda i: (i, 0))
    vec = pl.BlockSpec((1, d), lambda i: (0, 0))
    return pl.pallas_call(
        body, grid=(l // tl,), in_specs=[row, row, vec, vec], out_specs=[row, row],
        out_shape=[jax.ShapeDtypeStruct((l, d), F32), jax.ShapeDtypeStruct((l, d), BF16)],
        compiler_params=_params("parallel"), name=name,
    )(x, y, w.reshape(1, d), w_next.reshape(1, d))


def _post_loss(x, y, w, t, name):
    l, d = x.shape
    tl = _row_tile(l)
    nt = l // tl

    def body(x_ref, y_ref, w_ref, t_ref, g_ref, dy_ref, ls_ref, dw_ref, acc_ref):
        i = pl.program_id(0)

        @pl.when(i == 0)
        def _():
            ls_ref[...] = jnp.zeros_like(ls_ref)
            acc_ref[...] = jnp.zeros_like(acc_ref)

        yv = y_ref[...]
        r = lax.rsqrt(jnp.mean(yv * yv, axis=-1, keepdims=True) + EPS)
        nrm = yv * r
        e = x_ref[...] + nrm * w_ref[...] - t_ref[...]
        gv = e * (1.0 / d)
        g_ref[...] = gv
        ls_ref[...] += jnp.sum((e * e).reshape(tl // 8, 8, d), axis=0)
        gw = gv * w_ref[...]
        dy_ref[...] = (r * (gw - nrm * jnp.mean(gw * nrm, axis=-1, keepdims=True))).astype(dy_ref.dtype)
        acc_ref[...] += jnp.sum((gv * nrm).reshape(tl // 8, 8, d), axis=0)

        @pl.when(i == nt - 1)
        def _():
            dw_ref[...] = jnp.sum(acc_ref[...], axis=0, keepdims=True)

    row = pl.BlockSpec((tl, d), lambda i: (i, 0))
    vec = pl.BlockSpec((1, d), lambda i: (0, 0))
    return pl.pallas_call(
        body, grid=(nt,), in_specs=[row, row, vec, row],
        out_specs=[row, row, pl.BlockSpec((8, d), lambda i: (0, 0)), vec],
        out_shape=[jax.ShapeDtypeStruct((l, d), F32), jax.ShapeDtypeStruct((l, d), BF16),
                   jax.ShapeDtypeStruct((8, d), F32), jax.ShapeDtypeStruct((1, d), F32)],
        scratch_shapes=[pltpu.VMEM((8, d), F32)], compiler_params=_params("arbitrary"), name=name,
    )(x, y, w.reshape(1, d), t)


def _norm_bwd_chain(dh, x, w_pre, resid, y_prev, w_post_prev, name):
    l, d = x.shape
    tl = _row_tile(l)
    nt = l // tl

    def body(dh_ref, x_ref, wp_ref, r_ref, y_ref, wq_ref, g_ref, dy_ref, dwp_ref, dwq_ref, accp_ref, accq_ref):
        i = pl.program_id(0)

        @pl.when(i == 0)
        def _():
            accp_ref[...] = jnp.zeros_like(accp_ref)
            accq_ref[...] = jnp.zeros_like(accq_ref)

        xv = x_ref[...]
        dhv = dh_ref[...]
        rx = lax.rsqrt(jnp.mean(xv * xv, axis=-1, keepdims=True) + EPS)
        nx = xv * rx
        gw = dhv * wp_ref[...]
        gv = rx * (gw - nx * jnp.mean(gw * nx, axis=-1, keepdims=True)) + r_ref[...]
        g_ref[...] = gv
        accp_ref[...] += jnp.sum((dhv * nx).reshape(tl // 8, 8, d), axis=0)
        yv = y_ref[...]
        ry = lax.rsqrt(jnp.mean(yv * yv, axis=-1, keepdims=True) + EPS)
        ny = yv * ry
        gq = gv * wq_ref[...]
        dy_ref[...] = (ry * (gq - ny * jnp.mean(gq * ny, axis=-1, keepdims=True))).astype(dy_ref.dtype)
        accq_ref[...] += jnp.sum((gv * ny).reshape(tl // 8, 8, d), axis=0)

        @pl.when(i == nt - 1)
        def _():
            dwp_ref[...] = jnp.sum(accp_ref[...], axis=0, keepdims=True)
            dwq_ref[...] = jnp.sum(accq_ref[...], axis=0, keepdims=True)

    row = pl.BlockSpec((tl, d), lambda i: (i, 0))
    vec = pl.BlockSpec((1, d), lambda i: (0, 0))
    return pl.pallas_call(
        body, grid=(nt,), in_specs=[row, row, vec, row, row, vec], out_specs=[row, row, vec, vec],
        out_shape=[jax.ShapeDtypeStruct((l, d), F32), jax.ShapeDtypeStruct((l, d), BF16),
                   jax.ShapeDtypeStruct((1, d), F32), jax.ShapeDtypeStruct((1, d), F32)],
        scratch_shapes=[pltpu.VMEM((8, d), F32), pltpu.VMEM((8, d), F32)],
        compiler_params=_params("arbitrary"), name=name,
    )(dh, x, w_pre.reshape(1, d), resid, y_prev, w_post_prev.reshape(1, d))


def _rmsnorm_bwd(g, y, w, resid, name):
    l, d = y.shape
    tl = _row_tile(l)
    nt = l // tl

    def body(g_ref, y_ref, w_ref, r_ref, dy_ref, dw_ref, acc_ref):
        i = pl.program_id(0)

        @pl.when(i == 0)
        def _():
            acc_ref[...] = jnp.zeros_like(acc_ref)

        yv = y_ref[...]
        gv = g_ref[...]
        r = lax.rsqrt(jnp.mean(yv * yv, axis=-1, keepdims=True) + EPS)
        nrm = yv * r
        gw = gv * w_ref[...]
        dy_ref[...] = r * (gw - nrm * jnp.mean(gw * nrm, axis=-1, keepdims=True)) + r_ref[...]
        acc_ref[...] += jnp.sum((gv * nrm).reshape(tl // 8, 8, d), axis=0)

        @pl.when(i == nt - 1)
        def _():
            dw_ref[...] = jnp.sum(acc_ref[...], axis=0, keepdims=True)

    row = pl.BlockSpec((tl, d), lambda i: (i, 0))
    vec = pl.BlockSpec((1, d), lambda i: (0, 0))
    return pl.pallas_call(
        body, grid=(nt,), in_specs=[row, row, vec, row], out_specs=[row, vec],
        out_shape=[jax.ShapeDtypeStruct((l, d), F32), jax.ShapeDtypeStruct((1, d), F32)],
        scratch_shapes=[pltpu.VMEM((8, d), F32)], compiler_params=_params("arbitrary"), name=name,
    )(g, y, w.reshape(1, d), resid)


HALO = 8
CONV_SUB_ROWS = 64
CONV_SUB_COLS = 256


DT_COL_BLOCK = (SSM_D_INNER + SSM_CONV_DIM) // SSM_DT_PAD


def _split3(v):
    hi = v.astype(BF16)
    rest = v - hi.astype(F32)
    mid = rest.astype(BF16)
    lo = (rest - mid.astype(F32)).astype(BF16)
    return hi, mid, lo


def _dt_and_decay(v, a_log):
    head_dim_log2 = SSM_HEAD_DIM.bit_length() - 1
    dt_hi, dt_mid, _ = _split3(jnp.maximum(v, 0.0) + jnp.log1p(jnp.exp(-jnp.abs(v))))
    dt = dt_hi.astype(F32) + dt_mid.astype(F32)
    ri = lax.broadcasted_iota(jnp.int32, (SSM_CHUNK, SSM_CHUNK), 0)
    cj = lax.broadcasted_iota(jnp.int32, (SSM_CHUNK, SSM_CHUNK), 1)
    tri = (ri >= cj).astype(BF16)
    acs_pieces = _split3(sum(jnp.dot(tri, piece, preferred_element_type=F32)
                             for piece in _split3(dt * (-jnp.exp(a_log)))))
    acs = sum(piece.astype(F32) for piece in acs_pieces)
    head_of_lane = lax.shift_right_logical(lax.broadcasted_iota(jnp.int32, (SSM_DT_PAD, SSM_D_INNER), 1), head_dim_log2)
    spread = (head_of_lane == lax.broadcasted_iota(jnp.int32, (SSM_DT_PAD, SSM_D_INNER), 0)).astype(BF16)
    dtb = sum(jnp.dot(piece, spread, preferred_element_type=F32) for piece in (dt_hi, dt_mid))
    acsb = sum(jnp.dot(piece, spread, preferred_element_type=F32) for piece in acs_pieces)
    return dtb, acsb, dt.T, acs.T


def _dt_bwd(ddt, proj, bias, dproj, name):
    l = proj.shape[0]
    tl = _row_tile(l)

    def body(g_ref, p_ref, b_ref, _, o_ref, db_ref):
        @pl.when(pl.program_id(0) == 0)
        def _():
            db_ref[...] = jnp.zeros_like(db_ref)

        d = g_ref[...] * _sigmoid(p_ref[...] + b_ref[...])
        o_ref[...] = d.astype(o_ref.dtype)
        db_ref[...] += jnp.sum(d, axis=0, keepdims=True)

    return pl.pallas_call(
        body, grid=(l // tl,),
        in_specs=[pl.BlockSpec((tl, SSM_DT_PAD), lambda i: (i, 0)),
                  pl.BlockSpec((tl, SSM_DT_PAD), lambda i: (i, DT_COL_BLOCK)),
                  pl.BlockSpec((1, SSM_DT_PAD), lambda i: (0, 0)),
                  pl.BlockSpec(memory_space=pl.ANY)],
        out_specs=[pl.BlockSpec((tl, SSM_DT_PAD), lambda i: (i, DT_COL_BLOCK)),
                   pl.BlockSpec((1, SSM_DT_PAD), lambda i: (0, 0))],
        out_shape=[jax.ShapeDtypeStruct(dproj.shape, dproj.dtype), jax.ShapeDtypeStruct((1, SSM_DT_PAD), F32)],
        input_output_aliases={3: 0}, compiler_params=_params("arbitrary"), name=name,
    )(ddt, proj, bias, dproj)


GP = SSM_HPG * SSM_HEAD_DIM
HEAD_DIM_LOG2 = SSM_HEAD_DIM.bit_length() - 1
GPS = SSM_GROUPS
B_BLOCK0 = SSM_D_INNER // SSM_STATE
C_BLOCK0 = (SSM_D_INNER + SSM_BC_DIM) // SSM_STATE


def _chunk_iotas():
    ri = lax.broadcasted_iota(jnp.int32, (SSM_CHUNK, SSM_CHUNK), 0)
    cj = lax.broadcasted_iota(jnp.int32, (SSM_CHUNK, SSM_CHUNK), 1)
    return ri, cj


def _head_decay(acsb, acs_r, r, ri, cj):
    pair = acsb[:, (r // 2) * LANES:(r // 2 + 1) * LANES]
    mine_low = r % 2 == 0
    lane = lax.broadcasted_iota(jnp.int32, (1, LANES), 1)
    col = jnp.where((lane < SSM_HEAD_DIM) == mine_low, pair, pltpu.roll(pair, SSM_HEAD_DIM, 1))
    return jnp.exp(jnp.where(ri >= cj, col - acs_r[r:r + 1, :], NEG_INF))


def _head_masked_rows(v, dtype):
    head_of_lane = lax.shift_right_logical(lax.broadcasted_iota(jnp.int32, (1, GP), 1), HEAD_DIM_LOG2)
    narrow = v.astype(dtype)
    return jnp.concatenate([jnp.where(head_of_lane == r, narrow, jnp.zeros_like(narrow)) for r in range(SSM_HPG)], axis=0)


def _ssd_fwd(proj, cw, cb, dt_bias, a_log, d_lanes, gate_w, name, ride=()):
    l = proj.shape[0]
    nc = l // SSM_CHUNK
    assert GPS == SSM_GROUPS
    n_ride = len(ride)
    halo_blocks = SSM_CHUNK // HALO
    x_block = 1

    def body(*refs):
        u0_ref, u1_ref, h0_ref, h1_ref, cw_ref, cb_ref, dtraw_ref, bias_ref, alog_ref, d_ref, z_ref, gw_ref = refs[:12]
        ride_in = refs[12:12 + n_ride]
        (y_ref, act_ref, hin_ref, pre_ref, xbc_ref, dtb_out, acsb_out, dtr_out, acsr_out) = refs[12 + n_ride:21 + n_ride]
        ride_out = refs[21 + n_ride:21 + 2 * n_ride]
        h_ref, ext_ref, conv_ref, dtb_ref, acsb_ref, acsr_ref = refs[21 + 2 * n_ride:27 + 2 * n_ride]
        ride_sems = refs[27 + 2 * n_ride:]
        s = pl.program_id(0)
        if n_ride:
            @pl.when(s == 0)
            def _():
                _gather_between_chips(ride_in, ride_out, *ride_sems, wait=False)

            @pl.when(s == nc)
            def _():
                _gather_between_chips(ride_in, ride_out, *ride_sems, wait=True)

        @pl.when(s <= 1)
        def _():
            h_ref[...] = jnp.zeros_like(h_ref)

        @pl.when(s == 0)
        def _():
            conv_ref[1] = jnp.zeros((SSM_CHUNK, SSM_CONV_DIM), BF16)
            dtb_ref[1] = jnp.zeros((SSM_CHUNK, SSM_D_INNER), F32)
            acsb_ref[1] = jnp.zeros((SSM_CHUNK, SSM_D_INNER), F32)
            acsr_ref[1] = jnp.zeros((SSM_GROUPS, SSM_HPG, SSM_CHUNK), F32)

        conv_slot = s & 1
        scan_slot = (s - 1) & 1
        for half, (u_ref, hl_ref) in enumerate(((u0_ref, h0_ref), (u1_ref, h1_ref))):
            hc = slice(half * SSM_D_INNER, (half + 1) * SSM_D_INNER)
            ext_ref[0:HALO, hc] = jnp.where(s > 0, hl_ref[...], 0.0)
            ext_ref[HALO:HALO + SSM_CHUNK, hc] = u_ref[...]

        def conv_columns(c_lo, c_hi):
            for r0 in range(0, SSM_CHUNK, CONV_SUB_ROWS):
                for c0 in range(c_lo, c_hi, CONV_SUB_COLS):
                    cs = slice(c0, c0 + CONV_SUB_COLS)
                    ext = ext_ref[r0:r0 + CONV_SUB_ROWS + HALO, cs]
                    acc = cb_ref[:, cs] + cw_ref[SSM_CONV - 1:SSM_CONV, cs] * ext[HALO:]
                    for k in range(SSM_CONV - 1):
                        acc = acc + cw_ref[k:k + 1, cs] * pltpu.roll(ext, SSM_CONV - 1 - k, 0)[HALO:]
                    act = (acc * _sigmoid(acc)).astype(BF16)
                    pre_ref[r0:r0 + CONV_SUB_ROWS, cs] = acc.astype(pre_ref.dtype)
                    xbc_ref[r0:r0 + CONV_SUB_ROWS, cs] = act
                    conv_ref[conv_slot, r0:r0 + CONV_SUB_ROWS, cs] = act

        dtb, acsb, dt_rows, acs_rows = _dt_and_decay(dtraw_ref[...] + bias_ref[...], alog_ref[...])
        dtb_out[...] = dtb
        acsb_out[...] = acsb
        dtb_ref[conv_slot] = dtb
        acsb_ref[conv_slot] = acsb
        for g in range(SSM_GROUPS):
            heads = slice(g * SSM_HPG, (g + 1) * SSM_HPG)
            dtr_out[g] = dt_rows[heads, :]
            acsr_out[g] = acs_rows[heads, :]
            acsr_ref[conv_slot, g] = acs_rows[heads, :]

        ri, cj = _chunk_iotas()
        conv_share = SSM_CONV_DIM // GPS
        for k in range(GPS):
            conv_columns(k * conv_share, (k + 1) * conv_share)
            g = k
            cols = slice(k * GP, (k + 1) * GP)
            bcols = slice(SSM_D_INNER + k * SSM_STATE, SSM_D_INNER + (k + 1) * SSM_STATE)
            ccols = slice(SSM_D_INNER + SSM_BC_DIM + k * SSM_STATE, SSM_D_INNER + SSM_BC_DIM + (k + 1) * SSM_STATE)
            xv = conv_ref[scan_slot, :, cols].astype(F32)
            bb = conv_ref[scan_slot, :, bcols]
            cb16 = conv_ref[scan_slot, :, ccols]
            acs_v = acsb_ref[scan_slot, :, cols]
            acs_r_v = acsr_ref[scan_slot, k]
            lastb = acs_v[SSM_CHUNK - 1:SSM_CHUNK, :]
            xd = xv * dtb_ref[scan_slot, :, cols]
            cbm = lax.dot_general(cb16, bb, NT_DIMS, preferred_element_type=F32)
            hin = h_ref[g]
            hin_ref[0, k] = hin
            yoff = jnp.dot(cb16, hin.astype(BF16), preferred_element_type=F32)
            ms = [(cbm * _head_decay(acs_v, acs_r_v, r, ri, cj)).astype(BF16) for r in range(SSM_HPG)]
            ydiag = jnp.dot(jnp.concatenate(ms, axis=1), _head_masked_rows(xd, BF16), preferred_element_type=F32)
            y_ref[:, cols] = ydiag + jnp.exp(acs_v) * yoff + d_ref[k] * xv
            h_ref[g] = hin * jnp.exp(lastb) + _bdot_tn(bb, xd * jnp.exp(lastb - acs_v))
        z = z_ref[...]
        yg = y_ref[...] * (z * _sigmoid(z))
        r = lax.rsqrt(jnp.mean(yg * yg, axis=-1, keepdims=True) + EPS)
        act_ref[...] = (yg * r * gw_ref[...]).astype(act_ref.dtype)

    def conv_at(s):
        return jnp.minimum(s, nc - 1)

    def scan_at(s):
        return jnp.maximum(s - 1, 0)

    lanes = pl.BlockSpec((SSM_CHUNK, SSM_D_INNER), lambda s: (scan_at(s), 0))
    conv_out = pl.BlockSpec((SSM_CHUNK, SSM_CONV_DIM), lambda s: (conv_at(s), 0))
    lanes_ahead = pl.BlockSpec((SSM_CHUNK, SSM_D_INNER), lambda s: (conv_at(s), 0))
    rows_ahead = pl.BlockSpec((SSM_GROUPS, SSM_HPG, SSM_CHUNK), lambda s: (0, 0, conv_at(s)))
    return pl.pallas_call(
        body, grid=(nc + 1,),
        in_specs=[pl.BlockSpec((SSM_CHUNK, SSM_D_INNER), lambda s: (conv_at(s), x_block)),
                  pl.BlockSpec((SSM_CHUNK, SSM_D_INNER), lambda s: (conv_at(s), x_block + 1)),
                  pl.BlockSpec((HALO, SSM_D_INNER), lambda s: (jnp.maximum(conv_at(s) * halo_blocks - 1, 0), x_block)),
                  pl.BlockSpec((HALO, SSM_D_INNER), lambda s: (jnp.maximum(conv_at(s) * halo_blocks - 1, 0), x_block + 1)),
                  pl.BlockSpec((SSM_CONV, SSM_CONV_DIM), lambda s: (0, 0)),
                  pl.BlockSpec((1, SSM_CONV_DIM), lambda s: (0, 0)),
                  pl.BlockSpec((SSM_CHUNK, SSM_DT_PAD), lambda s: (conv_at(s), DT_COL_BLOCK)),
                  pl.BlockSpec((1, SSM_DT_PAD), lambda s: (0, 0)),
                  pl.BlockSpec((1, SSM_DT_PAD), lambda s: (0, 0)),
                  pl.BlockSpec((SSM_GROUPS, 1, GP), lambda s: (0, 0, 0)),
                  lanes, pl.BlockSpec((1, SSM_D_INNER), lambda s: (0, 0))] + [ANY] * n_ride,
        out_specs=[lanes, lanes, pl.BlockSpec((1, SSM_GROUPS, SSM_STATE, GP), lambda s: (scan_at(s), 0, 0, 0)),
                   conv_out, conv_out, lanes_ahead, lanes_ahead, rows_ahead, rows_ahead] + [ANY] * n_ride,
        out_shape=[jax.ShapeDtypeStruct((l, SSM_D_INNER), F32), jax.ShapeDtypeStruct((l, SSM_D_INNER), BF16),
                   jax.ShapeDtypeStruct((nc, SSM_GROUPS, SSM_STATE, GP), F32),
                   jax.ShapeDtypeStruct((l, SSM_CONV_DIM), BF16), jax.ShapeDtypeStruct((l, SSM_CONV_DIM), BF16),
                   jax.ShapeDtypeStruct((l, SSM_D_INNER), F32), jax.ShapeDtypeStruct((l, SSM_D_INNER), F32),
                   jax.ShapeDtypeStruct((SSM_GROUPS, SSM_HPG, l), F32),
                   jax.ShapeDtypeStruct((SSM_GROUPS, SSM_HPG, l), F32)]
        + [jax.ShapeDtypeStruct((N_CHIPS,) + a.shape, a.dtype) for a in ride],
        scratch_shapes=[pltpu.VMEM((SSM_GROUPS, SSM_STATE, GP), F32),
                        pltpu.VMEM((SSM_CHUNK + HALO, SSM_CONV_DIM), F32),
                        pltpu.VMEM((2, SSM_CHUNK, SSM_CONV_DIM), BF16),
                        pltpu.VMEM((2, SSM_CHUNK, SSM_D_INNER), F32), pltpu.VMEM((2, SSM_CHUNK, SSM_D_INNER), F32),
                        pltpu.VMEM((2, SSM_GROUPS, SSM_HPG, SSM_CHUNK), F32)] + (_gather_sems(n_ride) if n_ride else []),
        compiler_params=_params("arbitrary"), name=name,
    )(proj, proj, proj, proj, cw, cb.reshape(1, SSM_CONV_DIM), proj, dt_bias, a_log, d_lanes, proj,
      gate_w.reshape(1, SSM_D_INNER), *ride)


def _ssd_bwd(xbc, pre, cw, dtb, acsb, dtr, acs_r, a_log, d_lanes, hin, dact, y, proj, gate_w, name):
    l = xbc.shape[0]
    nc = l // SSM_CHUNK
    sub = CONV_SUB_ROWS

    def body(x_ref, b_ref, c_ref, dtb_ref, acsb_ref, dtr_ref, acsr_ref, alc_ref, d_ref, hin_ref,
             dact_ref, y_ref, z_ref, gw_ref, pre_ref, u0_ref, u1_ref, cw_ref,
             dproj_ref, ddt_ref, dal_ref, dd_ref, dgw_ref, dcw_ref, dcb_ref,
             dh_ref, dy_ref, acc_ref, dxbc_s, dz_s, ddt_s, carry_ref, ext_ref, dcw_acc, dcb_acc):
        step = pl.program_id(0)
        live = (step < nc).astype(F32)
        stage = step & 1
        staged = (step - 1) & 1

        @pl.when(step == 0)
        def _():
            for ref in (dal_ref, dd_ref, acc_ref, dh_ref, carry_ref, dcw_acc, dcb_acc):
                ref[...] = jnp.zeros_like(ref)
            dxbc_s[1] = jnp.zeros((SSM_CHUNK, SSM_CONV_DIM), F32)
            dz_s[1] = jnp.zeros((SSM_CHUNK, SSM_D_INNER), BF16)
            ddt_s[1] = jnp.zeros((SSM_GROUPS, SSM_HPG, SSM_CHUNK), F32)

        z = z_ref[...]
        yv = y_ref[...]
        s = _sigmoid(z)
        sz = z * s
        yg = yv * sz
        r = lax.rsqrt(jnp.mean(yg * yg, axis=-1, keepdims=True) + EPS)
        nrm = yg * r
        gv = dact_ref[...]
        gw = gv * gw_ref[...]
        dyg = r * (gw - nrm * jnp.mean(gw * nrm, axis=-1, keepdims=True))
        dy_ref[...] = dyg * sz
        dz_s[stage] = (dyg * yv * (s * (1.0 + z * (1.0 - s)))).astype(BF16)
        acc_ref[...] += live * jnp.sum((gv * nrm).reshape(SSM_CHUNK // 8, 8, SSM_D_INNER), axis=0)

        p = pre_ref[...].astype(F32)
        sp = _sigmoid(p)
        dp = dxbc_s[staged] * (sp * (1.0 + p * (1.0 - sp)))
        ext_ref[0:SSM_CHUNK, :] = dp
        ext_ref[SSM_CHUNK:SSM_CHUNK + HALO, :] = carry_ref[...]
        carry_ref[...] = dp[0:HALO]
        dproj_ref[:, 0:SSM_D_INNER] = dz_s[staged]
        ddt_ref[...] = ddt_s[staged]

        def fold(v):
            return jnp.sum(v.reshape(sub // 8, 8, CONV_SUB_COLS), axis=0)

        def conv_columns(c_lo, c_hi):
            for c0 in range(c_lo, c_hi, CONV_SUB_COLS):
                cs = slice(c0, c0 + CONV_SUB_COLS)
                u_ref, ucs = (u0_ref, cs) if c0 < SSM_D_INNER else (u1_ref, slice(c0 - SSM_D_INNER, c0 - SSM_D_INNER + CONV_SUB_COLS))
                for r0 in range(0, SSM_CHUNK, sub):
                    dext = ext_ref[r0:r0 + sub + HALO, cs]
                    uv = u_ref[r0:r0 + sub, ucs]
                    for k in range(SSM_CONV):
                        j = SSM_CONV - 1 - k
                        ahead = dext[:sub] if j == 0 else pltpu.roll(dext, sub + HALO - j, 0)[:sub]
                        term = cw_ref[k:k + 1, cs] * ahead
                        du = term if k == 0 else du + term
                        dcw_acc[k, :, cs] += fold(ahead * uv)
                    dcb_acc[:, cs] += fold(dext[:sub])
                    dproj_ref[r0:r0 + sub, SSM_D_INNER + c0:SSM_D_INNER + c0 + CONV_SUB_COLS] = du.astype(dproj_ref.dtype)

        conv_share = SSM_CONV_DIM // GPS
        for k in range(GPS):
            conv_columns(k * conv_share, (k + 1) * conv_share)
            one_group(live, stage, k, k, x_ref, b_ref, c_ref, dtb_ref, acsb_ref, dtr_ref, acsr_ref, alc_ref, d_ref,
                      hin_ref, dy_ref, dxbc_s, ddt_s, dal_ref, dd_ref, dh_ref)

        @pl.when(step == nc)
        def _():
            dgw_ref[...] = jnp.sum(acc_ref[...], axis=0, keepdims=True)
            dcw_ref[...] = jnp.sum(dcw_acc[...], axis=1)
            dcb_ref[...] = jnp.sum(dcb_acc[...], axis=0, keepdims=True)

    def one_group(live, stage, g, k, x_ref, b_ref, c_ref, dtb_ref, acsb_ref, dtr_ref, acsr_ref, alc_ref, d_ref, hin_ref,
                  dy_ref, dxbc_s, ddt_s, dal_ref, dd_ref, dh_ref):
        cols = slice(k * GP, (k + 1) * GP)
        ncols = slice(k * SSM_STATE, (k + 1) * SSM_STATE)
        bcols = slice(SSM_D_INNER + k * SSM_STATE, SSM_D_INNER + (k + 1) * SSM_STATE)
        ccols = slice(SSM_D_INNER + SSM_BC_DIM + k * SSM_STATE, SSM_D_INNER + SSM_BC_DIM + (k + 1) * SSM_STATE)

        xv = x_ref[:, cols].astype(F32)
        dyv = dy_ref[:, cols]
        bb = b_ref[:, ncols].astype(BF16)
        cb16 = c_ref[:, ncols].astype(BF16)
        dtb = dtb_ref[:, cols]
        acsb = acsb_ref[:, cols]
        dtr_v = dtr_ref[k]
        acs_r = acsr_ref[k]
        a_col = -jnp.exp(alc_ref[k])
        ri, cj = _chunk_iotas()
        head_of_lane = lax.shift_right_logical(lax.broadcasted_iota(jnp.int32, (SSM_HPG, GP), 1), HEAD_DIM_LOG2)
        ind_t = (head_of_lane == lax.broadcasted_iota(jnp.int32, (SSM_HPG, GP), 0)).astype(BF16)
        lastb = acsb[SSM_CHUNK - 1:SSM_CHUNK, :]
        ecb = jnp.exp(acsb)
        dteb = jnp.exp(lastb - acsb)
        xd = xv * dtb
        xw = xd * dteb
        cb = lax.dot_general(cb16, bb, NT_DIMS, preferred_element_type=F32)
        hin_v = hin_ref[0, k]
        dhn = dh_ref[g]
        h16 = hin_v.astype(BF16)
        dh16 = dhn.astype(BF16)
        ch = jnp.dot(cb16, h16, preferred_element_type=F32)
        bdh = jnp.dot(bb, dh16, preferred_element_type=F32)
        dym = _head_masked_rows(dyv, BF16)
        g_all = lax.dot_general(dym, xd.astype(BF16), NT_DIMS, preferred_element_type=F32)
        gl_sum = jnp.zeros((SSM_CHUNK, SSM_CHUNK), F32)
        ms, qs = [], []
        for r in range(SSM_HPG):
            decay = _head_decay(acsb, acs_r, r, ri, cj)
            gl = g_all[r * SSM_CHUNK:(r + 1) * SSM_CHUNK] * decay
            gl_sum = gl_sum + gl
            ms.append((cb * decay).astype(BF16))
            qs.append((gl * cb).astype(BF16))
        dxd = lax.dot_general(jnp.concatenate(ms, axis=0), dym, TN_DIMS, preferred_element_type=F32) + dteb * bdh
        cum = jnp.dot(jnp.concatenate(qs, axis=0), (ri < cj).astype(BF16), preferred_element_type=F32)
        sub4 = lax.broadcasted_iota(jnp.int32, (SSM_HPG, 1), 0)
        da = jnp.zeros((SSM_HPG, SSM_CHUNK), F32)
        for r in range(SSM_HPG):
            rect = jnp.sum(jnp.where(ri >= cj, cum[r * SSM_CHUNK:(r + 1) * SSM_CHUNK], 0.0), axis=0, keepdims=True)
            da = da + jnp.where(sub4 == r, rect, 0.0)
        z2 = xw * bdh
        sub8 = lax.broadcasted_iota(jnp.int32, (8, 1), 0)
        col_sums = (jnp.where(sub8 == 0, jnp.sum(z2, axis=0, keepdims=True), 0.0)
                    + jnp.where(sub8 == 1, jnp.sum(dhn * hin_v, axis=0, keepdims=True), 0.0)
                    + jnp.where(sub8 == 2, jnp.sum(dyv * xv, axis=0, keepdims=True), 0.0))
        wv = ecb * dyv
        summands = jnp.concatenate([wv * ch - z2, dxd * xv, col_sums], axis=0)
        sums = lax.dot_general(ind_t, summands.astype(BF16), NT_DIMS, preferred_element_type=F32)
        per_pos = sums[:, :2 * SSM_CHUNK]
        totals = sums[:, 2 * SSM_CHUNK:]
        e_last = totals[:, 0:1] + jnp.exp(acs_r[:, SSM_CHUNK - 1:SSM_CHUNK]) * totals[:, 1:2]
        da = (da + e_last + jnp.dot(per_pos[:, :SSM_CHUNK], (ri >= cj).astype(F32), preferred_element_type=F32,
                                    precision=lax.Precision.HIGHEST))
        ddt_s[stage, k] = a_col * da + per_pos[:, SSM_CHUNK:]
        dal_ref[g] += live * (a_col * jnp.sum(da * dtr_v, axis=1, keepdims=True))
        dd_ref[g] += live * totals[:, 2:3]
        dxbc_s[stage, :, cols] = dxd * dtb + d_ref[k] * dyv
        w16 = wv.astype(BF16)
        xw16 = xw.astype(BF16)
        gl16 = gl_sum.astype(BF16)
        dxbc_s[stage, :, ccols] = (jnp.dot(gl16, bb, preferred_element_type=F32)
                                   + lax.dot_general(w16, h16, NT_DIMS, preferred_element_type=F32))
        dxbc_s[stage, :, bcols] = (lax.dot_general(gl16, cb16, TN_DIMS, preferred_element_type=F32)
                                   + lax.dot_general(xw16, dh16, NT_DIMS, preferred_element_type=F32))
        dh_ref[g] = dhn * jnp.exp(lastb) + lax.dot_general(cb16, w16, TN_DIMS, preferred_element_type=F32)

    def scan_at(s):
        return nc - 1 - jnp.minimum(s, nc - 1)

    def conv_at(s):
        return nc - 1 - jnp.maximum(s - 1, 0)

    small = pl.BlockSpec((SSM_GROUPS, SSM_HPG, 1), lambda s: (0, 0, 0))
    lanes = pl.BlockSpec((SSM_CHUNK, SSM_D_INNER), lambda s: (scan_at(s), 0))
    rows = pl.BlockSpec((SSM_GROUPS, SSM_HPG, SSM_CHUNK), lambda s: (0, 0, scan_at(s)))
    vec = pl.BlockSpec((1, SSM_D_INNER), lambda s: (0, 0))
    return pl.pallas_call(
        body, grid=(nc + 1,),
        in_specs=[lanes,
                  pl.BlockSpec((SSM_CHUNK, SSM_BC_DIM), lambda s: (scan_at(s), B_BLOCK0 // GPS)),
                  pl.BlockSpec((SSM_CHUNK, SSM_BC_DIM), lambda s: (scan_at(s), C_BLOCK0 // GPS)),
                  lanes, lanes, rows, rows,
                  pl.BlockSpec((SSM_GROUPS, SSM_HPG, 1), lambda s: (0, 0, 0)),
                  pl.BlockSpec((SSM_GROUPS, 1, GP), lambda s: (0, 0, 0)),
                  pl.BlockSpec((1, SSM_GROUPS, SSM_STATE, GP), lambda s: (scan_at(s), 0, 0, 0)),
                  lanes, lanes, lanes, vec,
                  pl.BlockSpec((SSM_CHUNK, SSM_CONV_DIM), lambda s: (conv_at(s), 0)),
                  pl.BlockSpec((SSM_CHUNK, SSM_D_INNER), lambda s: (conv_at(s), 1)),
                  pl.BlockSpec((SSM_CHUNK, SSM_D_INNER), lambda s: (conv_at(s), 2)),
                  pl.BlockSpec((SSM_CONV, SSM_CONV_DIM), lambda s: (0, 0))],
        out_specs=[pl.BlockSpec((SSM_CHUNK, SSM_D_INNER + SSM_CONV_DIM), lambda s: (conv_at(s), 0)),
                   pl.BlockSpec((SSM_GROUPS, SSM_HPG, SSM_CHUNK), lambda s: (0, 0, conv_at(s))),
                   small, small, vec,
                   pl.BlockSpec((SSM_CONV, SSM_CONV_DIM), lambda s: (0, 0)),
                   pl.BlockSpec((1, SSM_CONV_DIM), lambda s: (0, 0))],
        out_shape=[jax.ShapeDtypeStruct((l, SSM_IN_PAD), BF16), jax.ShapeDtypeStruct((SSM_GROUPS, SSM_HPG, l), F32),
                   jax.ShapeDtypeStruct((SSM_GROUPS, SSM_HPG, 1), F32),
                   jax.ShapeDtypeStruct((SSM_GROUPS, SSM_HPG, 1), F32),
                   jax.ShapeDtypeStruct((1, SSM_D_INNER), F32),
                   jax.ShapeDtypeStruct((SSM_CONV, SSM_CONV_DIM), F32), jax.ShapeDtypeStruct((1, SSM_CONV_DIM), F32)],
        scratch_shapes=[pltpu.VMEM((SSM_GROUPS, SSM_STATE, GP), F32), pltpu.VMEM((SSM_CHUNK, SSM_D_INNER), F32),
                        pltpu.VMEM((8, SSM_D_INNER), F32),
                        pltpu.VMEM((2, SSM_CHUNK, SSM_CONV_DIM), F32), pltpu.VMEM((2, SSM_CHUNK, SSM_D_INNER), BF16),
                        pltpu.VMEM((2, SSM_GROUPS, SSM_HPG, SSM_CHUNK), F32), pltpu.VMEM((HALO, SSM_CONV_DIM), F32),
                        pltpu.VMEM((SSM_CHUNK + HALO, SSM_CONV_DIM), F32),
                        pltpu.VMEM((SSM_CONV, 8, SSM_CONV_DIM), F32), pltpu.VMEM((8, SSM_CONV_DIM), F32)],
        compiler_params=_params("arbitrary"), name=name,
    )(xbc, xbc, xbc, dtb, acsb, dtr, acs_r, a_log.reshape(SSM_GROUPS, SSM_HPG, 1), d_lanes, hin, dact, y, proj,
      gate_w.reshape(1, SSM_D_INNER), pre, proj, proj, cw)


LANES = 128
ROPE_Q_CHUNKS = ATT_WIDTH // LANES
ROPE_K_CHUNKS = ATT_KV_WIDTH // LANES


def _rope_tables(positions):
    inv = ROPE_THETA ** (-jnp.arange(0, ROPE_DIM, 2, dtype=F32) / ROPE_DIM)
    ang = positions.astype(F32)[:, None] * inv
    cos, sin = jnp.cos(ang), jnp.sin(ang)
    l = positions.shape[0]
    rest = ATT_HEAD_DIM - ROPE_DIM
    ones, zeros = jnp.ones((l, rest), F32), jnp.zeros((l, rest), F32)
    z8 = jnp.zeros((l, ROPE_HALF), F32)
    cos_f = jnp.concatenate([cos, cos, ones], axis=1)
    sin_a = jnp.concatenate([-sin, z8, zeros], axis=1)
    sin_b = jnp.concatenate([z8, sin, zeros], axis=1)
    reps = LANES // ATT_HEAD_DIM
    return tuple(jnp.tile(t, (1, reps)) for t in (cos_f, sin_a, sin_b))


ATT_QKV4 = 3 * ATT_WIDTH


def _both_halves(chunk):
    lane = lax.broadcasted_iota(jnp.int32, (1, LANES), 1)
    swapped = pltpu.roll(chunk, ATT_HEAD_DIM, 1)
    return jnp.where(lane < ATT_HEAD_DIM, chunk, swapped), jnp.where(lane < ATT_HEAD_DIM, swapped, chunk)


def _rope_fwd(proj, tables, name):
    l = proj.shape[0]
    tl = _pick(l, (256, 128))

    def body(p_ref, c_ref, sa_ref, sb_ref, o_ref):
        cos_f, sin_a, sin_b = c_ref[...], sa_ref[...], sb_ref[...]

        def rope(t):
            return t * cos_f + pltpu.roll(t, LANES - ROPE_HALF, 1) * sin_a + pltpu.roll(t, ROPE_HALF, 1) * sin_b

        for k in range(ROPE_Q_CHUNKS):
            sl = slice(k * LANES, (k + 1) * LANES)
            o_ref[:, sl] = (rope(p_ref[:, sl]) * Q_SCALE).astype(o_ref.dtype)
        for part in range(2):
            for k in range(ROPE_K_CHUNKS):
                src = ATT_WIDTH + part * ATT_KV_WIDTH + k * LANES
                t = p_ref[:, src:src + LANES]
                if part == 0:
                    t = rope(t)
                for head, dup in enumerate(_both_halves(t.astype(o_ref.dtype))):
                    dst = (1 + part) * ATT_WIDTH + (2 * k + head) * ATT_GQA * ATT_HEAD_DIM
                    o_ref[:, dst:dst + LANES] = dup
                    o_ref[:, dst + LANES:dst + 2 * LANES] = dup

    tab = pl.BlockSpec((tl, LANES), lambda i: (i, 0))
    return pl.pallas_call(
        body, grid=(l // tl,), in_specs=[pl.BlockSpec((tl, ATT_IN_DIM), lambda i: (i, 0)), tab, tab, tab],
        out_specs=pl.BlockSpec((tl, ATT_QKV4), lambda i: (i, 0)),
        out_shape=jax.ShapeDtypeStruct((l, ATT_QKV4), BF16), compiler_params=_params("parallel"), name=name,
    )(proj, *tables)


def _rope_bwd(dq, dk4, dv4, dgate, tables, name):
    l = dq.shape[0]
    tl = _pick(l, (256, 128))

    def body(dq_ref, dk_ref, dv_ref, dg_ref, c_ref, sa_ref, sb_ref, o_ref):
        cos_f, sin_a, sin_b = c_ref[...], sa_ref[...], sb_ref[...]
        lane = lax.broadcasted_iota(jnp.int32, (1, LANES), 1)

        def unrope(t):
            return t * cos_f + pltpu.roll(t * sin_a, ROPE_HALF, 1) + pltpu.roll(t * sin_b, LANES - ROPE_HALF, 1)

        def head_total(ref, kvh):
            base = kvh * ATT_GQA * ATT_HEAD_DIM
            s = ref[:, base:base + LANES] + ref[:, base + LANES:base + 2 * LANES]
            return s + pltpu.roll(s, ATT_HEAD_DIM, 1)

        for k in range(ROPE_Q_CHUNKS):
            sl = slice(k * LANES, (k + 1) * LANES)
            o_ref[:, sl] = unrope(dq_ref[:, sl] * Q_SCALE).astype(o_ref.dtype)
        for k in range(ROPE_K_CHUNKS):
            dk = jnp.where(lane < ATT_HEAD_DIM, head_total(dk_ref, 2 * k), head_total(dk_ref, 2 * k + 1))
            dv = jnp.where(lane < ATT_HEAD_DIM, head_total(dv_ref, 2 * k), head_total(dv_ref, 2 * k + 1))
            o_ref[:, ATT_WIDTH + k * LANES:ATT_WIDTH + (k + 1) * LANES] = unrope(dk).astype(o_ref.dtype)
            at = ATT_WIDTH + ATT_KV_WIDTH + k * LANES
            o_ref[:, at:at + LANES] = dv.astype(o_ref.dtype)
        o_ref[:, ATT_QKV:ATT_IN_DIM] = dg_ref[...].astype(o_ref.dtype)

    tab = pl.BlockSpec((tl, LANES), lambda i: (i, 0))
    wide = pl.BlockSpec((tl, ATT_WIDTH), lambda i: (i, 0))
    return pl.pallas_call(
        body, grid=(l // tl,), in_specs=[wide, wide, wide, wide, tab, tab, tab],
        out_specs=pl.BlockSpec((tl, ATT_IN_DIM), lambda i: (i, 0)),
        out_shape=jax.ShapeDtypeStruct((l, ATT_IN_DIM), BF16), compiler_params=_params("parallel"), name=name,
    )(dq, dk4, dv4, dgate, *tables)


GATE_HALF = ATT_WIDTH // 2
GATE_COL_BLOCK = ATT_QKV // GATE_HALF


ATT_STACK = ATT_GQA * ATT_BLOCK
BLOCK_LOG2 = ATT_BLOCK.bit_length() - 1


def _stack_masks(n):
    ri = lax.broadcasted_iota(jnp.int32, (ATT_STACK, ATT_BLOCK), 0) & (ATT_BLOCK - 1)
    cj = lax.broadcasted_iota(jnp.int32, (ATT_STACK, ATT_BLOCK), 1)
    return (cj > ri) & (n > 0), cj <= ri


def _stack_sinks(sink_ref, kvh):
    blk = lax.shift_right_logical(lax.broadcasted_iota(jnp.int32, (ATT_STACK, 1), 0), BLOCK_LOG2)
    col = jnp.zeros((ATT_STACK, 1), F32)
    for r in range(ATT_GQA):
        col = jnp.where(blk == r, sink_ref[kvh * ATT_GQA + r], col)
    return col


def _stack_fold(stack):
    head_of_lane = lax.shift_right_logical(lax.broadcasted_iota(jnp.int32, (1, GP), 1), HEAD_DIM_LOG2)
    out = jnp.zeros((ATT_BLOCK, GP), F32)
    for r in range(ATT_GQA):
        out = jnp.where(head_of_lane == r, stack[r * ATT_BLOCK:(r + 1) * ATT_BLOCK], out)
    return out


def _attn_fwd(qkv, proj, sinks, name):
    l = qkv.shape[0]
    nb = l // ATT_BLOCK

    def body(sink_ref, q_ref, kp_ref, kc_ref, vp_ref, vc_ref, g0_ref, g1_ref, og_ref, o_ref, lse_ref):
        n = pl.program_id(0)
        mask_p, mask_c = _stack_masks(n)
        ones = jnp.ones((ATT_BLOCK, LANES), BF16)
        for kvh in range(ATT_KV_HEADS):
            cols = slice(kvh * GP, (kvh + 1) * GP)
            q_stack = _head_masked_rows(q_ref[:, cols], BF16)
            sp = jnp.where(mask_p, lax.dot_general(q_stack, kp_ref[:, cols], NT_DIMS, preferred_element_type=F32), NEG_INF)
            sc = jnp.where(mask_c, lax.dot_general(q_stack, kc_ref[:, cols], NT_DIMS, preferred_element_type=F32), NEG_INF)
            sink = _stack_sinks(sink_ref, kvh)
            m = jnp.maximum(jnp.max(jnp.maximum(sp, sc), axis=1, keepdims=True), sink)
            pp = jnp.exp(sp - m).astype(BF16)
            pc = jnp.exp(sc - m).astype(BF16)
            acc = (jnp.dot(pp, jnp.concatenate([vp_ref[:, cols], ones], axis=1), preferred_element_type=F32)
                   + jnp.dot(pc, jnp.concatenate([vc_ref[:, cols], ones], axis=1), preferred_element_type=F32))
            den = acc[:, GP:] + jnp.exp(sink - m)
            inv = 1.0 / den
            o_ref[:, cols] = _stack_fold(acc[:, :GP] * jnp.concatenate([inv, inv], axis=1))
            lse = m + jnp.log(den)
            lse_ref[:, cols] = _stack_fold(jnp.concatenate([lse, lse], axis=1))
        for half, g_ref in enumerate((g0_ref, g1_ref)):
            sl = slice(half * GATE_HALF, (half + 1) * GATE_HALF)
            gate = g_ref[...]
            og_ref[:, sl] = (o_ref[:, sl] * (gate * _sigmoid(gate))).astype(og_ref.dtype)

    def prev(n):
        return jnp.maximum(n - 1, 0)

    wide = pl.BlockSpec((ATT_BLOCK, ATT_WIDTH), lambda n: (n, 0))
    return pl.pallas_call(
        body, grid=(nb,),
        in_specs=[pl.BlockSpec(memory_space=pltpu.SMEM), wide,
                  pl.BlockSpec((ATT_BLOCK, ATT_WIDTH), lambda n: (prev(n), 1)),
                  pl.BlockSpec((ATT_BLOCK, ATT_WIDTH), lambda n: (n, 1)),
                  pl.BlockSpec((ATT_BLOCK, ATT_WIDTH), lambda n: (prev(n), 2)),
                  pl.BlockSpec((ATT_BLOCK, ATT_WIDTH), lambda n: (n, 2)),
                  pl.BlockSpec((ATT_BLOCK, GATE_HALF), lambda n: (n, GATE_COL_BLOCK)),
                  pl.BlockSpec((ATT_BLOCK, GATE_HALF), lambda n: (n, GATE_COL_BLOCK + 1))],
        out_specs=[wide, wide, wide],
        out_shape=[jax.ShapeDtypeStruct((l, ATT_WIDTH), BF16), jax.ShapeDtypeStruct((l, ATT_WIDTH), F32),
                   jax.ShapeDtypeStruct((l, ATT_WIDTH), F32)],
        compiler_params=_params("parallel"), name=name,
    )(sinks, qkv, qkv, qkv, qkv, qkv, proj, proj)


def _attn_bwd(qkv, proj, sinks, o, lse, dog, name, ride=()):
    l = qkv.shape[0]
    nb = l // ATT_BLOCK
    n_ride = len(ride)

    def body(*refs):
        sink_ref, q_ref, kp_ref, kc_ref, vp_ref, vc_ref, g0_ref, g1_ref, o_ref, lse_ref, dog_ref = refs[:11]
        ride_in = refs[11:11 + n_ride]
        dq_ref, dk_ref, dv_ref, dg_ref, ds_ref = refs[11 + n_ride:16 + n_ride]
        ride_out = refs[16 + n_ride:16 + 2 * n_ride]
        ck_ref, cv_ref, do_ref = refs[16 + 2 * n_ride:19 + 2 * n_ride]
        ride_sems = refs[19 + 2 * n_ride:]
        n = pl.program_id(0)

        @pl.when(n == 0)
        def _():
            ds_ref[...] = jnp.zeros_like(ds_ref)
            ck_ref[...] = jnp.zeros_like(ck_ref)
            cv_ref[...] = jnp.zeros_like(cv_ref)
            if n_ride:
                _scatter_between_chips(ride_in, ride_out, *ride_sems, wait=False)

        @pl.when(n == nb)
        def _():
            dk_ref[...] = ck_ref[...]
            dv_ref[...] = cv_ref[...]
            if n_ride:
                _scatter_between_chips(ride_in, ride_out, *ride_sems, wait=True)

        @pl.when(n < nb)
        def _():
            mask_p, mask_c = _stack_masks(n)
            lane = lax.broadcasted_iota(jnp.int32, (1, ATT_Q_HEADS), 1)
            for half, g_ref in enumerate((g0_ref, g1_ref)):
                sl = slice(half * GATE_HALF, (half + 1) * GATE_HALF)
                gate = g_ref[...]
                s = _sigmoid(gate)
                dogv = dog_ref[:, sl]
                do_ref[:, sl] = dogv * (gate * s)
                dg_ref[:, sl] = dogv * o_ref[:, sl] * (s * (1.0 + gate * (1.0 - s)))
            ds_acc = jnp.zeros((1, ATT_Q_HEADS), F32)
            for kvh in range(ATT_KV_HEADS):
                cols = slice(kvh * GP, (kvh + 1) * GP)
                kp, kc, vp, vc = kp_ref[:, cols], kc_ref[:, cols], vp_ref[:, cols], vc_ref[:, cols]
                q_stack = _head_masked_rows(q_ref[:, cols], BF16)
                do_g = do_ref[:, cols]
                do_stack = _head_masked_rows(do_g, BF16)
                lse_g = lse_ref[:, cols]
                lse_stack = jnp.concatenate(
                    [_both_halves(lse_g[:, (r // 2) * LANES:(r // 2 + 1) * LANES])[r % 2] for r in range(ATT_GQA)], axis=0)
                pp = jnp.exp(jnp.where(
                    mask_p, lax.dot_general(q_stack, kp, NT_DIMS, preferred_element_type=F32) - lse_stack, NEG_INF))
                pc = jnp.exp(jnp.where(
                    mask_c, lax.dot_general(q_stack, kc, NT_DIMS, preferred_element_type=F32) - lse_stack, NEG_INF))
                dpp = lax.dot_general(do_stack, vp, NT_DIMS, preferred_element_type=F32)
                dpc = lax.dot_general(do_stack, vc, NT_DIMS, preferred_element_type=F32)
                delta = jnp.sum(pp * dpp + pc * dpc, axis=1, keepdims=True)
                dsp = (pp * (dpp - delta)).astype(BF16)
                dsc = (pc * (dpc - delta)).astype(BF16)
                dq_ref[:, cols] = _stack_fold(jnp.dot(dsp, kp, preferred_element_type=F32)
                                              + jnp.dot(dsc, kc, preferred_element_type=F32))
                dk_ref[:, cols] = ck_ref[:, cols] + lax.dot_general(dsp, q_stack, TN_DIMS, preferred_element_type=F32)
                dv_ref[:, cols] = cv_ref[:, cols] + lax.dot_general(pp.astype(BF16), do_stack, TN_DIMS,
                                                                    preferred_element_type=F32)
                ck_ref[:, cols] = lax.dot_general(dsc, q_stack, TN_DIMS, preferred_element_type=F32)
                cv_ref[:, cols] = lax.dot_general(pc.astype(BF16), do_stack, TN_DIMS, preferred_element_type=F32)
                t = jnp.exp(_stack_sinks(sink_ref, kvh) - lse_stack) * delta
                for r in range(ATT_GQA):
                    tot = jnp.sum(t[r * ATT_BLOCK:(r + 1) * ATT_BLOCK], axis=0, keepdims=True)
                    ds_acc = ds_acc - jnp.where(lane == kvh * ATT_GQA + r, tot[:, :ATT_Q_HEADS], 0.0)
            ds_ref[...] += ds_acc

    def cur(n):
        return jnp.minimum(n, nb - 1)

    def prev(n):
        return jnp.maximum(n - 1, 0)

    wide = pl.BlockSpec((ATT_BLOCK, ATT_WIDTH), lambda n: (cur(n), 0))
    late = pl.BlockSpec((ATT_BLOCK, ATT_WIDTH), lambda n: (prev(n), 0))
    return pl.pallas_call(
        body, grid=(nb + 1,),
        in_specs=[pl.BlockSpec(memory_space=pltpu.SMEM), wide,
                  pl.BlockSpec((ATT_BLOCK, ATT_WIDTH), lambda n: (prev(cur(n)), 1)),
                  pl.BlockSpec((ATT_BLOCK, ATT_WIDTH), lambda n: (cur(n), 1)),
                  pl.BlockSpec((ATT_BLOCK, ATT_WIDTH), lambda n: (prev(cur(n)), 2)),
                  pl.BlockSpec((ATT_BLOCK, ATT_WIDTH), lambda n: (cur(n), 2)),
                  pl.BlockSpec((ATT_BLOCK, GATE_HALF), lambda n: (cur(n), GATE_COL_BLOCK)),
                  pl.BlockSpec((ATT_BLOCK, GATE_HALF), lambda n: (cur(n), GATE_COL_BLOCK + 1)),
                  wide, wide, wide] + [ANY] * n_ride,
        out_specs=[wide, late, late, wide, pl.BlockSpec((1, ATT_Q_HEADS), lambda n: (0, 0))] + [ANY] * n_ride,
        out_shape=[jax.ShapeDtypeStruct((l, ATT_WIDTH), F32), jax.ShapeDtypeStruct((l, ATT_WIDTH), F32),
                   jax.ShapeDtypeStruct((l, ATT_WIDTH), F32), jax.ShapeDtypeStruct((l, ATT_WIDTH), F32),
                   jax.ShapeDtypeStruct((1, ATT_Q_HEADS), F32)] + _scatter_shapes(ride),
        scratch_shapes=[pltpu.VMEM((ATT_BLOCK, ATT_WIDTH), F32), pltpu.VMEM((ATT_BLOCK, ATT_WIDTH), F32),
                        pltpu.VMEM((ATT_BLOCK, ATT_WIDTH), F32)] + (_gather_sems(n_ride) if n_ride else []),
        compiler_params=_params("arbitrary"), name=name,
    )(sinks, qkv, qkv, qkv, qkv, qkv, proj, proj, o, lse, dog, *ride)


def _local_step(x, positions, pre_norm, post_norm, conv_b, dt_bias, a_log, d_skip, gate_norm, sinks, target,
                first_in, in_proj_with_first_pair, scan_with_second_pair, attn_bwd_with_second_pair_grads,
                in_dx_with_first_pair_grads):
    tables = _rope_tables(positions)
    dt_bias_pad = jnp.pad(dt_bias, ((0, 0), (0, SSM_DT_PAD - SSM_HEADS)))
    d_lanes = jnp.repeat(d_skip, SSM_HEAD_DIM, axis=1).reshape(-1, SSM_GROUPS, 1, GP)
    a_log_pad = jnp.pad(a_log, ((0, 0), (0, SSM_DT_PAD - SSM_HEADS)))
    pairs = [first_in, None]
    saved = []
    cur = x
    h = _rmsnorm_fwd(cur, pre_norm[0], "prenorm_fwd_0")
    for i in range(DEPTH):
        j = i // 2
        if i % 2 == 0:
            in_proj = functools.partial(_matmul, h, pairs[j]["ssm_w_in"], "nn", F32, f"ssm_in_{i}")
            if i == 0:
                proj, rest = in_proj_with_first_pair(in_proj)
                pairs[0] = {**first_in, **rest}
            else:
                proj = in_proj()
            scan = functools.partial(_ssd_fwd, proj, pairs[j]["ssm_conv_w"], conv_b[j], dt_bias_pad[j:j + 1],
                                     a_log_pad[j:j + 1], d_lanes[j], gate_norm[j], f"ssd_fwd_{i}")
            if i == 0:
                *scanned, pairs[1] = scan_with_second_pair(scan)
            else:
                scanned = scan()
            y, act, hin, pre, xbc, dtb, acsb, dtr, acs_r = scanned
            w_ssm_in = [p["ssm_w_in"] for p in pairs]
            w_ssm_out = [p["ssm_w_out"] for p in pairs]
            w_att_in = [p["att_w_in"] for p in pairs]
            w_att_out = [p["att_w_out"] for p in pairs]
            conv_w = [p["ssm_conv_w"] for p in pairs]
            ymix = _matmul(act, w_ssm_out[j], "nn", F32, f"ssm_out_{i}")
            saved.append(dict(x=cur, h=h, proj=proj, pre=pre, xbc=xbc, dtb=dtb, acsb=acsb, dtr=dtr, acs_r=acs_r, y=y,
                              hin=hin, act=act, ymix=ymix))
        else:
            proj = _matmul(h, w_att_in[j], "nn", F32, f"att_in_{i}")
            qkv = _rope_fwd(proj, tables, f"rope_fwd_{i}")
            act, o, lse = _attn_fwd(qkv, proj, sinks[j], f"attn_fwd_{i}")
            ymix = _matmul(act, w_att_out[j], "nn", F32, f"att_out_{i}")
            saved.append(dict(x=cur, h=h, proj=proj, qkv=qkv, o=o, lse=lse, act=act, ymix=ymix))
        if i + 1 < DEPTH:
            cur, h = _post_fwd(cur, ymix, post_norm[i], pre_norm[i + 1], f"post_fwd_{i}")

    gr = {k: [None] * 2 for k in ("ssm_w_in", "ssm_conv_w", "ssm_conv_b", "ssm_dt_bias", "ssm_a_log", "ssm_d",
                                  "ssm_gate_norm", "ssm_w_out", "att_w_in", "att_sinks", "att_w_out")}
    gr["pre_norm"] = [None] * DEPTH
    gr["post_norm"] = [None] * DEPTH
    last = DEPTH - 1
    g, dymix, loss_lanes, gr["post_norm"][last] = _post_loss(cur, ymix, post_norm[last], target, "post_loss")
    for i in reversed(range(DEPTH)):
        j = i // 2
        s = saved[i]
        if i % 2 == 0:
            dact = _matmul(dymix, w_ssm_out[j], "nt", F32, f"ssm_out_dx_{i}")
            gr["ssm_w_out"][j] = _matmul(s["act"], dymix, "tn", F32, f"ssm_out_dw_{i}")
            dproj, ddt8, dal, dd, gr["ssm_gate_norm"][j], gr["ssm_conv_w"][j], dcb = _ssd_bwd(
                s["xbc"], s["pre"], conv_w[j], s["dtb"], s["acsb"], s["dtr"], s["acs_r"], a_log[j], d_lanes[j], s["hin"],
                dact, s["y"], s["proj"], gate_norm[j], f"ssd_bwd_{i}")
            gr["ssm_conv_b"][j] = dcb[0]
            gr["ssm_a_log"][j] = dal.reshape(SSM_HEADS)
            gr["ssm_d"][j] = dd.reshape(SSM_HEADS)
            l = x.shape[0]
            ddt = jnp.pad(jnp.transpose(ddt8, (2, 0, 1)).reshape(l, SSM_HEADS), ((0, 0), (0, SSM_DT_PAD - SSM_HEADS)))
            dproj, dbias = _dt_bwd(ddt, s["proj"], dt_bias_pad[j:j + 1], dproj, f"dt_bwd_{i}")
            gr["ssm_dt_bias"][j] = dbias[0, :SSM_HEADS]
            w_in, key = w_ssm_in[j], "ssm_w_in"
        else:
            dog = _matmul(dymix, w_att_out[j], "nt", F32, f"att_out_dx_{i}")
            gr["att_w_out"][j] = _matmul(s["act"], dymix, "tn", F32, f"att_out_dw_{i}")
            attn_bwd = functools.partial(_attn_bwd, s["qkv"], s["proj"], sinks[j], s["o"], s["lse"], dog, f"attn_bwd_{i}")
            if i == 1:
                (dq, dk, dv, dgate, dsk), second_pair_reduced = attn_bwd_with_second_pair_grads(
                    attn_bwd, {k: gr[k][1] for k in BIG})
            else:
                dq, dk, dv, dgate, dsk = attn_bwd()
            gr["att_sinks"][j] = dsk[0]
            dproj = _rope_bwd(dq, dk, dv, dgate, tables, f"rope_bwd_{i}")
            w_in, key = w_att_in[j], "att_w_in"
        gr[key][j] = _matmul(s["h"], dproj, "tn", F32, f"in_dw_{i}")
        in_dx = functools.partial(_matmul, dproj, w_in, "nt", F32, f"in_dx_{i}")
        if i == 0:
            dh, first_pair_reduced = in_dx_with_first_pair_grads(in_dx, {k: gr[k][0] for k in BIG})
        else:
            dh = in_dx()
        if i > 0:
            g, dymix, gr["pre_norm"][i], gr["post_norm"][i - 1] = _norm_bwd_chain(
                dh, s["x"], pre_norm[i], g, saved[i - 1]["ymix"], post_norm[i - 1], f"norm_bwd_{i}")
        else:
            g, gr["pre_norm"][i] = _rmsnorm_bwd(dh, s["x"], pre_norm[i], g, f"prenorm_bwd_{i}")
    grads = {k: jnp.stack([v.reshape(v.shape[-1]) if k in ("pre_norm", "post_norm", "ssm_gate_norm") else v for v in vs])
             for k, vs in gr.items() if k not in BIG}
    return loss_lanes, g, grads, first_pair_reduced, second_pair_reduced


N_CHIPS = 4
N_DEV = 8
MESH = pl.DeviceIdType.MESH
ANY = pl.BlockSpec(memory_space=pl.ANY)


def _place():
    x, y, c = lax.axis_index("x"), lax.axis_index("y"), lax.axis_index("c")
    return x, y, c, 2 * x + y


def _gather_sems(n):
    return [pltpu.SemaphoreType.DMA((n, N_CHIPS)), pltpu.SemaphoreType.DMA((n, N_CHIPS)), pltpu.SemaphoreType.DMA((n,))]


def _gather_between_chips(ins, outs, send_sems, recv_sems, local_sems, wait):
    n = len(ins)
    _, _, c, s = _place()
    local = [pltpu.make_async_copy(ins[w], outs[w].at[s], local_sems.at[w]) for w in range(n)]

    def remote(w, t):
        return pltpu.make_async_remote_copy(
            src_ref=ins[w].at[c], dst_ref=outs[w].at[s, c], send_sem=send_sems.at[w, t],
            recv_sem=recv_sems.at[w, s], device_id=(t // 2, t % 2, c), device_id_type=MESH)

    def arrival(w, t):
        return pltpu.make_async_remote_copy(
            src_ref=ins[w].at[c], dst_ref=outs[w].at[t, c], send_sem=send_sems.at[w, t],
            recv_sem=recv_sems.at[w, t], device_id=(t // 2, t % 2, c), device_id_type=MESH)

    if not wait:
        for cp in local:
            cp.start()
    for t in range(N_CHIPS):
        @pl.when(s != t)
        def _():
            for w in range(n):
                if wait:
                    remote(w, t).wait_send()
                    arrival(w, t).wait_recv()
                else:
                    remote(w, t).start()
    if wait:
        for cp in local:
            cp.wait()


def _pair_handoff(bufs, name):
    n = len(bufs)

    def body(*refs):
        outs = refs[n:2 * n]
        send_sems, recv_sems = refs[2 * n:]
        x, y, c, s = _place()

        def handed_on(w, t):
            return pltpu.make_async_remote_copy(
                src_ref=outs[w].at[t, c], dst_ref=outs[w].at[t, c], send_sem=send_sems.at[w, t],
                recv_sem=recv_sems.at[w, t], device_id=(x, y, 1 - c), device_id_type=MESH)

        def handed_in(w, t):
            return pltpu.make_async_remote_copy(
                src_ref=outs[w].at[t, 1 - c], dst_ref=outs[w].at[t, 1 - c], send_sem=send_sems.at[w, t],
                recv_sem=recv_sems.at[w, t], device_id=(x, y, 1 - c), device_id_type=MESH)

        for t in range(N_CHIPS):
            @pl.when(s != t)
            def _():
                for w in range(n):
                    handed_on(w, t).start()
        for t in range(N_CHIPS):
            @pl.when(s != t)
            def _():
                for w in range(n):
                    handed_on(w, t).wait_send()
                    handed_in(w, t).wait_recv()

    return pl.pallas_call(
        body, in_specs=[ANY] * n, out_specs=[ANY] * n,
        out_shape=[jax.ShapeDtypeStruct(a.shape, a.dtype) for a in bufs],
        scratch_shapes=[pltpu.SemaphoreType.DMA((n, N_CHIPS)), pltpu.SemaphoreType.DMA((n, N_CHIPS))],
        input_output_aliases={w: w for w in range(n)}, name=name,
    )(*bufs)


def _chip_gather(shards, name):
    n = len(shards)

    def body(*refs):
        ins, outs = refs[:n], refs[n:2 * n]
        _gather_between_chips(ins, outs, *refs[2 * n:], wait=False)
        _gather_between_chips(ins, outs, *refs[2 * n:], wait=True)

    bufs = pl.pallas_call(
        body, in_specs=[ANY] * n, out_specs=[ANY] * n,
        out_shape=[jax.ShapeDtypeStruct((N_CHIPS,) + a.shape, a.dtype) for a in shards],
        scratch_shapes=_gather_sems(n), name=name,
    )(*shards)
    return _pair_handoff(bufs, name + "_handoff")


def _pair_swap(parts, name):
    n = len(parts)

    def body(*refs):
        ins, outs = refs[:n], refs[n:2 * n]
        send_sems, recv_sems = refs[2 * n:]
        x, y, c, _ = _place()
        cps = [pltpu.make_async_remote_copy(
            src_ref=ins[w].at[1 - c], dst_ref=outs[w], send_sem=send_sems.at[w], recv_sem=recv_sems.at[w],
            device_id=(x, y, 1 - c), device_id_type=MESH) for w in range(n)]
        for cp in cps:
            cp.start()
        for cp in cps:
            cp.wait()

    return pl.pallas_call(
        body, in_specs=[ANY] * n, out_specs=[ANY] * n,
        out_shape=[jax.ShapeDtypeStruct(a.shape[1:], a.dtype) for a in parts],
        scratch_shapes=[pltpu.SemaphoreType.DMA((n,)), pltpu.SemaphoreType.DMA((n,))],
        name=name,
    )(*parts)


def _scatter_between_chips(ins, outs, send_sems, recv_sems, local_sems, wait):
    n = len(ins)
    _, _, c, s = _place()

    def block(w, t):
        rows = ins[w].shape[0] // N_CHIPS
        return ins[w].at[pl.ds(t * rows, rows)]

    local = [pltpu.make_async_copy(block(w, s), outs[w].at[s], local_sems.at[w]) for w in range(n)]

    def remote(w, t):
        return pltpu.make_async_remote_copy(
            src_ref=block(w, t), dst_ref=outs[w].at[s], send_sem=send_sems.at[w, t], recv_sem=recv_sems.at[w, s],
            device_id=(t // 2, t % 2, c), device_id_type=MESH)

    def arrival(w, t):
        return pltpu.make_async_remote_copy(
            src_ref=block(w, t), dst_ref=outs[w].at[t], send_sem=send_sems.at[w, t], recv_sem=recv_sems.at[w, t],
            device_id=(t // 2, t % 2, c), device_id_type=MESH)

    if not wait:
        for cp in local:
            cp.start()
    for t in range(N_CHIPS):
        @pl.when(s != t)
        def _():
            for w in range(n):
                if wait:
                    remote(w, t).wait_send()
                    arrival(w, t).wait_recv()
                else:
                    remote(w, t).start()
    if wait:
        for cp in local:
            cp.wait()


def _scatter_shapes(parts):
    return [jax.ShapeDtypeStruct((N_CHIPS, a.shape[0] // N_CHIPS, a.shape[1]), a.dtype) for a in parts]


def _pair_merge(parts, name):
    n = len(parts)

    def body(*refs):
        ins, outs = refs[:n], refs[n:2 * n]
        send_sems, recv_sems = refs[2 * n:]
        x, y, c, _ = _place()
        cps = [pltpu.make_async_remote_copy(
            src_ref=ins[w], dst_ref=outs[w], send_sem=send_sems.at[w], recv_sem=recv_sems.at[w],
            device_id=(x, y, 1 - c), device_id_type=MESH) for w in range(n)]
        for cp in cps:
            cp.start()
        for cp in cps:
            cp.wait()

    return pl.pallas_call(
        body, in_specs=[ANY] * n, out_specs=[ANY] * n,
        out_shape=[jax.ShapeDtypeStruct(a.shape, a.dtype) for a in parts],
        scratch_shapes=[pltpu.SemaphoreType.DMA((n,)), pltpu.SemaphoreType.DMA((n,))],
        name=name,
    )(*parts)


def _all_gather_small(a, name):
    def body(in_ref, out_ref, send_sems, recv_sems, local_sem):
        x, y, c, _ = _place()
        me = 4 * x + 2 * y + c
        local = pltpu.make_async_copy(in_ref, out_ref.at[me], local_sem)
        local.start()

        def remote(d):
            return pltpu.make_async_remote_copy(
                src_ref=in_ref, dst_ref=out_ref.at[me], send_sem=send_sems.at[d], recv_sem=recv_sems.at[me],
                device_id=(d // 4, (d // 2) % 2, d % 2), device_id_type=MESH)

        def arrival(d):
            return pltpu.make_async_remote_copy(
                src_ref=in_ref, dst_ref=out_ref.at[d], send_sem=send_sems.at[d], recv_sem=recv_sems.at[d],
                device_id=(d // 4, (d // 2) % 2, d % 2), device_id_type=MESH)

        for d in range(N_DEV):
            @pl.when(me != d)
            def _():
                remote(d).start()
        for d in range(N_DEV):
            @pl.when(me != d)
            def _():
                remote(d).wait_send()
                arrival(d).wait_recv()
        local.wait()

    return pl.pallas_call(
        body, in_specs=[ANY], out_specs=ANY, out_shape=jax.ShapeDtypeStruct((N_DEV,) + a.shape, a.dtype),
        scratch_shapes=[pltpu.SemaphoreType.DMA((N_DEV,)), pltpu.SemaphoreType.DMA((N_DEV,)), pltpu.SemaphoreType.DMA],
        name=name,
    )(a)


def _reduce_tile(rows):
    return _pick(rows, (256, 128, 16))


def _pair_add(full, other, layer, name):
    _, rows, cols = full.shape
    tr = _reduce_tile(rows)

    def body(layer_ref, a_ref, b_ref, o_ref):
        o_ref[...] = (a_ref[0] + b_ref[...]).astype(o_ref.dtype)

    return pl.pallas_call(
        body,
        grid_spec=pltpu.PrefetchScalarGridSpec(
            num_scalar_prefetch=1, grid=(rows // tr,),
            in_specs=[pl.BlockSpec((1, tr, cols), lambda i, lr: (lr[0], i, 0)), pl.BlockSpec((tr, cols), lambda i, lr: (i, 0))],
            out_specs=pl.BlockSpec((tr, cols), lambda i, lr: (i, 0))),
        out_shape=jax.ShapeDtypeStruct((rows, cols), BF16), compiler_params=_params("parallel"), name=name,
    )(layer, full, other)


def _sum_slots(a, name):
    n, rows, cols = a.shape
    tr = _reduce_tile(rows)

    def body(a_ref, o_ref):
        acc = a_ref[0].astype(F32)
        for k in range(1, n):
            acc = acc + a_ref[k].astype(F32)
        o_ref[...] = acc

    return pl.pallas_call(
        body, grid=(rows // tr,), in_specs=[pl.BlockSpec((n, tr, cols), lambda i: (0, i, 0))],
        out_specs=pl.BlockSpec((tr, cols), lambda i: (i, 0)),
        out_shape=jax.ShapeDtypeStruct((rows, cols), F32), compiler_params=_params("parallel"), name=name,
    )(a)


def _adamw(w, g, m, v, name):
    rows, cols = w.shape
    tr = _pick(rows, (256, 8))

    def body(w_ref, g_ref, m_ref, v_ref, d_ref, nm_ref, nv_ref):
        gv = g_ref[...]
        mn = ADAM_B1 * m_ref[...] + (1.0 - ADAM_B1) * gv
        vn = ADAM_B2 * v_ref[...] + (1.0 - ADAM_B2) * jnp.square(gv)
        m_hat = mn / (1.0 - ADAM_B1 ** ADAM_STEP)
        v_hat = vn / (1.0 - ADAM_B2 ** ADAM_STEP)
        d_ref[...] = -ADAM_LR * (m_hat / (jnp.sqrt(v_hat) + ADAM_EPS) + ADAM_WD * w_ref[...])
        nm_ref[...] = mn
        nv_ref[...] = vn

    blk = pl.BlockSpec((tr, cols), lambda i: (i, 0))
    return pl.pallas_call(
        body, grid=(rows // tr,), in_specs=[blk] * 4, out_specs=[blk] * 3,
        out_shape=[jax.ShapeDtypeStruct((rows, cols), F32)] * 3, compiler_params=_params("parallel"), name=name,
    )(w, g, m, v)


BIG = ("ssm_w_in", "ssm_w_out", "att_w_in", "att_w_out")
SHARDED = BIG + ("ssm_conv_w",)
SMALL = ("pre_norm", "post_norm", "ssm_conv_b", "ssm_dt_bias", "ssm_a_log", "ssm_d", "ssm_gate_norm", "att_sinks")
WEIGHTS = ("pre_norm", "post_norm", "ssm_w_in", "ssm_conv_w", "ssm_conv_b", "ssm_dt_bias", "ssm_a_log", "ssm_d",
           "ssm_gate_norm", "ssm_w_out", "att_w_in", "att_sinks", "att_w_out")


def _halves(a):
    return a.reshape(2, a.shape[0] // 2, a.shape[1])


def _layer_shards(j, ssm_w_in, ssm_w_out, att_w_in, att_w_out, ssm_conv_w):
    return [_halves(ssm_w_in[j].astype(BF16)), _halves(ssm_w_out[j].astype(BF16)), _halves(att_w_in[j].astype(BF16)),
            _halves(att_w_out[j].astype(BF16)), _halves(ssm_conv_w[j])]


SHARD_KEYS = ("ssm_w_in", "ssm_w_out", "att_w_in", "att_w_out", "ssm_conv_w")


def _whole_weights(keys, gathered):
    out = {}
    for k, g in zip(keys, gathered):
        g = g.reshape((N_CHIPS, 2 * g.shape[2], g.shape[3]))
        if k in ("ssm_w_out", "att_w_out"):
            out[k] = g.reshape(N_CHIPS * g.shape[1], g.shape[2])
        else:
            out[k] = jnp.transpose(g, (1, 0, 2)).reshape(g.shape[1], N_CHIPS * g.shape[2])
    if "ssm_w_in" in out:
        out["ssm_w_in"] = jnp.pad(out["ssm_w_in"], ((0, 0), (0, SSM_IN_PAD - SSM_IN_DIM)))
    return out


def _halves_by_chip(key, g):
    if key in ("ssm_w_out", "att_w_out"):
        rows = g.shape[0] // N_CHIPS
        blocks = g.reshape(N_CHIPS, 2, rows // 2, g.shape[1])
        return jnp.transpose(blocks, (1, 0, 2, 3)).reshape(2, N_CHIPS * (rows // 2), g.shape[1])
    cols = (SSM_IN_DIM if key == "ssm_w_in" else g.shape[1]) // N_CHIPS
    rows = g.shape[0]
    blocks = g[:, :N_CHIPS * cols].reshape(2, rows // 2, N_CHIPS, cols)
    return jnp.transpose(blocks, (0, 2, 1, 3)).reshape(2, N_CHIPS * (rows // 2), cols)


def _pack_small(tree, keys):
    flat = jnp.concatenate([tree[k].reshape(-1) for k in keys])
    rows = -(-flat.shape[0] // (8 * LANES)) * 8
    return jnp.pad(flat, (0, rows * LANES - flat.shape[0])).reshape(rows, LANES)


def _unpack_small(packed, shapes, keys):
    flat = packed.reshape(-1)
    out, at = {}, 0
    for k in keys:
        n = 1
        for dim in shapes[k]:
            n *= dim
        out[k] = flat[at:at + n].reshape(shapes[k])
        at += n
    return out


def kernel(x, positions, pre_norm, post_norm, ssm_w_in, ssm_conv_w, ssm_conv_b, ssm_dt_bias, ssm_a_log, ssm_d, ssm_gate_norm, ssm_w_out, att_w_in, att_sinks, att_w_out, loss_target, m_pre_norm, m_post_norm, m_ssm_w_in, m_ssm_conv_w, m_ssm_conv_b, m_ssm_dt_bias, m_ssm_a_log, m_ssm_d, m_ssm_gate_norm, m_ssm_w_out, m_att_w_in, m_att_sinks, m_att_w_out, v_pre_norm, v_post_norm, v_ssm_w_in, v_ssm_conv_w, v_ssm_conv_b, v_ssm_dt_bias, v_ssm_a_log, v_ssm_d, v_ssm_gate_norm, v_ssm_w_out, v_att_w_in, v_att_sinks, v_att_w_out):
    w = dict(pre_norm=pre_norm, post_norm=post_norm, ssm_w_in=ssm_w_in, ssm_conv_w=ssm_conv_w, ssm_conv_b=ssm_conv_b,
             ssm_dt_bias=ssm_dt_bias, ssm_a_log=ssm_a_log, ssm_d=ssm_d, ssm_gate_norm=ssm_gate_norm, ssm_w_out=ssm_w_out,
             att_w_in=att_w_in, att_sinks=att_sinks, att_w_out=att_w_out)
    m = dict(pre_norm=m_pre_norm, post_norm=m_post_norm, ssm_w_in=m_ssm_w_in, ssm_conv_w=m_ssm_conv_w, ssm_conv_b=m_ssm_conv_b,
             ssm_dt_bias=m_ssm_dt_bias, ssm_a_log=m_ssm_a_log, ssm_d=m_ssm_d, ssm_gate_norm=m_ssm_gate_norm,
             ssm_w_out=m_ssm_w_out, att_w_in=m_att_w_in, att_sinks=m_att_sinks, att_w_out=m_att_w_out)
    v = dict(pre_norm=v_pre_norm, post_norm=v_post_norm, ssm_w_in=v_ssm_w_in, ssm_conv_w=v_ssm_conv_w, ssm_conv_b=v_ssm_conv_b,
             ssm_dt_bias=v_ssm_dt_bias, ssm_a_log=v_ssm_a_log, ssm_d=v_ssm_d, ssm_gate_norm=v_ssm_gate_norm,
             ssm_w_out=v_ssm_w_out, att_w_in=v_att_w_in, att_sinks=v_att_sinks, att_w_out=v_att_w_out)
    c = lax.axis_index("c")
    chip = 2 * lax.axis_index("x") + lax.axis_index("y")

    sharded = (ssm_w_in, ssm_w_out, att_w_in, att_w_out, ssm_conv_w)
    own = [dict(zip(SHARD_KEYS, _layer_shards(j, *sharded))) for j in range(2)]
    now_keys = ("ssm_w_in", "ssm_conv_w")
    later_keys = ("ssm_w_out", "att_w_in", "att_w_out")
    first_in = _whole_weights(now_keys, _chip_gather([own[0][k] for k in now_keys], "gather_weights_0"))

    def in_proj_with_first_pair(matmul):
        proj, *arrived = matmul(ride=[own[0][k] for k in later_keys])
        return proj, _whole_weights(later_keys, _pair_handoff(arrived, "gather_weights_0_rest_handoff"))

    def scan_with_second_pair(scan):
        results = scan(ride=[own[1][k] for k in SHARD_KEYS])
        scanned, arrived = results[:-len(SHARD_KEYS)], results[-len(SHARD_KEYS):]
        return (*scanned, _whole_weights(SHARD_KEYS, _pair_handoff(arrived, "gather_weights_1_handoff")))

    half = jnp.reshape(c, (1,)).astype(jnp.int32)

    def reduce_begin(pair_grads, tag):
        parts = [_halves_by_chip(k, pair_grads[k]) for k in BIG]
        from_sibling = _pair_swap(parts, f"reduce_pair_swap_{tag}")
        return [_pair_add(p, o, half, f"reduce_pair_add_{tag}_{n}") for n, (p, o) in enumerate(zip(parts, from_sibling))]

    def reduce_end(by_chip, tag):
        mine = [_sum_slots(a, f"reduce_chip_sum_{tag}_{n}") for n, a in enumerate(by_chip)]
        theirs = _pair_merge(mine, f"reduce_pair_merge_{tag}")
        return {k: jnp.where(c == 0, jnp.concatenate([a, b]), jnp.concatenate([b, a])) for k, a, b in zip(BIG, mine, theirs)}

    def attn_bwd_with_second_pair_grads(attn_bwd, pair_grads):
        dq, dk, dv, dgate, dsk, *by_chip = attn_bwd(ride=reduce_begin(pair_grads, "1"))
        return (dq, dk, dv, dgate, dsk), reduce_end(by_chip, "1")

    def in_dx_with_first_pair_grads(matmul, pair_grads):
        dh, *by_chip = matmul(ride=reduce_begin(pair_grads, "0"), ride_scatters=True)
        return dh, reduce_end(by_chip, "0")

    loss_lanes, grad_x, gr, reduced_0, reduced_1 = _local_step(
        x[0], positions[0], pre_norm, post_norm, ssm_conv_b, ssm_dt_bias, ssm_a_log, ssm_d, ssm_gate_norm, att_sinks,
        loss_target[0], first_in, in_proj_with_first_pair, scan_with_second_pair, attn_bwd_with_second_pair_grads,
        in_dx_with_first_pair_grads)
    loss = lax.psum(0.5 * jnp.sum(loss_lanes) / D_MODEL, ("x", "y", "c"))
    grads = {k: jnp.stack([reduced_0[k], reduced_1[k]]) for k in BIG}

    small_keys = SMALL + ("ssm_conv_w",)
    small_shapes = {k: w[k].shape for k in SMALL}
    small_shapes["ssm_conv_w"] = gr["ssm_conv_w"].shape
    small_sum = _sum_slots(_all_gather_small(_pack_small(gr, small_keys), "reduce_small_gather"), "reduce_small_sum")
    grads.update(_unpack_small(small_sum, small_shapes, small_keys))
    conv_cols = ssm_conv_w.shape[2]
    grads["ssm_conv_w"] = lax.dynamic_slice_in_dim(grads["ssm_conv_w"], chip * conv_cols, conv_cols, axis=2)

    delta, new_m, new_v = {}, {}, {}
    for k in SHARDED:
        shp = w[k].shape
        two_d = (shp[0] * shp[1], shp[2])
        d_, m_, v_ = _adamw(w[k].reshape(two_d), grads[k].reshape(two_d), m[k].reshape(two_d), v[k].reshape(two_d),
                            f"adamw_{k}")
        delta[k], new_m[k], new_v[k] = d_.reshape(shp), m_.reshape(shp), v_.reshape(shp)
    d_, m_, v_ = _adamw(_pack_small(w, SMALL), _pack_small(grads, SMALL), _pack_small(m, SMALL), _pack_small(v, SMALL),
                        "adamw_small")
    delta.update(_unpack_small(d_, small_shapes, SMALL))
    new_m.update(_unpack_small(m_, small_shapes, SMALL))
    new_v.update(_unpack_small(v_, small_shapes, SMALL))

    return (loss, grad_x[None], *[grads[k] for k in WEIGHTS], *[delta[k] for k in WEIGHTS],
            *[new_m[k] for k in WEIGHTS], *[new_v[k] for k in WEIGHTS])
```

```python
import functools

import jax
import jax.numpy as jnp
from jax import lax
from jax.experimental import pallas as pl
from jax.experimental.pallas import tpu as pltpu

F32 = jnp.float32
BF16 = jnp.bfloat16
EPS = 1e-6
NEG_INF = float("-inf")

D_MODEL = 1024
DEPTH = 4
SSM_D_INNER = 2048
SSM_HEAD_DIM = 64
SSM_HEADS = 32
SSM_GROUPS = 8
SSM_HPG = 4
SSM_STATE = 128
SSM_CONV = 4
SSM_CHUNK = 128
SSM_BC_DIM = 1024
SSM_CONV_DIM = 4096
SSM_IN_DIM = 6176
SSM_IN_PAD = 6272
SSM_DT_PAD = 128
ATT_HEAD_DIM = 64
ATT_Q_HEADS = 16
ATT_KV_HEADS = 4
ATT_GQA = 4
ATT_WIDTH = 1024
ATT_KV_WIDTH = 256
ATT_IN_DIM = 2560
ATT_QKV = ATT_WIDTH + 2 * ATT_KV_WIDTH
ATT_BLOCK = 128
ROPE_THETA = 500000.0
ROPE_DIM = 16
ROPE_HALF = 8
Q_SCALE = ATT_HEAD_DIM ** -0.5

ADAM_LR = 0.001
ADAM_B1 = 0.9
ADAM_B2 = 0.999
ADAM_EPS = 1e-08
ADAM_WD = 0.01
ADAM_STEP = 10

VMEM_LIMIT_BYTES = 48 * 1024 * 1024
NT_DIMS = (((1,), (1,)), ((), ()))
TN_DIMS = (((0,), (0,)), ((), ()))


def _params(*sem):
    return pltpu.CompilerParams(dimension_semantics=sem, vmem_limit_bytes=VMEM_LIMIT_BYTES)


def _pick(n, cands):
    for c in cands:
        if n % c == 0:
            return c
    return n


def _sigmoid(v):
    return 0.5 * jnp.tanh(0.5 * v) + 0.5


def _bdot_tn(a, b):
    return lax.dot_general(a.astype(BF16), b.astype(BF16), TN_DIMS, preferred_element_type=F32)


MATMUL_VMEM_BUDGET = 36 * 1024 * 1024


def _matmul_tiles(m, n, k, out_bytes, reduce_rows):
    best = None
    whole = [k] if (not reduce_rows or k <= 2048) else []
    for tk in whole + [c for c in (4096, 2048, 1024, 896, 512) if k % c == 0 and c < k]:
        for tm in (c for c in (2048, 1024, 512, 256) if m % c == 0):
            for tn in (c for c in (n, 1280, 1024, 896, 640, 512) if n % c == 0):
                acc = tm * tn * 4 if tk < k else 0
                need = 2 * (2 * tk * (tm + tn) + tm * tn * out_bytes) + acc
                if need <= MATMUL_VMEM_BUDGET and (best is None or tm * tn * min(tk, 2048) > best[0]):
                    best = (tm * tn * min(tk, 2048), tm, tn, tk)
        if best is not None and not reduce_rows:
            break
    return best[1:]


def _matmul(a, b, mode, out_dtype, name, ride=(), ride_scatters=False):
    if mode == "nn":
        (m, k), n = a.shape, b.shape[1]
    elif mode == "nt":
        (m, k), n = a.shape, b.shape[0]
    else:
        (k, m), n = a.shape, b.shape[1]
    tm, tn, tk = _matmul_tiles(m, n, k, jnp.dtype(out_dtype).itemsize, mode == "tn")
    nk = k // tk
    steps = (n // tn, m // tm, nk)
    dims = {"nn": (((1,), (0,)), ((), ())), "nt": NT_DIMS, "tn": TN_DIMS}[mode]
    n_ride = len(ride)
    exchange = _scatter_between_chips if ride_scatters else _gather_between_chips
    arrived = _scatter_shapes(ride) if ride_scatters else [jax.ShapeDtypeStruct((N_CHIPS,) + r.shape, r.dtype) for r in ride]

    def body(*refs):
        a_ref, b_ref = refs[:2]
        ride_in = refs[2:2 + n_ride]
        o_ref = refs[2 + n_ride]
        ride_out = refs[3 + n_ride:3 + 2 * n_ride]
        acc_ref = refs[3 + 2 * n_ride]
        ride_sems = refs[4 + 2 * n_ride:]
        kk = pl.program_id(2)
        at = [pl.program_id(d) for d in range(3)]
        if n_ride:
            @pl.when((at[0] == 0) & (at[1] == 0) & (at[2] == 0))
            def _():
                exchange(ride_in, ride_out, *ride_sems, wait=False)

        part = lax.dot_general(a_ref[...], b_ref[...], dims, preferred_element_type=F32)
        if nk == 1:
            o_ref[...] = part.astype(o_ref.dtype)
        else:
            @pl.when(kk == 0)
            def _():
                acc_ref[...] = part

            @pl.when(kk > 0)
            def _():
                acc_ref[...] += part

            @pl.when(kk == nk - 1)
            def _():
                o_ref[...] = acc_ref[...].astype(o_ref.dtype)

        if n_ride:
            @pl.when((at[0] == steps[0] - 1) & (at[1] == steps[1] - 1) & (at[2] == steps[2] - 1))
            def _():
                exchange(ride_in, ride_out, *ride_sems, wait=True)

    if mode == "nn":
        a_spec = pl.BlockSpec((tm, tk), lambda j, i, kk: (i, kk))
        b_spec = pl.BlockSpec((tk, tn), lambda j, i, kk: (kk, j))
    elif mode == "nt":
        a_spec = pl.BlockSpec((tm, tk), lambda j, i, kk: (i, kk))
        b_spec = pl.BlockSpec((tn, tk), lambda j, i, kk: (j, kk))
    else:
        a_spec = pl.BlockSpec((tk, tm), lambda j, i, kk: (kk, i))
        b_spec = pl.BlockSpec((tk, tn), lambda j, i, kk: (kk, j))
    out = pl.pallas_call(
        body, grid=steps, in_specs=[a_spec, b_spec] + [ANY] * n_ride,
        out_specs=[pl.BlockSpec((tm, tn), lambda j, i, kk: (i, j))] + [ANY] * n_ride,
        out_shape=[jax.ShapeDtypeStruct((m, n), out_dtype)] + arrived,
        scratch_shapes=[pltpu.VMEM((tm, tn), F32)] + (_gather_sems(n_ride) if n_ride else []),
        compiler_params=_params(*(["arbitrary"] * 3 if n_ride else ["parallel", "parallel", "arbitrary"])), name=name,
    )(a, b, *ride)
    return out if n_ride else out[0]


def _row_tile(l):
    return _pick(l, (512, 256, 128))


def _rmsnorm_fwd(x, w, name):
    l, d = x.shape
    tl = _row_tile(l)

    def body(x_ref, w_ref, o_ref):
        xv = x_ref[...]
        r = lax.rsqrt(jnp.mean(xv * xv, axis=-1, keepdims=True) + EPS)
        o_ref[...] = (xv * r * w_ref[...]).astype(o_ref.dtype)

    return pl.pallas_call(
        body, grid=(l // tl,),
        in_specs=[pl.BlockSpec((tl, d), lambda i: (i, 0)), pl.BlockSpec((1, d), lambda i: (0, 0))],
        out_specs=pl.BlockSpec((tl, d), lambda i: (i, 0)),
        out_shape=jax.ShapeDtypeStruct((l, d), BF16), compiler_params=_params("parallel"), name=name,
    )(x, w.reshape(1, d))


def _post_fwd(x, y, w, w_next, name):
    l, d = x.shape
    tl = _row_tile(l)

    def body(x_ref, y_ref, w_ref, wn_ref, o_ref, h_ref):
        yv = y_ref[...]
        r = lax.rsqrt(jnp.mean(yv * yv, axis=-1, keepdims=True) + EPS)
        out = x_ref[...] + yv * r * w_ref[...]
        o_ref[...] = out
        rn = lax.rsqrt(jnp.mean(out * out, axis=-1, keepdims=True) + EPS)
        h_ref[...] = (out * rn * wn_ref[...]).astype(h_ref.dtype)

    row = pl.BlockSpec((tl, d), lambda i: (i, 0))
    vec = pl.BlockSpec((1, d), lambda i: (0, 0))
    return pl.pallas_call(
        body, grid=(l // tl,), in_specs=[row, row, vec, vec], out_specs=[row, row],
        out_shape=[jax.ShapeDtypeStruct((l, d), F32), jax.ShapeDtypeStruct((l, d), BF16)],
        compiler_params=_params("parallel"), name=name,
    )(x, y, w.reshape(1, d), w_next.reshape(1, d))


def _post_loss(x, y, w, t, name):
    l, d = x.shape
    tl = _row_tile(l)
    nt = l // tl

    def body(x_ref, y_ref, w_ref, t_ref, g_ref, dy_ref, ls_ref, dw_ref, acc_ref):
        i = pl.program_id(0)

        @pl.when(i == 0)
        def _():
            ls_ref[...] = jnp.zeros_like(ls_ref)
            acc_ref[...] = jnp.zeros_like(acc_ref)

        yv = y_ref[...]
        r = lax.rsqrt(jnp.mean(yv * yv, axis=-1, keepdims=True) + EPS)
        nrm = yv * r
        e = x_ref[...] + nrm * w_ref[...] - t_ref[...]
        gv = e * (1.0 / d)
        g_ref[...] = gv
        ls_ref[...] += jnp.sum((e * e).reshape(tl // 8, 8, d), axis=0)
        gw = gv * w_ref[...]
        dy_ref[...] = (r * (gw - nrm * jnp.mean(gw * nrm, axis=-1, keepdims=True))).astype(dy_ref.dtype)
        acc_ref[...] += jnp.sum((gv * nrm).reshape(tl // 8, 8, d), axis=0)

        @pl.when(i == nt - 1)
        def _():
            dw_ref[...] = jnp.sum(acc_ref[...], axis=0, keepdims=True)

    row = pl.BlockSpec((tl, d), lambda i: (i, 0))
    vec = pl.BlockSpec((1, d), lambda i: (0, 0))
    return pl.pallas_call(
        body, grid=(nt,), in_specs=[row, row, vec, row],
        out_specs=[row, row, pl.BlockSpec((8, d), lambda i: (0, 0)), vec],
        out_shape=[jax.ShapeDtypeStruct((l, d), F32), jax.ShapeDtypeStruct((l, d), BF16),
                   jax.ShapeDtypeStruct((8, d), F32), jax.ShapeDtypeStruct((1, d), F32)],
        scratch_shapes=[pltpu.VMEM((8, d), F32)], compiler_params=_params("arbitrary"), name=name,
    )(x, y, w.reshape(1, d), t)


def _norm_bwd_chain(dh, x, w_pre, resid, y_prev, w_post_prev, name):
    l, d = x.shape
    tl = _row_tile(l)
    nt = l // tl

    def body(dh_ref, x_ref, wp_ref, r_ref, y_ref, wq_ref, g_ref, dy_ref, dwp_ref, dwq_ref, accp_ref, accq_ref):
        i = pl.program_id(0)

        @pl.when(i == 0)
        def _():
            accp_ref[...] = jnp.zeros_like(accp_ref)
            accq_ref[...] = jnp.zeros_like(accq_ref)

        xv = x_ref[...]
        dhv = dh_ref[...]
        rx = lax.rsqrt(jnp.mean(xv * xv, axis=-1, keepdims=True) + EPS)
        nx = xv * rx
        gw = dhv * wp_ref[...]
        gv = rx * (gw - nx * jnp.mean(gw * nx, axis=-1, keepdims=True)) + r_ref[...]
        g_ref[...] = gv
        accp_ref[...] += jnp.sum((dhv * nx).reshape(tl // 8, 8, d), axis=0)
        yv = y_ref[...]
        ry = lax.rsqrt(jnp.mean(yv * yv, axis=-1, keepdims=True) + EPS)
        ny = yv * ry
        gq = gv * wq_ref[...]
        dy_ref[...] = (ry * (gq - ny * jnp.mean(gq * ny, axis=-1, keepdims=True))).astype(dy_ref.dtype)
        accq_ref[...] += jnp.sum((gv * ny).reshape(tl // 8, 8, d), axis=0)

        @pl.when(i == nt - 1)
        def _():
            dwp_ref[...] = jnp.sum(accp_ref[...], axis=0, keepdims=True)
            dwq_ref[...] = jnp.sum(accq_ref[...], axis=0, keepdims=True)

    row = pl.BlockSpec((tl, d), lambda i: (i, 0))
    vec = pl.BlockSpec((1, d), lambda i: (0, 0))
    return pl.pallas_call(
        body, grid=(nt,), in_specs=[row, row, vec, row, row, vec], out_specs=[row, row, vec, vec],
        out_shape=[jax.ShapeDtypeStruct((l, d), F32), jax.ShapeDtypeStruct((l, d), BF16),
                   jax.ShapeDtypeStruct((1, d), F32), jax.ShapeDtypeStruct((1, d), F32)],
        scratch_shapes=[pltpu.VMEM((8, d), F32), pltpu.VMEM((8, d), F32)],
        compiler_params=_params("arbitrary"), name=name,
    )(dh, x, w_pre.reshape(1, d), resid, y_prev, w_post_prev.reshape(1, d))


def _rmsnorm_bwd(g, y, w, resid, name):
    l, d = y.shape
    tl = _row_tile(l)
    nt = l // tl

    def body(g_ref, y_ref, w_ref, r_ref, dy_ref, dw_ref, acc_ref):
        i = pl.program_id(0)

        @pl.when(i == 0)
        def _():
            acc_ref[...] = jnp.zeros_like(acc_ref)

        yv = y_ref[...]
        gv = g_ref[...]
        r = lax.rsqrt(jnp.mean(yv * yv, axis=-1, keepdims=True) + EPS)
        nrm = yv * r
        gw = gv * w_ref[...]
        dy_ref[...] = r * (gw - nrm * jnp.mean(gw * nrm, axis=-1, keepdims=True)) + r_ref[...]
        acc_ref[...] += jnp.sum((gv * nrm).reshape(tl // 8, 8, d), axis=0)

        @pl.when(i == nt - 1)
        def _():
            dw_ref[...] = jnp.sum(acc_ref[...], axis=0, keepdims=True)

    row = pl.BlockSpec((tl, d), lambda i: (i, 0))
    vec = pl.BlockSpec((1, d), lambda i: (0, 0))
    return pl.pallas_call(
        body, grid=(nt,), in_specs=[row, row, vec, row], out_specs=[row, vec],
        out_shape=[jax.ShapeDtypeStruct((l, d), F32), jax.ShapeDtypeStruct((1, d), F32)],
        scratch_shapes=[pltpu.VMEM((8, d), F32)], compiler_params=_params("arbitrary"), name=name,
    )(g, y, w.reshape(1, d), resid)


HALO = 8
CONV_SUB_ROWS = 64
CONV_SUB_COLS = 256


DT_COL_BLOCK = (SSM_D_INNER + SSM_CONV_DIM) // SSM_DT_PAD


def _split3(v):
    hi = v.astype(BF16)
    rest = v - hi.astype(F32)
    mid = rest.astype(BF16)
    lo = (rest - mid.astype(F32)).astype(BF16)
    return hi, mid, lo


def _dt_and_decay(v, a_log):
    head_dim_log2 = SSM_HEAD_DIM.bit_length() - 1
    dt_hi, dt_mid, _ = _split3(jnp.maximum(v, 0.0) + jnp.log1p(jnp.exp(-jnp.abs(v))))
    dt = dt_hi.astype(F32) + dt_mid.astype(F32)
    ri = lax.broadcasted_iota(jnp.int32, (SSM_CHUNK, SSM_CHUNK), 0)
    cj = lax.broadcasted_iota(jnp.int32, (SSM_CHUNK, SSM_CHUNK), 1)
    tri = (ri >= cj).astype(BF16)
    acs_pieces = _split3(sum(jnp.dot(tri, piece, preferred_element_type=F32)
                             for piece in _split3(dt * (-jnp.exp(a_log)))))
    acs = sum(piece.astype(F32) for piece in acs_pieces)
    head_of_lane = lax.shift_right_logical(lax.broadcasted_iota(jnp.int32, (SSM_DT_PAD, SSM_D_INNER), 1), head_dim_log2)
    spread = (head_of_lane == lax.broadcasted_iota(jnp.int32, (SSM_DT_PAD, SSM_D_INNER), 0)).astype(BF16)
    dtb = sum(jnp.dot(piece, spread, preferred_element_type=F32) for piece in (dt_hi, dt_mid))
    acsb = sum(jnp.dot(piece, spread, preferred_element_type=F32) for piece in acs_pieces)
    return dtb, acsb, dt.T, acs.T


def _dt_bwd(ddt, proj, bias, dproj, name):
    l = proj.shape[0]
    tl = _row_tile(l)

    def body(g_ref, p_ref, b_ref, _, o_ref, db_ref):
        @pl.when(pl.program_id(0) == 0)
        def _():
            db_ref[...] = jnp.zeros_like(db_ref)

        d = g_ref[...] * _sigmoid(p_ref[...] + b_ref[...])
        o_ref[...] = d.astype(o_ref.dtype)
        db_ref[...] += jnp.sum(d, axis=0, keepdims=True)

    return pl.pallas_call(
        body, grid=(l // tl,),
        in_specs=[pl.BlockSpec((tl, SSM_DT_PAD), lambda i: (i, 0)),
                  pl.BlockSpec((tl, SSM_DT_PAD), lambda i: (i, DT_COL_BLOCK)),
                  pl.BlockSpec((1, SSM_DT_PAD), lambda i: (0, 0)),
                  pl.BlockSpec(memory_space=pl.ANY)],
        out_specs=[pl.BlockSpec((tl, SSM_DT_PAD), lambda i: (i, DT_COL_BLOCK)),
                   pl.BlockSpec((1, SSM_DT_PAD), lambda i: (0, 0))],
        out_shape=[jax.ShapeDtypeStruct(dproj.shape, dproj.dtype), jax.ShapeDtypeStruct((1, SSM_DT_PAD), F32)],
        input_output_aliases={3: 0}, compiler_params=_params("arbitrary"), name=name,
    )(ddt, proj, bias, dproj)


GP = SSM_HPG * SSM_HEAD_DIM
HEAD_DIM_LOG2 = SSM_HEAD_DIM.bit_length() - 1
GPS = SSM_GROUPS
B_BLOCK0 = SSM_D_INNER // SSM_STATE
C_BLOCK0 = (SSM_D_INNER + SSM_BC_DIM) // SSM_STATE


def _chunk_iotas():
    ri = lax.broadcasted_iota(jnp.int32, (SSM_CHUNK, SSM_CHUNK), 0)
    cj = lax.broadcasted_iota(jnp.int32, (SSM_CHUNK, SSM_CHUNK), 1)
    return ri, cj


def _head_decay(acsb, acs_r, r, ri, cj):
    pair = acsb[:, (r // 2) * LANES:(r // 2 + 1) * LANES]
    mine_low = r % 2 == 0
    lane = lax.broadcasted_iota(jnp.int32, (1, LANES), 1)
    col = jnp.where((lane < SSM_HEAD_DIM) == mine_low, pair, pltpu.roll(pair, SSM_HEAD_DIM, 1))
    return jnp.exp(jnp.where(ri >= cj, col - acs_r[r:r + 1, :], NEG_INF))


def _head_masked_rows(v, dtype):
    head_of_lane = lax.shift_right_logical(lax.broadcasted_iota(jnp.int32, (1, GP), 1), HEAD_DIM_LOG2)
    narrow = v.astype(dtype)
    return jnp.concatenate([jnp.where(head_of_lane == r, narrow, jnp.zeros_like(narrow)) for r in range(SSM_HPG)], axis=0)


def _ssd_fwd(proj, cw, cb, dt_bias, a_log, d_lanes, gate_w, name, ride=()):
    l = proj.shape[0]
    nc = l // SSM_CHUNK
    assert GPS == SSM_GROUPS
    n_ride = len(ride)
    halo_blocks = SSM_CHUNK // HALO
    x_block = 1

    def body(*refs):
        u0_ref, u1_ref, h0_ref, h1_ref, cw_ref, cb_ref, dtraw_ref, bias_ref, alog_ref, d_ref, z_ref, gw_ref = refs[:12]
        ride_in = refs[12:12 + n_ride]
        (y_ref, act_ref, hin_ref, pre_ref, xbc_ref, dtb_out, acsb_out, dtr_out, acsr_out) = refs[12 + n_ride:21 + n_ride]
        ride_out = refs[21 + n_ride:21 + 2 * n_ride]
        h_ref, ext_ref, conv_ref, dtb_ref, acsb_ref, acsr_ref = refs[21 + 2 * n_ride:27 + 2 * n_ride]
        ride_sems = refs[27 + 2 * n_ride:]
        s = pl.program_id(0)
        if n_ride:
            @pl.when(s == 0)
            def _():
                _gather_between_chips(ride_in, ride_out, *ride_sems, wait=False)

            @pl.when(s == nc)
            def _():
                _gather_between_chips(ride_in, ride_out, *ride_sems, wait=True)

        @pl.when(s <= 1)
        def _():
            h_ref[...] = jnp.zeros_like(h_ref)

        @pl.when(s == 0)
        def _():
            conv_ref[1] = jnp.zeros((SSM_CHUNK, SSM_CONV_DIM), BF16)
            dtb_ref[1] = jnp.zeros((SSM_CHUNK, SSM_D_INNER), F32)
            acsb_ref[1] = jnp.zeros((SSM_CHUNK, SSM_D_INNER), F32)
            acsr_ref[1] = jnp.zeros((SSM_GROUPS, SSM_HPG, SSM_CHUNK), F32)

        conv_slot = s & 1
        scan_slot = (s - 1) & 1
        for half, (u_ref, hl_ref) in enumerate(((u0_ref, h0_ref), (u1_ref, h1_ref))):
            hc = slice(half * SSM_D_INNER, (half + 1) * SSM_D_INNER)
            ext_ref[0:HALO, hc] = jnp.where(s > 0, hl_ref[...], 0.0)
            ext_ref[HALO:HALO + SSM_CHUNK, hc] = u_ref[...]

        def conv_columns(c_lo, c_hi):
            for r0 in range(0, SSM_CHUNK, CONV_SUB_ROWS):
                for c0 in range(c_lo, c_hi, CONV_SUB_COLS):
                    cs = slice(c0, c0 + CONV_SUB_COLS)
                    ext = ext_ref[r0:r0 + CONV_SUB_ROWS + HALO, cs]
                    acc = cb_ref[:, cs] + cw_ref[SSM_CONV - 1:SSM_CONV, cs] * ext[HALO:]
                    for k in range(SSM_CONV - 1):
                        acc = acc + cw_ref[k:k + 1, cs] * pltpu.roll(ext, SSM_CONV - 1 - k, 0)[HALO:]
                    act = (acc * _sigmoid(acc)).astype(BF16)
                    pre_ref[r0:r0 + CONV_SUB_ROWS, cs] = acc.astype(pre_ref.dtype)
                    xbc_ref[r0:r0 + CONV_SUB_ROWS, cs] = act
                    conv_ref[conv_slot, r0:r0 + CONV_SUB_ROWS, cs] = act

        dtb, acsb, dt_rows, acs_rows = _dt_and_decay(dtraw_ref[...] + bias_ref[...], alog_ref[...])
        dtb_out[...] = dtb
        acsb_out[...] = acsb
        dtb_ref[conv_slot] = dtb
        acsb_ref[conv_slot] = acsb
        for g in range(SSM_GROUPS):
            heads = slice(g * SSM_HPG, (g + 1) * SSM_HPG)
            dtr_out[g] = dt_rows[heads, :]
            acsr_out[g] = acs_rows[heads, :]
            acsr_ref[conv_slot, g] = acs_rows[heads, :]

        ri, cj = _chunk_iotas()
        conv_share = SSM_CONV_DIM // GPS
        for k in range(GPS):
            conv_columns(k * conv_share, (k + 1) * conv_share)
            g = k
            cols = slice(k * GP, (k + 1) * GP)
            bcols = slice(SSM_D_INNER + k * SSM_STATE, SSM_D_INNER + (k + 1) * SSM_STATE)
            ccols = slice(SSM_D_INNER + SSM_BC_DIM + k * SSM_STATE, SSM_D_INNER + SSM_BC_DIM + (k + 1) * SSM_STATE)
            xv = conv_ref[scan_slot, :, cols].astype(F32)
            bb = conv_ref[scan_slot, :, bcols]
            cb16 = conv_ref[scan_slot, :, ccols]
            acs_v = acsb_ref[scan_slot, :, cols]
            acs_r_v = acsr_ref[scan_slot, k]
            lastb = acs_v[SSM_CHUNK - 1:SSM_CHUNK, :]
            xd = xv * dtb_ref[scan_slot, :, cols]
            cbm = lax.dot_general(cb16, bb, NT_DIMS, preferred_element_type=F32)
            hin = h_ref[g]
            hin_ref[0, k] = hin
            yoff = jnp.dot(cb16, hin.astype(BF16), preferred_element_type=F32)
            ms = [(cbm * _head_decay(acs_v, acs_r_v, r, ri, cj)).astype(BF16) for r in range(SSM_HPG)]
            ydiag = jnp.dot(jnp.concatenate(ms, axis=1), _head_masked_rows(xd, BF16), preferred_element_type=F32)
            y_ref[:, cols] = ydiag + jnp.exp(acs_v) * yoff + d_ref[k] * xv
            h_ref[g] = hin * jnp.exp(lastb) + _bdot_tn(bb, xd * jnp.exp(lastb - acs_v))
        z = z_ref[...]
        yg = y_ref[...] * (z * _sigmoid(z))
        r = lax.rsqrt(jnp.mean(yg * yg, axis=-1, keepdims=True) + EPS)
        act_ref[...] = (yg * r * gw_ref[...]).astype(act_ref.dtype)

    def conv_at(s):
        return jnp.minimum(s, nc - 1)

    def scan_at(s):
        return jnp.maximum(s - 1, 0)

    lanes = pl.BlockSpec((SSM_CHUNK, SSM_D_INNER), lambda s: (scan_at(s), 0))
    conv_out = pl.BlockSpec((SSM_CHUNK, SSM_CONV_DIM), lambda s: (conv_at(s), 0))
    lanes_ahead = pl.BlockSpec((SSM_CHUNK, SSM_D_INNER), lambda s: (conv_at(s), 0))
    rows_ahead = pl.BlockSpec((SSM_GROUPS, SSM_HPG, SSM_CHUNK), lambda s: (0, 0, conv_at(s)))
    return pl.pallas_call(
        body, grid=(nc + 1,),
        in_specs=[pl.BlockSpec((SSM_CHUNK, SSM_D_INNER), lambda s: (conv_at(s), x_block)),
                  pl.BlockSpec((SSM_CHUNK, SSM_D_INNER), lambda s: (conv_at(s), x_block + 1)),
                  pl.BlockSpec((HALO, SSM_D_INNER), lambda s: (jnp.maximum(conv_at(s) * halo_blocks - 1, 0), x_block)),
                  pl.BlockSpec((HALO, SSM_D_INNER), lambda s: (jnp.maximum(conv_at(s) * halo_blocks - 1, 0), x_block + 1)),
                  pl.BlockSpec((SSM_CONV, SSM_CONV_DIM), lambda s: (0, 0)),
                  pl.BlockSpec((1, SSM_CONV_DIM), lambda s: (0, 0)),
                  pl.BlockSpec((SSM_CHUNK, SSM_DT_PAD), lambda s: (conv_at(s), DT_COL_BLOCK)),
                  pl.BlockSpec((1, SSM_DT_PAD), lambda s: (0, 0)),
                  pl.BlockSpec((1, SSM_DT_PAD), lambda s: (0, 0)),
                  pl.BlockSpec((SSM_GROUPS, 1, GP), lambda s: (0, 0, 0)),
                  lanes, pl.BlockSpec((1, SSM_D_INNER), lambda s: (0, 0))] + [ANY] * n_ride,
        out_specs=[lanes, lanes, pl.BlockSpec((1, SSM_GROUPS, SSM_STATE, GP), lambda s: (scan_at(s), 0, 0, 0)),
                   conv_out, conv_out, lanes_ahead, lanes_ahead, rows_ahead, rows_ahead] + [ANY] * n_ride,
        out_shape=[jax.ShapeDtypeStruct((l, SSM_D_INNER), F32), jax.ShapeDtypeStruct((l, SSM_D_INNER), BF16),
                   jax.ShapeDtypeStruct((nc, SSM_GROUPS, SSM_STATE, GP), F32),
                   jax.ShapeDtypeStruct((l, SSM_CONV_DIM), BF16), jax.ShapeDtypeStruct((l, SSM_CONV_DIM), BF16),
                   jax.ShapeDtypeStruct((l, SSM_D_INNER), F32), jax.ShapeDtypeStruct((l, SSM_D_INNER), F32),
                   jax.ShapeDtypeStruct((SSM_GROUPS, SSM_HPG, l), F32),
                   jax.ShapeDtypeStruct((SSM_GROUPS, SSM_HPG, l), F32)]
        + [jax.ShapeDtypeStruct((N_CHIPS,) + a.shape, a.dtype) for a in ride],
        scratch_shapes=[pltpu.VMEM((SSM_GROUPS, SSM_STATE, GP), F32),
                        pltpu.VMEM((SSM_CHUNK + HALO, SSM_CONV_DIM), F32),
                        pltpu.VMEM((2, SSM_CHUNK, SSM_CONV_DIM), BF16),
                        pltpu.VMEM((2, SSM_CHUNK, SSM_D_INNER), F32), pltpu.VMEM((2, SSM_CHUNK, SSM_D_INNER), F32),
                        pltpu.VMEM((2, SSM_GROUPS, SSM_HPG, SSM_CHUNK), F32)] + (_gather_sems(n_ride) if n_ride else []),
        compiler_params=_params("arbitrary"), name=name,
    )(proj, proj, proj, proj, cw, cb.reshape(1, SSM_CONV_DIM), proj, dt_bias, a_log, d_lanes, proj,
      gate_w.reshape(1, SSM_D_INNER), *ride)


def _ssd_bwd(xbc, pre, cw, dtb, acsb, dtr, acs_r, a_log, d_lanes, hin, dact, y, proj, gate_w, name):
    l = xbc.shape[0]
    nc = l // SSM_CHUNK
    sub = CONV_SUB_ROWS

    def body(x_ref, b_ref, c_ref, dtb_ref, acsb_ref, dtr_ref, acsr_ref, alc_ref, d_ref, hin_ref,
             dact_ref, y_ref, z_ref, gw_ref, pre_ref, u0_ref, u1_ref, cw_ref,
             dproj_ref, ddt_ref, dal_ref, dd_ref, dgw_ref, dcw_ref, dcb_ref,
             dh_ref, dy_ref, acc_ref, dxbc_s, dz_s, ddt_s, carry_ref, ext_ref, dcw_acc, dcb_acc):
        step = pl.program_id(0)
        live = (step < nc).astype(F32)
        stage = step & 1
        staged = (step - 1) & 1

        @pl.when(step == 0)
        def _():
            for ref in (dal_ref, dd_ref, acc_ref, dh_ref, carry_ref, dcw_acc, dcb_acc):
                ref[...] = jnp.zeros_like(ref)
            dxbc_s[1] = jnp.zeros((SSM_CHUNK, SSM_CONV_DIM), F32)
            dz_s[1] = jnp.zeros((SSM_CHUNK, SSM_D_INNER), BF16)
            ddt_s[1] = jnp.zeros((SSM_GROUPS, SSM_HPG, SSM_CHUNK), F32)

        z = z_ref[...]
        yv = y_ref[...]
        s = _sigmoid(z)
        sz = z * s
        yg = yv * sz
        r = lax.rsqrt(jnp.mean(yg * yg, axis=-1, keepdims=True) + EPS)
        nrm = yg * r
        gv = dact_ref[...]
        gw = gv * gw_ref[...]
        dyg = r * (gw - nrm * jnp.mean(gw * nrm, axis=-1, keepdims=True))
        dy_ref[...] = dyg * sz
        dz_s[stage] = (dyg * yv * (s * (1.0 + z * (1.0 - s)))).astype(BF16)
        acc_ref[...] += live * jnp.sum((gv * nrm).reshape(SSM_CHUNK // 8, 8, SSM_D_INNER), axis=0)

        p = pre_ref[...].astype(F32)
        sp = _sigmoid(p)
        dp = dxbc_s[staged] * (sp * (1.0 + p * (1.0 - sp)))
        ext_ref[0:SSM_CHUNK, :] = dp
        ext_ref[SSM_CHUNK:SSM_CHUNK + HALO, :] = carry_ref[...]
        carry_ref[...] = dp[0:HALO]
        dproj_ref[:, 0:SSM_D_INNER] = dz_s[staged]
        ddt_ref[...] = ddt_s[staged]

        def fold(v):
            return jnp.sum(v.reshape(sub // 8, 8, CONV_SUB_COLS), axis=0)

        def conv_columns(c_lo, c_hi):
            for c0 in range(c_lo, c_hi, CONV_SUB_COLS):
                cs = slice(c0, c0 + CONV_SUB_COLS)
                u_ref, ucs = (u0_ref, cs) if c0 < SSM_D_INNER else (u1_ref, slice(c0 - SSM_D_INNER, c0 - SSM_D_INNER + CONV_SUB_COLS))
                for r0 in range(0, SSM_CHUNK, sub):
                    dext = ext_ref[r0:r0 + sub + HALO, cs]
                    uv = u_ref[r0:r0 + sub, ucs]
                    for k in range(SSM_CONV):
                        j = SSM_CONV - 1 - k
                        ahead = dext[:sub] if j == 0 else pltpu.roll(dext, sub + HALO - j, 0)[:sub]
                        term = cw_ref[k:k + 1, cs] * ahead
                        du = term if k == 0 else du + term
                        dcw_acc[k, :, cs] += fold(ahead * uv)
                    dcb_acc[:, cs] += fold(dext[:sub])
                    dproj_ref[r0:r0 + sub, SSM_D_INNER + c0:SSM_D_INNER + c0 + CONV_SUB_COLS] = du.astype(dproj_ref.dtype)

        conv_share = SSM_CONV_DIM // GPS
        for k in range(GPS):
            conv_columns(k * conv_share, (k + 1) * conv_share)
            one_group(live, stage, k, k, x_ref, b_ref, c_ref, dtb_ref, acsb_ref, dtr_ref, acsr_ref, alc_ref, d_ref,
                      hin_ref, dy_ref, dxbc_s, ddt_s, dal_ref, dd_ref, dh_ref)

        @pl.when(step == nc)
        def _():
            dgw_ref[...] = jnp.sum(acc_ref[...], axis=0, keepdims=True)
            dcw_ref[...] = jnp.sum(dcw_acc[...], axis=1)
            dcb_ref[...] = jnp.sum(dcb_acc[...], axis=0, keepdims=True)

    def one_group(live, stage, g, k, x_ref, b_ref, c_ref, dtb_ref, acsb_ref, dtr_ref, acsr_ref, alc_ref, d_ref, hin_ref,
                  dy_ref, dxbc_s, ddt_s, dal_ref, dd_ref, dh_ref):
        cols = slice(k * GP, (k + 1) * GP)
        ncols = slice(k * SSM_STATE, (k + 1) * SSM_STATE)
        bcols = slice(SSM_D_INNER + k * SSM_STATE, SSM_D_INNER + (k + 1) * SSM_STATE)
        ccols = slice(SSM_D_INNER + SSM_BC_DIM + k * SSM_STATE, SSM_D_INNER + SSM_BC_DIM + (k + 1) * SSM_STATE)

        xv = x_ref[:, cols].astype(F32)
        dyv = dy_ref[:, cols]
        bb = b_ref[:, ncols].astype(BF16)
        cb16 = c_ref[:, ncols].astype(BF16)
        dtb = dtb_ref[:, cols]
        acsb = acsb_ref[:, cols]
        dtr_v = dtr_ref[k]
        acs_r = acsr_ref[k]
        a_col = -jnp.exp(alc_ref[k])
        ri, cj = _chunk_iotas()
        head_of_lane = lax.shift_right_logical(lax.broadcasted_iota(jnp.int32, (SSM_HPG, GP), 1), HEAD_DIM_LOG2)
        ind_t = (head_of_lane == lax.broadcasted_iota(jnp.int32, (SSM_HPG, GP), 0)).astype(BF16)
        lastb = acsb[SSM_CHUNK - 1:SSM_CHUNK, :]
        ecb = jnp.exp(acsb)
        dteb = jnp.exp(lastb - acsb)
        xd = xv * dtb
        xw = xd * dteb
        cb = lax.dot_general(cb16, bb, NT_DIMS, preferred_element_type=F32)
        hin_v = hin_ref[0, k]
        dhn = dh_ref[g]
        h16 = hin_v.astype(BF16)
        dh16 = dhn.astype(BF16)
        ch = jnp.dot(cb16, h16, preferred_element_type=F32)
        bdh = jnp.dot(bb, dh16, preferred_element_type=F32)
        dym = _head_masked_rows(dyv, BF16)
        g_all = lax.dot_general(dym, xd.astype(BF16), NT_DIMS, preferred_element_type=F32)
        gl_sum = jnp.zeros((SSM_CHUNK, SSM_CHUNK), F32)
        ms, qs = [], []
        for r in range(SSM_HPG):
            decay = _head_decay(acsb, acs_r, r, ri, cj)
            gl = g_all[r * SSM_CHUNK:(r + 1) * SSM_CHUNK] * decay
            gl_sum = gl_sum + gl
            ms.append((cb * decay).astype(BF16))
            qs.append((gl * cb).astype(BF16))
        dxd = lax.dot_general(jnp.concatenate(ms, axis=0), dym, TN_DIMS, preferred_element_type=F32) + dteb * bdh
        cum = jnp.dot(jnp.concatenate(qs, axis=0), (ri < cj).astype(BF16), preferred_element_type=F32)
        sub4 = lax.broadcasted_iota(jnp.int32, (SSM_HPG, 1), 0)
        da = jnp.zeros((SSM_HPG, SSM_CHUNK), F32)
        for r in range(SSM_HPG):
            rect = jnp.sum(jnp.where(ri >= cj, cum[r * SSM_CHUNK:(r + 1) * SSM_CHUNK], 0.0), axis=0, keepdims=True)
            da = da + jnp.where(sub4 == r, rect, 0.0)
        z2 = xw * bdh
        sub8 = lax.broadcasted_iota(jnp.int32, (8, 1), 0)
        col_sums = (jnp.where(sub8 == 0, jnp.sum(z2, axis=0, keepdims=True), 0.0)
                    + jnp.where(sub8 == 1, jnp.sum(dhn * hin_v, axis=0, keepdims=True), 0.0)
                    + jnp.where(sub8 == 2, jnp.sum(dyv * xv, axis=0, keepdims=True), 0.0))
        wv = ecb * dyv
        summands = jnp.concatenate([wv * ch - z2, dxd * xv, col_sums], axis=0)
        sums = lax.dot_general(ind_t, summands.astype(BF16), NT_DIMS, preferred_element_type=F32)
        per_pos = sums[:, :2 * SSM_CHUNK]
        totals = sums[:, 2 * SSM_CHUNK:]
        e_last = totals[:, 0:1] + jnp.exp(acs_r[:, SSM_CHUNK - 1:SSM_CHUNK]) * totals[:, 1:2]
        da = (da + e_last + jnp.dot(per_pos[:, :SSM_CHUNK], (ri >= cj).astype(F32), preferred_element_type=F32,
                                    precision=lax.Precision.HIGHEST))
        ddt_s[stage, k] = a_col * da + per_pos[:, SSM_CHUNK:]
        dal_ref[g] += live * (a_col * jnp.sum(da * dtr_v, axis=1, keepdims=True))
        dd_ref[g] += live * totals[:, 2:3]
        dxbc_s[stage, :, cols] = dxd * dtb + d_ref[k] * dyv
        w16 = wv.astype(BF16)
        xw16 = xw.astype(BF16)
        gl16 = gl_sum.astype(BF16)
        dxbc_s[stage, :, ccols] = (jnp.dot(gl16, bb, preferred_element_type=F32)
                                   + lax.dot_general(w16, h16, NT_DIMS, preferred_element_type=F32))
        dxbc_s[stage, :, bcols] = (lax.dot_general(gl16, cb16, TN_DIMS, preferred_element_type=F32)
                                   + lax.dot_general(xw16, dh16, NT_DIMS, preferred_element_type=F32))
        dh_ref[g] = dhn * jnp.exp(lastb) + lax.dot_general(cb16, w16, TN_DIMS, preferred_element_type=F32)

    def scan_at(s):
        return nc - 1 - jnp.minimum(s, nc - 1)

    def conv_at(s):
        return nc - 1 - jnp.maximum(s - 1, 0)

    small = pl.BlockSpec((SSM_GROUPS, SSM_HPG, 1), lambda s: (0, 0, 0))
    lanes = pl.BlockSpec((SSM_CHUNK, SSM_D_INNER), lambda s: (scan_at(s), 0))
    rows = pl.BlockSpec((SSM_GROUPS, SSM_HPG, SSM_CHUNK), lambda s: (0, 0, scan_at(s)))
    vec = pl.BlockSpec((1, SSM_D_INNER), lambda s: (0, 0))
    return pl.pallas_call(
        body, grid=(nc + 1,),
        in_specs=[lanes,
                  pl.BlockSpec((SSM_CHUNK, SSM_BC_DIM), lambda s: (scan_at(s), B_BLOCK0 // GPS)),
                  pl.BlockSpec((SSM_CHUNK, SSM_BC_DIM), lambda s: (scan_at(s), C_BLOCK0 // GPS)),
                  lanes, lanes, rows, rows,
                  pl.BlockSpec((SSM_GROUPS, SSM_HPG, 1), lambda s: (0, 0, 0)),
                  pl.BlockSpec((SSM_GROUPS, 1, GP), lambda s: (0, 0, 0)),
                  pl.BlockSpec((1, SSM_GROUPS, SSM_STATE, GP), lambda s: (scan_at(s), 0, 0, 0)),
                  lanes, lanes, lanes, vec,
                  pl.BlockSpec((SSM_CHUNK, SSM_CONV_DIM), lambda s: (conv_at(s), 0)),
                  pl.BlockSpec((SSM_CHUNK, SSM_D_INNER), lambda s: (conv_at(s), 1)),
                  pl.BlockSpec((SSM_CHUNK, SSM_D_INNER), lambda s: (conv_at(s), 2)),
                  pl.BlockSpec((SSM_CONV, SSM_CONV_DIM), lambda s: (0, 0))],
        out_specs=[pl.BlockSpec((SSM_CHUNK, SSM_D_INNER + SSM_CONV_DIM), lambda s: (conv_at(s), 0)),
                   pl.BlockSpec((SSM_GROUPS, SSM_HPG, SSM_CHUNK), lambda s: (0, 0, conv_at(s))),
                   small, small, vec,
                   pl.BlockSpec((SSM_CONV, SSM_CONV_DIM), lambda s: (0, 0)),
                   pl.BlockSpec((1, SSM_CONV_DIM), lambda s: (0, 0))],
        out_shape=[jax.ShapeDtypeStruct((l, SSM_IN_PAD), BF16), jax.ShapeDtypeStruct((SSM_GROUPS, SSM_HPG, l), F32),
                   jax.ShapeDtypeStruct((SSM_GROUPS, SSM_HPG, 1), F32),
                   jax.ShapeDtypeStruct((SSM_GROUPS, SSM_HPG, 1), F32),
                   jax.ShapeDtypeStruct((1, SSM_D_INNER), F32),
                   jax.ShapeDtypeStruct((SSM_CONV, SSM_CONV_DIM), F32), jax.ShapeDtypeStruct((1, SSM_CONV_DIM), F32)],
        scratch_shapes=[pltpu.VMEM((SSM_GROUPS, SSM_STATE, GP), F32), pltpu.VMEM((SSM_CHUNK, SSM_D_INNER), F32),
                        pltpu.VMEM((8, SSM_D_INNER), F32),
                        pltpu.VMEM((2, SSM_CHUNK, SSM_CONV_DIM), F32), pltpu.VMEM((2, SSM_CHUNK, SSM_D_INNER), BF16),
                        pltpu.VMEM((2, SSM_GROUPS, SSM_HPG, SSM_CHUNK), F32), pltpu.VMEM((HALO, SSM_CONV_DIM), F32),
                        pltpu.VMEM((SSM_CHUNK + HALO, SSM_CONV_DIM), F32),
                        pltpu.VMEM((SSM_CONV, 8, SSM_CONV_DIM), F32), pltpu.VMEM((8, SSM_CONV_DIM), F32)],
        compiler_params=_params("arbitrary"), name=name,
    )(xbc, xbc, xbc, dtb, acsb, dtr, acs_r, a_log.reshape(SSM_GROUPS, SSM_HPG, 1), d_lanes, hin, dact, y, proj,
      gate_w.reshape(1, SSM_D_INNER), pre, proj, proj, cw)


LANES = 128
ROPE_Q_CHUNKS = ATT_WIDTH // LANES
ROPE_K_CHUNKS = ATT_KV_WIDTH // LANES


def _rope_tables(positions):
    inv = ROPE_THETA ** (-jnp.arange(0, ROPE_DIM, 2, dtype=F32) / ROPE_DIM)
    ang = positions.astype(F32)[:, None] * inv
    cos, sin = jnp.cos(ang), jnp.sin(ang)
    l = positions.shape[0]
    rest = ATT_HEAD_DIM - ROPE_DIM
    ones, zeros = jnp.ones((l, rest), F32), jnp.zeros((l, rest), F32)
    z8 = jnp.zeros((l, ROPE_HALF), F32)
    cos_f = jnp.concatenate([cos, cos, ones], axis=1)
    sin_a = jnp.concatenate([-sin, z8, zeros], axis=1)
    sin_b = jnp.concatenate([z8, sin, zeros], axis=1)
    reps = LANES // ATT_HEAD_DIM
    return tuple(jnp.tile(t, (1, reps)) for t in (cos_f, sin_a, sin_b))


ATT_QKV4 = 3 * ATT_WIDTH


def _both_halves(chunk):
    lane = lax.broadcasted_iota(jnp.int32, (1, LANES), 1)
    swapped = pltpu.roll(chunk, ATT_HEAD_DIM, 1)
    return jnp.where(lane < ATT_HEAD_DIM, chunk, swapped), jnp.where(lane < ATT_HEAD_DIM, swapped, chunk)


def _rope_bwd(dq, dk4, dv4, dgate, tables, name):
    l = dq.shape[0]
    tl = _pick(l, (256, 128))

    def body(dq_ref, dk_ref, dv_ref, dg_ref, c_ref, sa_ref, sb_ref, o_ref):
        cos_f, sin_a, sin_b = c_ref[...], sa_ref[...], sb_ref[...]
        lane = lax.broadcasted_iota(jnp.int32, (1, LANES), 1)

        def unrope(t):
            return t * cos_f + pltpu.roll(t * sin_a, ROPE_HALF, 1) + pltpu.roll(t * sin_b, LANES - ROPE_HALF, 1)

        def head_total(ref, kvh):
            base = kvh * ATT_GQA * ATT_HEAD_DIM
            s = ref[:, base:base + LANES] + ref[:, base + LANES:base + 2 * LANES]
            return s + pltpu.roll(s, ATT_HEAD_DIM, 1)

        for k in range(ROPE_Q_CHUNKS):
            sl = slice(k * LANES, (k + 1) * LANES)
            o_ref[:, sl] = unrope(dq_ref[:, sl] * Q_SCALE).astype(o_ref.dtype)
        for k in range(ROPE_K_CHUNKS):
            dk = jnp.where(lane < ATT_HEAD_DIM, head_total(dk_ref, 2 * k), head_total(dk_ref, 2 * k + 1))
            dv = jnp.where(lane < ATT_HEAD_DIM, head_total(dv_ref, 2 * k), head_total(dv_ref, 2 * k + 1))
            o_ref[:, ATT_WIDTH + k * LANES:ATT_WIDTH + (k + 1) * LANES] = unrope(dk).astype(o_ref.dtype)
            at = ATT_WIDTH + ATT_KV_WIDTH + k * LANES
            o_ref[:, at:at + LANES] = dv.astype(o_ref.dtype)
        o_ref[:, ATT_QKV:ATT_IN_DIM] = dg_ref[...].astype(o_ref.dtype)

    tab = pl.BlockSpec((tl, LANES), lambda i: (i, 0))
    wide = pl.BlockSpec((tl, ATT_WIDTH), lambda i: (i, 0))
    return pl.pallas_call(
        body, grid=(l // tl,), in_specs=[wide, wide, wide, wide, tab, tab, tab],
        out_specs=pl.BlockSpec((tl, ATT_IN_DIM), lambda i: (i, 0)),
        out_shape=jax.ShapeDtypeStruct((l, ATT_IN_DIM), BF16), compiler_params=_params("parallel"), name=name,
    )(dq, dk4, dv4, dgate, *tables)


GATE_HALF = ATT_WIDTH // 2
GATE_COL_BLOCK = ATT_QKV // GATE_HALF


ATT_STACK = ATT_GQA * ATT_BLOCK
BLOCK_LOG2 = ATT_BLOCK.bit_length() - 1


def _stack_masks(n):
    ri = lax.broadcasted_iota(jnp.int32, (ATT_STACK, ATT_BLOCK), 0) & (ATT_BLOCK - 1)
    cj = lax.broadcasted_iota(jnp.int32, (ATT_STACK, ATT_BLOCK), 1)
    return (cj > ri) & (n > 0), cj <= ri


def _stack_sinks(sink_ref, kvh):
    blk = lax.shift_right_logical(lax.broadcasted_iota(jnp.int32, (ATT_STACK, 1), 0), BLOCK_LOG2)
    col = jnp.zeros((ATT_STACK, 1), F32)
    for r in range(ATT_GQA):
        col = jnp.where(blk == r, sink_ref[kvh * ATT_GQA + r], col)
    return col


def _stack_fold(stack):
    head_of_lane = lax.shift_right_logical(lax.broadcasted_iota(jnp.int32, (1, GP), 1), HEAD_DIM_LOG2)
    out = jnp.zeros((ATT_BLOCK, GP), F32)
    for r in range(ATT_GQA):
        out = jnp.where(head_of_lane == r, stack[r * ATT_BLOCK:(r + 1) * ATT_BLOCK], out)
    return out


def _attn_fwd(proj, tables, sinks, name):
    l = proj.shape[0]
    nb = l // ATT_BLOCK
    ring = 3

    def body(sink_ref, p_ref, c_ref, sa_ref, sb_ref, g0_ref, g1_ref, og_ref, o_ref, lse_ref, qkv_ref, ring_ref):
        s = pl.program_id(0)

        @pl.when(s == 0)
        def _():
            ring_ref[...] = jnp.zeros_like(ring_ref)

        slot = lax.rem(s, ring)
        cos_f, sin_a, sin_b = c_ref[...], sa_ref[...], sb_ref[...]

        def rope(t):
            return t * cos_f + pltpu.roll(t, LANES - ROPE_HALF, 1) * sin_a + pltpu.roll(t, ROPE_HALF, 1) * sin_b

        def put(dst, value):
            qkv_ref[:, dst:dst + LANES] = value
            ring_ref[slot, :, dst:dst + LANES] = value

        for k in range(ROPE_Q_CHUNKS):
            put(k * LANES, (rope(p_ref[:, k * LANES:(k + 1) * LANES]) * Q_SCALE).astype(BF16))
        for part in range(2):
            for k in range(ROPE_K_CHUNKS):
                src = ATT_WIDTH + part * ATT_KV_WIDTH + k * LANES
                t = p_ref[:, src:src + LANES]
                if part == 0:
                    t = rope(t)
                for head, dup in enumerate(_both_halves(t.astype(BF16))):
                    dst = (1 + part) * ATT_WIDTH + (2 * k + head) * ATT_GQA * ATT_HEAD_DIM
                    put(dst, dup)
                    put(dst + LANES, dup)

        n = s - 1
        cur = lax.rem(s + ring - 1, ring)
        prv = lax.rem(s + ring - 2, ring)
        q_ref = ring_ref.at[cur, :, 0:ATT_WIDTH]
        kc_ref = ring_ref.at[cur, :, ATT_WIDTH:2 * ATT_WIDTH]
        vc_ref = ring_ref.at[cur, :, 2 * ATT_WIDTH:3 * ATT_WIDTH]
        kp_ref = ring_ref.at[prv, :, ATT_WIDTH:2 * ATT_WIDTH]
        vp_ref = ring_ref.at[prv, :, 2 * ATT_WIDTH:3 * ATT_WIDTH]
        mask_p, mask_c = _stack_masks(n)
        ones = jnp.ones((ATT_BLOCK, LANES), BF16)
        for kvh in range(ATT_KV_HEADS):
            cols = slice(kvh * GP, (kvh + 1) * GP)
            q_stack = _head_masked_rows(q_ref[:, cols], BF16)
            sp = jnp.where(mask_p, lax.dot_general(q_stack, kp_ref[:, cols], NT_DIMS, preferred_element_type=F32), NEG_INF)
            sc = jnp.where(mask_c, lax.dot_general(q_stack, kc_ref[:, cols], NT_DIMS, preferred_element_type=F32), NEG_INF)
            sink = _stack_sinks(sink_ref, kvh)
            m = jnp.maximum(jnp.max(jnp.maximum(sp, sc), axis=1, keepdims=True), sink)
            pp = jnp.exp(sp - m).astype(BF16)
            pc = jnp.exp(sc - m).astype(BF16)
            acc = (jnp.dot(pp, jnp.concatenate([vp_ref[:, cols], ones], axis=1), preferred_element_type=F32)
                   + jnp.dot(pc, jnp.concatenate([vc_ref[:, cols], ones], axis=1), preferred_element_type=F32))
            den = acc[:, GP:] + jnp.exp(sink - m)
            inv = 1.0 / den
            o_ref[:, cols] = _stack_fold(acc[:, :GP] * jnp.concatenate([inv, inv], axis=1))
            lse = m + jnp.log(den)
            lse_ref[:, cols] = _stack_fold(jnp.concatenate([lse, lse], axis=1))
        for half, g_ref in enumerate((g0_ref, g1_ref)):
            sl = slice(half * GATE_HALF, (half + 1) * GATE_HALF)
            gate = g_ref[...]
            og_ref[:, sl] = (o_ref[:, sl] * (gate * _sigmoid(gate))).astype(og_ref.dtype)

    def rope_at(s):
        return jnp.minimum(s, nb - 1)

    def attend_at(s):
        return jnp.maximum(s - 1, 0)

    wide = pl.BlockSpec((ATT_BLOCK, ATT_WIDTH), lambda s: (attend_at(s), 0))
    tab = pl.BlockSpec((ATT_BLOCK, LANES), lambda s: (rope_at(s), 0))
    return pl.pallas_call(
        body, grid=(nb + 1,),
        in_specs=[pl.BlockSpec(memory_space=pltpu.SMEM),
                  pl.BlockSpec((ATT_BLOCK, ATT_IN_DIM), lambda s: (rope_at(s), 0)), tab, tab, tab,
                  pl.BlockSpec((ATT_BLOCK, GATE_HALF), lambda s: (attend_at(s), GATE_COL_BLOCK)),
                  pl.BlockSpec((ATT_BLOCK, GATE_HALF), lambda s: (attend_at(s), GATE_COL_BLOCK + 1))],
        out_specs=[wide, wide, wide, pl.BlockSpec((ATT_BLOCK, ATT_QKV4), lambda s: (rope_at(s), 0))],
        out_shape=[jax.ShapeDtypeStruct((l, ATT_WIDTH), BF16), jax.ShapeDtypeStruct((l, ATT_WIDTH), F32),
                   jax.ShapeDtypeStruct((l, ATT_WIDTH), F32), jax.ShapeDtypeStruct((l, ATT_QKV4), BF16)],
        scratch_shapes=[pltpu.VMEM((ring, ATT_BLOCK, ATT_QKV4), BF16)],
        compiler_params=_params("arbitrary"), name=name,
    )(sinks, proj, *tables, proj, proj)


def _attn_bwd(qkv, proj, sinks, o, lse, dog, name, ride=()):
    l = qkv.shape[0]
    nb = l // ATT_BLOCK
    n_ride = len(ride)

    def body(*refs):
        sink_ref, q_ref, kp_ref, kc_ref, vp_ref, vc_ref, g0_ref, g1_ref, o_ref, lse_ref, dog_ref = refs[:11]
        ride_in = refs[11:11 + n_ride]
        dq_ref, dk_ref, dv_ref, dg_ref, ds_ref = refs[11 + n_ride:16 + n_ride]
        ride_out = refs[16 + n_ride:16 + 2 * n_ride]
        ck_ref, cv_ref, do_ref = refs[16 + 2 * n_ride:19 + 2 * n_ride]
        ride_sems = refs[19 + 2 * n_ride:]
        n = pl.program_id(0)

        @pl.when(n == 0)
        def _():
            ds_ref[...] = jnp.zeros_like(ds_ref)
            ck_ref[...] = jnp.zeros_like(ck_ref)
            cv_ref[...] = jnp.zeros_like(cv_ref)
            if n_ride:
                _scatter_between_chips(ride_in, ride_out, *ride_sems, wait=False)

        @pl.when(n == nb)
        def _():
            dk_ref[...] = ck_ref[...]
            dv_ref[...] = cv_ref[...]
            if n_ride:
                _scatter_between_chips(ride_in, ride_out, *ride_sems, wait=True)

        @pl.when(n < nb)
        def _():
            mask_p, mask_c = _stack_masks(n)
            lane = lax.broadcasted_iota(jnp.int32, (1, ATT_Q_HEADS), 1)
            for half, g_ref in enumerate((g0_ref, g1_ref)):
                sl = slice(half * GATE_HALF, (half + 1) * GATE_HALF)
                gate = g_ref[...]
                s = _sigmoid(gate)
                dogv = dog_ref[:, sl]
                do_ref[:, sl] = dogv * (gate * s)
                dg_ref[:, sl] = dogv * o_ref[:, sl] * (s * (1.0 + gate * (1.0 - s)))
            ds_acc = jnp.zeros((1, ATT_Q_HEADS), F32)
            for kvh in range(ATT_KV_HEADS):
                cols = slice(kvh * GP, (kvh + 1) * GP)
                kp, kc, vp, vc = kp_ref[:, cols], kc_ref[:, cols], vp_ref[:, cols], vc_ref[:, cols]
                q_stack = _head_masked_rows(q_ref[:, cols], BF16)
                do_g = do_ref[:, cols]
                do_stack = _head_masked_rows(do_g, BF16)
                lse_g = lse_ref[:, cols]
                lse_stack = jnp.concatenate(
                    [_both_halves(lse_g[:, (r // 2) * LANES:(r // 2 + 1) * LANES])[r % 2] for r in range(ATT_GQA)], axis=0)
                pp = jnp.exp(jnp.where(
                    mask_p, lax.dot_general(q_stack, kp, NT_DIMS, preferred_element_type=F32) - lse_stack, NEG_INF))
                pc = jnp.exp(jnp.where(
                    mask_c, lax.dot_general(q_stack, kc, NT_DIMS, preferred_element_type=F32) - lse_stack, NEG_INF))
                dpp = lax.dot_general(do_stack, vp, NT_DIMS, preferred_element_type=F32)
                dpc = lax.dot_general(do_stack, vc, NT_DIMS, preferred_element_type=F32)
                delta = jnp.sum(pp * dpp + pc * dpc, axis=1, keepdims=True)
                dsp = (pp * (dpp - delta)).astype(BF16)
                dsc = (pc * (dpc - delta)).astype(BF16)
                dq_ref[:, cols] = _stack_fold(jnp.dot(dsp, kp, preferred_element_type=F32)
                                              + jnp.dot(dsc, kc, preferred_element_type=F32))
                dk_ref[:, cols] = ck_ref[:, cols] + lax.dot_general(dsp, q_stack, TN_DIMS, preferred_element_type=F32)
                dv_ref[:, cols] = cv_ref[:, cols] + lax.dot_general(pp.astype(BF16), do_stack, TN_DIMS,
                                                                    preferred_element_type=F32)
                ck_ref[:, cols] = lax.dot_general(dsc, q_stack, TN_DIMS, preferred_element_type=F32)
                cv_ref[:, cols] = lax.dot_general(pc.astype(BF16), do_stack, TN_DIMS, preferred_element_type=F32)
                t = jnp.exp(_stack_sinks(sink_ref, kvh) - lse_stack) * delta
                for r in range(ATT_GQA):
                    tot = jnp.sum(t[r * ATT_BLOCK:(r + 1) * ATT_BLOCK], axis=0, keepdims=True)
                    ds_acc = ds_acc - jnp.where(lane == kvh * ATT_GQA + r, tot[:, :ATT_Q_HEADS], 0.0)
            ds_ref[...] += ds_acc

    def cur(n):
        return jnp.minimum(n, nb - 1)

    def prev(n):
        return jnp.maximum(n - 1, 0)

    wide = pl.BlockSpec((ATT_BLOCK, ATT_WIDTH), lambda n: (cur(n), 0))
    late = pl.BlockSpec((ATT_BLOCK, ATT_WIDTH), lambda n: (prev(n), 0))
    return pl.pallas_call(
        body, grid=(nb + 1,),
        in_specs=[pl.BlockSpec(memory_space=pltpu.SMEM), wide,
                  pl.BlockSpec((ATT_BLOCK, ATT_WIDTH), lambda n: (prev(cur(n)), 1)),
                  pl.BlockSpec((ATT_BLOCK, ATT_WIDTH), lambda n: (cur(n), 1)),
                  pl.BlockSpec((ATT_BLOCK, ATT_WIDTH), lambda n: (prev(cur(n)), 2)),
                  pl.BlockSpec((ATT_BLOCK, ATT_WIDTH), lambda n: (cur(n), 2)),
                  pl.BlockSpec((ATT_BLOCK, GATE_HALF), lambda n: (cur(n), GATE_COL_BLOCK)),
                  pl.BlockSpec((ATT_BLOCK, GATE_HALF), lambda n: (cur(n), GATE_COL_BLOCK + 1)),
                  wide, wide, wide] + [ANY] * n_ride,
        out_specs=[wide, late, late, wide, pl.BlockSpec((1, ATT_Q_HEADS), lambda n: (0, 0))] + [ANY] * n_ride,
        out_shape=[jax.ShapeDtypeStruct((l, ATT_WIDTH), F32), jax.ShapeDtypeStruct((l, ATT_WIDTH), F32),
                   jax.ShapeDtypeStruct((l, ATT_WIDTH), F32), jax.ShapeDtypeStruct((l, ATT_WIDTH), F32),
                   jax.ShapeDtypeStruct((1, ATT_Q_HEADS), F32)] + _scatter_shapes(ride),
        scratch_shapes=[pltpu.VMEM((ATT_BLOCK, ATT_WIDTH), F32), pltpu.VMEM((ATT_BLOCK, ATT_WIDTH), F32),
                        pltpu.VMEM((ATT_BLOCK, ATT_WIDTH), F32)] + (_gather_sems(n_ride) if n_ride else []),
        compiler_params=_params("arbitrary"), name=name,
    )(sinks, qkv, qkv, qkv, qkv, qkv, proj, proj, o, lse, dog, *ride)


def _local_step(x, positions, pre_norm, post_norm, conv_b, dt_bias, a_log, d_skip, gate_norm, sinks, target,
                first_in, in_proj_with_first_pair, scan_with_second_pair, attn_bwd_with_second_pair_grads,
                in_dx_with_first_pair_grads):
    tables = _rope_tables(positions)
    dt_bias_pad = jnp.pad(dt_bias, ((0, 0), (0, SSM_DT_PAD - SSM_HEADS)))
    d_lanes = jnp.repeat(d_skip, SSM_HEAD_DIM, axis=1).reshape(-1, SSM_GROUPS, 1, GP)
    a_log_pad = jnp.pad(a_log, ((0, 0), (0, SSM_DT_PAD - SSM_HEADS)))
    pairs = [first_in, None]
    saved = []
    cur = x
    h = _rmsnorm_fwd(cur, pre_norm[0], "prenorm_fwd_0")
    for i in range(DEPTH):
        j = i // 2
        if i % 2 == 0:
            in_proj = functools.partial(_matmul, h, pairs[j]["ssm_w_in"], "nn", F32, f"ssm_in_{i}")
            if i == 0:
                proj, rest = in_proj_with_first_pair(in_proj)
                pairs[0] = {**first_in, **rest}
            else:
                proj = in_proj()
            scan = functools.partial(_ssd_fwd, proj, pairs[j]["ssm_conv_w"], conv_b[j], dt_bias_pad[j:j + 1],
                                     a_log_pad[j:j + 1], d_lanes[j], gate_norm[j], f"ssd_fwd_{i}")
            if i == 0:
                *scanned, pairs[1] = scan_with_second_pair(scan)
            else:
                scanned = scan()
            y, act, hin, pre, xbc, dtb, acsb, dtr, acs_r = scanned
            w_ssm_in = [p["ssm_w_in"] for p in pairs]
            w_ssm_out = [p["ssm_w_out"] for p in pairs]
            w_att_in = [p["att_w_in"] for p in pairs]
            w_att_out = [p["att_w_out"] for p in pairs]
            conv_w = [p["ssm_conv_w"] for p in pairs]
            ymix = _matmul(act, w_ssm_out[j], "nn", F32, f"ssm_out_{i}")
            saved.append(dict(x=cur, h=h, proj=proj, pre=pre, xbc=xbc, dtb=dtb, acsb=acsb, dtr=dtr, acs_r=acs_r, y=y,
                              hin=hin, act=act, ymix=ymix))
        else:
            proj = _matmul(h, w_att_in[j], "nn", F32, f"att_in_{i}")
            act, o, lse, qkv = _attn_fwd(proj, tables, sinks[j], f"attn_fwd_{i}")
            ymix = _matmul(act, w_att_out[j], "nn", F32, f"att_out_{i}")
            saved.append(dict(x=cur, h=h, proj=proj, qkv=qkv, o=o, lse=lse, act=act, ymix=ymix))
        if i + 1 < DEPTH:
            cur, h = _post_fwd(cur, ymix, post_norm[i], pre_norm[i + 1], f"post_fwd_{i}")

    gr = {k: [None] * 2 for k in ("ssm_w_in", "ssm_conv_w", "ssm_conv_b", "ssm_dt_bias", "ssm_a_log", "ssm_d",
                                  "ssm_gate_norm", "ssm_w_out", "att_w_in", "att_sinks", "att_w_out")}
    gr["pre_norm"] = [None] * DEPTH
    gr["post_norm"] = [None] * DEPTH
    last = DEPTH - 1
    g, dymix, loss_lanes, gr["post_norm"][last] = _post_loss(cur, ymix, post_norm[last], target, "post_loss")
    for i in reversed(range(DEPTH)):
        j = i // 2
        s = saved[i]
        if i % 2 == 0:
            dact = _matmul(dymix, w_ssm_out[j], "nt", F32, f"ssm_out_dx_{i}")
            gr["ssm_w_out"][j] = _matmul(s["act"], dymix, "tn", F32, f"ssm_out_dw_{i}")
            dproj, ddt8, dal, dd, gr["ssm_gate_norm"][j], gr["ssm_conv_w"][j], dcb = _ssd_bwd(
                s["xbc"], s["pre"], conv_w[j], s["dtb"], s["acsb"], s["dtr"], s["acs_r"], a_log[j], d_lanes[j], s["hin"],
                dact, s["y"], s["proj"], gate_norm[j], f"ssd_bwd_{i}")
            gr["ssm_conv_b"][j] = dcb[0]
            gr["ssm_a_log"][j] = dal.reshape(SSM_HEADS)
            gr["ssm_d"][j] = dd.reshape(SSM_HEADS)
            l = x.shape[0]
            ddt = jnp.pad(jnp.transpose(ddt8, (2, 0, 1)).reshape(l, SSM_HEADS), ((0, 0), (0, SSM_DT_PAD - SSM_HEADS)))
            dproj, dbias = _dt_bwd(ddt, s["proj"], dt_bias_pad[j:j + 1], dproj, f"dt_bwd_{i}")
            gr["ssm_dt_bias"][j] = dbias[0, :SSM_HEADS]
            w_in, key = w_ssm_in[j], "ssm_w_in"
        else:
            dog = _matmul(dymix, w_att_out[j], "nt", F32, f"att_out_dx_{i}")
            gr["att_w_out"][j] = _matmul(s["act"], dymix, "tn", F32, f"att_out_dw_{i}")
            attn_bwd = functools.partial(_attn_bwd, s["qkv"], s["proj"], sinks[j], s["o"], s["lse"], dog, f"attn_bwd_{i}")
            if i == 1:
                (dq, dk, dv, dgate, dsk), second_pair_reduced = attn_bwd_with_second_pair_grads(
                    attn_bwd, {k: gr[k][1] for k in BIG})
            else:
                dq, dk, dv, dgate, dsk = attn_bwd()
            gr["att_sinks"][j] = dsk[0]
            dproj = _rope_bwd(dq, dk, dv, dgate, tables, f"rope_bwd_{i}")
            w_in, key = w_att_in[j], "att_w_in"
        gr[key][j] = _matmul(s["h"], dproj, "tn", F32, f"in_dw_{i}")
        in_dx = functools.partial(_matmul, dproj, w_in, "nt", F32, f"in_dx_{i}")
        if i == 0:
            dh, first_pair_reduced = in_dx_with_first_pair_grads(in_dx, {k: gr[k][0] for k in BIG})
        else:
            dh = in_dx()
        if i > 0:
            g, dymix, gr["pre_norm"][i], gr["post_norm"][i - 1] = _norm_bwd_chain(
                dh, s["x"], pre_norm[i], g, saved[i - 1]["ymix"], post_norm[i - 1], f"norm_bwd_{i}")
        else:
            g, gr["pre_norm"][i] = _rmsnorm_bwd(dh, s["x"], pre_norm[i], g, f"prenorm_bwd_{i}")
    grads = {k: jnp.stack([v.reshape(v.shape[-1]) if k in ("pre_norm", "post_norm", "ssm_gate_norm") else v for v in vs])
             for k, vs in gr.items() if k not in BIG}
    return loss_lanes, g, grads, first_pair_reduced, second_pair_reduced


N_CHIPS = 4
N_DEV = 8
MESH = pl.DeviceIdType.MESH
ANY = pl.BlockSpec(memory_space=pl.ANY)


def _place():
    x, y, c = lax.axis_index("x"), lax.axis_index("y"), lax.axis_index("c")
    return x, y, c, 2 * x + y


def _gather_sems(n):
    return [pltpu.SemaphoreType.DMA((n, N_CHIPS)), pltpu.SemaphoreType.DMA((n, N_CHIPS)), pltpu.SemaphoreType.DMA((n,))]


def _gather_between_chips(ins, outs, send_sems, recv_sems, local_sems, wait):
    n = len(ins)
    _, _, c, s = _place()
    local = [pltpu.make_async_copy(ins[w], outs[w].at[s], local_sems.at[w]) for w in range(n)]

    def remote(w, t):
        return pltpu.make_async_remote_copy(
            src_ref=ins[w].at[c], dst_ref=outs[w].at[s, c], send_sem=send_sems.at[w, t],
            recv_sem=recv_sems.at[w, s], device_id=(t // 2, t % 2, c), device_id_type=MESH)

    def arrival(w, t):
        return pltpu.make_async_remote_copy(
            src_ref=ins[w].at[c], dst_ref=outs[w].at[t, c], send_sem=send_sems.at[w, t],
            recv_sem=recv_sems.at[w, t], device_id=(t // 2, t % 2, c), device_id_type=MESH)

    if not wait:
        for cp in local:
            cp.start()
    for t in range(N_CHIPS):
        @pl.when(s != t)
        def _():
            for w in range(n):
                if wait:
                    remote(w, t).wait_send()
                    arrival(w, t).wait_recv()
                else:
                    remote(w, t).start()
    if wait:
        for cp in local:
            cp.wait()


def _pair_handoff(bufs, name):
    n = len(bufs)

    def body(*refs):
        outs = refs[n:2 * n]
        send_sems, recv_sems = refs[2 * n:]
        x, y, c, s = _place()

        def handed_on(w, t):
            return pltpu.make_async_remote_copy(
                src_ref=outs[w].at[t, c], dst_ref=outs[w].at[t, c], send_sem=send_sems.at[w, t],
                recv_sem=recv_sems.at[w, t], device_id=(x, y, 1 - c), device_id_type=MESH)

        def handed_in(w, t):
            return pltpu.make_async_remote_copy(
                src_ref=outs[w].at[t, 1 - c], dst_ref=outs[w].at[t, 1 - c], send_sem=send_sems.at[w, t],
                recv_sem=recv_sems.at[w, t], device_id=(x, y, 1 - c), device_id_type=MESH)

        for t in range(N_CHIPS):
            @pl.when(s != t)
            def _():
                for w in range(n):
                    handed_on(w, t).start()
        for t in range(N_CHIPS):
            @pl.when(s != t)
            def _():
                for w in range(n):
                    handed_on(w, t).wait_send()
                    handed_in(w, t).wait_recv()

    return pl.pallas_call(
        body, in_specs=[ANY] * n, out_specs=[ANY] * n,
        out_shape=[jax.ShapeDtypeStruct(a.shape, a.dtype) for a in bufs],
        scratch_shapes=[pltpu.SemaphoreType.DMA((n, N_CHIPS)), pltpu.SemaphoreType.DMA((n, N_CHIPS))],
        input_output_aliases={w: w for w in range(n)}, name=name,
    )(*bufs)


def _chip_gather(shards, name):
    n = len(shards)

    def body(*refs):
        ins, outs = refs[:n], refs[n:2 * n]
        _gather_between_chips(ins, outs, *refs[2 * n:], wait=False)
        _gather_between_chips(ins, outs, *refs[2 * n:], wait=True)

    bufs = pl.pallas_call(
        body, in_specs=[ANY] * n, out_specs=[ANY] * n,
        out_shape=[jax.ShapeDtypeStruct((N_CHIPS,) + a.shape, a.dtype) for a in shards],
        scratch_shapes=_gather_sems(n), name=name,
    )(*shards)
    return _pair_handoff(bufs, name + "_handoff")


def _pair_swap(parts, name):
    n = len(parts)

    def body(*refs):
        ins, outs = refs[:n], refs[n:2 * n]
        send_sems, recv_sems = refs[2 * n:]
        x, y, c, _ = _place()
        cps = [pltpu.make_async_remote_copy(
            src_ref=ins[w].at[1 - c], dst_ref=outs[w], send_sem=send_sems.at[w], recv_sem=recv_sems.at[w],
            device_id=(x, y, 1 - c), device_id_type=MESH) for w in range(n)]
        for cp in cps:
            cp.start()
        for cp in cps:
            cp.wait()

    return pl.pallas_call(
        body, in_specs=[ANY] * n, out_specs=[ANY] * n,
        out_shape=[jax.ShapeDtypeStruct(a.shape[1:], a.dtype) for a in parts],
        scratch_shapes=[pltpu.SemaphoreType.DMA((n,)), pltpu.SemaphoreType.DMA((n,))],
        name=name,
    )(*parts)


def _scatter_between_chips(ins, outs, send_sems, recv_sems, local_sems, wait):
    n = len(ins)
    _, _, c, s = _place()

    def block(w, t):
        rows = ins[w].shape[0] // N_CHIPS
        return ins[w].at[pl.ds(t * rows, rows)]

    local = [pltpu.make_async_copy(block(w, s), outs[w].at[s], local_sems.at[w]) for w in range(n)]

    def remote(w, t):
        return pltpu.make_async_remote_copy(
            src_ref=block(w, t), dst_ref=outs[w].at[s], send_sem=send_sems.at[w, t], recv_sem=recv_sems.at[w, s],
            device_id=(t // 2, t % 2, c), device_id_type=MESH)

    def arrival(w, t):
        return pltpu.make_async_remote_copy(
            src_ref=block(w, t), dst_ref=outs[w].at[t], send_sem=send_sems.at[w, t], recv_sem=recv_sems.at[w, t],
            device_id=(t // 2, t % 2, c), device_id_type=MESH)

    if not wait:
        for cp in local:
            cp.start()
    for t in range(N_CHIPS):
        @pl.when(s != t)
        def _():
            for w in range(n):
                if wait:
                    remote(w, t).wait_send()
                    arrival(w, t).wait_recv()
                else:
                    remote(w, t).start()
    if wait:
        for cp in local:
            cp.wait()


def _scatter_shapes(parts):
    return [jax.ShapeDtypeStruct((N_CHIPS, a.shape[0] // N_CHIPS, a.shape[1]), a.dtype) for a in parts]


def _pair_merge(parts, name):
    n = len(parts)

    def body(*refs):
        ins, outs = refs[:n], refs[n:2 * n]
        send_sems, recv_sems = refs[2 * n:]
        x, y, c, _ = _place()
        cps = [pltpu.make_async_remote_copy(
            src_ref=ins[w], dst_ref=outs[w], send_sem=send_sems.at[w], recv_sem=recv_sems.at[w],
            device_id=(x, y, 1 - c), device_id_type=MESH) for w in range(n)]
        for cp in cps:
            cp.start()
        for cp in cps:
            cp.wait()

    return pl.pallas_call(
        body, in_specs=[ANY] * n, out_specs=[ANY] * n,
        out_shape=[jax.ShapeDtypeStruct(a.shape, a.dtype) for a in parts],
        scratch_shapes=[pltpu.SemaphoreType.DMA((n,)), pltpu.SemaphoreType.DMA((n,))],
        name=name,
    )(*parts)


def _all_gather_small(a, name):
    def body(in_ref, out_ref, send_sems, recv_sems, local_sem):
        x, y, c, _ = _place()
        me = 4 * x + 2 * y + c
        local = pltpu.make_async_copy(in_ref, out_ref.at[me], local_sem)
        local.start()

        def remote(d):
            return pltpu.make_async_remote_copy(
                src_ref=in_ref, dst_ref=out_ref.at[me], send_sem=send_sems.at[d], recv_sem=recv_sems.at[me],
                device_id=(d // 4, (d // 2) % 2, d % 2), device_id_type=MESH)

        def arrival(d):
            return pltpu.make_async_remote_copy(
                src_ref=in_ref, dst_ref=out_ref.at[d], send_sem=send_sems.at[d], recv_sem=recv_sems.at[d],
                device_id=(d // 4, (d // 2) % 2, d % 2), device_id_type=MESH)

        for d in range(N_DEV):
            @pl.when(me != d)
            def _():
                remote(d).start()
        for d in range(N_DEV):
            @pl.when(me != d)
            def _():
                remote(d).wait_send()
                arrival(d).wait_recv()
        local.wait()

    return pl.pallas_call(
        body, in_specs=[ANY], out_specs=ANY, out_shape=jax.ShapeDtypeStruct((N_DEV,) + a.shape, a.dtype),
        scratch_shapes=[pltpu.SemaphoreType.DMA((N_DEV,)), pltpu.SemaphoreType.DMA((N_DEV,)), pltpu.SemaphoreType.DMA],
        name=name,
    )(a)


def _reduce_tile(rows):
    return _pick(rows, (256, 128, 16))


def _pair_add(full, other, layer, name):
    _, rows, cols = full.shape
    tr = _reduce_tile(rows)

    def body(layer_ref, a_ref, b_ref, o_ref):
        o_ref[...] = (a_ref[0] + b_ref[...]).astype(o_ref.dtype)

    return pl.pallas_call(
        body,
        grid_spec=pltpu.PrefetchScalarGridSpec(
            num_scalar_prefetch=1, grid=(rows // tr,),
            in_specs=[pl.BlockSpec((1, tr, cols), lambda i, lr: (lr[0], i, 0)), pl.BlockSpec((tr, cols), lambda i, lr: (i, 0))],
            out_specs=pl.BlockSpec((tr, cols), lambda i, lr: (i, 0))),
        out_shape=jax.ShapeDtypeStruct((rows, cols), BF16), compiler_params=_params("parallel"), name=name,
    )(layer, full, other)


def _sum_slots(a, name):
    n, rows, cols = a.shape
    tr = _reduce_tile(rows)

    def body(a_ref, o_ref):
        acc = a_ref[0].astype(F32)
        for k in range(1, n):
            acc = acc + a_ref[k].astype(F32)
        o_ref[...] = acc

    return pl.pallas_call(
        body, grid=(rows // tr,), in_specs=[pl.BlockSpec((n, tr, cols), lambda i: (0, i, 0))],
        out_specs=pl.BlockSpec((tr, cols), lambda i: (i, 0)),
        out_shape=jax.ShapeDtypeStruct((rows, cols), F32), compiler_params=_params("parallel"), name=name,
    )(a)


def _adamw(w, g, m, v, name):
    rows, cols = w.shape
    tr = _pick(rows, (256, 8))

    def body(w_ref, g_ref, m_ref, v_ref, d_ref, nm_ref, nv_ref):
        gv = g_ref[...]
        mn = ADAM_B1 * m_ref[...] + (1.0 - ADAM_B1) * gv
        vn = ADAM_B2 * v_ref[...] + (1.0 - ADAM_B2) * jnp.square(gv)
        m_hat = mn / (1.0 - ADAM_B1 ** ADAM_STEP)
        v_hat = vn / (1.0 - ADAM_B2 ** ADAM_STEP)
        d_ref[...] = -ADAM_LR * (m_hat / (jnp.sqrt(v_hat) + ADAM_EPS) + ADAM_WD * w_ref[...])
        nm_ref[...] = mn
        nv_ref[...] = vn

    blk = pl.BlockSpec((tr, cols), lambda i: (i, 0))
    return pl.pallas_call(
        body, grid=(rows // tr,), in_specs=[blk] * 4, out_specs=[blk] * 3,
        out_shape=[jax.ShapeDtypeStruct((rows, cols), F32)] * 3, compiler_params=_params("parallel"), name=name,
    )(w, g, m, v)


BIG = ("ssm_w_in", "ssm_w_out", "att_w_in", "att_w_out")
SHARDED = BIG + ("ssm_conv_w",)
SMALL = ("pre_norm", "post_norm", "ssm_conv_b", "ssm_dt_bias", "ssm_a_log", "ssm_d", "ssm_gate_norm", "att_sinks")
WEIGHTS = ("pre_norm", "post_norm", "ssm_w_in", "ssm_conv_w", "ssm_conv_b", "ssm_dt_bias", "ssm_a_log", "ssm_d",
           "ssm_gate_norm", "ssm_w_out", "att_w_in", "att_sinks", "att_w_out")


def _halves(a):
    return a.reshape(2, a.shape[0] // 2, a.shape[1])


def _layer_shards(j, ssm_w_in, ssm_w_out, att_w_in, att_w_out, ssm_conv_w):
    return [_halves(ssm_w_in[j].astype(BF16)), _halves(ssm_w_out[j].astype(BF16)), _halves(att_w_in[j].astype(BF16)),
            _halves(att_w_out[j].astype(BF16)), _halves(ssm_conv_w[j])]


SHARD_KEYS = ("ssm_w_in", "ssm_w_out", "att_w_in", "att_w_out", "ssm_conv_w")


def _whole_weights(keys, gathered):
    out = {}
    for k, g in zip(keys, gathered):
        g = g.reshape((N_CHIPS, 2 * g.shape[2], g.shape[3]))
        if k in ("ssm_w_out", "att_w_out"):
            out[k] = g.reshape(N_CHIPS * g.shape[1], g.shape[2])
        else:
            out[k] = jnp.transpose(g, (1, 0, 2)).reshape(g.shape[1], N_CHIPS * g.shape[2])
    if "ssm_w_in" in out:
        out["ssm_w_in"] = jnp.pad(out["ssm_w_in"], ((0, 0), (0, SSM_IN_PAD - SSM_IN_DIM)))
    return out


def _halves_by_chip(key, g):
    if key in ("ssm_w_out", "att_w_out"):
        rows = g.shape[0] // N_CHIPS
        blocks = g.reshape(N_CHIPS, 2, rows // 2, g.shape[1])
        return jnp.transpose(blocks, (1, 0, 2, 3)).reshape(2, N_CHIPS * (rows // 2), g.shape[1])
    cols = (SSM_IN_DIM if key == "ssm_w_in" else g.shape[1]) // N_CHIPS
    rows = g.shape[0]
    blocks = g[:, :N_CHIPS * cols].reshape(2, rows // 2, N_CHIPS, cols)
    return jnp.transpose(blocks, (0, 2, 1, 3)).reshape(2, N_CHIPS * (rows // 2), cols)


def _pack_small(tree, keys):
    flat = jnp.concatenate([tree[k].reshape(-1) for k in keys])
    rows = -(-flat.shape[0] // (8 * LANES)) * 8
    return jnp.pad(flat, (0, rows * LANES - flat.shape[0])).reshape(rows, LANES)


def _unpack_small(packed, shapes, keys):
    flat = packed.reshape(-1)
    out, at = {}, 0
    for k in keys:
        n = 1
        for dim in shapes[k]:
            n *= dim
        out[k] = flat[at:at + n].reshape(shapes[k])
        at += n
    return out


def kernel(x, positions, pre_norm, post_norm, ssm_w_in, ssm_conv_w, ssm_conv_b, ssm_dt_bias, ssm_a_log, ssm_d, ssm_gate_norm, ssm_w_out, att_w_in, att_sinks, att_w_out, loss_target, m_pre_norm, m_post_norm, m_ssm_w_in, m_ssm_conv_w, m_ssm_conv_b, m_ssm_dt_bias, m_ssm_a_log, m_ssm_d, m_ssm_gate_norm, m_ssm_w_out, m_att_w_in, m_att_sinks, m_att_w_out, v_pre_norm, v_post_norm, v_ssm_w_in, v_ssm_conv_w, v_ssm_conv_b, v_ssm_dt_bias, v_ssm_a_log, v_ssm_d, v_ssm_gate_norm, v_ssm_w_out, v_att_w_in, v_att_sinks, v_att_w_out):
    w = dict(pre_norm=pre_norm, post_norm=post_norm, ssm_w_in=ssm_w_in, ssm_conv_w=ssm_conv_w, ssm_conv_b=ssm_conv_b,
             ssm_dt_bias=ssm_dt_bias, ssm_a_log=ssm_a_log, ssm_d=ssm_d, ssm_gate_norm=ssm_gate_norm, ssm_w_out=ssm_w_out,
             att_w_in=att_w_in, att_sinks=att_sinks, att_w_out=att_w_out)
    m = dict(pre_norm=m_pre_norm, post_norm=m_post_norm, ssm_w_in=m_ssm_w_in, ssm_conv_w=m_ssm_conv_w, ssm_conv_b=m_ssm_conv_b,
             ssm_dt_bias=m_ssm_dt_bias, ssm_a_log=m_ssm_a_log, ssm_d=m_ssm_d, ssm_gate_norm=m_ssm_gate_norm,
             ssm_w_out=m_ssm_w_out, att_w_in=m_att_w_in, att_sinks=m_att_sinks, att_w_out=m_att_w_out)
    v = dict(pre_norm=v_pre_norm, post_norm=v_post_norm, ssm_w_in=v_ssm_w_in, ssm_conv_w=v_ssm_conv_w, ssm_conv_b=v_ssm_conv_b,
             ssm_dt_bias=v_ssm_dt_bias, ssm_a_log=v_ssm_a_log, ssm_d=v_ssm_d, ssm_gate_norm=v_ssm_gate_norm,
             ssm_w_out=v_ssm_w_out, att_w_in=v_att_w_in, att_sinks=v_att_sinks, att_w_out=v_att_w_out)
    c = lax.axis_index("c")
    chip = 2 * lax.axis_index("x") + lax.axis_index("y")

    sharded = (ssm_w_in, ssm_w_out, att_w_in, att_w_out, ssm_conv_w)
    own = [dict(zip(SHARD_KEYS, _layer_shards(j, *sharded))) for j in range(2)]
    now_keys = ("ssm_w_in", "ssm_conv_w")
    later_keys = ("ssm_w_out", "att_w_in", "att_w_out")
    first_in = _whole_weights(now_keys, _chip_gather([own[0][k] for k in now_keys], "gather_weights_0"))

    def in_proj_with_first_pair(matmul):
        proj, *arrived = matmul(ride=[own[0][k] for k in later_keys])
        return proj, _whole_weights(later_keys, _pair_handoff(arrived, "gather_weights_0_rest_handoff"))

    def scan_with_second_pair(scan):
        results = scan(ride=[own[1][k] for k in SHARD_KEYS])
        scanned, arrived = results[:-len(SHARD_KEYS)], results[-len(SHARD_KEYS):]
        return (*scanned, _whole_weights(SHARD_KEYS, _pair_handoff(arrived, "gather_weights_1_handoff")))

    half = jnp.reshape(c, (1,)).astype(jnp.int32)

    def reduce_begin(pair_grads, tag):
        parts = [_halves_by_chip(k, pair_grads[k]) for k in BIG]
        from_sibling = _pair_swap(parts, f"reduce_pair_swap_{tag}")
        return [_pair_add(p, o, half, f"reduce_pair_add_{tag}_{n}") for n, (p, o) in enumerate(zip(parts, from_sibling))]

    def reduce_end(by_chip, tag):
        mine = [_sum_slots(a, f"reduce_chip_sum_{tag}_{n}") for n, a in enumerate(by_chip)]
        theirs = _pair_merge(mine, f"reduce_pair_merge_{tag}")
        return {k: jnp.where(c == 0, jnp.concatenate([a, b]), jnp.concatenate([b, a])) for k, a, b in zip(BIG, mine, theirs)}

    def attn_bwd_with_second_pair_grads(attn_bwd, pair_grads):
        dq, dk, dv, dgate, dsk, *by_chip = attn_bwd(ride=reduce_begin(pair_grads, "1"))
        return (dq, dk, dv, dgate, dsk), reduce_end(by_chip, "1")

    def in_dx_with_first_pair_grads(matmul, pair_grads):
        dh, *by_chip = matmul(ride=reduce_begin(pair_grads, "0"), ride_scatters=True)
        return dh, reduce_end(by_chip, "0")

    loss_lanes, grad_x, gr, reduced_0, reduced_1 = _local_step(
        x[0], positions[0], pre_norm, post_norm, ssm_conv_b, ssm_dt_bias, ssm_a_log, ssm_d, ssm_gate_norm, att_sinks,
        loss_target[0], first_in, in_proj_with_first_pair, scan_with_second_pair, attn_bwd_with_second_pair_grads,
        in_dx_with_first_pair_grads)
    loss = lax.psum(0.5 * jnp.sum(loss_lanes) / D_MODEL, ("x", "y", "c"))
    grads = {k: jnp.stack([reduced_0[k], reduced_1[k]]) for k in BIG}

    small_keys = SMALL + ("ssm_conv_w",)
    small_shapes = {k: w[k].shape for k in SMALL}
    small_shapes["ssm_conv_w"] = gr["ssm_conv_w"].shape
    small_sum = _sum_slots(_all_gather_small(_pack_small(gr, small_keys), "reduce_small_gather"), "reduce_small_sum")
    grads.update(_unpack_small(small_sum, small_shapes, small_keys))
    conv_cols = ssm_conv_w.shape[2]
    grads["ssm_conv_w"] = lax.dynamic_slice_in_dim(grads["ssm_conv_w"], chip * conv_cols, conv_cols, axis=2)

    delta, new_m, new_v = {}, {}, {}
    for k in SHARDED:
        shp = w[k].shape
        two_d = (shp[0] * shp[1], shp[2])
        d_, m_, v_ = _adamw(w[k].reshape(two_d), grads[k].reshape(two_d), m[k].reshape(two_d), v[k].reshape(two_d),
                            f"adamw_{k}")
        delta[k], new_m[k], new_v[k] = d_.reshape(shp), m_.reshape(shp), v_.reshape(shp)
    d_, m_, v_ = _adamw(_pack_small(w, SMALL), _pack_small(grads, SMALL), _pack_small(m, SMALL), _pack_small(v, SMALL),
                        "adamw_small")
    delta.update(_unpack_small(d_, small_shapes, SMALL))
    new_m.update(_unpack_small(m_, small_shapes, SMALL))
    new_v.update(_unpack_small(v_, small_shapes, SMALL))

    return (loss, grad_x[None], *[grads[k] for k in WEIGHTS], *[delta[k] for k in WEIGHTS],
            *[new_m[k] for k in WEIGHTS], *[new_v[k] for k in WEIGHTS])
```

```python
import functools

import jax
import jax.numpy as jnp
from jax import lax
from jax.experimental import pallas as pl
from jax.experimental.pallas import tpu as pltpu

F32 = jnp.float32
BF16 = jnp.bfloat16
EPS = 1e-6
NEG_INF = float("-inf")

D_MODEL = 1024
DEPTH = 4
SSM_D_INNER = 2048
SSM_HEAD_DIM = 64
SSM_HEADS = 32
SSM_GROUPS = 8
SSM_HPG = 4
SSM_STATE = 128
SSM_CONV = 4
SSM_CHUNK = 128
SSM_BC_DIM = 1024
SSM_CONV_DIM = 4096
SSM_IN_DIM = 6176
SSM_IN_PAD = 6272
SSM_DT_PAD = 128
ATT_HEAD_DIM = 64
ATT_Q_HEADS = 16
ATT_KV_HEADS = 4
ATT_GQA = 4
ATT_WIDTH = 1024
ATT_KV_WIDTH = 256
ATT_IN_DIM = 2560
ATT_QKV = ATT_WIDTH + 2 * ATT_KV_WIDTH
ATT_BLOCK = 128
ROPE_THETA = 500000.0
ROPE_DIM = 16
ROPE_HALF = 8
Q_SCALE = ATT_HEAD_DIM ** -0.5

ADAM_LR = 0.001
ADAM_B1 = 0.9
ADAM_B2 = 0.999
ADAM_EPS = 1e-08
ADAM_WD = 0.01
ADAM_STEP = 10

VMEM_LIMIT_BYTES = 48 * 1024 * 1024
NT_DIMS = (((1,), (1,)), ((), ()))
TN_DIMS = (((0,), (0,)), ((), ()))


def _params(*sem):
    return pltpu.CompilerParams(dimension_semantics=sem, vmem_limit_bytes=VMEM_LIMIT_BYTES)


def _pick(n, cands):
    for c in cands:
        if n % c == 0:
            return c
    return n


def _sigmoid(v):
    return 0.5 * jnp.tanh(0.5 * v) + 0.5


def _bdot_tn(a, b):
    return lax.dot_general(a.astype(BF16), b.astype(BF16), TN_DIMS, preferred_element_type=F32)


MATMUL_VMEM_BUDGET = 36 * 1024 * 1024


def _matmul_tiles(m, n, k, out_bytes, reduce_rows):
    best = None
    whole = [k] if (not reduce_rows or k <= 2048) else []
    for tk in whole + [c for c in (4096, 2048, 1024, 896, 512) if k % c == 0 and c < k]:
        for tm in (c for c in (2048, 1024, 512, 256) if m % c == 0):
            for tn in (c for c in (n, 1280, 1024, 896, 640, 512) if n % c == 0):
                acc = tm * tn * 4 if tk < k else 0
                need = 2 * (2 * tk * (tm + tn) + tm * tn * out_bytes) + acc
                if need <= MATMUL_VMEM_BUDGET and (best is None or tm * tn * min(tk, 2048) > best[0]):
                    best = (tm * tn * min(tk, 2048), tm, tn, tk)
        if best is not None and not reduce_rows:
            break
    return best[1:]


def _matmul(a, b, mode, out_dtype, name, ride=(), ride_scatters=False):
    if mode == "nn":
        (m, k), n = a.shape, b.shape[1]
    elif mode == "nt":
        (m, k), n = a.shape, b.shape[0]
    else:
        (k, m), n = a.shape, b.shape[1]
    tm, tn, tk = _matmul_tiles(m, n, k, jnp.dtype(out_dtype).itemsize, mode == "tn")
    nk = k // tk
    steps = (n // tn, m // tm, nk)
    dims = {"nn": (((1,), (0,)), ((), ())), "nt": NT_DIMS, "tn": TN_DIMS}[mode]
    n_ride = len(ride)
    exchange = _scatter_between_chips if ride_scatters else _gather_between_chips
    arrived = _scatter_shapes(ride) if ride_scatters else [jax.ShapeDtypeStruct((N_CHIPS,) + r.shape, r.dtype) for r in ride]

    def body(*refs):
        a_ref, b_ref = refs[:2]
        ride_in = refs[2:2 + n_ride]
        o_ref = refs[2 + n_ride]
        ride_out = refs[3 + n_ride:3 + 2 * n_ride]
        acc_ref = refs[3 + 2 * n_ride]
        ride_sems = refs[4 + 2 * n_ride:]
        kk = pl.program_id(2)
        at = [pl.program_id(d) for d in range(3)]
        if n_ride:
            @pl.when((at[0] == 0) & (at[1] == 0) & (at[2] == 0))
            def _():
                exchange(ride_in, ride_out, *ride_sems, wait=False)

        part = lax.dot_general(a_ref[...], b_ref[...], dims, preferred_element_type=F32)
        if nk == 1:
            o_ref[...] = part.astype(o_ref.dtype)
        else:
            @pl.when(kk == 0)
            def _():
                acc_ref[...] = part

            @pl.when(kk > 0)
            def _():
                acc_ref[...] += part

            @pl.when(kk == nk - 1)
            def _():
                o_ref[...] = acc_ref[...].astype(o_ref.dtype)

        if n_ride:
            @pl.when((at[0] == steps[0] - 1) & (at[1] == steps[1] - 1) & (at[2] == steps[2] - 1))
            def _():
                exchange(ride_in, ride_out, *ride_sems, wait=True)

    if mode == "nn":
        a_spec = pl.BlockSpec((tm, tk), lambda j, i, kk: (i, kk))
        b_spec = pl.BlockSpec((tk, tn), lambda j, i, kk: (kk, j))
    elif mode == "nt":
        a_spec = pl.BlockSpec((tm, tk), lambda j, i, kk: (i, kk))
        b_spec = pl.BlockSpec((tn, tk), lambda j, i, kk: (j, kk))
    else:
        a_spec = pl.BlockSpec((tk, tm), lambda j, i, kk: (kk, i))
        b_spec = pl.BlockSpec((tk, tn), lambda j, i, kk: (kk, j))
    out = pl.pallas_call(
        body, grid=steps, in_specs=[a_spec, b_spec] + [ANY] * n_ride,
        out_specs=[pl.BlockSpec((tm, tn), lambda j, i, kk: (i, j))] + [ANY] * n_ride,
        out_shape=[jax.ShapeDtypeStruct((m, n), out_dtype)] + arrived,
        scratch_shapes=[pltpu.VMEM((tm, tn), F32)] + (_gather_sems(n_ride) if n_ride else []),
        compiler_params=_params(*(["arbitrary"] * 3 if n_ride else ["parallel", "parallel", "arbitrary"])), name=name,
    )(a, b, *ride)
    return out if n_ride else out[0]


def _row_tile(l):
    return _pick(l, (512, 256, 128))


def _rmsnorm_fwd(x, w, name):
    l, d = x.shape
    tl = _row_tile(l)

    def body(x_ref, w_ref, o_ref):
        xv = x_ref[...]
        r = lax.rsqrt(jnp.mean(xv * xv, axis=-1, keepdims=True) + EPS)
        o_ref[...] = (xv * r * w_ref[...]).astype(o_ref.dtype)

    return pl.pallas_call(
        body, grid=(l // tl,),
        in_specs=[pl.BlockSpec((tl, d), lambda i: (i, 0)), pl.BlockSpec((1, d), lambda i: (0, 0))],
        out_specs=pl.BlockSpec((tl, d), lambda i: (i, 0)),
        out_shape=jax.ShapeDtypeStruct((l, d), BF16), compiler_params=_params("parallel"), name=name,
    )(x, w.reshape(1, d))


def _post_fwd(x, y, w, w_next, name):
    l, d = x.shape
    tl = _row_tile(l)

    def body(x_ref, y_ref, w_ref, wn_ref, o_ref, h_ref):
        yv = y_ref[...]
        r = lax.rsqrt(jnp.mean(yv * yv, axis=-1, keepdims=True) + EPS)
        out = x_ref[...] + yv * r * w_ref[...]
        o_ref[...] = out
        rn = lax.rsqrt(jnp.mean(out * out, axis=-1, keepdims=True) + EPS)
        h_ref[...] = (out * rn * wn_ref[...]).astype(h_ref.dtype)

    row = pl.BlockSpec((tl, d), lambda i: (i, 0))
    vec = pl.BlockSpec((1, d), lambda i: (0, 0))
    return pl.pallas_call(
        body, grid=(l // tl,), in_specs=[row, row, vec, vec], out_specs=[row, row],
        out_shape=[jax.ShapeDtypeStruct((l, d), F32), jax.ShapeDtypeStruct((l, d), BF16)],
        compiler_params=_params("parallel"), name=name,
    )(x, y, w.reshape(1, d), w_next.reshape(1, d))


def _post_loss(x, y, w, t, name):
    l, d = x.shape
    tl = _row_tile(l)
    nt = l // tl

    def body(x_ref, y_ref, w_ref, t_ref, g_ref, dy_ref, ls_ref, dw_ref, acc_ref):
        i = pl.program_id(0)

        @pl.when(i == 0)
        def _():
            ls_ref[...] = jnp.zeros_like(ls_ref)
            acc_ref[...] = jnp.zeros_like(acc_ref)

        yv = y_ref[...]
        r = lax.rsqrt(jnp.mean(yv * yv, axis=-1, keepdims=True) + EPS)
        nrm = yv * r
        e = x_ref[...] + nrm * w_ref[...] - t_ref[...]
        gv = e * (1.0 / d)
        g_ref[...] = gv
        ls_ref[...] += jnp.sum((e * e).reshape(tl // 8, 8, d), axis=0)
        gw = gv * w_ref[...]
        dy_ref[...] = (r * (gw - nrm * jnp.mean(gw * nrm, axis=-1, keepdims=True))).astype(dy_ref.dtype)
        acc_ref[...] += jnp.sum((gv * nrm).reshape(tl // 8, 8, d), axis=0)

        @pl.when(i == nt - 1)
        def _():
            dw_ref[...] = jnp.sum(acc_ref[...], axis=0, keepdims=True)

    row = pl.BlockSpec((tl, d), lambda i: (i, 0))
    vec = pl.BlockSpec((1, d), lambda i: (0, 0))
    return pl.pallas_call(
        body, grid=(nt,), in_specs=[row, row, vec, row],
        out_specs=[row, row, pl.BlockSpec((8, d), lambda i: (0, 0)), vec],
        out_shape=[jax.ShapeDtypeStruct((l, d), F32), jax.ShapeDtypeStruct((l, d), BF16),
                   jax.ShapeDtypeStruct((8, d), F32), jax.ShapeDtypeStruct((1, d), F32)],
        scratch_shapes=[pltpu.VMEM((8, d), F32)], compiler_params=_params("arbitrary"), name=name,
    )(x, y, w.reshape(1, d), t)


def _norm_bwd_chain(dh, x, w_pre, resid, y_prev, w_post_prev, name):
    l, d = x.shape
    tl = _row_tile(l)
    nt = l // tl

    def body(dh_ref, x_ref, wp_ref, r_ref, y_ref, wq_ref, g_ref, dy_ref, dwp_ref, dwq_ref, accp_ref, accq_ref):
        i = pl.program_id(0)

        @pl.when(i == 0)
        def _():
            accp_ref[...] = jnp.zeros_like(accp_ref)
            accq_ref[...] = jnp.zeros_like(accq_ref)

        xv = x_ref[...]
        dhv = dh_ref[...]
        rx = lax.rsqrt(jnp.mean(xv * xv, axis=-1, keepdims=True) + EPS)
        nx = xv * rx
        gw = dhv * wp_ref[...]
        gv = rx * (gw - nx * jnp.mean(gw * nx, axis=-1, keepdims=True)) + r_ref[...]
        g_ref[...] = gv
        accp_ref[...] += jnp.sum((dhv * nx).reshape(tl // 8, 8, d), axis=0)
        yv = y_ref[...]
        ry = lax.rsqrt(jnp.mean(yv * yv, axis=-1, keepdims=True) + EPS)
        ny = yv * ry
        gq = gv * wq_ref[...]
        dy_ref[...] = (ry * (gq - ny * jnp.mean(gq * ny, axis=-1, keepdims=True))).astype(dy_ref.dtype)
        accq_ref[...] += jnp.sum((gv * ny).reshape(tl // 8, 8, d), axis=0)

        @pl.when(i == nt - 1)
        def _():
            dwp_ref[...] = jnp.sum(accp_ref[...], axis=0, keepdims=True)
            dwq_ref[...] = jnp.sum(accq_ref[...], axis=0, keepdims=True)

    row = pl.BlockSpec((tl, d), lambda i: (i, 0))
    vec = pl.BlockSpec((1, d), lambda i: (0, 0))
    return pl.pallas_call(
        body, grid=(nt,), in_specs=[row, row, vec, row, row, vec], out_specs=[row, row, vec, vec],
        out_shape=[jax.ShapeDtypeStruct((l, d), F32), jax.ShapeDtypeStruct((l, d), BF16),
                   jax.ShapeDtypeStruct((1, d), F32), jax.ShapeDtypeStruct((1, d), F32)],
        scratch_shapes=[pltpu.VMEM((8, d), F32), pltpu.VMEM((8, d), F32)],
        compiler_params=_params("arbitrary"), name=name,
    )(dh, x, w_pre.reshape(1, d), resid, y_prev, w_post_prev.reshape(1, d))


def _rmsnorm_bwd(g, y, w, resid, name):
    l, d = y.shape
    tl = _row_tile(l)
    nt = l // tl

    def body(g_ref, y_ref, w_ref, r_ref, dy_ref, dw_ref, acc_ref):
        i = pl.program_id(0)

        @pl.when(i == 0)
        def _():
            acc_ref[...] = jnp.zeros_like(acc_ref)

        yv = y_ref[...]
        gv = g_ref[...]
        r = lax.rsqrt(jnp.mean(yv * yv, axis=-1, keepdims=True) + EPS)
        nrm = yv * r
        gw = gv * w_ref[...]
        dy_ref[...] = r * (gw - nrm * jnp.mean(gw * nrm, axis=-1, keepdims=True)) + r_ref[...]
        acc_ref[...] += jnp.sum((gv * nrm).reshape(tl // 8, 8, d), axis=0)

        @pl.when(i == nt - 1)
        def _():
            dw_ref[...] = jnp.sum(acc_ref[...], axis=0, keepdims=True)

    row = pl.BlockSpec((tl, d), lambda i: (i, 0))
    vec = pl.BlockSpec((1, d), lambda i: (0, 0))
    return pl.pallas_call(
        body, grid=(nt,), in_specs=[row, row, vec, row], out_specs=[row, vec],
        out_shape=[jax.ShapeDtypeStruct((l, d), F32), jax.ShapeDtypeStruct((1, d), F32)],
        scratch_shapes=[pltpu.VMEM((8, d), F32)], compiler_params=_params("arbitrary"), name=name,
    )(g, y, w.reshape(1, d), resid)


HALO = 8
CONV_SUB_ROWS = 64
CONV_SUB_COLS = 256


DT_COL_BLOCK = (SSM_D_INNER + SSM_CONV_DIM) // SSM_DT_PAD


def _split3(v):
    hi = v.astype(BF16)
    rest = v - hi.astype(F32)
    mid = rest.astype(BF16)
    lo = (rest - mid.astype(F32)).astype(BF16)
    return hi, mid, lo


def _dt_and_decay(v, a_log):
    head_dim_log2 = SSM_HEAD_DIM.bit_length() - 1
    dt_hi, dt_mid, _ = _split3(jnp.maximum(v, 0.0) + jnp.log1p(jnp.exp(-jnp.abs(v))))
    dt = dt_hi.astype(F32) + dt_mid.astype(F32)
    ri = lax.broadcasted_iota(jnp.int32, (SSM_CHUNK, SSM_CHUNK), 0)
    cj = lax.broadcasted_iota(jnp.int32, (SSM_CHUNK, SSM_CHUNK), 1)
    tri = (ri >= cj).astype(BF16)
    acs_pieces = _split3(sum(jnp.dot(tri, piece, preferred_element_type=F32)
                             for piece in _split3(dt * (-jnp.exp(a_log)))))
    acs = sum(piece.astype(F32) for piece in acs_pieces)
    head_of_lane = lax.shift_right_logical(lax.broadcasted_iota(jnp.int32, (SSM_DT_PAD, SSM_D_INNER), 1), head_dim_log2)
    spread = (head_of_lane == lax.broadcasted_iota(jnp.int32, (SSM_DT_PAD, SSM_D_INNER), 0)).astype(BF16)
    dtb = sum(jnp.dot(piece, spread, preferred_element_type=F32) for piece in (dt_hi, dt_mid))
    acsb = sum(jnp.dot(piece, spread, preferred_element_type=F32) for piece in acs_pieces)
    return dtb, acsb, dt.T, acs.T


def _dt_bwd(ddt, proj, bias, dproj, name):
    l = proj.shape[0]
    tl = _row_tile(l)

    def body(g_ref, p_ref, b_ref, _, o_ref, db_ref):
        @pl.when(pl.program_id(0) == 0)
        def _():
            db_ref[...] = jnp.zeros_like(db_ref)

        d = g_ref[...] * _sigmoid(p_ref[...] + b_ref[...])
        o_ref[...] = d.astype(o_ref.dtype)
        db_ref[...] += jnp.sum(d, axis=0, keepdims=True)

    return pl.pallas_call(
        body, grid=(l // tl,),
        in_specs=[pl.BlockSpec((tl, SSM_DT_PAD), lambda i: (i, 0)),
                  pl.BlockSpec((tl, SSM_DT_PAD), lambda i: (i, DT_COL_BLOCK)),
                  pl.BlockSpec((1, SSM_DT_PAD), lambda i: (0, 0)),
                  pl.BlockSpec(memory_space=pl.ANY)],
        out_specs=[pl.BlockSpec((tl, SSM_DT_PAD), lambda i: (i, DT_COL_BLOCK)),
                   pl.BlockSpec((1, SSM_DT_PAD), lambda i: (0, 0))],
        out_shape=[jax.ShapeDtypeStruct(dproj.shape, dproj.dtype), jax.ShapeDtypeStruct((1, SSM_DT_PAD), F32)],
        input_output_aliases={3: 0}, compiler_params=_params("arbitrary"), name=name,
    )(ddt, proj, bias, dproj)


GP = SSM_HPG * SSM_HEAD_DIM
HEAD_DIM_LOG2 = SSM_HEAD_DIM.bit_length() - 1
GPS = SSM_GROUPS
B_BLOCK0 = SSM_D_INNER // SSM_STATE
C_BLOCK0 = (SSM_D_INNER + SSM_BC_DIM) // SSM_STATE


def _chunk_iotas():
    ri = lax.broadcasted_iota(jnp.int32, (SSM_CHUNK, SSM_CHUNK), 0)
    cj = lax.broadcasted_iota(jnp.int32, (SSM_CHUNK, SSM_CHUNK), 1)
    return ri, cj


def _head_decay(acsb, acs_r, r, ri, cj):
    pair = acsb[:, (r // 2) * LANES:(r // 2 + 1) * LANES]
    mine_low = r % 2 == 0
    lane = lax.broadcasted_iota(jnp.int32, (1, LANES), 1)
    col = jnp.where((lane < SSM_HEAD_DIM) == mine_low, pair, pltpu.roll(pair, SSM_HEAD_DIM, 1))
    return jnp.exp(jnp.where(ri >= cj, col - acs_r[r:r + 1, :], NEG_INF))


def _head_masked_rows(v, dtype):
    head_of_lane = lax.shift_right_logical(lax.broadcasted_iota(jnp.int32, (1, GP), 1), HEAD_DIM_LOG2)
    narrow = v.astype(dtype)
    return jnp.concatenate([jnp.where(head_of_lane == r, narrow, jnp.zeros_like(narrow)) for r in range(SSM_HPG)], axis=0)


def _ssd_fwd(proj, cw, cb, dt_bias, a_log, d_lanes, gate_w, name, ride=()):
    l = proj.shape[0]
    nc = l // SSM_CHUNK
    assert GPS == SSM_GROUPS
    n_ride = len(ride)
    halo_blocks = SSM_CHUNK // HALO
    x_block = 1

    def body(*refs):
        u0_ref, u1_ref, h0_ref, h1_ref, cw_ref, cb_ref, dtraw_ref, bias_ref, alog_ref, d_ref, z_ref, gw_ref = refs[:12]
        ride_in = refs[12:12 + n_ride]
        (y_ref, act_ref, hin_ref, pre_ref, xbc_ref, dtb_out, acsb_out, dtr_out, acsr_out) = refs[12 + n_ride:21 + n_ride]
        ride_out = refs[21 + n_ride:21 + 2 * n_ride]
        h_ref, ext_ref, conv_ref, dtb_ref, acsb_ref, acsr_ref = refs[21 + 2 * n_ride:27 + 2 * n_ride]
        ride_sems = refs[27 + 2 * n_ride:]
        s = pl.program_id(0)
        if n_ride:
            @pl.when(s == 0)
            def _():
                _gather_between_chips(ride_in, ride_out, *ride_sems, wait=False)

            @pl.when(s == nc)
            def _():
                _gather_between_chips(ride_in, ride_out, *ride_sems, wait=True)

        @pl.when(s <= 1)
        def _():
            h_ref[...] = jnp.zeros_like(h_ref)

        @pl.when(s == 0)
        def _():
            conv_ref[1] = jnp.zeros((SSM_CHUNK, SSM_CONV_DIM), BF16)
            dtb_ref[1] = jnp.zeros((SSM_CHUNK, SSM_D_INNER), F32)
            acsb_ref[1] = jnp.zeros((SSM_CHUNK, SSM_D_INNER), F32)
            acsr_ref[1] = jnp.zeros((SSM_GROUPS, SSM_HPG, SSM_CHUNK), F32)

        conv_slot = s & 1
        scan_slot = (s - 1) & 1
        for half, (u_ref, hl_ref) in enumerate(((u0_ref, h0_ref), (u1_ref, h1_ref))):
            hc = slice(half * SSM_D_INNER, (half + 1) * SSM_D_INNER)
            ext_ref[0:HALO, hc] = jnp.where(s > 0, hl_ref[...], 0.0)
            ext_ref[HALO:HALO + SSM_CHUNK, hc] = u_ref[...]

        def conv_columns(c_lo, c_hi):
            for r0 in range(0, SSM_CHUNK, CONV_SUB_ROWS):
                for c0 in range(c_lo, c_hi, CONV_SUB_COLS):
                    cs = slice(c0, c0 + CONV_SUB_COLS)
                    ext = ext_ref[r0:r0 + CONV_SUB_ROWS + HALO, cs]
                    acc = cb_ref[:, cs] + cw_ref[SSM_CONV - 1:SSM_CONV, cs] * ext[HALO:]
                    for k in range(SSM_CONV - 1):
                        acc = acc + cw_ref[k:k + 1, cs] * pltpu.roll(ext, SSM_CONV - 1 - k, 0)[HALO:]
                    act = (acc * _sigmoid(acc)).astype(BF16)
                    pre_ref[r0:r0 + CONV_SUB_ROWS, cs] = acc.astype(pre_ref.dtype)
                    xbc_ref[r0:r0 + CONV_SUB_ROWS, cs] = act
                    conv_ref[conv_slot, r0:r0 + CONV_SUB_ROWS, cs] = act

        dtb, acsb, dt_rows, acs_rows = _dt_and_decay(dtraw_ref[...] + bias_ref[...], alog_ref[...])
        dtb_out[...] = dtb
        acsb_out[...] = acsb
        dtb_ref[conv_slot] = dtb
        acsb_ref[conv_slot] = acsb
        for g in range(SSM_GROUPS):
            heads = slice(g * SSM_HPG, (g + 1) * SSM_HPG)
            dtr_out[g] = dt_rows[heads, :]
            acsr_out[g] = acs_rows[heads, :]
            acsr_ref[conv_slot, g] = acs_rows[heads, :]

        ri, cj = _chunk_iotas()
        conv_share = SSM_CONV_DIM // GPS
        for k in range(GPS):
            conv_columns(k * conv_share, (k + 1) * conv_share)
            g = k
            cols = slice(k * GP, (k + 1) * GP)
            bcols = slice(SSM_D_INNER + k * SSM_STATE, SSM_D_INNER + (k + 1) * SSM_STATE)
            ccols = slice(SSM_D_INNER + SSM_BC_DIM + k * SSM_STATE, SSM_D_INNER + SSM_BC_DIM + (k + 1) * SSM_STATE)
            xv = conv_ref[scan_slot, :, cols].astype(F32)
            bb = conv_ref[scan_slot, :, bcols]
            cb16 = conv_ref[scan_slot, :, ccols]
            acs_v = acsb_ref[scan_slot, :, cols]
            acs_r_v = acsr_ref[scan_slot, k]
            lastb = acs_v[SSM_CHUNK - 1:SSM_CHUNK, :]
            xd = xv * dtb_ref[scan_slot, :, cols]
            cbm = lax.dot_general(cb16, bb, NT_DIMS, preferred_element_type=F32)
            hin = h_ref[g]
            hin_ref[0, k] = hin
            yoff = jnp.dot(cb16, hin.astype(BF16), preferred_element_type=F32)
            ms = [(cbm * _head_decay(acs_v, acs_r_v, r, ri, cj)).astype(BF16) for r in range(SSM_HPG)]
            ydiag = jnp.dot(jnp.concatenate(ms, axis=1), _head_masked_rows(xd, BF16), preferred_element_type=F32)
            y_ref[:, cols] = ydiag + jnp.exp(acs_v) * yoff + d_ref[k] * xv
            h_ref[g] = hin * jnp.exp(lastb) + _bdot_tn(bb, xd * jnp.exp(lastb - acs_v))
        z = z_ref[...]
        yg = y_ref[...] * (z * _sigmoid(z))
        r = lax.rsqrt(jnp.mean(yg * yg, axis=-1, keepdims=True) + EPS)
        act_ref[...] = (yg * r * gw_ref[...]).astype(act_ref.dtype)

    def conv_at(s):
        return jnp.minimum(s, nc - 1)

    def scan_at(s):
        return jnp.maximum(s - 1, 0)

    lanes = pl.BlockSpec((SSM_CHUNK, SSM_D_INNER), lambda s: (scan_at(s), 0))
    conv_out = pl.BlockSpec((SSM_CHUNK, SSM_CONV_DIM), lambda s: (conv_at(s), 0))
    lanes_ahead = pl.BlockSpec((SSM_CHUNK, SSM_D_INNER), lambda s: (conv_at(s), 0))
    rows_ahead = pl.BlockSpec((SSM_GROUPS, SSM_HPG, SSM_CHUNK), lambda s: (0, 0, conv_at(s)))
    return pl.pallas_call(
        body, grid=(nc + 1,),
        in_specs=[pl.BlockSpec((SSM_CHUNK, SSM_D_INNER), lambda s: (conv_at(s), x_block)),
                  pl.BlockSpec((SSM_CHUNK, SSM_D_INNER), lambda s: (conv_at(s), x_block + 1)),
                  pl.BlockSpec((HALO, SSM_D_INNER), lambda s: (jnp.maximum(conv_at(s) * halo_blocks - 1, 0), x_block)),
                  pl.BlockSpec((HALO, SSM_D_INNER), lambda s: (jnp.maximum(conv_at(s) * halo_blocks - 1, 0), x_block + 1)),
                  pl.BlockSpec((SSM_CONV, SSM_CONV_DIM), lambda s: (0, 0)),
                  pl.BlockSpec((1, SSM_CONV_DIM), lambda s: (0, 0)),
                  pl.BlockSpec((SSM_CHUNK, SSM_DT_PAD), lambda s: (conv_at(s), DT_COL_BLOCK)),
                  pl.BlockSpec((1, SSM_DT_PAD), lambda s: (0, 0)),
                  pl.BlockSpec((1, SSM_DT_PAD), lambda s: (0, 0)),
                  pl.BlockSpec((SSM_GROUPS, 1, GP), lambda s: (0, 0, 0)),
                  lanes, pl.BlockSpec((1, SSM_D_INNER), lambda s: (0, 0))] + [ANY] * n_ride,
        out_specs=[lanes, lanes, pl.BlockSpec((1, SSM_GROUPS, SSM_STATE, GP), lambda s: (scan_at(s), 0, 0, 0)),
                   conv_out, conv_out, lanes_ahead, lanes_ahead, rows_ahead, rows_ahead] + [ANY] * n_ride,
        out_shape=[jax.ShapeDtypeStruct((l, SSM_D_INNER), F32), jax.ShapeDtypeStruct((l, SSM_D_INNER), BF16),
                   jax.ShapeDtypeStruct((nc, SSM_GROUPS, SSM_STATE, GP), F32),
                   jax.ShapeDtypeStruct((l, SSM_CONV_DIM), BF16), jax.ShapeDtypeStruct((l, SSM_CONV_DIM), BF16),
                   jax.ShapeDtypeStruct((l, SSM_D_INNER), F32), jax.ShapeDtypeStruct((l, SSM_D_INNER), F32),
                   jax.ShapeDtypeStruct((SSM_GROUPS, SSM_HPG, l), F32),
                   jax.ShapeDtypeStruct((SSM_GROUPS, SSM_HPG, l), F32)]
        + [jax.ShapeDtypeStruct((N_CHIPS,) + a.shape, a.dtype) for a in ride],
        scratch_shapes=[pltpu.VMEM((SSM_GROUPS, SSM_STATE, GP), F32),
                        pltpu.VMEM((SSM_CHUNK + HALO, SSM_CONV_DIM), F32),
                        pltpu.VMEM((2, SSM_CHUNK, SSM_CONV_DIM), BF16),
                        pltpu.VMEM((2, SSM_CHUNK, SSM_D_INNER), F32), pltpu.VMEM((2, SSM_CHUNK, SSM_D_INNER), F32),
                        pltpu.VMEM((2, SSM_GROUPS, SSM_HPG, SSM_CHUNK), F32)] + (_gather_sems(n_ride) if n_ride else []),
        compiler_params=_params("arbitrary"), name=name,
    )(proj, proj, proj, proj, cw, cb.reshape(1, SSM_CONV_DIM), proj, dt_bias, a_log, d_lanes, proj,
      gate_w.reshape(1, SSM_D_INNER), *ride)


def _ssd_bwd(xbc, pre, cw, dtb, acsb, dtr, acs_r, a_log, d_lanes, hin, dact, y, proj, gate_w, name):
    l = xbc.shape[0]
    nc = l // SSM_CHUNK
    sub = CONV_SUB_ROWS

    def body(x_ref, b_ref, c_ref, dtb_ref, acsb_ref, dtr_ref, acsr_ref, alc_ref, d_ref, hin_ref,
             dact_ref, y_ref, z_ref, gw_ref, pre_ref, u0_ref, u1_ref, cw_ref,
             dproj_ref, ddt_ref, dal_ref, dd_ref, dgw_ref, dcw_ref, dcb_ref,
             dh_ref, dy_ref, acc_ref, dxbc_s, dz_s, ddt_s, carry_ref, ext_ref, dcw_acc, dcb_acc):
        step = pl.program_id(0)
        live = (step < nc).astype(F32)
        stage = step & 1
        staged = (step - 1) & 1

        @pl.when(step == 0)
        def _():
            for ref in (dal_ref, dd_ref, acc_ref, dh_ref, carry_ref, dcw_acc, dcb_acc):
                ref[...] = jnp.zeros_like(ref)
            dxbc_s[1] = jnp.zeros((SSM_CHUNK, SSM_CONV_DIM), F32)
            dz_s[1] = jnp.zeros((SSM_CHUNK, SSM_D_INNER), BF16)
            ddt_s[1] = jnp.zeros((SSM_GROUPS, SSM_HPG, SSM_CHUNK), F32)

        z = z_ref[...]
        yv = y_ref[...]
        s = _sigmoid(z)
        sz = z * s
        yg = yv * sz
        r = lax.rsqrt(jnp.mean(yg * yg, axis=-1, keepdims=True) + EPS)
        nrm = yg * r
        gv = dact_ref[...]
        gw = gv * gw_ref[...]
        dyg = r * (gw - nrm * jnp.mean(gw * nrm, axis=-1, keepdims=True))
        dy_ref[...] = dyg * sz
        dz_s[stage] = (dyg * yv * (s * (1.0 + z * (1.0 - s)))).astype(BF16)
        acc_ref[...] += live * jnp.sum((gv * nrm).reshape(SSM_CHUNK // 8, 8, SSM_D_INNER), axis=0)

        p = pre_ref[...].astype(F32)
        sp = _sigmoid(p)
        dp = dxbc_s[staged] * (sp * (1.0 + p * (1.0 - sp)))
        ext_ref[0:SSM_CHUNK, :] = dp
        ext_ref[SSM_CHUNK:SSM_CHUNK + HALO, :] = carry_ref[...]
        carry_ref[...] = dp[0:HALO]
        dproj_ref[:, 0:SSM_D_INNER] = dz_s[staged]
        ddt_ref[...] = ddt_s[staged]

        def fold(v):
            return jnp.sum(v.reshape(sub // 8, 8, CONV_SUB_COLS), axis=0)

        def conv_columns(c_lo, c_hi):
            for c0 in range(c_lo, c_hi, CONV_SUB_COLS):
                cs = slice(c0, c0 + CONV_SUB_COLS)
                u_ref, ucs = (u0_ref, cs) if c0 < SSM_D_INNER else (u1_ref, slice(c0 - SSM_D_INNER, c0 - SSM_D_INNER + CONV_SUB_COLS))
                for r0 in range(0, SSM_CHUNK, sub):
                    dext = ext_ref[r0:r0 + sub + HALO, cs]
                    uv = u_ref[r0:r0 + sub, ucs]
                    for k in range(SSM_CONV):
                        j = SSM_CONV - 1 - k
                        ahead = dext[:sub] if j == 0 else pltpu.roll(dext, sub + HALO - j, 0)[:sub]
                        term = cw_ref[k:k + 1, cs] * ahead
                        du = term if k == 0 else du + term
                        dcw_acc[k, :, cs] += fold(ahead * uv)
                    dcb_acc[:, cs] += fold(dext[:sub])
                    dproj_ref[r0:r0 + sub, SSM_D_INNER + c0:SSM_D_INNER + c0 + CONV_SUB_COLS] = du.astype(dproj_ref.dtype)

        conv_share = SSM_CONV_DIM // GPS
        for k in range(GPS):
            conv_columns(k * conv_share, (k + 1) * conv_share)
            one_group(live, stage, k, k, x_ref, b_ref, c_ref, dtb_ref, acsb_ref, dtr_ref, acsr_ref, alc_ref, d_ref,
                      hin_ref, dy_ref, dxbc_s, ddt_s, dal_ref, dd_ref, dh_ref)

        @pl.when(step == nc)
        def _():
            dgw_ref[...] = jnp.sum(acc_ref[...], axis=0, keepdims=True)
            dcw_ref[...] = jnp.sum(dcw_acc[...], axis=1)
            dcb_ref[...] = jnp.sum(dcb_acc[...], axis=0, keepdims=True)

    def one_group(live, stage, g, k, x_ref, b_ref, c_ref, dtb_ref, acsb_ref, dtr_ref, acsr_ref, alc_ref, d_ref, hin_ref,
                  dy_ref, dxbc_s, ddt_s, dal_ref, dd_ref, dh_ref):
        cols = slice(k * GP, (k + 1) * GP)
        ncols = slice(k * SSM_STATE, (k + 1) * SSM_STATE)
        bcols = slice(SSM_D_INNER + k * SSM_STATE, SSM_D_INNER + (k + 1) * SSM_STATE)
        ccols = slice(SSM_D_INNER + SSM_BC_DIM + k * SSM_STATE, SSM_D_INNER + SSM_BC_DIM + (k + 1) * SSM_STATE)

        xv = x_ref[:, cols].astype(F32)
        dyv = dy_ref[:, cols]
        bb = b_ref[:, ncols].astype(BF16)
        cb16 = c_ref[:, ncols].astype(BF16)
        dtb = dtb_ref[:, cols]
        acsb = acsb_ref[:, cols]
        dtr_v = dtr_ref[k]
        acs_r = acsr_ref[k]
        a_col = -jnp.exp(alc_ref[k])
        ri, cj = _chunk_iotas()
        head_of_lane = lax.shift_right_logical(lax.broadcasted_iota(jnp.int32, (SSM_HPG, GP), 1), HEAD_DIM_LOG2)
        ind_t = (head_of_lane == lax.broadcasted_iota(jnp.int32, (SSM_HPG, GP), 0)).astype(BF16)
        lastb = acsb[SSM_CHUNK - 1:SSM_CHUNK, :]
        ecb = jnp.exp(acsb)
        dteb = jnp.exp(lastb - acsb)
        xd = xv * dtb
        xw = xd * dteb
        cb = lax.dot_general(cb16, bb, NT_DIMS, preferred_element_type=F32)
        hin_v = hin_ref[0, k]
        dhn = dh_ref[g]
        h16 = hin_v.astype(BF16)
        dh16 = dhn.astype(BF16)
        ch = jnp.dot(cb16, h16, preferred_element_type=F32)
        bdh = jnp.dot(bb, dh16, preferred_element_type=F32)
        dym = _head_masked_rows(dyv, BF16)
        g_all = lax.dot_general(dym, xd.astype(BF16), NT_DIMS, preferred_element_type=F32)
        gl_sum = jnp.zeros((SSM_CHUNK, SSM_CHUNK), F32)
        ms, qs = [], []
        for r in range(SSM_HPG):
            decay = _head_decay(acsb, acs_r, r, ri, cj)
            gl = g_all[r * SSM_CHUNK:(r + 1) * SSM_CHUNK] * decay
            gl_sum = gl_sum + gl
            ms.append((cb * decay).astype(BF16))
            qs.append((gl * cb).astype(BF16))
        dxd = lax.dot_general(jnp.concatenate(ms, axis=0), dym, TN_DIMS, preferred_element_type=F32) + dteb * bdh
        cum = jnp.dot(jnp.concatenate(qs, axis=0), (ri < cj).astype(BF16), preferred_element_type=F32)
        sub4 = lax.broadcasted_iota(jnp.int32, (SSM_HPG, 1), 0)
        da = jnp.zeros((SSM_HPG, SSM_CHUNK), F32)
        for r in range(SSM_HPG):
            rect = jnp.sum(jnp.where(ri >= cj, cum[r * SSM_CHUNK:(r + 1) * SSM_CHUNK], 0.0), axis=0, keepdims=True)
            da = da + jnp.where(sub4 == r, rect, 0.0)
        z2 = xw * bdh
        sub8 = lax.broadcasted_iota(jnp.int32, (8, 1), 0)
        col_sums = (jnp.where(sub8 == 0, jnp.sum(z2, axis=0, keepdims=True), 0.0)
                    + jnp.where(sub8 == 1, jnp.sum(dhn * hin_v, axis=0, keepdims=True), 0.0)
                    + jnp.where(sub8 == 2, jnp.sum(dyv * xv, axis=0, keepdims=True), 0.0))
        wv = ecb * dyv
        summands = jnp.concatenate([wv * ch - z2, dxd * xv, col_sums], axis=0)
        sums = lax.dot_general(ind_t, summands.astype(BF16), NT_DIMS, preferred_element_type=F32)
        per_pos = sums[:, :2 * SSM_CHUNK]
        totals = sums[:, 2 * SSM_CHUNK:]
        e_last = totals[:, 0:1] + jnp.exp(acs_r[:, SSM_CHUNK - 1:SSM_CHUNK]) * totals[:, 1:2]
        da = (da + e_last + jnp.dot(per_pos[:, :SSM_CHUNK], (ri >= cj).astype(F32), preferred_element_type=F32,
                                    precision=lax.Precision.HIGHEST))
        ddt_s[stage, k] = a_col * da + per_pos[:, SSM_CHUNK:]
        dal_ref[g] += live * (a_col * jnp.sum(da * dtr_v, axis=1, keepdims=True))
        dd_ref[g] += live * totals[:, 2:3]
        dxbc_s[stage, :, cols] = dxd * dtb + d_ref[k] * dyv
        w16 = wv.astype(BF16)
        xw16 = xw.astype(BF16)
        gl16 = gl_sum.astype(BF16)
        dxbc_s[stage, :, ccols] = (jnp.dot(gl16, bb, preferred_element_type=F32)
                                   + lax.dot_general(w16, h16, NT_DIMS, preferred_element_type=F32))
        dxbc_s[stage, :, bcols] = (lax.dot_general(gl16, cb16, TN_DIMS, preferred_element_type=F32)
                                   + lax.dot_general(xw16, dh16, NT_DIMS, preferred_element_type=F32))
        dh_ref[g] = dhn * jnp.exp(lastb) + lax.dot_general(cb16, w16, TN_DIMS, preferred_element_type=F32)

    def scan_at(s):
        return nc - 1 - jnp.minimum(s, nc - 1)

    def conv_at(s):
        return nc - 1 - jnp.maximum(s - 1, 0)

    small = pl.BlockSpec((SSM_GROUPS, SSM_HPG, 1), lambda s: (0, 0, 0))
    lanes = pl.BlockSpec((SSM_CHUNK, SSM_D_INNER), lambda s: (scan_at(s), 0))
    rows = pl.BlockSpec((SSM_GROUPS, SSM_HPG, SSM_CHUNK), lambda s: (0, 0, scan_at(s)))
    vec = pl.BlockSpec((1, SSM_D_INNER), lambda s: (0, 0))
    return pl.pallas_call(
        body, grid=(nc + 1,),
        in_specs=[lanes,
                  pl.BlockSpec((SSM_CHUNK, SSM_BC_DIM), lambda s: (scan_at(s), B_BLOCK0 // GPS)),
                  pl.BlockSpec((SSM_CHUNK, SSM_BC_DIM), lambda s: (scan_at(s), C_BLOCK0 // GPS)),
                  lanes, lanes, rows, rows,
                  pl.BlockSpec((SSM_GROUPS, SSM_HPG, 1), lambda s: (0, 0, 0)),
                  pl.BlockSpec((SSM_GROUPS, 1, GP), lambda s: (0, 0, 0)),
                  pl.BlockSpec((1, SSM_GROUPS, SSM_STATE, GP), lambda s: (scan_at(s), 0, 0, 0)),
                  lanes, lanes, lanes, vec,
                  pl.BlockSpec((SSM_CHUNK, SSM_CONV_DIM), lambda s: (conv_at(s), 0)),
                  pl.BlockSpec((SSM_CHUNK, SSM_D_INNER), lambda s: (conv_at(s), 1)),
                  pl.BlockSpec((SSM_CHUNK, SSM_D_INNER), lambda s: (conv_at(s), 2)),
                  pl.BlockSpec((SSM_CONV, SSM_CONV_DIM), lambda s: (0, 0))],
        out_specs=[pl.BlockSpec((SSM_CHUNK, SSM_D_INNER + SSM_CONV_DIM), lambda s: (conv_at(s), 0)),
                   pl.BlockSpec((SSM_GROUPS, SSM_HPG, SSM_CHUNK), lambda s: (0, 0, conv_at(s))),
                   small, small, vec,
                   pl.BlockSpec((SSM_CONV, SSM_CONV_DIM), lambda s: (0, 0)),
                   pl.BlockSpec((1, SSM_CONV_DIM), lambda s: (0, 0))],
        out_shape=[jax.ShapeDtypeStruct((l, SSM_IN_PAD), BF16), jax.ShapeDtypeStruct((SSM_GROUPS, SSM_HPG, l), F32),
                   jax.ShapeDtypeStruct((SSM_GROUPS, SSM_HPG, 1), F32),
                   jax.ShapeDtypeStruct((SSM_GROUPS, SSM_HPG, 1), F32),
                   jax.ShapeDtypeStruct((1, SSM_D_INNER), F32),
                   jax.ShapeDtypeStruct((SSM_CONV, SSM_CONV_DIM), F32), jax.ShapeDtypeStruct((1, SSM_CONV_DIM), F32)],
        scratch_shapes=[pltpu.VMEM((SSM_GROUPS, SSM_STATE, GP), F32), pltpu.VMEM((SSM_CHUNK, SSM_D_INNER), F32),
                        pltpu.VMEM((8, SSM_D_INNER), F32),
                        pltpu.VMEM((2, SSM_CHUNK, SSM_CONV_DIM), F32), pltpu.VMEM((2, SSM_CHUNK, SSM_D_INNER), BF16),
                        pltpu.VMEM((2, SSM_GROUPS, SSM_HPG, SSM_CHUNK), F32), pltpu.VMEM((HALO, SSM_CONV_DIM), F32),
                        pltpu.VMEM((SSM_CHUNK + HALO, SSM_CONV_DIM), F32),
                        pltpu.VMEM((SSM_CONV, 8, SSM_CONV_DIM), F32), pltpu.VMEM((8, SSM_CONV_DIM), F32)],
        compiler_params=_params("arbitrary"), name=name,
    )(xbc, xbc, xbc, dtb, acsb, dtr, acs_r, a_log.reshape(SSM_GROUPS, SSM_HPG, 1), d_lanes, hin, dact, y, proj,
      gate_w.reshape(1, SSM_D_INNER), pre, proj, proj, cw)


LANES = 128
ROPE_Q_CHUNKS = ATT_WIDTH // LANES
ROPE_K_CHUNKS = ATT_KV_WIDTH // LANES


def _rope_tables(positions):
    inv = ROPE_THETA ** (-jnp.arange(0, ROPE_DIM, 2, dtype=F32) / ROPE_DIM)
    ang = positions.astype(F32)[:, None] * inv
    cos, sin = jnp.cos(ang), jnp.sin(ang)
    l = positions.shape[0]
    rest = ATT_HEAD_DIM - ROPE_DIM
    ones, zeros = jnp.ones((l, rest), F32), jnp.zeros((l, rest), F32)
    z8 = jnp.zeros((l, ROPE_HALF), F32)
    cos_f = jnp.concatenate([cos, cos, ones], axis=1)
    sin_a = jnp.concatenate([-sin, z8, zeros], axis=1)
    sin_b = jnp.concatenate([z8, sin, zeros], axis=1)
    reps = LANES // ATT_HEAD_DIM
    return tuple(jnp.tile(t, (1, reps)) for t in (cos_f, sin_a, sin_b))


ATT_QKV4 = 3 * ATT_WIDTH


def _both_halves(chunk):
    lane = lax.broadcasted_iota(jnp.int32, (1, LANES), 1)
    swapped = pltpu.roll(chunk, ATT_HEAD_DIM, 1)
    return jnp.where(lane < ATT_HEAD_DIM, chunk, swapped), jnp.where(lane < ATT_HEAD_DIM, swapped, chunk)


def _rope_bwd(dq, dk4, dv4, dgate, tables, name):
    l = dq.shape[0]
    tl = _pick(l, (256, 128))

    def body(dq_ref, dk_ref, dv_ref, dg_ref, c_ref, sa_ref, sb_ref, o_ref):
        cos_f, sin_a, sin_b = c_ref[...], sa_ref[...], sb_ref[...]
        lane = lax.broadcasted_iota(jnp.int32, (1, LANES), 1)

        def unrope(t):
            return t * cos_f + pltpu.roll(t * sin_a, ROPE_HALF, 1) + pltpu.roll(t * sin_b, LANES - ROPE_HALF, 1)

        def head_total(ref, kvh):
            base = kvh * ATT_GQA * ATT_HEAD_DIM
            s = ref[:, base:base + LANES] + ref[:, base + LANES:base + 2 * LANES]
            return s + pltpu.roll(s, ATT_HEAD_DIM, 1)

        for k in range(ROPE_Q_CHUNKS):
            sl = slice(k * LANES, (k + 1) * LANES)
            o_ref[:, sl] = unrope(dq_ref[:, sl] * Q_SCALE).astype(o_ref.dtype)
        for k in range(ROPE_K_CHUNKS):
            dk = jnp.where(lane < ATT_HEAD_DIM, head_total(dk_ref, 2 * k), head_total(dk_ref, 2 * k + 1))
            dv = jnp.where(lane < ATT_HEAD_DIM, head_total(dv_ref, 2 * k), head_total(dv_ref, 2 * k + 1))
            o_ref[:, ATT_WIDTH + k * LANES:ATT_WIDTH + (k + 1) * LANES] = unrope(dk).astype(o_ref.dtype)
            at = ATT_WIDTH + ATT_KV_WIDTH + k * LANES
            o_ref[:, at:at + LANES] = dv.astype(o_ref.dtype)
        o_ref[:, ATT_QKV:ATT_IN_DIM] = dg_ref[...].astype(o_ref.dtype)

    tab = pl.BlockSpec((tl, LANES), lambda i: (i, 0))
    wide = pl.BlockSpec((tl, ATT_WIDTH), lambda i: (i, 0))
    return pl.pallas_call(
        body, grid=(l // tl,), in_specs=[wide, wide, wide, wide, tab, tab, tab],
        out_specs=pl.BlockSpec((tl, ATT_IN_DIM), lambda i: (i, 0)),
        out_shape=jax.ShapeDtypeStruct((l, ATT_IN_DIM), BF16), compiler_params=_params("parallel"), name=name,
    )(dq, dk4, dv4, dgate, *tables)


GATE_HALF = ATT_WIDTH // 2
GATE_COL_BLOCK = ATT_QKV // GATE_HALF


ATT_STACK = ATT_GQA * ATT_BLOCK
BLOCK_LOG2 = ATT_BLOCK.bit_length() - 1


def _stack_masks(n):
    ri = lax.broadcasted_iota(jnp.int32, (ATT_STACK, ATT_BLOCK), 0) & (ATT_BLOCK - 1)
    cj = lax.broadcasted_iota(jnp.int32, (ATT_STACK, ATT_BLOCK), 1)
    return (cj > ri) & (n > 0), cj <= ri


def _stack_sinks(sink_ref, kvh):
    blk = lax.shift_right_logical(lax.broadcasted_iota(jnp.int32, (ATT_STACK, 1), 0), BLOCK_LOG2)
    col = jnp.zeros((ATT_STACK, 1), F32)
    for r in range(ATT_GQA):
        col = jnp.where(blk == r, sink_ref[kvh * ATT_GQA + r], col)
    return col


def _stack_fold(stack):
    head_of_lane = lax.shift_right_logical(lax.broadcasted_iota(jnp.int32, (1, GP), 1), HEAD_DIM_LOG2)
    out = jnp.zeros((ATT_BLOCK, GP), F32)
    for r in range(ATT_GQA):
        out = jnp.where(head_of_lane == r, stack[r * ATT_BLOCK:(r + 1) * ATT_BLOCK], out)
    return out


def _attn_fwd(proj, tables, sinks, name):
    l = proj.shape[0]
    nb = l // ATT_BLOCK
    ring = 3

    def body(sink_ref, p_ref, c_ref, sa_ref, sb_ref, g0_ref, g1_ref, og_ref, o_ref, lse_ref, qkv_ref, ring_ref):
        s = pl.program_id(0)

        @pl.when(s == 0)
        def _():
            ring_ref[...] = jnp.zeros_like(ring_ref)

        n = s - 1
        cur = lax.rem(s + ring - 1, ring)
        prv = lax.rem(s + ring - 2, ring)
        q_ref = ring_ref.at[cur, :, 0:ATT_WIDTH]
        kc_ref = ring_ref.at[cur, :, ATT_WIDTH:2 * ATT_WIDTH]
        vc_ref = ring_ref.at[cur, :, 2 * ATT_WIDTH:3 * ATT_WIDTH]
        kp_ref = ring_ref.at[prv, :, ATT_WIDTH:2 * ATT_WIDTH]
        vp_ref = ring_ref.at[prv, :, 2 * ATT_WIDTH:3 * ATT_WIDTH]
        mask_p, mask_c = _stack_masks(n)
        ones = jnp.ones((ATT_BLOCK, LANES), BF16)
        for kvh in range(ATT_KV_HEADS):
            cols = slice(kvh * GP, (kvh + 1) * GP)
            q_stack = _head_masked_rows(q_ref[:, cols], BF16)
            sp = jnp.where(mask_p, lax.dot_general(q_stack, kp_ref[:, cols], NT_DIMS, preferred_element_type=F32), NEG_INF)
            sc = jnp.where(mask_c, lax.dot_general(q_stack, kc_ref[:, cols], NT_DIMS, preferred_element_type=F32), NEG_INF)
            sink = _stack_sinks(sink_ref, kvh)
            m = jnp.maximum(jnp.max(jnp.maximum(sp, sc), axis=1, keepdims=True), sink)
            pp = jnp.exp(sp - m).astype(BF16)
            pc = jnp.exp(sc - m).astype(BF16)
            acc = (jnp.dot(pp, jnp.concatenate([vp_ref[:, cols], ones], axis=1), preferred_element_type=F32)
                   + jnp.dot(pc, jnp.concatenate([vc_ref[:, cols], ones], axis=1), preferred_element_type=F32))
            den = acc[:, GP:] + jnp.exp(sink - m)
            inv = 1.0 / den
            o_ref[:, cols] = _stack_fold(acc[:, :GP] * jnp.concatenate([inv, inv], axis=1))
            lse = m + jnp.log(den)
            lse_ref[:, cols] = _stack_fold(jnp.concatenate([lse, lse], axis=1))
        for half, g_ref in enumerate((g0_ref, g1_ref)):
            sl = slice(half * GATE_HALF, (half + 1) * GATE_HALF)
            gate = g_ref[...]
            og_ref[:, sl] = (o_ref[:, sl] * (gate * _sigmoid(gate))).astype(og_ref.dtype)

        slot = lax.rem(s, ring)
        cos_f, sin_a, sin_b = c_ref[...], sa_ref[...], sb_ref[...]

        def rope(t):
            return t * cos_f + pltpu.roll(t, LANES - ROPE_HALF, 1) * sin_a + pltpu.roll(t, ROPE_HALF, 1) * sin_b

        def put(dst, value):
            qkv_ref[:, dst:dst + LANES] = value
            ring_ref[slot, :, dst:dst + LANES] = value

        for k in range(ROPE_Q_CHUNKS):
            put(k * LANES, (rope(p_ref[:, k * LANES:(k + 1) * LANES]) * Q_SCALE).astype(BF16))
        for part in range(2):
            for k in range(ROPE_K_CHUNKS):
                src = ATT_WIDTH + part * ATT_KV_WIDTH + k * LANES
                t = p_ref[:, src:src + LANES]
                if part == 0:
                    t = rope(t)
                for head, dup in enumerate(_both_halves(t.astype(BF16))):
                    dst = (1 + part) * ATT_WIDTH + (2 * k + head) * ATT_GQA * ATT_HEAD_DIM
                    put(dst, dup)
                    put(dst + LANES, dup)

    def rope_at(s):
        return jnp.minimum(s, nb - 1)

    def attend_at(s):
        return jnp.maximum(s - 1, 0)

    wide = pl.BlockSpec((ATT_BLOCK, ATT_WIDTH), lambda s: (attend_at(s), 0))
    tab = pl.BlockSpec((ATT_BLOCK, LANES), lambda s: (rope_at(s), 0))
    return pl.pallas_call(
        body, grid=(nb + 1,),
        in_specs=[pl.BlockSpec(memory_space=pltpu.SMEM),
                  pl.BlockSpec((ATT_BLOCK, ATT_IN_DIM), lambda s: (rope_at(s), 0)), tab, tab, tab,
                  pl.BlockSpec((ATT_BLOCK, GATE_HALF), lambda s: (attend_at(s), GATE_COL_BLOCK)),
                  pl.BlockSpec((ATT_BLOCK, GATE_HALF), lambda s: (attend_at(s), GATE_COL_BLOCK + 1))],
        out_specs=[wide, wide, wide, pl.BlockSpec((ATT_BLOCK, ATT_QKV4), lambda s: (rope_at(s), 0))],
        out_shape=[jax.ShapeDtypeStruct((l, ATT_WIDTH), BF16), jax.ShapeDtypeStruct((l, ATT_WIDTH), F32),
                   jax.ShapeDtypeStruct((l, ATT_WIDTH), F32), jax.ShapeDtypeStruct((l, ATT_QKV4), BF16)],
        scratch_shapes=[pltpu.VMEM((ring, ATT_BLOCK, ATT_QKV4), BF16)],
        compiler_params=_params("arbitrary"), name=name,
    )(sinks, proj, *tables, proj, proj)


def _attn_bwd(qkv, proj, sinks, o, lse, dog, name, ride=()):
    l = qkv.shape[0]
    nb = l // ATT_BLOCK
    n_ride = len(ride)

    def body(*refs):
        sink_ref, q_ref, kp_ref, kc_ref, vp_ref, vc_ref, g0_ref, g1_ref, o_ref, lse_ref, dog_ref = refs[:11]
        ride_in = refs[11:11 + n_ride]
        dq_ref, dk_ref, dv_ref, dg_ref, ds_ref = refs[11 + n_ride:16 + n_ride]
        ride_out = refs[16 + n_ride:16 + 2 * n_ride]
        ck_ref, cv_ref, do_ref = refs[16 + 2 * n_ride:19 + 2 * n_ride]
        ride_sems = refs[19 + 2 * n_ride:]
        n = pl.program_id(0)

        @pl.when(n == 0)
        def _():
            ds_ref[...] = jnp.zeros_like(ds_ref)
            ck_ref[...] = jnp.zeros_like(ck_ref)
            cv_ref[...] = jnp.zeros_like(cv_ref)
            if n_ride:
                _scatter_between_chips(ride_in, ride_out, *ride_sems, wait=False)

        @pl.when(n == nb)
        def _():
            dk_ref[...] = ck_ref[...]
            dv_ref[...] = cv_ref[...]
            if n_ride:
                _scatter_between_chips(ride_in, ride_out, *ride_sems, wait=True)

        @pl.when(n < nb)
        def _():
            mask_p, mask_c = _stack_masks(n)
            lane = lax.broadcasted_iota(jnp.int32, (1, ATT_Q_HEADS), 1)
            for half, g_ref in enumerate((g0_ref, g1_ref)):
                sl = slice(half * GATE_HALF, (half + 1) * GATE_HALF)
                gate = g_ref[...]
                s = _sigmoid(gate)
                dogv = dog_ref[:, sl]
                do_ref[:, sl] = dogv * (gate * s)
                dg_ref[:, sl] = dogv * o_ref[:, sl] * (s * (1.0 + gate * (1.0 - s)))
            ds_acc = jnp.zeros((1, ATT_Q_HEADS), F32)
            for kvh in range(ATT_KV_HEADS):
                cols = slice(kvh * GP, (kvh + 1) * GP)
                kp, kc, vp, vc = kp_ref[:, cols], kc_ref[:, cols], vp_ref[:, cols], vc_ref[:, cols]
                q_stack = _head_masked_rows(q_ref[:, cols], BF16)
                do_g = do_ref[:, cols]
                do_stack = _head_masked_rows(do_g, BF16)
                lse_g = lse_ref[:, cols]
                lse_stack = jnp.concatenate(
                    [_both_halves(lse_g[:, (r // 2) * LANES:(r // 2 + 1) * LANES])[r % 2] for r in range(ATT_GQA)], axis=0)
                pp = jnp.exp(jnp.where(
                    mask_p, lax.dot_general(q_stack, kp, NT_DIMS, preferred_element_type=F32) - lse_stack, NEG_INF))
                pc = jnp.exp(jnp.where(
                    mask_c, lax.dot_general(q_stack, kc, NT_DIMS, preferred_element_type=F32) - lse_stack, NEG_INF))
                dpp = lax.dot_general(do_stack, vp, NT_DIMS, preferred_element_type=F32)
                dpc = lax.dot_general(do_stack, vc, NT_DIMS, preferred_element_type=F32)
                delta = jnp.sum(pp * dpp + pc * dpc, axis=1, keepdims=True)
                dsp = (pp * (dpp - delta)).astype(BF16)
                dsc = (pc * (dpc - delta)).astype(BF16)
                dq_ref[:, cols] = _stack_fold(jnp.dot(dsp, kp, preferred_element_type=F32)
                                              + jnp.dot(dsc, kc, preferred_element_type=F32))
                dk_ref[:, cols] = ck_ref[:, cols] + lax.dot_general(dsp, q_stack, TN_DIMS, preferred_element_type=F32)
                dv_ref[:, cols] = cv_ref[:, cols] + lax.dot_general(pp.astype(BF16), do_stack, TN_DIMS,
                                                                    preferred_element_type=F32)
                ck_ref[:, cols] = lax.dot_general(dsc, q_stack, TN_DIMS, preferred_element_type=F32)
                cv_ref[:, cols] = lax.dot_general(pc.astype(BF16), do_stack, TN_DIMS, preferred_element_type=F32)
                t = jnp.exp(_stack_sinks(sink_ref, kvh) - lse_stack) * delta
                for r in range(ATT_GQA):
                    tot = jnp.sum(t[r * ATT_BLOCK:(r + 1) * ATT_BLOCK], axis=0, keepdims=True)
                    ds_acc = ds_acc - jnp.where(lane == kvh * ATT_GQA + r, tot[:, :ATT_Q_HEADS], 0.0)
            ds_ref[...] += ds_acc

    def cur(n):
        return jnp.minimum(n, nb - 1)

    def prev(n):
        return jnp.maximum(n - 1, 0)

    wide = pl.BlockSpec((ATT_BLOCK, ATT_WIDTH), lambda n: (cur(n), 0))
    late = pl.BlockSpec((ATT_BLOCK, ATT_WIDTH), lambda n: (prev(n), 0))
    return pl.pallas_call(
        body, grid=(nb + 1,),
        in_specs=[pl.BlockSpec(memory_space=pltpu.SMEM), wide,
                  pl.BlockSpec((ATT_BLOCK, ATT_WIDTH), lambda n: (prev(cur(n)), 1)),
                  pl.BlockSpec((ATT_BLOCK, ATT_WIDTH), lambda n: (cur(n), 1)),
                  pl.BlockSpec((ATT_BLOCK, ATT_WIDTH), lambda n: (prev(cur(n)), 2)),
                  pl.BlockSpec((ATT_BLOCK, ATT_WIDTH), lambda n: (cur(n), 2)),
                  pl.BlockSpec((ATT_BLOCK, GATE_HALF), lambda n: (cur(n), GATE_COL_BLOCK)),
                  pl.BlockSpec((ATT_BLOCK, GATE_HALF), lambda n: (cur(n), GATE_COL_BLOCK + 1)),
                  wide, wide, wide] + [ANY] * n_ride,
        out_specs=[wide, late, late, wide, pl.BlockSpec((1, ATT_Q_HEADS), lambda n: (0, 0))] + [ANY] * n_ride,
        out_shape=[jax.ShapeDtypeStruct((l, ATT_WIDTH), F32), jax.ShapeDtypeStruct((l, ATT_WIDTH), F32),
                   jax.ShapeDtypeStruct((l, ATT_WIDTH), F32), jax.ShapeDtypeStruct((l, ATT_WIDTH), F32),
                   jax.ShapeDtypeStruct((1, ATT_Q_HEADS), F32)] + _scatter_shapes(ride),
        scratch_shapes=[pltpu.VMEM((ATT_BLOCK, ATT_WIDTH), F32), pltpu.VMEM((ATT_BLOCK, ATT_WIDTH), F32),
                        pltpu.VMEM((ATT_BLOCK, ATT_WIDTH), F32)] + (_gather_sems(n_ride) if n_ride else []),
        compiler_params=_params("arbitrary"), name=name,
    )(sinks, qkv, qkv, qkv, qkv, qkv, proj, proj, o, lse, dog, *ride)


def _local_step(x, positions, pre_norm, post_norm, conv_b, dt_bias, a_log, d_skip, gate_norm, sinks, target,
                first_in, in_proj_with_first_pair, scan_with_second_pair, attn_bwd_with_second_pair_grads,
                in_dx_with_first_pair_grads):
    tables = _rope_tables(positions)
    dt_bias_pad = jnp.pad(dt_bias, ((0, 0), (0, SSM_DT_PAD - SSM_HEADS)))
    d_lanes = jnp.repeat(d_skip, SSM_HEAD_DIM, axis=1).reshape(-1, SSM_GROUPS, 1, GP)
    a_log_pad = jnp.pad(a_log, ((0, 0), (0, SSM_DT_PAD - SSM_HEADS)))
    pairs = [first_in, None]
    saved = []
    cur = x
    h = _rmsnorm_fwd(cur, pre_norm[0], "prenorm_fwd_0")
    for i in range(DEPTH):
        j = i // 2
        if i % 2 == 0:
            in_proj = functools.partial(_matmul, h, pairs[j]["ssm_w_in"], "nn", F32, f"ssm_in_{i}")
            if i == 0:
                proj, rest = in_proj_with_first_pair(in_proj)
                pairs[0] = {**first_in, **rest}
            else:
                proj = in_proj()
            scan = functools.partial(_ssd_fwd, proj, pairs[j]["ssm_conv_w"], conv_b[j], dt_bias_pad[j:j + 1],
                                     a_log_pad[j:j + 1], d_lanes[j], gate_norm[j], f"ssd_fwd_{i}")
            if i == 0:
                *scanned, pairs[1] = scan_with_second_pair(scan)
            else:
                scanned = scan()
            y, act, hin, pre, xbc, dtb, acsb, dtr, acs_r = scanned
            w_ssm_in = [p["ssm_w_in"] for p in pairs]
            w_ssm_out = [p["ssm_w_out"] for p in pairs]
            w_att_in = [p["att_w_in"] for p in pairs]
            w_att_out = [p["att_w_out"] for p in pairs]
            conv_w = [p["ssm_conv_w"] for p in pairs]
            ymix = _matmul(act, w_ssm_out[j], "nn", F32, f"ssm_out_{i}")
            saved.append(dict(x=cur, h=h, proj=proj, pre=pre, xbc=xbc, dtb=dtb, acsb=acsb, dtr=dtr, acs_r=acs_r, y=y,
                              hin=hin, act=act, ymix=ymix))
        else:
            proj = _matmul(h, w_att_in[j], "nn", F32, f"att_in_{i}")
            act, o, lse, qkv = _attn_fwd(proj, tables, sinks[j], f"attn_fwd_{i}")
            ymix = _matmul(act, w_att_out[j], "nn", F32, f"att_out_{i}")
            saved.append(dict(x=cur, h=h, proj=proj, qkv=qkv, o=o, lse=lse, act=act, ymix=ymix))
        if i + 1 < DEPTH:
            cur, h = _post_fwd(cur, ymix, post_norm[i], pre_norm[i + 1], f"post_fwd_{i}")

    gr = {k: [None] * 2 for k in ("ssm_w_in", "ssm_conv_w", "ssm_conv_b", "ssm_dt_bias", "ssm_a_log", "ssm_d",
                                  "ssm_gate_norm", "ssm_w_out", "att_w_in", "att_sinks", "att_w_out")}
    gr["pre_norm"] = [None] * DEPTH
    gr["post_norm"] = [None] * DEPTH
    last = DEPTH - 1
    g, dymix, loss_lanes, gr["post_norm"][last] = _post_loss(cur, ymix, post_norm[last], target, "post_loss")
    for i in reversed(range(DEPTH)):
        j = i // 2
        s = saved[i]
        if i % 2 == 0:
            dact = _matmul(dymix, w_ssm_out[j], "nt", F32, f"ssm_out_dx_{i}")
            gr["ssm_w_out"][j] = _matmul(s["act"], dymix, "tn", F32, f"ssm_out_dw_{i}")
            dproj, ddt8, dal, dd, gr["ssm_gate_norm"][j], gr["ssm_conv_w"][j], dcb = _ssd_bwd(
                s["xbc"], s["pre"], conv_w[j], s["dtb"], s["acsb"], s["dtr"], s["acs_r"], a_log[j], d_lanes[j], s["hin"],
                dact, s["y"], s["proj"], gate_norm[j], f"ssd_bwd_{i}")
            gr["ssm_conv_b"][j] = dcb[0]
            gr["ssm_a_log"][j] = dal.reshape(SSM_HEADS)
            gr["ssm_d"][j] = dd.reshape(SSM_HEADS)
            l = x.shape[0]
            ddt = jnp.pad(jnp.transpose(ddt8, (2, 0, 1)).reshape(l, SSM_HEADS), ((0, 0), (0, SSM_DT_PAD - SSM_HEADS)))
            dproj, dbias = _dt_bwd(ddt, s["proj"], dt_bias_pad[j:j + 1], dproj, f"dt_bwd_{i}")
            gr["ssm_dt_bias"][j] = dbias[0, :SSM_HEADS]
            w_in, key = w_ssm_in[j], "ssm_w_in"
        else:
            dog = _matmul(dymix, w_att_out[j], "nt", F32, f"att_out_dx_{i}")
            gr["att_w_out"][j] = _matmul(s["act"], dymix, "tn", F32, f"att_out_dw_{i}")
            attn_bwd = functools.partial(_attn_bwd, s["qkv"], s["proj"], sinks[j], s["o"], s["lse"], dog, f"attn_bwd_{i}")
            if i == 1:
                (dq, dk, dv, dgate, dsk), second_pair_reduced = attn_bwd_with_second_pair_grads(
                    attn_bwd, {k: gr[k][1] for k in BIG})
            else:
                dq, dk, dv, dgate, dsk = attn_bwd()
            gr["att_sinks"][j] = dsk[0]
            dproj = _rope_bwd(dq, dk, dv, dgate, tables, f"rope_bwd_{i}")
            w_in, key = w_att_in[j], "att_w_in"
        gr[key][j] = _matmul(s["h"], dproj, "tn", F32, f"in_dw_{i}")
        in_dx = functools.partial(_matmul, dproj, w_in, "nt", F32, f"in_dx_{i}")
        if i == 0:
            dh, first_pair_reduced = in_dx_with_first_pair_grads(in_dx, {k: gr[k][0] for k in BIG})
        else:
            dh = in_dx()
        if i > 0:
            g, dymix, gr["pre_norm"][i], gr["post_norm"][i - 1] = _norm_bwd_chain(
                dh, s["x"], pre_norm[i], g, saved[i - 1]["ymix"], post_norm[i - 1], f"norm_bwd_{i}")
        else:
            g, gr["pre_norm"][i] = _rmsnorm_bwd(dh, s["x"], pre_norm[i], g, f"prenorm_bwd_{i}")
    grads = {k: jnp.stack([v.reshape(v.shape[-1]) if k in ("pre_norm", "post_norm", "ssm_gate_norm") else v for v in vs])
             for k, vs in gr.items() if k not in BIG}
    return loss_lanes, g, grads, first_pair_reduced, second_pair_reduced


N_CHIPS = 4
N_DEV = 8
MESH = pl.DeviceIdType.MESH
ANY = pl.BlockSpec(memory_space=pl.ANY)


def _place():
    x, y, c = lax.axis_index("x"), lax.axis_index("y"), lax.axis_index("c")
    return x, y, c, 2 * x + y


def _gather_sems(n):
    return [pltpu.SemaphoreType.DMA((n, N_CHIPS)), pltpu.SemaphoreType.DMA((n, N_CHIPS)), pltpu.SemaphoreType.DMA((n,))]


def _gather_between_chips(ins, outs, send_sems, recv_sems, local_sems, wait):
    n = len(ins)
    _, _, c, s = _place()
    local = [pltpu.make_async_copy(ins[w], outs[w].at[s], local_sems.at[w]) for w in range(n)]

    def remote(w, t):
        return pltpu.make_async_remote_copy(
            src_ref=ins[w].at[c], dst_ref=outs[w].at[s, c], send_sem=send_sems.at[w, t],
            recv_sem=recv_sems.at[w, s], device_id=(t // 2, t % 2, c), device_id_type=MESH)

    def arrival(w, t):
        return pltpu.make_async_remote_copy(
            src_ref=ins[w].at[c], dst_ref=outs[w].at[t, c], send_sem=send_sems.at[w, t],
            recv_sem=recv_sems.at[w, t], device_id=(t // 2, t % 2, c), device_id_type=MESH)

    if not wait:
        for cp in local:
            cp.start()
    for t in range(N_CHIPS):
        @pl.when(s != t)
        def _():
            for w in range(n):
                if wait:
                    remote(w, t).wait_send()
                    arrival(w, t).wait_recv()
                else:
                    remote(w, t).start()
    if wait:
        for cp in local:
            cp.wait()


def _pair_handoff(bufs, name):
    n = len(bufs)

    def body(*refs):
        outs = refs[n:2 * n]
        send_sems, recv_sems = refs[2 * n:]
        x, y, c, s = _place()

        def handed_on(w, t):
            return pltpu.make_async_remote_copy(
                src_ref=outs[w].at[t, c], dst_ref=outs[w].at[t, c], send_sem=send_sems.at[w, t],
                recv_sem=recv_sems.at[w, t], device_id=(x, y, 1 - c), device_id_type=MESH)

        def handed_in(w, t):
            return pltpu.make_async_remote_copy(
                src_ref=outs[w].at[t, 1 - c], dst_ref=outs[w].at[t, 1 - c], send_sem=send_sems.at[w, t],
                recv_sem=recv_sems.at[w, t], device_id=(x, y, 1 - c), device_id_type=MESH)

        for t in range(N_CHIPS):
            @pl.when(s != t)
            def _():
                for w in range(n):
                    handed_on(w, t).start()
        for t in range(N_CHIPS):
            @pl.when(s != t)
            def _():
                for w in range(n):
                    handed_on(w, t).wait_send()
                    handed_in(w, t).wait_recv()

    return pl.pallas_call(
        body, in_specs=[ANY] * n, out_specs=[ANY] * n,
        out_shape=[jax.ShapeDtypeStruct(a.shape, a.dtype) for a in bufs],
        scratch_shapes=[pltpu.SemaphoreType.DMA((n, N_CHIPS)), pltpu.SemaphoreType.DMA((n, N_CHIPS))],
        input_output_aliases={w: w for w in range(n)}, name=name,
    )(*bufs)


def _chip_gather(shards, name):
    n = len(shards)

    def body(*refs):
        ins, outs = refs[:n], refs[n:2 * n]
        _gather_between_chips(ins, outs, *refs[2 * n:], wait=False)
        _gather_between_chips(ins, outs, *refs[2 * n:], wait=True)

    bufs = pl.pallas_call(
        body, in_specs=[ANY] * n, out_specs=[ANY] * n,
        out_shape=[jax.ShapeDtypeStruct((N_CHIPS,) + a.shape, a.dtype) for a in shards],
        scratch_shapes=_gather_sems(n), name=name,
    )(*shards)
    return _pair_handoff(bufs, name + "_handoff")


def _pair_swap(parts, name):
    n = len(parts)

    def body(*refs):
        ins, outs = refs[:n], refs[n:2 * n]
        send_sems, recv_sems = refs[2 * n:]
        x, y, c, _ = _place()
        cps = [pltpu.make_async_remote_copy(
            src_ref=ins[w].at[1 - c], dst_ref=outs[w], send_sem=send_sems.at[w], recv_sem=recv_sems.at[w],
            device_id=(x, y, 1 - c), device_id_type=MESH) for w in range(n)]
        for cp in cps:
            cp.start()
        for cp in cps:
            cp.wait()

    return pl.pallas_call(
        body, in_specs=[ANY] * n, out_specs=[ANY] * n,
        out_shape=[jax.ShapeDtypeStruct(a.shape[1:], a.dtype) for a in parts],
        scratch_shapes=[pltpu.SemaphoreType.DMA((n,)), pltpu.SemaphoreType.DMA((n,))],
        name=name,
    )(*parts)


def _scatter_between_chips(ins, outs, send_sems, recv_sems, local_sems, wait):
    n = len(ins)
    _, _, c, s = _place()

    def block(w, t):
        rows = ins[w].shape[0] // N_CHIPS
        return ins[w].at[pl.ds(t * rows, rows)]

    local = [pltpu.make_async_copy(block(w, s), outs[w].at[s], local_sems.at[w]) for w in range(n)]

    def remote(w, t):
        return pltpu.make_async_remote_copy(
            src_ref=block(w, t), dst_ref=outs[w].at[s], send_sem=send_sems.at[w, t], recv_sem=recv_sems.at[w, s],
            device_id=(t // 2, t % 2, c), device_id_type=MESH)

    def arrival(w, t):
        return pltpu.make_async_remote_copy(
            src_ref=block(w, t), dst_ref=outs[w].at[t], send_sem=send_sems.at[w, t], recv_sem=recv_sems.at[w, t],
            device_id=(t // 2, t % 2, c), device_id_type=MESH)

    if not wait:
        for cp in local:
            cp.start()
    for t in range(N_CHIPS):
        @pl.when(s != t)
        def _():
            for w in range(n):
                if wait:
                    remote(w, t).wait_send()
                    arrival(w, t).wait_recv()
                else:
                    remote(w, t).start()
    if wait:
        for cp in local:
            cp.wait()


def _scatter_shapes(parts):
    return [jax.ShapeDtypeStruct((N_CHIPS, a.shape[0] // N_CHIPS, a.shape[1]), a.dtype) for a in parts]


def _pair_merge(parts, name):
    n = len(parts)

    def body(*refs):
        ins, outs = refs[:n], refs[n:2 * n]
        send_sems, recv_sems = refs[2 * n:]
        x, y, c, _ = _place()
        cps = [pltpu.make_async_remote_copy(
            src_ref=ins[w], dst_ref=outs[w], send_sem=send_sems.at[w], recv_sem=recv_sems.at[w],
            device_id=(x, y, 1 - c), device_id_type=MESH) for w in range(n)]
        for cp in cps:
            cp.start()
        for cp in cps:
            cp.wait()

    return pl.pallas_call(
        body, in_specs=[ANY] * n, out_specs=[ANY] * n,
        out_shape=[jax.ShapeDtypeStruct(a.shape, a.dtype) for a in parts],
        scratch_shapes=[pltpu.SemaphoreType.DMA((n,)), pltpu.SemaphoreType.DMA((n,))],
        name=name,
    )(*parts)


def _all_gather_small(a, name):
    def body(in_ref, out_ref, send_sems, recv_sems, local_sem):
        x, y, c, _ = _place()
        me = 4 * x + 2 * y + c
        local = pltpu.make_async_copy(in_ref, out_ref.at[me], local_sem)
        local.start()

        def remote(d):
            return pltpu.make_async_remote_copy(
                src_ref=in_ref, dst_ref=out_ref.at[me], send_sem=send_sems.at[d], recv_sem=recv_sems.at[me],
                device_id=(d // 4, (d // 2) % 2, d % 2), device_id_type=MESH)

        def arrival(d):
            return pltpu.make_async_remote_copy(
                src_ref=in_ref, dst_ref=out_ref.at[d], send_sem=send_sems.at[d], recv_sem=recv_sems.at[d],
                device_id=(d // 4, (d // 2) % 2, d % 2), device_id_type=MESH)

        for d in range(N_DEV):
            @pl.when(me != d)
            def _():
                remote(d).start()
        for d in range(N_DEV):
            @pl.when(me != d)
            def _():
                remote(d).wait_send()
                arrival(d).wait_recv()
        local.wait()

    return pl.pallas_call(
        body, in_specs=[ANY], out_specs=ANY, out_shape=jax.ShapeDtypeStruct((N_DEV,) + a.shape, a.dtype),
        scratch_shapes=[pltpu.SemaphoreType.DMA((N_DEV,)), pltpu.SemaphoreType.DMA((N_DEV,)), pltpu.SemaphoreType.DMA],
        name=name,
    )(a)


def _reduce_tile(rows):
    return _pick(rows, (256, 128, 16))


def _pair_add(full, other, layer, name):
    _, rows, cols = full.shape
    tr = _reduce_tile(rows)

    def body(layer_ref, a_ref, b_ref, o_ref):
        o_ref[...] = (a_ref[0] + b_ref[...]).astype(o_ref.dtype)

    return pl.pallas_call(
        body,
        grid_spec=pltpu.PrefetchScalarGridSpec(
            num_scalar_prefetch=1, grid=(rows // tr,),
            in_specs=[pl.BlockSpec((1, tr, cols), lambda i, lr: (lr[0], i, 0)), pl.BlockSpec((tr, cols), lambda i, lr: (i, 0))],
            out_specs=pl.BlockSpec((tr, cols), lambda i, lr: (i, 0))),
        out_shape=jax.ShapeDtypeStruct((rows, cols), BF16), compiler_params=_params("parallel"), name=name,
    )(layer, full, other)


def _sum_slots(a, name):
    n, rows, cols = a.shape
    tr = _reduce_tile(rows)

    def body(a_ref, o_ref):
        acc = a_ref[0].astype(F32)
        for k in range(1, n):
            acc = acc + a_ref[k].astype(F32)
        o_ref[...] = acc

    return pl.pallas_call(
        body, grid=(rows // tr,), in_specs=[pl.BlockSpec((n, tr, cols), lambda i: (0, i, 0))],
        out_specs=pl.BlockSpec((tr, cols), lambda i: (i, 0)),
        out_shape=jax.ShapeDtypeStruct((rows, cols), F32), compiler_params=_params("parallel"), name=name,
    )(a)


def _adamw(w, g, m, v, name):
    rows, cols = w.shape
    tr = _pick(rows, (256, 8))

    def body(w_ref, g_ref, m_ref, v_ref, d_ref, nm_ref, nv_ref):
        gv = g_ref[...]
        mn = ADAM_B1 * m_ref[...] + (1.0 - ADAM_B1) * gv
        vn = ADAM_B2 * v_ref[...] + (1.0 - ADAM_B2) * jnp.square(gv)
        m_hat = mn / (1.0 - ADAM_B1 ** ADAM_STEP)
        v_hat = vn / (1.0 - ADAM_B2 ** ADAM_STEP)
        d_ref[...] = -ADAM_LR * (m_hat / (jnp.sqrt(v_hat) + ADAM_EPS) + ADAM_WD * w_ref[...])
        nm_ref[...] = mn
        nv_ref[...] = vn

    blk = pl.BlockSpec((tr, cols), lambda i: (i, 0))
    return pl.pallas_call(
        body, grid=(rows // tr,), in_specs=[blk] * 4, out_specs=[blk] * 3,
        out_shape=[jax.ShapeDtypeStruct((rows, cols), F32)] * 3, compiler_params=_params("parallel"), name=name,
    )(w, g, m, v)


BIG = ("ssm_w_in", "ssm_w_out", "att_w_in", "att_w_out")
SHARDED = BIG + ("ssm_conv_w",)
SMALL = ("pre_norm", "post_norm", "ssm_conv_b", "ssm_dt_bias", "ssm_a_log", "ssm_d", "ssm_gate_norm", "att_sinks")
WEIGHTS = ("pre_norm", "post_norm", "ssm_w_in", "ssm_conv_w", "ssm_conv_b", "ssm_dt_bias", "ssm_a_log", "ssm_d",
           "ssm_gate_norm", "ssm_w_out", "att_w_in", "att_sinks", "att_w_out")


def _halves(a):
    return a.reshape(2, a.shape[0] // 2, a.shape[1])


def _layer_shards(j, ssm_w_in, ssm_w_out, att_w_in, att_w_out, ssm_conv_w):
    return [_halves(ssm_w_in[j].astype(BF16)), _halves(ssm_w_out[j].astype(BF16)), _halves(att_w_in[j].astype(BF16)),
            _halves(att_w_out[j].astype(BF16)), _halves(ssm_conv_w[j])]


SHARD_KEYS = ("ssm_w_in", "ssm_w_out", "att_w_in", "att_w_out", "ssm_conv_w")


def _whole_weights(keys, gathered):
    out = {}
    for k, g in zip(keys, gathered):
        g = g.reshape((N_CHIPS, 2 * g.shape[2], g.shape[3]))
        if k in ("ssm_w_out", "att_w_out"):
            out[k] = g.reshape(N_CHIPS * g.shape[1], g.shape[2])
        else:
            out[k] = jnp.transpose(g, (1, 0, 2)).reshape(g.shape[1], N_CHIPS * g.shape[2])
    if "ssm_w_in" in out:
        out["ssm_w_in"] = jnp.pad(out["ssm_w_in"], ((0, 0), (0, SSM_IN_PAD - SSM_IN_DIM)))
    return out


def _halves_by_chip(key, g):
    if key in ("ssm_w_out", "att_w_out"):
        rows = g.shape[0] // N_CHIPS
        blocks = g.reshape(N_CHIPS, 2, rows // 2, g.shape[1])
        return jnp.transpose(blocks, (1, 0, 2, 3)).reshape(2, N_CHIPS * (rows // 2), g.shape[1])
    cols = (SSM_IN_DIM if key == "ssm_w_in" else g.shape[1]) // N_CHIPS
    rows = g.shape[0]
    blocks = g[:, :N_CHIPS * cols].reshape(2, rows // 2, N_CHIPS, cols)
    return jnp.transpose(blocks, (0, 2, 1, 3)).reshape(2, N_CHIPS * (rows // 2), cols)


def _pack_small(tree, keys):
    flat = jnp.concatenate([tree[k].reshape(-1) for k in keys])
    rows = -(-flat.shape[0] // (8 * LANES)) * 8
    return jnp.pad(flat, (0, rows * LANES - flat.shape[0])).reshape(rows, LANES)


def _unpack_small(packed, shapes, keys):
    flat = packed.reshape(-1)
    out, at = {}, 0
    for k in keys:
        n = 1
        for dim in shapes[k]:
            n *= dim
        out[k] = flat[at:at + n].reshape(shapes[k])
        at += n
    return out


def kernel(x, positions, pre_norm, post_norm, ssm_w_in, ssm_conv_w, ssm_conv_b, ssm_dt_bias, ssm_a_log, ssm_d, ssm_gate_norm, ssm_w_out, att_w_in, att_sinks, att_w_out, loss_target, m_pre_norm, m_post_norm, m_ssm_w_in, m_ssm_conv_w, m_ssm_conv_b, m_ssm_dt_bias, m_ssm_a_log, m_ssm_d, m_ssm_gate_norm, m_ssm_w_out, m_att_w_in, m_att_sinks, m_att_w_out, v_pre_norm, v_post_norm, v_ssm_w_in, v_ssm_conv_w, v_ssm_conv_b, v_ssm_dt_bias, v_ssm_a_log, v_ssm_d, v_ssm_gate_norm, v_ssm_w_out, v_att_w_in, v_att_sinks, v_att_w_out):
    w = dict(pre_norm=pre_norm, post_norm=post_norm, ssm_w_in=ssm_w_in, ssm_conv_w=ssm_conv_w, ssm_conv_b=ssm_conv_b,
             ssm_dt_bias=ssm_dt_bias, ssm_a_log=ssm_a_log, ssm_d=ssm_d, ssm_gate_norm=ssm_gate_norm, ssm_w_out=ssm_w_out,
             att_w_in=att_w_in, att_sinks=att_sinks, att_w_out=att_w_out)
    m = dict(pre_norm=m_pre_norm, post_norm=m_post_norm, ssm_w_in=m_ssm_w_in, ssm_conv_w=m_ssm_conv_w, ssm_conv_b=m_ssm_conv_b,
             ssm_dt_bias=m_ssm_dt_bias, ssm_a_log=m_ssm_a_log, ssm_d=m_ssm_d, ssm_gate_norm=m_ssm_gate_norm,
             ssm_w_out=m_ssm_w_out, att_w_in=m_att_w_in, att_sinks=m_att_sinks, att_w_out=m_att_w_out)
    v = dict(pre_norm=v_pre_norm, post_norm=v_post_norm, ssm_w_in=v_ssm_w_in, ssm_conv_w=v_ssm_conv_w, ssm_conv_b=v_ssm_conv_b,
             ssm_dt_bias=v_ssm_dt_bias, ssm_a_log=v_ssm_a_log, ssm_d=v_ssm_d, ssm_gate_norm=v_ssm_gate_norm,
             ssm_w_out=v_ssm_w_out, att_w_in=v_att_w_in, att_sinks=v_att_sinks, att_w_out=v_att_w_out)
    c = lax.axis_index("c")
    chip = 2 * lax.axis_index("x") + lax.axis_index("y")

    sharded = (ssm_w_in, ssm_w_out, att_w_in, att_w_out, ssm_conv_w)
    own = [dict(zip(SHARD_KEYS, _layer_shards(j, *sharded))) for j in range(2)]
    now_keys = ("ssm_w_in", "ssm_conv_w")
    later_keys = ("ssm_w_out", "att_w_in", "att_w_out")
    first_in = _whole_weights(now_keys, _chip_gather([own[0][k] for k in now_keys], "gather_weights_0"))

    def in_proj_with_first_pair(matmul):
        proj, *arrived = matmul(ride=[own[0][k] for k in later_keys])
        return proj, _whole_weights(later_keys, _pair_handoff(arrived, "gather_weights_0_rest_handoff"))

    def scan_with_second_pair(scan):
        results = scan(ride=[own[1][k] for k in SHARD_KEYS])
        scanned, arrived = results[:-len(SHARD_KEYS)], results[-len(SHARD_KEYS):]
        return (*scanned, _whole_weights(SHARD_KEYS, _pair_handoff(arrived, "gather_weights_1_handoff")))

    half = jnp.reshape(c, (1,)).astype(jnp.int32)

    def reduce_begin(pair_grads, tag):
        parts = [_halves_by_chip(k, pair_grads[k]) for k in BIG]
        from_sibling = _pair_swap(parts, f"reduce_pair_swap_{tag}")
        return [_pair_add(p, o, half, f"reduce_pair_add_{tag}_{n}") for n, (p, o) in enumerate(zip(parts, from_sibling))]

    def reduce_end(by_chip, tag):
        mine = [_sum_slots(a, f"reduce_chip_sum_{tag}_{n}") for n, a in enumerate(by_chip)]
        theirs = _pair_merge(mine, f"reduce_pair_merge_{tag}")
        return {k: jnp.where(c == 0, jnp.concatenate([a, b]), jnp.concatenate([b, a])) for k, a, b in zip(BIG, mine, theirs)}

    def attn_bwd_with_second_pair_grads(attn_bwd, pair_grads):
        dq, dk, dv, dgate, dsk, *by_chip = attn_bwd(ride=reduce_begin(pair_grads, "1"))
        return (dq, dk, dv, dgate, dsk), reduce_end(by_chip, "1")

    def in_dx_with_first_pair_grads(matmul, pair_grads):
        dh, *by_chip = matmul(ride=reduce_begin(pair_grads, "0"), ride_scatters=True)
        return dh, reduce_end(by_chip, "0")

    loss_lanes, grad_x, gr, reduced_0, reduced_1 = _local_step(
        x[0], positions[0], pre_norm, post_norm, ssm_conv_b, ssm_dt_bias, ssm_a_log, ssm_d, ssm_gate_norm, att_sinks,
        loss_target[0], first_in, in_proj_with_first_pair, scan_with_second_pair, attn_bwd_with_second_pair_grads,
        in_dx_with_first_pair_grads)
    loss = lax.psum(0.5 * jnp.sum(loss_lanes) / D_MODEL, ("x", "y", "c"))
    grads = {k: jnp.stack([reduced_0[k], reduced_1[k]]) for k in BIG}

    small_keys = SMALL + ("ssm_conv_w",)
    small_shapes = {k: w[k].shape for k in SMALL}
    small_shapes["ssm_conv_w"] = gr["ssm_conv_w"].shape
    small_sum = _sum_slots(_all_gather_small(_pack_small(gr, small_keys), "reduce_small_gather"), "reduce_small_sum")
    grads.update(_unpack_small(small_sum, small_shapes, small_keys))
    conv_cols = ssm_conv_w.shape[2]
    grads["ssm_conv_w"] = lax.dynamic_slice_in_dim(grads["ssm_conv_w"], chip * conv_cols, conv_cols, axis=2)

    delta, new_m, new_v = {}, {}, {}
    for k in SHARDED:
        shp = w[k].shape
        two_d = (shp[0] * shp[1], shp[2])
        d_, m_, v_ = _adamw(w[k].reshape(two_d), grads[k].reshape(two_d), m[k].reshape(two_d), v[k].reshape(two_d),
                            f"adamw_{k}")
        delta[k], new_m[k], new_v[k] = d_.reshape(shp), m_.reshape(shp), v_.reshape(shp)
    d_, m_, v_ = _adamw(_pack_small(w, SMALL), _pack_small(grads, SMALL), _pack_small(m, SMALL), _pack_small(v, SMALL),
                        "adamw_small")
    delta.update(_unpack_small(d_, small_shapes, SMALL))
    new_m.update(_unpack_small(m_, small_shapes, SMALL))
    new_v.update(_unpack_small(v_, small_shapes, SMALL))

    return (loss, grad_x[None], *[grads[k] for k in WEIGHTS], *[delta[k] for k in WEIGHTS],
            *[new_m[k] for k in WEIGHTS], *[new_v[k] for k in WEIGHTS])
```

```python
import functools

import jax
import jax.numpy as jnp
from jax import lax
from jax.experimental import pallas as pl
from jax.experimental.pallas import tpu as pltpu

F32 = jnp.float32
BF16 = jnp.bfloat16
EPS = 1e-6
NEG_INF = float("-inf")

D_MODEL = 1024
DEPTH = 4
SSM_D_INNER = 2048
SSM_HEAD_DIM = 64
SSM_HEADS = 32
SSM_GROUPS = 8
SSM_HPG = 4
SSM_STATE = 128
SSM_CONV = 4
SSM_CHUNK = 128
SSM_BC_DIM = 1024
SSM_CONV_DIM = 4096
SSM_IN_DIM = 6176
SSM_IN_PAD = 6272
SSM_DT_PAD = 128
ATT_HEAD_DIM = 64
ATT_Q_HEADS = 16
ATT_KV_HEADS = 4
ATT_GQA = 4
ATT_WIDTH = 1024
ATT_KV_WIDTH = 256
ATT_IN_DIM = 2560
ATT_QKV = ATT_WIDTH + 2 * ATT_KV_WIDTH
ATT_BLOCK = 128
ROPE_THETA = 500000.0
ROPE_DIM = 16
ROPE_HALF = 8
Q_SCALE = ATT_HEAD_DIM ** -0.5

ADAM_LR = 0.001
ADAM_B1 = 0.9
ADAM_B2 = 0.999
ADAM_EPS = 1e-08
ADAM_WD = 0.01
ADAM_STEP = 10

VMEM_LIMIT_BYTES = 48 * 1024 * 1024
NT_DIMS = (((1,), (1,)), ((), ()))
TN_DIMS = (((0,), (0,)), ((), ()))


def _params(*sem):
    return pltpu.CompilerParams(dimension_semantics=sem, vmem_limit_bytes=VMEM_LIMIT_BYTES)


def _pick(n, cands):
    for c in cands:
        if n % c == 0:
            return c
    return n


def _sigmoid(v):
    return 0.5 * jnp.tanh(0.5 * v) + 0.5


def _bdot_tn(a, b):
    return lax.dot_general(a.astype(BF16), b.astype(BF16), TN_DIMS, preferred_element_type=F32)


MATMUL_VMEM_BUDGET = 36 * 1024 * 1024


def _matmul_tiles(m, n, k, out_bytes, reduce_rows):
    best = None
    whole = [k] if (not reduce_rows or k <= 2048) else []
    for tk in whole + [c for c in (4096, 2048, 1024, 896, 512) if k % c == 0 and c < k]:
        for tm in (c for c in (2048, 1024, 512, 256) if m % c == 0):
            for tn in (c for c in (n, 1280, 1024, 896, 640, 512) if n % c == 0):
                acc = tm * tn * 4 if tk < k else 0
                need = 2 * (2 * tk * (tm + tn) + tm * tn * out_bytes) + acc
                if need <= MATMUL_VMEM_BUDGET and (best is None or tm * tn * min(tk, 2048) > best[0]):
                    best = (tm * tn * min(tk, 2048), tm, tn, tk)
        if best is not None and not reduce_rows:
            break
    return best[1:]


def _matmul(a, b, mode, out_dtype, name, ride=(), ride_scatters=False):
    if mode == "nn":
        (m, k), n = a.shape, b.shape[1]
    elif mode == "nt":
        (m, k), n = a.shape, b.shape[0]
    else:
        (k, m), n = a.shape, b.shape[1]
    tm, tn, tk = _matmul_tiles(m, n, k, jnp.dtype(out_dtype).itemsize, mode == "tn")
    nk = k // tk
    steps = (n // tn, m // tm, nk)
    dims = {"nn": (((1,), (0,)), ((), ())), "nt": NT_DIMS, "tn": TN_DIMS}[mode]
    n_ride = len(ride)
    exchange = _scatter_between_chips if ride_scatters else _gather_between_chips
    arrived = _scatter_shapes(ride) if ride_scatters else [jax.ShapeDtypeStruct((N_CHIPS,) + r.shape, r.dtype) for r in ride]

    def body(*refs):
        a_ref, b_ref = refs[:2]
        ride_in = refs[2:2 + n_ride]
        o_ref = refs[2 + n_ride]
        ride_out = refs[3 + n_ride:3 + 2 * n_ride]
        acc_ref = refs[3 + 2 * n_ride]
        ride_sems = refs[4 + 2 * n_ride:]
        kk = pl.program_id(2)
        at = [pl.program_id(d) for d in range(3)]
        if n_ride:
            @pl.when((at[0] == 0) & (at[1] == 0) & (at[2] == 0))
            def _():
                exchange(ride_in, ride_out, *ride_sems, wait=False)

        part = lax.dot_general(a_ref[...], b_ref[...], dims, preferred_element_type=F32)
        if nk == 1:
            o_ref[...] = part.astype(o_ref.dtype)
        else:
            @pl.when(kk == 0)
            def _():
                acc_ref[...] = part

            @pl.when(kk > 0)
            def _():
                acc_ref[...] += part

            @pl.when(kk == nk - 1)
            def _():
                o_ref[...] = acc_ref[...].astype(o_ref.dtype)

        if n_ride:
            @pl.when((at[0] == steps[0] - 1) & (at[1] == steps[1] - 1) & (at[2] == steps[2] - 1))
            def _():
                exchange(ride_in, ride_out, *ride_sems, wait=True)

    if mode == "nn":
        a_spec = pl.BlockSpec((tm, tk), lambda j, i, kk: (i, kk))
        b_spec = pl.BlockSpec((tk, tn), lambda j, i, kk: (kk, j))
    elif mode == "nt":
        a_spec = pl.BlockSpec((tm, tk), lambda j, i, kk: (i, kk))
        b_spec = pl.BlockSpec((tn, tk), lambda j, i, kk: (j, kk))
    else:
        a_spec = pl.BlockSpec((tk, tm), lambda j, i, kk: (kk, i))
        b_spec = pl.BlockSpec((tk, tn), lambda j, i, kk: (kk, j))
    out = pl.pallas_call(
        body, grid=steps, in_specs=[a_spec, b_spec] + [ANY] * n_ride,
        out_specs=[pl.BlockSpec((tm, tn), lambda j, i, kk: (i, j))] + [ANY] * n_ride,
        out_shape=[jax.ShapeDtypeStruct((m, n), out_dtype)] + arrived,
        scratch_shapes=[pltpu.VMEM((tm, tn), F32)] + (_gather_sems(n_ride) if n_ride else []),
        compiler_params=_params(*(["arbitrary"] * 3 if n_ride else ["parallel", "parallel", "arbitrary"])), name=name,
    )(a, b, *ride)
    return out if n_ride else out[0]


def _row_tile(l):
    return _pick(l, (512, 256, 128))


def _rmsnorm_fwd(x, w, name):
    l, d = x.shape
    tl = _row_tile(l)

    def body(x_ref, w_ref, o_ref):
        xv = x_ref[...]
        r = lax.rsqrt(jnp.mean(xv * xv, axis=-1, keepdims=True) + EPS)
        o_ref[...] = (xv * r * w_ref[...]).astype(o_ref.dtype)

    return pl.pallas_call(
        body, grid=(l // tl,),
        in_specs=[pl.BlockSpec((tl, d), lambda i: (i, 0)), pl.BlockSpec((1, d), lambda i: (0, 0))],
        out_specs=pl.BlockSpec((tl, d), lambda i: (i, 0)),
        out_shape=jax.ShapeDtypeStruct((l, d), BF16), compiler_params=_params("parallel"), name=name,
    )(x, w.reshape(1, d))


def _post_fwd(x, y, w, w_next, name):
    l, d = x.shape
    tl = _row_tile(l)

    def body(x_ref, y_ref, w_ref, wn_ref, o_ref, h_ref):
        yv = y_ref[...]
        r = lax.rsqrt(jnp.mean(yv * yv, axis=-1, keepdims=True) + EPS)
        out = x_ref[...] + yv * r * w_ref[...]
        o_ref[...] = out
        rn = lax.rsqrt(jnp.mean(out * out, axis=-1, keepdims=True) + EPS)
        h_ref[...] = (out * rn * wn_ref[...]).astype(h_ref.dtype)

    row = pl.BlockSpec((tl, d), lambda i: (i, 0))
    vec = pl.BlockSpec((1, d), lambda i: (0, 0))
    return pl.pallas_call(
        body, grid=(l // tl,), in_specs=[row, row, vec, vec], out_specs=[row, row],
        out_shape=[jax.ShapeDtypeStruct((l, d), F32), jax.ShapeDtypeStruct((l, d), BF16)],
        compiler_params=_params("parallel"), name=name,
    )(x, y, w.reshape(1, d), w_next.reshape(1, d))


def _post_loss(x, y, w, t, name):
    l, d = x.shape
    tl = _row_tile(l)
    nt = l // tl

    def body(x_ref, y_ref, w_ref, t_ref, g_ref, dy_ref, ls_ref, dw_ref, acc_ref):
        i = pl.program_id(0)

        @pl.when(i == 0)
        def _():
            ls_ref[...] = jnp.zeros_like(ls_ref)
            acc_ref[...] = jnp.zeros_like(acc_ref)

        yv = y_ref[...]
        r = lax.rsqrt(jnp.mean(yv * yv, axis=-1, keepdims=True) + EPS)
        nrm = yv * r
        e = x_ref[...] + nrm * w_ref[...] - t_ref[...]
        gv = e * (1.0 / d)
        g_ref[...] = gv
        ls_ref[...] += jnp.sum((e * e).reshape(tl // 8, 8, d), axis=0)
        gw = gv * w_ref[...]
        dy_ref[...] = (r * (gw - nrm * jnp.mean(gw * nrm, axis=-1, keepdims=True))).astype(dy_ref.dtype)
        acc_ref[...] += jnp.sum((gv * nrm).reshape(tl // 8, 8, d), axis=0)

        @pl.when(i == nt - 1)
        def _():
            dw_ref[...] = jnp.sum(acc_ref[...], axis=0, keepdims=True)

    row = pl.BlockSpec((tl, d), lambda i: (i, 0))
    vec = pl.BlockSpec((1, d), lambda i: (0, 0))
    return pl.pallas_call(
        body, grid=(nt,), in_specs=[row, row, vec, row],
        out_specs=[row, row, pl.BlockSpec((8, d), lambda i: (0, 0)), vec],
        out_shape=[jax.ShapeDtypeStruct((l, d), F32), jax.ShapeDtypeStruct((l, d), BF16),
                   jax.ShapeDtypeStruct((8, d), F32), jax.ShapeDtypeStruct((1, d), F32)],
        scratch_shapes=[pltpu.VMEM((8, d), F32)], compiler_params=_params("arbitrary"), name=name,
    )(x, y, w.reshape(1, d), t)


def _norm_bwd_chain(dh, x, w_pre, resid, y_prev, w_post_prev, name):
    l, d = x.shape
    tl = _row_tile(l)
    nt = l // tl

    def body(dh_ref, x_ref, wp_ref, r_ref, y_ref, wq_ref, g_ref, dy_ref, dwp_ref, dwq_ref, accp_ref, accq_ref):
        i = pl.program_id(0)

        @pl.when(i == 0)
        def _():
            accp_ref[...] = jnp.zeros_like(accp_ref)
            accq_ref[...] = jnp.zeros_like(accq_ref)

        xv = x_ref[...]
        dhv = dh_ref[...]
        rx = lax.rsqrt(jnp.mean(xv * xv, axis=-1, keepdims=True) + EPS)
        nx = xv * rx
        gw = dhv * wp_ref[...]
        gv = rx * (gw - nx * jnp.mean(gw * nx, axis=-1, keepdims=True)) + r_ref[...]
        g_ref[...] = gv
        accp_ref[...] += jnp.sum((dhv * nx).reshape(tl // 8, 8, d), axis=0)
        yv = y_ref[...]
        ry = lax.rsqrt(jnp.mean(yv * yv, axis=-1, keepdims=True) + EPS)
        ny = yv * ry
        gq = gv * wq_ref[...]
        dy_ref[...] = (ry * (gq - ny * jnp.mean(gq * ny, axis=-1, keepdims=True))).astype(dy_ref.dtype)
        accq_ref[...] += jnp.sum((gv * ny).reshape(tl // 8, 8, d), axis=0)

        @pl.when(i == nt - 1)
        def _():
            dwp_ref[...] = jnp.sum(accp_ref[...], axis=0, keepdims=True)
            dwq_ref[...] = jnp.sum(accq_ref[...], axis=0, keepdims=True)

    row = pl.BlockSpec((tl, d), lambda i: (i, 0))
    vec = pl.BlockSpec((1, d), lambda i: (0, 0))
    return pl.pallas_call(
        body, grid=(nt,), in_specs=[row, row, vec, row, row, vec], out_specs=[row, row, vec, vec],
        out_shape=[jax.ShapeDtypeStruct((l, d), F32), jax.ShapeDtypeStruct((l, d), BF16),
                   jax.ShapeDtypeStruct((1, d), F32), jax.ShapeDtypeStruct((1, d), F32)],
        scratch_shapes=[pltpu.VMEM((8, d), F32), pltpu.VMEM((8, d), F32)],
        compiler_params=_params("arbitrary"), name=name,
    )(dh, x, w_pre.reshape(1, d), resid, y_prev, w_post_prev.reshape(1, d))


def _rmsnorm_bwd(g, y, w, resid, name):
    l, d = y.shape
    tl = _row_tile(l)
    nt = l // tl

    def body(g_ref, y_ref, w_ref, r_ref, dy_ref, dw_ref, acc_ref):
        i = pl.program_id(0)

        @pl.when(i == 0)
        def _():
            acc_ref[...] = jnp.zeros_like(acc_ref)

        yv = y_ref[...]
        gv = g_ref[...]
        r = lax.rsqrt(jnp.mean(yv * yv, axis=-1, keepdims=True) + EPS)
        nrm = yv * r
        gw = gv * w_ref[...]
        dy_ref[...] = r * (gw - nrm * jnp.mean(gw * nrm, axis=-1, keepdims=True)) + r_ref[...]
        acc_ref[...] += jnp.sum((gv * nrm).reshape(tl // 8, 8, d), axis=0)

        @pl.when(i == nt - 1)
        def _():
            dw_ref[...] = jnp.sum(acc_ref[...], axis=0, keepdims=True)

    row = pl.BlockSpec((tl, d), lambda i: (i, 0))
    vec = pl.BlockSpec((1, d), lambda i: (0, 0))
    return pl.pallas_call(
        body, grid=(nt,), in_specs=[row, row, vec, row], out_specs=[row, vec],
        out_shape=[jax.ShapeDtypeStruct((l, d), F32), jax.ShapeDtypeStruct((1, d), F32)],
        scratch_shapes=[pltpu.VMEM((8, d), F32)], compiler_params=_params("arbitrary"), name=name,
    )(g, y, w.reshape(1, d), resid)


HALO = 8
CONV_SUB_ROWS = 64
CONV_SUB_COLS = 256


DT_COL_BLOCK = (SSM_D_INNER + SSM_CONV_DIM) // SSM_DT_PAD


def _split3(v):
    hi = v.astype(BF16)
    rest = v - hi.astype(F32)
    mid = rest.astype(BF16)
    lo = (rest - mid.astype(F32)).astype(BF16)
    return hi, mid, lo


def _dt_and_decay(v, a_log):
    head_dim_log2 = SSM_HEAD_DIM.bit_length() - 1
    dt_hi, dt_mid, _ = _split3(jnp.maximum(v, 0.0) + jnp.log1p(jnp.exp(-jnp.abs(v))))
    dt = dt_hi.astype(F32) + dt_mid.astype(F32)
    ri = lax.broadcasted_iota(jnp.int32, (SSM_CHUNK, SSM_CHUNK), 0)
    cj = lax.broadcasted_iota(jnp.int32, (SSM_CHUNK, SSM_CHUNK), 1)
    tri = (ri >= cj).astype(BF16)
    acs_pieces = _split3(sum(jnp.dot(tri, piece, preferred_element_type=F32)
                             for piece in _split3(dt * (-jnp.exp(a_log)))))
    acs = sum(piece.astype(F32) for piece in acs_pieces)
    head_of_lane = lax.shift_right_logical(lax.broadcasted_iota(jnp.int32, (SSM_DT_PAD, SSM_D_INNER), 1), head_dim_log2)
    spread = (head_of_lane == lax.broadcasted_iota(jnp.int32, (SSM_DT_PAD, SSM_D_INNER), 0)).astype(BF16)
    dtb = sum(jnp.dot(piece, spread, preferred_element_type=F32) for piece in (dt_hi, dt_mid))
    acsb = sum(jnp.dot(piece, spread, preferred_element_type=F32) for piece in acs_pieces)
    return dtb, acsb, dt.T, acs.T


def _dt_bwd(ddt, proj, bias, dproj, name):
    l = proj.shape[0]
    tl = _row_tile(l)

    def body(g_ref, p_ref, b_ref, _, o_ref, db_ref):
        @pl.when(pl.program_id(0) == 0)
        def _():
            db_ref[...] = jnp.zeros_like(db_ref)

        d = g_ref[...] * _sigmoid(p_ref[...] + b_ref[...])
        o_ref[...] = d.astype(o_ref.dtype)
        db_ref[...] += jnp.sum(d, axis=0, keepdims=True)

    return pl.pallas_call(
        body, grid=(l // tl,),
        in_specs=[pl.BlockSpec((tl, SSM_DT_PAD), lambda i: (i, 0)),
                  pl.BlockSpec((tl, SSM_DT_PAD), lambda i: (i, DT_COL_BLOCK)),
                  pl.BlockSpec((1, SSM_DT_PAD), lambda i: (0, 0)),
                  pl.BlockSpec(memory_space=pl.ANY)],
        out_specs=[pl.BlockSpec((tl, SSM_DT_PAD), lambda i: (i, DT_COL_BLOCK)),
                   pl.BlockSpec((1, SSM_DT_PAD), lambda i: (0, 0))],
        out_shape=[jax.ShapeDtypeStruct(dproj.shape, dproj.dtype), jax.ShapeDtypeStruct((1, SSM_DT_PAD), F32)],
        input_output_aliases={3: 0}, compiler_params=_params("arbitrary"), name=name,
    )(ddt, proj, bias, dproj)


GP = SSM_HPG * SSM_HEAD_DIM
HEAD_DIM_LOG2 = SSM_HEAD_DIM.bit_length() - 1
GPS = SSM_GROUPS
B_BLOCK0 = SSM_D_INNER // SSM_STATE
C_BLOCK0 = (SSM_D_INNER + SSM_BC_DIM) // SSM_STATE


def _chunk_iotas():
    ri = lax.broadcasted_iota(jnp.int32, (SSM_CHUNK, SSM_CHUNK), 0)
    cj = lax.broadcasted_iota(jnp.int32, (SSM_CHUNK, SSM_CHUNK), 1)
    return ri, cj


def _head_decay(acsb, acs_r, r, ri, cj):
    pair = acsb[:, (r // 2) * LANES:(r // 2 + 1) * LANES]
    mine_low = r % 2 == 0
    lane = lax.broadcasted_iota(jnp.int32, (1, LANES), 1)
    col = jnp.where((lane < SSM_HEAD_DIM) == mine_low, pair, pltpu.roll(pair, SSM_HEAD_DIM, 1))
    return jnp.exp(jnp.where(ri >= cj, col - acs_r[r:r + 1, :], NEG_INF))


def _head_masked_rows(v, dtype):
    head_of_lane = lax.shift_right_logical(lax.broadcasted_iota(jnp.int32, (1, GP), 1), HEAD_DIM_LOG2)
    narrow = v.astype(dtype)
    return jnp.concatenate([jnp.where(head_of_lane == r, narrow, jnp.zeros_like(narrow)) for r in range(SSM_HPG)], axis=0)


def _ssd_fwd(proj, cw, cb, dt_bias, a_log, d_lanes, gate_w, name, ride=()):
    l = proj.shape[0]
    nc = l // SSM_CHUNK
    assert GPS == SSM_GROUPS
    n_ride = len(ride)
    halo_blocks = SSM_CHUNK // HALO
    x_block = 1

    def body(*refs):
        u0_ref, u1_ref, h0_ref, h1_ref, cw_ref, cb_ref, dtraw_ref, bias_ref, alog_ref, d_ref, z_ref, gw_ref = refs[:12]
        ride_in = refs[12:12 + n_ride]
        (y_ref, act_ref, hin_ref, pre_ref, xbc_ref, dtb_out, acsb_out, dtr_out, acsr_out) = refs[12 + n_ride:21 + n_ride]
        ride_out = refs[21 + n_ride:21 + 2 * n_ride]
        h_ref, ext_ref, conv_ref, dtb_ref, acsb_ref, acsr_ref = refs[21 + 2 * n_ride:27 + 2 * n_ride]
        ride_sems = refs[27 + 2 * n_ride:]
        s = pl.program_id(0)
        if n_ride:
            @pl.when(s == 0)
            def _():
                _gather_between_chips(ride_in, ride_out, *ride_sems, wait=False)

            @pl.when(s == nc)
            def _():
                _gather_between_chips(ride_in, ride_out, *ride_sems, wait=True)

        @pl.when(s <= 1)
        def _():
            h_ref[...] = jnp.zeros_like(h_ref)

        @pl.when(s == 0)
        def _():
            conv_ref[1] = jnp.zeros((SSM_CHUNK, SSM_CONV_DIM), BF16)
            dtb_ref[1] = jnp.zeros((SSM_CHUNK, SSM_D_INNER), F32)
            acsb_ref[1] = jnp.zeros((SSM_CHUNK, SSM_D_INNER), F32)
            acsr_ref[1] = jnp.zeros((SSM_GROUPS, SSM_HPG, SSM_CHUNK), F32)

        conv_slot = s & 1
        scan_slot = (s - 1) & 1
        for half, (u_ref, hl_ref) in enumerate(((u0_ref, h0_ref), (u1_ref, h1_ref))):
            hc = slice(half * SSM_D_INNER, (half + 1) * SSM_D_INNER)
            ext_ref[0:HALO, hc] = jnp.where(s > 0, hl_ref[...], 0.0)
            ext_ref[HALO:HALO + SSM_CHUNK, hc] = u_ref[...]

        def conv_columns(c_lo, c_hi):
            for r0 in range(0, SSM_CHUNK, CONV_SUB_ROWS):
                for c0 in range(c_lo, c_hi, CONV_SUB_COLS):
                    cs = slice(c0, c0 + CONV_SUB_COLS)
                    ext = ext_ref[r0:r0 + CONV_SUB_ROWS + HALO, cs]
                    acc = cb_ref[:, cs] + cw_ref[SSM_CONV - 1:SSM_CONV, cs] * ext[HALO:]
                    for k in range(SSM_CONV - 1):
                        acc = acc + cw_ref[k:k + 1, cs] * pltpu.roll(ext, SSM_CONV - 1 - k, 0)[HALO:]
                    act = (acc * _sigmoid(acc)).astype(BF16)
                    pre_ref[r0:r0 + CONV_SUB_ROWS, cs] = acc.astype(pre_ref.dtype)
                    xbc_ref[r0:r0 + CONV_SUB_ROWS, cs] = act
                    conv_ref[conv_slot, r0:r0 + CONV_SUB_ROWS, cs] = act

        ri, cj = _chunk_iotas()
        conv_share = SSM_CONV_DIM // GPS
        for k in range(GPS):
            g = k
            cols = slice(k * GP, (k + 1) * GP)
            bcols = slice(SSM_D_INNER + k * SSM_STATE, SSM_D_INNER + (k + 1) * SSM_STATE)
            ccols = slice(SSM_D_INNER + SSM_BC_DIM + k * SSM_STATE, SSM_D_INNER + SSM_BC_DIM + (k + 1) * SSM_STATE)
            xv = conv_ref[scan_slot, :, cols].astype(F32)
            bb = conv_ref[scan_slot, :, bcols]
            cb16 = conv_ref[scan_slot, :, ccols]
            acs_v = acsb_ref[scan_slot, :, cols]
            acs_r_v = acsr_ref[scan_slot, k]
            lastb = acs_v[SSM_CHUNK - 1:SSM_CHUNK, :]
            xd = xv * dtb_ref[scan_slot, :, cols]
            cbm = lax.dot_general(cb16, bb, NT_DIMS, preferred_element_type=F32)
            hin = h_ref[g]
            hin_ref[0, k] = hin
            yoff = jnp.dot(cb16, hin.astype(BF16), preferred_element_type=F32)
            ms = [(cbm * _head_decay(acs_v, acs_r_v, r, ri, cj)).astype(BF16) for r in range(SSM_HPG)]
            ydiag = jnp.dot(jnp.concatenate(ms, axis=1), _head_masked_rows(xd, BF16), preferred_element_type=F32)
            y_ref[:, cols] = ydiag + jnp.exp(acs_v) * yoff + d_ref[k] * xv
            h_ref[g] = hin * jnp.exp(lastb) + _bdot_tn(bb, xd * jnp.exp(lastb - acs_v))
            conv_columns(k * conv_share, (k + 1) * conv_share)
        z = z_ref[...]
        yg = y_ref[...] * (z * _sigmoid(z))
        r = lax.rsqrt(jnp.mean(yg * yg, axis=-1, keepdims=True) + EPS)
        act_ref[...] = (yg * r * gw_ref[...]).astype(act_ref.dtype)

        dtb, acsb, dt_rows, acs_rows = _dt_and_decay(dtraw_ref[...] + bias_ref[...], alog_ref[...])
        dtb_out[...] = dtb
        acsb_out[...] = acsb
        dtb_ref[conv_slot] = dtb
        acsb_ref[conv_slot] = acsb
        for g in range(SSM_GROUPS):
            heads = slice(g * SSM_HPG, (g + 1) * SSM_HPG)
            dtr_out[g] = dt_rows[heads, :]
            acsr_out[g] = acs_rows[heads, :]
            acsr_ref[conv_slot, g] = acs_rows[heads, :]

    def conv_at(s):
        return jnp.minimum(s, nc - 1)

    def scan_at(s):
        return jnp.maximum(s - 1, 0)

    lanes = pl.BlockSpec((SSM_CHUNK, SSM_D_INNER), lambda s: (scan_at(s), 0))
    conv_out = pl.BlockSpec((SSM_CHUNK, SSM_CONV_DIM), lambda s: (conv_at(s), 0))
    lanes_ahead = pl.BlockSpec((SSM_CHUNK, SSM_D_INNER), lambda s: (conv_at(s), 0))
    rows_ahead = pl.BlockSpec((SSM_GROUPS, SSM_HPG, SSM_CHUNK), lambda s: (0, 0, conv_at(s)))
    return pl.pallas_call(
        body, grid=(nc + 1,),
        in_specs=[pl.BlockSpec((SSM_CHUNK, SSM_D_INNER), lambda s: (conv_at(s), x_block)),
                  pl.BlockSpec((SSM_CHUNK, SSM_D_INNER), lambda s: (conv_at(s), x_block + 1)),
                  pl.BlockSpec((HALO, SSM_D_INNER), lambda s: (jnp.maximum(conv_at(s) * halo_blocks - 1, 0), x_block)),
                  pl.BlockSpec((HALO, SSM_D_INNER), lambda s: (jnp.maximum(conv_at(s) * halo_blocks - 1, 0), x_block + 1)),
                  pl.BlockSpec((SSM_CONV, SSM_CONV_DIM), lambda s: (0, 0)),
                  pl.BlockSpec((1, SSM_CONV_DIM), lambda s: (0, 0)),
                  pl.BlockSpec((SSM_CHUNK, SSM_DT_PAD), lambda s: (conv_at(s), DT_COL_BLOCK)),
                  pl.BlockSpec((1, SSM_DT_PAD), lambda s: (0, 0)),
                  pl.BlockSpec((1, SSM_DT_PAD), lambda s: (0, 0)),
                  pl.BlockSpec((SSM_GROUPS, 1, GP), lambda s: (0, 0, 0)),
                  lanes, pl.BlockSpec((1, SSM_D_INNER), lambda s: (0, 0))] + [ANY] * n_ride,
        out_specs=[lanes, lanes, pl.BlockSpec((1, SSM_GROUPS, SSM_STATE, GP), lambda s: (scan_at(s), 0, 0, 0)),
                   conv_out, conv_out, lanes_ahead, lanes_ahead, rows_ahead, rows_ahead] + [ANY] * n_ride,
        out_shape=[jax.ShapeDtypeStruct((l, SSM_D_INNER), F32), jax.ShapeDtypeStruct((l, SSM_D_INNER), BF16),
                   jax.ShapeDtypeStruct((nc, SSM_GROUPS, SSM_STATE, GP), F32),
                   jax.ShapeDtypeStruct((l, SSM_CONV_DIM), BF16), jax.ShapeDtypeStruct((l, SSM_CONV_DIM), BF16),
                   jax.ShapeDtypeStruct((l, SSM_D_INNER), F32), jax.ShapeDtypeStruct((l, SSM_D_INNER), F32),
                   jax.ShapeDtypeStruct((SSM_GROUPS, SSM_HPG, l), F32),
                   jax.ShapeDtypeStruct((SSM_GROUPS, SSM_HPG, l), F32)]
        + [jax.ShapeDtypeStruct((N_CHIPS,) + a.shape, a.dtype) for a in ride],
        scratch_shapes=[pltpu.VMEM((SSM_GROUPS, SSM_STATE, GP), F32),
                        pltpu.VMEM((SSM_CHUNK + HALO, SSM_CONV_DIM), F32),
                        pltpu.VMEM((2, SSM_CHUNK, SSM_CONV_DIM), BF16),
                        pltpu.VMEM((2, SSM_CHUNK, SSM_D_INNER), F32), pltpu.VMEM((2, SSM_CHUNK, SSM_D_INNER), F32),
                        pltpu.VMEM((2, SSM_GROUPS, SSM_HPG, SSM_CHUNK), F32)] + (_gather_sems(n_ride) if n_ride else []),
        compiler_params=_params("arbitrary"), name=name,
    )(proj, proj, proj, proj, cw, cb.reshape(1, SSM_CONV_DIM), proj, dt_bias, a_log, d_lanes, proj,
      gate_w.reshape(1, SSM_D_INNER), *ride)


def _ssd_bwd(xbc, pre, cw, dtb, acsb, dtr, acs_r, a_log, d_lanes, hin, dact, y, proj, gate_w, name):
    l = xbc.shape[0]
    nc = l // SSM_CHUNK
    sub = CONV_SUB_ROWS

    def body(x_ref, b_ref, c_ref, dtb_ref, acsb_ref, dtr_ref, acsr_ref, alc_ref, d_ref, hin_ref,
             dact_ref, y_ref, z_ref, gw_ref, pre_ref, u0_ref, u1_ref, cw_ref,
             dproj_ref, ddt_ref, dal_ref, dd_ref, dgw_ref, dcw_ref, dcb_ref,
             dh_ref, dy_ref, acc_ref, dxbc_s, dz_s, ddt_s, carry_ref, ext_ref, dcw_acc, dcb_acc):
        step = pl.program_id(0)
        live = (step < nc).astype(F32)
        stage = step & 1
        staged = (step - 1) & 1

        @pl.when(step == 0)
        def _():
            for ref in (dal_ref, dd_ref, acc_ref, dh_ref, carry_ref, dcw_acc, dcb_acc):
                ref[...] = jnp.zeros_like(ref)
            dxbc_s[1] = jnp.zeros((SSM_CHUNK, SSM_CONV_DIM), F32)
            dz_s[1] = jnp.zeros((SSM_CHUNK, SSM_D_INNER), BF16)
            ddt_s[1] = jnp.zeros((SSM_GROUPS, SSM_HPG, SSM_CHUNK), F32)

        z = z_ref[...]
        yv = y_ref[...]
        s = _sigmoid(z)
        sz = z * s
        yg = yv * sz
        r = lax.rsqrt(jnp.mean(yg * yg, axis=-1, keepdims=True) + EPS)
        nrm = yg * r
        gv = dact_ref[...]
        gw = gv * gw_ref[...]
        dyg = r * (gw - nrm * jnp.mean(gw * nrm, axis=-1, keepdims=True))
        dy_ref[...] = dyg * sz
        dz_s[stage] = (dyg * yv * (s * (1.0 + z * (1.0 - s)))).astype(BF16)
        acc_ref[...] += live * jnp.sum((gv * nrm).reshape(SSM_CHUNK // 8, 8, SSM_D_INNER), axis=0)

        p = pre_ref[...].astype(F32)
        sp = _sigmoid(p)
        dp = dxbc_s[staged] * (sp * (1.0 + p * (1.0 - sp)))
        ext_ref[0:SSM_CHUNK, :] = dp
        ext_ref[SSM_CHUNK:SSM_CHUNK + HALO, :] = carry_ref[...]
        carry_ref[...] = dp[0:HALO]
        dproj_ref[:, 0:SSM_D_INNER] = dz_s[staged]
        ddt_ref[...] = ddt_s[staged]

        def fold(v):
            return jnp.sum(v.reshape(sub // 8, 8, CONV_SUB_COLS), axis=0)

        def conv_columns(c_lo, c_hi):
            for c0 in range(c_lo, c_hi, CONV_SUB_COLS):
                cs = slice(c0, c0 + CONV_SUB_COLS)
                u_ref, ucs = (u0_ref, cs) if c0 < SSM_D_INNER else (u1_ref, slice(c0 - SSM_D_INNER, c0 - SSM_D_INNER + CONV_SUB_COLS))
                for r0 in range(0, SSM_CHUNK, sub):
                    dext = ext_ref[r0:r0 + sub + HALO, cs]
                    uv = u_ref[r0:r0 + sub, ucs]
                    for k in range(SSM_CONV):
                        j = SSM_CONV - 1 - k
                        ahead = dext[:sub] if j == 0 else pltpu.roll(dext, sub + HALO - j, 0)[:sub]
                        term = cw_ref[k:k + 1, cs] * ahead
                        du = term if k == 0 else du + term
                        dcw_acc[k, :, cs] += fold(ahead * uv)
                    dcb_acc[:, cs] += fold(dext[:sub])
                    dproj_ref[r0:r0 + sub, SSM_D_INNER + c0:SSM_D_INNER + c0 + CONV_SUB_COLS] = du.astype(dproj_ref.dtype)

        conv_share = SSM_CONV_DIM // GPS
        for k in range(GPS):
            conv_columns(k * conv_share, (k + 1) * conv_share)
            one_group(live, stage, k, k, x_ref, b_ref, c_ref, dtb_ref, acsb_ref, dtr_ref, acsr_ref, alc_ref, d_ref,
                      hin_ref, dy_ref, dxbc_s, ddt_s, dal_ref, dd_ref, dh_ref)

        @pl.when(step == nc)
        def _():
            dgw_ref[...] = jnp.sum(acc_ref[...], axis=0, keepdims=True)
            dcw_ref[...] = jnp.sum(dcw_acc[...], axis=1)
            dcb_ref[...] = jnp.sum(dcb_acc[...], axis=0, keepdims=True)

    def one_group(live, stage, g, k, x_ref, b_ref, c_ref, dtb_ref, acsb_ref, dtr_ref, acsr_ref, alc_ref, d_ref, hin_ref,
                  dy_ref, dxbc_s, ddt_s, dal_ref, dd_ref, dh_ref):
        cols = slice(k * GP, (k + 1) * GP)
        ncols = slice(k * SSM_STATE, (k + 1) * SSM_STATE)
        bcols = slice(SSM_D_INNER + k * SSM_STATE, SSM_D_INNER + (k + 1) * SSM_STATE)
        ccols = slice(SSM_D_INNER + SSM_BC_DIM + k * SSM_STATE, SSM_D_INNER + SSM_BC_DIM + (k + 1) * SSM_STATE)

        xv = x_ref[:, cols].astype(F32)
        dyv = dy_ref[:, cols]
        bb = b_ref[:, ncols].astype(BF16)
        cb16 = c_ref[:, ncols].astype(BF16)
        dtb = dtb_ref[:, cols]
        acsb = acsb_ref[:, cols]
        dtr_v = dtr_ref[k]
        acs_r = acsr_ref[k]
        a_col = -jnp.exp(alc_ref[k])
        ri, cj = _chunk_iotas()
        head_of_lane = lax.shift_right_logical(lax.broadcasted_iota(jnp.int32, (SSM_HPG, GP), 1), HEAD_DIM_LOG2)
        ind_t = (head_of_lane == lax.broadcasted_iota(jnp.int32, (SSM_HPG, GP), 0)).astype(BF16)
        lastb = acsb[SSM_CHUNK - 1:SSM_CHUNK, :]
        ecb = jnp.exp(acsb)
        dteb = jnp.exp(lastb - acsb)
        xd = xv * dtb
        xw = xd * dteb
        cb = lax.dot_general(cb16, bb, NT_DIMS, preferred_element_type=F32)
        hin_v = hin_ref[0, k]
        dhn = dh_ref[g]
        h16 = hin_v.astype(BF16)
        dh16 = dhn.astype(BF16)
        ch = jnp.dot(cb16, h16, preferred_element_type=F32)
        bdh = jnp.dot(bb, dh16, preferred_element_type=F32)
        dym = _head_masked_rows(dyv, BF16)
        g_all = lax.dot_general(dym, xd.astype(BF16), NT_DIMS, preferred_element_type=F32)
        gl_sum = jnp.zeros((SSM_CHUNK, SSM_CHUNK), F32)
        ms, qs = [], []
        for r in range(SSM_HPG):
            decay = _head_decay(acsb, acs_r, r, ri, cj)
            gl = g_all[r * SSM_CHUNK:(r + 1) * SSM_CHUNK] * decay
            gl_sum = gl_sum + gl
            ms.append((cb * decay).astype(BF16))
            qs.append((gl * cb).astype(BF16))
        dxd = lax.dot_general(jnp.concatenate(ms, axis=0), dym, TN_DIMS, preferred_element_type=F32) + dteb * bdh
        cum = jnp.dot(jnp.concatenate(qs, axis=0), (ri < cj).astype(BF16), preferred_element_type=F32)
        sub4 = lax.broadcasted_iota(jnp.int32, (SSM_HPG, 1), 0)
        da = jnp.zeros((SSM_HPG, SSM_CHUNK), F32)
        for r in range(SSM_HPG):
            rect = jnp.sum(jnp.where(ri >= cj, cum[r * SSM_CHUNK:(r + 1) * SSM_CHUNK], 0.0), axis=0, keepdims=True)
            da = da + jnp.where(sub4 == r, rect, 0.0)
        z2 = xw * bdh
        sub8 = lax.broadcasted_iota(jnp.int32, (8, 1), 0)
        col_sums = (jnp.where(sub8 == 0, jnp.sum(z2, axis=0, keepdims=True), 0.0)
                    + jnp.where(sub8 == 1, jnp.sum(dhn * hin_v, axis=0, keepdims=True), 0.0)
                    + jnp.where(sub8 == 2, jnp.sum(dyv * xv, axis=0, keepdims=True), 0.0))
        wv = ecb * dyv
        summands = jnp.concatenate([wv * ch - z2, dxd * xv, col_sums], axis=0)
        sums = lax.dot_general(ind_t, summands.astype(BF16), NT_DIMS, preferred_element_type=F32)
        per_pos = sums[:, :2 * SSM_CHUNK]
        totals = sums[:, 2 * SSM_CHUNK:]
        e_last = totals[:, 0:1] + jnp.exp(acs_r[:, SSM_CHUNK - 1:SSM_CHUNK]) * totals[:, 1:2]
        da = (da + e_last + jnp.dot(per_pos[:, :SSM_CHUNK], (ri >= cj).astype(F32), preferred_element_type=F32,
                                    precision=lax.Precision.HIGHEST))
        ddt_s[stage, k] = a_col * da + per_pos[:, SSM_CHUNK:]
        dal_ref[g] += live * (a_col * jnp.sum(da * dtr_v, axis=1, keepdims=True))
        dd_ref[g] += live * totals[:, 2:3]
        dxbc_s[stage, :, cols] = dxd * dtb + d_ref[k] * dyv
        w16 = wv.astype(BF16)
        xw16 = xw.astype(BF16)
        gl16 = gl_sum.astype(BF16)
        dxbc_s[stage, :, ccols] = (jnp.dot(gl16, bb, preferred_element_type=F32)
                                   + lax.dot_general(w16, h16, NT_DIMS, preferred_element_type=F32))
        dxbc_s[stage, :, bcols] = (lax.dot_general(gl16, cb16, TN_DIMS, preferred_element_type=F32)
                                   + lax.dot_general(xw16, dh16, NT_DIMS, preferred_element_type=F32))
        dh_ref[g] = dhn * jnp.exp(lastb) + lax.dot_general(cb16, w16, TN_DIMS, preferred_element_type=F32)

    def scan_at(s):
        return nc - 1 - jnp.minimum(s, nc - 1)

    def conv_at(s):
        return nc - 1 - jnp.maximum(s - 1, 0)

    small = pl.BlockSpec((SSM_GROUPS, SSM_HPG, 1), lambda s: (0, 0, 0))
    lanes = pl.BlockSpec((SSM_CHUNK, SSM_D_INNER), lambda s: (scan_at(s), 0))
    rows = pl.BlockSpec((SSM_GROUPS, SSM_HPG, SSM_CHUNK), lambda s: (0, 0, scan_at(s)))
    vec = pl.BlockSpec((1, SSM_D_INNER), lambda s: (0, 0))
    return pl.pallas_call(
        body, grid=(nc + 1,),
        in_specs=[lanes,
                  pl.BlockSpec((SSM_CHUNK, SSM_BC_DIM), lambda s: (scan_at(s), B_BLOCK0 // GPS)),
                  pl.BlockSpec((SSM_CHUNK, SSM_BC_DIM), lambda s: (scan_at(s), C_BLOCK0 // GPS)),
                  lanes, lanes, rows, rows,
                  pl.BlockSpec((SSM_GROUPS, SSM_HPG, 1), lambda s: (0, 0, 0)),
                  pl.BlockSpec((SSM_GROUPS, 1, GP), lambda s: (0, 0, 0)),
                  pl.BlockSpec((1, SSM_GROUPS, SSM_STATE, GP), lambda s: (scan_at(s), 0, 0, 0)),
                  lanes, lanes, lanes, vec,
                  pl.BlockSpec((SSM_CHUNK, SSM_CONV_DIM), lambda s: (conv_at(s), 0)),
                  pl.BlockSpec((SSM_CHUNK, SSM_D_INNER), lambda s: (conv_at(s), 1)),
                  pl.BlockSpec((SSM_CHUNK, SSM_D_INNER), lambda s: (conv_at(s), 2)),
                  pl.BlockSpec((SSM_CONV, SSM_CONV_DIM), lambda s: (0, 0))],
        out_specs=[pl.BlockSpec((SSM_CHUNK, SSM_D_INNER + SSM_CONV_DIM), lambda s: (conv_at(s), 0)),
                   pl.BlockSpec((SSM_GROUPS, SSM_HPG, SSM_CHUNK), lambda s: (0, 0, conv_at(s))),
                   small, small, vec,
                   pl.BlockSpec((SSM_CONV, SSM_CONV_DIM), lambda s: (0, 0)),
                   pl.BlockSpec((1, SSM_CONV_DIM), lambda s: (0, 0))],
        out_shape=[jax.ShapeDtypeStruct((l, SSM_IN_PAD), BF16), jax.ShapeDtypeStruct((SSM_GROUPS, SSM_HPG, l), F32),
                   jax.ShapeDtypeStruct((SSM_GROUPS, SSM_HPG, 1), F32),
                   jax.ShapeDtypeStruct((SSM_GROUPS, SSM_HPG, 1), F32),
                   jax.ShapeDtypeStruct((1, SSM_D_INNER), F32),
                   jax.ShapeDtypeStruct((SSM_CONV, SSM_CONV_DIM), F32), jax.ShapeDtypeStruct((1, SSM_CONV_DIM), F32)],
        scratch_shapes=[pltpu.VMEM((SSM_GROUPS, SSM_STATE, GP), F32), pltpu.VMEM((SSM_CHUNK, SSM_D_INNER), F32),
                        pltpu.VMEM((8, SSM_D_INNER), F32),
                        pltpu.VMEM((2, SSM_CHUNK, SSM_CONV_DIM), F32), pltpu.VMEM((2, SSM_CHUNK, SSM_D_INNER), BF16),
                        pltpu.VMEM((2, SSM_GROUPS, SSM_HPG, SSM_CHUNK), F32), pltpu.VMEM((HALO, SSM_CONV_DIM), F32),
                        pltpu.VMEM((SSM_CHUNK + HALO, SSM_CONV_DIM), F32),
                        pltpu.VMEM((SSM_CONV, 8, SSM_CONV_DIM), F32), pltpu.VMEM((8, SSM_CONV_DIM), F32)],
        compiler_params=_params("arbitrary"), name=name,
    )(xbc, xbc, xbc, dtb, acsb, dtr, acs_r, a_log.reshape(SSM_GROUPS, SSM_HPG, 1), d_lanes, hin, dact, y, proj,
      gate_w.reshape(1, SSM_D_INNER), pre, proj, proj, cw)


LANES = 128
ROPE_Q_CHUNKS = ATT_WIDTH // LANES
ROPE_K_CHUNKS = ATT_KV_WIDTH // LANES


def _rope_tables(positions):
    inv = ROPE_THETA ** (-jnp.arange(0, ROPE_DIM, 2, dtype=F32) / ROPE_DIM)
    ang = positions.astype(F32)[:, None] * inv
    cos, sin = jnp.cos(ang), jnp.sin(ang)
    l = positions.shape[0]
    rest = ATT_HEAD_DIM - ROPE_DIM
    ones, zeros = jnp.ones((l, rest), F32), jnp.zeros((l, rest), F32)
    z8 = jnp.zeros((l, ROPE_HALF), F32)
    cos_f = jnp.concatenate([cos, cos, ones], axis=1)
    sin_a = jnp.concatenate([-sin, z8, zeros], axis=1)
    sin_b = jnp.concatenate([z8, sin, zeros], axis=1)
    reps = LANES // ATT_HEAD_DIM
    return tuple(jnp.tile(t, (1, reps)) for t in (cos_f, sin_a, sin_b))


ATT_QKV4 = 3 * ATT_WIDTH


def _both_halves(chunk):
    lane = lax.broadcasted_iota(jnp.int32, (1, LANES), 1)
    swapped = pltpu.roll(chunk, ATT_HEAD_DIM, 1)
    return jnp.where(lane < ATT_HEAD_DIM, chunk, swapped), jnp.where(lane < ATT_HEAD_DIM, swapped, chunk)


def _rope_bwd(dq, dk4, dv4, dgate, tables, name):
    l = dq.shape[0]
    tl = _pick(l, (256, 128))

    def body(dq_ref, dk_ref, dv_ref, dg_ref, c_ref, sa_ref, sb_ref, o_ref):
        cos_f, sin_a, sin_b = c_ref[...], sa_ref[...], sb_ref[...]
        lane = lax.broadcasted_iota(jnp.int32, (1, LANES), 1)

        def unrope(t):
            return t * cos_f + pltpu.roll(t * sin_a, ROPE_HALF, 1) + pltpu.roll(t * sin_b, LANES - ROPE_HALF, 1)

        def head_total(ref, kvh):
            base = kvh * ATT_GQA * ATT_HEAD_DIM
            s = ref[:, base:base + LANES] + ref[:, base + LANES:base + 2 * LANES]
            return s + pltpu.roll(s, ATT_HEAD_DIM, 1)

        for k in range(ROPE_Q_CHUNKS):
            sl = slice(k * LANES, (k + 1) * LANES)
            o_ref[:, sl] = unrope(dq_ref[:, sl] * Q_SCALE).astype(o_ref.dtype)
        for k in range(ROPE_K_CHUNKS):
            dk = jnp.where(lane < ATT_HEAD_DIM, head_total(dk_ref, 2 * k), head_total(dk_ref, 2 * k + 1))
            dv = jnp.where(lane < ATT_HEAD_DIM, head_total(dv_ref, 2 * k), head_total(dv_ref, 2 * k + 1))
            o_ref[:, ATT_WIDTH + k * LANES:ATT_WIDTH + (k + 1) * LANES] = unrope(dk).astype(o_ref.dtype)
            at = ATT_WIDTH + ATT_KV_WIDTH + k * LANES
            o_ref[:, at:at + LANES] = dv.astype(o_ref.dtype)
        o_ref[:, ATT_QKV:ATT_IN_DIM] = dg_ref[...].astype(o_ref.dtype)

    tab = pl.BlockSpec((tl, LANES), lambda i: (i, 0))
    wide = pl.BlockSpec((tl, ATT_WIDTH), lambda i: (i, 0))
    return pl.pallas_call(
        body, grid=(l // tl,), in_specs=[wide, wide, wide, wide, tab, tab, tab],
        out_specs=pl.BlockSpec((tl, ATT_IN_DIM), lambda i: (i, 0)),
        out_shape=jax.ShapeDtypeStruct((l, ATT_IN_DIM), BF16), compiler_params=_params("parallel"), name=name,
    )(dq, dk4, dv4, dgate, *tables)


GATE_HALF = ATT_WIDTH // 2
GATE_COL_BLOCK = ATT_QKV // GATE_HALF


ATT_STACK = ATT_GQA * ATT_BLOCK
BLOCK_LOG2 = ATT_BLOCK.bit_length() - 1


def _stack_masks(n):
    ri = lax.broadcasted_iota(jnp.int32, (ATT_STACK, ATT_BLOCK), 0) & (ATT_BLOCK - 1)
    cj = lax.broadcasted_iota(jnp.int32, (ATT_STACK, ATT_BLOCK), 1)
    return (cj > ri) & (n > 0), cj <= ri


def _stack_sinks(sink_ref, kvh):
    blk = lax.shift_right_logical(lax.broadcasted_iota(jnp.int32, (ATT_STACK, 1), 0), BLOCK_LOG2)
    col = jnp.zeros((ATT_STACK, 1), F32)
    for r in range(ATT_GQA):
        col = jnp.where(blk == r, sink_ref[kvh * ATT_GQA + r], col)
    return col


def _stack_fold(stack):
    head_of_lane = lax.shift_right_logical(lax.broadcasted_iota(jnp.int32, (1, GP), 1), HEAD_DIM_LOG2)
    out = jnp.zeros((ATT_BLOCK, GP), F32)
    for r in range(ATT_GQA):
        out = jnp.where(head_of_lane == r, stack[r * ATT_BLOCK:(r + 1) * ATT_BLOCK], out)
    return out


def _attn_fwd(proj, tables, sinks, name):
    l = proj.shape[0]
    nb = l // ATT_BLOCK
    ring = 3

    def body(sink_ref, p_ref, c_ref, sa_ref, sb_ref, g0_ref, g1_ref, og_ref, o_ref, lse_ref, qkv_ref, ring_ref):
        s = pl.program_id(0)

        @pl.when(s == 0)
        def _():
            ring_ref[...] = jnp.zeros_like(ring_ref)

        n = s - 1
        cur = lax.rem(s + ring - 1, ring)
        prv = lax.rem(s + ring - 2, ring)
        q_ref = ring_ref.at[cur, :, 0:ATT_WIDTH]
        kc_ref = ring_ref.at[cur, :, ATT_WIDTH:2 * ATT_WIDTH]
        vc_ref = ring_ref.at[cur, :, 2 * ATT_WIDTH:3 * ATT_WIDTH]
        kp_ref = ring_ref.at[prv, :, ATT_WIDTH:2 * ATT_WIDTH]
        vp_ref = ring_ref.at[prv, :, 2 * ATT_WIDTH:3 * ATT_WIDTH]
        mask_p, mask_c = _stack_masks(n)
        ones = jnp.ones((ATT_BLOCK, LANES), BF16)
        for kvh in range(ATT_KV_HEADS):
            cols = slice(kvh * GP, (kvh + 1) * GP)
            q_stack = _head_masked_rows(q_ref[:, cols], BF16)
            sp = jnp.where(mask_p, lax.dot_general(q_stack, kp_ref[:, cols], NT_DIMS, preferred_element_type=F32), NEG_INF)
            sc = jnp.where(mask_c, lax.dot_general(q_stack, kc_ref[:, cols], NT_DIMS, preferred_element_type=F32), NEG_INF)
            sink = _stack_sinks(sink_ref, kvh)
            m = jnp.maximum(jnp.max(jnp.maximum(sp, sc), axis=1, keepdims=True), sink)
            pp = jnp.exp(sp - m).astype(BF16)
            pc = jnp.exp(sc - m).astype(BF16)
            acc = (jnp.dot(pp, jnp.concatenate([vp_ref[:, cols], ones], axis=1), preferred_element_type=F32)
                   + jnp.dot(pc, jnp.concatenate([vc_ref[:, cols], ones], axis=1), preferred_element_type=F32))
            den = acc[:, GP:] + jnp.exp(sink - m)
            inv = 1.0 / den
            o_ref[:, cols] = _stack_fold(acc[:, :GP] * jnp.concatenate([inv, inv], axis=1))
            lse = m + jnp.log(den)
            lse_ref[:, cols] = _stack_fold(jnp.concatenate([lse, lse], axis=1))
        for half, g_ref in enumerate((g0_ref, g1_ref)):
            sl = slice(half * GATE_HALF, (half + 1) * GATE_HALF)
            gate = g_ref[...]
            og_ref[:, sl] = (o_ref[:, sl] * (gate * _sigmoid(gate))).astype(og_ref.dtype)

        slot = lax.rem(s, ring)
        cos_f, sin_a, sin_b = c_ref[...], sa_ref[...], sb_ref[...]

        def rope(t):
            return t * cos_f + pltpu.roll(t, LANES - ROPE_HALF, 1) * sin_a + pltpu.roll(t, ROPE_HALF, 1) * sin_b

        def put(dst, value):
            qkv_ref[:, dst:dst + LANES] = value
            ring_ref[slot, :, dst:dst + LANES] = value

        for k in range(ROPE_Q_CHUNKS):
            put(k * LANES, (rope(p_ref[:, k * LANES:(k + 1) * LANES]) * Q_SCALE).astype(BF16))
        for part in range(2):
            for k in range(ROPE_K_CHUNKS):
                src = ATT_WIDTH + part * ATT_KV_WIDTH + k * LANES
                t = p_ref[:, src:src + LANES]
                if part == 0:
                    t = rope(t)
                for head, dup in enumerate(_both_halves(t.astype(BF16))):
                    dst = (1 + part) * ATT_WIDTH + (2 * k + head) * ATT_GQA * ATT_HEAD_DIM
                    put(dst, dup)
                    put(dst + LANES, dup)

    def rope_at(s):
        return jnp.minimum(s, nb - 1)

    def attend_at(s):
        return jnp.maximum(s - 1, 0)

    wide = pl.BlockSpec((ATT_BLOCK, ATT_WIDTH), lambda s: (attend_at(s), 0))
    tab = pl.BlockSpec((ATT_BLOCK, LANES), lambda s: (rope_at(s), 0))
    return pl.pallas_call(
        body, grid=(nb + 1,),
        in_specs=[pl.BlockSpec(memory_space=pltpu.SMEM),
                  pl.BlockSpec((ATT_BLOCK, ATT_IN_DIM), lambda s: (rope_at(s), 0)), tab, tab, tab,
                  pl.BlockSpec((ATT_BLOCK, GATE_HALF), lambda s: (attend_at(s), GATE_COL_BLOCK)),
                  pl.BlockSpec((ATT_BLOCK, GATE_HALF), lambda s: (attend_at(s), GATE_COL_BLOCK + 1))],
        out_specs=[wide, wide, wide, pl.BlockSpec((ATT_BLOCK, ATT_QKV4), lambda s: (rope_at(s), 0))],
        out_shape=[jax.ShapeDtypeStruct((l, ATT_WIDTH), BF16), jax.ShapeDtypeStruct((l, ATT_WIDTH), F32),
                   jax.ShapeDtypeStruct((l, ATT_WIDTH), F32), jax.ShapeDtypeStruct((l, ATT_QKV4), BF16)],
        scratch_shapes=[pltpu.VMEM((ring, ATT_BLOCK, ATT_QKV4), BF16)],
        compiler_params=_params("arbitrary"), name=name,
    )(sinks, proj, *tables, proj, proj)


def _attn_bwd(qkv, proj, sinks, o, lse, dog, name, ride=()):
    l = qkv.shape[0]
    nb = l // ATT_BLOCK
    n_ride = len(ride)

    def body(*refs):
        sink_ref, q_ref, kp_ref, kc_ref, vp_ref, vc_ref, g0_ref, g1_ref, o_ref, lse_ref, dog_ref = refs[:11]
        ride_in = refs[11:11 + n_ride]
        dq_ref, dk_ref, dv_ref, dg_ref, ds_ref = refs[11 + n_ride:16 + n_ride]
        ride_out = refs[16 + n_ride:16 + 2 * n_ride]
        ck_ref, cv_ref, do_ref = refs[16 + 2 * n_ride:19 + 2 * n_ride]
        ride_sems = refs[19 + 2 * n_ride:]
        n = pl.program_id(0)

        @pl.when(n == 0)
        def _():
            ds_ref[...] = jnp.zeros_like(ds_ref)
            ck_ref[...] = jnp.zeros_like(ck_ref)
            cv_ref[...] = jnp.zeros_like(cv_ref)
            if n_ride:
                _scatter_between_chips(ride_in, ride_out, *ride_sems, wait=False)

        @pl.when(n == nb)
        def _():
            dk_ref[...] = ck_ref[...]
            dv_ref[...] = cv_ref[...]
            if n_ride:
                _scatter_between_chips(ride_in, ride_out, *ride_sems, wait=True)

        @pl.when(n < nb)
        def _():
            mask_p, mask_c = _stack_masks(n)
            lane = lax.broadcasted_iota(jnp.int32, (1, ATT_Q_HEADS), 1)
            for half, g_ref in enumerate((g0_ref, g1_ref)):
                sl = slice(half * GATE_HALF, (half + 1) * GATE_HALF)
                gate = g_ref[...]
                s = _sigmoid(gate)
                dogv = dog_ref[:, sl]
                do_ref[:, sl] = dogv * (gate * s)
                dg_ref[:, sl] = dogv * o_ref[:, sl] * (s * (1.0 + gate * (1.0 - s)))
            ds_acc = jnp.zeros((1, ATT_Q_HEADS), F32)
            for kvh in range(ATT_KV_HEADS):
                cols = slice(kvh * GP, (kvh + 1) * GP)
                kp, kc, vp, vc = kp_ref[:, cols], kc_ref[:, cols], vp_ref[:, cols], vc_ref[:, cols]
                q_stack = _head_masked_rows(q_ref[:, cols], BF16)
                do_g = do_ref[:, cols]
                do_stack = _head_masked_rows(do_g, BF16)
                lse_g = lse_ref[:, cols]
                lse_stack = jnp.concatenate(
                    [_both_halves(lse_g[:, (r // 2) * LANES:(r // 2 + 1) * LANES])[r % 2] for r in range(ATT_GQA)], axis=0)
                pp = jnp.exp(jnp.where(
                    mask_p, lax.dot_general(q_stack, kp, NT_DIMS, preferred_element_type=F32) - lse_stack, NEG_INF))
                pc = jnp.exp(jnp.where(
                    mask_c, lax.dot_general(q_stack, kc, NT_DIMS, preferred_element_type=F32) - lse_stack, NEG_INF))
                dpp = lax.dot_general(do_stack, vp, NT_DIMS, preferred_element_type=F32)
                dpc = lax.dot_general(do_stack, vc, NT_DIMS, preferred_element_type=F32)
                delta = jnp.sum(pp * dpp + pc * dpc, axis=1, keepdims=True)
                dsp = (pp * (dpp - delta)).astype(BF16)
                dsc = (pc * (dpc - delta)).astype(BF16)
                dq_ref[:, cols] = _stack_fold(jnp.dot(dsp, kp, preferred_element_type=F32)
                                              + jnp.dot(dsc, kc, preferred_element_type=F32))
                dk_ref[:, cols] = ck_ref[:, cols] + lax.dot_general(dsp, q_stack, TN_DIMS, preferred_element_type=F32)
                dv_ref[:, cols] = cv_ref[:, cols] + lax.dot_general(pp.astype(BF16), do_stack, TN_DIMS,
                                                                    preferred_element_type=F32)
                ck_ref[:, cols] = lax.dot_general(dsc, q_stack, TN_DIMS, preferred_element_type=F32)
                cv_ref[:, cols] = lax.dot_general(pc.astype(BF16), do_stack, TN_DIMS, preferred_element_type=F32)
                t = jnp.exp(_stack_sinks(sink_ref, kvh) - lse_stack) * delta
                for r in range(ATT_GQA):
                    tot = jnp.sum(t[r * ATT_BLOCK:(r + 1) * ATT_BLOCK], axis=0, keepdims=True)
                    ds_acc = ds_acc - jnp.where(lane == kvh * ATT_GQA + r, tot[:, :ATT_Q_HEADS], 0.0)
            ds_ref[...] += ds_acc

    def cur(n):
        return jnp.minimum(n, nb - 1)

    def prev(n):
        return jnp.maximum(n - 1, 0)

    wide = pl.BlockSpec((ATT_BLOCK, ATT_WIDTH), lambda n: (cur(n), 0))
    late = pl.BlockSpec((ATT_BLOCK, ATT_WIDTH), lambda n: (prev(n), 0))
    return pl.pallas_call(
        body, grid=(nb + 1,),
        in_specs=[pl.BlockSpec(memory_space=pltpu.SMEM), wide,
                  pl.BlockSpec((ATT_BLOCK, ATT_WIDTH), lambda n: (prev(cur(n)), 1)),
                  pl.BlockSpec((ATT_BLOCK, ATT_WIDTH), lambda n: (cur(n), 1)),
                  pl.BlockSpec((ATT_BLOCK, ATT_WIDTH), lambda n: (prev(cur(n)), 2)),
                  pl.BlockSpec((ATT_BLOCK, ATT_WIDTH), lambda n: (cur(n), 2)),
                  pl.BlockSpec((ATT_BLOCK, GATE_HALF), lambda n: (cur(n), GATE_COL_BLOCK)),
                  pl.BlockSpec((ATT_BLOCK, GATE_HALF), lambda n: (cur(n), GATE_COL_BLOCK + 1)),
                  wide, wide, wide] + [ANY] * n_ride,
        out_specs=[wide, late, late, wide, pl.BlockSpec((1, ATT_Q_HEADS), lambda n: (0, 0))] + [ANY] * n_ride,
        out_shape=[jax.ShapeDtypeStruct((l, ATT_WIDTH), F32), jax.ShapeDtypeStruct((l, ATT_WIDTH), F32),
                   jax.ShapeDtypeStruct((l, ATT_WIDTH), F32), jax.ShapeDtypeStruct((l, ATT_WIDTH), F32),
                   jax.ShapeDtypeStruct((1, ATT_Q_HEADS), F32)] + _scatter_shapes(ride),
        scratch_shapes=[pltpu.VMEM((ATT_BLOCK, ATT_WIDTH), F32), pltpu.VMEM((ATT_BLOCK, ATT_WIDTH), F32),
                        pltpu.VMEM((ATT_BLOCK, ATT_WIDTH), F32)] + (_gather_sems(n_ride) if n_ride else []),
        compiler_params=_params("arbitrary"), name=name,
    )(sinks, qkv, qkv, qkv, qkv, qkv, proj, proj, o, lse, dog, *ride)


def _local_step(x, positions, pre_norm, post_norm, conv_b, dt_bias, a_log, d_skip, gate_norm, sinks, target,
                first_in, in_proj_with_first_pair, scan_with_second_pair, attn_bwd_with_second_pair_grads,
                in_dx_with_first_pair_grads):
    tables = _rope_tables(positions)
    dt_bias_pad = jnp.pad(dt_bias, ((0, 0), (0, SSM_DT_PAD - SSM_HEADS)))
    d_lanes = jnp.repeat(d_skip, SSM_HEAD_DIM, axis=1).reshape(-1, SSM_GROUPS, 1, GP)
    a_log_pad = jnp.pad(a_log, ((0, 0), (0, SSM_DT_PAD - SSM_HEADS)))
    pairs = [first_in, None]
    saved = []
    cur = x
    h = _rmsnorm_fwd(cur, pre_norm[0], "prenorm_fwd_0")
    for i in range(DEPTH):
        j = i // 2
        if i % 2 == 0:
            in_proj = functools.partial(_matmul, h, pairs[j]["ssm_w_in"], "nn", F32, f"ssm_in_{i}")
            if i == 0:
                proj, rest = in_proj_with_first_pair(in_proj)
                pairs[0] = {**first_in, **rest}
            else:
                proj = in_proj()
            scan = functools.partial(_ssd_fwd, proj, pairs[j]["ssm_conv_w"], conv_b[j], dt_bias_pad[j:j + 1],
                                     a_log_pad[j:j + 1], d_lanes[j], gate_norm[j], f"ssd_fwd_{i}")
            if i == 0:
                *scanned, pairs[1] = scan_with_second_pair(scan)
            else:
                scanned = scan()
            y, act, hin, pre, xbc, dtb, acsb, dtr, acs_r = scanned
            w_ssm_in = [p["ssm_w_in"] for p in pairs]
            w_ssm_out = [p["ssm_w_out"] for p in pairs]
            w_att_in = [p["att_w_in"] for p in pairs]
            w_att_out = [p["att_w_out"] for p in pairs]
            conv_w = [p["ssm_conv_w"] for p in pairs]
            ymix = _matmul(act, w_ssm_out[j], "nn", F32, f"ssm_out_{i}")
            saved.append(dict(x=cur, h=h, proj=proj, pre=pre, xbc=xbc, dtb=dtb, acsb=acsb, dtr=dtr, acs_r=acs_r, y=y,
                              hin=hin, act=act, ymix=ymix))
        else:
            proj = _matmul(h, w_att_in[j], "nn", F32, f"att_in_{i}")
            act, o, lse, qkv = _attn_fwd(proj, tables, sinks[j], f"attn_fwd_{i}")
            ymix = _matmul(act, w_att_out[j], "nn", F32, f"att_out_{i}")
            saved.append(dict(x=cur, h=h, proj=proj, qkv=qkv, o=o, lse=lse, act=act, ymix=ymix))
        if i + 1 < DEPTH:
            cur, h = _post_fwd(cur, ymix, post_norm[i], pre_norm[i + 1], f"post_fwd_{i}")

    gr = {k: [None] * 2 for k in ("ssm_w_in", "ssm_conv_w", "ssm_conv_b", "ssm_dt_bias", "ssm_a_log", "ssm_d",
                                  "ssm_gate_norm", "ssm_w_out", "att_w_in", "att_sinks", "att_w_out")}
    gr["pre_norm"] = [None] * DEPTH
    gr["post_norm"] = [None] * DEPTH
    last = DEPTH - 1
    g, dymix, loss_lanes, gr["post_norm"][last] = _post_loss(cur, ymix, post_norm[last], target, "post_loss")
    for i in reversed(range(DEPTH)):
        j = i // 2
        s = saved[i]
        if i % 2 == 0:
            dact = _matmul(dymix, w_ssm_out[j], "nt", F32, f"ssm_out_dx_{i}")
            gr["ssm_w_out"][j] = _matmul(s["act"], dymix, "tn", F32, f"ssm_out_dw_{i}")
            dproj, ddt8, dal, dd, gr["ssm_gate_norm"][j], gr["ssm_conv_w"][j], dcb = _ssd_bwd(
                s["xbc"], s["pre"], conv_w[j], s["dtb"], s["acsb"], s["dtr"], s["acs_r"], a_log[j], d_lanes[j], s["hin"],
                dact, s["y"], s["proj"], gate_norm[j], f"ssd_bwd_{i}")
            gr["ssm_conv_b"][j] = dcb[0]
            gr["ssm_a_log"][j] = dal.reshape(SSM_HEADS)
            gr["ssm_d"][j] = dd.reshape(SSM_HEADS)
            l = x.shape[0]
            ddt = jnp.pad(jnp.transpose(ddt8, (2, 0, 1)).reshape(l, SSM_HEADS), ((0, 0), (0, SSM_DT_PAD - SSM_HEADS)))
            dproj, dbias = _dt_bwd(ddt, s["proj"], dt_bias_pad[j:j + 1], dproj, f"dt_bwd_{i}")
            gr["ssm_dt_bias"][j] = dbias[0, :SSM_HEADS]
            w_in, key = w_ssm_in[j], "ssm_w_in"
        else:
            dog = _matmul(dymix, w_att_out[j], "nt", F32, f"att_out_dx_{i}")
            gr["att_w_out"][j] = _matmul(s["act"], dymix, "tn", F32, f"att_out_dw_{i}")
            attn_bwd = functools.partial(_attn_bwd, s["qkv"], s["proj"], sinks[j], s["o"], s["lse"], dog, f"attn_bwd_{i}")
            if i == 1:
                (dq, dk, dv, dgate, dsk), second_pair_reduced = attn_bwd_with_second_pair_grads(
                    attn_bwd, {k: gr[k][1] for k in BIG})
            else:
                dq, dk, dv, dgate, dsk = attn_bwd()
            gr["att_sinks"][j] = dsk[0]
            dproj = _rope_bwd(dq, dk, dv, dgate, tables, f"rope_bwd_{i}")
            w_in, key = w_att_in[j], "att_w_in"
        gr[key][j] = _matmul(s["h"], dproj, "tn", F32, f"in_dw_{i}")
        in_dx = functools.partial(_matmul, dproj, w_in, "nt", F32, f"in_dx_{i}")
        if i == 0:
            dh, first_pair_reduced = in_dx_with_first_pair_grads(in_dx, {k: gr[k][0] for k in BIG})
        else:
            dh = in_dx()
        if i > 0:
            g, dymix, gr["pre_norm"][i], gr["post_norm"][i - 1] = _norm_bwd_chain(
                dh, s["x"], pre_norm[i], g, saved[i - 1]["ymix"], post_norm[i - 1], f"norm_bwd_{i}")
        else:
            g, gr["pre_norm"][i] = _rmsnorm_bwd(dh, s["x"], pre_norm[i], g, f"prenorm_bwd_{i}")
    grads = {k: jnp.stack([v.reshape(v.shape[-1]) if k in ("pre_norm", "post_norm", "ssm_gate_norm") else v for v in vs])
             for k, vs in gr.items() if k not in BIG}
    return loss_lanes, g, grads, first_pair_reduced, second_pair_reduced


N_CHIPS = 4
N_DEV = 8
MESH = pl.DeviceIdType.MESH
ANY = pl.BlockSpec(memory_space=pl.ANY)


def _place():
    x, y, c = lax.axis_index("x"), lax.axis_index("y"), lax.axis_index("c")
    return x, y, c, 2 * x + y


def _gather_sems(n):
    return [pltpu.SemaphoreType.DMA((n, N_CHIPS)), pltpu.SemaphoreType.DMA((n, N_CHIPS)), pltpu.SemaphoreType.DMA((n,))]


def _gather_between_chips(ins, outs, send_sems, recv_sems, local_sems, wait):
    n = len(ins)
    _, _, c, s = _place()
    local = [pltpu.make_async_copy(ins[w], outs[w].at[s], local_sems.at[w]) for w in range(n)]

    def remote(w, t):
        return pltpu.make_async_remote_copy(
            src_ref=ins[w].at[c], dst_ref=outs[w].at[s, c], send_sem=send_sems.at[w, t],
            recv_sem=recv_sems.at[w, s], device_id=(t // 2, t % 2, c), device_id_type=MESH)

    def arrival(w, t):
        return pltpu.make_async_remote_copy(
            src_ref=ins[w].at[c], dst_ref=outs[w].at[t, c], send_sem=send_sems.at[w, t],
            recv_sem=recv_sems.at[w, t], device_id=(t // 2, t % 2, c), device_id_type=MESH)

    if not wait:
        for cp in local:
            cp.start()
    for t in range(N_CHIPS):
        @pl.when(s != t)
        def _():
            for w in range(n):
                if wait:
                    remote(w, t).wait_send()
                    arrival(w, t).wait_recv()
                else:
                    remote(w, t).start()
    if wait:
        for cp in local:
            cp.wait()


def _pair_handoff(bufs, name):
    n = len(bufs)

    def body(*refs):
        outs = refs[n:2 * n]
        send_sems, recv_sems = refs[2 * n:]
        x, y, c, s = _place()

        def handed_on(w, t):
            return pltpu.make_async_remote_copy(
                src_ref=outs[w].at[t, c], dst_ref=outs[w].at[t, c], send_sem=send_sems.at[w, t],
                recv_sem=recv_sems.at[w, t], device_id=(x, y, 1 - c), device_id_type=MESH)

        def handed_in(w, t):
            return pltpu.make_async_remote_copy(
                src_ref=outs[w].at[t, 1 - c], dst_ref=outs[w].at[t, 1 - c], send_sem=send_sems.at[w, t],
                recv_sem=recv_sems.at[w, t], device_id=(x, y, 1 - c), device_id_type=MESH)

        for t in range(N_CHIPS):
            @pl.when(s != t)
            def _():
                for w in range(n):
                    handed_on(w, t).start()
        for t in range(N_CHIPS):
            @pl.when(s != t)
            def _():
                for w in range(n):
                    handed_on(w, t).wait_send()
                    handed_in(w, t).wait_recv()

    return pl.pallas_call(
        body, in_specs=[ANY] * n, out_specs=[ANY] * n,
        out_shape=[jax.ShapeDtypeStruct(a.shape, a.dtype) for a in bufs],
        scratch_shapes=[pltpu.SemaphoreType.DMA((n, N_CHIPS)), pltpu.SemaphoreType.DMA((n, N_CHIPS))],
        input_output_aliases={w: w for w in range(n)}, name=name,
    )(*bufs)


def _chip_gather(shards, name):
    n = len(shards)

    def body(*refs):
        ins, outs = refs[:n], refs[n:2 * n]
        _gather_between_chips(ins, outs, *refs[2 * n:], wait=False)
        _gather_between_chips(ins, outs, *refs[2 * n:], wait=True)

    bufs = pl.pallas_call(
        body, in_specs=[ANY] * n, out_specs=[ANY] * n,
        out_shape=[jax.ShapeDtypeStruct((N_CHIPS,) + a.shape, a.dtype) for a in shards],
        scratch_shapes=_gather_sems(n), name=name,
    )(*shards)
    return _pair_handoff(bufs, name + "_handoff")


def _pair_swap(parts, name):
    n = len(parts)

    def body(*refs):
        ins, outs = refs[:n], refs[n:2 * n]
        send_sems, recv_sems = refs[2 * n:]
        x, y, c, _ = _place()
        cps = [pltpu.make_async_remote_copy(
            src_ref=ins[w].at[1 - c], dst_ref=outs[w], send_sem=send_sems.at[w], recv_sem=recv_sems.at[w],
            device_id=(x, y, 1 - c), device_id_type=MESH) for w in range(n)]
        for cp in cps:
            cp.start()
        for cp in cps:
            cp.wait()

    return pl.pallas_call(
        body, in_specs=[ANY] * n, out_specs=[ANY] * n,
        out_shape=[jax.ShapeDtypeStruct(a.shape[1:], a.dtype) for a in parts],
        scratch_shapes=[pltpu.SemaphoreType.DMA((n,)), pltpu.SemaphoreType.DMA((n,))],
        name=name,
    )(*parts)


def _scatter_between_chips(ins, outs, send_sems, recv_sems, local_sems, wait):
    n = len(ins)
    _, _, c, s = _place()

    def block(w, t):
        rows = ins[w].shape[0] // N_CHIPS
        return ins[w].at[pl.ds(t * rows, rows)]

    local = [pltpu.make_async_copy(block(w, s), outs[w].at[s], local_sems.at[w]) for w in range(n)]

    def remote(w, t):
        return pltpu.make_async_remote_copy(
            src_ref=block(w, t), dst_ref=outs[w].at[s], send_sem=send_sems.at[w, t], recv_sem=recv_sems.at[w, s],
            device_id=(t // 2, t % 2, c), device_id_type=MESH)

    def arrival(w, t):
        return pltpu.make_async_remote_copy(
            src_ref=block(w, t), dst_ref=outs[w].at[t], send_sem=send_sems.at[w, t], recv_sem=recv_sems.at[w, t],
            device_id=(t // 2, t % 2, c), device_id_type=MESH)

    if not wait:
        for cp in local:
            cp.start()
    for t in range(N_CHIPS):
        @pl.when(s != t)
        def _():
            for w in range(n):
                if wait:
                    remote(w, t).wait_send()
                    arrival(w, t).wait_recv()
                else:
                    remote(w, t).start()
    if wait:
        for cp in local:
            cp.wait()


def _scatter_shapes(parts):
    return [jax.ShapeDtypeStruct((N_CHIPS, a.shape[0] // N_CHIPS, a.shape[1]), a.dtype) for a in parts]


def _pair_merge(parts, name):
    n = len(parts)

    def body(*refs):
        ins, outs = refs[:n], refs[n:2 * n]
        send_sems, recv_sems = refs[2 * n:]
        x, y, c, _ = _place()
        cps = [pltpu.make_async_remote_copy(
            src_ref=ins[w], dst_ref=outs[w], send_sem=send_sems.at[w], recv_sem=recv_sems.at[w],
            device_id=(x, y, 1 - c), device_id_type=MESH) for w in range(n)]
        for cp in cps:
            cp.start()
        for cp in cps:
            cp.wait()

    return pl.pallas_call(
        body, in_specs=[ANY] * n, out_specs=[ANY] * n,
        out_shape=[jax.ShapeDtypeStruct(a.shape, a.dtype) for a in parts],
        scratch_shapes=[pltpu.SemaphoreType.DMA((n,)), pltpu.SemaphoreType.DMA((n,))],
        name=name,
    )(*parts)


def _all_gather_small(a, name):
    def body(in_ref, out_ref, send_sems, recv_sems, local_sem):
        x, y, c, _ = _place()
        me = 4 * x + 2 * y + c
        local = pltpu.make_async_copy(in_ref, out_ref.at[me], local_sem)
        local.start()

        def remote(d):
            return pltpu.make_async_remote_copy(
                src_ref=in_ref, dst_ref=out_ref.at[me], send_sem=send_sems.at[d], recv_sem=recv_sems.at[me],
                device_id=(d // 4, (d // 2) % 2, d % 2), device_id_type=MESH)

        def arrival(d):
            return pltpu.make_async_remote_copy(
                src_ref=in_ref, dst_ref=out_ref.at[d], send_sem=send_sems.at[d], recv_sem=recv_sems.at[d],
                device_id=(d // 4, (d // 2) % 2, d % 2), device_id_type=MESH)

        for d in range(N_DEV):
            @pl.when(me != d)
            def _():
                remote(d).start()
        for d in range(N_DEV):
            @pl.when(me != d)
            def _():
                remote(d).wait_send()
                arrival(d).wait_recv()
        local.wait()

    return pl.pallas_call(
        body, in_specs=[ANY], out_specs=ANY, out_shape=jax.ShapeDtypeStruct((N_DEV,) + a.shape, a.dtype),
        scratch_shapes=[pltpu.SemaphoreType.DMA((N_DEV,)), pltpu.SemaphoreType.DMA((N_DEV,)), pltpu.SemaphoreType.DMA],
        name=name,
    )(a)


def _reduce_tile(rows):
    return _pick(rows, (256, 128, 16))


def _pair_add(full, other, layer, name):
    _, rows, cols = full.shape
    tr = _reduce_tile(rows)

    def body(layer_ref, a_ref, b_ref, o_ref):
        o_ref[...] = (a_ref[0] + b_ref[...]).astype(o_ref.dtype)

    return pl.pallas_call(
        body,
        grid_spec=pltpu.PrefetchScalarGridSpec(
            num_scalar_prefetch=1, grid=(rows // tr,),
            in_specs=[pl.BlockSpec((1, tr, cols), lambda i, lr: (lr[0], i, 0)), pl.BlockSpec((tr, cols), lambda i, lr: (i, 0))],
            out_specs=pl.BlockSpec((tr, cols), lambda i, lr: (i, 0))),
        out_shape=jax.ShapeDtypeStruct((rows, cols), BF16), compiler_params=_params("parallel"), name=name,
    )(layer, full, other)


def _sum_slots(a, name):
    n, rows, cols = a.shape
    tr = _reduce_tile(rows)

    def body(a_ref, o_ref):
        acc = a_ref[0].astype(F32)
        for k in range(1, n):
            acc = acc + a_ref[k].astype(F32)
        o_ref[...] = acc

    return pl.pallas_call(
        body, grid=(rows // tr,), in_specs=[pl.BlockSpec((n, tr, cols), lambda i: (0, i, 0))],
        out_specs=pl.BlockSpec((tr, cols), lambda i: (i, 0)),
        out_shape=jax.ShapeDtypeStruct((rows, cols), F32), compiler_params=_params("parallel"), name=name,
    )(a)


def _adamw(w, g, m, v, name):
    rows, cols = w.shape
    tr = _pick(rows, (256, 8))

    def body(w_ref, g_ref, m_ref, v_ref, d_ref, nm_ref, nv_ref):
        gv = g_ref[...]
        mn = ADAM_B1 * m_ref[...] + (1.0 - ADAM_B1) * gv
        vn = ADAM_B2 * v_ref[...] + (1.0 - ADAM_B2) * jnp.square(gv)
        m_hat = mn / (1.0 - ADAM_B1 ** ADAM_STEP)
        v_hat = vn / (1.0 - ADAM_B2 ** ADAM_STEP)
        d_ref[...] = -ADAM_LR * (m_hat / (jnp.sqrt(v_hat) + ADAM_EPS) + ADAM_WD * w_ref[...])
        nm_ref[...] = mn
        nv_ref[...] = vn

    blk = pl.BlockSpec((tr, cols), lambda i: (i, 0))
    return pl.pallas_call(
        body, grid=(rows // tr,), in_specs=[blk] * 4, out_specs=[blk] * 3,
        out_shape=[jax.ShapeDtypeStruct((rows, cols), F32)] * 3, compiler_params=_params("parallel"), name=name,
    )(w, g, m, v)


BIG = ("ssm_w_in", "ssm_w_out", "att_w_in", "att_w_out")
SHARDED = BIG + ("ssm_conv_w",)
SMALL = ("pre_norm", "post_norm", "ssm_conv_b", "ssm_dt_bias", "ssm_a_log", "ssm_d", "ssm_gate_norm", "att_sinks")
WEIGHTS = ("pre_norm", "post_norm", "ssm_w_in", "ssm_conv_w", "ssm_conv_b", "ssm_dt_bias", "ssm_a_log", "ssm_d",
           "ssm_gate_norm", "ssm_w_out", "att_w_in", "att_sinks", "att_w_out")


def _halves(a):
    return a.reshape(2, a.shape[0] // 2, a.shape[1])


def _layer_shards(j, ssm_w_in, ssm_w_out, att_w_in, att_w_out, ssm_conv_w):
    return [_halves(ssm_w_in[j].astype(BF16)), _halves(ssm_w_out[j].astype(BF16)), _halves(att_w_in[j].astype(BF16)),
            _halves(att_w_out[j].astype(BF16)), _halves(ssm_conv_w[j])]


SHARD_KEYS = ("ssm_w_in", "ssm_w_out", "att_w_in", "att_w_out", "ssm_conv_w")


def _whole_weights(keys, gathered):
    out = {}
    for k, g in zip(keys, gathered):
        g = g.reshape((N_CHIPS, 2 * g.shape[2], g.shape[3]))
        if k in ("ssm_w_out", "att_w_out"):
            out[k] = g.reshape(N_CHIPS * g.shape[1], g.shape[2])
        else:
            out[k] = jnp.transpose(g, (1, 0, 2)).reshape(g.shape[1], N_CHIPS * g.shape[2])
    if "ssm_w_in" in out:
        out["ssm_w_in"] = jnp.pad(out["ssm_w_in"], ((0, 0), (0, SSM_IN_PAD - SSM_IN_DIM)))
    return out


def _halves_by_chip(key, g):
    if key in ("ssm_w_out", "att_w_out"):
        rows = g.shape[0] // N_CHIPS
        blocks = g.reshape(N_CHIPS, 2, rows // 2, g.shape[1])
        return jnp.transpose(blocks, (1, 0, 2, 3)).reshape(2, N_CHIPS * (rows // 2), g.shape[1])
    cols = (SSM_IN_DIM if key == "ssm_w_in" else g.shape[1]) // N_CHIPS
    rows = g.shape[0]
    blocks = g[:, :N_CHIPS * cols].reshape(2, rows // 2, N_CHIPS, cols)
    return jnp.transpose(blocks, (0, 2, 1, 3)).reshape(2, N_CHIPS * (rows // 2), cols)


def _pack_small(tree, keys):
    flat = jnp.concatenate([tree[k].reshape(-1) for k in keys])
    rows = -(-flat.shape[0] // (8 * LANES)) * 8
    return jnp.pad(flat, (0, rows * LANES - flat.shape[0])).reshape(rows, LANES)


def _unpack_small(packed, shapes, keys):
    flat = packed.reshape(-1)
    out, at = {}, 0
    for k in keys:
        n = 1
        for dim in shapes[k]:
            n *= dim
        out[k] = flat[at:at + n].reshape(shapes[k])
        at += n
    return out


def kernel(x, positions, pre_norm, post_norm, ssm_w_in, ssm_conv_w, ssm_conv_b, ssm_dt_bias, ssm_a_log, ssm_d, ssm_gate_norm, ssm_w_out, att_w_in, att_sinks, att_w_out, loss_target, m_pre_norm, m_post_norm, m_ssm_w_in, m_ssm_conv_w, m_ssm_conv_b, m_ssm_dt_bias, m_ssm_a_log, m_ssm_d, m_ssm_gate_norm, m_ssm_w_out, m_att_w_in, m_att_sinks, m_att_w_out, v_pre_norm, v_post_norm, v_ssm_w_in, v_ssm_conv_w, v_ssm_conv_b, v_ssm_dt_bias, v_ssm_a_log, v_ssm_d, v_ssm_gate_norm, v_ssm_w_out, v_att_w_in, v_att_sinks, v_att_w_out):
    w = dict(pre_norm=pre_norm, post_norm=post_norm, ssm_w_in=ssm_w_in, ssm_conv_w=ssm_conv_w, ssm_conv_b=ssm_conv_b,
             ssm_dt_bias=ssm_dt_bias, ssm_a_log=ssm_a_log, ssm_d=ssm_d, ssm_gate_norm=ssm_gate_norm, ssm_w_out=ssm_w_out,
             att_w_in=att_w_in, att_sinks=att_sinks, att_w_out=att_w_out)
    m = dict(pre_norm=m_pre_norm, post_norm=m_post_norm, ssm_w_in=m_ssm_w_in, ssm_conv_w=m_ssm_conv_w, ssm_conv_b=m_ssm_conv_b,
             ssm_dt_bias=m_ssm_dt_bias, ssm_a_log=m_ssm_a_log, ssm_d=m_ssm_d, ssm_gate_norm=m_ssm_gate_norm,
             ssm_w_out=m_ssm_w_out, att_w_in=m_att_w_in, att_sinks=m_att_sinks, att_w_out=m_att_w_out)
    v = dict(pre_norm=v_pre_norm, post_norm=v_post_norm, ssm_w_in=v_ssm_w_in, ssm_conv_w=v_ssm_conv_w, ssm_conv_b=v_ssm_conv_b,
             ssm_dt_bias=v_ssm_dt_bias, ssm_a_log=v_ssm_a_log, ssm_d=v_ssm_d, ssm_gate_norm=v_ssm_gate_norm,
             ssm_w_out=v_ssm_w_out, att_w_in=v_att_w_in, att_sinks=v_att_sinks, att_w_out=v_att_w_out)
    c = lax.axis_index("c")
    chip = 2 * lax.axis_index("x") + lax.axis_index("y")

    sharded = (ssm_w_in, ssm_w_out, att_w_in, att_w_out, ssm_conv_w)
    own = [dict(zip(SHARD_KEYS, _layer_shards(j, *sharded))) for j in range(2)]
    now_keys = ("ssm_w_in", "ssm_conv_w")
    later_keys = ("ssm_w_out", "att_w_in", "att_w_out")
    first_in = _whole_weights(now_keys, _chip_gather([own[0][k] for k in now_keys], "gather_weights_0"))

    def in_proj_with_first_pair(matmul):
        proj, *arrived = matmul(ride=[own[0][k] for k in later_keys])
        return proj, _whole_weights(later_keys, _pair_handoff(arrived, "gather_weights_0_rest_handoff"))

    def scan_with_second_pair(scan):
        results = scan(ride=[own[1][k] for k in SHARD_KEYS])
        scanned, arrived = results[:-len(SHARD_KEYS)], results[-len(SHARD_KEYS):]
        return (*scanned, _whole_weights(SHARD_KEYS, _pair_handoff(arrived, "gather_weights_1_handoff")))

    half = jnp.reshape(c, (1,)).astype(jnp.int32)

    def reduce_begin(pair_grads, tag):
        parts = [_halves_by_chip(k, pair_grads[k]) for k in BIG]
        from_sibling = _pair_swap(parts, f"reduce_pair_swap_{tag}")
        return [_pair_add(p, o, half, f"reduce_pair_add_{tag}_{n}") for n, (p, o) in enumerate(zip(parts, from_sibling))]

    def reduce_end(by_chip, tag):
        mine = [_sum_slots(a, f"reduce_chip_sum_{tag}_{n}") for n, a in enumerate(by_chip)]
        theirs = _pair_merge(mine, f"reduce_pair_merge_{tag}")
        return {k: jnp.where(c == 0, jnp.concatenate([a, b]), jnp.concatenate([b, a])) for k, a, b in zip(BIG, mine, theirs)}

    def attn_bwd_with_second_pair_grads(attn_bwd, pair_grads):
        dq, dk, dv, dgate, dsk, *by_chip = attn_bwd(ride=reduce_begin(pair_grads, "1"))
        return (dq, dk, dv, dgate, dsk), reduce_end(by_chip, "1")

    def in_dx_with_first_pair_grads(matmul, pair_grads):
        dh, *by_chip = matmul(ride=reduce_begin(pair_grads, "0"), ride_scatters=True)
        return dh, reduce_end(by_chip, "0")

    loss_lanes, grad_x, gr, reduced_0, reduced_1 = _local_step(
        x[0], positions[0], pre_norm, post_norm, ssm_conv_b, ssm_dt_bias, ssm_a_log, ssm_d, ssm_gate_norm, att_sinks,
        loss_target[0], first_in, in_proj_with_first_pair, scan_with_second_pair, attn_bwd_with_second_pair_grads,
        in_dx_with_first_pair_grads)
    loss = lax.psum(0.5 * jnp.sum(loss_lanes) / D_MODEL, ("x", "y", "c"))
    grads = {k: jnp.stack([reduced_0[k], reduced_1[k]]) for k in BIG}

    small_keys = SMALL + ("ssm_conv_w",)
    small_shapes = {k: w[k].shape for k in SMALL}
    small_shapes["ssm_conv_w"] = gr["ssm_conv_w"].shape
    small_sum = _sum_slots(_all_gather_small(_pack_small(gr, small_keys), "reduce_small_gather"), "reduce_small_sum")
    grads.update(_unpack_small(small_sum, small_shapes, small_keys))
    conv_cols = ssm_conv_w.shape[2]
    grads["ssm_conv_w"] = lax.dynamic_slice_in_dim(grads["ssm_conv_w"], chip * conv_cols, conv_cols, axis=2)

    delta, new_m, new_v = {}, {}, {}
    for k in SHARDED:
        shp = w[k].shape
        two_d = (shp[0] * shp[1], shp[2])
        d_, m_, v_ = _adamw(w[k].reshape(two_d), grads[k].reshape(two_d), m[k].reshape(two_d), v[k].reshape(two_d),
                            f"adamw_{k}")
        delta[k], new_m[k], new_v[k] = d_.reshape(shp), m_.reshape(shp), v_.reshape(shp)
    d_, m_, v_ = _adamw(_pack_small(w, SMALL), _pack_small(grads, SMALL), _pack_small(m, SMALL), _pack_small(v, SMALL),
                        "adamw_small")
    delta.update(_unpack_small(d_, small_shapes, SMALL))
    new_m.update(_unpack_small(m_, small_shapes, SMALL))
    new_v.update(_unpack_small(v_, small_shapes, SMALL))

    return (loss, grad_x[None], *[grads[k] for k in WEIGHTS], *[delta[k] for k in WEIGHTS],
            *[new_m[k] for k in WEIGHTS], *[new_v[k] for k in WEIGHTS])
```

```python
import functools

import jax
import jax.numpy as jnp
from jax import lax
from jax.experimental import pallas as pl
from jax.experimental.pallas import tpu as pltpu

F32 = jnp.float32
BF16 = jnp.bfloat16
EPS = 1e-6
NEG_INF = float("-inf")

D_MODEL = 1024
DEPTH = 4
SSM_D_INNER = 2048
SSM_HEAD_DIM = 64
SSM_HEADS = 32
SSM_GROUPS = 8
SSM_HPG = 4
SSM_STATE = 128
SSM_CONV = 4
SSM_CHUNK = 128
SSM_BC_DIM = 1024
SSM_CONV_DIM = 4096
SSM_IN_DIM = 6176
SSM_IN_PAD = 6272
SSM_DT_PAD = 128
ATT_HEAD_DIM = 64
ATT_Q_HEADS = 16
ATT_KV_HEADS = 4
ATT_GQA = 4
ATT_WIDTH = 1024
ATT_KV_WIDTH = 256
ATT_IN_DIM = 2560
ATT_QKV = ATT_WIDTH + 2 * ATT_KV_WIDTH
ATT_BLOCK = 128
ROPE_THETA = 500000.0
ROPE_DIM = 16
ROPE_HALF = 8
Q_SCALE = ATT_HEAD_DIM ** -0.5

ADAM_LR = 0.001
ADAM_B1 = 0.9
ADAM_B2 = 0.999
ADAM_EPS = 1e-08
ADAM_WD = 0.01
ADAM_STEP = 10

VMEM_LIMIT_BYTES = 48 * 1024 * 1024
NT_DIMS = (((1,), (1,)), ((), ()))
TN_DIMS = (((0,), (0,)), ((), ()))


def _params(*sem):
    return pltpu.CompilerParams(dimension_semantics=sem, vmem_limit_bytes=VMEM_LIMIT_BYTES)


def _pick(n, cands):
    for c in cands:
        if n % c == 0:
            return c
    return n


def _sigmoid(v):
    return 0.5 * jnp.tanh(0.5 * v) + 0.5


def _bdot_tn(a, b):
    return lax.dot_general(a.astype(BF16), b.astype(BF16), TN_DIMS, preferred_element_type=F32)


MATMUL_VMEM_BUDGET = 36 * 1024 * 1024


def _matmul_tiles(m, n, k, out_bytes, reduce_rows):
    best = None
    whole = [k] if (not reduce_rows or k <= 2048) else []
    for tk in whole + [c for c in (4096, 2048, 1024, 896, 512) if k % c == 0 and c < k]:
        for tm in (c for c in (2048, 1024, 512, 256) if m % c == 0):
            for tn in (c for c in (n, 1280, 1024, 896, 640, 512) if n % c == 0):
                acc = tm * tn * 4 if tk < k else 0
                need = 2 * (2 * tk * (tm + tn) + tm * tn * out_bytes) + acc
                if need <= MATMUL_VMEM_BUDGET and (best is None or tm * tn * min(tk, 2048) > best[0]):
                    best = (tm * tn * min(tk, 2048), tm, tn, tk)
        if best is not None and not reduce_rows:
            break
    return best[1:]


def _matmul(a, b, mode, out_dtype, name, ride=(), ride_scatters=False):
    if mode == "nn":
        (m, k), n = a.shape, b.shape[1]
    elif mode == "nt":
        (m, k), n = a.shape, b.shape[0]
    else:
        (k, m), n = a.shape, b.shape[1]
    tm, tn, tk = _matmul_tiles(m, n, k, jnp.dtype(out_dtype).itemsize, mode == "tn")
    nk = k // tk
    steps = (n // tn, m // tm, nk)
    dims = {"nn": (((1,), (0,)), ((), ())), "nt": NT_DIMS, "tn": TN_DIMS}[mode]
    n_ride = len(ride)
    exchange = _scatter_between_chips if ride_scatters else _gather_between_chips
    arrived = _scatter_shapes(ride) if ride_scatters else [jax.ShapeDtypeStruct((N_CHIPS,) + r.shape, r.dtype) for r in ride]

    def body(*refs):
        a_ref, b_ref = refs[:2]
        ride_in = refs[2:2 + n_ride]
        o_ref = refs[2 + n_ride]
        ride_out = refs[3 + n_ride:3 + 2 * n_ride]
        acc_ref = refs[3 + 2 * n_ride]
        ride_sems = refs[4 + 2 * n_ride:]
        kk = pl.program_id(2)
        at = [pl.program_id(d) for d in range(3)]
        if n_ride:
            @pl.when((at[0] == 0) & (at[1] == 0) & (at[2] == 0))
            def _():
                exchange(ride_in, ride_out, *ride_sems, wait=False)

        part = lax.dot_general(a_ref[...], b_ref[...], dims, preferred_element_type=F32)
        if nk == 1:
            o_ref[...] = part.astype(o_ref.dtype)
        else:
            @pl.when(kk == 0)
            def _():
                acc_ref[...] = part

            @pl.when(kk > 0)
            def _():
                acc_ref[...] += part

            @pl.when(kk == nk - 1)
            def _():
                o_ref[...] = acc_ref[...].astype(o_ref.dtype)

        if n_ride:
            @pl.when((at[0] == steps[0] - 1) & (at[1] == steps[1] - 1) & (at[2] == steps[2] - 1))
            def _():
                exchange(ride_in, ride_out, *ride_sems, wait=True)

    if mode == "nn":
        a_spec = pl.BlockSpec((tm, tk), lambda j, i, kk: (i, kk))
        b_spec = pl.BlockSpec((tk, tn), lambda j, i, kk: (kk, j))
    elif mode == "nt":
        a_spec = pl.BlockSpec((tm, tk), lambda j, i, kk: (i, kk))
        b_spec = pl.BlockSpec((tn, tk), lambda j, i, kk: (j, kk))
    else:
        a_spec = pl.BlockSpec((tk, tm), lambda j, i, kk: (kk, i))
        b_spec = pl.BlockSpec((tk, tn), lambda j, i, kk: (kk, j))
    out = pl.pallas_call(
        body, grid=steps, in_specs=[a_spec, b_spec] + [ANY] * n_ride,
        out_specs=[pl.BlockSpec((tm, tn), lambda j, i, kk: (i, j))] + [ANY] * n_ride,
        out_shape=[jax.ShapeDtypeStruct((m, n), out_dtype)] + arrived,
        scratch_shapes=[pltpu.VMEM((tm, tn), F32)] + (_gather_sems(n_ride) if n_ride else []),
        compiler_params=_params(*(["arbitrary"] * 3 if n_ride else ["parallel", "parallel", "arbitrary"])), name=name,
    )(a, b, *ride)
    return out if n_ride else out[0]


def _row_tile(l):
    return _pick(l, (512, 256, 128))


def _rmsnorm_fwd(x, w, name, ride=()):
    l, d = x.shape
    tl = _row_tile(l)
    n_ride = len(ride)
    steps = l // tl

    def body(*refs):
        x_ref, w_ref = refs[:2]
        ride_in = refs[2:2 + n_ride]
        o_ref = refs[2 + n_ride]
        ride_out = refs[3 + n_ride:3 + 2 * n_ride]
        ride_sems = refs[3 + 2 * n_ride:]
        if n_ride:
            @pl.when(pl.program_id(0) == 0)
            def _():
                _gather_between_chips(ride_in, ride_out, *ride_sems, wait=False)

        xv = x_ref[...]
        r = lax.rsqrt(jnp.mean(xv * xv, axis=-1, keepdims=True) + EPS)
        o_ref[...] = (xv * r * w_ref[...]).astype(o_ref.dtype)
        if n_ride:
            @pl.when(pl.program_id(0) == steps - 1)
            def _():
                _gather_between_chips(ride_in, ride_out, *ride_sems, wait=True)

    out = pl.pallas_call(
        body, grid=(steps,),
        in_specs=[pl.BlockSpec((tl, d), lambda i: (i, 0)), pl.BlockSpec((1, d), lambda i: (0, 0))] + [ANY] * n_ride,
        out_specs=[pl.BlockSpec((tl, d), lambda i: (i, 0))] + [ANY] * n_ride,
        out_shape=[jax.ShapeDtypeStruct((l, d), BF16)] + [jax.ShapeDtypeStruct((N_CHIPS,) + r.shape, r.dtype) for r in ride],
        scratch_shapes=_gather_sems(n_ride) if n_ride else [],
        compiler_params=_params("arbitrary" if n_ride else "parallel"), name=name,
    )(x, w.reshape(1, d), *ride)
    return out if n_ride else out[0]


def _post_fwd(x, y, w, w_next, name):
    l, d = x.shape
    tl = _row_tile(l)

    def body(x_ref, y_ref, w_ref, wn_ref, o_ref, h_ref):
        yv = y_ref[...]
        r = lax.rsqrt(jnp.mean(yv * yv, axis=-1, keepdims=True) + EPS)
        out = x_ref[...] + yv * r * w_ref[...]
        o_ref[...] = out
        rn = lax.rsqrt(jnp.mean(out * out, axis=-1, keepdims=True) + EPS)
        h_ref[...] = (out * rn * wn_ref[...]).astype(h_ref.dtype)

    row = pl.BlockSpec((tl, d), lambda i: (i, 0))
    vec = pl.BlockSpec((1, d), lambda i: (0, 0))
    return pl.pallas_call(
        body, grid=(l // tl,), in_specs=[row, row, vec, vec], out_specs=[row, row],
        out_shape=[jax.ShapeDtypeStruct((l, d), F32), jax.ShapeDtypeStruct((l, d), BF16)],
        compiler_params=_params("parallel"), name=name,
    )(x, y, w.reshape(1, d), w_next.reshape(1, d))


def _post_loss(x, y, w, t, name):
    l, d = x.shape
    tl = _row_tile(l)
    nt = l // tl

    def body(x_ref, y_ref, w_ref, t_ref, g_ref, dy_ref, ls_ref, dw_ref, acc_ref):
        i = pl.program_id(0)

        @pl.when(i == 0)
        def _():
            ls_ref[...] = jnp.zeros_like(ls_ref)
            acc_ref[...] = jnp.zeros_like(acc_ref)

        yv = y_ref[...]
        r = lax.rsqrt(jnp.mean(yv * yv, axis=-1, keepdims=True) + EPS)
        nrm = yv * r
        e = x_ref[...] + nrm * w_ref[...] - t_ref[...]
        gv = e * (1.0 / d)
        g_ref[...] = gv
        ls_ref[...] += jnp.sum((e * e).reshape(tl // 8, 8, d), axis=0)
        gw = gv * w_ref[...]
        dy_ref[...] = (r * (gw - nrm * jnp.mean(gw * nrm, axis=-1, keepdims=True))).astype(dy_ref.dtype)
        acc_ref[...] += jnp.sum((gv * nrm).reshape(tl // 8, 8, d), axis=0)

        @pl.when(i == nt - 1)
        def _():
            dw_ref[...] = jnp.sum(acc_ref[...], axis=0, keepdims=True)

    row = pl.BlockSpec((tl, d), lambda i: (i, 0))
    vec = pl.BlockSpec((1, d), lambda i: (0, 0))
    return pl.pallas_call(
        body, grid=(nt,), in_specs=[row, row, vec, row],
        out_specs=[row, row, pl.BlockSpec((8, d), lambda i: (0, 0)), vec],
        out_shape=[jax.ShapeDtypeStruct((l, d), F32), jax.ShapeDtypeStruct((l, d), BF16),
                   jax.ShapeDtypeStruct((8, d), F32), jax.ShapeDtypeStruct((1, d), F32)],
        scratch_shapes=[pltpu.VMEM((8, d), F32)], compiler_params=_params("arbitrary"), name=name,
    )(x, y, w.reshape(1, d), t)


def _norm_bwd_chain(dh, x, w_pre, resid, y_prev, w_post_prev, name):
    l, d = x.shape
    tl = _row_tile(l)
    nt = l // tl

    def body(dh_ref, x_ref, wp_ref, r_ref, y_ref, wq_ref, g_ref, dy_ref, dwp_ref, dwq_ref, accp_ref, accq_ref):
        i = pl.program_id(0)

        @pl.when(i == 0)
        def _():
            accp_ref[...] = jnp.zeros_like(accp_ref)
            accq_ref[...] = jnp.zeros_like(accq_ref)

        xv = x_ref[...]
        dhv = dh_ref[...]
        rx = lax.rsqrt(jnp.mean(xv * xv, axis=-1, keepdims=True) + EPS)
        nx = xv * rx
        gw = dhv * wp_ref[...]
        gv = rx * (gw - nx * jnp.mean(gw * nx, axis=-1, keepdims=True)) + r_ref[...]
        g_ref[...] = gv
        accp_ref[...] += jnp.sum((dhv * nx).reshape(tl // 8, 8, d), axis=0)
        yv = y_ref[...]
        ry = lax.rsqrt(jnp.mean(yv * yv, axis=-1, keepdims=True) + EPS)
        ny = yv * ry
        gq = gv * wq_ref[...]
        dy_ref[...] = (ry * (gq - ny * jnp.mean(gq * ny, axis=-1, keepdims=True))).astype(dy_ref.dtype)
        accq_ref[...] += jnp.sum((gv * ny).reshape(tl // 8, 8, d), axis=0)

        @pl.when(i == nt - 1)
        def _():
            dwp_ref[...] = jnp.sum(accp_ref[...], axis=0, keepdims=True)
            dwq_ref[...] = jnp.sum(accq_ref[...], axis=0, keepdims=True)

    row = pl.BlockSpec((tl, d), lambda i: (i, 0))
    vec = pl.BlockSpec((1, d), lambda i: (0, 0))
    return pl.pallas_call(
        body, grid=(nt,), in_specs=[row, row, vec, row, row, vec], out_specs=[row, row, vec, vec],
        out_shape=[jax.ShapeDtypeStruct((l, d), F32), jax.ShapeDtypeStruct((l, d), BF16),
                   jax.ShapeDtypeStruct((1, d), F32), jax.ShapeDtypeStruct((1, d), F32)],
        scratch_shapes=[pltpu.VMEM((8, d), F32), pltpu.VMEM((8, d), F32)],
        compiler_params=_params("arbitrary"), name=name,
    )(dh, x, w_pre.reshape(1, d), resid, y_prev, w_post_prev.reshape(1, d))


def _rmsnorm_bwd(g, y, w, resid, name):
    l, d = y.shape
    tl = _row_tile(l)
    nt = l // tl

    def body(g_ref, y_ref, w_ref, r_ref, dy_ref, dw_ref, acc_ref):
        i = pl.program_id(0)

        @pl.when(i == 0)
        def _():
            acc_ref[...] = jnp.zeros_like(acc_ref)

        yv = y_ref[...]
        gv = g_ref[...]
        r = lax.rsqrt(jnp.mean(yv * yv, axis=-1, keepdims=True) + EPS)
        nrm = yv * r
        gw = gv * w_ref[...]
        dy_ref[...] = r * (gw - nrm * jnp.mean(gw * nrm, axis=-1, keepdims=True)) + r_ref[...]
        acc_ref[...] += jnp.sum((gv * nrm).reshape(tl // 8, 8, d), axis=0)

        @pl.when(i == nt - 1)
        def _():
            dw_ref[...] = jnp.sum(acc_ref[...], axis=0, keepdims=True)

    row = pl.BlockSpec((tl, d), lambda i: (i, 0))
    vec = pl.BlockSpec((1, d), lambda i: (0, 0))
    return pl.pallas_call(
        body, grid=(nt,), in_specs=[row, row, vec, row], out_specs=[row, vec],
        out_shape=[jax.ShapeDtypeStruct((l, d), F32), jax.ShapeDtypeStruct((1, d), F32)],
        scratch_shapes=[pltpu.VMEM((8, d), F32)], compiler_params=_params("arbitrary"), name=name,
    )(g, y, w.reshape(1, d), resid)


HALO = 8
CONV_SUB_ROWS = 64
CONV_SUB_COLS = 256


DT_COL_BLOCK = (SSM_D_INNER + SSM_CONV_DIM) // SSM_DT_PAD


def _split3(v):
    hi = v.astype(BF16)
    rest = v - hi.astype(F32)
    mid = rest.astype(BF16)
    lo = (rest - mid.astype(F32)).astype(BF16)
    return hi, mid, lo


def _dt_and_decay(v, a_log):
    head_dim_log2 = SSM_HEAD_DIM.bit_length() - 1
    dt_hi, dt_mid, _ = _split3(jnp.maximum(v, 0.0) + jnp.log1p(jnp.exp(-jnp.abs(v))))
    dt = dt_hi.astype(F32) + dt_mid.astype(F32)
    ri = lax.broadcasted_iota(jnp.int32, (SSM_CHUNK, SSM_CHUNK), 0)
    cj = lax.broadcasted_iota(jnp.int32, (SSM_CHUNK, SSM_CHUNK), 1)
    tri = (ri >= cj).astype(BF16)
    acs_pieces = _split3(sum(jnp.dot(tri, piece, preferred_element_type=F32)
                             for piece in _split3(dt * (-jnp.exp(a_log)))))
    acs = sum(piece.astype(F32) for piece in acs_pieces)
    head_of_lane = lax.shift_right_logical(lax.broadcasted_iota(jnp.int32, (SSM_DT_PAD, SSM_D_INNER), 1), head_dim_log2)
    spread = (head_of_lane == lax.broadcasted_iota(jnp.int32, (SSM_DT_PAD, SSM_D_INNER), 0)).astype(BF16)
    dtb = sum(jnp.dot(piece, spread, preferred_element_type=F32) for piece in (dt_hi, dt_mid))
    acsb = sum(jnp.dot(piece, spread, preferred_element_type=F32) for piece in acs_pieces)
    return dtb, acsb, dt.T, acs.T


def _dt_bwd(ddt, proj, bias, dproj, name):
    l = proj.shape[0]
    tl = _row_tile(l)

    def body(g_ref, p_ref, b_ref, _, o_ref, db_ref):
        @pl.when(pl.program_id(0) == 0)
        def _():
            db_ref[...] = jnp.zeros_like(db_ref)

        d = g_ref[...] * _sigmoid(p_ref[...] + b_ref[...])
        o_ref[...] = d.astype(o_ref.dtype)
        db_ref[...] += jnp.sum(d, axis=0, keepdims=True)

    return pl.pallas_call(
        body, grid=(l // tl,),
        in_specs=[pl.BlockSpec((tl, SSM_DT_PAD), lambda i: (i, 0)),
                  pl.BlockSpec((tl, SSM_DT_PAD), lambda i: (i, DT_COL_BLOCK)),
                  pl.BlockSpec((1, SSM_DT_PAD), lambda i: (0, 0)),
                  pl.BlockSpec(memory_space=pl.ANY)],
        out_specs=[pl.BlockSpec((tl, SSM_DT_PAD), lambda i: (i, DT_COL_BLOCK)),
                   pl.BlockSpec((1, SSM_DT_PAD), lambda i: (0, 0))],
        out_shape=[jax.ShapeDtypeStruct(dproj.shape, dproj.dtype), jax.ShapeDtypeStruct((1, SSM_DT_PAD), F32)],
        input_output_aliases={3: 0}, compiler_params=_params("arbitrary"), name=name,
    )(ddt, proj, bias, dproj)


GP = SSM_HPG * SSM_HEAD_DIM
HEAD_DIM_LOG2 = SSM_HEAD_DIM.bit_length() - 1
GPS = SSM_GROUPS
B_BLOCK0 = SSM_D_INNER // SSM_STATE
C_BLOCK0 = (SSM_D_INNER + SSM_BC_DIM) // SSM_STATE


def _chunk_iotas():
    ri = lax.broadcasted_iota(jnp.int32, (SSM_CHUNK, SSM_CHUNK), 0)
    cj = lax.broadcasted_iota(jnp.int32, (SSM_CHUNK, SSM_CHUNK), 1)
    return ri, cj


def _head_decay(acsb, acs_r, r, ri, cj):
    pair = acsb[:, (r // 2) * LANES:(r // 2 + 1) * LANES]
    mine_low = r % 2 == 0
    lane = lax.broadcasted_iota(jnp.int32, (1, LANES), 1)
    col = jnp.where((lane < SSM_HEAD_DIM) == mine_low, pair, pltpu.roll(pair, SSM_HEAD_DIM, 1))
    return jnp.exp(jnp.where(ri >= cj, col - acs_r[r:r + 1, :], NEG_INF))


def _head_masked_rows(v, dtype):
    head_of_lane = lax.shift_right_logical(lax.broadcasted_iota(jnp.int32, (1, GP), 1), HEAD_DIM_LOG2)
    narrow = v.astype(dtype)
    return jnp.concatenate([jnp.where(head_of_lane == r, narrow, jnp.zeros_like(narrow)) for r in range(SSM_HPG)], axis=0)


def _ssd_fwd(proj, cw, cb, dt_bias, a_log, d_lanes, gate_w, name, ride=()):
    l = proj.shape[0]
    nc = l // SSM_CHUNK
    assert GPS == SSM_GROUPS
    n_ride = len(ride)
    halo_blocks = SSM_CHUNK // HALO
    x_block = 1

    def body(*refs):
        u0_ref, u1_ref, h0_ref, h1_ref, cw_ref, cb_ref, dtraw_ref, bias_ref, alog_ref, d_ref, z_ref, gw_ref = refs[:12]
        ride_in = refs[12:12 + n_ride]
        (y_ref, act_ref, hin_ref, pre_ref, xbc_ref, dtb_out, acsb_out, dtr_out, acsr_out) = refs[12 + n_ride:21 + n_ride]
        ride_out = refs[21 + n_ride:21 + 2 * n_ride]
        h_ref, ext_ref, conv_ref, dtb_ref, acsb_ref, acsr_ref = refs[21 + 2 * n_ride:27 + 2 * n_ride]
        ride_sems = refs[27 + 2 * n_ride:]
        s = pl.program_id(0)
        if n_ride:
            @pl.when(s == 0)
            def _():
                _gather_between_chips(ride_in, ride_out, *ride_sems, wait=False)

            @pl.when(s == nc)
            def _():
                _gather_between_chips(ride_in, ride_out, *ride_sems, wait=True)

        @pl.when(s <= 1)
        def _():
            h_ref[...] = jnp.zeros_like(h_ref)

        @pl.when(s == 0)
        def _():
            conv_ref[1] = jnp.zeros((SSM_CHUNK, SSM_CONV_DIM), BF16)
            dtb_ref[1] = jnp.zeros((SSM_CHUNK, SSM_D_INNER), F32)
            acsb_ref[1] = jnp.zeros((SSM_CHUNK, SSM_D_INNER), F32)
            acsr_ref[1] = jnp.zeros((SSM_GROUPS, SSM_HPG, SSM_CHUNK), F32)

        conv_slot = s & 1
        scan_slot = (s - 1) & 1
        for half, (u_ref, hl_ref) in enumerate(((u0_ref, h0_ref), (u1_ref, h1_ref))):
            hc = slice(half * SSM_D_INNER, (half + 1) * SSM_D_INNER)
            ext_ref[0:HALO, hc] = jnp.where(s > 0, hl_ref[...], 0.0)
            ext_ref[HALO:HALO + SSM_CHUNK, hc] = u_ref[...]

        def conv_columns(c_lo, c_hi):
            for r0 in range(0, SSM_CHUNK, CONV_SUB_ROWS):
                for c0 in range(c_lo, c_hi, CONV_SUB_COLS):
                    cs = slice(c0, c0 + CONV_SUB_COLS)
                    ext = ext_ref[r0:r0 + CONV_SUB_ROWS + HALO, cs]
                    acc = cb_ref[:, cs] + cw_ref[SSM_CONV - 1:SSM_CONV, cs] * ext[HALO:]
                    for k in range(SSM_CONV - 1):
                        acc = acc + cw_ref[k:k + 1, cs] * pltpu.roll(ext, SSM_CONV - 1 - k, 0)[HALO:]
                    act = (acc * _sigmoid(acc)).astype(BF16)
                    pre_ref[r0:r0 + CONV_SUB_ROWS, cs] = acc.astype(pre_ref.dtype)
                    xbc_ref[r0:r0 + CONV_SUB_ROWS, cs] = act
                    conv_ref[conv_slot, r0:r0 + CONV_SUB_ROWS, cs] = act

        ri, cj = _chunk_iotas()
        conv_share = SSM_CONV_DIM // GPS
        for k in range(GPS):
            g = k
            cols = slice(k * GP, (k + 1) * GP)
            bcols = slice(SSM_D_INNER + k * SSM_STATE, SSM_D_INNER + (k + 1) * SSM_STATE)
            ccols = slice(SSM_D_INNER + SSM_BC_DIM + k * SSM_STATE, SSM_D_INNER + SSM_BC_DIM + (k + 1) * SSM_STATE)
            xv = conv_ref[scan_slot, :, cols].astype(F32)
            bb = conv_ref[scan_slot, :, bcols]
            cb16 = conv_ref[scan_slot, :, ccols]
            acs_v = acsb_ref[scan_slot, :, cols]
            acs_r_v = acsr_ref[scan_slot, k]
            lastb = acs_v[SSM_CHUNK - 1:SSM_CHUNK, :]
            xd = xv * dtb_ref[scan_slot, :, cols]
            cbm = lax.dot_general(cb16, bb, NT_DIMS, preferred_element_type=F32)
            hin = h_ref[g]
            hin_ref[0, k] = hin
            yoff = jnp.dot(cb16, hin.astype(BF16), preferred_element_type=F32)
            ms = [(cbm * _head_decay(acs_v, acs_r_v, r, ri, cj)).astype(BF16) for r in range(SSM_HPG)]
            ydiag = jnp.dot(jnp.concatenate(ms, axis=1), _head_masked_rows(xd, BF16), preferred_element_type=F32)
            y_ref[:, cols] = ydiag + jnp.exp(acs_v) * yoff + d_ref[k] * xv
            h_ref[g] = hin * jnp.exp(lastb) + _bdot_tn(bb, xd * jnp.exp(lastb - acs_v))
            conv_columns(k * conv_share, (k + 1) * conv_share)
        z = z_ref[...]
        yg = y_ref[...] * (z * _sigmoid(z))
        r = lax.rsqrt(jnp.mean(yg * yg, axis=-1, keepdims=True) + EPS)
        act_ref[...] = (yg * r * gw_ref[...]).astype(act_ref.dtype)

        dtb, acsb, dt_rows, acs_rows = _dt_and_decay(dtraw_ref[...] + bias_ref[...], alog_ref[...])
        dtb_out[...] = dtb
        acsb_out[...] = acsb
        dtb_ref[conv_slot] = dtb
        acsb_ref[conv_slot] = acsb
        for g in range(SSM_GROUPS):
            heads = slice(g * SSM_HPG, (g + 1) * SSM_HPG)
            dtr_out[g] = dt_rows[heads, :]
            acsr_out[g] = acs_rows[heads, :]
            acsr_ref[conv_slot, g] = acs_rows[heads, :]

    def conv_at(s):
        return jnp.minimum(s, nc - 1)

    def scan_at(s):
        return jnp.maximum(s - 1, 0)

    lanes = pl.BlockSpec((SSM_CHUNK, SSM_D_INNER), lambda s: (scan_at(s), 0))
    conv_out = pl.BlockSpec((SSM_CHUNK, SSM_CONV_DIM), lambda s: (conv_at(s), 0))
    lanes_ahead = pl.BlockSpec((SSM_CHUNK, SSM_D_INNER), lambda s: (conv_at(s), 0))
    rows_ahead = pl.BlockSpec((SSM_GROUPS, SSM_HPG, SSM_CHUNK), lambda s: (0, 0, conv_at(s)))
    return pl.pallas_call(
        body, grid=(nc + 1,),
        in_specs=[pl.BlockSpec((SSM_CHUNK, SSM_D_INNER), lambda s: (conv_at(s), x_block)),
                  pl.BlockSpec((SSM_CHUNK, SSM_D_INNER), lambda s: (conv_at(s), x_block + 1)),
                  pl.BlockSpec((HALO, SSM_D_INNER), lambda s: (jnp.maximum(conv_at(s) * halo_blocks - 1, 0), x_block)),
                  pl.BlockSpec((HALO, SSM_D_INNER), lambda s: (jnp.maximum(conv_at(s) * halo_blocks - 1, 0), x_block + 1)),
                  pl.BlockSpec((SSM_CONV, SSM_CONV_DIM), lambda s: (0, 0)),
                  pl.BlockSpec((1, SSM_CONV_DIM), lambda s: (0, 0)),
                  pl.BlockSpec((SSM_CHUNK, SSM_DT_PAD), lambda s: (conv_at(s), DT_COL_BLOCK)),
                  pl.BlockSpec((1, SSM_DT_PAD), lambda s: (0, 0)),
                  pl.BlockSpec((1, SSM_DT_PAD), lambda s: (0, 0)),
                  pl.BlockSpec((SSM_GROUPS, 1, GP), lambda s: (0, 0, 0)),
                  lanes, pl.BlockSpec((1, SSM_D_INNER), lambda s: (0, 0))] + [ANY] * n_ride,
        out_specs=[lanes, lanes, pl.BlockSpec((1, SSM_GROUPS, SSM_STATE, GP), lambda s: (scan_at(s), 0, 0, 0)),
                   conv_out, conv_out, lanes_ahead, lanes_ahead, rows_ahead, rows_ahead] + [ANY] * n_ride,
        out_shape=[jax.ShapeDtypeStruct((l, SSM_D_INNER), F32), jax.ShapeDtypeStruct((l, SSM_D_INNER), BF16),
                   jax.ShapeDtypeStruct((nc, SSM_GROUPS, SSM_STATE, GP), F32),
                   jax.ShapeDtypeStruct((l, SSM_CONV_DIM), BF16), jax.ShapeDtypeStruct((l, SSM_CONV_DIM), BF16),
                   jax.ShapeDtypeStruct((l, SSM_D_INNER), F32), jax.ShapeDtypeStruct((l, SSM_D_INNER), F32),
                   jax.ShapeDtypeStruct((SSM_GROUPS, SSM_HPG, l), F32),
                   jax.ShapeDtypeStruct((SSM_GROUPS, SSM_HPG, l), F32)]
        + [jax.ShapeDtypeStruct((N_CHIPS,) + a.shape, a.dtype) for a in ride],
        scratch_shapes=[pltpu.VMEM((SSM_GROUPS, SSM_STATE, GP), F32),
                        pltpu.VMEM((SSM_CHUNK + HALO, SSM_CONV_DIM), F32),
                        pltpu.VMEM((2, SSM_CHUNK, SSM_CONV_DIM), BF16),
                        pltpu.VMEM((2, SSM_CHUNK, SSM_D_INNER), F32), pltpu.VMEM((2, SSM_CHUNK, SSM_D_INNER), F32),
                        pltpu.VMEM((2, SSM_GROUPS, SSM_HPG, SSM_CHUNK), F32)] + (_gather_sems(n_ride) if n_ride else []),
        compiler_params=_params("arbitrary"), name=name,
    )(proj, proj, proj, proj, cw, cb.reshape(1, SSM_CONV_DIM), proj, dt_bias, a_log, d_lanes, proj,
      gate_w.reshape(1, SSM_D_INNER), *ride)


def _ssd_bwd(xbc, pre, cw, dtb, acsb, dtr, acs_r, a_log, d_lanes, hin, dact, y, proj, gate_w, name):
    l = xbc.shape[0]
    nc = l // SSM_CHUNK
    sub = CONV_SUB_ROWS

    def body(x_ref, b_ref, c_ref, dtb_ref, acsb_ref, dtr_ref, acsr_ref, alc_ref, d_ref, hin_ref,
             dact_ref, y_ref, z_ref, gw_ref, pre_ref, u0_ref, u1_ref, cw_ref,
             dproj_ref, ddt_ref, dal_ref, dd_ref, dgw_ref, dcw_ref, dcb_ref,
             dh_ref, dy_ref, acc_ref, dxbc_s, dz_s, ddt_s, carry_ref, ext_ref, dcw_acc, dcb_acc):
        step = pl.program_id(0)
        live = (step < nc).astype(F32)
        stage = step & 1
        staged = (step - 1) & 1

        @pl.when(step == 0)
        def _():
            for ref in (dal_ref, dd_ref, acc_ref, dh_ref, carry_ref, dcw_acc, dcb_acc):
                ref[...] = jnp.zeros_like(ref)
            dxbc_s[1] = jnp.zeros((SSM_CHUNK, SSM_CONV_DIM), F32)
            dz_s[1] = jnp.zeros((SSM_CHUNK, SSM_D_INNER), BF16)
            ddt_s[1] = jnp.zeros((SSM_GROUPS, SSM_HPG, SSM_CHUNK), F32)

        z = z_ref[...]
        yv = y_ref[...]
        s = _sigmoid(z)
        sz = z * s
        yg = yv * sz
        r = lax.rsqrt(jnp.mean(yg * yg, axis=-1, keepdims=True) + EPS)
        nrm = yg * r
        gv = dact_ref[...]
        gw = gv * gw_ref[...]
        dyg = r * (gw - nrm * jnp.mean(gw * nrm, axis=-1, keepdims=True))
        dy_ref[...] = dyg * sz
        dz_s[stage] = (dyg * yv * (s * (1.0 + z * (1.0 - s)))).astype(BF16)
        acc_ref[...] += live * jnp.sum((gv * nrm).reshape(SSM_CHUNK // 8, 8, SSM_D_INNER), axis=0)

        p = pre_ref[...].astype(F32)
        sp = _sigmoid(p)
        dp = dxbc_s[staged] * (sp * (1.0 + p * (1.0 - sp)))
        ext_ref[0:SSM_CHUNK, :] = dp
        ext_ref[SSM_CHUNK:SSM_CHUNK + HALO, :] = carry_ref[...]
        carry_ref[...] = dp[0:HALO]
        dproj_ref[:, 0:SSM_D_INNER] = dz_s[staged]
        ddt_ref[...] = ddt_s[staged]

        def fold(v):
            return jnp.sum(v.reshape(sub // 8, 8, CONV_SUB_COLS), axis=0)

        def conv_columns(c_lo, c_hi):
            for c0 in range(c_lo, c_hi, CONV_SUB_COLS):
                cs = slice(c0, c0 + CONV_SUB_COLS)
                u_ref, ucs = (u0_ref, cs) if c0 < SSM_D_INNER else (u1_ref, slice(c0 - SSM_D_INNER, c0 - SSM_D_INNER + CONV_SUB_COLS))
                for r0 in range(0, SSM_CHUNK, sub):
                    dext = ext_ref[r0:r0 + sub + HALO, cs]
                    uv = u_ref[r0:r0 + sub, ucs]
                    for k in range(SSM_CONV):
                        j = SSM_CONV - 1 - k
                        ahead = dext[:sub] if j == 0 else pltpu.roll(dext, sub + HALO - j, 0)[:sub]
                        term = cw_ref[k:k + 1, cs] * ahead
                        du = term if k == 0 else du + term
                        dcw_acc[k, :, cs] += fold(ahead * uv)
                    dcb_acc[:, cs] += fold(dext[:sub])
                    dproj_ref[r0:r0 + sub, SSM_D_INNER + c0:SSM_D_INNER + c0 + CONV_SUB_COLS] = du.astype(dproj_ref.dtype)

        conv_share = SSM_CONV_DIM // GPS
        for k in range(GPS):
            conv_columns(k * conv_share, (k + 1) * conv_share)
            one_group(live, stage, k, k, x_ref, b_ref, c_ref, dtb_ref, acsb_ref, dtr_ref, acsr_ref, alc_ref, d_ref,
                      hin_ref, dy_ref, dxbc_s, ddt_s, dal_ref, dd_ref, dh_ref)

        @pl.when(step == nc)
        def _():
            dgw_ref[...] = jnp.sum(acc_ref[...], axis=0, keepdims=True)
            dcw_ref[...] = jnp.sum(dcw_acc[...], axis=1)
            dcb_ref[...] = jnp.sum(dcb_acc[...], axis=0, keepdims=True)

    def one_group(live, stage, g, k, x_ref, b_ref, c_ref, dtb_ref, acsb_ref, dtr_ref, acsr_ref, alc_ref, d_ref, hin_ref,
                  dy_ref, dxbc_s, ddt_s, dal_ref, dd_ref, dh_ref):
        cols = slice(k * GP, (k + 1) * GP)
        ncols = slice(k * SSM_STATE, (k + 1) * SSM_STATE)
        bcols = slice(SSM_D_INNER + k * SSM_STATE, SSM_D_INNER + (k + 1) * SSM_STATE)
        ccols = slice(SSM_D_INNER + SSM_BC_DIM + k * SSM_STATE, SSM_D_INNER + SSM_BC_DIM + (k + 1) * SSM_STATE)

        xv = x_ref[:, cols].astype(F32)
        dyv = dy_ref[:, cols]
        bb = b_ref[:, ncols].astype(BF16)
        cb16 = c_ref[:, ncols].astype(BF16)
        dtb = dtb_ref[:, cols]
        acsb = acsb_ref[:, cols]
        dtr_v = dtr_ref[k]
        acs_r = acsr_ref[k]
        a_col = -jnp.exp(alc_ref[k])
        ri, cj = _chunk_iotas()
        head_of_lane = lax.shift_right_logical(lax.broadcasted_iota(jnp.int32, (SSM_HPG, GP), 1), HEAD_DIM_LOG2)
        ind_t = (head_of_lane == lax.broadcasted_iota(jnp.int32, (SSM_HPG, GP), 0)).astype(BF16)
        lastb = acsb[SSM_CHUNK - 1:SSM_CHUNK, :]
        ecb = jnp.exp(acsb)
        dteb = jnp.exp(lastb - acsb)
        xd = xv * dtb
        xw = xd * dteb
        cb = lax.dot_general(cb16, bb, NT_DIMS, preferred_element_type=F32)
        hin_v = hin_ref[0, k]
        dhn = dh_ref[g]
        h16 = hin_v.astype(BF16)
        dh16 = dhn.astype(BF16)
        ch = jnp.dot(cb16, h16, preferred_element_type=F32)
        bdh = jnp.dot(bb, dh16, preferred_element_type=F32)
        dym = _head_masked_rows(dyv, BF16)
        g_all = lax.dot_general(dym, xd.astype(BF16), NT_DIMS, preferred_element_type=F32)
        gl_sum = jnp.zeros((SSM_CHUNK, SSM_CHUNK), F32)
        ms, qs = [], []
        for r in range(SSM_HPG):
            decay = _head_decay(acsb, acs_r, r, ri, cj)
            gl = g_all[r * SSM_CHUNK:(r + 1) * SSM_CHUNK] * decay
            gl_sum = gl_sum + gl
            ms.append((cb * decay).astype(BF16))
            qs.append((gl * cb).astype(BF16))
        dxd = lax.dot_general(jnp.concatenate(ms, axis=0), dym, TN_DIMS, preferred_element_type=F32) + dteb * bdh
        cum = jnp.dot(jnp.concatenate(qs, axis=0), (ri < cj).astype(BF16), preferred_element_type=F32)
        sub4 = lax.broadcasted_iota(jnp.int32, (SSM_HPG, 1), 0)
        da = jnp.zeros((SSM_HPG, SSM_CHUNK), F32)
        for r in range(SSM_HPG):
            rect = jnp.sum(jnp.where(ri >= cj, cum[r * SSM_CHUNK:(r + 1) * SSM_CHUNK], 0.0), axis=0, keepdims=True)
            da = da + jnp.where(sub4 == r, rect, 0.0)
        z2 = xw * bdh
        sub8 = lax.broadcasted_iota(jnp.int32, (8, 1), 0)
        col_sums = (jnp.where(sub8 == 0, jnp.sum(z2, axis=0, keepdims=True), 0.0)
                    + jnp.where(sub8 == 1, jnp.sum(dhn * hin_v, axis=0, keepdims=True), 0.0)
                    + jnp.where(sub8 == 2, jnp.sum(dyv * xv, axis=0, keepdims=True), 0.0))
        wv = ecb * dyv
        summands = jnp.concatenate([wv * ch - z2, dxd * xv, col_sums], axis=0)
        sums = lax.dot_general(ind_t, summands.astype(BF16), NT_DIMS, preferred_element_type=F32)
        per_pos = sums[:, :2 * SSM_CHUNK]
        totals = sums[:, 2 * SSM_CHUNK:]
        e_last = totals[:, 0:1] + jnp.exp(acs_r[:, SSM_CHUNK - 1:SSM_CHUNK]) * totals[:, 1:2]
        da = (da + e_last + jnp.dot(per_pos[:, :SSM_CHUNK], (ri >= cj).astype(F32), preferred_element_type=F32,
                                    precision=lax.Precision.HIGHEST))
        ddt_s[stage, k] = a_col * da + per_pos[:, SSM_CHUNK:]
        dal_ref[g] += live * (a_col * jnp.sum(da * dtr_v, axis=1, keepdims=True))
        dd_ref[g] += live * totals[:, 2:3]
        dxbc_s[stage, :, cols] = dxd * dtb + d_ref[k] * dyv
        w16 = wv.astype(BF16)
        xw16 = xw.astype(BF16)
        gl16 = gl_sum.astype(BF16)
        dxbc_s[stage, :, ccols] = (jnp.dot(gl16, bb, preferred_element_type=F32)
                                   + lax.dot_general(w16, h16, NT_DIMS, preferred_element_type=F32))
        dxbc_s[stage, :, bcols] = (lax.dot_general(gl16, cb16, TN_DIMS, preferred_element_type=F32)
                                   + lax.dot_general(xw16, dh16, NT_DIMS, preferred_element_type=F32))
        dh_ref[g] = dhn * jnp.exp(lastb) + lax.dot_general(cb16, w16, TN_DIMS, preferred_element_type=F32)

    def scan_at(s):
        return nc - 1 - jnp.minimum(s, nc - 1)

    def conv_at(s):
        return nc - 1 - jnp.maximum(s - 1, 0)

    small = pl.BlockSpec((SSM_GROUPS, SSM_HPG, 1), lambda s: (0, 0, 0))
    lanes = pl.BlockSpec((SSM_CHUNK, SSM_D_INNER), lambda s: (scan_at(s), 0))
    rows = pl.BlockSpec((SSM_GROUPS, SSM_HPG, SSM_CHUNK), lambda s: (0, 0, scan_at(s)))
    vec = pl.BlockSpec((1, SSM_D_INNER), lambda s: (0, 0))
    return pl.pallas_call(
        body, grid=(nc + 1,),
        in_specs=[lanes,
                  pl.BlockSpec((SSM_CHUNK, SSM_BC_DIM), lambda s: (scan_at(s), B_BLOCK0 // GPS)),
                  pl.BlockSpec((SSM_CHUNK, SSM_BC_DIM), lambda s: (scan_at(s), C_BLOCK0 // GPS)),
                  lanes, lanes, rows, rows,
                  pl.BlockSpec((SSM_GROUPS, SSM_HPG, 1), lambda s: (0, 0, 0)),
                  pl.BlockSpec((SSM_GROUPS, 1, GP), lambda s: (0, 0, 0)),
                  pl.BlockSpec((1, SSM_GROUPS, SSM_STATE, GP), lambda s: (scan_at(s), 0, 0, 0)),
                  lanes, lanes, lanes, vec,
                  pl.BlockSpec((SSM_CHUNK, SSM_CONV_DIM), lambda s: (conv_at(s), 0)),
                  pl.BlockSpec((SSM_CHUNK, SSM_D_INNER), lambda s: (conv_at(s), 1)),
                  pl.BlockSpec((SSM_CHUNK, SSM_D_INNER), lambda s: (conv_at(s), 2)),
                  pl.BlockSpec((SSM_CONV, SSM_CONV_DIM), lambda s: (0, 0))],
        out_specs=[pl.BlockSpec((SSM_CHUNK, SSM_D_INNER + SSM_CONV_DIM), lambda s: (conv_at(s), 0)),
                   pl.BlockSpec((SSM_GROUPS, SSM_HPG, SSM_CHUNK), lambda s: (0, 0, conv_at(s))),
                   small, small, vec,
                   pl.BlockSpec((SSM_CONV, SSM_CONV_DIM), lambda s: (0, 0)),
                   pl.BlockSpec((1, SSM_CONV_DIM), lambda s: (0, 0))],
        out_shape=[jax.ShapeDtypeStruct((l, SSM_IN_PAD), BF16), jax.ShapeDtypeStruct((SSM_GROUPS, SSM_HPG, l), F32),
                   jax.ShapeDtypeStruct((SSM_GROUPS, SSM_HPG, 1), F32),
                   jax.ShapeDtypeStruct((SSM_GROUPS, SSM_HPG, 1), F32),
                   jax.ShapeDtypeStruct((1, SSM_D_INNER), F32),
                   jax.ShapeDtypeStruct((SSM_CONV, SSM_CONV_DIM), F32), jax.ShapeDtypeStruct((1, SSM_CONV_DIM), F32)],
        scratch_shapes=[pltpu.VMEM((SSM_GROUPS, SSM_STATE, GP), F32), pltpu.VMEM((SSM_CHUNK, SSM_D_INNER), F32),
                        pltpu.VMEM((8, SSM_D_INNER), F32),
                        pltpu.VMEM((2, SSM_CHUNK, SSM_CONV_DIM), F32), pltpu.VMEM((2, SSM_CHUNK, SSM_D_INNER), BF16),
                        pltpu.VMEM((2, SSM_GROUPS, SSM_HPG, SSM_CHUNK), F32), pltpu.VMEM((HALO, SSM_CONV_DIM), F32),
                        pltpu.VMEM((SSM_CHUNK + HALO, SSM_CONV_DIM), F32),
                        pltpu.VMEM((SSM_CONV, 8, SSM_CONV_DIM), F32), pltpu.VMEM((8, SSM_CONV_DIM), F32)],
        compiler_params=_params("arbitrary"), name=name,
    )(xbc, xbc, xbc, dtb, acsb, dtr, acs_r, a_log.reshape(SSM_GROUPS, SSM_HPG, 1), d_lanes, hin, dact, y, proj,
      gate_w.reshape(1, SSM_D_INNER), pre, proj, proj, cw)


LANES = 128
ROPE_Q_CHUNKS = ATT_WIDTH // LANES
ROPE_K_CHUNKS = ATT_KV_WIDTH // LANES


def _rope_tables(positions):
    inv = ROPE_THETA ** (-jnp.arange(0, ROPE_DIM, 2, dtype=F32) / ROPE_DIM)
    ang = positions.astype(F32)[:, None] * inv
    cos, sin = jnp.cos(ang), jnp.sin(ang)
    l = positions.shape[0]
    rest = ATT_HEAD_DIM - ROPE_DIM
    ones, zeros = jnp.ones((l, rest), F32), jnp.zeros((l, rest), F32)
    z8 = jnp.zeros((l, ROPE_HALF), F32)
    cos_f = jnp.concatenate([cos, cos, ones], axis=1)
    sin_a = jnp.concatenate([-sin, z8, zeros], axis=1)
    sin_b = jnp.concatenate([z8, sin, zeros], axis=1)
    reps = LANES // ATT_HEAD_DIM
    return tuple(jnp.tile(t, (1, reps)) for t in (cos_f, sin_a, sin_b))


ATT_QKV4 = 3 * ATT_WIDTH


def _both_halves(chunk):
    lane = lax.broadcasted_iota(jnp.int32, (1, LANES), 1)
    swapped = pltpu.roll(chunk, ATT_HEAD_DIM, 1)
    return jnp.where(lane < ATT_HEAD_DIM, chunk, swapped), jnp.where(lane < ATT_HEAD_DIM, swapped, chunk)


def _rope_bwd(dq, dk4, dv4, dgate, tables, name):
    l = dq.shape[0]
    tl = _pick(l, (256, 128))

    def body(dq_ref, dk_ref, dv_ref, dg_ref, c_ref, sa_ref, sb_ref, o_ref):
        cos_f, sin_a, sin_b = c_ref[...], sa_ref[...], sb_ref[...]
        lane = lax.broadcasted_iota(jnp.int32, (1, LANES), 1)

        def unrope(t):
            return t * cos_f + pltpu.roll(t * sin_a, ROPE_HALF, 1) + pltpu.roll(t * sin_b, LANES - ROPE_HALF, 1)

        def head_total(ref, kvh):
            base = kvh * ATT_GQA * ATT_HEAD_DIM
            s = ref[:, base:base + LANES] + ref[:, base + LANES:base + 2 * LANES]
            return s + pltpu.roll(s, ATT_HEAD_DIM, 1)

        for k in range(ROPE_Q_CHUNKS):
            sl = slice(k * LANES, (k + 1) * LANES)
            o_ref[:, sl] = unrope(dq_ref[:, sl] * Q_SCALE).astype(o_ref.dtype)
        for k in range(ROPE_K_CHUNKS):
            dk = jnp.where(lane < ATT_HEAD_DIM, head_total(dk_ref, 2 * k), head_total(dk_ref, 2 * k + 1))
            dv = jnp.where(lane < ATT_HEAD_DIM, head_total(dv_ref, 2 * k), head_total(dv_ref, 2 * k + 1))
            o_ref[:, ATT_WIDTH + k * LANES:ATT_WIDTH + (k + 1) * LANES] = unrope(dk).astype(o_ref.dtype)
            at = ATT_WIDTH + ATT_KV_WIDTH + k * LANES
            o_ref[:, at:at + LANES] = dv.astype(o_ref.dtype)
        o_ref[:, ATT_QKV:ATT_IN_DIM] = dg_ref[...].astype(o_ref.dtype)

    tab = pl.BlockSpec((tl, LANES), lambda i: (i, 0))
    wide = pl.BlockSpec((tl, ATT_WIDTH), lambda i: (i, 0))
    return pl.pallas_call(
        body, grid=(l // tl,), in_specs=[wide, wide, wide, wide, tab, tab, tab],
        out_specs=pl.BlockSpec((tl, ATT_IN_DIM), lambda i: (i, 0)),
        out_shape=jax.ShapeDtypeStruct((l, ATT_IN_DIM), BF16), compiler_params=_params("parallel"), name=name,
    )(dq, dk4, dv4, dgate, *tables)


GATE_HALF = ATT_WIDTH // 2
GATE_COL_BLOCK = ATT_QKV // GATE_HALF


ATT_STACK = ATT_GQA * ATT_BLOCK
BLOCK_LOG2 = ATT_BLOCK.bit_length() - 1


def _stack_masks(n):
    ri = lax.broadcasted_iota(jnp.int32, (ATT_STACK, ATT_BLOCK), 0) & (ATT_BLOCK - 1)
    cj = lax.broadcasted_iota(jnp.int32, (ATT_STACK, ATT_BLOCK), 1)
    return (cj > ri) & (n > 0), cj <= ri


def _stack_sinks(sink_ref, kvh):
    blk = lax.shift_right_logical(lax.broadcasted_iota(jnp.int32, (ATT_STACK, 1), 0), BLOCK_LOG2)
    col = jnp.zeros((ATT_STACK, 1), F32)
    for r in range(ATT_GQA):
        col = jnp.where(blk == r, sink_ref[kvh * ATT_GQA + r], col)
    return col


def _stack_fold(stack):
    head_of_lane = lax.shift_right_logical(lax.broadcasted_iota(jnp.int32, (1, GP), 1), HEAD_DIM_LOG2)
    out = jnp.zeros((ATT_BLOCK, GP), F32)
    for r in range(ATT_GQA):
        out = jnp.where(head_of_lane == r, stack[r * ATT_BLOCK:(r + 1) * ATT_BLOCK], out)
    return out


def _attn_fwd(proj, tables, sinks, name):
    l = proj.shape[0]
    nb = l // ATT_BLOCK
    ring = 3

    def body(sink_ref, p_ref, c_ref, sa_ref, sb_ref, g0_ref, g1_ref, og_ref, o_ref, lse_ref, qkv_ref, ring_ref):
        s = pl.program_id(0)

        @pl.when(s == 0)
        def _():
            ring_ref[...] = jnp.zeros_like(ring_ref)

        n = s - 1
        cur = lax.rem(s + ring - 1, ring)
        prv = lax.rem(s + ring - 2, ring)
        q_ref = ring_ref.at[cur, :, 0:ATT_WIDTH]
        kc_ref = ring_ref.at[cur, :, ATT_WIDTH:2 * ATT_WIDTH]
        vc_ref = ring_ref.at[cur, :, 2 * ATT_WIDTH:3 * ATT_WIDTH]
        kp_ref = ring_ref.at[prv, :, ATT_WIDTH:2 * ATT_WIDTH]
        vp_ref = ring_ref.at[prv, :, 2 * ATT_WIDTH:3 * ATT_WIDTH]
        mask_p, mask_c = _stack_masks(n)
        ones = jnp.ones((ATT_BLOCK, LANES), BF16)
        for kvh in range(ATT_KV_HEADS):
            cols = slice(kvh * GP, (kvh + 1) * GP)
            q_stack = _head_masked_rows(q_ref[:, cols], BF16)
            sp = jnp.where(mask_p, lax.dot_general(q_stack, kp_ref[:, cols], NT_DIMS, preferred_element_type=F32), NEG_INF)
            sc = jnp.where(mask_c, lax.dot_general(q_stack, kc_ref[:, cols], NT_DIMS, preferred_element_type=F32), NEG_INF)
            sink = _stack_sinks(sink_ref, kvh)
            m = jnp.maximum(jnp.max(jnp.maximum(sp, sc), axis=1, keepdims=True), sink)
            pp = jnp.exp(sp - m).astype(BF16)
            pc = jnp.exp(sc - m).astype(BF16)
            acc = (jnp.dot(pp, jnp.concatenate([vp_ref[:, cols], ones], axis=1), preferred_element_type=F32)
                   + jnp.dot(pc, jnp.concatenate([vc_ref[:, cols], ones], axis=1), preferred_element_type=F32))
            den = acc[:, GP:] + jnp.exp(sink - m)
            inv = 1.0 / den
            o_ref[:, cols] = _stack_fold(acc[:, :GP] * jnp.concatenate([inv, inv], axis=1))
            lse = m + jnp.log(den)
            lse_ref[:, cols] = _stack_fold(jnp.concatenate([lse, lse], axis=1))
        for half, g_ref in enumerate((g0_ref, g1_ref)):
            sl = slice(half * GATE_HALF, (half + 1) * GATE_HALF)
            gate = g_ref[...]
            og_ref[:, sl] = (o_ref[:, sl] * (gate * _sigmoid(gate))).astype(og_ref.dtype)

        slot = lax.rem(s, ring)
        cos_f, sin_a, sin_b = c_ref[...], sa_ref[...], sb_ref[...]

        def rope(t):
            return t * cos_f + pltpu.roll(t, LANES - ROPE_HALF, 1) * sin_a + pltpu.roll(t, ROPE_HALF, 1) * sin_b

        def put(dst, value):
            qkv_ref[:, dst:dst + LANES] = value
            ring_ref[slot, :, dst:dst + LANES] = value

        for k in range(ROPE_Q_CHUNKS):
            put(k * LANES, (rope(p_ref[:, k * LANES:(k + 1) * LANES]) * Q_SCALE).astype(BF16))
        for part in range(2):
            for k in range(ROPE_K_CHUNKS):
                src = ATT_WIDTH + part * ATT_KV_WIDTH + k * LANES
                t = p_ref[:, src:src + LANES]
                if part == 0:
                    t = rope(t)
                for head, dup in enumerate(_both_halves(t.astype(BF16))):
                    dst = (1 + part) * ATT_WIDTH + (2 * k + head) * ATT_GQA * ATT_HEAD_DIM
                    put(dst, dup)
                    put(dst + LANES, dup)

    def rope_at(s):
        return jnp.minimum(s, nb - 1)

    def attend_at(s):
        return jnp.maximum(s - 1, 0)

    wide = pl.BlockSpec((ATT_BLOCK, ATT_WIDTH), lambda s: (attend_at(s), 0))
    tab = pl.BlockSpec((ATT_BLOCK, LANES), lambda s: (rope_at(s), 0))
    return pl.pallas_call(
        body, grid=(nb + 1,),
        in_specs=[pl.BlockSpec(memory_space=pltpu.SMEM),
                  pl.BlockSpec((ATT_BLOCK, ATT_IN_DIM), lambda s: (rope_at(s), 0)), tab, tab, tab,
                  pl.BlockSpec((ATT_BLOCK, GATE_HALF), lambda s: (attend_at(s), GATE_COL_BLOCK)),
                  pl.BlockSpec((ATT_BLOCK, GATE_HALF), lambda s: (attend_at(s), GATE_COL_BLOCK + 1))],
        out_specs=[wide, wide, wide, pl.BlockSpec((ATT_BLOCK, ATT_QKV4), lambda s: (rope_at(s), 0))],
        out_shape=[jax.ShapeDtypeStruct((l, ATT_WIDTH), BF16), jax.ShapeDtypeStruct((l, ATT_WIDTH), F32),
                   jax.ShapeDtypeStruct((l, ATT_WIDTH), F32), jax.ShapeDtypeStruct((l, ATT_QKV4), BF16)],
        scratch_shapes=[pltpu.VMEM((ring, ATT_BLOCK, ATT_QKV4), BF16)],
        compiler_params=_params("arbitrary"), name=name,
    )(sinks, proj, *tables, proj, proj)


def _attn_bwd(qkv, proj, sinks, o, lse, dog, name, ride=()):
    l = qkv.shape[0]
    nb = l // ATT_BLOCK
    n_ride = len(ride)

    def body(*refs):
        sink_ref, q_ref, kp_ref, kc_ref, vp_ref, vc_ref, g0_ref, g1_ref, o_ref, lse_ref, dog_ref = refs[:11]
        ride_in = refs[11:11 + n_ride]
        dq_ref, dk_ref, dv_ref, dg_ref, ds_ref = refs[11 + n_ride:16 + n_ride]
        ride_out = refs[16 + n_ride:16 + 2 * n_ride]
        ck_ref, cv_ref, do_ref = refs[16 + 2 * n_ride:19 + 2 * n_ride]
        ride_sems = refs[19 + 2 * n_ride:]
        n = pl.program_id(0)

        @pl.when(n == 0)
        def _():
            ds_ref[...] = jnp.zeros_like(ds_ref)
            ck_ref[...] = jnp.zeros_like(ck_ref)
            cv_ref[...] = jnp.zeros_like(cv_ref)
            if n_ride:
                _scatter_between_chips(ride_in, ride_out, *ride_sems, wait=False)

        @pl.when(n == nb)
        def _():
            dk_ref[...] = ck_ref[...]
            dv_ref[...] = cv_ref[...]
            if n_ride:
                _scatter_between_chips(ride_in, ride_out, *ride_sems, wait=True)

        @pl.when(n < nb)
        def _():
            mask_p, mask_c = _stack_masks(n)
            lane = lax.broadcasted_iota(jnp.int32, (1, ATT_Q_HEADS), 1)
            for half, g_ref in enumerate((g0_ref, g1_ref)):
                sl = slice(half * GATE_HALF, (half + 1) * GATE_HALF)
                gate = g_ref[...]
                s = _sigmoid(gate)
                dogv = dog_ref[:, sl]
                do_ref[:, sl] = dogv * (gate * s)
                dg_ref[:, sl] = dogv * o_ref[:, sl] * (s * (1.0 + gate * (1.0 - s)))
            ds_acc = jnp.zeros((1, ATT_Q_HEADS), F32)
            for kvh in range(ATT_KV_HEADS):
                cols = slice(kvh * GP, (kvh + 1) * GP)
                kp, kc, vp, vc = kp_ref[:, cols], kc_ref[:, cols], vp_ref[:, cols], vc_ref[:, cols]
                q_stack = _head_masked_rows(q_ref[:, cols], BF16)
                do_g = do_ref[:, cols]
                do_stack = _head_masked_rows(do_g, BF16)
                lse_g = lse_ref[:, cols]
                lse_stack = jnp.concatenate(
                    [_both_halves(lse_g[:, (r // 2) * LANES:(r // 2 + 1) * LANES])[r % 2] for r in range(ATT_GQA)], axis=0)
                pp = jnp.exp(jnp.where(
                    mask_p, lax.dot_general(q_stack, kp, NT_DIMS, preferred_element_type=F32) - lse_stack, NEG_INF))
                pc = jnp.exp(jnp.where(
                    mask_c, lax.dot_general(q_stack, kc, NT_DIMS, preferred_element_type=F32) - lse_stack, NEG_INF))
                dpp = lax.dot_general(do_stack, vp, NT_DIMS, preferred_element_type=F32)
                dpc = lax.dot_general(do_stack, vc, NT_DIMS, preferred_element_type=F32)
                delta = jnp.sum(pp * dpp + pc * dpc, axis=1, keepdims=True)
                dsp = (pp * (dpp - delta)).astype(BF16)
                dsc = (pc * (dpc - delta)).astype(BF16)
                dq_ref[:, cols] = _stack_fold(jnp.dot(dsp, kp, preferred_element_type=F32)
                                              + jnp.dot(dsc, kc, preferred_element_type=F32))
                dk_ref[:, cols] = ck_ref[:, cols] + lax.dot_general(dsp, q_stack, TN_DIMS, preferred_element_type=F32)
                dv_ref[:, cols] = cv_ref[:, cols] + lax.dot_general(pp.astype(BF16), do_stack, TN_DIMS,
                                                                    preferred_element_type=F32)
                ck_ref[:, cols] = lax.dot_general(dsc, q_stack, TN_DIMS, preferred_element_type=F32)
                cv_ref[:, cols] = lax.dot_general(pc.astype(BF16), do_stack, TN_DIMS, preferred_element_type=F32)
                t = jnp.exp(_stack_sinks(sink_ref, kvh) - lse_stack) * delta
                for r in range(ATT_GQA):
                    tot = jnp.sum(t[r * ATT_BLOCK:(r + 1) * ATT_BLOCK], axis=0, keepdims=True)
                    ds_acc = ds_acc - jnp.where(lane == kvh * ATT_GQA + r, tot[:, :ATT_Q_HEADS], 0.0)
            ds_ref[...] += ds_acc

    def cur(n):
        return jnp.minimum(n, nb - 1)

    def prev(n):
        return jnp.maximum(n - 1, 0)

    wide = pl.BlockSpec((ATT_BLOCK, ATT_WIDTH), lambda n: (cur(n), 0))
    late = pl.BlockSpec((ATT_BLOCK, ATT_WIDTH), lambda n: (prev(n), 0))
    return pl.pallas_call(
        body, grid=(nb + 1,),
        in_specs=[pl.BlockSpec(memory_space=pltpu.SMEM), wide,
                  pl.BlockSpec((ATT_BLOCK, ATT_WIDTH), lambda n: (prev(cur(n)), 1)),
                  pl.BlockSpec((ATT_BLOCK, ATT_WIDTH), lambda n: (cur(n), 1)),
                  pl.BlockSpec((ATT_BLOCK, ATT_WIDTH), lambda n: (prev(cur(n)), 2)),
                  pl.BlockSpec((ATT_BLOCK, ATT_WIDTH), lambda n: (cur(n), 2)),
                  pl.BlockSpec((ATT_BLOCK, GATE_HALF), lambda n: (cur(n), GATE_COL_BLOCK)),
                  pl.BlockSpec((ATT_BLOCK, GATE_HALF), lambda n: (cur(n), GATE_COL_BLOCK + 1)),
                  wide, wide, wide] + [ANY] * n_ride,
        out_specs=[wide, late, late, wide, pl.BlockSpec((1, ATT_Q_HEADS), lambda n: (0, 0))] + [ANY] * n_ride,
        out_shape=[jax.ShapeDtypeStruct((l, ATT_WIDTH), F32), jax.ShapeDtypeStruct((l, ATT_WIDTH), F32),
                   jax.ShapeDtypeStruct((l, ATT_WIDTH), F32), jax.ShapeDtypeStruct((l, ATT_WIDTH), F32),
                   jax.ShapeDtypeStruct((1, ATT_Q_HEADS), F32)] + _scatter_shapes(ride),
        scratch_shapes=[pltpu.VMEM((ATT_BLOCK, ATT_WIDTH), F32), pltpu.VMEM((ATT_BLOCK, ATT_WIDTH), F32),
                        pltpu.VMEM((ATT_BLOCK, ATT_WIDTH), F32)] + (_gather_sems(n_ride) if n_ride else []),
        compiler_params=_params("arbitrary"), name=name,
    )(sinks, qkv, qkv, qkv, qkv, qkv, proj, proj, o, lse, dog, *ride)


def _local_step(x, positions, pre_norm, post_norm, conv_b, dt_bias, a_log, d_skip, gate_norm, sinks, target,
                prenorm_with_first_in, in_proj_with_first_pair, scan_with_second_pair, attn_bwd_with_second_pair_grads,
                in_dx_with_first_pair_grads):
    tables = _rope_tables(positions)
    dt_bias_pad = jnp.pad(dt_bias, ((0, 0), (0, SSM_DT_PAD - SSM_HEADS)))
    d_lanes = jnp.repeat(d_skip, SSM_HEAD_DIM, axis=1).reshape(-1, SSM_GROUPS, 1, GP)
    a_log_pad = jnp.pad(a_log, ((0, 0), (0, SSM_DT_PAD - SSM_HEADS)))
    saved = []
    cur = x
    h, first_in = prenorm_with_first_in(functools.partial(_rmsnorm_fwd, cur, pre_norm[0], "prenorm_fwd_0"))
    pairs = [first_in, None]
    for i in range(DEPTH):
        j = i // 2
        if i % 2 == 0:
            in_proj = functools.partial(_matmul, h, pairs[j]["ssm_w_in"], "nn", F32, f"ssm_in_{i}")
            if i == 0:
                proj, rest = in_proj_with_first_pair(in_proj)
                pairs[0] = {**first_in, **rest}
            else:
                proj = in_proj()
            scan = functools.partial(_ssd_fwd, proj, pairs[j]["ssm_conv_w"], conv_b[j], dt_bias_pad[j:j + 1],
                                     a_log_pad[j:j + 1], d_lanes[j], gate_norm[j], f"ssd_fwd_{i}")
            if i == 0:
                *scanned, pairs[1] = scan_with_second_pair(scan)
            else:
                scanned = scan()
            y, act, hin, pre, xbc, dtb, acsb, dtr, acs_r = scanned
            w_ssm_in = [p["ssm_w_in"] for p in pairs]
            w_ssm_out = [p["ssm_w_out"] for p in pairs]
            w_att_in = [p["att_w_in"] for p in pairs]
            w_att_out = [p["att_w_out"] for p in pairs]
            conv_w = [p["ssm_conv_w"] for p in pairs]
            ymix = _matmul(act, w_ssm_out[j], "nn", F32, f"ssm_out_{i}")
            saved.append(dict(x=cur, h=h, proj=proj, pre=pre, xbc=xbc, dtb=dtb, acsb=acsb, dtr=dtr, acs_r=acs_r, y=y,
                              hin=hin, act=act, ymix=ymix))
        else:
            proj = _matmul(h, w_att_in[j], "nn", F32, f"att_in_{i}")
            act, o, lse, qkv = _attn_fwd(proj, tables, sinks[j], f"attn_fwd_{i}")
            ymix = _matmul(act, w_att_out[j], "nn", F32, f"att_out_{i}")
            saved.append(dict(x=cur, h=h, proj=proj, qkv=qkv, o=o, lse=lse, act=act, ymix=ymix))
        if i + 1 < DEPTH:
            cur, h = _post_fwd(cur, ymix, post_norm[i], pre_norm[i + 1], f"post_fwd_{i}")

    gr = {k: [None] * 2 for k in ("ssm_w_in", "ssm_conv_w", "ssm_conv_b", "ssm_dt_bias", "ssm_a_log", "ssm_d",
                                  "ssm_gate_norm", "ssm_w_out", "att_w_in", "att_sinks", "att_w_out")}
    gr["pre_norm"] = [None] * DEPTH
    gr["post_norm"] = [None] * DEPTH
    last = DEPTH - 1
    g, dymix, loss_lanes, gr["post_norm"][last] = _post_loss(cur, ymix, post_norm[last], target, "post_loss")
    for i in reversed(range(DEPTH)):
        j = i // 2
        s = saved[i]
        if i % 2 == 0:
            dact = _matmul(dymix, w_ssm_out[j], "nt", F32, f"ssm_out_dx_{i}")
            gr["ssm_w_out"][j] = _matmul(s["act"], dymix, "tn", F32, f"ssm_out_dw_{i}")
            dproj, ddt8, dal, dd, gr["ssm_gate_norm"][j], gr["ssm_conv_w"][j], dcb = _ssd_bwd(
                s["xbc"], s["pre"], conv_w[j], s["dtb"], s["acsb"], s["dtr"], s["acs_r"], a_log[j], d_lanes[j], s["hin"],
                dact, s["y"], s["proj"], gate_norm[j], f"ssd_bwd_{i}")
            gr["ssm_conv_b"][j] = dcb[0]
            gr["ssm_a_log"][j] = dal.reshape(SSM_HEADS)
            gr["ssm_d"][j] = dd.reshape(SSM_HEADS)
            l = x.shape[0]
            ddt = jnp.pad(jnp.transpose(ddt8, (2, 0, 1)).reshape(l, SSM_HEADS), ((0, 0), (0, SSM_DT_PAD - SSM_HEADS)))
            dproj, dbias = _dt_bwd(ddt, s["proj"], dt_bias_pad[j:j + 1], dproj, f"dt_bwd_{i}")
            gr["ssm_dt_bias"][j] = dbias[0, :SSM_HEADS]
            w_in, key = w_ssm_in[j], "ssm_w_in"
        else:
            dog = _matmul(dymix, w_att_out[j], "nt", F32, f"att_out_dx_{i}")
            gr["att_w_out"][j] = _matmul(s["act"], dymix, "tn", F32, f"att_out_dw_{i}")
            attn_bwd = functools.partial(_attn_bwd, s["qkv"], s["proj"], sinks[j], s["o"], s["lse"], dog, f"attn_bwd_{i}")
            if i == 1:
                (dq, dk, dv, dgate, dsk), second_pair_reduced = attn_bwd_with_second_pair_grads(
                    attn_bwd, {k: gr[k][1] for k in BIG})
            else:
                dq, dk, dv, dgate, dsk = attn_bwd()
            gr["att_sinks"][j] = dsk[0]
            dproj = _rope_bwd(dq, dk, dv, dgate, tables, f"rope_bwd_{i}")
            w_in, key = w_att_in[j], "att_w_in"
        gr[key][j] = _matmul(s["h"], dproj, "tn", F32, f"in_dw_{i}")
        in_dx = functools.partial(_matmul, dproj, w_in, "nt", F32, f"in_dx_{i}")
        if i == 0:
            dh, first_pair_reduced = in_dx_with_first_pair_grads(in_dx, {k: gr[k][0] for k in BIG})
        else:
            dh = in_dx()
        if i > 0:
            g, dymix, gr["pre_norm"][i], gr["post_norm"][i - 1] = _norm_bwd_chain(
                dh, s["x"], pre_norm[i], g, saved[i - 1]["ymix"], post_norm[i - 1], f"norm_bwd_{i}")
        else:
            g, gr["pre_norm"][i] = _rmsnorm_bwd(dh, s["x"], pre_norm[i], g, f"prenorm_bwd_{i}")
    grads = {k: jnp.stack([v.reshape(v.shape[-1]) if k in ("pre_norm", "post_norm", "ssm_gate_norm") else v for v in vs])
             for k, vs in gr.items() if k not in BIG}
    return loss_lanes, g, grads, first_pair_reduced, second_pair_reduced


N_CHIPS = 4
N_DEV = 8
MESH = pl.DeviceIdType.MESH
ANY = pl.BlockSpec(memory_space=pl.ANY)


def _place():
    x, y, c = lax.axis_index("x"), lax.axis_index("y"), lax.axis_index("c")
    return x, y, c, 2 * x + y


def _gather_sems(n):
    return [pltpu.SemaphoreType.DMA((n, N_CHIPS)), pltpu.SemaphoreType.DMA((n, N_CHIPS)), pltpu.SemaphoreType.DMA((n,))]


def _gather_between_chips(ins, outs, send_sems, recv_sems, local_sems, wait):
    n = len(ins)
    _, _, c, s = _place()
    local = [pltpu.make_async_copy(ins[w], outs[w].at[s], local_sems.at[w]) for w in range(n)]

    def remote(w, t):
        return pltpu.make_async_remote_copy(
            src_ref=ins[w].at[c], dst_ref=outs[w].at[s, c], send_sem=send_sems.at[w, t],
            recv_sem=recv_sems.at[w, s], device_id=(t // 2, t % 2, c), device_id_type=MESH)

    def arrival(w, t):
        return pltpu.make_async_remote_copy(
            src_ref=ins[w].at[c], dst_ref=outs[w].at[t, c], send_sem=send_sems.at[w, t],
            recv_sem=recv_sems.at[w, t], device_id=(t // 2, t % 2, c), device_id_type=MESH)

    if not wait:
        for cp in local:
            cp.start()
    for t in range(N_CHIPS):
        @pl.when(s != t)
        def _():
            for w in range(n):
                if wait:
                    remote(w, t).wait_send()
                    arrival(w, t).wait_recv()
                else:
                    remote(w, t).start()
    if wait:
        for cp in local:
            cp.wait()


def _pair_handoff(bufs, name):
    n = len(bufs)

    def body(*refs):
        outs = refs[n:2 * n]
        send_sems, recv_sems = refs[2 * n:]
        x, y, c, s = _place()

        def handed_on(w, t):
            return pltpu.make_async_remote_copy(
                src_ref=outs[w].at[t, c], dst_ref=outs[w].at[t, c], send_sem=send_sems.at[w, t],
                recv_sem=recv_sems.at[w, t], device_id=(x, y, 1 - c), device_id_type=MESH)

        def handed_in(w, t):
            return pltpu.make_async_remote_copy(
                src_ref=outs[w].at[t, 1 - c], dst_ref=outs[w].at[t, 1 - c], send_sem=send_sems.at[w, t],
                recv_sem=recv_sems.at[w, t], device_id=(x, y, 1 - c), device_id_type=MESH)

        for t in range(N_CHIPS):
            @pl.when(s != t)
            def _():
                for w in range(n):
                    handed_on(w, t).start()
        for t in range(N_CHIPS):
            @pl.when(s != t)
            def _():
                for w in range(n):
                    handed_on(w, t).wait_send()
                    handed_in(w, t).wait_recv()

    return pl.pallas_call(
        body, in_specs=[ANY] * n, out_specs=[ANY] * n,
        out_shape=[jax.ShapeDtypeStruct(a.shape, a.dtype) for a in bufs],
        scratch_shapes=[pltpu.SemaphoreType.DMA((n, N_CHIPS)), pltpu.SemaphoreType.DMA((n, N_CHIPS))],
        input_output_aliases={w: w for w in range(n)}, name=name,
    )(*bufs)


def _pair_swap(parts, name):
    n = len(parts)

    def body(*refs):
        ins, outs = refs[:n], refs[n:2 * n]
        send_sems, recv_sems = refs[2 * n:]
        x, y, c, _ = _place()
        cps = [pltpu.make_async_remote_copy(
            src_ref=ins[w].at[1 - c], dst_ref=outs[w], send_sem=send_sems.at[w], recv_sem=recv_sems.at[w],
            device_id=(x, y, 1 - c), device_id_type=MESH) for w in range(n)]
        for cp in cps:
            cp.start()
        for cp in cps:
            cp.wait()

    return pl.pallas_call(
        body, in_specs=[ANY] * n, out_specs=[ANY] * n,
        out_shape=[jax.ShapeDtypeStruct(a.shape[1:], a.dtype) for a in parts],
        scratch_shapes=[pltpu.SemaphoreType.DMA((n,)), pltpu.SemaphoreType.DMA((n,))],
        name=name,
    )(*parts)


def _scatter_between_chips(ins, outs, send_sems, recv_sems, local_sems, wait):
    n = len(ins)
    _, _, c, s = _place()

    def block(w, t):
        rows = ins[w].shape[0] // N_CHIPS
        return ins[w].at[pl.ds(t * rows, rows)]

    local = [pltpu.make_async_copy(block(w, s), outs[w].at[s], local_sems.at[w]) for w in range(n)]

    def remote(w, t):
        return pltpu.make_async_remote_copy(
            src_ref=block(w, t), dst_ref=outs[w].at[s], send_sem=send_sems.at[w, t], recv_sem=recv_sems.at[w, s],
            device_id=(t // 2, t % 2, c), device_id_type=MESH)

    def arrival(w, t):
        return pltpu.make_async_remote_copy(
            src_ref=block(w, t), dst_ref=outs[w].at[t], send_sem=send_sems.at[w, t], recv_sem=recv_sems.at[w, t],
            device_id=(t // 2, t % 2, c), device_id_type=MESH)

    if not wait:
        for cp in local:
            cp.start()
    for t in range(N_CHIPS):
        @pl.when(s != t)
        def _():
            for w in range(n):
                if wait:
                    remote(w, t).wait_send()
                    arrival(w, t).wait_recv()
                else:
                    remote(w, t).start()
    if wait:
        for cp in local:
            cp.wait()


def _scatter_shapes(parts):
    return [jax.ShapeDtypeStruct((N_CHIPS, a.shape[0] // N_CHIPS, a.shape[1]), a.dtype) for a in parts]


def _pair_merge(parts, name):
    n = len(parts)

    def body(*refs):
        ins, outs = refs[:n], refs[n:2 * n]
        send_sems, recv_sems = refs[2 * n:]
        x, y, c, _ = _place()
        cps = [pltpu.make_async_remote_copy(
            src_ref=ins[w], dst_ref=outs[w], send_sem=send_sems.at[w], recv_sem=recv_sems.at[w],
            device_id=(x, y, 1 - c), device_id_type=MESH) for w in range(n)]
        for cp in cps:
            cp.start()
        for cp in cps:
            cp.wait()

    return pl.pallas_call(
        body, in_specs=[ANY] * n, out_specs=[ANY] * n,
        out_shape=[jax.ShapeDtypeStruct(a.shape, a.dtype) for a in parts],
        scratch_shapes=[pltpu.SemaphoreType.DMA((n,)), pltpu.SemaphoreType.DMA((n,))],
        name=name,
    )(*parts)


def _all_gather_small(a, name):
    def body(in_ref, out_ref, send_sems, recv_sems, local_sem):
        x, y, c, _ = _place()
        me = 4 * x + 2 * y + c
        local = pltpu.make_async_copy(in_ref, out_ref.at[me], local_sem)
        local.start()

        def remote(d):
            return pltpu.make_async_remote_copy(
                src_ref=in_ref, dst_ref=out_ref.at[me], send_sem=send_sems.at[d], recv_sem=recv_sems.at[me],
                device_id=(d // 4, (d // 2) % 2, d % 2), device_id_type=MESH)

        def arrival(d):
            return pltpu.make_async_remote_copy(
                src_ref=in_ref, dst_ref=out_ref.at[d], send_sem=send_sems.at[d], recv_sem=recv_sems.at[d],
                device_id=(d // 4, (d // 2) % 2, d % 2), device_id_type=MESH)

        for d in range(N_DEV):
            @pl.when(me != d)
            def _():
                remote(d).start()
        for d in range(N_DEV):
            @pl.when(me != d)
            def _():
                remote(d).wait_send()
                arrival(d).wait_recv()
        local.wait()

    return pl.pallas_call(
        body, in_specs=[ANY], out_specs=ANY, out_shape=jax.ShapeDtypeStruct((N_DEV,) + a.shape, a.dtype),
        scratch_shapes=[pltpu.SemaphoreType.DMA((N_DEV,)), pltpu.SemaphoreType.DMA((N_DEV,)), pltpu.SemaphoreType.DMA],
        name=name,
    )(a)


def _reduce_tile(rows):
    return _pick(rows, (256, 128, 16))


def _pair_add(full, other, layer, name):
    _, rows, cols = full.shape
    tr = _reduce_tile(rows)

    def body(layer_ref, a_ref, b_ref, o_ref):
        o_ref[...] = (a_ref[0] + b_ref[...]).astype(o_ref.dtype)

    return pl.pallas_call(
        body,
        grid_spec=pltpu.PrefetchScalarGridSpec(
            num_scalar_prefetch=1, grid=(rows // tr,),
            in_specs=[pl.BlockSpec((1, tr, cols), lambda i, lr: (lr[0], i, 0)), pl.BlockSpec((tr, cols), lambda i, lr: (i, 0))],
            out_specs=pl.BlockSpec((tr, cols), lambda i, lr: (i, 0))),
        out_shape=jax.ShapeDtypeStruct((rows, cols), BF16), compiler_params=_params("parallel"), name=name,
    )(layer, full, other)


def _sum_slots(a, name):
    n, rows, cols = a.shape
    tr = _reduce_tile(rows)

    def body(a_ref, o_ref):
        acc = a_ref[0].astype(F32)
        for k in range(1, n):
            acc = acc + a_ref[k].astype(F32)
        o_ref[...] = acc

    return pl.pallas_call(
        body, grid=(rows // tr,), in_specs=[pl.BlockSpec((n, tr, cols), lambda i: (0, i, 0))],
        out_specs=pl.BlockSpec((tr, cols), lambda i: (i, 0)),
        out_shape=jax.ShapeDtypeStruct((rows, cols), F32), compiler_params=_params("parallel"), name=name,
    )(a)


def _adamw(w, g, m, v, name):
    rows, cols = w.shape
    tr = _pick(rows, (256, 8))

    def body(w_ref, g_ref, m_ref, v_ref, d_ref, nm_ref, nv_ref):
        gv = g_ref[...]
        mn = ADAM_B1 * m_ref[...] + (1.0 - ADAM_B1) * gv
        vn = ADAM_B2 * v_ref[...] + (1.0 - ADAM_B2) * jnp.square(gv)
        m_hat = mn / (1.0 - ADAM_B1 ** ADAM_STEP)
        v_hat = vn / (1.0 - ADAM_B2 ** ADAM_STEP)
        d_ref[...] = -ADAM_LR * (m_hat / (jnp.sqrt(v_hat) + ADAM_EPS) + ADAM_WD * w_ref[...])
        nm_ref[...] = mn
        nv_ref[...] = vn

    blk = pl.BlockSpec((tr, cols), lambda i: (i, 0))
    return pl.pallas_call(
        body, grid=(rows // tr,), in_specs=[blk] * 4, out_specs=[blk] * 3,
        out_shape=[jax.ShapeDtypeStruct((rows, cols), F32)] * 3, compiler_params=_params("parallel"), name=name,
    )(w, g, m, v)


BIG = ("ssm_w_in", "ssm_w_out", "att_w_in", "att_w_out")
SHARDED = BIG + ("ssm_conv_w",)
SMALL = ("pre_norm", "post_norm", "ssm_conv_b", "ssm_dt_bias", "ssm_a_log", "ssm_d", "ssm_gate_norm", "att_sinks")
WEIGHTS = ("pre_norm", "post_norm", "ssm_w_in", "ssm_conv_w", "ssm_conv_b", "ssm_dt_bias", "ssm_a_log", "ssm_d",
           "ssm_gate_norm", "ssm_w_out", "att_w_in", "att_sinks", "att_w_out")


def _halves(a):
    return a.reshape(2, a.shape[0] // 2, a.shape[1])


def _layer_shards(j, ssm_w_in, ssm_w_out, att_w_in, att_w_out, ssm_conv_w):
    return [_halves(ssm_w_in[j].astype(BF16)), _halves(ssm_w_out[j].astype(BF16)), _halves(att_w_in[j].astype(BF16)),
            _halves(att_w_out[j].astype(BF16)), _halves(ssm_conv_w[j])]


SHARD_KEYS = ("ssm_w_in", "ssm_w_out", "att_w_in", "att_w_out", "ssm_conv_w")


def _whole_weights(keys, gathered):
    out = {}
    for k, g in zip(keys, gathered):
        g = g.reshape((N_CHIPS, 2 * g.shape[2], g.shape[3]))
        if k in ("ssm_w_out", "att_w_out"):
            out[k] = g.reshape(N_CHIPS * g.shape[1], g.shape[2])
        else:
            out[k] = jnp.transpose(g, (1, 0, 2)).reshape(g.shape[1], N_CHIPS * g.shape[2])
    if "ssm_w_in" in out:
        out["ssm_w_in"] = jnp.pad(out["ssm_w_in"], ((0, 0), (0, SSM_IN_PAD - SSM_IN_DIM)))
    return out


def _halves_by_chip(key, g):
    if key in ("ssm_w_out", "att_w_out"):
        rows = g.shape[0] // N_CHIPS
        blocks = g.reshape(N_CHIPS, 2, rows // 2, g.shape[1])
        return jnp.transpose(blocks, (1, 0, 2, 3)).reshape(2, N_CHIPS * (rows // 2), g.shape[1])
    cols = (SSM_IN_DIM if key == "ssm_w_in" else g.shape[1]) // N_CHIPS
    rows = g.shape[0]
    blocks = g[:, :N_CHIPS * cols].reshape(2, rows // 2, N_CHIPS, cols)
    return jnp.transpose(blocks, (0, 2, 1, 3)).reshape(2, N_CHIPS * (rows // 2), cols)


def _pack_small(tree, keys):
    flat = jnp.concatenate([tree[k].reshape(-1) for k in keys])
    rows = -(-flat.shape[0] // (8 * LANES)) * 8
    return jnp.pad(flat, (0, rows * LANES - flat.shape[0])).reshape(rows, LANES)


def _unpack_small(packed, shapes, keys):
    flat = packed.reshape(-1)
    out, at = {}, 0
    for k in keys:
        n = 1
        for dim in shapes[k]:
            n *= dim
        out[k] = flat[at:at + n].reshape(shapes[k])
        at += n
    return out


def kernel(x, positions, pre_norm, post_norm, ssm_w_in, ssm_conv_w, ssm_conv_b, ssm_dt_bias, ssm_a_log, ssm_d, ssm_gate_norm, ssm_w_out, att_w_in, att_sinks, att_w_out, loss_target, m_pre_norm, m_post_norm, m_ssm_w_in, m_ssm_conv_w, m_ssm_conv_b, m_ssm_dt_bias, m_ssm_a_log, m_ssm_d, m_ssm_gate_norm, m_ssm_w_out, m_att_w_in, m_att_sinks, m_att_w_out, v_pre_norm, v_post_norm, v_ssm_w_in, v_ssm_conv_w, v_ssm_conv_b, v_ssm_dt_bias, v_ssm_a_log, v_ssm_d, v_ssm_gate_norm, v_ssm_w_out, v_att_w_in, v_att_sinks, v_att_w_out):
    w = dict(pre_norm=pre_norm, post_norm=post_norm, ssm_w_in=ssm_w_in, ssm_conv_w=ssm_conv_w, ssm_conv_b=ssm_conv_b,
             ssm_dt_bias=ssm_dt_bias, ssm_a_log=ssm_a_log, ssm_d=ssm_d, ssm_gate_norm=ssm_gate_norm, ssm_w_out=ssm_w_out,
             att_w_in=att_w_in, att_sinks=att_sinks, att_w_out=att_w_out)
    m = dict(pre_norm=m_pre_norm, post_norm=m_post_norm, ssm_w_in=m_ssm_w_in, ssm_conv_w=m_ssm_conv_w, ssm_conv_b=m_ssm_conv_b,
             ssm_dt_bias=m_ssm_dt_bias, ssm_a_log=m_ssm_a_log, ssm_d=m_ssm_d, ssm_gate_norm=m_ssm_gate_norm,
             ssm_w_out=m_ssm_w_out, att_w_in=m_att_w_in, att_sinks=m_att_sinks, att_w_out=m_att_w_out)
    v = dict(pre_norm=v_pre_norm, post_norm=v_post_norm, ssm_w_in=v_ssm_w_in, ssm_conv_w=v_ssm_conv_w, ssm_conv_b=v_ssm_conv_b,
             ssm_dt_bias=v_ssm_dt_bias, ssm_a_log=v_ssm_a_log, ssm_d=v_ssm_d, ssm_gate_norm=v_ssm_gate_norm,
             ssm_w_out=v_ssm_w_out, att_w_in=v_att_w_in, att_sinks=v_att_sinks, att_w_out=v_att_w_out)
    c = lax.axis_index("c")
    chip = 2 * lax.axis_index("x") + lax.axis_index("y")

    sharded = (ssm_w_in, ssm_w_out, att_w_in, att_w_out, ssm_conv_w)
    own = [dict(zip(SHARD_KEYS, _layer_shards(j, *sharded))) for j in range(2)]
    now_keys = ("ssm_w_in", "ssm_conv_w")
    later_keys = ("ssm_w_out", "att_w_in", "att_w_out")

    def prenorm_with_first_in(norm):
        h, *arrived = norm(ride=[own[0][k] for k in now_keys])
        return h, _whole_weights(now_keys, _pair_handoff(arrived, "gather_weights_0_handoff"))

    def in_proj_with_first_pair(matmul):
        proj, *arrived = matmul(ride=[own[0][k] for k in later_keys])
        return proj, _whole_weights(later_keys, _pair_handoff(arrived, "gather_weights_0_rest_handoff"))

    def scan_with_second_pair(scan):
        results = scan(ride=[own[1][k] for k in SHARD_KEYS])
        scanned, arrived = results[:-len(SHARD_KEYS)], results[-len(SHARD_KEYS):]
        return (*scanned, _whole_weights(SHARD_KEYS, _pair_handoff(arrived, "gather_weights_1_handoff")))

    half = jnp.reshape(c, (1,)).astype(jnp.int32)

    def reduce_begin(pair_grads, tag):
        parts = [_halves_by_chip(k, pair_grads[k]) for k in BIG]
        from_sibling = _pair_swap(parts, f"reduce_pair_swap_{tag}")
        return [_pair_add(p, o, half, f"reduce_pair_add_{tag}_{n}") for n, (p, o) in enumerate(zip(parts, from_sibling))]

    def reduce_end(by_chip, tag):
        mine = [_sum_slots(a, f"reduce_chip_sum_{tag}_{n}") for n, a in enumerate(by_chip)]
        theirs = _pair_merge(mine, f"reduce_pair_merge_{tag}")
        return {k: jnp.where(c == 0, jnp.concatenate([a, b]), jnp.concatenate([b, a])) for k, a, b in zip(BIG, mine, theirs)}

    def attn_bwd_with_second_pair_grads(attn_bwd, pair_grads):
        dq, dk, dv, dgate, dsk, *by_chip = attn_bwd(ride=reduce_begin(pair_grads, "1"))
        return (dq, dk, dv, dgate, dsk), reduce_end(by_chip, "1")

    def in_dx_with_first_pair_grads(matmul, pair_grads):
        dh, *by_chip = matmul(ride=reduce_begin(pair_grads, "0"), ride_scatters=True)
        return dh, reduce_end(by_chip, "0")

    loss_lanes, grad_x, gr, reduced_0, reduced_1 = _local_step(
        x[0], positions[0], pre_norm, post_norm, ssm_conv_b, ssm_dt_bias, ssm_a_log, ssm_d, ssm_gate_norm, att_sinks,
        loss_target[0], prenorm_with_first_in, in_proj_with_first_pair, scan_with_second_pair, attn_bwd_with_second_pair_grads,
        in_dx_with_first_pair_grads)
    loss = lax.psum(0.5 * jnp.sum(loss_lanes) / D_MODEL, ("x", "y", "c"))
    grads = {k: jnp.stack([reduced_0[k], reduced_1[k]]) for k in BIG}

    small_keys = SMALL + ("ssm_conv_w",)
    small_shapes = {k: w[k].shape for k in SMALL}
    small_shapes["ssm_conv_w"] = gr["ssm_conv_w"].shape
    small_sum = _sum_slots(_all_gather_small(_pack_small(gr, small_keys), "reduce_small_gather"), "reduce_small_sum")
    grads.update(_unpack_small(small_sum, small_shapes, small_keys))
    conv_cols = ssm_conv_w.shape[2]
    grads["ssm_conv_w"] = lax.dynamic_slice_in_dim(grads["ssm_conv_w"], chip * conv_cols, conv_cols, axis=2)

    delta, new_m, new_v = {}, {}, {}
    for k in SHARDED:
        shp = w[k].shape
        two_d = (shp[0] * shp[1], shp[2])
        d_, m_, v_ = _adamw(w[k].reshape(two_d), grads[k].reshape(two_d), m[k].reshape(two_d), v[k].reshape(two_d),
                            f"adamw_{k}")
        delta[k], new_m[k], new_v[k] = d_.reshape(shp), m_.reshape(shp), v_.reshape(shp)
    d_, m_, v_ = _adamw(_pack_small(w, SMALL), _pack_small(grads, SMALL), _pack_small(m, SMALL), _pack_small(v, SMALL),
                        "adamw_small")
    delta.update(_unpack_small(d_, small_shapes, SMALL))
    new_m.update(_unpack_small(m_, small_shapes, SMALL))
    new_v.update(_unpack_small(v_, small_shapes, SMALL))

    return (loss, grad_x[None], *[grads[k] for k in WEIGHTS], *[delta[k] for k in WEIGHTS],
            *[new_m[k] for k in WEIGHTS], *[new_v[k] for k in WEIGHTS])
```
